```python
import math
import jax, jax.numpy as jnp
from jax import lax
import numpy as np

D_MODEL = 1024
BATCH = 8
SEQ = 8192
DEPTH = 1

HEAD_DIM = 64
SWA_Q_HEADS = 8
SWA_KV_HEADS = 2
SWA_GROUP = SWA_Q_HEADS // SWA_KV_HEADS
SWA_WINDOW = 128
SB_HEADS = 8
BLOCK = 128
REL_BUCKETS = 32
REL_MAX_DIST = 128
D_FF = 2816
N_BRANCH = 2
RMS_EPS = 1e-6
NEG_BIG = -1e30

SWA_Q_W = SWA_Q_HEADS * HEAD_DIM
SWA_KV_W = SWA_KV_HEADS * HEAD_DIM
SB_W = SB_HEADS * HEAD_DIM
IN_SIZES = (SWA_Q_W, SWA_KV_W, SWA_KV_W, SB_W, SB_W, SB_W, D_MODEL, D_MODEL)
IN_W = sum(IN_SIZES)
IN_SPLITS = tuple(int(v) for v in np.cumsum(IN_SIZES)[:-1])

kernel_name = 'hybrid_swa_sink_stickbreaking_macaron'


def rmsnorm(x, g):
    xf = x.astype(jnp.float32)
    y = xf * lax.rsqrt(jnp.mean(xf * xf, axis=-1, keepdims=True) + RMS_EPS) * g.astype(jnp.float32)
    return y.astype(x.dtype)


def swiglu(h, w1, w3, w2):
    return (jax.nn.silu(h @ w1) * (h @ w3)) @ w2


def rel_bucket(dist):
    max_exact = REL_BUCKETS // 2
    d = jnp.maximum(dist, 1).astype(jnp.float32)
    large = max_exact + (jnp.log(d / max_exact) / math.log(REL_MAX_DIST / max_exact)
                         * (REL_BUCKETS - max_exact)).astype(jnp.int32)
    large = jnp.minimum(large, REL_BUCKETS - 1)
    return jnp.where(dist < max_exact, dist, large)


def sliding_window_attention(q, k, v, sinks, rel_table):
    B, S = q.shape[0], q.shape[1]
    nb = S // BLOCK
    qb = q.astype(jnp.float32).reshape(B, nb, BLOCK, SWA_KV_HEADS, SWA_GROUP, HEAD_DIM)
    kb = k.astype(jnp.float32).reshape(B, nb, BLOCK, SWA_KV_HEADS, HEAD_DIM)
    vb = v.astype(jnp.float32).reshape(B, nb, BLOCK, SWA_KV_HEADS, HEAD_DIM)
    pad = ((0, 0), (1, 0), (0, 0), (0, 0), (0, 0))
    kw = jnp.concatenate([jnp.pad(kb, pad)[:, :-1], kb], axis=2)
    vw = jnp.concatenate([jnp.pad(vb, pad)[:, :-1], vb], axis=2)
    logits = jnp.einsum('bnqhgd,bnkhd->bnhgqk', qb, kw) * (HEAD_DIM ** -0.5)
    qi = jnp.arange(BLOCK)[:, None] + BLOCK
    kj = jnp.arange(2 * BLOCK)[None, :]
    dist = qi - kj
    band = (dist >= 0) & (dist < SWA_WINDOW)
    bias = rel_table.astype(jnp.float32)[rel_bucket(jnp.maximum(dist, 0))]
    bias = bias.transpose(2, 0, 1).reshape(SWA_KV_HEADS, SWA_GROUP, BLOCK, 2 * BLOCK)
    key_pos = jnp.arange(nb)[:, None] * BLOCK + jnp.arange(2 * BLOCK)[None, :] - BLOCK
    valid = band[None] & (key_pos >= 0)[:, None, :]
    logits = jnp.where(valid[None, :, None, None], logits + bias, NEG_BIG)
    sink = sinks.astype(jnp.float32).reshape(SWA_KV_HEADS, SWA_GROUP)[None, None, :, :, None, None]
    m = jnp.maximum(jnp.max(logits, axis=-1, keepdims=True), sink)
    p = jnp.exp(logits - m)
    p = p / (jnp.sum(p, axis=-1, keepdims=True) + jnp.exp(sink - m))
    o = jnp.einsum('bnhgqk,bnkhd->bnqhgd', p, vw)
    return o.reshape(B, S, SWA_Q_W).astype(q.dtype)


def stick_breaking_attention(q, k, v):
    B, S = q.shape[0], q.shape[1]
    nb = S // BLOCK
    qf = q.astype(jnp.float32).transpose(0, 2, 1, 3) * (HEAD_DIM ** -0.5)
    kf = k.astype(jnp.float32).transpose(0, 2, 1, 3)
    vf = v.astype(jnp.float32).transpose(0, 2, 1, 3)
    qblocks = qf.reshape(B, SB_HEADS, nb, BLOCK, HEAD_DIM).transpose(2, 0, 1, 3, 4)
    key_pos = jnp.arange(S)

    def one_block(args):
        q_blk, start = args
        z = jnp.einsum('bhqd,bhkd->bhqk', q_blk, kf)
        q_pos = start + jnp.arange(BLOCK)
        causal = key_pos[None, :] < q_pos[:, None]
        log_keep = jnp.where(causal, jax.nn.log_sigmoid(-z), 0.0)
        rev = lax.cumsum(log_keep, axis=3, reverse=True)
        between = jnp.concatenate([rev[..., 1:], jnp.zeros_like(rev[..., :1])], axis=-1)
        a = jnp.where(causal, jnp.exp(jax.nn.log_sigmoid(z) + between), 0.0)
        return jnp.einsum('bhqk,bhkd->bhqd', a, vf)

    o = lax.map(one_block, (qblocks, jnp.arange(nb) * BLOCK))
    return o.transpose(1, 0, 3, 2, 4).reshape(B, S, SB_W).astype(q.dtype)


def _fwd_setup_inputs(seed: int = 0) -> dict:
    key = jax.random.key(seed)
    ks = jax.random.split(key, 20)
    f32 = jnp.float32

    def w(k, shape, fan_in):
        return jax.random.normal(k, shape, f32) * (fan_in ** -0.5)

    def gain(k):
        return 1.0 + 0.02 * jax.random.normal(k, (DEPTH, D_MODEL), f32)

    return {
        'x': jax.random.normal(ks[0], (BATCH, SEQ, D_MODEL), f32),
        'norm_ffn1': gain(ks[1]),
        'ffn1_w1': w(ks[2], (DEPTH, D_MODEL, D_FF), D_MODEL),
        'ffn1_w3': w(ks[3], (DEPTH, D_MODEL, D_FF), D_MODEL),
        'ffn1_w2': w(ks[4], (DEPTH, D_FF, D_MODEL), D_FF),
        'norm_mix': gain(ks[5]),
        'w_in': w(ks[6], (DEPTH, D_MODEL, IN_W), D_MODEL),
        'swa_sinks': 0.5 * jax.random.normal(ks[7], (DEPTH, SWA_Q_HEADS), f32),
        'rel_bias': 0.5 * jax.random.normal(ks[8], (REL_BUCKETS, SWA_Q_HEADS), f32),
        'w_branch_swa': w(ks[9], (DEPTH, SWA_Q_W, D_MODEL), SWA_Q_W),
        'w_branch_sb': w(ks[10], (DEPTH, SB_W, D_MODEL), SB_W),
        'w_out': w(ks[11], (DEPTH, D_MODEL, D_MODEL), D_MODEL),
        'norm_ffn2': gain(ks[12]),
        'ffn2_w1': w(ks[13], (DEPTH, D_MODEL, D_FF), D_MODEL),
        'ffn2_w3': w(ks[14], (DEPTH, D_MODEL, D_FF), D_MODEL),
        'ffn2_w2': w(ks[15], (DEPTH, D_FF, D_MODEL), D_FF),
        'norm_final': 1.0 + 0.02 * jax.random.normal(ks[16], (D_MODEL,), f32),
    }


def _fwd_reference(x, norm_ffn1, ffn1_w1, ffn1_w3, ffn1_w2, norm_mix, w_in, swa_sinks, rel_bias,
              w_branch_swa, w_branch_sb, w_out, norm_ffn2, ffn2_w1, ffn2_w3, ffn2_w2, norm_final):
    B, S = x.shape[0], x.shape[1]
    for layer in range(DEPTH):
        h = rmsnorm(x, norm_ffn1[layer])
        x = x + 0.5 * swiglu(h, ffn1_w1[layer], ffn1_w3[layer], ffn1_w2[layer])
        h = rmsnorm(x, norm_mix[layer])
        proj = h @ w_in[layer]
        q_a, k_a, v_a, q_b, k_b, v_b, g_a, g_b = jnp.split(proj, IN_SPLITS, axis=-1)
        o_a = sliding_window_attention(
            q_a.reshape(B, S, SWA_Q_HEADS, HEAD_DIM),
            k_a.reshape(B, S, SWA_KV_HEADS, HEAD_DIM),
            v_a.reshape(B, S, SWA_KV_HEADS, HEAD_DIM),
            swa_sinks[layer], rel_bias)
        o_b = stick_breaking_attention(
            q_b.reshape(B, S, SB_HEADS, HEAD_DIM),
            k_b.reshape(B, S, SB_HEADS, HEAD_DIM),
            v_b.reshape(B, S, SB_HEADS, HEAD_DIM))
        merged = (jax.nn.sigmoid(g_a) * (o_a @ w_branch_swa[layer])
                  + jax.nn.sigmoid(g_b) * (o_b @ w_branch_sb[layer]))
        x = x + merged @ w_out[layer]
        h = rmsnorm(x, norm_ffn2[layer])
        x = x + 0.5 * swiglu(h, ffn2_w1[layer], ffn2_w3[layer], ffn2_w2[layer])
    return rmsnorm(x, norm_final)


import jax as _jax
import jax.numpy as _jnp

TWIN_FORMAT = 'train_step'
FWD_PARAMS = ['x', 'norm_ffn1', 'ffn1_w1', 'ffn1_w3', 'ffn1_w2', 'norm_mix', 'w_in', 'swa_sinks', 'rel_bias', 'w_branch_swa', 'w_branch_sb', 'w_out', 'norm_ffn2', 'ffn2_w1', 'ffn2_w3', 'ffn2_w2', 'norm_final']
TWIN_WEIGHTS = ['norm_ffn1', 'ffn1_w1', 'ffn1_w3', 'ffn1_w2', 'norm_mix', 'w_in', 'swa_sinks', 'rel_bias', 'w_branch_swa', 'w_branch_sb', 'w_out', 'norm_ffn2', 'ffn2_w1', 'ffn2_w3', 'ffn2_w2', 'norm_final']
TWIN_DIFF_INPUT = 'x'
TWIN_INPUTS = ['x', 'norm_ffn1', 'ffn1_w1', 'ffn1_w3', 'ffn1_w2', 'norm_mix', 'w_in', 'swa_sinks', 'rel_bias', 'w_branch_swa', 'w_branch_sb', 'w_out', 'norm_ffn2', 'ffn2_w1', 'ffn2_w3', 'ffn2_w2', 'norm_final', 'loss_target', 'm_norm_ffn1', 'm_ffn1_w1', 'm_ffn1_w3', 'm_ffn1_w2', 'm_norm_mix', 'm_w_in', 'm_swa_sinks', 'm_rel_bias', 'm_w_branch_swa', 'm_w_branch_sb', 'm_w_out', 'm_norm_ffn2', 'm_ffn2_w1', 'm_ffn2_w3', 'm_ffn2_w2', 'm_norm_final', 'v_norm_ffn1', 'v_ffn1_w1', 'v_ffn1_w3', 'v_ffn1_w2', 'v_norm_mix', 'v_w_in', 'v_swa_sinks', 'v_rel_bias', 'v_w_branch_swa', 'v_w_branch_sb', 'v_w_out', 'v_norm_ffn2', 'v_ffn2_w1', 'v_ffn2_w3', 'v_ffn2_w2', 'v_norm_final']
TWIN_OUTPUTS = ['loss', 'grad_x', 'grad_norm_ffn1', 'grad_ffn1_w1', 'grad_ffn1_w3', 'grad_ffn1_w2', 'grad_norm_mix', 'grad_w_in', 'grad_swa_sinks', 'grad_rel_bias', 'grad_w_branch_swa', 'grad_w_branch_sb', 'grad_w_out', 'grad_norm_ffn2', 'grad_ffn2_w1', 'grad_ffn2_w3', 'grad_ffn2_w2', 'grad_norm_final', 'delta_norm_ffn1', 'delta_ffn1_w1', 'delta_ffn1_w3', 'delta_ffn1_w2', 'delta_norm_mix', 'delta_w_in', 'delta_swa_sinks', 'delta_rel_bias', 'delta_w_branch_swa', 'delta_w_branch_sb', 'delta_w_out', 'delta_norm_ffn2', 'delta_ffn2_w1', 'delta_ffn2_w3', 'delta_ffn2_w2', 'delta_norm_final', 'new_m_norm_ffn1', 'new_m_ffn1_w1', 'new_m_ffn1_w3', 'new_m_ffn1_w2', 'new_m_norm_mix', 'new_m_w_in', 'new_m_swa_sinks', 'new_m_rel_bias', 'new_m_w_branch_swa', 'new_m_w_branch_sb', 'new_m_w_out', 'new_m_norm_ffn2', 'new_m_ffn2_w1', 'new_m_ffn2_w3', 'new_m_ffn2_w2', 'new_m_norm_final', 'new_v_norm_ffn1', 'new_v_ffn1_w1', 'new_v_ffn1_w3', 'new_v_ffn1_w2', 'new_v_norm_mix', 'new_v_w_in', 'new_v_swa_sinks', 'new_v_rel_bias', 'new_v_w_branch_swa', 'new_v_w_branch_sb', 'new_v_w_out', 'new_v_norm_ffn2', 'new_v_ffn2_w1', 'new_v_ffn2_w3', 'new_v_ffn2_w2', 'new_v_norm_final']
TWIN_LEAF_KINDS = {'loss': 'loss', 'grad_x': 'grad_x', 'grad_norm_ffn1': 'grad_w', 'grad_ffn1_w1': 'grad_w', 'grad_ffn1_w3': 'grad_w', 'grad_ffn1_w2': 'grad_w', 'grad_norm_mix': 'grad_w', 'grad_w_in': 'grad_w', 'grad_swa_sinks': 'grad_w', 'grad_rel_bias': 'grad_w', 'grad_w_branch_swa': 'grad_w', 'grad_w_branch_sb': 'grad_w', 'grad_w_out': 'grad_w', 'grad_norm_ffn2': 'grad_w', 'grad_ffn2_w1': 'grad_w', 'grad_ffn2_w3': 'grad_w', 'grad_ffn2_w2': 'grad_w', 'grad_norm_final': 'grad_w', 'delta_norm_ffn1': 'delta_w', 'delta_ffn1_w1': 'delta_w', 'delta_ffn1_w3': 'delta_w', 'delta_ffn1_w2': 'delta_w', 'delta_norm_mix': 'delta_w', 'delta_w_in': 'delta_w', 'delta_swa_sinks': 'delta_w', 'delta_rel_bias': 'delta_w', 'delta_w_branch_swa': 'delta_w', 'delta_w_branch_sb': 'delta_w', 'delta_w_out': 'delta_w', 'delta_norm_ffn2': 'delta_w', 'delta_ffn2_w1': 'delta_w', 'delta_ffn2_w3': 'delta_w', 'delta_ffn2_w2': 'delta_w', 'delta_norm_final': 'delta_w', 'new_m_norm_ffn1': 'new_m', 'new_m_ffn1_w1': 'new_m', 'new_m_ffn1_w3': 'new_m', 'new_m_ffn1_w2': 'new_m', 'new_m_norm_mix': 'new_m', 'new_m_w_in': 'new_m', 'new_m_swa_sinks': 'new_m', 'new_m_rel_bias': 'new_m', 'new_m_w_branch_swa': 'new_m', 'new_m_w_branch_sb': 'new_m', 'new_m_w_out': 'new_m', 'new_m_norm_ffn2': 'new_m', 'new_m_ffn2_w1': 'new_m', 'new_m_ffn2_w3': 'new_m', 'new_m_ffn2_w2': 'new_m', 'new_m_norm_final': 'new_m', 'new_v_norm_ffn1': 'new_v', 'new_v_ffn1_w1': 'new_v', 'new_v_ffn1_w3': 'new_v', 'new_v_ffn1_w2': 'new_v', 'new_v_norm_mix': 'new_v', 'new_v_w_in': 'new_v', 'new_v_swa_sinks': 'new_v', 'new_v_rel_bias': 'new_v', 'new_v_w_branch_swa': 'new_v', 'new_v_w_branch_sb': 'new_v', 'new_v_w_out': 'new_v', 'new_v_norm_ffn2': 'new_v', 'new_v_ffn2_w1': 'new_v', 'new_v_ffn2_w3': 'new_v', 'new_v_ffn2_w2': 'new_v', 'new_v_norm_final': 'new_v'}


def _forward(args):
    return _fwd_reference(*[args[k] for k in FWD_PARAMS])


def _output_shape():
    out = _jax.eval_shape(lambda: _forward(_fwd_setup_inputs(0)))
    return out.shape, out.dtype

N_MICROBATCH = 1
ADAM_LR = 0.001
ADAM_B1 = 0.9
ADAM_B2 = 0.999
ADAM_EPS = 1e-08
ADAM_WD = 0.01
ADAM_STEP = 10
PER_EXAMPLE_BATCH_AXIS = {'x': 0, 'loss_target': 0}
SHARED_INPUTS = []
_WEIGHT_DTYPES = {'norm_ffn1': _jnp.float32, 'ffn1_w1': _jnp.float32, 'ffn1_w3': _jnp.float32, 'ffn1_w2': _jnp.float32, 'norm_mix': _jnp.float32, 'w_in': _jnp.float32, 'swa_sinks': _jnp.float32, 'rel_bias': _jnp.float32, 'w_branch_swa': _jnp.float32, 'w_branch_sb': _jnp.float32, 'w_out': _jnp.float32, 'norm_ffn2': _jnp.float32, 'ffn2_w1': _jnp.float32, 'ffn2_w3': _jnp.float32, 'ffn2_w2': _jnp.float32, 'norm_final': _jnp.float32}
MOMENT_SCALE = {'norm_ffn1': 1.340594e-01, 'ffn1_w1': 4.788408e-02, 'ffn1_w3': 4.629052e-02, 'ffn1_w2': 7.683567e-02, 'norm_mix': 1.124161e-01, 'w_in': 5.446466e-02, 'swa_sinks': 2.378265e-02, 'rel_bias': 4.811496e-02, 'w_branch_swa': 2.840777e-02, 'w_branch_sb': 8.308592e-02, 'w_out': 8.621419e-02, 'norm_ffn2': 1.025533e-01, 'ffn2_w1': 4.096486e-02, 'ffn2_w3': 3.969761e-02, 'ffn2_w2': 6.564405e-02, 'norm_final': 6.396384e+01}


def _to_microbatches(a, axis):
    t = _jnp.moveaxis(a, axis, 0)
    t = t.reshape((N_MICROBATCH, t.shape[0] // N_MICROBATCH) + t.shape[1:])
    return _jnp.moveaxis(t, 1, axis + 1)


def setup_inputs(seed: int = 0) -> dict:
    inp = _fwd_setup_inputs(seed)
    key = _jax.random.fold_in(_jax.random.key(seed), 7919)
    shape, _ = _output_shape()
    out = dict(inp)
    out["loss_target"] = _jax.random.normal(_jax.random.fold_in(key, 0), shape, _jnp.float32)
    for i, name in enumerate(TWIN_WEIGHTS):
        w = inp[name].astype(_jnp.float32)
        if MOMENT_SCALE is None:
            s = _jnp.sqrt(_jnp.mean(_jnp.square(w)) + 1e-30)
        else:
            s = MOMENT_SCALE[name]
        km, kv = _jax.random.split(_jax.random.fold_in(key, i + 1))
        out[name] = w
        out["m_" + name] = s * _jax.random.normal(km, w.shape, _jnp.float32)
        out["v_" + name] = (s * s) * _jax.random.uniform(kv, w.shape, _jnp.float32, 0.5, 1.5)
    if N_MICROBATCH > 1:
        for name, axis in PER_EXAMPLE_BATCH_AXIS.items():
            out[name] = _to_microbatches(out[name], axis)
    return {'x': out['x'], 'norm_ffn1': out['norm_ffn1'], 'ffn1_w1': out['ffn1_w1'], 'ffn1_w3': out['ffn1_w3'], 'ffn1_w2': out['ffn1_w2'], 'norm_mix': out['norm_mix'], 'w_in': out['w_in'], 'swa_sinks': out['swa_sinks'], 'rel_bias': out['rel_bias'], 'w_branch_swa': out['w_branch_swa'], 'w_branch_sb': out['w_branch_sb'], 'w_out': out['w_out'], 'norm_ffn2': out['norm_ffn2'], 'ffn2_w1': out['ffn2_w1'], 'ffn2_w3': out['ffn2_w3'], 'ffn2_w2': out['ffn2_w2'], 'norm_final': out['norm_final'], 'loss_target': out['loss_target'], 'm_norm_ffn1': out['m_norm_ffn1'], 'm_ffn1_w1': out['m_ffn1_w1'], 'm_ffn1_w3': out['m_ffn1_w3'], 'm_ffn1_w2': out['m_ffn1_w2'], 'm_norm_mix': out['m_norm_mix'], 'm_w_in': out['m_w_in'], 'm_swa_sinks': out['m_swa_sinks'], 'm_rel_bias': out['m_rel_bias'], 'm_w_branch_swa': out['m_w_branch_swa'], 'm_w_branch_sb': out['m_w_branch_sb'], 'm_w_out': out['m_w_out'], 'm_norm_ffn2': out['m_norm_ffn2'], 'm_ffn2_w1': out['m_ffn2_w1'], 'm_ffn2_w3': out['m_ffn2_w3'], 'm_ffn2_w2': out['m_ffn2_w2'], 'm_norm_final': out['m_norm_final'], 'v_norm_ffn1': out['v_norm_ffn1'], 'v_ffn1_w1': out['v_ffn1_w1'], 'v_ffn1_w3': out['v_ffn1_w3'], 'v_ffn1_w2': out['v_ffn1_w2'], 'v_norm_mix': out['v_norm_mix'], 'v_w_in': out['v_w_in'], 'v_swa_sinks': out['v_swa_sinks'], 'v_rel_bias': out['v_rel_bias'], 'v_w_branch_swa': out['v_w_branch_swa'], 'v_w_branch_sb': out['v_w_branch_sb'], 'v_w_out': out['v_w_out'], 'v_norm_ffn2': out['v_norm_ffn2'], 'v_ffn2_w1': out['v_ffn2_w1'], 'v_ffn2_w3': out['v_ffn2_w3'], 'v_ffn2_w2': out['v_ffn2_w2'], 'v_norm_final': out['v_norm_final']}


def _loss(weights, diff, rest, loss_target):
    with _jax.named_scope("forward"):
        args = {**rest, TWIN_DIFF_INPUT: diff, **{k: w.astype(_WEIGHT_DTYPES[k]) for k, w in weights.items()}}
        y = _forward(args)
    with _jax.named_scope("loss_head"):
        err = _jnp.square(y.astype(_jnp.float32) - loss_target)
        return 0.5 * _jnp.sum(_jnp.mean(err, axis=-1)) if err.ndim else 0.5 * err


def _adamw(w, g, m, v):
    m = ADAM_B1 * m + (1.0 - ADAM_B1) * g
    v = ADAM_B2 * v + (1.0 - ADAM_B2) * _jnp.square(g)
    m_hat = m / (1.0 - ADAM_B1 ** ADAM_STEP)
    v_hat = v / (1.0 - ADAM_B2 ** ADAM_STEP)
    delta = -ADAM_LR * (m_hat / (_jnp.sqrt(v_hat) + ADAM_EPS) + ADAM_WD * w)
    return delta, m, v


def reference(x, norm_ffn1, ffn1_w1, ffn1_w3, ffn1_w2, norm_mix, w_in, swa_sinks, rel_bias, w_branch_swa, w_branch_sb, w_out, norm_ffn2, ffn2_w1, ffn2_w3, ffn2_w2, norm_final, loss_target, m_norm_ffn1, m_ffn1_w1, m_ffn1_w3, m_ffn1_w2, m_norm_mix, m_w_in, m_swa_sinks, m_rel_bias, m_w_branch_swa, m_w_branch_sb, m_w_out, m_norm_ffn2, m_ffn2_w1, m_ffn2_w3, m_ffn2_w2, m_norm_final, v_norm_ffn1, v_ffn1_w1, v_ffn1_w3, v_ffn1_w2, v_norm_mix, v_w_in, v_swa_sinks, v_rel_bias, v_w_branch_swa, v_w_branch_sb, v_w_out, v_norm_ffn2, v_ffn2_w1, v_ffn2_w3, v_ffn2_w2, v_norm_final):
    given = dict(x=x, norm_ffn1=norm_ffn1, ffn1_w1=ffn1_w1, ffn1_w3=ffn1_w3, ffn1_w2=ffn1_w2, norm_mix=norm_mix, w_in=w_in, swa_sinks=swa_sinks, rel_bias=rel_bias, w_branch_swa=w_branch_swa, w_branch_sb=w_branch_sb, w_out=w_out, norm_ffn2=norm_ffn2, ffn2_w1=ffn2_w1, ffn2_w3=ffn2_w3, ffn2_w2=ffn2_w2, norm_final=norm_final, loss_target=loss_target, m_norm_ffn1=m_norm_ffn1, m_ffn1_w1=m_ffn1_w1, m_ffn1_w3=m_ffn1_w3, m_ffn1_w2=m_ffn1_w2, m_norm_mix=m_norm_mix, m_w_in=m_w_in, m_swa_sinks=m_swa_sinks, m_rel_bias=m_rel_bias, m_w_branch_swa=m_w_branch_swa, m_w_branch_sb=m_w_branch_sb, m_w_out=m_w_out, m_norm_ffn2=m_norm_ffn2, m_ffn2_w1=m_ffn2_w1, m_ffn2_w3=m_ffn2_w3, m_ffn2_w2=m_ffn2_w2, m_norm_final=m_norm_final, v_norm_ffn1=v_norm_ffn1, v_ffn1_w1=v_ffn1_w1, v_ffn1_w3=v_ffn1_w3, v_ffn1_w2=v_ffn1_w2, v_norm_mix=v_norm_mix, v_w_in=v_w_in, v_swa_sinks=v_swa_sinks, v_rel_bias=v_rel_bias, v_w_branch_swa=v_w_branch_swa, v_w_branch_sb=v_w_branch_sb, v_w_out=v_w_out, v_norm_ffn2=v_norm_ffn2, v_ffn2_w1=v_ffn2_w1, v_ffn2_w3=v_ffn2_w3, v_ffn2_w2=v_ffn2_w2, v_norm_final=v_norm_final)
    weights = {n: given[n] for n in TWIN_WEIGHTS}
    shared = {n: given[n] for n in SHARED_INPUTS}
    per_example = {n: given[n] for n in ['x']}
    grad_fn = _jax.value_and_grad(_loss, argnums=(0, 1))

    def one_microbatch(ex, loss_target):
        ex = dict(ex)
        diff = ex.pop(TWIN_DIFF_INPUT)
        return grad_fn(weights, diff, {**shared, **ex}, loss_target)

    if N_MICROBATCH == 1:
        loss, (grad_w, grad_x) = one_microbatch(per_example, given["loss_target"])
    else:
        def body(carry, xs):
            loss_sum, grad_sum = carry
            l_k, (gw_k, gx_k) = one_microbatch(xs[0], xs[1])
            with _jax.named_scope("update"):
                return (loss_sum + l_k, _jax.tree.map(_jnp.add, grad_sum, gw_k)), gx_k

        init = (_jnp.zeros((), _jnp.float32), _jax.tree.map(_jnp.zeros_like, weights))
        (loss, grad_w), grad_x = _jax.lax.scan(body, init, (per_example, given["loss_target"]))
    with _jax.named_scope("update"):
        delta_w, new_m, new_v = {}, {}, {}
        for n in TWIN_WEIGHTS:
            delta_w[n], new_m[n], new_v[n] = _adamw(weights[n], grad_w[n], given["m_" + n], given["v_" + n])
    return (loss, grad_x, *[grad_w[n] for n in TWIN_WEIGHTS], *[delta_w[n] for n in TWIN_WEIGHTS],
            *[new_m[n] for n in TWIN_WEIGHTS], *[new_v[n] for n in TWIN_WEIGHTS])
```

```python
import functools

import jax
import jax.numpy as jnp
import numpy as np
from jax import lax
from jax.experimental import pallas as pl
from jax.experimental.pallas import tpu as pltpu

F32 = jnp.float32
BF16 = jnp.bfloat16

D_MODEL = 1024
D_FF = 2816
HEAD_DIM = 64
N_HEADS = 8
SWA_KV_HEADS = 2
SWA_GROUP = 4
SWA_BLOCK = 128
REL_BUCKETS = 32
REL_MAX_DIST = 128
RMS_EPS = 1e-6
NEG_BIG = -1e30
Q_SCALE = HEAD_DIM ** -0.5
LANES = 128

N_DEV = 8
AXES = ("x", "y", "c")

ADAM_LR = 0.001
ADAM_B1 = 0.9
ADAM_B2 = 0.999
ADAM_EPS = 1e-08
ADAM_WD = 0.01
ADAM_STEP = 10

IN_SIZES = (512, 128, 128, 512, 512, 512, 1024, 1024)
IN_OFFS = tuple(int(v) for v in np.cumsum((0,) + IN_SIZES))
IN_W = IN_OFFS[-1]

BIG_NAMES = ("ffn1_w1", "ffn1_w3", "ffn1_w2", "w_in", "w_branch_swa", "w_branch_sb", "w_out",
             "ffn2_w1", "ffn2_w3", "ffn2_w2")
BIG_ROWS = (352, 352, 352, 544, 64, 64, 128, 352, 352, 352)
BIG_OFFS = tuple(int(v) for v in np.cumsum((0,) + BIG_ROWS))
PACK_ROWS = BIG_OFFS[-1]
SMALL_ROWS = 8

VMEM_LIMIT = 56 * 1024 * 1024
SB_TILE = 256


def _dot(a, b):
    return jnp.dot(a, b, preferred_element_type=F32)


def _dot_nt(a, b):
    return lax.dot_general(a, b, (((1,), (1,)), ((), ())), preferred_element_type=F32)


def _dot_tn(a, b):
    return lax.dot_general(a, b, (((0,), (0,)), ((), ())), preferred_element_type=F32)


def _cparams(*sem):
    return pltpu.CompilerParams(dimension_semantics=sem, vmem_limit_bytes=VMEM_LIMIT)


def _rms_rstd(xv):
    return lax.rsqrt(jnp.mean(xv * xv, axis=-1, keepdims=True) + RMS_EPS)


def _rms_bwd(dh, xv, r, g):
    xhat = xv * r
    dg = jnp.sum(dh * xhat, axis=0, keepdims=True)
    dxn = dh * g
    dx = r * (dxn - xhat * jnp.mean(dxn * xhat, axis=-1, keepdims=True))
    return dx, dg


def _ffn_fwd(x, g, w1t, w3t, w2, tag):
    s_len = x.shape[0]
    tm, tf = min(1024, s_len), 256
    nf = D_FF // tf

    def body(x_ref, g_ref, w1_ref, w3_ref, w2_ref, xo_ref, h_ref, a_ref, b_ref, u_ref, acc_ref, hs_ref):
        j = pl.program_id(1)

        @pl.when(j == 0)
        def _():
            xv = x_ref[...]
            h = (xv * _rms_rstd(xv) * g_ref[...]).astype(BF16)
            hs_ref[...] = h
            h_ref[...] = h
            acc_ref[...] = jnp.zeros_like(acc_ref)

        h = hs_ref[...]
        a = _dot_nt(h, w1_ref[...])
        b = _dot_nt(h, w3_ref[...])
        a_ref[...] = a.astype(BF16)
        b_ref[...] = b.astype(BF16)
        uh = (0.5 * (a * jax.nn.sigmoid(a) * b)).astype(BF16)
        u_ref[...] = uh
        acc_ref[...] += _dot(uh, w2_ref[...])

        @pl.when(j == nf - 1)
        def _():
            xo_ref[...] = x_ref[...] + acc_ref[...]

    row = lambda i, j: (i, 0)
    return pl.pallas_call(
        body, name=f"ffn_fwd_{tag}",
        grid=(s_len // tm, nf),
        in_specs=[pl.BlockSpec((tm, D_MODEL), row), pl.BlockSpec((1, D_MODEL), lambda i, j: (0, 0)),
                  pl.BlockSpec((tf, D_MODEL), lambda i, j: (j, 0)), pl.BlockSpec((tf, D_MODEL), lambda i, j: (j, 0)),
                  pl.BlockSpec((tf, D_MODEL), lambda i, j: (j, 0))],
        out_specs=[pl.BlockSpec((tm, D_MODEL), row), pl.BlockSpec((tm, D_MODEL), row),
                   pl.BlockSpec((tm, tf), lambda i, j: (i, j)), pl.BlockSpec((tm, tf), lambda i, j: (i, j)),
                   pl.BlockSpec((tm, tf), lambda i, j: (i, j))],
        out_shape=[jax.ShapeDtypeStruct((s_len, D_MODEL), F32), jax.ShapeDtypeStruct((s_len, D_MODEL), BF16),
                   jax.ShapeDtypeStruct((s_len, D_FF), BF16), jax.ShapeDtypeStruct((s_len, D_FF), BF16),
                   jax.ShapeDtypeStruct((s_len, D_FF), BF16)],
        scratch_shapes=[pltpu.VMEM((tm, D_MODEL), F32), pltpu.VMEM((tm, D_MODEL), BF16)],
        compiler_params=_cparams("parallel", "arbitrary"),
    )(x, g, w1t, w3t, w2)


def _ffn_bwd(dy, x, g, a, b, w1t, w3t, w2, tag):
    s_len = x.shape[0]
    tm, tf = min(1024, s_len), 256
    nf = D_FF // tf

    def body(dy_ref, x_ref, g_ref, a_ref, b_ref, w1_ref, w3_ref, w2_ref,
             dx_ref, dg_ref, da_ref, db_ref, dyb_ref, acc_ref, dys_ref):
        i, j = pl.program_id(0), pl.program_id(1)

        @pl.when(j == 0)
        def _():
            dyb = dy_ref[...].astype(BF16)
            dys_ref[...] = dyb
            dyb_ref[...] = dyb
            acc_ref[...] = jnp.zeros_like(acc_ref)

        @pl.when((i == 0) & (j == 0))
        def _():
            dg_ref[...] = jnp.zeros_like(dg_ref)

        du = 0.5 * _dot_nt(dys_ref[...], w2_ref[...])
        av = a_ref[...].astype(F32)
        bv = b_ref[...].astype(F32)
        sg = jax.nn.sigmoid(av)
        sil = av * sg
        da = (du * bv * (sg + sil * (1.0 - sg))).astype(BF16)
        db = (du * sil).astype(BF16)
        da_ref[...] = da
        db_ref[...] = db
        acc_ref[...] += _dot(da, w1_ref[...]) + _dot(db, w3_ref[...])

        @pl.when(j == nf - 1)
        def _():
            xv = x_ref[...]
            dx, dg = _rms_bwd(acc_ref[...], xv, _rms_rstd(xv), g_ref[...])
            dx_ref[...] = dy_ref[...] + dx
            dg_ref[...] += dg

    row = lambda i, j: (i, 0)
    blk = lambda i, j: (i, j)
    wsp = pl.BlockSpec((tf, D_MODEL), lambda i, j: (j, 0))
    return pl.pallas_call(
        body, name=f"ffn_bwd_{tag}",
        grid=(s_len // tm, nf),
        in_specs=[pl.BlockSpec((tm, D_MODEL), row), pl.BlockSpec((tm, D_MODEL), row),
                  pl.BlockSpec((1, D_MODEL), lambda i, j: (0, 0)),
                  pl.BlockSpec((tm, tf), blk), pl.BlockSpec((tm, tf), blk), wsp, wsp, wsp],
        out_specs=[pl.BlockSpec((tm, D_MODEL), row), pl.BlockSpec((1, D_MODEL), lambda i, j: (0, 0)),
                   pl.BlockSpec((tm, tf), blk), pl.BlockSpec((tm, tf), blk), pl.BlockSpec((tm, D_MODEL), row)],
        out_shape=[jax.ShapeDtypeStruct((s_len, D_MODEL), F32), jax.ShapeDtypeStruct((1, D_MODEL), F32),
                   jax.ShapeDtypeStruct((s_len, D_FF), BF16), jax.ShapeDtypeStruct((s_len, D_FF), BF16),
                   jax.ShapeDtypeStruct((s_len, D_MODEL), BF16)],
        scratch_shapes=[pltpu.VMEM((tm, D_MODEL), F32), pltpu.VMEM((tm, D_MODEL), BF16)],
        compiler_params=_cparams("arbitrary", "arbitrary"),
    )(dy, x, g, a, b, w1t, w3t, w2)


def _matmul_tn(lhs, rhs, tag):
    s_len, m = lhs.shape
    n = rhs.shape[1]
    tm = min(512, s_len)
    tj = m if m <= 1024 else 1408
    assert m % tj == 0

    def body(l_ref, r_ref, o_ref):
        @pl.when(pl.program_id(1) == 0)
        def _():
            o_ref[...] = jnp.zeros_like(o_ref)

        o_ref[...] += _dot_tn(l_ref[...], r_ref[...])

    return pl.pallas_call(
        body, name=f"matmul_tn_{tag}",
        grid=(m // tj, s_len // tm),
        in_specs=[pl.BlockSpec((tm, tj), lambda j, i: (i, j)), pl.BlockSpec((tm, n), lambda j, i: (i, 0))],
        out_specs=pl.BlockSpec((tj, n), lambda j, i: (j, 0)),
        out_shape=jax.ShapeDtypeStruct((m, n), F32),
        compiler_params=_cparams("parallel", "arbitrary"),
    )(lhs, rhs)


def _proj_fwd(x1, g, wint):
    s_len = x1.shape[0]
    tm = min(512, s_len)
    dts = (BF16, BF16, BF16, BF16, BF16, BF16, F32, F32)

    def body(x_ref, g_ref, w_ref, h_ref, *outs):
        xv = x_ref[...]
        h = (xv * _rms_rstd(xv) * g_ref[...]).astype(BF16)
        h_ref[...] = h
        for p, o_ref in enumerate(outs):
            val = _dot_nt(h, w_ref[IN_OFFS[p]:IN_OFFS[p + 1], :])
            if p == 3:
                val = val * Q_SCALE
            o_ref[...] = val.astype(dts[p])

    row = lambda i: (i, 0)
    return pl.pallas_call(
        body, name="proj_fwd",
        grid=(s_len // tm,),
        in_specs=[pl.BlockSpec((tm, D_MODEL), row), pl.BlockSpec((1, D_MODEL), lambda i: (0, 0)),
                  pl.BlockSpec((IN_W, D_MODEL), lambda i: (0, 0))],
        out_specs=[pl.BlockSpec((tm, D_MODEL), row)] + [pl.BlockSpec((tm, w), row) for w in IN_SIZES],
        out_shape=[jax.ShapeDtypeStruct((s_len, D_MODEL), BF16)]
        + [jax.ShapeDtypeStruct((s_len, w), dt) for w, dt in zip(IN_SIZES, dts)],
        compiler_params=_cparams("parallel"),
    )(x1, g, wint)


def _proj_bwd(dpieces, dx2, x1, g, wint):
    s_len = x1.shape[0]
    tm = min(512, s_len)

    def body(*refs):
        dps = refs[:8]
        dx2_ref, x_ref, g_ref, w_ref, dx_ref, dg_ref = refs[8:]

        @pl.when(pl.program_id(0) == 0)
        def _():
            dg_ref[...] = jnp.zeros_like(dg_ref)

        dh = _dot(dps[0][...], w_ref[IN_OFFS[0]:IN_OFFS[1], :])
        for p in range(1, 8):
            dh += _dot(dps[p][...], w_ref[IN_OFFS[p]:IN_OFFS[p + 1], :])
        xv = x_ref[...]
        dx, dg = _rms_bwd(dh, xv, _rms_rstd(xv), g_ref[...])
        dx_ref[...] = dx2_ref[...] + dx
        dg_ref[...] += dg

    row = lambda i: (i, 0)
    return pl.pallas_call(
        body, name="proj_bwd",
        grid=(s_len // tm,),
        in_specs=[pl.BlockSpec((tm, w), row) for w in IN_SIZES]
        + [pl.BlockSpec((tm, D_MODEL), row), pl.BlockSpec((tm, D_MODEL), row),
           pl.BlockSpec((1, D_MODEL), lambda i: (0, 0)), pl.BlockSpec((IN_W, D_MODEL), lambda i: (0, 0))],
        out_specs=[pl.BlockSpec((tm, D_MODEL), row), pl.BlockSpec((1, D_MODEL), lambda i: (0, 0))],
        out_shape=[jax.ShapeDtypeStruct((s_len, D_MODEL), F32), jax.ShapeDtypeStruct((1, D_MODEL), F32)],
        compiler_params=_cparams("arbitrary"),
    )(*dpieces, dx2, x1, g, wint)


def _merge_fwd(x1, oa, ob, ga, gb, wswa, wsb, wout):
    s_len = x1.shape[0]
    tm = min(512, s_len)

    def body(x_ref, oa_ref, ob_ref, ga_ref, gb_ref, wa_ref, wb_ref, wo_ref, xo_ref, mg_ref):
        pa = _dot(oa_ref[...], wa_ref[...])
        pb = _dot(ob_ref[...], wb_ref[...])
        mg = (jax.nn.sigmoid(ga_ref[...]) * pa + jax.nn.sigmoid(gb_ref[...]) * pb).astype(BF16)
        mg_ref[...] = mg
        xo_ref[...] = x_ref[...] + _dot(mg, wo_ref[...])

    row = lambda i: (i, 0)
    full = lambda i: (0, 0)
    return pl.pallas_call(
        body, name="merge_fwd",
        grid=(s_len // tm,),
        in_specs=[pl.BlockSpec((tm, D_MODEL), row), pl.BlockSpec((tm, 512), row), pl.BlockSpec((tm, 512), row),
                  pl.BlockSpec((tm, D_MODEL), row), pl.BlockSpec((tm, D_MODEL), row),
                  pl.BlockSpec((512, D_MODEL), full), pl.BlockSpec((512, D_MODEL), full),
                  pl.BlockSpec((D_MODEL, D_MODEL), full)],
        out_specs=[pl.BlockSpec((tm, D_MODEL), row), pl.BlockSpec((tm, D_MODEL), row)],
        out_shape=[jax.ShapeDtypeStruct((s_len, D_MODEL), F32), jax.ShapeDtypeStruct((s_len, D_MODEL), BF16)],
        compiler_params=_cparams("parallel"),
    )(x1, oa, ob, ga, gb, wswa, wsb, wout)


def _merge_bwd(dx2, oa, ob, ga, gb, wswa, wsb, wout):
    s_len = dx2.shape[0]
    tm = min(512, s_len)

    def body(dx_ref, oa_ref, ob_ref, ga_ref, gb_ref, wa_ref, wb_ref, wo_ref,
             doa_ref, dob_ref, dga_ref, dgb_ref, dpa_ref, dpb_ref, dxb_ref):
        dxb = dx_ref[...].astype(BF16)
        dxb_ref[...] = dxb
        dmg = _dot_nt(dxb, wo_ref[...])
        for o_ref, g_ref, w_ref, do_ref, dg_ref, dp_ref in (
                (oa_ref, ga_ref, wa_ref, doa_ref, dga_ref, dpa_ref),
                (ob_ref, gb_ref, wb_ref, dob_ref, dgb_ref, dpb_ref)):
            pv = _dot(o_ref[...], w_ref[...])
            sg = jax.nn.sigmoid(g_ref[...])
            dp = (dmg * sg).astype(BF16)
            dp_ref[...] = dp
            dg_ref[...] = (dmg * pv * sg * (1.0 - sg)).astype(BF16)
            do_ref[...] = _dot_nt(dp, w_ref[...]).astype(BF16)

    row = lambda i: (i, 0)
    full = lambda i: (0, 0)
    wide = pl.BlockSpec((tm, D_MODEL), row)
    half = pl.BlockSpec((tm, 512), row)
    return pl.pallas_call(
        body, name="merge_bwd",
        grid=(s_len // tm,),
        in_specs=[wide, half, half, wide, wide, pl.BlockSpec((512, D_MODEL), full),
                  pl.BlockSpec((512, D_MODEL), full), pl.BlockSpec((D_MODEL, D_MODEL), full)],
        out_specs=[half, half, wide, wide, wide, wide, wide],
        out_shape=[jax.ShapeDtypeStruct((s_len, 512), BF16)] * 2 + [jax.ShapeDtypeStruct((s_len, D_MODEL), BF16)] * 5,
        compiler_params=_cparams("parallel"),
    )(dx2, oa, ob, ga, gb, wswa, wsb, wout)


def _loss_fwd_bwd(x3, tgt, g):
    s_len = x3.shape[0]
    tm = min(1024, s_len)

    def body(x_ref, t_ref, g_ref, dx_ref, loss_ref, dg_ref):
        @pl.when(pl.program_id(0) == 0)
        def _():
            loss_ref[...] = jnp.zeros_like(loss_ref)
            dg_ref[...] = jnp.zeros_like(dg_ref)

        xv = x_ref[...]
        gv = g_ref[...]
        r = _rms_rstd(xv)
        err = xv * r * gv - t_ref[...]
        loss_ref[...] += 0.5 * jnp.sum(jnp.mean(err * err, axis=-1, keepdims=True), axis=0, keepdims=True)
        dx, dg = _rms_bwd(err * (1.0 / D_MODEL), xv, r, gv)
        dx_ref[...] = dx
        dg_ref[...] += dg

    row = lambda i: (i, 0)
    return pl.pallas_call(
        body, name="loss_fwd_bwd",
        grid=(s_len // tm,),
        in_specs=[pl.BlockSpec((tm, D_MODEL), row), pl.BlockSpec((tm, D_MODEL), row),
                  pl.BlockSpec((1, D_MODEL), lambda i: (0, 0))],
        out_specs=[pl.BlockSpec((tm, D_MODEL), row), pl.BlockSpec((1, 1), lambda i: (0, 0)),
                   pl.BlockSpec((1, D_MODEL), lambda i: (0, 0))],
        out_shape=[jax.ShapeDtypeStruct((s_len, D_MODEL), F32), jax.ShapeDtypeStruct((1, 1), F32),
                   jax.ShapeDtypeStruct((1, D_MODEL), F32)],
        compiler_params=_cparams("arbitrary"),
    )(x3, tgt, g)


def _rel_bucket_matrix():
    qi = jnp.arange(SWA_BLOCK)[:, None] + SWA_BLOCK
    kj = jnp.arange(2 * SWA_BLOCK)[None, :]
    dist = jnp.maximum(qi - kj, 0)
    max_exact = REL_BUCKETS // 2
    d = jnp.maximum(dist, 1).astype(F32)
    large = max_exact + (jnp.log(d / max_exact) / np.log(REL_MAX_DIST / max_exact)
                         * (REL_BUCKETS - max_exact)).astype(jnp.int32)
    large = jnp.minimum(large, REL_BUCKETS - 1)
    return jnp.where(dist < max_exact, dist, large).astype(jnp.int32)


def _swa_bias_into(bias_ref, bkt_ref, tab_ref):
    bk = bkt_ref[...]
    for h in range(N_HEADS):
        acc = jnp.zeros(bk.shape, F32)
        for bucket in range(REL_BUCKETS):
            acc = jnp.where(bk == bucket, tab_ref[bucket, h], acc)
        bias_ref[h] = acc


def _swa_valid(n):
    shape = (SWA_BLOCK, 2 * SWA_BLOCK)
    row = lax.broadcasted_iota(jnp.int32, shape, 0)
    col = lax.broadcasted_iota(jnp.int32, shape, 1)
    dist = row + SWA_BLOCK - col
    return (dist >= 0) & (dist < SWA_BLOCK) & ((col >= SWA_BLOCK) | (n > 0))


def _swa_probs(q, k, bias, sink, valid):
    lg = jnp.where(valid, _dot_nt(q, k) * Q_SCALE + bias, NEG_BIG)
    m = jnp.maximum(jnp.max(lg, axis=-1, keepdims=True), sink)
    e = jnp.exp(lg - m)
    es = jnp.exp(sink - m)
    den = jnp.sum(e, axis=-1, keepdims=True) + es
    return e / den, es / den


def _swa_specs(s_len):
    blk = SWA_BLOCK
    cur = lambda n: (n, 0)
    prev = lambda n: (jnp.maximum(n - 1, 0), 0)
    kvw = SWA_KV_HEADS * LANES
    return [pl.BlockSpec(memory_space=pltpu.SMEM), pl.BlockSpec(memory_space=pltpu.SMEM),
            pl.BlockSpec((blk, 2 * blk), lambda n: (0, 0)),
            pl.BlockSpec((blk, N_HEADS * LANES), cur),
            pl.BlockSpec((blk, kvw), prev), pl.BlockSpec((blk, kvw), cur),
            pl.BlockSpec((blk, kvw), prev), pl.BlockSpec((blk, kvw), cur)]


def _swa_fwd(tab, sinks, bkt, q, k, v):
    s_len = q.shape[0]
    blk = SWA_BLOCK

    def body(tab_ref, sink_ref, bkt_ref, q_ref, kp_ref, kc_ref, vp_ref, vc_ref, o_ref, bias_ref):
        n = pl.program_id(0)

        @pl.when(n == 0)
        def _():
            _swa_bias_into(bias_ref, bkt_ref, tab_ref)

        valid = _swa_valid(n)
        for grp in range(SWA_KV_HEADS):
            gl = slice(grp * LANES, (grp + 1) * LANES)
            kk = jnp.concatenate([kp_ref[:, gl], kc_ref[:, gl]], axis=0)
            vv = jnp.concatenate([vp_ref[:, gl], vc_ref[:, gl]], axis=0)
            for hh in range(SWA_GROUP):
                h = grp * SWA_GROUP + hh
                hl = slice(h * LANES, (h + 1) * LANES)
                p, _ = _swa_probs(q_ref[:, hl], kk, bias_ref[h], sink_ref[0, h], valid)
                o_ref[:, hl] = _dot(p.astype(BF16), vv).astype(BF16)

    return pl.pallas_call(
        body, name="swa_fwd",
        grid=(s_len // blk,),
        in_specs=_swa_specs(s_len),
        out_specs=pl.BlockSpec((blk, N_HEADS * LANES), lambda n: (n, 0)),
        out_shape=jax.ShapeDtypeStruct((s_len, N_HEADS * LANES), BF16),
        scratch_shapes=[pltpu.VMEM((N_HEADS, blk, 2 * blk), F32)],
        compiler_params=_cparams("arbitrary"),
    )(tab, sinks, bkt, q, k, k, v, v)


def _swa_bwd(tab, sinks, bkt, q, k, v, do):
    s_len = q.shape[0]
    blk = SWA_BLOCK
    nb = s_len // blk
    kvw = SWA_KV_HEADS * LANES

    def body(tab_ref, sink_ref, bkt_ref, q_ref, kp_ref, kc_ref, vp_ref, vc_ref, do_ref,
             dq_ref, dk_ref, dv_ref, dtab_ref, dsink_ref, bias_ref, dbias_ref):
        n = pl.program_id(0)

        @pl.when(n == 0)
        def _():
            _swa_bias_into(bias_ref, bkt_ref, tab_ref)
            dbias_ref[...] = jnp.zeros_like(dbias_ref)
            dk_ref[...] = jnp.zeros_like(dk_ref)
            dv_ref[...] = jnp.zeros_like(dv_ref)
            dsink_ref[...] = jnp.zeros_like(dsink_ref)
            dtab_ref[...] = jnp.zeros_like(dtab_ref)

        valid = _swa_valid(n)
        cur_rows = pl.ds(pl.multiple_of(n * blk, blk), blk)
        prev_rows = pl.ds(pl.multiple_of(jnp.maximum(n - 1, 0) * blk, blk), blk)
        for grp in range(SWA_KV_HEADS):
            gl = slice(grp * LANES, (grp + 1) * LANES)
            kk = jnp.concatenate([kp_ref[:, gl], kc_ref[:, gl]], axis=0)
            vv = jnp.concatenate([vp_ref[:, gl], vc_ref[:, gl]], axis=0)
            dk_acc = jnp.zeros((2 * blk, LANES), F32)
            dv_acc = jnp.zeros((2 * blk, LANES), F32)
            for hh in range(SWA_GROUP):
                h = grp * SWA_GROUP + hh
                hl = slice(h * LANES, (h + 1) * LANES)
                qh = q_ref[:, hl]
                doh = do_ref[:, hl]
                p, ps = _swa_probs(qh, kk, bias_ref[h], sink_ref[0, h], valid)
                dp = _dot_nt(doh, vv)
                delta = jnp.sum(p * dp, axis=-1, keepdims=True)
                dl = p * (dp - delta)
                dsink_ref[h:h + 1, :] += jnp.broadcast_to(-jnp.sum(ps * delta, axis=0, keepdims=True), (1, LANES))
                dbias_ref[h] += dl
                dlb = dl.astype(BF16)
                dq_ref[:, hl] = (Q_SCALE * _dot(dlb, kk)).astype(BF16)
                dk_acc += Q_SCALE * _dot_tn(dlb, qh)
                dv_acc += _dot_tn(p.astype(BF16), doh)
            dk_ref[cur_rows, gl] += dk_acc[blk:]
            dv_ref[cur_rows, gl] += dv_acc[blk:]

            @pl.when(n > 0)
            def _():
                dk_ref[prev_rows, gl] += dk_acc[:blk]
                dv_ref[prev_rows, gl] += dv_acc[:blk]

        @pl.when(n == nb - 1)
        def _():
            bk = bkt_ref[...]
            lane = lax.broadcasted_iota(jnp.int32, (1, LANES), 1)
            for bucket in range(REL_BUCKETS):
                rowv = jnp.zeros((1, LANES), F32)
                for h in range(N_HEADS):
                    val = jnp.sum(jnp.where(bk == bucket, dbias_ref[h], 0.0), axis=1, keepdims=True)
                    val = jnp.sum(val, axis=0, keepdims=True)
                    rowv = jnp.where(lane == h, val, rowv)
                dtab_ref[bucket:bucket + 1, :] = rowv

    return pl.pallas_call(
        body, name="swa_bwd",
        grid=(nb,),
        in_specs=_swa_specs(s_len) + [pl.BlockSpec((blk, N_HEADS * LANES), lambda n: (n, 0))],
        out_specs=[pl.BlockSpec((blk, N_HEADS * LANES), lambda n: (n, 0)),
                   pl.BlockSpec((s_len, kvw), lambda n: (0, 0)), pl.BlockSpec((s_len, kvw), lambda n: (0, 0)),
                   pl.BlockSpec((REL_BUCKETS, LANES), lambda n: (0, 0)), pl.BlockSpec((N_HEADS, LANES), lambda n: (0, 0))],
        out_shape=[jax.ShapeDtypeStruct((s_len, N_HEADS * LANES), BF16),
                   jax.ShapeDtypeStruct((s_len, kvw), F32), jax.ShapeDtypeStruct((s_len, kvw), F32),
                   jax.ShapeDtypeStruct((REL_BUCKETS, LANES), F32), jax.ShapeDtypeStruct((N_HEADS, LANES), F32)],
        scratch_shapes=[pltpu.VMEM((N_HEADS, blk, 2 * blk), F32), pltpu.VMEM((N_HEADS, blk, 2 * blk), F32)],
        compiler_params=_cparams("arbitrary"),
    )(tab, sinks, bkt, q, k, k, v, v, do)


def _split_dot(vals, tri):
    hi = vals.astype(BF16)
    lo = (vals - hi.astype(F32)).astype(BF16)
    return _dot(hi, tri) + _dot(lo, tri)


def _sb_tile_terms(q, kt, valid):
    z = _dot(q, kt)
    sp = jnp.maximum(z, 0.0) + jnp.log(1.0 + jnp.exp(-jnp.abs(z)))
    return z - sp, jnp.where(valid, -sp, 0.0)


def _sb_fwd(q, kt, v):
    s_len = q.shape[0]
    t = SB_TILE
    nk = s_len // t
    assert nk <= LANES

    def body(q_ref, kt_ref, v_ref, o_ref, car_ref, c_ref, oacc_ref):
        i = pl.program_id(1)
        qv = q_ref[...]
        row = lax.broadcasted_iota(jnp.int32, (t, t), 0)
        col = lax.broadcasted_iota(jnp.int32, (t, t), 1)
        tri = (row > col).astype(BF16)
        lane = lax.broadcasted_iota(jnp.int32, (t, LANES), 1)
        c_ref[...] = jnp.zeros_like(c_ref)
        oacc_ref[...] = jnp.zeros_like(oacc_ref)
        car_ref[...] = jnp.zeros_like(car_ref)

        def step(jj, carry):
            j = i - jj
            valid = (col + j * t) < (row + i * t)
            lsz, lk = _sb_tile_terms(qv, kt_ref[0, j], valid)
            c = c_ref[...]
            between = _split_dot(lk, tri) + jnp.tile(c, (1, t // LANES))
            a = jnp.where(valid, jnp.exp(lsz + between), 0.0)
            oacc_ref[...] += _dot(a.astype(BF16), v_ref[pl.ds(pl.multiple_of(j * t, t), t), :])
            car_ref[0] = jnp.where(lane == j, c, car_ref[0])
            c_ref[...] = c + jnp.broadcast_to(jnp.sum(lk, axis=-1, keepdims=True), (t, LANES))
            return carry

        lax.fori_loop(0, i + 1, step, 0)
        o_ref[...] = oacc_ref[...].astype(BF16)

    return pl.pallas_call(
        body, name="sb_fwd",
        grid=(N_HEADS, nk),
        in_specs=[pl.BlockSpec((t, LANES), lambda h, i: (i, h)),
                  pl.BlockSpec((1, nk, LANES, t), lambda h, i: (h, 0, 0, 0)),
                  pl.BlockSpec((s_len, LANES), lambda h, i: (0, h))],
        out_specs=[pl.BlockSpec((t, LANES), lambda h, i: (i, h)), pl.BlockSpec((1, t, LANES), lambda h, i: (h, i, 0))],
        out_shape=[jax.ShapeDtypeStruct((s_len, N_HEADS * LANES), BF16),
                   jax.ShapeDtypeStruct((N_HEADS, s_len, LANES), F32)],
        scratch_shapes=[pltpu.VMEM((t, LANES), F32), pltpu.VMEM((t, LANES), F32)],
        compiler_params=_cparams("parallel", "arbitrary"),
    )(q, kt, v)


def _sb_bwd(q, kt, k, vt, do, cars):
    s_len = q.shape[0]
    t = SB_TILE
    nk = s_len // t

    def body(q_ref, kt_ref, k_ref, vt_ref, do_ref, car_ref, dq_ref, dk_ref, dv_ref, gleft_ref, dqacc_ref):
        i = pl.program_id(1)

        @pl.when(i == 0)
        def _():
            dk_ref[...] = jnp.zeros_like(dk_ref)
            dv_ref[...] = jnp.zeros_like(dv_ref)

        qv = q_ref[...]
        dov = do_ref[...]
        row = lax.broadcasted_iota(jnp.int32, (t, t), 0)
        col = lax.broadcasted_iota(jnp.int32, (t, t), 1)
        tri_right = (row > col).astype(BF16)
        tri_left = (row < col).astype(BF16)
        lane = lax.broadcasted_iota(jnp.int32, (t, LANES), 1)
        gleft_ref[...] = jnp.zeros_like(gleft_ref)
        dqacc_ref[...] = jnp.zeros_like(dqacc_ref)

        def step(j, carry):
            rows = pl.ds(pl.multiple_of(j * t, t), t)
            valid = (col + j * t) < (row + i * t)
            lsz, lk = _sb_tile_terms(qv, kt_ref[0, j], valid)
            c = jnp.sum(jnp.where(lane == j, car_ref[0], 0.0), axis=-1, keepdims=True)
            a = jnp.where(valid, jnp.exp(lsz + _split_dot(lk, tri_right) + c), 0.0)
            g = a * _dot(dov, vt_ref[0, j])
            gleft = gleft_ref[...]
            gsum = _split_dot(g, tri_left) + jnp.tile(gleft, (1, t // LANES))
            sig = jnp.exp(lsz)
            dz = jnp.where(valid, g * (1.0 - sig) - sig * gsum, 0.0).astype(BF16)
            dk_ref[rows, :] += _dot_tn(dz, qv)
            dv_ref[rows, :] += _dot_tn(a.astype(BF16), dov)
            dqacc_ref[...] += _dot(dz, k_ref[rows, :])
            gleft_ref[...] = gleft + jnp.broadcast_to(jnp.sum(g, axis=-1, keepdims=True), (t, LANES))
            return carry

        lax.fori_loop(0, i + 1, step, 0)
        dq_ref[...] = (Q_SCALE * dqacc_ref[...]).astype(BF16)

    qblk = pl.BlockSpec((t, LANES), lambda h, i: (i, h))
    tblk = pl.BlockSpec((1, nk, LANES, t), lambda h, i: (h, 0, 0, 0))
    col_full = pl.BlockSpec((s_len, LANES), lambda h, i: (0, h))
    return pl.pallas_call(
        body, name="sb_bwd",
        grid=(N_HEADS, nk),
        in_specs=[qblk, tblk, col_full, tblk, qblk, pl.BlockSpec((1, t, LANES), lambda h, i: (h, i, 0))],
        out_specs=[qblk, col_full, col_full],
        out_shape=[jax.ShapeDtypeStruct((s_len, N_HEADS * LANES), BF16),
                   jax.ShapeDtypeStruct((s_len, N_HEADS * LANES), F32),
                   jax.ShapeDtypeStruct((s_len, N_HEADS * LANES), F32)],
        scratch_shapes=[pltpu.VMEM((t, LANES), F32), pltpu.VMEM((t, LANES), F32)],
        compiler_params=_cparams("parallel", "arbitrary"),
    )(q, kt, k, vt, do, cars)


def _pad_heads(a, heads):
    s_len = a.shape[0]
    a = a.reshape(s_len, heads, HEAD_DIM)
    return jnp.pad(a, ((0, 0), (0, 0), (0, LANES - HEAD_DIM))).reshape(s_len, heads * LANES)


def _unpad_heads(a, heads):
    s_len = a.shape[0]
    return a.reshape(s_len, heads, LANES)[:, :, :HEAD_DIM].reshape(s_len, heads * HEAD_DIM)


def _tile_transposed(a_pad, heads):
    s_len = a_pad.shape[0]
    a = a_pad.reshape(s_len // SB_TILE, SB_TILE, heads, LANES)
    return a.transpose(2, 0, 3, 1)


def _local_step(xs, tgt, gains, sinks, rel_bias, wts):
    g1, gmix, g2, gfin = gains
    bkt = _rel_bucket_matrix()

    x1, h1, a1, b1, u1 = _ffn_fwd(xs, g1, wts["ffn1_w1t"], wts["ffn1_w3t"], wts["ffn1_w2"], "1")
    hm, qa, ka, va, qb, kb, vb, ga, gb = _proj_fwd(x1, gmix, wts["w_int"])
    qa_p, ka_p, va_p = _pad_heads(qa, N_HEADS), _pad_heads(ka, SWA_KV_HEADS), _pad_heads(va, SWA_KV_HEADS)
    oa_p = _swa_fwd(rel_bias, sinks, bkt, qa_p, ka_p, va_p)
    qb_p, kb_p, vb_p = _pad_heads(qb, N_HEADS), _pad_heads(kb, N_HEADS), _pad_heads(vb, N_HEADS)
    kbt = _tile_transposed(kb_p, N_HEADS)
    ob_p, cars = _sb_fwd(qb_p, kbt, vb_p)
    oa, ob = _unpad_heads(oa_p, N_HEADS), _unpad_heads(ob_p, N_HEADS)
    x2, mg = _merge_fwd(x1, oa, ob, ga, gb, wts["w_swa"], wts["w_sb"], wts["w_out"])
    x3, h3, a3, b3, u3 = _ffn_fwd(x2, g2, wts["ffn2_w1t"], wts["ffn2_w3t"], wts["ffn2_w2"], "2")
    dx3, loss, dgfin = _loss_fwd_bwd(x3, tgt, gfin)

    big = {}
    dx2, dg2, da3, db3, dx3b = _ffn_bwd(dx3, x2, g2, a3, b3, wts["ffn2_w1t"], wts["ffn2_w3t"], wts["ffn2_w2"], "2")
    big["ffn2_w1t"] = _matmul_tn(da3, h3, "ffn2_w1")
    big["ffn2_w3t"] = _matmul_tn(db3, h3, "ffn2_w3")
    big["ffn2_w2"] = _matmul_tn(u3, dx3b, "ffn2_w2")

    doa, dob, dga, dgb, dpa, dpb, dx2b = _merge_bwd(dx2, oa, ob, ga, gb, wts["w_swa"], wts["w_sb"], wts["w_out"])
    big["w_out"] = _matmul_tn(mg, dx2b, "w_out")
    big["w_swa"] = _matmul_tn(oa, dpa, "w_swa")
    big["w_sb"] = _matmul_tn(ob, dpb, "w_sb")

    dqa_p, dka_p, dva_p, dtab, dsink = _swa_bwd(rel_bias, sinks, bkt, qa_p, ka_p, va_p, _pad_heads(doa, N_HEADS))
    vbt = _tile_transposed(vb_p, N_HEADS)
    dqb_p, dkb_p, dvb_p = _sb_bwd(qb_p, kbt, kb_p, vbt, _pad_heads(dob, N_HEADS), cars)
    dpieces = (_unpad_heads(dqa_p, N_HEADS), _unpad_heads(dka_p, SWA_KV_HEADS).astype(BF16),
               _unpad_heads(dva_p, SWA_KV_HEADS).astype(BF16), _unpad_heads(dqb_p, N_HEADS),
               _unpad_heads(dkb_p, N_HEADS).astype(BF16), _unpad_heads(dvb_p, N_HEADS).astype(BF16), dga, dgb)
    big["w_int"] = jnp.concatenate([_matmul_tn(dp, hm, f"w_in{p}") for p, dp in enumerate(dpieces)], axis=0)
    dx1, dgmix = _proj_bwd(dpieces, dx2, x1, gmix, wts["w_int"])

    dx0, dg1, da1, db1, dx1b = _ffn_bwd(dx1, xs, g1, a1, b1, wts["ffn1_w1t"], wts["ffn1_w3t"], wts["ffn1_w2"], "1")
    big["ffn1_w1t"] = _matmul_tn(da1, h1, "ffn1_w1")
    big["ffn1_w3t"] = _matmul_tn(db1, h1, "ffn1_w3")
    big["ffn1_w2"] = _matmul_tn(u1, dx1b, "ffn1_w2")

    small = {"gains": (dg1, dgmix, dg2, dgfin), "sinks": dsink[:, 0], "rel_bias": dtab[:, :N_HEADS]}
    return loss, dx0, big, small


def _my_place():
    return lax.axis_index("x"), lax.axis_index("y"), lax.axis_index("c")


def _flip(v, bit):
    return 1 - v if bit else v


_RELATIONS = tuple((k >> 2 & 1, k >> 1 & 1, k & 1) for k in range(1, N_DEV))


def _gather_weights(wp):
    def body(x_ref, out_ref, send_sems, recv_sems, local_sem):
        x, y, c = _my_place()
        me, sibling = (x, y, c), (x, y, 1 - c)
        chips = [(1 - x, y), (x, 1 - y), (1 - x, 1 - y)]

        def rows(px, py, pc):
            return out_ref.at[4 * px + 2 * py + pc]

        def copy(k, block, to, src=None):
            return pltpu.make_async_remote_copy(
                src_ref=rows(*block) if src is None else src, dst_ref=rows(*block),
                send_sem=send_sems.at[k], recv_sem=recv_sems.at[k],
                device_id=to, device_id_type=pl.DeviceIdType.MESH)

        mine = pltpu.make_async_copy(x_ref, rows(*me), local_sem)
        mine.start()
        first = [copy(0, me, sibling, src=x_ref)]
        first += [copy(1 + j, me, (*chip, c), src=x_ref) for j, chip in enumerate(chips)]
        for cp in first:
            cp.start()
        passed = [copy(4 + j, (*chip, c), sibling) for j, chip in enumerate(chips)]
        for j, chip in enumerate(chips):
            copy(1 + j, (*chip, c), me).wait_recv()
            passed[j].start()
        copy(0, sibling, me).wait_recv()
        for j, chip in enumerate(chips):
            copy(4 + j, (*chip, 1 - c), me).wait_recv()
        for cp in first + passed:
            cp.wait_send()
        mine.wait()

    return pl.pallas_call(
        body, name="gather_weights",
        out_shape=jax.ShapeDtypeStruct((N_DEV,) + wp.shape, wp.dtype),
        in_specs=[pl.BlockSpec(memory_space=pl.ANY)],
        out_specs=pl.BlockSpec(memory_space=pl.ANY),
        scratch_shapes=[pltpu.SemaphoreType.DMA((7,)), pltpu.SemaphoreType.DMA((7,)), pltpu.SemaphoreType.DMA(())],
    )(wp)


def _exchange_grads(gp):
    def body(g_ref, out_ref, send_sems, recv_sems, local_sem):
        x, y, c = _my_place()
        me = 4 * x + 2 * y + c
        mine = pltpu.make_async_copy(g_ref.at[me], out_ref.at[me], local_sem)
        mine.start()
        copies = []
        for k, (fx, fy, fc) in enumerate(_RELATIONS):
            px, py, pc = _flip(x, fx), _flip(y, fy), _flip(c, fc)
            peer = 4 * px + 2 * py + pc
            copies.append((
                pltpu.make_async_remote_copy(
                    src_ref=g_ref.at[peer], dst_ref=out_ref.at[me], send_sem=send_sems.at[k], recv_sem=recv_sems.at[k],
                    device_id=(px, py, pc), device_id_type=pl.DeviceIdType.MESH),
                pltpu.make_async_remote_copy(
                    src_ref=g_ref.at[peer], dst_ref=out_ref.at[peer], send_sem=send_sems.at[k], recv_sem=recv_sems.at[k],
                    device_id=(px, py, pc), device_id_type=pl.DeviceIdType.MESH)))
        for out_cp, _ in copies:
            out_cp.start()
        for _, in_cp in copies:
            in_cp.wait_recv()
        for out_cp, _ in copies:
            out_cp.wait_send()
        mine.wait()

    return pl.pallas_call(
        body, name="exchange_grads",
        out_shape=jax.ShapeDtypeStruct(gp.shape, gp.dtype),
        in_specs=[pl.BlockSpec(memory_space=pl.ANY)],
        out_specs=pl.BlockSpec(memory_space=pl.ANY),
        scratch_shapes=[pltpu.SemaphoreType.DMA((7,)), pltpu.SemaphoreType.DMA((7,)), pltpu.SemaphoreType.DMA(())],
    )(gp)


def _adamw(w, g, m, v):
    m = ADAM_B1 * m + (1.0 - ADAM_B1) * g
    v = ADAM_B2 * v + (1.0 - ADAM_B2) * jnp.square(g)
    m_hat = m / (1.0 - ADAM_B1 ** ADAM_STEP)
    v_hat = v / (1.0 - ADAM_B2 ** ADAM_STEP)
    delta = -ADAM_LR * (m_hat / (jnp.sqrt(v_hat) + ADAM_EPS) + ADAM_WD * w)
    return delta, m, v


def _sum_and_adamw(parts, w, m, v):
    rows = w.shape[0]
    tr = 112
    assert rows % tr == 0

    def body(p_ref, w_ref, m_ref, v_ref, g_out, d_out, m_out, v_out):
        g = p_ref[0]
        for d in range(1, N_DEV):
            g = g + p_ref[d]
        delta, mn, vn = _adamw(w_ref[...], g, m_ref[...], v_ref[...])
        g_out[...] = g
        d_out[...] = delta
        m_out[...] = mn
        v_out[...] = vn

    sp = pl.BlockSpec((tr, D_MODEL), lambda i: (i, 0))
    return pl.pallas_call(
        body, name="sum_and_adamw",
        grid=(rows // tr,),
        in_specs=[pl.BlockSpec((N_DEV, tr, D_MODEL), lambda i: (0, i, 0)), sp, sp, sp],
        out_specs=[sp] * 4,
        out_shape=[jax.ShapeDtypeStruct(w.shape, F32)] * 4,
        compiler_params=_cparams("parallel"),
    )(parts, w, m, v)


def _small_allreduce_adamw(part, w, m, v):
    def body(p_ref, w_ref, m_ref, v_ref, g_out, d_out, m_out, v_out, buf, send_sems, recv_sems):
        x, y, c = _my_place()
        me = 4 * x + 2 * y + c
        buf[me] = p_ref[...]
        copies = []
        for k, (fx, fy, fc) in enumerate(_RELATIONS):
            px, py, pc = _flip(x, fx), _flip(y, fy), _flip(c, fc)
            peer = 4 * px + 2 * py + pc
            copies.append((
                pltpu.make_async_remote_copy(
                    src_ref=buf.at[me], dst_ref=buf.at[me], send_sem=send_sems.at[k], recv_sem=recv_sems.at[k],
                    device_id=(px, py, pc), device_id_type=pl.DeviceIdType.MESH),
                pltpu.make_async_remote_copy(
                    src_ref=buf.at[me], dst_ref=buf.at[peer], send_sem=send_sems.at[k], recv_sem=recv_sems.at[k],
                    device_id=(px, py, pc), device_id_type=pl.DeviceIdType.MESH)))
        for out_cp, _ in copies:
            out_cp.start()
        for _, in_cp in copies:
            in_cp.wait_recv()
        for out_cp, _ in copies:
            out_cp.wait_send()
        g = buf[0]
        for d in range(1, N_DEV):
            g = g + buf[d]
        delta, mn, vn = _adamw(w_ref[...], g, m_ref[...], v_ref[...])
        g_out[...] = g
        d_out[...] = delta
        m_out[...] = mn
        v_out[...] = vn

    vm = pl.BlockSpec(memory_space=pltpu.VMEM)
    return pl.pallas_call(
        body, name="small_allreduce_adamw",
        in_specs=[vm] * 4, out_specs=[vm] * 4,
        out_shape=[jax.ShapeDtypeStruct(w.shape, F32)] * 4,
        scratch_shapes=[pltpu.VMEM((N_DEV,) + part.shape, F32),
                        pltpu.SemaphoreType.DMA((7,)), pltpu.SemaphoreType.DMA((7,))],
    )(part, w, m, v)


_TRANSPOSED = ("ffn1_w1", "ffn1_w3", "w_in", "ffn2_w1", "ffn2_w3")
_BRANCH = ("w_branch_swa", "w_branch_sb")


def _pack_shards(t):
    parts = []
    for name in BIG_NAMES:
        a = t[name][0]
        if name in _TRANSPOSED:
            a = a.T
        elif name in _BRANCH:
            a = a.reshape(64, D_MODEL)
        parts.append(a)
    return jnp.concatenate(parts, axis=0)


def _unpack_shards(p):
    out = {}
    for name, lo, hi in zip(BIG_NAMES, BIG_OFFS[:-1], BIG_OFFS[1:]):
        a = p[lo:hi]
        if name in _TRANSPOSED:
            a = a.T
        elif name in _BRANCH:
            a = a.reshape(512, 128)
        out[name] = a[None]
    return out


def _full_weights(wg):
    def part(name):
        k = BIG_NAMES.index(name)
        return wg[:, BIG_OFFS[k]:BIG_OFFS[k + 1]]

    def branch(name):
        return part(name).reshape(N_DEV, 512, 128).transpose(1, 0, 2).reshape(512, D_MODEL)

    return {
        "ffn1_w1t": part("ffn1_w1").reshape(D_FF, D_MODEL), "ffn1_w3t": part("ffn1_w3").reshape(D_FF, D_MODEL),
        "ffn1_w2": part("ffn1_w2").reshape(D_FF, D_MODEL), "w_int": part("w_in").reshape(IN_W, D_MODEL),
        "w_swa": branch("w_branch_swa"), "w_sb": branch("w_branch_sb"),
        "w_out": part("w_out").reshape(D_MODEL, D_MODEL),
        "ffn2_w1t": part("ffn2_w1").reshape(D_FF, D_MODEL), "ffn2_w3t": part("ffn2_w3").reshape(D_FF, D_MODEL),
        "ffn2_w2": part("ffn2_w2").reshape(D_FF, D_MODEL),
    }


def _pack_full_grads(big):
    def branch(a):
        return a.reshape(512, N_DEV, 128).transpose(1, 0, 2).reshape(N_DEV, 64, D_MODEL)

    parts = [big["ffn1_w1t"].reshape(N_DEV, 352, D_MODEL), big["ffn1_w3t"].reshape(N_DEV, 352, D_MODEL),
             big["ffn1_w2"].reshape(N_DEV, 352, D_MODEL), big["w_int"].reshape(N_DEV, 544, D_MODEL),
             branch(big["w_swa"]), branch(big["w_sb"]), big["w_out"].reshape(N_DEV, 128, D_MODEL),
             big["ffn2_w1t"].reshape(N_DEV, 352, D_MODEL), big["ffn2_w3t"].reshape(N_DEV, 352, D_MODEL),
             big["ffn2_w2"].reshape(N_DEV, 352, D_MODEL)]
    return jnp.concatenate(parts, axis=1)


_SMALL_NAMES = ("norm_ffn1", "norm_mix", "norm_ffn2", "norm_final", "swa_sinks", "rel_bias")


def _pack_small(vals):
    rows = []
    for a in vals:
        a = a.reshape(-1)
        rows.append(jnp.pad(a, (0, D_MODEL - a.shape[0])))
    rows += [jnp.zeros((D_MODEL,), F32)] * (SMALL_ROWS - len(rows))
    return jnp.stack(rows)


def _unpack_small(p):
    return {"norm_ffn1": p[0:1], "norm_mix": p[1:2], "norm_ffn2": p[2:3], "norm_final": p[3],
            "swa_sinks": p[4:5, :N_HEADS], "rel_bias": p[5, :REL_BUCKETS * N_HEADS].reshape(REL_BUCKETS, N_HEADS)}


ALL_NAMES = ("norm_ffn1", "ffn1_w1", "ffn1_w3", "ffn1_w2", "norm_mix", "w_in", "swa_sinks", "rel_bias",
             "w_branch_swa", "w_branch_sb", "w_out", "norm_ffn2", "ffn2_w1", "ffn2_w3", "ffn2_w2", "norm_final")


def kernel(x, norm_ffn1, ffn1_w1, ffn1_w3, ffn1_w2, norm_mix, w_in, swa_sinks, rel_bias, w_branch_swa, w_branch_sb, w_out, norm_ffn2, ffn2_w1, ffn2_w3, ffn2_w2, norm_final, loss_target, m_norm_ffn1, m_ffn1_w1, m_ffn1_w3, m_ffn1_w2, m_norm_mix, m_w_in, m_swa_sinks, m_rel_bias, m_w_branch_swa, m_w_branch_sb, m_w_out, m_norm_ffn2, m_ffn2_w1, m_ffn2_w3, m_ffn2_w2, m_norm_final, v_norm_ffn1, v_ffn1_w1, v_ffn1_w3, v_ffn1_w2, v_norm_mix, v_w_in, v_swa_sinks, v_rel_bias, v_w_branch_swa, v_w_branch_sb, v_w_out, v_norm_ffn2, v_ffn2_w1, v_ffn2_w3, v_ffn2_w2, v_norm_final):
    w = dict(zip(ALL_NAMES, (norm_ffn1, ffn1_w1, ffn1_w3, ffn1_w2, norm_mix, w_in, swa_sinks, rel_bias,
                             w_branch_swa, w_branch_sb, w_out, norm_ffn2, ffn2_w1, ffn2_w3, ffn2_w2, norm_final)))
    m = dict(zip(ALL_NAMES, (m_norm_ffn1, m_ffn1_w1, m_ffn1_w3, m_ffn1_w2, m_norm_mix, m_w_in, m_swa_sinks, m_rel_bias,
                             m_w_branch_swa, m_w_branch_sb, m_w_out, m_norm_ffn2, m_ffn2_w1, m_ffn2_w3, m_ffn2_w2,
                             m_norm_final)))
    v = dict(zip(ALL_NAMES, (v_norm_ffn1, v_ffn1_w1, v_ffn1_w3, v_ffn1_w2, v_norm_mix, v_w_in, v_swa_sinks, v_rel_bias,
                             v_w_branch_swa, v_w_branch_sb, v_w_out, v_norm_ffn2, v_ffn2_w1, v_ffn2_w3, v_ffn2_w2,
                             v_norm_final)))

    w_packed = _pack_shards(w)
    wts = _full_weights(_gather_weights(w_packed.astype(BF16)))
    gains = (norm_ffn1, norm_mix, norm_ffn2, norm_final.reshape(1, D_MODEL))
    loss, dx, big, small = _local_step(x[0], loss_target[0], gains, swa_sinks, rel_bias, wts)

    parts = _exchange_grads(_pack_full_grads(big))
    g_big, d_big, m_big, v_big = (_unpack_shards(p) for p in
                                  _sum_and_adamw(parts, w_packed, _pack_shards(m), _pack_shards(v)))

    small_part = _pack_small(small["gains"] + (small["sinks"], small["rel_bias"]))
    g_sm, d_sm, m_sm, v_sm = (_unpack_small(p) for p in _small_allreduce_adamw(
        small_part, _pack_small([w[n] for n in _SMALL_NAMES]), _pack_small([m[n] for n in _SMALL_NAMES]),
        _pack_small([v[n] for n in _SMALL_NAMES])))

    total_loss = lax.psum(loss[0, 0], AXES)
    outs = [total_loss, dx[None]]
    for big_d, small_d in ((g_big, g_sm), (d_big, d_sm), (m_big, m_sm), (v_big, v_sm)):
        merged = {**big_d, **small_d}
        outs += [merged[n] for n in ALL_NAMES]
    return tuple(outs)
```

```python
import functools

import jax
import jax.numpy as jnp
import numpy as np
from jax import lax
from jax.experimental import pallas as pl
from jax.experimental.pallas import tpu as pltpu

F32 = jnp.float32
BF16 = jnp.bfloat16

D_MODEL = 1024
D_FF = 2816
HEAD_DIM = 64
N_HEADS = 8
SWA_KV_HEADS = 2
SWA_GROUP = 4
SWA_BLOCK = 128
REL_BUCKETS = 32
REL_MAX_DIST = 128
RMS_EPS = 1e-6
NEG_BIG = -1e30
Q_SCALE = HEAD_DIM ** -0.5
LANES = 128

N_DEV = 8
AXES = ("x", "y", "c")

ADAM_LR = 0.001
ADAM_B1 = 0.9
ADAM_B2 = 0.999
ADAM_EPS = 1e-08
ADAM_WD = 0.01
ADAM_STEP = 10

IN_SIZES = (512, 128, 128, 512, 512, 512, 1024, 1024)
IN_OFFS = tuple(int(v) for v in np.cumsum((0,) + IN_SIZES))
IN_W = IN_OFFS[-1]

BIG_NAMES = ("ffn1_w1", "ffn1_w3", "ffn1_w2", "w_in", "w_branch_swa", "w_branch_sb", "w_out",
             "ffn2_w1", "ffn2_w3", "ffn2_w2")
BIG_ROWS = (352, 352, 352, 544, 64, 64, 128, 352, 352, 352)
BIG_OFFS = tuple(int(v) for v in np.cumsum((0,) + BIG_ROWS))
PACK_ROWS = BIG_OFFS[-1]
SMALL_ROWS = 8

VMEM_LIMIT = 56 * 1024 * 1024
SB_TILE = 256
SB_ROWS = 128


def _dot(a, b):
    return jnp.dot(a, b, preferred_element_type=F32)


def _dot_nt(a, b):
    return lax.dot_general(a, b, (((1,), (1,)), ((), ())), preferred_element_type=F32)


def _dot_tn(a, b):
    return lax.dot_general(a, b, (((0,), (0,)), ((), ())), preferred_element_type=F32)


def _cparams(*sem):
    return pltpu.CompilerParams(dimension_semantics=sem, vmem_limit_bytes=VMEM_LIMIT)


def _rms_rstd(xv):
    return lax.rsqrt(jnp.mean(xv * xv, axis=-1, keepdims=True) + RMS_EPS)


def _rms_bwd(dh, xv, r, g):
    xhat = xv * r
    dg = jnp.sum(dh * xhat, axis=0, keepdims=True)
    dxn = dh * g
    dx = r * (dxn - xhat * jnp.mean(dxn * xhat, axis=-1, keepdims=True))
    return dx, dg


def _ffn_fwd(x, g, w1t, w3t, w2, tag):
    s_len = x.shape[0]
    tm, tf = min(1024, s_len), 256
    nf = D_FF // tf

    def body(x_ref, g_ref, w1_ref, w3_ref, w2_ref, xo_ref, h_ref, a_ref, b_ref, u_ref, acc_ref, hs_ref):
        j = pl.program_id(1)

        @pl.when(j == 0)
        def _():
            xv = x_ref[...]
            h = (xv * _rms_rstd(xv) * g_ref[...]).astype(BF16)
            hs_ref[...] = h
            h_ref[...] = h
            acc_ref[...] = jnp.zeros_like(acc_ref)

        h = hs_ref[...]
        a = _dot_nt(h, w1_ref[...])
        b = _dot_nt(h, w3_ref[...])
        a_ref[...] = a.astype(BF16)
        b_ref[...] = b.astype(BF16)
        uh = (0.5 * (a * jax.nn.sigmoid(a) * b)).astype(BF16)
        u_ref[...] = uh
        acc_ref[...] += _dot(uh, w2_ref[...])

        @pl.when(j == nf - 1)
        def _():
            xo_ref[...] = x_ref[...] + acc_ref[...]

    row = lambda i, j: (i, 0)
    return pl.pallas_call(
        body, name=f"ffn_fwd_{tag}",
        grid=(s_len // tm, nf),
        in_specs=[pl.BlockSpec((tm, D_MODEL), row), pl.BlockSpec((1, D_MODEL), lambda i, j: (0, 0)),
                  pl.BlockSpec((tf, D_MODEL), lambda i, j: (j, 0)), pl.BlockSpec((tf, D_MODEL), lambda i, j: (j, 0)),
                  pl.BlockSpec((tf, D_MODEL), lambda i, j: (j, 0))],
        out_specs=[pl.BlockSpec((tm, D_MODEL), row), pl.BlockSpec((tm, D_MODEL), row),
                   pl.BlockSpec((tm, tf), lambda i, j: (i, j)), pl.BlockSpec((tm, tf), lambda i, j: (i, j)),
                   pl.BlockSpec((tm, tf), lambda i, j: (i, j))],
        out_shape=[jax.ShapeDtypeStruct((s_len, D_MODEL), F32), jax.ShapeDtypeStruct((s_len, D_MODEL), BF16),
                   jax.ShapeDtypeStruct((s_len, D_FF), BF16), jax.ShapeDtypeStruct((s_len, D_FF), BF16),
                   jax.ShapeDtypeStruct((s_len, D_FF), BF16)],
        scratch_shapes=[pltpu.VMEM((tm, D_MODEL), F32), pltpu.VMEM((tm, D_MODEL), BF16)],
        compiler_params=_cparams("parallel", "arbitrary"),
    )(x, g, w1t, w3t, w2)


def _ffn_bwd(dy, x, g, a, b, w1t, w3t, w2, tag):
    s_len = x.shape[0]
    tm, tf = min(1024, s_len), 256
    nf = D_FF // tf

    def body(dy_ref, x_ref, g_ref, a_ref, b_ref, w1_ref, w3_ref, w2_ref,
             dx_ref, dg_ref, da_ref, db_ref, dyb_ref, acc_ref, dys_ref):
        i, j = pl.program_id(0), pl.program_id(1)

        @pl.when(j == 0)
        def _():
            dyb = dy_ref[...].astype(BF16)
            dys_ref[...] = dyb
            dyb_ref[...] = dyb
            acc_ref[...] = jnp.zeros_like(acc_ref)

        @pl.when((i == 0) & (j == 0))
        def _():
            dg_ref[...] = jnp.zeros_like(dg_ref)

        du = 0.5 * _dot_nt(dys_ref[...], w2_ref[...])
        av = a_ref[...].astype(F32)
        bv = b_ref[...].astype(F32)
        sg = jax.nn.sigmoid(av)
        sil = av * sg
        da = (du * bv * (sg + sil * (1.0 - sg))).astype(BF16)
        db = (du * sil).astype(BF16)
        da_ref[...] = da
        db_ref[...] = db
        acc_ref[...] += _dot(da, w1_ref[...]) + _dot(db, w3_ref[...])

        @pl.when(j == nf - 1)
        def _():
            xv = x_ref[...]
            dx, dg = _rms_bwd(acc_ref[...], xv, _rms_rstd(xv), g_ref[...])
            dx_ref[...] = dy_ref[...] + dx
            dg_ref[...] += dg

    row = lambda i, j: (i, 0)
    blk = lambda i, j: (i, j)
    wsp = pl.BlockSpec((tf, D_MODEL), lambda i, j: (j, 0))
    return pl.pallas_call(
        body, name=f"ffn_bwd_{tag}",
        grid=(s_len // tm, nf),
        in_specs=[pl.BlockSpec((tm, D_MODEL), row), pl.BlockSpec((tm, D_MODEL), row),
                  pl.BlockSpec((1, D_MODEL), lambda i, j: (0, 0)),
                  pl.BlockSpec((tm, tf), blk), pl.BlockSpec((tm, tf), blk), wsp, wsp, wsp],
        out_specs=[pl.BlockSpec((tm, D_MODEL), row), pl.BlockSpec((1, D_MODEL), lambda i, j: (0, 0)),
                   pl.BlockSpec((tm, tf), blk), pl.BlockSpec((tm, tf), blk), pl.BlockSpec((tm, D_MODEL), row)],
        out_shape=[jax.ShapeDtypeStruct((s_len, D_MODEL), F32), jax.ShapeDtypeStruct((1, D_MODEL), F32),
                   jax.ShapeDtypeStruct((s_len, D_FF), BF16), jax.ShapeDtypeStruct((s_len, D_FF), BF16),
                   jax.ShapeDtypeStruct((s_len, D_MODEL), BF16)],
        scratch_shapes=[pltpu.VMEM((tm, D_MODEL), F32), pltpu.VMEM((tm, D_MODEL), BF16)],
        compiler_params=_cparams("arbitrary", "arbitrary"),
    )(dy, x, g, a, b, w1t, w3t, w2)


def _matmul_tn(lhs, rhs, tag):
    s_len, m = lhs.shape
    n = rhs.shape[1]
    tm = min(512, s_len)
    tj = m if m <= 1024 else 1408
    assert m % tj == 0

    def body(l_ref, r_ref, o_ref):
        @pl.when(pl.program_id(1) == 0)
        def _():
            o_ref[...] = jnp.zeros_like(o_ref)

        o_ref[...] += _dot_tn(l_ref[...], r_ref[...])

    return pl.pallas_call(
        body, name=f"matmul_tn_{tag}",
        grid=(m // tj, s_len // tm),
        in_specs=[pl.BlockSpec((tm, tj), lambda j, i: (i, j)), pl.BlockSpec((tm, n), lambda j, i: (i, 0))],
        out_specs=pl.BlockSpec((tj, n), lambda j, i: (j, 0)),
        out_shape=jax.ShapeDtypeStruct((m, n), F32),
        compiler_params=_cparams("parallel", "arbitrary"),
    )(lhs, rhs)


def _proj_fwd(x1, g, wint):
    s_len = x1.shape[0]
    tm = min(512, s_len)
    dts = (BF16, BF16, BF16, BF16, BF16, BF16, F32, F32)

    def body(x_ref, g_ref, w_ref, h_ref, *outs):
        xv = x_ref[...]
        h = (xv * _rms_rstd(xv) * g_ref[...]).astype(BF16)
        h_ref[...] = h
        for p, o_ref in enumerate(outs):
            val = _dot_nt(h, w_ref[IN_OFFS[p]:IN_OFFS[p + 1], :])
            if p == 3:
                val = val * Q_SCALE
            o_ref[...] = val.astype(dts[p])

    row = lambda i: (i, 0)
    return pl.pallas_call(
        body, name="proj_fwd",
        grid=(s_len // tm,),
        in_specs=[pl.BlockSpec((tm, D_MODEL), row), pl.BlockSpec((1, D_MODEL), lambda i: (0, 0)),
                  pl.BlockSpec((IN_W, D_MODEL), lambda i: (0, 0))],
        out_specs=[pl.BlockSpec((tm, D_MODEL), row)] + [pl.BlockSpec((tm, w), row) for w in IN_SIZES],
        out_shape=[jax.ShapeDtypeStruct((s_len, D_MODEL), BF16)]
        + [jax.ShapeDtypeStruct((s_len, w), dt) for w, dt in zip(IN_SIZES, dts)],
        compiler_params=_cparams("parallel"),
    )(x1, g, wint)


def _proj_bwd(dpieces, dx2, x1, g, wint):
    s_len = x1.shape[0]
    tm = min(512, s_len)

    def body(*refs):
        dps = refs[:8]
        dx2_ref, x_ref, g_ref, w_ref, dx_ref, dg_ref = refs[8:]

        @pl.when(pl.program_id(0) == 0)
        def _():
            dg_ref[...] = jnp.zeros_like(dg_ref)

        dh = _dot(dps[0][...], w_ref[IN_OFFS[0]:IN_OFFS[1], :])
        for p in range(1, 8):
            dh += _dot(dps[p][...], w_ref[IN_OFFS[p]:IN_OFFS[p + 1], :])
        xv = x_ref[...]
        dx, dg = _rms_bwd(dh, xv, _rms_rstd(xv), g_ref[...])
        dx_ref[...] = dx2_ref[...] + dx
        dg_ref[...] += dg

    row = lambda i: (i, 0)
    return pl.pallas_call(
        body, name="proj_bwd",
        grid=(s_len // tm,),
        in_specs=[pl.BlockSpec((tm, w), row) for w in IN_SIZES]
        + [pl.BlockSpec((tm, D_MODEL), row), pl.BlockSpec((tm, D_MODEL), row),
           pl.BlockSpec((1, D_MODEL), lambda i: (0, 0)), pl.BlockSpec((IN_W, D_MODEL), lambda i: (0, 0))],
        out_specs=[pl.BlockSpec((tm, D_MODEL), row), pl.BlockSpec((1, D_MODEL), lambda i: (0, 0))],
        out_shape=[jax.ShapeDtypeStruct((s_len, D_MODEL), F32), jax.ShapeDtypeStruct((1, D_MODEL), F32)],
        compiler_params=_cparams("arbitrary"),
    )(*dpieces, dx2, x1, g, wint)


def _merge_fwd(x1, oa, ob, ga, gb, wswa, wsb, wout):
    s_len = x1.shape[0]
    tm = min(512, s_len)

    def body(x_ref, oa_ref, ob_ref, ga_ref, gb_ref, wa_ref, wb_ref, wo_ref, xo_ref, mg_ref):
        pa = _dot(oa_ref[...], wa_ref[...])
        pb = _dot(ob_ref[...], wb_ref[...])
        mg = (jax.nn.sigmoid(ga_ref[...]) * pa + jax.nn.sigmoid(gb_ref[...]) * pb).astype(BF16)
        mg_ref[...] = mg
        xo_ref[...] = x_ref[...] + _dot(mg, wo_ref[...])

    row = lambda i: (i, 0)
    full = lambda i: (0, 0)
    return pl.pallas_call(
        body, name="merge_fwd",
        grid=(s_len // tm,),
        in_specs=[pl.BlockSpec((tm, D_MODEL), row), pl.BlockSpec((tm, 512), row), pl.BlockSpec((tm, 512), row),
                  pl.BlockSpec((tm, D_MODEL), row), pl.BlockSpec((tm, D_MODEL), row),
                  pl.BlockSpec((512, D_MODEL), full), pl.BlockSpec((512, D_MODEL), full),
                  pl.BlockSpec((D_MODEL, D_MODEL), full)],
        out_specs=[pl.BlockSpec((tm, D_MODEL), row), pl.BlockSpec((tm, D_MODEL), row)],
        out_shape=[jax.ShapeDtypeStruct((s_len, D_MODEL), F32), jax.ShapeDtypeStruct((s_len, D_MODEL), BF16)],
        compiler_params=_cparams("parallel"),
    )(x1, oa, ob, ga, gb, wswa, wsb, wout)


def _merge_bwd(dx2, oa, ob, ga, gb, wswa, wsb, wout):
    s_len = dx2.shape[0]
    tm = min(512, s_len)

    def body(dx_ref, oa_ref, ob_ref, ga_ref, gb_ref, wa_ref, wb_ref, wo_ref,
             doa_ref, dob_ref, dga_ref, dgb_ref, dpa_ref, dpb_ref, dxb_ref):
        dxb = dx_ref[...].astype(BF16)
        dxb_ref[...] = dxb
        dmg = _dot_nt(dxb, wo_ref[...])
        for o_ref, g_ref, w_ref, do_ref, dg_ref, dp_ref in (
                (oa_ref, ga_ref, wa_ref, doa_ref, dga_ref, dpa_ref),
                (ob_ref, gb_ref, wb_ref, dob_ref, dgb_ref, dpb_ref)):
            pv = _dot(o_ref[...], w_ref[...])
            sg = jax.nn.sigmoid(g_ref[...])
            dp = (dmg * sg).astype(BF16)
            dp_ref[...] = dp
            dg_ref[...] = (dmg * pv * sg * (1.0 - sg)).astype(BF16)
            do_ref[...] = _dot_nt(dp, w_ref[...]).astype(BF16)

    row = lambda i: (i, 0)
    full = lambda i: (0, 0)
    wide = pl.BlockSpec((tm, D_MODEL), row)
    half = pl.BlockSpec((tm, 512), row)
    return pl.pallas_call(
        body, name="merge_bwd",
        grid=(s_len // tm,),
        in_specs=[wide, half, half, wide, wide, pl.BlockSpec((512, D_MODEL), full),
                  pl.BlockSpec((512, D_MODEL), full), pl.BlockSpec((D_MODEL, D_MODEL), full)],
        out_specs=[half, half, wide, wide, wide, wide, wide],
        out_shape=[jax.ShapeDtypeStruct((s_len, 512), BF16)] * 2 + [jax.ShapeDtypeStruct((s_len, D_MODEL), BF16)] * 5,
        compiler_params=_cparams("parallel"),
    )(dx2, oa, ob, ga, gb, wswa, wsb, wout)


def _loss_fwd_bwd(x3, tgt, g):
    s_len = x3.shape[0]
    tm = min(1024, s_len)

    def body(x_ref, t_ref, g_ref, dx_ref, loss_ref, dg_ref):
        @pl.when(pl.program_id(0) == 0)
        def _():
            loss_ref[...] = jnp.zeros_like(loss_ref)
            dg_ref[...] = jnp.zeros_like(dg_ref)

        xv = x_ref[...]
        gv = g_ref[...]
        r = _rms_rstd(xv)
        err = xv * r * gv - t_ref[...]
        loss_ref[...] += 0.5 * jnp.sum(jnp.mean(err * err, axis=-1, keepdims=True), axis=0, keepdims=True)
        dx, dg = _rms_bwd(err * (1.0 / D_MODEL), xv, r, gv)
        dx_ref[...] = dx
        dg_ref[...] += dg

    row = lambda i: (i, 0)
    return pl.pallas_call(
        body, name="loss_fwd_bwd",
        grid=(s_len // tm,),
        in_specs=[pl.BlockSpec((tm, D_MODEL), row), pl.BlockSpec((tm, D_MODEL), row),
                  pl.BlockSpec((1, D_MODEL), lambda i: (0, 0))],
        out_specs=[pl.BlockSpec((tm, D_MODEL), row), pl.BlockSpec((1, 1), lambda i: (0, 0)),
                   pl.BlockSpec((1, D_MODEL), lambda i: (0, 0))],
        out_shape=[jax.ShapeDtypeStruct((s_len, D_MODEL), F32), jax.ShapeDtypeStruct((1, 1), F32),
                   jax.ShapeDtypeStruct((1, D_MODEL), F32)],
        compiler_params=_cparams("arbitrary"),
    )(x3, tgt, g)


def _rel_bucket_matrix():
    qi = jnp.arange(SWA_BLOCK)[:, None] + SWA_BLOCK
    kj = jnp.arange(2 * SWA_BLOCK)[None, :]
    dist = jnp.maximum(qi - kj, 0)
    max_exact = REL_BUCKETS // 2
    d = jnp.maximum(dist, 1).astype(F32)
    large = max_exact + (jnp.log(d / max_exact) / np.log(REL_MAX_DIST / max_exact)
                         * (REL_BUCKETS - max_exact)).astype(jnp.int32)
    large = jnp.minimum(large, REL_BUCKETS - 1)
    return jnp.where(dist < max_exact, dist, large).astype(jnp.int32)


def _swa_bias_into(bias_ref, bkt_ref, tab_ref):
    bk = bkt_ref[...]
    for h in range(N_HEADS):
        acc = jnp.zeros(bk.shape, F32)
        for bucket in range(REL_BUCKETS):
            acc = jnp.where(bk == bucket, tab_ref[bucket, h], acc)
        bias_ref[h] = acc


def _swa_valid(n):
    shape = (SWA_BLOCK, 2 * SWA_BLOCK)
    row = lax.broadcasted_iota(jnp.int32, shape, 0)
    col = lax.broadcasted_iota(jnp.int32, shape, 1)
    dist = row + SWA_BLOCK - col
    return (dist >= 0) & (dist < SWA_BLOCK) & ((col >= SWA_BLOCK) | (n > 0))


def _swa_probs(q, k, bias, sink, valid):
    lg = jnp.where(valid, _dot_nt(q, k) * Q_SCALE + bias, NEG_BIG)
    m = jnp.maximum(jnp.max(lg, axis=-1, keepdims=True), sink)
    e = jnp.exp(lg - m)
    es = jnp.exp(sink - m)
    den = jnp.sum(e, axis=-1, keepdims=True) + es
    return e / den, es / den


def _swa_specs(s_len):
    blk = SWA_BLOCK
    cur = lambda n: (n, 0)
    prev = lambda n: (jnp.maximum(n - 1, 0), 0)
    kvw = SWA_KV_HEADS * LANES
    return [pl.BlockSpec(memory_space=pltpu.SMEM), pl.BlockSpec(memory_space=pltpu.SMEM),
            pl.BlockSpec((blk, 2 * blk), lambda n: (0, 0)),
            pl.BlockSpec((blk, N_HEADS * LANES), cur),
            pl.BlockSpec((blk, kvw), prev), pl.BlockSpec((blk, kvw), cur),
            pl.BlockSpec((blk, kvw), prev), pl.BlockSpec((blk, kvw), cur)]


def _swa_fwd(tab, sinks, bkt, q, k, v):
    s_len = q.shape[0]
    blk = SWA_BLOCK

    def body(tab_ref, sink_ref, bkt_ref, q_ref, kp_ref, kc_ref, vp_ref, vc_ref, o_ref, bias_ref):
        n = pl.program_id(0)

        @pl.when(n == 0)
        def _():
            _swa_bias_into(bias_ref, bkt_ref, tab_ref)

        valid = _swa_valid(n)
        for grp in range(SWA_KV_HEADS):
            gl = slice(grp * LANES, (grp + 1) * LANES)
            kk = jnp.concatenate([kp_ref[:, gl], kc_ref[:, gl]], axis=0)
            vv = jnp.concatenate([vp_ref[:, gl], vc_ref[:, gl]], axis=0)
            for hh in range(SWA_GROUP):
                h = grp * SWA_GROUP + hh
                hl = slice(h * LANES, (h + 1) * LANES)
                p, _ = _swa_probs(q_ref[:, hl], kk, bias_ref[h], sink_ref[0, h], valid)
                o_ref[:, hl] = _dot(p.astype(BF16), vv).astype(BF16)

    return pl.pallas_call(
        body, name="swa_fwd",
        grid=(s_len // blk,),
        in_specs=_swa_specs(s_len),
        out_specs=pl.BlockSpec((blk, N_HEADS * LANES), lambda n: (n, 0)),
        out_shape=jax.ShapeDtypeStruct((s_len, N_HEADS * LANES), BF16),
        scratch_shapes=[pltpu.VMEM((N_HEADS, blk, 2 * blk), F32)],
        compiler_params=_cparams("arbitrary"),
    )(tab, sinks, bkt, q, k, k, v, v)


def _swa_bwd(tab, sinks, bkt, q, k, v, do):
    s_len = q.shape[0]
    blk = SWA_BLOCK
    nb = s_len // blk
    kvw = SWA_KV_HEADS * LANES

    def body(tab_ref, sink_ref, bkt_ref, q_ref, kp_ref, kc_ref, vp_ref, vc_ref, do_ref,
             dq_ref, dk_ref, dv_ref, dtab_ref, dsink_ref, bias_ref, dbias_ref):
        n = pl.program_id(0)

        @pl.when(n == 0)
        def _():
            _swa_bias_into(bias_ref, bkt_ref, tab_ref)
            dbias_ref[...] = jnp.zeros_like(dbias_ref)
            dk_ref[...] = jnp.zeros_like(dk_ref)
            dv_ref[...] = jnp.zeros_like(dv_ref)
            dsink_ref[...] = jnp.zeros_like(dsink_ref)
            dtab_ref[...] = jnp.zeros_like(dtab_ref)

        valid = _swa_valid(n)
        cur_rows = pl.ds(pl.multiple_of(n * blk, blk), blk)
        prev_rows = pl.ds(pl.multiple_of(jnp.maximum(n - 1, 0) * blk, blk), blk)
        for grp in range(SWA_KV_HEADS):
            gl = slice(grp * LANES, (grp + 1) * LANES)
            kk = jnp.concatenate([kp_ref[:, gl], kc_ref[:, gl]], axis=0)
            vv = jnp.concatenate([vp_ref[:, gl], vc_ref[:, gl]], axis=0)
            dk_acc = jnp.zeros((2 * blk, LANES), F32)
            dv_acc = jnp.zeros((2 * blk, LANES), F32)
            for hh in range(SWA_GROUP):
                h = grp * SWA_GROUP + hh
                hl = slice(h * LANES, (h + 1) * LANES)
                qh = q_ref[:, hl]
                doh = do_ref[:, hl]
                p, ps = _swa_probs(qh, kk, bias_ref[h], sink_ref[0, h], valid)
                dp = _dot_nt(doh, vv)
                delta = jnp.sum(p * dp, axis=-1, keepdims=True)
                dl = p * (dp - delta)
                dsink_ref[h:h + 1, :] += jnp.broadcast_to(-jnp.sum(ps * delta, axis=0, keepdims=True), (1, LANES))
                dbias_ref[h] += dl
                dlb = dl.astype(BF16)
                dq_ref[:, hl] = (Q_SCALE * _dot(dlb, kk)).astype(BF16)
                dk_acc += Q_SCALE * _dot_tn(dlb, qh)
                dv_acc += _dot_tn(p.astype(BF16), doh)
            dk_ref[cur_rows, gl] += dk_acc[blk:]
            dv_ref[cur_rows, gl] += dv_acc[blk:]

            @pl.when(n > 0)
            def _():
                dk_ref[prev_rows, gl] += dk_acc[:blk]
                dv_ref[prev_rows, gl] += dv_acc[:blk]

        @pl.when(n == nb - 1)
        def _():
            bk = bkt_ref[...]
            lane = lax.broadcasted_iota(jnp.int32, (1, LANES), 1)
            for bucket in range(REL_BUCKETS):
                rowv = jnp.zeros((1, LANES), F32)
                for h in range(N_HEADS):
                    val = jnp.sum(jnp.where(bk == bucket, dbias_ref[h], 0.0), axis=1, keepdims=True)
                    val = jnp.sum(val, axis=0, keepdims=True)
                    rowv = jnp.where(lane == h, val, rowv)
                dtab_ref[bucket:bucket + 1, :] = rowv

    return pl.pallas_call(
        body, name="swa_bwd",
        grid=(nb,),
        in_specs=_swa_specs(s_len) + [pl.BlockSpec((blk, N_HEADS * LANES), lambda n: (n, 0))],
        out_specs=[pl.BlockSpec((blk, N_HEADS * LANES), lambda n: (n, 0)),
                   pl.BlockSpec((s_len, kvw), lambda n: (0, 0)), pl.BlockSpec((s_len, kvw), lambda n: (0, 0)),
                   pl.BlockSpec((REL_BUCKETS, LANES), lambda n: (0, 0)), pl.BlockSpec((N_HEADS, LANES), lambda n: (0, 0))],
        out_shape=[jax.ShapeDtypeStruct((s_len, N_HEADS * LANES), BF16),
                   jax.ShapeDtypeStruct((s_len, kvw), F32), jax.ShapeDtypeStruct((s_len, kvw), F32),
                   jax.ShapeDtypeStruct((REL_BUCKETS, LANES), F32), jax.ShapeDtypeStruct((N_HEADS, LANES), F32)],
        scratch_shapes=[pltpu.VMEM((N_HEADS, blk, 2 * blk), F32), pltpu.VMEM((N_HEADS, blk, 2 * blk), F32)],
        compiler_params=_cparams("arbitrary"),
    )(tab, sinks, bkt, q, k, k, v, v, do)


def _split_dot(vals, tri):
    hi = vals.astype(BF16)
    lo = (vals - hi.astype(F32)).astype(BF16)
    return _dot(hi, tri) + _dot(lo, tri)


def _sb_terms(z, valid):
    lk = -jnp.log(1.0 + jnp.exp(-jnp.abs(z))) - jnp.maximum(z, 0.0)
    lsz = lk + z
    return lsz, (lk if valid is None else jnp.where(valid, lk, 0.0))


def _row_sum_lanes(vals, t):
    return jnp.broadcast_to(jnp.sum(vals, axis=-1, keepdims=True), (t, LANES))


def _emit_skewed(*groups):
    for step in range(max(len(items) + len(stages) - 1 for items, stages in groups)):
        for items, stages in groups:
            for s, stage in enumerate(stages):
                if 0 <= step - s < len(items):
                    stage(items[step - s])


def _sb_fwd(q, kt, v):
    s_len = q.shape[0]
    t, tr = SB_TILE, SB_ROWS
    nk = s_len // t
    assert nk <= LANES

    def body(q_ref, kt_ref, v_ref, o_ref, car_ref, c_ref, oacc_ref, logw_ref, lksum_ref):
        i = pl.program_id(1)
        qv = q_ref[...]
        lane = lax.broadcasted_iota(jnp.int32, (t, LANES), 1)
        low = lane < HEAD_DIM
        zero = jnp.zeros_like(qv)
        q_heads = (jnp.where(low, qv, zero), jnp.where(low, zero, qv))
        row = lax.broadcasted_iota(jnp.int32, (t, t), 0)
        col = lax.broadcasted_iota(jnp.int32, (t, t), 1)
        tri = (row > col).astype(BF16)
        c_ref[...] = jnp.zeros_like(c_ref)
        oacc_ref[...] = jnp.zeros_like(oacc_ref)
        car_ref[...] = jnp.zeros_like(car_ref)

        items = [(h, r) for h in range(2) for r in range(t // tr)]

        def rows_of(w):
            return slice(w[1] * tr, (w[1] + 1) * tr)

        def front(j, slot, masked):
            ktv = kt_ref[0, j]
            st = {}

            def s_logits(w):
                st[w, "z"] = _dot(q_heads[w[0]][rows_of(w)], ktv)

            def s_terms(w):
                valid = (col[rows_of(w)] < row[rows_of(w)]) if masked else None
                lsz, lk = _sb_terms(st.pop((w, "z")), valid)
                hi = lk.astype(BF16)
                st[w, "hi"], st[w, "lo"] = hi, (lk - hi.astype(F32)).astype(BF16)
                st[w, "lsz"] = lsz if valid is None else jnp.where(valid, lsz, NEG_BIG)
                lksum_ref[slot, w[0], rows_of(w), :] = _row_sum_lanes(lk, tr)

            def s_suffix(w):
                rc = _dot(st.pop((w, "hi")), tri) + _dot(st.pop((w, "lo")), tri)
                logw_ref[slot, w[0], rows_of(w), :] = st.pop((w, "lsz")) + rc

            return items, [s_logits, s_terms, s_suffix]

        def back(j, slot):
            vv = v_ref[pl.ds(pl.multiple_of(j * t, t), t), :]
            st = {}

            def s_weights(w):
                h, rs = w[0], rows_of(w)
                c = c_ref[h, rs, :]
                st[w, "a"] = jnp.exp(logw_ref[slot, h, rs, :] + jnp.tile(c, (1, t // LANES))).astype(BF16)
                car_ref[h, rs, :] = jnp.where(lane[rs] == j, c, car_ref[h, rs, :])
                c_ref[h, rs, :] = c + lksum_ref[slot, h, rs, :]

            def s_values(w):
                oacc_ref[w[0], rows_of(w), :] += _dot(st.pop((w, "a")), vv)

            return items, [s_weights, s_values]

        _emit_skewed(front(i, 0, True))

        def step(jj, carry):
            _emit_skewed(front(i - jj, jj % 2, False), back(i - jj + 1, (jj - 1) % 2))
            return carry

        lax.fori_loop(1, i + 1, step, 0)
        _emit_skewed(back(0, i % 2))
        o_ref[...] = jnp.where(low, oacc_ref[0], oacc_ref[1]).astype(BF16)

    return pl.pallas_call(
        body, name="sb_fwd",
        grid=(N_HEADS // 2, nk),
        in_specs=[pl.BlockSpec((t, LANES), lambda p, i: (i, p)),
                  pl.BlockSpec((1, nk, LANES, t), lambda p, i: (p, 0, 0, 0)),
                  pl.BlockSpec((s_len, LANES), lambda p, i: (0, p))],
        out_specs=[pl.BlockSpec((t, LANES), lambda p, i: (i, p)), pl.BlockSpec((2, t, LANES), lambda p, i: (p, i, 0))],
        out_shape=[jax.ShapeDtypeStruct((s_len, N_HEADS * HEAD_DIM), BF16),
                   jax.ShapeDtypeStruct((N_HEADS, s_len, LANES), F32)],
        scratch_shapes=[pltpu.VMEM((2, t, LANES), F32), pltpu.VMEM((2, t, LANES), F32),
                        pltpu.VMEM((2, 2, t, t), F32), pltpu.VMEM((2, 2, t, LANES), F32)],
        compiler_params=_cparams("parallel", "arbitrary"),
    )(q, kt, v)


def _sb_bwd(q, kt, k, vt, do, cars):
    s_len = q.shape[0]
    t, tr = SB_TILE, SB_ROWS
    nk = s_len // t

    def body(q_ref, kt_ref, k_ref, vt_ref, do_ref, car_ref, dq_ref, dk_ref, dv_ref,
             gleft_ref, dqacc_ref, dkacc_ref, dvacc_ref, logw_ref, lsz_ref, da_ref):
        i = pl.program_id(1)

        @pl.when(i == 0)
        def _():
            dkacc_ref[...] = jnp.zeros_like(dkacc_ref)
            dvacc_ref[...] = jnp.zeros_like(dvacc_ref)

        qv = q_ref[...]
        dov = do_ref[...]
        lane = lax.broadcasted_iota(jnp.int32, (t, LANES), 1)
        low = lane < HEAD_DIM
        zero = jnp.zeros_like(qv)
        q_heads = (jnp.where(low, qv, zero), jnp.where(low, zero, qv))
        do_heads = (jnp.where(low, dov, zero), jnp.where(low, zero, dov))
        row = lax.broadcasted_iota(jnp.int32, (t, t), 0)
        col = lax.broadcasted_iota(jnp.int32, (t, t), 1)
        tri_right = (row > col).astype(BF16)
        tri_left = (row < col).astype(BF16)
        gleft_ref[...] = jnp.zeros_like(gleft_ref)
        dqacc_ref[...] = jnp.zeros_like(dqacc_ref)

        items = [(h, r) for h in range(2) for r in range(t // tr)]

        def rows_of(w):
            return slice(w[1] * tr, (w[1] + 1) * tr)

        def front(j, slot, masked):
            ktv = kt_ref[0, j]
            vtv = vt_ref[0, j]
            st = {}

            def s_logits(w):
                h, rs = w[0], rows_of(w)
                st[w, "z"] = _dot(q_heads[h][rs], ktv)
                da_ref[slot, h, rs, :] = _dot(do_heads[h][rs], vtv)

            def s_terms(w):
                h, rs = w[0], rows_of(w)
                valid = (col[rs] < row[rs]) if masked else None
                lsz, lk = _sb_terms(st.pop((w, "z")), valid)
                hi = lk.astype(BF16)
                st[w, "hi"], st[w, "lo"] = hi, (lk - hi.astype(F32)).astype(BF16)
                lsz = lsz if valid is None else jnp.where(valid, lsz, NEG_BIG)
                lsz_ref[slot, h, rs, :] = lsz
                st[w, "lszc"] = lsz + jnp.sum(jnp.where(lane[rs] == j, car_ref[h, rs, :], 0.0), axis=-1, keepdims=True)

            def s_suffix(w):
                rc = _dot(st.pop((w, "hi")), tri_right) + _dot(st.pop((w, "lo")), tri_right)
                logw_ref[slot, w[0], rows_of(w), :] = st.pop((w, "lszc")) + rc

            return items, [s_logits, s_terms, s_suffix]

        def back(j, slot):
            key_rows = pl.ds(pl.multiple_of(j * t, t), t)
            kv = k_ref[key_rows, :]
            st = {}

            def s_weights(w):
                h, rs = w[0], rows_of(w)
                a = jnp.exp(logw_ref[slot, h, rs, :])
                g = a * da_ref[slot, h, rs, :]
                hi = g.astype(BF16)
                st[w, "a"], st[w, "g"], st[w, "hi"] = a.astype(BF16), g, hi
                st[w, "lo"] = (g - hi.astype(F32)).astype(BF16)

            def s_prefix(w):
                st[w, "gs"] = _dot(st.pop((w, "hi")), tri_left) + _dot(st.pop((w, "lo")), tri_left)

            def s_dz(w):
                h, rs = w[0], rows_of(w)
                g = st.pop((w, "g"))
                gleft = gleft_ref[h, rs, :]
                gsum = st.pop((w, "gs")) + jnp.tile(gleft, (1, t // LANES))
                st[w, "dz"] = (g - jnp.exp(lsz_ref[slot, h, rs, :]) * (g + gsum)).astype(BF16)
                gleft_ref[h, rs, :] = gleft + _row_sum_lanes(g, tr)

            def s_products(w):
                h, rs = w[0], rows_of(w)
                dz = st.pop((w, "dz"))
                dqacc_ref[h, rs, :] += _dot(dz, kv)
                dkacc_ref[key_rows, :] += _dot_tn(dz, q_heads[h][rs])
                dvacc_ref[key_rows, :] += _dot_tn(st.pop((w, "a")), do_heads[h][rs])

            return items, [s_weights, s_prefix, s_dz, s_products]

        @pl.when(i == 0)
        def _():
            _emit_skewed(front(0, 0, True))

        @pl.when(i > 0)
        def _():
            _emit_skewed(front(0, 0, False))

        def step(jj, carry):
            _emit_skewed(front(jj, jj % 2, False), back(jj - 1, (jj - 1) % 2))
            return carry

        lax.fori_loop(1, i, step, 0)

        @pl.when(i > 0)
        def _():
            _emit_skewed(front(i, i % 2, True), back(i - 1, (i - 1) % 2))

        _emit_skewed(back(i, i % 2))
        dq_ref[...] = (Q_SCALE * jnp.where(low, dqacc_ref[0], dqacc_ref[1])).astype(BF16)

        @pl.when(i == nk - 1)
        def _():
            dk_ref[...] = dkacc_ref[...].astype(BF16)
            dv_ref[...] = dvacc_ref[...].astype(BF16)

    qblk = pl.BlockSpec((t, LANES), lambda p, i: (i, p))
    tblk = pl.BlockSpec((1, nk, LANES, t), lambda p, i: (p, 0, 0, 0))
    col_full = pl.BlockSpec((s_len, LANES), lambda p, i: (0, p))
    return pl.pallas_call(
        body, name="sb_bwd",
        grid=(N_HEADS // 2, nk),
        in_specs=[qblk, tblk, col_full, tblk, qblk, pl.BlockSpec((2, t, LANES), lambda p, i: (p, i, 0))],
        out_specs=[qblk, col_full, col_full],
        out_shape=[jax.ShapeDtypeStruct((s_len, N_HEADS * HEAD_DIM), BF16)] * 3,
        scratch_shapes=[pltpu.VMEM((2, t, LANES), F32), pltpu.VMEM((2, t, LANES), F32),
                        pltpu.VMEM((s_len, LANES), F32), pltpu.VMEM((s_len, LANES), F32)]
        + [pltpu.VMEM((2, 2, t, t), F32)] * 3,
        compiler_params=_cparams("parallel", "arbitrary"),
    )(q, kt, k, vt, do, cars)


def _pad_heads(a, heads):
    s_len = a.shape[0]
    a = a.reshape(s_len, heads, HEAD_DIM)
    return jnp.pad(a, ((0, 0), (0, 0), (0, LANES - HEAD_DIM))).reshape(s_len, heads * LANES)


def _unpad_heads(a, heads):
    s_len = a.shape[0]
    return a.reshape(s_len, heads, LANES)[:, :, :HEAD_DIM].reshape(s_len, heads * HEAD_DIM)


def _tile_transposed(a, groups):
    s_len = a.shape[0]
    return a.reshape(s_len // SB_TILE, SB_TILE, groups, LANES).transpose(2, 0, 3, 1)


def _local_step(xs, tgt, gains, sinks, rel_bias, wts):
    g1, gmix, g2, gfin = gains
    bkt = _rel_bucket_matrix()

    x1, h1, a1, b1, u1 = _ffn_fwd(xs, g1, wts["ffn1_w1t"], wts["ffn1_w3t"], wts["ffn1_w2"], "1")
    hm, qa, ka, va, qb, kb, vb, ga, gb = _proj_fwd(x1, gmix, wts["w_int"])
    qa_p, ka_p, va_p = _pad_heads(qa, N_HEADS), _pad_heads(ka, SWA_KV_HEADS), _pad_heads(va, SWA_KV_HEADS)
    oa_p = _swa_fwd(rel_bias, sinks, bkt, qa_p, ka_p, va_p)
    kbt = _tile_transposed(kb, N_HEADS // 2)
    ob, cars = _sb_fwd(qb, kbt, vb)
    oa = _unpad_heads(oa_p, N_HEADS)
    x2, mg = _merge_fwd(x1, oa, ob, ga, gb, wts["w_swa"], wts["w_sb"], wts["w_out"])
    x3, h3, a3, b3, u3 = _ffn_fwd(x2, g2, wts["ffn2_w1t"], wts["ffn2_w3t"], wts["ffn2_w2"], "2")
    dx3, loss, dgfin = _loss_fwd_bwd(x3, tgt, gfin)

    big = {}
    dx2, dg2, da3, db3, dx3b = _ffn_bwd(dx3, x2, g2, a3, b3, wts["ffn2_w1t"], wts["ffn2_w3t"], wts["ffn2_w2"], "2")
    big["ffn2_w1t"] = _matmul_tn(da3, h3, "ffn2_w1")
    big["ffn2_w3t"] = _matmul_tn(db3, h3, "ffn2_w3")
    big["ffn2_w2"] = _matmul_tn(u3, dx3b, "ffn2_w2")

    doa, dob, dga, dgb, dpa, dpb, dx2b = _merge_bwd(dx2, oa, ob, ga, gb, wts["w_swa"], wts["w_sb"], wts["w_out"])
    big["w_out"] = _matmul_tn(mg, dx2b, "w_out")
    big["w_swa"] = _matmul_tn(oa, dpa, "w_swa")
    big["w_sb"] = _matmul_tn(ob, dpb, "w_sb")

    dqa_p, dka_p, dva_p, dtab, dsink = _swa_bwd(rel_bias, sinks, bkt, qa_p, ka_p, va_p, _pad_heads(doa, N_HEADS))
    vbt = _tile_transposed(vb, N_HEADS // 2)
    dqb, dkb, dvb = _sb_bwd(qb, kbt, kb, vbt, dob, cars)
    dpieces = (_unpad_heads(dqa_p, N_HEADS), _unpad_heads(dka_p, SWA_KV_HEADS).astype(BF16),
               _unpad_heads(dva_p, SWA_KV_HEADS).astype(BF16), dqb, dkb, dvb, dga, dgb)
    big["w_int"] = jnp.concatenate([_matmul_tn(dp, hm, f"w_in{p}") for p, dp in enumerate(dpieces)], axis=0)
    dx1, dgmix = _proj_bwd(dpieces, dx2, x1, gmix, wts["w_int"])

    dx0, dg1, da1, db1, dx1b = _ffn_bwd(dx1, xs, g1, a1, b1, wts["ffn1_w1t"], wts["ffn1_w3t"], wts["ffn1_w2"], "1")
    big["ffn1_w1t"] = _matmul_tn(da1, h1, "ffn1_w1")
    big["ffn1_w3t"] = _matmul_tn(db1, h1, "ffn1_w3")
    big["ffn1_w2"] = _matmul_tn(u1, dx1b, "ffn1_w2")

    small = {"gains": (dg1, dgmix, dg2, dgfin), "sinks": dsink[:, 0], "rel_bias": dtab[:, :N_HEADS]}
    return loss, dx0, big, small


def _my_place():
    return lax.axis_index("x"), lax.axis_index("y"), lax.axis_index("c")


def _flip(v, bit):
    return 1 - v if bit else v


_RELATIONS = tuple((k >> 2 & 1, k >> 1 & 1, k & 1) for k in range(1, N_DEV))


def _gather_weights(wp):
    def body(x_ref, out_ref, send_sems, recv_sems, local_sem):
        x, y, c = _my_place()
        me, sibling = (x, y, c), (x, y, 1 - c)
        chips = [(1 - x, y), (x, 1 - y), (1 - x, 1 - y)]

        def rows(px, py, pc):
            return out_ref.at[4 * px + 2 * py + pc]

        def copy(k, block, to, src=None):
            return pltpu.make_async_remote_copy(
                src_ref=rows(*block) if src is None else src, dst_ref=rows(*block),
                send_sem=send_sems.at[k], recv_sem=recv_sems.at[k],
                device_id=to, device_id_type=pl.DeviceIdType.MESH)

        mine = pltpu.make_async_copy(x_ref, rows(*me), local_sem)
        mine.start()
        first = [copy(0, me, sibling, src=x_ref)]
        first += [copy(1 + j, me, (*chip, c), src=x_ref) for j, chip in enumerate(chips)]
        for cp in first:
            cp.start()
        passed = [copy(4 + j, (*chip, c), sibling) for j, chip in enumerate(chips)]
        for j, chip in enumerate(chips):
            copy(1 + j, (*chip, c), me).wait_recv()
            passed[j].start()
        copy(0, sibling, me).wait_recv()
        for j, chip in enumerate(chips):
            copy(4 + j, (*chip, 1 - c), me).wait_recv()
        for cp in first + passed:
            cp.wait_send()
        mine.wait()

    return pl.pallas_call(
        body, name="gather_weights",
        out_shape=jax.ShapeDtypeStruct((N_DEV,) + wp.shape, wp.dtype),
        in_specs=[pl.BlockSpec(memory_space=pl.ANY)],
        out_specs=pl.BlockSpec(memory_space=pl.ANY),
        scratch_shapes=[pltpu.SemaphoreType.DMA((7,)), pltpu.SemaphoreType.DMA((7,)), pltpu.SemaphoreType.DMA(())],
    )(wp)


def _exchange_grads(gp):
    def body(g_ref, out_ref, send_sems, recv_sems, local_sem):
        x, y, c = _my_place()
        me = 4 * x + 2 * y + c
        mine = pltpu.make_async_copy(g_ref.at[me], out_ref.at[me], local_sem)
        mine.start()
        copies = []
        for k, (fx, fy, fc) in enumerate(_RELATIONS):
            px, py, pc = _flip(x, fx), _flip(y, fy), _flip(c, fc)
            peer = 4 * px + 2 * py + pc
            copies.append((
                pltpu.make_async_remote_copy(
                    src_ref=g_ref.at[peer], dst_ref=out_ref.at[me], send_sem=send_sems.at[k], recv_sem=recv_sems.at[k],
                    device_id=(px, py, pc), device_id_type=pl.DeviceIdType.MESH),
                pltpu.make_async_remote_copy(
                    src_ref=g_ref.at[peer], dst_ref=out_ref.at[peer], send_sem=send_sems.at[k], recv_sem=recv_sems.at[k],
                    device_id=(px, py, pc), device_id_type=pl.DeviceIdType.MESH)))
        for out_cp, _ in copies:
            out_cp.start()
        for _, in_cp in copies:
            in_cp.wait_recv()
        for out_cp, _ in copies:
            out_cp.wait_send()
        mine.wait()

    return pl.pallas_call(
        body, name="exchange_grads",
        out_shape=jax.ShapeDtypeStruct(gp.shape, gp.dtype),
        in_specs=[pl.BlockSpec(memory_space=pl.ANY)],
        out_specs=pl.BlockSpec(memory_space=pl.ANY),
        scratch_shapes=[pltpu.SemaphoreType.DMA((7,)), pltpu.SemaphoreType.DMA((7,)), pltpu.SemaphoreType.DMA(())],
    )(gp)


def _adamw(w, g, m, v):
    m = ADAM_B1 * m + (1.0 - ADAM_B1) * g
    v = ADAM_B2 * v + (1.0 - ADAM_B2) * jnp.square(g)
    m_hat = m / (1.0 - ADAM_B1 ** ADAM_STEP)
    v_hat = v / (1.0 - ADAM_B2 ** ADAM_STEP)
    delta = -ADAM_LR * (m_hat / (jnp.sqrt(v_hat) + ADAM_EPS) + ADAM_WD * w)
    return delta, m, v


def _sum_and_adamw(parts, w, m, v):
    rows = w.shape[0]
    tr = 112
    assert rows % tr == 0

    def body(p_ref, w_ref, m_ref, v_ref, g_out, d_out, m_out, v_out):
        g = p_ref[0]
        for d in range(1, N_DEV):
            g = g + p_ref[d]
        delta, mn, vn = _adamw(w_ref[...], g, m_ref[...], v_ref[...])
        g_out[...] = g
        d_out[...] = delta
        m_out[...] = mn
        v_out[...] = vn

    sp = pl.BlockSpec((tr, D_MODEL), lambda i: (i, 0))
    return pl.pallas_call(
        body, name="sum_and_adamw",
        grid=(rows // tr,),
        in_specs=[pl.BlockSpec((N_DEV, tr, D_MODEL), lambda i: (0, i, 0)), sp, sp, sp],
        out_specs=[sp] * 4,
        out_shape=[jax.ShapeDtypeStruct(w.shape, F32)] * 4,
        compiler_params=_cparams("parallel"),
    )(parts, w, m, v)


def _small_allreduce_adamw(part, w, m, v):
    def body(p_ref, w_ref, m_ref, v_ref, g_out, d_out, m_out, v_out, buf, send_sems, recv_sems):
        x, y, c = _my_place()
        me = 4 * x + 2 * y + c
        buf[me] = p_ref[...]
        copies = []
        for k, (fx, fy, fc) in enumerate(_RELATIONS):
            px, py, pc = _flip(x, fx), _flip(y, fy), _flip(c, fc)
            peer = 4 * px + 2 * py + pc
            copies.append((
                pltpu.make_async_remote_copy(
                    src_ref=buf.at[me], dst_ref=buf.at[me], send_sem=send_sems.at[k], recv_sem=recv_sems.at[k],
                    device_id=(px, py, pc), device_id_type=pl.DeviceIdType.MESH),
                pltpu.make_async_remote_copy(
                    src_ref=buf.at[me], dst_ref=buf.at[peer], send_sem=send_sems.at[k], recv_sem=recv_sems.at[k],
                    device_id=(px, py, pc), device_id_type=pl.DeviceIdType.MESH)))
        for out_cp, _ in copies:
            out_cp.start()
        for _, in_cp in copies:
            in_cp.wait_recv()
        for out_cp, _ in copies:
            out_cp.wait_send()
        g = buf[0]
        for d in range(1, N_DEV):
            g = g + buf[d]
        delta, mn, vn = _adamw(w_ref[...], g, m_ref[...], v_ref[...])
        g_out[...] = g
        d_out[...] = delta
        m_out[...] = mn
        v_out[...] = vn

    vm = pl.BlockSpec(memory_space=pltpu.VMEM)
    return pl.pallas_call(
        body, name="small_allreduce_adamw",
        in_specs=[vm] * 4, out_specs=[vm] * 4,
        out_shape=[jax.ShapeDtypeStruct(w.shape, F32)] * 4,
        scratch_shapes=[pltpu.VMEM((N_DEV,) + part.shape, F32),
                        pltpu.SemaphoreType.DMA((7,)), pltpu.SemaphoreType.DMA((7,))],
    )(part, w, m, v)


_TRANSPOSED = ("ffn1_w1", "ffn1_w3", "w_in", "ffn2_w1", "ffn2_w3")
_BRANCH = ("w_branch_swa", "w_branch_sb")


def _pack_shards(t):
    parts = []
    for name in BIG_NAMES:
        a = t[name][0]
        if name in _TRANSPOSED:
            a = a.T
        elif name in _BRANCH:
            a = a.reshape(64, D_MODEL)
        parts.append(a)
    return jnp.concatenate(parts, axis=0)


def _unpack_shards(p):
    out = {}
    for name, lo, hi in zip(BIG_NAMES, BIG_OFFS[:-1], BIG_OFFS[1:]):
        a = p[lo:hi]
        if name in _TRANSPOSED:
            a = a.T
        elif name in _BRANCH:
            a = a.reshape(512, 128)
        out[name] = a[None]
    return out


def _full_weights(wg):
    def part(name):
        k = BIG_NAMES.index(name)
        return wg[:, BIG_OFFS[k]:BIG_OFFS[k + 1]]

    def branch(name):
        return part(name).reshape(N_DEV, 512, 128).transpose(1, 0, 2).reshape(512, D_MODEL)

    return {
        "ffn1_w1t": part("ffn1_w1").reshape(D_FF, D_MODEL), "ffn1_w3t": part("ffn1_w3").reshape(D_FF, D_MODEL),
        "ffn1_w2": part("ffn1_w2").reshape(D_FF, D_MODEL), "w_int": part("w_in").reshape(IN_W, D_MODEL),
        "w_swa": branch("w_branch_swa"), "w_sb": branch("w_branch_sb"),
        "w_out": part("w_out").reshape(D_MODEL, D_MODEL),
        "ffn2_w1t": part("ffn2_w1").reshape(D_FF, D_MODEL), "ffn2_w3t": part("ffn2_w3").reshape(D_FF, D_MODEL),
        "ffn2_w2": part("ffn2_w2").reshape(D_FF, D_MODEL),
    }


def _pack_full_grads(big):
    def branch(a):
        return a.reshape(512, N_DEV, 128).transpose(1, 0, 2).reshape(N_DEV, 64, D_MODEL)

    parts = [big["ffn1_w1t"].reshape(N_DEV, 352, D_MODEL), big["ffn1_w3t"].reshape(N_DEV, 352, D_MODEL),
             big["ffn1_w2"].reshape(N_DEV, 352, D_MODEL), big["w_int"].reshape(N_DEV, 544, D_MODEL),
             branch(big["w_swa"]), branch(big["w_sb"]), big["w_out"].reshape(N_DEV, 128, D_MODEL),
             big["ffn2_w1t"].reshape(N_DEV, 352, D_MODEL), big["ffn2_w3t"].reshape(N_DEV, 352, D_MODEL),
             big["ffn2_w2"].reshape(N_DEV, 352, D_MODEL)]
    return jnp.concatenate(parts, axis=1)


_SMALL_NAMES = ("norm_ffn1", "norm_mix", "norm_ffn2", "norm_final", "swa_sinks", "rel_bias")


def _pack_small(vals):
    rows = []
    for a in vals:
        a = a.reshape(-1)
        rows.append(jnp.pad(a, (0, D_MODEL - a.shape[0])))
    rows += [jnp.zeros((D_MODEL,), F32)] * (SMALL_ROWS - len(rows))
    return jnp.stack(rows)


def _unpack_small(p):
    return {"norm_ffn1": p[0:1], "norm_mix": p[1:2], "norm_ffn2": p[2:3], "norm_final": p[3],
            "swa_sinks": p[4:5, :N_HEADS], "rel_bias": p[5, :REL_BUCKETS * N_HEADS].reshape(REL_BUCKETS, N_HEADS)}


ALL_NAMES = ("norm_ffn1", "ffn1_w1", "ffn1_w3", "ffn1_w2", "norm_mix", "w_in", "swa_sinks", "rel_bias",
             "w_branch_swa", "w_branch_sb", "w_out", "norm_ffn2", "ffn2_w1", "ffn2_w3", "ffn2_w2", "norm_final")


def kernel(x, norm_ffn1, ffn1_w1, ffn1_w3, ffn1_w2, norm_mix, w_in, swa_sinks, rel_bias, w_branch_swa, w_branch_sb, w_out, norm_ffn2, ffn2_w1, ffn2_w3, ffn2_w2, norm_final, loss_target, m_norm_ffn1, m_ffn1_w1, m_ffn1_w3, m_ffn1_w2, m_norm_mix, m_w_in, m_swa_sinks, m_rel_bias, m_w_branch_swa, m_w_branch_sb, m_w_out, m_norm_ffn2, m_ffn2_w1, m_ffn2_w3, m_ffn2_w2, m_norm_final, v_norm_ffn1, v_ffn1_w1, v_ffn1_w3, v_ffn1_w2, v_norm_mix, v_w_in, v_swa_sinks, v_rel_bias, v_w_branch_swa, v_w_branch_sb, v_w_out, v_norm_ffn2, v_ffn2_w1, v_ffn2_w3, v_ffn2_w2, v_norm_final):
    w = dict(zip(ALL_NAMES, (norm_ffn1, ffn1_w1, ffn1_w3, ffn1_w2, norm_mix, w_in, swa_sinks, rel_bias,
                             w_branch_swa, w_branch_sb, w_out, norm_ffn2, ffn2_w1, ffn2_w3, ffn2_w2, norm_final)))
    m = dict(zip(ALL_NAMES, (m_norm_ffn1, m_ffn1_w1, m_ffn1_w3, m_ffn1_w2, m_norm_mix, m_w_in, m_swa_sinks, m_rel_bias,
                             m_w_branch_swa, m_w_branch_sb, m_w_out, m_norm_ffn2, m_ffn2_w1, m_ffn2_w3, m_ffn2_w2,
                             m_norm_final)))
    v = dict(zip(ALL_NAMES, (v_norm_ffn1, v_ffn1_w1, v_ffn1_w3, v_ffn1_w2, v_norm_mix, v_w_in, v_swa_sinks, v_rel_bias,
                             v_w_branch_swa, v_w_branch_sb, v_w_out, v_norm_ffn2, v_ffn2_w1, v_ffn2_w3, v_ffn2_w2,
                             v_norm_final)))

    w_packed = _pack_shards(w)
    wts = _full_weights(_gather_weights(w_packed.astype(BF16)))
    gains = (norm_ffn1, norm_mix, norm_ffn2, norm_final.reshape(1, D_MODEL))
    loss, dx, big, small = _local_step(x[0], loss_target[0], gains, swa_sinks, rel_bias, wts)

    parts = _exchange_grads(_pack_full_grads(big))
    g_big, d_big, m_big, v_big = (_unpack_shards(p) for p in
                                  _sum_and_adamw(parts, w_packed, _pack_shards(m), _pack_shards(v)))

    small_part = _pack_small(small["gains"] + (small["sinks"], small["rel_bias"]))
    g_sm, d_sm, m_sm, v_sm = (_unpack_small(p) for p in _small_allreduce_adamw(
        small_part, _pack_small([w[n] for n in _SMALL_NAMES]), _pack_small([m[n] for n in _SMALL_NAMES]),
        _pack_small([v[n] for n in _SMALL_NAMES])))

    total_loss = lax.psum(loss[0, 0], AXES)
    outs = [total_loss, dx[None]]
    for big_d, small_d in ((g_big, g_sm), (d_big, d_sm), (m_big, m_sm), (v_big, v_sm)):
        merged = {**big_d, **small_d}
        outs += [merged[n] for n in ALL_NAMES]
    return tuple(outs)
```

```python
import jax
import jax.numpy as jnp
import numpy as np
from jax import lax
from jax.experimental import pallas as pl
from jax.experimental.pallas import tpu as pltpu

F32 = jnp.float32
BF16 = jnp.bfloat16

D_MODEL = 1024
D_FF = 2816
HEAD_DIM = 64
N_HEADS = 8
SWA_KV_HEADS = 2
SWA_GROUP = 4
SWA_BLOCK = 128
REL_BUCKETS = 32
REL_MAX_DIST = 128
RMS_EPS = 1e-6
NEG_BIG = -1e30
Q_SCALE = HEAD_DIM ** -0.5
LANES = 128

N_DEV = 8
AXES = ("x", "y", "c")

ADAM_LR = 0.001
ADAM_B1 = 0.9
ADAM_B2 = 0.999
ADAM_EPS = 1e-08
ADAM_WD = 0.01
ADAM_STEP = 10

IN_SIZES = (512, 128, 128, 512, 512, 512, 1024, 1024)
IN_OFFS = tuple(int(v) for v in np.cumsum((0,) + IN_SIZES))
IN_W = IN_OFFS[-1]

BIG_NAMES = ("ffn1_w1", "ffn1_w3", "ffn1_w2", "w_in", "w_branch_swa", "w_branch_sb", "w_out",
             "ffn2_w1", "ffn2_w3", "ffn2_w2")
BIG_ROWS = (352, 352, 352, 544, 64, 64, 128, 352, 352, 352)
BIG_OFFS = tuple(int(v) for v in np.cumsum((0,) + BIG_ROWS))
PACK_ROWS = BIG_OFFS[-1]
SMALL_ROWS = 8

VMEM_LIMIT = 56 * 1024 * 1024
SB_QUERIES = 512
SB_KEYS = 256
SB_ROWS = 128


def _dot(a, b):
    return jnp.dot(a, b, preferred_element_type=F32)


def _dot_nt(a, b):
    return lax.dot_general(a, b, (((1,), (1,)), ((), ())), preferred_element_type=F32)


def _dot_tn(a, b):
    return lax.dot_general(a, b, (((0,), (0,)), ((), ())), preferred_element_type=F32)


def _cparams(*sem):
    return pltpu.CompilerParams(dimension_semantics=sem, vmem_limit_bytes=VMEM_LIMIT)


def _rms_rstd(xv):
    return lax.rsqrt(jnp.mean(xv * xv, axis=-1, keepdims=True) + RMS_EPS)


def _rms_bwd(dh, xv, r, g):
    xhat = xv * r
    dg = jnp.sum(dh * xhat, axis=0, keepdims=True)
    dxn = dh * g
    dx = r * (dxn - xhat * jnp.mean(dxn * xhat, axis=-1, keepdims=True))
    return dx, dg


def _ffn_fwd(x, g, w1t, w3t, w2, tag):
    s_len = x.shape[0]
    tm, tf = min(1024, s_len), 256
    nf = D_FF // tf

    def body(x_ref, g_ref, w1_ref, w3_ref, w2_ref, xo_ref, h_ref, a_ref, b_ref, u_ref, acc_ref, hs_ref):
        j = pl.program_id(1)

        @pl.when(j == 0)
        def _():
            xv = x_ref[...]
            h = (xv * _rms_rstd(xv) * g_ref[...]).astype(BF16)
            hs_ref[...] = h
            h_ref[...] = h
            acc_ref[...] = jnp.zeros_like(acc_ref)

        h = hs_ref[...]
        a = _dot_nt(h, w1_ref[...])
        b = _dot_nt(h, w3_ref[...])
        a_ref[...] = a.astype(BF16)
        b_ref[...] = b.astype(BF16)
        uh = (0.5 * (a * jax.nn.sigmoid(a) * b)).astype(BF16)
        u_ref[...] = uh
        acc_ref[...] += _dot(uh, w2_ref[...])

        @pl.when(j == nf - 1)
        def _():
            xo_ref[...] = x_ref[...] + acc_ref[...]

    row = lambda i, j: (i, 0)
    return pl.pallas_call(
        body, name=f"ffn_fwd_{tag}",
        grid=(s_len // tm, nf),
        in_specs=[pl.BlockSpec((tm, D_MODEL), row), pl.BlockSpec((1, D_MODEL), lambda i, j: (0, 0)),
                  pl.BlockSpec((tf, D_MODEL), lambda i, j: (j, 0)), pl.BlockSpec((tf, D_MODEL), lambda i, j: (j, 0)),
                  pl.BlockSpec((tf, D_MODEL), lambda i, j: (j, 0))],
        out_specs=[pl.BlockSpec((tm, D_MODEL), row), pl.BlockSpec((tm, D_MODEL), row),
                   pl.BlockSpec((tm, tf), lambda i, j: (i, j)), pl.BlockSpec((tm, tf), lambda i, j: (i, j)),
                   pl.BlockSpec((tm, tf), lambda i, j: (i, j))],
        out_shape=[jax.ShapeDtypeStruct((s_len, D_MODEL), F32), jax.ShapeDtypeStruct((s_len, D_MODEL), BF16),
                   jax.ShapeDtypeStruct((s_len, D_FF), BF16), jax.ShapeDtypeStruct((s_len, D_FF), BF16),
                   jax.ShapeDtypeStruct((s_len, D_FF), BF16)],
        scratch_shapes=[pltpu.VMEM((tm, D_MODEL), F32), pltpu.VMEM((tm, D_MODEL), BF16)],
        compiler_params=_cparams("parallel", "arbitrary"),
    )(x, g, w1t, w3t, w2)


def _ffn_bwd(dy, x, g, a, b, w1t, w3t, w2, tag):
    s_len = x.shape[0]
    tm, tf = min(1024, s_len), 256
    nf = D_FF // tf

    def body(dy_ref, x_ref, g_ref, a_ref, b_ref, w1_ref, w3_ref, w2_ref,
             dx_ref, dg_ref, da_ref, db_ref, dyb_ref, acc_ref, dys_ref):
        i, j = pl.program_id(0), pl.program_id(1)

        @pl.when(j == 0)
        def _():
            dyb = dy_ref[...].astype(BF16)
            dys_ref[...] = dyb
            dyb_ref[...] = dyb
            acc_ref[...] = jnp.zeros_like(acc_ref)

        @pl.when((i == 0) & (j == 0))
        def _():
            dg_ref[...] = jnp.zeros_like(dg_ref)

        du = 0.5 * _dot_nt(dys_ref[...], w2_ref[...])
        av = a_ref[...].astype(F32)
        bv = b_ref[...].astype(F32)
        sg = jax.nn.sigmoid(av)
        sil = av * sg
        da = (du * bv * (sg + sil * (1.0 - sg))).astype(BF16)
        db = (du * sil).astype(BF16)
        da_ref[...] = da
        db_ref[...] = db
        acc_ref[...] += _dot(da, w1_ref[...]) + _dot(db, w3_ref[...])

        @pl.when(j == nf - 1)
        def _():
            xv = x_ref[...]
            dx, dg = _rms_bwd(acc_ref[...], xv, _rms_rstd(xv), g_ref[...])
            dx_ref[...] = dy_ref[...] + dx
            dg_ref[...] += dg

    row = lambda i, j: (i, 0)
    blk = lambda i, j: (i, j)
    wsp = pl.BlockSpec((tf, D_MODEL), lambda i, j: (j, 0))
    return pl.pallas_call(
        body, name=f"ffn_bwd_{tag}",
        grid=(s_len // tm, nf),
        in_specs=[pl.BlockSpec((tm, D_MODEL), row), pl.BlockSpec((tm, D_MODEL), row),
                  pl.BlockSpec((1, D_MODEL), lambda i, j: (0, 0)),
                  pl.BlockSpec((tm, tf), blk), pl.BlockSpec((tm, tf), blk), wsp, wsp, wsp],
        out_specs=[pl.BlockSpec((tm, D_MODEL), row), pl.BlockSpec((1, D_MODEL), lambda i, j: (0, 0)),
                   pl.BlockSpec((tm, tf), blk), pl.BlockSpec((tm, tf), blk), pl.BlockSpec((tm, D_MODEL), row)],
        out_shape=[jax.ShapeDtypeStruct((s_len, D_MODEL), F32), jax.ShapeDtypeStruct((1, D_MODEL), F32),
                   jax.ShapeDtypeStruct((s_len, D_FF), BF16), jax.ShapeDtypeStruct((s_len, D_FF), BF16),
                   jax.ShapeDtypeStruct((s_len, D_MODEL), BF16)],
        scratch_shapes=[pltpu.VMEM((tm, D_MODEL), F32), pltpu.VMEM((tm, D_MODEL), BF16)],
        compiler_params=_cparams("arbitrary", "arbitrary"),
    )(dy, x, g, a, b, w1t, w3t, w2)


def _matmul_tn(lhs, rhs, tag):
    s_len, m = lhs.shape
    n = rhs.shape[1]
    tm = min(512, s_len)
    tj = m if m <= 1024 else 1408
    assert m % tj == 0

    def body(l_ref, r_ref, o_ref):
        @pl.when(pl.program_id(1) == 0)
        def _():
            o_ref[...] = jnp.zeros_like(o_ref)

        o_ref[...] += _dot_tn(l_ref[...], r_ref[...])

    return pl.pallas_call(
        body, name=f"matmul_tn_{tag}",
        grid=(m // tj, s_len // tm),
        in_specs=[pl.BlockSpec((tm, tj), lambda j, i: (i, j)), pl.BlockSpec((tm, n), lambda j, i: (i, 0))],
        out_specs=pl.BlockSpec((tj, n), lambda j, i: (j, 0)),
        out_shape=jax.ShapeDtypeStruct((m, n), F32),
        compiler_params=_cparams("parallel", "arbitrary"),
    )(lhs, rhs)


def _proj_fwd(x1, g, wint):
    s_len = x1.shape[0]
    tm = min(512, s_len)
    dts = (BF16, BF16, BF16, BF16, BF16, BF16, F32, F32)

    def body(x_ref, g_ref, w_ref, h_ref, *outs):
        xv = x_ref[...]
        h = (xv * _rms_rstd(xv) * g_ref[...]).astype(BF16)
        h_ref[...] = h
        for p, o_ref in enumerate(outs):
            val = _dot_nt(h, w_ref[IN_OFFS[p]:IN_OFFS[p + 1], :])
            if p == 3:
                val = val * Q_SCALE
            o_ref[...] = val.astype(dts[p])

    row = lambda i: (i, 0)
    return pl.pallas_call(
        body, name="proj_fwd",
        grid=(s_len // tm,),
        in_specs=[pl.BlockSpec((tm, D_MODEL), row), pl.BlockSpec((1, D_MODEL), lambda i: (0, 0)),
                  pl.BlockSpec((IN_W, D_MODEL), lambda i: (0, 0))],
        out_specs=[pl.BlockSpec((tm, D_MODEL), row)] + [pl.BlockSpec((tm, w), row) for w in IN_SIZES],
        out_shape=[jax.ShapeDtypeStruct((s_len, D_MODEL), BF16)]
        + [jax.ShapeDtypeStruct((s_len, w), dt) for w, dt in zip(IN_SIZES, dts)],
        compiler_params=_cparams("parallel"),
    )(x1, g, wint)


def _proj_bwd(dpieces, dx2, x1, g, wint):
    s_len = x1.shape[0]
    tm = min(512, s_len)

    def body(*refs):
        dps = refs[:8]
        dx2_ref, x_ref, g_ref, w_ref, dx_ref, dg_ref = refs[8:]

        @pl.when(pl.program_id(0) == 0)
        def _():
            dg_ref[...] = jnp.zeros_like(dg_ref)

        dh = _dot(dps[0][...], w_ref[IN_OFFS[0]:IN_OFFS[1], :])
        for p in range(1, 8):
            dh += _dot(dps[p][...], w_ref[IN_OFFS[p]:IN_OFFS[p + 1], :])
        xv = x_ref[...]
        dx, dg = _rms_bwd(dh, xv, _rms_rstd(xv), g_ref[...])
        dx_ref[...] = dx2_ref[...] + dx
        dg_ref[...] += dg

    row = lambda i: (i, 0)
    return pl.pallas_call(
        body, name="proj_bwd",
        grid=(s_len // tm,),
        in_specs=[pl.BlockSpec((tm, w), row) for w in IN_SIZES]
        + [pl.BlockSpec((tm, D_MODEL), row), pl.BlockSpec((tm, D_MODEL), row),
           pl.BlockSpec((1, D_MODEL), lambda i: (0, 0)), pl.BlockSpec((IN_W, D_MODEL), lambda i: (0, 0))],
        out_specs=[pl.BlockSpec((tm, D_MODEL), row), pl.BlockSpec((1, D_MODEL), lambda i: (0, 0))],
        out_shape=[jax.ShapeDtypeStruct((s_len, D_MODEL), F32), jax.ShapeDtypeStruct((1, D_MODEL), F32)],
        compiler_params=_cparams("arbitrary"),
    )(*dpieces, dx2, x1, g, wint)


def _merge_fwd(x1, oa, ob, ga, gb, wswa, wsb, wout):
    s_len = x1.shape[0]
    tm = min(512, s_len)

    def body(x_ref, oa_ref, ob_ref, ga_ref, gb_ref, wa_ref, wb_ref, wo_ref, xo_ref, mg_ref):
        pa = _dot(oa_ref[...], wa_ref[...])
        pb = _dot(ob_ref[...], wb_ref[...])
        mg = (jax.nn.sigmoid(ga_ref[...]) * pa + jax.nn.sigmoid(gb_ref[...]) * pb).astype(BF16)
        mg_ref[...] = mg
        xo_ref[...] = x_ref[...] + _dot(mg, wo_ref[...])

    row = lambda i: (i, 0)
    full = lambda i: (0, 0)
    return pl.pallas_call(
        body, name="merge_fwd",
        grid=(s_len // tm,),
        in_specs=[pl.BlockSpec((tm, D_MODEL), row), pl.BlockSpec((tm, 512), row), pl.BlockSpec((tm, 512), row),
                  pl.BlockSpec((tm, D_MODEL), row), pl.BlockSpec((tm, D_MODEL), row),
                  pl.BlockSpec((512, D_MODEL), full), pl.BlockSpec((512, D_MODEL), full),
                  pl.BlockSpec((D_MODEL, D_MODEL), full)],
        out_specs=[pl.BlockSpec((tm, D_MODEL), row), pl.BlockSpec((tm, D_MODEL), row)],
        out_shape=[jax.ShapeDtypeStruct((s_len, D_MODEL), F32), jax.ShapeDtypeStruct((s_len, D_MODEL), BF16)],
        compiler_params=_cparams("parallel"),
    )(x1, oa, ob, ga, gb, wswa, wsb, wout)


def _merge_bwd(dx2, oa, ob, ga, gb, wswa, wsb, wout):
    s_len = dx2.shape[0]
    tm = min(512, s_len)

    def body(dx_ref, oa_ref, ob_ref, ga_ref, gb_ref, wa_ref, wb_ref, wo_ref,
             doa_ref, dob_ref, dga_ref, dgb_ref, dpa_ref, dpb_ref, dxb_ref):
        dxb = dx_ref[...].astype(BF16)
        dxb_ref[...] = dxb
        dmg = _dot_nt(dxb, wo_ref[...])
        for o_ref, g_ref, w_ref, do_ref, dg_ref, dp_ref in (
                (oa_ref, ga_ref, wa_ref, doa_ref, dga_ref, dpa_ref),
                (ob_ref, gb_ref, wb_ref, dob_ref, dgb_ref, dpb_ref)):
            pv = _dot(o_ref[...], w_ref[...])
            sg = jax.nn.sigmoid(g_ref[...])
            dp = (dmg * sg).astype(BF16)
            dp_ref[...] = dp
            dg_ref[...] = (dmg * pv * sg * (1.0 - sg)).astype(BF16)
            do_ref[...] = _dot_nt(dp, w_ref[...]).astype(BF16)

    row = lambda i: (i, 0)
    full = lambda i: (0, 0)
    wide = pl.BlockSpec((tm, D_MODEL), row)
    half = pl.BlockSpec((tm, 512), row)
    return pl.pallas_call(
        body, name="merge_bwd",
        grid=(s_len // tm,),
        in_specs=[wide, half, half, wide, wide, pl.BlockSpec((512, D_MODEL), full),
                  pl.BlockSpec((512, D_MODEL), full), pl.BlockSpec((D_MODEL, D_MODEL), full)],
        out_specs=[half, half, wide, wide, wide, wide, wide],
        out_shape=[jax.ShapeDtypeStruct((s_len, 512), BF16)] * 2 + [jax.ShapeDtypeStruct((s_len, D_MODEL), BF16)] * 5,
        compiler_params=_cparams("parallel"),
    )(dx2, oa, ob, ga, gb, wswa, wsb, wout)


def _loss_fwd_bwd(x3, tgt, g):
    s_len = x3.shape[0]
    tm = min(1024, s_len)

    def body(x_ref, t_ref, g_ref, dx_ref, loss_ref, dg_ref):
        @pl.when(pl.program_id(0) == 0)
        def _():
            loss_ref[...] = jnp.zeros_like(loss_ref)
            dg_ref[...] = jnp.zeros_like(dg_ref)

        xv = x_ref[...]
        gv = g_ref[...]
        r = _rms_rstd(xv)
        err = xv * r * gv - t_ref[...]
        loss_ref[...] += 0.5 * jnp.sum(jnp.mean(err * err, axis=-1, keepdims=True), axis=0, keepdims=True)
        dx, dg = _rms_bwd(err * (1.0 / D_MODEL), xv, r, gv)
        dx_ref[...] = dx
        dg_ref[...] += dg

    row = lambda i: (i, 0)
    return pl.pallas_call(
        body, name="loss_fwd_bwd",
        grid=(s_len // tm,),
        in_specs=[pl.BlockSpec((tm, D_MODEL), row), pl.BlockSpec((tm, D_MODEL), row),
                  pl.BlockSpec((1, D_MODEL), lambda i: (0, 0))],
        out_specs=[pl.BlockSpec((tm, D_MODEL), row), pl.BlockSpec((1, 1), lambda i: (0, 0)),
                   pl.BlockSpec((1, D_MODEL), lambda i: (0, 0))],
        out_shape=[jax.ShapeDtypeStruct((s_len, D_MODEL), F32), jax.ShapeDtypeStruct((1, 1), F32),
                   jax.ShapeDtypeStruct((1, D_MODEL), F32)],
        compiler_params=_cparams("arbitrary"),
    )(x3, tgt, g)


def _rel_bucket_matrix():
    qi = jnp.arange(SWA_BLOCK)[:, None] + SWA_BLOCK
    kj = jnp.arange(2 * SWA_BLOCK)[None, :]
    dist = jnp.maximum(qi - kj, 0)
    max_exact = REL_BUCKETS // 2
    d = jnp.maximum(dist, 1).astype(F32)
    large = max_exact + (jnp.log(d / max_exact) / np.log(REL_MAX_DIST / max_exact)
                         * (REL_BUCKETS - max_exact)).astype(jnp.int32)
    large = jnp.minimum(large, REL_BUCKETS - 1)
    return jnp.where(dist < max_exact, dist, large).astype(jnp.int32)


def _swa_bias_into(bias_ref, bkt_ref, tab_ref):
    bk = bkt_ref[...]
    for h in range(N_HEADS):
        acc = jnp.zeros(bk.shape, F32)
        for bucket in range(REL_BUCKETS):
            acc = jnp.where(bk == bucket, tab_ref[bucket, h], acc)
        bias_ref[h] = acc


def _swa_valid(n):
    shape = (SWA_BLOCK, 2 * SWA_BLOCK)
    row = lax.broadcasted_iota(jnp.int32, shape, 0)
    col = lax.broadcasted_iota(jnp.int32, shape, 1)
    dist = row + SWA_BLOCK - col
    return (dist >= 0) & (dist < SWA_BLOCK) & ((col >= SWA_BLOCK) | (n > 0))


def _swa_probs(q, k, bias, sink, valid):
    lg = jnp.where(valid, _dot_nt(q, k) * Q_SCALE + bias, NEG_BIG)
    m = jnp.maximum(jnp.max(lg, axis=-1, keepdims=True), sink)
    e = jnp.exp(lg - m)
    es = jnp.exp(sink - m)
    den = jnp.sum(e, axis=-1, keepdims=True) + es
    return e / den, es / den


def _swa_specs(s_len):
    blk = SWA_BLOCK
    cur = lambda n: (n, 0)
    prev = lambda n: (jnp.maximum(n - 1, 0), 0)
    kvw = SWA_KV_HEADS * LANES
    return [pl.BlockSpec(memory_space=pltpu.SMEM), pl.BlockSpec(memory_space=pltpu.SMEM),
            pl.BlockSpec((blk, 2 * blk), lambda n: (0, 0)),
            pl.BlockSpec((blk, N_HEADS * LANES), cur),
            pl.BlockSpec((blk, kvw), prev), pl.BlockSpec((blk, kvw), cur),
            pl.BlockSpec((blk, kvw), prev), pl.BlockSpec((blk, kvw), cur)]


def _swa_fwd(tab, sinks, bkt, q, k, v):
    s_len = q.shape[0]
    blk = SWA_BLOCK

    def body(tab_ref, sink_ref, bkt_ref, q_ref, kp_ref, kc_ref, vp_ref, vc_ref, o_ref, bias_ref):
        n = pl.program_id(0)

        @pl.when(n == 0)
        def _():
            _swa_bias_into(bias_ref, bkt_ref, tab_ref)

        valid = _swa_valid(n)
        for grp in range(SWA_KV_HEADS):
            gl = slice(grp * LANES, (grp + 1) * LANES)
            kk = jnp.concatenate([kp_ref[:, gl], kc_ref[:, gl]], axis=0)
            vv = jnp.concatenate([vp_ref[:, gl], vc_ref[:, gl]], axis=0)
            for hh in range(SWA_GROUP):
                h = grp * SWA_GROUP + hh
                hl = slice(h * LANES, (h + 1) * LANES)
                p, _ = _swa_probs(q_ref[:, hl], kk, bias_ref[h], sink_ref[0, h], valid)
                o_ref[:, hl] = _dot(p.astype(BF16), vv).astype(BF16)

    return pl.pallas_call(
        body, name="swa_fwd",
        grid=(s_len // blk,),
        in_specs=_swa_specs(s_len),
        out_specs=pl.BlockSpec((blk, N_HEADS * LANES), lambda n: (n, 0)),
        out_shape=jax.ShapeDtypeStruct((s_len, N_HEADS * LANES), BF16),
        scratch_shapes=[pltpu.VMEM((N_HEADS, blk, 2 * blk), F32)],
        compiler_params=_cparams("arbitrary"),
    )(tab, sinks, bkt, q, k, k, v, v)


def _swa_bwd(tab, sinks, bkt, q, k, v, do):
    s_len = q.shape[0]
    blk = SWA_BLOCK
    nb = s_len // blk
    kvw = SWA_KV_HEADS * LANES

    def body(tab_ref, sink_ref, bkt_ref, q_ref, kp_ref, kc_ref, vp_ref, vc_ref, do_ref,
             dq_ref, dk_ref, dv_ref, dtab_ref, dsink_ref, bias_ref, dbias_ref):
        n = pl.program_id(0)

        @pl.when(n == 0)
        def _():
            _swa_bias_into(bias_ref, bkt_ref, tab_ref)
            dbias_ref[...] = jnp.zeros_like(dbias_ref)
            dk_ref[...] = jnp.zeros_like(dk_ref)
            dv_ref[...] = jnp.zeros_like(dv_ref)
            dsink_ref[...] = jnp.zeros_like(dsink_ref)
            dtab_ref[...] = jnp.zeros_like(dtab_ref)

        valid = _swa_valid(n)
        cur_rows = pl.ds(pl.multiple_of(n * blk, blk), blk)
        prev_rows = pl.ds(pl.multiple_of(jnp.maximum(n - 1, 0) * blk, blk), blk)
        for grp in range(SWA_KV_HEADS):
            gl = slice(grp * LANES, (grp + 1) * LANES)
            kk = jnp.concatenate([kp_ref[:, gl], kc_ref[:, gl]], axis=0)
            vv = jnp.concatenate([vp_ref[:, gl], vc_ref[:, gl]], axis=0)
            dk_acc = jnp.zeros((2 * blk, LANES), F32)
            dv_acc = jnp.zeros((2 * blk, LANES), F32)
            for hh in range(SWA_GROUP):
                h = grp * SWA_GROUP + hh
                hl = slice(h * LANES, (h + 1) * LANES)
                qh = q_ref[:, hl]
                doh = do_ref[:, hl]
                p, ps = _swa_probs(qh, kk, bias_ref[h], sink_ref[0, h], valid)
                dp = _dot_nt(doh, vv)
                delta = jnp.sum(p * dp, axis=-1, keepdims=True)
                dl = p * (dp - delta)
                dsink_ref[h:h + 1, :] += jnp.broadcast_to(-jnp.sum(ps * delta, axis=0, keepdims=True), (1, LANES))
                dbias_ref[h] += dl
                dlb = dl.astype(BF16)
                dq_ref[:, hl] = (Q_SCALE * _dot(dlb, kk)).astype(BF16)
                dk_acc += Q_SCALE * _dot_tn(dlb, qh)
                dv_acc += _dot_tn(p.astype(BF16), doh)
            dk_ref[cur_rows, gl] += dk_acc[blk:]
            dv_ref[cur_rows, gl] += dv_acc[blk:]

            @pl.when(n > 0)
            def _():
                dk_ref[prev_rows, gl] += dk_acc[:blk]
                dv_ref[prev_rows, gl] += dv_acc[:blk]

        @pl.when(n == nb - 1)
        def _():
            bk = bkt_ref[...]
            lane = lax.broadcasted_iota(jnp.int32, (1, LANES), 1)
            for bucket in range(REL_BUCKETS):
                rowv = jnp.zeros((1, LANES), F32)
                for h in range(N_HEADS):
                    val = jnp.sum(jnp.where(bk == bucket, dbias_ref[h], 0.0), axis=1, keepdims=True)
                    val = jnp.sum(val, axis=0, keepdims=True)
                    rowv = jnp.where(lane == h, val, rowv)
                dtab_ref[bucket:bucket + 1, :] = rowv

    return pl.pallas_call(
        body, name="swa_bwd",
        grid=(nb,),
        in_specs=_swa_specs(s_len) + [pl.BlockSpec((blk, N_HEADS * LANES), lambda n: (n, 0))],
        out_specs=[pl.BlockSpec((blk, N_HEADS * LANES), lambda n: (n, 0)),
                   pl.BlockSpec((s_len, kvw), lambda n: (0, 0)), pl.BlockSpec((s_len, kvw), lambda n: (0, 0)),
                   pl.BlockSpec((REL_BUCKETS, LANES), lambda n: (0, 0)), pl.BlockSpec((N_HEADS, LANES), lambda n: (0, 0))],
        out_shape=[jax.ShapeDtypeStruct((s_len, N_HEADS * LANES), BF16),
                   jax.ShapeDtypeStruct((s_len, kvw), F32), jax.ShapeDtypeStruct((s_len, kvw), F32),
                   jax.ShapeDtypeStruct((REL_BUCKETS, LANES), F32), jax.ShapeDtypeStruct((N_HEADS, LANES), F32)],
        scratch_shapes=[pltpu.VMEM((N_HEADS, blk, 2 * blk), F32), pltpu.VMEM((N_HEADS, blk, 2 * blk), F32)],
        compiler_params=_cparams("arbitrary"),
    )(tab, sinks, bkt, q, k, k, v, v, do)


def _sb_terms(z, valid):
    lk = -jnp.log(1.0 + jnp.exp(-jnp.abs(z))) - jnp.maximum(z, 0.0)
    lsz = lk + z
    return lsz, (lk if valid is None else jnp.where(valid, lk, 0.0))


def _hi_lo(vals):
    hi = vals.astype(BF16)
    return jnp.concatenate([hi, (vals - hi.astype(F32)).astype(BF16)], axis=1)


def _row_sum_lanes(vals):
    return jnp.broadcast_to(jnp.sum(vals, axis=-1, keepdims=True), (vals.shape[0], LANES))


def _emit_skewed(*groups):
    for step in range(max(len(items) + len(stages) - 1 for items, stages in groups)):
        for items, stages in groups:
            for s, stage in enumerate(stages):
                if 0 <= step - s < len(items):
                    stage(items[step - s])


def _sb_items(edge):
    items = []
    for h in range(2):
        for r0 in range(0, SB_QUERIES, SB_ROWS):
            if edge is None or r0 >= (edge + 1) * SB_KEYS:
                items.append((h, r0, False))
            elif r0 + SB_ROWS - 1 > edge * SB_KEYS:
                items.append((h, r0, True))
    return items


def _sb_valid(w, edge):
    row = lax.broadcasted_iota(jnp.int32, (SB_ROWS, SB_KEYS), 0) + w[1]
    col = lax.broadcasted_iota(jnp.int32, (SB_ROWS, SB_KEYS), 1) + edge * SB_KEYS
    return col < row


def _sb_consts(tq, tk):
    low = lax.broadcasted_iota(jnp.int32, (tq, LANES), 1) < HEAD_DIM
    row = lax.broadcasted_iota(jnp.int32, (tk, tk), 0)
    col = lax.broadcasted_iota(jnp.int32, (tk, tk), 1)
    right = (row > col).astype(BF16)
    left = (row < col).astype(BF16)
    return low, jnp.concatenate([right, right], axis=0), jnp.concatenate([left, left], axis=0)


def _sb_fwd(q, kt, v):
    s_len = q.shape[0]
    tq, tk, tr = SB_QUERIES, SB_KEYS, SB_ROWS
    nk, ratio = s_len // tk, tq // tk
    assert nk <= LANES

    def body(q_ref, kt_ref, v_ref, o_ref, car_ref, c_ref, oacc_ref, logw_ref, lksum_ref):
        i = pl.program_id(1)
        qv = q_ref[...]
        low, tri2, _ = _sb_consts(tq, tk)
        lane = lax.broadcasted_iota(jnp.int32, (tr, LANES), 1)
        zero = jnp.zeros_like(qv)
        q_heads = (jnp.where(low, qv, zero), jnp.where(low, zero, qv))
        c_ref[...] = jnp.zeros_like(c_ref)
        oacc_ref[...] = jnp.zeros_like(oacc_ref)
        car_ref[...] = jnp.zeros_like(car_ref)

        def front(j, edge):
            ktv = kt_ref[0, j]
            slot = j % 2
            st = {}

            def s_logits(w):
                st[w, "z"] = _dot(q_heads[w[0]][w[1]:w[1] + tr], ktv)

            def s_terms(w):
                valid = _sb_valid(w, edge) if w[2] else None
                lsz, lk = _sb_terms(st.pop((w, "z")), valid)
                st[w, "parts"] = _hi_lo(lk)
                st[w, "lsz"] = lsz if valid is None else jnp.where(valid, lsz, NEG_BIG)
                lksum_ref[slot, w[0], w[1]:w[1] + tr, :] = _row_sum_lanes(lk)

            def s_suffix(w):
                logw_ref[slot, w[0], w[1]:w[1] + tr, :] = st.pop((w, "lsz")) + _dot(st.pop((w, "parts")), tri2)

            return _sb_items(edge), [s_logits, s_terms, s_suffix]

        def back(j, edge):
            vv = v_ref[pl.ds(pl.multiple_of(j * tk, tk), tk), :]
            slot = j % 2
            st = {}

            def s_weights(w):
                h, rs = w[0], slice(w[1], w[1] + tr)
                c = c_ref[h, rs, :]
                st[w, "a"] = jnp.exp(logw_ref[slot, h, rs, :] + jnp.tile(c, (1, tk // LANES))).astype(BF16)
                car_ref[h, rs, :] = jnp.where(lane == j, c, car_ref[h, rs, :])
                c_ref[h, rs, :] = c + lksum_ref[slot, h, rs, :]

            def s_values(w):
                oacc_ref[w[0], w[1]:w[1] + tr, :] += _dot(st.pop((w, "a")), vv)

            return _sb_items(edge), [s_weights, s_values]

        first = i * ratio
        _emit_skewed(front(first + ratio - 1, ratio - 1))
        for m in reversed(range(ratio - 1)):
            _emit_skewed(front(first + m, m), back(first + m + 1, m + 1))

        @pl.when(i == 0)
        def _():
            _emit_skewed(back(0, 0))

        @pl.when(i > 0)
        def _():
            _emit_skewed(front(first - 1, None), back(first, 0))

            def step(jj, carry):
                _emit_skewed(front(first - jj, None), back(first - jj + 1, None))
                return carry

            lax.fori_loop(2, first + 1, step, 0)
            _emit_skewed(back(0, None))

        o_ref[...] = jnp.where(low, oacc_ref[0], oacc_ref[1]).astype(BF16)

    return pl.pallas_call(
        body, name="sb_fwd",
        grid=(N_HEADS // 2, s_len // tq),
        in_specs=[pl.BlockSpec((tq, LANES), lambda p, i: (i, p)),
                  pl.BlockSpec((1, nk, LANES, tk), lambda p, i: (p, 0, 0, 0)),
                  pl.BlockSpec((s_len, LANES), lambda p, i: (0, p))],
        out_specs=[pl.BlockSpec((tq, LANES), lambda p, i: (i, p)), pl.BlockSpec((2, tq, LANES), lambda p, i: (p, i, 0))],
        out_shape=[jax.ShapeDtypeStruct((s_len, N_HEADS * HEAD_DIM), BF16),
                   jax.ShapeDtypeStruct((N_HEADS, s_len, LANES), F32)],
        scratch_shapes=[pltpu.VMEM((2, tq, LANES), F32), pltpu.VMEM((2, tq, LANES), F32),
                        pltpu.VMEM((2, 2, tq, tk), F32), pltpu.VMEM((2, 2, tq, LANES), F32)],
        compiler_params=_cparams("parallel", "arbitrary"),
    )(q, kt, v)


def _sb_bwd(q, qt, kt, k, vt, do, dot, cars):
    s_len = q.shape[0]
    tq, tk, tr = SB_QUERIES, SB_KEYS, SB_ROWS
    nk, ratio = s_len // tk, tq // tk

    def body(q_ref, qt_ref, kt_ref, k_ref, vt_ref, do_ref, dot_ref, car_ref, dq_ref, dk_ref, dv_ref,
             gleft_ref, dqacc_ref, dkacc_ref, dvacc_ref, logw_ref, lsz_ref, da_ref):
        i = pl.program_id(1)

        @pl.when(i == 0)
        def _():
            dkacc_ref[...] = jnp.zeros_like(dkacc_ref)
            dvacc_ref[...] = jnp.zeros_like(dvacc_ref)

        qv = q_ref[...]
        dov = do_ref[...]
        low, tri_right2, tri_left2 = _sb_consts(tq, tk)
        lane = lax.broadcasted_iota(jnp.int32, (tr, LANES), 1)
        zero = jnp.zeros_like(qv)
        q_heads = (jnp.where(low, qv, zero), jnp.where(low, zero, qv))
        do_heads = (jnp.where(low, dov, zero), jnp.where(low, zero, dov))
        q_t = qt_ref[0, 0]
        do_t = dot_ref[0, 0]
        gleft_ref[...] = jnp.zeros_like(gleft_ref)
        dqacc_ref[...] = jnp.zeros_like(dqacc_ref)

        def front(j, edge):
            ktv = kt_ref[0, j]
            vtv = vt_ref[0, j]
            slot = j % 2
            st = {}

            def s_logits(w):
                h, rs = w[0], slice(w[1], w[1] + tr)
                st[w, "z"] = _dot(q_heads[h][rs], ktv)
                da_ref[slot, h, rs, :] = _dot(do_heads[h][rs], vtv)

            def s_terms(w):
                h, rs = w[0], slice(w[1], w[1] + tr)
                valid = _sb_valid(w, edge) if w[2] else None
                lsz, lk = _sb_terms(st.pop((w, "z")), valid)
                st[w, "parts"] = _hi_lo(lk)
                lsz = lsz if valid is None else jnp.where(valid, lsz, NEG_BIG)
                lsz_ref[slot, h, rs, :] = lsz
                st[w, "lszc"] = lsz + jnp.sum(jnp.where(lane == j, car_ref[h, rs, :], 0.0), axis=-1, keepdims=True)

            def s_suffix(w):
                logw_ref[slot, w[0], w[1]:w[1] + tr, :] = st.pop((w, "lszc")) + _dot(st.pop((w, "parts")), tri_right2)

            return _sb_items(edge), [s_logits, s_terms, s_suffix]

        def back(j, edge):
            kv = k_ref[pl.ds(pl.multiple_of(j * tk, tk), tk), :]
            slot = j % 2
            st = {}

            def s_weights(w):
                h, rs = w[0], slice(w[1], w[1] + tr)
                a = jnp.exp(logw_ref[slot, h, rs, :])
                g = a * da_ref[slot, h, rs, :]
                st[w, "a"], st[w, "g"], st[w, "parts"] = a.astype(BF16), g, _hi_lo(g)

            def s_prefix(w):
                st[w, "gs"] = _dot(st.pop((w, "parts")), tri_left2)

            def s_dz(w):
                h, rs = w[0], slice(w[1], w[1] + tr)
                g = st.pop((w, "g"))
                gleft = gleft_ref[h, rs, :]
                gsum = st.pop((w, "gs")) + jnp.tile(gleft, (1, tk // LANES))
                st[w, "dz"] = (g - jnp.exp(lsz_ref[slot, h, rs, :]) * (g + gsum)).astype(BF16)
                gleft_ref[h, rs, :] = gleft + _row_sum_lanes(g)

            def s_products(w):
                h, rs = w[0], slice(w[1], w[1] + tr)
                dz = st.pop((w, "dz"))
                dqacc_ref[h, rs, :] += _dot(dz, kv)
                feat = slice(h * HEAD_DIM, (h + 1) * HEAD_DIM)
                dkacc_ref[j, feat, :] += _dot(q_t[feat, rs], dz)
                dvacc_ref[j, feat, :] += _dot(do_t[feat, rs], st.pop((w, "a")))

            return _sb_items(edge), [s_weights, s_prefix, s_dz, s_products]

        first = i * ratio

        @pl.when(i == 0)
        def _():
            _emit_skewed(front(0, 0))

        @pl.when(i > 0)
        def _():
            _emit_skewed(front(0, None))

            def step(jj, carry):
                _emit_skewed(front(jj, None), back(jj - 1, None))
                return carry

            lax.fori_loop(1, first, step, 0)
            _emit_skewed(front(first, 0), back(first - 1, None))

        for m in range(1, ratio):
            _emit_skewed(front(first + m, m), back(first + m - 1, m - 1))
        _emit_skewed(back(first + ratio - 1, ratio - 1))
        dq_ref[...] = (Q_SCALE * jnp.where(low, dqacc_ref[0], dqacc_ref[1])).astype(BF16)

        @pl.when(i == s_len // tq - 1)
        def _():
            dk_ref[0] = dkacc_ref[...].astype(BF16)
            dv_ref[0] = dvacc_ref[...].astype(BF16)

    qblk = pl.BlockSpec((tq, LANES), lambda p, i: (i, p))
    qtblk = pl.BlockSpec((1, 1, LANES, tq), lambda p, i: (p, i, 0, 0))
    tblk = pl.BlockSpec((1, nk, LANES, tk), lambda p, i: (p, 0, 0, 0))
    col_full = pl.BlockSpec((s_len, LANES), lambda p, i: (0, p))
    tshape = jax.ShapeDtypeStruct((N_HEADS // 2, nk, LANES, tk), BF16)
    return pl.pallas_call(
        body, name="sb_bwd",
        grid=(N_HEADS // 2, s_len // tq),
        in_specs=[qblk, qtblk, tblk, col_full, tblk, qblk, qtblk, pl.BlockSpec((2, tq, LANES), lambda p, i: (p, i, 0))],
        out_specs=[qblk, tblk, tblk],
        out_shape=[jax.ShapeDtypeStruct((s_len, N_HEADS * HEAD_DIM), BF16), tshape, tshape],
        scratch_shapes=[pltpu.VMEM((2, tq, LANES), F32), pltpu.VMEM((2, tq, LANES), F32),
                        pltpu.VMEM((nk, LANES, tk), F32), pltpu.VMEM((nk, LANES, tk), F32)]
        + [pltpu.VMEM((2, 2, tq, tk), F32)] * 3,
        compiler_params=_cparams("parallel", "arbitrary"),
    )(q, qt, kt, k, vt, do, dot, cars)


def _pad_heads(a, heads):
    s_len = a.shape[0]
    a = a.reshape(s_len, heads, HEAD_DIM)
    return jnp.pad(a, ((0, 0), (0, 0), (0, LANES - HEAD_DIM))).reshape(s_len, heads * LANES)


def _unpad_heads(a, heads):
    s_len = a.shape[0]
    return a.reshape(s_len, heads, LANES)[:, :, :HEAD_DIM].reshape(s_len, heads * HEAD_DIM)


def _tile_transposed(a, groups, t):
    s_len = a.shape[0]
    return a.reshape(s_len // t, t, groups, LANES).transpose(2, 0, 3, 1)


def _tile_untransposed(a):
    groups, nt, _, t = a.shape
    return a.transpose(1, 3, 0, 2).reshape(nt * t, groups * LANES)


def _local_step(xs, tgt, gains, sinks, rel_bias, wts):
    g1, gmix, g2, gfin = gains
    bkt = _rel_bucket_matrix()
    groups = N_HEADS // 2

    x1, h1, a1, b1, u1 = _ffn_fwd(xs, g1, wts["ffn1_w1t"], wts["ffn1_w3t"], wts["ffn1_w2"], "1")
    hm, qa, ka, va, qb, kb, vb, ga, gb = _proj_fwd(x1, gmix, wts["w_int"])
    qa_p, ka_p, va_p = _pad_heads(qa, N_HEADS), _pad_heads(ka, SWA_KV_HEADS), _pad_heads(va, SWA_KV_HEADS)
    oa_p = _swa_fwd(rel_bias, sinks, bkt, qa_p, ka_p, va_p)
    kbt = _tile_transposed(kb, groups, SB_KEYS)
    ob, cars = _sb_fwd(qb, kbt, vb)
    oa = _unpad_heads(oa_p, N_HEADS)
    x2, mg = _merge_fwd(x1, oa, ob, ga, gb, wts["w_swa"], wts["w_sb"], wts["w_out"])
    x3, h3, a3, b3, u3 = _ffn_fwd(x2, g2, wts["ffn2_w1t"], wts["ffn2_w3t"], wts["ffn2_w2"], "2")
    dx3, loss, dgfin = _loss_fwd_bwd(x3, tgt, gfin)

    big = {}
    dx2, dg2, da3, db3, dx3b = _ffn_bwd(dx3, x2, g2, a3, b3, wts["ffn2_w1t"], wts["ffn2_w3t"], wts["ffn2_w2"], "2")
    big["ffn2_w1t"] = _matmul_tn(da3, h3, "ffn2_w1")
    big["ffn2_w3t"] = _matmul_tn(db3, h3, "ffn2_w3")
    big["ffn2_w2"] = _matmul_tn(u3, dx3b, "ffn2_w2")

    doa, dob, dga, dgb, dpa, dpb, dx2b = _merge_bwd(dx2, oa, ob, ga, gb, wts["w_swa"], wts["w_sb"], wts["w_out"])
    big["w_out"] = _matmul_tn(mg, dx2b, "w_out")
    big["w_swa"] = _matmul_tn(oa, dpa, "w_swa")
    big["w_sb"] = _matmul_tn(ob, dpb, "w_sb")

    dqa_p, dka_p, dva_p, dtab, dsink = _swa_bwd(rel_bias, sinks, bkt, qa_p, ka_p, va_p, _pad_heads(doa, N_HEADS))
    dqb, dkbt, dvbt = _sb_bwd(qb, _tile_transposed(qb, groups, SB_QUERIES), kbt, kb,
                              _tile_transposed(vb, groups, SB_KEYS), dob, _tile_transposed(dob, groups, SB_QUERIES), cars)
    dkb, dvb = _tile_untransposed(dkbt), _tile_untransposed(dvbt)
    dpieces = (_unpad_heads(dqa_p, N_HEADS), _unpad_heads(dka_p, SWA_KV_HEADS).astype(BF16),
               _unpad_heads(dva_p, SWA_KV_HEADS).astype(BF16), dqb, dkb, dvb, dga, dgb)
    big["w_int"] = jnp.concatenate([_matmul_tn(dp, hm, f"w_in{p}") for p, dp in enumerate(dpieces)], axis=0)
    dx1, dgmix = _proj_bwd(dpieces, dx2, x1, gmix, wts["w_int"])

    dx0, dg1, da1, db1, dx1b = _ffn_bwd(dx1, xs, g1, a1, b1, wts["ffn1_w1t"], wts["ffn1_w3t"], wts["ffn1_w2"], "1")
    big["ffn1_w1t"] = _matmul_tn(da1, h1, "ffn1_w1")
    big["ffn1_w3t"] = _matmul_tn(db1, h1, "ffn1_w3")
    big["ffn1_w2"] = _matmul_tn(u1, dx1b, "ffn1_w2")

    small = {"gains": (dg1, dgmix, dg2, dgfin), "sinks": dsink[:, 0], "rel_bias": dtab[:, :N_HEADS]}
    return loss, dx0, big, small


def _my_place():
    return lax.axis_index("x"), lax.axis_index("y"), lax.axis_index("c")


def _flip(v, bit):
    return 1 - v if bit else v


_RELATIONS = tuple((k >> 2 & 1, k >> 1 & 1, k & 1) for k in range(1, N_DEV))


def _gather_weights(wp):
    def body(x_ref, out_ref, send_sems, recv_sems, local_sem):
        x, y, c = _my_place()
        me, sibling = (x, y, c), (x, y, 1 - c)
        chips = [(1 - x, y), (x, 1 - y), (1 - x, 1 - y)]

        def rows(px, py, pc):
            return out_ref.at[4 * px + 2 * py + pc]

        def copy(k, block, to, src=None):
            return pltpu.make_async_remote_copy(
                src_ref=rows(*block) if src is None else src, dst_ref=rows(*block),
                send_sem=send_sems.at[k], recv_sem=recv_sems.at[k],
                device_id=to, device_id_type=pl.DeviceIdType.MESH)

        mine = pltpu.make_async_copy(x_ref, rows(*me), local_sem)
        mine.start()
        first = [copy(0, me, sibling, src=x_ref)]
        first += [copy(1 + j, me, (*chip, c), src=x_ref) for j, chip in enumerate(chips)]
        for cp in first:
            cp.start()
        passed = [copy(4 + j, (*chip, c), sibling) for j, chip in enumerate(chips)]
        for j, chip in enumerate(chips):
            copy(1 + j, (*chip, c), me).wait_recv()
            passed[j].start()
        copy(0, sibling, me).wait_recv()
        for j, chip in enumerate(chips):
            copy(4 + j, (*chip, 1 - c), me).wait_recv()
        for cp in first + passed:
            cp.wait_send()
        mine.wait()

    return pl.pallas_call(
        body, name="gather_weights",
        out_shape=jax.ShapeDtypeStruct((N_DEV,) + wp.shape, wp.dtype),
        in_specs=[pl.BlockSpec(memory_space=pl.ANY)],
        out_specs=pl.BlockSpec(memory_space=pl.ANY),
        scratch_shapes=[pltpu.SemaphoreType.DMA((7,)), pltpu.SemaphoreType.DMA((7,)), pltpu.SemaphoreType.DMA(())],
    )(wp)


def _exchange_grads(gp):
    def body(g_ref, out_ref, send_sems, recv_sems, local_sem):
        x, y, c = _my_place()
        me = 4 * x + 2 * y + c
        mine = pltpu.make_async_copy(g_ref.at[me], out_ref.at[me], local_sem)
        mine.start()
        copies = []
        for k, (fx, fy, fc) in enumerate(_RELATIONS):
            px, py, pc = _flip(x, fx), _flip(y, fy), _flip(c, fc)
            peer = 4 * px + 2 * py + pc
            copies.append((
                pltpu.make_async_remote_copy(
                    src_ref=g_ref.at[peer], dst_ref=out_ref.at[me], send_sem=send_sems.at[k], recv_sem=recv_sems.at[k],
                    device_id=(px, py, pc), device_id_type=pl.DeviceIdType.MESH),
                pltpu.make_async_remote_copy(
                    src_ref=g_ref.at[peer], dst_ref=out_ref.at[peer], send_sem=send_sems.at[k], recv_sem=recv_sems.at[k],
                    device_id=(px, py, pc), device_id_type=pl.DeviceIdType.MESH)))
        for out_cp, _ in copies:
            out_cp.start()
        for _, in_cp in copies:
            in_cp.wait_recv()
        for out_cp, _ in copies:
            out_cp.wait_send()
        mine.wait()

    return pl.pallas_call(
        body, name="exchange_grads",
        out_shape=jax.ShapeDtypeStruct(gp.shape, gp.dtype),
        in_specs=[pl.BlockSpec(memory_space=pl.ANY)],
        out_specs=pl.BlockSpec(memory_space=pl.ANY),
        scratch_shapes=[pltpu.SemaphoreType.DMA((7,)), pltpu.SemaphoreType.DMA((7,)), pltpu.SemaphoreType.DMA(())],
    )(gp)


def _adamw(w, g, m, v):
    m = ADAM_B1 * m + (1.0 - ADAM_B1) * g
    v = ADAM_B2 * v + (1.0 - ADAM_B2) * jnp.square(g)
    m_hat = m / (1.0 - ADAM_B1 ** ADAM_STEP)
    v_hat = v / (1.0 - ADAM_B2 ** ADAM_STEP)
    delta = -ADAM_LR * (m_hat / (jnp.sqrt(v_hat) + ADAM_EPS) + ADAM_WD * w)
    return delta, m, v


def _sum_and_adamw(parts, w, m, v):
    rows = w.shape[0]
    tr = 112
    assert rows % tr == 0

    def body(p_ref, w_ref, m_ref, v_ref, g_out, d_out, m_out, v_out):
        g = p_ref[0].astype(F32)
        for d in range(1, N_DEV):
            g = g + p_ref[d].astype(F32)
        delta, mn, vn = _adamw(w_ref[...], g, m_ref[...], v_ref[...])
        g_out[...] = g
        d_out[...] = delta
        m_out[...] = mn
        v_out[...] = vn

    sp = pl.BlockSpec((tr, D_MODEL), lambda i: (i, 0))
    return pl.pallas_call(
        body, name="sum_and_adamw",
        grid=(rows // tr,),
        in_specs=[pl.BlockSpec((N_DEV, tr, D_MODEL), lambda i: (0, i, 0)), sp, sp, sp],
        out_specs=[sp] * 4,
        out_shape=[jax.ShapeDtypeStruct(w.shape, F32)] * 4,
        compiler_params=_cparams("parallel"),
    )(parts, w, m, v)


def _small_allreduce_adamw(part, w, m, v):
    def body(p_ref, w_ref, m_ref, v_ref, g_out, d_out, m_out, v_out, buf, send_sems, recv_sems):
        x, y, c = _my_place()
        me = 4 * x + 2 * y + c
        buf[me] = p_ref[...]
        copies = []
        for k, (fx, fy, fc) in enumerate(_RELATIONS):
            px, py, pc = _flip(x, fx), _flip(y, fy), _flip(c, fc)
            peer = 4 * px + 2 * py + pc
            copies.append((
                pltpu.make_async_remote_copy(
                    src_ref=buf.at[me], dst_ref=buf.at[me], send_sem=send_sems.at[k], recv_sem=recv_sems.at[k],
                    device_id=(px, py, pc), device_id_type=pl.DeviceIdType.MESH),
                pltpu.make_async_remote_copy(
                    src_ref=buf.at[me], dst_ref=buf.at[peer], send_sem=send_sems.at[k], recv_sem=recv_sems.at[k],
                    device_id=(px, py, pc), device_id_type=pl.DeviceIdType.MESH)))
        for out_cp, _ in copies:
            out_cp.start()
        for _, in_cp in copies:
            in_cp.wait_recv()
        for out_cp, _ in copies:
            out_cp.wait_send()
        g = buf[0]
        for d in range(1, N_DEV):
            g = g + buf[d]
        delta, mn, vn = _adamw(w_ref[...], g, m_ref[...], v_ref[...])
        g_out[...] = g
        d_out[...] = delta
        m_out[...] = mn
        v_out[...] = vn

    vm = pl.BlockSpec(memory_space=pltpu.VMEM)
    return pl.pallas_call(
        body, name="small_allreduce_adamw",
        in_specs=[vm] * 4, out_specs=[vm] * 4,
        out_shape=[jax.ShapeDtypeStruct(w.shape, F32)] * 4,
        scratch_shapes=[pltpu.VMEM((N_DEV,) + part.shape, F32),
                        pltpu.SemaphoreType.DMA((7,)), pltpu.SemaphoreType.DMA((7,))],
    )(part, w, m, v)


_TRANSPOSED = ("ffn1_w1", "ffn1_w3", "w_in", "ffn2_w1", "ffn2_w3")
_BRANCH = ("w_branch_swa", "w_branch_sb")


def _pack_shards(t):
    parts = []
    for name in BIG_NAMES:
        a = t[name][0]
        if name in _TRANSPOSED:
            a = a.T
        elif name in _BRANCH:
            a = a.reshape(64, D_MODEL)
        parts.append(a)
    return jnp.concatenate(parts, axis=0)


def _unpack_shards(p):
    out = {}
    for name, lo, hi in zip(BIG_NAMES, BIG_OFFS[:-1], BIG_OFFS[1:]):
        a = p[lo:hi]
        if name in _TRANSPOSED:
            a = a.T
        elif name in _BRANCH:
            a = a.reshape(512, 128)
        out[name] = a[None]
    return out


def _full_weights(wg):
    def part(name):
        k = BIG_NAMES.index(name)
        return wg[:, BIG_OFFS[k]:BIG_OFFS[k + 1]]

    def branch(name):
        return part(name).reshape(N_DEV, 512, 128).transpose(1, 0, 2).reshape(512, D_MODEL)

    return {
        "ffn1_w1t": part("ffn1_w1").reshape(D_FF, D_MODEL), "ffn1_w3t": part("ffn1_w3").reshape(D_FF, D_MODEL),
        "ffn1_w2": part("ffn1_w2").reshape(D_FF, D_MODEL), "w_int": part("w_in").reshape(IN_W, D_MODEL),
        "w_swa": branch("w_branch_swa"), "w_sb": branch("w_branch_sb"),
        "w_out": part("w_out").reshape(D_MODEL, D_MODEL),
        "ffn2_w1t": part("ffn2_w1").reshape(D_FF, D_MODEL), "ffn2_w3t": part("ffn2_w3").reshape(D_FF, D_MODEL),
        "ffn2_w2": part("ffn2_w2").reshape(D_FF, D_MODEL),
    }


def _pack_full_grads(big):
    def branch(a):
        return a.reshape(512, N_DEV, 128).transpose(1, 0, 2).reshape(N_DEV, 64, D_MODEL)

    parts = [big["ffn1_w1t"].reshape(N_DEV, 352, D_MODEL), big["ffn1_w3t"].reshape(N_DEV, 352, D_MODEL),
             big["ffn1_w2"].reshape(N_DEV, 352, D_MODEL), big["w_int"].reshape(N_DEV, 544, D_MODEL),
             branch(big["w_swa"]), branch(big["w_sb"]), big["w_out"].reshape(N_DEV, 128, D_MODEL),
             big["ffn2_w1t"].reshape(N_DEV, 352, D_MODEL), big["ffn2_w3t"].reshape(N_DEV, 352, D_MODEL),
             big["ffn2_w2"].reshape(N_DEV, 352, D_MODEL)]
    return jnp.concatenate([p.astype(BF16) for p in parts], axis=1)


_SMALL_NAMES = ("norm_ffn1", "norm_mix", "norm_ffn2", "norm_final", "swa_sinks", "rel_bias")


def _pack_small(vals):
    rows = []
    for a in vals:
        a = a.reshape(-1)
        rows.append(jnp.pad(a, (0, D_MODEL - a.shape[0])))
    rows += [jnp.zeros((D_MODEL,), F32)] * (SMALL_ROWS - len(rows))
    return jnp.stack(rows)


def _unpack_small(p):
    return {"norm_ffn1": p[0:1], "norm_mix": p[1:2], "norm_ffn2": p[2:3], "norm_final": p[3],
            "swa_sinks": p[4:5, :N_HEADS], "rel_bias": p[5, :REL_BUCKETS * N_HEADS].reshape(REL_BUCKETS, N_HEADS)}


ALL_NAMES = ("norm_ffn1", "ffn1_w1", "ffn1_w3", "ffn1_w2", "norm_mix", "w_in", "swa_sinks", "rel_bias",
             "w_branch_swa", "w_branch_sb", "w_out", "norm_ffn2", "ffn2_w1", "ffn2_w3", "ffn2_w2", "norm_final")


def kernel(x, norm_ffn1, ffn1_w1, ffn1_w3, ffn1_w2, norm_mix, w_in, swa_sinks, rel_bias, w_branch_swa, w_branch_sb, w_out, norm_ffn2, ffn2_w1, ffn2_w3, ffn2_w2, norm_final, loss_target, m_norm_ffn1, m_ffn1_w1, m_ffn1_w3, m_ffn1_w2, m_norm_mix, m_w_in, m_swa_sinks, m_rel_bias, m_w_branch_swa, m_w_branch_sb, m_w_out, m_norm_ffn2, m_ffn2_w1, m_ffn2_w3, m_ffn2_w2, m_norm_final, v_norm_ffn1, v_ffn1_w1, v_ffn1_w3, v_ffn1_w2, v_norm_mix, v_w_in, v_swa_sinks, v_rel_bias, v_w_branch_swa, v_w_branch_sb, v_w_out, v_norm_ffn2, v_ffn2_w1, v_ffn2_w3, v_ffn2_w2, v_norm_final):
    w = dict(zip(ALL_NAMES, (norm_ffn1, ffn1_w1, ffn1_w3, ffn1_w2, norm_mix, w_in, swa_sinks, rel_bias,
                             w_branch_swa, w_branch_sb, w_out, norm_ffn2, ffn2_w1, ffn2_w3, ffn2_w2, norm_final)))
    m = dict(zip(ALL_NAMES, (m_norm_ffn1, m_ffn1_w1, m_ffn1_w3, m_ffn1_w2, m_norm_mix, m_w_in, m_swa_sinks, m_rel_bias,
                             m_w_branch_swa, m_w_branch_sb, m_w_out, m_norm_ffn2, m_ffn2_w1, m_ffn2_w3, m_ffn2_w2,
                             m_norm_final)))
    v = dict(zip(ALL_NAMES, (v_norm_ffn1, v_ffn1_w1, v_ffn1_w3, v_ffn1_w2, v_norm_mix, v_w_in, v_swa_sinks, v_rel_bias,
                             v_w_branch_swa, v_w_branch_sb, v_w_out, v_norm_ffn2, v_ffn2_w1, v_ffn2_w3, v_ffn2_w2,
                             v_norm_final)))

    w_packed = _pack_shards(w)
    wts = _full_weights(_gather_weights(w_packed.astype(BF16)))
    gains = (norm_ffn1, norm_mix, norm_ffn2, norm_final.reshape(1, D_MODEL))
    loss, dx, big, small = _local_step(x[0], loss_target[0], gains, swa_sinks, rel_bias, wts)

    parts = _exchange_grads(_pack_full_grads(big))
    g_big, d_big, m_big, v_big = (_unpack_shards(p) for p in
                                  _sum_and_adamw(parts, w_packed, _pack_shards(m), _pack_shards(v)))

    small_part = _pack_small(small["gains"] + (small["sinks"], small["rel_bias"]))
    g_sm, d_sm, m_sm, v_sm = (_unpack_small(p) for p in _small_allreduce_adamw(
        small_part, _pack_small([w[n] for n in _SMALL_NAMES]), _pack_small([m[n] for n in _SMALL_NAMES]),
        _pack_small([v[n] for n in _SMALL_NAMES])))

    total_loss = lax.psum(loss[0, 0], AXES)
    outs = [total_loss, dx[None]]
    for big_d, small_d in ((g_big, g_sm), (d_big, d_sm), (m_big, m_sm), (v_big, v_sm)):
        merged = {**big_d, **small_d}
        outs += [merged[n] for n in ALL_NAMES]
    return tuple(outs)
```

```python
import jax
import jax.numpy as jnp
import numpy as np
from jax import lax
from jax.experimental import pallas as pl
from jax.experimental.pallas import tpu as pltpu

F32 = jnp.float32
BF16 = jnp.bfloat16

D_MODEL = 1024
D_FF = 2816
HEAD_DIM = 64
N_HEADS = 8
SWA_KV_HEADS = 2
SWA_GROUP = 4
SWA_BLOCK = 128
REL_BUCKETS = 32
REL_MAX_DIST = 128
RMS_EPS = 1e-6
NEG_BIG = -1e30
Q_SCALE = HEAD_DIM ** -0.5
LANES = 128

N_DEV = 8
AXES = ("x", "y", "c")

ADAM_LR = 0.001
ADAM_B1 = 0.9
ADAM_B2 = 0.999
ADAM_EPS = 1e-08
ADAM_WD = 0.01
ADAM_STEP = 10

IN_SIZES = (512, 128, 128, 512, 512, 512, 1024, 1024)
IN_OFFS = tuple(int(v) for v in np.cumsum((0,) + IN_SIZES))
IN_W = IN_OFFS[-1]

BIG_NAMES = ("ffn1_w1", "ffn1_w3", "ffn1_w2", "w_in", "w_branch_swa", "w_branch_sb", "w_out",
             "ffn2_w1", "ffn2_w3", "ffn2_w2")
BIG_ROWS = (352, 352, 352, 544, 64, 64, 128, 352, 352, 352)
BIG_OFFS = tuple(int(v) for v in np.cumsum((0,) + BIG_ROWS))
PACK_ROWS = BIG_OFFS[-1]
SMALL_ROWS = 8

VMEM_LIMIT = 56 * 1024 * 1024
SB_QUERIES = 512
SB_KEYS = 256
SB_ROWS = 128
SB_SUM_PARTS = 1
SB_LOGIT_CAP = 80.0


def _dot(a, b):
    return jnp.dot(a, b, preferred_element_type=F32)


def _dot_nt(a, b):
    return lax.dot_general(a, b, (((1,), (1,)), ((), ())), preferred_element_type=F32)


def _dot_tn(a, b):
    return lax.dot_general(a, b, (((0,), (0,)), ((), ())), preferred_element_type=F32)


def _cparams(*sem):
    return pltpu.CompilerParams(dimension_semantics=sem, vmem_limit_bytes=VMEM_LIMIT)


def _rms_rstd(xv):
    return lax.rsqrt(jnp.mean(xv * xv, axis=-1, keepdims=True) + RMS_EPS)


def _rms_bwd(dh, xv, r, g):
    xhat = xv * r
    dg = jnp.sum(dh * xhat, axis=0, keepdims=True)
    dxn = dh * g
    dx = r * (dxn - xhat * jnp.mean(dxn * xhat, axis=-1, keepdims=True))
    return dx, dg


def _ffn_fwd(x, g, w1t, w3t, w2, tag):
    s_len = x.shape[0]
    tm, tf = min(1024, s_len), 256
    nf = D_FF // tf

    def body(x_ref, g_ref, w1_ref, w3_ref, w2_ref, xo_ref, h_ref, a_ref, b_ref, u_ref, acc_ref, hs_ref):
        j = pl.program_id(1)

        @pl.when(j == 0)
        def _():
            xv = x_ref[...]
            h = (xv * _rms_rstd(xv) * g_ref[...]).astype(BF16)
            hs_ref[...] = h
            h_ref[...] = h
            acc_ref[...] = jnp.zeros_like(acc_ref)

        h = hs_ref[...]
        a = _dot_nt(h, w1_ref[...])
        b = _dot_nt(h, w3_ref[...])
        a_ref[...] = a.astype(BF16)
        b_ref[...] = b.astype(BF16)
        uh = (0.5 * (a * jax.nn.sigmoid(a) * b)).astype(BF16)
        u_ref[...] = uh
        acc_ref[...] += _dot(uh, w2_ref[...])

        @pl.when(j == nf - 1)
        def _():
            xo_ref[...] = x_ref[...] + acc_ref[...]

    row = lambda i, j: (i, 0)
    return pl.pallas_call(
        body, name=f"ffn_fwd_{tag}",
        grid=(s_len // tm, nf),
        in_specs=[pl.BlockSpec((tm, D_MODEL), row), pl.BlockSpec((1, D_MODEL), lambda i, j: (0, 0)),
                  pl.BlockSpec((tf, D_MODEL), lambda i, j: (j, 0)), pl.BlockSpec((tf, D_MODEL), lambda i, j: (j, 0)),
                  pl.BlockSpec((tf, D_MODEL), lambda i, j: (j, 0))],
        out_specs=[pl.BlockSpec((tm, D_MODEL), row), pl.BlockSpec((tm, D_MODEL), row),
                   pl.BlockSpec((tm, tf), lambda i, j: (i, j)), pl.BlockSpec((tm, tf), lambda i, j: (i, j)),
                   pl.BlockSpec((tm, tf), lambda i, j: (i, j))],
        out_shape=[jax.ShapeDtypeStruct((s_len, D_MODEL), F32), jax.ShapeDtypeStruct((s_len, D_MODEL), BF16),
                   jax.ShapeDtypeStruct((s_len, D_FF), BF16), jax.ShapeDtypeStruct((s_len, D_FF), BF16),
                   jax.ShapeDtypeStruct((s_len, D_FF), BF16)],
        scratch_shapes=[pltpu.VMEM((tm, D_MODEL), F32), pltpu.VMEM((tm, D_MODEL), BF16)],
        compiler_params=_cparams("parallel", "arbitrary"),
    )(x, g, w1t, w3t, w2)


def _ffn_bwd(dy, x, g, a, b, w1t, w3t, w2, tag):
    s_len = x.shape[0]
    tm, tf = min(1024, s_len), 256
    nf = D_FF // tf

    def body(dy_ref, x_ref, g_ref, a_ref, b_ref, w1_ref, w3_ref, w2_ref,
             dx_ref, dg_ref, da_ref, db_ref, dyb_ref, acc_ref, dys_ref):
        i, j = pl.program_id(0), pl.program_id(1)

        @pl.when(j == 0)
        def _():
            dyb = dy_ref[...].astype(BF16)
            dys_ref[...] = dyb
            dyb_ref[...] = dyb
            acc_ref[...] = jnp.zeros_like(acc_ref)

        @pl.when((i == 0) & (j == 0))
        def _():
            dg_ref[...] = jnp.zeros_like(dg_ref)

        du = 0.5 * _dot_nt(dys_ref[...], w2_ref[...])
        av = a_ref[...].astype(F32)
        bv = b_ref[...].astype(F32)
        sg = jax.nn.sigmoid(av)
        sil = av * sg
        da = (du * bv * (sg + sil * (1.0 - sg))).astype(BF16)
        db = (du * sil).astype(BF16)
        da_ref[...] = da
        db_ref[...] = db
        acc_ref[...] += _dot(da, w1_ref[...]) + _dot(db, w3_ref[...])

        @pl.when(j == nf - 1)
        def _():
            xv = x_ref[...]
            dx, dg = _rms_bwd(acc_ref[...], xv, _rms_rstd(xv), g_ref[...])
            dx_ref[...] = dy_ref[...] + dx
            dg_ref[...] += dg

    row = lambda i, j: (i, 0)
    blk = lambda i, j: (i, j)
    wsp = pl.BlockSpec((tf, D_MODEL), lambda i, j: (j, 0))
    return pl.pallas_call(
        body, name=f"ffn_bwd_{tag}",
        grid=(s_len // tm, nf),
        in_specs=[pl.BlockSpec((tm, D_MODEL), row), pl.BlockSpec((tm, D_MODEL), row),
                  pl.BlockSpec((1, D_MODEL), lambda i, j: (0, 0)),
                  pl.BlockSpec((tm, tf), blk), pl.BlockSpec((tm, tf), blk), wsp, wsp, wsp],
        out_specs=[pl.BlockSpec((tm, D_MODEL), row), pl.BlockSpec((1, D_MODEL), lambda i, j: (0, 0)),
                   pl.BlockSpec((tm, tf), blk), pl.BlockSpec((tm, tf), blk), pl.BlockSpec((tm, D_MODEL), row)],
        out_shape=[jax.ShapeDtypeStruct((s_len, D_MODEL), F32), jax.ShapeDtypeStruct((1, D_MODEL), F32),
                   jax.ShapeDtypeStruct((s_len, D_FF), BF16), jax.ShapeDtypeStruct((s_len, D_FF), BF16),
                   jax.ShapeDtypeStruct((s_len, D_MODEL), BF16)],
        scratch_shapes=[pltpu.VMEM((tm, D_MODEL), F32), pltpu.VMEM((tm, D_MODEL), BF16)],
        compiler_params=_cparams("arbitrary", "arbitrary"),
    )(dy, x, g, a, b, w1t, w3t, w2)


def _matmul_tn(lhs, rhs, tag):
    s_len, m = lhs.shape
    n = rhs.shape[1]
    tm = min(512, s_len)
    tj = m if m <= 1024 else 1408
    assert m % tj == 0

    def body(l_ref, r_ref, o_ref):
        @pl.when(pl.program_id(1) == 0)
        def _():
            o_ref[...] = jnp.zeros_like(o_ref)

        o_ref[...] += _dot_tn(l_ref[...], r_ref[...])

    return pl.pallas_call(
        body, name=f"matmul_tn_{tag}",
        grid=(m // tj, s_len // tm),
        in_specs=[pl.BlockSpec((tm, tj), lambda j, i: (i, j)), pl.BlockSpec((tm, n), lambda j, i: (i, 0))],
        out_specs=pl.BlockSpec((tj, n), lambda j, i: (j, 0)),
        out_shape=jax.ShapeDtypeStruct((m, n), F32),
        compiler_params=_cparams("parallel", "arbitrary"),
    )(lhs, rhs)


def _proj_fwd(x1, g, wint):
    s_len = x1.shape[0]
    tm = min(512, s_len)
    dts = (BF16, BF16, BF16, BF16, BF16, BF16, F32, F32)

    def body(x_ref, g_ref, w_ref, h_ref, *outs):
        xv = x_ref[...]
        h = (xv * _rms_rstd(xv) * g_ref[...]).astype(BF16)
        h_ref[...] = h
        for p, o_ref in enumerate(outs):
            val = _dot_nt(h, w_ref[IN_OFFS[p]:IN_OFFS[p + 1], :])
            if p == 3:
                val = val * Q_SCALE
            o_ref[...] = val.astype(dts[p])

    row = lambda i: (i, 0)
    return pl.pallas_call(
        body, name="proj_fwd",
        grid=(s_len // tm,),
        in_specs=[pl.BlockSpec((tm, D_MODEL), row), pl.BlockSpec((1, D_MODEL), lambda i: (0, 0)),
                  pl.BlockSpec((IN_W, D_MODEL), lambda i: (0, 0))],
        out_specs=[pl.BlockSpec((tm, D_MODEL), row)] + [pl.BlockSpec((tm, w), row) for w in IN_SIZES],
        out_shape=[jax.ShapeDtypeStruct((s_len, D_MODEL), BF16)]
        + [jax.ShapeDtypeStruct((s_len, w), dt) for w, dt in zip(IN_SIZES, dts)],
        compiler_params=_cparams("parallel"),
    )(x1, g, wint)


def _proj_bwd(dpieces, dx2, x1, g, wint):
    s_len = x1.shape[0]
    tm = min(512, s_len)

    def body(*refs):
        dps = refs[:8]
        dx2_ref, x_ref, g_ref, w_ref, dx_ref, dg_ref = refs[8:]

        @pl.when(pl.program_id(0) == 0)
        def _():
            dg_ref[...] = jnp.zeros_like(dg_ref)

        dh = _dot(dps[0][...], w_ref[IN_OFFS[0]:IN_OFFS[1], :])
        for p in range(1, 8):
            dh += _dot(dps[p][...], w_ref[IN_OFFS[p]:IN_OFFS[p + 1], :])
        xv = x_ref[...]
        dx, dg = _rms_bwd(dh, xv, _rms_rstd(xv), g_ref[...])
        dx_ref[...] = dx2_ref[...] + dx
        dg_ref[...] += dg

    row = lambda i: (i, 0)
    return pl.pallas_call(
        body, name="proj_bwd",
        grid=(s_len // tm,),
        in_specs=[pl.BlockSpec((tm, w), row) for w in IN_SIZES]
        + [pl.BlockSpec((tm, D_MODEL), row), pl.BlockSpec((tm, D_MODEL), row),
           pl.BlockSpec((1, D_MODEL), lambda i: (0, 0)), pl.BlockSpec((IN_W, D_MODEL), lambda i: (0, 0))],
        out_specs=[pl.BlockSpec((tm, D_MODEL), row), pl.BlockSpec((1, D_MODEL), lambda i: (0, 0))],
        out_shape=[jax.ShapeDtypeStruct((s_len, D_MODEL), F32), jax.ShapeDtypeStruct((1, D_MODEL), F32)],
        compiler_params=_cparams("arbitrary"),
    )(*dpieces, dx2, x1, g, wint)


def _merge_fwd(x1, oa, ob, ga, gb, wswa, wsb, wout):
    s_len = x1.shape[0]
    tm = min(512, s_len)

    def body(x_ref, oa_ref, ob_ref, ga_ref, gb_ref, wa_ref, wb_ref, wo_ref, xo_ref, mg_ref):
        pa = _dot(oa_ref[...], wa_ref[...])
        pb = _dot(ob_ref[...], wb_ref[...])
        mg = (jax.nn.sigmoid(ga_ref[...]) * pa + jax.nn.sigmoid(gb_ref[...]) * pb).astype(BF16)
        mg_ref[...] = mg
        xo_ref[...] = x_ref[...] + _dot(mg, wo_ref[...])

    row = lambda i: (i, 0)
    full = lambda i: (0, 0)
    return pl.pallas_call(
        body, name="merge_fwd",
        grid=(s_len // tm,),
        in_specs=[pl.BlockSpec((tm, D_MODEL), row), pl.BlockSpec((tm, 512), row), pl.BlockSpec((tm, 512), row),
                  pl.BlockSpec((tm, D_MODEL), row), pl.BlockSpec((tm, D_MODEL), row),
                  pl.BlockSpec((512, D_MODEL), full), pl.BlockSpec((512, D_MODEL), full),
                  pl.BlockSpec((D_MODEL, D_MODEL), full)],
        out_specs=[pl.BlockSpec((tm, D_MODEL), row), pl.BlockSpec((tm, D_MODEL), row)],
        out_shape=[jax.ShapeDtypeStruct((s_len, D_MODEL), F32), jax.ShapeDtypeStruct((s_len, D_MODEL), BF16)],
        compiler_params=_cparams("parallel"),
    )(x1, oa, ob, ga, gb, wswa, wsb, wout)


def _merge_bwd(dx2, oa, ob, ga, gb, wswa, wsb, wout):
    s_len = dx2.shape[0]
    tm = min(512, s_len)

    def body(dx_ref, oa_ref, ob_ref, ga_ref, gb_ref, wa_ref, wb_ref, wo_ref,
             doa_ref, dob_ref, dga_ref, dgb_ref, dpa_ref, dpb_ref, dxb_ref):
        dxb = dx_ref[...].astype(BF16)
        dxb_ref[...] = dxb
        dmg = _dot_nt(dxb, wo_ref[...])
        for o_ref, g_ref, w_ref, do_ref, dg_ref, dp_ref in (
                (oa_ref, ga_ref, wa_ref, doa_ref, dga_ref, dpa_ref),
                (ob_ref, gb_ref, wb_ref, dob_ref, dgb_ref, dpb_ref)):
            pv = _dot(o_ref[...], w_ref[...])
            sg = jax.nn.sigmoid(g_ref[...])
            dp = (dmg * sg).astype(BF16)
            dp_ref[...] = dp
            dg_ref[...] = (dmg * pv * sg * (1.0 - sg)).astype(BF16)
            do_ref[...] = _dot_nt(dp, w_ref[...]).astype(BF16)

    row = lambda i: (i, 0)
    full = lambda i: (0, 0)
    wide = pl.BlockSpec((tm, D_MODEL), row)
    half = pl.BlockSpec((tm, 512), row)
    return pl.pallas_call(
        body, name="merge_bwd",
        grid=(s_len // tm,),
        in_specs=[wide, half, half, wide, wide, pl.BlockSpec((512, D_MODEL), full),
                  pl.BlockSpec((512, D_MODEL), full), pl.BlockSpec((D_MODEL, D_MODEL), full)],
        out_specs=[half, half, wide, wide, wide, wide, wide],
        out_shape=[jax.ShapeDtypeStruct((s_len, 512), BF16)] * 2 + [jax.ShapeDtypeStruct((s_len, D_MODEL), BF16)] * 5,
        compiler_params=_cparams("parallel"),
    )(dx2, oa, ob, ga, gb, wswa, wsb, wout)


def _loss_fwd_bwd(x3, tgt, g):
    s_len = x3.shape[0]
    tm = min(1024, s_len)

    def body(x_ref, t_ref, g_ref, dx_ref, loss_ref, dg_ref):
        @pl.when(pl.program_id(0) == 0)
        def _():
            loss_ref[...] = jnp.zeros_like(loss_ref)
            dg_ref[...] = jnp.zeros_like(dg_ref)

        xv = x_ref[...]
        gv = g_ref[...]
        r = _rms_rstd(xv)
        err = xv * r * gv - t_ref[...]
        loss_ref[...] += 0.5 * jnp.sum(jnp.mean(err * err, axis=-1, keepdims=True), axis=0, keepdims=True)
        dx, dg = _rms_bwd(err * (1.0 / D_MODEL), xv, r, gv)
        dx_ref[...] = dx
        dg_ref[...] += dg

    row = lambda i: (i, 0)
    return pl.pallas_call(
        body, name="loss_fwd_bwd",
        grid=(s_len // tm,),
        in_specs=[pl.BlockSpec((tm, D_MODEL), row), pl.BlockSpec((tm, D_MODEL), row),
                  pl.BlockSpec((1, D_MODEL), lambda i: (0, 0))],
        out_specs=[pl.BlockSpec((tm, D_MODEL), row), pl.BlockSpec((1, 1), lambda i: (0, 0)),
                   pl.BlockSpec((1, D_MODEL), lambda i: (0, 0))],
        out_shape=[jax.ShapeDtypeStruct((s_len, D_MODEL), F32), jax.ShapeDtypeStruct((1, 1), F32),
                   jax.ShapeDtypeStruct((1, D_MODEL), F32)],
        compiler_params=_cparams("arbitrary"),
    )(x3, tgt, g)


def _rel_bucket_matrix():
    qi = jnp.arange(SWA_BLOCK)[:, None] + SWA_BLOCK
    kj = jnp.arange(2 * SWA_BLOCK)[None, :]
    dist = jnp.maximum(qi - kj, 0)
    max_exact = REL_BUCKETS // 2
    d = jnp.maximum(dist, 1).astype(F32)
    large = max_exact + (jnp.log(d / max_exact) / np.log(REL_MAX_DIST / max_exact)
                         * (REL_BUCKETS - max_exact)).astype(jnp.int32)
    large = jnp.minimum(large, REL_BUCKETS - 1)
    return jnp.where(dist < max_exact, dist, large).astype(jnp.int32)


def _swa_bias_into(bias_ref, bkt_ref, tab_ref):
    bk = bkt_ref[...]
    for h in range(N_HEADS):
        acc = jnp.zeros(bk.shape, F32)
        for bucket in range(REL_BUCKETS):
            acc = jnp.where(bk == bucket, tab_ref[bucket, h], acc)
        bias_ref[h] = acc


def _swa_valid(n):
    shape = (SWA_BLOCK, 2 * SWA_BLOCK)
    row = lax.broadcasted_iota(jnp.int32, shape, 0)
    col = lax.broadcasted_iota(jnp.int32, shape, 1)
    dist = row + SWA_BLOCK - col
    return (dist >= 0) & (dist < SWA_BLOCK) & ((col >= SWA_BLOCK) | (n > 0))


def _swa_probs(q, k, bias, sink, valid):
    lg = jnp.where(valid, _dot_nt(q, k) * Q_SCALE + bias, NEG_BIG)
    m = jnp.maximum(jnp.max(lg, axis=-1, keepdims=True), sink)
    e = jnp.exp(lg - m)
    es = jnp.exp(sink - m)
    den = jnp.sum(e, axis=-1, keepdims=True) + es
    return e / den, es / den


def _swa_specs(s_len):
    blk = SWA_BLOCK
    cur = lambda n: (n, 0)
    prev = lambda n: (jnp.maximum(n - 1, 0), 0)
    kvw = SWA_KV_HEADS * LANES
    return [pl.BlockSpec(memory_space=pltpu.SMEM), pl.BlockSpec(memory_space=pltpu.SMEM),
            pl.BlockSpec((blk, 2 * blk), lambda n: (0, 0)),
            pl.BlockSpec((blk, N_HEADS * LANES), cur),
            pl.BlockSpec((blk, kvw), prev), pl.BlockSpec((blk, kvw), cur),
            pl.BlockSpec((blk, kvw), prev), pl.BlockSpec((blk, kvw), cur)]


def _swa_fwd(tab, sinks, bkt, q, k, v):
    s_len = q.shape[0]
    blk = SWA_BLOCK

    def body(tab_ref, sink_ref, bkt_ref, q_ref, kp_ref, kc_ref, vp_ref, vc_ref, o_ref, bias_ref):
        n = pl.program_id(0)

        @pl.when(n == 0)
        def _():
            _swa_bias_into(bias_ref, bkt_ref, tab_ref)

        valid = _swa_valid(n)
        for grp in range(SWA_KV_HEADS):
            gl = slice(grp * LANES, (grp + 1) * LANES)
            kk = jnp.concatenate([kp_ref[:, gl], kc_ref[:, gl]], axis=0)
            vv = jnp.concatenate([vp_ref[:, gl], vc_ref[:, gl]], axis=0)
            for hh in range(SWA_GROUP):
                h = grp * SWA_GROUP + hh
                hl = slice(h * LANES, (h + 1) * LANES)
                p, _ = _swa_probs(q_ref[:, hl], kk, bias_ref[h], sink_ref[0, h], valid)
                o_ref[:, hl] = _dot(p.astype(BF16), vv).astype(BF16)

    return pl.pallas_call(
        body, name="swa_fwd",
        grid=(s_len // blk,),
        in_specs=_swa_specs(s_len),
        out_specs=pl.BlockSpec((blk, N_HEADS * LANES), lambda n: (n, 0)),
        out_shape=jax.ShapeDtypeStruct((s_len, N_HEADS * LANES), BF16),
        scratch_shapes=[pltpu.VMEM((N_HEADS, blk, 2 * blk), F32)],
        compiler_params=_cparams("arbitrary"),
    )(tab, sinks, bkt, q, k, k, v, v)


def _swa_bwd(tab, sinks, bkt, q, k, v, do):
    s_len = q.shape[0]
    blk = SWA_BLOCK
    nb = s_len // blk
    kvw = SWA_KV_HEADS * LANES

    def body(tab_ref, sink_ref, bkt_ref, q_ref, kp_ref, kc_ref, vp_ref, vc_ref, do_ref,
             dq_ref, dk_ref, dv_ref, dtab_ref, dsink_ref, bias_ref, dbias_ref):
        n = pl.program_id(0)

        @pl.when(n == 0)
        def _():
            _swa_bias_into(bias_ref, bkt_ref, tab_ref)
            dbias_ref[...] = jnp.zeros_like(dbias_ref)
            dk_ref[...] = jnp.zeros_like(dk_ref)
            dv_ref[...] = jnp.zeros_like(dv_ref)
            dsink_ref[...] = jnp.zeros_like(dsink_ref)
            dtab_ref[...] = jnp.zeros_like(dtab_ref)

        valid = _swa_valid(n)
        cur_rows = pl.ds(pl.multiple_of(n * blk, blk), blk)
        prev_rows = pl.ds(pl.multiple_of(jnp.maximum(n - 1, 0) * blk, blk), blk)
        for grp in range(SWA_KV_HEADS):
            gl = slice(grp * LANES, (grp + 1) * LANES)
            kk = jnp.concatenate([kp_ref[:, gl], kc_ref[:, gl]], axis=0)
            vv = jnp.concatenate([vp_ref[:, gl], vc_ref[:, gl]], axis=0)
            dk_acc = jnp.zeros((2 * blk, LANES), F32)
            dv_acc = jnp.zeros((2 * blk, LANES), F32)
            for hh in range(SWA_GROUP):
                h = grp * SWA_GROUP + hh
                hl = slice(h * LANES, (h + 1) * LANES)
                qh = q_ref[:, hl]
                doh = do_ref[:, hl]
                p, ps = _swa_probs(qh, kk, bias_ref[h], sink_ref[0, h], valid)
                dp = _dot_nt(doh, vv)
                delta = jnp.sum(p * dp, axis=-1, keepdims=True)
                dl = p * (dp - delta)
                dsink_ref[h:h + 1, :] += jnp.broadcast_to(-jnp.sum(ps * delta, axis=0, keepdims=True), (1, LANES))
                dbias_ref[h] += dl
                dlb = dl.astype(BF16)
                dq_ref[:, hl] = (Q_SCALE * _dot(dlb, kk)).astype(BF16)
                dk_acc += Q_SCALE * _dot_tn(dlb, qh)
                dv_acc += _dot_tn(p.astype(BF16), doh)
            dk_ref[cur_rows, gl] += dk_acc[blk:]
            dv_ref[cur_rows, gl] += dv_acc[blk:]

            @pl.when(n > 0)
            def _():
                dk_ref[prev_rows, gl] += dk_acc[:blk]
                dv_ref[prev_rows, gl] += dv_acc[:blk]

        @pl.when(n == nb - 1)
        def _():
            bk = bkt_ref[...]
            lane = lax.broadcasted_iota(jnp.int32, (1, LANES), 1)
            for bucket in range(REL_BUCKETS):
                rowv = jnp.zeros((1, LANES), F32)
                for h in range(N_HEADS):
                    val = jnp.sum(jnp.where(bk == bucket, dbias_ref[h], 0.0), axis=1, keepdims=True)
                    val = jnp.sum(val, axis=0, keepdims=True)
                    rowv = jnp.where(lane == h, val, rowv)
                dtab_ref[bucket:bucket + 1, :] = rowv

    return pl.pallas_call(
        body, name="swa_bwd",
        grid=(nb,),
        in_specs=_swa_specs(s_len) + [pl.BlockSpec((blk, N_HEADS * LANES), lambda n: (n, 0))],
        out_specs=[pl.BlockSpec((blk, N_HEADS * LANES), lambda n: (n, 0)),
                   pl.BlockSpec((s_len, kvw), lambda n: (0, 0)), pl.BlockSpec((s_len, kvw), lambda n: (0, 0)),
                   pl.BlockSpec((REL_BUCKETS, LANES), lambda n: (0, 0)), pl.BlockSpec((N_HEADS, LANES), lambda n: (0, 0))],
        out_shape=[jax.ShapeDtypeStruct((s_len, N_HEADS * LANES), BF16),
                   jax.ShapeDtypeStruct((s_len, kvw), F32), jax.ShapeDtypeStruct((s_len, kvw), F32),
                   jax.ShapeDtypeStruct((REL_BUCKETS, LANES), F32), jax.ShapeDtypeStruct((N_HEADS, LANES), F32)],
        scratch_shapes=[pltpu.VMEM((N_HEADS, blk, 2 * blk), F32), pltpu.VMEM((N_HEADS, blk, 2 * blk), F32)],
        compiler_params=_cparams("arbitrary"),
    )(tab, sinks, bkt, q, k, k, v, v, do)


def _sb_terms(z, valid):
    zc = jnp.minimum(z, SB_LOGIT_CAP)
    lk = -jnp.log(1.0 + jnp.exp(zc))
    lsz = zc + lk
    return lsz, (lk if valid is None else jnp.where(valid, lk, 0.0))


def _bf16_parts(vals):
    parts, rest = [], vals
    for n in range(SB_SUM_PARTS):
        parts.append(rest.astype(BF16))
        if n + 1 < SB_SUM_PARTS:
            rest = rest - parts[-1].astype(F32)
    return parts[0] if len(parts) == 1 else jnp.concatenate(parts, axis=1)


def _row_sum_lanes(vals):
    return jnp.broadcast_to(jnp.sum(vals, axis=-1, keepdims=True), (vals.shape[0], LANES))


def _emit_skewed(*groups):
    for step in range(max(len(items) + len(stages) - 1 for items, stages in groups)):
        for items, stages in groups:
            for s, stage in enumerate(stages):
                if 0 <= step - s < len(items):
                    stage(items[step - s])


def _sb_items(edge):
    items = []
    for h in range(2):
        for r0 in range(0, SB_QUERIES, SB_ROWS):
            if edge is None or r0 >= (edge + 1) * SB_KEYS:
                items.append((h, r0, False))
            elif r0 + SB_ROWS - 1 > edge * SB_KEYS:
                items.append((h, r0, True))
    return items


def _sb_valid(w, edge):
    row = lax.broadcasted_iota(jnp.int32, (SB_ROWS, SB_KEYS), 0) + w[1]
    col = lax.broadcasted_iota(jnp.int32, (SB_ROWS, SB_KEYS), 1) + edge * SB_KEYS
    return col < row


def _sb_consts(tq, tk):
    low = lax.broadcasted_iota(jnp.int32, (tq, LANES), 1) < HEAD_DIM
    row = lax.broadcasted_iota(jnp.int32, (tk, tk), 0)
    col = lax.broadcasted_iota(jnp.int32, (tk, tk), 1)
    right = (row > col).astype(BF16)
    left = (row < col).astype(BF16)
    return low, jnp.concatenate([right] * SB_SUM_PARTS, axis=0), jnp.concatenate([left] * SB_SUM_PARTS, axis=0)


def _sb_fwd(q, kt, v):
    s_len = q.shape[0]
    tq, tk, tr = SB_QUERIES, SB_KEYS, SB_ROWS
    nk, ratio = s_len // tk, tq // tk
    assert nk <= LANES

    def body(q_ref, kt_ref, v_ref, o_ref, car_ref, c_ref, oacc_ref, logw_ref, lksum_ref):
        i = pl.program_id(1)
        qv = q_ref[...]
        low, tri2, _ = _sb_consts(tq, tk)
        lane = lax.broadcasted_iota(jnp.int32, (tr, LANES), 1)
        zero = jnp.zeros_like(qv)
        q_heads = (jnp.where(low, qv, zero), jnp.where(low, zero, qv))
        c_ref[...] = jnp.zeros_like(c_ref)
        oacc_ref[...] = jnp.zeros_like(oacc_ref)
        car_ref[...] = jnp.zeros_like(car_ref)

        def front(j, edge):
            ktv = kt_ref[0, j]
            slot = j % 2
            st = {}

            def s_logits(w):
                st[w, "z"] = _dot(q_heads[w[0]][w[1]:w[1] + tr], ktv)

            def s_terms(w):
                valid = _sb_valid(w, edge) if w[2] else None
                lsz, lk = _sb_terms(st.pop((w, "z")), valid)
                st[w, "parts"] = _bf16_parts(lk)
                st[w, "lsz"] = lsz if valid is None else jnp.where(valid, lsz, NEG_BIG)
                lksum_ref[slot, w[0], w[1]:w[1] + tr, :] = _row_sum_lanes(lk)

            def s_suffix(w):
                logw_ref[slot, w[0], w[1]:w[1] + tr, :] = st.pop((w, "lsz")) + _dot(st.pop((w, "parts")), tri2)

            return _sb_items(edge), [s_logits, s_terms, s_suffix]

        def back(j, edge):
            vv = v_ref[pl.ds(pl.multiple_of(j * tk, tk), tk), :]
            slot = j % 2
            st = {}

            def s_weights(w):
                h, rs = w[0], slice(w[1], w[1] + tr)
                c = c_ref[h, rs, :]
                st[w, "a"] = jnp.exp(logw_ref[slot, h, rs, :] + jnp.tile(c, (1, tk // LANES))).astype(BF16)
                car_ref[h, rs, :] = jnp.where(lane == j, c, car_ref[h, rs, :])
                c_ref[h, rs, :] = c + lksum_ref[slot, h, rs, :]

            def s_values(w):
                oacc_ref[w[0], w[1]:w[1] + tr, :] += _dot(st.pop((w, "a")), vv)

            return _sb_items(edge), [s_weights, s_values]

        first = i * ratio
        _emit_skewed(front(first + ratio - 1, ratio - 1))
        for m in reversed(range(ratio - 1)):
            _emit_skewed(front(first + m, m), back(first + m + 1, m + 1))

        @pl.when(i == 0)
        def _():
            _emit_skewed(back(0, 0))

        @pl.when(i > 0)
        def _():
            _emit_skewed(front(first - 1, None), back(first, 0))

            def step(jj, carry):
                _emit_skewed(front(first - jj, None), back(first - jj + 1, None))
                return carry

            lax.fori_loop(2, first + 1, step, 0)
            _emit_skewed(back(0, None))

        o_ref[...] = jnp.where(low, oacc_ref[0], oacc_ref[1]).astype(BF16)

    return pl.pallas_call(
        body, name="sb_fwd",
        grid=(N_HEADS // 2, s_len // tq),
        in_specs=[pl.BlockSpec((tq, LANES), lambda p, i: (i, p)),
                  pl.BlockSpec((1, nk, LANES, tk), lambda p, i: (p, 0, 0, 0)),
                  pl.BlockSpec((s_len, LANES), lambda p, i: (0, p))],
        out_specs=[pl.BlockSpec((tq, LANES), lambda p, i: (i, p)), pl.BlockSpec((2, tq, LANES), lambda p, i: (p, i, 0))],
        out_shape=[jax.ShapeDtypeStruct((s_len, N_HEADS * HEAD_DIM), BF16),
                   jax.ShapeDtypeStruct((N_HEADS, s_len, LANES), F32)],
        scratch_shapes=[pltpu.VMEM((2, tq, LANES), F32), pltpu.VMEM((2, tq, LANES), F32),
                        pltpu.VMEM((2, 2, tq, tk), F32), pltpu.VMEM((2, 2, tq, LANES), F32)],
        compiler_params=_cparams("parallel", "arbitrary"),
    )(q, kt, v)


def _sb_bwd(q, qt, kt, k, vt, do, dot, cars):
    s_len = q.shape[0]
    tq, tk, tr = SB_QUERIES, SB_KEYS, SB_ROWS
    nk, ratio = s_len // tk, tq // tk

    def body(q_ref, qt_ref, kt_ref, k_ref, vt_ref, do_ref, dot_ref, car_ref, dq_ref, dk_ref, dv_ref,
             gleft_ref, dqacc_ref, dkacc_ref, dvacc_ref, logw_ref, lsz_ref, da_ref, a_ref, dz_ref):
        i = pl.program_id(1)

        @pl.when(i == 0)
        def _():
            dkacc_ref[...] = jnp.zeros_like(dkacc_ref)
            dvacc_ref[...] = jnp.zeros_like(dvacc_ref)

        qv = q_ref[...]
        dov = do_ref[...]
        low, tri_right2, tri_left2 = _sb_consts(tq, tk)
        lane = lax.broadcasted_iota(jnp.int32, (tr, LANES), 1)
        zero = jnp.zeros_like(qv)
        q_heads = (jnp.where(low, qv, zero), jnp.where(low, zero, qv))
        do_heads = (jnp.where(low, dov, zero), jnp.where(low, zero, dov))
        q_t = qt_ref[0, 0]
        do_t = dot_ref[0, 0]
        gleft_ref[...] = jnp.zeros_like(gleft_ref)
        dqacc_ref[...] = jnp.zeros_like(dqacc_ref)

        def front(j, edge):
            ktv = kt_ref[0, j]
            vtv = vt_ref[0, j]
            slot = j % 2
            st = {}

            def s_logits(w):
                h, rs = w[0], slice(w[1], w[1] + tr)
                st[w, "z"] = _dot(q_heads[h][rs], ktv)
                da_ref[slot, h, rs, :] = _dot(do_heads[h][rs], vtv)

            def s_terms(w):
                h, rs = w[0], slice(w[1], w[1] + tr)
                valid = _sb_valid(w, edge) if w[2] else None
                lsz, lk = _sb_terms(st.pop((w, "z")), valid)
                st[w, "parts"] = _bf16_parts(lk)
                lsz = lsz if valid is None else jnp.where(valid, lsz, NEG_BIG)
                lsz_ref[slot, h, rs, :] = lsz
                st[w, "lszc"] = lsz + jnp.sum(jnp.where(lane == j, car_ref[h, rs, :], 0.0), axis=-1, keepdims=True)

            def s_suffix(w):
                logw_ref[slot, w[0], w[1]:w[1] + tr, :] = st.pop((w, "lszc")) + _dot(st.pop((w, "parts")), tri_right2)

            return _sb_items(edge), [s_logits, s_terms, s_suffix]

        def back(j, edge):
            kv = k_ref[pl.ds(pl.multiple_of(j * tk, tk), tk), :]
            slot = j % 2
            st = {}

            items = _sb_items(edge)
            head_rows = [[r0 for hh, r0, _ in items if hh == h] for h in range(2)]

            def s_weights(w):
                h, rs = w[0], slice(w[1], w[1] + tr)
                a = jnp.exp(logw_ref[slot, h, rs, :])
                g = a * da_ref[slot, h, rs, :]
                a_ref[h, rs, :] = a.astype(BF16)
                st[w, "g"], st[w, "parts"] = g, _bf16_parts(g)

            def s_prefix(w):
                st[w, "gs"] = _dot(st.pop((w, "parts")), tri_left2)

            def s_dz(w):
                h, rs = w[0], slice(w[1], w[1] + tr)
                g = st.pop((w, "g"))
                gleft = gleft_ref[h, rs, :]
                gsum = st.pop((w, "gs")) + jnp.tile(gleft, (1, tk // LANES))
                dz = (g - jnp.exp(lsz_ref[slot, h, rs, :]) * (g + gsum)).astype(BF16)
                st[w, "dz"] = dz
                dz_ref[h, rs, :] = dz
                gleft_ref[h, rs, :] = gleft + _row_sum_lanes(g)

            def s_products(w):
                h, rs = w[0], slice(w[1], w[1] + tr)
                dqacc_ref[h, rs, :] += _dot(st.pop((w, "dz")), kv)
                if w[1] == head_rows[h][-1]:
                    feat = slice(h * HEAD_DIM, (h + 1) * HEAD_DIM)
                    hr = slice(head_rows[h][0], tq)
                    dkacc_ref[j, feat, :] += _dot(q_t[feat, hr], dz_ref[h, hr, :])
                    dvacc_ref[j, feat, :] += _dot(do_t[feat, hr], a_ref[h, hr, :])

            return items, [s_weights, s_prefix, s_dz, s_products]

        first = i * ratio

        @pl.when(i == 0)
        def _():
            _emit_skewed(front(0, 0))

        @pl.when(i > 0)
        def _():
            _emit_skewed(front(0, None))

            def step(jj, carry):
                _emit_skewed(front(jj, None), back(jj - 1, None))
                return carry

            lax.fori_loop(1, first, step, 0)
            _emit_skewed(front(first, 0), back(first - 1, None))

        for m in range(1, ratio):
            _emit_skewed(front(first + m, m), back(first + m - 1, m - 1))
        _emit_skewed(back(first + ratio - 1, ratio - 1))
        dq_ref[...] = (Q_SCALE * jnp.where(low, dqacc_ref[0], dqacc_ref[1])).astype(BF16)

        @pl.when(i == s_len // tq - 1)
        def _():
            dk_ref[0] = dkacc_ref[...].astype(BF16)
            dv_ref[0] = dvacc_ref[...].astype(BF16)

    qblk = pl.BlockSpec((tq, LANES), lambda p, i: (i, p))
    qtblk = pl.BlockSpec((1, 1, LANES, tq), lambda p, i: (p, i, 0, 0))
    tblk = pl.BlockSpec((1, nk, LANES, tk), lambda p, i: (p, 0, 0, 0))
    col_full = pl.BlockSpec((s_len, LANES), lambda p, i: (0, p))
    tshape = jax.ShapeDtypeStruct((N_HEADS // 2, nk, LANES, tk), BF16)
    return pl.pallas_call(
        body, name="sb_bwd",
        grid=(N_HEADS // 2, s_len // tq),
        in_specs=[qblk, qtblk, tblk, col_full, tblk, qblk, qtblk, pl.BlockSpec((2, tq, LANES), lambda p, i: (p, i, 0))],
        out_specs=[qblk, tblk, tblk],
        out_shape=[jax.ShapeDtypeStruct((s_len, N_HEADS * HEAD_DIM), BF16), tshape, tshape],
        scratch_shapes=[pltpu.VMEM((2, tq, LANES), F32), pltpu.VMEM((2, tq, LANES), F32),
                        pltpu.VMEM((nk, LANES, tk), F32), pltpu.VMEM((nk, LANES, tk), F32)]
        + [pltpu.VMEM((2, 2, tq, tk), F32)] * 3 + [pltpu.VMEM((2, tq, tk), BF16)] * 2,
        compiler_params=_cparams("parallel", "arbitrary"),
    )(q, qt, kt, k, vt, do, dot, cars)


def _pad_heads(a, heads):
    s_len = a.shape[0]
    a = a.reshape(s_len, heads, HEAD_DIM)
    return jnp.pad(a, ((0, 0), (0, 0), (0, LANES - HEAD_DIM))).reshape(s_len, heads * LANES)


def _unpad_heads(a, heads):
    s_len = a.shape[0]
    return a.reshape(s_len, heads, LANES)[:, :, :HEAD_DIM].reshape(s_len, heads * HEAD_DIM)


def _tile_transposed(a, groups, t):
    s_len = a.shape[0]
    return a.reshape(s_len // t, t, groups, LANES).transpose(2, 0, 3, 1)


def _tile_untransposed(a):
    groups, nt, _, t = a.shape
    return a.transpose(1, 3, 0, 2).reshape(nt * t, groups * LANES)


def _local_step(xs, tgt, gains, sinks, rel_bias, wts):
    g1, gmix, g2, gfin = gains
    bkt = _rel_bucket_matrix()
    groups = N_HEADS // 2

    x1, h1, a1, b1, u1 = _ffn_fwd(xs, g1, wts["ffn1_w1t"], wts["ffn1_w3t"], wts["ffn1_w2"], "1")
    hm, qa, ka, va, qb, kb, vb, ga, gb = _proj_fwd(x1, gmix, wts["w_int"])
    qa_p, ka_p, va_p = _pad_heads(qa, N_HEADS), _pad_heads(ka, SWA_KV_HEADS), _pad_heads(va, SWA_KV_HEADS)
    oa_p = _swa_fwd(rel_bias, sinks, bkt, qa_p, ka_p, va_p)
    kbt = _tile_transposed(kb, groups, SB_KEYS)
    ob, cars = _sb_fwd(qb, kbt, vb)
    oa = _unpad_heads(oa_p, N_HEADS)
    x2, mg = _merge_fwd(x1, oa, ob, ga, gb, wts["w_swa"], wts["w_sb"], wts["w_out"])
    x3, h3, a3, b3, u3 = _ffn_fwd(x2, g2, wts["ffn2_w1t"], wts["ffn2_w3t"], wts["ffn2_w2"], "2")
    dx3, loss, dgfin = _loss_fwd_bwd(x3, tgt, gfin)

    big = {}
    dx2, dg2, da3, db3, dx3b = _ffn_bwd(dx3, x2, g2, a3, b3, wts["ffn2_w1t"], wts["ffn2_w3t"], wts["ffn2_w2"], "2")
    big["ffn2_w1t"] = _matmul_tn(da3, h3, "ffn2_w1")
    big["ffn2_w3t"] = _matmul_tn(db3, h3, "ffn2_w3")
    big["ffn2_w2"] = _matmul_tn(u3, dx3b, "ffn2_w2")

    doa, dob, dga, dgb, dpa, dpb, dx2b = _merge_bwd(dx2, oa, ob, ga, gb, wts["w_swa"], wts["w_sb"], wts["w_out"])
    big["w_out"] = _matmul_tn(mg, dx2b, "w_out")
    big["w_swa"] = _matmul_tn(oa, dpa, "w_swa")
    big["w_sb"] = _matmul_tn(ob, dpb, "w_sb")

    dqa_p, dka_p, dva_p, dtab, dsink = _swa_bwd(rel_bias, sinks, bkt, qa_p, ka_p, va_p, _pad_heads(doa, N_HEADS))
    dqb, dkbt, dvbt = _sb_bwd(qb, _tile_transposed(qb, groups, SB_QUERIES), kbt, kb,
                              _tile_transposed(vb, groups, SB_KEYS), dob, _tile_transposed(dob, groups, SB_QUERIES), cars)
    dkb, dvb = _tile_untransposed(dkbt), _tile_untransposed(dvbt)
    dpieces = (_unpad_heads(dqa_p, N_HEADS), _unpad_heads(dka_p, SWA_KV_HEADS).astype(BF16),
               _unpad_heads(dva_p, SWA_KV_HEADS).astype(BF16), dqb, dkb, dvb, dga, dgb)
    big["w_int"] = jnp.concatenate([_matmul_tn(dp, hm, f"w_in{p}") for p, dp in enumerate(dpieces)], axis=0)
    dx1, dgmix = _proj_bwd(dpieces, dx2, x1, gmix, wts["w_int"])

    dx0, dg1, da1, db1, dx1b = _ffn_bwd(dx1, xs, g1, a1, b1, wts["ffn1_w1t"], wts["ffn1_w3t"], wts["ffn1_w2"], "1")
    big["ffn1_w1t"] = _matmul_tn(da1, h1, "ffn1_w1")
    big["ffn1_w3t"] = _matmul_tn(db1, h1, "ffn1_w3")
    big["ffn1_w2"] = _matmul_tn(u1, dx1b, "ffn1_w2")

    small = {"gains": (dg1, dgmix, dg2, dgfin), "sinks": dsink[:, 0], "rel_bias": dtab[:, :N_HEADS]}
    return loss, dx0, big, small


def _my_place():
    return lax.axis_index("x"), lax.axis_index("y"), lax.axis_index("c")


def _flip(v, bit):
    return 1 - v if bit else v


_RELATIONS = tuple((k >> 2 & 1, k >> 1 & 1, k & 1) for k in range(1, N_DEV))


def _gather_weights(wp):
    def body(x_ref, out_ref, send_sems, recv_sems, local_sem):
        x, y, c = _my_place()
        me, sibling = (x, y, c), (x, y, 1 - c)
        chips = [(1 - x, y), (x, 1 - y), (1 - x, 1 - y)]

        def rows(px, py, pc):
            return out_ref.at[4 * px + 2 * py + pc]

        def copy(k, block, to, src=None):
            return pltpu.make_async_remote_copy(
                src_ref=rows(*block) if src is None else src, dst_ref=rows(*block),
                send_sem=send_sems.at[k], recv_sem=recv_sems.at[k],
                device_id=to, device_id_type=pl.DeviceIdType.MESH)

        mine = pltpu.make_async_copy(x_ref, rows(*me), local_sem)
        mine.start()
        first = [copy(0, me, sibling, src=x_ref)]
        first += [copy(1 + j, me, (*chip, c), src=x_ref) for j, chip in enumerate(chips)]
        for cp in first:
            cp.start()
        passed = [copy(4 + j, (*chip, c), sibling) for j, chip in enumerate(chips)]
        for j, chip in enumerate(chips):
            copy(1 + j, (*chip, c), me).wait_recv()
            passed[j].start()
        copy(0, sibling, me).wait_recv()
        for j, chip in enumerate(chips):
            copy(4 + j, (*chip, 1 - c), me).wait_recv()
        for cp in first + passed:
            cp.wait_send()
        mine.wait()

    return pl.pallas_call(
        body, name="gather_weights",
        out_shape=jax.ShapeDtypeStruct((N_DEV,) + wp.shape, wp.dtype),
        in_specs=[pl.BlockSpec(memory_space=pl.ANY)],
        out_specs=pl.BlockSpec(memory_space=pl.ANY),
        scratch_shapes=[pltpu.SemaphoreType.DMA((7,)), pltpu.SemaphoreType.DMA((7,)), pltpu.SemaphoreType.DMA(())],
    )(wp)


def _exchange_grads(gp):
    def body(g_ref, out_ref, send_sems, recv_sems, local_sem):
        x, y, c = _my_place()
        me = 4 * x + 2 * y + c
        mine = pltpu.make_async_copy(g_ref.at[me], out_ref.at[me], local_sem)
        mine.start()
        copies = []
        for k, (fx, fy, fc) in enumerate(_RELATIONS):
            px, py, pc = _flip(x, fx), _flip(y, fy), _flip(c, fc)
            peer = 4 * px + 2 * py + pc
            copies.append((
                pltpu.make_async_remote_copy(
                    src_ref=g_ref.at[peer], dst_ref=out_ref.at[me], send_sem=send_sems.at[k], recv_sem=recv_sems.at[k],
                    device_id=(px, py, pc), device_id_type=pl.DeviceIdType.MESH),
                pltpu.make_async_remote_copy(
                    src_ref=g_ref.at[peer], dst_ref=out_ref.at[peer], send_sem=send_sems.at[k], recv_sem=recv_sems.at[k],
                    device_id=(px, py, pc), device_id_type=pl.DeviceIdType.MESH)))
        for out_cp, _ in copies:
            out_cp.start()
        for _, in_cp in copies:
            in_cp.wait_recv()
        for out_cp, _ in copies:
            out_cp.wait_send()
        mine.wait()

    return pl.pallas_call(
        body, name="exchange_grads",
        out_shape=jax.ShapeDtypeStruct(gp.shape, gp.dtype),
        in_specs=[pl.BlockSpec(memory_space=pl.ANY)],
        out_specs=pl.BlockSpec(memory_space=pl.ANY),
        scratch_shapes=[pltpu.SemaphoreType.DMA((7,)), pltpu.SemaphoreType.DMA((7,)), pltpu.SemaphoreType.DMA(())],
    )(gp)


def _adamw(w, g, m, v):
    m = ADAM_B1 * m + (1.0 - ADAM_B1) * g
    v = ADAM_B2 * v + (1.0 - ADAM_B2) * jnp.square(g)
    m_hat = m / (1.0 - ADAM_B1 ** ADAM_STEP)
    v_hat = v / (1.0 - ADAM_B2 ** ADAM_STEP)
    delta = -ADAM_LR * (m_hat / (jnp.sqrt(v_hat) + ADAM_EPS) + ADAM_WD * w)
    return delta, m, v


def _sum_and_adamw(parts, w, m, v):
    rows = w.shape[0]
    tr = 112
    assert rows % tr == 0

    def body(p_ref, w_ref, m_ref, v_ref, g_out, d_out, m_out, v_out):
        g = p_ref[0].astype(F32)
        for d in range(1, N_DEV):
            g = g + p_ref[d].astype(F32)
        delta, mn, vn = _adamw(w_ref[...], g, m_ref[...], v_ref[...])
        g_out[...] = g
        d_out[...] = delta
        m_out[...] = mn
        v_out[...] = vn

    sp = pl.BlockSpec((tr, D_MODEL), lambda i: (i, 0))
    return pl.pallas_call(
        body, name="sum_and_adamw",
        grid=(rows // tr,),
        in_specs=[pl.BlockSpec((N_DEV, tr, D_MODEL), lambda i: (0, i, 0)), sp, sp, sp],
        out_specs=[sp] * 4,
        out_shape=[jax.ShapeDtypeStruct(w.shape, F32)] * 4,
        compiler_params=_cparams("parallel"),
    )(parts, w, m, v)


def _small_allreduce_adamw(part, w, m, v):
    def body(p_ref, w_ref, m_ref, v_ref, g_out, d_out, m_out, v_out, buf, send_sems, recv_sems):
        x, y, c = _my_place()
        me = 4 * x + 2 * y + c
        buf[me] = p_ref[...]
        copies = []
        for k, (fx, fy, fc) in enumerate(_RELATIONS):
            px, py, pc = _flip(x, fx), _flip(y, fy), _flip(c, fc)
            peer = 4 * px + 2 * py + pc
            copies.append((
                pltpu.make_async_remote_copy(
                    src_ref=buf.at[me], dst_ref=buf.at[me], send_sem=send_sems.at[k], recv_sem=recv_sems.at[k],
                    device_id=(px, py, pc), device_id_type=pl.DeviceIdType.MESH),
                pltpu.make_async_remote_copy(
                    src_ref=buf.at[me], dst_ref=buf.at[peer], send_sem=send_sems.at[k], recv_sem=recv_sems.at[k],
                    device_id=(px, py, pc), device_id_type=pl.DeviceIdType.MESH)))
        for out_cp, _ in copies:
            out_cp.start()
        for _, in_cp in copies:
            in_cp.wait_recv()
        for out_cp, _ in copies:
            out_cp.wait_send()
        g = buf[0]
        for d in range(1, N_DEV):
            g = g + buf[d]
        delta, mn, vn = _adamw(w_ref[...], g, m_ref[...], v_ref[...])
        g_out[...] = g
        d_out[...] = delta
        m_out[...] = mn
        v_out[...] = vn

    vm = pl.BlockSpec(memory_space=pltpu.VMEM)
    return pl.pallas_call(
        body, name="small_allreduce_adamw",
        in_specs=[vm] * 4, out_specs=[vm] * 4,
        out_shape=[jax.ShapeDtypeStruct(w.shape, F32)] * 4,
        scratch_shapes=[pltpu.VMEM((N_DEV,) + part.shape, F32),
                        pltpu.SemaphoreType.DMA((7,)), pltpu.SemaphoreType.DMA((7,))],
    )(part, w, m, v)


_TRANSPOSED = ("ffn1_w1", "ffn1_w3", "w_in", "ffn2_w1", "ffn2_w3")
_BRANCH = ("w_branch_swa", "w_branch_sb")


def _pack_shards(t):
    parts = []
    for name in BIG_NAMES:
        a = t[name][0]
        if name in _TRANSPOSED:
            a = a.T
        elif name in _BRANCH:
            a = a.reshape(64, D_MODEL)
        parts.append(a)
    return jnp.concatenate(parts, axis=0)


def _unpack_shards(p):
    out = {}
    for name, lo, hi in zip(BIG_NAMES, BIG_OFFS[:-1], BIG_OFFS[1:]):
        a = p[lo:hi]
        if name in _TRANSPOSED:
            a = a.T
        elif name in _BRANCH:
            a = a.reshape(512, 128)
        out[name] = a[None]
    return out


def _full_weights(wg):
    def part(name):
        k = BIG_NAMES.index(name)
        return wg[:, BIG_OFFS[k]:BIG_OFFS[k + 1]]

    def branch(name):
        return part(name).reshape(N_DEV, 512, 128).transpose(1, 0, 2).reshape(512, D_MODEL)

    return {
        "ffn1_w1t": part("ffn1_w1").reshape(D_FF, D_MODEL), "ffn1_w3t": part("ffn1_w3").reshape(D_FF, D_MODEL),
        "ffn1_w2": part("ffn1_w2").reshape(D_FF, D_MODEL), "w_int": part("w_in").reshape(IN_W, D_MODEL),
        "w_swa": branch("w_branch_swa"), "w_sb": branch("w_branch_sb"),
        "w_out": part("w_out").reshape(D_MODEL, D_MODEL),
        "ffn2_w1t": part("ffn2_w1").reshape(D_FF, D_MODEL), "ffn2_w3t": part("ffn2_w3").reshape(D_FF, D_MODEL),
        "ffn2_w2": part("ffn2_w2").reshape(D_FF, D_MODEL),
    }


def _pack_full_grads(big):
    def branch(a):
        return a.reshape(512, N_DEV, 128).transpose(1, 0, 2).reshape(N_DEV, 64, D_MODEL)

    parts = [big["ffn1_w1t"].reshape(N_DEV, 352, D_MODEL), big["ffn1_w3t"].reshape(N_DEV, 352, D_MODEL),
             big["ffn1_w2"].reshape(N_DEV, 352, D_MODEL), big["w_int"].reshape(N_DEV, 544, D_MODEL),
             branch(big["w_swa"]), branch(big["w_sb"]), big["w_out"].reshape(N_DEV, 128, D_MODEL),
             big["ffn2_w1t"].reshape(N_DEV, 352, D_MODEL), big["ffn2_w3t"].reshape(N_DEV, 352, D_MODEL),
             big["ffn2_w2"].reshape(N_DEV, 352, D_MODEL)]
    return jnp.concatenate([p.astype(BF16) for p in parts], axis=1)


_SMALL_NAMES = ("norm_ffn1", "norm_mix", "norm_ffn2", "norm_final", "swa_sinks", "rel_bias")


def _pack_small(vals):
    rows = []
    for a in vals:
        a = a.reshape(-1)
        rows.append(jnp.pad(a, (0, D_MODEL - a.shape[0])))
    rows += [jnp.zeros((D_MODEL,), F32)] * (SMALL_ROWS - len(rows))
    return jnp.stack(rows)


def _unpack_small(p):
    return {"norm_ffn1": p[0:1], "norm_mix": p[1:2], "norm_ffn2": p[2:3], "norm_final": p[3],
            "swa_sinks": p[4:5, :N_HEADS], "rel_bias": p[5, :REL_BUCKETS * N_HEADS].reshape(REL_BUCKETS, N_HEADS)}


ALL_NAMES = ("norm_ffn1", "ffn1_w1", "ffn1_w3", "ffn1_w2", "norm_mix", "w_in", "swa_sinks", "rel_bias",
             "w_branch_swa", "w_branch_sb", "w_out", "norm_ffn2", "ffn2_w1", "ffn2_w3", "ffn2_w2", "norm_final")


def kernel(x, norm_ffn1, ffn1_w1, ffn1_w3, ffn1_w2, norm_mix, w_in, swa_sinks, rel_bias, w_branch_swa, w_branch_sb, w_out, norm_ffn2, ffn2_w1, ffn2_w3, ffn2_w2, norm_final, loss_target, m_norm_ffn1, m_ffn1_w1, m_ffn1_w3, m_ffn1_w2, m_norm_mix, m_w_in, m_swa_sinks, m_rel_bias, m_w_branch_swa, m_w_branch_sb, m_w_out, m_norm_ffn2, m_ffn2_w1, m_ffn2_w3, m_ffn2_w2, m_norm_final, v_norm_ffn1, v_ffn1_w1, v_ffn1_w3, v_ffn1_w2, v_norm_mix, v_w_in, v_swa_sinks, v_rel_bias, v_w_branch_swa, v_w_branch_sb, v_w_out, v_norm_ffn2, v_ffn2_w1, v_ffn2_w3, v_ffn2_w2, v_norm_final):
    w = dict(zip(ALL_NAMES, (norm_ffn1, ffn1_w1, ffn1_w3, ffn1_w2, norm_mix, w_in, swa_sinks, rel_bias,
                             w_branch_swa, w_branch_sb, w_out, norm_ffn2, ffn2_w1, ffn2_w3, ffn2_w2, norm_final)))
    m = dict(zip(ALL_NAMES, (m_norm_ffn1, m_ffn1_w1, m_ffn1_w3, m_ffn1_w2, m_norm_mix, m_w_in, m_swa_sinks, m_rel_bias,
                             m_w_branch_swa, m_w_branch_sb, m_w_out, m_norm_ffn2, m_ffn2_w1, m_ffn2_w3, m_ffn2_w2,
                             m_norm_final)))
    v = dict(zip(ALL_NAMES, (v_norm_ffn1, v_ffn1_w1, v_ffn1_w3, v_ffn1_w2, v_norm_mix, v_w_in, v_swa_sinks, v_rel_bias,
                             v_w_branch_swa, v_w_branch_sb, v_w_out, v_norm_ffn2, v_ffn2_w1, v_ffn2_w3, v_ffn2_w2,
                             v_norm_final)))

    w_packed = _pack_shards(w)
    wts = _full_weights(_gather_weights(w_packed.astype(BF16)))
    gains = (norm_ffn1, norm_mix, norm_ffn2, norm_final.reshape(1, D_MODEL))
    loss, dx, big, small = _local_step(x[0], loss_target[0], gains, swa_sinks, rel_bias, wts)

    parts = _exchange_grads(_pack_full_grads(big))
    g_big, d_big, m_big, v_big = (_unpack_shards(p) for p in
                                  _sum_and_adamw(parts, w_packed, _pack_shards(m), _pack_shards(v)))

    small_part = _pack_small(small["gains"] + (small["sinks"], small["rel_bias"]))
    g_sm, d_sm, m_sm, v_sm = (_unpack_small(p) for p in _small_allreduce_adamw(
        small_part, _pack_small([w[n] for n in _SMALL_NAMES]), _pack_small([m[n] for n in _SMALL_NAMES]),
        _pack_small([v[n] for n in _SMALL_NAMES])))

    total_loss = lax.psum(loss[0, 0], AXES)
    outs = [total_loss, dx[None]]
    for big_d, small_d in ((g_big, g_sm), (d_big, d_sm), (m_big, m_sm), (v_big, v_sm)):
        merged = {**big_d, **small_d}
        outs += [merged[n] for n in ALL_NAMES]
    return tuple(outs)
```

```python
import jax
import jax.numpy as jnp
import numpy as np
from jax import lax
from jax.experimental import pallas as pl
from jax.experimental.pallas import tpu as pltpu

F32 = jnp.float32
BF16 = jnp.bfloat16

D_MODEL = 1024
D_FF = 2816
HEAD_DIM = 64
N_HEADS = 8
SWA_KV_HEADS = 2
SWA_GROUP = 4
SWA_BLOCK = 128
REL_BUCKETS = 32
REL_MAX_DIST = 128
RMS_EPS = 1e-6
NEG_BIG = -1e30
Q_SCALE = HEAD_DIM ** -0.5
LANES = 128

N_DEV = 8
AXES = ("x", "y", "c")

ADAM_LR = 0.001
ADAM_B1 = 0.9
ADAM_B2 = 0.999
ADAM_EPS = 1e-08
ADAM_WD = 0.01
ADAM_STEP = 10

IN_SIZES = (512, 128, 128, 512, 512, 512, 1024, 1024)
IN_OFFS = tuple(int(v) for v in np.cumsum((0,) + IN_SIZES))
IN_W = IN_OFFS[-1]

BIG_NAMES = ("ffn1_w1", "ffn1_w3", "ffn1_w2", "w_in", "w_branch_swa", "w_branch_sb", "w_out",
             "ffn2_w1", "ffn2_w3", "ffn2_w2")
BIG_ROWS = (352, 352, 352, 544, 64, 64, 128, 352, 352, 352)
BIG_OFFS = tuple(int(v) for v in np.cumsum((0,) + BIG_ROWS))
PACK_ROWS = BIG_OFFS[-1]
SMALL_ROWS = 8
GROUPS = (BIG_NAMES[0:3], BIG_NAMES[3:7], BIG_NAMES[7:10])
GROUP_TILE = (96, 160, 96)

VMEM_LIMIT = 56 * 1024 * 1024
SB_QUERIES = 512
SB_KEYS = 256
SB_ROWS = 256
SB_SUM_PARTS = 1
SB_LOGIT_CAP = 80.0


def _dot(a, b):
    return jnp.dot(a, b, preferred_element_type=F32)


def _dot_nt(a, b):
    return lax.dot_general(a, b, (((1,), (1,)), ((), ())), preferred_element_type=F32)


def _dot_tn(a, b):
    return lax.dot_general(a, b, (((0,), (0,)), ((), ())), preferred_element_type=F32)


def _cparams(*sem):
    return pltpu.CompilerParams(dimension_semantics=sem, vmem_limit_bytes=VMEM_LIMIT)


def _rms_rstd(xv):
    return lax.rsqrt(jnp.mean(xv * xv, axis=-1, keepdims=True) + RMS_EPS)


def _rms_bwd(dh, xv, r, g):
    xhat = xv * r
    dg = jnp.sum(dh * xhat, axis=0, keepdims=True)
    dxn = dh * g
    dx = r * (dxn - xhat * jnp.mean(dxn * xhat, axis=-1, keepdims=True))
    return dx, dg


def _ffn_fwd(x, g, w1t, w3t, w2, tag):
    s_len = x.shape[0]
    tm, tf = min(1024, s_len), 256
    nf = D_FF // tf

    def body(x_ref, g_ref, w1_ref, w3_ref, w2_ref, xo_ref, h_ref, a_ref, b_ref, u_ref, acc_ref, hs_ref):
        j = pl.program_id(1)

        @pl.when(j == 0)
        def _():
            xv = x_ref[...]
            h = (xv * _rms_rstd(xv) * g_ref[...]).astype(BF16)
            hs_ref[...] = h
            h_ref[...] = h
            acc_ref[...] = jnp.zeros_like(acc_ref)

        h = hs_ref[...]
        a = _dot_nt(h, w1_ref[...])
        b = _dot_nt(h, w3_ref[...])
        a_ref[...] = a.astype(BF16)
        b_ref[...] = b.astype(BF16)
        uh = (0.5 * (a * jax.nn.sigmoid(a) * b)).astype(BF16)
        u_ref[...] = uh
        acc_ref[...] += _dot(uh, w2_ref[...])

        @pl.when(j == nf - 1)
        def _():
            xo_ref[...] = x_ref[...] + acc_ref[...]

    row = lambda i, j: (i, 0)
    return pl.pallas_call(
        body, name=f"ffn_fwd_{tag}",
        grid=(s_len // tm, nf),
        in_specs=[pl.BlockSpec((tm, D_MODEL), row), pl.BlockSpec((1, D_MODEL), lambda i, j: (0, 0)),
                  pl.BlockSpec((tf, D_MODEL), lambda i, j: (j, 0)), pl.BlockSpec((tf, D_MODEL), lambda i, j: (j, 0)),
                  pl.BlockSpec((tf, D_MODEL), lambda i, j: (j, 0))],
        out_specs=[pl.BlockSpec((tm, D_MODEL), row), pl.BlockSpec((tm, D_MODEL), row),
                   pl.BlockSpec((tm, tf), lambda i, j: (i, j)), pl.BlockSpec((tm, tf), lambda i, j: (i, j)),
                   pl.BlockSpec((tm, tf), lambda i, j: (i, j))],
        out_shape=[jax.ShapeDtypeStruct((s_len, D_MODEL), F32), jax.ShapeDtypeStruct((s_len, D_MODEL), BF16),
                   jax.ShapeDtypeStruct((s_len, D_FF), BF16), jax.ShapeDtypeStruct((s_len, D_FF), BF16),
                   jax.ShapeDtypeStruct((s_len, D_FF), BF16)],
        scratch_shapes=[pltpu.VMEM((tm, D_MODEL), F32), pltpu.VMEM((tm, D_MODEL), BF16)],
        compiler_params=_cparams("parallel", "arbitrary"),
    )(x, g, w1t, w3t, w2)


def _ffn_bwd(dy, x, g, a, b, w1t, w3t, w2, tag):
    s_len = x.shape[0]
    tm, tf = min(1024, s_len), 256
    nf = D_FF // tf

    def body(dy_ref, x_ref, g_ref, a_ref, b_ref, w1_ref, w3_ref, w2_ref,
             dx_ref, dg_ref, da_ref, db_ref, dyb_ref, acc_ref, dys_ref):
        i, j = pl.program_id(0), pl.program_id(1)

        @pl.when(j == 0)
        def _():
            dyb = dy_ref[...].astype(BF16)
            dys_ref[...] = dyb
            dyb_ref[...] = dyb
            acc_ref[...] = jnp.zeros_like(acc_ref)

        @pl.when((i == 0) & (j == 0))
        def _():
            dg_ref[...] = jnp.zeros_like(dg_ref)

        du = 0.5 * _dot_nt(dys_ref[...], w2_ref[...])
        av = a_ref[...].astype(F32)
        bv = b_ref[...].astype(F32)
        sg = jax.nn.sigmoid(av)
        sil = av * sg
        da = (du * bv * (sg + sil * (1.0 - sg))).astype(BF16)
        db = (du * sil).astype(BF16)
        da_ref[...] = da
        db_ref[...] = db
        acc_ref[...] += _dot(da, w1_ref[...]) + _dot(db, w3_ref[...])

        @pl.when(j == nf - 1)
        def _():
            xv = x_ref[...]
            dx, dg = _rms_bwd(acc_ref[...], xv, _rms_rstd(xv), g_ref[...])
            dx_ref[...] = dy_ref[...] + dx
            dg_ref[...] += dg

    row = lambda i, j: (i, 0)
    blk = lambda i, j: (i, j)
    wsp = pl.BlockSpec((tf, D_MODEL), lambda i, j: (j, 0))
    return pl.pallas_call(
        body, name=f"ffn_bwd_{tag}",
        grid=(s_len // tm, nf),
        in_specs=[pl.BlockSpec((tm, D_MODEL), row), pl.BlockSpec((tm, D_MODEL), row),
                  pl.BlockSpec((1, D_MODEL), lambda i, j: (0, 0)),
                  pl.BlockSpec((tm, tf), blk), pl.BlockSpec((tm, tf), blk), wsp, wsp, wsp],
        out_specs=[pl.BlockSpec((tm, D_MODEL), row), pl.BlockSpec((1, D_MODEL), lambda i, j: (0, 0)),
                   pl.BlockSpec((tm, tf), blk), pl.BlockSpec((tm, tf), blk), pl.BlockSpec((tm, D_MODEL), row)],
        out_shape=[jax.ShapeDtypeStruct((s_len, D_MODEL), F32), jax.ShapeDtypeStruct((1, D_MODEL), F32),
                   jax.ShapeDtypeStruct((s_len, D_FF), BF16), jax.ShapeDtypeStruct((s_len, D_FF), BF16),
                   jax.ShapeDtypeStruct((s_len, D_MODEL), BF16)],
        scratch_shapes=[pltpu.VMEM((tm, D_MODEL), F32), pltpu.VMEM((tm, D_MODEL), BF16)],
        compiler_params=_cparams("arbitrary", "arbitrary"),
    )(dy, x, g, a, b, w1t, w3t, w2)


def _matmul_tn(lhs, rhs, tag):
    s_len, m = lhs.shape
    n = rhs.shape[1]
    tm = min(512, s_len)
    tj = m if m <= 1024 else 1408
    assert m % tj == 0

    def body(l_ref, r_ref, o_ref):
        @pl.when(pl.program_id(1) == 0)
        def _():
            o_ref[...] = jnp.zeros_like(o_ref)

        o_ref[...] += _dot_tn(l_ref[...], r_ref[...])

    return pl.pallas_call(
        body, name=f"matmul_tn_{tag}",
        grid=(m // tj, s_len // tm),
        in_specs=[pl.BlockSpec((tm, tj), lambda j, i: (i, j)), pl.BlockSpec((tm, n), lambda j, i: (i, 0))],
        out_specs=pl.BlockSpec((tj, n), lambda j, i: (j, 0)),
        out_shape=jax.ShapeDtypeStruct((m, n), F32),
        compiler_params=_cparams("parallel", "arbitrary"),
    )(lhs, rhs)


def _proj_fwd(x1, g, wint):
    s_len = x1.shape[0]
    tm = min(512, s_len)
    dts = (BF16, BF16, BF16, BF16, BF16, BF16, F32, F32)

    def body(x_ref, g_ref, w_ref, h_ref, *outs):
        xv = x_ref[...]
        h = (xv * _rms_rstd(xv) * g_ref[...]).astype(BF16)
        h_ref[...] = h
        for p, o_ref in enumerate(outs):
            val = _dot_nt(h, w_ref[IN_OFFS[p]:IN_OFFS[p + 1], :])
            if p == 3:
                val = val * Q_SCALE
            o_ref[...] = val.astype(dts[p])

    row = lambda i: (i, 0)
    return pl.pallas_call(
        body, name="proj_fwd",
        grid=(s_len // tm,),
        in_specs=[pl.BlockSpec((tm, D_MODEL), row), pl.BlockSpec((1, D_MODEL), lambda i: (0, 0)),
                  pl.BlockSpec((IN_W, D_MODEL), lambda i: (0, 0))],
        out_specs=[pl.BlockSpec((tm, D_MODEL), row)] + [pl.BlockSpec((tm, w), row) for w in IN_SIZES],
        out_shape=[jax.ShapeDtypeStruct((s_len, D_MODEL), BF16)]
        + [jax.ShapeDtypeStruct((s_len, w), dt) for w, dt in zip(IN_SIZES, dts)],
        compiler_params=_cparams("parallel"),
    )(x1, g, wint)


def _proj_bwd(dpieces, dx2, x1, g, wint, dep):
    s_len = x1.shape[0]
    tm = min(512, s_len)

    def body(*refs):
        dps = refs[:8]
        dx2_ref, x_ref, g_ref, w_ref, _, dx_ref, dg_ref = refs[8:]

        @pl.when(pl.program_id(0) == 0)
        def _():
            dg_ref[...] = jnp.zeros_like(dg_ref)

        dh = _dot(dps[0][...], w_ref[IN_OFFS[0]:IN_OFFS[1], :])
        for p in range(1, 8):
            dh += _dot(dps[p][...], w_ref[IN_OFFS[p]:IN_OFFS[p + 1], :])
        xv = x_ref[...]
        dx, dg = _rms_bwd(dh, xv, _rms_rstd(xv), g_ref[...])
        dx_ref[...] = dx2_ref[...] + dx
        dg_ref[...] += dg

    row = lambda i: (i, 0)
    return pl.pallas_call(
        body, name="proj_bwd",
        grid=(s_len // tm,),
        in_specs=[pl.BlockSpec((tm, w), row) for w in IN_SIZES]
        + [pl.BlockSpec((tm, D_MODEL), row), pl.BlockSpec((tm, D_MODEL), row),
           pl.BlockSpec((1, D_MODEL), lambda i: (0, 0)), pl.BlockSpec((IN_W, D_MODEL), lambda i: (0, 0)),
           pl.BlockSpec((8, LANES), lambda i: (0, 0))],
        out_specs=[pl.BlockSpec((tm, D_MODEL), row), pl.BlockSpec((1, D_MODEL), lambda i: (0, 0))],
        out_shape=[jax.ShapeDtypeStruct((s_len, D_MODEL), F32), jax.ShapeDtypeStruct((1, D_MODEL), F32)],
        compiler_params=_cparams("arbitrary"),
    )(*dpieces, dx2, x1, g, wint, dep)


def _merge_fwd(x1, oa, ob, ga, gb, wswa, wsb, wout):
    s_len = x1.shape[0]
    tm = min(512, s_len)

    def body(x_ref, oa_ref, ob_ref, ga_ref, gb_ref, wa_ref, wb_ref, wo_ref, xo_ref, mg_ref):
        pa = _dot(oa_ref[...], wa_ref[...])
        pb = _dot(ob_ref[...], wb_ref[...])
        mg = (jax.nn.sigmoid(ga_ref[...]) * pa + jax.nn.sigmoid(gb_ref[...]) * pb).astype(BF16)
        mg_ref[...] = mg
        xo_ref[...] = x_ref[...] + _dot(mg, wo_ref[...])

    row = lambda i: (i, 0)
    full = lambda i: (0, 0)
    return pl.pallas_call(
        body, name="merge_fwd",
        grid=(s_len // tm,),
        in_specs=[pl.BlockSpec((tm, D_MODEL), row), pl.BlockSpec((tm, 512), row), pl.BlockSpec((tm, 512), row),
                  pl.BlockSpec((tm, D_MODEL), row), pl.BlockSpec((tm, D_MODEL), row),
                  pl.BlockSpec((512, D_MODEL), full), pl.BlockSpec((512, D_MODEL), full),
                  pl.BlockSpec((D_MODEL, D_MODEL), full)],
        out_specs=[pl.BlockSpec((tm, D_MODEL), row), pl.BlockSpec((tm, D_MODEL), row)],
        out_shape=[jax.ShapeDtypeStruct((s_len, D_MODEL), F32), jax.ShapeDtypeStruct((s_len, D_MODEL), BF16)],
        compiler_params=_cparams("parallel"),
    )(x1, oa, ob, ga, gb, wswa, wsb, wout)


def _merge_bwd(dx2, oa, ob, ga, gb, wswa, wsb, wout, dep):
    s_len = dx2.shape[0]
    tm = min(512, s_len)

    def body(dx_ref, oa_ref, ob_ref, ga_ref, gb_ref, wa_ref, wb_ref, wo_ref, dep_ref,
             doa_ref, dob_ref, dga_ref, dgb_ref, dpa_ref, dpb_ref, dxb_ref):
        dxb = dx_ref[...].astype(BF16)
        dxb_ref[...] = dxb
        dmg = _dot_nt(dxb, wo_ref[...])
        for o_ref, g_ref, w_ref, do_ref, dg_ref, dp_ref in (
                (oa_ref, ga_ref, wa_ref, doa_ref, dga_ref, dpa_ref),
                (ob_ref, gb_ref, wb_ref, dob_ref, dgb_ref, dpb_ref)):
            pv = _dot(o_ref[...], w_ref[...])
            sg = jax.nn.sigmoid(g_ref[...])
            dp = (dmg * sg).astype(BF16)
            dp_ref[...] = dp
            dg_ref[...] = (dmg * pv * sg * (1.0 - sg)).astype(BF16)
            do_ref[...] = _dot_nt(dp, w_ref[...]).astype(BF16)

    row = lambda i: (i, 0)
    full = lambda i: (0, 0)
    wide = pl.BlockSpec((tm, D_MODEL), row)
    half = pl.BlockSpec((tm, 512), row)
    return pl.pallas_call(
        body, name="merge_bwd",
        grid=(s_len // tm,),
        in_specs=[wide, half, half, wide, wide, pl.BlockSpec((512, D_MODEL), full),
                  pl.BlockSpec((512, D_MODEL), full), pl.BlockSpec((D_MODEL, D_MODEL), full),
                  pl.BlockSpec((8, LANES), full)],
        out_specs=[half, half, wide, wide, wide, wide, wide],
        out_shape=[jax.ShapeDtypeStruct((s_len, 512), BF16)] * 2 + [jax.ShapeDtypeStruct((s_len, D_MODEL), BF16)] * 5,
        compiler_params=_cparams("parallel"),
    )(dx2, oa, ob, ga, gb, wswa, wsb, wout, dep)


def _loss_fwd_bwd(x3, tgt, g):
    s_len = x3.shape[0]
    tm = min(1024, s_len)

    def body(x_ref, t_ref, g_ref, dx_ref, loss_ref, dg_ref):
        @pl.when(pl.program_id(0) == 0)
        def _():
            loss_ref[...] = jnp.zeros_like(loss_ref)
            dg_ref[...] = jnp.zeros_like(dg_ref)

        xv = x_ref[...]
        gv = g_ref[...]
        r = _rms_rstd(xv)
        err = xv * r * gv - t_ref[...]
        loss_ref[...] += 0.5 * jnp.sum(jnp.mean(err * err, axis=-1, keepdims=True), axis=0, keepdims=True)
        dx, dg = _rms_bwd(err * (1.0 / D_MODEL), xv, r, gv)
        dx_ref[...] = dx
        dg_ref[...] += dg

    row = lambda i: (i, 0)
    return pl.pallas_call(
        body, name="loss_fwd_bwd",
        grid=(s_len // tm,),
        in_specs=[pl.BlockSpec((tm, D_MODEL), row), pl.BlockSpec((tm, D_MODEL), row),
                  pl.BlockSpec((1, D_MODEL), lambda i: (0, 0))],
        out_specs=[pl.BlockSpec((tm, D_MODEL), row), pl.BlockSpec((1, 1), lambda i: (0, 0)),
                   pl.BlockSpec((1, D_MODEL), lambda i: (0, 0))],
        out_shape=[jax.ShapeDtypeStruct((s_len, D_MODEL), F32), jax.ShapeDtypeStruct((1, 1), F32),
                   jax.ShapeDtypeStruct((1, D_MODEL), F32)],
        compiler_params=_cparams("arbitrary"),
    )(x3, tgt, g)


def _rel_bucket_matrix():
    qi = jnp.arange(SWA_BLOCK)[:, None] + SWA_BLOCK
    kj = jnp.arange(2 * SWA_BLOCK)[None, :]
    dist = jnp.maximum(qi - kj, 0)
    max_exact = REL_BUCKETS // 2
    d = jnp.maximum(dist, 1).astype(F32)
    large = max_exact + (jnp.log(d / max_exact) / np.log(REL_MAX_DIST / max_exact)
                         * (REL_BUCKETS - max_exact)).astype(jnp.int32)
    large = jnp.minimum(large, REL_BUCKETS - 1)
    return jnp.where(dist < max_exact, dist, large).astype(jnp.int32)


def _swa_bias_into(bias_ref, bkt_ref, tab_ref):
    bk = bkt_ref[...]
    for h in range(N_HEADS):
        acc = jnp.zeros(bk.shape, F32)
        for bucket in range(REL_BUCKETS):
            acc = jnp.where(bk == bucket, tab_ref[bucket, h], acc)
        bias_ref[h] = acc


def _swa_valid(n):
    shape = (SWA_BLOCK, 2 * SWA_BLOCK)
    row = lax.broadcasted_iota(jnp.int32, shape, 0)
    col = lax.broadcasted_iota(jnp.int32, shape, 1)
    dist = row + SWA_BLOCK - col
    return (dist >= 0) & (dist < SWA_BLOCK) & ((col >= SWA_BLOCK) | (n > 0))


def _swa_probs(q, k, bias, sink, valid):
    lg = jnp.where(valid, _dot_nt(q, k) * Q_SCALE + bias, NEG_BIG)
    m = jnp.maximum(jnp.max(lg, axis=-1, keepdims=True), sink)
    e = jnp.exp(lg - m)
    es = jnp.exp(sink - m)
    den = jnp.sum(e, axis=-1, keepdims=True) + es
    return e / den, es / den


def _swa_specs(s_len):
    blk = SWA_BLOCK
    cur = lambda n: (n, 0)
    prev = lambda n: (jnp.maximum(n - 1, 0), 0)
    kvw = SWA_KV_HEADS * LANES
    return [pl.BlockSpec(memory_space=pltpu.SMEM), pl.BlockSpec(memory_space=pltpu.SMEM),
            pl.BlockSpec((blk, 2 * blk), lambda n: (0, 0)),
            pl.BlockSpec((blk, N_HEADS * LANES), cur),
            pl.BlockSpec((blk, kvw), prev), pl.BlockSpec((blk, kvw), cur),
            pl.BlockSpec((blk, kvw), prev), pl.BlockSpec((blk, kvw), cur)]


def _swa_fwd(tab, sinks, bkt, q, k, v):
    s_len = q.shape[0]
    blk = SWA_BLOCK

    def body(tab_ref, sink_ref, bkt_ref, q_ref, kp_ref, kc_ref, vp_ref, vc_ref, o_ref, bias_ref):
        n = pl.program_id(0)

        @pl.when(n == 0)
        def _():
            _swa_bias_into(bias_ref, bkt_ref, tab_ref)

        valid = _swa_valid(n)
        for grp in range(SWA_KV_HEADS):
            gl = slice(grp * LANES, (grp + 1) * LANES)
            kk = jnp.concatenate([kp_ref[:, gl], kc_ref[:, gl]], axis=0)
            vv = jnp.concatenate([vp_ref[:, gl], vc_ref[:, gl]], axis=0)
            for hh in range(SWA_GROUP):
                h = grp * SWA_GROUP + hh
                hl = slice(h * LANES, (h + 1) * LANES)
                p, _ = _swa_probs(q_ref[:, hl], kk, bias_ref[h], sink_ref[0, h], valid)
                o_ref[:, hl] = _dot(p.astype(BF16), vv).astype(BF16)

    return pl.pallas_call(
        body, name="swa_fwd",
        grid=(s_len // blk,),
        in_specs=_swa_specs(s_len),
        out_specs=pl.BlockSpec((blk, N_HEADS * LANES), lambda n: (n, 0)),
        out_shape=jax.ShapeDtypeStruct((s_len, N_HEADS * LANES), BF16),
        scratch_shapes=[pltpu.VMEM((N_HEADS, blk, 2 * blk), F32)],
        compiler_params=_cparams("arbitrary"),
    )(tab, sinks, bkt, q, k, k, v, v)


def _swa_bwd(tab, sinks, bkt, q, k, v, do):
    s_len = q.shape[0]
    blk = SWA_BLOCK
    nb = s_len // blk
    kvw = SWA_KV_HEADS * LANES

    def body(tab_ref, sink_ref, bkt_ref, q_ref, kp_ref, kc_ref, vp_ref, vc_ref, do_ref,
             dq_ref, dk_ref, dv_ref, dtab_ref, dsink_ref, bias_ref, dbias_ref):
        n = pl.program_id(0)

        @pl.when(n == 0)
        def _():
            _swa_bias_into(bias_ref, bkt_ref, tab_ref)
            dbias_ref[...] = jnp.zeros_like(dbias_ref)
            dk_ref[...] = jnp.zeros_like(dk_ref)
            dv_ref[...] = jnp.zeros_like(dv_ref)
            dsink_ref[...] = jnp.zeros_like(dsink_ref)
            dtab_ref[...] = jnp.zeros_like(dtab_ref)

        valid = _swa_valid(n)
        cur_rows = pl.ds(pl.multiple_of(n * blk, blk), blk)
        prev_rows = pl.ds(pl.multiple_of(jnp.maximum(n - 1, 0) * blk, blk), blk)
        for grp in range(SWA_KV_HEADS):
            gl = slice(grp * LANES, (grp + 1) * LANES)
            kk = jnp.concatenate([kp_ref[:, gl], kc_ref[:, gl]], axis=0)
            vv = jnp.concatenate([vp_ref[:, gl], vc_ref[:, gl]], axis=0)
            dk_acc = jnp.zeros((2 * blk, LANES), F32)
            dv_acc = jnp.zeros((2 * blk, LANES), F32)
            for hh in range(SWA_GROUP):
                h = grp * SWA_GROUP + hh
                hl = slice(h * LANES, (h + 1) * LANES)
                qh = q_ref[:, hl]
                doh = do_ref[:, hl]
                p, ps = _swa_probs(qh, kk, bias_ref[h], sink_ref[0, h], valid)
                dp = _dot_nt(doh, vv)
                delta = jnp.sum(p * dp, axis=-1, keepdims=True)
                dl = p * (dp - delta)
                dsink_ref[h:h + 1, :] += jnp.broadcast_to(-jnp.sum(ps * delta, axis=0, keepdims=True), (1, LANES))
                dbias_ref[h] += dl
                dlb = dl.astype(BF16)
                dq_ref[:, hl] = (Q_SCALE * _dot(dlb, kk)).astype(BF16)
                dk_acc += Q_SCALE * _dot_tn(dlb, qh)
                dv_acc += _dot_tn(p.astype(BF16), doh)
            dk_ref[cur_rows, gl] += dk_acc[blk:]
            dv_ref[cur_rows, gl] += dv_acc[blk:]

            @pl.when(n > 0)
            def _():
                dk_ref[prev_rows, gl] += dk_acc[:blk]
                dv_ref[prev_rows, gl] += dv_acc[:blk]

        @pl.when(n == nb - 1)
        def _():
            bk = bkt_ref[...]
            lane = lax.broadcasted_iota(jnp.int32, (1, LANES), 1)
            for bucket in range(REL_BUCKETS):
                rowv = jnp.zeros((1, LANES), F32)
                for h in range(N_HEADS):
                    val = jnp.sum(jnp.where(bk == bucket, dbias_ref[h], 0.0), axis=1, keepdims=True)
                    val = jnp.sum(val, axis=0, keepdims=True)
                    rowv = jnp.where(lane == h, val, rowv)
                dtab_ref[bucket:bucket + 1, :] = rowv

    return pl.pallas_call(
        body, name="swa_bwd",
        grid=(nb,),
        in_specs=_swa_specs(s_len) + [pl.BlockSpec((blk, N_HEADS * LANES), lambda n: (n, 0))],
        out_specs=[pl.BlockSpec((blk, N_HEADS * LANES), lambda n: (n, 0)),
                   pl.BlockSpec((s_len, kvw), lambda n: (0, 0)), pl.BlockSpec((s_len, kvw), lambda n: (0, 0)),
                   pl.BlockSpec((REL_BUCKETS, LANES), lambda n: (0, 0)), pl.BlockSpec((N_HEADS, LANES), lambda n: (0, 0))],
        out_shape=[jax.ShapeDtypeStruct((s_len, N_HEADS * LANES), BF16),
                   jax.ShapeDtypeStruct((s_len, kvw), F32), jax.ShapeDtypeStruct((s_len, kvw), F32),
                   jax.ShapeDtypeStruct((REL_BUCKETS, LANES), F32), jax.ShapeDtypeStruct((N_HEADS, LANES), F32)],
        scratch_shapes=[pltpu.VMEM((N_HEADS, blk, 2 * blk), F32), pltpu.VMEM((N_HEADS, blk, 2 * blk), F32)],
        compiler_params=_cparams("arbitrary"),
    )(tab, sinks, bkt, q, k, k, v, v, do)


def _sb_terms(z, valid):
    zc = jnp.minimum(z, SB_LOGIT_CAP)
    lk = -jnp.log(1.0 + jnp.exp(zc))
    lsz = zc + lk
    return lsz, (lk if valid is None else jnp.where(valid, lk, 0.0))


def _bf16_parts(vals):
    parts, rest = [], vals
    for n in range(SB_SUM_PARTS):
        parts.append(rest.astype(BF16))
        if n + 1 < SB_SUM_PARTS:
            rest = rest - parts[-1].astype(F32)
    return parts[0] if len(parts) == 1 else jnp.concatenate(parts, axis=1)


def _row_sum_lanes(vals):
    return jnp.broadcast_to(jnp.sum(vals, axis=-1, keepdims=True), (vals.shape[0], LANES))


def _emit_skewed(*groups):
    for step in range(max(len(items) + len(stages) - 1 for items, stages in groups)):
        for items, stages in groups:
            for s, stage in enumerate(stages):
                if 0 <= step - s < len(items):
                    stage(items[step - s])


def _sb_items(edge):
    items = []
    for h in range(2):
        for r0 in range(0, SB_QUERIES, SB_ROWS):
            if edge is None or r0 >= (edge + 1) * SB_KEYS:
                items.append((h, r0, False))
            elif r0 + SB_ROWS - 1 > edge * SB_KEYS:
                items.append((h, r0, True))
    return items


def _sb_valid(w, edge):
    row = lax.broadcasted_iota(jnp.int32, (SB_ROWS, SB_KEYS), 0) + w[1]
    col = lax.broadcasted_iota(jnp.int32, (SB_ROWS, SB_KEYS), 1) + edge * SB_KEYS
    return col < row


def _sb_consts(tq, tk):
    low = lax.broadcasted_iota(jnp.int32, (tq, LANES), 1) < HEAD_DIM
    row = lax.broadcasted_iota(jnp.int32, (tk, tk), 0)
    col = lax.broadcasted_iota(jnp.int32, (tk, tk), 1)
    right = (row > col).astype(BF16)
    left = (row < col).astype(BF16)
    return low, jnp.concatenate([right] * SB_SUM_PARTS, axis=0), jnp.concatenate([left] * SB_SUM_PARTS, axis=0)


def _sb_fwd(q, kt, v):
    s_len = q.shape[0]
    tq, tk, tr = SB_QUERIES, SB_KEYS, SB_ROWS
    nk, ratio = s_len // tk, tq // tk
    assert nk <= LANES

    def body(q_ref, kt_ref, v_ref, o_ref, car_ref, c_ref, oacc_ref, logw_ref, lksum_ref):
        i = pl.program_id(1)
        qv = q_ref[...]
        low, tri2, _ = _sb_consts(tq, tk)
        lane = lax.broadcasted_iota(jnp.int32, (tr, LANES), 1)
        zero = jnp.zeros_like(qv)
        q_heads = (jnp.where(low, qv, zero), jnp.where(low, zero, qv))
        c_ref[...] = jnp.zeros_like(c_ref)
        oacc_ref[...] = jnp.zeros_like(oacc_ref)
        car_ref[...] = jnp.zeros_like(car_ref)

        def front(j, edge):
            ktv = kt_ref[0, j]
            slot = j % 2
            st = {}

            def s_logits(w):
                st[w, "z"] = _dot(q_heads[w[0]][w[1]:w[1] + tr], ktv)

            def s_terms(w):
                valid = _sb_valid(w, edge) if w[2] else None
                lsz, lk = _sb_terms(st.pop((w, "z")), valid)
                st[w, "parts"] = _bf16_parts(lk)
                st[w, "lsz"] = lsz if valid is None else jnp.where(valid, lsz, NEG_BIG)
                lksum_ref[slot, w[0], w[1]:w[1] + tr, :] = _row_sum_lanes(lk)

            def s_suffix(w):
                logw_ref[slot, w[0], w[1]:w[1] + tr, :] = st.pop((w, "lsz")) + _dot(st.pop((w, "parts")), tri2)

            return _sb_items(edge), [s_logits, s_terms, s_suffix]

        def back(j, edge):
            vv = v_ref[pl.ds(pl.multiple_of(j * tk, tk), tk), :]
            slot = j % 2
            st = {}

            def s_weights(w):
                h, rs = w[0], slice(w[1], w[1] + tr)
                c = c_ref[h, rs, :]
                st[w, "a"] = jnp.exp(logw_ref[slot, h, rs, :] + jnp.tile(c, (1, tk // LANES))).astype(BF16)
                car_ref[h, rs, :] = jnp.where(lane == j, c, car_ref[h, rs, :])
                c_ref[h, rs, :] = c + lksum_ref[slot, h, rs, :]

            def s_values(w):
                oacc_ref[w[0], w[1]:w[1] + tr, :] += _dot(st.pop((w, "a")), vv)

            return _sb_items(edge), [s_weights, s_values]

        first = i * ratio
        _emit_skewed(front(first + ratio - 1, ratio - 1))
        for m in reversed(range(ratio - 1)):
            _emit_skewed(front(first + m, m), back(first + m + 1, m + 1))

        @pl.when(i == 0)
        def _():
            _emit_skewed(back(0, 0))

        @pl.when(i > 0)
        def _():
            _emit_skewed(front(first - 1, None), back(first, 0))

            def step(jj, carry):
                _emit_skewed(front(first - jj, None), back(first - jj + 1, None))
                return carry

            lax.fori_loop(2, first + 1, step, 0)
            _emit_skewed(back(0, None))

        o_ref[...] = jnp.where(low, oacc_ref[0], oacc_ref[1]).astype(BF16)

    return pl.pallas_call(
        body, name="sb_fwd",
        grid=(N_HEADS // 2, s_len // tq),
        in_specs=[pl.BlockSpec((tq, LANES), lambda p, i: (i, p)),
                  pl.BlockSpec((1, nk, LANES, tk), lambda p, i: (p, 0, 0, 0)),
                  pl.BlockSpec((s_len, LANES), lambda p, i: (0, p))],
        out_specs=[pl.BlockSpec((tq, LANES), lambda p, i: (i, p)), pl.BlockSpec((2, tq, LANES), lambda p, i: (p, i, 0))],
        out_shape=[jax.ShapeDtypeStruct((s_len, N_HEADS * HEAD_DIM), BF16),
                   jax.ShapeDtypeStruct((N_HEADS, s_len, LANES), F32)],
        scratch_shapes=[pltpu.VMEM((2, tq, LANES), F32), pltpu.VMEM((2, tq, LANES), F32),
                        pltpu.VMEM((2, 2, tq, tk), F32), pltpu.VMEM((2, 2, tq, LANES), F32)],
        compiler_params=_cparams("parallel", "arbitrary"),
    )(q, kt, v)


def _sb_bwd(q, qt, kt, k, vt, do, dot, cars):
    s_len = q.shape[0]
    tq, tk, tr = SB_QUERIES, SB_KEYS, SB_ROWS
    nk, ratio = s_len // tk, tq // tk

    def body(q_ref, qt_ref, kt_ref, k_ref, vt_ref, do_ref, dot_ref, car_ref, dq_ref, dk_ref, dv_ref,
             gleft_ref, dqacc_ref, dkacc_ref, dvacc_ref, logw_ref, lsz_ref, da_ref, a_ref, dz_ref):
        i = pl.program_id(1)

        @pl.when(i == 0)
        def _():
            dkacc_ref[...] = jnp.zeros_like(dkacc_ref)
            dvacc_ref[...] = jnp.zeros_like(dvacc_ref)

        qv = q_ref[...]
        dov = do_ref[...]
        low, tri_right2, tri_left2 = _sb_consts(tq, tk)
        lane = lax.broadcasted_iota(jnp.int32, (tr, LANES), 1)
        zero = jnp.zeros_like(qv)
        q_heads = (jnp.where(low, qv, zero), jnp.where(low, zero, qv))
        do_heads = (jnp.where(low, dov, zero), jnp.where(low, zero, dov))
        q_t = qt_ref[0, 0]
        do_t = dot_ref[0, 0]
        gleft_ref[...] = jnp.zeros_like(gleft_ref)
        dqacc_ref[...] = jnp.zeros_like(dqacc_ref)

        def front(j, edge):
            ktv = kt_ref[0, j]
            vtv = vt_ref[0, j]
            slot = j % 2
            st = {}

            def s_logits(w):
                h, rs = w[0], slice(w[1], w[1] + tr)
                st[w, "z"] = _dot(q_heads[h][rs], ktv)
                da_ref[slot, h, rs, :] = _dot(do_heads[h][rs], vtv)

            def s_terms(w):
                h, rs = w[0], slice(w[1], w[1] + tr)
                valid = _sb_valid(w, edge) if w[2] else None
                lsz, lk = _sb_terms(st.pop((w, "z")), valid)
                st[w, "parts"] = _bf16_parts(lk)
                lsz = lsz if valid is None else jnp.where(valid, lsz, NEG_BIG)
                lsz_ref[slot, h, rs, :] = lsz
                st[w, "lszc"] = lsz + jnp.sum(jnp.where(lane == j, car_ref[h, rs, :], 0.0), axis=-1, keepdims=True)

            def s_suffix(w):
                logw_ref[slot, w[0], w[1]:w[1] + tr, :] = st.pop((w, "lszc")) + _dot(st.pop((w, "parts")), tri_right2)

            return _sb_items(edge), [s_logits, s_terms, s_suffix]

        def back(j, edge):
            kv = k_ref[pl.ds(pl.multiple_of(j * tk, tk), tk), :]
            slot = j % 2
            st = {}

            items = _sb_items(edge)
            head_rows = [[r0 for hh, r0, _ in items if hh == h] for h in range(2)]

            def s_weights(w):
                h, rs = w[0], slice(w[1], w[1] + tr)
                a = jnp.exp(logw_ref[slot, h, rs, :])
                g = a * da_ref[slot, h, rs, :]
                a_ref[h, rs, :] = a.astype(BF16)
                st[w, "g"], st[w, "parts"] = g, _bf16_parts(g)

            def s_prefix(w):
                st[w, "gs"] = _dot(st.pop((w, "parts")), tri_left2)

            def s_dz(w):
                h, rs = w[0], slice(w[1], w[1] + tr)
                g = st.pop((w, "g"))
                gleft = gleft_ref[h, rs, :]
                gsum = st.pop((w, "gs")) + jnp.tile(gleft, (1, tk // LANES))
                dz = (g - jnp.exp(lsz_ref[slot, h, rs, :]) * (g + gsum)).astype(BF16)
                st[w, "dz"] = dz
                dz_ref[h, rs, :] = dz
                gleft_ref[h, rs, :] = gleft + _row_sum_lanes(g)

            def s_products(w):
                h, rs = w[0], slice(w[1], w[1] + tr)
                dqacc_ref[h, rs, :] += _dot(st.pop((w, "dz")), kv)
                if w[1] == head_rows[h][-1]:
                    feat = slice(h * HEAD_DIM, (h + 1) * HEAD_DIM)
                    hr = slice(head_rows[h][0], tq)
                    dkacc_ref[j, feat, :] += _dot(q_t[feat, hr], dz_ref[h, hr, :])
                    dvacc_ref[j, feat, :] += _dot(do_t[feat, hr], a_ref[h, hr, :])

            return items, [s_weights, s_prefix, s_dz, s_products]

        first = i * ratio

        @pl.when(i == 0)
        def _():
            _emit_skewed(front(0, 0))

        @pl.when(i > 0)
        def _():
            _emit_skewed(front(0, None))

            def step(jj, carry):
                _emit_skewed(front(jj, None), back(jj - 1, None))
                return carry

            lax.fori_loop(1, first, step, 0)
            _emit_skewed(front(first, 0), back(first - 1, None))

        for m in range(1, ratio):
            _emit_skewed(front(first + m, m), back(first + m - 1, m - 1))
        _emit_skewed(back(first + ratio - 1, ratio - 1))
        dq_ref[...] = (Q_SCALE * jnp.where(low, dqacc_ref[0], dqacc_ref[1])).astype(BF16)

        @pl.when(i == s_len // tq - 1)
        def _():
            dk_ref[0] = dkacc_ref[...].astype(BF16)
            dv_ref[0] = dvacc_ref[...].astype(BF16)

    qblk = pl.BlockSpec((tq, LANES), lambda p, i: (i, p))
    qtblk = pl.BlockSpec((1, 1, LANES, tq), lambda p, i: (p, i, 0, 0))
    tblk = pl.BlockSpec((1, nk, LANES, tk), lambda p, i: (p, 0, 0, 0))
    col_full = pl.BlockSpec((s_len, LANES), lambda p, i: (0, p))
    tshape = jax.ShapeDtypeStruct((N_HEADS // 2, nk, LANES, tk), BF16)
    return pl.pallas_call(
        body, name="sb_bwd",
        grid=(N_HEADS // 2, s_len // tq),
        in_specs=[qblk, qtblk, tblk, col_full, tblk, qblk, qtblk, pl.BlockSpec((2, tq, LANES), lambda p, i: (p, i, 0))],
        out_specs=[qblk, tblk, tblk],
        out_shape=[jax.ShapeDtypeStruct((s_len, N_HEADS * HEAD_DIM), BF16), tshape, tshape],
        scratch_shapes=[pltpu.VMEM((2, tq, LANES), F32), pltpu.VMEM((2, tq, LANES), F32),
                        pltpu.VMEM((nk, LANES, tk), F32), pltpu.VMEM((nk, LANES, tk), F32)]
        + [pltpu.VMEM((2, 2, tq, tk), F32)] * 3 + [pltpu.VMEM((2, tq, tk), BF16)] * 2,
        compiler_params=_cparams("parallel", "arbitrary"),
    )(q, qt, kt, k, vt, do, dot, cars)


def _pad_heads(a, heads):
    s_len = a.shape[0]
    a = a.reshape(s_len, heads, HEAD_DIM)
    return jnp.pad(a, ((0, 0), (0, 0), (0, LANES - HEAD_DIM))).reshape(s_len, heads * LANES)


def _unpad_heads(a, heads):
    s_len = a.shape[0]
    return a.reshape(s_len, heads, LANES)[:, :, :HEAD_DIM].reshape(s_len, heads * HEAD_DIM)


def _tile_transposed(a, groups, t):
    s_len = a.shape[0]
    return a.reshape(s_len // t, t, groups, LANES).transpose(2, 0, 3, 1)


def _tile_untransposed(a):
    groups, nt, _, t = a.shape
    return a.transpose(1, 3, 0, 2).reshape(nt * t, groups * LANES)


def _local_step(xs, tgt, gains, sinks, rel_bias, wts, send_early):
    g1, gmix, g2, gfin = gains
    bkt = _rel_bucket_matrix()
    groups = N_HEADS // 2

    x1, h1, a1, b1, u1 = _ffn_fwd(xs, g1, wts["ffn1_w1t"], wts["ffn1_w3t"], wts["ffn1_w2"], "1")
    hm, qa, ka, va, qb, kb, vb, ga, gb = _proj_fwd(x1, gmix, wts["w_int"])
    qa_p, ka_p, va_p = _pad_heads(qa, N_HEADS), _pad_heads(ka, SWA_KV_HEADS), _pad_heads(va, SWA_KV_HEADS)
    oa_p = _swa_fwd(rel_bias, sinks, bkt, qa_p, ka_p, va_p)
    kbt = _tile_transposed(kb, groups, SB_KEYS)
    ob, cars = _sb_fwd(qb, kbt, vb)
    oa = _unpad_heads(oa_p, N_HEADS)
    x2, mg = _merge_fwd(x1, oa, ob, ga, gb, wts["w_swa"], wts["w_sb"], wts["w_out"])
    x3, h3, a3, b3, u3 = _ffn_fwd(x2, g2, wts["ffn2_w1t"], wts["ffn2_w3t"], wts["ffn2_w2"], "2")
    dx3, loss, dgfin = _loss_fwd_bwd(x3, tgt, gfin)

    big = {}
    dx2, dg2, da3, db3, dx3b = _ffn_bwd(dx3, x2, g2, a3, b3, wts["ffn2_w1t"], wts["ffn2_w3t"], wts["ffn2_w2"], "2")
    big["ffn2_w1t"] = _matmul_tn(da3, h3, "ffn2_w1")
    big["ffn2_w3t"] = _matmul_tn(db3, h3, "ffn2_w3")
    big["ffn2_w2"] = _matmul_tn(u3, dx3b, "ffn2_w2")
    dep = send_early(2, big)

    doa, dob, dga, dgb, dpa, dpb, dx2b = _merge_bwd(dx2, oa, ob, ga, gb, wts["w_swa"], wts["w_sb"], wts["w_out"], dep)
    big = {}
    big["w_out"] = _matmul_tn(mg, dx2b, "w_out")
    big["w_swa"] = _matmul_tn(oa, dpa, "w_swa")
    big["w_sb"] = _matmul_tn(ob, dpb, "w_sb")

    dqa_p, dka_p, dva_p, dtab, dsink = _swa_bwd(rel_bias, sinks, bkt, qa_p, ka_p, va_p, _pad_heads(doa, N_HEADS))
    dqb, dkbt, dvbt = _sb_bwd(qb, _tile_transposed(qb, groups, SB_QUERIES), kbt, kb,
                              _tile_transposed(vb, groups, SB_KEYS), dob, _tile_transposed(dob, groups, SB_QUERIES), cars)
    dkb, dvb = _tile_untransposed(dkbt), _tile_untransposed(dvbt)
    dpieces = (_unpad_heads(dqa_p, N_HEADS), _unpad_heads(dka_p, SWA_KV_HEADS).astype(BF16),
               _unpad_heads(dva_p, SWA_KV_HEADS).astype(BF16), dqb, dkb, dvb, dga, dgb)
    big["w_int"] = jnp.concatenate([_matmul_tn(dp, hm, f"w_in{p}") for p, dp in enumerate(dpieces)], axis=0)
    dep = send_early(1, big)
    dx1, dgmix = _proj_bwd(dpieces, dx2, x1, gmix, wts["w_int"], dep)
    big = {}

    dx0, dg1, da1, db1, dx1b = _ffn_bwd(dx1, xs, g1, a1, b1, wts["ffn1_w1t"], wts["ffn1_w3t"], wts["ffn1_w2"], "1")
    big["ffn1_w1t"] = _matmul_tn(da1, h1, "ffn1_w1")
    big["ffn1_w3t"] = _matmul_tn(db1, h1, "ffn1_w3")
    big["ffn1_w2"] = _matmul_tn(u1, dx1b, "ffn1_w2")

    small = {"gains": (dg1, dgmix, dg2, dgfin), "sinks": dsink[:, 0], "rel_bias": dtab[:, :N_HEADS]}
    return loss, dx0, big, small


def _my_place():
    return lax.axis_index("x"), lax.axis_index("y"), lax.axis_index("c")


def _flip(v, bit):
    return 1 - v if bit else v


_RELATIONS = tuple((k >> 2 & 1, k >> 1 & 1, k & 1) for k in range(1, N_DEV))


def _gather_weights(wp):
    def body(x_ref, out_ref, send_sems, recv_sems, local_sem):
        x, y, c = _my_place()
        me, sibling = (x, y, c), (x, y, 1 - c)
        chips = [(1 - x, y), (x, 1 - y), (1 - x, 1 - y)]

        def rows(px, py, pc):
            return out_ref.at[4 * px + 2 * py + pc]

        def copy(k, block, to, src=None):
            return pltpu.make_async_remote_copy(
                src_ref=rows(*block) if src is None else src, dst_ref=rows(*block),
                send_sem=send_sems.at[k], recv_sem=recv_sems.at[k],
                device_id=to, device_id_type=pl.DeviceIdType.MESH)

        mine = pltpu.make_async_copy(x_ref, rows(*me), local_sem)
        mine.start()
        first = [copy(0, me, sibling, src=x_ref)]
        first += [copy(1 + j, me, (*chip, c), src=x_ref) for j, chip in enumerate(chips)]
        for cp in first:
            cp.start()
        passed = [copy(4 + j, (*chip, c), sibling) for j, chip in enumerate(chips)]
        for j, chip in enumerate(chips):
            copy(1 + j, (*chip, c), me).wait_recv()
            passed[j].start()
        copy(0, sibling, me).wait_recv()
        for j, chip in enumerate(chips):
            copy(4 + j, (*chip, 1 - c), me).wait_recv()
        for cp in first + passed:
            cp.wait_send()
        mine.wait()

    return pl.pallas_call(
        body, name="gather_weights",
        out_shape=jax.ShapeDtypeStruct((N_DEV,) + wp.shape, wp.dtype),
        in_specs=[pl.BlockSpec(memory_space=pl.ANY)],
        out_specs=pl.BlockSpec(memory_space=pl.ANY),
        scratch_shapes=[pltpu.SemaphoreType.DMA((7,)), pltpu.SemaphoreType.DMA((7,)), pltpu.SemaphoreType.DMA(())],
    )(wp)


def _exchange_grads(gp):
    def body(g_ref, out_ref, send_sems, recv_sems, local_sem):
        x, y, c = _my_place()
        me = 4 * x + 2 * y + c
        mine = pltpu.make_async_copy(g_ref.at[me], out_ref.at[me], local_sem)
        mine.start()
        copies = []
        for k, (fx, fy, fc) in enumerate(_RELATIONS):
            px, py, pc = _flip(x, fx), _flip(y, fy), _flip(c, fc)
            peer = 4 * px + 2 * py + pc
            copies.append((
                pltpu.make_async_remote_copy(
                    src_ref=g_ref.at[peer], dst_ref=out_ref.at[me], send_sem=send_sems.at[k], recv_sem=recv_sems.at[k],
                    device_id=(px, py, pc), device_id_type=pl.DeviceIdType.MESH),
                pltpu.make_async_remote_copy(
                    src_ref=g_ref.at[peer], dst_ref=out_ref.at[peer], send_sem=send_sems.at[k], recv_sem=recv_sems.at[k],
                    device_id=(px, py, pc), device_id_type=pl.DeviceIdType.MESH)))
        for out_cp, _ in copies:
            out_cp.start()
        for _, in_cp in copies:
            in_cp.wait_recv()
        for out_cp, _ in copies:
            out_cp.wait_send()
        mine.wait()

    return pl.pallas_call(
        body, name="exchange_grads",
        out_shape=jax.ShapeDtypeStruct(gp.shape, gp.dtype),
        in_specs=[pl.BlockSpec(memory_space=pl.ANY)],
        out_specs=pl.BlockSpec(memory_space=pl.ANY),
        scratch_shapes=[pltpu.SemaphoreType.DMA((7,)), pltpu.SemaphoreType.DMA((7,)), pltpu.SemaphoreType.DMA(())],
    )(gp)


_HBM = pl.BlockSpec(memory_space=pltpu.HBM)
_SEM = pl.BlockSpec(memory_space=pltpu.SEMAPHORE)
_EFFECT = pltpu.SideEffectType.DATAFLOW_SIDE_EFFECTING


def _peers():
    x, y, c = _my_place()
    out = []
    for k, (fx, fy, fc) in enumerate(_RELATIONS):
        px, py, pc = _flip(x, fx), _flip(y, fy), _flip(c, fc)
        out.append((k, (px, py, pc), 4 * px + 2 * py + pc))
    return out, 4 * x + 2 * y + c


def _exchange_start(gp, tag):
    def body(g_ref, land_ref, send_sems, recv_sems, g_thru, land_thru, token):
        peers, me = _peers()
        for k, where, slab in peers:
            pltpu.make_async_remote_copy(
                src_ref=g_ref.at[slab], dst_ref=land_ref.at[me], send_sem=send_sems.at[k], recv_sem=recv_sems.at[k],
                device_id=where, device_id_type=pl.DeviceIdType.MESH).start()
        token[...] = jnp.zeros_like(token)

    return pl.pallas_call(
        body, name=f"exchange_start_{tag}",
        out_shape=(pltpu.SemaphoreType.DMA((N_DEV - 1,)), pltpu.SemaphoreType.DMA((N_DEV - 1,)),
                   pltpu.HBM(gp.shape, gp.dtype), pltpu.HBM(gp.shape, gp.dtype), jax.ShapeDtypeStruct((8, LANES), F32)),
        in_specs=(_HBM, _HBM), out_specs=(_SEM, _SEM, _HBM, _HBM, pl.BlockSpec(memory_space=pltpu.VMEM)),
        input_output_aliases={0: 2, 1: 3},
        compiler_params=pltpu.CompilerParams(has_side_effects=_EFFECT),
    )(pltpu.with_memory_space_constraint(gp, pltpu.HBM),
      pltpu.with_memory_space_constraint(lax.empty(gp.shape, gp.dtype), pltpu.HBM))


def _exchange_wait(send_sems, recv_sems, g_thru, land_thru, after, tag):
    def body(g_ref, land_ref, send_sems, recv_sems, after_ref, g_out, land_out):
        peers, _ = _peers()
        for k, where, slab in peers:
            copy = pltpu.make_async_remote_copy(
                src_ref=g_ref.at[slab], dst_ref=land_ref.at[slab], send_sem=send_sems.at[k], recv_sem=recv_sems.at[k],
                device_id=where, device_id_type=pl.DeviceIdType.MESH)
            copy.wait_send()
            copy.wait_recv()

    return pl.pallas_call(
        body, name=f"exchange_wait_{tag}",
        out_shape=(pltpu.HBM(g_thru.shape, g_thru.dtype), pltpu.HBM(land_thru.shape, land_thru.dtype)),
        in_specs=(_HBM, _HBM, _SEM, _SEM, pl.BlockSpec(memory_space=pl.ANY)), out_specs=(_HBM, _HBM),
        input_output_aliases={0: 0, 1: 1},
        compiler_params=pltpu.CompilerParams(has_side_effects=_EFFECT),
    )(g_thru, land_thru, send_sems, recv_sems, after)


def _adamw(w, g, m, v):
    m = ADAM_B1 * m + (1.0 - ADAM_B1) * g
    v = ADAM_B2 * v + (1.0 - ADAM_B2) * jnp.square(g)
    m_hat = m / (1.0 - ADAM_B1 ** ADAM_STEP)
    v_hat = v / (1.0 - ADAM_B2 ** ADAM_STEP)
    delta = -ADAM_LR * (m_hat / (jnp.sqrt(v_hat) + ADAM_EPS) + ADAM_WD * w)
    return delta, m, v


def _sum_and_adamw(parts, w, m, v, tr, tag):
    rows = w.shape[0]
    assert rows % tr == 0

    def body(p_ref, w_ref, m_ref, v_ref, g_out, d_out, m_out, v_out):
        g = p_ref[0].astype(F32)
        for d in range(1, N_DEV):
            g = g + p_ref[d].astype(F32)
        delta, mn, vn = _adamw(w_ref[...], g, m_ref[...], v_ref[...])
        g_out[...] = g
        d_out[...] = delta
        m_out[...] = mn
        v_out[...] = vn

    sp = pl.BlockSpec((tr, D_MODEL), lambda i: (i, 0))
    return pl.pallas_call(
        body, name=f"sum_and_adamw_{tag}",
        grid=(rows // tr,),
        in_specs=[pl.BlockSpec((N_DEV, tr, D_MODEL), lambda i: (0, i, 0)), sp, sp, sp],
        out_specs=[sp] * 4,
        out_shape=[jax.ShapeDtypeStruct(w.shape, F32)] * 4,
        compiler_params=_cparams("parallel"),
    )(parts, w, m, v)


def _small_allreduce_adamw(part, w, m, v):
    def body(p_ref, w_ref, m_ref, v_ref, g_out, d_out, m_out, v_out, buf, send_sems, recv_sems):
        x, y, c = _my_place()
        me = 4 * x + 2 * y + c
        buf[me] = p_ref[...]
        copies = []
        for k, (fx, fy, fc) in enumerate(_RELATIONS):
            px, py, pc = _flip(x, fx), _flip(y, fy), _flip(c, fc)
            peer = 4 * px + 2 * py + pc
            copies.append((
                pltpu.make_async_remote_copy(
                    src_ref=buf.at[me], dst_ref=buf.at[me], send_sem=send_sems.at[k], recv_sem=recv_sems.at[k],
                    device_id=(px, py, pc), device_id_type=pl.DeviceIdType.MESH),
                pltpu.make_async_remote_copy(
                    src_ref=buf.at[me], dst_ref=buf.at[peer], send_sem=send_sems.at[k], recv_sem=recv_sems.at[k],
                    device_id=(px, py, pc), device_id_type=pl.DeviceIdType.MESH)))
        for out_cp, _ in copies:
            out_cp.start()
        for _, in_cp in copies:
            in_cp.wait_recv()
        for out_cp, _ in copies:
            out_cp.wait_send()
        g = buf[0]
        for d in range(1, N_DEV):
            g = g + buf[d]
        delta, mn, vn = _adamw(w_ref[...], g, m_ref[...], v_ref[...])
        g_out[...] = g
        d_out[...] = delta
        m_out[...] = mn
        v_out[...] = vn

    vm = pl.BlockSpec(memory_space=pltpu.VMEM)
    return pl.pallas_call(
        body, name="small_allreduce_adamw",
        in_specs=[vm] * 4, out_specs=[vm] * 4,
        out_shape=[jax.ShapeDtypeStruct(w.shape, F32)] * 4,
        scratch_shapes=[pltpu.VMEM((N_DEV,) + part.shape, F32),
                        pltpu.SemaphoreType.DMA((7,)), pltpu.SemaphoreType.DMA((7,))],
    )(part, w, m, v)


_TRANSPOSED = ("ffn1_w1", "ffn1_w3", "w_in", "ffn2_w1", "ffn2_w3")
_BRANCH = ("w_branch_swa", "w_branch_sb")


def _pack_shards(t, names):
    parts = []
    for name in names:
        a = t[name][0]
        if name in _TRANSPOSED:
            a = a.T
        elif name in _BRANCH:
            a = a.reshape(64, D_MODEL)
        parts.append(a)
    return jnp.concatenate(parts, axis=0)


def _unpack_shards(p, names):
    out, lo = {}, 0
    for name in names:
        a = p[lo:lo + BIG_ROWS[BIG_NAMES.index(name)]]
        lo += a.shape[0]
        if name in _TRANSPOSED:
            a = a.T
        elif name in _BRANCH:
            a = a.reshape(512, 128)
        out[name] = a[None]
    return out


def _full_weights(wg):
    def part(name):
        k = BIG_NAMES.index(name)
        return wg[:, BIG_OFFS[k]:BIG_OFFS[k + 1]]

    def branch(name):
        return part(name).reshape(N_DEV, 512, 128).transpose(1, 0, 2).reshape(512, D_MODEL)

    return {
        "ffn1_w1t": part("ffn1_w1").reshape(D_FF, D_MODEL), "ffn1_w3t": part("ffn1_w3").reshape(D_FF, D_MODEL),
        "ffn1_w2": part("ffn1_w2").reshape(D_FF, D_MODEL), "w_int": part("w_in").reshape(IN_W, D_MODEL),
        "w_swa": branch("w_branch_swa"), "w_sb": branch("w_branch_sb"),
        "w_out": part("w_out").reshape(D_MODEL, D_MODEL),
        "ffn2_w1t": part("ffn2_w1").reshape(D_FF, D_MODEL), "ffn2_w3t": part("ffn2_w3").reshape(D_FF, D_MODEL),
        "ffn2_w2": part("ffn2_w2").reshape(D_FF, D_MODEL),
    }


_GRAD_KEY = {"ffn1_w1": "ffn1_w1t", "ffn1_w3": "ffn1_w3t", "ffn1_w2": "ffn1_w2", "w_in": "w_int",
             "w_branch_swa": "w_swa", "w_branch_sb": "w_sb", "w_out": "w_out",
             "ffn2_w1": "ffn2_w1t", "ffn2_w3": "ffn2_w3t", "ffn2_w2": "ffn2_w2"}


def _pack_full_grads(big, names):
    parts = []
    for name in names:
        a = big[_GRAD_KEY[name]]
        if name in _BRANCH:
            a = a.reshape(512, N_DEV, 128).transpose(1, 0, 2)
        parts.append(a.reshape(N_DEV, BIG_ROWS[BIG_NAMES.index(name)], D_MODEL).astype(BF16))
    return jnp.concatenate(parts, axis=1)


_SMALL_NAMES = ("norm_ffn1", "norm_mix", "norm_ffn2", "norm_final", "swa_sinks", "rel_bias")


def _pack_small(vals):
    rows = []
    for a in vals:
        a = a.reshape(-1)
        rows.append(jnp.pad(a, (0, D_MODEL - a.shape[0])))
    rows += [jnp.zeros((D_MODEL,), F32)] * (SMALL_ROWS - len(rows))
    return jnp.stack(rows)


def _unpack_small(p):
    return {"norm_ffn1": p[0:1], "norm_mix": p[1:2], "norm_ffn2": p[2:3], "norm_final": p[3],
            "swa_sinks": p[4:5, :N_HEADS], "rel_bias": p[5, :REL_BUCKETS * N_HEADS].reshape(REL_BUCKETS, N_HEADS)}


ALL_NAMES = ("norm_ffn1", "ffn1_w1", "ffn1_w3", "ffn1_w2", "norm_mix", "w_in", "swa_sinks", "rel_bias",
             "w_branch_swa", "w_branch_sb", "w_out", "norm_ffn2", "ffn2_w1", "ffn2_w3", "ffn2_w2", "norm_final")


def kernel(x, norm_ffn1, ffn1_w1, ffn1_w3, ffn1_w2, norm_mix, w_in, swa_sinks, rel_bias, w_branch_swa, w_branch_sb, w_out, norm_ffn2, ffn2_w1, ffn2_w3, ffn2_w2, norm_final, loss_target, m_norm_ffn1, m_ffn1_w1, m_ffn1_w3, m_ffn1_w2, m_norm_mix, m_w_in, m_swa_sinks, m_rel_bias, m_w_branch_swa, m_w_branch_sb, m_w_out, m_norm_ffn2, m_ffn2_w1, m_ffn2_w3, m_ffn2_w2, m_norm_final, v_norm_ffn1, v_ffn1_w1, v_ffn1_w3, v_ffn1_w2, v_norm_mix, v_w_in, v_swa_sinks, v_rel_bias, v_w_branch_swa, v_w_branch_sb, v_w_out, v_norm_ffn2, v_ffn2_w1, v_ffn2_w3, v_ffn2_w2, v_norm_final):
    w = dict(zip(ALL_NAMES, (norm_ffn1, ffn1_w1, ffn1_w3, ffn1_w2, norm_mix, w_in, swa_sinks, rel_bias,
                             w_branch_swa, w_branch_sb, w_out, norm_ffn2, ffn2_w1, ffn2_w3, ffn2_w2, norm_final)))
    m = dict(zip(ALL_NAMES, (m_norm_ffn1, m_ffn1_w1, m_ffn1_w3, m_ffn1_w2, m_norm_mix, m_w_in, m_swa_sinks, m_rel_bias,
                             m_w_branch_swa, m_w_branch_sb, m_w_out, m_norm_ffn2, m_ffn2_w1, m_ffn2_w3, m_ffn2_w2,
                             m_norm_final)))
    v = dict(zip(ALL_NAMES, (v_norm_ffn1, v_ffn1_w1, v_ffn1_w3, v_ffn1_w2, v_norm_mix, v_w_in, v_swa_sinks, v_rel_bias,
                             v_w_branch_swa, v_w_branch_sb, v_w_out, v_norm_ffn2, v_ffn2_w1, v_ffn2_w3, v_ffn2_w2,
                             v_norm_final)))

    w_packed = [_pack_shards(w, names) for names in GROUPS]
    wts = _full_weights(_gather_weights(jnp.concatenate(w_packed, axis=0).astype(BF16)))
    gains = (norm_ffn1, norm_mix, norm_ffn2, norm_final.reshape(1, D_MODEL))

    in_flight = {}

    def send_early(group, grads):
        in_flight[group] = _exchange_start(_pack_full_grads(grads, GROUPS[group]), f"group{group}")
        return in_flight[group][4]

    loss, dx, grads0, small = _local_step(x[0], loss_target[0], gains, swa_sinks, rel_bias, wts, send_early)
    parts = {0: _exchange_grads(_pack_full_grads(grads0, GROUPS[0]))}
    me = 4 * lax.axis_index("x") + 2 * lax.axis_index("y") + lax.axis_index("c")
    after = parts[0]
    for group in (1, 2):
        send_sems, recv_sems, g_thru, land_thru, _ = in_flight[group]
        sent, landed = _exchange_wait(send_sems, recv_sems, g_thru, land_thru, after, f"group{group}")
        own = lax.dynamic_slice_in_dim(sent, me, 1, axis=0)
        parts[group] = lax.dynamic_update_slice_in_dim(landed, own, me, axis=0)
        after = landed

    big_outs = [{}, {}, {}, {}]
    for group, names in enumerate(GROUPS):
        res = _sum_and_adamw(parts[group], w_packed[group], _pack_shards(m, names), _pack_shards(v, names),
                             GROUP_TILE[group], f"group{group}")
        for acc, packed in zip(big_outs, res):
            acc.update(_unpack_shards(packed, names))
    g_big, d_big, m_big, v_big = big_outs

    small_part = _pack_small(small["gains"] + (small["sinks"], small["rel_bias"]))
    g_sm, d_sm, m_sm, v_sm = (_unpack_small(p) for p in _small_allreduce_adamw(
        small_part, _pack_small([w[n] for n in _SMALL_NAMES]), _pack_small([m[n] for n in _SMALL_NAMES]),
        _pack_small([v[n] for n in _SMALL_NAMES])))

    total_loss = lax.psum(loss[0, 0], AXES)
    outs = [total_loss, dx[None]]
    for big_d, small_d in ((g_big, g_sm), (d_big, d_sm), (m_big, m_sm), (v_big, v_sm)):
        merged = {**big_d, **small_d}
        outs += [merged[n] for n in ALL_NAMES]
    return tuple(outs)
```

```python
import jax
import jax.numpy as jnp
import numpy as np
from jax import lax
from jax.experimental import pallas as pl
from jax.experimental.pallas import tpu as pltpu

F32 = jnp.float32
BF16 = jnp.bfloat16

D_MODEL = 1024
D_FF = 2816
HEAD_DIM = 64
N_HEADS = 8
SWA_KV_HEADS = 2
SWA_GROUP = 4
SWA_BLOCK = 128
REL_BUCKETS = 32
REL_MAX_DIST = 128
RMS_EPS = 1e-6
NEG_BIG = -1e30
Q_SCALE = HEAD_DIM ** -0.5
LANES = 128

N_DEV = 8
AXES = ("x", "y", "c")

ADAM_LR = 0.001
ADAM_B1 = 0.9
ADAM_B2 = 0.999
ADAM_EPS = 1e-08
ADAM_WD = 0.01
ADAM_STEP = 10

IN_SIZES = (512, 128, 128, 512, 512, 512, 1024, 1024)
IN_OFFS = tuple(int(v) for v in np.cumsum((0,) + IN_SIZES))
IN_W = IN_OFFS[-1]

BIG_NAMES = ("ffn1_w1", "ffn1_w3", "ffn1_w2", "w_in", "w_branch_swa", "w_branch_sb", "w_out",
             "ffn2_w1", "ffn2_w3", "ffn2_w2")
BIG_ROWS = (352, 352, 352, 544, 64, 64, 128, 352, 352, 352)
BIG_OFFS = tuple(int(v) for v in np.cumsum((0,) + BIG_ROWS))
PACK_ROWS = BIG_OFFS[-1]
SMALL_ROWS = 8
GROUPS = (BIG_NAMES[0:3], BIG_NAMES[3:7], BIG_NAMES[7:10])
GROUP_TILE = (96, 160, 96)

VMEM_LIMIT = 56 * 1024 * 1024
SB_QUERIES = 512
SB_KEYS = 256
SB_ROWS = 256
SB_SUM_PARTS = 1
SB_LOGIT_CAP = 80.0


def _dot(a, b):
    return jnp.dot(a, b, preferred_element_type=F32)


def _dot_nt(a, b):
    return lax.dot_general(a, b, (((1,), (1,)), ((), ())), preferred_element_type=F32)


def _dot_tn(a, b):
    return lax.dot_general(a, b, (((0,), (0,)), ((), ())), preferred_element_type=F32)


def _cparams(*sem):
    return pltpu.CompilerParams(dimension_semantics=sem, vmem_limit_bytes=VMEM_LIMIT)


def _rms_rstd(xv):
    return lax.rsqrt(jnp.mean(xv * xv, axis=-1, keepdims=True) + RMS_EPS)


def _rms_bwd(dh, xv, r, g):
    xhat = xv * r
    dg = jnp.sum(dh * xhat, axis=0, keepdims=True)
    dxn = dh * g
    dx = r * (dxn - xhat * jnp.mean(dxn * xhat, axis=-1, keepdims=True))
    return dx, dg


def _ffn_fwd(x, g, w1t, w3t, w2, tag):
    s_len = x.shape[0]
    tm, tf = min(1024, s_len), 256
    nf = D_FF // tf

    def body(x_ref, g_ref, w1_ref, w3_ref, w2_ref, xo_ref, h_ref, a_ref, b_ref, u_ref, acc_ref, hs_ref):
        j = pl.program_id(1)

        @pl.when(j == 0)
        def _():
            xv = x_ref[...]
            h = (xv * _rms_rstd(xv) * g_ref[...]).astype(BF16)
            hs_ref[...] = h
            h_ref[...] = h
            acc_ref[...] = jnp.zeros_like(acc_ref)

        h = hs_ref[...]
        a = _dot_nt(h, w1_ref[...])
        b = _dot_nt(h, w3_ref[...])
        a_ref[...] = a.astype(BF16)
        b_ref[...] = b.astype(BF16)
        uh = (0.5 * (a * jax.nn.sigmoid(a) * b)).astype(BF16)
        u_ref[...] = uh
        acc_ref[...] += _dot(uh, w2_ref[...])

        @pl.when(j == nf - 1)
        def _():
            xo_ref[...] = x_ref[...] + acc_ref[...]

    row = lambda i, j: (i, 0)
    return pl.pallas_call(
        body, name=f"ffn_fwd_{tag}",
        grid=(s_len // tm, nf),
        in_specs=[pl.BlockSpec((tm, D_MODEL), row), pl.BlockSpec((1, D_MODEL), lambda i, j: (0, 0)),
                  pl.BlockSpec((tf, D_MODEL), lambda i, j: (j, 0)), pl.BlockSpec((tf, D_MODEL), lambda i, j: (j, 0)),
                  pl.BlockSpec((tf, D_MODEL), lambda i, j: (j, 0))],
        out_specs=[pl.BlockSpec((tm, D_MODEL), row), pl.BlockSpec((tm, D_MODEL), row),
                   pl.BlockSpec((tm, tf), lambda i, j: (i, j)), pl.BlockSpec((tm, tf), lambda i, j: (i, j)),
                   pl.BlockSpec((tm, tf), lambda i, j: (i, j))],
        out_shape=[jax.ShapeDtypeStruct((s_len, D_MODEL), F32), jax.ShapeDtypeStruct((s_len, D_MODEL), BF16),
                   jax.ShapeDtypeStruct((s_len, D_FF), BF16), jax.ShapeDtypeStruct((s_len, D_FF), BF16),
                   jax.ShapeDtypeStruct((s_len, D_FF), BF16)],
        scratch_shapes=[pltpu.VMEM((tm, D_MODEL), F32), pltpu.VMEM((tm, D_MODEL), BF16)],
        compiler_params=_cparams("parallel", "arbitrary"),
    )(x, g, w1t, w3t, w2)


def _ffn_bwd(dy, x, g, a, b, w1t, w3t, w2, tag):
    s_len = x.shape[0]
    tm, tf = min(1024, s_len), 256
    nf = D_FF // tf

    def body(dy_ref, x_ref, g_ref, a_ref, b_ref, w1_ref, w3_ref, w2_ref,
             dx_ref, dg_ref, da_ref, db_ref, dyb_ref, acc_ref, dys_ref):
        i, j = pl.program_id(0), pl.program_id(1)

        @pl.when(j == 0)
        def _():
            dyb = dy_ref[...].astype(BF16)
            dys_ref[...] = dyb
            dyb_ref[...] = dyb
            acc_ref[...] = jnp.zeros_like(acc_ref)

        @pl.when((i == 0) & (j == 0))
        def _():
            dg_ref[...] = jnp.zeros_like(dg_ref)

        du = 0.5 * _dot_nt(dys_ref[...], w2_ref[...])
        av = a_ref[...].astype(F32)
        bv = b_ref[...].astype(F32)
        sg = jax.nn.sigmoid(av)
        sil = av * sg
        da = (du * bv * (sg + sil * (1.0 - sg))).astype(BF16)
        db = (du * sil).astype(BF16)
        da_ref[...] = da
        db_ref[...] = db
        acc_ref[...] += _dot(da, w1_ref[...]) + _dot(db, w3_ref[...])

        @pl.when(j == nf - 1)
        def _():
            xv = x_ref[...]
            dx, dg = _rms_bwd(acc_ref[...], xv, _rms_rstd(xv), g_ref[...])
            dx_ref[...] = dy_ref[...] + dx
            dg_ref[...] += dg

    row = lambda i, j: (i, 0)
    blk = lambda i, j: (i, j)
    wsp = pl.BlockSpec((tf, D_MODEL), lambda i, j: (j, 0))
    return pl.pallas_call(
        body, name=f"ffn_bwd_{tag}",
        grid=(s_len // tm, nf),
        in_specs=[pl.BlockSpec((tm, D_MODEL), row), pl.BlockSpec((tm, D_MODEL), row),
                  pl.BlockSpec((1, D_MODEL), lambda i, j: (0, 0)),
                  pl.BlockSpec((tm, tf), blk), pl.BlockSpec((tm, tf), blk), wsp, wsp, wsp],
        out_specs=[pl.BlockSpec((tm, D_MODEL), row), pl.BlockSpec((1, D_MODEL), lambda i, j: (0, 0)),
                   pl.BlockSpec((tm, tf), blk), pl.BlockSpec((tm, tf), blk), pl.BlockSpec((tm, D_MODEL), row)],
        out_shape=[jax.ShapeDtypeStruct((s_len, D_MODEL), F32), jax.ShapeDtypeStruct((1, D_MODEL), F32),
                   jax.ShapeDtypeStruct((s_len, D_FF), BF16), jax.ShapeDtypeStruct((s_len, D_FF), BF16),
                   jax.ShapeDtypeStruct((s_len, D_MODEL), BF16)],
        scratch_shapes=[pltpu.VMEM((tm, D_MODEL), F32), pltpu.VMEM((tm, D_MODEL), BF16)],
        compiler_params=_cparams("arbitrary", "arbitrary"),
    )(dy, x, g, a, b, w1t, w3t, w2)


def _matmul_tn(lhs, rhs, tag):
    s_len, m = lhs.shape
    n = rhs.shape[1]
    tm = min(512, s_len)
    tj = m if m <= 1024 else 1408
    assert m % tj == 0

    def body(l_ref, r_ref, o_ref):
        @pl.when(pl.program_id(1) == 0)
        def _():
            o_ref[...] = jnp.zeros_like(o_ref)

        o_ref[...] += _dot_tn(l_ref[...], r_ref[...])

    return pl.pallas_call(
        body, name=f"matmul_tn_{tag}",
        grid=(m // tj, s_len // tm),
        in_specs=[pl.BlockSpec((tm, tj), lambda j, i: (i, j)), pl.BlockSpec((tm, n), lambda j, i: (i, 0))],
        out_specs=pl.BlockSpec((tj, n), lambda j, i: (j, 0)),
        out_shape=jax.ShapeDtypeStruct((m, n), F32),
        compiler_params=_cparams("parallel", "arbitrary"),
    )(lhs, rhs)


def _proj_fwd(x1, g, wint):
    s_len = x1.shape[0]
    tm = min(512, s_len)
    dts = (BF16, BF16, BF16, BF16, BF16, BF16, F32, F32)

    def body(x_ref, g_ref, w_ref, h_ref, *outs):
        xv = x_ref[...]
        h = (xv * _rms_rstd(xv) * g_ref[...]).astype(BF16)
        h_ref[...] = h
        for p, o_ref in enumerate(outs):
            val = _dot_nt(h, w_ref[IN_OFFS[p]:IN_OFFS[p + 1], :])
            if p == 3:
                val = val * Q_SCALE
            o_ref[...] = val.astype(dts[p])

    row = lambda i: (i, 0)
    return pl.pallas_call(
        body, name="proj_fwd",
        grid=(s_len // tm,),
        in_specs=[pl.BlockSpec((tm, D_MODEL), row), pl.BlockSpec((1, D_MODEL), lambda i: (0, 0)),
                  pl.BlockSpec((IN_W, D_MODEL), lambda i: (0, 0))],
        out_specs=[pl.BlockSpec((tm, D_MODEL), row)] + [pl.BlockSpec((tm, w), row) for w in IN_SIZES],
        out_shape=[jax.ShapeDtypeStruct((s_len, D_MODEL), BF16)]
        + [jax.ShapeDtypeStruct((s_len, w), dt) for w, dt in zip(IN_SIZES, dts)],
        compiler_params=_cparams("parallel"),
    )(x1, g, wint)


def _proj_bwd(dpieces, dx2, x1, g, wint, dep):
    s_len = x1.shape[0]
    tm = min(512, s_len)

    def body(*refs):
        dps = refs[:8]
        dx2_ref, x_ref, g_ref, w_ref, _, dx_ref, dg_ref = refs[8:]

        @pl.when(pl.program_id(0) == 0)
        def _():
            dg_ref[...] = jnp.zeros_like(dg_ref)

        dh = _dot(dps[0][...], w_ref[IN_OFFS[0]:IN_OFFS[1], :])
        for p in range(1, 8):
            dh += _dot(dps[p][...], w_ref[IN_OFFS[p]:IN_OFFS[p + 1], :])
        xv = x_ref[...]
        dx, dg = _rms_bwd(dh, xv, _rms_rstd(xv), g_ref[...])
        dx_ref[...] = dx2_ref[...] + dx
        dg_ref[...] += dg

    row = lambda i: (i, 0)
    return pl.pallas_call(
        body, name="proj_bwd",
        grid=(s_len // tm,),
        in_specs=[pl.BlockSpec((tm, w), row) for w in IN_SIZES]
        + [pl.BlockSpec((tm, D_MODEL), row), pl.BlockSpec((tm, D_MODEL), row),
           pl.BlockSpec((1, D_MODEL), lambda i: (0, 0)), pl.BlockSpec((IN_W, D_MODEL), lambda i: (0, 0)),
           pl.BlockSpec((8, LANES), lambda i: (0, 0))],
        out_specs=[pl.BlockSpec((tm, D_MODEL), row), pl.BlockSpec((1, D_MODEL), lambda i: (0, 0))],
        out_shape=[jax.ShapeDtypeStruct((s_len, D_MODEL), F32), jax.ShapeDtypeStruct((1, D_MODEL), F32)],
        compiler_params=_cparams("arbitrary"),
    )(*dpieces, dx2, x1, g, wint, dep)


def _merge_fwd(x1, oa, ob, ga, gb, wswa, wsb, wout):
    s_len = x1.shape[0]
    tm = min(512, s_len)

    def body(x_ref, oa_ref, ob_ref, ga_ref, gb_ref, wa_ref, wb_ref, wo_ref, xo_ref, mg_ref):
        pa = _dot(oa_ref[...], wa_ref[...])
        pb = _dot(ob_ref[...], wb_ref[...])
        mg = (jax.nn.sigmoid(ga_ref[...]) * pa + jax.nn.sigmoid(gb_ref[...]) * pb).astype(BF16)
        mg_ref[...] = mg
        xo_ref[...] = x_ref[...] + _dot(mg, wo_ref[...])

    row = lambda i: (i, 0)
    full = lambda i: (0, 0)
    return pl.pallas_call(
        body, name="merge_fwd",
        grid=(s_len // tm,),
        in_specs=[pl.BlockSpec((tm, D_MODEL), row), pl.BlockSpec((tm, 512), row), pl.BlockSpec((tm, 512), row),
                  pl.BlockSpec((tm, D_MODEL), row), pl.BlockSpec((tm, D_MODEL), row),
                  pl.BlockSpec((512, D_MODEL), full), pl.BlockSpec((512, D_MODEL), full),
                  pl.BlockSpec((D_MODEL, D_MODEL), full)],
        out_specs=[pl.BlockSpec((tm, D_MODEL), row), pl.BlockSpec((tm, D_MODEL), row)],
        out_shape=[jax.ShapeDtypeStruct((s_len, D_MODEL), F32), jax.ShapeDtypeStruct((s_len, D_MODEL), BF16)],
        compiler_params=_cparams("parallel"),
    )(x1, oa, ob, ga, gb, wswa, wsb, wout)


def _merge_bwd(dx2, oa, ob, ga, gb, wswa, wsb, wout, dep):
    s_len = dx2.shape[0]
    tm = min(512, s_len)

    def body(dx_ref, oa_ref, ob_ref, ga_ref, gb_ref, wa_ref, wb_ref, wo_ref, dep_ref,
             doa_ref, dob_ref, dga_ref, dgb_ref, dpa_ref, dpb_ref, dxb_ref):
        dxb = dx_ref[...].astype(BF16)
        dxb_ref[...] = dxb
        dmg = _dot_nt(dxb, wo_ref[...])
        for o_ref, g_ref, w_ref, do_ref, dg_ref, dp_ref in (
                (oa_ref, ga_ref, wa_ref, doa_ref, dga_ref, dpa_ref),
                (ob_ref, gb_ref, wb_ref, dob_ref, dgb_ref, dpb_ref)):
            pv = _dot(o_ref[...], w_ref[...])
            sg = jax.nn.sigmoid(g_ref[...])
            dp = (dmg * sg).astype(BF16)
            dp_ref[...] = dp
            dg_ref[...] = (dmg * pv * sg * (1.0 - sg)).astype(BF16)
            do_ref[...] = _dot_nt(dp, w_ref[...]).astype(BF16)

    row = lambda i: (i, 0)
    full = lambda i: (0, 0)
    wide = pl.BlockSpec((tm, D_MODEL), row)
    half = pl.BlockSpec((tm, 512), row)
    return pl.pallas_call(
        body, name="merge_bwd",
        grid=(s_len // tm,),
        in_specs=[wide, half, half, wide, wide, pl.BlockSpec((512, D_MODEL), full),
                  pl.BlockSpec((512, D_MODEL), full), pl.BlockSpec((D_MODEL, D_MODEL), full),
                  pl.BlockSpec((8, LANES), full)],
        out_specs=[half, half, wide, wide, wide, wide, wide],
        out_shape=[jax.ShapeDtypeStruct((s_len, 512), BF16)] * 2 + [jax.ShapeDtypeStruct((s_len, D_MODEL), BF16)] * 5,
        compiler_params=_cparams("parallel"),
    )(dx2, oa, ob, ga, gb, wswa, wsb, wout, dep)


def _loss_fwd_bwd(x3, tgt, g):
    s_len = x3.shape[0]
    tm = min(1024, s_len)

    def body(x_ref, t_ref, g_ref, dx_ref, loss_ref, dg_ref):
        @pl.when(pl.program_id(0) == 0)
        def _():
            loss_ref[...] = jnp.zeros_like(loss_ref)
            dg_ref[...] = jnp.zeros_like(dg_ref)

        xv = x_ref[...]
        gv = g_ref[...]
        r = _rms_rstd(xv)
        err = xv * r * gv - t_ref[...]
        loss_ref[...] += 0.5 * jnp.sum(jnp.mean(err * err, axis=-1, keepdims=True), axis=0, keepdims=True)
        dx, dg = _rms_bwd(err * (1.0 / D_MODEL), xv, r, gv)
        dx_ref[...] = dx
        dg_ref[...] += dg

    row = lambda i: (i, 0)
    return pl.pallas_call(
        body, name="loss_fwd_bwd",
        grid=(s_len // tm,),
        in_specs=[pl.BlockSpec((tm, D_MODEL), row), pl.BlockSpec((tm, D_MODEL), row),
                  pl.BlockSpec((1, D_MODEL), lambda i: (0, 0))],
        out_specs=[pl.BlockSpec((tm, D_MODEL), row), pl.BlockSpec((1, 1), lambda i: (0, 0)),
                   pl.BlockSpec((1, D_MODEL), lambda i: (0, 0))],
        out_shape=[jax.ShapeDtypeStruct((s_len, D_MODEL), F32), jax.ShapeDtypeStruct((1, 1), F32),
                   jax.ShapeDtypeStruct((1, D_MODEL), F32)],
        compiler_params=_cparams("arbitrary"),
    )(x3, tgt, g)


def _rel_bucket_matrix():
    qi = jnp.arange(SWA_BLOCK)[:, None] + SWA_BLOCK
    kj = jnp.arange(2 * SWA_BLOCK)[None, :]
    dist = jnp.maximum(qi - kj, 0)
    max_exact = REL_BUCKETS // 2
    d = jnp.maximum(dist, 1).astype(F32)
    large = max_exact + (jnp.log(d / max_exact) / np.log(REL_MAX_DIST / max_exact)
                         * (REL_BUCKETS - max_exact)).astype(jnp.int32)
    large = jnp.minimum(large, REL_BUCKETS - 1)
    return jnp.where(dist < max_exact, dist, large).astype(jnp.int32)


def _swa_bias_into(bias_ref, bkt_ref, tab_ref):
    bk = bkt_ref[...]
    for h in range(N_HEADS):
        acc = jnp.zeros(bk.shape, F32)
        for bucket in range(REL_BUCKETS):
            acc = jnp.where(bk == bucket, tab_ref[bucket, h], acc)
        bias_ref[h] = acc


def _swa_valid(n):
    shape = (SWA_BLOCK, 2 * SWA_BLOCK)
    row = lax.broadcasted_iota(jnp.int32, shape, 0)
    col = lax.broadcasted_iota(jnp.int32, shape, 1)
    dist = row + SWA_BLOCK - col
    return (dist >= 0) & (dist < SWA_BLOCK) & ((col >= SWA_BLOCK) | (n > 0))


def _swa_probs(q, k, bias, sink, valid):
    lg = jnp.where(valid, _dot_nt(q, k) * Q_SCALE + bias, NEG_BIG)
    m = jnp.maximum(jnp.max(lg, axis=-1, keepdims=True), sink)
    e = jnp.exp(lg - m)
    es = jnp.exp(sink - m)
    den = jnp.sum(e, axis=-1, keepdims=True) + es
    return e / den, es / den


def _swa_specs(s_len):
    blk = SWA_BLOCK
    cur = lambda n: (n, 0)
    prev = lambda n: (jnp.maximum(n - 1, 0), 0)
    kvw = SWA_KV_HEADS * LANES
    return [pl.BlockSpec(memory_space=pltpu.SMEM), pl.BlockSpec(memory_space=pltpu.SMEM),
            pl.BlockSpec((blk, 2 * blk), lambda n: (0, 0)),
            pl.BlockSpec((blk, N_HEADS * LANES), cur),
            pl.BlockSpec((blk, kvw), prev), pl.BlockSpec((blk, kvw), cur),
            pl.BlockSpec((blk, kvw), prev), pl.BlockSpec((blk, kvw), cur)]


def _swa_fwd(tab, sinks, bkt, q, k, v):
    s_len = q.shape[0]
    blk = SWA_BLOCK

    def body(tab_ref, sink_ref, bkt_ref, q_ref, kp_ref, kc_ref, vp_ref, vc_ref, o_ref, bias_ref):
        n = pl.program_id(0)

        @pl.when(n == 0)
        def _():
            _swa_bias_into(bias_ref, bkt_ref, tab_ref)

        valid = _swa_valid(n)
        for grp in range(SWA_KV_HEADS):
            gl = slice(grp * LANES, (grp + 1) * LANES)
            kk = jnp.concatenate([kp_ref[:, gl], kc_ref[:, gl]], axis=0)
            vv = jnp.concatenate([vp_ref[:, gl], vc_ref[:, gl]], axis=0)
            for hh in range(SWA_GROUP):
                h = grp * SWA_GROUP + hh
                hl = slice(h * LANES, (h + 1) * LANES)
                p, _ = _swa_probs(q_ref[:, hl], kk, bias_ref[h], sink_ref[0, h], valid)
                o_ref[:, hl] = _dot(p.astype(BF16), vv).astype(BF16)

    return pl.pallas_call(
        body, name="swa_fwd",
        grid=(s_len // blk,),
        in_specs=_swa_specs(s_len),
        out_specs=pl.BlockSpec((blk, N_HEADS * LANES), lambda n: (n, 0)),
        out_shape=jax.ShapeDtypeStruct((s_len, N_HEADS * LANES), BF16),
        scratch_shapes=[pltpu.VMEM((N_HEADS, blk, 2 * blk), F32)],
        compiler_params=_cparams("arbitrary"),
    )(tab, sinks, bkt, q, k, k, v, v)


def _swa_bwd(tab, sinks, bkt, q, k, v, do):
    s_len = q.shape[0]
    blk = SWA_BLOCK
    nb = s_len // blk
    kvw = SWA_KV_HEADS * LANES

    def body(tab_ref, sink_ref, bkt_ref, q_ref, kp_ref, kc_ref, vp_ref, vc_ref, do_ref,
             dq_ref, dk_ref, dv_ref, dtab_ref, dsink_ref, bias_ref, dbias_ref):
        n = pl.program_id(0)

        @pl.when(n == 0)
        def _():
            _swa_bias_into(bias_ref, bkt_ref, tab_ref)
            dbias_ref[...] = jnp.zeros_like(dbias_ref)
            dk_ref[...] = jnp.zeros_like(dk_ref)
            dv_ref[...] = jnp.zeros_like(dv_ref)
            dsink_ref[...] = jnp.zeros_like(dsink_ref)
            dtab_ref[...] = jnp.zeros_like(dtab_ref)

        valid = _swa_valid(n)
        cur_rows = pl.ds(pl.multiple_of(n * blk, blk), blk)
        prev_rows = pl.ds(pl.multiple_of(jnp.maximum(n - 1, 0) * blk, blk), blk)
        for grp in range(SWA_KV_HEADS):
            gl = slice(grp * LANES, (grp + 1) * LANES)
            kk = jnp.concatenate([kp_ref[:, gl], kc_ref[:, gl]], axis=0)
            vv = jnp.concatenate([vp_ref[:, gl], vc_ref[:, gl]], axis=0)
            dk_acc = jnp.zeros((2 * blk, LANES), F32)
            dv_acc = jnp.zeros((2 * blk, LANES), F32)
            for hh in range(SWA_GROUP):
                h = grp * SWA_GROUP + hh
                hl = slice(h * LANES, (h + 1) * LANES)
                qh = q_ref[:, hl]
                doh = do_ref[:, hl]
                p, ps = _swa_probs(qh, kk, bias_ref[h], sink_ref[0, h], valid)
                dp = _dot_nt(doh, vv)
                delta = jnp.sum(p * dp, axis=-1, keepdims=True)
                dl = p * (dp - delta)
                dsink_ref[h:h + 1, :] += jnp.broadcast_to(-jnp.sum(ps * delta, axis=0, keepdims=True), (1, LANES))
                dbias_ref[h] += dl
                dlb = dl.astype(BF16)
                dq_ref[:, hl] = (Q_SCALE * _dot(dlb, kk)).astype(BF16)
                dk_acc += Q_SCALE * _dot_tn(dlb, qh)
                dv_acc += _dot_tn(p.astype(BF16), doh)
            dk_ref[cur_rows, gl] += dk_acc[blk:]
            dv_ref[cur_rows, gl] += dv_acc[blk:]

            @pl.when(n > 0)
            def _():
                dk_ref[prev_rows, gl] += dk_acc[:blk]
                dv_ref[prev_rows, gl] += dv_acc[:blk]

        @pl.when(n == nb - 1)
        def _():
            bk = bkt_ref[...]
            lane = lax.broadcasted_iota(jnp.int32, (1, LANES), 1)
            for bucket in range(REL_BUCKETS):
                rowv = jnp.zeros((1, LANES), F32)
                for h in range(N_HEADS):
                    val = jnp.sum(jnp.where(bk == bucket, dbias_ref[h], 0.0), axis=1, keepdims=True)
                    val = jnp.sum(val, axis=0, keepdims=True)
                    rowv = jnp.where(lane == h, val, rowv)
                dtab_ref[bucket:bucket + 1, :] = rowv

    return pl.pallas_call(
        body, name="swa_bwd",
        grid=(nb,),
        in_specs=_swa_specs(s_len) + [pl.BlockSpec((blk, N_HEADS * LANES), lambda n: (n, 0))],
        out_specs=[pl.BlockSpec((blk, N_HEADS * LANES), lambda n: (n, 0)),
                   pl.BlockSpec((s_len, kvw), lambda n: (0, 0)), pl.BlockSpec((s_len, kvw), lambda n: (0, 0)),
                   pl.BlockSpec((REL_BUCKETS, LANES), lambda n: (0, 0)), pl.BlockSpec((N_HEADS, LANES), lambda n: (0, 0))],
        out_shape=[jax.ShapeDtypeStruct((s_len, N_HEADS * LANES), BF16),
                   jax.ShapeDtypeStruct((s_len, kvw), F32), jax.ShapeDtypeStruct((s_len, kvw), F32),
                   jax.ShapeDtypeStruct((REL_BUCKETS, LANES), F32), jax.ShapeDtypeStruct((N_HEADS, LANES), F32)],
        scratch_shapes=[pltpu.VMEM((N_HEADS, blk, 2 * blk), F32), pltpu.VMEM((N_HEADS, blk, 2 * blk), F32)],
        compiler_params=_cparams("arbitrary"),
    )(tab, sinks, bkt, q, k, k, v, v, do)


def _sb_terms(z, valid):
    zc = jnp.minimum(z, SB_LOGIT_CAP)
    lk = -jnp.log(1.0 + jnp.exp(zc))
    lsz = zc + lk
    return lsz, (lk if valid is None else jnp.where(valid, lk, 0.0))


def _bf16_parts(vals):
    parts, rest = [], vals
    for n in range(SB_SUM_PARTS):
        parts.append(rest.astype(BF16))
        if n + 1 < SB_SUM_PARTS:
            rest = rest - parts[-1].astype(F32)
    return parts[0] if len(parts) == 1 else jnp.concatenate(parts, axis=1)


def _row_sum_lanes(vals):
    return jnp.broadcast_to(jnp.sum(vals, axis=-1, keepdims=True), (vals.shape[0], LANES))


def _emit_skewed(*groups):
    for step in range(max(len(items) + len(stages) - 1 for items, stages in groups)):
        for items, stages in groups:
            for s, stage in enumerate(stages):
                if 0 <= step - s < len(items):
                    stage(items[step - s])


def _sb_items(edge):
    items = []
    for h in range(2):
        for r0 in range(0, SB_QUERIES, SB_ROWS):
            if edge is None or r0 >= (edge + 1) * SB_KEYS:
                items.append((h, r0, False))
            elif r0 + SB_ROWS - 1 > edge * SB_KEYS:
                items.append((h, r0, True))
    return items


def _sb_valid(w, edge):
    row = lax.broadcasted_iota(jnp.int32, (SB_ROWS, SB_KEYS), 0) + w[1]
    col = lax.broadcasted_iota(jnp.int32, (SB_ROWS, SB_KEYS), 1) + edge * SB_KEYS
    return col < row


def _sb_consts(tq, tk):
    low = lax.broadcasted_iota(jnp.int32, (tq, LANES), 1) < HEAD_DIM
    row = lax.broadcasted_iota(jnp.int32, (tk, tk), 0)
    col = lax.broadcasted_iota(jnp.int32, (tk, tk), 1)
    right = (row > col).astype(BF16)
    left = (row < col).astype(BF16)
    return low, jnp.concatenate([right] * SB_SUM_PARTS, axis=0), jnp.concatenate([left] * SB_SUM_PARTS, axis=0)


def _sb_fwd(q, kt, v):
    s_len = q.shape[0]
    tq, tk, tr = SB_QUERIES, SB_KEYS, SB_ROWS
    nk, ratio = s_len // tk, tq // tk
    assert nk <= LANES

    def body(q_ref, kt_ref, v_ref, o_ref, car_ref, c_ref, oacc_ref, logw_ref, lksum_ref):
        i = pl.program_id(1)
        qv = q_ref[...]
        low, tri2, _ = _sb_consts(tq, tk)
        lane = lax.broadcasted_iota(jnp.int32, (tr, LANES), 1)
        zero = jnp.zeros_like(qv)
        q_heads = (jnp.where(low, qv, zero), jnp.where(low, zero, qv))
        c_ref[...] = jnp.zeros_like(c_ref)
        oacc_ref[...] = jnp.zeros_like(oacc_ref)
        car_ref[...] = jnp.zeros_like(car_ref)

        def front(j, edge):
            ktv = kt_ref[0, j]
            slot = j % 2
            st = {}

            def s_logits(w):
                st[w, "z"] = _dot(q_heads[w[0]][w[1]:w[1] + tr], ktv)

            def s_terms(w):
                valid = _sb_valid(w, edge) if w[2] else None
                lsz, lk = _sb_terms(st.pop((w, "z")), valid)
                st[w, "parts"] = _bf16_parts(lk)
                st[w, "lsz"] = lsz if valid is None else jnp.where(valid, lsz, NEG_BIG)
                lksum_ref[slot, w[0], w[1]:w[1] + tr, :] = _row_sum_lanes(lk)

            def s_suffix(w):
                logw_ref[slot, w[0], w[1]:w[1] + tr, :] = st.pop((w, "lsz")) + _dot(st.pop((w, "parts")), tri2)

            return _sb_items(edge), [s_logits, s_terms, s_suffix]

        def back(j, edge):
            vv = v_ref[pl.ds(pl.multiple_of(j * tk, tk), tk), :]
            slot = j % 2
            st = {}

            def s_weights(w):
                h, rs = w[0], slice(w[1], w[1] + tr)
                c = c_ref[h, rs, :]
                st[w, "a"] = jnp.exp(logw_ref[slot, h, rs, :] + jnp.tile(c, (1, tk // LANES))).astype(BF16)
                car_ref[h, rs, :] = jnp.where(lane == j, c, car_ref[h, rs, :])
                c_ref[h, rs, :] = c + lksum_ref[slot, h, rs, :]

            def s_values(w):
                oacc_ref[w[0], w[1]:w[1] + tr, :] += _dot(st.pop((w, "a")), vv)

            return _sb_items(edge), [s_weights, s_values]

        first = i * ratio
        _emit_skewed(front(first + ratio - 1, ratio - 1))
        for m in reversed(range(ratio - 1)):
            _emit_skewed(front(first + m, m), back(first + m + 1, m + 1))

        @pl.when(i == 0)
        def _():
            _emit_skewed(back(0, 0))

        @pl.when(i > 0)
        def _():
            _emit_skewed(front(first - 1, None), back(first, 0))

            def step(jj, carry):
                _emit_skewed(front(first - jj, None), back(first - jj + 1, None))
                return carry

            lax.fori_loop(2, first + 1, step, 0)
            _emit_skewed(back(0, None))

        o_ref[...] = jnp.where(low, oacc_ref[0], oacc_ref[1]).astype(BF16)

    return pl.pallas_call(
        body, name="sb_fwd",
        grid=(N_HEADS // 2, s_len // tq),
        in_specs=[pl.BlockSpec((tq, LANES), lambda p, i: (i, p)),
                  pl.BlockSpec((1, nk, LANES, tk), lambda p, i: (p, 0, 0, 0)),
                  pl.BlockSpec((s_len, LANES), lambda p, i: (0, p))],
        out_specs=[pl.BlockSpec((tq, LANES), lambda p, i: (i, p)), pl.BlockSpec((2, tq, LANES), lambda p, i: (p, i, 0))],
        out_shape=[jax.ShapeDtypeStruct((s_len, N_HEADS * HEAD_DIM), BF16),
                   jax.ShapeDtypeStruct((N_HEADS, s_len, LANES), F32)],
        scratch_shapes=[pltpu.VMEM((2, tq, LANES), F32), pltpu.VMEM((2, tq, LANES), F32),
                        pltpu.VMEM((2, 2, tq, tk), F32), pltpu.VMEM((2, 2, tq, LANES), F32)],
        compiler_params=_cparams("parallel", "arbitrary"),
    )(q, kt, v)


def _sb_bwd(q, qt, kt, k, vt, do, dot, cars):
    s_len = q.shape[0]
    tq, tk, tr = SB_QUERIES, SB_KEYS, SB_ROWS
    nk, ratio = s_len // tk, tq // tk

    def body(q_ref, qt_ref, kt_ref, k_ref, vt_ref, do_ref, dot_ref, car_ref, dq_ref, dk_ref, dv_ref,
             gleft_ref, dqacc_ref, dkacc_ref, dvacc_ref, logw_ref, lsz_ref, da_ref, a_ref, dz_ref):
        i = pl.program_id(1)

        @pl.when(i == 0)
        def _():
            dkacc_ref[...] = jnp.zeros_like(dkacc_ref)
            dvacc_ref[...] = jnp.zeros_like(dvacc_ref)

        qv = q_ref[...]
        dov = do_ref[...]
        low, tri_right2, tri_left2 = _sb_consts(tq, tk)
        lane = lax.broadcasted_iota(jnp.int32, (tr, LANES), 1)
        zero = jnp.zeros_like(qv)
        q_heads = (jnp.where(low, qv, zero), jnp.where(low, zero, qv))
        do_heads = (jnp.where(low, dov, zero), jnp.where(low, zero, dov))
        q_t = qt_ref[0, 0]
        do_t = dot_ref[0, 0]
        gleft_ref[...] = jnp.zeros_like(gleft_ref)
        dqacc_ref[...] = jnp.zeros_like(dqacc_ref)

        def front(j, edge):
            ktv = kt_ref[0, j]
            vtv = vt_ref[0, j]
            slot = j % 2
            st = {}

            def s_logits(w):
                h, rs = w[0], slice(w[1], w[1] + tr)
                st[w, "z"] = _dot(q_heads[h][rs], ktv)
                da_ref[slot, h, rs, :] = _dot(do_heads[h][rs], vtv)

            def s_terms(w):
                h, rs = w[0], slice(w[1], w[1] + tr)
                valid = _sb_valid(w, edge) if w[2] else None
                lsz, lk = _sb_terms(st.pop((w, "z")), valid)
                st[w, "parts"] = _bf16_parts(lk)
                lsz = lsz if valid is None else jnp.where(valid, lsz, NEG_BIG)
                lsz_ref[slot, h, rs, :] = lsz
                st[w, "lszc"] = lsz + jnp.sum(jnp.where(lane == j, car_ref[h, rs, :], 0.0), axis=-1, keepdims=True)

            def s_suffix(w):
                logw_ref[slot, w[0], w[1]:w[1] + tr, :] = st.pop((w, "lszc")) + _dot(st.pop((w, "parts")), tri_right2)

            return _sb_items(edge), [s_logits, s_terms, s_suffix]

        def back(j, edge):
            kv = k_ref[pl.ds(pl.multiple_of(j * tk, tk), tk), :]
            slot = j % 2
            st = {}

            items = _sb_items(edge)
            head_rows = [[r0 for hh, r0, _ in items if hh == h] for h in range(2)]

            def s_weights(w):
                h, rs = w[0], slice(w[1], w[1] + tr)
                a = jnp.exp(logw_ref[slot, h, rs, :])
                g = a * da_ref[slot, h, rs, :]
                a_ref[h, rs, :] = a.astype(BF16)
                st[w, "g"], st[w, "parts"] = g, _bf16_parts(g)

            def s_prefix(w):
                st[w, "gs"] = _dot(st.pop((w, "parts")), tri_left2)

            def s_dz(w):
                h, rs = w[0], slice(w[1], w[1] + tr)
                g = st.pop((w, "g"))
                gleft = gleft_ref[h, rs, :]
                gsum = st.pop((w, "gs")) + jnp.tile(gleft, (1, tk // LANES))
                dz = (g - jnp.exp(lsz_ref[slot, h, rs, :]) * (g + gsum)).astype(BF16)
                st[w, "dz"] = dz
                dz_ref[h, rs, :] = dz
                gleft_ref[h, rs, :] = gleft + _row_sum_lanes(g)

            def s_products(w):
                h, rs = w[0], slice(w[1], w[1] + tr)
                dqacc_ref[h, rs, :] += _dot(st.pop((w, "dz")), kv)
                if w[1] == head_rows[h][-1]:
                    feat = slice(h * HEAD_DIM, (h + 1) * HEAD_DIM)
                    hr = slice(head_rows[h][0], tq)
                    dkacc_ref[j, feat, :] += _dot(q_t[feat, hr], dz_ref[h, hr, :])
                    dvacc_ref[j, feat, :] += _dot(do_t[feat, hr], a_ref[h, hr, :])

            return items, [s_weights, s_prefix, s_dz, s_products]

        first = i * ratio

        @pl.when(i == 0)
        def _():
            _emit_skewed(front(0, 0))

        @pl.when(i > 0)
        def _():
            _emit_skewed(front(0, None))

            def step(jj, carry):
                _emit_skewed(front(jj, None), back(jj - 1, None))
                return carry

            lax.fori_loop(1, first, step, 0)
            _emit_skewed(front(first, 0), back(first - 1, None))

        for m in range(1, ratio):
            _emit_skewed(front(first + m, m), back(first + m - 1, m - 1))
        _emit_skewed(back(first + ratio - 1, ratio - 1))
        dq_ref[...] = (Q_SCALE * jnp.where(low, dqacc_ref[0], dqacc_ref[1])).astype(BF16)

        @pl.when(i == s_len // tq - 1)
        def _():
            dk_ref[0] = dkacc_ref[...].astype(BF16)
            dv_ref[0] = dvacc_ref[...].astype(BF16)

    qblk = pl.BlockSpec((tq, LANES), lambda p, i: (i, p))
    qtblk = pl.BlockSpec((1, 1, LANES, tq), lambda p, i: (p, i, 0, 0))
    tblk = pl.BlockSpec((1, nk, LANES, tk), lambda p, i: (p, 0, 0, 0))
    col_full = pl.BlockSpec((s_len, LANES), lambda p, i: (0, p))
    tshape = jax.ShapeDtypeStruct((N_HEADS // 2, nk, LANES, tk), BF16)
    return pl.pallas_call(
        body, name="sb_bwd",
        grid=(N_HEADS // 2, s_len // tq),
        in_specs=[qblk, qtblk, tblk, col_full, tblk, qblk, qtblk, pl.BlockSpec((2, tq, LANES), lambda p, i: (p, i, 0))],
        out_specs=[qblk, tblk, tblk],
        out_shape=[jax.ShapeDtypeStruct((s_len, N_HEADS * HEAD_DIM), BF16), tshape, tshape],
        scratch_shapes=[pltpu.VMEM((2, tq, LANES), F32), pltpu.VMEM((2, tq, LANES), F32),
                        pltpu.VMEM((nk, LANES, tk), F32), pltpu.VMEM((nk, LANES, tk), F32)]
        + [pltpu.VMEM((2, 2, tq, tk), F32)] * 3 + [pltpu.VMEM((2, tq, tk), BF16)] * 2,
        compiler_params=_cparams("parallel", "arbitrary"),
    )(q, qt, kt, k, vt, do, dot, cars)


def _pad_heads(a, heads):
    s_len = a.shape[0]
    a = a.reshape(s_len, heads, HEAD_DIM)
    return jnp.pad(a, ((0, 0), (0, 0), (0, LANES - HEAD_DIM))).reshape(s_len, heads * LANES)


def _unpad_heads(a, heads):
    s_len = a.shape[0]
    return a.reshape(s_len, heads, LANES)[:, :, :HEAD_DIM].reshape(s_len, heads * HEAD_DIM)


def _tile_transposed(a, groups, t):
    s_len = a.shape[0]
    return a.reshape(s_len // t, t, groups, LANES).transpose(2, 0, 3, 1)


def _tile_untransposed(a):
    groups, nt, _, t = a.shape
    return a.transpose(1, 3, 0, 2).reshape(nt * t, groups * LANES)


def _local_step(xs, tgt, gains, sinks, rel_bias, weights_of, send_early):
    g1, gmix, g2, gfin = gains
    bkt = _rel_bucket_matrix()
    groups = N_HEADS // 2

    wts = dict(weights_of(0, xs))
    x1, h1, a1, b1, u1 = _ffn_fwd(xs, g1, wts["ffn1_w1t"], wts["ffn1_w3t"], wts["ffn1_w2"], "1")
    wts.update(weights_of(1, x1))
    hm, qa, ka, va, qb, kb, vb, ga, gb = _proj_fwd(x1, gmix, wts["w_int"])
    qa_p, ka_p, va_p = _pad_heads(qa, N_HEADS), _pad_heads(ka, SWA_KV_HEADS), _pad_heads(va, SWA_KV_HEADS)
    oa_p = _swa_fwd(rel_bias, sinks, bkt, qa_p, ka_p, va_p)
    kbt = _tile_transposed(kb, groups, SB_KEYS)
    ob, cars = _sb_fwd(qb, kbt, vb)
    oa = _unpad_heads(oa_p, N_HEADS)
    x2, mg = _merge_fwd(x1, oa, ob, ga, gb, wts["w_swa"], wts["w_sb"], wts["w_out"])
    wts.update(weights_of(2, x2))
    x3, h3, a3, b3, u3 = _ffn_fwd(x2, g2, wts["ffn2_w1t"], wts["ffn2_w3t"], wts["ffn2_w2"], "2")
    dx3, loss, dgfin = _loss_fwd_bwd(x3, tgt, gfin)

    big = {}
    dx2, dg2, da3, db3, dx3b = _ffn_bwd(dx3, x2, g2, a3, b3, wts["ffn2_w1t"], wts["ffn2_w3t"], wts["ffn2_w2"], "2")
    big["ffn2_w1t"] = _matmul_tn(da3, h3, "ffn2_w1")
    big["ffn2_w3t"] = _matmul_tn(db3, h3, "ffn2_w3")
    big["ffn2_w2"] = _matmul_tn(u3, dx3b, "ffn2_w2")
    dep = send_early(2, big)

    doa, dob, dga, dgb, dpa, dpb, dx2b = _merge_bwd(dx2, oa, ob, ga, gb, wts["w_swa"], wts["w_sb"], wts["w_out"], dep)
    big = {}
    big["w_out"] = _matmul_tn(mg, dx2b, "w_out")
    big["w_swa"] = _matmul_tn(oa, dpa, "w_swa")
    big["w_sb"] = _matmul_tn(ob, dpb, "w_sb")

    dqa_p, dka_p, dva_p, dtab, dsink = _swa_bwd(rel_bias, sinks, bkt, qa_p, ka_p, va_p, _pad_heads(doa, N_HEADS))
    dqb, dkbt, dvbt = _sb_bwd(qb, _tile_transposed(qb, groups, SB_QUERIES), kbt, kb,
                              _tile_transposed(vb, groups, SB_KEYS), dob, _tile_transposed(dob, groups, SB_QUERIES), cars)
    dkb, dvb = _tile_untransposed(dkbt), _tile_untransposed(dvbt)
    dpieces = (_unpad_heads(dqa_p, N_HEADS), _unpad_heads(dka_p, SWA_KV_HEADS).astype(BF16),
               _unpad_heads(dva_p, SWA_KV_HEADS).astype(BF16), dqb, dkb, dvb, dga, dgb)
    big["w_int"] = jnp.concatenate([_matmul_tn(dp, hm, f"w_in{p}") for p, dp in enumerate(dpieces)], axis=0)
    dep = send_early(1, big)
    dx1, dgmix = _proj_bwd(dpieces, dx2, x1, gmix, wts["w_int"], dep)
    big = {}

    dx0, dg1, da1, db1, dx1b = _ffn_bwd(dx1, xs, g1, a1, b1, wts["ffn1_w1t"], wts["ffn1_w3t"], wts["ffn1_w2"], "1")
    big["ffn1_w1t"] = _matmul_tn(da1, h1, "ffn1_w1")
    big["ffn1_w3t"] = _matmul_tn(db1, h1, "ffn1_w3")
    big["ffn1_w2"] = _matmul_tn(u1, dx1b, "ffn1_w2")

    small = {"gains": (dg1, dgmix, dg2, dgfin), "sinks": dsink[:, 0], "rel_bias": dtab[:, :N_HEADS]}
    return loss, dx0, big, small


def _my_place():
    return lax.axis_index("x"), lax.axis_index("y"), lax.axis_index("c")


def _flip(v, bit):
    return 1 - v if bit else v


_RELATIONS = tuple((k >> 2 & 1, k >> 1 & 1, k & 1) for k in range(1, N_DEV))


def _gather_weights(wp):
    def body(x_ref, out_ref, send_sems, recv_sems, local_sem):
        x, y, c = _my_place()
        me, sibling = (x, y, c), (x, y, 1 - c)
        chips = [(1 - x, y), (x, 1 - y), (1 - x, 1 - y)]

        def rows(px, py, pc):
            return out_ref.at[4 * px + 2 * py + pc]

        def copy(k, block, to, src=None):
            return pltpu.make_async_remote_copy(
                src_ref=rows(*block) if src is None else src, dst_ref=rows(*block),
                send_sem=send_sems.at[k], recv_sem=recv_sems.at[k],
                device_id=to, device_id_type=pl.DeviceIdType.MESH)

        mine = pltpu.make_async_copy(x_ref, rows(*me), local_sem)
        mine.start()
        first = [copy(0, me, sibling, src=x_ref)]
        first += [copy(1 + j, me, (*chip, c), src=x_ref) for j, chip in enumerate(chips)]
        for cp in first:
            cp.start()
        passed = [copy(4 + j, (*chip, c), sibling) for j, chip in enumerate(chips)]
        for j, chip in enumerate(chips):
            copy(1 + j, (*chip, c), me).wait_recv()
            passed[j].start()
        copy(0, sibling, me).wait_recv()
        for j, chip in enumerate(chips):
            copy(4 + j, (*chip, 1 - c), me).wait_recv()
        for cp in first + passed:
            cp.wait_send()
        mine.wait()

    return pl.pallas_call(
        body, name="gather_weights",
        out_shape=jax.ShapeDtypeStruct((N_DEV,) + wp.shape, wp.dtype),
        in_specs=[pl.BlockSpec(memory_space=pl.ANY)],
        out_specs=pl.BlockSpec(memory_space=pl.ANY),
        scratch_shapes=[pltpu.SemaphoreType.DMA((7,)), pltpu.SemaphoreType.DMA((7,)), pltpu.SemaphoreType.DMA(())],
    )(wp)


def _exchange_grads(gp):
    def body(g_ref, out_ref, send_sems, recv_sems, local_sem):
        x, y, c = _my_place()
        me = 4 * x + 2 * y + c
        mine = pltpu.make_async_copy(g_ref.at[me], out_ref.at[me], local_sem)
        mine.start()
        copies = []
        for k, (fx, fy, fc) in enumerate(_RELATIONS):
            px, py, pc = _flip(x, fx), _flip(y, fy), _flip(c, fc)
            peer = 4 * px + 2 * py + pc
            copies.append((
                pltpu.make_async_remote_copy(
                    src_ref=g_ref.at[peer], dst_ref=out_ref.at[me], send_sem=send_sems.at[k], recv_sem=recv_sems.at[k],
                    device_id=(px, py, pc), device_id_type=pl.DeviceIdType.MESH),
                pltpu.make_async_remote_copy(
                    src_ref=g_ref.at[peer], dst_ref=out_ref.at[peer], send_sem=send_sems.at[k], recv_sem=recv_sems.at[k],
                    device_id=(px, py, pc), device_id_type=pl.DeviceIdType.MESH)))
        for out_cp, _ in copies:
            out_cp.start()
        for _, in_cp in copies:
            in_cp.wait_recv()
        for out_cp, _ in copies:
            out_cp.wait_send()
        mine.wait()

    return pl.pallas_call(
        body, name="exchange_grads",
        out_shape=jax.ShapeDtypeStruct(gp.shape, gp.dtype),
        in_specs=[pl.BlockSpec(memory_space=pl.ANY)],
        out_specs=pl.BlockSpec(memory_space=pl.ANY),
        scratch_shapes=[pltpu.SemaphoreType.DMA((7,)), pltpu.SemaphoreType.DMA((7,)), pltpu.SemaphoreType.DMA(())],
    )(gp)


_HBM = pl.BlockSpec(memory_space=pltpu.HBM)
_SEM = pl.BlockSpec(memory_space=pltpu.SEMAPHORE)
_EFFECT = pltpu.SideEffectType.DATAFLOW_SIDE_EFFECTING


def _peers():
    x, y, c = _my_place()
    out = []
    for k, (fx, fy, fc) in enumerate(_RELATIONS):
        px, py, pc = _flip(x, fx), _flip(y, fy), _flip(c, fc)
        out.append((k, (px, py, pc), 4 * px + 2 * py + pc))
    return out, 4 * x + 2 * y + c


def _send_start(src, after, per_peer, tag):
    land_shape = src.shape if per_peer else (N_DEV,) + src.shape

    def body(s_ref, land_ref, after_ref, send_sems, recv_sems, s_thru, land_thru, token):
        peers, me = _peers()
        for k, where, slab in peers:
            pltpu.make_async_remote_copy(
                src_ref=s_ref.at[slab] if per_peer else s_ref, dst_ref=land_ref.at[me],
                send_sem=send_sems.at[k], recv_sem=recv_sems.at[k],
                device_id=where, device_id_type=pl.DeviceIdType.MESH).start()
        token[...] = jnp.zeros_like(token)

    return pl.pallas_call(
        body, name=f"send_start_{tag}",
        out_shape=(pltpu.SemaphoreType.DMA((N_DEV - 1,)), pltpu.SemaphoreType.DMA((N_DEV - 1,)),
                   pltpu.HBM(src.shape, src.dtype), pltpu.HBM(land_shape, src.dtype), jax.ShapeDtypeStruct((8, LANES), F32)),
        in_specs=(_HBM, _HBM, pl.BlockSpec(memory_space=pl.ANY)),
        out_specs=(_SEM, _SEM, _HBM, _HBM, pl.BlockSpec(memory_space=pltpu.VMEM)),
        input_output_aliases={0: 2, 1: 3},
        compiler_params=pltpu.CompilerParams(has_side_effects=_EFFECT),
    )(pltpu.with_memory_space_constraint(src, pltpu.HBM),
      pltpu.with_memory_space_constraint(lax.empty(land_shape, src.dtype), pltpu.HBM), after)


def _send_wait(handles, after, per_peer, tag):
    send_sems, recv_sems, s_thru, land_thru, _ = handles

    def body(s_ref, land_ref, send_sems, recv_sems, after_ref, s_out, land_out):
        peers, _ = _peers()
        for k, where, slab in peers:
            copy = pltpu.make_async_remote_copy(
                src_ref=s_ref.at[slab] if per_peer else s_ref, dst_ref=land_ref.at[slab],
                send_sem=send_sems.at[k], recv_sem=recv_sems.at[k],
                device_id=where, device_id_type=pl.DeviceIdType.MESH)
            copy.wait_send()
            copy.wait_recv()

    sent, landed = pl.pallas_call(
        body, name=f"send_wait_{tag}",
        out_shape=(pltpu.HBM(s_thru.shape, s_thru.dtype), pltpu.HBM(land_thru.shape, land_thru.dtype)),
        in_specs=(_HBM, _HBM, _SEM, _SEM, pl.BlockSpec(memory_space=pl.ANY)), out_specs=(_HBM, _HBM),
        input_output_aliases={0: 0, 1: 1},
        compiler_params=pltpu.CompilerParams(has_side_effects=_EFFECT),
    )(s_thru, land_thru, send_sems, recv_sems, after)
    me = 4 * lax.axis_index("x") + 2 * lax.axis_index("y") + lax.axis_index("c")
    own = lax.dynamic_slice_in_dim(sent, me, 1, axis=0) if per_peer else sent[None]
    return lax.dynamic_update_slice_in_dim(landed, own, me, axis=0)


def _adamw(w, g, m, v):
    m = ADAM_B1 * m + (1.0 - ADAM_B1) * g
    v = ADAM_B2 * v + (1.0 - ADAM_B2) * jnp.square(g)
    m_hat = m / (1.0 - ADAM_B1 ** ADAM_STEP)
    v_hat = v / (1.0 - ADAM_B2 ** ADAM_STEP)
    delta = -ADAM_LR * (m_hat / (jnp.sqrt(v_hat) + ADAM_EPS) + ADAM_WD * w)
    return delta, m, v


def _sum_and_adamw(parts, w, m, v, tr, tag):
    rows = w.shape[0]
    assert rows % tr == 0

    def body(p_ref, w_ref, m_ref, v_ref, g_out, d_out, m_out, v_out):
        g = p_ref[0].astype(F32)
        for d in range(1, N_DEV):
            g = g + p_ref[d].astype(F32)
        delta, mn, vn = _adamw(w_ref[...], g, m_ref[...], v_ref[...])
        g_out[...] = g
        d_out[...] = delta
        m_out[...] = mn
        v_out[...] = vn

    sp = pl.BlockSpec((tr, D_MODEL), lambda i: (i, 0))
    return pl.pallas_call(
        body, name=f"sum_and_adamw_{tag}",
        grid=(rows // tr,),
        in_specs=[pl.BlockSpec((N_DEV, tr, D_MODEL), lambda i: (0, i, 0)), sp, sp, sp],
        out_specs=[sp] * 4,
        out_shape=[jax.ShapeDtypeStruct(w.shape, F32)] * 4,
        compiler_params=_cparams("parallel"),
    )(parts, w, m, v)


def _small_allreduce_adamw(part, w, m, v):
    def body(p_ref, w_ref, m_ref, v_ref, g_out, d_out, m_out, v_out, buf, send_sems, recv_sems):
        x, y, c = _my_place()
        me = 4 * x + 2 * y + c
        buf[me] = p_ref[...]
        copies = []
        for k, (fx, fy, fc) in enumerate(_RELATIONS):
            px, py, pc = _flip(x, fx), _flip(y, fy), _flip(c, fc)
            peer = 4 * px + 2 * py + pc
            copies.append((
                pltpu.make_async_remote_copy(
                    src_ref=buf.at[me], dst_ref=buf.at[me], send_sem=send_sems.at[k], recv_sem=recv_sems.at[k],
                    device_id=(px, py, pc), device_id_type=pl.DeviceIdType.MESH),
                pltpu.make_async_remote_copy(
                    src_ref=buf.at[me], dst_ref=buf.at[peer], send_sem=send_sems.at[k], recv_sem=recv_sems.at[k],
                    device_id=(px, py, pc), device_id_type=pl.DeviceIdType.MESH)))
        for out_cp, _ in copies:
            out_cp.start()
        for _, in_cp in copies:
            in_cp.wait_recv()
        for out_cp, _ in copies:
            out_cp.wait_send()
        g = buf[0]
        for d in range(1, N_DEV):
            g = g + buf[d]
        delta, mn, vn = _adamw(w_ref[...], g, m_ref[...], v_ref[...])
        g_out[...] = g
        d_out[...] = delta
        m_out[...] = mn
        v_out[...] = vn

    vm = pl.BlockSpec(memory_space=pltpu.VMEM)
    return pl.pallas_call(
        body, name="small_allreduce_adamw",
        in_specs=[vm] * 4, out_specs=[vm] * 4,
        out_shape=[jax.ShapeDtypeStruct(w.shape, F32)] * 4,
        scratch_shapes=[pltpu.VMEM((N_DEV,) + part.shape, F32),
                        pltpu.SemaphoreType.DMA((7,)), pltpu.SemaphoreType.DMA((7,))],
    )(part, w, m, v)


_TRANSPOSED = ("ffn1_w1", "ffn1_w3", "w_in", "ffn2_w1", "ffn2_w3")
_BRANCH = ("w_branch_swa", "w_branch_sb")


def _pack_shards(t, names):
    parts = []
    for name in names:
        a = t[name][0]
        if name in _TRANSPOSED:
            a = a.T
        elif name in _BRANCH:
            a = a.reshape(64, D_MODEL)
        parts.append(a)
    return jnp.concatenate(parts, axis=0)


def _unpack_shards(p, names):
    out, lo = {}, 0
    for name in names:
        a = p[lo:lo + BIG_ROWS[BIG_NAMES.index(name)]]
        lo += a.shape[0]
        if name in _TRANSPOSED:
            a = a.T
        elif name in _BRANCH:
            a = a.reshape(512, 128)
        out[name] = a[None]
    return out


def _full_weights(wg, names):
    out, lo = {}, 0
    for name in names:
        rows = BIG_ROWS[BIG_NAMES.index(name)]
        a = wg[:, lo:lo + rows]
        lo += rows
        if name in _BRANCH:
            a = a.reshape(N_DEV, 512, 128).transpose(1, 0, 2).reshape(512, D_MODEL)
        out[_GRAD_KEY[name]] = a.reshape(-1, D_MODEL)
    return out


_GRAD_KEY = {"ffn1_w1": "ffn1_w1t", "ffn1_w3": "ffn1_w3t", "ffn1_w2": "ffn1_w2", "w_in": "w_int",
             "w_branch_swa": "w_swa", "w_branch_sb": "w_sb", "w_out": "w_out",
             "ffn2_w1": "ffn2_w1t", "ffn2_w3": "ffn2_w3t", "ffn2_w2": "ffn2_w2"}


def _pack_full_grads(big, names):
    parts = []
    for name in names:
        a = big[_GRAD_KEY[name]]
        if name in _BRANCH:
            a = a.reshape(512, N_DEV, 128).transpose(1, 0, 2)
        parts.append(a.reshape(N_DEV, BIG_ROWS[BIG_NAMES.index(name)], D_MODEL).astype(BF16))
    return jnp.concatenate(parts, axis=1)


_SMALL_NAMES = ("norm_ffn1", "norm_mix", "norm_ffn2", "norm_final", "swa_sinks", "rel_bias")


def _pack_small(vals):
    rows = []
    for a in vals:
        a = a.reshape(-1)
        rows.append(jnp.pad(a, (0, D_MODEL - a.shape[0])))
    rows += [jnp.zeros((D_MODEL,), F32)] * (SMALL_ROWS - len(rows))
    return jnp.stack(rows)


def _unpack_small(p):
    return {"norm_ffn1": p[0:1], "norm_mix": p[1:2], "norm_ffn2": p[2:3], "norm_final": p[3],
            "swa_sinks": p[4:5, :N_HEADS], "rel_bias": p[5, :REL_BUCKETS * N_HEADS].reshape(REL_BUCKETS, N_HEADS)}


ALL_NAMES = ("norm_ffn1", "ffn1_w1", "ffn1_w3", "ffn1_w2", "norm_mix", "w_in", "swa_sinks", "rel_bias",
             "w_branch_swa", "w_branch_sb", "w_out", "norm_ffn2", "ffn2_w1", "ffn2_w3", "ffn2_w2", "norm_final")


def kernel(x, norm_ffn1, ffn1_w1, ffn1_w3, ffn1_w2, norm_mix, w_in, swa_sinks, rel_bias, w_branch_swa, w_branch_sb, w_out, norm_ffn2, ffn2_w1, ffn2_w3, ffn2_w2, norm_final, loss_target, m_norm_ffn1, m_ffn1_w1, m_ffn1_w3, m_ffn1_w2, m_norm_mix, m_w_in, m_swa_sinks, m_rel_bias, m_w_branch_swa, m_w_branch_sb, m_w_out, m_norm_ffn2, m_ffn2_w1, m_ffn2_w3, m_ffn2_w2, m_norm_final, v_norm_ffn1, v_ffn1_w1, v_ffn1_w3, v_ffn1_w2, v_norm_mix, v_w_in, v_swa_sinks, v_rel_bias, v_w_branch_swa, v_w_branch_sb, v_w_out, v_norm_ffn2, v_ffn2_w1, v_ffn2_w3, v_ffn2_w2, v_norm_final):
    w = dict(zip(ALL_NAMES, (norm_ffn1, ffn1_w1, ffn1_w3, ffn1_w2, norm_mix, w_in, swa_sinks, rel_bias,
                             w_branch_swa, w_branch_sb, w_out, norm_ffn2, ffn2_w1, ffn2_w3, ffn2_w2, norm_final)))
    m = dict(zip(ALL_NAMES, (m_norm_ffn1, m_ffn1_w1, m_ffn1_w3, m_ffn1_w2, m_norm_mix, m_w_in, m_swa_sinks, m_rel_bias,
                             m_w_branch_swa, m_w_branch_sb, m_w_out, m_norm_ffn2, m_ffn2_w1, m_ffn2_w3, m_ffn2_w2,
                             m_norm_final)))
    v = dict(zip(ALL_NAMES, (v_norm_ffn1, v_ffn1_w1, v_ffn1_w3, v_ffn1_w2, v_norm_mix, v_w_in, v_swa_sinks, v_rel_bias,
                             v_w_branch_swa, v_w_branch_sb, v_w_out, v_norm_ffn2, v_ffn2_w1, v_ffn2_w3, v_ffn2_w2,
                             v_norm_final)))

    w_packed = [_pack_shards(w, names) for names in GROUPS]
    gathered0 = _gather_weights(w_packed[0].astype(BF16))
    w_flight = {1: _send_start(w_packed[1].astype(BF16), gathered0, False, "w1")}
    w_flight[2] = _send_start(w_packed[2].astype(BF16), w_flight[1][4], False, "w2")

    def weights_of(group, after):
        wg = gathered0 if group == 0 else _send_wait(w_flight[group], after, False, f"w{group}")
        return _full_weights(wg, GROUPS[group])

    g_flight = {}

    def send_early(group, grads):
        gp = _pack_full_grads(grads, GROUPS[group])
        g_flight[group] = _send_start(gp, next(iter(grads.values())), True, f"g{group}")
        return g_flight[group][4]

    gains = (norm_ffn1 + w_flight[2][4][:1, :1], norm_mix, norm_ffn2, norm_final.reshape(1, D_MODEL))
    loss, dx, grads0, small = _local_step(x[0], loss_target[0], gains, swa_sinks, rel_bias, weights_of, send_early)
    parts = {0: _exchange_grads(_pack_full_grads(grads0, GROUPS[0]))}
    for group in (1, 2):
        parts[group] = _send_wait(g_flight[group], parts[group - 1], True, f"g{group}")

    big_outs = [{}, {}, {}, {}]
    for group, names in enumerate(GROUPS):
        res = _sum_and_adamw(parts[group], w_packed[group], _pack_shards(m, names), _pack_shards(v, names),
                             GROUP_TILE[group], f"group{group}")
        for acc, packed in zip(big_outs, res):
            acc.update(_unpack_shards(packed, names))
    g_big, d_big, m_big, v_big = big_outs

    small_part = _pack_small(small["gains"] + (small["sinks"], small["rel_bias"]))
    g_sm, d_sm, m_sm, v_sm = (_unpack_small(p) for p in _small_allreduce_adamw(
        small_part, _pack_small([w[n] for n in _SMALL_NAMES]), _pack_small([m[n] for n in _SMALL_NAMES]),
        _pack_small([v[n] for n in _SMALL_NAMES])))

    total_loss = lax.psum(loss[0, 0], AXES)
    outs = [total_loss, dx[None]]
    for big_d, small_d in ((g_big, g_sm), (d_big, d_sm), (m_big, m_sm), (v_big, v_sm)):
        merged = {**big_d, **small_d}
        outs += [merged[n] for n in ALL_NAMES]
    return tuple(outs)
```

```python
import jax
import jax.numpy as jnp
import numpy as np
from jax import lax
from jax.experimental import pallas as pl
from jax.experimental.pallas import tpu as pltpu

F32 = jnp.float32
BF16 = jnp.bfloat16

D_MODEL = 1024
D_FF = 2816
HEAD_DIM = 64
N_HEADS = 8
SWA_KV_HEADS = 2
SWA_GROUP = 4
SWA_BLOCK = 128
REL_BUCKETS = 32
REL_MAX_DIST = 128
RMS_EPS = 1e-6
NEG_BIG = -1e30
Q_SCALE = HEAD_DIM ** -0.5
LANES = 128

N_DEV = 8
AXES = ("x", "y", "c")

ADAM_LR = 0.001
ADAM_B1 = 0.9
ADAM_B2 = 0.999
ADAM_EPS = 1e-08
ADAM_WD = 0.01
ADAM_STEP = 10

IN_SIZES = (512, 128, 128, 512, 512, 512, 1024, 1024)
IN_OFFS = tuple(int(v) for v in np.cumsum((0,) + IN_SIZES))
IN_W = IN_OFFS[-1]

BIG_NAMES = ("ffn1_w1", "ffn1_w3", "ffn1_w2", "w_in", "w_branch_swa", "w_branch_sb", "w_out",
             "ffn2_w1", "ffn2_w3", "ffn2_w2")
BIG_ROWS = (352, 352, 352, 544, 64, 64, 128, 352, 352, 352)
BIG_OFFS = tuple(int(v) for v in np.cumsum((0,) + BIG_ROWS))
PACK_ROWS = BIG_OFFS[-1]
SMALL_ROWS = 8
GROUPS = (BIG_NAMES[0:3], BIG_NAMES[3:7], BIG_NAMES[7:10])
GROUP_TILE = (96, 160, 96)

VMEM_LIMIT = 56 * 1024 * 1024
FFN_PIECES = 2
SB_QUERIES = 512
SB_KEYS = 256
SB_ROWS = 256
SB_SUM_PARTS = 1
SB_LOGIT_CAP = 80.0


def _dot(a, b):
    return jnp.dot(a, b, preferred_element_type=F32)


def _dot_nt(a, b):
    return lax.dot_general(a, b, (((1,), (1,)), ((), ())), preferred_element_type=F32)


def _dot_tn(a, b):
    return lax.dot_general(a, b, (((0,), (0,)), ((), ())), preferred_element_type=F32)


def _cparams(*sem):
    return pltpu.CompilerParams(dimension_semantics=sem, vmem_limit_bytes=VMEM_LIMIT)


def _rms_rstd(xv):
    return lax.rsqrt(jnp.mean(xv * xv, axis=-1, keepdims=True) + RMS_EPS)


def _rms_bwd(dh, xv, r, g):
    xhat = xv * r
    dg = jnp.sum(dh * xhat, axis=0, keepdims=True)
    dxn = dh * g
    dx = r * (dxn - xhat * jnp.mean(dxn * xhat, axis=-1, keepdims=True))
    return dx, dg


def _ffn_fwd(x, g, w1t, w3t, w2, tag):
    s_len = x.shape[0]
    tm, tf = min(1024, s_len), 256
    nf = D_FF // tf

    def body(x_ref, g_ref, w1_ref, w3_ref, w2_ref, xo_ref, h_ref, a_ref, b_ref, u_ref, acc_ref, hs_ref):
        j = pl.program_id(1)

        @pl.when(j == 0)
        def _():
            xv = x_ref[...]
            h = (xv * _rms_rstd(xv) * g_ref[...]).astype(BF16)
            hs_ref[...] = h
            h_ref[...] = h
            acc_ref[...] = jnp.zeros_like(acc_ref)

        st = {}

        def s_up(rs):
            h = hs_ref[rs, :]
            st[rs.start, "ab"] = (_dot_nt(h, w1_ref[...]), _dot_nt(h, w3_ref[...]))

        def s_act(rs):
            a, b = st.pop((rs.start, "ab"))
            a_ref[rs, :] = a.astype(BF16)
            b_ref[rs, :] = b.astype(BF16)
            uh = (0.5 * (a * jax.nn.sigmoid(a) * b)).astype(BF16)
            u_ref[rs, :] = uh
            st[rs.start, "u"] = uh

        def s_down(rs):
            acc_ref[rs, :] += _dot(st.pop((rs.start, "u")), w2_ref[...])

        _emit_skewed(([slice(r, r + tm // FFN_PIECES) for r in range(0, tm, tm // FFN_PIECES)], [s_up, s_act, s_down]))

        @pl.when(j == nf - 1)
        def _():
            xo_ref[...] = x_ref[...] + acc_ref[...]

    row = lambda i, j: (i, 0)
    return pl.pallas_call(
        body, name=f"ffn_fwd_{tag}",
        grid=(s_len // tm, nf),
        in_specs=[pl.BlockSpec((tm, D_MODEL), row), pl.BlockSpec((1, D_MODEL), lambda i, j: (0, 0)),
                  pl.BlockSpec((tf, D_MODEL), lambda i, j: (j, 0)), pl.BlockSpec((tf, D_MODEL), lambda i, j: (j, 0)),
                  pl.BlockSpec((tf, D_MODEL), lambda i, j: (j, 0))],
        out_specs=[pl.BlockSpec((tm, D_MODEL), row), pl.BlockSpec((tm, D_MODEL), row),
                   pl.BlockSpec((tm, tf), lambda i, j: (i, j)), pl.BlockSpec((tm, tf), lambda i, j: (i, j)),
                   pl.BlockSpec((tm, tf), lambda i, j: (i, j))],
        out_shape=[jax.ShapeDtypeStruct((s_len, D_MODEL), F32), jax.ShapeDtypeStruct((s_len, D_MODEL), BF16),
                   jax.ShapeDtypeStruct((s_len, D_FF), BF16), jax.ShapeDtypeStruct((s_len, D_FF), BF16),
                   jax.ShapeDtypeStruct((s_len, D_FF), BF16)],
        scratch_shapes=[pltpu.VMEM((tm, D_MODEL), F32), pltpu.VMEM((tm, D_MODEL), BF16)],
        compiler_params=_cparams("parallel", "arbitrary"),
    )(x, g, w1t, w3t, w2)


def _ffn_bwd(dy, x, g, a, b, w1t, w3t, w2, tag):
    s_len = x.shape[0]
    tm, tf = min(1024, s_len), 256
    nf = D_FF // tf

    def body(dy_ref, x_ref, g_ref, a_ref, b_ref, w1_ref, w3_ref, w2_ref,
             dx_ref, dg_ref, da_ref, db_ref, dyb_ref, acc_ref, dys_ref):
        i, j = pl.program_id(0), pl.program_id(1)

        @pl.when(j == 0)
        def _():
            dyb = dy_ref[...].astype(BF16)
            dys_ref[...] = dyb
            dyb_ref[...] = dyb
            acc_ref[...] = jnp.zeros_like(acc_ref)

        @pl.when((i == 0) & (j == 0))
        def _():
            dg_ref[...] = jnp.zeros_like(dg_ref)

        st = {}

        def s_du(rs):
            st[rs.start, "du"] = 0.5 * _dot_nt(dys_ref[rs, :], w2_ref[...])

        def s_act(rs):
            du = st.pop((rs.start, "du"))
            av = a_ref[rs, :].astype(F32)
            bv = b_ref[rs, :].astype(F32)
            sg = jax.nn.sigmoid(av)
            sil = av * sg
            da = (du * bv * (sg + sil * (1.0 - sg))).astype(BF16)
            db = (du * sil).astype(BF16)
            da_ref[rs, :] = da
            db_ref[rs, :] = db
            st[rs.start, "dab"] = (da, db)

        def s_dh(rs):
            da, db = st.pop((rs.start, "dab"))
            acc_ref[rs, :] += _dot(da, w1_ref[...]) + _dot(db, w3_ref[...])

        _emit_skewed(([slice(r, r + tm // FFN_PIECES) for r in range(0, tm, tm // FFN_PIECES)], [s_du, s_act, s_dh]))

        @pl.when(j == nf - 1)
        def _():
            xv = x_ref[...]
            dx, dg = _rms_bwd(acc_ref[...], xv, _rms_rstd(xv), g_ref[...])
            dx_ref[...] = dy_ref[...] + dx
            dg_ref[...] += dg

    row = lambda i, j: (i, 0)
    blk = lambda i, j: (i, j)
    wsp = pl.BlockSpec((tf, D_MODEL), lambda i, j: (j, 0))
    return pl.pallas_call(
        body, name=f"ffn_bwd_{tag}",
        grid=(s_len // tm, nf),
        in_specs=[pl.BlockSpec((tm, D_MODEL), row), pl.BlockSpec((tm, D_MODEL), row),
                  pl.BlockSpec((1, D_MODEL), lambda i, j: (0, 0)),
                  pl.BlockSpec((tm, tf), blk), pl.BlockSpec((tm, tf), blk), wsp, wsp, wsp],
        out_specs=[pl.BlockSpec((tm, D_MODEL), row), pl.BlockSpec((1, D_MODEL), lambda i, j: (0, 0)),
                   pl.BlockSpec((tm, tf), blk), pl.BlockSpec((tm, tf), blk), pl.BlockSpec((tm, D_MODEL), row)],
        out_shape=[jax.ShapeDtypeStruct((s_len, D_MODEL), F32), jax.ShapeDtypeStruct((1, D_MODEL), F32),
                   jax.ShapeDtypeStruct((s_len, D_FF), BF16), jax.ShapeDtypeStruct((s_len, D_FF), BF16),
                   jax.ShapeDtypeStruct((s_len, D_MODEL), BF16)],
        scratch_shapes=[pltpu.VMEM((tm, D_MODEL), F32), pltpu.VMEM((tm, D_MODEL), BF16)],
        compiler_params=_cparams("arbitrary", "arbitrary"),
    )(dy, x, g, a, b, w1t, w3t, w2)


def _matmul_tn(lhs, rhs, tag):
    s_len, m = lhs.shape
    n = rhs.shape[1]
    tm = min(512, s_len)
    tj = m if m <= 1024 else 1408
    assert m % tj == 0

    def body(l_ref, r_ref, o_ref):
        @pl.when(pl.program_id(1) == 0)
        def _():
            o_ref[...] = jnp.zeros_like(o_ref)

        o_ref[...] += _dot_tn(l_ref[...], r_ref[...])

    return pl.pallas_call(
        body, name=f"matmul_tn_{tag}",
        grid=(m // tj, s_len // tm),
        in_specs=[pl.BlockSpec((tm, tj), lambda j, i: (i, j)), pl.BlockSpec((tm, n), lambda j, i: (i, 0))],
        out_specs=pl.BlockSpec((tj, n), lambda j, i: (j, 0)),
        out_shape=jax.ShapeDtypeStruct((m, n), F32),
        compiler_params=_cparams("parallel", "arbitrary"),
    )(lhs, rhs)


def _proj_fwd(x1, g, wint):
    s_len = x1.shape[0]
    tm = min(512, s_len)
    dts = (BF16, BF16, BF16, BF16, BF16, BF16, F32, F32)

    def body(x_ref, g_ref, w_ref, h_ref, *outs):
        xv = x_ref[...]
        h = (xv * _rms_rstd(xv) * g_ref[...]).astype(BF16)
        h_ref[...] = h
        for p, o_ref in enumerate(outs):
            val = _dot_nt(h, w_ref[IN_OFFS[p]:IN_OFFS[p + 1], :])
            if p == 3:
                val = val * Q_SCALE
            o_ref[...] = val.astype(dts[p])

    row = lambda i: (i, 0)
    return pl.pallas_call(
        body, name="proj_fwd",
        grid=(s_len // tm,),
        in_specs=[pl.BlockSpec((tm, D_MODEL), row), pl.BlockSpec((1, D_MODEL), lambda i: (0, 0)),
                  pl.BlockSpec((IN_W, D_MODEL), lambda i: (0, 0))],
        out_specs=[pl.BlockSpec((tm, D_MODEL), row)] + [pl.BlockSpec((tm, w), row) for w in IN_SIZES],
        out_shape=[jax.ShapeDtypeStruct((s_len, D_MODEL), BF16)]
        + [jax.ShapeDtypeStruct((s_len, w), dt) for w, dt in zip(IN_SIZES, dts)],
        compiler_params=_cparams("parallel"),
    )(x1, g, wint)


def _proj_bwd(dpieces, dx2, x1, g, wint, dep):
    s_len = x1.shape[0]
    tm = min(512, s_len)

    def body(*refs):
        dps = refs[:8]
        dx2_ref, x_ref, g_ref, w_ref, _, dx_ref, dg_ref = refs[8:]

        @pl.when(pl.program_id(0) == 0)
        def _():
            dg_ref[...] = jnp.zeros_like(dg_ref)

        dh = _dot(dps[0][...], w_ref[IN_OFFS[0]:IN_OFFS[1], :])
        for p in range(1, 8):
            dh += _dot(dps[p][...], w_ref[IN_OFFS[p]:IN_OFFS[p + 1], :])
        xv = x_ref[...]
        dx, dg = _rms_bwd(dh, xv, _rms_rstd(xv), g_ref[...])
        dx_ref[...] = dx2_ref[...] + dx
        dg_ref[...] += dg

    row = lambda i: (i, 0)
    return pl.pallas_call(
        body, name="proj_bwd",
        grid=(s_len // tm,),
        in_specs=[pl.BlockSpec((tm, w), row) for w in IN_SIZES]
        + [pl.BlockSpec((tm, D_MODEL), row), pl.BlockSpec((tm, D_MODEL), row),
           pl.BlockSpec((1, D_MODEL), lambda i: (0, 0)), pl.BlockSpec((IN_W, D_MODEL), lambda i: (0, 0)),
           pl.BlockSpec((8, LANES), lambda i: (0, 0))],
        out_specs=[pl.BlockSpec((tm, D_MODEL), row), pl.BlockSpec((1, D_MODEL), lambda i: (0, 0))],
        out_shape=[jax.ShapeDtypeStruct((s_len, D_MODEL), F32), jax.ShapeDtypeStruct((1, D_MODEL), F32)],
        compiler_params=_cparams("arbitrary"),
    )(*dpieces, dx2, x1, g, wint, dep)


def _merge_fwd(x1, oa, ob, ga, gb, wswa, wsb, wout):
    s_len = x1.shape[0]
    tm = min(512, s_len)

    def body(x_ref, oa_ref, ob_ref, ga_ref, gb_ref, wa_ref, wb_ref, wo_ref, xo_ref, mg_ref):
        pa = _dot(oa_ref[...], wa_ref[...])
        pb = _dot(ob_ref[...], wb_ref[...])
        mg = (jax.nn.sigmoid(ga_ref[...]) * pa + jax.nn.sigmoid(gb_ref[...]) * pb).astype(BF16)
        mg_ref[...] = mg
        xo_ref[...] = x_ref[...] + _dot(mg, wo_ref[...])

    row = lambda i: (i, 0)
    full = lambda i: (0, 0)
    return pl.pallas_call(
        body, name="merge_fwd",
        grid=(s_len // tm,),
        in_specs=[pl.BlockSpec((tm, D_MODEL), row), pl.BlockSpec((tm, 512), row), pl.BlockSpec((tm, 512), row),
                  pl.BlockSpec((tm, D_MODEL), row), pl.BlockSpec((tm, D_MODEL), row),
                  pl.BlockSpec((512, D_MODEL), full), pl.BlockSpec((512, D_MODEL), full),
                  pl.BlockSpec((D_MODEL, D_MODEL), full)],
        out_specs=[pl.BlockSpec((tm, D_MODEL), row), pl.BlockSpec((tm, D_MODEL), row)],
        out_shape=[jax.ShapeDtypeStruct((s_len, D_MODEL), F32), jax.ShapeDtypeStruct((s_len, D_MODEL), BF16)],
        compiler_params=_cparams("parallel"),
    )(x1, oa, ob, ga, gb, wswa, wsb, wout)


def _merge_bwd(dx2, oa, ob, ga, gb, wswa, wsb, wout, dep):
    s_len = dx2.shape[0]
    tm = min(512, s_len)

    def body(dx_ref, oa_ref, ob_ref, ga_ref, gb_ref, wa_ref, wb_ref, wo_ref, dep_ref,
             doa_ref, dob_ref, dga_ref, dgb_ref, dpa_ref, dpb_ref, dxb_ref):
        dxb = dx_ref[...].astype(BF16)
        dxb_ref[...] = dxb
        dmg = _dot_nt(dxb, wo_ref[...])
        for o_ref, g_ref, w_ref, do_ref, dg_ref, dp_ref in (
                (oa_ref, ga_ref, wa_ref, doa_ref, dga_ref, dpa_ref),
                (ob_ref, gb_ref, wb_ref, dob_ref, dgb_ref, dpb_ref)):
            pv = _dot(o_ref[...], w_ref[...])
            sg = jax.nn.sigmoid(g_ref[...])
            dp = (dmg * sg).astype(BF16)
            dp_ref[...] = dp
            dg_ref[...] = (dmg * pv * sg * (1.0 - sg)).astype(BF16)
            do_ref[...] = _dot_nt(dp, w_ref[...]).astype(BF16)

    row = lambda i: (i, 0)
    full = lambda i: (0, 0)
    wide = pl.BlockSpec((tm, D_MODEL), row)
    half = pl.BlockSpec((tm, 512), row)
    return pl.pallas_call(
        body, name="merge_bwd",
        grid=(s_len // tm,),
        in_specs=[wide, half, half, wide, wide, pl.BlockSpec((512, D_MODEL), full),
                  pl.BlockSpec((512, D_MODEL), full), pl.BlockSpec((D_MODEL, D_MODEL), full),
                  pl.BlockSpec((8, LANES), full)],
        out_specs=[half, half, wide, wide, wide, wide, wide],
        out_shape=[jax.ShapeDtypeStruct((s_len, 512), BF16)] * 2 + [jax.ShapeDtypeStruct((s_len, D_MODEL), BF16)] * 5,
        compiler_params=_cparams("parallel"),
    )(dx2, oa, ob, ga, gb, wswa, wsb, wout, dep)


def _loss_fwd_bwd(x3, tgt, g):
    s_len = x3.shape[0]
    tm = min(1024, s_len)

    def body(x_ref, t_ref, g_ref, dx_ref, loss_ref, dg_ref):
        @pl.when(pl.program_id(0) == 0)
        def _():
            loss_ref[...] = jnp.zeros_like(loss_ref)
            dg_ref[...] = jnp.zeros_like(dg_ref)

        xv = x_ref[...]
        gv = g_ref[...]
        r = _rms_rstd(xv)
        err = xv * r * gv - t_ref[...]
        loss_ref[...] += 0.5 * jnp.sum(jnp.mean(err * err, axis=-1, keepdims=True), axis=0, keepdims=True)
        dx, dg = _rms_bwd(err * (1.0 / D_MODEL), xv, r, gv)
        dx_ref[...] = dx
        dg_ref[...] += dg

    row = lambda i: (i, 0)
    return pl.pallas_call(
        body, name="loss_fwd_bwd",
        grid=(s_len // tm,),
        in_specs=[pl.BlockSpec((tm, D_MODEL), row), pl.BlockSpec((tm, D_MODEL), row),
                  pl.BlockSpec((1, D_MODEL), lambda i: (0, 0))],
        out_specs=[pl.BlockSpec((tm, D_MODEL), row), pl.BlockSpec((1, 1), lambda i: (0, 0)),
                   pl.BlockSpec((1, D_MODEL), lambda i: (0, 0))],
        out_shape=[jax.ShapeDtypeStruct((s_len, D_MODEL), F32), jax.ShapeDtypeStruct((1, 1), F32),
                   jax.ShapeDtypeStruct((1, D_MODEL), F32)],
        compiler_params=_cparams("arbitrary"),
    )(x3, tgt, g)


def _rel_bucket_matrix():
    qi = jnp.arange(SWA_BLOCK)[:, None] + SWA_BLOCK
    kj = jnp.arange(2 * SWA_BLOCK)[None, :]
    dist = jnp.maximum(qi - kj, 0)
    max_exact = REL_BUCKETS // 2
    d = jnp.maximum(dist, 1).astype(F32)
    large = max_exact + (jnp.log(d / max_exact) / np.log(REL_MAX_DIST / max_exact)
                         * (REL_BUCKETS - max_exact)).astype(jnp.int32)
    large = jnp.minimum(large, REL_BUCKETS - 1)
    return jnp.where(dist < max_exact, dist, large).astype(jnp.int32)


def _swa_bias_into(bias_ref, bkt_ref, tab_ref):
    bk = bkt_ref[...]
    for h in range(N_HEADS):
        acc = jnp.zeros(bk.shape, F32)
        for bucket in range(REL_BUCKETS):
            acc = jnp.where(bk == bucket, tab_ref[bucket, h], acc)
        bias_ref[h] = acc


def _swa_valid(n):
    shape = (SWA_BLOCK, 2 * SWA_BLOCK)
    row = lax.broadcasted_iota(jnp.int32, shape, 0)
    col = lax.broadcasted_iota(jnp.int32, shape, 1)
    dist = row + SWA_BLOCK - col
    return (dist >= 0) & (dist < SWA_BLOCK) & ((col >= SWA_BLOCK) | (n > 0))


def _swa_probs(q, k, bias, sink, valid):
    lg = jnp.where(valid, _dot_nt(q, k) * Q_SCALE + bias, NEG_BIG)
    m = jnp.maximum(jnp.max(lg, axis=-1, keepdims=True), sink)
    e = jnp.exp(lg - m)
    es = jnp.exp(sink - m)
    inv = 1.0 / (jnp.sum(e, axis=-1, keepdims=True) + es)
    return e * inv, es * inv


def _swa_specs(s_len):
    blk = SWA_BLOCK
    cur = lambda n: (n, 0)
    prev = lambda n: (jnp.maximum(n - 1, 0), 0)
    kvw = SWA_KV_HEADS * LANES
    return [pl.BlockSpec(memory_space=pltpu.SMEM), pl.BlockSpec(memory_space=pltpu.SMEM),
            pl.BlockSpec((blk, 2 * blk), lambda n: (0, 0)),
            pl.BlockSpec((blk, N_HEADS * LANES), cur),
            pl.BlockSpec((blk, kvw), prev), pl.BlockSpec((blk, kvw), cur),
            pl.BlockSpec((blk, kvw), prev), pl.BlockSpec((blk, kvw), cur)]


def _swa_fwd(tab, sinks, bkt, q, k, v):
    s_len = q.shape[0]
    blk = SWA_BLOCK

    def body(tab_ref, sink_ref, bkt_ref, q_ref, kp_ref, kc_ref, vp_ref, vc_ref, o_ref, bias_ref):
        n = pl.program_id(0)

        @pl.when(n == 0)
        def _():
            _swa_bias_into(bias_ref, bkt_ref, tab_ref)

        valid = _swa_valid(n)
        for grp in range(SWA_KV_HEADS):
            gl = slice(grp * LANES, (grp + 1) * LANES)
            kk = jnp.concatenate([kp_ref[:, gl], kc_ref[:, gl]], axis=0)
            vv = jnp.concatenate([vp_ref[:, gl], vc_ref[:, gl]], axis=0)
            for hh in range(SWA_GROUP):
                h = grp * SWA_GROUP + hh
                hl = slice(h * LANES, (h + 1) * LANES)
                p, _ = _swa_probs(q_ref[:, hl], kk, bias_ref[h], sink_ref[0, h], valid)
                o_ref[:, hl] = _dot(p.astype(BF16), vv).astype(BF16)

    return pl.pallas_call(
        body, name="swa_fwd",
        grid=(s_len // blk,),
        in_specs=_swa_specs(s_len),
        out_specs=pl.BlockSpec((blk, N_HEADS * LANES), lambda n: (n, 0)),
        out_shape=jax.ShapeDtypeStruct((s_len, N_HEADS * LANES), BF16),
        scratch_shapes=[pltpu.VMEM((N_HEADS, blk, 2 * blk), F32)],
        compiler_params=_cparams("arbitrary"),
    )(tab, sinks, bkt, q, k, k, v, v)


def _swa_bwd(tab, sinks, bkt, q, k, v, do):
    s_len = q.shape[0]
    blk = SWA_BLOCK
    nb = s_len // blk
    kvw = SWA_KV_HEADS * LANES

    def body(tab_ref, sink_ref, bkt_ref, q_ref, kp_ref, kc_ref, vp_ref, vc_ref, do_ref,
             dq_ref, dk_ref, dv_ref, dtab_ref, dsink_ref, bias_ref, dbias_ref):
        n = pl.program_id(0)

        @pl.when(n == 0)
        def _():
            _swa_bias_into(bias_ref, bkt_ref, tab_ref)
            dbias_ref[...] = jnp.zeros_like(dbias_ref)
            dk_ref[...] = jnp.zeros_like(dk_ref)
            dv_ref[...] = jnp.zeros_like(dv_ref)
            dsink_ref[...] = jnp.zeros_like(dsink_ref)
            dtab_ref[...] = jnp.zeros_like(dtab_ref)

        valid = _swa_valid(n)
        cur_rows = pl.ds(pl.multiple_of(n * blk, blk), blk)
        prev_rows = pl.ds(pl.multiple_of(jnp.maximum(n - 1, 0) * blk, blk), blk)
        for grp in range(SWA_KV_HEADS):
            gl = slice(grp * LANES, (grp + 1) * LANES)
            kk = jnp.concatenate([kp_ref[:, gl], kc_ref[:, gl]], axis=0)
            vv = jnp.concatenate([vp_ref[:, gl], vc_ref[:, gl]], axis=0)
            dk_acc = jnp.zeros((2 * blk, LANES), F32)
            dv_acc = jnp.zeros((2 * blk, LANES), F32)
            for hh in range(SWA_GROUP):
                h = grp * SWA_GROUP + hh
                hl = slice(h * LANES, (h + 1) * LANES)
                qh = q_ref[:, hl]
                doh = do_ref[:, hl]
                p, ps = _swa_probs(qh, kk, bias_ref[h], sink_ref[0, h], valid)
                dp = _dot_nt(doh, vv)
                delta = jnp.sum(p * dp, axis=-1, keepdims=True)
                dl = p * (dp - delta)
                dsink_ref[h:h + 1, :] += jnp.broadcast_to(-jnp.sum(ps * delta, axis=0, keepdims=True), (1, LANES))
                dbias_ref[h] += dl
                dlb = dl.astype(BF16)
                dq_ref[:, hl] = (Q_SCALE * _dot(dlb, kk)).astype(BF16)
                dk_acc += Q_SCALE * _dot_tn(dlb, qh)
                dv_acc += _dot_tn(p.astype(BF16), doh)
            dk_ref[cur_rows, gl] += dk_acc[blk:]
            dv_ref[cur_rows, gl] += dv_acc[blk:]

            @pl.when(n > 0)
            def _():
                dk_ref[prev_rows, gl] += dk_acc[:blk]
                dv_ref[prev_rows, gl] += dv_acc[:blk]

        @pl.when(n == nb - 1)
        def _():
            bk = bkt_ref[...]
            lane = lax.broadcasted_iota(jnp.int32, (1, LANES), 1)
            for bucket in range(REL_BUCKETS):
                rowv = jnp.zeros((1, LANES), F32)
                for h in range(N_HEADS):
                    val = jnp.sum(jnp.where(bk == bucket, dbias_ref[h], 0.0), axis=1, keepdims=True)
                    val = jnp.sum(val, axis=0, keepdims=True)
                    rowv = jnp.where(lane == h, val, rowv)
                dtab_ref[bucket:bucket + 1, :] = rowv

    return pl.pallas_call(
        body, name="swa_bwd",
        grid=(nb,),
        in_specs=_swa_specs(s_len) + [pl.BlockSpec((blk, N_HEADS * LANES), lambda n: (n, 0))],
        out_specs=[pl.BlockSpec((blk, N_HEADS * LANES), lambda n: (n, 0)),
                   pl.BlockSpec((s_len, kvw), lambda n: (0, 0)), pl.BlockSpec((s_len, kvw), lambda n: (0, 0)),
                   pl.BlockSpec((REL_BUCKETS, LANES), lambda n: (0, 0)), pl.BlockSpec((N_HEADS, LANES), lambda n: (0, 0))],
        out_shape=[jax.ShapeDtypeStruct((s_len, N_HEADS * LANES), BF16),
                   jax.ShapeDtypeStruct((s_len, kvw), F32), jax.ShapeDtypeStruct((s_len, kvw), F32),
                   jax.ShapeDtypeStruct((REL_BUCKETS, LANES), F32), jax.ShapeDtypeStruct((N_HEADS, LANES), F32)],
        scratch_shapes=[pltpu.VMEM((N_HEADS, blk, 2 * blk), F32), pltpu.VMEM((N_HEADS, blk, 2 * blk), F32)],
        compiler_params=_cparams("arbitrary"),
    )(tab, sinks, bkt, q, k, k, v, v, do)


def _sb_terms(z, valid):
    zc = jnp.minimum(z, SB_LOGIT_CAP)
    lk = -jnp.log(1.0 + jnp.exp(zc))
    lsz = zc + lk
    return lsz, (lk if valid is None else jnp.where(valid, lk, 0.0))


def _bf16_parts(vals):
    parts, rest = [], vals
    for n in range(SB_SUM_PARTS):
        parts.append(rest.astype(BF16))
        if n + 1 < SB_SUM_PARTS:
            rest = rest - parts[-1].astype(F32)
    return parts[0] if len(parts) == 1 else jnp.concatenate(parts, axis=1)


def _row_sum_lanes(vals):
    return jnp.broadcast_to(jnp.sum(vals, axis=-1, keepdims=True), (vals.shape[0], LANES))


def _emit_skewed(*groups):
    for step in range(max(len(items) + len(stages) - 1 for items, stages in groups)):
        for items, stages in groups:
            for s, stage in enumerate(stages):
                if 0 <= step - s < len(items):
                    stage(items[step - s])


def _sb_items(edge):
    items = []
    for h in range(2):
        for r0 in range(0, SB_QUERIES, SB_ROWS):
            if edge is None or r0 >= (edge + 1) * SB_KEYS:
                items.append((h, r0, False))
            elif r0 + SB_ROWS - 1 > edge * SB_KEYS:
                items.append((h, r0, True))
    return items


def _sb_valid(w, edge):
    row = lax.broadcasted_iota(jnp.int32, (SB_ROWS, SB_KEYS), 0) + w[1]
    col = lax.broadcasted_iota(jnp.int32, (SB_ROWS, SB_KEYS), 1) + edge * SB_KEYS
    return col < row


def _sb_consts(tq, tk):
    low = lax.broadcasted_iota(jnp.int32, (tq, LANES), 1) < HEAD_DIM
    row = lax.broadcasted_iota(jnp.int32, (tk, tk), 0)
    col = lax.broadcasted_iota(jnp.int32, (tk, tk), 1)
    right = (row > col).astype(BF16)
    left = (row < col).astype(BF16)
    return low, jnp.concatenate([right] * SB_SUM_PARTS, axis=0), jnp.concatenate([left] * SB_SUM_PARTS, axis=0)


def _sb_fwd(q, kt, v):
    s_len = q.shape[0]
    tq, tk, tr = SB_QUERIES, SB_KEYS, SB_ROWS
    nk, ratio = s_len // tk, tq // tk
    assert nk <= LANES

    def body(q_ref, kt_ref, v_ref, o_ref, car_ref, c_ref, oacc_ref, logw_ref, lksum_ref):
        i = pl.program_id(1)
        qv = q_ref[...]
        low, tri2, _ = _sb_consts(tq, tk)
        lane = lax.broadcasted_iota(jnp.int32, (tr, LANES), 1)
        zero = jnp.zeros_like(qv)
        q_heads = (jnp.where(low, qv, zero), jnp.where(low, zero, qv))
        c_ref[...] = jnp.zeros_like(c_ref)
        oacc_ref[...] = jnp.zeros_like(oacc_ref)
        car_ref[...] = jnp.zeros_like(car_ref)

        def front(j, edge):
            ktv = kt_ref[0, j]
            slot = j % 2
            st = {}

            def s_logits(w):
                st[w, "z"] = _dot(q_heads[w[0]][w[1]:w[1] + tr], ktv)

            def s_terms(w):
                valid = _sb_valid(w, edge) if w[2] else None
                lsz, lk = _sb_terms(st.pop((w, "z")), valid)
                st[w, "parts"] = _bf16_parts(lk)
                st[w, "lsz"] = lsz if valid is None else jnp.where(valid, lsz, NEG_BIG)
                lksum_ref[slot, w[0], w[1]:w[1] + tr, :] = _row_sum_lanes(lk)

            def s_suffix(w):
                logw_ref[slot, w[0], w[1]:w[1] + tr, :] = st.pop((w, "lsz")) + _dot(st.pop((w, "parts")), tri2)

            return _sb_items(edge), [s_logits, s_terms, s_suffix]

        def back(j, edge):
            vv = v_ref[pl.ds(pl.multiple_of(j * tk, tk), tk), :]
            slot = j % 2
            st = {}

            def s_weights(w):
                h, rs = w[0], slice(w[1], w[1] + tr)
                c = c_ref[h, rs, :]
                st[w, "a"] = jnp.exp(logw_ref[slot, h, rs, :] + jnp.tile(c, (1, tk // LANES))).astype(BF16)
                car_ref[h, rs, :] = jnp.where(lane == j, c, car_ref[h, rs, :])
                c_ref[h, rs, :] = c + lksum_ref[slot, h, rs, :]

            def s_values(w):
                oacc_ref[w[0], w[1]:w[1] + tr, :] += _dot(st.pop((w, "a")), vv)

            return _sb_items(edge), [s_weights, s_values]

        first = i * ratio
        _emit_skewed(front(first + ratio - 1, ratio - 1))
        for m in reversed(range(ratio - 1)):
            _emit_skewed(front(first + m, m), back(first + m + 1, m + 1))

        @pl.when(i == 0)
        def _():
            _emit_skewed(back(0, 0))

        @pl.when(i > 0)
        def _():
            _emit_skewed(front(first - 1, None), back(first, 0))

            def step(jj, carry):
                _emit_skewed(front(first - jj, None), back(first - jj + 1, None))
                return carry

            lax.fori_loop(2, first + 1, step, 0)
            _emit_skewed(back(0, None))

        o_ref[...] = jnp.where(low, oacc_ref[0], oacc_ref[1]).astype(BF16)

    return pl.pallas_call(
        body, name="sb_fwd",
        grid=(N_HEADS // 2, s_len // tq),
        in_specs=[pl.BlockSpec((tq, LANES), lambda p, i: (i, p)),
                  pl.BlockSpec((1, nk, LANES, tk), lambda p, i: (p, 0, 0, 0)),
                  pl.BlockSpec((s_len, LANES), lambda p, i: (0, p))],
        out_specs=[pl.BlockSpec((tq, LANES), lambda p, i: (i, p)), pl.BlockSpec((2, tq, LANES), lambda p, i: (p, i, 0))],
        out_shape=[jax.ShapeDtypeStruct((s_len, N_HEADS * HEAD_DIM), BF16),
                   jax.ShapeDtypeStruct((N_HEADS, s_len, LANES), F32)],
        scratch_shapes=[pltpu.VMEM((2, tq, LANES), F32), pltpu.VMEM((2, tq, LANES), F32),
                        pltpu.VMEM((2, 2, tq, tk), F32), pltpu.VMEM((2, 2, tq, LANES), F32)],
        compiler_params=_cparams("parallel", "arbitrary"),
    )(q, kt, v)


def _sb_bwd(q, qt, kt, k, vt, do, dot, cars):
    s_len = q.shape[0]
    tq, tk, tr = SB_QUERIES, SB_KEYS, SB_ROWS
    nk, ratio = s_len // tk, tq // tk

    def body(q_ref, qt_ref, kt_ref, k_ref, vt_ref, do_ref, dot_ref, car_ref, dq_ref, dk_ref, dv_ref,
             gleft_ref, dqacc_ref, dkacc_ref, dvacc_ref, logw_ref, lsz_ref, da_ref, a_ref, dz_ref):
        i = pl.program_id(1)

        @pl.when(i == 0)
        def _():
            dkacc_ref[...] = jnp.zeros_like(dkacc_ref)
            dvacc_ref[...] = jnp.zeros_like(dvacc_ref)

        qv = q_ref[...]
        dov = do_ref[...]
        low, tri_right2, tri_left2 = _sb_consts(tq, tk)
        lane = lax.broadcasted_iota(jnp.int32, (tr, LANES), 1)
        zero = jnp.zeros_like(qv)
        q_heads = (jnp.where(low, qv, zero), jnp.where(low, zero, qv))
        do_heads = (jnp.where(low, dov, zero), jnp.where(low, zero, dov))
        q_t = qt_ref[0, 0]
        do_t = dot_ref[0, 0]
        gleft_ref[...] = jnp.zeros_like(gleft_ref)
        dqacc_ref[...] = jnp.zeros_like(dqacc_ref)

        def front(j, edge):
            ktv = kt_ref[0, j]
            vtv = vt_ref[0, j]
            slot = j % 2
            st = {}

            def s_logits(w):
                h, rs = w[0], slice(w[1], w[1] + tr)
                st[w, "z"] = _dot(q_heads[h][rs], ktv)
                da_ref[slot, h, rs, :] = _dot(do_heads[h][rs], vtv)

            def s_terms(w):
                h, rs = w[0], slice(w[1], w[1] + tr)
                valid = _sb_valid(w, edge) if w[2] else None
                lsz, lk = _sb_terms(st.pop((w, "z")), valid)
                st[w, "parts"] = _bf16_parts(lk)
                lsz = lsz if valid is None else jnp.where(valid, lsz, NEG_BIG)
                lsz_ref[slot, h, rs, :] = lsz
                st[w, "lszc"] = lsz + jnp.sum(jnp.where(lane == j, car_ref[h, rs, :], 0.0), axis=-1, keepdims=True)

            def s_suffix(w):
                logw_ref[slot, w[0], w[1]:w[1] + tr, :] = st.pop((w, "lszc")) + _dot(st.pop((w, "parts")), tri_right2)

            return _sb_items(edge), [s_logits, s_terms, s_suffix]

        def back(j, edge):
            kv = k_ref[pl.ds(pl.multiple_of(j * tk, tk), tk), :]
            slot = j % 2
            st = {}

            items = _sb_items(edge)
            head_rows = [[r0 for hh, r0, _ in items if hh == h] for h in range(2)]

            def s_weights(w):
                h, rs = w[0], slice(w[1], w[1] + tr)
                a = jnp.exp(logw_ref[slot, h, rs, :])
                g = a * da_ref[slot, h, rs, :]
                a_ref[h, rs, :] = a.astype(BF16)
                st[w, "g"], st[w, "parts"] = g, _bf16_parts(g)

            def s_prefix(w):
                st[w, "gs"] = _dot(st.pop((w, "parts")), tri_left2)

            def s_dz(w):
                h, rs = w[0], slice(w[1], w[1] + tr)
                g = st.pop((w, "g"))
                gleft = gleft_ref[h, rs, :]
                gsum = st.pop((w, "gs")) + jnp.tile(gleft, (1, tk // LANES))
                dz = (g - jnp.exp(lsz_ref[slot, h, rs, :]) * (g + gsum)).astype(BF16)
                st[w, "dz"] = dz
                dz_ref[h, rs, :] = dz
                gleft_ref[h, rs, :] = gleft + _row_sum_lanes(g)

            def s_products(w):
                h, rs = w[0], slice(w[1], w[1] + tr)
                dqacc_ref[h, rs, :] += _dot(st.pop((w, "dz")), kv)
                if w[1] == head_rows[h][-1]:
                    feat = slice(h * HEAD_DIM, (h + 1) * HEAD_DIM)
                    hr = slice(head_rows[h][0], tq)
                    dkacc_ref[j, feat, :] += _dot(q_t[feat, hr], dz_ref[h, hr, :])
                    dvacc_ref[j, feat, :] += _dot(do_t[feat, hr], a_ref[h, hr, :])

            return items, [s_weights, s_prefix, s_dz, s_products]

        first = i * ratio

        @pl.when(i == 0)
        def _():
            _emit_skewed(front(0, 0))

        @pl.when(i > 0)
        def _():
            _emit_skewed(front(0, None))

            def step(jj, carry):
                _emit_skewed(front(jj, None), back(jj - 1, None))
                return carry

            lax.fori_loop(1, first, step, 0)
            _emit_skewed(front(first, 0), back(first - 1, None))

        for m in range(1, ratio):
            _emit_skewed(front(first + m, m), back(first + m - 1, m - 1))
        _emit_skewed(back(first + ratio - 1, ratio - 1))
        dq_ref[...] = (Q_SCALE * jnp.where(low, dqacc_ref[0], dqacc_ref[1])).astype(BF16)

        @pl.when(i == s_len // tq - 1)
        def _():
            dk_ref[0] = dkacc_ref[...].astype(BF16)
            dv_ref[0] = dvacc_ref[...].astype(BF16)

    qblk = pl.BlockSpec((tq, LANES), lambda p, i: (i, p))
    qtblk = pl.BlockSpec((1, 1, LANES, tq), lambda p, i: (p, i, 0, 0))
    tblk = pl.BlockSpec((1, nk, LANES, tk), lambda p, i: (p, 0, 0, 0))
    col_full = pl.BlockSpec((s_len, LANES), lambda p, i: (0, p))
    tshape = jax.ShapeDtypeStruct((N_HEADS // 2, nk, LANES, tk), BF16)
    return pl.pallas_call(
        body, name="sb_bwd",
        grid=(N_HEADS // 2, s_len // tq),
        in_specs=[qblk, qtblk, tblk, col_full, tblk, qblk, qtblk, pl.BlockSpec((2, tq, LANES), lambda p, i: (p, i, 0))],
        out_specs=[qblk, tblk, tblk],
        out_shape=[jax.ShapeDtypeStruct((s_len, N_HEADS * HEAD_DIM), BF16), tshape, tshape],
        scratch_shapes=[pltpu.VMEM((2, tq, LANES), F32), pltpu.VMEM((2, tq, LANES), F32),
                        pltpu.VMEM((nk, LANES, tk), F32), pltpu.VMEM((nk, LANES, tk), F32)]
        + [pltpu.VMEM((2, 2, tq, tk), F32)] * 3 + [pltpu.VMEM((2, tq, tk), BF16)] * 2,
        compiler_params=_cparams("parallel", "arbitrary"),
    )(q, qt, kt, k, vt, do, dot, cars)


def _pad_heads(a, heads):
    s_len = a.shape[0]
    a = a.reshape(s_len, heads, HEAD_DIM)
    return jnp.pad(a, ((0, 0), (0, 0), (0, LANES - HEAD_DIM))).reshape(s_len, heads * LANES)


def _unpad_heads(a, heads):
    s_len = a.shape[0]
    return a.reshape(s_len, heads, LANES)[:, :, :HEAD_DIM].reshape(s_len, heads * HEAD_DIM)


def _tile_transposed(a, groups, t):
    s_len = a.shape[0]
    return a.reshape(s_len // t, t, groups, LANES).transpose(2, 0, 3, 1)


def _tile_untransposed(a):
    groups, nt, _, t = a.shape
    return a.transpose(1, 3, 0, 2).reshape(nt * t, groups * LANES)


def _local_step(xs, tgt, gains, sinks, rel_bias, weights_of, send_early):
    g1, gmix, g2, gfin = gains
    bkt = _rel_bucket_matrix()
    groups = N_HEADS // 2

    wts = dict(weights_of(0, xs))
    x1, h1, a1, b1, u1 = _ffn_fwd(xs, g1, wts["ffn1_w1t"], wts["ffn1_w3t"], wts["ffn1_w2"], "1")
    wts.update(weights_of(1, x1))
    hm, qa, ka, va, qb, kb, vb, ga, gb = _proj_fwd(x1, gmix, wts["w_int"])
    qa_p, ka_p, va_p = _pad_heads(qa, N_HEADS), _pad_heads(ka, SWA_KV_HEADS), _pad_heads(va, SWA_KV_HEADS)
    oa_p = _swa_fwd(rel_bias, sinks, bkt, qa_p, ka_p, va_p)
    kbt = _tile_transposed(kb, groups, SB_KEYS)
    ob, cars = _sb_fwd(qb, kbt, vb)
    oa = _unpad_heads(oa_p, N_HEADS)
    x2, mg = _merge_fwd(x1, oa, ob, ga, gb, wts["w_swa"], wts["w_sb"], wts["w_out"])
    wts.update(weights_of(2, x2))
    x3, h3, a3, b3, u3 = _ffn_fwd(x2, g2, wts["ffn2_w1t"], wts["ffn2_w3t"], wts["ffn2_w2"], "2")
    dx3, loss, dgfin = _loss_fwd_bwd(x3, tgt, gfin)

    big = {}
    dx2, dg2, da3, db3, dx3b = _ffn_bwd(dx3, x2, g2, a3, b3, wts["ffn2_w1t"], wts["ffn2_w3t"], wts["ffn2_w2"], "2")
    big["ffn2_w1t"] = _matmul_tn(da3, h3, "ffn2_w1")
    big["ffn2_w3t"] = _matmul_tn(db3, h3, "ffn2_w3")
    big["ffn2_w2"] = _matmul_tn(u3, dx3b, "ffn2_w2")
    dep = send_early(2, big)

    doa, dob, dga, dgb, dpa, dpb, dx2b = _merge_bwd(dx2, oa, ob, ga, gb, wts["w_swa"], wts["w_sb"], wts["w_out"], dep)
    big = {}
    big["w_out"] = _matmul_tn(mg, dx2b, "w_out")
    big["w_swa"] = _matmul_tn(oa, dpa, "w_swa")
    big["w_sb"] = _matmul_tn(ob, dpb, "w_sb")

    dqa_p, dka_p, dva_p, dtab, dsink = _swa_bwd(rel_bias, sinks, bkt, qa_p, ka_p, va_p, _pad_heads(doa, N_HEADS))
    dqb, dkbt, dvbt = _sb_bwd(qb, _tile_transposed(qb, groups, SB_QUERIES), kbt, kb,
                              _tile_transposed(vb, groups, SB_KEYS), dob, _tile_transposed(dob, groups, SB_QUERIES), cars)
    dkb, dvb = _tile_untransposed(dkbt), _tile_untransposed(dvbt)
    dpieces = (_unpad_heads(dqa_p, N_HEADS), _unpad_heads(dka_p, SWA_KV_HEADS).astype(BF16),
               _unpad_heads(dva_p, SWA_KV_HEADS).astype(BF16), dqb, dkb, dvb, dga, dgb)
    big["w_int"] = jnp.concatenate([_matmul_tn(dp, hm, f"w_in{p}") for p, dp in enumerate(dpieces)], axis=0)
    dep = send_early(1, big)
    dx1, dgmix = _proj_bwd(dpieces, dx2, x1, gmix, wts["w_int"], dep)
    big = {}

    dx0, dg1, da1, db1, dx1b = _ffn_bwd(dx1, xs, g1, a1, b1, wts["ffn1_w1t"], wts["ffn1_w3t"], wts["ffn1_w2"], "1")
    big["ffn1_w1t"] = _matmul_tn(da1, h1, "ffn1_w1")
    big["ffn1_w3t"] = _matmul_tn(db1, h1, "ffn1_w3")
    big["ffn1_w2"] = _matmul_tn(u1, dx1b, "ffn1_w2")

    small = {"gains": (dg1, dgmix, dg2, dgfin), "sinks": dsink[:, 0], "rel_bias": dtab[:, :N_HEADS]}
    return loss, dx0, big, small


def _my_place():
    return lax.axis_index("x"), lax.axis_index("y"), lax.axis_index("c")


def _flip(v, bit):
    return 1 - v if bit else v


_RELATIONS = tuple((k >> 2 & 1, k >> 1 & 1, k & 1) for k in range(1, N_DEV))


def _gather_weights(wp):
    def body(x_ref, out_ref, send_sems, recv_sems, local_sem):
        x, y, c = _my_place()
        me, sibling = (x, y, c), (x, y, 1 - c)
        chips = [(1 - x, y), (x, 1 - y), (1 - x, 1 - y)]

        def rows(px, py, pc):
            return out_ref.at[4 * px + 2 * py + pc]

        def copy(k, block, to, src=None):
            return pltpu.make_async_remote_copy(
                src_ref=rows(*block) if src is None else src, dst_ref=rows(*block),
                send_sem=send_sems.at[k], recv_sem=recv_sems.at[k],
                device_id=to, device_id_type=pl.DeviceIdType.MESH)

        mine = pltpu.make_async_copy(x_ref, rows(*me), local_sem)
        mine.start()
        first = [copy(0, me, sibling, src=x_ref)]
        first += [copy(1 + j, me, (*chip, c), src=x_ref) for j, chip in enumerate(chips)]
        for cp in first:
            cp.start()
        passed = [copy(4 + j, (*chip, c), sibling) for j, chip in enumerate(chips)]
        for j, chip in enumerate(chips):
            copy(1 + j, (*chip, c), me).wait_recv()
            passed[j].start()
        copy(0, sibling, me).wait_recv()
        for j, chip in enumerate(chips):
            copy(4 + j, (*chip, 1 - c), me).wait_recv()
        for cp in first + passed:
            cp.wait_send()
        mine.wait()

    return pl.pallas_call(
        body, name="gather_weights",
        out_shape=jax.ShapeDtypeStruct((N_DEV,) + wp.shape, wp.dtype),
        in_specs=[pl.BlockSpec(memory_space=pl.ANY)],
        out_specs=pl.BlockSpec(memory_space=pl.ANY),
        scratch_shapes=[pltpu.SemaphoreType.DMA((7,)), pltpu.SemaphoreType.DMA((7,)), pltpu.SemaphoreType.DMA(())],
    )(wp)


def _exchange_grads(gp):
    def body(g_ref, out_ref, send_sems, recv_sems, local_sem):
        x, y, c = _my_place()
        me = 4 * x + 2 * y + c
        mine = pltpu.make_async_copy(g_ref.at[me], out_ref.at[me], local_sem)
        mine.start()
        copies = []
        for k, (fx, fy, fc) in enumerate(_RELATIONS):
            px, py, pc = _flip(x, fx), _flip(y, fy), _flip(c, fc)
            peer = 4 * px + 2 * py + pc
            copies.append((
                pltpu.make_async_remote_copy(
                    src_ref=g_ref.at[peer], dst_ref=out_ref.at[me], send_sem=send_sems.at[k], recv_sem=recv_sems.at[k],
                    device_id=(px, py, pc), device_id_type=pl.DeviceIdType.MESH),
                pltpu.make_async_remote_copy(
                    src_ref=g_ref.at[peer], dst_ref=out_ref.at[peer], send_sem=send_sems.at[k], recv_sem=recv_sems.at[k],
                    device_id=(px, py, pc), device_id_type=pl.DeviceIdType.MESH)))
        for out_cp, _ in copies:
            out_cp.start()
        for _, in_cp in copies:
            in_cp.wait_recv()
        for out_cp, _ in copies:
            out_cp.wait_send()
        mine.wait()

    return pl.pallas_call(
        body, name="exchange_grads",
        out_shape=jax.ShapeDtypeStruct(gp.shape, gp.dtype),
        in_specs=[pl.BlockSpec(memory_space=pl.ANY)],
        out_specs=pl.BlockSpec(memory_space=pl.ANY),
        scratch_shapes=[pltpu.SemaphoreType.DMA((7,)), pltpu.SemaphoreType.DMA((7,)), pltpu.SemaphoreType.DMA(())],
    )(gp)


_HBM = pl.BlockSpec(memory_space=pltpu.HBM)
_SEM = pl.BlockSpec(memory_space=pltpu.SEMAPHORE)
_EFFECT = pltpu.SideEffectType.DATAFLOW_SIDE_EFFECTING


def _peers():
    x, y, c = _my_place()
    out = []
    for k, (fx, fy, fc) in enumerate(_RELATIONS):
        px, py, pc = _flip(x, fx), _flip(y, fy), _flip(c, fc)
        out.append((k, (px, py, pc), 4 * px + 2 * py + pc))
    return out, 4 * x + 2 * y + c


def _send_start(src, after, per_peer, tag):
    land_shape = src.shape if per_peer else (N_DEV,) + src.shape

    def body(s_ref, land_ref, after_ref, send_sems, recv_sems, s_thru, land_thru, token):
        peers, me = _peers()
        for k, where, slab in peers:
            pltpu.make_async_remote_copy(
                src_ref=s_ref.at[slab] if per_peer else s_ref, dst_ref=land_ref.at[me],
                send_sem=send_sems.at[k], recv_sem=recv_sems.at[k],
                device_id=where, device_id_type=pl.DeviceIdType.MESH).start()
        token[...] = jnp.zeros_like(token)

    return pl.pallas_call(
        body, name=f"send_start_{tag}",
        out_shape=(pltpu.SemaphoreType.DMA((N_DEV - 1,)), pltpu.SemaphoreType.DMA((N_DEV - 1,)),
                   pltpu.HBM(src.shape, src.dtype), pltpu.HBM(land_shape, src.dtype), jax.ShapeDtypeStruct((8, LANES), F32)),
        in_specs=(_HBM, _HBM, pl.BlockSpec(memory_space=pl.ANY)),
        out_specs=(_SEM, _SEM, _HBM, _HBM, pl.BlockSpec(memory_space=pltpu.VMEM)),
        input_output_aliases={0: 2, 1: 3},
        compiler_params=pltpu.CompilerParams(has_side_effects=_EFFECT),
    )(pltpu.with_memory_space_constraint(src, pltpu.HBM),
      pltpu.with_memory_space_constraint(lax.empty(land_shape, src.dtype), pltpu.HBM), after)


def _send_wait(handles, after, per_peer, tag):
    send_sems, recv_sems, s_thru, land_thru, _ = handles

    def body(s_ref, land_ref, send_sems, recv_sems, after_ref, s_out, land_out):
        peers, _ = _peers()
        for k, where, slab in peers:
            copy = pltpu.make_async_remote_copy(
                src_ref=s_ref.at[slab] if per_peer else s_ref, dst_ref=land_ref.at[slab],
                send_sem=send_sems.at[k], recv_sem=recv_sems.at[k],
                device_id=where, device_id_type=pl.DeviceIdType.MESH)
            copy.wait_send()
            copy.wait_recv()

    sent, landed = pl.pallas_call(
        body, name=f"send_wait_{tag}",
        out_shape=(pltpu.HBM(s_thru.shape, s_thru.dtype), pltpu.HBM(land_thru.shape, land_thru.dtype)),
        in_specs=(_HBM, _HBM, _SEM, _SEM, pl.BlockSpec(memory_space=pl.ANY)), out_specs=(_HBM, _HBM),
        input_output_aliases={0: 0, 1: 1},
        compiler_params=pltpu.CompilerParams(has_side_effects=_EFFECT),
    )(s_thru, land_thru, send_sems, recv_sems, after)
    me = 4 * lax.axis_index("x") + 2 * lax.axis_index("y") + lax.axis_index("c")
    own = lax.dynamic_slice_in_dim(sent, me, 1, axis=0) if per_peer else sent[None]
    return lax.dynamic_update_slice_in_dim(landed, own, me, axis=0)


def _adamw(w, g, m, v):
    m = ADAM_B1 * m + (1.0 - ADAM_B1) * g
    v = ADAM_B2 * v + (1.0 - ADAM_B2) * jnp.square(g)
    m_hat = m / (1.0 - ADAM_B1 ** ADAM_STEP)
    v_hat = v / (1.0 - ADAM_B2 ** ADAM_STEP)
    delta = -ADAM_LR * (m_hat / (jnp.sqrt(v_hat) + ADAM_EPS) + ADAM_WD * w)
    return delta, m, v


def _sum_and_adamw(parts, w, m, v, tr, tag):
    rows = w.shape[0]
    assert rows % tr == 0

    def body(p_ref, w_ref, m_ref, v_ref, g_out, d_out, m_out, v_out):
        g = p_ref[0].astype(F32)
        for d in range(1, N_DEV):
            g = g + p_ref[d].astype(F32)
        delta, mn, vn = _adamw(w_ref[...], g, m_ref[...], v_ref[...])
        g_out[...] = g
        d_out[...] = delta
        m_out[...] = mn
        v_out[...] = vn

    sp = pl.BlockSpec((tr, D_MODEL), lambda i: (i, 0))
    return pl.pallas_call(
        body, name=f"sum_and_adamw_{tag}",
        grid=(rows // tr,),
        in_specs=[pl.BlockSpec((N_DEV, tr, D_MODEL), lambda i: (0, i, 0)), sp, sp, sp],
        out_specs=[sp] * 4,
        out_shape=[jax.ShapeDtypeStruct(w.shape, F32)] * 4,
        compiler_params=_cparams("parallel"),
    )(parts, w, m, v)


def _small_allreduce_adamw(part, w, m, v):
    def body(p_ref, w_ref, m_ref, v_ref, g_out, d_out, m_out, v_out, buf, send_sems, recv_sems):
        x, y, c = _my_place()
        me = 4 * x + 2 * y + c
        buf[me] = p_ref[...]
        copies = []
        for k, (fx, fy, fc) in enumerate(_RELATIONS):
            px, py, pc = _flip(x, fx), _flip(y, fy), _flip(c, fc)
            peer = 4 * px + 2 * py + pc
            copies.append((
                pltpu.make_async_remote_copy(
                    src_ref=buf.at[me], dst_ref=buf.at[me], send_sem=send_sems.at[k], recv_sem=recv_sems.at[k],
                    device_id=(px, py, pc), device_id_type=pl.DeviceIdType.MESH),
                pltpu.make_async_remote_copy(
                    src_ref=buf.at[me], dst_ref=buf.at[peer], send_sem=send_sems.at[k], recv_sem=recv_sems.at[k],
                    device_id=(px, py, pc), device_id_type=pl.DeviceIdType.MESH)))
        for out_cp, _ in copies:
            out_cp.start()
        for _, in_cp in copies:
            in_cp.wait_recv()
        for out_cp, _ in copies:
            out_cp.wait_send()
        g = buf[0]
        for d in range(1, N_DEV):
            g = g + buf[d]
        delta, mn, vn = _adamw(w_ref[...], g, m_ref[...], v_ref[...])
        g_out[...] = g
        d_out[...] = delta
        m_out[...] = mn
        v_out[...] = vn

    vm = pl.BlockSpec(memory_space=pltpu.VMEM)
    return pl.pallas_call(
        body, name="small_allreduce_adamw",
        in_specs=[vm] * 4, out_specs=[vm] * 4,
        out_shape=[jax.ShapeDtypeStruct(w.shape, F32)] * 4,
        scratch_shapes=[pltpu.VMEM((N_DEV,) + part.shape, F32),
                        pltpu.SemaphoreType.DMA((7,)), pltpu.SemaphoreType.DMA((7,))],
    )(part, w, m, v)


_TRANSPOSED = ("ffn1_w1", "ffn1_w3", "w_in", "ffn2_w1", "ffn2_w3")
_BRANCH = ("w_branch_swa", "w_branch_sb")


def _pack_shards(t, names):
    parts = []
    for name in names:
        a = t[name][0]
        if name in _TRANSPOSED:
            a = a.T
        elif name in _BRANCH:
            a = a.reshape(64, D_MODEL)
        parts.append(a)
    return jnp.concatenate(parts, axis=0)


def _unpack_shards(p, names):
    out, lo = {}, 0
    for name in names:
        a = p[lo:lo + BIG_ROWS[BIG_NAMES.index(name)]]
        lo += a.shape[0]
        if name in _TRANSPOSED:
            a = a.T
        elif name in _BRANCH:
            a = a.reshape(512, 128)
        out[name] = a[None]
    return out


def _full_weights(wg, names):
    out, lo = {}, 0
    for name in names:
        rows = BIG_ROWS[BIG_NAMES.index(name)]
        a = wg[:, lo:lo + rows]
        lo += rows
        if name in _BRANCH:
            a = a.reshape(N_DEV, 512, 128).transpose(1, 0, 2).reshape(512, D_MODEL)
        out[_GRAD_KEY[name]] = a.reshape(-1, D_MODEL)
    return out


_GRAD_KEY = {"ffn1_w1": "ffn1_w1t", "ffn1_w3": "ffn1_w3t", "ffn1_w2": "ffn1_w2", "w_in": "w_int",
             "w_branch_swa": "w_swa", "w_branch_sb": "w_sb", "w_out": "w_out",
             "ffn2_w1": "ffn2_w1t", "ffn2_w3": "ffn2_w3t", "ffn2_w2": "ffn2_w2"}


def _pack_full_grads(big, names):
    parts = []
    for name in names:
        a = big[_GRAD_KEY[name]]
        if name in _BRANCH:
            a = a.reshape(512, N_DEV, 128).transpose(1, 0, 2)
        parts.append(a.reshape(N_DEV, BIG_ROWS[BIG_NAMES.index(name)], D_MODEL).astype(BF16))
    return jnp.concatenate(parts, axis=1)


_SMALL_NAMES = ("norm_ffn1", "norm_mix", "norm_ffn2", "norm_final", "swa_sinks", "rel_bias")


def _pack_small(vals):
    rows = []
    for a in vals:
        a = a.reshape(-1)
        rows.append(jnp.pad(a, (0, D_MODEL - a.shape[0])))
    rows += [jnp.zeros((D_MODEL,), F32)] * (SMALL_ROWS - len(rows))
    return jnp.stack(rows)


def _unpack_small(p):
    return {"norm_ffn1": p[0:1], "norm_mix": p[1:2], "norm_ffn2": p[2:3], "norm_final": p[3],
            "swa_sinks": p[4:5, :N_HEADS], "rel_bias": p[5, :REL_BUCKETS * N_HEADS].reshape(REL_BUCKETS, N_HEADS)}


ALL_NAMES = ("norm_ffn1", "ffn1_w1", "ffn1_w3", "ffn1_w2", "norm_mix", "w_in", "swa_sinks", "rel_bias",
             "w_branch_swa", "w_branch_sb", "w_out", "norm_ffn2", "ffn2_w1", "ffn2_w3", "ffn2_w2", "norm_final")


def kernel(x, norm_ffn1, ffn1_w1, ffn1_w3, ffn1_w2, norm_mix, w_in, swa_sinks, rel_bias, w_branch_swa, w_branch_sb, w_out, norm_ffn2, ffn2_w1, ffn2_w3, ffn2_w2, norm_final, loss_target, m_norm_ffn1, m_ffn1_w1, m_ffn1_w3, m_ffn1_w2, m_norm_mix, m_w_in, m_swa_sinks, m_rel_bias, m_w_branch_swa, m_w_branch_sb, m_w_out, m_norm_ffn2, m_ffn2_w1, m_ffn2_w3, m_ffn2_w2, m_norm_final, v_norm_ffn1, v_ffn1_w1, v_ffn1_w3, v_ffn1_w2, v_norm_mix, v_w_in, v_swa_sinks, v_rel_bias, v_w_branch_swa, v_w_branch_sb, v_w_out, v_norm_ffn2, v_ffn2_w1, v_ffn2_w3, v_ffn2_w2, v_norm_final):
    w = dict(zip(ALL_NAMES, (norm_ffn1, ffn1_w1, ffn1_w3, ffn1_w2, norm_mix, w_in, swa_sinks, rel_bias,
                             w_branch_swa, w_branch_sb, w_out, norm_ffn2, ffn2_w1, ffn2_w3, ffn2_w2, norm_final)))
    m = dict(zip(ALL_NAMES, (m_norm_ffn1, m_ffn1_w1, m_ffn1_w3, m_ffn1_w2, m_norm_mix, m_w_in, m_swa_sinks, m_rel_bias,
                             m_w_branch_swa, m_w_branch_sb, m_w_out, m_norm_ffn2, m_ffn2_w1, m_ffn2_w3, m_ffn2_w2,
                             m_norm_final)))
    v = dict(zip(ALL_NAMES, (v_norm_ffn1, v_ffn1_w1, v_ffn1_w3, v_ffn1_w2, v_norm_mix, v_w_in, v_swa_sinks, v_rel_bias,
                             v_w_branch_swa, v_w_branch_sb, v_w_out, v_norm_ffn2, v_ffn2_w1, v_ffn2_w3, v_ffn2_w2,
                             v_norm_final)))

    w_packed = [_pack_shards(w, names) for names in GROUPS]
    gathered0 = _gather_weights(w_packed[0].astype(BF16))
    w_flight = {1: _send_start(w_packed[1].astype(BF16), gathered0, False, "w1")}
    w_flight[2] = _send_start(w_packed[2].astype(BF16), w_flight[1][4], False, "w2")

    def weights_of(group, after):
        wg = gathered0 if group == 0 else _send_wait(w_flight[group], after, False, f"w{group}")
        return _full_weights(wg, GROUPS[group])

    g_flight = {}

    def send_early(group, grads):
        gp = _pack_full_grads(grads, GROUPS[group])
        g_flight[group] = _send_start(gp, next(iter(grads.values())), True, f"g{group}")
        return g_flight[group][4]

    gains = (norm_ffn1 + w_flight[2][4][:1, :1], norm_mix, norm_ffn2, norm_final.reshape(1, D_MODEL))
    loss, dx, grads0, small = _local_step(x[0], loss_target[0], gains, swa_sinks, rel_bias, weights_of, send_early)
    parts = {0: _exchange_grads(_pack_full_grads(grads0, GROUPS[0]))}
    for group in (1, 2):
        parts[group] = _send_wait(g_flight[group], parts[group - 1], True, f"g{group}")

    big_outs = [{}, {}, {}, {}]
    for group, names in enumerate(GROUPS):
        res = _sum_and_adamw(parts[group], w_packed[group], _pack_shards(m, names), _pack_shards(v, names),
                             GROUP_TILE[group], f"group{group}")
        for acc, packed in zip(big_outs, res):
            acc.update(_unpack_shards(packed, names))
    g_big, d_big, m_big, v_big = big_outs

    small_part = _pack_small(small["gains"] + (small["sinks"], small["rel_bias"], loss))
    zero = jnp.zeros((1,), F32)
    small_res = _small_allreduce_adamw(
        small_part, _pack_small([w[n] for n in _SMALL_NAMES] + [zero]), _pack_small([m[n] for n in _SMALL_NAMES] + [zero]),
        _pack_small([v[n] for n in _SMALL_NAMES] + [zero]))
    g_sm, d_sm, m_sm, v_sm = (_unpack_small(p) for p in small_res)

    outs = [small_res[0][len(_SMALL_NAMES), 0], dx[None]]
    for big_d, small_d in ((g_big, g_sm), (d_big, d_sm), (m_big, m_sm), (v_big, v_sm)):
        merged = {**big_d, **small_d}
        outs += [merged[n] for n in ALL_NAMES]
    return tuple(outs)
```

```python
import jax
import jax.numpy as jnp
import numpy as np
from jax import lax
from jax.experimental import pallas as pl
from jax.experimental.pallas import tpu as pltpu

F32 = jnp.float32
BF16 = jnp.bfloat16

D_MODEL = 1024
D_FF = 2816
HEAD_DIM = 64
N_HEADS = 8
SWA_KV_HEADS = 2
SWA_GROUP = 4
SWA_BLOCK = 128
REL_BUCKETS = 32
REL_MAX_DIST = 128
RMS_EPS = 1e-6
NEG_BIG = -1e30
Q_SCALE = HEAD_DIM ** -0.5
LANES = 128

N_DEV = 8

ADAM_LR = 0.001
ADAM_B1 = 0.9
ADAM_B2 = 0.999
ADAM_EPS = 1e-08
ADAM_WD = 0.01
ADAM_STEP = 10

IN_SIZES = (512, 128, 128, 512, 512, 512, 1024, 1024)
IN_OFFS = tuple(int(v) for v in np.cumsum((0,) + IN_SIZES))
IN_W = IN_OFFS[-1]

BIG_NAMES = ("ffn1_w1", "ffn1_w3", "ffn1_w2", "w_in", "w_branch_swa", "w_branch_sb", "w_out",
             "ffn2_w1", "ffn2_w3", "ffn2_w2")
BIG_ROWS = (352, 352, 352, 544, 64, 64, 128, 352, 352, 352)
SMALL_ROWS = 8
GROUPS = (BIG_NAMES[0:3], BIG_NAMES[3:7], BIG_NAMES[7:10])
GROUP_TILE = (96, 160, 96)

VMEM_LIMIT = 56 * 1024 * 1024
FFN_PIECES = 2
SB_QUERIES = 512
SB_KEYS = 256
SB_ROWS = 256
SB_SUM_PARTS = 1
SB_LOGIT_CAP = 80.0
SB_DEAD_CARRY = -110.0


def _dot(a, b):
    return jnp.dot(a, b, preferred_element_type=F32)


def _dot_nt(a, b):
    return lax.dot_general(a, b, (((1,), (1,)), ((), ())), preferred_element_type=F32)


def _dot_tn(a, b):
    return lax.dot_general(a, b, (((0,), (0,)), ((), ())), preferred_element_type=F32)


def _cparams(*sem):
    return pltpu.CompilerParams(dimension_semantics=sem, vmem_limit_bytes=VMEM_LIMIT)


def _rms_rstd(xv):
    return lax.rsqrt(jnp.mean(xv * xv, axis=-1, keepdims=True) + RMS_EPS)


def _rms_bwd(dh, xv, r, g):
    xhat = xv * r
    dg = jnp.sum(dh * xhat, axis=0, keepdims=True)
    dxn = dh * g
    dx = r * (dxn - xhat * jnp.mean(dxn * xhat, axis=-1, keepdims=True))
    return dx, dg


def _ffn_fwd(x, g, w1t, w3t, w2, tag):
    s_len = x.shape[0]
    tm, tf = min(1024, s_len), 256
    nf = D_FF // tf

    def body(x_ref, g_ref, w1_ref, w3_ref, w2_ref, xo_ref, h_ref, a_ref, b_ref, u_ref, acc_ref, hs_ref):
        j = pl.program_id(1)

        @pl.when(j == 0)
        def _():
            xv = x_ref[...]
            h = (xv * _rms_rstd(xv) * g_ref[...]).astype(BF16)
            hs_ref[...] = h
            h_ref[...] = h
            acc_ref[...] = jnp.zeros_like(acc_ref)

        st = {}

        def s_up(rs):
            h = hs_ref[rs, :]
            st[rs.start, "ab"] = (_dot_nt(h, w1_ref[...]), _dot_nt(h, w3_ref[...]))

        def s_act(rs):
            a, b = st.pop((rs.start, "ab"))
            a_ref[rs, :] = a.astype(BF16)
            b_ref[rs, :] = b.astype(BF16)
            uh = (0.5 * (a * jax.nn.sigmoid(a) * b)).astype(BF16)
            u_ref[rs, :] = uh
            st[rs.start, "u"] = uh

        def s_down(rs):
            acc_ref[rs, :] += _dot(st.pop((rs.start, "u")), w2_ref[...])

        _emit_skewed(([slice(r, r + tm // FFN_PIECES) for r in range(0, tm, tm // FFN_PIECES)], [s_up, s_act, s_down]))

        @pl.when(j == nf - 1)
        def _():
            xo_ref[...] = x_ref[...] + acc_ref[...]

    row = lambda i, j: (i, 0)
    return pl.pallas_call(
        body, name=f"ffn_fwd_{tag}",
        grid=(s_len // tm, nf),
        in_specs=[pl.BlockSpec((tm, D_MODEL), row), pl.BlockSpec((1, D_MODEL), lambda i, j: (0, 0)),
                  pl.BlockSpec((tf, D_MODEL), lambda i, j: (j, 0)), pl.BlockSpec((tf, D_MODEL), lambda i, j: (j, 0)),
                  pl.BlockSpec((tf, D_MODEL), lambda i, j: (j, 0))],
        out_specs=[pl.BlockSpec((tm, D_MODEL), row), pl.BlockSpec((tm, D_MODEL), row),
                   pl.BlockSpec((tm, tf), lambda i, j: (i, j)), pl.BlockSpec((tm, tf), lambda i, j: (i, j)),
                   pl.BlockSpec((tm, tf), lambda i, j: (i, j))],
        out_shape=[jax.ShapeDtypeStruct((s_len, D_MODEL), F32), jax.ShapeDtypeStruct((s_len, D_MODEL), BF16),
                   jax.ShapeDtypeStruct((s_len, D_FF), BF16), jax.ShapeDtypeStruct((s_len, D_FF), BF16),
                   jax.ShapeDtypeStruct((s_len, D_FF), BF16)],
        scratch_shapes=[pltpu.VMEM((tm, D_MODEL), F32), pltpu.VMEM((tm, D_MODEL), BF16)],
        compiler_params=_cparams("parallel", "arbitrary"),
    )(x, g, w1t, w3t, w2)


def _ffn_bwd(dy, x, g, a, b, w1t, w3t, w2, tag):
    s_len = x.shape[0]
    tm, tf = min(1024, s_len), 256
    nf = D_FF // tf

    def body(dy_ref, x_ref, g_ref, a_ref, b_ref, w1_ref, w3_ref, w2_ref,
             dx_ref, dg_ref, da_ref, db_ref, dyb_ref, acc_ref, dys_ref):
        i, j = pl.program_id(0), pl.program_id(1)

        @pl.when(j == 0)
        def _():
            dyb = dy_ref[...].astype(BF16)
            dys_ref[...] = dyb
            dyb_ref[...] = dyb
            acc_ref[...] = jnp.zeros_like(acc_ref)

        @pl.when((i == 0) & (j == 0))
        def _():
            dg_ref[...] = jnp.zeros_like(dg_ref)

        st = {}

        def s_du(rs):
            st[rs.start, "du"] = 0.5 * _dot_nt(dys_ref[rs, :], w2_ref[...])

        def s_act(rs):
            du = st.pop((rs.start, "du"))
            av = a_ref[rs, :].astype(F32)
            bv = b_ref[rs, :].astype(F32)
            sg = jax.nn.sigmoid(av)
            sil = av * sg
            da = (du * bv * (sg + sil * (1.0 - sg))).astype(BF16)
            db = (du * sil).astype(BF16)
            da_ref[rs, :] = da
            db_ref[rs, :] = db
            st[rs.start, "dab"] = (da, db)

        def s_dh(rs):
            da, db = st.pop((rs.start, "dab"))
            acc_ref[rs, :] += _dot(da, w1_ref[...]) + _dot(db, w3_ref[...])

        _emit_skewed(([slice(r, r + tm // FFN_PIECES) for r in range(0, tm, tm // FFN_PIECES)], [s_du, s_act, s_dh]))

        @pl.when(j == nf - 1)
        def _():
            xv = x_ref[...]
            dx, dg = _rms_bwd(acc_ref[...], xv, _rms_rstd(xv), g_ref[...])
            dx_ref[...] = dy_ref[...] + dx
            dg_ref[...] += dg

    row = lambda i, j: (i, 0)
    blk = lambda i, j: (i, j)
    wsp = pl.BlockSpec((tf, D_MODEL), lambda i, j: (j, 0))
    return pl.pallas_call(
        body, name=f"ffn_bwd_{tag}",
        grid=(s_len // tm, nf),
        in_specs=[pl.BlockSpec((tm, D_MODEL), row), pl.BlockSpec((tm, D_MODEL), row),
                  pl.BlockSpec((1, D_MODEL), lambda i, j: (0, 0)),
                  pl.BlockSpec((tm, tf), blk), pl.BlockSpec((tm, tf), blk), wsp, wsp, wsp],
        out_specs=[pl.BlockSpec((tm, D_MODEL), row), pl.BlockSpec((1, D_MODEL), lambda i, j: (0, 0)),
                   pl.BlockSpec((tm, tf), blk), pl.BlockSpec((tm, tf), blk), pl.BlockSpec((tm, D_MODEL), row)],
        out_shape=[jax.ShapeDtypeStruct((s_len, D_MODEL), F32), jax.ShapeDtypeStruct((1, D_MODEL), F32),
                   jax.ShapeDtypeStruct((s_len, D_FF), BF16), jax.ShapeDtypeStruct((s_len, D_FF), BF16),
                   jax.ShapeDtypeStruct((s_len, D_MODEL), BF16)],
        scratch_shapes=[pltpu.VMEM((tm, D_MODEL), F32), pltpu.VMEM((tm, D_MODEL), BF16)],
        compiler_params=_cparams("arbitrary", "arbitrary"),
    )(dy, x, g, a, b, w1t, w3t, w2)


def _matmul_tn(lhs, rhs, tag):
    s_len, m = lhs.shape
    n = rhs.shape[1]
    tm = min(512, s_len)
    tj = m if m <= 1024 else 1408
    assert m % tj == 0

    def body(l_ref, r_ref, o_ref):
        @pl.when(pl.program_id(1) == 0)
        def _():
            o_ref[...] = jnp.zeros_like(o_ref)

        o_ref[...] += _dot_tn(l_ref[...], r_ref[...])

    return pl.pallas_call(
        body, name=f"matmul_tn_{tag}",
        grid=(m // tj, s_len // tm),
        in_specs=[pl.BlockSpec((tm, tj), lambda j, i: (i, j)), pl.BlockSpec((tm, n), lambda j, i: (i, 0))],
        out_specs=pl.BlockSpec((tj, n), lambda j, i: (j, 0)),
        out_shape=jax.ShapeDtypeStruct((m, n), F32),
        compiler_params=_cparams("parallel", "arbitrary"),
    )(lhs, rhs)


def _proj_fwd(x1, g, wint):
    s_len = x1.shape[0]
    tm = min(512, s_len)
    dts = (BF16, BF16, BF16, BF16, BF16, BF16, F32, F32)

    def body(x_ref, g_ref, w_ref, h_ref, *outs):
        xv = x_ref[...]
        h = (xv * _rms_rstd(xv) * g_ref[...]).astype(BF16)
        h_ref[...] = h
        for p, o_ref in enumerate(outs):
            val = _dot_nt(h, w_ref[IN_OFFS[p]:IN_OFFS[p + 1], :])
            if p == 3:
                val = val * Q_SCALE
            o_ref[...] = val.astype(dts[p])

    row = lambda i: (i, 0)
    return pl.pallas_call(
        body, name="proj_fwd",
        grid=(s_len // tm,),
        in_specs=[pl.BlockSpec((tm, D_MODEL), row), pl.BlockSpec((1, D_MODEL), lambda i: (0, 0)),
                  pl.BlockSpec((IN_W, D_MODEL), lambda i: (0, 0))],
        out_specs=[pl.BlockSpec((tm, D_MODEL), row)] + [pl.BlockSpec((tm, w), row) for w in IN_SIZES],
        out_shape=[jax.ShapeDtypeStruct((s_len, D_MODEL), BF16)]
        + [jax.ShapeDtypeStruct((s_len, w), dt) for w, dt in zip(IN_SIZES, dts)],
        compiler_params=_cparams("parallel"),
    )(x1, g, wint)


def _proj_bwd(dpieces, dx2, x1, g, wint, dep):
    s_len = x1.shape[0]
    tm = min(512, s_len)

    def body(*refs):
        dps = refs[:8]
        dx2_ref, x_ref, g_ref, w_ref, _, dx_ref, dg_ref = refs[8:]

        @pl.when(pl.program_id(0) == 0)
        def _():
            dg_ref[...] = jnp.zeros_like(dg_ref)

        dh = _dot(dps[0][...], w_ref[IN_OFFS[0]:IN_OFFS[1], :])
        for p in range(1, 8):
            dh += _dot(dps[p][...], w_ref[IN_OFFS[p]:IN_OFFS[p + 1], :])
        xv = x_ref[...]
        dx, dg = _rms_bwd(dh, xv, _rms_rstd(xv), g_ref[...])
        dx_ref[...] = dx2_ref[...] + dx
        dg_ref[...] += dg

    row = lambda i: (i, 0)
    return pl.pallas_call(
        body, name="proj_bwd",
        grid=(s_len // tm,),
        in_specs=[pl.BlockSpec((tm, w), row) for w in IN_SIZES]
        + [pl.BlockSpec((tm, D_MODEL), row), pl.BlockSpec((tm, D_MODEL), row),
           pl.BlockSpec((1, D_MODEL), lambda i: (0, 0)), pl.BlockSpec((IN_W, D_MODEL), lambda i: (0, 0)),
           pl.BlockSpec((8, LANES), lambda i: (0, 0))],
        out_specs=[pl.BlockSpec((tm, D_MODEL), row), pl.BlockSpec((1, D_MODEL), lambda i: (0, 0))],
        out_shape=[jax.ShapeDtypeStruct((s_len, D_MODEL), F32), jax.ShapeDtypeStruct((1, D_MODEL), F32)],
        compiler_params=_cparams("arbitrary"),
    )(*dpieces, dx2, x1, g, wint, dep)


def _merge_fwd(x1, oa, ob, ga, gb, wswa, wsb, wout):
    s_len = x1.shape[0]
    tm = min(512, s_len)

    def body(x_ref, oa_ref, ob_ref, ga_ref, gb_ref, wa_ref, wb_ref, wo_ref, xo_ref, mg_ref):
        pa = _dot(oa_ref[...], wa_ref[...])
        pb = _dot(ob_ref[...], wb_ref[...])
        mg = (jax.nn.sigmoid(ga_ref[...]) * pa + jax.nn.sigmoid(gb_ref[...]) * pb).astype(BF16)
        mg_ref[...] = mg
        xo_ref[...] = x_ref[...] + _dot(mg, wo_ref[...])

    row = lambda i: (i, 0)
    full = lambda i: (0, 0)
    return pl.pallas_call(
        body, name="merge_fwd",
        grid=(s_len // tm,),
        in_specs=[pl.BlockSpec((tm, D_MODEL), row), pl.BlockSpec((tm, 512), row), pl.BlockSpec((tm, 512), row),
                  pl.BlockSpec((tm, D_MODEL), row), pl.BlockSpec((tm, D_MODEL), row),
                  pl.BlockSpec((512, D_MODEL), full), pl.BlockSpec((512, D_MODEL), full),
                  pl.BlockSpec((D_MODEL, D_MODEL), full)],
        out_specs=[pl.BlockSpec((tm, D_MODEL), row), pl.BlockSpec((tm, D_MODEL), row)],
        out_shape=[jax.ShapeDtypeStruct((s_len, D_MODEL), F32), jax.ShapeDtypeStruct((s_len, D_MODEL), BF16)],
        compiler_params=_cparams("parallel"),
    )(x1, oa, ob, ga, gb, wswa, wsb, wout)


def _merge_bwd(dx2, oa, ob, ga, gb, wswa, wsb, wout, dep):
    s_len = dx2.shape[0]
    tm = min(512, s_len)

    def body(dx_ref, oa_ref, ob_ref, ga_ref, gb_ref, wa_ref, wb_ref, wo_ref, dep_ref,
             doa_ref, dob_ref, dga_ref, dgb_ref, dpa_ref, dpb_ref, dxb_ref):
        dxb = dx_ref[...].astype(BF16)
        dxb_ref[...] = dxb
        dmg = _dot_nt(dxb, wo_ref[...])
        for o_ref, g_ref, w_ref, do_ref, dg_ref, dp_ref in (
                (oa_ref, ga_ref, wa_ref, doa_ref, dga_ref, dpa_ref),
                (ob_ref, gb_ref, wb_ref, dob_ref, dgb_ref, dpb_ref)):
            pv = _dot(o_ref[...], w_ref[...])
            sg = jax.nn.sigmoid(g_ref[...])
            dp = (dmg * sg).astype(BF16)
            dp_ref[...] = dp
            dg_ref[...] = (dmg * pv * sg * (1.0 - sg)).astype(BF16)
            do_ref[...] = _dot_nt(dp, w_ref[...]).astype(BF16)

    row = lambda i: (i, 0)
    full = lambda i: (0, 0)
    wide = pl.BlockSpec((tm, D_MODEL), row)
    half = pl.BlockSpec((tm, 512), row)
    return pl.pallas_call(
        body, name="merge_bwd",
        grid=(s_len // tm,),
        in_specs=[wide, half, half, wide, wide, pl.BlockSpec((512, D_MODEL), full),
                  pl.BlockSpec((512, D_MODEL), full), pl.BlockSpec((D_MODEL, D_MODEL), full),
                  pl.BlockSpec((8, LANES), full)],
        out_specs=[half, half, wide, wide, wide, wide, wide],
        out_shape=[jax.ShapeDtypeStruct((s_len, 512), BF16)] * 2 + [jax.ShapeDtypeStruct((s_len, D_MODEL), BF16)] * 5,
        compiler_params=_cparams("parallel"),
    )(dx2, oa, ob, ga, gb, wswa, wsb, wout, dep)


def _loss_fwd_bwd(x3, tgt, g):
    s_len = x3.shape[0]
    tm = min(1024, s_len)

    def body(x_ref, t_ref, g_ref, dx_ref, loss_ref, dg_ref):
        @pl.when(pl.program_id(0) == 0)
        def _():
            loss_ref[...] = jnp.zeros_like(loss_ref)
            dg_ref[...] = jnp.zeros_like(dg_ref)

        xv = x_ref[...]
        gv = g_ref[...]
        r = _rms_rstd(xv)
        err = xv * r * gv - t_ref[...]
        loss_ref[...] += 0.5 * jnp.sum(jnp.mean(err * err, axis=-1, keepdims=True), axis=0, keepdims=True)
        dx, dg = _rms_bwd(err * (1.0 / D_MODEL), xv, r, gv)
        dx_ref[...] = dx
        dg_ref[...] += dg

    row = lambda i: (i, 0)
    return pl.pallas_call(
        body, name="loss_fwd_bwd",
        grid=(s_len // tm,),
        in_specs=[pl.BlockSpec((tm, D_MODEL), row), pl.BlockSpec((tm, D_MODEL), row),
                  pl.BlockSpec((1, D_MODEL), lambda i: (0, 0))],
        out_specs=[pl.BlockSpec((tm, D_MODEL), row), pl.BlockSpec((1, 1), lambda i: (0, 0)),
                   pl.BlockSpec((1, D_MODEL), lambda i: (0, 0))],
        out_shape=[jax.ShapeDtypeStruct((s_len, D_MODEL), F32), jax.ShapeDtypeStruct((1, 1), F32),
                   jax.ShapeDtypeStruct((1, D_MODEL), F32)],
        compiler_params=_cparams("arbitrary"),
    )(x3, tgt, g)


def _rel_bucket_matrix():
    qi = jnp.arange(SWA_BLOCK)[:, None] + SWA_BLOCK
    kj = jnp.arange(2 * SWA_BLOCK)[None, :]
    dist = jnp.maximum(qi - kj, 0)
    max_exact = REL_BUCKETS // 2
    d = jnp.maximum(dist, 1).astype(F32)
    large = max_exact + (jnp.log(d / max_exact) / np.log(REL_MAX_DIST / max_exact)
                         * (REL_BUCKETS - max_exact)).astype(jnp.int32)
    large = jnp.minimum(large, REL_BUCKETS - 1)
    return jnp.where(dist < max_exact, dist, large).astype(jnp.int32)


def _swa_bias_into(bias_ref, bkt_ref, tab_ref):
    bk = bkt_ref[...]
    for h in range(N_HEADS):
        acc = jnp.zeros(bk.shape, F32)
        for bucket in range(REL_BUCKETS):
            acc = jnp.where(bk == bucket, tab_ref[bucket, h], acc)
        bias_ref[h] = acc


def _swa_valid(n):
    shape = (SWA_BLOCK, 2 * SWA_BLOCK)
    row = lax.broadcasted_iota(jnp.int32, shape, 0)
    col = lax.broadcasted_iota(jnp.int32, shape, 1)
    dist = row + SWA_BLOCK - col
    return (dist >= 0) & (dist < SWA_BLOCK) & ((col >= SWA_BLOCK) | (n > 0))


def _swa_probs(q, k, bias, sink, valid):
    lg = jnp.where(valid, _dot_nt(q, k) * Q_SCALE + bias, NEG_BIG)
    m = jnp.maximum(jnp.max(lg, axis=-1, keepdims=True), sink)
    e = jnp.exp(lg - m)
    es = jnp.exp(sink - m)
    inv = 1.0 / (jnp.sum(e, axis=-1, keepdims=True) + es)
    return e * inv, es * inv


def _swa_specs(s_len):
    blk = SWA_BLOCK
    cur = lambda n: (n, 0)
    prev = lambda n: (jnp.maximum(n - 1, 0), 0)
    kvw = SWA_KV_HEADS * LANES
    return [pl.BlockSpec(memory_space=pltpu.SMEM), pl.BlockSpec(memory_space=pltpu.SMEM),
            pl.BlockSpec((blk, 2 * blk), lambda n: (0, 0)),
            pl.BlockSpec((blk, N_HEADS * LANES), cur),
            pl.BlockSpec((blk, kvw), prev), pl.BlockSpec((blk, kvw), cur),
            pl.BlockSpec((blk, kvw), prev), pl.BlockSpec((blk, kvw), cur)]


def _swa_fwd(tab, sinks, bkt, q, k, v):
    s_len = q.shape[0]
    blk = SWA_BLOCK

    def body(tab_ref, sink_ref, bkt_ref, q_ref, kp_ref, kc_ref, vp_ref, vc_ref, o_ref, bias_ref):
        n = pl.program_id(0)

        @pl.when(n == 0)
        def _():
            _swa_bias_into(bias_ref, bkt_ref, tab_ref)

        valid = _swa_valid(n)
        for grp in range(SWA_KV_HEADS):
            gl = slice(grp * LANES, (grp + 1) * LANES)
            kk = jnp.concatenate([kp_ref[:, gl], kc_ref[:, gl]], axis=0)
            vv = jnp.concatenate([vp_ref[:, gl], vc_ref[:, gl]], axis=0)
            for hh in range(SWA_GROUP):
                h = grp * SWA_GROUP + hh
                hl = slice(h * LANES, (h + 1) * LANES)
                p, _ = _swa_probs(q_ref[:, hl], kk, bias_ref[h], sink_ref[0, h], valid)
                o_ref[:, hl] = _dot(p.astype(BF16), vv).astype(BF16)

    return pl.pallas_call(
        body, name="swa_fwd",
        grid=(s_len // blk,),
        in_specs=_swa_specs(s_len),
        out_specs=pl.BlockSpec((blk, N_HEADS * LANES), lambda n: (n, 0)),
        out_shape=jax.ShapeDtypeStruct((s_len, N_HEADS * LANES), BF16),
        scratch_shapes=[pltpu.VMEM((N_HEADS, blk, 2 * blk), F32)],
        compiler_params=_cparams("arbitrary"),
    )(tab, sinks, bkt, q, k, k, v, v)


def _swa_bwd(tab, sinks, bkt, q, k, v, do):
    s_len = q.shape[0]
    blk = SWA_BLOCK
    nb = s_len // blk
    kvw = SWA_KV_HEADS * LANES

    def body(tab_ref, sink_ref, bkt_ref, q_ref, kp_ref, kc_ref, vp_ref, vc_ref, do_ref,
             dq_ref, dk_ref, dv_ref, dtab_ref, dsink_ref, bias_ref, dbias_ref):
        n = pl.program_id(0)

        @pl.when(n == 0)
        def _():
            _swa_bias_into(bias_ref, bkt_ref, tab_ref)
            dbias_ref[...] = jnp.zeros_like(dbias_ref)
            dk_ref[...] = jnp.zeros_like(dk_ref)
            dv_ref[...] = jnp.zeros_like(dv_ref)
            dsink_ref[...] = jnp.zeros_like(dsink_ref)
            dtab_ref[...] = jnp.zeros_like(dtab_ref)

        valid = _swa_valid(n)
        cur_rows = pl.ds(pl.multiple_of(n * blk, blk), blk)
        prev_rows = pl.ds(pl.multiple_of(jnp.maximum(n - 1, 0) * blk, blk), blk)
        for grp in range(SWA_KV_HEADS):
            gl = slice(grp * LANES, (grp + 1) * LANES)
            kk = jnp.concatenate([kp_ref[:, gl], kc_ref[:, gl]], axis=0)
            vv = jnp.concatenate([vp_ref[:, gl], vc_ref[:, gl]], axis=0)
            dk_acc = jnp.zeros((2 * blk, LANES), F32)
            dv_acc = jnp.zeros((2 * blk, LANES), F32)
            for hh in range(SWA_GROUP):
                h = grp * SWA_GROUP + hh
                hl = slice(h * LANES, (h + 1) * LANES)
                qh = q_ref[:, hl]
                doh = do_ref[:, hl]
                p, ps = _swa_probs(qh, kk, bias_ref[h], sink_ref[0, h], valid)
                dp = _dot_nt(doh, vv)
                delta = jnp.sum(p * dp, axis=-1, keepdims=True)
                dl = p * (dp - delta)
                dsink_ref[h:h + 1, :] += jnp.broadcast_to(-jnp.sum(ps * delta, axis=0, keepdims=True), (1, LANES))
                dbias_ref[h] += dl
                dlb = dl.astype(BF16)
                dq_ref[:, hl] = (Q_SCALE * _dot(dlb, kk)).astype(BF16)
                dk_acc += Q_SCALE * _dot_tn(dlb, qh)
                dv_acc += _dot_tn(p.astype(BF16), doh)
            dk_ref[cur_rows, gl] += dk_acc[blk:]
            dv_ref[cur_rows, gl] += dv_acc[blk:]

            @pl.when(n > 0)
            def _():
                dk_ref[prev_rows, gl] += dk_acc[:blk]
                dv_ref[prev_rows, gl] += dv_acc[:blk]

        @pl.when(n == nb - 1)
        def _():
            bk = bkt_ref[...]
            lane = lax.broadcasted_iota(jnp.int32, (1, LANES), 1)
            for bucket in range(REL_BUCKETS):
                rowv = jnp.zeros((1, LANES), F32)
                for h in range(N_HEADS):
                    val = jnp.sum(jnp.where(bk == bucket, dbias_ref[h], 0.0), axis=1, keepdims=True)
                    val = jnp.sum(val, axis=0, keepdims=True)
                    rowv = jnp.where(lane == h, val, rowv)
                dtab_ref[bucket:bucket + 1, :] = rowv

    return pl.pallas_call(
        body, name="swa_bwd",
        grid=(nb,),
        in_specs=_swa_specs(s_len) + [pl.BlockSpec((blk, N_HEADS * LANES), lambda n: (n, 0))],
        out_specs=[pl.BlockSpec((blk, N_HEADS * LANES), lambda n: (n, 0)),
                   pl.BlockSpec((s_len, kvw), lambda n: (0, 0)), pl.BlockSpec((s_len, kvw), lambda n: (0, 0)),
                   pl.BlockSpec((REL_BUCKETS, LANES), lambda n: (0, 0)), pl.BlockSpec((N_HEADS, LANES), lambda n: (0, 0))],
        out_shape=[jax.ShapeDtypeStruct((s_len, N_HEADS * LANES), BF16),
                   jax.ShapeDtypeStruct((s_len, kvw), F32), jax.ShapeDtypeStruct((s_len, kvw), F32),
                   jax.ShapeDtypeStruct((REL_BUCKETS, LANES), F32), jax.ShapeDtypeStruct((N_HEADS, LANES), F32)],
        scratch_shapes=[pltpu.VMEM((N_HEADS, blk, 2 * blk), F32), pltpu.VMEM((N_HEADS, blk, 2 * blk), F32)],
        compiler_params=_cparams("arbitrary"),
    )(tab, sinks, bkt, q, k, k, v, v, do)


def _sb_terms(z, valid):
    zc = jnp.minimum(z, SB_LOGIT_CAP)
    lk = -jnp.log(1.0 + jnp.exp(zc))
    lsz = zc + lk
    return lsz, (lk if valid is None else jnp.where(valid, lk, 0.0))


def _bf16_parts(vals):
    parts, rest = [], vals
    for n in range(SB_SUM_PARTS):
        parts.append(rest.astype(BF16))
        if n + 1 < SB_SUM_PARTS:
            rest = rest - parts[-1].astype(F32)
    return parts[0] if len(parts) == 1 else jnp.concatenate(parts, axis=1)


def _row_sum_lanes(vals):
    return jnp.broadcast_to(jnp.sum(vals, axis=-1, keepdims=True), (vals.shape[0], LANES))


def _emit_skewed(*groups):
    for step in range(max(len(items) + len(stages) - 1 for items, stages in groups)):
        for items, stages in groups:
            for s, stage in enumerate(stages):
                if 0 <= step - s < len(items):
                    stage(items[step - s])


def _sb_items(edge):
    items = []
    for h in range(2):
        for r0 in range(0, SB_QUERIES, SB_ROWS):
            if edge is None or r0 >= (edge + 1) * SB_KEYS:
                items.append((h, r0, False))
            elif r0 + SB_ROWS - 1 > edge * SB_KEYS:
                items.append((h, r0, True))
    return items


def _sb_valid(w, edge):
    row = lax.broadcasted_iota(jnp.int32, (SB_ROWS, SB_KEYS), 0) + w[1]
    col = lax.broadcasted_iota(jnp.int32, (SB_ROWS, SB_KEYS), 1) + edge * SB_KEYS
    return col < row


def _sb_consts(tq, tk):
    low = lax.broadcasted_iota(jnp.int32, (tq, LANES), 1) < HEAD_DIM
    row = lax.broadcasted_iota(jnp.int32, (tk, tk), 0)
    col = lax.broadcasted_iota(jnp.int32, (tk, tk), 1)
    right = (row > col).astype(BF16)
    left = (row < col).astype(BF16)
    return low, jnp.concatenate([right] * SB_SUM_PARTS, axis=0), jnp.concatenate([left] * SB_SUM_PARTS, axis=0)


def _sb_fwd(q, kt, v):
    s_len = q.shape[0]
    tq, tk, tr = SB_QUERIES, SB_KEYS, SB_ROWS
    nk, ratio = s_len // tk, tq // tk
    assert nk <= LANES

    def body(q_ref, kt_ref, v_ref, o_ref, car_ref, c_ref, oacc_ref, logw_ref, lksum_ref):
        i = pl.program_id(1)
        qv = q_ref[...]
        low, tri2, _ = _sb_consts(tq, tk)
        lane = lax.broadcasted_iota(jnp.int32, (tr, LANES), 1)
        zero = jnp.zeros_like(qv)
        q_heads = (jnp.where(low, qv, zero), jnp.where(low, zero, qv))
        c_ref[...] = jnp.zeros_like(c_ref)
        oacc_ref[...] = jnp.zeros_like(oacc_ref)
        car_ref[...] = jnp.full_like(car_ref, NEG_BIG)

        def front(j, edge):
            ktv = kt_ref[0, j]
            slot = j % 2
            st = {}

            def s_logits(w):
                st[w, "z"] = _dot(q_heads[w[0]][w[1]:w[1] + tr], ktv)

            def s_terms(w):
                valid = _sb_valid(w, edge) if w[2] else None
                lsz, lk = _sb_terms(st.pop((w, "z")), valid)
                st[w, "parts"] = _bf16_parts(lk)
                st[w, "lsz"] = lsz if valid is None else jnp.where(valid, lsz, NEG_BIG)
                lksum_ref[slot, w[0], w[1]:w[1] + tr, :] = _row_sum_lanes(lk)

            def s_suffix(w):
                logw_ref[slot, w[0], w[1]:w[1] + tr, :] = st.pop((w, "lsz")) + _dot(st.pop((w, "parts")), tri2)

            return _sb_items(edge), [s_logits, s_terms, s_suffix]

        def back(j, edge):
            vv = v_ref[pl.ds(pl.multiple_of(j * tk, tk), tk), :]
            slot = j % 2
            st = {}

            def s_weights(w):
                h, rs = w[0], slice(w[1], w[1] + tr)
                c = c_ref[h, rs, :]
                st[w, "a"] = jnp.exp(logw_ref[slot, h, rs, :] + jnp.tile(c, (1, tk // LANES))).astype(BF16)
                car_ref[h, rs, :] = jnp.where(lane == j, c, car_ref[h, rs, :])
                c_ref[h, rs, :] = c + lksum_ref[slot, h, rs, :]

            def s_values(w):
                oacc_ref[w[0], w[1]:w[1] + tr, :] += _dot(st.pop((w, "a")), vv)

            return _sb_items(edge), [s_weights, s_values]

        first = i * ratio
        _emit_skewed(front(first + ratio - 1, ratio - 1))
        for m in reversed(range(ratio - 1)):
            _emit_skewed(front(first + m, m), back(first + m + 1, m + 1))

        @pl.when(i == 0)
        def _():
            _emit_skewed(back(0, 0))

        def alive():
            return (jnp.max(c_ref[...]) >= SB_DEAD_CARRY).astype(jnp.int32)

        @pl.when(i > 0)
        def _():
            _emit_skewed(front(first - 1, None), back(first, 0))

            def step(state):
                pending, _ = state
                _emit_skewed(front(pending - 1, None), back(pending, None))
                return pending - 1, alive()

            pending, live = lax.while_loop(lambda s: (s[0] > 0) & (s[1] > 0), step, (first - 1, alive()))

            @pl.when(live > 0)
            def _():
                _emit_skewed(back(pending, None))

        o_ref[...] = jnp.where(low, oacc_ref[0], oacc_ref[1]).astype(BF16)

    return pl.pallas_call(
        body, name="sb_fwd",
        grid=(N_HEADS // 2, s_len // tq),
        in_specs=[pl.BlockSpec((tq, LANES), lambda p, i: (i, p)),
                  pl.BlockSpec((1, nk, LANES, tk), lambda p, i: (p, 0, 0, 0)),
                  pl.BlockSpec((s_len, LANES), lambda p, i: (0, p))],
        out_specs=[pl.BlockSpec((tq, LANES), lambda p, i: (i, p)), pl.BlockSpec((2, tq, LANES), lambda p, i: (p, i, 0))],
        out_shape=[jax.ShapeDtypeStruct((s_len, N_HEADS * HEAD_DIM), BF16),
                   jax.ShapeDtypeStruct((N_HEADS, s_len, LANES), F32)],
        scratch_shapes=[pltpu.VMEM((2, tq, LANES), F32), pltpu.VMEM((2, tq, LANES), F32),
                        pltpu.VMEM((2, 2, tq, tk), F32), pltpu.VMEM((2, 2, tq, LANES), F32)],
        compiler_params=_cparams("parallel", "arbitrary"),
    )(q, kt, v)


def _sb_bwd(q, qt, kt, k, vt, do, dot, cars):
    s_len = q.shape[0]
    tq, tk, tr = SB_QUERIES, SB_KEYS, SB_ROWS
    nk, ratio = s_len // tk, tq // tk

    def body(q_ref, qt_ref, kt_ref, k_ref, vt_ref, do_ref, dot_ref, car_ref, dq_ref, dk_ref, dv_ref,
             gleft_ref, dqacc_ref, dkacc_ref, dvacc_ref, logw_ref, lsz_ref, da_ref, a_ref, dz_ref):
        i = pl.program_id(1)

        @pl.when(i == 0)
        def _():
            dkacc_ref[...] = jnp.zeros_like(dkacc_ref)
            dvacc_ref[...] = jnp.zeros_like(dvacc_ref)

        qv = q_ref[...]
        dov = do_ref[...]
        low, tri_right2, tri_left2 = _sb_consts(tq, tk)
        lane = lax.broadcasted_iota(jnp.int32, (tr, LANES), 1)
        zero = jnp.zeros_like(qv)
        q_heads = (jnp.where(low, qv, zero), jnp.where(low, zero, qv))
        do_heads = (jnp.where(low, dov, zero), jnp.where(low, zero, dov))
        q_t = qt_ref[0, 0]
        do_t = dot_ref[0, 0]
        gleft_ref[...] = jnp.zeros_like(gleft_ref)
        dqacc_ref[...] = jnp.zeros_like(dqacc_ref)

        def front(j, edge):
            ktv = kt_ref[0, j]
            vtv = vt_ref[0, j]
            slot = j % 2
            st = {}

            def s_logits(w):
                h, rs = w[0], slice(w[1], w[1] + tr)
                st[w, "z"] = _dot(q_heads[h][rs], ktv)
                da_ref[slot, h, rs, :] = _dot(do_heads[h][rs], vtv)

            def s_terms(w):
                h, rs = w[0], slice(w[1], w[1] + tr)
                valid = _sb_valid(w, edge) if w[2] else None
                lsz, lk = _sb_terms(st.pop((w, "z")), valid)
                st[w, "parts"] = _bf16_parts(lk)
                lsz = lsz if valid is None else jnp.where(valid, lsz, NEG_BIG)
                lsz_ref[slot, h, rs, :] = lsz
                st[w, "lszc"] = lsz + jnp.sum(jnp.where(lane == j, car_ref[h, rs, :], 0.0), axis=-1, keepdims=True)

            def s_suffix(w):
                logw_ref[slot, w[0], w[1]:w[1] + tr, :] = st.pop((w, "lszc")) + _dot(st.pop((w, "parts")), tri_right2)

            return _sb_items(edge), [s_logits, s_terms, s_suffix]

        def back(j, edge):
            kv = k_ref[pl.ds(pl.multiple_of(j * tk, tk), tk), :]
            slot = j % 2
            st = {}

            items = _sb_items(edge)
            head_rows = [[r0 for hh, r0, _ in items if hh == h] for h in range(2)]

            def s_weights(w):
                h, rs = w[0], slice(w[1], w[1] + tr)
                a = jnp.exp(logw_ref[slot, h, rs, :])
                g = a * da_ref[slot, h, rs, :]
                a_ref[h, rs, :] = a.astype(BF16)
                st[w, "g"], st[w, "parts"] = g, _bf16_parts(g)

            def s_prefix(w):
                st[w, "gs"] = _dot(st.pop((w, "parts")), tri_left2)

            def s_dz(w):
                h, rs = w[0], slice(w[1], w[1] + tr)
                g = st.pop((w, "g"))
                gleft = gleft_ref[h, rs, :]
                gsum = st.pop((w, "gs")) + jnp.tile(gleft, (1, tk // LANES))
                dz = (g - jnp.exp(lsz_ref[slot, h, rs, :]) * (g + gsum)).astype(BF16)
                st[w, "dz"] = dz
                dz_ref[h, rs, :] = dz
                gleft_ref[h, rs, :] = gleft + _row_sum_lanes(g)

            def s_products(w):
                h, rs = w[0], slice(w[1], w[1] + tr)
                dqacc_ref[h, rs, :] += _dot(st.pop((w, "dz")), kv)
                if w[1] == head_rows[h][-1]:
                    feat = slice(h * HEAD_DIM, (h + 1) * HEAD_DIM)
                    hr = slice(head_rows[h][0], tq)
                    dkacc_ref[j, feat, :] += _dot(q_t[feat, hr], dz_ref[h, hr, :])
                    dvacc_ref[j, feat, :] += _dot(do_t[feat, hr], a_ref[h, hr, :])

            return items, [s_weights, s_prefix, s_dz, s_products]

        first = i * ratio
        tile_max = jnp.max(jnp.maximum(car_ref[0], car_ref[1]), axis=0, keepdims=True)
        start = jnp.clip(first + ratio - jnp.sum(jnp.where(tile_max >= SB_DEAD_CARRY, 1, 0)), 0, first)

        @pl.when(start == first)
        def _():
            _emit_skewed(front(first, 0))

        @pl.when(start < first)
        def _():
            _emit_skewed(front(start, None))

            def step(jj, carry):
                _emit_skewed(front(jj, None), back(jj - 1, None))
                return carry

            lax.fori_loop(start + 1, first, step, 0)
            _emit_skewed(front(first, 0), back(first - 1, None))

        for m in range(1, ratio):
            _emit_skewed(front(first + m, m), back(first + m - 1, m - 1))
        _emit_skewed(back(first + ratio - 1, ratio - 1))
        dq_ref[...] = (Q_SCALE * jnp.where(low, dqacc_ref[0], dqacc_ref[1])).astype(BF16)

        @pl.when(i == s_len // tq - 1)
        def _():
            dk_ref[0] = dkacc_ref[...].astype(BF16)
            dv_ref[0] = dvacc_ref[...].astype(BF16)

    qblk = pl.BlockSpec((tq, LANES), lambda p, i: (i, p))
    qtblk = pl.BlockSpec((1, 1, LANES, tq), lambda p, i: (p, i, 0, 0))
    tblk = pl.BlockSpec((1, nk, LANES, tk), lambda p, i: (p, 0, 0, 0))
    col_full = pl.BlockSpec((s_len, LANES), lambda p, i: (0, p))
    tshape = jax.ShapeDtypeStruct((N_HEADS // 2, nk, LANES, tk), BF16)
    return pl.pallas_call(
        body, name="sb_bwd",
        grid=(N_HEADS // 2, s_len // tq),
        in_specs=[qblk, qtblk, tblk, col_full, tblk, qblk, qtblk, pl.BlockSpec((2, tq, LANES), lambda p, i: (p, i, 0))],
        out_specs=[qblk, tblk, tblk],
        out_shape=[jax.ShapeDtypeStruct((s_len, N_HEADS * HEAD_DIM), BF16), tshape, tshape],
        scratch_shapes=[pltpu.VMEM((2, tq, LANES), F32), pltpu.VMEM((2, tq, LANES), F32),
                        pltpu.VMEM((nk, LANES, tk), F32), pltpu.VMEM((nk, LANES, tk), F32)]
        + [pltpu.VMEM((2, 2, tq, tk), F32)] * 3 + [pltpu.VMEM((2, tq, tk), BF16)] * 2,
        compiler_params=_cparams("parallel", "arbitrary"),
    )(q, qt, kt, k, vt, do, dot, cars)


def _pad_heads(a, heads):
    s_len = a.shape[0]
    a = a.reshape(s_len, heads, HEAD_DIM)
    return jnp.pad(a, ((0, 0), (0, 0), (0, LANES - HEAD_DIM))).reshape(s_len, heads * LANES)


def _unpad_heads(a, heads):
    s_len = a.shape[0]
    return a.reshape(s_len, heads, LANES)[:, :, :HEAD_DIM].reshape(s_len, heads * HEAD_DIM)


def _tile_transposed(a, groups, t):
    s_len = a.shape[0]
    return a.reshape(s_len // t, t, groups, LANES).transpose(2, 0, 3, 1)


def _tile_untransposed(a):
    groups, nt, _, t = a.shape
    return a.transpose(1, 3, 0, 2).reshape(nt * t, groups * LANES)


def _local_step(xs, tgt, gains, sinks, rel_bias, weights_of, send_early):
    g1, gmix, g2, gfin = gains
    bkt = _rel_bucket_matrix()
    groups = N_HEADS // 2

    wts = dict(weights_of(0, xs))
    x1, h1, a1, b1, u1 = _ffn_fwd(xs, g1, wts["ffn1_w1t"], wts["ffn1_w3t"], wts["ffn1_w2"], "1")
    wts.update(weights_of(1, x1))
    hm, qa, ka, va, qb, kb, vb, ga, gb = _proj_fwd(x1, gmix, wts["w_int"])
    qa_p, ka_p, va_p = _pad_heads(qa, N_HEADS), _pad_heads(ka, SWA_KV_HEADS), _pad_heads(va, SWA_KV_HEADS)
    oa_p = _swa_fwd(rel_bias, sinks, bkt, qa_p, ka_p, va_p)
    kbt = _tile_transposed(kb, groups, SB_KEYS)
    ob, cars = _sb_fwd(qb, kbt, vb)
    oa = _unpad_heads(oa_p, N_HEADS)
    x2, mg = _merge_fwd(x1, oa, ob, ga, gb, wts["w_swa"], wts["w_sb"], wts["w_out"])
    wts.update(weights_of(2, x2))
    x3, h3, a3, b3, u3 = _ffn_fwd(x2, g2, wts["ffn2_w1t"], wts["ffn2_w3t"], wts["ffn2_w2"], "2")
    dx3, loss, dgfin = _loss_fwd_bwd(x3, tgt, gfin)

    big = {}
    dx2, dg2, da3, db3, dx3b = _ffn_bwd(dx3, x2, g2, a3, b3, wts["ffn2_w1t"], wts["ffn2_w3t"], wts["ffn2_w2"], "2")
    big["ffn2_w1t"] = _matmul_tn(da3, h3, "ffn2_w1")
    big["ffn2_w3t"] = _matmul_tn(db3, h3, "ffn2_w3")
    big["ffn2_w2"] = _matmul_tn(u3, dx3b, "ffn2_w2")
    dep = send_early(2, big)

    doa, dob, dga, dgb, dpa, dpb, dx2b = _merge_bwd(dx2, oa, ob, ga, gb, wts["w_swa"], wts["w_sb"], wts["w_out"], dep)
    big = {}
    big["w_out"] = _matmul_tn(mg, dx2b, "w_out")
    big["w_swa"] = _matmul_tn(oa, dpa, "w_swa")
    big["w_sb"] = _matmul_tn(ob, dpb, "w_sb")

    dqa_p, dka_p, dva_p, dtab, dsink = _swa_bwd(rel_bias, sinks, bkt, qa_p, ka_p, va_p, _pad_heads(doa, N_HEADS))
    dqb, dkbt, dvbt = _sb_bwd(qb, _tile_transposed(qb, groups, SB_QUERIES), kbt, kb,
                              _tile_transposed(vb, groups, SB_KEYS), dob, _tile_transposed(dob, groups, SB_QUERIES), cars)
    dkb, dvb = _tile_untransposed(dkbt), _tile_untransposed(dvbt)
    dpieces = (_unpad_heads(dqa_p, N_HEADS), _unpad_heads(dka_p, SWA_KV_HEADS).astype(BF16),
               _unpad_heads(dva_p, SWA_KV_HEADS).astype(BF16), dqb, dkb, dvb, dga, dgb)
    big["w_int"] = jnp.concatenate([_matmul_tn(dp, hm, f"w_in{p}") for p, dp in enumerate(dpieces)], axis=0)
    dep = send_early(1, big)
    dx1, dgmix = _proj_bwd(dpieces, dx2, x1, gmix, wts["w_int"], dep)
    big = {}

    dx0, dg1, da1, db1, dx1b = _ffn_bwd(dx1, xs, g1, a1, b1, wts["ffn1_w1t"], wts["ffn1_w3t"], wts["ffn1_w2"], "1")
    big["ffn1_w1t"] = _matmul_tn(da1, h1, "ffn1_w1")
    big["ffn1_w3t"] = _matmul_tn(db1, h1, "ffn1_w3")
    big["ffn1_w2"] = _matmul_tn(u1, dx1b, "ffn1_w2")

    small = {"gains": (dg1, dgmix, dg2, dgfin), "sinks": dsink[:, 0], "rel_bias": dtab[:, :N_HEADS]}
    return loss, dx0, big, small


def _my_place():
    return lax.axis_index("x"), lax.axis_index("y"), lax.axis_index("c")


def _flip(v, bit):
    return 1 - v if bit else v


_RELATIONS = tuple((k >> 2 & 1, k >> 1 & 1, k & 1) for k in range(1, N_DEV))


def _gather_weights(wp, tag):
    def body(x_ref, out_ref, send_sems, recv_sems, local_sem):
        x, y, c = _my_place()
        me, sibling = (x, y, c), (x, y, 1 - c)
        chips = [(1 - x, y), (x, 1 - y), (1 - x, 1 - y)]

        def rows(px, py, pc):
            return out_ref.at[4 * px + 2 * py + pc]

        def copy(k, block, to, src=None):
            return pltpu.make_async_remote_copy(
                src_ref=rows(*block) if src is None else src, dst_ref=rows(*block),
                send_sem=send_sems.at[k], recv_sem=recv_sems.at[k],
                device_id=to, device_id_type=pl.DeviceIdType.MESH)

        mine = pltpu.make_async_copy(x_ref, rows(*me), local_sem)
        mine.start()
        first = [copy(0, me, sibling, src=x_ref)]
        first += [copy(1 + j, me, (*chip, c), src=x_ref) for j, chip in enumerate(chips)]
        for cp in first:
            cp.start()
        passed = [copy(4 + j, (*chip, c), sibling) for j, chip in enumerate(chips)]
        for j, chip in enumerate(chips):
            copy(1 + j, (*chip, c), me).wait_recv()
            passed[j].start()
        copy(0, sibling, me).wait_recv()
        for j, chip in enumerate(chips):
            copy(4 + j, (*chip, 1 - c), me).wait_recv()
        for cp in first + passed:
            cp.wait_send()
        mine.wait()

    return pl.pallas_call(
        body, name=f"gather_weights_{tag}",
        out_shape=jax.ShapeDtypeStruct((N_DEV,) + wp.shape, wp.dtype),
        in_specs=[pl.BlockSpec(memory_space=pl.ANY)],
        out_specs=pl.BlockSpec(memory_space=pl.ANY),
        scratch_shapes=[pltpu.SemaphoreType.DMA((7,)), pltpu.SemaphoreType.DMA((7,)), pltpu.SemaphoreType.DMA(())],
    )(wp)


def _exchange_grads(gp, tag):
    def body(g_ref, out_ref, send_sems, recv_sems, local_sem):
        x, y, c = _my_place()
        me = 4 * x + 2 * y + c
        mine = pltpu.make_async_copy(g_ref.at[me], out_ref.at[me], local_sem)
        mine.start()
        copies = []
        for k, (fx, fy, fc) in enumerate(_RELATIONS):
            px, py, pc = _flip(x, fx), _flip(y, fy), _flip(c, fc)
            peer = 4 * px + 2 * py + pc
            copies.append((
                pltpu.make_async_remote_copy(
                    src_ref=g_ref.at[peer], dst_ref=out_ref.at[me], send_sem=send_sems.at[k], recv_sem=recv_sems.at[k],
                    device_id=(px, py, pc), device_id_type=pl.DeviceIdType.MESH),
                pltpu.make_async_remote_copy(
                    src_ref=g_ref.at[peer], dst_ref=out_ref.at[peer], send_sem=send_sems.at[k], recv_sem=recv_sems.at[k],
                    device_id=(px, py, pc), device_id_type=pl.DeviceIdType.MESH)))
        for out_cp, _ in copies:
            out_cp.start()
        for _, in_cp in copies:
            in_cp.wait_recv()
        for out_cp, _ in copies:
            out_cp.wait_send()
        mine.wait()

    return pl.pallas_call(
        body, name=f"exchange_grads_{tag}",
        out_shape=jax.ShapeDtypeStruct(gp.shape, gp.dtype),
        in_specs=[pl.BlockSpec(memory_space=pl.ANY)],
        out_specs=pl.BlockSpec(memory_space=pl.ANY),
        scratch_shapes=[pltpu.SemaphoreType.DMA((7,)), pltpu.SemaphoreType.DMA((7,)), pltpu.SemaphoreType.DMA(())],
    )(gp)


def _peers():
    x, y, c = _my_place()
    out = []
    for k, (fx, fy, fc) in enumerate(_RELATIONS):
        px, py, pc = _flip(x, fx), _flip(y, fy), _flip(c, fc)
        out.append((k, (px, py, pc), 4 * px + 2 * py + pc))
    return out, 4 * x + 2 * y + c


def _adamw(w, g, m, v):
    m = ADAM_B1 * m + (1.0 - ADAM_B1) * g
    v = ADAM_B2 * v + (1.0 - ADAM_B2) * jnp.square(g)
    m_hat = m / (1.0 - ADAM_B1 ** ADAM_STEP)
    v_hat = v / (1.0 - ADAM_B2 ** ADAM_STEP)
    delta = -ADAM_LR * (m_hat / (jnp.sqrt(v_hat) + ADAM_EPS) + ADAM_WD * w)
    return delta, m, v


def _sum_and_adamw(parts, w, m, v, tr, tag):
    rows = w.shape[0]
    assert rows % tr == 0

    def body(p_ref, w_ref, m_ref, v_ref, g_out, d_out, m_out, v_out):
        g = p_ref[0].astype(F32)
        for d in range(1, N_DEV):
            g = g + p_ref[d].astype(F32)
        delta, mn, vn = _adamw(w_ref[...], g, m_ref[...], v_ref[...])
        g_out[...] = g
        d_out[...] = delta
        m_out[...] = mn
        v_out[...] = vn

    sp = pl.BlockSpec((tr, D_MODEL), lambda i: (i, 0))
    return pl.pallas_call(
        body, name=f"sum_and_adamw_{tag}",
        grid=(rows // tr,),
        in_specs=[pl.BlockSpec((N_DEV, tr, D_MODEL), lambda i: (0, i, 0)), sp, sp, sp],
        out_specs=[sp] * 4,
        out_shape=[jax.ShapeDtypeStruct(w.shape, F32)] * 4,
        compiler_params=_cparams("parallel"),
    )(parts, w, m, v)


def _small_allreduce_adamw(part, w, m, v):
    def body(p_ref, w_ref, m_ref, v_ref, g_out, d_out, m_out, v_out, buf, send_sems, recv_sems):
        x, y, c = _my_place()
        me = 4 * x + 2 * y + c
        buf[me] = p_ref[...]
        copies = []
        for k, (fx, fy, fc) in enumerate(_RELATIONS):
            px, py, pc = _flip(x, fx), _flip(y, fy), _flip(c, fc)
            peer = 4 * px + 2 * py + pc
            copies.append((
                pltpu.make_async_remote_copy(
                    src_ref=buf.at[me], dst_ref=buf.at[me], send_sem=send_sems.at[k], recv_sem=recv_sems.at[k],
                    device_id=(px, py, pc), device_id_type=pl.DeviceIdType.MESH),
                pltpu.make_async_remote_copy(
                    src_ref=buf.at[me], dst_ref=buf.at[peer], send_sem=send_sems.at[k], recv_sem=recv_sems.at[k],
                    device_id=(px, py, pc), device_id_type=pl.DeviceIdType.MESH)))
        for out_cp, _ in copies:
            out_cp.start()
        for _, in_cp in copies:
            in_cp.wait_recv()
        for out_cp, _ in copies:
            out_cp.wait_send()
        g = buf[0]
        for d in range(1, N_DEV):
            g = g + buf[d]
        delta, mn, vn = _adamw(w_ref[...], g, m_ref[...], v_ref[...])
        g_out[...] = g
        d_out[...] = delta
        m_out[...] = mn
        v_out[...] = vn

    vm = pl.BlockSpec(memory_space=pltpu.VMEM)
    return pl.pallas_call(
        body, name="small_allreduce_adamw",
        in_specs=[vm] * 4, out_specs=[vm] * 4,
        out_shape=[jax.ShapeDtypeStruct(w.shape, F32)] * 4,
        scratch_shapes=[pltpu.VMEM((N_DEV,) + part.shape, F32),
                        pltpu.SemaphoreType.DMA((7,)), pltpu.SemaphoreType.DMA((7,))],
    )(part, w, m, v)


_TRANSPOSED = ("ffn1_w1", "ffn1_w3", "w_in", "ffn2_w1", "ffn2_w3")
_BRANCH = ("w_branch_swa", "w_branch_sb")


def _pack_shards(t, names):
    parts = []
    for name in names:
        a = t[name][0]
        if name in _TRANSPOSED:
            a = a.T
        elif name in _BRANCH:
            a = a.reshape(64, D_MODEL)
        parts.append(a)
    return jnp.concatenate(parts, axis=0)


def _unpack_shards(p, names):
    out, lo = {}, 0
    for name in names:
        a = p[lo:lo + BIG_ROWS[BIG_NAMES.index(name)]]
        lo += a.shape[0]
        if name in _TRANSPOSED:
            a = a.T
        elif name in _BRANCH:
            a = a.reshape(512, 128)
        out[name] = a[None]
    return out


def _full_weights(wg, names):
    out, lo = {}, 0
    for name in names:
        rows = BIG_ROWS[BIG_NAMES.index(name)]
        a = wg[:, lo:lo + rows]
        lo += rows
        if name in _BRANCH:
            a = a.reshape(N_DEV, 512, 128).transpose(1, 0, 2).reshape(512, D_MODEL)
        out[_GRAD_KEY[name]] = a.reshape(-1, D_MODEL)
    return out


_GRAD_KEY = {"ffn1_w1": "ffn1_w1t", "ffn1_w3": "ffn1_w3t", "ffn1_w2": "ffn1_w2", "w_in": "w_int",
             "w_branch_swa": "w_swa", "w_branch_sb": "w_sb", "w_out": "w_out",
             "ffn2_w1": "ffn2_w1t", "ffn2_w3": "ffn2_w3t", "ffn2_w2": "ffn2_w2"}


def _pack_full_grads(big, names):
    parts = []
    for name in names:
        a = big[_GRAD_KEY[name]]
        if name in _BRANCH:
            a = a.reshape(512, N_DEV, 128).transpose(1, 0, 2)
        parts.append(a.reshape(N_DEV, BIG_ROWS[BIG_NAMES.index(name)], D_MODEL).astype(BF16))
    return jnp.concatenate(parts, axis=1)


_SMALL_NAMES = ("norm_ffn1", "norm_mix", "norm_ffn2", "norm_final", "swa_sinks", "rel_bias")


def _pack_small(vals):
    rows = []
    for a in vals:
        a = a.reshape(-1)
        rows.append(jnp.pad(a, (0, D_MODEL - a.shape[0])))
    rows += [jnp.zeros((D_MODEL,), F32)] * (SMALL_ROWS - len(rows))
    return jnp.stack(rows)


def _unpack_small(p):
    return {"norm_ffn1": p[0:1], "norm_mix": p[1:2], "norm_ffn2": p[2:3], "norm_final": p[3],
            "swa_sinks": p[4:5, :N_HEADS], "rel_bias": p[5, :REL_BUCKETS * N_HEADS].reshape(REL_BUCKETS, N_HEADS)}


ALL_NAMES = ("norm_ffn1", "ffn1_w1", "ffn1_w3", "ffn1_w2", "norm_mix", "w_in", "swa_sinks", "rel_bias",
             "w_branch_swa", "w_branch_sb", "w_out", "norm_ffn2", "ffn2_w1", "ffn2_w3", "ffn2_w2", "norm_final")


def kernel(x, norm_ffn1, ffn1_w1, ffn1_w3, ffn1_w2, norm_mix, w_in, swa_sinks, rel_bias, w_branch_swa, w_branch_sb, w_out, norm_ffn2, ffn2_w1, ffn2_w3, ffn2_w2, norm_final, loss_target, m_norm_ffn1, m_ffn1_w1, m_ffn1_w3, m_ffn1_w2, m_norm_mix, m_w_in, m_swa_sinks, m_rel_bias, m_w_branch_swa, m_w_branch_sb, m_w_out, m_norm_ffn2, m_ffn2_w1, m_ffn2_w3, m_ffn2_w2, m_norm_final, v_norm_ffn1, v_ffn1_w1, v_ffn1_w3, v_ffn1_w2, v_norm_mix, v_w_in, v_swa_sinks, v_rel_bias, v_w_branch_swa, v_w_branch_sb, v_w_out, v_norm_ffn2, v_ffn2_w1, v_ffn2_w3, v_ffn2_w2, v_norm_final):
    w = dict(zip(ALL_NAMES, (norm_ffn1, ffn1_w1, ffn1_w3, ffn1_w2, norm_mix, w_in, swa_sinks, rel_bias,
                             w_branch_swa, w_branch_sb, w_out, norm_ffn2, ffn2_w1, ffn2_w3, ffn2_w2, norm_final)))
    m = dict(zip(ALL_NAMES, (m_norm_ffn1, m_ffn1_w1, m_ffn1_w3, m_ffn1_w2, m_norm_mix, m_w_in, m_swa_sinks, m_rel_bias,
                             m_w_branch_swa, m_w_branch_sb, m_w_out, m_norm_ffn2, m_ffn2_w1, m_ffn2_w3, m_ffn2_w2,
                             m_norm_final)))
    v = dict(zip(ALL_NAMES, (v_norm_ffn1, v_ffn1_w1, v_ffn1_w3, v_ffn1_w2, v_norm_mix, v_w_in, v_swa_sinks, v_rel_bias,
                             v_w_branch_swa, v_w_branch_sb, v_w_out, v_norm_ffn2, v_ffn2_w1, v_ffn2_w3, v_ffn2_w2,
                             v_norm_final)))

    w_packed = [_pack_shards(w, names) for names in GROUPS]
    gathered = [_gather_weights(wp.astype(BF16), f"group{group}") for group, wp in enumerate(w_packed)]

    def weights_of(group, after):
        return _full_weights(gathered[group], GROUPS[group])

    complete = {}

    def send_early(group, grads):
        complete[group] = dict(grads)
        return jnp.zeros((8, LANES), F32)

    gains = (norm_ffn1, norm_mix, norm_ffn2, norm_final.reshape(1, D_MODEL))
    loss, dx, complete[0], small = _local_step(x[0], loss_target[0], gains, swa_sinks, rel_bias, weights_of, send_early)
    parts = {group: _exchange_grads(_pack_full_grads(complete[group], GROUPS[group]), f"group{group}")
             for group in (2, 1, 0)}

    big_outs = [{}, {}, {}, {}]
    for group, names in enumerate(GROUPS):
        res = _sum_and_adamw(parts[group], w_packed[group], _pack_shards(m, names), _pack_shards(v, names),
                             GROUP_TILE[group], f"group{group}")
        for acc, packed in zip(big_outs, res):
            acc.update(_unpack_shards(packed, names))
    g_big, d_big, m_big, v_big = big_outs

    small_part = _pack_small(small["gains"] + (small["sinks"], small["rel_bias"], loss))
    zero = jnp.zeros((1,), F32)
    small_res = _small_allreduce_adamw(
        small_part, _pack_small([w[n] for n in _SMALL_NAMES] + [zero]), _pack_small([m[n] for n in _SMALL_NAMES] + [zero]),
        _pack_small([v[n] for n in _SMALL_NAMES] + [zero]))
    g_sm, d_sm, m_sm, v_sm = (_unpack_small(p) for p in small_res)

    outs = [small_res[0][len(_SMALL_NAMES), 0], dx[None]]
    for big_d, small_d in ((g_big, g_sm), (d_big, d_sm), (m_big, m_sm), (v_big, v_sm)):
        merged = {**big_d, **small_d}
        outs += [merged[n] for n in ALL_NAMES]
    return tuple(outs)
```

```python
import functools

import jax
import jax.numpy as jnp
import numpy as np
from jax import lax
from jax.experimental import pallas as pl
from jax.experimental.pallas import tpu as pltpu

F32 = jnp.float32
BF16 = jnp.bfloat16

D_MODEL = 1024
D_FF = 2816
HEAD_DIM = 64
N_HEADS = 8
SWA_KV_HEADS = 2
SWA_GROUP = 4
SWA_BLOCK = 128
REL_BUCKETS = 32
REL_MAX_DIST = 128
RMS_EPS = 1e-6
NEG_BIG = -1e30
Q_SCALE = HEAD_DIM ** -0.5
LANES = 128

N_DEV = 8

ADAM_LR = 0.001
ADAM_B1 = 0.9
ADAM_B2 = 0.999
ADAM_EPS = 1e-08
ADAM_WD = 0.01
ADAM_STEP = 10

IN_SIZES = (512, 128, 128, 512, 512, 512, 1024, 1024)
IN_OFFS = tuple(int(v) for v in np.cumsum((0,) + IN_SIZES))
IN_W = IN_OFFS[-1]

BIG_NAMES = ("ffn1_w1", "ffn1_w3", "ffn1_w2", "w_in", "w_branch_swa", "w_branch_sb", "w_out",
             "ffn2_w1", "ffn2_w3", "ffn2_w2")
BIG_ROWS = (352, 352, 352, 544, 64, 64, 128, 352, 352, 352)
SMALL_ROWS = 8
GROUPS = (BIG_NAMES[0:3], BIG_NAMES[3:7], BIG_NAMES[7:10])
GROUP_TILE = (96, 160, 96)

VMEM_LIMIT = 56 * 1024 * 1024
FFN_PIECES = 2
SB_QUERIES = 512
SB_KEYS = 256
SB_ROWS = 256
SB_SUM_PARTS = 1
SB_LOGIT_CAP = 80.0
SB_DEAD_CARRY = -110.0


def _dot(a, b):
    return jnp.dot(a, b, preferred_element_type=F32)


def _dot_nt(a, b):
    return lax.dot_general(a, b, (((1,), (1,)), ((), ())), preferred_element_type=F32)


def _dot_tn(a, b):
    return lax.dot_general(a, b, (((0,), (0,)), ((), ())), preferred_element_type=F32)


def _cparams(*sem):
    return pltpu.CompilerParams(dimension_semantics=sem, vmem_limit_bytes=VMEM_LIMIT)


def _rms_rstd(xv):
    return lax.rsqrt(jnp.mean(xv * xv, axis=-1, keepdims=True) + RMS_EPS)


def _rms_bwd(dh, xv, r, g):
    xhat = xv * r
    dg = jnp.sum(dh * xhat, axis=0, keepdims=True)
    dxn = dh * g
    dx = r * (dxn - xhat * jnp.mean(dxn * xhat, axis=-1, keepdims=True))
    return dx, dg


def _ffn_fwd(x, g, w1t, w3t, w2, tag, comm=None):
    s_len = x.shape[0]
    tm, tf = min(1024, s_len), 256
    nf = D_FF // tf

    def body(x_ref, g_ref, w1_ref, w3_ref, w2_ref, xo_ref, h_ref, a_ref, b_ref, u_ref, acc_ref, hs_ref):
        j = pl.program_id(1)

        @pl.when(j == 0)
        def _():
            xv = x_ref[...]
            h = (xv * _rms_rstd(xv) * g_ref[...]).astype(BF16)
            hs_ref[...] = h
            h_ref[...] = h
            acc_ref[...] = jnp.zeros_like(acc_ref)

        st = {}

        def s_up(rs):
            h = hs_ref[rs, :]
            st[rs.start, "ab"] = (_dot_nt(h, w1_ref[...]), _dot_nt(h, w3_ref[...]))

        def s_act(rs):
            a, b = st.pop((rs.start, "ab"))
            a_ref[rs, :] = a.astype(BF16)
            b_ref[rs, :] = b.astype(BF16)
            uh = (0.5 * (a * jax.nn.sigmoid(a) * b)).astype(BF16)
            u_ref[rs, :] = uh
            st[rs.start, "u"] = uh

        def s_down(rs):
            acc_ref[rs, :] += _dot(st.pop((rs.start, "u")), w2_ref[...])

        _emit_skewed(([slice(r, r + tm // FFN_PIECES) for r in range(0, tm, tm // FFN_PIECES)], [s_up, s_act, s_down]))

        @pl.when(j == nf - 1)
        def _():
            xo_ref[...] = x_ref[...] + acc_ref[...]

    row = lambda i, j: (i, 0)
    return _call(
        body, (x, g, w1t, w3t, w2), comm=comm, **_grid_ends(s_len // tm, nf), name=f"ffn_fwd_{tag}",
        grid=(s_len // tm, nf),
        in_specs=[pl.BlockSpec((tm, D_MODEL), row), pl.BlockSpec((1, D_MODEL), lambda i, j: (0, 0)),
                  pl.BlockSpec((tf, D_MODEL), lambda i, j: (j, 0)), pl.BlockSpec((tf, D_MODEL), lambda i, j: (j, 0)),
                  pl.BlockSpec((tf, D_MODEL), lambda i, j: (j, 0))],
        out_specs=[pl.BlockSpec((tm, D_MODEL), row), pl.BlockSpec((tm, D_MODEL), row),
                   pl.BlockSpec((tm, tf), lambda i, j: (i, j)), pl.BlockSpec((tm, tf), lambda i, j: (i, j)),
                   pl.BlockSpec((tm, tf), lambda i, j: (i, j))],
        out_shape=[jax.ShapeDtypeStruct((s_len, D_MODEL), F32), jax.ShapeDtypeStruct((s_len, D_MODEL), BF16),
                   jax.ShapeDtypeStruct((s_len, D_FF), BF16), jax.ShapeDtypeStruct((s_len, D_FF), BF16),
                   jax.ShapeDtypeStruct((s_len, D_FF), BF16)],
        scratch_shapes=[pltpu.VMEM((tm, D_MODEL), F32), pltpu.VMEM((tm, D_MODEL), BF16)],
        compiler_params=_cparams("arbitrary", "arbitrary"),
    )


def _ffn_bwd(dy, x, g, a, b, w1t, w3t, w2, tag, comm=None):
    s_len = x.shape[0]
    tm, tf = min(1024, s_len), 256
    nf = D_FF // tf

    def body(dy_ref, x_ref, g_ref, a_ref, b_ref, w1_ref, w3_ref, w2_ref,
             dx_ref, dg_ref, da_ref, db_ref, dyb_ref, acc_ref, dys_ref):
        i, j = pl.program_id(0), pl.program_id(1)

        @pl.when(j == 0)
        def _():
            dyb = dy_ref[...].astype(BF16)
            dys_ref[...] = dyb
            dyb_ref[...] = dyb
            acc_ref[...] = jnp.zeros_like(acc_ref)

        @pl.when((i == 0) & (j == 0))
        def _():
            dg_ref[...] = jnp.zeros_like(dg_ref)

        st = {}

        def s_du(rs):
            st[rs.start, "du"] = 0.5 * _dot_nt(dys_ref[rs, :], w2_ref[...])

        def s_act(rs):
            du = st.pop((rs.start, "du"))
            av = a_ref[rs, :].astype(F32)
            bv = b_ref[rs, :].astype(F32)
            sg = jax.nn.sigmoid(av)
            sil = av * sg
            da = (du * bv * (sg + sil * (1.0 - sg))).astype(BF16)
            db = (du * sil).astype(BF16)
            da_ref[rs, :] = da
            db_ref[rs, :] = db
            st[rs.start, "dab"] = (da, db)

        def s_dh(rs):
            da, db = st.pop((rs.start, "dab"))
            acc_ref[rs, :] += _dot(da, w1_ref[...]) + _dot(db, w3_ref[...])

        _emit_skewed(([slice(r, r + tm // FFN_PIECES) for r in range(0, tm, tm // FFN_PIECES)], [s_du, s_act, s_dh]))

        @pl.when(j == nf - 1)
        def _():
            xv = x_ref[...]
            dx, dg = _rms_bwd(acc_ref[...], xv, _rms_rstd(xv), g_ref[...])
            dx_ref[...] = dy_ref[...] + dx
            dg_ref[...] += dg

    row = lambda i, j: (i, 0)
    blk = lambda i, j: (i, j)
    wsp = pl.BlockSpec((tf, D_MODEL), lambda i, j: (j, 0))
    return _call(
        body, (dy, x, g, a, b, w1t, w3t, w2), comm=comm, **_grid_ends(s_len // tm, nf), name=f"ffn_bwd_{tag}",
        grid=(s_len // tm, nf),
        in_specs=[pl.BlockSpec((tm, D_MODEL), row), pl.BlockSpec((tm, D_MODEL), row),
                  pl.BlockSpec((1, D_MODEL), lambda i, j: (0, 0)),
                  pl.BlockSpec((tm, tf), blk), pl.BlockSpec((tm, tf), blk), wsp, wsp, wsp],
        out_specs=[pl.BlockSpec((tm, D_MODEL), row), pl.BlockSpec((1, D_MODEL), lambda i, j: (0, 0)),
                   pl.BlockSpec((tm, tf), blk), pl.BlockSpec((tm, tf), blk), pl.BlockSpec((tm, D_MODEL), row)],
        out_shape=[jax.ShapeDtypeStruct((s_len, D_MODEL), F32), jax.ShapeDtypeStruct((1, D_MODEL), F32),
                   jax.ShapeDtypeStruct((s_len, D_FF), BF16), jax.ShapeDtypeStruct((s_len, D_FF), BF16),
                   jax.ShapeDtypeStruct((s_len, D_MODEL), BF16)],
        scratch_shapes=[pltpu.VMEM((tm, D_MODEL), F32), pltpu.VMEM((tm, D_MODEL), BF16)],
        compiler_params=_cparams("arbitrary", "arbitrary"),
    )


def _matmul_tn(lhs, rhs, tag):
    s_len, m = lhs.shape
    n = rhs.shape[1]
    tm = min(512, s_len)
    tj = m if m <= 1024 else 1408
    assert m % tj == 0

    def body(l_ref, r_ref, o_ref):
        @pl.when(pl.program_id(1) == 0)
        def _():
            o_ref[...] = jnp.zeros_like(o_ref)

        o_ref[...] += _dot_tn(l_ref[...], r_ref[...])

    return pl.pallas_call(
        body, name=f"matmul_tn_{tag}",
        grid=(m // tj, s_len // tm),
        in_specs=[pl.BlockSpec((tm, tj), lambda j, i: (i, j)), pl.BlockSpec((tm, n), lambda j, i: (i, 0))],
        out_specs=pl.BlockSpec((tj, n), lambda j, i: (j, 0)),
        out_shape=jax.ShapeDtypeStruct((m, n), F32),
        compiler_params=_cparams("parallel", "arbitrary"),
    )(lhs, rhs)


def _proj_fwd(x1, g, wint):
    s_len = x1.shape[0]
    tm = min(512, s_len)
    dts = (BF16, BF16, BF16, BF16, BF16, BF16, F32, F32)

    def body(x_ref, g_ref, w_ref, h_ref, *outs):
        xv = x_ref[...]
        h = (xv * _rms_rstd(xv) * g_ref[...]).astype(BF16)
        h_ref[...] = h
        for p, o_ref in enumerate(outs):
            val = _dot_nt(h, w_ref[IN_OFFS[p]:IN_OFFS[p + 1], :])
            if p == 3:
                val = val * Q_SCALE
            o_ref[...] = val.astype(dts[p])

    row = lambda i: (i, 0)
    return pl.pallas_call(
        body, name="proj_fwd",
        grid=(s_len // tm,),
        in_specs=[pl.BlockSpec((tm, D_MODEL), row), pl.BlockSpec((1, D_MODEL), lambda i: (0, 0)),
                  pl.BlockSpec((IN_W, D_MODEL), lambda i: (0, 0))],
        out_specs=[pl.BlockSpec((tm, D_MODEL), row)] + [pl.BlockSpec((tm, w), row) for w in IN_SIZES],
        out_shape=[jax.ShapeDtypeStruct((s_len, D_MODEL), BF16)]
        + [jax.ShapeDtypeStruct((s_len, w), dt) for w, dt in zip(IN_SIZES, dts)],
        compiler_params=_cparams("parallel"),
    )(x1, g, wint)


def _proj_bwd(dpieces, dx2, x1, g, wint):
    s_len = x1.shape[0]
    tm = min(512, s_len)

    def body(*refs):
        dps = refs[:8]
        dx2_ref, x_ref, g_ref, w_ref, dx_ref, dg_ref = refs[8:]

        @pl.when(pl.program_id(0) == 0)
        def _():
            dg_ref[...] = jnp.zeros_like(dg_ref)

        dh = _dot(dps[0][...], w_ref[IN_OFFS[0]:IN_OFFS[1], :])
        for p in range(1, 8):
            dh += _dot(dps[p][...], w_ref[IN_OFFS[p]:IN_OFFS[p + 1], :])
        xv = x_ref[...]
        dx, dg = _rms_bwd(dh, xv, _rms_rstd(xv), g_ref[...])
        dx_ref[...] = dx2_ref[...] + dx
        dg_ref[...] += dg

    row = lambda i: (i, 0)
    return pl.pallas_call(
        body, name="proj_bwd",
        grid=(s_len // tm,),
        in_specs=[pl.BlockSpec((tm, w), row) for w in IN_SIZES]
        + [pl.BlockSpec((tm, D_MODEL), row), pl.BlockSpec((tm, D_MODEL), row),
           pl.BlockSpec((1, D_MODEL), lambda i: (0, 0)), pl.BlockSpec((IN_W, D_MODEL), lambda i: (0, 0))],
        out_specs=[pl.BlockSpec((tm, D_MODEL), row), pl.BlockSpec((1, D_MODEL), lambda i: (0, 0))],
        out_shape=[jax.ShapeDtypeStruct((s_len, D_MODEL), F32), jax.ShapeDtypeStruct((1, D_MODEL), F32)],
        compiler_params=_cparams("arbitrary"),
    )(*dpieces, dx2, x1, g, wint)


def _merge_fwd(x1, oa, ob, ga, gb, wswa, wsb, wout):
    s_len = x1.shape[0]
    tm = min(512, s_len)

    def body(x_ref, oa_ref, ob_ref, ga_ref, gb_ref, wa_ref, wb_ref, wo_ref, xo_ref, mg_ref):
        pa = _dot(oa_ref[...], wa_ref[...])
        pb = _dot(ob_ref[...], wb_ref[...])
        mg = (jax.nn.sigmoid(ga_ref[...]) * pa + jax.nn.sigmoid(gb_ref[...]) * pb).astype(BF16)
        mg_ref[...] = mg
        xo_ref[...] = x_ref[...] + _dot(mg, wo_ref[...])

    row = lambda i: (i, 0)
    full = lambda i: (0, 0)
    return pl.pallas_call(
        body, name="merge_fwd",
        grid=(s_len // tm,),
        in_specs=[pl.BlockSpec((tm, D_MODEL), row), pl.BlockSpec((tm, 512), row), pl.BlockSpec((tm, 512), row),
                  pl.BlockSpec((tm, D_MODEL), row), pl.BlockSpec((tm, D_MODEL), row),
                  pl.BlockSpec((512, D_MODEL), full), pl.BlockSpec((512, D_MODEL), full),
                  pl.BlockSpec((D_MODEL, D_MODEL), full)],
        out_specs=[pl.BlockSpec((tm, D_MODEL), row), pl.BlockSpec((tm, D_MODEL), row)],
        out_shape=[jax.ShapeDtypeStruct((s_len, D_MODEL), F32), jax.ShapeDtypeStruct((s_len, D_MODEL), BF16)],
        compiler_params=_cparams("parallel"),
    )(x1, oa, ob, ga, gb, wswa, wsb, wout)


def _merge_bwd(dx2, oa, ob, ga, gb, wswa, wsb, wout):
    s_len = dx2.shape[0]
    tm = min(512, s_len)

    def body(dx_ref, oa_ref, ob_ref, ga_ref, gb_ref, wa_ref, wb_ref, wo_ref,
             doa_ref, dob_ref, dga_ref, dgb_ref, dpa_ref, dpb_ref, dxb_ref):
        dxb = dx_ref[...].astype(BF16)
        dxb_ref[...] = dxb
        dmg = _dot_nt(dxb, wo_ref[...])
        for o_ref, g_ref, w_ref, do_ref, dg_ref, dp_ref in (
                (oa_ref, ga_ref, wa_ref, doa_ref, dga_ref, dpa_ref),
                (ob_ref, gb_ref, wb_ref, dob_ref, dgb_ref, dpb_ref)):
            pv = _dot(o_ref[...], w_ref[...])
            sg = jax.nn.sigmoid(g_ref[...])
            dp = (dmg * sg).astype(BF16)
            dp_ref[...] = dp
            dg_ref[...] = (dmg * pv * sg * (1.0 - sg)).astype(BF16)
            do_ref[...] = _dot_nt(dp, w_ref[...]).astype(BF16)

    row = lambda i: (i, 0)
    full = lambda i: (0, 0)
    wide = pl.BlockSpec((tm, D_MODEL), row)
    half = pl.BlockSpec((tm, 512), row)
    return pl.pallas_call(
        body, name="merge_bwd",
        grid=(s_len // tm,),
        in_specs=[wide, half, half, wide, wide, pl.BlockSpec((512, D_MODEL), full),
                  pl.BlockSpec((512, D_MODEL), full), pl.BlockSpec((D_MODEL, D_MODEL), full)],
        out_specs=[half, half, wide, wide, wide, wide, wide],
        out_shape=[jax.ShapeDtypeStruct((s_len, 512), BF16)] * 2 + [jax.ShapeDtypeStruct((s_len, D_MODEL), BF16)] * 5,
        compiler_params=_cparams("parallel"),
    )(dx2, oa, ob, ga, gb, wswa, wsb, wout)


def _loss_fwd_bwd(x3, tgt, g):
    s_len = x3.shape[0]
    tm = min(1024, s_len)

    def body(x_ref, t_ref, g_ref, dx_ref, loss_ref, dg_ref):
        @pl.when(pl.program_id(0) == 0)
        def _():
            loss_ref[...] = jnp.zeros_like(loss_ref)
            dg_ref[...] = jnp.zeros_like(dg_ref)

        xv = x_ref[...]
        gv = g_ref[...]
        r = _rms_rstd(xv)
        err = xv * r * gv - t_ref[...]
        loss_ref[...] += 0.5 * jnp.sum(jnp.mean(err * err, axis=-1, keepdims=True), axis=0, keepdims=True)
        dx, dg = _rms_bwd(err * (1.0 / D_MODEL), xv, r, gv)
        dx_ref[...] = dx
        dg_ref[...] += dg

    row = lambda i: (i, 0)
    return pl.pallas_call(
        body, name="loss_fwd_bwd",
        grid=(s_len // tm,),
        in_specs=[pl.BlockSpec((tm, D_MODEL), row), pl.BlockSpec((tm, D_MODEL), row),
                  pl.BlockSpec((1, D_MODEL), lambda i: (0, 0))],
        out_specs=[pl.BlockSpec((tm, D_MODEL), row), pl.BlockSpec((1, 1), lambda i: (0, 0)),
                   pl.BlockSpec((1, D_MODEL), lambda i: (0, 0))],
        out_shape=[jax.ShapeDtypeStruct((s_len, D_MODEL), F32), jax.ShapeDtypeStruct((1, 1), F32),
                   jax.ShapeDtypeStruct((1, D_MODEL), F32)],
        compiler_params=_cparams("arbitrary"),
    )(x3, tgt, g)


def _rel_bucket_matrix():
    qi = jnp.arange(SWA_BLOCK)[:, None] + SWA_BLOCK
    kj = jnp.arange(2 * SWA_BLOCK)[None, :]
    dist = jnp.maximum(qi - kj, 0)
    max_exact = REL_BUCKETS // 2
    d = jnp.maximum(dist, 1).astype(F32)
    large = max_exact + (jnp.log(d / max_exact) / np.log(REL_MAX_DIST / max_exact)
                         * (REL_BUCKETS - max_exact)).astype(jnp.int32)
    large = jnp.minimum(large, REL_BUCKETS - 1)
    return jnp.where(dist < max_exact, dist, large).astype(jnp.int32)


def _swa_bias_into(bias_ref, bkt_ref, tab_ref):
    bk = bkt_ref[...]
    for h in range(N_HEADS):
        acc = jnp.zeros(bk.shape, F32)
        for bucket in range(REL_BUCKETS):
            acc = jnp.where(bk == bucket, tab_ref[bucket, h], acc)
        bias_ref[h] = acc


def _swa_valid(n):
    shape = (SWA_BLOCK, 2 * SWA_BLOCK)
    row = lax.broadcasted_iota(jnp.int32, shape, 0)
    col = lax.broadcasted_iota(jnp.int32, shape, 1)
    dist = row + SWA_BLOCK - col
    return (dist >= 0) & (dist < SWA_BLOCK) & ((col >= SWA_BLOCK) | (n > 0))


def _swa_probs(q, k, bias, sink, valid):
    lg = jnp.where(valid, _dot_nt(q, k) * Q_SCALE + bias, NEG_BIG)
    m = jnp.maximum(jnp.max(lg, axis=-1, keepdims=True), sink)
    e = jnp.exp(lg - m)
    es = jnp.exp(sink - m)
    inv = 1.0 / (jnp.sum(e, axis=-1, keepdims=True) + es)
    return e * inv, es * inv


def _swa_specs(s_len):
    blk = SWA_BLOCK
    cur = lambda n: (n, 0)
    prev = lambda n: (jnp.maximum(n - 1, 0), 0)
    kvw = SWA_KV_HEADS * LANES
    return [pl.BlockSpec(memory_space=pltpu.SMEM), pl.BlockSpec(memory_space=pltpu.SMEM),
            pl.BlockSpec((blk, 2 * blk), lambda n: (0, 0)),
            pl.BlockSpec((blk, N_HEADS * LANES), cur),
            pl.BlockSpec((blk, kvw), prev), pl.BlockSpec((blk, kvw), cur),
            pl.BlockSpec((blk, kvw), prev), pl.BlockSpec((blk, kvw), cur)]


def _swa_fwd(tab, sinks, bkt, q, k, v):
    s_len = q.shape[0]
    blk = SWA_BLOCK

    def body(tab_ref, sink_ref, bkt_ref, q_ref, kp_ref, kc_ref, vp_ref, vc_ref, o_ref, bias_ref):
        n = pl.program_id(0)

        @pl.when(n == 0)
        def _():
            _swa_bias_into(bias_ref, bkt_ref, tab_ref)

        valid = _swa_valid(n)
        for grp in range(SWA_KV_HEADS):
            gl = slice(grp * LANES, (grp + 1) * LANES)
            kk = jnp.concatenate([kp_ref[:, gl], kc_ref[:, gl]], axis=0)
            vv = jnp.concatenate([vp_ref[:, gl], vc_ref[:, gl]], axis=0)
            for hh in range(SWA_GROUP):
                h = grp * SWA_GROUP + hh
                hl = slice(h * LANES, (h + 1) * LANES)
                p, _ = _swa_probs(q_ref[:, hl], kk, bias_ref[h], sink_ref[0, h], valid)
                o_ref[:, hl] = _dot(p.astype(BF16), vv).astype(BF16)

    return pl.pallas_call(
        body, name="swa_fwd",
        grid=(s_len // blk,),
        in_specs=_swa_specs(s_len),
        out_specs=pl.BlockSpec((blk, N_HEADS * LANES), lambda n: (n, 0)),
        out_shape=jax.ShapeDtypeStruct((s_len, N_HEADS * LANES), BF16),
        scratch_shapes=[pltpu.VMEM((N_HEADS, blk, 2 * blk), F32)],
        compiler_params=_cparams("arbitrary"),
    )(tab, sinks, bkt, q, k, k, v, v)


def _swa_bwd(tab, sinks, bkt, q, k, v, do, comm=None):
    s_len = q.shape[0]
    blk = SWA_BLOCK
    nb = s_len // blk
    kvw = SWA_KV_HEADS * LANES

    def body(tab_ref, sink_ref, bkt_ref, q_ref, kp_ref, kc_ref, vp_ref, vc_ref, do_ref,
             dq_ref, dk_ref, dv_ref, dtab_ref, dsink_ref, bias_ref, dbias_ref):
        n = pl.program_id(0)

        @pl.when(n == 0)
        def _():
            _swa_bias_into(bias_ref, bkt_ref, tab_ref)
            dbias_ref[...] = jnp.zeros_like(dbias_ref)
            dk_ref[...] = jnp.zeros_like(dk_ref)
            dv_ref[...] = jnp.zeros_like(dv_ref)
            dsink_ref[...] = jnp.zeros_like(dsink_ref)
            dtab_ref[...] = jnp.zeros_like(dtab_ref)

        valid = _swa_valid(n)
        cur_rows = pl.ds(pl.multiple_of(n * blk, blk), blk)
        prev_rows = pl.ds(pl.multiple_of(jnp.maximum(n - 1, 0) * blk, blk), blk)
        for grp in range(SWA_KV_HEADS):
            gl = slice(grp * LANES, (grp + 1) * LANES)
            kk = jnp.concatenate([kp_ref[:, gl], kc_ref[:, gl]], axis=0)
            vv = jnp.concatenate([vp_ref[:, gl], vc_ref[:, gl]], axis=0)
            dk_acc = jnp.zeros((2 * blk, LANES), F32)
            dv_acc = jnp.zeros((2 * blk, LANES), F32)
            for hh in range(SWA_GROUP):
                h = grp * SWA_GROUP + hh
                hl = slice(h * LANES, (h + 1) * LANES)
                qh = q_ref[:, hl]
                doh = do_ref[:, hl]
                p, ps = _swa_probs(qh, kk, bias_ref[h], sink_ref[0, h], valid)
                dp = _dot_nt(doh, vv)
                delta = jnp.sum(p * dp, axis=-1, keepdims=True)
                dl = p * (dp - delta)
                dsink_ref[h:h + 1, :] += jnp.broadcast_to(-jnp.sum(ps * delta, axis=0, keepdims=True), (1, LANES))
                dbias_ref[h] += dl
                dlb = dl.astype(BF16)
                dq_ref[:, hl] = (Q_SCALE * _dot(dlb, kk)).astype(BF16)
                dk_acc += Q_SCALE * _dot_tn(dlb, qh)
                dv_acc += _dot_tn(p.astype(BF16), doh)
            dk_ref[cur_rows, gl] += dk_acc[blk:]
            dv_ref[cur_rows, gl] += dv_acc[blk:]

            @pl.when(n > 0)
            def _():
                dk_ref[prev_rows, gl] += dk_acc[:blk]
                dv_ref[prev_rows, gl] += dv_acc[:blk]

        @pl.when(n == nb - 1)
        def _():
            bk = bkt_ref[...]
            lane = lax.broadcasted_iota(jnp.int32, (1, LANES), 1)
            for bucket in range(REL_BUCKETS):
                rowv = jnp.zeros((1, LANES), F32)
                for h in range(N_HEADS):
                    val = jnp.sum(jnp.where(bk == bucket, dbias_ref[h], 0.0), axis=1, keepdims=True)
                    val = jnp.sum(val, axis=0, keepdims=True)
                    rowv = jnp.where(lane == h, val, rowv)
                dtab_ref[bucket:bucket + 1, :] = rowv

    return _call(
        body, (tab, sinks, bkt, q, k, k, v, v, do), comm=comm, **_grid_ends(nb), name="swa_bwd",
        grid=(nb,),
        in_specs=_swa_specs(s_len) + [pl.BlockSpec((blk, N_HEADS * LANES), lambda n: (n, 0))],
        out_specs=[pl.BlockSpec((blk, N_HEADS * LANES), lambda n: (n, 0)),
                   pl.BlockSpec((s_len, kvw), lambda n: (0, 0)), pl.BlockSpec((s_len, kvw), lambda n: (0, 0)),
                   pl.BlockSpec((REL_BUCKETS, LANES), lambda n: (0, 0)), pl.BlockSpec((N_HEADS, LANES), lambda n: (0, 0))],
        out_shape=[jax.ShapeDtypeStruct((s_len, N_HEADS * LANES), BF16),
                   jax.ShapeDtypeStruct((s_len, kvw), F32), jax.ShapeDtypeStruct((s_len, kvw), F32),
                   jax.ShapeDtypeStruct((REL_BUCKETS, LANES), F32), jax.ShapeDtypeStruct((N_HEADS, LANES), F32)],
        scratch_shapes=[pltpu.VMEM((N_HEADS, blk, 2 * blk), F32), pltpu.VMEM((N_HEADS, blk, 2 * blk), F32)],
        compiler_params=_cparams("arbitrary"),
    )


def _sb_terms(z, valid):
    zc = jnp.minimum(z, SB_LOGIT_CAP)
    lk = -jnp.log(1.0 + jnp.exp(zc))
    lsz = zc + lk
    return lsz, (lk if valid is None else jnp.where(valid, lk, 0.0))


def _bf16_parts(vals):
    parts, rest = [], vals
    for n in range(SB_SUM_PARTS):
        parts.append(rest.astype(BF16))
        if n + 1 < SB_SUM_PARTS:
            rest = rest - parts[-1].astype(F32)
    return parts[0] if len(parts) == 1 else jnp.concatenate(parts, axis=1)


def _row_sum_lanes(vals):
    return jnp.broadcast_to(jnp.sum(vals, axis=-1, keepdims=True), (vals.shape[0], LANES))


def _emit_skewed(*groups):
    for step in range(max(len(items) + len(stages) - 1 for items, stages in groups)):
        for items, stages in groups:
            for s, stage in enumerate(stages):
                if 0 <= step - s < len(items):
                    stage(items[step - s])


def _sb_items(edge):
    items = []
    for h in range(2):
        for r0 in range(0, SB_QUERIES, SB_ROWS):
            if edge is None or r0 >= (edge + 1) * SB_KEYS:
                items.append((h, r0, False))
            elif r0 + SB_ROWS - 1 > edge * SB_KEYS:
                items.append((h, r0, True))
    return items


def _sb_valid(w, edge):
    row = lax.broadcasted_iota(jnp.int32, (SB_ROWS, SB_KEYS), 0) + w[1]
    col = lax.broadcasted_iota(jnp.int32, (SB_ROWS, SB_KEYS), 1) + edge * SB_KEYS
    return col < row


def _sb_consts(tq, tk):
    low = lax.broadcasted_iota(jnp.int32, (tq, LANES), 1) < HEAD_DIM
    row = lax.broadcasted_iota(jnp.int32, (tk, tk), 0)
    col = lax.broadcasted_iota(jnp.int32, (tk, tk), 1)
    right = (row > col).astype(BF16)
    left = (row < col).astype(BF16)
    return low, jnp.concatenate([right] * SB_SUM_PARTS, axis=0), jnp.concatenate([left] * SB_SUM_PARTS, axis=0)


def _sb_fwd(q, kt, v, comm=None):
    s_len = q.shape[0]
    tq, tk, tr = SB_QUERIES, SB_KEYS, SB_ROWS
    nk, ratio = s_len // tk, tq // tk
    assert nk <= LANES

    def body(q_ref, kt_ref, v_ref, o_ref, car_ref, c_ref, oacc_ref, logw_ref, lksum_ref):
        i = pl.program_id(1)
        qv = q_ref[...]
        low, tri2, _ = _sb_consts(tq, tk)
        lane = lax.broadcasted_iota(jnp.int32, (tr, LANES), 1)
        zero = jnp.zeros_like(qv)
        q_heads = (jnp.where(low, qv, zero), jnp.where(low, zero, qv))
        c_ref[...] = jnp.zeros_like(c_ref)
        oacc_ref[...] = jnp.zeros_like(oacc_ref)
        car_ref[...] = jnp.full_like(car_ref, NEG_BIG)

        def front(j, edge):
            ktv = kt_ref[0, j]
            slot = j % 2
            st = {}

            def s_logits(w):
                st[w, "z"] = _dot(q_heads[w[0]][w[1]:w[1] + tr], ktv)

            def s_terms(w):
                valid = _sb_valid(w, edge) if w[2] else None
                lsz, lk = _sb_terms(st.pop((w, "z")), valid)
                st[w, "parts"] = _bf16_parts(lk)
                st[w, "lsz"] = lsz if valid is None else jnp.where(valid, lsz, NEG_BIG)
                lksum_ref[slot, w[0], w[1]:w[1] + tr, :] = _row_sum_lanes(lk)

            def s_suffix(w):
                logw_ref[slot, w[0], w[1]:w[1] + tr, :] = st.pop((w, "lsz")) + _dot(st.pop((w, "parts")), tri2)

            return _sb_items(edge), [s_logits, s_terms, s_suffix]

        def back(j, edge):
            vv = v_ref[pl.ds(pl.multiple_of(j * tk, tk), tk), :]
            slot = j % 2
            st = {}

            def s_weights(w):
                h, rs = w[0], slice(w[1], w[1] + tr)
                c = c_ref[h, rs, :]
                st[w, "a"] = jnp.exp(logw_ref[slot, h, rs, :] + jnp.tile(c, (1, tk // LANES))).astype(BF16)
                car_ref[h, rs, :] = jnp.where(lane == j, c, car_ref[h, rs, :])
                c_ref[h, rs, :] = c + lksum_ref[slot, h, rs, :]

            def s_values(w):
                oacc_ref[w[0], w[1]:w[1] + tr, :] += _dot(st.pop((w, "a")), vv)

            return _sb_items(edge), [s_weights, s_values]

        first = i * ratio
        _emit_skewed(front(first + ratio - 1, ratio - 1))
        for m in reversed(range(ratio - 1)):
            _emit_skewed(front(first + m, m), back(first + m + 1, m + 1))

        @pl.when(i == 0)
        def _():
            _emit_skewed(back(0, 0))

        def alive():
            return (jnp.max(c_ref[...]) >= SB_DEAD_CARRY).astype(jnp.int32)

        @pl.when(i > 0)
        def _():
            _emit_skewed(front(first - 1, None), back(first, 0))

            def step(state):
                pending, _ = state
                _emit_skewed(front(pending - 1, None), back(pending, None))
                return pending - 1, alive()

            pending, live = lax.while_loop(lambda s: (s[0] > 0) & (s[1] > 0), step, (first - 1, alive()))

            @pl.when(live > 0)
            def _():
                _emit_skewed(back(pending, None))

        o_ref[...] = jnp.where(low, oacc_ref[0], oacc_ref[1]).astype(BF16)

    return _call(
        body, (q, kt, v), comm=comm, **_grid_ends(N_HEADS // 2, s_len // tq), name="sb_fwd",
        grid=(N_HEADS // 2, s_len // tq),
        in_specs=[pl.BlockSpec((tq, LANES), lambda p, i: (i, p)),
                  pl.BlockSpec((1, nk, LANES, tk), lambda p, i: (p, 0, 0, 0)),
                  pl.BlockSpec((s_len, LANES), lambda p, i: (0, p))],
        out_specs=[pl.BlockSpec((tq, LANES), lambda p, i: (i, p)), pl.BlockSpec((2, tq, LANES), lambda p, i: (p, i, 0))],
        out_shape=[jax.ShapeDtypeStruct((s_len, N_HEADS * HEAD_DIM), BF16),
                   jax.ShapeDtypeStruct((N_HEADS, s_len, LANES), F32)],
        scratch_shapes=[pltpu.VMEM((2, tq, LANES), F32), pltpu.VMEM((2, tq, LANES), F32),
                        pltpu.VMEM((2, 2, tq, tk), F32), pltpu.VMEM((2, 2, tq, LANES), F32)],
        compiler_params=_cparams("arbitrary", "arbitrary"),
    )


def _sb_bwd(q, qt, kt, k, vt, do, dot, cars):
    s_len = q.shape[0]
    tq, tk, tr = SB_QUERIES, SB_KEYS, SB_ROWS
    nk, ratio = s_len // tk, tq // tk

    def body(q_ref, qt_ref, kt_ref, k_ref, vt_ref, do_ref, dot_ref, car_ref, dq_ref, dk_ref, dv_ref,
             gleft_ref, dqacc_ref, dkacc_ref, dvacc_ref, logw_ref, lsz_ref, da_ref, a_ref, dz_ref):
        i = pl.program_id(1)

        @pl.when(i == 0)
        def _():
            dkacc_ref[...] = jnp.zeros_like(dkacc_ref)
            dvacc_ref[...] = jnp.zeros_like(dvacc_ref)

        qv = q_ref[...]
        dov = do_ref[...]
        low, tri_right2, tri_left2 = _sb_consts(tq, tk)
        lane = lax.broadcasted_iota(jnp.int32, (tr, LANES), 1)
        zero = jnp.zeros_like(qv)
        q_heads = (jnp.where(low, qv, zero), jnp.where(low, zero, qv))
        do_heads = (jnp.where(low, dov, zero), jnp.where(low, zero, dov))
        q_t = qt_ref[0, 0]
        do_t = dot_ref[0, 0]
        gleft_ref[...] = jnp.zeros_like(gleft_ref)
        dqacc_ref[...] = jnp.zeros_like(dqacc_ref)

        def front(j, edge):
            ktv = kt_ref[0, j]
            vtv = vt_ref[0, j]
            slot = j % 2
            st = {}

            def s_logits(w):
                h, rs = w[0], slice(w[1], w[1] + tr)
                st[w, "z"] = _dot(q_heads[h][rs], ktv)
                da_ref[slot, h, rs, :] = _dot(do_heads[h][rs], vtv)

            def s_terms(w):
                h, rs = w[0], slice(w[1], w[1] + tr)
                valid = _sb_valid(w, edge) if w[2] else None
                lsz, lk = _sb_terms(st.pop((w, "z")), valid)
                st[w, "parts"] = _bf16_parts(lk)
                lsz = lsz if valid is None else jnp.where(valid, lsz, NEG_BIG)
                lsz_ref[slot, h, rs, :] = lsz
                st[w, "lszc"] = lsz + jnp.sum(jnp.where(lane == j, car_ref[h, rs, :], 0.0), axis=-1, keepdims=True)

            def s_suffix(w):
                logw_ref[slot, w[0], w[1]:w[1] + tr, :] = st.pop((w, "lszc")) + _dot(st.pop((w, "parts")), tri_right2)

            return _sb_items(edge), [s_logits, s_terms, s_suffix]

        def back(j, edge):
            kv = k_ref[pl.ds(pl.multiple_of(j * tk, tk), tk), :]
            slot = j % 2
            st = {}

            items = _sb_items(edge)
            head_rows = [[r0 for hh, r0, _ in items if hh == h] for h in range(2)]

            def s_weights(w):
                h, rs = w[0], slice(w[1], w[1] + tr)
                a = jnp.exp(logw_ref[slot, h, rs, :])
                g = a * da_ref[slot, h, rs, :]
                a_ref[h, rs, :] = a.astype(BF16)
                st[w, "g"], st[w, "parts"] = g, _bf16_parts(g)

            def s_prefix(w):
                st[w, "gs"] = _dot(st.pop((w, "parts")), tri_left2)

            def s_dz(w):
                h, rs = w[0], slice(w[1], w[1] + tr)
                g = st.pop((w, "g"))
                gleft = gleft_ref[h, rs, :]
                gsum = st.pop((w, "gs")) + jnp.tile(gleft, (1, tk // LANES))
                dz = (g - jnp.exp(lsz_ref[slot, h, rs, :]) * (g + gsum)).astype(BF16)
                st[w, "dz"] = dz
                dz_ref[h, rs, :] = dz
                gleft_ref[h, rs, :] = gleft + _row_sum_lanes(g)

            def s_products(w):
                h, rs = w[0], slice(w[1], w[1] + tr)
                dqacc_ref[h, rs, :] += _dot(st.pop((w, "dz")), kv)
                if w[1] == head_rows[h][-1]:
                    feat = slice(h * HEAD_DIM, (h + 1) * HEAD_DIM)
                    hr = slice(head_rows[h][0], tq)
                    dkacc_ref[j, feat, :] += _dot(q_t[feat, hr], dz_ref[h, hr, :])
                    dvacc_ref[j, feat, :] += _dot(do_t[feat, hr], a_ref[h, hr, :])

            return items, [s_weights, s_prefix, s_dz, s_products]

        first = i * ratio
        tile_max = jnp.max(jnp.maximum(car_ref[0], car_ref[1]), axis=0, keepdims=True)
        start = jnp.clip(first + ratio - jnp.sum(jnp.where(tile_max >= SB_DEAD_CARRY, 1, 0)), 0, first)

        @pl.when(start == first)
        def _():
            _emit_skewed(front(first, 0))

        @pl.when(start < first)
        def _():
            _emit_skewed(front(start, None))

            def step(jj, carry):
                _emit_skewed(front(jj, None), back(jj - 1, None))
                return carry

            lax.fori_loop(start + 1, first, step, 0)
            _emit_skewed(front(first, 0), back(first - 1, None))

        for m in range(1, ratio):
            _emit_skewed(front(first + m, m), back(first + m - 1, m - 1))
        _emit_skewed(back(first + ratio - 1, ratio - 1))
        dq_ref[...] = (Q_SCALE * jnp.where(low, dqacc_ref[0], dqacc_ref[1])).astype(BF16)

        @pl.when(i == s_len // tq - 1)
        def _():
            dk_ref[0] = dkacc_ref[...].astype(BF16)
            dv_ref[0] = dvacc_ref[...].astype(BF16)

    qblk = pl.BlockSpec((tq, LANES), lambda p, i: (i, p))
    qtblk = pl.BlockSpec((1, 1, LANES, tq), lambda p, i: (p, i, 0, 0))
    tblk = pl.BlockSpec((1, nk, LANES, tk), lambda p, i: (p, 0, 0, 0))
    col_full = pl.BlockSpec((s_len, LANES), lambda p, i: (0, p))
    tshape = jax.ShapeDtypeStruct((N_HEADS // 2, nk, LANES, tk), BF16)
    return pl.pallas_call(
        body, name="sb_bwd",
        grid=(N_HEADS // 2, s_len // tq),
        in_specs=[qblk, qtblk, tblk, col_full, tblk, qblk, qtblk, pl.BlockSpec((2, tq, LANES), lambda p, i: (p, i, 0))],
        out_specs=[qblk, tblk, tblk],
        out_shape=[jax.ShapeDtypeStruct((s_len, N_HEADS * HEAD_DIM), BF16), tshape, tshape],
        scratch_shapes=[pltpu.VMEM((2, tq, LANES), F32), pltpu.VMEM((2, tq, LANES), F32),
                        pltpu.VMEM((nk, LANES, tk), F32), pltpu.VMEM((nk, LANES, tk), F32)]
        + [pltpu.VMEM((2, 2, tq, tk), F32)] * 3 + [pltpu.VMEM((2, tq, tk), BF16)] * 2,
        compiler_params=_cparams("parallel", "arbitrary"),
    )(q, qt, kt, k, vt, do, dot, cars)


def _pad_heads(a, heads):
    s_len = a.shape[0]
    a = a.reshape(s_len, heads, HEAD_DIM)
    return jnp.pad(a, ((0, 0), (0, 0), (0, LANES - HEAD_DIM))).reshape(s_len, heads * LANES)


def _unpad_heads(a, heads):
    s_len = a.shape[0]
    return a.reshape(s_len, heads, LANES)[:, :, :HEAD_DIM].reshape(s_len, heads * HEAD_DIM)


def _tile_transposed(a, groups, t):
    s_len = a.shape[0]
    return a.reshape(s_len // t, t, groups, LANES).transpose(2, 0, 3, 1)


def _tile_untransposed(a):
    groups, nt, _, t = a.shape
    return a.transpose(1, 3, 0, 2).reshape(nt * t, groups * LANES)


def _local_step(xs, tgt, gains, sinks, rel_bias, weights_of, ship):
    g1, gmix, g2, gfin = gains
    bkt = _rel_bucket_matrix()
    groups = N_HEADS // 2
    grads = {}

    def carried(outs, comm, count):
        return outs[:count], (outs[count] if comm is not None else None)

    wts = dict(weights_of(0, None))
    comm = ship("weights", 1)
    (x1, h1, a1, b1, u1), landed = carried(
        _ffn_fwd(xs, g1, wts["ffn1_w1t"], wts["ffn1_w3t"], wts["ffn1_w2"], "1", comm), comm, 5)
    wts.update(weights_of(1, landed))
    hm, qa, ka, va, qb, kb, vb, ga, gb = _proj_fwd(x1, gmix, wts["w_int"])
    qa_p, ka_p, va_p = _pad_heads(qa, N_HEADS), _pad_heads(ka, SWA_KV_HEADS), _pad_heads(va, SWA_KV_HEADS)
    oa_p = _swa_fwd(rel_bias, sinks, bkt, qa_p, ka_p, va_p)
    kbt = _tile_transposed(kb, groups, SB_KEYS)
    comm = ship("weights", 2)
    (ob, cars), landed = carried(_sb_fwd(qb, kbt, vb, comm), comm, 2)
    wts.update(weights_of(2, landed))
    oa = _unpad_heads(oa_p, N_HEADS)
    x2, mg = _merge_fwd(x1, oa, ob, ga, gb, wts["w_swa"], wts["w_sb"], wts["w_out"])
    x3, h3, a3, b3, u3 = _ffn_fwd(x2, g2, wts["ffn2_w1t"], wts["ffn2_w3t"], wts["ffn2_w2"], "2")
    dx3, loss, dgfin = _loss_fwd_bwd(x3, tgt, gfin)

    dx2, dg2, da3, db3, dx3b = _ffn_bwd(dx3, x2, g2, a3, b3, wts["ffn2_w1t"], wts["ffn2_w3t"], wts["ffn2_w2"], "2")
    big = {"ffn2_w1t": _matmul_tn(da3, h3, "ffn2_w1"), "ffn2_w3t": _matmul_tn(db3, h3, "ffn2_w3"),
           "ffn2_w2": _matmul_tn(u3, dx3b, "ffn2_w2")}

    doa, dob, dga, dgb, dpa, dpb, dx2b = _merge_bwd(dx2, oa, ob, ga, gb, wts["w_swa"], wts["w_sb"], wts["w_out"])
    comm = ship("grads", 2, big)
    (dqa_p, dka_p, dva_p, dtab, dsink), landed = carried(
        _swa_bwd(rel_bias, sinks, bkt, qa_p, ka_p, va_p, _pad_heads(doa, N_HEADS), comm), comm, 5)
    grads[2] = big if comm is None else landed

    big = {"w_out": _matmul_tn(mg, dx2b, "w_out"), "w_swa": _matmul_tn(oa, dpa, "w_swa"),
           "w_sb": _matmul_tn(ob, dpb, "w_sb")}
    dqb, dkbt, dvbt = _sb_bwd(qb, _tile_transposed(qb, groups, SB_QUERIES), kbt, kb,
                              _tile_transposed(vb, groups, SB_KEYS), dob, _tile_transposed(dob, groups, SB_QUERIES), cars)
    dkb, dvb = _tile_untransposed(dkbt), _tile_untransposed(dvbt)
    dpieces = (_unpad_heads(dqa_p, N_HEADS), _unpad_heads(dka_p, SWA_KV_HEADS).astype(BF16),
               _unpad_heads(dva_p, SWA_KV_HEADS).astype(BF16), dqb, dkb, dvb, dga, dgb)
    big["w_int"] = jnp.concatenate([_matmul_tn(dp, hm, f"w_in{p}") for p, dp in enumerate(dpieces)], axis=0)
    dx1, dgmix = _proj_bwd(dpieces, dx2, x1, gmix, wts["w_int"])

    comm = ship("grads", 1, big)
    (dx0, dg1, da1, db1, dx1b), landed = carried(
        _ffn_bwd(dx1, xs, g1, a1, b1, wts["ffn1_w1t"], wts["ffn1_w3t"], wts["ffn1_w2"], "1", comm), comm, 5)
    grads[1] = big if comm is None else landed
    grads[0] = {"ffn1_w1t": _matmul_tn(da1, h1, "ffn1_w1"), "ffn1_w3t": _matmul_tn(db1, h1, "ffn1_w3"),
                "ffn1_w2": _matmul_tn(u1, dx1b, "ffn1_w2")}

    small = {"gains": (dg1, dgmix, dg2, dgfin), "sinks": dsink[:, 0], "rel_bias": dtab[:, :N_HEADS]}
    return loss, dx0, small, grads


def _my_place():
    return lax.axis_index("x"), lax.axis_index("y"), lax.axis_index("c")


def _flip(v, bit):
    return 1 - v if bit else v


_RELATIONS = tuple((k >> 2 & 1, k >> 1 & 1, k & 1) for k in range(1, N_DEV))


def _gather_weights(wp, tag):
    def body(x_ref, out_ref, send_sems, recv_sems, local_sem):
        x, y, c = _my_place()
        me, sibling = (x, y, c), (x, y, 1 - c)
        chips = [(1 - x, y), (x, 1 - y), (1 - x, 1 - y)]

        def rows(px, py, pc):
            return out_ref.at[4 * px + 2 * py + pc]

        def copy(k, block, to, src=None):
            return pltpu.make_async_remote_copy(
                src_ref=rows(*block) if src is None else src, dst_ref=rows(*block),
                send_sem=send_sems.at[k], recv_sem=recv_sems.at[k],
                device_id=to, device_id_type=pl.DeviceIdType.MESH)

        mine = pltpu.make_async_copy(x_ref, rows(*me), local_sem)
        mine.start()
        first = [copy(0, me, sibling, src=x_ref)]
        first += [copy(1 + j, me, (*chip, c), src=x_ref) for j, chip in enumerate(chips)]
        for cp in first:
            cp.start()
        passed = [copy(4 + j, (*chip, c), sibling) for j, chip in enumerate(chips)]
        for j, chip in enumerate(chips):
            copy(1 + j, (*chip, c), me).wait_recv()
            passed[j].start()
        copy(0, sibling, me).wait_recv()
        for j, chip in enumerate(chips):
            copy(4 + j, (*chip, 1 - c), me).wait_recv()
        for cp in first + passed:
            cp.wait_send()
        mine.wait()

    return pl.pallas_call(
        body, name=f"gather_weights_{tag}",
        out_shape=jax.ShapeDtypeStruct((N_DEV,) + wp.shape, wp.dtype),
        in_specs=[pl.BlockSpec(memory_space=pl.ANY)],
        out_specs=pl.BlockSpec(memory_space=pl.ANY),
        scratch_shapes=[pltpu.SemaphoreType.DMA((7,)), pltpu.SemaphoreType.DMA((7,)), pltpu.SemaphoreType.DMA(())],
    )(wp)


def _exchange_grads(gp, tag):
    def body(g_ref, out_ref, send_sems, recv_sems, local_sem):
        x, y, c = _my_place()
        me = 4 * x + 2 * y + c
        mine = pltpu.make_async_copy(g_ref.at[me], out_ref.at[me], local_sem)
        mine.start()
        copies = []
        for k, (fx, fy, fc) in enumerate(_RELATIONS):
            px, py, pc = _flip(x, fx), _flip(y, fy), _flip(c, fc)
            peer = 4 * px + 2 * py + pc
            copies.append((
                pltpu.make_async_remote_copy(
                    src_ref=g_ref.at[peer], dst_ref=out_ref.at[me], send_sem=send_sems.at[k], recv_sem=recv_sems.at[k],
                    device_id=(px, py, pc), device_id_type=pl.DeviceIdType.MESH),
                pltpu.make_async_remote_copy(
                    src_ref=g_ref.at[peer], dst_ref=out_ref.at[peer], send_sem=send_sems.at[k], recv_sem=recv_sems.at[k],
                    device_id=(px, py, pc), device_id_type=pl.DeviceIdType.MESH)))
        for out_cp, _ in copies:
            out_cp.start()
        for _, in_cp in copies:
            in_cp.wait_recv()
        for out_cp, _ in copies:
            out_cp.wait_send()
        mine.wait()

    return pl.pallas_call(
        body, name=f"exchange_grads_{tag}",
        out_shape=jax.ShapeDtypeStruct(gp.shape, gp.dtype),
        in_specs=[pl.BlockSpec(memory_space=pl.ANY)],
        out_specs=pl.BlockSpec(memory_space=pl.ANY),
        scratch_shapes=[pltpu.SemaphoreType.DMA((7,)), pltpu.SemaphoreType.DMA((7,)), pltpu.SemaphoreType.DMA(())],
    )(gp)


def _peers():
    x, y, c = _my_place()
    out = []
    for k, (fx, fy, fc) in enumerate(_RELATIONS):
        px, py, pc = _flip(x, fx), _flip(y, fy), _flip(c, fc)
        out.append((k, (px, py, pc), 4 * px + 2 * py + pc))
    return out, 4 * x + 2 * y + c


def _grid_ends(*grid):
    def first():
        return functools.reduce(lambda a, b: a & b, [pl.program_id(d) == 0 for d in range(len(grid))])

    def last():
        return functools.reduce(lambda a, b: a & b, [pl.program_id(d) == n - 1 for d, n in enumerate(grid)])

    return {"first": first, "last": last}


def _call(body, operands, *, comm=None, first=None, last=None, **kw):
    if comm is None:
        return pl.pallas_call(body, **kw)(*operands)
    src, per_peer = comm
    in_specs, out_specs, out_shape = list(kw.pop("in_specs")), list(kw.pop("out_specs")), list(kw.pop("out_shape"))
    scratch = list(kw.pop("scratch_shapes", ()))
    n_in, n_out, n_scr = len(in_specs), len(out_specs), len(scratch)
    land_shape = src.shape if per_peer else (N_DEV,) + src.shape

    def wrapped(*refs):
        ins, src_ref = refs[:n_in], refs[n_in]
        outs, land_ref = refs[n_in + 1:n_in + 1 + n_out], refs[n_in + 1 + n_out]
        scr = refs[n_in + 2 + n_out:n_in + 2 + n_out + n_scr]
        send_sems, recv_sems, local_sem = refs[n_in + 2 + n_out + n_scr:]
        peers, me = _peers()
        mine = pltpu.make_async_copy(src_ref.at[me] if per_peer else src_ref, land_ref.at[me], local_sem)
        going, coming = [], []
        for k, where, slab in peers:
            piece = src_ref.at[slab] if per_peer else src_ref
            going.append(pltpu.make_async_remote_copy(
                src_ref=piece, dst_ref=land_ref.at[me], send_sem=send_sems.at[k], recv_sem=recv_sems.at[k],
                device_id=where, device_id_type=pl.DeviceIdType.MESH))
            coming.append(pltpu.make_async_remote_copy(
                src_ref=piece, dst_ref=land_ref.at[slab], send_sem=send_sems.at[k], recv_sem=recv_sems.at[k],
                device_id=where, device_id_type=pl.DeviceIdType.MESH))

        @pl.when(first())
        def _():
            mine.start()
            for cp in going:
                cp.start()

        body(*ins, *outs, *scr)

        @pl.when(last())
        def _():
            for cp in coming:
                cp.wait_recv()
            for cp in going:
                cp.wait_send()
            mine.wait()

    anywhere = pl.BlockSpec(memory_space=pl.ANY)
    return pl.pallas_call(
        wrapped, in_specs=in_specs + [anywhere], out_specs=out_specs + [anywhere],
        out_shape=out_shape + [jax.ShapeDtypeStruct(land_shape, src.dtype)],
        scratch_shapes=scratch + [pltpu.SemaphoreType.DMA((N_DEV - 1,)), pltpu.SemaphoreType.DMA((N_DEV - 1,)),
                                  pltpu.SemaphoreType.DMA(())],
        **kw)(*operands, src)


def _adamw(w, g, m, v):
    m = ADAM_B1 * m + (1.0 - ADAM_B1) * g
    v = ADAM_B2 * v + (1.0 - ADAM_B2) * jnp.square(g)
    m_hat = m / (1.0 - ADAM_B1 ** ADAM_STEP)
    v_hat = v / (1.0 - ADAM_B2 ** ADAM_STEP)
    delta = -ADAM_LR * (m_hat / (jnp.sqrt(v_hat) + ADAM_EPS) + ADAM_WD * w)
    return delta, m, v


def _sum_and_adamw(parts, w, m, v, tr, tag):
    rows = w.shape[0]
    assert rows % tr == 0

    def body(p_ref, w_ref, m_ref, v_ref, g_out, d_out, m_out, v_out):
        g = p_ref[0].astype(F32)
        for d in range(1, N_DEV):
            g = g + p_ref[d].astype(F32)
        delta, mn, vn = _adamw(w_ref[...], g, m_ref[...], v_ref[...])
        g_out[...] = g
        d_out[...] = delta
        m_out[...] = mn
        v_out[...] = vn

    sp = pl.BlockSpec((tr, D_MODEL), lambda i: (i, 0))
    return pl.pallas_call(
        body, name=f"sum_and_adamw_{tag}",
        grid=(rows // tr,),
        in_specs=[pl.BlockSpec((N_DEV, tr, D_MODEL), lambda i: (0, i, 0)), sp, sp, sp],
        out_specs=[sp] * 4,
        out_shape=[jax.ShapeDtypeStruct(w.shape, F32)] * 4,
        compiler_params=_cparams("parallel"),
    )(parts, w, m, v)


def _small_allreduce_adamw(part, w, m, v):
    def body(p_ref, w_ref, m_ref, v_ref, g_out, d_out, m_out, v_out, buf, send_sems, recv_sems):
        x, y, c = _my_place()
        me = 4 * x + 2 * y + c
        buf[me] = p_ref[...]
        copies = []
        for k, (fx, fy, fc) in enumerate(_RELATIONS):
            px, py, pc = _flip(x, fx), _flip(y, fy), _flip(c, fc)
            peer = 4 * px + 2 * py + pc
            copies.append((
                pltpu.make_async_remote_copy(
                    src_ref=buf.at[me], dst_ref=buf.at[me], send_sem=send_sems.at[k], recv_sem=recv_sems.at[k],
                    device_id=(px, py, pc), device_id_type=pl.DeviceIdType.MESH),
                pltpu.make_async_remote_copy(
                    src_ref=buf.at[me], dst_ref=buf.at[peer], send_sem=send_sems.at[k], recv_sem=recv_sems.at[k],
                    device_id=(px, py, pc), device_id_type=pl.DeviceIdType.MESH)))
        for out_cp, _ in copies:
            out_cp.start()
        for _, in_cp in copies:
            in_cp.wait_recv()
        for out_cp, _ in copies:
            out_cp.wait_send()
        g = buf[0]
        for d in range(1, N_DEV):
            g = g + buf[d]
        delta, mn, vn = _adamw(w_ref[...], g, m_ref[...], v_ref[...])
        g_out[...] = g
        d_out[...] = delta
        m_out[...] = mn
        v_out[...] = vn

    vm = pl.BlockSpec(memory_space=pltpu.VMEM)
    return pl.pallas_call(
        body, name="small_allreduce_adamw",
        in_specs=[vm] * 4, out_specs=[vm] * 4,
        out_shape=[jax.ShapeDtypeStruct(w.shape, F32)] * 4,
        scratch_shapes=[pltpu.VMEM((N_DEV,) + part.shape, F32),
                        pltpu.SemaphoreType.DMA((7,)), pltpu.SemaphoreType.DMA((7,))],
    )(part, w, m, v)


_TRANSPOSED = ("ffn1_w1", "ffn1_w3", "w_in", "ffn2_w1", "ffn2_w3")
_BRANCH = ("w_branch_swa", "w_branch_sb")


def _pack_shards(t, names):
    parts = []
    for name in names:
        a = t[name][0]
        if name in _TRANSPOSED:
            a = a.T
        elif name in _BRANCH:
            a = a.reshape(64, D_MODEL)
        parts.append(a)
    return jnp.concatenate(parts, axis=0)


def _unpack_shards(p, names):
    out, lo = {}, 0
    for name in names:
        a = p[lo:lo + BIG_ROWS[BIG_NAMES.index(name)]]
        lo += a.shape[0]
        if name in _TRANSPOSED:
            a = a.T
        elif name in _BRANCH:
            a = a.reshape(512, 128)
        out[name] = a[None]
    return out


def _full_weights(wg, names):
    out, lo = {}, 0
    for name in names:
        rows = BIG_ROWS[BIG_NAMES.index(name)]
        a = wg[:, lo:lo + rows]
        lo += rows
        if name in _BRANCH:
            a = a.reshape(N_DEV, 512, 128).transpose(1, 0, 2).reshape(512, D_MODEL)
        out[_GRAD_KEY[name]] = a.reshape(-1, D_MODEL)
    return out


_GRAD_KEY = {"ffn1_w1": "ffn1_w1t", "ffn1_w3": "ffn1_w3t", "ffn1_w2": "ffn1_w2", "w_in": "w_int",
             "w_branch_swa": "w_swa", "w_branch_sb": "w_sb", "w_out": "w_out",
             "ffn2_w1": "ffn2_w1t", "ffn2_w3": "ffn2_w3t", "ffn2_w2": "ffn2_w2"}


def _pack_full_grads(big, names):
    parts = []
    for name in names:
        a = big[_GRAD_KEY[name]]
        if name in _BRANCH:
            a = a.reshape(512, N_DEV, 128).transpose(1, 0, 2)
        parts.append(a.reshape(N_DEV, BIG_ROWS[BIG_NAMES.index(name)], D_MODEL).astype(BF16))
    return jnp.concatenate(parts, axis=1)


_SMALL_NAMES = ("norm_ffn1", "norm_mix", "norm_ffn2", "norm_final", "swa_sinks", "rel_bias")


def _pack_small(vals):
    rows = []
    for a in vals:
        a = a.reshape(-1)
        rows.append(jnp.pad(a, (0, D_MODEL - a.shape[0])))
    rows += [jnp.zeros((D_MODEL,), F32)] * (SMALL_ROWS - len(rows))
    return jnp.stack(rows)


def _unpack_small(p):
    return {"norm_ffn1": p[0:1], "norm_mix": p[1:2], "norm_ffn2": p[2:3], "norm_final": p[3],
            "swa_sinks": p[4:5, :N_HEADS], "rel_bias": p[5, :REL_BUCKETS * N_HEADS].reshape(REL_BUCKETS, N_HEADS)}


ALL_NAMES = ("norm_ffn1", "ffn1_w1", "ffn1_w3", "ffn1_w2", "norm_mix", "w_in", "swa_sinks", "rel_bias",
             "w_branch_swa", "w_branch_sb", "w_out", "norm_ffn2", "ffn2_w1", "ffn2_w3", "ffn2_w2", "norm_final")


def kernel(x, norm_ffn1, ffn1_w1, ffn1_w3, ffn1_w2, norm_mix, w_in, swa_sinks, rel_bias, w_branch_swa, w_branch_sb, w_out, norm_ffn2, ffn2_w1, ffn2_w3, ffn2_w2, norm_final, loss_target, m_norm_ffn1, m_ffn1_w1, m_ffn1_w3, m_ffn1_w2, m_norm_mix, m_w_in, m_swa_sinks, m_rel_bias, m_w_branch_swa, m_w_branch_sb, m_w_out, m_norm_ffn2, m_ffn2_w1, m_ffn2_w3, m_ffn2_w2, m_norm_final, v_norm_ffn1, v_ffn1_w1, v_ffn1_w3, v_ffn1_w2, v_norm_mix, v_w_in, v_swa_sinks, v_rel_bias, v_w_branch_swa, v_w_branch_sb, v_w_out, v_norm_ffn2, v_ffn2_w1, v_ffn2_w3, v_ffn2_w2, v_norm_final):
    w = dict(zip(ALL_NAMES, (norm_ffn1, ffn1_w1, ffn1_w3, ffn1_w2, norm_mix, w_in, swa_sinks, rel_bias,
                             w_branch_swa, w_branch_sb, w_out, norm_ffn2, ffn2_w1, ffn2_w3, ffn2_w2, norm_final)))
    m = dict(zip(ALL_NAMES, (m_norm_ffn1, m_ffn1_w1, m_ffn1_w3, m_ffn1_w2, m_norm_mix, m_w_in, m_swa_sinks, m_rel_bias,
                             m_w_branch_swa, m_w_branch_sb, m_w_out, m_norm_ffn2, m_ffn2_w1, m_ffn2_w3, m_ffn2_w2,
                             m_norm_final)))
    v = dict(zip(ALL_NAMES, (v_norm_ffn1, v_ffn1_w1, v_ffn1_w3, v_ffn1_w2, v_norm_mix, v_w_in, v_swa_sinks, v_rel_bias,
                             v_w_branch_swa, v_w_branch_sb, v_w_out, v_norm_ffn2, v_ffn2_w1, v_ffn2_w3, v_ffn2_w2,
                             v_norm_final)))

    w_packed = [_pack_shards(w, names) for names in GROUPS]
    gathered0 = _gather_weights(w_packed[0].astype(BF16), "group0")

    def weights_of(group, landed):
        return _full_weights(gathered0 if group == 0 else landed, GROUPS[group])

    def ship(kind, group, grads=None):
        if kind == "weights":
            return w_packed[group].astype(BF16), False
        return _pack_full_grads(grads, GROUPS[group]), True

    gains = (norm_ffn1, norm_mix, norm_ffn2, norm_final.reshape(1, D_MODEL))
    loss, dx, small, parts = _local_step(x[0], loss_target[0], gains, swa_sinks, rel_bias, weights_of, ship)
    parts[0] = _exchange_grads(_pack_full_grads(parts[0], GROUPS[0]), "group0")

    big_outs = [{}, {}, {}, {}]
    for group, names in enumerate(GROUPS):
        res = _sum_and_adamw(parts[group], w_packed[group], _pack_shards(m, names), _pack_shards(v, names),
                             GROUP_TILE[group], f"group{group}")
        for acc, packed in zip(big_outs, res):
            acc.update(_unpack_shards(packed, names))
    g_big, d_big, m_big, v_big = big_outs

    small_part = _pack_small(small["gains"] + (small["sinks"], small["rel_bias"], loss))
    zero = jnp.zeros((1,), F32)
    small_res = _small_allreduce_adamw(
        small_part, _pack_small([w[n] for n in _SMALL_NAMES] + [zero]), _pack_small([m[n] for n in _SMALL_NAMES] + [zero]),
        _pack_small([v[n] for n in _SMALL_NAMES] + [zero]))
    g_sm, d_sm, m_sm, v_sm = (_unpack_small(p) for p in small_res)

    outs = [small_res[0][len(_SMALL_NAMES), 0], dx[None]]
    for big_d, small_d in ((g_big, g_sm), (d_big, d_sm), (m_big, m_sm), (v_big, v_sm)):
        merged = {**big_d, **small_d}
        outs += [merged[n] for n in ALL_NAMES]
    return tuple(outs)
```

```python
import functools

import jax
import jax.numpy as jnp
import numpy as np
from jax import lax
from jax.experimental import pallas as pl
from jax.experimental.pallas import tpu as pltpu

F32 = jnp.float32
BF16 = jnp.bfloat16

D_MODEL = 1024
D_FF = 2816
HEAD_DIM = 64
N_HEADS = 8
SWA_KV_HEADS = 2
SWA_GROUP = 4
SWA_BLOCK = 128
REL_BUCKETS = 32
REL_MAX_DIST = 128
RMS_EPS = 1e-6
NEG_BIG = -1e30
Q_SCALE = HEAD_DIM ** -0.5
LANES = 128

N_DEV = 8

ADAM_LR = 0.001
ADAM_B1 = 0.9
ADAM_B2 = 0.999
ADAM_EPS = 1e-08
ADAM_WD = 0.01
ADAM_STEP = 10

IN_SIZES = (512, 128, 128, 512, 512, 512, 1024, 1024)
IN_OFFS = tuple(int(v) for v in np.cumsum((0,) + IN_SIZES))
IN_W = IN_OFFS[-1]

BIG_NAMES = ("ffn1_w1", "ffn1_w3", "ffn1_w2", "w_in", "w_branch_swa", "w_branch_sb", "w_out",
             "ffn2_w1", "ffn2_w3", "ffn2_w2")
BIG_ROWS = (352, 352, 352, 544, 64, 64, 128, 352, 352, 352)
SMALL_ROWS = 8
GROUPS = (BIG_NAMES[0:3], BIG_NAMES[3:7], BIG_NAMES[7:10])
SUM_GROUPS = tuple((n,) for n in GROUPS[0]) + GROUPS[1:]
SUM_TILE = (176, 176, 176, 160, 96)

VMEM_LIMIT = 56 * 1024 * 1024
FFN_PIECES = 2
SB_QUERIES = 512
SB_KEYS = 256
SB_ROWS = 256
SB_SUM_PARTS = 1
SB_LOGIT_CAP = 80.0
SB_DEAD_CARRY = -110.0


def _dot(a, b):
    return jnp.dot(a, b, preferred_element_type=F32)


def _dot_nt(a, b):
    return lax.dot_general(a, b, (((1,), (1,)), ((), ())), preferred_element_type=F32)


def _dot_tn(a, b):
    return lax.dot_general(a, b, (((0,), (0,)), ((), ())), preferred_element_type=F32)


def _cparams(*sem):
    return pltpu.CompilerParams(dimension_semantics=sem, vmem_limit_bytes=VMEM_LIMIT)


def _rms_rstd(xv):
    return lax.rsqrt(jnp.mean(xv * xv, axis=-1, keepdims=True) + RMS_EPS)


def _rms_bwd(dh, xv, r, g):
    xhat = xv * r
    dg = jnp.sum(dh * xhat, axis=0, keepdims=True)
    dxn = dh * g
    dx = r * (dxn - xhat * jnp.mean(dxn * xhat, axis=-1, keepdims=True))
    return dx, dg


def _ffn_fwd(x, g, w1t, w3t, w2, tag, comm=None):
    s_len = x.shape[0]
    tm, tf = min(1024, s_len), 256
    nf = D_FF // tf

    def body(x_ref, g_ref, w1_ref, w3_ref, w2_ref, xo_ref, h_ref, a_ref, b_ref, u_ref, acc_ref, hs_ref):
        j = pl.program_id(1)

        @pl.when(j == 0)
        def _():
            xv = x_ref[...]
            h = (xv * _rms_rstd(xv) * g_ref[...]).astype(BF16)
            hs_ref[...] = h
            h_ref[...] = h
            acc_ref[...] = jnp.zeros_like(acc_ref)

        st = {}

        def s_up(rs):
            h = hs_ref[rs, :]
            st[rs.start, "ab"] = (_dot_nt(h, w1_ref[...]), _dot_nt(h, w3_ref[...]))

        def s_act(rs):
            a, b = st.pop((rs.start, "ab"))
            a_ref[rs, :] = a.astype(BF16)
            b_ref[rs, :] = b.astype(BF16)
            uh = (0.5 * (a * jax.nn.sigmoid(a) * b)).astype(BF16)
            u_ref[rs, :] = uh
            st[rs.start, "u"] = uh

        def s_down(rs):
            acc_ref[rs, :] += _dot(st.pop((rs.start, "u")), w2_ref[...])

        _emit_skewed(([slice(r, r + tm // FFN_PIECES) for r in range(0, tm, tm // FFN_PIECES)], [s_up, s_act, s_down]))

        @pl.when(j == nf - 1)
        def _():
            xo_ref[...] = x_ref[...] + acc_ref[...]

    row = lambda i, j: (i, 0)
    return _call(
        body, (x, g, w1t, w3t, w2), comm=comm, **_grid_ends(s_len // tm, nf), name=f"ffn_fwd_{tag}",
        grid=(s_len // tm, nf),
        in_specs=[pl.BlockSpec((tm, D_MODEL), row), pl.BlockSpec((1, D_MODEL), lambda i, j: (0, 0)),
                  pl.BlockSpec((tf, D_MODEL), lambda i, j: (j, 0)), pl.BlockSpec((tf, D_MODEL), lambda i, j: (j, 0)),
                  pl.BlockSpec((tf, D_MODEL), lambda i, j: (j, 0))],
        out_specs=[pl.BlockSpec((tm, D_MODEL), row), pl.BlockSpec((tm, D_MODEL), row),
                   pl.BlockSpec((tm, tf), lambda i, j: (i, j)), pl.BlockSpec((tm, tf), lambda i, j: (i, j)),
                   pl.BlockSpec((tm, tf), lambda i, j: (i, j))],
        out_shape=[jax.ShapeDtypeStruct((s_len, D_MODEL), F32), jax.ShapeDtypeStruct((s_len, D_MODEL), BF16),
                   jax.ShapeDtypeStruct((s_len, D_FF), BF16), jax.ShapeDtypeStruct((s_len, D_FF), BF16),
                   jax.ShapeDtypeStruct((s_len, D_FF), BF16)],
        scratch_shapes=[pltpu.VMEM((tm, D_MODEL), F32), pltpu.VMEM((tm, D_MODEL), BF16)],
        compiler_params=_cparams("arbitrary", "arbitrary"),
    )


def _ffn_bwd(dy, x, g, a, b, w1t, w3t, w2, tag, comm=None):
    s_len = x.shape[0]
    tm, tf = min(1024, s_len), 256
    nf = D_FF // tf

    def body(dy_ref, x_ref, g_ref, a_ref, b_ref, w1_ref, w3_ref, w2_ref,
             dx_ref, dg_ref, da_ref, db_ref, dyb_ref, acc_ref, dys_ref):
        i, j = pl.program_id(0), pl.program_id(1)

        @pl.when(j == 0)
        def _():
            dyb = dy_ref[...].astype(BF16)
            dys_ref[...] = dyb
            dyb_ref[...] = dyb
            acc_ref[...] = jnp.zeros_like(acc_ref)

        @pl.when((i == 0) & (j == 0))
        def _():
            dg_ref[...] = jnp.zeros_like(dg_ref)

        st = {}

        def s_du(rs):
            st[rs.start, "du"] = 0.5 * _dot_nt(dys_ref[rs, :], w2_ref[...])

        def s_act(rs):
            du = st.pop((rs.start, "du"))
            av = a_ref[rs, :].astype(F32)
            bv = b_ref[rs, :].astype(F32)
            sg = jax.nn.sigmoid(av)
            sil = av * sg
            da = (du * bv * (sg + sil * (1.0 - sg))).astype(BF16)
            db = (du * sil).astype(BF16)
            da_ref[rs, :] = da
            db_ref[rs, :] = db
            st[rs.start, "dab"] = (da, db)

        def s_dh(rs):
            da, db = st.pop((rs.start, "dab"))
            acc_ref[rs, :] += _dot(da, w1_ref[...]) + _dot(db, w3_ref[...])

        _emit_skewed(([slice(r, r + tm // FFN_PIECES) for r in range(0, tm, tm // FFN_PIECES)], [s_du, s_act, s_dh]))

        @pl.when(j == nf - 1)
        def _():
            xv = x_ref[...]
            dx, dg = _rms_bwd(acc_ref[...], xv, _rms_rstd(xv), g_ref[...])
            dx_ref[...] = dy_ref[...] + dx
            dg_ref[...] += dg

    row = lambda i, j: (i, 0)
    blk = lambda i, j: (i, j)
    wsp = pl.BlockSpec((tf, D_MODEL), lambda i, j: (j, 0))
    return _call(
        body, (dy, x, g, a, b, w1t, w3t, w2), comm=comm, **_grid_ends(s_len // tm, nf), name=f"ffn_bwd_{tag}",
        grid=(s_len // tm, nf),
        in_specs=[pl.BlockSpec((tm, D_MODEL), row), pl.BlockSpec((tm, D_MODEL), row),
                  pl.BlockSpec((1, D_MODEL), lambda i, j: (0, 0)),
                  pl.BlockSpec((tm, tf), blk), pl.BlockSpec((tm, tf), blk), wsp, wsp, wsp],
        out_specs=[pl.BlockSpec((tm, D_MODEL), row), pl.BlockSpec((1, D_MODEL), lambda i, j: (0, 0)),
                   pl.BlockSpec((tm, tf), blk), pl.BlockSpec((tm, tf), blk), pl.BlockSpec((tm, D_MODEL), row)],
        out_shape=[jax.ShapeDtypeStruct((s_len, D_MODEL), F32), jax.ShapeDtypeStruct((1, D_MODEL), F32),
                   jax.ShapeDtypeStruct((s_len, D_FF), BF16), jax.ShapeDtypeStruct((s_len, D_FF), BF16),
                   jax.ShapeDtypeStruct((s_len, D_MODEL), BF16)],
        scratch_shapes=[pltpu.VMEM((tm, D_MODEL), F32), pltpu.VMEM((tm, D_MODEL), BF16)],
        compiler_params=_cparams("arbitrary", "arbitrary"),
    )


def _matmul_tn(lhs, rhs, tag, comm=None):
    s_len, m = lhs.shape
    n = rhs.shape[1]
    tm = min(512, s_len)
    tj = m if m <= 1024 else 1408
    assert m % tj == 0

    def body(l_ref, r_ref, o_ref):
        @pl.when(pl.program_id(1) == 0)
        def _():
            o_ref[...] = jnp.zeros_like(o_ref)

        o_ref[...] += _dot_tn(l_ref[...], r_ref[...])

    res = _call(
        body, (lhs, rhs), comm=comm, **_grid_ends(m // tj, s_len // tm), name=f"matmul_tn_{tag}",
        grid=(m // tj, s_len // tm),
        in_specs=[pl.BlockSpec((tm, tj), lambda j, i: (i, j)), pl.BlockSpec((tm, n), lambda j, i: (i, 0))],
        out_specs=[pl.BlockSpec((tj, n), lambda j, i: (j, 0))],
        out_shape=[jax.ShapeDtypeStruct((m, n), F32)],
        compiler_params=_cparams("arbitrary", "arbitrary"),
    )
    return res[0] if comm is None else tuple(res)


def _proj_fwd(x1, g, wint):
    s_len = x1.shape[0]
    tm = min(512, s_len)
    dts = (BF16, BF16, BF16, BF16, BF16, BF16, F32, F32)

    def body(x_ref, g_ref, w_ref, h_ref, *outs):
        xv = x_ref[...]
        h = (xv * _rms_rstd(xv) * g_ref[...]).astype(BF16)
        h_ref[...] = h
        for p, o_ref in enumerate(outs):
            val = _dot_nt(h, w_ref[IN_OFFS[p]:IN_OFFS[p + 1], :])
            if p == 3:
                val = val * Q_SCALE
            o_ref[...] = val.astype(dts[p])

    row = lambda i: (i, 0)
    return pl.pallas_call(
        body, name="proj_fwd",
        grid=(s_len // tm,),
        in_specs=[pl.BlockSpec((tm, D_MODEL), row), pl.BlockSpec((1, D_MODEL), lambda i: (0, 0)),
                  pl.BlockSpec((IN_W, D_MODEL), lambda i: (0, 0))],
        out_specs=[pl.BlockSpec((tm, D_MODEL), row)] + [pl.BlockSpec((tm, w), row) for w in IN_SIZES],
        out_shape=[jax.ShapeDtypeStruct((s_len, D_MODEL), BF16)]
        + [jax.ShapeDtypeStruct((s_len, w), dt) for w, dt in zip(IN_SIZES, dts)],
        compiler_params=_cparams("parallel"),
    )(x1, g, wint)


def _proj_bwd(dpieces, dx2, x1, g, wint):
    s_len = x1.shape[0]
    tm = min(512, s_len)

    def body(*refs):
        dps = refs[:8]
        dx2_ref, x_ref, g_ref, w_ref, dx_ref, dg_ref = refs[8:]

        @pl.when(pl.program_id(0) == 0)
        def _():
            dg_ref[...] = jnp.zeros_like(dg_ref)

        dh = _dot(dps[0][...], w_ref[IN_OFFS[0]:IN_OFFS[1], :])
        for p in range(1, 8):
            dh += _dot(dps[p][...], w_ref[IN_OFFS[p]:IN_OFFS[p + 1], :])
        xv = x_ref[...]
        dx, dg = _rms_bwd(dh, xv, _rms_rstd(xv), g_ref[...])
        dx_ref[...] = dx2_ref[...] + dx
        dg_ref[...] += dg

    row = lambda i: (i, 0)
    return pl.pallas_call(
        body, name="proj_bwd",
        grid=(s_len // tm,),
        in_specs=[pl.BlockSpec((tm, w), row) for w in IN_SIZES]
        + [pl.BlockSpec((tm, D_MODEL), row), pl.BlockSpec((tm, D_MODEL), row),
           pl.BlockSpec((1, D_MODEL), lambda i: (0, 0)), pl.BlockSpec((IN_W, D_MODEL), lambda i: (0, 0))],
        out_specs=[pl.BlockSpec((tm, D_MODEL), row), pl.BlockSpec((1, D_MODEL), lambda i: (0, 0))],
        out_shape=[jax.ShapeDtypeStruct((s_len, D_MODEL), F32), jax.ShapeDtypeStruct((1, D_MODEL), F32)],
        compiler_params=_cparams("arbitrary"),
    )(*dpieces, dx2, x1, g, wint)


def _merge_fwd(x1, oa, ob, ga, gb, wswa, wsb, wout):
    s_len = x1.shape[0]
    tm = min(512, s_len)

    def body(x_ref, oa_ref, ob_ref, ga_ref, gb_ref, wa_ref, wb_ref, wo_ref, xo_ref, mg_ref):
        pa = _dot(oa_ref[...], wa_ref[...])
        pb = _dot(ob_ref[...], wb_ref[...])
        mg = (jax.nn.sigmoid(ga_ref[...]) * pa + jax.nn.sigmoid(gb_ref[...]) * pb).astype(BF16)
        mg_ref[...] = mg
        xo_ref[...] = x_ref[...] + _dot(mg, wo_ref[...])

    row = lambda i: (i, 0)
    full = lambda i: (0, 0)
    return pl.pallas_call(
        body, name="merge_fwd",
        grid=(s_len // tm,),
        in_specs=[pl.BlockSpec((tm, D_MODEL), row), pl.BlockSpec((tm, 512), row), pl.BlockSpec((tm, 512), row),
                  pl.BlockSpec((tm, D_MODEL), row), pl.BlockSpec((tm, D_MODEL), row),
                  pl.BlockSpec((512, D_MODEL), full), pl.BlockSpec((512, D_MODEL), full),
                  pl.BlockSpec((D_MODEL, D_MODEL), full)],
        out_specs=[pl.BlockSpec((tm, D_MODEL), row), pl.BlockSpec((tm, D_MODEL), row)],
        out_shape=[jax.ShapeDtypeStruct((s_len, D_MODEL), F32), jax.ShapeDtypeStruct((s_len, D_MODEL), BF16)],
        compiler_params=_cparams("parallel"),
    )(x1, oa, ob, ga, gb, wswa, wsb, wout)


def _merge_bwd(dx2, oa, ob, ga, gb, wswa, wsb, wout):
    s_len = dx2.shape[0]
    tm = min(512, s_len)

    def body(dx_ref, oa_ref, ob_ref, ga_ref, gb_ref, wa_ref, wb_ref, wo_ref,
             doa_ref, dob_ref, dga_ref, dgb_ref, dpa_ref, dpb_ref, dxb_ref):
        dxb = dx_ref[...].astype(BF16)
        dxb_ref[...] = dxb
        dmg = _dot_nt(dxb, wo_ref[...])
        for o_ref, g_ref, w_ref, do_ref, dg_ref, dp_ref in (
                (oa_ref, ga_ref, wa_ref, doa_ref, dga_ref, dpa_ref),
                (ob_ref, gb_ref, wb_ref, dob_ref, dgb_ref, dpb_ref)):
            pv = _dot(o_ref[...], w_ref[...])
            sg = jax.nn.sigmoid(g_ref[...])
            dp = (dmg * sg).astype(BF16)
            dp_ref[...] = dp
            dg_ref[...] = (dmg * pv * sg * (1.0 - sg)).astype(BF16)
            do_ref[...] = _dot_nt(dp, w_ref[...]).astype(BF16)

    row = lambda i: (i, 0)
    full = lambda i: (0, 0)
    wide = pl.BlockSpec((tm, D_MODEL), row)
    half = pl.BlockSpec((tm, 512), row)
    return pl.pallas_call(
        body, name="merge_bwd",
        grid=(s_len // tm,),
        in_specs=[wide, half, half, wide, wide, pl.BlockSpec((512, D_MODEL), full),
                  pl.BlockSpec((512, D_MODEL), full), pl.BlockSpec((D_MODEL, D_MODEL), full)],
        out_specs=[half, half, wide, wide, wide, wide, wide],
        out_shape=[jax.ShapeDtypeStruct((s_len, 512), BF16)] * 2 + [jax.ShapeDtypeStruct((s_len, D_MODEL), BF16)] * 5,
        compiler_params=_cparams("parallel"),
    )(dx2, oa, ob, ga, gb, wswa, wsb, wout)


def _loss_fwd_bwd(x3, tgt, g):
    s_len = x3.shape[0]
    tm = min(1024, s_len)

    def body(x_ref, t_ref, g_ref, dx_ref, loss_ref, dg_ref):
        @pl.when(pl.program_id(0) == 0)
        def _():
            loss_ref[...] = jnp.zeros_like(loss_ref)
            dg_ref[...] = jnp.zeros_like(dg_ref)

        xv = x_ref[...]
        gv = g_ref[...]
        r = _rms_rstd(xv)
        err = xv * r * gv - t_ref[...]
        loss_ref[...] += 0.5 * jnp.sum(jnp.mean(err * err, axis=-1, keepdims=True), axis=0, keepdims=True)
        dx, dg = _rms_bwd(err * (1.0 / D_MODEL), xv, r, gv)
        dx_ref[...] = dx
        dg_ref[...] += dg

    row = lambda i: (i, 0)
    return pl.pallas_call(
        body, name="loss_fwd_bwd",
        grid=(s_len // tm,),
        in_specs=[pl.BlockSpec((tm, D_MODEL), row), pl.BlockSpec((tm, D_MODEL), row),
                  pl.BlockSpec((1, D_MODEL), lambda i: (0, 0))],
        out_specs=[pl.BlockSpec((tm, D_MODEL), row), pl.BlockSpec((1, 1), lambda i: (0, 0)),
                   pl.BlockSpec((1, D_MODEL), lambda i: (0, 0))],
        out_shape=[jax.ShapeDtypeStruct((s_len, D_MODEL), F32), jax.ShapeDtypeStruct((1, 1), F32),
                   jax.ShapeDtypeStruct((1, D_MODEL), F32)],
        compiler_params=_cparams("arbitrary"),
    )(x3, tgt, g)


def _rel_bucket_matrix():
    qi = jnp.arange(SWA_BLOCK)[:, None] + SWA_BLOCK
    kj = jnp.arange(2 * SWA_BLOCK)[None, :]
    dist = jnp.maximum(qi - kj, 0)
    max_exact = REL_BUCKETS // 2
    d = jnp.maximum(dist, 1).astype(F32)
    large = max_exact + (jnp.log(d / max_exact) / np.log(REL_MAX_DIST / max_exact)
                         * (REL_BUCKETS - max_exact)).astype(jnp.int32)
    large = jnp.minimum(large, REL_BUCKETS - 1)
    return jnp.where(dist < max_exact, dist, large).astype(jnp.int32)


def _swa_bias_into(bias_ref, bkt_ref, tab_ref):
    bk = bkt_ref[...]
    for h in range(N_HEADS):
        acc = jnp.zeros(bk.shape, F32)
        for bucket in range(REL_BUCKETS):
            acc = jnp.where(bk == bucket, tab_ref[bucket, h], acc)
        bias_ref[h] = acc


def _swa_valid(n):
    shape = (SWA_BLOCK, 2 * SWA_BLOCK)
    row = lax.broadcasted_iota(jnp.int32, shape, 0)
    col = lax.broadcasted_iota(jnp.int32, shape, 1)
    dist = row + SWA_BLOCK - col
    return (dist >= 0) & (dist < SWA_BLOCK) & ((col >= SWA_BLOCK) | (n > 0))


def _swa_windows(kp_ref, kc_ref, vp_ref, vc_ref):
    lanes = [slice(g * LANES, (g + 1) * LANES) for g in range(SWA_KV_HEADS)]
    return ([jnp.concatenate([kp_ref[:, gl], kc_ref[:, gl]], axis=0) for gl in lanes],
            [jnp.concatenate([vp_ref[:, gl], vc_ref[:, gl]], axis=0) for gl in lanes])


def _swa_probs(qk, bias, sink, valid):
    lg = jnp.where(valid, qk * Q_SCALE + bias, NEG_BIG)
    m = jnp.maximum(jnp.max(lg, axis=-1, keepdims=True), sink)
    e = jnp.exp(lg - m)
    es = jnp.exp(sink - m)
    inv = 1.0 / (jnp.sum(e, axis=-1, keepdims=True) + es)
    return e * inv, es * inv


def _swa_specs(s_len):
    blk = SWA_BLOCK
    cur = lambda n: (n, 0)
    prev = lambda n: (jnp.maximum(n - 1, 0), 0)
    kvw = SWA_KV_HEADS * LANES
    return [pl.BlockSpec(memory_space=pltpu.SMEM), pl.BlockSpec(memory_space=pltpu.SMEM),
            pl.BlockSpec((blk, 2 * blk), lambda n: (0, 0)),
            pl.BlockSpec((blk, N_HEADS * LANES), cur),
            pl.BlockSpec((blk, kvw), prev), pl.BlockSpec((blk, kvw), cur),
            pl.BlockSpec((blk, kvw), prev), pl.BlockSpec((blk, kvw), cur)]


def _swa_fwd(tab, sinks, bkt, q, k, v):
    s_len = q.shape[0]
    blk = SWA_BLOCK

    def body(tab_ref, sink_ref, bkt_ref, q_ref, kp_ref, kc_ref, vp_ref, vc_ref, o_ref, bias_ref):
        n = pl.program_id(0)

        @pl.when(n == 0)
        def _():
            _swa_bias_into(bias_ref, bkt_ref, tab_ref)

        valid = _swa_valid(n)
        kk, vv = _swa_windows(kp_ref, kc_ref, vp_ref, vc_ref)
        st = {}

        def s_logits(h):
            st[h, "lg"] = _dot_nt(q_ref[:, h * LANES:(h + 1) * LANES], kk[h // SWA_GROUP])

        def s_probs(h):
            st[h, "p"] = _swa_probs(st.pop((h, "lg")), bias_ref[h], sink_ref[0, h], valid)[0].astype(BF16)

        def s_values(h):
            o_ref[:, h * LANES:(h + 1) * LANES] = _dot(st.pop((h, "p")), vv[h // SWA_GROUP]).astype(BF16)

        _emit_skewed((list(range(N_HEADS)), [s_logits, s_probs, s_values]))

    return pl.pallas_call(
        body, name="swa_fwd",
        grid=(s_len // blk,),
        in_specs=_swa_specs(s_len),
        out_specs=pl.BlockSpec((blk, N_HEADS * LANES), lambda n: (n, 0)),
        out_shape=jax.ShapeDtypeStruct((s_len, N_HEADS * LANES), BF16),
        scratch_shapes=[pltpu.VMEM((N_HEADS, blk, 2 * blk), F32)],
        compiler_params=_cparams("arbitrary"),
    )(tab, sinks, bkt, q, k, k, v, v)


def _swa_bwd(tab, sinks, bkt, q, k, v, do, comm=None):
    s_len = q.shape[0]
    blk = SWA_BLOCK
    nb = s_len // blk
    kvw = SWA_KV_HEADS * LANES

    def body(tab_ref, sink_ref, bkt_ref, q_ref, kp_ref, kc_ref, vp_ref, vc_ref, do_ref,
             dq_ref, dk_ref, dv_ref, dtab_ref, dsink_ref, bias_ref, dbias_ref):
        n = pl.program_id(0)

        @pl.when(n == 0)
        def _():
            _swa_bias_into(bias_ref, bkt_ref, tab_ref)
            dbias_ref[...] = jnp.zeros_like(dbias_ref)
            dk_ref[...] = jnp.zeros_like(dk_ref)
            dv_ref[...] = jnp.zeros_like(dv_ref)
            dsink_ref[...] = jnp.zeros_like(dsink_ref)
            dtab_ref[...] = jnp.zeros_like(dtab_ref)

        valid = _swa_valid(n)
        cur_rows = pl.ds(pl.multiple_of(n * blk, blk), blk)
        prev_rows = pl.ds(pl.multiple_of(jnp.maximum(n - 1, 0) * blk, blk), blk)
        kk, vv = _swa_windows(kp_ref, kc_ref, vp_ref, vc_ref)
        st = {}

        def s_logits(h):
            hl = slice(h * LANES, (h + 1) * LANES)
            st[h, "lg"] = _dot_nt(q_ref[:, hl], kk[h // SWA_GROUP])
            st[h, "dp"] = _dot_nt(do_ref[:, hl], vv[h // SWA_GROUP])

        def s_probs(h):
            p, ps = _swa_probs(st.pop((h, "lg")), bias_ref[h], sink_ref[0, h], valid)
            dp = st.pop((h, "dp"))
            delta = jnp.sum(p * dp, axis=-1, keepdims=True)
            dl = p * (dp - delta)
            dsink_ref[h:h + 1, :] += jnp.broadcast_to(-jnp.sum(ps * delta, axis=0, keepdims=True), (1, LANES))
            dbias_ref[h] += dl
            st[h, "dl"], st[h, "p"] = dl.astype(BF16), p.astype(BF16)

        def s_products(h):
            hl = slice(h * LANES, (h + 1) * LANES)
            gl = slice(h // SWA_GROUP * LANES, (h // SWA_GROUP + 1) * LANES)
            dlb = st.pop((h, "dl"))
            dq_ref[:, hl] = (Q_SCALE * _dot(dlb, kk[h // SWA_GROUP])).astype(BF16)
            dk_win = Q_SCALE * _dot_tn(dlb, q_ref[:, hl])
            dv_win = _dot_tn(st.pop((h, "p")), do_ref[:, hl])
            dk_ref[prev_rows, gl] += dk_win[:blk]
            dv_ref[prev_rows, gl] += dv_win[:blk]
            dk_ref[cur_rows, gl] += dk_win[blk:]
            dv_ref[cur_rows, gl] += dv_win[blk:]

        _emit_skewed((list(range(N_HEADS)), [s_logits, s_probs, s_products]))

        @pl.when(n == nb - 1)
        def _():
            bk = bkt_ref[...]
            lane = lax.broadcasted_iota(jnp.int32, (1, LANES), 1)
            for bucket in range(REL_BUCKETS):
                rowv = jnp.zeros((1, LANES), F32)
                for h in range(N_HEADS):
                    val = jnp.sum(jnp.where(bk == bucket, dbias_ref[h], 0.0), axis=1, keepdims=True)
                    val = jnp.sum(val, axis=0, keepdims=True)
                    rowv = jnp.where(lane == h, val, rowv)
                dtab_ref[bucket:bucket + 1, :] = rowv

    return _call(
        body, (tab, sinks, bkt, q, k, k, v, v, do), comm=comm, **_grid_ends(nb), name="swa_bwd",
        grid=(nb,),
        in_specs=_swa_specs(s_len) + [pl.BlockSpec((blk, N_HEADS * LANES), lambda n: (n, 0))],
        out_specs=[pl.BlockSpec((blk, N_HEADS * LANES), lambda n: (n, 0)),
                   pl.BlockSpec((s_len, kvw), lambda n: (0, 0)), pl.BlockSpec((s_len, kvw), lambda n: (0, 0)),
                   pl.BlockSpec((REL_BUCKETS, LANES), lambda n: (0, 0)), pl.BlockSpec((N_HEADS, LANES), lambda n: (0, 0))],
        out_shape=[jax.ShapeDtypeStruct((s_len, N_HEADS * LANES), BF16),
                   jax.ShapeDtypeStruct((s_len, kvw), F32), jax.ShapeDtypeStruct((s_len, kvw), F32),
                   jax.ShapeDtypeStruct((REL_BUCKETS, LANES), F32), jax.ShapeDtypeStruct((N_HEADS, LANES), F32)],
        scratch_shapes=[pltpu.VMEM((N_HEADS, blk, 2 * blk), F32), pltpu.VMEM((N_HEADS, blk, 2 * blk), F32)],
        compiler_params=_cparams("arbitrary"),
    )


def _sb_terms(z, valid):
    zc = jnp.minimum(z, SB_LOGIT_CAP)
    lk = -jnp.log(1.0 + jnp.exp(zc))
    lsz = zc + lk
    return lsz, (lk if valid is None else jnp.where(valid, lk, 0.0))


def _bf16_parts(vals):
    parts, rest = [], vals
    for n in range(SB_SUM_PARTS):
        parts.append(rest.astype(BF16))
        if n + 1 < SB_SUM_PARTS:
            rest = rest - parts[-1].astype(F32)
    return parts[0] if len(parts) == 1 else jnp.concatenate(parts, axis=1)


def _row_sum_lanes(vals):
    return jnp.broadcast_to(jnp.sum(vals, axis=-1, keepdims=True), (vals.shape[0], LANES))


def _emit_skewed(*groups):
    for step in range(max(len(items) + len(stages) - 1 for items, stages in groups)):
        for items, stages in groups:
            for s, stage in enumerate(stages):
                if 0 <= step - s < len(items):
                    stage(items[step - s])


def _sb_items(edge):
    items = []
    for h in range(2):
        for r0 in range(0, SB_QUERIES, SB_ROWS):
            if edge is None or r0 >= (edge + 1) * SB_KEYS:
                items.append((h, r0, False))
            elif r0 + SB_ROWS - 1 > edge * SB_KEYS:
                items.append((h, r0, True))
    return items


def _sb_valid(w, edge):
    row = lax.broadcasted_iota(jnp.int32, (SB_ROWS, SB_KEYS), 0) + w[1]
    col = lax.broadcasted_iota(jnp.int32, (SB_ROWS, SB_KEYS), 1) + edge * SB_KEYS
    return col < row


def _sb_consts(tq, tk):
    low = lax.broadcasted_iota(jnp.int32, (tq, LANES), 1) < HEAD_DIM
    row = lax.broadcasted_iota(jnp.int32, (tk, tk), 0)
    col = lax.broadcasted_iota(jnp.int32, (tk, tk), 1)
    right = (row > col).astype(BF16)
    left = (row < col).astype(BF16)
    return low, jnp.concatenate([right] * SB_SUM_PARTS, axis=0), jnp.concatenate([left] * SB_SUM_PARTS, axis=0)


def _sb_fwd(q, kt, v, comm=None):
    s_len = q.shape[0]
    tq, tk, tr = SB_QUERIES, SB_KEYS, SB_ROWS
    nk, ratio = s_len // tk, tq // tk
    assert nk <= LANES

    def body(q_ref, kt_ref, v_ref, o_ref, car_ref, c_ref, oacc_ref, logw_ref, lksum_ref):
        i = pl.program_id(1)
        qv = q_ref[...]
        low, tri2, _ = _sb_consts(tq, tk)
        lane = lax.broadcasted_iota(jnp.int32, (tr, LANES), 1)
        zero = jnp.zeros_like(qv)
        q_heads = (jnp.where(low, qv, zero), jnp.where(low, zero, qv))
        c_ref[...] = jnp.zeros_like(c_ref)
        oacc_ref[...] = jnp.zeros_like(oacc_ref)
        car_ref[...] = jnp.full_like(car_ref, NEG_BIG)

        def front(j, edge):
            ktv = kt_ref[0, j]
            slot = j % 2
            st = {}

            def s_logits(w):
                st[w, "z"] = _dot(q_heads[w[0]][w[1]:w[1] + tr], ktv)

            def s_terms(w):
                valid = _sb_valid(w, edge) if w[2] else None
                lsz, lk = _sb_terms(st.pop((w, "z")), valid)
                st[w, "parts"] = _bf16_parts(lk)
                st[w, "lsz"] = lsz if valid is None else jnp.where(valid, lsz, NEG_BIG)
                lksum_ref[slot, w[0], w[1]:w[1] + tr, :] = _row_sum_lanes(lk)

            def s_suffix(w):
                logw_ref[slot, w[0], w[1]:w[1] + tr, :] = st.pop((w, "lsz")) + _dot(st.pop((w, "parts")), tri2)

            return _sb_items(edge), [s_logits, s_terms, s_suffix]

        def back(j, edge):
            vv = v_ref[pl.ds(pl.multiple_of(j * tk, tk), tk), :]
            slot = j % 2
            st = {}

            def s_weights(w):
                h, rs = w[0], slice(w[1], w[1] + tr)
                c = c_ref[h, rs, :]
                st[w, "a"] = jnp.exp(logw_ref[slot, h, rs, :] + jnp.tile(c, (1, tk // LANES))).astype(BF16)
                car_ref[h, rs, :] = jnp.where(lane == j, c, car_ref[h, rs, :])
                c_ref[h, rs, :] = c + lksum_ref[slot, h, rs, :]

            def s_values(w):
                oacc_ref[w[0], w[1]:w[1] + tr, :] += _dot(st.pop((w, "a")), vv)

            return _sb_items(edge), [s_weights, s_values]

        first = i * ratio
        _emit_skewed(front(first + ratio - 1, ratio - 1))
        for m in reversed(range(ratio - 1)):
            _emit_skewed(front(first + m, m), back(first + m + 1, m + 1))

        @pl.when(i == 0)
        def _():
            _emit_skewed(back(0, 0))

        def alive():
            return (jnp.max(c_ref[...]) >= SB_DEAD_CARRY).astype(jnp.int32)

        @pl.when(i > 0)
        def _():
            _emit_skewed(front(first - 1, None), back(first, 0))

            def step(state):
                pending, _ = state
                _emit_skewed(front(pending - 1, None), back(pending, None))
                return pending - 1, alive()

            pending, live = lax.while_loop(lambda s: (s[0] > 0) & (s[1] > 0), step, (first - 1, alive()))

            @pl.when(live > 0)
            def _():
                _emit_skewed(back(pending, None))

        o_ref[...] = jnp.where(low, oacc_ref[0], oacc_ref[1]).astype(BF16)

    return _call(
        body, (q, kt, v), comm=comm, **_grid_ends(N_HEADS // 2, s_len // tq), name="sb_fwd",
        grid=(N_HEADS // 2, s_len // tq),
        in_specs=[pl.BlockSpec((tq, LANES), lambda p, i: (i, p)),
                  pl.BlockSpec((1, nk, LANES, tk), lambda p, i: (p, 0, 0, 0)),
                  pl.BlockSpec((s_len, LANES), lambda p, i: (0, p))],
        out_specs=[pl.BlockSpec((tq, LANES), lambda p, i: (i, p)), pl.BlockSpec((2, tq, LANES), lambda p, i: (p, i, 0))],
        out_shape=[jax.ShapeDtypeStruct((s_len, N_HEADS * HEAD_DIM), BF16),
                   jax.ShapeDtypeStruct((N_HEADS, s_len, LANES), F32)],
        scratch_shapes=[pltpu.VMEM((2, tq, LANES), F32), pltpu.VMEM((2, tq, LANES), F32),
                        pltpu.VMEM((2, 2, tq, tk), F32), pltpu.VMEM((2, 2, tq, LANES), F32)],
        compiler_params=_cparams("arbitrary", "arbitrary"),
    )


def _sb_bwd(q, qt, kt, k, vt, do, dot, cars):
    s_len = q.shape[0]
    tq, tk, tr = SB_QUERIES, SB_KEYS, SB_ROWS
    nk, ratio = s_len // tk, tq // tk

    def body(q_ref, qt_ref, kt_ref, k_ref, vt_ref, do_ref, dot_ref, car_ref, dq_ref, dk_ref, dv_ref,
             gleft_ref, dqacc_ref, dkacc_ref, dvacc_ref, logw_ref, lsz_ref, da_ref, a_ref, dz_ref):
        i = pl.program_id(1)

        @pl.when(i == 0)
        def _():
            dkacc_ref[...] = jnp.zeros_like(dkacc_ref)
            dvacc_ref[...] = jnp.zeros_like(dvacc_ref)

        qv = q_ref[...]
        dov = do_ref[...]
        low, tri_right2, tri_left2 = _sb_consts(tq, tk)
        lane = lax.broadcasted_iota(jnp.int32, (tr, LANES), 1)
        zero = jnp.zeros_like(qv)
        q_heads = (jnp.where(low, qv, zero), jnp.where(low, zero, qv))
        do_heads = (jnp.where(low, dov, zero), jnp.where(low, zero, dov))
        q_t = qt_ref[0, 0]
        do_t = dot_ref[0, 0]
        gleft_ref[...] = jnp.zeros_like(gleft_ref)
        dqacc_ref[...] = jnp.zeros_like(dqacc_ref)

        def front(j, edge):
            ktv = kt_ref[0, j]
            vtv = vt_ref[0, j]
            slot = j % 2
            st = {}

            def s_logits(w):
                h, rs = w[0], slice(w[1], w[1] + tr)
                st[w, "z"] = _dot(q_heads[h][rs], ktv)
                da_ref[slot, h, rs, :] = _dot(do_heads[h][rs], vtv)

            def s_terms(w):
                h, rs = w[0], slice(w[1], w[1] + tr)
                valid = _sb_valid(w, edge) if w[2] else None
                lsz, lk = _sb_terms(st.pop((w, "z")), valid)
                st[w, "parts"] = _bf16_parts(lk)
                lsz = lsz if valid is None else jnp.where(valid, lsz, NEG_BIG)
                lsz_ref[slot, h, rs, :] = lsz
                st[w, "lszc"] = lsz + jnp.sum(jnp.where(lane == j, car_ref[h, rs, :], 0.0), axis=-1, keepdims=True)

            def s_suffix(w):
                logw_ref[slot, w[0], w[1]:w[1] + tr, :] = st.pop((w, "lszc")) + _dot(st.pop((w, "parts")), tri_right2)

            return _sb_items(edge), [s_logits, s_terms, s_suffix]

        def back(j, edge):
            kv = k_ref[pl.ds(pl.multiple_of(j * tk, tk), tk), :]
            slot = j % 2
            st = {}

            items = _sb_items(edge)
            head_rows = [[r0 for hh, r0, _ in items if hh == h] for h in range(2)]

            def s_weights(w):
                h, rs = w[0], slice(w[1], w[1] + tr)
                a = jnp.exp(logw_ref[slot, h, rs, :])
                g = a * da_ref[slot, h, rs, :]
                a_ref[h, rs, :] = a.astype(BF16)
                st[w, "g"], st[w, "parts"] = g, _bf16_parts(g)

            def s_prefix(w):
                st[w, "gs"] = _dot(st.pop((w, "parts")), tri_left2)

            def s_dz(w):
                h, rs = w[0], slice(w[1], w[1] + tr)
                g = st.pop((w, "g"))
                gleft = gleft_ref[h, rs, :]
                gsum = st.pop((w, "gs")) + jnp.tile(gleft, (1, tk // LANES))
                dz = (g - jnp.exp(lsz_ref[slot, h, rs, :]) * (g + gsum)).astype(BF16)
                st[w, "dz"] = dz
                dz_ref[h, rs, :] = dz
                gleft_ref[h, rs, :] = gleft + _row_sum_lanes(g)

            def s_products(w):
                h, rs = w[0], slice(w[1], w[1] + tr)
                dqacc_ref[h, rs, :] += _dot(st.pop((w, "dz")), kv)
                if w[1] == head_rows[h][-1]:
                    feat = slice(h * HEAD_DIM, (h + 1) * HEAD_DIM)
                    hr = slice(head_rows[h][0], tq)
                    dkacc_ref[j, feat, :] += _dot(q_t[feat, hr], dz_ref[h, hr, :])
                    dvacc_ref[j, feat, :] += _dot(do_t[feat, hr], a_ref[h, hr, :])

            return items, [s_weights, s_prefix, s_dz, s_products]

        first = i * ratio
        tile_max = jnp.max(jnp.maximum(car_ref[0], car_ref[1]), axis=0, keepdims=True)
        start = jnp.clip(first + ratio - jnp.sum(jnp.where(tile_max >= SB_DEAD_CARRY, 1, 0)), 0, first)

        @pl.when(start == first)
        def _():
            _emit_skewed(front(first, 0))

        @pl.when(start < first)
        def _():
            _emit_skewed(front(start, None))

            def step(jj, carry):
                _emit_skewed(front(jj, None), back(jj - 1, None))
                return carry

            lax.fori_loop(start + 1, first, step, 0)
            _emit_skewed(front(first, 0), back(first - 1, None))

        for m in range(1, ratio):
            _emit_skewed(front(first + m, m), back(first + m - 1, m - 1))
        _emit_skewed(back(first + ratio - 1, ratio - 1))
        dq_ref[...] = (Q_SCALE * jnp.where(low, dqacc_ref[0], dqacc_ref[1])).astype(BF16)

        @pl.when(i == s_len // tq - 1)
        def _():
            dk_ref[0] = dkacc_ref[...].astype(BF16)
            dv_ref[0] = dvacc_ref[...].astype(BF16)

    qblk = pl.BlockSpec((tq, LANES), lambda p, i: (i, p))
    qtblk = pl.BlockSpec((1, 1, LANES, tq), lambda p, i: (p, i, 0, 0))
    tblk = pl.BlockSpec((1, nk, LANES, tk), lambda p, i: (p, 0, 0, 0))
    col_full = pl.BlockSpec((s_len, LANES), lambda p, i: (0, p))
    tshape = jax.ShapeDtypeStruct((N_HEADS // 2, nk, LANES, tk), BF16)
    return pl.pallas_call(
        body, name="sb_bwd",
        grid=(N_HEADS // 2, s_len // tq),
        in_specs=[qblk, qtblk, tblk, col_full, tblk, qblk, qtblk, pl.BlockSpec((2, tq, LANES), lambda p, i: (p, i, 0))],
        out_specs=[qblk, tblk, tblk],
        out_shape=[jax.ShapeDtypeStruct((s_len, N_HEADS * HEAD_DIM), BF16), tshape, tshape],
        scratch_shapes=[pltpu.VMEM((2, tq, LANES), F32), pltpu.VMEM((2, tq, LANES), F32),
                        pltpu.VMEM((nk, LANES, tk), F32), pltpu.VMEM((nk, LANES, tk), F32)]
        + [pltpu.VMEM((2, 2, tq, tk), F32)] * 3 + [pltpu.VMEM((2, tq, tk), BF16)] * 2,
        compiler_params=_cparams("parallel", "arbitrary"),
    )(q, qt, kt, k, vt, do, dot, cars)


def _pad_heads(a, heads):
    s_len = a.shape[0]
    a = a.reshape(s_len, heads, HEAD_DIM)
    return jnp.pad(a, ((0, 0), (0, 0), (0, LANES - HEAD_DIM))).reshape(s_len, heads * LANES)


def _unpad_heads(a, heads):
    s_len = a.shape[0]
    return a.reshape(s_len, heads, LANES)[:, :, :HEAD_DIM].reshape(s_len, heads * HEAD_DIM)


def _tile_transposed(a, groups, t):
    s_len = a.shape[0]
    return a.reshape(s_len // t, t, groups, LANES).transpose(2, 0, 3, 1)


def _tile_untransposed(a):
    groups, nt, _, t = a.shape
    return a.transpose(1, 3, 0, 2).reshape(nt * t, groups * LANES)


def _local_step(xs, tgt, gains, sinks, rel_bias, weights_of, ship):
    g1, gmix, g2, gfin = gains
    bkt = _rel_bucket_matrix()
    groups = N_HEADS // 2
    grads = {}

    def carried(outs, comm, count):
        return outs[:count], (outs[count] if comm is not None else None)

    wts = dict(weights_of(0, None))
    comm = ship("weights", 1)
    (x1, h1, a1, b1, u1), landed = carried(
        _ffn_fwd(xs, g1, wts["ffn1_w1t"], wts["ffn1_w3t"], wts["ffn1_w2"], "1", comm), comm, 5)
    wts.update(weights_of(1, landed))
    hm, qa, ka, va, qb, kb, vb, ga, gb = _proj_fwd(x1, gmix, wts["w_int"])
    qa_p, ka_p, va_p = _pad_heads(qa, N_HEADS), _pad_heads(ka, SWA_KV_HEADS), _pad_heads(va, SWA_KV_HEADS)
    oa_p = _swa_fwd(rel_bias, sinks, bkt, qa_p, ka_p, va_p)
    kbt = _tile_transposed(kb, groups, SB_KEYS)
    comm = ship("weights", 2)
    (ob, cars), landed = carried(_sb_fwd(qb, kbt, vb, comm), comm, 2)
    wts.update(weights_of(2, landed))
    oa = _unpad_heads(oa_p, N_HEADS)
    x2, mg = _merge_fwd(x1, oa, ob, ga, gb, wts["w_swa"], wts["w_sb"], wts["w_out"])
    x3, h3, a3, b3, u3 = _ffn_fwd(x2, g2, wts["ffn2_w1t"], wts["ffn2_w3t"], wts["ffn2_w2"], "2")
    dx3, loss, dgfin = _loss_fwd_bwd(x3, tgt, gfin)

    dx2, dg2, da3, db3, dx3b = _ffn_bwd(dx3, x2, g2, a3, b3, wts["ffn2_w1t"], wts["ffn2_w3t"], wts["ffn2_w2"], "2")
    big = {"ffn2_w1t": _matmul_tn(da3, h3, "ffn2_w1"), "ffn2_w3t": _matmul_tn(db3, h3, "ffn2_w3"),
           "ffn2_w2": _matmul_tn(u3, dx3b, "ffn2_w2")}

    doa, dob, dga, dgb, dpa, dpb, dx2b = _merge_bwd(dx2, oa, ob, ga, gb, wts["w_swa"], wts["w_sb"], wts["w_out"])
    comm = ship("grads", GROUPS[2], big)
    (dqa_p, dka_p, dva_p, dtab, dsink), landed = carried(
        _swa_bwd(rel_bias, sinks, bkt, qa_p, ka_p, va_p, _pad_heads(doa, N_HEADS), comm), comm, 5)
    grads[GROUPS[2]] = big if comm is None else landed

    big = {"w_out": _matmul_tn(mg, dx2b, "w_out"), "w_swa": _matmul_tn(oa, dpa, "w_swa"),
           "w_sb": _matmul_tn(ob, dpb, "w_sb")}
    dqb, dkbt, dvbt = _sb_bwd(qb, _tile_transposed(qb, groups, SB_QUERIES), kbt, kb,
                              _tile_transposed(vb, groups, SB_KEYS), dob, _tile_transposed(dob, groups, SB_QUERIES), cars)
    dkb, dvb = _tile_untransposed(dkbt), _tile_untransposed(dvbt)
    dpieces = (_unpad_heads(dqa_p, N_HEADS), _unpad_heads(dka_p, SWA_KV_HEADS).astype(BF16),
               _unpad_heads(dva_p, SWA_KV_HEADS).astype(BF16), dqb, dkb, dvb, dga, dgb)
    big["w_int"] = jnp.concatenate([_matmul_tn(dp, hm, f"w_in{p}") for p, dp in enumerate(dpieces)], axis=0)
    dx1, dgmix = _proj_bwd(dpieces, dx2, x1, gmix, wts["w_int"])

    comm = ship("grads", GROUPS[1], big)
    (dx0, dg1, da1, db1, dx1b), landed = carried(
        _ffn_bwd(dx1, xs, g1, a1, b1, wts["ffn1_w1t"], wts["ffn1_w3t"], wts["ffn1_w2"], "1", comm), comm, 5)
    grads[GROUPS[1]] = big if comm is None else landed

    prev = None
    for name, lhs, rhs in (("ffn1_w1", da1, h1), ("ffn1_w3", db1, h1), ("ffn1_w2", u1, dx1b)):
        comm = None if prev is None else ship("grads", (prev[0],), prev[1])
        res = _matmul_tn(lhs, rhs, name, comm)
        if prev is not None:
            grads[(prev[0],)] = prev[1] if comm is None else res[1]
        prev = (name, {_GRAD_KEY[name]: res if comm is None else res[0]})
    grads[(prev[0],)] = prev[1]

    small = {"gains": (dg1, dgmix, dg2, dgfin), "sinks": dsink[:, 0], "rel_bias": dtab[:, :N_HEADS]}
    return loss, dx0, small, grads


def _my_place():
    return lax.axis_index("x"), lax.axis_index("y"), lax.axis_index("c")


def _flip(v, bit):
    return 1 - v if bit else v


_RELATIONS = tuple((k >> 2 & 1, k >> 1 & 1, k & 1) for k in range(1, N_DEV))


def _gather_weights(wp, tag):
    def body(x_ref, out_ref, send_sems, recv_sems, local_sem):
        x, y, c = _my_place()
        me, sibling = (x, y, c), (x, y, 1 - c)
        chips = [(1 - x, y), (x, 1 - y), (1 - x, 1 - y)]

        def rows(px, py, pc):
            return out_ref.at[4 * px + 2 * py + pc]

        def copy(k, block, to, src=None):
            return pltpu.make_async_remote_copy(
                src_ref=rows(*block) if src is None else src, dst_ref=rows(*block),
                send_sem=send_sems.at[k], recv_sem=recv_sems.at[k],
                device_id=to, device_id_type=pl.DeviceIdType.MESH)

        mine = pltpu.make_async_copy(x_ref, rows(*me), local_sem)
        mine.start()
        first = [copy(0, me, sibling, src=x_ref)]
        first += [copy(1 + j, me, (*chip, c), src=x_ref) for j, chip in enumerate(chips)]
        for cp in first:
            cp.start()
        passed = [copy(4 + j, (*chip, c), sibling) for j, chip in enumerate(chips)]
        for j, chip in enumerate(chips):
            copy(1 + j, (*chip, c), me).wait_recv()
            passed[j].start()
        copy(0, sibling, me).wait_recv()
        for j, chip in enumerate(chips):
            copy(4 + j, (*chip, 1 - c), me).wait_recv()
        for cp in first + passed:
            cp.wait_send()
        mine.wait()

    return pl.pallas_call(
        body, name=f"gather_weights_{tag}",
        out_shape=jax.ShapeDtypeStruct((N_DEV,) + wp.shape, wp.dtype),
        in_specs=[pl.BlockSpec(memory_space=pl.ANY)],
        out_specs=pl.BlockSpec(memory_space=pl.ANY),
        scratch_shapes=[pltpu.SemaphoreType.DMA((7,)), pltpu.SemaphoreType.DMA((7,)), pltpu.SemaphoreType.DMA(())],
    )(wp)


def _exchange_grads(gp, tag):
    def body(g_ref, out_ref, send_sems, recv_sems, local_sem):
        x, y, c = _my_place()
        me = 4 * x + 2 * y + c
        mine = pltpu.make_async_copy(g_ref.at[me], out_ref.at[me], local_sem)
        mine.start()
        copies = []
        for k, (fx, fy, fc) in enumerate(_RELATIONS):
            px, py, pc = _flip(x, fx), _flip(y, fy), _flip(c, fc)
            peer = 4 * px + 2 * py + pc
            copies.append((
                pltpu.make_async_remote_copy(
                    src_ref=g_ref.at[peer], dst_ref=out_ref.at[me], send_sem=send_sems.at[k], recv_sem=recv_sems.at[k],
                    device_id=(px, py, pc), device_id_type=pl.DeviceIdType.MESH),
                pltpu.make_async_remote_copy(
                    src_ref=g_ref.at[peer], dst_ref=out_ref.at[peer], send_sem=send_sems.at[k], recv_sem=recv_sems.at[k],
                    device_id=(px, py, pc), device_id_type=pl.DeviceIdType.MESH)))
        for out_cp, _ in copies:
            out_cp.start()
        for _, in_cp in copies:
            in_cp.wait_recv()
        for out_cp, _ in copies:
            out_cp.wait_send()
        mine.wait()

    return pl.pallas_call(
        body, name=f"exchange_grads_{tag}",
        out_shape=jax.ShapeDtypeStruct(gp.shape, gp.dtype),
        in_specs=[pl.BlockSpec(memory_space=pl.ANY)],
        out_specs=pl.BlockSpec(memory_space=pl.ANY),
        scratch_shapes=[pltpu.SemaphoreType.DMA((7,)), pltpu.SemaphoreType.DMA((7,)), pltpu.SemaphoreType.DMA(())],
    )(gp)


def _peers():
    x, y, c = _my_place()
    out = []
    for k, (fx, fy, fc) in enumerate(_RELATIONS):
        px, py, pc = _flip(x, fx), _flip(y, fy), _flip(c, fc)
        out.append((k, (px, py, pc), 4 * px + 2 * py + pc))
    return out, 4 * x + 2 * y + c


def _grid_ends(*grid):
    def first():
        return functools.reduce(lambda a, b: a & b, [pl.program_id(d) == 0 for d in range(len(grid))])

    def last():
        return functools.reduce(lambda a, b: a & b, [pl.program_id(d) == n - 1 for d, n in enumerate(grid)])

    return {"first": first, "last": last}


def _call(body, operands, *, comm=None, first=None, last=None, **kw):
    if comm is None:
        return pl.pallas_call(body, **kw)(*operands)
    src, per_peer = comm
    in_specs, out_specs, out_shape = list(kw.pop("in_specs")), list(kw.pop("out_specs")), list(kw.pop("out_shape"))
    scratch = list(kw.pop("scratch_shapes", ()))
    n_in, n_out, n_scr = len(in_specs), len(out_specs), len(scratch)
    land_shape = src.shape if per_peer else (N_DEV,) + src.shape

    def wrapped(*refs):
        ins, src_ref = refs[:n_in], refs[n_in]
        outs, land_ref = refs[n_in + 1:n_in + 1 + n_out], refs[n_in + 1 + n_out]
        scr = refs[n_in + 2 + n_out:n_in + 2 + n_out + n_scr]
        send_sems, recv_sems, local_sem = refs[n_in + 2 + n_out + n_scr:]
        peers, me = _peers()
        mine = pltpu.make_async_copy(src_ref.at[me] if per_peer else src_ref, land_ref.at[me], local_sem)
        going, coming = [], []
        for k, where, slab in peers:
            piece = src_ref.at[slab] if per_peer else src_ref
            going.append(pltpu.make_async_remote_copy(
                src_ref=piece, dst_ref=land_ref.at[me], send_sem=send_sems.at[k], recv_sem=recv_sems.at[k],
                device_id=where, device_id_type=pl.DeviceIdType.MESH))
            coming.append(pltpu.make_async_remote_copy(
                src_ref=piece, dst_ref=land_ref.at[slab], send_sem=send_sems.at[k], recv_sem=recv_sems.at[k],
                device_id=where, device_id_type=pl.DeviceIdType.MESH))

        @pl.when(first())
        def _():
            mine.start()
            for cp in going:
                cp.start()

        body(*ins, *outs, *scr)

        @pl.when(last())
        def _():
            for cp in coming:
                cp.wait_recv()
            for cp in going:
                cp.wait_send()
            mine.wait()

    anywhere = pl.BlockSpec(memory_space=pl.ANY)
    return pl.pallas_call(
        wrapped, in_specs=in_specs + [anywhere], out_specs=out_specs + [anywhere],
        out_shape=out_shape + [jax.ShapeDtypeStruct(land_shape, src.dtype)],
        scratch_shapes=scratch + [pltpu.SemaphoreType.DMA((N_DEV - 1,)), pltpu.SemaphoreType.DMA((N_DEV - 1,)),
                                  pltpu.SemaphoreType.DMA(())],
        **kw)(*operands, src)


def _adamw(w, g, m, v):
    m = ADAM_B1 * m + (1.0 - ADAM_B1) * g
    v = ADAM_B2 * v + (1.0 - ADAM_B2) * jnp.square(g)
    m_hat = m / (1.0 - ADAM_B1 ** ADAM_STEP)
    v_hat = v / (1.0 - ADAM_B2 ** ADAM_STEP)
    delta = -ADAM_LR * (m_hat / (jnp.sqrt(v_hat) + ADAM_EPS) + ADAM_WD * w)
    return delta, m, v


def _sum_and_adamw(parts, w, m, v, tr, tag):
    rows = w.shape[0]
    assert rows % tr == 0

    def body(p_ref, w_ref, m_ref, v_ref, g_out, d_out, m_out, v_out):
        g = p_ref[0].astype(F32)
        for d in range(1, N_DEV):
            g = g + p_ref[d].astype(F32)
        delta, mn, vn = _adamw(w_ref[...], g, m_ref[...], v_ref[...])
        g_out[...] = g
        d_out[...] = delta
        m_out[...] = mn
        v_out[...] = vn

    sp = pl.BlockSpec((tr, D_MODEL), lambda i: (i, 0))
    return pl.pallas_call(
        body, name=f"sum_and_adamw_{tag}",
        grid=(rows // tr,),
        in_specs=[pl.BlockSpec((N_DEV, tr, D_MODEL), lambda i: (0, i, 0)), sp, sp, sp],
        out_specs=[sp] * 4,
        out_shape=[jax.ShapeDtypeStruct(w.shape, F32)] * 4,
        compiler_params=_cparams("parallel"),
    )(parts, w, m, v)


def _small_allreduce_adamw(part, w, m, v):
    def body(p_ref, w_ref, m_ref, v_ref, g_out, d_out, m_out, v_out, buf, send_sems, recv_sems):
        x, y, c = _my_place()
        me = 4 * x + 2 * y + c
        buf[me] = p_ref[...]
        copies = []
        for k, (fx, fy, fc) in enumerate(_RELATIONS):
            px, py, pc = _flip(x, fx), _flip(y, fy), _flip(c, fc)
            peer = 4 * px + 2 * py + pc
            copies.append((
                pltpu.make_async_remote_copy(
                    src_ref=buf.at[me], dst_ref=buf.at[me], send_sem=send_sems.at[k], recv_sem=recv_sems.at[k],
                    device_id=(px, py, pc), device_id_type=pl.DeviceIdType.MESH),
                pltpu.make_async_remote_copy(
                    src_ref=buf.at[me], dst_ref=buf.at[peer], send_sem=send_sems.at[k], recv_sem=recv_sems.at[k],
                    device_id=(px, py, pc), device_id_type=pl.DeviceIdType.MESH)))
        for out_cp, _ in copies:
            out_cp.start()
        for _, in_cp in copies:
            in_cp.wait_recv()
        for out_cp, _ in copies:
            out_cp.wait_send()
        g = buf[0]
        for d in range(1, N_DEV):
            g = g + buf[d]
        delta, mn, vn = _adamw(w_ref[...], g, m_ref[...], v_ref[...])
        g_out[...] = g
        d_out[...] = delta
        m_out[...] = mn
        v_out[...] = vn

    vm = pl.BlockSpec(memory_space=pltpu.VMEM)
    return pl.pallas_call(
        body, name="small_allreduce_adamw",
        in_specs=[vm] * 4, out_specs=[vm] * 4,
        out_shape=[jax.ShapeDtypeStruct(w.shape, F32)] * 4,
        scratch_shapes=[pltpu.VMEM((N_DEV,) + part.shape, F32),
                        pltpu.SemaphoreType.DMA((7,)), pltpu.SemaphoreType.DMA((7,))],
    )(part, w, m, v)


_TRANSPOSED = ("ffn1_w1", "ffn1_w3", "w_in", "ffn2_w1", "ffn2_w3")
_BRANCH = ("w_branch_swa", "w_branch_sb")


def _pack_shards(t, names):
    parts = []
    for name in names:
        a = t[name][0]
        if name in _TRANSPOSED:
            a = a.T
        elif name in _BRANCH:
            a = a.reshape(64, D_MODEL)
        parts.append(a)
    return jnp.concatenate(parts, axis=0)


def _unpack_shards(p, names):
    out, lo = {}, 0
    for name in names:
        a = p[lo:lo + BIG_ROWS[BIG_NAMES.index(name)]]
        lo += a.shape[0]
        if name in _TRANSPOSED:
            a = a.T
        elif name in _BRANCH:
            a = a.reshape(512, 128)
        out[name] = a[None]
    return out


def _full_weights(wg, names):
    out, lo = {}, 0
    for name in names:
        rows = BIG_ROWS[BIG_NAMES.index(name)]
        a = wg[:, lo:lo + rows]
        lo += rows
        if name in _BRANCH:
            a = a.reshape(N_DEV, 512, 128).transpose(1, 0, 2).reshape(512, D_MODEL)
        out[_GRAD_KEY[name]] = a.reshape(-1, D_MODEL)
    return out


_GRAD_KEY = {"ffn1_w1": "ffn1_w1t", "ffn1_w3": "ffn1_w3t", "ffn1_w2": "ffn1_w2", "w_in": "w_int",
             "w_branch_swa": "w_swa", "w_branch_sb": "w_sb", "w_out": "w_out",
             "ffn2_w1": "ffn2_w1t", "ffn2_w3": "ffn2_w3t", "ffn2_w2": "ffn2_w2"}


def _pack_full_grads(big, names):
    parts = []
    for name in names:
        a = big[_GRAD_KEY[name]]
        if name in _BRANCH:
            a = a.reshape(512, N_DEV, 128).transpose(1, 0, 2)
        parts.append(a.reshape(N_DEV, BIG_ROWS[BIG_NAMES.index(name)], D_MODEL).astype(BF16))
    return jnp.concatenate(parts, axis=1)


_SMALL_NAMES = ("norm_ffn1", "norm_mix", "norm_ffn2", "norm_final", "swa_sinks", "rel_bias")


def _pack_small(vals):
    rows = []
    for a in vals:
        a = a.reshape(-1)
        rows.append(jnp.pad(a, (0, D_MODEL - a.shape[0])))
    rows += [jnp.zeros((D_MODEL,), F32)] * (SMALL_ROWS - len(rows))
    return jnp.stack(rows)


def _unpack_small(p):
    return {"norm_ffn1": p[0:1], "norm_mix": p[1:2], "norm_ffn2": p[2:3], "norm_final": p[3],
            "swa_sinks": p[4:5, :N_HEADS], "rel_bias": p[5, :REL_BUCKETS * N_HEADS].reshape(REL_BUCKETS, N_HEADS)}


ALL_NAMES = ("norm_ffn1", "ffn1_w1", "ffn1_w3", "ffn1_w2", "norm_mix", "w_in", "swa_sinks", "rel_bias",
             "w_branch_swa", "w_branch_sb", "w_out", "norm_ffn2", "ffn2_w1", "ffn2_w3", "ffn2_w2", "norm_final")


def kernel(x, norm_ffn1, ffn1_w1, ffn1_w3, ffn1_w2, norm_mix, w_in, swa_sinks, rel_bias, w_branch_swa, w_branch_sb, w_out, norm_ffn2, ffn2_w1, ffn2_w3, ffn2_w2, norm_final, loss_target, m_norm_ffn1, m_ffn1_w1, m_ffn1_w3, m_ffn1_w2, m_norm_mix, m_w_in, m_swa_sinks, m_rel_bias, m_w_branch_swa, m_w_branch_sb, m_w_out, m_norm_ffn2, m_ffn2_w1, m_ffn2_w3, m_ffn2_w2, m_norm_final, v_norm_ffn1, v_ffn1_w1, v_ffn1_w3, v_ffn1_w2, v_norm_mix, v_w_in, v_swa_sinks, v_rel_bias, v_w_branch_swa, v_w_branch_sb, v_w_out, v_norm_ffn2, v_ffn2_w1, v_ffn2_w3, v_ffn2_w2, v_norm_final):
    w = dict(zip(ALL_NAMES, (norm_ffn1, ffn1_w1, ffn1_w3, ffn1_w2, norm_mix, w_in, swa_sinks, rel_bias,
                             w_branch_swa, w_branch_sb, w_out, norm_ffn2, ffn2_w1, ffn2_w3, ffn2_w2, norm_final)))
    m = dict(zip(ALL_NAMES, (m_norm_ffn1, m_ffn1_w1, m_ffn1_w3, m_ffn1_w2, m_norm_mix, m_w_in, m_swa_sinks, m_rel_bias,
                             m_w_branch_swa, m_w_branch_sb, m_w_out, m_norm_ffn2, m_ffn2_w1, m_ffn2_w3, m_ffn2_w2,
                             m_norm_final)))
    v = dict(zip(ALL_NAMES, (v_norm_ffn1, v_ffn1_w1, v_ffn1_w3, v_ffn1_w2, v_norm_mix, v_w_in, v_swa_sinks, v_rel_bias,
                             v_w_branch_swa, v_w_branch_sb, v_w_out, v_norm_ffn2, v_ffn2_w1, v_ffn2_w3, v_ffn2_w2,
                             v_norm_final)))

    w_packed = [_pack_shards(w, names) for names in GROUPS]
    gathered0 = _gather_weights(w_packed[0].astype(BF16), "group0")

    def weights_of(group, landed):
        return _full_weights(gathered0 if group == 0 else landed, GROUPS[group])

    def ship(kind, which, grads=None):
        if kind == "weights":
            return w_packed[which].astype(BF16), False
        return _pack_full_grads(grads, which), True

    gains = (norm_ffn1, norm_mix, norm_ffn2, norm_final.reshape(1, D_MODEL))
    loss, dx, small, parts = _local_step(x[0], loss_target[0], gains, swa_sinks, rel_bias, weights_of, ship)

    big_outs = [{}, {}, {}, {}]
    for names, tile in zip(SUM_GROUPS, SUM_TILE):
        landed = parts[names]
        if isinstance(landed, dict):
            landed = _exchange_grads(_pack_full_grads(landed, names), names[0])
        res = _sum_and_adamw(landed, _pack_shards(w, names), _pack_shards(m, names), _pack_shards(v, names),
                             tile, names[0])
        for acc, packed in zip(big_outs, res):
            acc.update(_unpack_shards(packed, names))
    g_big, d_big, m_big, v_big = big_outs

    small_part = _pack_small(small["gains"] + (small["sinks"], small["rel_bias"], loss))
    zero = jnp.zeros((1,), F32)
    small_res = _small_allreduce_adamw(
        small_part, _pack_small([w[n] for n in _SMALL_NAMES] + [zero]), _pack_small([m[n] for n in _SMALL_NAMES] + [zero]),
        _pack_small([v[n] for n in _SMALL_NAMES] + [zero]))
    g_sm, d_sm, m_sm, v_sm = (_unpack_small(p) for p in small_res)

    outs = [small_res[0][len(_SMALL_NAMES), 0], dx[None]]
    for big_d, small_d in ((g_big, g_sm), (d_big, d_sm), (m_big, m_sm), (v_big, v_sm)):
        merged = {**big_d, **small_d}
        outs += [merged[n] for n in ALL_NAMES]
    return tuple(outs)
```

```python
import functools

import jax
import jax.numpy as jnp
import numpy as np
from jax import lax
from jax.experimental import pallas as pl
from jax.experimental.pallas import tpu as pltpu

F32 = jnp.float32
BF16 = jnp.bfloat16

D_MODEL = 1024
D_FF = 2816
HEAD_DIM = 64
N_HEADS = 8
SWA_KV_HEADS = 2
SWA_GROUP = 4
SWA_BLOCK = 128
REL_BUCKETS = 32
REL_MAX_DIST = 128
RMS_EPS = 1e-6
NEG_BIG = -1e30
Q_SCALE = HEAD_DIM ** -0.5
LANES = 128

N_DEV = 8

ADAM_LR = 0.001
ADAM_B1 = 0.9
ADAM_B2 = 0.999
ADAM_EPS = 1e-08
ADAM_WD = 0.01
ADAM_STEP = 10

IN_SIZES = (512, 128, 128, 512, 512, 512, 1024, 1024)
IN_OFFS = tuple(int(v) for v in np.cumsum((0,) + IN_SIZES))
IN_W = IN_OFFS[-1]

BIG_NAMES = ("ffn1_w1", "ffn1_w3", "ffn1_w2", "w_in", "w_branch_swa", "w_branch_sb", "w_out",
             "ffn2_w1", "ffn2_w3", "ffn2_w2")
BIG_ROWS = (352, 352, 352, 544, 64, 64, 128, 352, 352, 352)
SMALL_ROWS = 8
GROUPS = (BIG_NAMES[0:3], BIG_NAMES[3:7], BIG_NAMES[7:10])
SUM_GROUPS = tuple((n,) for n in GROUPS[0]) + GROUPS[1:]
SUM_TILE = (176, 176, 176, 160, 96)

VMEM_LIMIT = 56 * 1024 * 1024
FFN_PIECES = 2
SB_QUERIES = 512
SB_KEYS = 256
SB_ROWS = 256
SB_SUM_PARTS = 1
SB_LOGIT_CAP = 80.0
SB_DEAD_CARRY = -110.0


def _dot(a, b):
    return jnp.dot(a, b, preferred_element_type=F32)


def _dot_nt(a, b):
    return lax.dot_general(a, b, (((1,), (1,)), ((), ())), preferred_element_type=F32)


def _dot_tn(a, b):
    return lax.dot_general(a, b, (((0,), (0,)), ((), ())), preferred_element_type=F32)


def _cparams(*sem):
    return pltpu.CompilerParams(dimension_semantics=sem, vmem_limit_bytes=VMEM_LIMIT)


def _rms_rstd(xv):
    return lax.rsqrt(jnp.mean(xv * xv, axis=-1, keepdims=True) + RMS_EPS)


def _rms_bwd(dh, xv, r, g):
    xhat = xv * r
    dg = jnp.sum(dh * xhat, axis=0, keepdims=True)
    dxn = dh * g
    dx = r * (dxn - xhat * jnp.mean(dxn * xhat, axis=-1, keepdims=True))
    return dx, dg


def _ffn_fwd(x, g, w1t, w3t, w2, tag, comm=None):
    s_len = x.shape[0]
    tm, tf = min(1024, s_len), 256
    nf = D_FF // tf

    def body(x_ref, g_ref, w1_ref, w3_ref, w2_ref, xo_ref, h_ref, a_ref, b_ref, u_ref, acc_ref, hs_ref):
        j = pl.program_id(1)

        @pl.when(j == 0)
        def _():
            xv = x_ref[...]
            h = (xv * _rms_rstd(xv) * g_ref[...]).astype(BF16)
            hs_ref[...] = h
            h_ref[...] = h
            acc_ref[...] = jnp.zeros_like(acc_ref)

        st = {}

        def s_up(rs):
            h = hs_ref[rs, :]
            st[rs.start, "ab"] = (_dot_nt(h, w1_ref[...]), _dot_nt(h, w3_ref[...]))

        def s_act(rs):
            a, b = st.pop((rs.start, "ab"))
            a_ref[rs, :] = a.astype(BF16)
            b_ref[rs, :] = b.astype(BF16)
            uh = (0.5 * (a * jax.nn.sigmoid(a) * b)).astype(BF16)
            u_ref[rs, :] = uh
            st[rs.start, "u"] = uh

        def s_down(rs):
            acc_ref[rs, :] += _dot(st.pop((rs.start, "u")), w2_ref[...])

        _emit_skewed(([slice(r, r + tm // FFN_PIECES) for r in range(0, tm, tm // FFN_PIECES)], [s_up, s_act, s_down]))

        @pl.when(j == nf - 1)
        def _():
            xo_ref[...] = x_ref[...] + acc_ref[...]

    row = lambda i, j: (i, 0)
    return _call(
        body, (x, g, w1t, w3t, w2), comm=comm, **_grid_ends(s_len // tm, nf), name=f"ffn_fwd_{tag}",
        grid=(s_len // tm, nf),
        in_specs=[pl.BlockSpec((tm, D_MODEL), row), pl.BlockSpec((1, D_MODEL), lambda i, j: (0, 0)),
                  pl.BlockSpec((tf, D_MODEL), lambda i, j: (j, 0)), pl.BlockSpec((tf, D_MODEL), lambda i, j: (j, 0)),
                  pl.BlockSpec((tf, D_MODEL), lambda i, j: (j, 0))],
        out_specs=[pl.BlockSpec((tm, D_MODEL), row), pl.BlockSpec((tm, D_MODEL), row),
                   pl.BlockSpec((tm, tf), lambda i, j: (i, j)), pl.BlockSpec((tm, tf), lambda i, j: (i, j)),
                   pl.BlockSpec((tm, tf), lambda i, j: (i, j))],
        out_shape=[jax.ShapeDtypeStruct((s_len, D_MODEL), F32), jax.ShapeDtypeStruct((s_len, D_MODEL), BF16),
                   jax.ShapeDtypeStruct((s_len, D_FF), BF16), jax.ShapeDtypeStruct((s_len, D_FF), BF16),
                   jax.ShapeDtypeStruct((s_len, D_FF), BF16)],
        scratch_shapes=[pltpu.VMEM((tm, D_MODEL), F32), pltpu.VMEM((tm, D_MODEL), BF16)],
        compiler_params=_cparams("arbitrary", "arbitrary"),
    )


def _ffn_bwd(dy, x, g, a, b, w1t, w3t, w2, tag, comm=None):
    s_len = x.shape[0]
    tm, tf = min(1024, s_len), 256
    nf = D_FF // tf

    def body(dy_ref, x_ref, g_ref, a_ref, b_ref, w1_ref, w3_ref, w2_ref,
             dx_ref, dg_ref, da_ref, db_ref, dyb_ref, acc_ref, dys_ref):
        i, j = pl.program_id(0), pl.program_id(1)

        @pl.when(j == 0)
        def _():
            dyb = dy_ref[...].astype(BF16)
            dys_ref[...] = dyb
            dyb_ref[...] = dyb
            acc_ref[...] = jnp.zeros_like(acc_ref)

        @pl.when((i == 0) & (j == 0))
        def _():
            dg_ref[...] = jnp.zeros_like(dg_ref)

        st = {}

        def s_du(rs):
            st[rs.start, "du"] = 0.5 * _dot_nt(dys_ref[rs, :], w2_ref[...])

        def s_act(rs):
            du = st.pop((rs.start, "du"))
            av = a_ref[rs, :].astype(F32)
            bv = b_ref[rs, :].astype(F32)
            sg = jax.nn.sigmoid(av)
            sil = av * sg
            da = (du * bv * (sg + sil * (1.0 - sg))).astype(BF16)
            db = (du * sil).astype(BF16)
            da_ref[rs, :] = da
            db_ref[rs, :] = db
            st[rs.start, "dab"] = (da, db)

        def s_dh(rs):
            da, db = st.pop((rs.start, "dab"))
            acc_ref[rs, :] += _dot(da, w1_ref[...]) + _dot(db, w3_ref[...])

        _emit_skewed(([slice(r, r + tm // FFN_PIECES) for r in range(0, tm, tm // FFN_PIECES)], [s_du, s_act, s_dh]))

        @pl.when(j == nf - 1)
        def _():
            xv = x_ref[...]
            dx, dg = _rms_bwd(acc_ref[...], xv, _rms_rstd(xv), g_ref[...])
            dx_ref[...] = dy_ref[...] + dx
            dg_ref[...] += dg

    row = lambda i, j: (i, 0)
    blk = lambda i, j: (i, j)
    wsp = pl.BlockSpec((tf, D_MODEL), lambda i, j: (j, 0))
    return _call(
        body, (dy, x, g, a, b, w1t, w3t, w2), comm=comm, **_grid_ends(s_len // tm, nf), name=f"ffn_bwd_{tag}",
        grid=(s_len // tm, nf),
        in_specs=[pl.BlockSpec((tm, D_MODEL), row), pl.BlockSpec((tm, D_MODEL), row),
                  pl.BlockSpec((1, D_MODEL), lambda i, j: (0, 0)),
                  pl.BlockSpec((tm, tf), blk), pl.BlockSpec((tm, tf), blk), wsp, wsp, wsp],
        out_specs=[pl.BlockSpec((tm, D_MODEL), row), pl.BlockSpec((1, D_MODEL), lambda i, j: (0, 0)),
                   pl.BlockSpec((tm, tf), blk), pl.BlockSpec((tm, tf), blk), pl.BlockSpec((tm, D_MODEL), row)],
        out_shape=[jax.ShapeDtypeStruct((s_len, D_MODEL), F32), jax.ShapeDtypeStruct((1, D_MODEL), F32),
                   jax.ShapeDtypeStruct((s_len, D_FF), BF16), jax.ShapeDtypeStruct((s_len, D_FF), BF16),
                   jax.ShapeDtypeStruct((s_len, D_MODEL), BF16)],
        scratch_shapes=[pltpu.VMEM((tm, D_MODEL), F32), pltpu.VMEM((tm, D_MODEL), BF16)],
        compiler_params=_cparams("arbitrary", "arbitrary"),
    )


def _matmul_tn(lhs, rhs, tag, comm=None):
    s_len, m = lhs.shape
    n = rhs.shape[1]
    tm = min(512, s_len)
    tj = m if m <= 1024 else 1408
    assert m % tj == 0
    last_rows = s_len // tm - 1

    def body(l_ref, r_ref, o_ref, acc_ref):
        i = pl.program_id(1)

        @pl.when(i == 0)
        def _():
            acc_ref[...] = jnp.zeros_like(acc_ref)

        acc_ref[...] += _dot_tn(l_ref[...], r_ref[...])

        @pl.when(i == last_rows)
        def _():
            o_ref[...] = acc_ref[...].astype(BF16)

    res = _call(
        body, (lhs, rhs), comm=comm, **_grid_ends(m // tj, s_len // tm), name=f"matmul_tn_{tag}",
        grid=(m // tj, s_len // tm),
        in_specs=[pl.BlockSpec((tm, tj), lambda j, i: (i, j)), pl.BlockSpec((tm, n), lambda j, i: (i, 0))],
        out_specs=[pl.BlockSpec((tj, n), lambda j, i: (j, 0))],
        out_shape=[jax.ShapeDtypeStruct((m, n), BF16)],
        scratch_shapes=[pltpu.VMEM((tj, n), F32)],
        compiler_params=_cparams("arbitrary", "arbitrary"),
    )
    return res[0] if comm is None else tuple(res)


def _proj_fwd(x1, g, wint):
    s_len = x1.shape[0]
    tm = min(512, s_len)
    dts = (BF16, BF16, BF16, BF16, BF16, BF16, F32, F32)

    def body(x_ref, g_ref, w_ref, h_ref, *outs):
        xv = x_ref[...]
        h = (xv * _rms_rstd(xv) * g_ref[...]).astype(BF16)
        h_ref[...] = h
        for p, o_ref in enumerate(outs):
            val = _dot_nt(h, w_ref[IN_OFFS[p]:IN_OFFS[p + 1], :])
            if p == 3:
                val = val * Q_SCALE
            o_ref[...] = val.astype(dts[p])

    row = lambda i: (i, 0)
    return pl.pallas_call(
        body, name="proj_fwd",
        grid=(s_len // tm,),
        in_specs=[pl.BlockSpec((tm, D_MODEL), row), pl.BlockSpec((1, D_MODEL), lambda i: (0, 0)),
                  pl.BlockSpec((IN_W, D_MODEL), lambda i: (0, 0))],
        out_specs=[pl.BlockSpec((tm, D_MODEL), row)] + [pl.BlockSpec((tm, w), row) for w in IN_SIZES],
        out_shape=[jax.ShapeDtypeStruct((s_len, D_MODEL), BF16)]
        + [jax.ShapeDtypeStruct((s_len, w), dt) for w, dt in zip(IN_SIZES, dts)],
        compiler_params=_cparams("parallel"),
    )(x1, g, wint)


def _proj_bwd(dpieces, dx2, x1, g, wint):
    s_len = x1.shape[0]
    tm = min(512, s_len)

    def body(*refs):
        dps = refs[:8]
        dx2_ref, x_ref, g_ref, w_ref, dx_ref, dg_ref = refs[8:]

        @pl.when(pl.program_id(0) == 0)
        def _():
            dg_ref[...] = jnp.zeros_like(dg_ref)

        dh = _dot(dps[0][...], w_ref[IN_OFFS[0]:IN_OFFS[1], :])
        for p in range(1, 8):
            dh += _dot(dps[p][...], w_ref[IN_OFFS[p]:IN_OFFS[p + 1], :])
        xv = x_ref[...]
        dx, dg = _rms_bwd(dh, xv, _rms_rstd(xv), g_ref[...])
        dx_ref[...] = dx2_ref[...] + dx
        dg_ref[...] += dg

    row = lambda i: (i, 0)
    return pl.pallas_call(
        body, name="proj_bwd",
        grid=(s_len // tm,),
        in_specs=[pl.BlockSpec((tm, w), row) for w in IN_SIZES]
        + [pl.BlockSpec((tm, D_MODEL), row), pl.BlockSpec((tm, D_MODEL), row),
           pl.BlockSpec((1, D_MODEL), lambda i: (0, 0)), pl.BlockSpec((IN_W, D_MODEL), lambda i: (0, 0))],
        out_specs=[pl.BlockSpec((tm, D_MODEL), row), pl.BlockSpec((1, D_MODEL), lambda i: (0, 0))],
        out_shape=[jax.ShapeDtypeStruct((s_len, D_MODEL), F32), jax.ShapeDtypeStruct((1, D_MODEL), F32)],
        compiler_params=_cparams("arbitrary"),
    )(*dpieces, dx2, x1, g, wint)


def _merge_fwd(x1, oa, ob, ga, gb, wswa, wsb, wout):
    s_len = x1.shape[0]
    tm = min(512, s_len)

    def body(x_ref, oa_ref, ob_ref, ga_ref, gb_ref, wa_ref, wb_ref, wo_ref, xo_ref, mg_ref):
        pa = _dot(oa_ref[...], wa_ref[...])
        pb = _dot(ob_ref[...], wb_ref[...])
        mg = (jax.nn.sigmoid(ga_ref[...]) * pa + jax.nn.sigmoid(gb_ref[...]) * pb).astype(BF16)
        mg_ref[...] = mg
        xo_ref[...] = x_ref[...] + _dot(mg, wo_ref[...])

    row = lambda i: (i, 0)
    full = lambda i: (0, 0)
    return pl.pallas_call(
        body, name="merge_fwd",
        grid=(s_len // tm,),
        in_specs=[pl.BlockSpec((tm, D_MODEL), row), pl.BlockSpec((tm, 512), row), pl.BlockSpec((tm, 512), row),
                  pl.BlockSpec((tm, D_MODEL), row), pl.BlockSpec((tm, D_MODEL), row),
                  pl.BlockSpec((512, D_MODEL), full), pl.BlockSpec((512, D_MODEL), full),
                  pl.BlockSpec((D_MODEL, D_MODEL), full)],
        out_specs=[pl.BlockSpec((tm, D_MODEL), row), pl.BlockSpec((tm, D_MODEL), row)],
        out_shape=[jax.ShapeDtypeStruct((s_len, D_MODEL), F32), jax.ShapeDtypeStruct((s_len, D_MODEL), BF16)],
        compiler_params=_cparams("parallel"),
    )(x1, oa, ob, ga, gb, wswa, wsb, wout)


def _merge_bwd(dx2, oa, ob, ga, gb, wswa, wsb, wout):
    s_len = dx2.shape[0]
    tm = min(512, s_len)

    def body(dx_ref, oa_ref, ob_ref, ga_ref, gb_ref, wa_ref, wb_ref, wo_ref,
             doa_ref, dob_ref, dga_ref, dgb_ref, dpa_ref, dpb_ref, dxb_ref):
        dxb = dx_ref[...].astype(BF16)
        dxb_ref[...] = dxb
        dmg = _dot_nt(dxb, wo_ref[...])
        for o_ref, g_ref, w_ref, do_ref, dg_ref, dp_ref in (
                (oa_ref, ga_ref, wa_ref, doa_ref, dga_ref, dpa_ref),
                (ob_ref, gb_ref, wb_ref, dob_ref, dgb_ref, dpb_ref)):
            pv = _dot(o_ref[...], w_ref[...])
            sg = jax.nn.sigmoid(g_ref[...])
            dp = (dmg * sg).astype(BF16)
            dp_ref[...] = dp
            dg_ref[...] = (dmg * pv * sg * (1.0 - sg)).astype(BF16)
            do_ref[...] = _dot_nt(dp, w_ref[...]).astype(BF16)

    row = lambda i: (i, 0)
    full = lambda i: (0, 0)
    wide = pl.BlockSpec((tm, D_MODEL), row)
    half = pl.BlockSpec((tm, 512), row)
    return pl.pallas_call(
        body, name="merge_bwd",
        grid=(s_len // tm,),
        in_specs=[wide, half, half, wide, wide, pl.BlockSpec((512, D_MODEL), full),
                  pl.BlockSpec((512, D_MODEL), full), pl.BlockSpec((D_MODEL, D_MODEL), full)],
        out_specs=[half, half, wide, wide, wide, wide, wide],
        out_shape=[jax.ShapeDtypeStruct((s_len, 512), BF16)] * 2 + [jax.ShapeDtypeStruct((s_len, D_MODEL), BF16)] * 5,
        compiler_params=_cparams("parallel"),
    )(dx2, oa, ob, ga, gb, wswa, wsb, wout)


def _loss_fwd_bwd(x3, tgt, g):
    s_len = x3.shape[0]
    tm = min(1024, s_len)

    def body(x_ref, t_ref, g_ref, dx_ref, loss_ref, dg_ref):
        @pl.when(pl.program_id(0) == 0)
        def _():
            loss_ref[...] = jnp.zeros_like(loss_ref)
            dg_ref[...] = jnp.zeros_like(dg_ref)

        xv = x_ref[...]
        gv = g_ref[...]
        r = _rms_rstd(xv)
        err = xv * r * gv - t_ref[...]
        loss_ref[...] += 0.5 * jnp.sum(jnp.mean(err * err, axis=-1, keepdims=True), axis=0, keepdims=True)
        dx, dg = _rms_bwd(err * (1.0 / D_MODEL), xv, r, gv)
        dx_ref[...] = dx
        dg_ref[...] += dg

    row = lambda i: (i, 0)
    return pl.pallas_call(
        body, name="loss_fwd_bwd",
        grid=(s_len // tm,),
        in_specs=[pl.BlockSpec((tm, D_MODEL), row), pl.BlockSpec((tm, D_MODEL), row),
                  pl.BlockSpec((1, D_MODEL), lambda i: (0, 0))],
        out_specs=[pl.BlockSpec((tm, D_MODEL), row), pl.BlockSpec((1, 1), lambda i: (0, 0)),
                   pl.BlockSpec((1, D_MODEL), lambda i: (0, 0))],
        out_shape=[jax.ShapeDtypeStruct((s_len, D_MODEL), F32), jax.ShapeDtypeStruct((1, 1), F32),
                   jax.ShapeDtypeStruct((1, D_MODEL), F32)],
        compiler_params=_cparams("arbitrary"),
    )(x3, tgt, g)


def _rel_bucket_matrix():
    qi = jnp.arange(SWA_BLOCK)[:, None] + SWA_BLOCK
    kj = jnp.arange(2 * SWA_BLOCK)[None, :]
    dist = jnp.maximum(qi - kj, 0)
    max_exact = REL_BUCKETS // 2
    d = jnp.maximum(dist, 1).astype(F32)
    large = max_exact + (jnp.log(d / max_exact) / np.log(REL_MAX_DIST / max_exact)
                         * (REL_BUCKETS - max_exact)).astype(jnp.int32)
    large = jnp.minimum(large, REL_BUCKETS - 1)
    return jnp.where(dist < max_exact, dist, large).astype(jnp.int32)


def _swa_bias_into(bias_ref, bkt_ref, tab_ref):
    bk = bkt_ref[...]
    for h in range(N_HEADS):
        acc = jnp.zeros(bk.shape, F32)
        for bucket in range(REL_BUCKETS):
            acc = jnp.where(bk == bucket, tab_ref[bucket, h], acc)
        bias_ref[h] = acc


def _swa_valid(n):
    shape = (SWA_BLOCK, 2 * SWA_BLOCK)
    row = lax.broadcasted_iota(jnp.int32, shape, 0)
    col = lax.broadcasted_iota(jnp.int32, shape, 1)
    dist = row + SWA_BLOCK - col
    return (dist >= 0) & (dist < SWA_BLOCK) & ((col >= SWA_BLOCK) | (n > 0))


def _swa_windows(kp_ref, kc_ref, vp_ref, vc_ref):
    lanes = [slice(g * LANES, (g + 1) * LANES) for g in range(SWA_KV_HEADS)]
    return ([jnp.concatenate([kp_ref[:, gl], kc_ref[:, gl]], axis=0) for gl in lanes],
            [jnp.concatenate([vp_ref[:, gl], vc_ref[:, gl]], axis=0) for gl in lanes])


def _swa_probs(qk, bias, sink, valid):
    lg = jnp.where(valid, qk * Q_SCALE + bias, NEG_BIG)
    m = jnp.maximum(jnp.max(lg, axis=-1, keepdims=True), sink)
    e = jnp.exp(lg - m)
    es = jnp.exp(sink - m)
    inv = 1.0 / (jnp.sum(e, axis=-1, keepdims=True) + es)
    return e * inv, es * inv


def _swa_specs(s_len):
    blk = SWA_BLOCK
    cur = lambda n: (n, 0)
    prev = lambda n: (jnp.maximum(n - 1, 0), 0)
    kvw = SWA_KV_HEADS * LANES
    return [pl.BlockSpec(memory_space=pltpu.SMEM), pl.BlockSpec(memory_space=pltpu.SMEM),
            pl.BlockSpec((blk, 2 * blk), lambda n: (0, 0)),
            pl.BlockSpec((blk, N_HEADS * LANES), cur),
            pl.BlockSpec((blk, kvw), prev), pl.BlockSpec((blk, kvw), cur),
            pl.BlockSpec((blk, kvw), prev), pl.BlockSpec((blk, kvw), cur)]


def _swa_fwd(tab, sinks, bkt, q, k, v):
    s_len = q.shape[0]
    blk = SWA_BLOCK

    def body(tab_ref, sink_ref, bkt_ref, q_ref, kp_ref, kc_ref, vp_ref, vc_ref, o_ref, bias_ref):
        n = pl.program_id(0)

        @pl.when(n == 0)
        def _():
            _swa_bias_into(bias_ref, bkt_ref, tab_ref)

        valid = _swa_valid(n)
        kk, vv = _swa_windows(kp_ref, kc_ref, vp_ref, vc_ref)
        st = {}

        def s_logits(h):
            st[h, "lg"] = _dot_nt(q_ref[:, h * LANES:(h + 1) * LANES], kk[h // SWA_GROUP])

        def s_probs(h):
            st[h, "p"] = _swa_probs(st.pop((h, "lg")), bias_ref[h], sink_ref[0, h], valid)[0].astype(BF16)

        def s_values(h):
            o_ref[:, h * LANES:(h + 1) * LANES] = _dot(st.pop((h, "p")), vv[h // SWA_GROUP]).astype(BF16)

        _emit_skewed((list(range(N_HEADS)), [s_logits, s_probs, s_values]))

    return pl.pallas_call(
        body, name="swa_fwd",
        grid=(s_len // blk,),
        in_specs=_swa_specs(s_len),
        out_specs=pl.BlockSpec((blk, N_HEADS * LANES), lambda n: (n, 0)),
        out_shape=jax.ShapeDtypeStruct((s_len, N_HEADS * LANES), BF16),
        scratch_shapes=[pltpu.VMEM((N_HEADS, blk, 2 * blk), F32)],
        compiler_params=_cparams("arbitrary"),
    )(tab, sinks, bkt, q, k, k, v, v)


def _swa_bwd(tab, sinks, bkt, q, k, v, do, comm=None):
    s_len = q.shape[0]
    blk = SWA_BLOCK
    nb = s_len // blk
    kvw = SWA_KV_HEADS * LANES

    def body(tab_ref, sink_ref, bkt_ref, q_ref, kp_ref, kc_ref, vp_ref, vc_ref, do_ref,
             dq_ref, dk_ref, dv_ref, dtab_ref, dsink_ref, bias_ref, dbias_ref):
        n = pl.program_id(0)

        @pl.when(n == 0)
        def _():
            _swa_bias_into(bias_ref, bkt_ref, tab_ref)
            dbias_ref[...] = jnp.zeros_like(dbias_ref)
            dk_ref[...] = jnp.zeros_like(dk_ref)
            dv_ref[...] = jnp.zeros_like(dv_ref)
            dsink_ref[...] = jnp.zeros_like(dsink_ref)
            dtab_ref[...] = jnp.zeros_like(dtab_ref)

        valid = _swa_valid(n)
        cur_rows = pl.ds(pl.multiple_of(n * blk, blk), blk)
        prev_rows = pl.ds(pl.multiple_of(jnp.maximum(n - 1, 0) * blk, blk), blk)
        kk, vv = _swa_windows(kp_ref, kc_ref, vp_ref, vc_ref)
        st = {}

        def s_logits(h):
            hl = slice(h * LANES, (h + 1) * LANES)
            st[h, "lg"] = _dot_nt(q_ref[:, hl], kk[h // SWA_GROUP])
            st[h, "dp"] = _dot_nt(do_ref[:, hl], vv[h // SWA_GROUP])

        def s_probs(h):
            p, ps = _swa_probs(st.pop((h, "lg")), bias_ref[h], sink_ref[0, h], valid)
            dp = st.pop((h, "dp"))
            delta = jnp.sum(p * dp, axis=-1, keepdims=True)
            dl = p * (dp - delta)
            dsink_ref[h:h + 1, :] += jnp.broadcast_to(-jnp.sum(ps * delta, axis=0, keepdims=True), (1, LANES))
            dbias_ref[h] += dl
            st[h, "dl"], st[h, "p"] = dl.astype(BF16), p.astype(BF16)

        def s_products(h):
            hl = slice(h * LANES, (h + 1) * LANES)
            gl = slice(h // SWA_GROUP * LANES, (h // SWA_GROUP + 1) * LANES)
            dlb = st.pop((h, "dl"))
            dq_ref[:, hl] = (Q_SCALE * _dot(dlb, kk[h // SWA_GROUP])).astype(BF16)
            dk_win = Q_SCALE * _dot_tn(dlb, q_ref[:, hl])
            dv_win = _dot_tn(st.pop((h, "p")), do_ref[:, hl])
            dk_ref[prev_rows, gl] += dk_win[:blk]
            dv_ref[prev_rows, gl] += dv_win[:blk]
            dk_ref[cur_rows, gl] += dk_win[blk:]
            dv_ref[cur_rows, gl] += dv_win[blk:]

        _emit_skewed((list(range(N_HEADS)), [s_logits, s_probs, s_products]))

        @pl.when(n == nb - 1)
        def _():
            bk = bkt_ref[...]
            lane = lax.broadcasted_iota(jnp.int32, (1, LANES), 1)
            for bucket in range(REL_BUCKETS):
                rowv = jnp.zeros((1, LANES), F32)
                for h in range(N_HEADS):
                    val = jnp.sum(jnp.where(bk == bucket, dbias_ref[h], 0.0), axis=1, keepdims=True)
                    val = jnp.sum(val, axis=0, keepdims=True)
                    rowv = jnp.where(lane == h, val, rowv)
                dtab_ref[bucket:bucket + 1, :] = rowv

    return _call(
        body, (tab, sinks, bkt, q, k, k, v, v, do), comm=comm, **_grid_ends(nb), name="swa_bwd",
        grid=(nb,),
        in_specs=_swa_specs(s_len) + [pl.BlockSpec((blk, N_HEADS * LANES), lambda n: (n, 0))],
        out_specs=[pl.BlockSpec((blk, N_HEADS * LANES), lambda n: (n, 0)),
                   pl.BlockSpec((s_len, kvw), lambda n: (0, 0)), pl.BlockSpec((s_len, kvw), lambda n: (0, 0)),
                   pl.BlockSpec((REL_BUCKETS, LANES), lambda n: (0, 0)), pl.BlockSpec((N_HEADS, LANES), lambda n: (0, 0))],
        out_shape=[jax.ShapeDtypeStruct((s_len, N_HEADS * LANES), BF16),
                   jax.ShapeDtypeStruct((s_len, kvw), F32), jax.ShapeDtypeStruct((s_len, kvw), F32),
                   jax.ShapeDtypeStruct((REL_BUCKETS, LANES), F32), jax.ShapeDtypeStruct((N_HEADS, LANES), F32)],
        scratch_shapes=[pltpu.VMEM((N_HEADS, blk, 2 * blk), F32), pltpu.VMEM((N_HEADS, blk, 2 * blk), F32)],
        compiler_params=_cparams("arbitrary"),
    )


def _sb_terms(z, valid):
    zc = jnp.minimum(z, SB_LOGIT_CAP)
    lk = -jnp.log(1.0 + jnp.exp(zc))
    lsz = zc + lk
    return lsz, (lk if valid is None else jnp.where(valid, lk, 0.0))


def _bf16_parts(vals):
    parts, rest = [], vals
    for n in range(SB_SUM_PARTS):
        parts.append(rest.astype(BF16))
        if n + 1 < SB_SUM_PARTS:
            rest = rest - parts[-1].astype(F32)
    return parts[0] if len(parts) == 1 else jnp.concatenate(parts, axis=1)


def _row_sum_lanes(vals):
    return jnp.broadcast_to(jnp.sum(vals, axis=-1, keepdims=True), (vals.shape[0], LANES))


def _emit_skewed(*groups):
    for step in range(max(len(items) + len(stages) - 1 for items, stages in groups)):
        for items, stages in groups:
            for s, stage in enumerate(stages):
                if 0 <= step - s < len(items):
                    stage(items[step - s])


def _sb_items(edge):
    items = []
    for h in range(2):
        for r0 in range(0, SB_QUERIES, SB_ROWS):
            if edge is None or r0 >= (edge + 1) * SB_KEYS:
                items.append((h, r0, False))
            elif r0 + SB_ROWS - 1 > edge * SB_KEYS:
                items.append((h, r0, True))
    return items


def _sb_valid(w, edge):
    row = lax.broadcasted_iota(jnp.int32, (SB_ROWS, SB_KEYS), 0) + w[1]
    col = lax.broadcasted_iota(jnp.int32, (SB_ROWS, SB_KEYS), 1) + edge * SB_KEYS
    return col < row


def _sb_consts(tq, tk):
    low = lax.broadcasted_iota(jnp.int32, (tq, LANES), 1) < HEAD_DIM
    row = lax.broadcasted_iota(jnp.int32, (tk, tk), 0)
    col = lax.broadcasted_iota(jnp.int32, (tk, tk), 1)
    right = (row > col).astype(BF16)
    left = (row < col).astype(BF16)
    return low, jnp.concatenate([right] * SB_SUM_PARTS, axis=0), jnp.concatenate([left] * SB_SUM_PARTS, axis=0)


def _sb_fwd(q, kt, v, comm=None):
    s_len = q.shape[0]
    tq, tk, tr = SB_QUERIES, SB_KEYS, SB_ROWS
    nk, ratio = s_len // tk, tq // tk
    assert nk <= LANES

    def body(q_ref, kt_ref, v_ref, o_ref, car_ref, c_ref, oacc_ref, logw_ref, lksum_ref):
        i = pl.program_id(1)
        qv = q_ref[...]
        low, tri2, _ = _sb_consts(tq, tk)
        lane = lax.broadcasted_iota(jnp.int32, (tr, LANES), 1)
        zero = jnp.zeros_like(qv)
        q_heads = (jnp.where(low, qv, zero), jnp.where(low, zero, qv))
        c_ref[...] = jnp.zeros_like(c_ref)
        oacc_ref[...] = jnp.zeros_like(oacc_ref)
        car_ref[...] = jnp.full_like(car_ref, NEG_BIG)

        def front(j, edge):
            ktv = kt_ref[0, j]
            slot = j % 2
            st = {}

            def s_logits(w):
                st[w, "z"] = _dot(q_heads[w[0]][w[1]:w[1] + tr], ktv)

            def s_terms(w):
                valid = _sb_valid(w, edge) if w[2] else None
                lsz, lk = _sb_terms(st.pop((w, "z")), valid)
                st[w, "parts"] = _bf16_parts(lk)
                st[w, "lsz"] = lsz if valid is None else jnp.where(valid, lsz, NEG_BIG)
                lksum_ref[slot, w[0], w[1]:w[1] + tr, :] = _row_sum_lanes(lk)

            def s_suffix(w):
                logw_ref[slot, w[0], w[1]:w[1] + tr, :] = st.pop((w, "lsz")) + _dot(st.pop((w, "parts")), tri2)

            return _sb_items(edge), [s_logits, s_terms, s_suffix]

        def back(j, edge):
            vv = v_ref[pl.ds(pl.multiple_of(j * tk, tk), tk), :]
            slot = j % 2
            st = {}

            def s_weights(w):
                h, rs = w[0], slice(w[1], w[1] + tr)
                c = c_ref[h, rs, :]
                st[w, "a"] = jnp.exp(logw_ref[slot, h, rs, :] + jnp.tile(c, (1, tk // LANES))).astype(BF16)
                car_ref[h, rs, :] = jnp.where(lane == j, c, car_ref[h, rs, :])
                c_ref[h, rs, :] = c + lksum_ref[slot, h, rs, :]

            def s_values(w):
                oacc_ref[w[0], w[1]:w[1] + tr, :] += _dot(st.pop((w, "a")), vv)

            return _sb_items(edge), [s_weights, s_values]

        first = i * ratio
        _emit_skewed(front(first + ratio - 1, ratio - 1))
        for m in reversed(range(ratio - 1)):
            _emit_skewed(front(first + m, m), back(first + m + 1, m + 1))

        @pl.when(i == 0)
        def _():
            _emit_skewed(back(0, 0))

        def alive():
            return (jnp.max(c_ref[...]) >= SB_DEAD_CARRY).astype(jnp.int32)

        @pl.when(i > 0)
        def _():
            _emit_skewed(front(first - 1, None), back(first, 0))

            def step(state):
                pending, _ = state
                _emit_skewed(front(pending - 1, None), back(pending, None))
                return pending - 1, alive()

            pending, live = lax.while_loop(lambda s: (s[0] > 0) & (s[1] > 0), step, (first - 1, alive()))

            @pl.when(live > 0)
            def _():
                _emit_skewed(back(pending, None))

        o_ref[...] = jnp.where(low, oacc_ref[0], oacc_ref[1]).astype(BF16)

    return _call(
        body, (q, kt, v), comm=comm, **_grid_ends(N_HEADS // 2, s_len // tq), name="sb_fwd",
        grid=(N_HEADS // 2, s_len // tq),
        in_specs=[pl.BlockSpec((tq, LANES), lambda p, i: (i, p)),
                  pl.BlockSpec((1, nk, LANES, tk), lambda p, i: (p, 0, 0, 0)),
                  pl.BlockSpec((s_len, LANES), lambda p, i: (0, p))],
        out_specs=[pl.BlockSpec((tq, LANES), lambda p, i: (i, p)), pl.BlockSpec((2, tq, LANES), lambda p, i: (p, i, 0))],
        out_shape=[jax.ShapeDtypeStruct((s_len, N_HEADS * HEAD_DIM), BF16),
                   jax.ShapeDtypeStruct((N_HEADS, s_len, LANES), F32)],
        scratch_shapes=[pltpu.VMEM((2, tq, LANES), F32), pltpu.VMEM((2, tq, LANES), F32),
                        pltpu.VMEM((2, 2, tq, tk), F32), pltpu.VMEM((2, 2, tq, LANES), F32)],
        compiler_params=_cparams("arbitrary", "arbitrary"),
    )


def _sb_bwd(q, qt, kt, k, vt, do, dot, cars):
    s_len = q.shape[0]
    tq, tk, tr = SB_QUERIES, SB_KEYS, SB_ROWS
    nk, ratio = s_len // tk, tq // tk

    def body(q_ref, qt_ref, kt_ref, k_ref, vt_ref, do_ref, dot_ref, car_ref, dq_ref, dk_ref, dv_ref,
             gleft_ref, dqacc_ref, dkacc_ref, dvacc_ref, logw_ref, lsz_ref, da_ref, a_ref, dz_ref):
        i = pl.program_id(1)

        @pl.when(i == 0)
        def _():
            dkacc_ref[...] = jnp.zeros_like(dkacc_ref)
            dvacc_ref[...] = jnp.zeros_like(dvacc_ref)

        qv = q_ref[...]
        dov = do_ref[...]
        low, tri_right2, tri_left2 = _sb_consts(tq, tk)
        lane = lax.broadcasted_iota(jnp.int32, (tr, LANES), 1)
        zero = jnp.zeros_like(qv)
        q_heads = (jnp.where(low, qv, zero), jnp.where(low, zero, qv))
        do_heads = (jnp.where(low, dov, zero), jnp.where(low, zero, dov))
        q_t = qt_ref[0, 0]
        do_t = dot_ref[0, 0]
        gleft_ref[...] = jnp.zeros_like(gleft_ref)
        dqacc_ref[...] = jnp.zeros_like(dqacc_ref)

        def front(j, edge):
            ktv = kt_ref[0, j]
            vtv = vt_ref[0, j]
            slot = j % 2
            st = {}

            def s_logits(w):
                h, rs = w[0], slice(w[1], w[1] + tr)
                st[w, "z"] = _dot(q_heads[h][rs], ktv)
                da_ref[slot, h, rs, :] = _dot(do_heads[h][rs], vtv)

            def s_terms(w):
                h, rs = w[0], slice(w[1], w[1] + tr)
                valid = _sb_valid(w, edge) if w[2] else None
                lsz, lk = _sb_terms(st.pop((w, "z")), valid)
                st[w, "parts"] = _bf16_parts(lk)
                lsz = lsz if valid is None else jnp.where(valid, lsz, NEG_BIG)
                lsz_ref[slot, h, rs, :] = lsz
                st[w, "lszc"] = lsz + jnp.sum(jnp.where(lane == j, car_ref[h, rs, :], 0.0), axis=-1, keepdims=True)

            def s_suffix(w):
                logw_ref[slot, w[0], w[1]:w[1] + tr, :] = st.pop((w, "lszc")) + _dot(st.pop((w, "parts")), tri_right2)

            return _sb_items(edge), [s_logits, s_terms, s_suffix]

        def back(j, edge):
            kv = k_ref[pl.ds(pl.multiple_of(j * tk, tk), tk), :]
            slot = j % 2
            st = {}

            items = _sb_items(edge)
            head_rows = [[r0 for hh, r0, _ in items if hh == h] for h in range(2)]

            def s_weights(w):
                h, rs = w[0], slice(w[1], w[1] + tr)
                a = jnp.exp(logw_ref[slot, h, rs, :])
                g = a * da_ref[slot, h, rs, :]
                a_ref[h, rs, :] = a.astype(BF16)
                st[w, "g"], st[w, "parts"] = g, _bf16_parts(g)

            def s_prefix(w):
                st[w, "gs"] = _dot(st.pop((w, "parts")), tri_left2)

            def s_dz(w):
                h, rs = w[0], slice(w[1], w[1] + tr)
                g = st.pop((w, "g"))
                gleft = gleft_ref[h, rs, :]
                gsum = st.pop((w, "gs")) + jnp.tile(gleft, (1, tk // LANES))
                dz = (g - jnp.exp(lsz_ref[slot, h, rs, :]) * (g + gsum)).astype(BF16)
                st[w, "dz"] = dz
                dz_ref[h, rs, :] = dz
                gleft_ref[h, rs, :] = gleft + _row_sum_lanes(g)

            def s_products(w):
                h, rs = w[0], slice(w[1], w[1] + tr)
                dqacc_ref[h, rs, :] += _dot(st.pop((w, "dz")), kv)
                if w[1] == head_rows[h][-1]:
                    feat = slice(h * HEAD_DIM, (h + 1) * HEAD_DIM)
                    hr = slice(head_rows[h][0], tq)
                    dkacc_ref[j, feat, :] += _dot(q_t[feat, hr], dz_ref[h, hr, :])
                    dvacc_ref[j, feat, :] += _dot(do_t[feat, hr], a_ref[h, hr, :])

            return items, [s_weights, s_prefix, s_dz, s_products]

        first = i * ratio
        tile_max = jnp.max(jnp.maximum(car_ref[0], car_ref[1]), axis=0, keepdims=True)
        start = jnp.clip(first + ratio - jnp.sum(jnp.where(tile_max >= SB_DEAD_CARRY, 1, 0)), 0, first)

        @pl.when(start == first)
        def _():
            _emit_skewed(front(first, 0))

        @pl.when(start < first)
        def _():
            _emit_skewed(front(start, None))

            def step(jj, carry):
                _emit_skewed(front(jj, None), back(jj - 1, None))
                return carry

            lax.fori_loop(start + 1, first, step, 0)
            _emit_skewed(front(first, 0), back(first - 1, None))

        for m in range(1, ratio):
            _emit_skewed(front(first + m, m), back(first + m - 1, m - 1))
        _emit_skewed(back(first + ratio - 1, ratio - 1))
        dq_ref[...] = (Q_SCALE * jnp.where(low, dqacc_ref[0], dqacc_ref[1])).astype(BF16)

        @pl.when(i == s_len // tq - 1)
        def _():
            dk_ref[0] = dkacc_ref[...].astype(BF16)
            dv_ref[0] = dvacc_ref[...].astype(BF16)

    qblk = pl.BlockSpec((tq, LANES), lambda p, i: (i, p))
    qtblk = pl.BlockSpec((1, 1, LANES, tq), lambda p, i: (p, i, 0, 0))
    tblk = pl.BlockSpec((1, nk, LANES, tk), lambda p, i: (p, 0, 0, 0))
    col_full = pl.BlockSpec((s_len, LANES), lambda p, i: (0, p))
    tshape = jax.ShapeDtypeStruct((N_HEADS // 2, nk, LANES, tk), BF16)
    return pl.pallas_call(
        body, name="sb_bwd",
        grid=(N_HEADS // 2, s_len // tq),
        in_specs=[qblk, qtblk, tblk, col_full, tblk, qblk, qtblk, pl.BlockSpec((2, tq, LANES), lambda p, i: (p, i, 0))],
        out_specs=[qblk, tblk, tblk],
        out_shape=[jax.ShapeDtypeStruct((s_len, N_HEADS * HEAD_DIM), BF16), tshape, tshape],
        scratch_shapes=[pltpu.VMEM((2, tq, LANES), F32), pltpu.VMEM((2, tq, LANES), F32),
                        pltpu.VMEM((nk, LANES, tk), F32), pltpu.VMEM((nk, LANES, tk), F32)]
        + [pltpu.VMEM((2, 2, tq, tk), F32)] * 3 + [pltpu.VMEM((2, tq, tk), BF16)] * 2,
        compiler_params=_cparams("parallel", "arbitrary"),
    )(q, qt, kt, k, vt, do, dot, cars)


def _pad_heads(a, heads):
    s_len = a.shape[0]
    a = a.reshape(s_len, heads, HEAD_DIM)
    return jnp.pad(a, ((0, 0), (0, 0), (0, LANES - HEAD_DIM))).reshape(s_len, heads * LANES)


def _unpad_heads(a, heads):
    s_len = a.shape[0]
    return a.reshape(s_len, heads, LANES)[:, :, :HEAD_DIM].reshape(s_len, heads * HEAD_DIM)


def _tile_transposed(a, groups, t):
    s_len = a.shape[0]
    return a.reshape(s_len // t, t, groups, LANES).transpose(2, 0, 3, 1)


def _tile_untransposed(a):
    groups, nt, _, t = a.shape
    return a.transpose(1, 3, 0, 2).reshape(nt * t, groups * LANES)


def _local_step(xs, tgt, gains, sinks, rel_bias, weights_of, ship):
    g1, gmix, g2, gfin = gains
    bkt = _rel_bucket_matrix()
    groups = N_HEADS // 2
    grads = {}

    def carried(outs, comm, count):
        return outs[:count], (list(outs[count:]) if comm is not None else None)

    wts = dict(weights_of(0, None))
    comm = ship("weights", 1)
    (x1, h1, a1, b1, u1), landed = carried(
        _ffn_fwd(xs, g1, wts["ffn1_w1t"], wts["ffn1_w3t"], wts["ffn1_w2"], "1", comm), comm, 5)
    wts.update(weights_of(1, landed))
    hm, qa, ka, va, qb, kb, vb, ga, gb = _proj_fwd(x1, gmix, wts["w_int"])
    qa_p, ka_p, va_p = _pad_heads(qa, N_HEADS), _pad_heads(ka, SWA_KV_HEADS), _pad_heads(va, SWA_KV_HEADS)
    oa_p = _swa_fwd(rel_bias, sinks, bkt, qa_p, ka_p, va_p)
    kbt = _tile_transposed(kb, groups, SB_KEYS)
    comm = ship("weights", 2)
    (ob, cars), landed = carried(_sb_fwd(qb, kbt, vb, comm), comm, 2)
    wts.update(weights_of(2, landed))
    oa = _unpad_heads(oa_p, N_HEADS)
    x2, mg = _merge_fwd(x1, oa, ob, ga, gb, wts["w_swa"], wts["w_sb"], wts["w_out"])
    x3, h3, a3, b3, u3 = _ffn_fwd(x2, g2, wts["ffn2_w1t"], wts["ffn2_w3t"], wts["ffn2_w2"], "2")
    dx3, loss, dgfin = _loss_fwd_bwd(x3, tgt, gfin)

    dx2, dg2, da3, db3, dx3b = _ffn_bwd(dx3, x2, g2, a3, b3, wts["ffn2_w1t"], wts["ffn2_w3t"], wts["ffn2_w2"], "2")
    big = {"ffn2_w1t": _matmul_tn(da3, h3, "ffn2_w1"), "ffn2_w3t": _matmul_tn(db3, h3, "ffn2_w3"),
           "ffn2_w2": _matmul_tn(u3, dx3b, "ffn2_w2")}

    doa, dob, dga, dgb, dpa, dpb, dx2b = _merge_bwd(dx2, oa, ob, ga, gb, wts["w_swa"], wts["w_sb"], wts["w_out"])
    comm = ship("grads", GROUPS[2], big)
    (dqa_p, dka_p, dva_p, dtab, dsink), landed = carried(
        _swa_bwd(rel_bias, sinks, bkt, qa_p, ka_p, va_p, _pad_heads(doa, N_HEADS), comm), comm, 5)
    grads[GROUPS[2]] = big if comm is None else landed[0]

    big = {"w_out": _matmul_tn(mg, dx2b, "w_out"), "w_swa": _matmul_tn(oa, dpa, "w_swa"),
           "w_sb": _matmul_tn(ob, dpb, "w_sb")}
    dqb, dkbt, dvbt = _sb_bwd(qb, _tile_transposed(qb, groups, SB_QUERIES), kbt, kb,
                              _tile_transposed(vb, groups, SB_KEYS), dob, _tile_transposed(dob, groups, SB_QUERIES), cars)
    dkb, dvb = _tile_untransposed(dkbt), _tile_untransposed(dvbt)
    dpieces = (_unpad_heads(dqa_p, N_HEADS), _unpad_heads(dka_p, SWA_KV_HEADS).astype(BF16),
               _unpad_heads(dva_p, SWA_KV_HEADS).astype(BF16), dqb, dkb, dvb, dga, dgb)
    big["w_int"] = jnp.concatenate([_matmul_tn(dp, hm, f"w_in{p}") for p, dp in enumerate(dpieces)], axis=0)
    dx1, dgmix = _proj_bwd(dpieces, dx2, x1, gmix, wts["w_int"])

    comm = ship("grads", GROUPS[1], big)
    (dx0, dg1, da1, db1, dx1b), landed = carried(
        _ffn_bwd(dx1, xs, g1, a1, b1, wts["ffn1_w1t"], wts["ffn1_w3t"], wts["ffn1_w2"], "1", comm), comm, 5)
    grads[GROUPS[1]] = big if comm is None else landed[0]

    prev = None
    for name, lhs, rhs in (("ffn1_w1", da1, h1), ("ffn1_w3", db1, h1), ("ffn1_w2", u1, dx1b)):
        comm = None if prev is None else ship("grads", (prev[0],), prev[1])
        res = _matmul_tn(lhs, rhs, name, comm)
        if prev is not None:
            grads[(prev[0],)] = prev[1] if comm is None else res[1]
        prev = (name, {_GRAD_KEY[name]: res if comm is None else res[0]})
    grads[(prev[0],)] = prev[1]

    small = {"gains": (dg1, dgmix, dg2, dgfin), "sinks": dsink[:, 0], "rel_bias": dtab[:, :N_HEADS]}
    return loss, dx0, small, grads


def _my_place():
    return lax.axis_index("x"), lax.axis_index("y"), lax.axis_index("c")


def _flip(v, bit):
    return 1 - v if bit else v


_RELATIONS = tuple((k >> 2 & 1, k >> 1 & 1, k & 1) for k in range(1, N_DEV))


def _gather_weights(blocks, tag):
    count = len(blocks)

    def body(*refs):
        x_refs, out_refs = refs[:count], refs[count:2 * count]
        send_sems, recv_sems, local_sems = refs[2 * count:]
        x, y, c = _my_place()
        me, sibling = (x, y, c), (x, y, 1 - c)
        chips = [(1 - x, y), (x, 1 - y), (1 - x, 1 - y)]

        def rows(s, px, py, pc):
            return out_refs[s].at[4 * px + 2 * py + pc]

        def copy(s, k, block, to, src=None):
            return pltpu.make_async_remote_copy(
                src_ref=rows(s, *block) if src is None else src, dst_ref=rows(s, *block),
                send_sem=send_sems.at[s, k], recv_sem=recv_sems.at[s, k],
                device_id=to, device_id_type=pl.DeviceIdType.MESH)

        mine = [pltpu.make_async_copy(x_refs[s], rows(s, *me), local_sems.at[s]) for s in range(count)]
        first, passed = [], []
        for s in range(count):
            mine[s].start()
            first.append(copy(s, 0, me, sibling, src=x_refs[s]))
            first += [copy(s, 1 + j, me, (*chip, c), src=x_refs[s]) for j, chip in enumerate(chips)]
        for cp in first:
            cp.start()
        for s in range(count):
            for j, chip in enumerate(chips):
                copy(s, 1 + j, (*chip, c), me).wait_recv()
                passed.append(copy(s, 4 + j, (*chip, c), sibling))
                passed[-1].start()
        for s in range(count):
            copy(s, 0, sibling, me).wait_recv()
            for j, chip in enumerate(chips):
                copy(s, 4 + j, (*chip, 1 - c), me).wait_recv()
        for cp in first + passed:
            cp.wait_send()
        for cp in mine:
            cp.wait()

    anywhere = pl.BlockSpec(memory_space=pl.ANY)
    return pl.pallas_call(
        body, name=f"gather_weights_{tag}",
        out_shape=[jax.ShapeDtypeStruct((N_DEV,) + b.shape, b.dtype) for b in blocks],
        in_specs=[anywhere] * count, out_specs=[anywhere] * count,
        scratch_shapes=[pltpu.SemaphoreType.DMA((count, N_DEV - 1)), pltpu.SemaphoreType.DMA((count, N_DEV - 1)),
                        pltpu.SemaphoreType.DMA((count,))],
    )(*blocks)


def _exchange_grads(gp, tag):
    def body(g_ref, out_ref, send_sems, recv_sems, local_sem):
        x, y, c = _my_place()
        me = 4 * x + 2 * y + c
        mine = pltpu.make_async_copy(g_ref.at[me], out_ref.at[me], local_sem)
        mine.start()
        copies = []
        for k, (fx, fy, fc) in enumerate(_RELATIONS):
            px, py, pc = _flip(x, fx), _flip(y, fy), _flip(c, fc)
            peer = 4 * px + 2 * py + pc
            copies.append((
                pltpu.make_async_remote_copy(
                    src_ref=g_ref.at[peer], dst_ref=out_ref.at[me], send_sem=send_sems.at[k], recv_sem=recv_sems.at[k],
                    device_id=(px, py, pc), device_id_type=pl.DeviceIdType.MESH),
                pltpu.make_async_remote_copy(
                    src_ref=g_ref.at[peer], dst_ref=out_ref.at[peer], send_sem=send_sems.at[k], recv_sem=recv_sems.at[k],
                    device_id=(px, py, pc), device_id_type=pl.DeviceIdType.MESH)))
        for out_cp, _ in copies:
            out_cp.start()
        for _, in_cp in copies:
            in_cp.wait_recv()
        for out_cp, _ in copies:
            out_cp.wait_send()
        mine.wait()

    return pl.pallas_call(
        body, name=f"exchange_grads_{tag}",
        out_shape=jax.ShapeDtypeStruct(gp.shape, gp.dtype),
        in_specs=[pl.BlockSpec(memory_space=pl.ANY)],
        out_specs=pl.BlockSpec(memory_space=pl.ANY),
        scratch_shapes=[pltpu.SemaphoreType.DMA((7,)), pltpu.SemaphoreType.DMA((7,)), pltpu.SemaphoreType.DMA(())],
    )(gp)


def _peers():
    x, y, c = _my_place()
    out = []
    for k, (fx, fy, fc) in enumerate(_RELATIONS):
        px, py, pc = _flip(x, fx), _flip(y, fy), _flip(c, fc)
        out.append((k, (px, py, pc), 4 * px + 2 * py + pc))
    return out, 4 * x + 2 * y + c


def _grid_ends(*grid):
    def first():
        return functools.reduce(lambda a, b: a & b, [pl.program_id(d) == 0 for d in range(len(grid))])

    def last():
        return functools.reduce(lambda a, b: a & b, [pl.program_id(d) == n - 1 for d, n in enumerate(grid)])

    return {"first": first, "last": last}


def _call(body, operands, *, comm=None, first=None, last=None, **kw):
    if comm is None:
        return pl.pallas_call(body, **kw)(*operands)
    in_specs, out_specs, out_shape = list(kw.pop("in_specs")), list(kw.pop("out_specs")), list(kw.pop("out_shape"))
    scratch = list(kw.pop("scratch_shapes", ()))
    n_in, n_out, n_scr, n_src = len(in_specs), len(out_specs), len(scratch), len(comm)

    def wrapped(*refs):
        ins, src_refs = refs[:n_in], refs[n_in:n_in + n_src]
        outs = refs[n_in + n_src:n_in + n_src + n_out]
        land_refs = refs[n_in + n_src + n_out:n_in + 2 * n_src + n_out]
        scr = refs[n_in + 2 * n_src + n_out:n_in + 2 * n_src + n_out + n_scr]
        send_sems, recv_sems, local_sems = refs[n_in + 2 * n_src + n_out + n_scr:]
        peers, me = _peers()
        mine, going, coming = [], [], []
        for s, (_, per_peer) in enumerate(comm):
            src_ref, land_ref = src_refs[s], land_refs[s]
            mine.append(pltpu.make_async_copy(src_ref.at[me] if per_peer else src_ref, land_ref.at[me], local_sems.at[s]))
            for k, where, slab in peers:
                piece = src_ref.at[slab] if per_peer else src_ref
                going.append(pltpu.make_async_remote_copy(
                    src_ref=piece, dst_ref=land_ref.at[me], send_sem=send_sems.at[s, k], recv_sem=recv_sems.at[s, k],
                    device_id=where, device_id_type=pl.DeviceIdType.MESH))
                coming.append(pltpu.make_async_remote_copy(
                    src_ref=piece, dst_ref=land_ref.at[slab], send_sem=send_sems.at[s, k], recv_sem=recv_sems.at[s, k],
                    device_id=where, device_id_type=pl.DeviceIdType.MESH))

        @pl.when(first())
        def _():
            for cp in mine + going:
                cp.start()

        body(*ins, *outs, *scr)

        @pl.when(last())
        def _():
            for cp in coming:
                cp.wait_recv()
            for cp in going:
                cp.wait_send()
            for cp in mine:
                cp.wait()

    anywhere = pl.BlockSpec(memory_space=pl.ANY)
    lands = [jax.ShapeDtypeStruct(src.shape if per_peer else (N_DEV,) + src.shape, src.dtype) for src, per_peer in comm]
    return pl.pallas_call(
        wrapped, in_specs=in_specs + [anywhere] * n_src, out_specs=out_specs + [anywhere] * n_src,
        out_shape=out_shape + lands,
        scratch_shapes=scratch + [pltpu.SemaphoreType.DMA((n_src, N_DEV - 1)), pltpu.SemaphoreType.DMA((n_src, N_DEV - 1)),
                                  pltpu.SemaphoreType.DMA((n_src,))],
        **kw)(*operands, *[src for src, _ in comm])


def _adamw(w, g, m, v):
    m = ADAM_B1 * m + (1.0 - ADAM_B1) * g
    v = ADAM_B2 * v + (1.0 - ADAM_B2) * jnp.square(g)
    m_hat = m / (1.0 - ADAM_B1 ** ADAM_STEP)
    v_hat = v / (1.0 - ADAM_B2 ** ADAM_STEP)
    delta = -ADAM_LR * (m_hat / (jnp.sqrt(v_hat) + ADAM_EPS) + ADAM_WD * w)
    return delta, m, v


def _sum_and_adamw(parts, w, m, v, tr, tag):
    rows = w.shape[0]
    assert rows % tr == 0

    def body(p_ref, w_ref, m_ref, v_ref, g_out, d_out, m_out, v_out):
        g = p_ref[0].astype(F32)
        for d in range(1, N_DEV):
            g = g + p_ref[d].astype(F32)
        delta, mn, vn = _adamw(w_ref[...], g, m_ref[...], v_ref[...])
        g_out[...] = g
        d_out[...] = delta
        m_out[...] = mn
        v_out[...] = vn

    sp = pl.BlockSpec((tr, D_MODEL), lambda i: (i, 0))
    return pl.pallas_call(
        body, name=f"sum_and_adamw_{tag}",
        grid=(rows // tr,),
        in_specs=[pl.BlockSpec((N_DEV, tr, D_MODEL), lambda i: (0, i, 0)), sp, sp, sp],
        out_specs=[sp] * 4,
        out_shape=[jax.ShapeDtypeStruct(w.shape, F32)] * 4,
        compiler_params=_cparams("parallel"),
    )(parts, w, m, v)


def _small_allreduce_adamw(part, w, m, v):
    def body(p_ref, w_ref, m_ref, v_ref, g_out, d_out, m_out, v_out, buf, send_sems, recv_sems):
        x, y, c = _my_place()
        me = 4 * x + 2 * y + c
        buf[me] = p_ref[...]
        copies = []
        for k, (fx, fy, fc) in enumerate(_RELATIONS):
            px, py, pc = _flip(x, fx), _flip(y, fy), _flip(c, fc)
            peer = 4 * px + 2 * py + pc
            copies.append((
                pltpu.make_async_remote_copy(
                    src_ref=buf.at[me], dst_ref=buf.at[me], send_sem=send_sems.at[k], recv_sem=recv_sems.at[k],
                    device_id=(px, py, pc), device_id_type=pl.DeviceIdType.MESH),
                pltpu.make_async_remote_copy(
                    src_ref=buf.at[me], dst_ref=buf.at[peer], send_sem=send_sems.at[k], recv_sem=recv_sems.at[k],
                    device_id=(px, py, pc), device_id_type=pl.DeviceIdType.MESH)))
        for out_cp, _ in copies:
            out_cp.start()
        for _, in_cp in copies:
            in_cp.wait_recv()
        for out_cp, _ in copies:
            out_cp.wait_send()
        g = buf[0]
        for d in range(1, N_DEV):
            g = g + buf[d]
        delta, mn, vn = _adamw(w_ref[...], g, m_ref[...], v_ref[...])
        g_out[...] = g
        d_out[...] = delta
        m_out[...] = mn
        v_out[...] = vn

    vm = pl.BlockSpec(memory_space=pltpu.VMEM)
    return pl.pallas_call(
        body, name="small_allreduce_adamw",
        in_specs=[vm] * 4, out_specs=[vm] * 4,
        out_shape=[jax.ShapeDtypeStruct(w.shape, F32)] * 4,
        scratch_shapes=[pltpu.VMEM((N_DEV,) + part.shape, F32),
                        pltpu.SemaphoreType.DMA((7,)), pltpu.SemaphoreType.DMA((7,))],
    )(part, w, m, v)


_TRANSPOSED = ("ffn1_w1", "ffn1_w3", "w_in", "ffn2_w1", "ffn2_w3")
_BRANCH = ("w_branch_swa", "w_branch_sb")


def _pack_shards(t, names):
    parts = []
    for name in names:
        a = t[name][0]
        if name in _TRANSPOSED:
            a = a.T
        elif name in _BRANCH:
            a = a.reshape(64, D_MODEL)
        parts.append(a)
    return jnp.concatenate(parts, axis=0)


def _unpack_shards(p, names):
    out, lo = {}, 0
    for name in names:
        a = p[lo:lo + BIG_ROWS[BIG_NAMES.index(name)]]
        lo += a.shape[0]
        if name in _TRANSPOSED:
            a = a.T
        elif name in _BRANCH:
            a = a.reshape(512, 128)
        out[name] = a[None]
    return out


def _full_weights(zones, names):
    out = {}
    for name, a in zip(names, zones):
        if name in _BRANCH:
            a = a.reshape(N_DEV, 512, 128).transpose(1, 0, 2).reshape(512, D_MODEL)
        out[_GRAD_KEY[name]] = a.reshape(-1, D_MODEL)
    return out


_GRAD_KEY = {"ffn1_w1": "ffn1_w1t", "ffn1_w3": "ffn1_w3t", "ffn1_w2": "ffn1_w2", "w_in": "w_int",
             "w_branch_swa": "w_swa", "w_branch_sb": "w_sb", "w_out": "w_out",
             "ffn2_w1": "ffn2_w1t", "ffn2_w3": "ffn2_w3t", "ffn2_w2": "ffn2_w2"}


def _pack_full_grads(big, names):
    parts = []
    for name in names:
        a = big[_GRAD_KEY[name]]
        if name in _BRANCH:
            a = a.reshape(512, N_DEV, 128).transpose(1, 0, 2)
        parts.append(a.reshape(N_DEV, BIG_ROWS[BIG_NAMES.index(name)], D_MODEL).astype(BF16))
    return jnp.concatenate(parts, axis=1)


_SMALL_NAMES = ("norm_ffn1", "norm_mix", "norm_ffn2", "norm_final", "swa_sinks", "rel_bias")


def _pack_small(vals):
    rows = []
    for a in vals:
        a = a.reshape(-1)
        rows.append(jnp.pad(a, (0, D_MODEL - a.shape[0])))
    rows += [jnp.zeros((D_MODEL,), F32)] * (SMALL_ROWS - len(rows))
    return jnp.stack(rows)


def _unpack_small(p):
    return {"norm_ffn1": p[0:1], "norm_mix": p[1:2], "norm_ffn2": p[2:3], "norm_final": p[3],
            "swa_sinks": p[4:5, :N_HEADS], "rel_bias": p[5, :REL_BUCKETS * N_HEADS].reshape(REL_BUCKETS, N_HEADS)}


ALL_NAMES = ("norm_ffn1", "ffn1_w1", "ffn1_w3", "ffn1_w2", "norm_mix", "w_in", "swa_sinks", "rel_bias",
             "w_branch_swa", "w_branch_sb", "w_out", "norm_ffn2", "ffn2_w1", "ffn2_w3", "ffn2_w2", "norm_final")


def kernel(x, norm_ffn1, ffn1_w1, ffn1_w3, ffn1_w2, norm_mix, w_in, swa_sinks, rel_bias, w_branch_swa, w_branch_sb, w_out, norm_ffn2, ffn2_w1, ffn2_w3, ffn2_w2, norm_final, loss_target, m_norm_ffn1, m_ffn1_w1, m_ffn1_w3, m_ffn1_w2, m_norm_mix, m_w_in, m_swa_sinks, m_rel_bias, m_w_branch_swa, m_w_branch_sb, m_w_out, m_norm_ffn2, m_ffn2_w1, m_ffn2_w3, m_ffn2_w2, m_norm_final, v_norm_ffn1, v_ffn1_w1, v_ffn1_w3, v_ffn1_w2, v_norm_mix, v_w_in, v_swa_sinks, v_rel_bias, v_w_branch_swa, v_w_branch_sb, v_w_out, v_norm_ffn2, v_ffn2_w1, v_ffn2_w3, v_ffn2_w2, v_norm_final):
    w = dict(zip(ALL_NAMES, (norm_ffn1, ffn1_w1, ffn1_w3, ffn1_w2, norm_mix, w_in, swa_sinks, rel_bias,
                             w_branch_swa, w_branch_sb, w_out, norm_ffn2, ffn2_w1, ffn2_w3, ffn2_w2, norm_final)))
    m = dict(zip(ALL_NAMES, (m_norm_ffn1, m_ffn1_w1, m_ffn1_w3, m_ffn1_w2, m_norm_mix, m_w_in, m_swa_sinks, m_rel_bias,
                             m_w_branch_swa, m_w_branch_sb, m_w_out, m_norm_ffn2, m_ffn2_w1, m_ffn2_w3, m_ffn2_w2,
                             m_norm_final)))
    v = dict(zip(ALL_NAMES, (v_norm_ffn1, v_ffn1_w1, v_ffn1_w3, v_ffn1_w2, v_norm_mix, v_w_in, v_swa_sinks, v_rel_bias,
                             v_w_branch_swa, v_w_branch_sb, v_w_out, v_norm_ffn2, v_ffn2_w1, v_ffn2_w3, v_ffn2_w2,
                             v_norm_final)))

    def my_blocks(group):
        return [_pack_shards(w, (name,)).astype(BF16) for name in GROUPS[group]]

    gathered0 = _gather_weights(my_blocks(0), "group0")

    def weights_of(group, landed):
        return _full_weights(gathered0 if group == 0 else landed, GROUPS[group])

    def ship(kind, which, grads=None):
        if kind == "weights":
            return [(block, False) for block in my_blocks(which)]
        return [(_pack_full_grads(grads, which), True)]

    gains = (norm_ffn1, norm_mix, norm_ffn2, norm_final.reshape(1, D_MODEL))
    loss, dx, small, parts = _local_step(x[0], loss_target[0], gains, swa_sinks, rel_bias, weights_of, ship)

    big_outs = [{}, {}, {}, {}]
    for names, tile in zip(SUM_GROUPS, SUM_TILE):
        landed = parts[names]
        if isinstance(landed, dict):
            landed = _exchange_grads(_pack_full_grads(landed, names), names[0])
        res = _sum_and_adamw(landed, _pack_shards(w, names), _pack_shards(m, names), _pack_shards(v, names),
                             tile, names[0])
        for acc, packed in zip(big_outs, res):
            acc.update(_unpack_shards(packed, names))
    g_big, d_big, m_big, v_big = big_outs

    small_part = _pack_small(small["gains"] + (small["sinks"], small["rel_bias"], loss))
    zero = jnp.zeros((1,), F32)
    small_res = _small_allreduce_adamw(
        small_part, _pack_small([w[n] for n in _SMALL_NAMES] + [zero]), _pack_small([m[n] for n in _SMALL_NAMES] + [zero]),
        _pack_small([v[n] for n in _SMALL_NAMES] + [zero]))
    g_sm, d_sm, m_sm, v_sm = (_unpack_small(p) for p in small_res)

    outs = [small_res[0][len(_SMALL_NAMES), 0], dx[None]]
    for big_d, small_d in ((g_big, g_sm), (d_big, d_sm), (m_big, m_sm), (v_big, v_sm)):
        merged = {**big_d, **small_d}
        outs += [merged[n] for n in ALL_NAMES]
    return tuple(outs)
```

```python
import functools

import jax
import jax.numpy as jnp
import numpy as np
from jax import lax
from jax.experimental import pallas as pl
from jax.experimental.pallas import tpu as pltpu

F32 = jnp.float32
BF16 = jnp.bfloat16

D_MODEL = 1024
D_FF = 2816
HEAD_DIM = 64
N_HEADS = 8
SWA_KV_HEADS = 2
SWA_GROUP = 4
SWA_BLOCK = 128
REL_BUCKETS = 32
REL_MAX_DIST = 128
RMS_EPS = 1e-6
NEG_BIG = -1e30
Q_SCALE = HEAD_DIM ** -0.5
LANES = 128

N_DEV = 8

ADAM_LR = 0.001
ADAM_B1 = 0.9
ADAM_B2 = 0.999
ADAM_EPS = 1e-08
ADAM_WD = 0.01
ADAM_STEP = 10

IN_SIZES = (512, 128, 128, 512, 512, 512, 1024, 1024)
IN_OFFS = tuple(int(v) for v in np.cumsum((0,) + IN_SIZES))
IN_W = IN_OFFS[-1]

BIG_NAMES = ("ffn1_w1", "ffn1_w3", "ffn1_w2", "w_in", "w_branch_swa", "w_branch_sb", "w_out",
             "ffn2_w1", "ffn2_w3", "ffn2_w2")
BIG_ROWS = (352, 352, 352, 544, 64, 64, 128, 352, 352, 352)
SMALL_ROWS = 8
GROUPS = (BIG_NAMES[0:3], BIG_NAMES[3:7], BIG_NAMES[7:10])
SUM_GROUPS = tuple((n,) for n in GROUPS[0]) + GROUPS[1:]
SUM_TILE = (176, 176, 176, 160, 96)

VMEM_LIMIT = 56 * 1024 * 1024
FFN_PIECES = 2
SB_QUERIES = 512
SB_KEYS = 256
SB_ROWS = 256
SB_SUM_PARTS = 1
SB_LOGIT_CAP = 80.0
SB_DEAD_CARRY = -110.0


def _dot(a, b):
    return jnp.dot(a, b, preferred_element_type=F32)


def _dot_nt(a, b):
    return lax.dot_general(a, b, (((1,), (1,)), ((), ())), preferred_element_type=F32)


def _dot_tn(a, b):
    return lax.dot_general(a, b, (((0,), (0,)), ((), ())), preferred_element_type=F32)


def _cparams(*sem):
    return pltpu.CompilerParams(dimension_semantics=sem, vmem_limit_bytes=VMEM_LIMIT)


def _rms_rstd(xv):
    return lax.rsqrt(jnp.mean(xv * xv, axis=-1, keepdims=True) + RMS_EPS)


def _rms_bwd(dh, xv, r, g):
    xhat = xv * r
    dg = jnp.sum(dh * xhat, axis=0, keepdims=True)
    dxn = dh * g
    dx = r * (dxn - xhat * jnp.mean(dxn * xhat, axis=-1, keepdims=True))
    return dx, dg


def _ffn_fwd(x, g, w1t, w3t, w2, tag, comm=None):
    s_len = x.shape[0]
    tm, tf = min(1024, s_len), 256
    nf = D_FF // tf

    def body(x_ref, g_ref, w1_ref, w3_ref, w2_ref, xo_ref, h_ref, a_ref, b_ref, u_ref, acc_ref, hs_ref):
        j = pl.program_id(1)

        @pl.when(j == 0)
        def _():
            xv = x_ref[...]
            h = (xv * _rms_rstd(xv) * g_ref[...]).astype(BF16)
            hs_ref[...] = h
            h_ref[...] = h
            acc_ref[...] = jnp.zeros_like(acc_ref)

        st = {}

        def s_up(rs):
            h = hs_ref[rs, :]
            st[rs.start, "ab"] = (_dot_nt(h, w1_ref[...]), _dot_nt(h, w3_ref[...]))

        def s_act(rs):
            a, b = st.pop((rs.start, "ab"))
            a_ref[rs, :] = a.astype(BF16)
            b_ref[rs, :] = b.astype(BF16)
            uh = (0.5 * (a * jax.nn.sigmoid(a) * b)).astype(BF16)
            u_ref[rs, :] = uh
            st[rs.start, "u"] = uh

        def s_down(rs):
            acc_ref[rs, :] += _dot(st.pop((rs.start, "u")), w2_ref[...])

        _emit_skewed(([slice(r, r + tm // FFN_PIECES) for r in range(0, tm, tm // FFN_PIECES)], [s_up, s_act, s_down]))

        @pl.when(j == nf - 1)
        def _():
            xo_ref[...] = x_ref[...] + acc_ref[...]

    row = lambda i, j: (i, 0)
    return _call(
        body, (x, g, w1t, w3t, w2), comm=comm, **_grid_ends(s_len // tm, nf), name=f"ffn_fwd_{tag}",
        grid=(s_len // tm, nf),
        in_specs=[pl.BlockSpec((tm, D_MODEL), row), pl.BlockSpec((1, D_MODEL), lambda i, j: (0, 0)),
                  pl.BlockSpec((tf, D_MODEL), lambda i, j: (j, 0)), pl.BlockSpec((tf, D_MODEL), lambda i, j: (j, 0)),
                  pl.BlockSpec((tf, D_MODEL), lambda i, j: (j, 0))],
        out_specs=[pl.BlockSpec((tm, D_MODEL), row), pl.BlockSpec((tm, D_MODEL), row),
                   pl.BlockSpec((tm, tf), lambda i, j: (i, j)), pl.BlockSpec((tm, tf), lambda i, j: (i, j)),
                   pl.BlockSpec((tm, tf), lambda i, j: (i, j))],
        out_shape=[jax.ShapeDtypeStruct((s_len, D_MODEL), F32), jax.ShapeDtypeStruct((s_len, D_MODEL), BF16),
                   jax.ShapeDtypeStruct((s_len, D_FF), BF16), jax.ShapeDtypeStruct((s_len, D_FF), BF16),
                   jax.ShapeDtypeStruct((s_len, D_FF), BF16)],
        scratch_shapes=[pltpu.VMEM((tm, D_MODEL), F32), pltpu.VMEM((tm, D_MODEL), BF16)],
        compiler_params=_cparams("arbitrary", "arbitrary"),
    )


def _ffn_bwd(dy, x, g, a, b, w1t, w3t, w2, tag, comm=None):
    s_len = x.shape[0]
    tm, tf = min(1024, s_len), 256
    nf = D_FF // tf

    def body(dy_ref, x_ref, g_ref, a_ref, b_ref, w1_ref, w3_ref, w2_ref,
             dx_ref, dg_ref, da_ref, db_ref, dyb_ref, acc_ref, dys_ref):
        i, j = pl.program_id(0), pl.program_id(1)

        @pl.when(j == 0)
        def _():
            dyb = dy_ref[...].astype(BF16)
            dys_ref[...] = dyb
            dyb_ref[...] = dyb
            acc_ref[...] = jnp.zeros_like(acc_ref)

        @pl.when((i == 0) & (j == 0))
        def _():
            dg_ref[...] = jnp.zeros_like(dg_ref)

        st = {}

        def s_du(rs):
            st[rs.start, "du"] = 0.5 * _dot_nt(dys_ref[rs, :], w2_ref[...])

        def s_act(rs):
            du = st.pop((rs.start, "du"))
            av = a_ref[rs, :].astype(F32)
            bv = b_ref[rs, :].astype(F32)
            sg = jax.nn.sigmoid(av)
            sil = av * sg
            da = (du * bv * (sg + sil * (1.0 - sg))).astype(BF16)
            db = (du * sil).astype(BF16)
            da_ref[rs, :] = da
            db_ref[rs, :] = db
            st[rs.start, "dab"] = (da, db)

        def s_dh(rs):
            da, db = st.pop((rs.start, "dab"))
            acc_ref[rs, :] += _dot(da, w1_ref[...]) + _dot(db, w3_ref[...])

        _emit_skewed(([slice(r, r + tm // FFN_PIECES) for r in range(0, tm, tm // FFN_PIECES)], [s_du, s_act, s_dh]))

        @pl.when(j == nf - 1)
        def _():
            xv = x_ref[...]
            dx, dg = _rms_bwd(acc_ref[...], xv, _rms_rstd(xv), g_ref[...])
            dx_ref[...] = dy_ref[...] + dx
            dg_ref[...] += dg

    row = lambda i, j: (i, 0)
    blk = lambda i, j: (i, j)
    wsp = pl.BlockSpec((tf, D_MODEL), lambda i, j: (j, 0))
    return _call(
        body, (dy, x, g, a, b, w1t, w3t, w2), comm=comm, **_grid_ends(s_len // tm, nf), name=f"ffn_bwd_{tag}",
        grid=(s_len // tm, nf),
        in_specs=[pl.BlockSpec((tm, D_MODEL), row), pl.BlockSpec((tm, D_MODEL), row),
                  pl.BlockSpec((1, D_MODEL), lambda i, j: (0, 0)),
                  pl.BlockSpec((tm, tf), blk), pl.BlockSpec((tm, tf), blk), wsp, wsp, wsp],
        out_specs=[pl.BlockSpec((tm, D_MODEL), row), pl.BlockSpec((1, D_MODEL), lambda i, j: (0, 0)),
                   pl.BlockSpec((tm, tf), blk), pl.BlockSpec((tm, tf), blk), pl.BlockSpec((tm, D_MODEL), row)],
        out_shape=[jax.ShapeDtypeStruct((s_len, D_MODEL), F32), jax.ShapeDtypeStruct((1, D_MODEL), F32),
                   jax.ShapeDtypeStruct((s_len, D_FF), BF16), jax.ShapeDtypeStruct((s_len, D_FF), BF16),
                   jax.ShapeDtypeStruct((s_len, D_MODEL), BF16)],
        scratch_shapes=[pltpu.VMEM((tm, D_MODEL), F32), pltpu.VMEM((tm, D_MODEL), BF16)],
        compiler_params=_cparams("arbitrary", "arbitrary"),
    )


def _matmul_tn(lhs, rhs, tag, comm=None):
    s_len, m = lhs.shape
    n = rhs.shape[1]
    tm = min(512, s_len)
    tj = m if m <= 1024 else 1408
    assert m % tj == 0
    last_rows = s_len // tm - 1

    def body(l_ref, r_ref, o_ref, acc_ref):
        i = pl.program_id(1)

        @pl.when(i == 0)
        def _():
            acc_ref[...] = jnp.zeros_like(acc_ref)

        acc_ref[...] += _dot_tn(l_ref[...], r_ref[...])

        @pl.when(i == last_rows)
        def _():
            o_ref[...] = acc_ref[...].astype(BF16)

    res = _call(
        body, (lhs, rhs), comm=comm, **_grid_ends(m // tj, s_len // tm), name=f"matmul_tn_{tag}",
        grid=(m // tj, s_len // tm),
        in_specs=[pl.BlockSpec((tm, tj), lambda j, i: (i, j)), pl.BlockSpec((tm, n), lambda j, i: (i, 0))],
        out_specs=[pl.BlockSpec((tj, n), lambda j, i: (j, 0))],
        out_shape=[jax.ShapeDtypeStruct((m, n), BF16)],
        scratch_shapes=[pltpu.VMEM((tj, n), F32)],
        compiler_params=_cparams("arbitrary", "arbitrary"),
    )
    return res[0] if comm is None else tuple(res)


def _proj_fwd(x1, g, wint):
    s_len = x1.shape[0]
    tm = min(512, s_len)
    dts = (BF16, BF16, BF16, BF16, BF16, BF16, F32, F32)

    def body(x_ref, g_ref, w_ref, h_ref, *outs):
        xv = x_ref[...]
        h = (xv * _rms_rstd(xv) * g_ref[...]).astype(BF16)
        h_ref[...] = h
        for p, o_ref in enumerate(outs):
            val = _dot_nt(h, w_ref[IN_OFFS[p]:IN_OFFS[p + 1], :])
            if p == 3:
                val = val * Q_SCALE
            o_ref[...] = val.astype(dts[p])

    row = lambda i: (i, 0)
    return pl.pallas_call(
        body, name="proj_fwd",
        grid=(s_len // tm,),
        in_specs=[pl.BlockSpec((tm, D_MODEL), row), pl.BlockSpec((1, D_MODEL), lambda i: (0, 0)),
                  pl.BlockSpec((IN_W, D_MODEL), lambda i: (0, 0))],
        out_specs=[pl.BlockSpec((tm, D_MODEL), row)] + [pl.BlockSpec((tm, w), row) for w in IN_SIZES],
        out_shape=[jax.ShapeDtypeStruct((s_len, D_MODEL), BF16)]
        + [jax.ShapeDtypeStruct((s_len, w), dt) for w, dt in zip(IN_SIZES, dts)],
        compiler_params=_cparams("parallel"),
    )(x1, g, wint)


def _proj_bwd(dpieces, dx2, x1, g, wint):
    s_len = x1.shape[0]
    tm = min(512, s_len)

    def body(*refs):
        dps = refs[:8]
        dx2_ref, x_ref, g_ref, w_ref, dx_ref, dg_ref = refs[8:]

        @pl.when(pl.program_id(0) == 0)
        def _():
            dg_ref[...] = jnp.zeros_like(dg_ref)

        dh = _dot(dps[0][...], w_ref[IN_OFFS[0]:IN_OFFS[1], :])
        for p in range(1, 8):
            dh += _dot(dps[p][...], w_ref[IN_OFFS[p]:IN_OFFS[p + 1], :])
        xv = x_ref[...]
        dx, dg = _rms_bwd(dh, xv, _rms_rstd(xv), g_ref[...])
        dx_ref[...] = dx2_ref[...] + dx
        dg_ref[...] += dg

    row = lambda i: (i, 0)
    return pl.pallas_call(
        body, name="proj_bwd",
        grid=(s_len // tm,),
        in_specs=[pl.BlockSpec((tm, w), row) for w in IN_SIZES]
        + [pl.BlockSpec((tm, D_MODEL), row), pl.BlockSpec((tm, D_MODEL), row),
           pl.BlockSpec((1, D_MODEL), lambda i: (0, 0)), pl.BlockSpec((IN_W, D_MODEL), lambda i: (0, 0))],
        out_specs=[pl.BlockSpec((tm, D_MODEL), row), pl.BlockSpec((1, D_MODEL), lambda i: (0, 0))],
        out_shape=[jax.ShapeDtypeStruct((s_len, D_MODEL), F32), jax.ShapeDtypeStruct((1, D_MODEL), F32)],
        compiler_params=_cparams("arbitrary"),
    )(*dpieces, dx2, x1, g, wint)


def _merge_fwd(x1, oa, ob, ga, gb, wswa, wsb, wout):
    s_len = x1.shape[0]
    tm = min(512, s_len)

    def body(x_ref, oa_ref, ob_ref, ga_ref, gb_ref, wa_ref, wb_ref, wo_ref, xo_ref, mg_ref):
        pa = _dot(oa_ref[...], wa_ref[...])
        pb = _dot(ob_ref[...], wb_ref[...])
        mg = (jax.nn.sigmoid(ga_ref[...]) * pa + jax.nn.sigmoid(gb_ref[...]) * pb).astype(BF16)
        mg_ref[...] = mg
        xo_ref[...] = x_ref[...] + _dot(mg, wo_ref[...])

    row = lambda i: (i, 0)
    full = lambda i: (0, 0)
    return pl.pallas_call(
        body, name="merge_fwd",
        grid=(s_len // tm,),
        in_specs=[pl.BlockSpec((tm, D_MODEL), row), pl.BlockSpec((tm, 512), row), pl.BlockSpec((tm, 512), row),
                  pl.BlockSpec((tm, D_MODEL), row), pl.BlockSpec((tm, D_MODEL), row),
                  pl.BlockSpec((512, D_MODEL), full), pl.BlockSpec((512, D_MODEL), full),
                  pl.BlockSpec((D_MODEL, D_MODEL), full)],
        out_specs=[pl.BlockSpec((tm, D_MODEL), row), pl.BlockSpec((tm, D_MODEL), row)],
        out_shape=[jax.ShapeDtypeStruct((s_len, D_MODEL), F32), jax.ShapeDtypeStruct((s_len, D_MODEL), BF16)],
        compiler_params=_cparams("parallel"),
    )(x1, oa, ob, ga, gb, wswa, wsb, wout)


def _merge_bwd(dx2, oa, ob, ga, gb, wswa, wsb, wout):
    s_len = dx2.shape[0]
    tm = min(512, s_len)

    def body(dx_ref, oa_ref, ob_ref, ga_ref, gb_ref, wa_ref, wb_ref, wo_ref,
             doa_ref, dob_ref, dga_ref, dgb_ref, dpa_ref, dpb_ref, dxb_ref):
        dxb = dx_ref[...].astype(BF16)
        dxb_ref[...] = dxb
        dmg = _dot_nt(dxb, wo_ref[...])
        for o_ref, g_ref, w_ref, do_ref, dg_ref, dp_ref in (
                (oa_ref, ga_ref, wa_ref, doa_ref, dga_ref, dpa_ref),
                (ob_ref, gb_ref, wb_ref, dob_ref, dgb_ref, dpb_ref)):
            pv = _dot(o_ref[...], w_ref[...])
            sg = jax.nn.sigmoid(g_ref[...])
            dp = (dmg * sg).astype(BF16)
            dp_ref[...] = dp
            dg_ref[...] = (dmg * pv * sg * (1.0 - sg)).astype(BF16)
            do_ref[...] = _dot_nt(dp, w_ref[...]).astype(BF16)

    row = lambda i: (i, 0)
    full = lambda i: (0, 0)
    wide = pl.BlockSpec((tm, D_MODEL), row)
    half = pl.BlockSpec((tm, 512), row)
    return pl.pallas_call(
        body, name="merge_bwd",
        grid=(s_len // tm,),
        in_specs=[wide, half, half, wide, wide, pl.BlockSpec((512, D_MODEL), full),
                  pl.BlockSpec((512, D_MODEL), full), pl.BlockSpec((D_MODEL, D_MODEL), full)],
        out_specs=[half, half, wide, wide, wide, wide, wide],
        out_shape=[jax.ShapeDtypeStruct((s_len, 512), BF16)] * 2 + [jax.ShapeDtypeStruct((s_len, D_MODEL), BF16)] * 5,
        compiler_params=_cparams("parallel"),
    )(dx2, oa, ob, ga, gb, wswa, wsb, wout)


def _loss_fwd_bwd(x3, tgt, g):
    s_len = x3.shape[0]
    tm = min(1024, s_len)

    def body(x_ref, t_ref, g_ref, dx_ref, loss_ref, dg_ref):
        @pl.when(pl.program_id(0) == 0)
        def _():
            loss_ref[...] = jnp.zeros_like(loss_ref)
            dg_ref[...] = jnp.zeros_like(dg_ref)

        xv = x_ref[...]
        gv = g_ref[...]
        r = _rms_rstd(xv)
        err = xv * r * gv - t_ref[...]
        loss_ref[...] += 0.5 * jnp.sum(jnp.mean(err * err, axis=-1, keepdims=True), axis=0, keepdims=True)
        dx, dg = _rms_bwd(err * (1.0 / D_MODEL), xv, r, gv)
        dx_ref[...] = dx
        dg_ref[...] += dg

    row = lambda i: (i, 0)
    return pl.pallas_call(
        body, name="loss_fwd_bwd",
        grid=(s_len // tm,),
        in_specs=[pl.BlockSpec((tm, D_MODEL), row), pl.BlockSpec((tm, D_MODEL), row),
                  pl.BlockSpec((1, D_MODEL), lambda i: (0, 0))],
        out_specs=[pl.BlockSpec((tm, D_MODEL), row), pl.BlockSpec((1, 1), lambda i: (0, 0)),
                   pl.BlockSpec((1, D_MODEL), lambda i: (0, 0))],
        out_shape=[jax.ShapeDtypeStruct((s_len, D_MODEL), F32), jax.ShapeDtypeStruct((1, 1), F32),
                   jax.ShapeDtypeStruct((1, D_MODEL), F32)],
        compiler_params=_cparams("arbitrary"),
    )(x3, tgt, g)


def _rel_bucket_matrix():
    qi = jnp.arange(SWA_BLOCK)[:, None] + SWA_BLOCK
    kj = jnp.arange(2 * SWA_BLOCK)[None, :]
    dist = jnp.maximum(qi - kj, 0)
    max_exact = REL_BUCKETS // 2
    d = jnp.maximum(dist, 1).astype(F32)
    large = max_exact + (jnp.log(d / max_exact) / np.log(REL_MAX_DIST / max_exact)
                         * (REL_BUCKETS - max_exact)).astype(jnp.int32)
    large = jnp.minimum(large, REL_BUCKETS - 1)
    return jnp.where(dist < max_exact, dist, large).astype(jnp.int32)


def _swa_bias_into(bias_ref, bkt_ref, tab_ref):
    bk = bkt_ref[...]
    for h in range(N_HEADS):
        acc = jnp.zeros(bk.shape, F32)
        for bucket in range(REL_BUCKETS):
            acc = jnp.where(bk == bucket, tab_ref[bucket, h], acc)
        bias_ref[h] = acc


def _swa_valid(n):
    shape = (SWA_BLOCK, 2 * SWA_BLOCK)
    row = lax.broadcasted_iota(jnp.int32, shape, 0)
    col = lax.broadcasted_iota(jnp.int32, shape, 1)
    dist = row + SWA_BLOCK - col
    return (dist >= 0) & (dist < SWA_BLOCK) & ((col >= SWA_BLOCK) | (n > 0))


def _swa_windows(kp_ref, kc_ref, vp_ref, vc_ref):
    lanes = [slice(g * LANES, (g + 1) * LANES) for g in range(SWA_KV_HEADS)]
    return ([jnp.concatenate([kp_ref[:, gl], kc_ref[:, gl]], axis=0) for gl in lanes],
            [jnp.concatenate([vp_ref[:, gl], vc_ref[:, gl]], axis=0) for gl in lanes])


def _swa_probs(qk, bias, sink, valid):
    lg = jnp.where(valid, qk * Q_SCALE + bias, NEG_BIG)
    m = jnp.maximum(jnp.max(lg, axis=-1, keepdims=True), sink)
    e = jnp.exp(lg - m)
    es = jnp.exp(sink - m)
    inv = 1.0 / (jnp.sum(e, axis=-1, keepdims=True) + es)
    return e * inv, es * inv


def _swa_specs(s_len):
    blk = SWA_BLOCK
    cur = lambda n: (n, 0)
    prev = lambda n: (jnp.maximum(n - 1, 0), 0)
    kvw = SWA_KV_HEADS * LANES
    return [pl.BlockSpec(memory_space=pltpu.SMEM), pl.BlockSpec(memory_space=pltpu.SMEM),
            pl.BlockSpec((blk, 2 * blk), lambda n: (0, 0)),
            pl.BlockSpec((blk, N_HEADS * LANES), cur),
            pl.BlockSpec((blk, kvw), prev), pl.BlockSpec((blk, kvw), cur),
            pl.BlockSpec((blk, kvw), prev), pl.BlockSpec((blk, kvw), cur)]


def _swa_fwd(tab, sinks, bkt, q, k, v):
    s_len = q.shape[0]
    blk = SWA_BLOCK

    def body(tab_ref, sink_ref, bkt_ref, q_ref, kp_ref, kc_ref, vp_ref, vc_ref, o_ref, bias_ref):
        n = pl.program_id(0)

        @pl.when(n == 0)
        def _():
            _swa_bias_into(bias_ref, bkt_ref, tab_ref)

        valid = _swa_valid(n)
        kk, vv = _swa_windows(kp_ref, kc_ref, vp_ref, vc_ref)
        st = {}

        def s_logits(h):
            st[h, "lg"] = _dot_nt(q_ref[:, h * LANES:(h + 1) * LANES], kk[h // SWA_GROUP])

        def s_probs(h):
            st[h, "p"] = _swa_probs(st.pop((h, "lg")), bias_ref[h], sink_ref[0, h], valid)[0].astype(BF16)

        def s_values(h):
            o_ref[:, h * LANES:(h + 1) * LANES] = _dot(st.pop((h, "p")), vv[h // SWA_GROUP]).astype(BF16)

        _emit_skewed((list(range(N_HEADS)), [s_logits, s_probs, s_values]))

    return pl.pallas_call(
        body, name="swa_fwd",
        grid=(s_len // blk,),
        in_specs=_swa_specs(s_len),
        out_specs=pl.BlockSpec((blk, N_HEADS * LANES), lambda n: (n, 0)),
        out_shape=jax.ShapeDtypeStruct((s_len, N_HEADS * LANES), BF16),
        scratch_shapes=[pltpu.VMEM((N_HEADS, blk, 2 * blk), F32)],
        compiler_params=_cparams("arbitrary"),
    )(tab, sinks, bkt, q, k, k, v, v)


def _swa_bwd(tab, sinks, bkt, q, k, v, do, comm=None):
    s_len = q.shape[0]
    blk = SWA_BLOCK
    nb = s_len // blk
    kvw = SWA_KV_HEADS * LANES

    def body(tab_ref, sink_ref, bkt_ref, q_ref, kp_ref, kc_ref, vp_ref, vc_ref, do_ref,
             dq_ref, dk_ref, dv_ref, dtab_ref, dsink_ref, bias_ref, dbias_ref):
        n = pl.program_id(0)

        @pl.when(n == 0)
        def _():
            _swa_bias_into(bias_ref, bkt_ref, tab_ref)
            dbias_ref[...] = jnp.zeros_like(dbias_ref)
            dk_ref[...] = jnp.zeros_like(dk_ref)
            dv_ref[...] = jnp.zeros_like(dv_ref)
            dsink_ref[...] = jnp.zeros_like(dsink_ref)
            dtab_ref[...] = jnp.zeros_like(dtab_ref)

        valid = _swa_valid(n)
        cur_rows = pl.ds(pl.multiple_of(n * blk, blk), blk)
        prev_rows = pl.ds(pl.multiple_of(jnp.maximum(n - 1, 0) * blk, blk), blk)
        kk, vv = _swa_windows(kp_ref, kc_ref, vp_ref, vc_ref)
        st = {}

        def s_logits(h):
            hl = slice(h * LANES, (h + 1) * LANES)
            st[h, "lg"] = _dot_nt(q_ref[:, hl], kk[h // SWA_GROUP])
            st[h, "dp"] = _dot_nt(do_ref[:, hl], vv[h // SWA_GROUP])

        def s_probs(h):
            p, ps = _swa_probs(st.pop((h, "lg")), bias_ref[h], sink_ref[0, h], valid)
            dp = st.pop((h, "dp"))
            delta = jnp.sum(p * dp, axis=-1, keepdims=True)
            dl = p * (dp - delta)
            dsink_ref[h:h + 1, :] += jnp.broadcast_to(-jnp.sum(ps * delta, axis=0, keepdims=True), (1, LANES))
            dbias_ref[h] += dl
            st[h, "dl"], st[h, "p"] = dl.astype(BF16), p.astype(BF16)

        def s_products(h):
            hl = slice(h * LANES, (h + 1) * LANES)
            gl = slice(h // SWA_GROUP * LANES, (h // SWA_GROUP + 1) * LANES)
            dlb = st.pop((h, "dl"))
            dq_ref[:, hl] = (Q_SCALE * _dot(dlb, kk[h // SWA_GROUP])).astype(BF16)
            dk_win = Q_SCALE * _dot_tn(dlb, q_ref[:, hl])
            dv_win = _dot_tn(st.pop((h, "p")), do_ref[:, hl])
            dk_ref[prev_rows, gl] += dk_win[:blk]
            dv_ref[prev_rows, gl] += dv_win[:blk]
            dk_ref[cur_rows, gl] += dk_win[blk:]
            dv_ref[cur_rows, gl] += dv_win[blk:]

        _emit_skewed((list(range(N_HEADS)), [s_logits, s_probs, s_products]))

        @pl.when(n == nb - 1)
        def _():
            bk = bkt_ref[...]
            lane = lax.broadcasted_iota(jnp.int32, (1, LANES), 1)
            for bucket in range(REL_BUCKETS):
                rowv = jnp.zeros((1, LANES), F32)
                for h in range(N_HEADS):
                    val = jnp.sum(jnp.where(bk == bucket, dbias_ref[h], 0.0), axis=1, keepdims=True)
                    val = jnp.sum(val, axis=0, keepdims=True)
                    rowv = jnp.where(lane == h, val, rowv)
                dtab_ref[bucket:bucket + 1, :] = rowv

    return _call(
        body, (tab, sinks, bkt, q, k, k, v, v, do), comm=comm, **_grid_ends(nb), name="swa_bwd",
        grid=(nb,),
        in_specs=_swa_specs(s_len) + [pl.BlockSpec((blk, N_HEADS * LANES), lambda n: (n, 0))],
        out_specs=[pl.BlockSpec((blk, N_HEADS * LANES), lambda n: (n, 0)),
                   pl.BlockSpec((s_len, kvw), lambda n: (0, 0)), pl.BlockSpec((s_len, kvw), lambda n: (0, 0)),
                   pl.BlockSpec((REL_BUCKETS, LANES), lambda n: (0, 0)), pl.BlockSpec((N_HEADS, LANES), lambda n: (0, 0))],
        out_shape=[jax.ShapeDtypeStruct((s_len, N_HEADS * LANES), BF16),
                   jax.ShapeDtypeStruct((s_len, kvw), F32), jax.ShapeDtypeStruct((s_len, kvw), F32),
                   jax.ShapeDtypeStruct((REL_BUCKETS, LANES), F32), jax.ShapeDtypeStruct((N_HEADS, LANES), F32)],
        scratch_shapes=[pltpu.VMEM((N_HEADS, blk, 2 * blk), F32), pltpu.VMEM((N_HEADS, blk, 2 * blk), F32)],
        compiler_params=_cparams("arbitrary"),
    )


def _sb_terms(z, valid):
    zc = jnp.minimum(z, SB_LOGIT_CAP)
    lk = -jnp.log(1.0 + jnp.exp(zc))
    lsz = zc + lk
    return lsz, (lk if valid is None else jnp.where(valid, lk, 0.0))


def _bf16_parts(vals):
    parts, rest = [], vals
    for n in range(SB_SUM_PARTS):
        parts.append(rest.astype(BF16))
        if n + 1 < SB_SUM_PARTS:
            rest = rest - parts[-1].astype(F32)
    return parts[0] if len(parts) == 1 else jnp.concatenate(parts, axis=1)


def _row_sum_lanes(vals):
    return jnp.broadcast_to(jnp.sum(vals, axis=-1, keepdims=True), (vals.shape[0], LANES))


def _emit_skewed(*groups):
    for step in range(max(len(items) + len(stages) - 1 for items, stages in groups)):
        for items, stages in groups:
            for s, stage in enumerate(stages):
                if 0 <= step - s < len(items):
                    stage(items[step - s])


def _sb_items(edge):
    items = []
    for h in range(2):
        for r0 in range(0, SB_QUERIES, SB_ROWS):
            if edge is None or r0 >= (edge + 1) * SB_KEYS:
                items.append((h, r0, False))
            elif r0 + SB_ROWS - 1 > edge * SB_KEYS:
                items.append((h, r0, True))
    return items


def _sb_valid(w, edge):
    row = lax.broadcasted_iota(jnp.int32, (SB_ROWS, SB_KEYS), 0) + w[1]
    col = lax.broadcasted_iota(jnp.int32, (SB_ROWS, SB_KEYS), 1) + edge * SB_KEYS
    return col < row


def _sb_consts(tq, tk):
    low = lax.broadcasted_iota(jnp.int32, (tq, LANES), 1) < HEAD_DIM
    row = lax.broadcasted_iota(jnp.int32, (tk, tk), 0)
    col = lax.broadcasted_iota(jnp.int32, (tk, tk), 1)
    right = (row > col).astype(BF16)
    left = (row < col).astype(BF16)
    return low, jnp.concatenate([right] * SB_SUM_PARTS, axis=0), jnp.concatenate([left] * SB_SUM_PARTS, axis=0)


def _sb_fwd(q, kt, v, comm=None):
    s_len = q.shape[0]
    tq, tk, tr = SB_QUERIES, SB_KEYS, SB_ROWS
    nk, ratio = s_len // tk, tq // tk
    assert nk <= LANES

    def body(q_ref, kt_ref, v_ref, o_ref, car_ref, c_ref, oacc_ref, logw_ref, lksum_ref):
        i = pl.program_id(1)
        qv = q_ref[...]
        low, tri2, _ = _sb_consts(tq, tk)
        lane = lax.broadcasted_iota(jnp.int32, (tr, LANES), 1)
        zero = jnp.zeros_like(qv)
        q_heads = (jnp.where(low, qv, zero), jnp.where(low, zero, qv))
        c_ref[...] = jnp.zeros_like(c_ref)
        oacc_ref[...] = jnp.zeros_like(oacc_ref)
        car_ref[...] = jnp.full_like(car_ref, NEG_BIG)

        def front(j, edge):
            ktv = kt_ref[0, j]
            slot = j % 2
            st = {}

            def s_logits(w):
                st[w, "z"] = _dot(q_heads[w[0]][w[1]:w[1] + tr], ktv)

            def s_terms(w):
                valid = _sb_valid(w, edge) if w[2] else None
                lsz, lk = _sb_terms(st.pop((w, "z")), valid)
                st[w, "parts"] = _bf16_parts(lk)
                st[w, "lsz"] = lsz if valid is None else jnp.where(valid, lsz, NEG_BIG)
                lksum_ref[slot, w[0], w[1]:w[1] + tr, :] = _row_sum_lanes(lk)

            def s_suffix(w):
                logw_ref[slot, w[0], w[1]:w[1] + tr, :] = st.pop((w, "lsz")) + _dot(st.pop((w, "parts")), tri2)

            return _sb_items(edge), [s_logits, s_terms, s_suffix]

        def back(j, edge):
            vv = v_ref[pl.ds(pl.multiple_of(j * tk, tk), tk), :]
            slot = j % 2
            st = {}

            def s_weights(w):
                h, rs = w[0], slice(w[1], w[1] + tr)
                c = c_ref[h, rs, :]
                st[w, "a"] = jnp.exp(logw_ref[slot, h, rs, :] + jnp.tile(c, (1, tk // LANES))).astype(BF16)
                car_ref[h, rs, :] = jnp.where(lane == j, c, car_ref[h, rs, :])
                c_ref[h, rs, :] = c + lksum_ref[slot, h, rs, :]

            def s_values(w):
                oacc_ref[w[0], w[1]:w[1] + tr, :] += _dot(st.pop((w, "a")), vv)

            return _sb_items(edge), [s_weights, s_values]

        first = i * ratio
        _emit_skewed(front(first + ratio - 1, ratio - 1))
        for m in reversed(range(ratio - 1)):
            _emit_skewed(front(first + m, m), back(first + m + 1, m + 1))

        @pl.when(i == 0)
        def _():
            _emit_skewed(back(0, 0))

        def alive():
            return (jnp.max(c_ref[...]) >= SB_DEAD_CARRY).astype(jnp.int32)

        @pl.when(i > 0)
        def _():
            _emit_skewed(front(first - 1, None), back(first, 0))

            def step(state):
                pending, _ = state
                _emit_skewed(front(pending - 1, None), back(pending, None))
                return pending - 1, alive()

            pending, live = lax.while_loop(lambda s: (s[0] > 0) & (s[1] > 0), step, (first - 1, alive()))

            @pl.when(live > 0)
            def _():
                _emit_skewed(back(pending, None))

        o_ref[...] = jnp.where(low, oacc_ref[0], oacc_ref[1]).astype(BF16)

    return _call(
        body, (q, kt, v), comm=comm, **_grid_ends(N_HEADS // 2, s_len // tq), name="sb_fwd",
        grid=(N_HEADS // 2, s_len // tq),
        in_specs=[pl.BlockSpec((tq, LANES), lambda p, i: (i, p)),
                  pl.BlockSpec((1, nk, LANES, tk), lambda p, i: (p, 0, 0, 0)),
                  pl.BlockSpec((s_len, LANES), lambda p, i: (0, p))],
        out_specs=[pl.BlockSpec((tq, LANES), lambda p, i: (i, p)), pl.BlockSpec((2, tq, LANES), lambda p, i: (p, i, 0))],
        out_shape=[jax.ShapeDtypeStruct((s_len, N_HEADS * HEAD_DIM), BF16),
                   jax.ShapeDtypeStruct((N_HEADS, s_len, LANES), F32)],
        scratch_shapes=[pltpu.VMEM((2, tq, LANES), F32), pltpu.VMEM((2, tq, LANES), F32),
                        pltpu.VMEM((2, 2, tq, tk), F32), pltpu.VMEM((2, 2, tq, LANES), F32)],
        compiler_params=_cparams("arbitrary", "arbitrary"),
    )


def _sb_bwd(q, qt, kt, k, vt, do, dot, cars):
    s_len = q.shape[0]
    tq, tk, tr = SB_QUERIES, SB_KEYS, SB_ROWS
    nk, ratio = s_len // tk, tq // tk

    def body(q_ref, qt_ref, kt_ref, k_ref, vt_ref, do_ref, dot_ref, car_ref, dq_ref, dk_ref, dv_ref,
             gleft_ref, dqacc_ref, dkacc_ref, dvacc_ref, logw_ref, lsz_ref, da_ref, a_ref, dz_ref):
        i = pl.program_id(1)

        @pl.when(i == 0)
        def _():
            dkacc_ref[...] = jnp.zeros_like(dkacc_ref)
            dvacc_ref[...] = jnp.zeros_like(dvacc_ref)

        qv = q_ref[...]
        dov = do_ref[...]
        low, tri_right2, tri_left2 = _sb_consts(tq, tk)
        lane = lax.broadcasted_iota(jnp.int32, (tr, LANES), 1)
        zero = jnp.zeros_like(qv)
        q_heads = (jnp.where(low, qv, zero), jnp.where(low, zero, qv))
        do_heads = (jnp.where(low, dov, zero), jnp.where(low, zero, dov))
        q_t = qt_ref[0, 0]
        do_t = dot_ref[0, 0]
        gleft_ref[...] = jnp.zeros_like(gleft_ref)
        dqacc_ref[...] = jnp.zeros_like(dqacc_ref)

        def front(j, edge):
            ktv = kt_ref[0, j]
            vtv = vt_ref[0, j]
            slot = j % 2
            st = {}

            def s_logits(w):
                h, rs = w[0], slice(w[1], w[1] + tr)
                st[w, "z"] = _dot(q_heads[h][rs], ktv)
                da_ref[slot, h, rs, :] = _dot(do_heads[h][rs], vtv)

            def s_terms(w):
                h, rs = w[0], slice(w[1], w[1] + tr)
                valid = _sb_valid(w, edge) if w[2] else None
                lsz, lk = _sb_terms(st.pop((w, "z")), valid)
                st[w, "parts"] = _bf16_parts(lk)
                lsz = lsz if valid is None else jnp.where(valid, lsz, NEG_BIG)
                lsz_ref[slot, h, rs, :] = lsz
                st[w, "lszc"] = lsz + jnp.sum(jnp.where(lane == j, car_ref[h, rs, :], 0.0), axis=-1, keepdims=True)

            def s_suffix(w):
                logw_ref[slot, w[0], w[1]:w[1] + tr, :] = st.pop((w, "lszc")) + _dot(st.pop((w, "parts")), tri_right2)

            return _sb_items(edge), [s_logits, s_terms, s_suffix]

        def back(j, edge):
            kv = k_ref[pl.ds(pl.multiple_of(j * tk, tk), tk), :]
            slot = j % 2
            st = {}

            items = _sb_items(edge)
            head_rows = [[r0 for hh, r0, _ in items if hh == h] for h in range(2)]

            def s_weights(w):
                h, rs = w[0], slice(w[1], w[1] + tr)
                a = jnp.exp(logw_ref[slot, h, rs, :])
                g = a * da_ref[slot, h, rs, :]
                a_ref[h, rs, :] = a.astype(BF16)
                st[w, "g"], st[w, "parts"] = g, _bf16_parts(g)

            def s_prefix(w):
                st[w, "gs"] = _dot(st.pop((w, "parts")), tri_left2)

            def s_dz(w):
                h, rs = w[0], slice(w[1], w[1] + tr)
                g = st.pop((w, "g"))
                gleft = gleft_ref[h, rs, :]
                gsum = st.pop((w, "gs")) + jnp.tile(gleft, (1, tk // LANES))
                dz = (g - jnp.exp(lsz_ref[slot, h, rs, :]) * (g + gsum)).astype(BF16)
                st[w, "dz"] = dz
                dz_ref[h, rs, :] = dz
                gleft_ref[h, rs, :] = gleft + _row_sum_lanes(g)

            def s_products(w):
                h, rs = w[0], slice(w[1], w[1] + tr)
                dqacc_ref[h, rs, :] += _dot(st.pop((w, "dz")), kv)
                if w[1] == head_rows[h][-1]:
                    feat = slice(h * HEAD_DIM, (h + 1) * HEAD_DIM)
                    hr = slice(head_rows[h][0], tq)
                    dkacc_ref[j, feat, :] += _dot(q_t[feat, hr], dz_ref[h, hr, :])
                    dvacc_ref[j, feat, :] += _dot(do_t[feat, hr], a_ref[h, hr, :])

            return items, [s_weights, s_prefix, s_dz, s_products]

        first = i * ratio
        tile_max = jnp.max(jnp.maximum(car_ref[0], car_ref[1]), axis=0, keepdims=True)
        start = jnp.clip(first + ratio - jnp.sum(jnp.where(tile_max >= SB_DEAD_CARRY, 1, 0)), 0, first)

        @pl.when(start == first)
        def _():
            _emit_skewed(front(first, 0))

        @pl.when(start < first)
        def _():
            _emit_skewed(front(start, None))

            def step(jj, carry):
                _emit_skewed(front(jj, None), back(jj - 1, None))
                return carry

            lax.fori_loop(start + 1, first, step, 0)
            _emit_skewed(front(first, 0), back(first - 1, None))

        for m in range(1, ratio):
            _emit_skewed(front(first + m, m), back(first + m - 1, m - 1))
        _emit_skewed(back(first + ratio - 1, ratio - 1))
        dq_ref[...] = (Q_SCALE * jnp.where(low, dqacc_ref[0], dqacc_ref[1])).astype(BF16)

        @pl.when(i == s_len // tq - 1)
        def _():
            dk_ref[0] = dkacc_ref[...].astype(BF16)
            dv_ref[0] = dvacc_ref[...].astype(BF16)

    qblk = pl.BlockSpec((tq, LANES), lambda p, i: (i, p))
    qtblk = pl.BlockSpec((1, 1, LANES, tq), lambda p, i: (p, i, 0, 0))
    tblk = pl.BlockSpec((1, nk, LANES, tk), lambda p, i: (p, 0, 0, 0))
    col_full = pl.BlockSpec((s_len, LANES), lambda p, i: (0, p))
    tshape = jax.ShapeDtypeStruct((N_HEADS // 2, nk, LANES, tk), BF16)
    return pl.pallas_call(
        body, name="sb_bwd",
        grid=(N_HEADS // 2, s_len // tq),
        in_specs=[qblk, qtblk, tblk, col_full, tblk, qblk, qtblk, pl.BlockSpec((2, tq, LANES), lambda p, i: (p, i, 0))],
        out_specs=[qblk, tblk, tblk],
        out_shape=[jax.ShapeDtypeStruct((s_len, N_HEADS * HEAD_DIM), BF16), tshape, tshape],
        scratch_shapes=[pltpu.VMEM((2, tq, LANES), F32), pltpu.VMEM((2, tq, LANES), F32),
                        pltpu.VMEM((nk, LANES, tk), F32), pltpu.VMEM((nk, LANES, tk), F32)]
        + [pltpu.VMEM((2, 2, tq, tk), F32)] * 3 + [pltpu.VMEM((2, tq, tk), BF16)] * 2,
        compiler_params=_cparams("parallel", "arbitrary"),
    )(q, qt, kt, k, vt, do, dot, cars)


def _pad_heads(a, heads):
    s_len = a.shape[0]
    a = a.reshape(s_len, heads, HEAD_DIM)
    return jnp.pad(a, ((0, 0), (0, 0), (0, LANES - HEAD_DIM))).reshape(s_len, heads * LANES)


def _unpad_heads(a, heads):
    s_len = a.shape[0]
    return a.reshape(s_len, heads, LANES)[:, :, :HEAD_DIM].reshape(s_len, heads * HEAD_DIM)


def _tile_transposed(a, groups, t):
    s_len = a.shape[0]
    return a.T.reshape(groups, LANES, s_len // t, t).transpose(0, 2, 1, 3)


def _tile_untransposed(a):
    groups, nt, _, t = a.shape
    return a.transpose(0, 2, 1, 3).reshape(groups * LANES, nt * t).T


def _local_step(xs, tgt, gains, sinks, rel_bias, weights_of, ship):
    g1, gmix, g2, gfin = gains
    bkt = _rel_bucket_matrix()
    groups = N_HEADS // 2
    grads = {}

    def carried(outs, comm, count):
        return outs[:count], (list(outs[count:]) if comm is not None else None)

    wts = dict(weights_of(0, None))
    comm = ship("weights", 1)
    (x1, h1, a1, b1, u1), landed = carried(
        _ffn_fwd(xs, g1, wts["ffn1_w1t"], wts["ffn1_w3t"], wts["ffn1_w2"], "1", comm), comm, 5)
    wts.update(weights_of(1, landed))
    hm, qa, ka, va, qb, kb, vb, ga, gb = _proj_fwd(x1, gmix, wts["w_int"])
    qa_p, ka_p, va_p = _pad_heads(qa, N_HEADS), _pad_heads(ka, SWA_KV_HEADS), _pad_heads(va, SWA_KV_HEADS)
    oa_p = _swa_fwd(rel_bias, sinks, bkt, qa_p, ka_p, va_p)
    kbt = _tile_transposed(kb, groups, SB_KEYS)
    comm = ship("weights", 2)
    (ob, cars), landed = carried(_sb_fwd(qb, kbt, vb, comm), comm, 2)
    wts.update(weights_of(2, landed))
    oa = _unpad_heads(oa_p, N_HEADS)
    x2, mg = _merge_fwd(x1, oa, ob, ga, gb, wts["w_swa"], wts["w_sb"], wts["w_out"])
    x3, h3, a3, b3, u3 = _ffn_fwd(x2, g2, wts["ffn2_w1t"], wts["ffn2_w3t"], wts["ffn2_w2"], "2")
    dx3, loss, dgfin = _loss_fwd_bwd(x3, tgt, gfin)

    dx2, dg2, da3, db3, dx3b = _ffn_bwd(dx3, x2, g2, a3, b3, wts["ffn2_w1t"], wts["ffn2_w3t"], wts["ffn2_w2"], "2")
    big = {"ffn2_w1t": _matmul_tn(da3, h3, "ffn2_w1"), "ffn2_w3t": _matmul_tn(db3, h3, "ffn2_w3"),
           "ffn2_w2": _matmul_tn(u3, dx3b, "ffn2_w2")}

    doa, dob, dga, dgb, dpa, dpb, dx2b = _merge_bwd(dx2, oa, ob, ga, gb, wts["w_swa"], wts["w_sb"], wts["w_out"])
    comm = ship("grads", GROUPS[2], big)
    (dqa_p, dka_p, dva_p, dtab, dsink), landed = carried(
        _swa_bwd(rel_bias, sinks, bkt, qa_p, ka_p, va_p, _pad_heads(doa, N_HEADS), comm), comm, 5)
    grads[GROUPS[2]] = big if comm is None else landed[0]

    big = {"w_out": _matmul_tn(mg, dx2b, "w_out"), "w_swa": _matmul_tn(oa, dpa, "w_swa"),
           "w_sb": _matmul_tn(ob, dpb, "w_sb")}
    dqb, dkbt, dvbt = _sb_bwd(qb, _tile_transposed(qb, groups, SB_QUERIES), kbt, kb,
                              _tile_transposed(vb, groups, SB_KEYS), dob, _tile_transposed(dob, groups, SB_QUERIES), cars)
    dkb, dvb = _tile_untransposed(dkbt), _tile_untransposed(dvbt)
    dpieces = (_unpad_heads(dqa_p, N_HEADS), _unpad_heads(dka_p, SWA_KV_HEADS).astype(BF16),
               _unpad_heads(dva_p, SWA_KV_HEADS).astype(BF16), dqb, dkb, dvb, dga, dgb)
    big["w_int"] = jnp.concatenate([_matmul_tn(dp, hm, f"w_in{p}") for p, dp in enumerate(dpieces)], axis=0)
    dx1, dgmix = _proj_bwd(dpieces, dx2, x1, gmix, wts["w_int"])

    comm = ship("grads", GROUPS[1], big)
    (dx0, dg1, da1, db1, dx1b), landed = carried(
        _ffn_bwd(dx1, xs, g1, a1, b1, wts["ffn1_w1t"], wts["ffn1_w3t"], wts["ffn1_w2"], "1", comm), comm, 5)
    grads[GROUPS[1]] = big if comm is None else landed[0]

    prev = None
    for name, lhs, rhs in (("ffn1_w1", da1, h1), ("ffn1_w3", db1, h1), ("ffn1_w2", u1, dx1b)):
        comm = None if prev is None else ship("grads", (prev[0],), prev[1])
        res = _matmul_tn(lhs, rhs, name, comm)
        if prev is not None:
            grads[(prev[0],)] = prev[1] if comm is None else res[1]
        prev = (name, {_GRAD_KEY[name]: res if comm is None else res[0]})
    grads[(prev[0],)] = prev[1]

    small = {"gains": (dg1, dgmix, dg2, dgfin), "sinks": dsink[:, 0], "rel_bias": dtab[:, :N_HEADS]}
    return loss, dx0, small, grads


def _my_place():
    return lax.axis_index("x"), lax.axis_index("y"), lax.axis_index("c")


def _flip(v, bit):
    return 1 - v if bit else v


_RELATIONS = tuple((k >> 2 & 1, k >> 1 & 1, k & 1) for k in range(1, N_DEV))


def _gather_weights(blocks, tag):
    count = len(blocks)

    def body(*refs):
        x_refs, out_refs = refs[:count], refs[count:2 * count]
        send_sems, recv_sems, local_sems = refs[2 * count:]
        x, y, c = _my_place()
        me, sibling = (x, y, c), (x, y, 1 - c)
        chips = [(1 - x, y), (x, 1 - y), (1 - x, 1 - y)]

        def rows(s, px, py, pc):
            return out_refs[s].at[4 * px + 2 * py + pc]

        def copy(s, k, block, to, src=None):
            return pltpu.make_async_remote_copy(
                src_ref=rows(s, *block) if src is None else src, dst_ref=rows(s, *block),
                send_sem=send_sems.at[s, k], recv_sem=recv_sems.at[s, k],
                device_id=to, device_id_type=pl.DeviceIdType.MESH)

        mine = [pltpu.make_async_copy(x_refs[s], rows(s, *me), local_sems.at[s]) for s in range(count)]
        first, passed = [], []
        for s in range(count):
            mine[s].start()
            first.append(copy(s, 0, me, sibling, src=x_refs[s]))
            first += [copy(s, 1 + j, me, (*chip, c), src=x_refs[s]) for j, chip in enumerate(chips)]
        for cp in first:
            cp.start()
        for s in range(count):
            for j, chip in enumerate(chips):
                copy(s, 1 + j, (*chip, c), me).wait_recv()
                passed.append(copy(s, 4 + j, (*chip, c), sibling))
                passed[-1].start()
        for s in range(count):
            copy(s, 0, sibling, me).wait_recv()
            for j, chip in enumerate(chips):
                copy(s, 4 + j, (*chip, 1 - c), me).wait_recv()
        for cp in first + passed:
            cp.wait_send()
        for cp in mine:
            cp.wait()

    anywhere = pl.BlockSpec(memory_space=pl.ANY)
    return pl.pallas_call(
        body, name=f"gather_weights_{tag}",
        out_shape=[jax.ShapeDtypeStruct((N_DEV,) + b.shape, b.dtype) for b in blocks],
        in_specs=[anywhere] * count, out_specs=[anywhere] * count,
        scratch_shapes=[pltpu.SemaphoreType.DMA((count, N_DEV - 1)), pltpu.SemaphoreType.DMA((count, N_DEV - 1)),
                        pltpu.SemaphoreType.DMA((count,))],
    )(*blocks)


def _exchange_grads(gp, tag):
    def body(g_ref, out_ref, send_sems, recv_sems, local_sem):
        x, y, c = _my_place()
        me = 4 * x + 2 * y + c
        mine = pltpu.make_async_copy(g_ref.at[me], out_ref.at[me], local_sem)
        mine.start()
        copies = []
        for k, (fx, fy, fc) in enumerate(_RELATIONS):
            px, py, pc = _flip(x, fx), _flip(y, fy), _flip(c, fc)
            peer = 4 * px + 2 * py + pc
            copies.append((
                pltpu.make_async_remote_copy(
                    src_ref=g_ref.at[peer], dst_ref=out_ref.at[me], send_sem=send_sems.at[k], recv_sem=recv_sems.at[k],
                    device_id=(px, py, pc), device_id_type=pl.DeviceIdType.MESH),
                pltpu.make_async_remote_copy(
                    src_ref=g_ref.at[peer], dst_ref=out_ref.at[peer], send_sem=send_sems.at[k], recv_sem=recv_sems.at[k],
                    device_id=(px, py, pc), device_id_type=pl.DeviceIdType.MESH)))
        for out_cp, _ in copies:
            out_cp.start()
        for _, in_cp in copies:
            in_cp.wait_recv()
        for out_cp, _ in copies:
            out_cp.wait_send()
        mine.wait()

    return pl.pallas_call(
        body, name=f"exchange_grads_{tag}",
        out_shape=jax.ShapeDtypeStruct(gp.shape, gp.dtype),
        in_specs=[pl.BlockSpec(memory_space=pl.ANY)],
        out_specs=pl.BlockSpec(memory_space=pl.ANY),
        scratch_shapes=[pltpu.SemaphoreType.DMA((7,)), pltpu.SemaphoreType.DMA((7,)), pltpu.SemaphoreType.DMA(())],
    )(gp)


def _peers():
    x, y, c = _my_place()
    out = []
    for k, (fx, fy, fc) in enumerate(_RELATIONS):
        px, py, pc = _flip(x, fx), _flip(y, fy), _flip(c, fc)
        out.append((k, (px, py, pc), 4 * px + 2 * py + pc))
    return out, 4 * x + 2 * y + c


def _grid_ends(*grid):
    def first():
        return functools.reduce(lambda a, b: a & b, [pl.program_id(d) == 0 for d in range(len(grid))])

    def last():
        return functools.reduce(lambda a, b: a & b, [pl.program_id(d) == n - 1 for d, n in enumerate(grid)])

    return {"first": first, "last": last}


def _call(body, operands, *, comm=None, first=None, last=None, **kw):
    if comm is None:
        return pl.pallas_call(body, **kw)(*operands)
    in_specs, out_specs, out_shape = list(kw.pop("in_specs")), list(kw.pop("out_specs")), list(kw.pop("out_shape"))
    scratch = list(kw.pop("scratch_shapes", ()))
    n_in, n_out, n_scr, n_src = len(in_specs), len(out_specs), len(scratch), len(comm)

    def wrapped(*refs):
        ins, src_refs = refs[:n_in], refs[n_in:n_in + n_src]
        outs = refs[n_in + n_src:n_in + n_src + n_out]
        land_refs = refs[n_in + n_src + n_out:n_in + 2 * n_src + n_out]
        scr = refs[n_in + 2 * n_src + n_out:n_in + 2 * n_src + n_out + n_scr]
        send_sems, recv_sems, local_sems = refs[n_in + 2 * n_src + n_out + n_scr:]
        peers, me = _peers()
        mine, going, coming = [], [], []
        for s, (_, per_peer) in enumerate(comm):
            src_ref, land_ref = src_refs[s], land_refs[s]
            mine.append(pltpu.make_async_copy(src_ref.at[me] if per_peer else src_ref, land_ref.at[me], local_sems.at[s]))
            for k, where, slab in peers:
                piece = src_ref.at[slab] if per_peer else src_ref
                going.append(pltpu.make_async_remote_copy(
                    src_ref=piece, dst_ref=land_ref.at[me], send_sem=send_sems.at[s, k], recv_sem=recv_sems.at[s, k],
                    device_id=where, device_id_type=pl.DeviceIdType.MESH))
                coming.append(pltpu.make_async_remote_copy(
                    src_ref=piece, dst_ref=land_ref.at[slab], send_sem=send_sems.at[s, k], recv_sem=recv_sems.at[s, k],
                    device_id=where, device_id_type=pl.DeviceIdType.MESH))

        @pl.when(first())
        def _():
            for cp in mine + going:
                cp.start()

        body(*ins, *outs, *scr)

        @pl.when(last())
        def _():
            for cp in coming:
                cp.wait_recv()
            for cp in going:
                cp.wait_send()
            for cp in mine:
                cp.wait()

    anywhere = pl.BlockSpec(memory_space=pl.ANY)
    lands = [jax.ShapeDtypeStruct(src.shape if per_peer else (N_DEV,) + src.shape, src.dtype) for src, per_peer in comm]
    return pl.pallas_call(
        wrapped, in_specs=in_specs + [anywhere] * n_src, out_specs=out_specs + [anywhere] * n_src,
        out_shape=out_shape + lands,
        scratch_shapes=scratch + [pltpu.SemaphoreType.DMA((n_src, N_DEV - 1)), pltpu.SemaphoreType.DMA((n_src, N_DEV - 1)),
                                  pltpu.SemaphoreType.DMA((n_src,))],
        **kw)(*operands, *[src for src, _ in comm])


def _adamw(w, g, m, v):
    m = ADAM_B1 * m + (1.0 - ADAM_B1) * g
    v = ADAM_B2 * v + (1.0 - ADAM_B2) * jnp.square(g)
    m_hat = m / (1.0 - ADAM_B1 ** ADAM_STEP)
    v_hat = v / (1.0 - ADAM_B2 ** ADAM_STEP)
    delta = -ADAM_LR * (m_hat / (jnp.sqrt(v_hat) + ADAM_EPS) + ADAM_WD * w)
    return delta, m, v


def _sum_and_adamw(parts, w, m, v, tr, tag):
    rows = w.shape[0]
    assert rows % tr == 0

    def body(p_ref, w_ref, m_ref, v_ref, g_out, d_out, m_out, v_out):
        g = p_ref[0].astype(F32)
        for d in range(1, N_DEV):
            g = g + p_ref[d].astype(F32)
        delta, mn, vn = _adamw(w_ref[...], g, m_ref[...], v_ref[...])
        g_out[...] = g
        d_out[...] = delta
        m_out[...] = mn
        v_out[...] = vn

    sp = pl.BlockSpec((tr, D_MODEL), lambda i: (i, 0))
    return pl.pallas_call(
        body, name=f"sum_and_adamw_{tag}",
        grid=(rows // tr,),
        in_specs=[pl.BlockSpec((N_DEV, tr, D_MODEL), lambda i: (0, i, 0)), sp, sp, sp],
        out_specs=[sp] * 4,
        out_shape=[jax.ShapeDtypeStruct(w.shape, F32)] * 4,
        compiler_params=_cparams("parallel"),
    )(parts, w, m, v)


def _small_allreduce_adamw(part, w, m, v):
    def body(p_ref, w_ref, m_ref, v_ref, g_out, d_out, m_out, v_out, buf, send_sems, recv_sems):
        x, y, c = _my_place()
        me = 4 * x + 2 * y + c
        buf[me] = p_ref[...]
        copies = []
        for k, (fx, fy, fc) in enumerate(_RELATIONS):
            px, py, pc = _flip(x, fx), _flip(y, fy), _flip(c, fc)
            peer = 4 * px + 2 * py + pc
            copies.append((
                pltpu.make_async_remote_copy(
                    src_ref=buf.at[me], dst_ref=buf.at[me], send_sem=send_sems.at[k], recv_sem=recv_sems.at[k],
                    device_id=(px, py, pc), device_id_type=pl.DeviceIdType.MESH),
                pltpu.make_async_remote_copy(
                    src_ref=buf.at[me], dst_ref=buf.at[peer], send_sem=send_sems.at[k], recv_sem=recv_sems.at[k],
                    device_id=(px, py, pc), device_id_type=pl.DeviceIdType.MESH)))
        for out_cp, _ in copies:
            out_cp.start()
        for _, in_cp in copies:
            in_cp.wait_recv()
        for out_cp, _ in copies:
            out_cp.wait_send()
        g = buf[0]
        for d in range(1, N_DEV):
            g = g + buf[d]
        delta, mn, vn = _adamw(w_ref[...], g, m_ref[...], v_ref[...])
        g_out[...] = g
        d_out[...] = delta
        m_out[...] = mn
        v_out[...] = vn

    vm = pl.BlockSpec(memory_space=pltpu.VMEM)
    return pl.pallas_call(
        body, name="small_allreduce_adamw",
        in_specs=[vm] * 4, out_specs=[vm] * 4,
        out_shape=[jax.ShapeDtypeStruct(w.shape, F32)] * 4,
        scratch_shapes=[pltpu.VMEM((N_DEV,) + part.shape, F32),
                        pltpu.SemaphoreType.DMA((7,)), pltpu.SemaphoreType.DMA((7,))],
    )(part, w, m, v)


_TRANSPOSED = ("ffn1_w1", "ffn1_w3", "w_in", "ffn2_w1", "ffn2_w3")
_BRANCH = ("w_branch_swa", "w_branch_sb")


def _pack_shards(t, names):
    parts = []
    for name in names:
        a = t[name][0]
        if name in _TRANSPOSED:
            a = a.T
        elif name in _BRANCH:
            a = a.reshape(64, D_MODEL)
        parts.append(a)
    return jnp.concatenate(parts, axis=0)


def _unpack_shards(p, names):
    out, lo = {}, 0
    for name in names:
        a = p[lo:lo + BIG_ROWS[BIG_NAMES.index(name)]]
        lo += a.shape[0]
        if name in _TRANSPOSED:
            a = a.T
        elif name in _BRANCH:
            a = a.reshape(512, 128)
        out[name] = a[None]
    return out


def _full_weights(zones, names):
    out = {}
    for name, a in zip(names, zones):
        if name in _BRANCH:
            a = a.reshape(N_DEV, 512, 128).transpose(1, 0, 2).reshape(512, D_MODEL)
        out[_GRAD_KEY[name]] = a.reshape(-1, D_MODEL)
    return out


_GRAD_KEY = {"ffn1_w1": "ffn1_w1t", "ffn1_w3": "ffn1_w3t", "ffn1_w2": "ffn1_w2", "w_in": "w_int",
             "w_branch_swa": "w_swa", "w_branch_sb": "w_sb", "w_out": "w_out",
             "ffn2_w1": "ffn2_w1t", "ffn2_w3": "ffn2_w3t", "ffn2_w2": "ffn2_w2"}


def _pack_full_grads(big, names):
    parts = []
    for name in names:
        a = big[_GRAD_KEY[name]]
        if name in _BRANCH:
            a = a.reshape(512, N_DEV, 128).transpose(1, 0, 2)
        parts.append(a.reshape(N_DEV, BIG_ROWS[BIG_NAMES.index(name)], D_MODEL).astype(BF16))
    return jnp.concatenate(parts, axis=1)


_SMALL_NAMES = ("norm_ffn1", "norm_mix", "norm_ffn2", "norm_final", "swa_sinks", "rel_bias")


def _pack_small(vals):
    rows = []
    for a in vals:
        a = a.reshape(-1)
        rows.append(jnp.pad(a, (0, D_MODEL - a.shape[0])))
    rows += [jnp.zeros((D_MODEL,), F32)] * (SMALL_ROWS - len(rows))
    return jnp.stack(rows)


def _unpack_small(p):
    return {"norm_ffn1": p[0:1], "norm_mix": p[1:2], "norm_ffn2": p[2:3], "norm_final": p[3],
            "swa_sinks": p[4:5, :N_HEADS], "rel_bias": p[5, :REL_BUCKETS * N_HEADS].reshape(REL_BUCKETS, N_HEADS)}


ALL_NAMES = ("norm_ffn1", "ffn1_w1", "ffn1_w3", "ffn1_w2", "norm_mix", "w_in", "swa_sinks", "rel_bias",
             "w_branch_swa", "w_branch_sb", "w_out", "norm_ffn2", "ffn2_w1", "ffn2_w3", "ffn2_w2", "norm_final")


def kernel(x, norm_ffn1, ffn1_w1, ffn1_w3, ffn1_w2, norm_mix, w_in, swa_sinks, rel_bias, w_branch_swa, w_branch_sb, w_out, norm_ffn2, ffn2_w1, ffn2_w3, ffn2_w2, norm_final, loss_target, m_norm_ffn1, m_ffn1_w1, m_ffn1_w3, m_ffn1_w2, m_norm_mix, m_w_in, m_swa_sinks, m_rel_bias, m_w_branch_swa, m_w_branch_sb, m_w_out, m_norm_ffn2, m_ffn2_w1, m_ffn2_w3, m_ffn2_w2, m_norm_final, v_norm_ffn1, v_ffn1_w1, v_ffn1_w3, v_ffn1_w2, v_norm_mix, v_w_in, v_swa_sinks, v_rel_bias, v_w_branch_swa, v_w_branch_sb, v_w_out, v_norm_ffn2, v_ffn2_w1, v_ffn2_w3, v_ffn2_w2, v_norm_final):
    w = dict(zip(ALL_NAMES, (norm_ffn1, ffn1_w1, ffn1_w3, ffn1_w2, norm_mix, w_in, swa_sinks, rel_bias,
                             w_branch_swa, w_branch_sb, w_out, norm_ffn2, ffn2_w1, ffn2_w3, ffn2_w2, norm_final)))
    m = dict(zip(ALL_NAMES, (m_norm_ffn1, m_ffn1_w1, m_ffn1_w3, m_ffn1_w2, m_norm_mix, m_w_in, m_swa_sinks, m_rel_bias,
                             m_w_branch_swa, m_w_branch_sb, m_w_out, m_norm_ffn2, m_ffn2_w1, m_ffn2_w3, m_ffn2_w2,
                             m_norm_final)))
    v = dict(zip(ALL_NAMES, (v_norm_ffn1, v_ffn1_w1, v_ffn1_w3, v_ffn1_w2, v_norm_mix, v_w_in, v_swa_sinks, v_rel_bias,
                             v_w_branch_swa, v_w_branch_sb, v_w_out, v_norm_ffn2, v_ffn2_w1, v_ffn2_w3, v_ffn2_w2,
                             v_norm_final)))

    def my_blocks(group):
        return [_pack_shards(w, (name,)).astype(BF16) for name in GROUPS[group]]

    gathered0 = _gather_weights(my_blocks(0), "group0")

    def weights_of(group, landed):
        return _full_weights(gathered0 if group == 0 else landed, GROUPS[group])

    def ship(kind, which, grads=None):
        if kind == "weights":
            return [(block, False) for block in my_blocks(which)]
        return [(_pack_full_grads(grads, which), True)]

    gains = (norm_ffn1, norm_mix, norm_ffn2, norm_final.reshape(1, D_MODEL))
    loss, dx, small, parts = _local_step(x[0], loss_target[0], gains, swa_sinks, rel_bias, weights_of, ship)

    big_outs = [{}, {}, {}, {}]
    for names, tile in zip(SUM_GROUPS, SUM_TILE):
        landed = parts[names]
        if isinstance(landed, dict):
            landed = _exchange_grads(_pack_full_grads(landed, names), names[0])
        res = _sum_and_adamw(landed, _pack_shards(w, names), _pack_shards(m, names), _pack_shards(v, names),
                             tile, names[0])
        for acc, packed in zip(big_outs, res):
            acc.update(_unpack_shards(packed, names))
    g_big, d_big, m_big, v_big = big_outs

    small_part = _pack_small(small["gains"] + (small["sinks"], small["rel_bias"], loss))
    zero = jnp.zeros((1,), F32)
    small_res = _small_allreduce_adamw(
        small_part, _pack_small([w[n] for n in _SMALL_NAMES] + [zero]), _pack_small([m[n] for n in _SMALL_NAMES] + [zero]),
        _pack_small([v[n] for n in _SMALL_NAMES] + [zero]))
    g_sm, d_sm, m_sm, v_sm = (_unpack_small(p) for p in small_res)

    outs = [small_res[0][len(_SMALL_NAMES), 0], dx[None]]
    for big_d, small_d in ((g_big, g_sm), (d_big, d_sm), (m_big, m_sm), (v_big, v_sm)):
        merged = {**big_d, **small_d}
        outs += [merged[n] for n in ALL_NAMES]
    return tuple(outs)
```

```python
import functools

import jax
import jax.numpy as jnp
import numpy as np
from jax import lax
from jax.experimental import pallas as pl
from jax.experimental.pallas import tpu as pltpu

F32 = jnp.float32
BF16 = jnp.bfloat16

D_MODEL = 1024
D_FF = 2816
HEAD_DIM = 64
N_HEADS = 8
SWA_KV_HEADS = 2
SWA_GROUP = 4
SWA_BLOCK = 128
REL_BUCKETS = 32
REL_MAX_DIST = 128
RMS_EPS = 1e-6
NEG_BIG = -1e30
Q_SCALE = HEAD_DIM ** -0.5
LANES = 128

N_DEV = 8

ADAM_LR = 0.001
ADAM_B1 = 0.9
ADAM_B2 = 0.999
ADAM_EPS = 1e-08
ADAM_WD = 0.01
ADAM_STEP = 10

IN_SIZES = (512, 128, 128, 512, 512, 512, 1024, 1024)
IN_OFFS = tuple(int(v) for v in np.cumsum((0,) + IN_SIZES))
IN_W = IN_OFFS[-1]

BIG_NAMES = ("ffn1_w1", "ffn1_w3", "ffn1_w2", "w_in", "w_branch_swa", "w_branch_sb", "w_out",
             "ffn2_w1", "ffn2_w3", "ffn2_w2")
BIG_ROWS = (352, 352, 352, 544, 64, 64, 128, 352, 352, 352)
SMALL_ROWS = 8
GROUPS = (BIG_NAMES[0:3], BIG_NAMES[3:7], BIG_NAMES[7:10])
SUM_GROUPS = tuple((n,) for n in GROUPS[0]) + GROUPS[1:]
SUM_TILE = (176, 176, 176, 160, 96)

VMEM_LIMIT = 56 * 1024 * 1024
FFN_PIECES = 2
SB_QUERIES = 512
SB_KEYS = 256
SB_ROWS = 256
SB_SUM_PARTS = 1
SB_LOGIT_CAP = 80.0
SB_DEAD_CARRY = -110.0


def _dot(a, b):
    return jnp.dot(a, b, preferred_element_type=F32)


def _dot_nt(a, b):
    return lax.dot_general(a, b, (((1,), (1,)), ((), ())), preferred_element_type=F32)


def _dot_tn(a, b):
    return lax.dot_general(a, b, (((0,), (0,)), ((), ())), preferred_element_type=F32)


def _cparams(*sem):
    return pltpu.CompilerParams(dimension_semantics=sem, vmem_limit_bytes=VMEM_LIMIT)


def _rms_rstd(xv):
    return lax.rsqrt(jnp.mean(xv * xv, axis=-1, keepdims=True) + RMS_EPS)


def _rms_bwd(dh, xv, r, g):
    xhat = xv * r
    dg = jnp.sum(dh * xhat, axis=0, keepdims=True)
    dxn = dh * g
    dx = r * (dxn - xhat * jnp.mean(dxn * xhat, axis=-1, keepdims=True))
    return dx, dg


def _ffn_fwd(x, g, w1t, w3t, w2, tag, comm=None):
    s_len = x.shape[0]
    tm, tf = min(1024, s_len), 256
    nf = D_FF // tf

    def body(x_ref, g_ref, w1_ref, w3_ref, w2_ref, xo_ref, h_ref, a_ref, b_ref, u_ref, acc_ref, hs_ref):
        j = pl.program_id(1)

        @pl.when(j == 0)
        def _():
            xv = x_ref[...]
            h = (xv * _rms_rstd(xv) * g_ref[...]).astype(BF16)
            hs_ref[...] = h
            h_ref[...] = h
            acc_ref[...] = jnp.zeros_like(acc_ref)

        st = {}

        def s_up(rs):
            h = hs_ref[rs, :]
            st[rs.start, "ab"] = (_dot_nt(h, w1_ref[...]), _dot_nt(h, w3_ref[...]))

        def s_act(rs):
            a, b = st.pop((rs.start, "ab"))
            a_ref[rs, :] = a.astype(BF16)
            b_ref[rs, :] = b.astype(BF16)
            uh = (0.5 * (a * jax.nn.sigmoid(a) * b)).astype(BF16)
            u_ref[rs, :] = uh
            st[rs.start, "u"] = uh

        def s_down(rs):
            acc_ref[rs, :] += _dot(st.pop((rs.start, "u")), w2_ref[...])

        _emit_skewed(([slice(r, r + tm // FFN_PIECES) for r in range(0, tm, tm // FFN_PIECES)], [s_up, s_act, s_down]))

        @pl.when(j == nf - 1)
        def _():
            xo_ref[...] = x_ref[...] + acc_ref[...]

    row = lambda i, j: (i, 0)
    return _call(
        body, (x, g, w1t, w3t, w2), comm=comm, **_grid_ends(s_len // tm, nf), name=f"ffn_fwd_{tag}",
        grid=(s_len // tm, nf),
        in_specs=[pl.BlockSpec((tm, D_MODEL), row), pl.BlockSpec((1, D_MODEL), lambda i, j: (0, 0)),
                  pl.BlockSpec((tf, D_MODEL), lambda i, j: (j, 0)), pl.BlockSpec((tf, D_MODEL), lambda i, j: (j, 0)),
                  pl.BlockSpec((tf, D_MODEL), lambda i, j: (j, 0))],
        out_specs=[pl.BlockSpec((tm, D_MODEL), row), pl.BlockSpec((tm, D_MODEL), row),
                   pl.BlockSpec((tm, tf), lambda i, j: (i, j)), pl.BlockSpec((tm, tf), lambda i, j: (i, j)),
                   pl.BlockSpec((tm, tf), lambda i, j: (i, j))],
        out_shape=[jax.ShapeDtypeStruct((s_len, D_MODEL), F32), jax.ShapeDtypeStruct((s_len, D_MODEL), BF16),
                   jax.ShapeDtypeStruct((s_len, D_FF), BF16), jax.ShapeDtypeStruct((s_len, D_FF), BF16),
                   jax.ShapeDtypeStruct((s_len, D_FF), BF16)],
        scratch_shapes=[pltpu.VMEM((tm, D_MODEL), F32), pltpu.VMEM((tm, D_MODEL), BF16)],
        compiler_params=_cparams("arbitrary", "arbitrary"),
    )


def _ffn_bwd(dy, x, g, a, b, w1t, w3t, w2, tag, comm=None):
    s_len = x.shape[0]
    tm, tf = min(1024, s_len), 256
    nf = D_FF // tf

    def body(dy_ref, x_ref, g_ref, a_ref, b_ref, w1_ref, w3_ref, w2_ref,
             dx_ref, dg_ref, da_ref, db_ref, dyb_ref, acc_ref, dys_ref):
        i, j = pl.program_id(0), pl.program_id(1)

        @pl.when(j == 0)
        def _():
            dyb = dy_ref[...].astype(BF16)
            dys_ref[...] = dyb
            dyb_ref[...] = dyb
            acc_ref[...] = jnp.zeros_like(acc_ref)

        @pl.when((i == 0) & (j == 0))
        def _():
            dg_ref[...] = jnp.zeros_like(dg_ref)

        st = {}

        def s_du(rs):
            st[rs.start, "du"] = 0.5 * _dot_nt(dys_ref[rs, :], w2_ref[...])

        def s_act(rs):
            du = st.pop((rs.start, "du"))
            av = a_ref[rs, :].astype(F32)
            bv = b_ref[rs, :].astype(F32)
            sg = jax.nn.sigmoid(av)
            sil = av * sg
            da = (du * bv * (sg + sil * (1.0 - sg))).astype(BF16)
            db = (du * sil).astype(BF16)
            da_ref[rs, :] = da
            db_ref[rs, :] = db
            st[rs.start, "dab"] = (da, db)

        def s_dh(rs):
            da, db = st.pop((rs.start, "dab"))
            acc_ref[rs, :] += _dot(da, w1_ref[...]) + _dot(db, w3_ref[...])

        _emit_skewed(([slice(r, r + tm // FFN_PIECES) for r in range(0, tm, tm // FFN_PIECES)], [s_du, s_act, s_dh]))

        @pl.when(j == nf - 1)
        def _():
            xv = x_ref[...]
            dx, dg = _rms_bwd(acc_ref[...], xv, _rms_rstd(xv), g_ref[...])
            dx_ref[...] = dy_ref[...] + dx
            dg_ref[...] += dg

    row = lambda i, j: (i, 0)
    blk = lambda i, j: (i, j)
    wsp = pl.BlockSpec((tf, D_MODEL), lambda i, j: (j, 0))
    return _call(
        body, (dy, x, g, a, b, w1t, w3t, w2), comm=comm, **_grid_ends(s_len // tm, nf), name=f"ffn_bwd_{tag}",
        grid=(s_len // tm, nf),
        in_specs=[pl.BlockSpec((tm, D_MODEL), row), pl.BlockSpec((tm, D_MODEL), row),
                  pl.BlockSpec((1, D_MODEL), lambda i, j: (0, 0)),
                  pl.BlockSpec((tm, tf), blk), pl.BlockSpec((tm, tf), blk), wsp, wsp, wsp],
        out_specs=[pl.BlockSpec((tm, D_MODEL), row), pl.BlockSpec((1, D_MODEL), lambda i, j: (0, 0)),
                   pl.BlockSpec((tm, tf), blk), pl.BlockSpec((tm, tf), blk), pl.BlockSpec((tm, D_MODEL), row)],
        out_shape=[jax.ShapeDtypeStruct((s_len, D_MODEL), F32), jax.ShapeDtypeStruct((1, D_MODEL), F32),
                   jax.ShapeDtypeStruct((s_len, D_FF), BF16), jax.ShapeDtypeStruct((s_len, D_FF), BF16),
                   jax.ShapeDtypeStruct((s_len, D_MODEL), BF16)],
        scratch_shapes=[pltpu.VMEM((tm, D_MODEL), F32), pltpu.VMEM((tm, D_MODEL), BF16)],
        compiler_params=_cparams("arbitrary", "arbitrary"),
    )


def _matmul_tn(lhs, rhs, tag, comm=None):
    s_len, m = lhs.shape
    n = rhs.shape[1]
    tm = min(512, s_len)
    tj = m if m <= 1024 else 1408
    assert m % tj == 0
    last_rows = s_len // tm - 1

    def body(l_ref, r_ref, o_ref, acc_ref):
        i = pl.program_id(1)

        @pl.when(i == 0)
        def _():
            acc_ref[...] = jnp.zeros_like(acc_ref)

        acc_ref[...] += _dot_tn(l_ref[...], r_ref[...])

        @pl.when(i == last_rows)
        def _():
            o_ref[...] = acc_ref[...].astype(BF16)

    res = _call(
        body, (lhs, rhs), comm=comm, **_grid_ends(m // tj, s_len // tm), name=f"matmul_tn_{tag}",
        grid=(m // tj, s_len // tm),
        in_specs=[pl.BlockSpec((tm, tj), lambda j, i: (i, j)), pl.BlockSpec((tm, n), lambda j, i: (i, 0))],
        out_specs=[pl.BlockSpec((tj, n), lambda j, i: (j, 0))],
        out_shape=[jax.ShapeDtypeStruct((m, n), BF16)],
        scratch_shapes=[pltpu.VMEM((tj, n), F32)],
        compiler_params=_cparams("arbitrary", "arbitrary"),
    )
    return res[0] if comm is None else tuple(res)


def _proj_fwd(x1, g, wint):
    s_len = x1.shape[0]
    tm = min(512, s_len)
    dts = (BF16, BF16, BF16, BF16, BF16, BF16, F32, F32)

    def body(x_ref, g_ref, w_ref, h_ref, *outs):
        xv = x_ref[...]
        h = (xv * _rms_rstd(xv) * g_ref[...]).astype(BF16)
        h_ref[...] = h
        for p, o_ref in enumerate(outs):
            val = _dot_nt(h, w_ref[IN_OFFS[p]:IN_OFFS[p + 1], :])
            if p == 3:
                val = val * Q_SCALE
            o_ref[...] = val.astype(dts[p])

    row = lambda i: (i, 0)
    return pl.pallas_call(
        body, name="proj_fwd",
        grid=(s_len // tm,),
        in_specs=[pl.BlockSpec((tm, D_MODEL), row), pl.BlockSpec((1, D_MODEL), lambda i: (0, 0)),
                  pl.BlockSpec((IN_W, D_MODEL), lambda i: (0, 0))],
        out_specs=[pl.BlockSpec((tm, D_MODEL), row)] + [pl.BlockSpec((tm, w), row) for w in IN_SIZES],
        out_shape=[jax.ShapeDtypeStruct((s_len, D_MODEL), BF16)]
        + [jax.ShapeDtypeStruct((s_len, w), dt) for w, dt in zip(IN_SIZES, dts)],
        compiler_params=_cparams("parallel"),
    )(x1, g, wint)


def _proj_bwd(dpieces, dx2, x1, g, wint):
    s_len = x1.shape[0]
    tm = min(512, s_len)

    def body(*refs):
        dps = refs[:8]
        dx2_ref, x_ref, g_ref, w_ref, dx_ref, dg_ref = refs[8:]

        @pl.when(pl.program_id(0) == 0)
        def _():
            dg_ref[...] = jnp.zeros_like(dg_ref)

        dh = _dot(dps[0][...], w_ref[IN_OFFS[0]:IN_OFFS[1], :])
        for p in range(1, 8):
            dh += _dot(dps[p][...], w_ref[IN_OFFS[p]:IN_OFFS[p + 1], :])
        xv = x_ref[...]
        dx, dg = _rms_bwd(dh, xv, _rms_rstd(xv), g_ref[...])
        dx_ref[...] = dx2_ref[...] + dx
        dg_ref[...] += dg

    row = lambda i: (i, 0)
    return pl.pallas_call(
        body, name="proj_bwd",
        grid=(s_len // tm,),
        in_specs=[pl.BlockSpec((tm, w), row) for w in IN_SIZES]
        + [pl.BlockSpec((tm, D_MODEL), row), pl.BlockSpec((tm, D_MODEL), row),
           pl.BlockSpec((1, D_MODEL), lambda i: (0, 0)), pl.BlockSpec((IN_W, D_MODEL), lambda i: (0, 0))],
        out_specs=[pl.BlockSpec((tm, D_MODEL), row), pl.BlockSpec((1, D_MODEL), lambda i: (0, 0))],
        out_shape=[jax.ShapeDtypeStruct((s_len, D_MODEL), F32), jax.ShapeDtypeStruct((1, D_MODEL), F32)],
        compiler_params=_cparams("arbitrary"),
    )(*dpieces, dx2, x1, g, wint)


def _merge_fwd(x1, oa, ob, ga, gb, wswa, wsb, wout):
    s_len = x1.shape[0]
    tm = min(512, s_len)

    def body(x_ref, oa_ref, ob_ref, ga_ref, gb_ref, wa_ref, wb_ref, wo_ref, xo_ref, mg_ref):
        pa = _dot(oa_ref[...], wa_ref[...])
        pb = _dot(ob_ref[...], wb_ref[...])
        mg = (jax.nn.sigmoid(ga_ref[...]) * pa + jax.nn.sigmoid(gb_ref[...]) * pb).astype(BF16)
        mg_ref[...] = mg
        xo_ref[...] = x_ref[...] + _dot(mg, wo_ref[...])

    row = lambda i: (i, 0)
    full = lambda i: (0, 0)
    return pl.pallas_call(
        body, name="merge_fwd",
        grid=(s_len // tm,),
        in_specs=[pl.BlockSpec((tm, D_MODEL), row), pl.BlockSpec((tm, 512), row), pl.BlockSpec((tm, 512), row),
                  pl.BlockSpec((tm, D_MODEL), row), pl.BlockSpec((tm, D_MODEL), row),
                  pl.BlockSpec((512, D_MODEL), full), pl.BlockSpec((512, D_MODEL), full),
                  pl.BlockSpec((D_MODEL, D_MODEL), full)],
        out_specs=[pl.BlockSpec((tm, D_MODEL), row), pl.BlockSpec((tm, D_MODEL), row)],
        out_shape=[jax.ShapeDtypeStruct((s_len, D_MODEL), F32), jax.ShapeDtypeStruct((s_len, D_MODEL), BF16)],
        compiler_params=_cparams("parallel"),
    )(x1, oa, ob, ga, gb, wswa, wsb, wout)


def _merge_bwd(dx2, oa, ob, ga, gb, wswa, wsb, wout):
    s_len = dx2.shape[0]
    tm = min(512, s_len)

    def body(dx_ref, oa_ref, ob_ref, ga_ref, gb_ref, wa_ref, wb_ref, wo_ref,
             doa_ref, dob_ref, dga_ref, dgb_ref, dpa_ref, dpb_ref, dxb_ref):
        dxb = dx_ref[...].astype(BF16)
        dxb_ref[...] = dxb
        dmg = _dot_nt(dxb, wo_ref[...])
        for o_ref, g_ref, w_ref, do_ref, dg_ref, dp_ref in (
                (oa_ref, ga_ref, wa_ref, doa_ref, dga_ref, dpa_ref),
                (ob_ref, gb_ref, wb_ref, dob_ref, dgb_ref, dpb_ref)):
            pv = _dot(o_ref[...], w_ref[...])
            sg = jax.nn.sigmoid(g_ref[...])
            dp = (dmg * sg).astype(BF16)
            dp_ref[...] = dp
            dg_ref[...] = (dmg * pv * sg * (1.0 - sg)).astype(BF16)
            do_ref[...] = _dot_nt(dp, w_ref[...]).astype(BF16)

    row = lambda i: (i, 0)
    full = lambda i: (0, 0)
    wide = pl.BlockSpec((tm, D_MODEL), row)
    half = pl.BlockSpec((tm, 512), row)
    return pl.pallas_call(
        body, name="merge_bwd",
        grid=(s_len // tm,),
        in_specs=[wide, half, half, wide, wide, pl.BlockSpec((512, D_MODEL), full),
                  pl.BlockSpec((512, D_MODEL), full), pl.BlockSpec((D_MODEL, D_MODEL), full)],
        out_specs=[half, half, wide, wide, wide, wide, wide],
        out_shape=[jax.ShapeDtypeStruct((s_len, 512), BF16)] * 2 + [jax.ShapeDtypeStruct((s_len, D_MODEL), BF16)] * 5,
        compiler_params=_cparams("parallel"),
    )(dx2, oa, ob, ga, gb, wswa, wsb, wout)


def _loss_fwd_bwd(x3, tgt, g):
    s_len = x3.shape[0]
    tm = min(1024, s_len)

    def body(x_ref, t_ref, g_ref, dx_ref, loss_ref, dg_ref):
        @pl.when(pl.program_id(0) == 0)
        def _():
            loss_ref[...] = jnp.zeros_like(loss_ref)
            dg_ref[...] = jnp.zeros_like(dg_ref)

        xv = x_ref[...]
        gv = g_ref[...]
        r = _rms_rstd(xv)
        err = xv * r * gv - t_ref[...]
        loss_ref[...] += 0.5 * jnp.sum(jnp.mean(err * err, axis=-1, keepdims=True), axis=0, keepdims=True)
        dx, dg = _rms_bwd(err * (1.0 / D_MODEL), xv, r, gv)
        dx_ref[...] = dx
        dg_ref[...] += dg

    row = lambda i: (i, 0)
    return pl.pallas_call(
        body, name="loss_fwd_bwd",
        grid=(s_len // tm,),
        in_specs=[pl.BlockSpec((tm, D_MODEL), row), pl.BlockSpec((tm, D_MODEL), row),
                  pl.BlockSpec((1, D_MODEL), lambda i: (0, 0))],
        out_specs=[pl.BlockSpec((tm, D_MODEL), row), pl.BlockSpec((1, 1), lambda i: (0, 0)),
                   pl.BlockSpec((1, D_MODEL), lambda i: (0, 0))],
        out_shape=[jax.ShapeDtypeStruct((s_len, D_MODEL), F32), jax.ShapeDtypeStruct((1, 1), F32),
                   jax.ShapeDtypeStruct((1, D_MODEL), F32)],
        compiler_params=_cparams("arbitrary"),
    )(x3, tgt, g)


def _rel_bucket_matrix():
    qi = jnp.arange(SWA_BLOCK)[:, None] + SWA_BLOCK
    kj = jnp.arange(2 * SWA_BLOCK)[None, :]
    dist = jnp.maximum(qi - kj, 0)
    max_exact = REL_BUCKETS // 2
    d = jnp.maximum(dist, 1).astype(F32)
    large = max_exact + (jnp.log(d / max_exact) / np.log(REL_MAX_DIST / max_exact)
                         * (REL_BUCKETS - max_exact)).astype(jnp.int32)
    large = jnp.minimum(large, REL_BUCKETS - 1)
    return jnp.where(dist < max_exact, dist, large).astype(jnp.int32)


def _swa_bias_into(bias_ref, bkt_ref, tab_ref):
    bk = bkt_ref[...]
    for h in range(N_HEADS):
        acc = jnp.zeros(bk.shape, F32)
        for bucket in range(REL_BUCKETS):
            acc = jnp.where(bk == bucket, tab_ref[bucket, h], acc)
        bias_ref[h] = acc


def _swa_valid(n):
    shape = (SWA_BLOCK, 2 * SWA_BLOCK)
    row = lax.broadcasted_iota(jnp.int32, shape, 0)
    col = lax.broadcasted_iota(jnp.int32, shape, 1)
    dist = row + SWA_BLOCK - col
    return (dist >= 0) & (dist < SWA_BLOCK) & ((col >= SWA_BLOCK) | (n > 0))


def _swa_windows(kp_ref, kc_ref, vp_ref, vc_ref):
    lanes = [slice(g * LANES, (g + 1) * LANES) for g in range(SWA_KV_HEADS)]
    return ([jnp.concatenate([kp_ref[:, gl], kc_ref[:, gl]], axis=0) for gl in lanes],
            [jnp.concatenate([vp_ref[:, gl], vc_ref[:, gl]], axis=0) for gl in lanes])


def _swa_probs(qk, bias, sink, valid):
    lg = jnp.where(valid, qk * Q_SCALE + bias, NEG_BIG)
    m = jnp.maximum(jnp.max(lg, axis=-1, keepdims=True), sink)
    e = jnp.exp(lg - m)
    es = jnp.exp(sink - m)
    inv = 1.0 / (jnp.sum(e, axis=-1, keepdims=True) + es)
    return e * inv, es * inv


def _swa_specs(s_len):
    blk = SWA_BLOCK
    cur = lambda n: (n, 0)
    prev = lambda n: (jnp.maximum(n - 1, 0), 0)
    kvw = SWA_KV_HEADS * LANES
    return [pl.BlockSpec(memory_space=pltpu.SMEM), pl.BlockSpec(memory_space=pltpu.SMEM),
            pl.BlockSpec((blk, 2 * blk), lambda n: (0, 0)),
            pl.BlockSpec((blk, N_HEADS * LANES), cur),
            pl.BlockSpec((blk, kvw), prev), pl.BlockSpec((blk, kvw), cur),
            pl.BlockSpec((blk, kvw), prev), pl.BlockSpec((blk, kvw), cur)]


def _swa_fwd(tab, sinks, bkt, q, k, v):
    s_len = q.shape[0]
    blk = SWA_BLOCK

    def body(tab_ref, sink_ref, bkt_ref, q_ref, kp_ref, kc_ref, vp_ref, vc_ref, o_ref, bias_ref):
        n = pl.program_id(0)

        @pl.when(n == 0)
        def _():
            _swa_bias_into(bias_ref, bkt_ref, tab_ref)

        valid = _swa_valid(n)
        kk, vv = _swa_windows(kp_ref, kc_ref, vp_ref, vc_ref)
        st = {}

        def s_logits(h):
            st[h, "lg"] = _dot_nt(q_ref[:, h * LANES:(h + 1) * LANES], kk[h // SWA_GROUP])

        def s_probs(h):
            st[h, "p"] = _swa_probs(st.pop((h, "lg")), bias_ref[h], sink_ref[0, h], valid)[0].astype(BF16)

        def s_values(h):
            o_ref[:, h * LANES:(h + 1) * LANES] = _dot(st.pop((h, "p")), vv[h // SWA_GROUP]).astype(BF16)

        _emit_skewed((list(range(N_HEADS)), [s_logits, s_probs, s_values]))

    return pl.pallas_call(
        body, name="swa_fwd",
        grid=(s_len // blk,),
        in_specs=_swa_specs(s_len),
        out_specs=pl.BlockSpec((blk, N_HEADS * LANES), lambda n: (n, 0)),
        out_shape=jax.ShapeDtypeStruct((s_len, N_HEADS * LANES), BF16),
        scratch_shapes=[pltpu.VMEM((N_HEADS, blk, 2 * blk), F32)],
        compiler_params=_cparams("arbitrary"),
    )(tab, sinks, bkt, q, k, k, v, v)


def _swa_bwd(tab, sinks, bkt, q, k, v, do, comm=None):
    s_len = q.shape[0]
    blk = SWA_BLOCK
    nb = s_len // blk
    kvw = SWA_KV_HEADS * LANES

    def body(tab_ref, sink_ref, bkt_ref, q_ref, kp_ref, kc_ref, vp_ref, vc_ref, do_ref,
             dq_ref, dk_ref, dv_ref, dtab_ref, dsink_ref, bias_ref, dbias_ref):
        n = pl.program_id(0)

        @pl.when(n == 0)
        def _():
            _swa_bias_into(bias_ref, bkt_ref, tab_ref)
            dbias_ref[...] = jnp.zeros_like(dbias_ref)
            dk_ref[...] = jnp.zeros_like(dk_ref)
            dv_ref[...] = jnp.zeros_like(dv_ref)
            dsink_ref[...] = jnp.zeros_like(dsink_ref)
            dtab_ref[...] = jnp.zeros_like(dtab_ref)

        valid = _swa_valid(n)
        cur_rows = pl.ds(pl.multiple_of(n * blk, blk), blk)
        prev_rows = pl.ds(pl.multiple_of(jnp.maximum(n - 1, 0) * blk, blk), blk)
        kk, vv = _swa_windows(kp_ref, kc_ref, vp_ref, vc_ref)
        st = {}

        def s_logits(h):
            hl = slice(h * LANES, (h + 1) * LANES)
            st[h, "lg"] = _dot_nt(q_ref[:, hl], kk[h // SWA_GROUP])
            st[h, "dp"] = _dot_nt(do_ref[:, hl], vv[h // SWA_GROUP])

        def s_probs(h):
            p, ps = _swa_probs(st.pop((h, "lg")), bias_ref[h], sink_ref[0, h], valid)
            dp = st.pop((h, "dp"))
            delta = jnp.sum(p * dp, axis=-1, keepdims=True)
            dl = p * (dp - delta)
            dsink_ref[h:h + 1, :] += jnp.broadcast_to(-jnp.sum(ps * delta, axis=0, keepdims=True), (1, LANES))
            dbias_ref[h] += dl
            st[h, "dl"], st[h, "p"] = dl.astype(BF16), p.astype(BF16)

        def s_products(h):
            hl = slice(h * LANES, (h + 1) * LANES)
            gl = slice(h // SWA_GROUP * LANES, (h // SWA_GROUP + 1) * LANES)
            dlb = st.pop((h, "dl"))
            dq_ref[:, hl] = (Q_SCALE * _dot(dlb, kk[h // SWA_GROUP])).astype(BF16)
            dk_win = Q_SCALE * _dot_tn(dlb, q_ref[:, hl])
            dv_win = _dot_tn(st.pop((h, "p")), do_ref[:, hl])
            dk_ref[prev_rows, gl] += dk_win[:blk]
            dv_ref[prev_rows, gl] += dv_win[:blk]
            dk_ref[cur_rows, gl] += dk_win[blk:]
            dv_ref[cur_rows, gl] += dv_win[blk:]

        _emit_skewed((list(range(N_HEADS)), [s_logits, s_probs, s_products]))

        @pl.when(n == nb - 1)
        def _():
            bk = bkt_ref[...]
            lane = lax.broadcasted_iota(jnp.int32, (1, LANES), 1)
            for bucket in range(REL_BUCKETS):
                rowv = jnp.zeros((1, LANES), F32)
                for h in range(N_HEADS):
                    val = jnp.sum(jnp.where(bk == bucket, dbias_ref[h], 0.0), axis=1, keepdims=True)
                    val = jnp.sum(val, axis=0, keepdims=True)
                    rowv = jnp.where(lane == h, val, rowv)
                dtab_ref[bucket:bucket + 1, :] = rowv

    return _call(
        body, (tab, sinks, bkt, q, k, k, v, v, do), comm=comm, **_grid_ends(nb), name="swa_bwd",
        grid=(nb,),
        in_specs=_swa_specs(s_len) + [pl.BlockSpec((blk, N_HEADS * LANES), lambda n: (n, 0))],
        out_specs=[pl.BlockSpec((blk, N_HEADS * LANES), lambda n: (n, 0)),
                   pl.BlockSpec((s_len, kvw), lambda n: (0, 0)), pl.BlockSpec((s_len, kvw), lambda n: (0, 0)),
                   pl.BlockSpec((REL_BUCKETS, LANES), lambda n: (0, 0)), pl.BlockSpec((N_HEADS, LANES), lambda n: (0, 0))],
        out_shape=[jax.ShapeDtypeStruct((s_len, N_HEADS * LANES), BF16),
                   jax.ShapeDtypeStruct((s_len, kvw), F32), jax.ShapeDtypeStruct((s_len, kvw), F32),
                   jax.ShapeDtypeStruct((REL_BUCKETS, LANES), F32), jax.ShapeDtypeStruct((N_HEADS, LANES), F32)],
        scratch_shapes=[pltpu.VMEM((N_HEADS, blk, 2 * blk), F32), pltpu.VMEM((N_HEADS, blk, 2 * blk), F32)],
        compiler_params=_cparams("arbitrary"),
    )


def _sb_terms(z, valid):
    zc = jnp.minimum(z, SB_LOGIT_CAP)
    lk = -jnp.log(1.0 + jnp.exp(zc))
    lsz = zc + lk
    return lsz, (lk if valid is None else jnp.where(valid, lk, 0.0))


def _bf16_parts(vals):
    parts, rest = [], vals
    for n in range(SB_SUM_PARTS):
        parts.append(rest.astype(BF16))
        if n + 1 < SB_SUM_PARTS:
            rest = rest - parts[-1].astype(F32)
    return parts[0] if len(parts) == 1 else jnp.concatenate(parts, axis=1)


def _row_sum_lanes(vals):
    return jnp.broadcast_to(jnp.sum(vals, axis=-1, keepdims=True), (vals.shape[0], LANES))


def _emit_skewed(*groups):
    for step in range(max(len(items) + len(stages) - 1 for items, stages in groups)):
        for items, stages in groups:
            for s, stage in enumerate(stages):
                if 0 <= step - s < len(items):
                    stage(items[step - s])


def _sb_items(edge):
    items = []
    for h in range(2):
        for r0 in range(0, SB_QUERIES, SB_ROWS):
            if edge is None or r0 >= (edge + 1) * SB_KEYS:
                items.append((h, r0, False))
            elif r0 + SB_ROWS - 1 > edge * SB_KEYS:
                items.append((h, r0, True))
    return items


def _sb_valid(w, edge):
    row = lax.broadcasted_iota(jnp.int32, (SB_ROWS, SB_KEYS), 0) + w[1]
    col = lax.broadcasted_iota(jnp.int32, (SB_ROWS, SB_KEYS), 1) + edge * SB_KEYS
    return col < row


def _sb_consts(tq, tk):
    low = lax.broadcasted_iota(jnp.int32, (tq, LANES), 1) < HEAD_DIM
    row = lax.broadcasted_iota(jnp.int32, (tk, tk), 0)
    col = lax.broadcasted_iota(jnp.int32, (tk, tk), 1)
    right = (row > col).astype(BF16)
    left = (row < col).astype(BF16)
    return low, jnp.concatenate([right] * SB_SUM_PARTS, axis=0), jnp.concatenate([left] * SB_SUM_PARTS, axis=0)


def _sb_fwd(q, k, v, comm=None):
    s_len = q.shape[0]
    tq, tk, tr = SB_QUERIES, SB_KEYS, SB_ROWS
    nk, ratio = s_len // tk, tq // tk
    assert nk <= LANES

    def body(q_ref, k_ref, v_ref, o_ref, car_ref, c_ref, oacc_ref, logw_ref, lksum_ref):
        i = pl.program_id(1)
        qv = q_ref[...]
        low, tri2, _ = _sb_consts(tq, tk)
        lane = lax.broadcasted_iota(jnp.int32, (tr, LANES), 1)
        zero = jnp.zeros_like(qv)
        q_heads = (jnp.where(low, qv, zero), jnp.where(low, zero, qv))
        c_ref[...] = jnp.zeros_like(c_ref)
        oacc_ref[...] = jnp.zeros_like(oacc_ref)
        car_ref[...] = jnp.full_like(car_ref, NEG_BIG)

        def front(j, edge):
            keys = k_ref[pl.ds(pl.multiple_of(j * tk, tk), tk), :]
            slot = j % 2
            st = {}

            def s_logits(w):
                st[w, "z"] = _dot_nt(q_heads[w[0]][w[1]:w[1] + tr], keys)

            def s_terms(w):
                valid = _sb_valid(w, edge) if w[2] else None
                lsz, lk = _sb_terms(st.pop((w, "z")), valid)
                st[w, "parts"] = _bf16_parts(lk)
                st[w, "lsz"] = lsz if valid is None else jnp.where(valid, lsz, NEG_BIG)
                lksum_ref[slot, w[0], w[1]:w[1] + tr, :] = _row_sum_lanes(lk)

            def s_suffix(w):
                logw_ref[slot, w[0], w[1]:w[1] + tr, :] = st.pop((w, "lsz")) + _dot(st.pop((w, "parts")), tri2)

            return _sb_items(edge), [s_logits, s_terms, s_suffix]

        def back(j, edge):
            vv = v_ref[pl.ds(pl.multiple_of(j * tk, tk), tk), :]
            slot = j % 2
            st = {}

            def s_weights(w):
                h, rs = w[0], slice(w[1], w[1] + tr)
                c = c_ref[h, rs, :]
                st[w, "a"] = jnp.exp(logw_ref[slot, h, rs, :] + jnp.tile(c, (1, tk // LANES))).astype(BF16)
                car_ref[h, rs, :] = jnp.where(lane == j, c, car_ref[h, rs, :])
                c_ref[h, rs, :] = c + lksum_ref[slot, h, rs, :]

            def s_values(w):
                oacc_ref[w[0], w[1]:w[1] + tr, :] += _dot(st.pop((w, "a")), vv)

            return _sb_items(edge), [s_weights, s_values]

        first = i * ratio
        _emit_skewed(front(first + ratio - 1, ratio - 1))
        for m in reversed(range(ratio - 1)):
            _emit_skewed(front(first + m, m), back(first + m + 1, m + 1))

        @pl.when(i == 0)
        def _():
            _emit_skewed(back(0, 0))

        def alive():
            return (jnp.max(c_ref[...]) >= SB_DEAD_CARRY).astype(jnp.int32)

        @pl.when(i > 0)
        def _():
            _emit_skewed(front(first - 1, None), back(first, 0))

            def step(state):
                pending, _ = state
                _emit_skewed(front(pending - 1, None), back(pending, None))
                return pending - 1, alive()

            pending, live = lax.while_loop(lambda s: (s[0] > 0) & (s[1] > 0), step, (first - 1, alive()))

            @pl.when(live > 0)
            def _():
                _emit_skewed(back(pending, None))

        o_ref[...] = jnp.where(low, oacc_ref[0], oacc_ref[1]).astype(BF16)

    return _call(
        body, (q, k, v), comm=comm, **_grid_ends(N_HEADS // 2, s_len // tq), name="sb_fwd",
        grid=(N_HEADS // 2, s_len // tq),
        in_specs=[pl.BlockSpec((tq, LANES), lambda p, i: (i, p)),
                  pl.BlockSpec((s_len, LANES), lambda p, i: (0, p)),
                  pl.BlockSpec((s_len, LANES), lambda p, i: (0, p))],
        out_specs=[pl.BlockSpec((tq, LANES), lambda p, i: (i, p)), pl.BlockSpec((2, tq, LANES), lambda p, i: (p, i, 0))],
        out_shape=[jax.ShapeDtypeStruct((s_len, N_HEADS * HEAD_DIM), BF16),
                   jax.ShapeDtypeStruct((N_HEADS, s_len, LANES), F32)],
        scratch_shapes=[pltpu.VMEM((2, tq, LANES), F32), pltpu.VMEM((2, tq, LANES), F32),
                        pltpu.VMEM((2, 2, tq, tk), F32), pltpu.VMEM((2, 2, tq, LANES), F32)],
        compiler_params=_cparams("arbitrary", "arbitrary"),
    )


def _sb_bwd(q, k, v, do, cars):
    s_len = q.shape[0]
    tq, tk, tr = SB_QUERIES, SB_KEYS, SB_ROWS
    nk, ratio = s_len // tk, tq // tk

    def body(q_ref, k_ref, v_ref, do_ref, car_ref, dq_ref, dk_ref, dv_ref,
             gleft_ref, dqacc_ref, dkacc_ref, dvacc_ref, logw_ref, lsz_ref, da_ref, a_ref, dz_ref):
        i = pl.program_id(1)

        @pl.when(i == 0)
        def _():
            dkacc_ref[...] = jnp.zeros_like(dkacc_ref)
            dvacc_ref[...] = jnp.zeros_like(dvacc_ref)

        qv = q_ref[...]
        dov = do_ref[...]
        low, tri_right2, tri_left2 = _sb_consts(tq, tk)
        lane = lax.broadcasted_iota(jnp.int32, (tr, LANES), 1)
        zero = jnp.zeros_like(qv)
        q_heads = (jnp.where(low, qv, zero), jnp.where(low, zero, qv))
        do_heads = (jnp.where(low, dov, zero), jnp.where(low, zero, dov))
        q_t = qv.astype(F32).T.astype(BF16)
        do_t = dov.astype(F32).T.astype(BF16)
        gleft_ref[...] = jnp.zeros_like(gleft_ref)
        dqacc_ref[...] = jnp.zeros_like(dqacc_ref)

        def front(j, edge):
            key_rows = pl.ds(pl.multiple_of(j * tk, tk), tk)
            keys, values = k_ref[key_rows, :], v_ref[key_rows, :]
            slot = j % 2
            st = {}

            def s_logits(w):
                h, rs = w[0], slice(w[1], w[1] + tr)
                st[w, "z"] = _dot_nt(q_heads[h][rs], keys)
                da_ref[slot, h, rs, :] = _dot_nt(do_heads[h][rs], values)

            def s_terms(w):
                h, rs = w[0], slice(w[1], w[1] + tr)
                valid = _sb_valid(w, edge) if w[2] else None
                lsz, lk = _sb_terms(st.pop((w, "z")), valid)
                st[w, "parts"] = _bf16_parts(lk)
                lsz = lsz if valid is None else jnp.where(valid, lsz, NEG_BIG)
                lsz_ref[slot, h, rs, :] = lsz
                st[w, "lszc"] = lsz + jnp.sum(jnp.where(lane == j, car_ref[h, rs, :], 0.0), axis=-1, keepdims=True)

            def s_suffix(w):
                logw_ref[slot, w[0], w[1]:w[1] + tr, :] = st.pop((w, "lszc")) + _dot(st.pop((w, "parts")), tri_right2)

            return _sb_items(edge), [s_logits, s_terms, s_suffix]

        def back(j, edge):
            kv = k_ref[pl.ds(pl.multiple_of(j * tk, tk), tk), :]
            slot = j % 2
            st = {}

            items = _sb_items(edge)
            head_rows = [[r0 for hh, r0, _ in items if hh == h] for h in range(2)]

            def s_weights(w):
                h, rs = w[0], slice(w[1], w[1] + tr)
                a = jnp.exp(logw_ref[slot, h, rs, :])
                g = a * da_ref[slot, h, rs, :]
                a_ref[h, rs, :] = a.astype(BF16)
                st[w, "g"], st[w, "parts"] = g, _bf16_parts(g)

            def s_prefix(w):
                st[w, "gs"] = _dot(st.pop((w, "parts")), tri_left2)

            def s_dz(w):
                h, rs = w[0], slice(w[1], w[1] + tr)
                g = st.pop((w, "g"))
                gleft = gleft_ref[h, rs, :]
                gsum = st.pop((w, "gs")) + jnp.tile(gleft, (1, tk // LANES))
                dz = (g - jnp.exp(lsz_ref[slot, h, rs, :]) * (g + gsum)).astype(BF16)
                st[w, "dz"] = dz
                dz_ref[h, rs, :] = dz
                gleft_ref[h, rs, :] = gleft + _row_sum_lanes(g)

            def s_products(w):
                h, rs = w[0], slice(w[1], w[1] + tr)
                dqacc_ref[h, rs, :] += _dot(st.pop((w, "dz")), kv)
                if w[1] == head_rows[h][-1]:
                    feat = slice(h * HEAD_DIM, (h + 1) * HEAD_DIM)
                    hr = slice(head_rows[h][0], tq)
                    dkacc_ref[j, feat, :] += _dot(q_t[feat, hr], dz_ref[h, hr, :])
                    dvacc_ref[j, feat, :] += _dot(do_t[feat, hr], a_ref[h, hr, :])

            return items, [s_weights, s_prefix, s_dz, s_products]

        first = i * ratio
        tile_max = jnp.max(jnp.maximum(car_ref[0], car_ref[1]), axis=0, keepdims=True)
        start = jnp.clip(first + ratio - jnp.sum(jnp.where(tile_max >= SB_DEAD_CARRY, 1, 0)), 0, first)

        @pl.when(start == first)
        def _():
            _emit_skewed(front(first, 0))

        @pl.when(start < first)
        def _():
            _emit_skewed(front(start, None))

            def step(jj, carry):
                _emit_skewed(front(jj, None), back(jj - 1, None))
                return carry

            lax.fori_loop(start + 1, first, step, 0)
            _emit_skewed(front(first, 0), back(first - 1, None))

        for m in range(1, ratio):
            _emit_skewed(front(first + m, m), back(first + m - 1, m - 1))
        _emit_skewed(back(first + ratio - 1, ratio - 1))
        dq_ref[...] = (Q_SCALE * jnp.where(low, dqacc_ref[0], dqacc_ref[1])).astype(BF16)

        @pl.when(i == s_len // tq - 1)
        def _():
            for j in range(nk):
                dk_ref[j * tk:(j + 1) * tk, :] = dkacc_ref[j].T.astype(BF16)
                dv_ref[j * tk:(j + 1) * tk, :] = dvacc_ref[j].T.astype(BF16)

    qblk = pl.BlockSpec((tq, LANES), lambda p, i: (i, p))
    col_full = pl.BlockSpec((s_len, LANES), lambda p, i: (0, p))
    return pl.pallas_call(
        body, name="sb_bwd",
        grid=(N_HEADS // 2, s_len // tq),
        in_specs=[qblk, col_full, col_full, qblk, pl.BlockSpec((2, tq, LANES), lambda p, i: (p, i, 0))],
        out_specs=[qblk, col_full, col_full],
        out_shape=[jax.ShapeDtypeStruct((s_len, N_HEADS * HEAD_DIM), BF16)] * 3,
        scratch_shapes=[pltpu.VMEM((2, tq, LANES), F32), pltpu.VMEM((2, tq, LANES), F32),
                        pltpu.VMEM((nk, LANES, tk), F32), pltpu.VMEM((nk, LANES, tk), F32)]
        + [pltpu.VMEM((2, 2, tq, tk), F32)] * 3 + [pltpu.VMEM((2, tq, tk), BF16)] * 2,
        compiler_params=_cparams("parallel", "arbitrary"),
    )(q, k, v, do, cars)


def _pad_heads(a, heads):
    s_len = a.shape[0]
    a = a.reshape(s_len, heads, HEAD_DIM)
    return jnp.pad(a, ((0, 0), (0, 0), (0, LANES - HEAD_DIM))).reshape(s_len, heads * LANES)


def _unpad_heads(a, heads):
    s_len = a.shape[0]
    return a.reshape(s_len, heads, LANES)[:, :, :HEAD_DIM].reshape(s_len, heads * HEAD_DIM)


def _local_step(xs, tgt, gains, sinks, rel_bias, weights_of, ship):
    g1, gmix, g2, gfin = gains
    bkt = _rel_bucket_matrix()
    grads = {}

    def carried(outs, comm, count):
        return outs[:count], (list(outs[count:]) if comm is not None else None)

    wts = dict(weights_of(0, None))
    comm = ship("weights", 1)
    (x1, h1, a1, b1, u1), landed = carried(
        _ffn_fwd(xs, g1, wts["ffn1_w1t"], wts["ffn1_w3t"], wts["ffn1_w2"], "1", comm), comm, 5)
    wts.update(weights_of(1, landed))
    hm, qa, ka, va, qb, kb, vb, ga, gb = _proj_fwd(x1, gmix, wts["w_int"])
    qa_p, ka_p, va_p = _pad_heads(qa, N_HEADS), _pad_heads(ka, SWA_KV_HEADS), _pad_heads(va, SWA_KV_HEADS)
    oa_p = _swa_fwd(rel_bias, sinks, bkt, qa_p, ka_p, va_p)
    comm = ship("weights", 2)
    (ob, cars), landed = carried(_sb_fwd(qb, kb, vb, comm), comm, 2)
    wts.update(weights_of(2, landed))
    oa = _unpad_heads(oa_p, N_HEADS)
    x2, mg = _merge_fwd(x1, oa, ob, ga, gb, wts["w_swa"], wts["w_sb"], wts["w_out"])
    x3, h3, a3, b3, u3 = _ffn_fwd(x2, g2, wts["ffn2_w1t"], wts["ffn2_w3t"], wts["ffn2_w2"], "2")
    dx3, loss, dgfin = _loss_fwd_bwd(x3, tgt, gfin)

    dx2, dg2, da3, db3, dx3b = _ffn_bwd(dx3, x2, g2, a3, b3, wts["ffn2_w1t"], wts["ffn2_w3t"], wts["ffn2_w2"], "2")
    big = {"ffn2_w1t": _matmul_tn(da3, h3, "ffn2_w1"), "ffn2_w3t": _matmul_tn(db3, h3, "ffn2_w3"),
           "ffn2_w2": _matmul_tn(u3, dx3b, "ffn2_w2")}

    doa, dob, dga, dgb, dpa, dpb, dx2b = _merge_bwd(dx2, oa, ob, ga, gb, wts["w_swa"], wts["w_sb"], wts["w_out"])
    comm = ship("grads", GROUPS[2], big)
    (dqa_p, dka_p, dva_p, dtab, dsink), landed = carried(
        _swa_bwd(rel_bias, sinks, bkt, qa_p, ka_p, va_p, _pad_heads(doa, N_HEADS), comm), comm, 5)
    grads[GROUPS[2]] = big if comm is None else landed[0]

    big = {"w_out": _matmul_tn(mg, dx2b, "w_out"), "w_swa": _matmul_tn(oa, dpa, "w_swa"),
           "w_sb": _matmul_tn(ob, dpb, "w_sb")}
    dqb, dkb, dvb = _sb_bwd(qb, kb, vb, dob, cars)
    dpieces = (_unpad_heads(dqa_p, N_HEADS), _unpad_heads(dka_p, SWA_KV_HEADS).astype(BF16),
               _unpad_heads(dva_p, SWA_KV_HEADS).astype(BF16), dqb, dkb, dvb, dga, dgb)
    big["w_int"] = jnp.concatenate([_matmul_tn(dp, hm, f"w_in{p}") for p, dp in enumerate(dpieces)], axis=0)
    dx1, dgmix = _proj_bwd(dpieces, dx2, x1, gmix, wts["w_int"])

    comm = ship("grads", GROUPS[1], big)
    (dx0, dg1, da1, db1, dx1b), landed = carried(
        _ffn_bwd(dx1, xs, g1, a1, b1, wts["ffn1_w1t"], wts["ffn1_w3t"], wts["ffn1_w2"], "1", comm), comm, 5)
    grads[GROUPS[1]] = big if comm is None else landed[0]

    prev = None
    for name, lhs, rhs in (("ffn1_w1", da1, h1), ("ffn1_w3", db1, h1), ("ffn1_w2", u1, dx1b)):
        comm = None if prev is None else ship("grads", (prev[0],), prev[1])
        res = _matmul_tn(lhs, rhs, name, comm)
        if prev is not None:
            grads[(prev[0],)] = prev[1] if comm is None else res[1]
        prev = (name, {_GRAD_KEY[name]: res if comm is None else res[0]})
    grads[(prev[0],)] = prev[1]

    small = {"gains": (dg1, dgmix, dg2, dgfin), "sinks": dsink[:, 0], "rel_bias": dtab[:, :N_HEADS]}
    return loss, dx0, small, grads


def _my_place():
    return lax.axis_index("x"), lax.axis_index("y"), lax.axis_index("c")


def _flip(v, bit):
    return 1 - v if bit else v


_RELATIONS = tuple((k >> 2 & 1, k >> 1 & 1, k & 1) for k in range(1, N_DEV))


def _gather_weights(blocks, tag):
    count = len(blocks)

    def body(*refs):
        x_refs, out_refs = refs[:count], refs[count:2 * count]
        send_sems, recv_sems, local_sems = refs[2 * count:]
        x, y, c = _my_place()
        me, sibling = (x, y, c), (x, y, 1 - c)
        chips = [(1 - x, y), (x, 1 - y), (1 - x, 1 - y)]

        def rows(s, px, py, pc):
            return out_refs[s].at[4 * px + 2 * py + pc]

        def copy(s, k, block, to, src=None):
            return pltpu.make_async_remote_copy(
                src_ref=rows(s, *block) if src is None else src, dst_ref=rows(s, *block),
                send_sem=send_sems.at[s, k], recv_sem=recv_sems.at[s, k],
                device_id=to, device_id_type=pl.DeviceIdType.MESH)

        mine = [pltpu.make_async_copy(x_refs[s], rows(s, *me), local_sems.at[s]) for s in range(count)]
        first, passed = [], []
        for s in range(count):
            mine[s].start()
            first.append(copy(s, 0, me, sibling, src=x_refs[s]))
            first += [copy(s, 1 + j, me, (*chip, c), src=x_refs[s]) for j, chip in enumerate(chips)]
        for cp in first:
            cp.start()
        for s in range(count):
            for j, chip in enumerate(chips):
                copy(s, 1 + j, (*chip, c), me).wait_recv()
                passed.append(copy(s, 4 + j, (*chip, c), sibling))
                passed[-1].start()
        for s in range(count):
            copy(s, 0, sibling, me).wait_recv()
            for j, chip in enumerate(chips):
                copy(s, 4 + j, (*chip, 1 - c), me).wait_recv()
        for cp in first + passed:
            cp.wait_send()
        for cp in mine:
            cp.wait()

    anywhere = pl.BlockSpec(memory_space=pl.ANY)
    return pl.pallas_call(
        body, name=f"gather_weights_{tag}",
        out_shape=[jax.ShapeDtypeStruct((N_DEV,) + b.shape, b.dtype) for b in blocks],
        in_specs=[anywhere] * count, out_specs=[anywhere] * count,
        scratch_shapes=[pltpu.SemaphoreType.DMA((count, N_DEV - 1)), pltpu.SemaphoreType.DMA((count, N_DEV - 1)),
                        pltpu.SemaphoreType.DMA((count,))],
    )(*blocks)


def _exchange_grads(gp, tag):
    def body(g_ref, out_ref, send_sems, recv_sems, local_sem):
        x, y, c = _my_place()
        me = 4 * x + 2 * y + c
        mine = pltpu.make_async_copy(g_ref.at[me], out_ref.at[me], local_sem)
        mine.start()
        copies = []
        for k, (fx, fy, fc) in enumerate(_RELATIONS):
            px, py, pc = _flip(x, fx), _flip(y, fy), _flip(c, fc)
            peer = 4 * px + 2 * py + pc
            copies.append((
                pltpu.make_async_remote_copy(
                    src_ref=g_ref.at[peer], dst_ref=out_ref.at[me], send_sem=send_sems.at[k], recv_sem=recv_sems.at[k],
                    device_id=(px, py, pc), device_id_type=pl.DeviceIdType.MESH),
                pltpu.make_async_remote_copy(
                    src_ref=g_ref.at[peer], dst_ref=out_ref.at[peer], send_sem=send_sems.at[k], recv_sem=recv_sems.at[k],
                    device_id=(px, py, pc), device_id_type=pl.DeviceIdType.MESH)))
        for out_cp, _ in copies:
            out_cp.start()
        for _, in_cp in copies:
            in_cp.wait_recv()
        for out_cp, _ in copies:
            out_cp.wait_send()
        mine.wait()

    return pl.pallas_call(
        body, name=f"exchange_grads_{tag}",
        out_shape=jax.ShapeDtypeStruct(gp.shape, gp.dtype),
        in_specs=[pl.BlockSpec(memory_space=pl.ANY)],
        out_specs=pl.BlockSpec(memory_space=pl.ANY),
        scratch_shapes=[pltpu.SemaphoreType.DMA((7,)), pltpu.SemaphoreType.DMA((7,)), pltpu.SemaphoreType.DMA(())],
    )(gp)


def _peers():
    x, y, c = _my_place()
    out = []
    for k, (fx, fy, fc) in enumerate(_RELATIONS):
        px, py, pc = _flip(x, fx), _flip(y, fy), _flip(c, fc)
        out.append((k, (px, py, pc), 4 * px + 2 * py + pc))
    return out, 4 * x + 2 * y + c


def _grid_ends(*grid):
    def first():
        return functools.reduce(lambda a, b: a & b, [pl.program_id(d) == 0 for d in range(len(grid))])

    def last():
        return functools.reduce(lambda a, b: a & b, [pl.program_id(d) == n - 1 for d, n in enumerate(grid)])

    return {"first": first, "last": last}


def _call(body, operands, *, comm=None, first=None, last=None, **kw):
    if comm is None:
        return pl.pallas_call(body, **kw)(*operands)
    in_specs, out_specs, out_shape = list(kw.pop("in_specs")), list(kw.pop("out_specs")), list(kw.pop("out_shape"))
    scratch = list(kw.pop("scratch_shapes", ()))
    n_in, n_out, n_scr, n_src = len(in_specs), len(out_specs), len(scratch), len(comm)

    def wrapped(*refs):
        ins, src_refs = refs[:n_in], refs[n_in:n_in + n_src]
        outs = refs[n_in + n_src:n_in + n_src + n_out]
        land_refs = refs[n_in + n_src + n_out:n_in + 2 * n_src + n_out]
        scr = refs[n_in + 2 * n_src + n_out:n_in + 2 * n_src + n_out + n_scr]
        send_sems, recv_sems, local_sems = refs[n_in + 2 * n_src + n_out + n_scr:]
        peers, me = _peers()
        mine, going, coming = [], [], []
        for s, (_, per_peer) in enumerate(comm):
            src_ref, land_ref = src_refs[s], land_refs[s]
            mine.append(pltpu.make_async_copy(src_ref.at[me] if per_peer else src_ref, land_ref.at[me], local_sems.at[s]))
            for k, where, slab in peers:
                piece = src_ref.at[slab] if per_peer else src_ref
                going.append(pltpu.make_async_remote_copy(
                    src_ref=piece, dst_ref=land_ref.at[me], send_sem=send_sems.at[s, k], recv_sem=recv_sems.at[s, k],
                    device_id=where, device_id_type=pl.DeviceIdType.MESH))
                coming.append(pltpu.make_async_remote_copy(
                    src_ref=piece, dst_ref=land_ref.at[slab], send_sem=send_sems.at[s, k], recv_sem=recv_sems.at[s, k],
                    device_id=where, device_id_type=pl.DeviceIdType.MESH))

        @pl.when(first())
        def _():
            for cp in mine + going:
                cp.start()

        body(*ins, *outs, *scr)

        @pl.when(last())
        def _():
            for cp in coming:
                cp.wait_recv()
            for cp in going:
                cp.wait_send()
            for cp in mine:
                cp.wait()

    anywhere = pl.BlockSpec(memory_space=pl.ANY)
    lands = [jax.ShapeDtypeStruct(src.shape if per_peer else (N_DEV,) + src.shape, src.dtype) for src, per_peer in comm]
    return pl.pallas_call(
        wrapped, in_specs=in_specs + [anywhere] * n_src, out_specs=out_specs + [anywhere] * n_src,
        out_shape=out_shape + lands,
        scratch_shapes=scratch + [pltpu.SemaphoreType.DMA((n_src, N_DEV - 1)), pltpu.SemaphoreType.DMA((n_src, N_DEV - 1)),
                                  pltpu.SemaphoreType.DMA((n_src,))],
        **kw)(*operands, *[src for src, _ in comm])


def _adamw(w, g, m, v):
    m = ADAM_B1 * m + (1.0 - ADAM_B1) * g
    v = ADAM_B2 * v + (1.0 - ADAM_B2) * jnp.square(g)
    m_hat = m / (1.0 - ADAM_B1 ** ADAM_STEP)
    v_hat = v / (1.0 - ADAM_B2 ** ADAM_STEP)
    delta = -ADAM_LR * (m_hat / (jnp.sqrt(v_hat) + ADAM_EPS) + ADAM_WD * w)
    return delta, m, v


def _sum_and_adamw(parts, w, m, v, tr, tag):
    rows = w.shape[0]
    assert rows % tr == 0

    def body(p_ref, w_ref, m_ref, v_ref, g_out, d_out, m_out, v_out):
        g = p_ref[0].astype(F32)
        for d in range(1, N_DEV):
            g = g + p_ref[d].astype(F32)
        delta, mn, vn = _adamw(w_ref[...], g, m_ref[...], v_ref[...])
        g_out[...] = g
        d_out[...] = delta
        m_out[...] = mn
        v_out[...] = vn

    sp = pl.BlockSpec((tr, D_MODEL), lambda i: (i, 0))
    return pl.pallas_call(
        body, name=f"sum_and_adamw_{tag}",
        grid=(rows // tr,),
        in_specs=[pl.BlockSpec((N_DEV, tr, D_MODEL), lambda i: (0, i, 0)), sp, sp, sp],
        out_specs=[sp] * 4,
        out_shape=[jax.ShapeDtypeStruct(w.shape, F32)] * 4,
        compiler_params=_cparams("parallel"),
    )(parts, w, m, v)


def _small_allreduce_adamw(part, w, m, v):
    def body(p_ref, w_ref, m_ref, v_ref, g_out, d_out, m_out, v_out, buf, send_sems, recv_sems):
        x, y, c = _my_place()
        me = 4 * x + 2 * y + c
        buf[me] = p_ref[...]
        copies = []
        for k, (fx, fy, fc) in enumerate(_RELATIONS):
            px, py, pc = _flip(x, fx), _flip(y, fy), _flip(c, fc)
            peer = 4 * px + 2 * py + pc
            copies.append((
                pltpu.make_async_remote_copy(
                    src_ref=buf.at[me], dst_ref=buf.at[me], send_sem=send_sems.at[k], recv_sem=recv_sems.at[k],
                    device_id=(px, py, pc), device_id_type=pl.DeviceIdType.MESH),
                pltpu.make_async_remote_copy(
                    src_ref=buf.at[me], dst_ref=buf.at[peer], send_sem=send_sems.at[k], recv_sem=recv_sems.at[k],
                    device_id=(px, py, pc), device_id_type=pl.DeviceIdType.MESH)))
        for out_cp, _ in copies:
            out_cp.start()
        for _, in_cp in copies:
            in_cp.wait_recv()
        for out_cp, _ in copies:
            out_cp.wait_send()
        g = buf[0]
        for d in range(1, N_DEV):
            g = g + buf[d]
        delta, mn, vn = _adamw(w_ref[...], g, m_ref[...], v_ref[...])
        g_out[...] = g
        d_out[...] = delta
        m_out[...] = mn
        v_out[...] = vn

    vm = pl.BlockSpec(memory_space=pltpu.VMEM)
    return pl.pallas_call(
        body, name="small_allreduce_adamw",
        in_specs=[vm] * 4, out_specs=[vm] * 4,
        out_shape=[jax.ShapeDtypeStruct(w.shape, F32)] * 4,
        scratch_shapes=[pltpu.VMEM((N_DEV,) + part.shape, F32),
                        pltpu.SemaphoreType.DMA((7,)), pltpu.SemaphoreType.DMA((7,))],
    )(part, w, m, v)


_TRANSPOSED = ("ffn1_w1", "ffn1_w3", "w_in", "ffn2_w1", "ffn2_w3")
_BRANCH = ("w_branch_swa", "w_branch_sb")


def _pack_shards(t, names):
    parts = []
    for name in names:
        a = t[name][0]
        if name in _TRANSPOSED:
            a = a.T
        elif name in _BRANCH:
            a = a.reshape(64, D_MODEL)
        parts.append(a)
    return jnp.concatenate(parts, axis=0)


def _unpack_shards(p, names):
    out, lo = {}, 0
    for name in names:
        a = p[lo:lo + BIG_ROWS[BIG_NAMES.index(name)]]
        lo += a.shape[0]
        if name in _TRANSPOSED:
            a = a.T
        elif name in _BRANCH:
            a = a.reshape(512, 128)
        out[name] = a[None]
    return out


def _full_weights(zones, names):
    out = {}
    for name, a in zip(names, zones):
        if name in _BRANCH:
            a = a.reshape(N_DEV, 512, 128).transpose(1, 0, 2).reshape(512, D_MODEL)
        out[_GRAD_KEY[name]] = a.reshape(-1, D_MODEL)
    return out


_GRAD_KEY = {"ffn1_w1": "ffn1_w1t", "ffn1_w3": "ffn1_w3t", "ffn1_w2": "ffn1_w2", "w_in": "w_int",
             "w_branch_swa": "w_swa", "w_branch_sb": "w_sb", "w_out": "w_out",
             "ffn2_w1": "ffn2_w1t", "ffn2_w3": "ffn2_w3t", "ffn2_w2": "ffn2_w2"}


def _pack_full_grads(big, names):
    parts = []
    for name in names:
        a = big[_GRAD_KEY[name]]
        if name in _BRANCH:
            a = a.reshape(512, N_DEV, 128).transpose(1, 0, 2)
        parts.append(a.reshape(N_DEV, BIG_ROWS[BIG_NAMES.index(name)], D_MODEL).astype(BF16))
    return jnp.concatenate(parts, axis=1)


_SMALL_NAMES = ("norm_ffn1", "norm_mix", "norm_ffn2", "norm_final", "swa_sinks", "rel_bias")


def _pack_small(vals):
    rows = []
    for a in vals:
        a = a.reshape(-1)
        rows.append(jnp.pad(a, (0, D_MODEL - a.shape[0])))
    rows += [jnp.zeros((D_MODEL,), F32)] * (SMALL_ROWS - len(rows))
    return jnp.stack(rows)


def _unpack_small(p):
    return {"norm_ffn1": p[0:1], "norm_mix": p[1:2], "norm_ffn2": p[2:3], "norm_final": p[3],
            "swa_sinks": p[4:5, :N_HEADS], "rel_bias": p[5, :REL_BUCKETS * N_HEADS].reshape(REL_BUCKETS, N_HEADS)}


ALL_NAMES = ("norm_ffn1", "ffn1_w1", "ffn1_w3", "ffn1_w2", "norm_mix", "w_in", "swa_sinks", "rel_bias",
             "w_branch_swa", "w_branch_sb", "w_out", "norm_ffn2", "ffn2_w1", "ffn2_w3", "ffn2_w2", "norm_final")


def kernel(x, norm_ffn1, ffn1_w1, ffn1_w3, ffn1_w2, norm_mix, w_in, swa_sinks, rel_bias, w_branch_swa, w_branch_sb, w_out, norm_ffn2, ffn2_w1, ffn2_w3, ffn2_w2, norm_final, loss_target, m_norm_ffn1, m_ffn1_w1, m_ffn1_w3, m_ffn1_w2, m_norm_mix, m_w_in, m_swa_sinks, m_rel_bias, m_w_branch_swa, m_w_branch_sb, m_w_out, m_norm_ffn2, m_ffn2_w1, m_ffn2_w3, m_ffn2_w2, m_norm_final, v_norm_ffn1, v_ffn1_w1, v_ffn1_w3, v_ffn1_w2, v_norm_mix, v_w_in, v_swa_sinks, v_rel_bias, v_w_branch_swa, v_w_branch_sb, v_w_out, v_norm_ffn2, v_ffn2_w1, v_ffn2_w3, v_ffn2_w2, v_norm_final):
    w = dict(zip(ALL_NAMES, (norm_ffn1, ffn1_w1, ffn1_w3, ffn1_w2, norm_mix, w_in, swa_sinks, rel_bias,
                             w_branch_swa, w_branch_sb, w_out, norm_ffn2, ffn2_w1, ffn2_w3, ffn2_w2, norm_final)))
    m = dict(zip(ALL_NAMES, (m_norm_ffn1, m_ffn1_w1, m_ffn1_w3, m_ffn1_w2, m_norm_mix, m_w_in, m_swa_sinks, m_rel_bias,
                             m_w_branch_swa, m_w_branch_sb, m_w_out, m_norm_ffn2, m_ffn2_w1, m_ffn2_w3, m_ffn2_w2,
                             m_norm_final)))
    v = dict(zip(ALL_NAMES, (v_norm_ffn1, v_ffn1_w1, v_ffn1_w3, v_ffn1_w2, v_norm_mix, v_w_in, v_swa_sinks, v_rel_bias,
                             v_w_branch_swa, v_w_branch_sb, v_w_out, v_norm_ffn2, v_ffn2_w1, v_ffn2_w3, v_ffn2_w2,
                             v_norm_final)))

    def my_blocks(group):
        return [_pack_shards(w, (name,)).astype(BF16) for name in GROUPS[group]]

    gathered0 = _gather_weights(my_blocks(0), "group0")

    def weights_of(group, landed):
        return _full_weights(gathered0 if group == 0 else landed, GROUPS[group])

    def ship(kind, which, grads=None):
        if kind == "weights":
            return [(block, False) for block in my_blocks(which)]
        return [(_pack_full_grads(grads, which), True)]

    gains = (norm_ffn1, norm_mix, norm_ffn2, norm_final.reshape(1, D_MODEL))
    loss, dx, small, parts = _local_step(x[0], loss_target[0], gains, swa_sinks, rel_bias, weights_of, ship)

    big_outs = [{}, {}, {}, {}]
    for names, tile in zip(SUM_GROUPS, SUM_TILE):
        landed = parts[names]
        if isinstance(landed, dict):
            landed = _exchange_grads(_pack_full_grads(landed, names), names[0])
        res = _sum_and_adamw(landed, _pack_shards(w, names), _pack_shards(m, names), _pack_shards(v, names),
                             tile, names[0])
        for acc, packed in zip(big_outs, res):
            acc.update(_unpack_shards(packed, names))
    g_big, d_big, m_big, v_big = big_outs

    small_part = _pack_small(small["gains"] + (small["sinks"], small["rel_bias"], loss))
    zero = jnp.zeros((1,), F32)
    small_res = _small_allreduce_adamw(
        small_part, _pack_small([w[n] for n in _SMALL_NAMES] + [zero]), _pack_small([m[n] for n in _SMALL_NAMES] + [zero]),
        _pack_small([v[n] for n in _SMALL_NAMES] + [zero]))
    g_sm, d_sm, m_sm, v_sm = (_unpack_small(p) for p in small_res)

    outs = [small_res[0][len(_SMALL_NAMES), 0], dx[None]]
    for big_d, small_d in ((g_big, g_sm), (d_big, d_sm), (m_big, m_sm), (v_big, v_sm)):
        merged = {**big_d, **small_d}
        outs += [merged[n] for n in ALL_NAMES]
    return tuple(outs)
```

```python
import functools

import jax
import jax.numpy as jnp
import numpy as np
from jax import lax
from jax.experimental import pallas as pl
from jax.experimental.pallas import tpu as pltpu

F32 = jnp.float32
BF16 = jnp.bfloat16

D_MODEL = 1024
D_FF = 2816
HEAD_DIM = 64
N_HEADS = 8
SWA_KV_HEADS = 2
SWA_GROUP = 4
SWA_BLOCK = 128
REL_BUCKETS = 32
REL_MAX_DIST = 128
RMS_EPS = 1e-6
NEG_BIG = -1e30
Q_SCALE = HEAD_DIM ** -0.5
LANES = 128

N_DEV = 8

ADAM_LR = 0.001
ADAM_B1 = 0.9
ADAM_B2 = 0.999
ADAM_EPS = 1e-08
ADAM_WD = 0.01
ADAM_STEP = 10

IN_SIZES = (512, 128, 128, 512, 512, 512, 1024, 1024)
IN_OFFS = tuple(int(v) for v in np.cumsum((0,) + IN_SIZES))
IN_W = IN_OFFS[-1]

BIG_NAMES = ("ffn1_w1", "ffn1_w3", "ffn1_w2", "w_in", "w_branch_swa", "w_branch_sb", "w_out",
             "ffn2_w1", "ffn2_w3", "ffn2_w2")
BIG_ROWS = (352, 352, 352, 544, 64, 64, 128, 352, 352, 352)
SMALL_ROWS = 8
GROUPS = (BIG_NAMES[0:3], BIG_NAMES[3:7], BIG_NAMES[7:10])
SUM_GROUPS = tuple((n,) for n in GROUPS[0]) + GROUPS[1:]
SUM_TILE = (176, 176, 176, 160, 96)

VMEM_LIMIT = 56 * 1024 * 1024
FFN_PIECES = 2
SB_QUERIES = 512
SB_KEYS = 256
SB_ROWS = 256
SB_SUM_PARTS = 1
SB_LOGIT_CAP = 80.0
SB_DEAD_CARRY = -110.0


def _dot(a, b):
    return jnp.dot(a, b, preferred_element_type=F32)


def _dot_nt(a, b):
    return lax.dot_general(a, b, (((1,), (1,)), ((), ())), preferred_element_type=F32)


def _dot_tn(a, b):
    return lax.dot_general(a, b, (((0,), (0,)), ((), ())), preferred_element_type=F32)


def _cparams(*sem):
    return pltpu.CompilerParams(dimension_semantics=sem, vmem_limit_bytes=VMEM_LIMIT)


def _rms_rstd(xv):
    return lax.rsqrt(jnp.mean(xv * xv, axis=-1, keepdims=True) + RMS_EPS)


def _rms_bwd(dh, xv, r, g):
    xhat = xv * r
    dg = jnp.sum(dh * xhat, axis=0, keepdims=True)
    dxn = dh * g
    dx = r * (dxn - xhat * jnp.mean(dxn * xhat, axis=-1, keepdims=True))
    return dx, dg


def _ffn_fwd(x, g, w1t, w3t, w2, tag, comm=None):
    s_len = x.shape[0]
    tm, tf = min(1024, s_len), 256
    nf = D_FF // tf

    def body(x_ref, g_ref, w1_ref, w3_ref, w2_ref, xo_ref, h_ref, a_ref, b_ref, u_ref, acc_ref, hs_ref):
        j = pl.program_id(1)

        @pl.when(j == 0)
        def _():
            xv = x_ref[...]
            h = (xv * _rms_rstd(xv) * g_ref[...]).astype(BF16)
            hs_ref[...] = h
            h_ref[...] = h
            acc_ref[...] = jnp.zeros_like(acc_ref)

        st = {}

        def s_up(rs):
            h = hs_ref[rs, :]
            st[rs.start, "ab"] = (_dot_nt(h, w1_ref[...]), _dot_nt(h, w3_ref[...]))

        def s_act(rs):
            a, b = st.pop((rs.start, "ab"))
            a_ref[rs, :] = a.astype(BF16)
            b_ref[rs, :] = b.astype(BF16)
            uh = (0.5 * (a * jax.nn.sigmoid(a) * b)).astype(BF16)
            u_ref[rs, :] = uh
            st[rs.start, "u"] = uh

        def s_down(rs):
            acc_ref[rs, :] += _dot(st.pop((rs.start, "u")), w2_ref[...])

        _emit_skewed(([slice(r, r + tm // FFN_PIECES) for r in range(0, tm, tm // FFN_PIECES)], [s_up, s_act, s_down]))

        @pl.when(j == nf - 1)
        def _():
            xo_ref[...] = x_ref[...] + acc_ref[...]

    row = lambda i, j: (i, 0)
    return _call(
        body, (x, g, w1t, w3t, w2), comm=comm, **_grid_ends(s_len // tm, nf), name=f"ffn_fwd_{tag}",
        grid=(s_len // tm, nf),
        in_specs=[pl.BlockSpec((tm, D_MODEL), row), pl.BlockSpec((1, D_MODEL), lambda i, j: (0, 0)),
                  pl.BlockSpec((tf, D_MODEL), lambda i, j: (j, 0)), pl.BlockSpec((tf, D_MODEL), lambda i, j: (j, 0)),
                  pl.BlockSpec((tf, D_MODEL), lambda i, j: (j, 0))],
        out_specs=[pl.BlockSpec((tm, D_MODEL), row), pl.BlockSpec((tm, D_MODEL), row),
                   pl.BlockSpec((tm, tf), lambda i, j: (i, j)), pl.BlockSpec((tm, tf), lambda i, j: (i, j)),
                   pl.BlockSpec((tm, tf), lambda i, j: (i, j))],
        out_shape=[jax.ShapeDtypeStruct((s_len, D_MODEL), F32), jax.ShapeDtypeStruct((s_len, D_MODEL), BF16),
                   jax.ShapeDtypeStruct((s_len, D_FF), BF16), jax.ShapeDtypeStruct((s_len, D_FF), BF16),
                   jax.ShapeDtypeStruct((s_len, D_FF), BF16)],
        scratch_shapes=[pltpu.VMEM((tm, D_MODEL), F32), pltpu.VMEM((tm, D_MODEL), BF16)],
        compiler_params=_cparams("arbitrary", "arbitrary"),
    )


def _ffn_bwd(dy, x, g, a, b, w1t, w3t, w2, tag, comm=None):
    s_len = x.shape[0]
    tm, tf = min(1024, s_len), 256
    nf = D_FF // tf

    def body(dy_ref, x_ref, g_ref, a_ref, b_ref, w1_ref, w3_ref, w2_ref,
             dx_ref, dg_ref, da_ref, db_ref, dyb_ref, acc_ref, dys_ref):
        i, j = pl.program_id(0), pl.program_id(1)

        @pl.when(j == 0)
        def _():
            dyb = dy_ref[...].astype(BF16)
            dys_ref[...] = dyb
            dyb_ref[...] = dyb
            acc_ref[...] = jnp.zeros_like(acc_ref)

        @pl.when((i == 0) & (j == 0))
        def _():
            dg_ref[...] = jnp.zeros_like(dg_ref)

        st = {}

        def s_du(rs):
            st[rs.start, "du"] = 0.5 * _dot_nt(dys_ref[rs, :], w2_ref[...])

        def s_act(rs):
            du = st.pop((rs.start, "du"))
            av = a_ref[rs, :].astype(F32)
            bv = b_ref[rs, :].astype(F32)
            sg = jax.nn.sigmoid(av)
            sil = av * sg
            da = (du * bv * (sg + sil * (1.0 - sg))).astype(BF16)
            db = (du * sil).astype(BF16)
            da_ref[rs, :] = da
            db_ref[rs, :] = db
            st[rs.start, "dab"] = (da, db)

        def s_dh(rs):
            da, db = st.pop((rs.start, "dab"))
            acc_ref[rs, :] += _dot(da, w1_ref[...]) + _dot(db, w3_ref[...])

        _emit_skewed(([slice(r, r + tm // FFN_PIECES) for r in range(0, tm, tm // FFN_PIECES)], [s_du, s_act, s_dh]))

        @pl.when(j == nf - 1)
        def _():
            xv = x_ref[...]
            dx, dg = _rms_bwd(acc_ref[...], xv, _rms_rstd(xv), g_ref[...])
            dx_ref[...] = dy_ref[...] + dx
            dg_ref[...] += dg

    row = lambda i, j: (i, 0)
    blk = lambda i, j: (i, j)
    wsp = pl.BlockSpec((tf, D_MODEL), lambda i, j: (j, 0))
    return _call(
        body, (dy, x, g, a, b, w1t, w3t, w2), comm=comm, **_grid_ends(s_len // tm, nf), name=f"ffn_bwd_{tag}",
        grid=(s_len // tm, nf),
        in_specs=[pl.BlockSpec((tm, D_MODEL), row), pl.BlockSpec((tm, D_MODEL), row),
                  pl.BlockSpec((1, D_MODEL), lambda i, j: (0, 0)),
                  pl.BlockSpec((tm, tf), blk), pl.BlockSpec((tm, tf), blk), wsp, wsp, wsp],
        out_specs=[pl.BlockSpec((tm, D_MODEL), row), pl.BlockSpec((1, D_MODEL), lambda i, j: (0, 0)),
                   pl.BlockSpec((tm, tf), blk), pl.BlockSpec((tm, tf), blk), pl.BlockSpec((tm, D_MODEL), row)],
        out_shape=[jax.ShapeDtypeStruct((s_len, D_MODEL), F32), jax.ShapeDtypeStruct((1, D_MODEL), F32),
                   jax.ShapeDtypeStruct((s_len, D_FF), BF16), jax.ShapeDtypeStruct((s_len, D_FF), BF16),
                   jax.ShapeDtypeStruct((s_len, D_MODEL), BF16)],
        scratch_shapes=[pltpu.VMEM((tm, D_MODEL), F32), pltpu.VMEM((tm, D_MODEL), BF16)],
        compiler_params=_cparams("arbitrary", "arbitrary"),
    )


def _matmul_tn(lhs, rhs, tag, comm=None):
    s_len, m = lhs.shape
    n = rhs.shape[1]
    tm = min(512, s_len)
    tj = m if m <= 1024 else 1408
    assert m % tj == 0
    last_rows = s_len // tm - 1

    def body(l_ref, r_ref, o_ref, acc_ref):
        i = pl.program_id(1)

        @pl.when(i == 0)
        def _():
            acc_ref[...] = jnp.zeros_like(acc_ref)

        acc_ref[...] += _dot_tn(l_ref[...], r_ref[...])

        @pl.when(i == last_rows)
        def _():
            o_ref[...] = acc_ref[...].astype(BF16)

    res = _call(
        body, (lhs, rhs), comm=comm, **_grid_ends(m // tj, s_len // tm), name=f"matmul_tn_{tag}",
        grid=(m // tj, s_len // tm),
        in_specs=[pl.BlockSpec((tm, tj), lambda j, i: (i, j)), pl.BlockSpec((tm, n), lambda j, i: (i, 0))],
        out_specs=[pl.BlockSpec((tj, n), lambda j, i: (j, 0))],
        out_shape=[jax.ShapeDtypeStruct((m, n), BF16)],
        scratch_shapes=[pltpu.VMEM((tj, n), F32)],
        compiler_params=_cparams("arbitrary", "arbitrary"),
    )
    return res[0] if comm is None else tuple(res)


def _proj_fwd(x1, g, wint):
    s_len = x1.shape[0]
    tm = min(512, s_len)
    dts = (BF16, BF16, BF16, BF16, BF16, BF16, F32, F32)

    def body(x_ref, g_ref, w_ref, h_ref, *outs):
        xv = x_ref[...]
        h = (xv * _rms_rstd(xv) * g_ref[...]).astype(BF16)
        h_ref[...] = h
        for p, o_ref in enumerate(outs):
            val = _dot_nt(h, w_ref[IN_OFFS[p]:IN_OFFS[p + 1], :])
            if p == 3:
                val = val * Q_SCALE
            o_ref[...] = val.astype(dts[p])

    row = lambda i: (i, 0)
    return pl.pallas_call(
        body, name="proj_fwd",
        grid=(s_len // tm,),
        in_specs=[pl.BlockSpec((tm, D_MODEL), row), pl.BlockSpec((1, D_MODEL), lambda i: (0, 0)),
                  pl.BlockSpec((IN_W, D_MODEL), lambda i: (0, 0))],
        out_specs=[pl.BlockSpec((tm, D_MODEL), row)] + [pl.BlockSpec((tm, w), row) for w in IN_SIZES],
        out_shape=[jax.ShapeDtypeStruct((s_len, D_MODEL), BF16)]
        + [jax.ShapeDtypeStruct((s_len, w), dt) for w, dt in zip(IN_SIZES, dts)],
        compiler_params=_cparams("parallel"),
    )(x1, g, wint)


def _proj_bwd(dpieces, dx2, x1, g, wint):
    s_len = x1.shape[0]
    tm = min(512, s_len)

    def body(*refs):
        dps = refs[:8]
        dx2_ref, x_ref, g_ref, w_ref, dx_ref, dg_ref = refs[8:]

        @pl.when(pl.program_id(0) == 0)
        def _():
            dg_ref[...] = jnp.zeros_like(dg_ref)

        dh = _dot(dps[0][...], w_ref[IN_OFFS[0]:IN_OFFS[1], :])
        for p in range(1, 8):
            dh += _dot(dps[p][...], w_ref[IN_OFFS[p]:IN_OFFS[p + 1], :])
        xv = x_ref[...]
        dx, dg = _rms_bwd(dh, xv, _rms_rstd(xv), g_ref[...])
        dx_ref[...] = dx2_ref[...] + dx
        dg_ref[...] += dg

    row = lambda i: (i, 0)
    return pl.pallas_call(
        body, name="proj_bwd",
        grid=(s_len // tm,),
        in_specs=[pl.BlockSpec((tm, w), row) for w in IN_SIZES]
        + [pl.BlockSpec((tm, D_MODEL), row), pl.BlockSpec((tm, D_MODEL), row),
           pl.BlockSpec((1, D_MODEL), lambda i: (0, 0)), pl.BlockSpec((IN_W, D_MODEL), lambda i: (0, 0))],
        out_specs=[pl.BlockSpec((tm, D_MODEL), row), pl.BlockSpec((1, D_MODEL), lambda i: (0, 0))],
        out_shape=[jax.ShapeDtypeStruct((s_len, D_MODEL), F32), jax.ShapeDtypeStruct((1, D_MODEL), F32)],
        compiler_params=_cparams("arbitrary"),
    )(*dpieces, dx2, x1, g, wint)


def _merge_fwd(x1, oa, ob, ga, gb, wswa, wsb, wout):
    s_len = x1.shape[0]
    tm = min(512, s_len)

    def body(x_ref, oa_ref, ob_ref, ga_ref, gb_ref, wa_ref, wb_ref, wo_ref, xo_ref, mg_ref):
        pa = _dot(oa_ref[...], wa_ref[...])
        pb = _dot(ob_ref[...], wb_ref[...])
        mg = (jax.nn.sigmoid(ga_ref[...]) * pa + jax.nn.sigmoid(gb_ref[...]) * pb).astype(BF16)
        mg_ref[...] = mg
        xo_ref[...] = x_ref[...] + _dot(mg, wo_ref[...])

    row = lambda i: (i, 0)
    full = lambda i: (0, 0)
    return pl.pallas_call(
        body, name="merge_fwd",
        grid=(s_len // tm,),
        in_specs=[pl.BlockSpec((tm, D_MODEL), row), pl.BlockSpec((tm, 512), row), pl.BlockSpec((tm, 512), row),
                  pl.BlockSpec((tm, D_MODEL), row), pl.BlockSpec((tm, D_MODEL), row),
                  pl.BlockSpec((512, D_MODEL), full), pl.BlockSpec((512, D_MODEL), full),
                  pl.BlockSpec((D_MODEL, D_MODEL), full)],
        out_specs=[pl.BlockSpec((tm, D_MODEL), row), pl.BlockSpec((tm, D_MODEL), row)],
        out_shape=[jax.ShapeDtypeStruct((s_len, D_MODEL), F32), jax.ShapeDtypeStruct((s_len, D_MODEL), BF16)],
        compiler_params=_cparams("parallel"),
    )(x1, oa, ob, ga, gb, wswa, wsb, wout)


def _merge_bwd(dx2, oa, ob, ga, gb, wswa, wsb, wout):
    s_len = dx2.shape[0]
    tm = min(512, s_len)

    def body(dx_ref, oa_ref, ob_ref, ga_ref, gb_ref, wa_ref, wb_ref, wo_ref,
             doa_ref, dob_ref, dga_ref, dgb_ref, dpa_ref, dpb_ref, dxb_ref):
        dxb = dx_ref[...].astype(BF16)
        dxb_ref[...] = dxb
        dmg = _dot_nt(dxb, wo_ref[...])
        for o_ref, g_ref, w_ref, do_ref, dg_ref, dp_ref in (
                (oa_ref, ga_ref, wa_ref, doa_ref, dga_ref, dpa_ref),
                (ob_ref, gb_ref, wb_ref, dob_ref, dgb_ref, dpb_ref)):
            pv = _dot(o_ref[...], w_ref[...])
            sg = jax.nn.sigmoid(g_ref[...])
            dp = (dmg * sg).astype(BF16)
            dp_ref[...] = dp
            dg_ref[...] = (dmg * pv * sg * (1.0 - sg)).astype(BF16)
            do_ref[...] = _dot_nt(dp, w_ref[...]).astype(BF16)

    row = lambda i: (i, 0)
    full = lambda i: (0, 0)
    wide = pl.BlockSpec((tm, D_MODEL), row)
    half = pl.BlockSpec((tm, 512), row)
    return pl.pallas_call(
        body, name="merge_bwd",
        grid=(s_len // tm,),
        in_specs=[wide, half, half, wide, wide, pl.BlockSpec((512, D_MODEL), full),
                  pl.BlockSpec((512, D_MODEL), full), pl.BlockSpec((D_MODEL, D_MODEL), full)],
        out_specs=[half, half, wide, wide, wide, wide, wide],
        out_shape=[jax.ShapeDtypeStruct((s_len, 512), BF16)] * 2 + [jax.ShapeDtypeStruct((s_len, D_MODEL), BF16)] * 5,
        compiler_params=_cparams("parallel"),
    )(dx2, oa, ob, ga, gb, wswa, wsb, wout)


def _loss_fwd_bwd(x3, tgt, g):
    s_len = x3.shape[0]
    tm = min(1024, s_len)

    def body(x_ref, t_ref, g_ref, dx_ref, loss_ref, dg_ref):
        @pl.when(pl.program_id(0) == 0)
        def _():
            loss_ref[...] = jnp.zeros_like(loss_ref)
            dg_ref[...] = jnp.zeros_like(dg_ref)

        xv = x_ref[...]
        gv = g_ref[...]
        r = _rms_rstd(xv)
        err = xv * r * gv - t_ref[...]
        loss_ref[...] += 0.5 * jnp.sum(jnp.mean(err * err, axis=-1, keepdims=True), axis=0, keepdims=True)
        dx, dg = _rms_bwd(err * (1.0 / D_MODEL), xv, r, gv)
        dx_ref[...] = dx
        dg_ref[...] += dg

    row = lambda i: (i, 0)
    return pl.pallas_call(
        body, name="loss_fwd_bwd",
        grid=(s_len // tm,),
        in_specs=[pl.BlockSpec((tm, D_MODEL), row), pl.BlockSpec((tm, D_MODEL), row),
                  pl.BlockSpec((1, D_MODEL), lambda i: (0, 0))],
        out_specs=[pl.BlockSpec((tm, D_MODEL), row), pl.BlockSpec((1, 1), lambda i: (0, 0)),
                   pl.BlockSpec((1, D_MODEL), lambda i: (0, 0))],
        out_shape=[jax.ShapeDtypeStruct((s_len, D_MODEL), F32), jax.ShapeDtypeStruct((1, 1), F32),
                   jax.ShapeDtypeStruct((1, D_MODEL), F32)],
        compiler_params=_cparams("arbitrary"),
    )(x3, tgt, g)


def _rel_bucket_matrix():
    qi = jnp.arange(SWA_BLOCK)[:, None] + SWA_BLOCK
    kj = jnp.arange(2 * SWA_BLOCK)[None, :]
    dist = jnp.maximum(qi - kj, 0)
    max_exact = REL_BUCKETS // 2
    d = jnp.maximum(dist, 1).astype(F32)
    large = max_exact + (jnp.log(d / max_exact) / np.log(REL_MAX_DIST / max_exact)
                         * (REL_BUCKETS - max_exact)).astype(jnp.int32)
    large = jnp.minimum(large, REL_BUCKETS - 1)
    return jnp.where(dist < max_exact, dist, large).astype(jnp.int32)


def _swa_bias_into(bias_ref, bkt_ref, tab_ref):
    bk = bkt_ref[...]
    for h in range(N_HEADS):
        acc = jnp.zeros(bk.shape, F32)
        for bucket in range(REL_BUCKETS):
            acc = jnp.where(bk == bucket, tab_ref[bucket, h], acc)
        bias_ref[h] = acc


def _swa_valid(n):
    shape = (SWA_BLOCK, 2 * SWA_BLOCK)
    row = lax.broadcasted_iota(jnp.int32, shape, 0)
    col = lax.broadcasted_iota(jnp.int32, shape, 1)
    dist = row + SWA_BLOCK - col
    return (dist >= 0) & (dist < SWA_BLOCK) & ((col >= SWA_BLOCK) | (n > 0))


def _swa_windows(kp_ref, kc_ref, vp_ref, vc_ref):
    return (jnp.concatenate([kp_ref[...], kc_ref[...]], axis=0), jnp.concatenate([vp_ref[...], vc_ref[...]], axis=0))


def _swa_place(h):
    return slice(h // 2 * LANES, (h // 2 + 1) * LANES), h % 2, h // SWA_GROUP


def _move_half(x, src, dst):
    moved = x if src == dst else pltpu.roll(x, HEAD_DIM, 1)
    in_dst = (lax.broadcasted_iota(jnp.int32, x.shape, 1) >= HEAD_DIM) == bool(dst)
    return jnp.where(in_dst, moved, 0.0)


def _swa_probs(qk, bias, sink, valid):
    lg = jnp.where(valid, qk * Q_SCALE + bias, NEG_BIG)
    m = jnp.maximum(jnp.max(lg, axis=-1, keepdims=True), sink)
    e = jnp.exp(lg - m)
    es = jnp.exp(sink - m)
    inv = 1.0 / (jnp.sum(e, axis=-1, keepdims=True) + es)
    return e * inv, es * inv


def _swa_specs(s_len):
    blk = SWA_BLOCK
    cur = lambda n: (n, 0)
    prev = lambda n: (jnp.maximum(n - 1, 0), 0)
    kvw = SWA_KV_HEADS * HEAD_DIM
    return [pl.BlockSpec(memory_space=pltpu.SMEM), pl.BlockSpec(memory_space=pltpu.SMEM),
            pl.BlockSpec((blk, 2 * blk), lambda n: (0, 0)),
            pl.BlockSpec((blk, N_HEADS * HEAD_DIM), cur),
            pl.BlockSpec((blk, kvw), prev), pl.BlockSpec((blk, kvw), cur),
            pl.BlockSpec((blk, kvw), prev), pl.BlockSpec((blk, kvw), cur)]


def _swa_fwd(tab, sinks, bkt, q, k, v):
    s_len = q.shape[0]
    blk = SWA_BLOCK

    def body(tab_ref, sink_ref, bkt_ref, q_ref, kp_ref, kc_ref, vp_ref, vc_ref, o_ref, bias_ref):
        n = pl.program_id(0)

        @pl.when(n == 0)
        def _():
            _swa_bias_into(bias_ref, bkt_ref, tab_ref)

        valid = _swa_valid(n)
        kk, vv = _swa_windows(kp_ref, kc_ref, vp_ref, vc_ref)
        st = {}

        def s_logits(h):
            tile, mine, kv = _swa_place(h)
            st[h, "lg"] = _dot_nt(_move_half(q_ref[:, tile].astype(F32), mine, kv).astype(BF16), kk)

        def s_probs(h):
            st[h, "p"] = _swa_probs(st.pop((h, "lg")), bias_ref[h], sink_ref[0, h], valid)[0].astype(BF16)

        def s_values(h):
            tile, mine, kv = _swa_place(h)
            part = _move_half(_dot(st.pop((h, "p")), vv), kv, mine)
            if mine == 0:
                st[h + 1, "o"] = part
            else:
                o_ref[:, tile] = (st.pop((h, "o")) + part).astype(BF16)

        _emit_skewed((list(range(N_HEADS)), [s_logits, s_probs, s_values]))

    return pl.pallas_call(
        body, name="swa_fwd",
        grid=(s_len // blk,),
        in_specs=_swa_specs(s_len),
        out_specs=pl.BlockSpec((blk, N_HEADS * HEAD_DIM), lambda n: (n, 0)),
        out_shape=jax.ShapeDtypeStruct((s_len, N_HEADS * HEAD_DIM), BF16),
        scratch_shapes=[pltpu.VMEM((N_HEADS, blk, 2 * blk), F32)],
        compiler_params=_cparams("arbitrary"),
    )(tab, sinks, bkt, q, k, k, v, v)


def _swa_bwd(tab, sinks, bkt, q, k, v, do, comm=None):
    s_len = q.shape[0]
    blk = SWA_BLOCK
    nb = s_len // blk
    kvw = SWA_KV_HEADS * HEAD_DIM

    def body(tab_ref, sink_ref, bkt_ref, q_ref, kp_ref, kc_ref, vp_ref, vc_ref, do_ref,
             dq_ref, dk_ref, dv_ref, dtab_ref, dsink_ref, bias_ref, dbias_ref):
        n = pl.program_id(0)

        @pl.when(n == 0)
        def _():
            _swa_bias_into(bias_ref, bkt_ref, tab_ref)
            dbias_ref[...] = jnp.zeros_like(dbias_ref)
            dk_ref[...] = jnp.zeros_like(dk_ref)
            dv_ref[...] = jnp.zeros_like(dv_ref)
            dsink_ref[...] = jnp.zeros_like(dsink_ref)
            dtab_ref[...] = jnp.zeros_like(dtab_ref)

        valid = _swa_valid(n)
        cur_rows = pl.ds(pl.multiple_of(n * blk, blk), blk)
        prev_rows = pl.ds(pl.multiple_of(jnp.maximum(n - 1, 0) * blk, blk), blk)
        kk, vv = _swa_windows(kp_ref, kc_ref, vp_ref, vc_ref)
        st = {}

        def s_logits(h):
            tile, mine, kv = _swa_place(h)
            st[h, "q"] = _move_half(q_ref[:, tile].astype(F32), mine, kv).astype(BF16)
            st[h, "do"] = _move_half(do_ref[:, tile].astype(F32), mine, kv).astype(BF16)
            st[h, "lg"] = _dot_nt(st[h, "q"], kk)
            st[h, "dp"] = _dot_nt(st[h, "do"], vv)

        def s_probs(h):
            p, ps = _swa_probs(st.pop((h, "lg")), bias_ref[h], sink_ref[0, h], valid)
            dp = st.pop((h, "dp"))
            delta = jnp.sum(p * dp, axis=-1, keepdims=True)
            dl = p * (dp - delta)
            dsink_ref[h:h + 1, :] += jnp.broadcast_to(-jnp.sum(ps * delta, axis=0, keepdims=True), (1, LANES))
            dbias_ref[h] += dl
            st[h, "dl"], st[h, "p"] = dl.astype(BF16), p.astype(BF16)

        def s_products(h):
            tile, mine, kv = _swa_place(h)
            dlb = st.pop((h, "dl"))
            part = _move_half(Q_SCALE * _dot(dlb, kk), kv, mine)
            if mine == 0:
                st[h + 1, "dq"] = part
            else:
                dq_ref[:, tile] = (st.pop((h, "dq")) + part).astype(BF16)
            dk_win = Q_SCALE * _dot_tn(dlb, st.pop((h, "q")))
            dv_win = _dot_tn(st.pop((h, "p")), st.pop((h, "do")))
            dk_ref[prev_rows, :] += dk_win[:blk]
            dv_ref[prev_rows, :] += dv_win[:blk]
            dk_ref[cur_rows, :] += dk_win[blk:]
            dv_ref[cur_rows, :] += dv_win[blk:]

        _emit_skewed((list(range(N_HEADS)), [s_logits, s_probs, s_products]))

        @pl.when(n == nb - 1)
        def _():
            bk = bkt_ref[...]
            lane = lax.broadcasted_iota(jnp.int32, (1, LANES), 1)
            for bucket in range(REL_BUCKETS):
                rowv = jnp.zeros((1, LANES), F32)
                for h in range(N_HEADS):
                    val = jnp.sum(jnp.where(bk == bucket, dbias_ref[h], 0.0), axis=1, keepdims=True)
                    val = jnp.sum(val, axis=0, keepdims=True)
                    rowv = jnp.where(lane == h, val, rowv)
                dtab_ref[bucket:bucket + 1, :] = rowv

    return _call(
        body, (tab, sinks, bkt, q, k, k, v, v, do), comm=comm, **_grid_ends(nb), name="swa_bwd",
        grid=(nb,),
        in_specs=_swa_specs(s_len) + [pl.BlockSpec((blk, N_HEADS * HEAD_DIM), lambda n: (n, 0))],
        out_specs=[pl.BlockSpec((blk, N_HEADS * HEAD_DIM), lambda n: (n, 0)),
                   pl.BlockSpec((s_len, kvw), lambda n: (0, 0)), pl.BlockSpec((s_len, kvw), lambda n: (0, 0)),
                   pl.BlockSpec((REL_BUCKETS, LANES), lambda n: (0, 0)), pl.BlockSpec((N_HEADS, LANES), lambda n: (0, 0))],
        out_shape=[jax.ShapeDtypeStruct((s_len, N_HEADS * HEAD_DIM), BF16),
                   jax.ShapeDtypeStruct((s_len, kvw), F32), jax.ShapeDtypeStruct((s_len, kvw), F32),
                   jax.ShapeDtypeStruct((REL_BUCKETS, LANES), F32), jax.ShapeDtypeStruct((N_HEADS, LANES), F32)],
        scratch_shapes=[pltpu.VMEM((N_HEADS, blk, 2 * blk), F32), pltpu.VMEM((N_HEADS, blk, 2 * blk), F32)],
        compiler_params=_cparams("arbitrary"),
    )


def _sb_terms(z, valid):
    zc = jnp.minimum(z, SB_LOGIT_CAP)
    lk = -jnp.log(1.0 + jnp.exp(zc))
    lsz = zc + lk
    return lsz, (lk if valid is None else jnp.where(valid, lk, 0.0))


def _bf16_parts(vals):
    parts, rest = [], vals
    for n in range(SB_SUM_PARTS):
        parts.append(rest.astype(BF16))
        if n + 1 < SB_SUM_PARTS:
            rest = rest - parts[-1].astype(F32)
    return parts[0] if len(parts) == 1 else jnp.concatenate(parts, axis=1)


def _row_sum_lanes(vals):
    return jnp.broadcast_to(jnp.sum(vals, axis=-1, keepdims=True), (vals.shape[0], LANES))


def _emit_skewed(*groups):
    for step in range(max(len(items) + len(stages) - 1 for items, stages in groups)):
        for items, stages in groups:
            for s, stage in enumerate(stages):
                if 0 <= step - s < len(items):
                    stage(items[step - s])


def _sb_items(edge):
    items = []
    for h in range(2):
        for r0 in range(0, SB_QUERIES, SB_ROWS):
            if edge is None or r0 >= (edge + 1) * SB_KEYS:
                items.append((h, r0, False))
            elif r0 + SB_ROWS - 1 > edge * SB_KEYS:
                items.append((h, r0, True))
    return items


def _sb_valid(w, edge):
    row = lax.broadcasted_iota(jnp.int32, (SB_ROWS, SB_KEYS), 0) + w[1]
    col = lax.broadcasted_iota(jnp.int32, (SB_ROWS, SB_KEYS), 1) + edge * SB_KEYS
    return col < row


def _sb_consts(tq, tk):
    low = lax.broadcasted_iota(jnp.int32, (tq, LANES), 1) < HEAD_DIM
    row = lax.broadcasted_iota(jnp.int32, (tk, tk), 0)
    col = lax.broadcasted_iota(jnp.int32, (tk, tk), 1)
    right = (row > col).astype(BF16)
    left = (row < col).astype(BF16)
    return low, jnp.concatenate([right] * SB_SUM_PARTS, axis=0), jnp.concatenate([left] * SB_SUM_PARTS, axis=0)


def _sb_fwd(q, k, v, comm=None):
    s_len = q.shape[0]
    tq, tk, tr = SB_QUERIES, SB_KEYS, SB_ROWS
    nk, ratio = s_len // tk, tq // tk
    assert nk <= LANES

    def body(q_ref, k_ref, v_ref, o_ref, car_ref, c_ref, oacc_ref, logw_ref, lksum_ref):
        i = pl.program_id(1)
        qv = q_ref[...]
        low, tri2, _ = _sb_consts(tq, tk)
        lane = lax.broadcasted_iota(jnp.int32, (tr, LANES), 1)
        zero = jnp.zeros_like(qv)
        q_heads = (jnp.where(low, qv, zero), jnp.where(low, zero, qv))
        c_ref[...] = jnp.zeros_like(c_ref)
        oacc_ref[...] = jnp.zeros_like(oacc_ref)
        car_ref[...] = jnp.full_like(car_ref, NEG_BIG)

        def front(j, edge):
            keys = k_ref[pl.ds(pl.multiple_of(j * tk, tk), tk), :]
            slot = j % 2
            st = {}

            def s_logits(w):
                st[w, "z"] = _dot_nt(q_heads[w[0]][w[1]:w[1] + tr], keys)

            def s_terms(w):
                valid = _sb_valid(w, edge) if w[2] else None
                lsz, lk = _sb_terms(st.pop((w, "z")), valid)
                st[w, "parts"] = _bf16_parts(lk)
                st[w, "lsz"] = lsz if valid is None else jnp.where(valid, lsz, NEG_BIG)
                lksum_ref[slot, w[0], w[1]:w[1] + tr, :] = _row_sum_lanes(lk)

            def s_suffix(w):
                logw_ref[slot, w[0], w[1]:w[1] + tr, :] = st.pop((w, "lsz")) + _dot(st.pop((w, "parts")), tri2)

            return _sb_items(edge), [s_logits, s_terms, s_suffix]

        def back(j, edge):
            vv = v_ref[pl.ds(pl.multiple_of(j * tk, tk), tk), :]
            slot = j % 2
            st = {}

            def s_weights(w):
                h, rs = w[0], slice(w[1], w[1] + tr)
                c = c_ref[h, rs, :]
                st[w, "a"] = jnp.exp(logw_ref[slot, h, rs, :] + jnp.tile(c, (1, tk // LANES))).astype(BF16)
                car_ref[h, rs, :] = jnp.where(lane == j, c, car_ref[h, rs, :])
                c_ref[h, rs, :] = c + lksum_ref[slot, h, rs, :]

            def s_values(w):
                oacc_ref[w[0], w[1]:w[1] + tr, :] += _dot(st.pop((w, "a")), vv)

            return _sb_items(edge), [s_weights, s_values]

        first = i * ratio
        _emit_skewed(front(first + ratio - 1, ratio - 1))
        for m in reversed(range(ratio - 1)):
            _emit_skewed(front(first + m, m), back(first + m + 1, m + 1))

        @pl.when(i == 0)
        def _():
            _emit_skewed(back(0, 0))

        def alive():
            return (jnp.max(c_ref[...]) >= SB_DEAD_CARRY).astype(jnp.int32)

        @pl.when(i > 0)
        def _():
            _emit_skewed(front(first - 1, None), back(first, 0))

            def step(state):
                pending, _ = state
                _emit_skewed(front(pending - 1, None), back(pending, None))
                return pending - 1, alive()

            pending, live = lax.while_loop(lambda s: (s[0] > 0) & (s[1] > 0), step, (first - 1, alive()))

            @pl.when(live > 0)
            def _():
                _emit_skewed(back(pending, None))

        o_ref[...] = jnp.where(low, oacc_ref[0], oacc_ref[1]).astype(BF16)

    return _call(
        body, (q, k, v), comm=comm, **_grid_ends(N_HEADS // 2, s_len // tq), name="sb_fwd",
        grid=(N_HEADS // 2, s_len // tq),
        in_specs=[pl.BlockSpec((tq, LANES), lambda p, i: (i, p)),
                  pl.BlockSpec((s_len, LANES), lambda p, i: (0, p)),
                  pl.BlockSpec((s_len, LANES), lambda p, i: (0, p))],
        out_specs=[pl.BlockSpec((tq, LANES), lambda p, i: (i, p)), pl.BlockSpec((2, tq, LANES), lambda p, i: (p, i, 0))],
        out_shape=[jax.ShapeDtypeStruct((s_len, N_HEADS * HEAD_DIM), BF16),
                   jax.ShapeDtypeStruct((N_HEADS, s_len, LANES), F32)],
        scratch_shapes=[pltpu.VMEM((2, tq, LANES), F32), pltpu.VMEM((2, tq, LANES), F32),
                        pltpu.VMEM((2, 2, tq, tk), F32), pltpu.VMEM((2, 2, tq, LANES), F32)],
        compiler_params=_cparams("arbitrary", "arbitrary"),
    )


def _sb_bwd(q, k, v, do, cars):
    s_len = q.shape[0]
    tq, tk, tr = SB_QUERIES, SB_KEYS, SB_ROWS
    nk, ratio = s_len // tk, tq // tk

    def body(q_ref, k_ref, v_ref, do_ref, car_ref, dq_ref, dk_ref, dv_ref,
             gleft_ref, dqacc_ref, dkacc_ref, dvacc_ref, logw_ref, lsz_ref, da_ref, a_ref, dz_ref):
        i = pl.program_id(1)

        @pl.when(i == 0)
        def _():
            dkacc_ref[...] = jnp.zeros_like(dkacc_ref)
            dvacc_ref[...] = jnp.zeros_like(dvacc_ref)

        qv = q_ref[...]
        dov = do_ref[...]
        low, tri_right2, tri_left2 = _sb_consts(tq, tk)
        lane = lax.broadcasted_iota(jnp.int32, (tr, LANES), 1)
        zero = jnp.zeros_like(qv)
        q_heads = (jnp.where(low, qv, zero), jnp.where(low, zero, qv))
        do_heads = (jnp.where(low, dov, zero), jnp.where(low, zero, dov))
        q_t = qv.astype(F32).T.astype(BF16)
        do_t = dov.astype(F32).T.astype(BF16)
        gleft_ref[...] = jnp.zeros_like(gleft_ref)
        dqacc_ref[...] = jnp.zeros_like(dqacc_ref)

        def front(j, edge):
            key_rows = pl.ds(pl.multiple_of(j * tk, tk), tk)
            keys, values = k_ref[key_rows, :], v_ref[key_rows, :]
            slot = j % 2
            st = {}

            def s_logits(w):
                h, rs = w[0], slice(w[1], w[1] + tr)
                st[w, "z"] = _dot_nt(q_heads[h][rs], keys)
                da_ref[slot, h, rs, :] = _dot_nt(do_heads[h][rs], values)

            def s_terms(w):
                h, rs = w[0], slice(w[1], w[1] + tr)
                valid = _sb_valid(w, edge) if w[2] else None
                lsz, lk = _sb_terms(st.pop((w, "z")), valid)
                st[w, "parts"] = _bf16_parts(lk)
                lsz = lsz if valid is None else jnp.where(valid, lsz, NEG_BIG)
                lsz_ref[slot, h, rs, :] = lsz
                st[w, "lszc"] = lsz + jnp.sum(jnp.where(lane == j, car_ref[h, rs, :], 0.0), axis=-1, keepdims=True)

            def s_suffix(w):
                logw_ref[slot, w[0], w[1]:w[1] + tr, :] = st.pop((w, "lszc")) + _dot(st.pop((w, "parts")), tri_right2)

            return _sb_items(edge), [s_logits, s_terms, s_suffix]

        def back(j, edge):
            kv = k_ref[pl.ds(pl.multiple_of(j * tk, tk), tk), :]
            slot = j % 2
            st = {}

            items = _sb_items(edge)
            head_rows = [[r0 for hh, r0, _ in items if hh == h] for h in range(2)]

            def s_weights(w):
                h, rs = w[0], slice(w[1], w[1] + tr)
                a = jnp.exp(logw_ref[slot, h, rs, :])
                g = a * da_ref[slot, h, rs, :]
                a_ref[h, rs, :] = a.astype(BF16)
                st[w, "g"], st[w, "parts"] = g, _bf16_parts(g)

            def s_prefix(w):
                st[w, "gs"] = _dot(st.pop((w, "parts")), tri_left2)

            def s_dz(w):
                h, rs = w[0], slice(w[1], w[1] + tr)
                g = st.pop((w, "g"))
                gleft = gleft_ref[h, rs, :]
                gsum = st.pop((w, "gs")) + jnp.tile(gleft, (1, tk // LANES))
                dz = (g - jnp.exp(lsz_ref[slot, h, rs, :]) * (g + gsum)).astype(BF16)
                st[w, "dz"] = dz
                dz_ref[h, rs, :] = dz
                gleft_ref[h, rs, :] = gleft + _row_sum_lanes(g)

            def s_products(w):
                h, rs = w[0], slice(w[1], w[1] + tr)
                dqacc_ref[h, rs, :] += _dot(st.pop((w, "dz")), kv)
                if w[1] == head_rows[h][-1]:
                    feat = slice(h * HEAD_DIM, (h + 1) * HEAD_DIM)
                    hr = slice(head_rows[h][0], tq)
                    dkacc_ref[j, feat, :] += _dot(q_t[feat, hr], dz_ref[h, hr, :])
                    dvacc_ref[j, feat, :] += _dot(do_t[feat, hr], a_ref[h, hr, :])

            return items, [s_weights, s_prefix, s_dz, s_products]

        first = i * ratio
        tile_max = jnp.max(jnp.maximum(car_ref[0], car_ref[1]), axis=0, keepdims=True)
        start = jnp.clip(first + ratio - jnp.sum(jnp.where(tile_max >= SB_DEAD_CARRY, 1, 0)), 0, first)

        @pl.when(start == first)
        def _():
            _emit_skewed(front(first, 0))

        @pl.when(start < first)
        def _():
            _emit_skewed(front(start, None))

            def step(jj, carry):
                _emit_skewed(front(jj, None), back(jj - 1, None))
                return carry

            lax.fori_loop(start + 1, first, step, 0)
            _emit_skewed(front(first, 0), back(first - 1, None))

        for m in range(1, ratio):
            _emit_skewed(front(first + m, m), back(first + m - 1, m - 1))
        _emit_skewed(back(first + ratio - 1, ratio - 1))
        dq_ref[...] = (Q_SCALE * jnp.where(low, dqacc_ref[0], dqacc_ref[1])).astype(BF16)

        @pl.when(i == s_len // tq - 1)
        def _():
            for j in range(nk):
                dk_ref[j * tk:(j + 1) * tk, :] = dkacc_ref[j].T.astype(BF16)
                dv_ref[j * tk:(j + 1) * tk, :] = dvacc_ref[j].T.astype(BF16)

    qblk = pl.BlockSpec((tq, LANES), lambda p, i: (i, p))
    col_full = pl.BlockSpec((s_len, LANES), lambda p, i: (0, p))
    return pl.pallas_call(
        body, name="sb_bwd",
        grid=(N_HEADS // 2, s_len // tq),
        in_specs=[qblk, col_full, col_full, qblk, pl.BlockSpec((2, tq, LANES), lambda p, i: (p, i, 0))],
        out_specs=[qblk, col_full, col_full],
        out_shape=[jax.ShapeDtypeStruct((s_len, N_HEADS * HEAD_DIM), BF16)] * 3,
        scratch_shapes=[pltpu.VMEM((2, tq, LANES), F32), pltpu.VMEM((2, tq, LANES), F32),
                        pltpu.VMEM((nk, LANES, tk), F32), pltpu.VMEM((nk, LANES, tk), F32)]
        + [pltpu.VMEM((2, 2, tq, tk), F32)] * 3 + [pltpu.VMEM((2, tq, tk), BF16)] * 2,
        compiler_params=_cparams("parallel", "arbitrary"),
    )(q, k, v, do, cars)


def _local_step(xs, tgt, gains, sinks, rel_bias, weights_of, ship):
    g1, gmix, g2, gfin = gains
    bkt = _rel_bucket_matrix()
    grads = {}

    def carried(outs, comm, count):
        return outs[:count], (list(outs[count:]) if comm is not None else None)

    wts = dict(weights_of(0, None))
    comm = ship("weights", 1)
    (x1, h1, a1, b1, u1), landed = carried(
        _ffn_fwd(xs, g1, wts["ffn1_w1t"], wts["ffn1_w3t"], wts["ffn1_w2"], "1", comm), comm, 5)
    wts.update(weights_of(1, landed))
    hm, qa, ka, va, qb, kb, vb, ga, gb = _proj_fwd(x1, gmix, wts["w_int"])
    oa = _swa_fwd(rel_bias, sinks, bkt, qa, ka, va)
    comm = ship("weights", 2)
    (ob, cars), landed = carried(_sb_fwd(qb, kb, vb, comm), comm, 2)
    wts.update(weights_of(2, landed))
    x2, mg = _merge_fwd(x1, oa, ob, ga, gb, wts["w_swa"], wts["w_sb"], wts["w_out"])
    x3, h3, a3, b3, u3 = _ffn_fwd(x2, g2, wts["ffn2_w1t"], wts["ffn2_w3t"], wts["ffn2_w2"], "2")
    dx3, loss, dgfin = _loss_fwd_bwd(x3, tgt, gfin)

    dx2, dg2, da3, db3, dx3b = _ffn_bwd(dx3, x2, g2, a3, b3, wts["ffn2_w1t"], wts["ffn2_w3t"], wts["ffn2_w2"], "2")
    big = {"ffn2_w1t": _matmul_tn(da3, h3, "ffn2_w1"), "ffn2_w3t": _matmul_tn(db3, h3, "ffn2_w3"),
           "ffn2_w2": _matmul_tn(u3, dx3b, "ffn2_w2")}

    doa, dob, dga, dgb, dpa, dpb, dx2b = _merge_bwd(dx2, oa, ob, ga, gb, wts["w_swa"], wts["w_sb"], wts["w_out"])
    comm = ship("grads", GROUPS[2], big)
    (dqa, dka, dva, dtab, dsink), landed = carried(_swa_bwd(rel_bias, sinks, bkt, qa, ka, va, doa, comm), comm, 5)
    grads[GROUPS[2]] = big if comm is None else landed[0]

    big = {"w_out": _matmul_tn(mg, dx2b, "w_out"), "w_swa": _matmul_tn(oa, dpa, "w_swa"),
           "w_sb": _matmul_tn(ob, dpb, "w_sb")}
    dqb, dkb, dvb = _sb_bwd(qb, kb, vb, dob, cars)
    dpieces = (dqa, dka.astype(BF16), dva.astype(BF16), dqb, dkb, dvb, dga, dgb)
    big["w_int"] = jnp.concatenate([_matmul_tn(dp, hm, f"w_in{p}") for p, dp in enumerate(dpieces)], axis=0)
    dx1, dgmix = _proj_bwd(dpieces, dx2, x1, gmix, wts["w_int"])

    comm = ship("grads", GROUPS[1], big)
    (dx0, dg1, da1, db1, dx1b), landed = carried(
        _ffn_bwd(dx1, xs, g1, a1, b1, wts["ffn1_w1t"], wts["ffn1_w3t"], wts["ffn1_w2"], "1", comm), comm, 5)
    grads[GROUPS[1]] = big if comm is None else landed[0]

    prev = None
    for name, lhs, rhs in (("ffn1_w1", da1, h1), ("ffn1_w3", db1, h1), ("ffn1_w2", u1, dx1b)):
        comm = None if prev is None else ship("grads", (prev[0],), prev[1])
        res = _matmul_tn(lhs, rhs, name, comm)
        if prev is not None:
            grads[(prev[0],)] = prev[1] if comm is None else res[1]
        prev = (name, {_GRAD_KEY[name]: res if comm is None else res[0]})
    grads[(prev[0],)] = prev[1]

    small = {"gains": (dg1, dgmix, dg2, dgfin), "sinks": dsink[:, 0], "rel_bias": dtab[:, :N_HEADS]}
    return loss, dx0, small, grads


def _my_place():
    return lax.axis_index("x"), lax.axis_index("y"), lax.axis_index("c")


def _flip(v, bit):
    return 1 - v if bit else v


_RELATIONS = tuple((k >> 2 & 1, k >> 1 & 1, k & 1) for k in range(1, N_DEV))


def _gather_weights(blocks, tag):
    count = len(blocks)

    def body(*refs):
        x_refs, out_refs = refs[:count], refs[count:2 * count]
        send_sems, recv_sems, local_sems = refs[2 * count:]
        x, y, c = _my_place()
        me, sibling = (x, y, c), (x, y, 1 - c)
        chips = [(1 - x, y), (x, 1 - y), (1 - x, 1 - y)]

        def rows(s, px, py, pc):
            return out_refs[s].at[4 * px + 2 * py + pc]

        def copy(s, k, block, to, src=None):
            return pltpu.make_async_remote_copy(
                src_ref=rows(s, *block) if src is None else src, dst_ref=rows(s, *block),
                send_sem=send_sems.at[s, k], recv_sem=recv_sems.at[s, k],
                device_id=to, device_id_type=pl.DeviceIdType.MESH)

        mine = [pltpu.make_async_copy(x_refs[s], rows(s, *me), local_sems.at[s]) for s in range(count)]
        first, passed = [], []
        for s in range(count):
            mine[s].start()
            first.append(copy(s, 0, me, sibling, src=x_refs[s]))
            first += [copy(s, 1 + j, me, (*chip, c), src=x_refs[s]) for j, chip in enumerate(chips)]
        for cp in first:
            cp.start()
        for s in range(count):
            for j, chip in enumerate(chips):
                copy(s, 1 + j, (*chip, c), me).wait_recv()
                passed.append(copy(s, 4 + j, (*chip, c), sibling))
                passed[-1].start()
        for s in range(count):
            copy(s, 0, sibling, me).wait_recv()
            for j, chip in enumerate(chips):
                copy(s, 4 + j, (*chip, 1 - c), me).wait_recv()
        for cp in first + passed:
            cp.wait_send()
        for cp in mine:
            cp.wait()

    anywhere = pl.BlockSpec(memory_space=pl.ANY)
    return pl.pallas_call(
        body, name=f"gather_weights_{tag}",
        out_shape=[jax.ShapeDtypeStruct((N_DEV,) + b.shape, b.dtype) for b in blocks],
        in_specs=[anywhere] * count, out_specs=[anywhere] * count,
        scratch_shapes=[pltpu.SemaphoreType.DMA((count, N_DEV - 1)), pltpu.SemaphoreType.DMA((count, N_DEV - 1)),
                        pltpu.SemaphoreType.DMA((count,))],
    )(*blocks)


def _exchange_grads(gp, tag):
    def body(g_ref, out_ref, send_sems, recv_sems, local_sem):
        x, y, c = _my_place()
        me = 4 * x + 2 * y + c
        mine = pltpu.make_async_copy(g_ref.at[me], out_ref.at[me], local_sem)
        mine.start()
        copies = []
        for k, (fx, fy, fc) in enumerate(_RELATIONS):
            px, py, pc = _flip(x, fx), _flip(y, fy), _flip(c, fc)
            peer = 4 * px + 2 * py + pc
            copies.append((
                pltpu.make_async_remote_copy(
                    src_ref=g_ref.at[peer], dst_ref=out_ref.at[me], send_sem=send_sems.at[k], recv_sem=recv_sems.at[k],
                    device_id=(px, py, pc), device_id_type=pl.DeviceIdType.MESH),
                pltpu.make_async_remote_copy(
                    src_ref=g_ref.at[peer], dst_ref=out_ref.at[peer], send_sem=send_sems.at[k], recv_sem=recv_sems.at[k],
                    device_id=(px, py, pc), device_id_type=pl.DeviceIdType.MESH)))
        for out_cp, _ in copies:
            out_cp.start()
        for _, in_cp in copies:
            in_cp.wait_recv()
        for out_cp, _ in copies:
            out_cp.wait_send()
        mine.wait()

    return pl.pallas_call(
        body, name=f"exchange_grads_{tag}",
        out_shape=jax.ShapeDtypeStruct(gp.shape, gp.dtype),
        in_specs=[pl.BlockSpec(memory_space=pl.ANY)],
        out_specs=pl.BlockSpec(memory_space=pl.ANY),
        scratch_shapes=[pltpu.SemaphoreType.DMA((7,)), pltpu.SemaphoreType.DMA((7,)), pltpu.SemaphoreType.DMA(())],
    )(gp)


def _peers():
    x, y, c = _my_place()
    out = []
    for k, (fx, fy, fc) in enumerate(_RELATIONS):
        px, py, pc = _flip(x, fx), _flip(y, fy), _flip(c, fc)
        out.append((k, (px, py, pc), 4 * px + 2 * py + pc))
    return out, 4 * x + 2 * y + c


def _grid_ends(*grid):
    def first():
        return functools.reduce(lambda a, b: a & b, [pl.program_id(d) == 0 for d in range(len(grid))])

    def last():
        return functools.reduce(lambda a, b: a & b, [pl.program_id(d) == n - 1 for d, n in enumerate(grid)])

    return {"first": first, "last": last}


def _call(body, operands, *, comm=None, first=None, last=None, **kw):
    if comm is None:
        return pl.pallas_call(body, **kw)(*operands)
    in_specs, out_specs, out_shape = list(kw.pop("in_specs")), list(kw.pop("out_specs")), list(kw.pop("out_shape"))
    scratch = list(kw.pop("scratch_shapes", ()))
    n_in, n_out, n_scr, n_src = len(in_specs), len(out_specs), len(scratch), len(comm)

    def wrapped(*refs):
        ins, src_refs = refs[:n_in], refs[n_in:n_in + n_src]
        outs = refs[n_in + n_src:n_in + n_src + n_out]
        land_refs = refs[n_in + n_src + n_out:n_in + 2 * n_src + n_out]
        scr = refs[n_in + 2 * n_src + n_out:n_in + 2 * n_src + n_out + n_scr]
        send_sems, recv_sems, local_sems = refs[n_in + 2 * n_src + n_out + n_scr:]
        peers, me = _peers()
        mine, going, coming = [], [], []
        for s, (_, per_peer) in enumerate(comm):
            src_ref, land_ref = src_refs[s], land_refs[s]
            mine.append(pltpu.make_async_copy(src_ref.at[me] if per_peer else src_ref, land_ref.at[me], local_sems.at[s]))
            for k, where, slab in peers:
                piece = src_ref.at[slab] if per_peer else src_ref
                going.append(pltpu.make_async_remote_copy(
                    src_ref=piece, dst_ref=land_ref.at[me], send_sem=send_sems.at[s, k], recv_sem=recv_sems.at[s, k],
                    device_id=where, device_id_type=pl.DeviceIdType.MESH))
                coming.append(pltpu.make_async_remote_copy(
                    src_ref=piece, dst_ref=land_ref.at[slab], send_sem=send_sems.at[s, k], recv_sem=recv_sems.at[s, k],
                    device_id=where, device_id_type=pl.DeviceIdType.MESH))

        @pl.when(first())
        def _():
            for cp in mine + going:
                cp.start()

        body(*ins, *outs, *scr)

        @pl.when(last())
        def _():
            for cp in coming:
                cp.wait_recv()
            for cp in going:
                cp.wait_send()
            for cp in mine:
                cp.wait()

    anywhere = pl.BlockSpec(memory_space=pl.ANY)
    lands = [jax.ShapeDtypeStruct(src.shape if per_peer else (N_DEV,) + src.shape, src.dtype) for src, per_peer in comm]
    return pl.pallas_call(
        wrapped, in_specs=in_specs + [anywhere] * n_src, out_specs=out_specs + [anywhere] * n_src,
        out_shape=out_shape + lands,
        scratch_shapes=scratch + [pltpu.SemaphoreType.DMA((n_src, N_DEV - 1)), pltpu.SemaphoreType.DMA((n_src, N_DEV - 1)),
                                  pltpu.SemaphoreType.DMA((n_src,))],
        **kw)(*operands, *[src for src, _ in comm])


def _adamw(w, g, m, v):
    m = ADAM_B1 * m + (1.0 - ADAM_B1) * g
    v = ADAM_B2 * v + (1.0 - ADAM_B2) * jnp.square(g)
    m_hat = m / (1.0 - ADAM_B1 ** ADAM_STEP)
    v_hat = v / (1.0 - ADAM_B2 ** ADAM_STEP)
    delta = -ADAM_LR * (m_hat / (jnp.sqrt(v_hat) + ADAM_EPS) + ADAM_WD * w)
    return delta, m, v


def _sum_and_adamw(parts, w, m, v, tr, tag):
    rows = w.shape[0]
    assert rows % tr == 0

    def body(p_ref, w_ref, m_ref, v_ref, g_out, d_out, m_out, v_out):
        g = p_ref[0].astype(F32)
        for d in range(1, N_DEV):
            g = g + p_ref[d].astype(F32)
        delta, mn, vn = _adamw(w_ref[...], g, m_ref[...], v_ref[...])
        g_out[...] = g
        d_out[...] = delta
        m_out[...] = mn
        v_out[...] = vn

    sp = pl.BlockSpec((tr, D_MODEL), lambda i: (i, 0))
    return pl.pallas_call(
        body, name=f"sum_and_adamw_{tag}",
        grid=(rows // tr,),
        in_specs=[pl.BlockSpec((N_DEV, tr, D_MODEL), lambda i: (0, i, 0)), sp, sp, sp],
        out_specs=[sp] * 4,
        out_shape=[jax.ShapeDtypeStruct(w.shape, F32)] * 4,
        compiler_params=_cparams("parallel"),
    )(parts, w, m, v)


def _small_allreduce_adamw(part, w, m, v):
    def body(p_ref, w_ref, m_ref, v_ref, g_out, d_out, m_out, v_out, buf, send_sems, recv_sems):
        x, y, c = _my_place()
        me = 4 * x + 2 * y + c
        buf[me] = p_ref[...]
        copies = []
        for k, (fx, fy, fc) in enumerate(_RELATIONS):
            px, py, pc = _flip(x, fx), _flip(y, fy), _flip(c, fc)
            peer = 4 * px + 2 * py + pc
            copies.append((
                pltpu.make_async_remote_copy(
                    src_ref=buf.at[me], dst_ref=buf.at[me], send_sem=send_sems.at[k], recv_sem=recv_sems.at[k],
                    device_id=(px, py, pc), device_id_type=pl.DeviceIdType.MESH),
                pltpu.make_async_remote_copy(
                    src_ref=buf.at[me], dst_ref=buf.at[peer], send_sem=send_sems.at[k], recv_sem=recv_sems.at[k],
                    device_id=(px, py, pc), device_id_type=pl.DeviceIdType.MESH)))
        for out_cp, _ in copies:
            out_cp.start()
        for _, in_cp in copies:
            in_cp.wait_recv()
        for out_cp, _ in copies:
            out_cp.wait_send()
        g = buf[0]
        for d in range(1, N_DEV):
            g = g + buf[d]
        delta, mn, vn = _adamw(w_ref[...], g, m_ref[...], v_ref[...])
        g_out[...] = g
        d_out[...] = delta
        m_out[...] = mn
        v_out[...] = vn

    vm = pl.BlockSpec(memory_space=pltpu.VMEM)
    return pl.pallas_call(
        body, name="small_allreduce_adamw",
        in_specs=[vm] * 4, out_specs=[vm] * 4,
        out_shape=[jax.ShapeDtypeStruct(w.shape, F32)] * 4,
        scratch_shapes=[pltpu.VMEM((N_DEV,) + part.shape, F32),
                        pltpu.SemaphoreType.DMA((7,)), pltpu.SemaphoreType.DMA((7,))],
    )(part, w, m, v)


_TRANSPOSED = ("ffn1_w1", "ffn1_w3", "w_in", "ffn2_w1", "ffn2_w3")
_BRANCH = ("w_branch_swa", "w_branch_sb")


def _pack_shards(t, names):
    parts = []
    for name in names:
        a = t[name][0]
        if name in _TRANSPOSED:
            a = a.T
        elif name in _BRANCH:
            a = a.reshape(64, D_MODEL)
        parts.append(a)
    return jnp.concatenate(parts, axis=0)


def _unpack_shards(p, names):
    out, lo = {}, 0
    for name in names:
        a = p[lo:lo + BIG_ROWS[BIG_NAMES.index(name)]]
        lo += a.shape[0]
        if name in _TRANSPOSED:
            a = a.T
        elif name in _BRANCH:
            a = a.reshape(512, 128)
        out[name] = a[None]
    return out


def _full_weights(zones, names):
    out = {}
    for name, a in zip(names, zones):
        if name in _BRANCH:
            a = a.reshape(N_DEV, 512, 128).transpose(1, 0, 2).reshape(512, D_MODEL)
        out[_GRAD_KEY[name]] = a.reshape(-1, D_MODEL)
    return out


_GRAD_KEY = {"ffn1_w1": "ffn1_w1t", "ffn1_w3": "ffn1_w3t", "ffn1_w2": "ffn1_w2", "w_in": "w_int",
             "w_branch_swa": "w_swa", "w_branch_sb": "w_sb", "w_out": "w_out",
             "ffn2_w1": "ffn2_w1t", "ffn2_w3": "ffn2_w3t", "ffn2_w2": "ffn2_w2"}


def _pack_full_grads(big, names):
    parts = []
    for name in names:
        a = big[_GRAD_KEY[name]]
        if name in _BRANCH:
            a = a.reshape(512, N_DEV, 128).transpose(1, 0, 2)
        parts.append(a.reshape(N_DEV, BIG_ROWS[BIG_NAMES.index(name)], D_MODEL).astype(BF16))
    return jnp.concatenate(parts, axis=1)


_SMALL_NAMES = ("norm_ffn1", "norm_mix", "norm_ffn2", "norm_final", "swa_sinks", "rel_bias")


def _pack_small(vals):
    rows = []
    for a in vals:
        a = a.reshape(-1)
        rows.append(jnp.pad(a, (0, D_MODEL - a.shape[0])))
    rows += [jnp.zeros((D_MODEL,), F32)] * (SMALL_ROWS - len(rows))
    return jnp.stack(rows)


def _unpack_small(p):
    return {"norm_ffn1": p[0:1], "norm_mix": p[1:2], "norm_ffn2": p[2:3], "norm_final": p[3],
            "swa_sinks": p[4:5, :N_HEADS], "rel_bias": p[5, :REL_BUCKETS * N_HEADS].reshape(REL_BUCKETS, N_HEADS)}


ALL_NAMES = ("norm_ffn1", "ffn1_w1", "ffn1_w3", "ffn1_w2", "norm_mix", "w_in", "swa_sinks", "rel_bias",
             "w_branch_swa", "w_branch_sb", "w_out", "norm_ffn2", "ffn2_w1", "ffn2_w3", "ffn2_w2", "norm_final")


def kernel(x, norm_ffn1, ffn1_w1, ffn1_w3, ffn1_w2, norm_mix, w_in, swa_sinks, rel_bias, w_branch_swa, w_branch_sb, w_out, norm_ffn2, ffn2_w1, ffn2_w3, ffn2_w2, norm_final, loss_target, m_norm_ffn1, m_ffn1_w1, m_ffn1_w3, m_ffn1_w2, m_norm_mix, m_w_in, m_swa_sinks, m_rel_bias, m_w_branch_swa, m_w_branch_sb, m_w_out, m_norm_ffn2, m_ffn2_w1, m_ffn2_w3, m_ffn2_w2, m_norm_final, v_norm_ffn1, v_ffn1_w1, v_ffn1_w3, v_ffn1_w2, v_norm_mix, v_w_in, v_swa_sinks, v_rel_bias, v_w_branch_swa, v_w_branch_sb, v_w_out, v_norm_ffn2, v_ffn2_w1, v_ffn2_w3, v_ffn2_w2, v_norm_final):
    w = dict(zip(ALL_NAMES, (norm_ffn1, ffn1_w1, ffn1_w3, ffn1_w2, norm_mix, w_in, swa_sinks, rel_bias,
                             w_branch_swa, w_branch_sb, w_out, norm_ffn2, ffn2_w1, ffn2_w3, ffn2_w2, norm_final)))
    m = dict(zip(ALL_NAMES, (m_norm_ffn1, m_ffn1_w1, m_ffn1_w3, m_ffn1_w2, m_norm_mix, m_w_in, m_swa_sinks, m_rel_bias,
                             m_w_branch_swa, m_w_branch_sb, m_w_out, m_norm_ffn2, m_ffn2_w1, m_ffn2_w3, m_ffn2_w2,
                             m_norm_final)))
    v = dict(zip(ALL_NAMES, (v_norm_ffn1, v_ffn1_w1, v_ffn1_w3, v_ffn1_w2, v_norm_mix, v_w_in, v_swa_sinks, v_rel_bias,
                             v_w_branch_swa, v_w_branch_sb, v_w_out, v_norm_ffn2, v_ffn2_w1, v_ffn2_w3, v_ffn2_w2,
                             v_norm_final)))

    def my_blocks(group):
        return [_pack_shards(w, (name,)).astype(BF16) for name in GROUPS[group]]

    gathered0 = _gather_weights(my_blocks(0), "group0")

    def weights_of(group, landed):
        return _full_weights(gathered0 if group == 0 else landed, GROUPS[group])

    def ship(kind, which, grads=None):
        if kind == "weights":
            return [(block, False) for block in my_blocks(which)]
        return [(_pack_full_grads(grads, which), True)]

    gains = (norm_ffn1, norm_mix, norm_ffn2, norm_final.reshape(1, D_MODEL))
    loss, dx, small, parts = _local_step(x[0], loss_target[0], gains, swa_sinks, rel_bias, weights_of, ship)

    big_outs = [{}, {}, {}, {}]
    for names, tile in zip(SUM_GROUPS, SUM_TILE):
        landed = parts[names]
        if isinstance(landed, dict):
            landed = _exchange_grads(_pack_full_grads(landed, names), names[0])
        res = _sum_and_adamw(landed, _pack_shards(w, names), _pack_shards(m, names), _pack_shards(v, names),
                             tile, names[0])
        for acc, packed in zip(big_outs, res):
            acc.update(_unpack_shards(packed, names))
    g_big, d_big, m_big, v_big = big_outs

    small_part = _pack_small(small["gains"] + (small["sinks"], small["rel_bias"], loss))
    zero = jnp.zeros((1,), F32)
    small_res = _small_allreduce_adamw(
        small_part, _pack_small([w[n] for n in _SMALL_NAMES] + [zero]), _pack_small([m[n] for n in _SMALL_NAMES] + [zero]),
        _pack_small([v[n] for n in _SMALL_NAMES] + [zero]))
    g_sm, d_sm, m_sm, v_sm = (_unpack_small(p) for p in small_res)

    outs = [small_res[0][len(_SMALL_NAMES), 0], dx[None]]
    for big_d, small_d in ((g_big, g_sm), (d_big, d_sm), (m_big, m_sm), (v_big, v_sm)):
        merged = {**big_d, **small_d}
        outs += [merged[n] for n in ALL_NAMES]
    return tuple(outs)
```

```python
import functools

import jax
import jax.numpy as jnp
import numpy as np
from jax import lax
from jax.experimental import pallas as pl
from jax.experimental.pallas import tpu as pltpu

F32 = jnp.float32
BF16 = jnp.bfloat16

D_MODEL = 1024
D_FF = 2816
HEAD_DIM = 64
N_HEADS = 8
SWA_KV_HEADS = 2
SWA_GROUP = 4
SWA_BLOCK = 128
REL_BUCKETS = 32
REL_MAX_DIST = 128
RMS_EPS = 1e-6
NEG_BIG = -1e30
Q_SCALE = HEAD_DIM ** -0.5
LANES = 128

N_DEV = 8

ADAM_LR = 0.001
ADAM_B1 = 0.9
ADAM_B2 = 0.999
ADAM_EPS = 1e-08
ADAM_WD = 0.01
ADAM_STEP = 10

IN_SIZES = (512, 128, 128, 512, 512, 512, 1024, 1024)
IN_OFFS = tuple(int(v) for v in np.cumsum((0,) + IN_SIZES))
IN_W = IN_OFFS[-1]

BIG_NAMES = ("ffn1_w1", "ffn1_w3", "ffn1_w2", "w_in", "w_branch_swa", "w_branch_sb", "w_out",
             "ffn2_w1", "ffn2_w3", "ffn2_w2")
BIG_ROWS = (352, 352, 352, 544, 64, 64, 128, 352, 352, 352)
SMALL_ROWS = 8
GROUPS = (BIG_NAMES[0:3], BIG_NAMES[3:7], BIG_NAMES[7:10])
SUM_GROUPS = tuple((n,) for n in GROUPS[0]) + GROUPS[1:]
SUM_TILE = (176, 176, 176, 160, 96)

VMEM_LIMIT = 56 * 1024 * 1024
FFN_PIECES = 2
SB_QUERIES = 512
SB_KEYS = 256
SB_ROWS = 256
SB_SUM_PARTS = 1
SB_LOGIT_CAP = 80.0
SB_DEAD_CARRY = -110.0


def _dot(a, b):
    return jnp.dot(a, b, preferred_element_type=F32)


def _dot_nt(a, b):
    return lax.dot_general(a, b, (((1,), (1,)), ((), ())), preferred_element_type=F32)


def _dot_tn(a, b):
    return lax.dot_general(a, b, (((0,), (0,)), ((), ())), preferred_element_type=F32)


def _cparams(*sem):
    return pltpu.CompilerParams(dimension_semantics=sem, vmem_limit_bytes=VMEM_LIMIT)


def _rms_rstd(xv):
    return lax.rsqrt(jnp.mean(xv * xv, axis=-1, keepdims=True) + RMS_EPS)


def _rms_bwd(dh, xv, r, g):
    xhat = xv * r
    dg = jnp.sum(dh * xhat, axis=0, keepdims=True)
    dxn = dh * g
    dx = r * (dxn - xhat * jnp.mean(dxn * xhat, axis=-1, keepdims=True))
    return dx, dg


def _ffn_fwd(x, g, w1t, w3t, w2, tag, comm=None):
    s_len = x.shape[0]
    tm, tf = min(1024, s_len), 256
    nf = D_FF // tf

    def body(x_ref, g_ref, w1_ref, w3_ref, w2_ref, xo_ref, h_ref, a_ref, b_ref, u_ref, acc_ref, hs_ref):
        j = pl.program_id(1)

        @pl.when(j == 0)
        def _():
            xv = x_ref[...]
            h = (xv * _rms_rstd(xv) * g_ref[...]).astype(BF16)
            hs_ref[...] = h
            h_ref[...] = h
            acc_ref[...] = jnp.zeros_like(acc_ref)

        st = {}

        def s_up(rs):
            h = hs_ref[rs, :]
            st[rs.start, "ab"] = (_dot_nt(h, w1_ref[...]), _dot_nt(h, w3_ref[...]))

        def s_act(rs):
            a, b = st.pop((rs.start, "ab"))
            a_ref[rs, :] = a.astype(BF16)
            b_ref[rs, :] = b.astype(BF16)
            uh = (0.5 * (a * jax.nn.sigmoid(a) * b)).astype(BF16)
            u_ref[rs, :] = uh
            st[rs.start, "u"] = uh

        def s_down(rs):
            acc_ref[rs, :] += _dot(st.pop((rs.start, "u")), w2_ref[...])

        _emit_skewed(([slice(r, r + tm // FFN_PIECES) for r in range(0, tm, tm // FFN_PIECES)], [s_up, s_act, s_down]))

        @pl.when(j == nf - 1)
        def _():
            xo_ref[...] = x_ref[...] + acc_ref[...]

    row = lambda i, j: (i, 0)
    return _call(
        body, (x, g, w1t, w3t, w2), comm=comm, **_grid_ends(s_len // tm, nf), name=f"ffn_fwd_{tag}",
        grid=(s_len // tm, nf),
        in_specs=[pl.BlockSpec((tm, D_MODEL), row), pl.BlockSpec((1, D_MODEL), lambda i, j: (0, 0)),
                  pl.BlockSpec((tf, D_MODEL), lambda i, j: (j, 0)), pl.BlockSpec((tf, D_MODEL), lambda i, j: (j, 0)),
                  pl.BlockSpec((tf, D_MODEL), lambda i, j: (j, 0))],
        out_specs=[pl.BlockSpec((tm, D_MODEL), row), pl.BlockSpec((tm, D_MODEL), row),
                   pl.BlockSpec((tm, tf), lambda i, j: (i, j)), pl.BlockSpec((tm, tf), lambda i, j: (i, j)),
                   pl.BlockSpec((tm, tf), lambda i, j: (i, j))],
        out_shape=[jax.ShapeDtypeStruct((s_len, D_MODEL), F32), jax.ShapeDtypeStruct((s_len, D_MODEL), BF16),
                   jax.ShapeDtypeStruct((s_len, D_FF), BF16), jax.ShapeDtypeStruct((s_len, D_FF), BF16),
                   jax.ShapeDtypeStruct((s_len, D_FF), BF16)],
        scratch_shapes=[pltpu.VMEM((tm, D_MODEL), F32), pltpu.VMEM((tm, D_MODEL), BF16)],
        compiler_params=_cparams("arbitrary", "arbitrary"),
    )


def _ffn_bwd(dy, x, g, a, b, w1t, w3t, w2, tag, comm=None):
    s_len = x.shape[0]
    tm, tf = min(1024, s_len), 256
    nf = D_FF // tf

    def body(dy_ref, x_ref, g_ref, a_ref, b_ref, w1_ref, w3_ref, w2_ref,
             dx_ref, dg_ref, da_ref, db_ref, dyb_ref, acc_ref, dys_ref):
        i, j = pl.program_id(0), pl.program_id(1)

        @pl.when(j == 0)
        def _():
            dyb = dy_ref[...].astype(BF16)
            dys_ref[...] = dyb
            dyb_ref[...] = dyb
            acc_ref[...] = jnp.zeros_like(acc_ref)

        @pl.when((i == 0) & (j == 0))
        def _():
            dg_ref[...] = jnp.zeros_like(dg_ref)

        st = {}

        def s_du(rs):
            st[rs.start, "du"] = 0.5 * _dot_nt(dys_ref[rs, :], w2_ref[...])

        def s_act(rs):
            du = st.pop((rs.start, "du"))
            av = a_ref[rs, :].astype(F32)
            bv = b_ref[rs, :].astype(F32)
            sg = jax.nn.sigmoid(av)
            sil = av * sg
            da = (du * bv * (sg + sil * (1.0 - sg))).astype(BF16)
            db = (du * sil).astype(BF16)
            da_ref[rs, :] = da
            db_ref[rs, :] = db
            st[rs.start, "dab"] = (da, db)

        def s_dh(rs):
            da, db = st.pop((rs.start, "dab"))
            acc_ref[rs, :] += _dot(da, w1_ref[...]) + _dot(db, w3_ref[...])

        _emit_skewed(([slice(r, r + tm // FFN_PIECES) for r in range(0, tm, tm // FFN_PIECES)], [s_du, s_act, s_dh]))

        @pl.when(j == nf - 1)
        def _():
            xv = x_ref[...]
            dx, dg = _rms_bwd(acc_ref[...], xv, _rms_rstd(xv), g_ref[...])
            dx_ref[...] = dy_ref[...] + dx
            dg_ref[...] += dg

    row = lambda i, j: (i, 0)
    blk = lambda i, j: (i, j)
    wsp = pl.BlockSpec((tf, D_MODEL), lambda i, j: (j, 0))
    return _call(
        body, (dy, x, g, a, b, w1t, w3t, w2), comm=comm, **_grid_ends(s_len // tm, nf), name=f"ffn_bwd_{tag}",
        grid=(s_len // tm, nf),
        in_specs=[pl.BlockSpec((tm, D_MODEL), row), pl.BlockSpec((tm, D_MODEL), row),
                  pl.BlockSpec((1, D_MODEL), lambda i, j: (0, 0)),
                  pl.BlockSpec((tm, tf), blk), pl.BlockSpec((tm, tf), blk), wsp, wsp, wsp],
        out_specs=[pl.BlockSpec((tm, D_MODEL), row), pl.BlockSpec((1, D_MODEL), lambda i, j: (0, 0)),
                   pl.BlockSpec((tm, tf), blk), pl.BlockSpec((tm, tf), blk), pl.BlockSpec((tm, D_MODEL), row)],
        out_shape=[jax.ShapeDtypeStruct((s_len, D_MODEL), F32), jax.ShapeDtypeStruct((1, D_MODEL), F32),
                   jax.ShapeDtypeStruct((s_len, D_FF), BF16), jax.ShapeDtypeStruct((s_len, D_FF), BF16),
                   jax.ShapeDtypeStruct((s_len, D_MODEL), BF16)],
        scratch_shapes=[pltpu.VMEM((tm, D_MODEL), F32), pltpu.VMEM((tm, D_MODEL), BF16)],
        compiler_params=_cparams("arbitrary", "arbitrary"),
    )


def _matmul_tn(lhs, rhs, tag, comm=None):
    s_len, m = lhs.shape
    n = rhs.shape[1]
    tm = min(512, s_len)
    tj = m if m <= 1024 else 1408
    assert m % tj == 0
    last_rows = s_len // tm - 1

    def body(l_ref, r_ref, o_ref, acc_ref):
        i = pl.program_id(1)

        @pl.when(i == 0)
        def _():
            acc_ref[...] = jnp.zeros_like(acc_ref)

        acc_ref[...] += _dot_tn(l_ref[...], r_ref[...])

        @pl.when(i == last_rows)
        def _():
            o_ref[...] = acc_ref[...].astype(BF16)

    res = _call(
        body, (lhs, rhs), comm=comm, **_grid_ends(m // tj, s_len // tm), name=f"matmul_tn_{tag}",
        grid=(m // tj, s_len // tm),
        in_specs=[pl.BlockSpec((tm, tj), lambda j, i: (i, j)), pl.BlockSpec((tm, n), lambda j, i: (i, 0))],
        out_specs=[pl.BlockSpec((tj, n), lambda j, i: (j, 0))],
        out_shape=[jax.ShapeDtypeStruct((m, n), BF16)],
        scratch_shapes=[pltpu.VMEM((tj, n), F32)],
        compiler_params=_cparams("arbitrary", "arbitrary"),
    )
    return res[0] if comm is None else tuple(res)


def _matmul_tn_stacked(pieces, rhs, tag):
    s_len, n = rhs.shape
    widths = [p.shape[1] for p in pieces]
    offs = [sum(widths[:k]) for k in range(len(widths) + 1)]
    tm = min(256, s_len)
    last_rows = s_len // tm - 1

    def body(*refs):
        l_refs, r_ref, o_ref, acc_ref = refs[:len(pieces)], refs[-3], refs[-2], refs[-1]
        i = pl.program_id(0)

        @pl.when(i == 0)
        def _():
            acc_ref[...] = jnp.zeros_like(acc_ref)

        rv = r_ref[...]
        for k, l_ref in enumerate(l_refs):
            acc_ref[offs[k]:offs[k + 1], :] += _dot_tn(l_ref[...], rv)

        @pl.when(i == last_rows)
        def _():
            o_ref[...] = acc_ref[...].astype(BF16)

    row = lambda i: (i, 0)
    return pl.pallas_call(
        body, name=f"matmul_tn_{tag}",
        grid=(s_len // tm,),
        in_specs=[pl.BlockSpec((tm, w), row) for w in widths] + [pl.BlockSpec((tm, n), row)],
        out_specs=pl.BlockSpec((offs[-1], n), lambda i: (0, 0)),
        out_shape=jax.ShapeDtypeStruct((offs[-1], n), BF16),
        scratch_shapes=[pltpu.VMEM((offs[-1], n), F32)],
        compiler_params=_cparams("arbitrary"),
    )(*pieces, rhs)


def _proj_fwd(x1, g, wint):
    s_len = x1.shape[0]
    tm = min(512, s_len)
    dts = (BF16, BF16, BF16, BF16, BF16, BF16, F32, F32)

    def body(x_ref, g_ref, w_ref, h_ref, *outs):
        xv = x_ref[...]
        h = (xv * _rms_rstd(xv) * g_ref[...]).astype(BF16)
        h_ref[...] = h
        for p, o_ref in enumerate(outs):
            val = _dot_nt(h, w_ref[IN_OFFS[p]:IN_OFFS[p + 1], :])
            if p == 3:
                val = val * Q_SCALE
            o_ref[...] = val.astype(dts[p])

    row = lambda i: (i, 0)
    return pl.pallas_call(
        body, name="proj_fwd",
        grid=(s_len // tm,),
        in_specs=[pl.BlockSpec((tm, D_MODEL), row), pl.BlockSpec((1, D_MODEL), lambda i: (0, 0)),
                  pl.BlockSpec((IN_W, D_MODEL), lambda i: (0, 0))],
        out_specs=[pl.BlockSpec((tm, D_MODEL), row)] + [pl.BlockSpec((tm, w), row) for w in IN_SIZES],
        out_shape=[jax.ShapeDtypeStruct((s_len, D_MODEL), BF16)]
        + [jax.ShapeDtypeStruct((s_len, w), dt) for w, dt in zip(IN_SIZES, dts)],
        compiler_params=_cparams("parallel"),
    )(x1, g, wint)


def _proj_bwd(dpieces, dx2, x1, g, wint):
    s_len = x1.shape[0]
    tm = min(512, s_len)

    def body(*refs):
        dps = refs[:8]
        dx2_ref, x_ref, g_ref, w_ref, dx_ref, dg_ref = refs[8:]

        @pl.when(pl.program_id(0) == 0)
        def _():
            dg_ref[...] = jnp.zeros_like(dg_ref)

        dh = _dot(dps[0][...], w_ref[IN_OFFS[0]:IN_OFFS[1], :])
        for p in range(1, 8):
            dh += _dot(dps[p][...], w_ref[IN_OFFS[p]:IN_OFFS[p + 1], :])
        xv = x_ref[...]
        dx, dg = _rms_bwd(dh, xv, _rms_rstd(xv), g_ref[...])
        dx_ref[...] = dx2_ref[...] + dx
        dg_ref[...] += dg

    row = lambda i: (i, 0)
    return pl.pallas_call(
        body, name="proj_bwd",
        grid=(s_len // tm,),
        in_specs=[pl.BlockSpec((tm, w), row) for w in IN_SIZES]
        + [pl.BlockSpec((tm, D_MODEL), row), pl.BlockSpec((tm, D_MODEL), row),
           pl.BlockSpec((1, D_MODEL), lambda i: (0, 0)), pl.BlockSpec((IN_W, D_MODEL), lambda i: (0, 0))],
        out_specs=[pl.BlockSpec((tm, D_MODEL), row), pl.BlockSpec((1, D_MODEL), lambda i: (0, 0))],
        out_shape=[jax.ShapeDtypeStruct((s_len, D_MODEL), F32), jax.ShapeDtypeStruct((1, D_MODEL), F32)],
        compiler_params=_cparams("arbitrary"),
    )(*dpieces, dx2, x1, g, wint)


def _merge_fwd(x1, oa, ob, ga, gb, wswa, wsb, wout):
    s_len = x1.shape[0]
    tm = min(512, s_len)

    def body(x_ref, oa_ref, ob_ref, ga_ref, gb_ref, wa_ref, wb_ref, wo_ref, xo_ref, mg_ref):
        pa = _dot(oa_ref[...], wa_ref[...])
        pb = _dot(ob_ref[...], wb_ref[...])
        mg = (jax.nn.sigmoid(ga_ref[...]) * pa + jax.nn.sigmoid(gb_ref[...]) * pb).astype(BF16)
        mg_ref[...] = mg
        xo_ref[...] = x_ref[...] + _dot(mg, wo_ref[...])

    row = lambda i: (i, 0)
    full = lambda i: (0, 0)
    return pl.pallas_call(
        body, name="merge_fwd",
        grid=(s_len // tm,),
        in_specs=[pl.BlockSpec((tm, D_MODEL), row), pl.BlockSpec((tm, 512), row), pl.BlockSpec((tm, 512), row),
                  pl.BlockSpec((tm, D_MODEL), row), pl.BlockSpec((tm, D_MODEL), row),
                  pl.BlockSpec((512, D_MODEL), full), pl.BlockSpec((512, D_MODEL), full),
                  pl.BlockSpec((D_MODEL, D_MODEL), full)],
        out_specs=[pl.BlockSpec((tm, D_MODEL), row), pl.BlockSpec((tm, D_MODEL), row)],
        out_shape=[jax.ShapeDtypeStruct((s_len, D_MODEL), F32), jax.ShapeDtypeStruct((s_len, D_MODEL), BF16)],
        compiler_params=_cparams("parallel"),
    )(x1, oa, ob, ga, gb, wswa, wsb, wout)


def _merge_bwd(dx2, oa, ob, ga, gb, wswa, wsb, wout):
    s_len = dx2.shape[0]
    tm = min(512, s_len)

    def body(dx_ref, oa_ref, ob_ref, ga_ref, gb_ref, wa_ref, wb_ref, wo_ref,
             doa_ref, dob_ref, dga_ref, dgb_ref, dpa_ref, dpb_ref, dxb_ref):
        dxb = dx_ref[...].astype(BF16)
        dxb_ref[...] = dxb
        dmg = _dot_nt(dxb, wo_ref[...])
        for o_ref, g_ref, w_ref, do_ref, dg_ref, dp_ref in (
                (oa_ref, ga_ref, wa_ref, doa_ref, dga_ref, dpa_ref),
                (ob_ref, gb_ref, wb_ref, dob_ref, dgb_ref, dpb_ref)):
            pv = _dot(o_ref[...], w_ref[...])
            sg = jax.nn.sigmoid(g_ref[...])
            dp = (dmg * sg).astype(BF16)
            dp_ref[...] = dp
            dg_ref[...] = (dmg * pv * sg * (1.0 - sg)).astype(BF16)
            do_ref[...] = _dot_nt(dp, w_ref[...]).astype(BF16)

    row = lambda i: (i, 0)
    full = lambda i: (0, 0)
    wide = pl.BlockSpec((tm, D_MODEL), row)
    half = pl.BlockSpec((tm, 512), row)
    return pl.pallas_call(
        body, name="merge_bwd",
        grid=(s_len // tm,),
        in_specs=[wide, half, half, wide, wide, pl.BlockSpec((512, D_MODEL), full),
                  pl.BlockSpec((512, D_MODEL), full), pl.BlockSpec((D_MODEL, D_MODEL), full)],
        out_specs=[half, half, wide, wide, wide, wide, wide],
        out_shape=[jax.ShapeDtypeStruct((s_len, 512), BF16)] * 2 + [jax.ShapeDtypeStruct((s_len, D_MODEL), BF16)] * 5,
        compiler_params=_cparams("parallel"),
    )(dx2, oa, ob, ga, gb, wswa, wsb, wout)


def _loss_fwd_bwd(x3, tgt, g):
    s_len = x3.shape[0]
    tm = min(1024, s_len)

    def body(x_ref, t_ref, g_ref, dx_ref, loss_ref, dg_ref):
        @pl.when(pl.program_id(0) == 0)
        def _():
            loss_ref[...] = jnp.zeros_like(loss_ref)
            dg_ref[...] = jnp.zeros_like(dg_ref)

        xv = x_ref[...]
        gv = g_ref[...]
        r = _rms_rstd(xv)
        err = xv * r * gv - t_ref[...]
        loss_ref[...] += 0.5 * jnp.sum(jnp.mean(err * err, axis=-1, keepdims=True), axis=0, keepdims=True)
        dx, dg = _rms_bwd(err * (1.0 / D_MODEL), xv, r, gv)
        dx_ref[...] = dx
        dg_ref[...] += dg

    row = lambda i: (i, 0)
    return pl.pallas_call(
        body, name="loss_fwd_bwd",
        grid=(s_len // tm,),
        in_specs=[pl.BlockSpec((tm, D_MODEL), row), pl.BlockSpec((tm, D_MODEL), row),
                  pl.BlockSpec((1, D_MODEL), lambda i: (0, 0))],
        out_specs=[pl.BlockSpec((tm, D_MODEL), row), pl.BlockSpec((1, 1), lambda i: (0, 0)),
                   pl.BlockSpec((1, D_MODEL), lambda i: (0, 0))],
        out_shape=[jax.ShapeDtypeStruct((s_len, D_MODEL), F32), jax.ShapeDtypeStruct((1, 1), F32),
                   jax.ShapeDtypeStruct((1, D_MODEL), F32)],
        compiler_params=_cparams("arbitrary"),
    )(x3, tgt, g)


def _rel_bucket_matrix():
    qi = jnp.arange(SWA_BLOCK)[:, None] + SWA_BLOCK
    kj = jnp.arange(2 * SWA_BLOCK)[None, :]
    dist = jnp.maximum(qi - kj, 0)
    max_exact = REL_BUCKETS // 2
    d = jnp.maximum(dist, 1).astype(F32)
    large = max_exact + (jnp.log(d / max_exact) / np.log(REL_MAX_DIST / max_exact)
                         * (REL_BUCKETS - max_exact)).astype(jnp.int32)
    large = jnp.minimum(large, REL_BUCKETS - 1)
    return jnp.where(dist < max_exact, dist, large).astype(jnp.int32)


def _swa_bias_into(bias_ref, bkt_ref, tab_ref):
    bk = bkt_ref[...]
    for h in range(N_HEADS):
        acc = jnp.zeros(bk.shape, F32)
        for bucket in range(REL_BUCKETS):
            acc = jnp.where(bk == bucket, tab_ref[bucket, h], acc)
        bias_ref[h] = acc


def _swa_valid(n):
    shape = (SWA_BLOCK, 2 * SWA_BLOCK)
    row = lax.broadcasted_iota(jnp.int32, shape, 0)
    col = lax.broadcasted_iota(jnp.int32, shape, 1)
    dist = row + SWA_BLOCK - col
    return (dist >= 0) & (dist < SWA_BLOCK) & ((col >= SWA_BLOCK) | (n > 0))


def _swa_windows(kp_ref, kc_ref, vp_ref, vc_ref):
    return (jnp.concatenate([kp_ref[...], kc_ref[...]], axis=0), jnp.concatenate([vp_ref[...], vc_ref[...]], axis=0))


def _swa_place(h):
    return slice(h // 2 * LANES, (h // 2 + 1) * LANES), h % 2, h // SWA_GROUP


def _move_half(x, src, dst):
    moved = x if src == dst else pltpu.roll(x, HEAD_DIM, 1)
    in_dst = (lax.broadcasted_iota(jnp.int32, x.shape, 1) >= HEAD_DIM) == bool(dst)
    return jnp.where(in_dst, moved, 0.0)


def _swa_probs(qk, bias, sink, valid):
    lg = jnp.where(valid, qk * Q_SCALE + bias, NEG_BIG)
    m = jnp.maximum(jnp.max(lg, axis=-1, keepdims=True), sink)
    e = jnp.exp(lg - m)
    es = jnp.exp(sink - m)
    inv = 1.0 / (jnp.sum(e, axis=-1, keepdims=True) + es)
    return e * inv, es * inv


def _swa_specs(s_len):
    blk = SWA_BLOCK
    cur = lambda n: (n, 0)
    prev = lambda n: (jnp.maximum(n - 1, 0), 0)
    kvw = SWA_KV_HEADS * HEAD_DIM
    return [pl.BlockSpec(memory_space=pltpu.SMEM), pl.BlockSpec(memory_space=pltpu.SMEM),
            pl.BlockSpec((blk, 2 * blk), lambda n: (0, 0)),
            pl.BlockSpec((blk, N_HEADS * HEAD_DIM), cur),
            pl.BlockSpec((blk, kvw), prev), pl.BlockSpec((blk, kvw), cur),
            pl.BlockSpec((blk, kvw), prev), pl.BlockSpec((blk, kvw), cur)]


def _swa_fwd(tab, sinks, bkt, q, k, v):
    s_len = q.shape[0]
    blk = SWA_BLOCK

    def body(tab_ref, sink_ref, bkt_ref, q_ref, kp_ref, kc_ref, vp_ref, vc_ref, o_ref, bias_ref):
        n = pl.program_id(0)

        @pl.when(n == 0)
        def _():
            _swa_bias_into(bias_ref, bkt_ref, tab_ref)

        valid = _swa_valid(n)
        kk, vv = _swa_windows(kp_ref, kc_ref, vp_ref, vc_ref)
        st = {}

        def s_logits(h):
            tile, mine, kv = _swa_place(h)
            st[h, "lg"] = _dot_nt(_move_half(q_ref[:, tile].astype(F32), mine, kv).astype(BF16), kk)

        def s_probs(h):
            st[h, "p"] = _swa_probs(st.pop((h, "lg")), bias_ref[h], sink_ref[0, h], valid)[0].astype(BF16)

        def s_values(h):
            tile, mine, kv = _swa_place(h)
            part = _move_half(_dot(st.pop((h, "p")), vv), kv, mine)
            if mine == 0:
                st[h + 1, "o"] = part
            else:
                o_ref[:, tile] = (st.pop((h, "o")) + part).astype(BF16)

        _emit_skewed((list(range(N_HEADS)), [s_logits, s_probs, s_values]))

    return pl.pallas_call(
        body, name="swa_fwd",
        grid=(s_len // blk,),
        in_specs=_swa_specs(s_len),
        out_specs=pl.BlockSpec((blk, N_HEADS * HEAD_DIM), lambda n: (n, 0)),
        out_shape=jax.ShapeDtypeStruct((s_len, N_HEADS * HEAD_DIM), BF16),
        scratch_shapes=[pltpu.VMEM((N_HEADS, blk, 2 * blk), F32)],
        compiler_params=_cparams("arbitrary"),
    )(tab, sinks, bkt, q, k, k, v, v)


def _swa_bwd(tab, sinks, bkt, q, k, v, do, comm=None):
    s_len = q.shape[0]
    blk = SWA_BLOCK
    nb = s_len // blk
    kvw = SWA_KV_HEADS * HEAD_DIM

    def body(tab_ref, sink_ref, bkt_ref, q_ref, kp_ref, kc_ref, vp_ref, vc_ref, do_ref,
             dq_ref, dk_ref, dv_ref, dtab_ref, dsink_ref, bias_ref, dbias_ref):
        n = pl.program_id(0)

        @pl.when(n == 0)
        def _():
            _swa_bias_into(bias_ref, bkt_ref, tab_ref)
            dbias_ref[...] = jnp.zeros_like(dbias_ref)
            dk_ref[...] = jnp.zeros_like(dk_ref)
            dv_ref[...] = jnp.zeros_like(dv_ref)
            dsink_ref[...] = jnp.zeros_like(dsink_ref)
            dtab_ref[...] = jnp.zeros_like(dtab_ref)

        valid = _swa_valid(n)
        cur_rows = pl.ds(pl.multiple_of(n * blk, blk), blk)
        prev_rows = pl.ds(pl.multiple_of(jnp.maximum(n - 1, 0) * blk, blk), blk)
        kk, vv = _swa_windows(kp_ref, kc_ref, vp_ref, vc_ref)
        st = {}

        def s_logits(h):
            tile, mine, kv = _swa_place(h)
            st[h, "q"] = _move_half(q_ref[:, tile].astype(F32), mine, kv).astype(BF16)
            st[h, "do"] = _move_half(do_ref[:, tile].astype(F32), mine, kv).astype(BF16)
            st[h, "lg"] = _dot_nt(st[h, "q"], kk)
            st[h, "dp"] = _dot_nt(st[h, "do"], vv)

        def s_probs(h):
            p, ps = _swa_probs(st.pop((h, "lg")), bias_ref[h], sink_ref[0, h], valid)
            dp = st.pop((h, "dp"))
            delta = jnp.sum(p * dp, axis=-1, keepdims=True)
            dl = p * (dp - delta)
            dsink_ref[h:h + 1, :] += jnp.broadcast_to(-jnp.sum(ps * delta, axis=0, keepdims=True), (1, LANES))
            dbias_ref[h] += dl
            st[h, "dl"], st[h, "p"] = dl.astype(BF16), p.astype(BF16)

        def s_products(h):
            tile, mine, kv = _swa_place(h)
            dlb = st.pop((h, "dl"))
            part = _move_half(Q_SCALE * _dot(dlb, kk), kv, mine)
            if mine == 0:
                st[h + 1, "dq"] = part
            else:
                dq_ref[:, tile] = (st.pop((h, "dq")) + part).astype(BF16)
            dk_win = Q_SCALE * _dot_tn(dlb, st.pop((h, "q")))
            dv_win = _dot_tn(st.pop((h, "p")), st.pop((h, "do")))
            dk_ref[prev_rows, :] += dk_win[:blk]
            dv_ref[prev_rows, :] += dv_win[:blk]
            dk_ref[cur_rows, :] += dk_win[blk:]
            dv_ref[cur_rows, :] += dv_win[blk:]

        _emit_skewed((list(range(N_HEADS)), [s_logits, s_probs, s_products]))

        @pl.when(n == nb - 1)
        def _():
            bk = bkt_ref[...]
            lane = lax.broadcasted_iota(jnp.int32, (1, LANES), 1)
            for bucket in range(REL_BUCKETS):
                rowv = jnp.zeros((1, LANES), F32)
                for h in range(N_HEADS):
                    val = jnp.sum(jnp.where(bk == bucket, dbias_ref[h], 0.0), axis=1, keepdims=True)
                    val = jnp.sum(val, axis=0, keepdims=True)
                    rowv = jnp.where(lane == h, val, rowv)
                dtab_ref[bucket:bucket + 1, :] = rowv

    return _call(
        body, (tab, sinks, bkt, q, k, k, v, v, do), comm=comm, **_grid_ends(nb), name="swa_bwd",
        grid=(nb,),
        in_specs=_swa_specs(s_len) + [pl.BlockSpec((blk, N_HEADS * HEAD_DIM), lambda n: (n, 0))],
        out_specs=[pl.BlockSpec((blk, N_HEADS * HEAD_DIM), lambda n: (n, 0)),
                   pl.BlockSpec((s_len, kvw), lambda n: (0, 0)), pl.BlockSpec((s_len, kvw), lambda n: (0, 0)),
                   pl.BlockSpec((REL_BUCKETS, LANES), lambda n: (0, 0)), pl.BlockSpec((N_HEADS, LANES), lambda n: (0, 0))],
        out_shape=[jax.ShapeDtypeStruct((s_len, N_HEADS * HEAD_DIM), BF16),
                   jax.ShapeDtypeStruct((s_len, kvw), F32), jax.ShapeDtypeStruct((s_len, kvw), F32),
                   jax.ShapeDtypeStruct((REL_BUCKETS, LANES), F32), jax.ShapeDtypeStruct((N_HEADS, LANES), F32)],
        scratch_shapes=[pltpu.VMEM((N_HEADS, blk, 2 * blk), F32), pltpu.VMEM((N_HEADS, blk, 2 * blk), F32)],
        compiler_params=_cparams("arbitrary"),
    )


def _sb_terms(z, valid):
    zc = jnp.minimum(z, SB_LOGIT_CAP)
    lk = -jnp.log(1.0 + jnp.exp(zc))
    lsz = zc + lk
    return lsz, (lk if valid is None else jnp.where(valid, lk, 0.0))


def _bf16_parts(vals):
    parts, rest = [], vals
    for n in range(SB_SUM_PARTS):
        parts.append(rest.astype(BF16))
        if n + 1 < SB_SUM_PARTS:
            rest = rest - parts[-1].astype(F32)
    return parts[0] if len(parts) == 1 else jnp.concatenate(parts, axis=1)


def _row_sum_lanes(vals):
    return jnp.broadcast_to(jnp.sum(vals, axis=-1, keepdims=True), (vals.shape[0], LANES))


def _emit_skewed(*groups):
    for step in range(max(len(items) + len(stages) - 1 for items, stages in groups)):
        for items, stages in groups:
            for s, stage in enumerate(stages):
                if 0 <= step - s < len(items):
                    stage(items[step - s])


def _sb_items(edge):
    items = []
    for h in range(2):
        for r0 in range(0, SB_QUERIES, SB_ROWS):
            if edge is None or r0 >= (edge + 1) * SB_KEYS:
                items.append((h, r0, False))
            elif r0 + SB_ROWS - 1 > edge * SB_KEYS:
                items.append((h, r0, True))
    return items


def _sb_valid(w, edge):
    row = lax.broadcasted_iota(jnp.int32, (SB_ROWS, SB_KEYS), 0) + w[1]
    col = lax.broadcasted_iota(jnp.int32, (SB_ROWS, SB_KEYS), 1) + edge * SB_KEYS
    return col < row


def _sb_consts(tq, tk):
    low = lax.broadcasted_iota(jnp.int32, (tq, LANES), 1) < HEAD_DIM
    row = lax.broadcasted_iota(jnp.int32, (tk, tk), 0)
    col = lax.broadcasted_iota(jnp.int32, (tk, tk), 1)
    right = (row > col).astype(BF16)
    left = (row < col).astype(BF16)
    return low, jnp.concatenate([right] * SB_SUM_PARTS, axis=0), jnp.concatenate([left] * SB_SUM_PARTS, axis=0)


def _sb_fwd(q, k, v, comm=None):
    s_len = q.shape[0]
    tq, tk, tr = SB_QUERIES, SB_KEYS, SB_ROWS
    nk, ratio = s_len // tk, tq // tk
    assert nk <= LANES

    def body(q_ref, k_ref, v_ref, o_ref, car_ref, c_ref, oacc_ref, logw_ref, lksum_ref):
        i = pl.program_id(1)
        qv = q_ref[...]
        low, tri2, _ = _sb_consts(tq, tk)
        lane = lax.broadcasted_iota(jnp.int32, (tr, LANES), 1)
        zero = jnp.zeros_like(qv)
        q_heads = (jnp.where(low, qv, zero), jnp.where(low, zero, qv))
        c_ref[...] = jnp.zeros_like(c_ref)
        oacc_ref[...] = jnp.zeros_like(oacc_ref)
        car_ref[...] = jnp.full_like(car_ref, NEG_BIG)

        def front(j, edge):
            keys = k_ref[pl.ds(pl.multiple_of(j * tk, tk), tk), :]
            slot = j % 2
            st = {}

            def s_logits(w):
                st[w, "z"] = _dot_nt(q_heads[w[0]][w[1]:w[1] + tr], keys)

            def s_terms(w):
                valid = _sb_valid(w, edge) if w[2] else None
                lsz, lk = _sb_terms(st.pop((w, "z")), valid)
                st[w, "parts"] = _bf16_parts(lk)
                st[w, "lsz"] = lsz if valid is None else jnp.where(valid, lsz, NEG_BIG)
                lksum_ref[slot, w[0], w[1]:w[1] + tr, :] = _row_sum_lanes(lk)

            def s_suffix(w):
                logw_ref[slot, w[0], w[1]:w[1] + tr, :] = st.pop((w, "lsz")) + _dot(st.pop((w, "parts")), tri2)

            return _sb_items(edge), [s_logits, s_terms, s_suffix]

        def back(j, edge):
            vv = v_ref[pl.ds(pl.multiple_of(j * tk, tk), tk), :]
            slot = j % 2
            st = {}

            def s_weights(w):
                h, rs = w[0], slice(w[1], w[1] + tr)
                c = c_ref[h, rs, :]
                st[w, "a"] = jnp.exp(logw_ref[slot, h, rs, :] + jnp.tile(c, (1, tk // LANES))).astype(BF16)
                car_ref[h, rs, :] = jnp.where(lane == j, c, car_ref[h, rs, :])
                c_ref[h, rs, :] = c + lksum_ref[slot, h, rs, :]

            def s_values(w):
                oacc_ref[w[0], w[1]:w[1] + tr, :] += _dot(st.pop((w, "a")), vv)

            return _sb_items(edge), [s_weights, s_values]

        first = i * ratio
        _emit_skewed(front(first + ratio - 1, ratio - 1))
        for m in reversed(range(ratio - 1)):
            _emit_skewed(front(first + m, m), back(first + m + 1, m + 1))

        @pl.when(i == 0)
        def _():
            _emit_skewed(back(0, 0))

        def alive():
            return (jnp.max(c_ref[...]) >= SB_DEAD_CARRY).astype(jnp.int32)

        @pl.when(i > 0)
        def _():
            _emit_skewed(front(first - 1, None), back(first, 0))

            def step(state):
                pending, _ = state
                _emit_skewed(front(pending - 1, None), back(pending, None))
                return pending - 1, alive()

            pending, live = lax.while_loop(lambda s: (s[0] > 0) & (s[1] > 0), step, (first - 1, alive()))

            @pl.when(live > 0)
            def _():
                _emit_skewed(back(pending, None))

        o_ref[...] = jnp.where(low, oacc_ref[0], oacc_ref[1]).astype(BF16)

    return _call(
        body, (q, k, v), comm=comm, **_grid_ends(N_HEADS // 2, s_len // tq), name="sb_fwd",
        grid=(N_HEADS // 2, s_len // tq),
        in_specs=[pl.BlockSpec((tq, LANES), lambda p, i: (i, p)),
                  pl.BlockSpec((s_len, LANES), lambda p, i: (0, p)),
                  pl.BlockSpec((s_len, LANES), lambda p, i: (0, p))],
        out_specs=[pl.BlockSpec((tq, LANES), lambda p, i: (i, p)), pl.BlockSpec((2, tq, LANES), lambda p, i: (p, i, 0))],
        out_shape=[jax.ShapeDtypeStruct((s_len, N_HEADS * HEAD_DIM), BF16),
                   jax.ShapeDtypeStruct((N_HEADS, s_len, LANES), F32)],
        scratch_shapes=[pltpu.VMEM((2, tq, LANES), F32), pltpu.VMEM((2, tq, LANES), F32),
                        pltpu.VMEM((2, 2, tq, tk), F32), pltpu.VMEM((2, 2, tq, LANES), F32)],
        compiler_params=_cparams("arbitrary", "arbitrary"),
    )


def _sb_bwd(q, k, v, do, cars):
    s_len = q.shape[0]
    tq, tk, tr = SB_QUERIES, SB_KEYS, SB_ROWS
    nk, ratio = s_len // tk, tq // tk

    def body(q_ref, k_ref, v_ref, do_ref, car_ref, dq_ref, dk_ref, dv_ref,
             gleft_ref, dqacc_ref, dkacc_ref, dvacc_ref, logw_ref, lsz_ref, da_ref, a_ref, dz_ref):
        i = pl.program_id(1)

        @pl.when(i == 0)
        def _():
            dkacc_ref[...] = jnp.zeros_like(dkacc_ref)
            dvacc_ref[...] = jnp.zeros_like(dvacc_ref)

        qv = q_ref[...]
        dov = do_ref[...]
        low, tri_right2, tri_left2 = _sb_consts(tq, tk)
        lane = lax.broadcasted_iota(jnp.int32, (tr, LANES), 1)
        zero = jnp.zeros_like(qv)
        q_heads = (jnp.where(low, qv, zero), jnp.where(low, zero, qv))
        do_heads = (jnp.where(low, dov, zero), jnp.where(low, zero, dov))
        q_t = qv.astype(F32).T.astype(BF16)
        do_t = dov.astype(F32).T.astype(BF16)
        gleft_ref[...] = jnp.zeros_like(gleft_ref)
        dqacc_ref[...] = jnp.zeros_like(dqacc_ref)

        def front(j, edge):
            key_rows = pl.ds(pl.multiple_of(j * tk, tk), tk)
            keys, values = k_ref[key_rows, :], v_ref[key_rows, :]
            slot = j % 2
            st = {}

            def s_logits(w):
                h, rs = w[0], slice(w[1], w[1] + tr)
                st[w, "z"] = _dot_nt(q_heads[h][rs], keys)
                da_ref[slot, h, rs, :] = _dot_nt(do_heads[h][rs], values)

            def s_terms(w):
                h, rs = w[0], slice(w[1], w[1] + tr)
                valid = _sb_valid(w, edge) if w[2] else None
                lsz, lk = _sb_terms(st.pop((w, "z")), valid)
                st[w, "parts"] = _bf16_parts(lk)
                lsz = lsz if valid is None else jnp.where(valid, lsz, NEG_BIG)
                lsz_ref[slot, h, rs, :] = lsz
                st[w, "lszc"] = lsz + jnp.sum(jnp.where(lane == j, car_ref[h, rs, :], 0.0), axis=-1, keepdims=True)

            def s_suffix(w):
                logw_ref[slot, w[0], w[1]:w[1] + tr, :] = st.pop((w, "lszc")) + _dot(st.pop((w, "parts")), tri_right2)

            return _sb_items(edge), [s_logits, s_terms, s_suffix]

        def back(j, edge):
            kv = k_ref[pl.ds(pl.multiple_of(j * tk, tk), tk), :]
            slot = j % 2
            st = {}

            items = _sb_items(edge)
            head_rows = [[r0 for hh, r0, _ in items if hh == h] for h in range(2)]

            def s_weights(w):
                h, rs = w[0], slice(w[1], w[1] + tr)
                a = jnp.exp(logw_ref[slot, h, rs, :])
                g = a * da_ref[slot, h, rs, :]
                a_ref[h, rs, :] = a.astype(BF16)
                st[w, "g"], st[w, "parts"] = g, _bf16_parts(g)

            def s_prefix(w):
                st[w, "gs"] = _dot(st.pop((w, "parts")), tri_left2)

            def s_dz(w):
                h, rs = w[0], slice(w[1], w[1] + tr)
                g = st.pop((w, "g"))
                gleft = gleft_ref[h, rs, :]
                gsum = st.pop((w, "gs")) + jnp.tile(gleft, (1, tk // LANES))
                dz = (g - jnp.exp(lsz_ref[slot, h, rs, :]) * (g + gsum)).astype(BF16)
                st[w, "dz"] = dz
                dz_ref[h, rs, :] = dz
                gleft_ref[h, rs, :] = gleft + _row_sum_lanes(g)

            def s_products(w):
                h, rs = w[0], slice(w[1], w[1] + tr)
                dqacc_ref[h, rs, :] += _dot(st.pop((w, "dz")), kv)
                if w[1] == head_rows[h][-1]:
                    feat = slice(h * HEAD_DIM, (h + 1) * HEAD_DIM)
                    hr = slice(head_rows[h][0], tq)
                    dkacc_ref[j, feat, :] += _dot(q_t[feat, hr], dz_ref[h, hr, :])
                    dvacc_ref[j, feat, :] += _dot(do_t[feat, hr], a_ref[h, hr, :])

            return items, [s_weights, s_prefix, s_dz, s_products]

        first = i * ratio
        tile_max = jnp.max(jnp.maximum(car_ref[0], car_ref[1]), axis=0, keepdims=True)
        start = jnp.clip(first + ratio - jnp.sum(jnp.where(tile_max >= SB_DEAD_CARRY, 1, 0)), 0, first)

        @pl.when(start == first)
        def _():
            _emit_skewed(front(first, 0))

        @pl.when(start < first)
        def _():
            _emit_skewed(front(start, None))

            def step(jj, carry):
                _emit_skewed(front(jj, None), back(jj - 1, None))
                return carry

            lax.fori_loop(start + 1, first, step, 0)
            _emit_skewed(front(first, 0), back(first - 1, None))

        for m in range(1, ratio):
            _emit_skewed(front(first + m, m), back(first + m - 1, m - 1))
        _emit_skewed(back(first + ratio - 1, ratio - 1))
        dq_ref[...] = (Q_SCALE * jnp.where(low, dqacc_ref[0], dqacc_ref[1])).astype(BF16)

        @pl.when(i == s_len // tq - 1)
        def _():
            for j in range(nk):
                dk_ref[j * tk:(j + 1) * tk, :] = dkacc_ref[j].T.astype(BF16)
                dv_ref[j * tk:(j + 1) * tk, :] = dvacc_ref[j].T.astype(BF16)

    qblk = pl.BlockSpec((tq, LANES), lambda p, i: (i, p))
    col_full = pl.BlockSpec((s_len, LANES), lambda p, i: (0, p))
    return pl.pallas_call(
        body, name="sb_bwd",
        grid=(N_HEADS // 2, s_len // tq),
        in_specs=[qblk, col_full, col_full, qblk, pl.BlockSpec((2, tq, LANES), lambda p, i: (p, i, 0))],
        out_specs=[qblk, col_full, col_full],
        out_shape=[jax.ShapeDtypeStruct((s_len, N_HEADS * HEAD_DIM), BF16)] * 3,
        scratch_shapes=[pltpu.VMEM((2, tq, LANES), F32), pltpu.VMEM((2, tq, LANES), F32),
                        pltpu.VMEM((nk, LANES, tk), F32), pltpu.VMEM((nk, LANES, tk), F32)]
        + [pltpu.VMEM((2, 2, tq, tk), F32)] * 3 + [pltpu.VMEM((2, tq, tk), BF16)] * 2,
        compiler_params=_cparams("parallel", "arbitrary"),
    )(q, k, v, do, cars)


def _local_step(xs, tgt, gains, sinks, rel_bias, weights_of, ship):
    g1, gmix, g2, gfin = gains
    bkt = _rel_bucket_matrix()
    grads = {}

    def carried(outs, comm, count):
        return outs[:count], (list(outs[count:]) if comm is not None else None)

    wts = dict(weights_of(0, None))
    comm = ship("weights", 1)
    (x1, h1, a1, b1, u1), landed = carried(
        _ffn_fwd(xs, g1, wts["ffn1_w1t"], wts["ffn1_w3t"], wts["ffn1_w2"], "1", comm), comm, 5)
    wts.update(weights_of(1, landed))
    hm, qa, ka, va, qb, kb, vb, ga, gb = _proj_fwd(x1, gmix, wts["w_int"])
    oa = _swa_fwd(rel_bias, sinks, bkt, qa, ka, va)
    comm = ship("weights", 2)
    (ob, cars), landed = carried(_sb_fwd(qb, kb, vb, comm), comm, 2)
    wts.update(weights_of(2, landed))
    x2, mg = _merge_fwd(x1, oa, ob, ga, gb, wts["w_swa"], wts["w_sb"], wts["w_out"])
    x3, h3, a3, b3, u3 = _ffn_fwd(x2, g2, wts["ffn2_w1t"], wts["ffn2_w3t"], wts["ffn2_w2"], "2")
    dx3, loss, dgfin = _loss_fwd_bwd(x3, tgt, gfin)

    dx2, dg2, da3, db3, dx3b = _ffn_bwd(dx3, x2, g2, a3, b3, wts["ffn2_w1t"], wts["ffn2_w3t"], wts["ffn2_w2"], "2")
    big = {"ffn2_w1t": _matmul_tn(da3, h3, "ffn2_w1"), "ffn2_w3t": _matmul_tn(db3, h3, "ffn2_w3"),
           "ffn2_w2": _matmul_tn(u3, dx3b, "ffn2_w2")}

    doa, dob, dga, dgb, dpa, dpb, dx2b = _merge_bwd(dx2, oa, ob, ga, gb, wts["w_swa"], wts["w_sb"], wts["w_out"])
    comm = ship("grads", GROUPS[2], big)
    (dqa, dka, dva, dtab, dsink), landed = carried(_swa_bwd(rel_bias, sinks, bkt, qa, ka, va, doa, comm), comm, 5)
    grads[GROUPS[2]] = big if comm is None else landed[0]

    big = {"w_out": _matmul_tn(mg, dx2b, "w_out"), "w_swa": _matmul_tn(oa, dpa, "w_swa"),
           "w_sb": _matmul_tn(ob, dpb, "w_sb")}
    dqb, dkb, dvb = _sb_bwd(qb, kb, vb, dob, cars)
    dpieces = (dqa, dka.astype(BF16), dva.astype(BF16), dqb, dkb, dvb, dga, dgb)
    big["w_int"] = _matmul_tn_stacked(dpieces, hm, "w_in")
    dx1, dgmix = _proj_bwd(dpieces, dx2, x1, gmix, wts["w_int"])

    comm = ship("grads", GROUPS[1], big)
    (dx0, dg1, da1, db1, dx1b), landed = carried(
        _ffn_bwd(dx1, xs, g1, a1, b1, wts["ffn1_w1t"], wts["ffn1_w3t"], wts["ffn1_w2"], "1", comm), comm, 5)
    grads[GROUPS[1]] = big if comm is None else landed[0]

    prev = None
    for name, lhs, rhs in (("ffn1_w1", da1, h1), ("ffn1_w3", db1, h1), ("ffn1_w2", u1, dx1b)):
        comm = None if prev is None else ship("grads", (prev[0],), prev[1])
        res = _matmul_tn(lhs, rhs, name, comm)
        if prev is not None:
            grads[(prev[0],)] = prev[1] if comm is None else res[1]
        prev = (name, {_GRAD_KEY[name]: res if comm is None else res[0]})
    grads[(prev[0],)] = prev[1]

    small = {"gains": (dg1, dgmix, dg2, dgfin), "sinks": dsink[:, 0], "rel_bias": dtab[:, :N_HEADS]}
    return loss, dx0, small, grads


def _my_place():
    return lax.axis_index("x"), lax.axis_index("y"), lax.axis_index("c")


def _flip(v, bit):
    return 1 - v if bit else v


_RELATIONS = tuple((k >> 2 & 1, k >> 1 & 1, k & 1) for k in range(1, N_DEV))


def _gather_weights(blocks, tag):
    count = len(blocks)

    def body(*refs):
        x_refs, out_refs = refs[:count], refs[count:2 * count]
        send_sems, recv_sems, local_sems = refs[2 * count:]
        x, y, c = _my_place()
        me, sibling = (x, y, c), (x, y, 1 - c)
        chips = [(1 - x, y), (x, 1 - y), (1 - x, 1 - y)]

        def rows(s, px, py, pc):
            return out_refs[s].at[4 * px + 2 * py + pc]

        def copy(s, k, block, to, src=None):
            return pltpu.make_async_remote_copy(
                src_ref=rows(s, *block) if src is None else src, dst_ref=rows(s, *block),
                send_sem=send_sems.at[s, k], recv_sem=recv_sems.at[s, k],
                device_id=to, device_id_type=pl.DeviceIdType.MESH)

        mine = [pltpu.make_async_copy(x_refs[s], rows(s, *me), local_sems.at[s]) for s in range(count)]
        first, passed = [], []
        for s in range(count):
            mine[s].start()
            first.append(copy(s, 0, me, sibling, src=x_refs[s]))
            first += [copy(s, 1 + j, me, (*chip, c), src=x_refs[s]) for j, chip in enumerate(chips)]
        for cp in first:
            cp.start()
        for s in range(count):
            for j, chip in enumerate(chips):
                copy(s, 1 + j, (*chip, c), me).wait_recv()
                passed.append(copy(s, 4 + j, (*chip, c), sibling))
                passed[-1].start()
        for s in range(count):
            copy(s, 0, sibling, me).wait_recv()
            for j, chip in enumerate(chips):
                copy(s, 4 + j, (*chip, 1 - c), me).wait_recv()
        for cp in first + passed:
            cp.wait_send()
        for cp in mine:
            cp.wait()

    anywhere = pl.BlockSpec(memory_space=pl.ANY)
    return pl.pallas_call(
        body, name=f"gather_weights_{tag}",
        out_shape=[jax.ShapeDtypeStruct((N_DEV,) + b.shape, b.dtype) for b in blocks],
        in_specs=[anywhere] * count, out_specs=[anywhere] * count,
        scratch_shapes=[pltpu.SemaphoreType.DMA((count, N_DEV - 1)), pltpu.SemaphoreType.DMA((count, N_DEV - 1)),
                        pltpu.SemaphoreType.DMA((count,))],
    )(*blocks)


def _exchange_grads(gp, tag):
    def body(g_ref, out_ref, send_sems, recv_sems, local_sem):
        x, y, c = _my_place()
        me = 4 * x + 2 * y + c
        mine = pltpu.make_async_copy(g_ref.at[me], out_ref.at[me], local_sem)
        mine.start()
        copies = []
        for k, (fx, fy, fc) in enumerate(_RELATIONS):
            px, py, pc = _flip(x, fx), _flip(y, fy), _flip(c, fc)
            peer = 4 * px + 2 * py + pc
            copies.append((
                pltpu.make_async_remote_copy(
                    src_ref=g_ref.at[peer], dst_ref=out_ref.at[me], send_sem=send_sems.at[k], recv_sem=recv_sems.at[k],
                    device_id=(px, py, pc), device_id_type=pl.DeviceIdType.MESH),
                pltpu.make_async_remote_copy(
                    src_ref=g_ref.at[peer], dst_ref=out_ref.at[peer], send_sem=send_sems.at[k], recv_sem=recv_sems.at[k],
                    device_id=(px, py, pc), device_id_type=pl.DeviceIdType.MESH)))
        for out_cp, _ in copies:
            out_cp.start()
        for _, in_cp in copies:
            in_cp.wait_recv()
        for out_cp, _ in copies:
            out_cp.wait_send()
        mine.wait()

    return pl.pallas_call(
        body, name=f"exchange_grads_{tag}",
        out_shape=jax.ShapeDtypeStruct(gp.shape, gp.dtype),
        in_specs=[pl.BlockSpec(memory_space=pl.ANY)],
        out_specs=pl.BlockSpec(memory_space=pl.ANY),
        scratch_shapes=[pltpu.SemaphoreType.DMA((7,)), pltpu.SemaphoreType.DMA((7,)), pltpu.SemaphoreType.DMA(())],
    )(gp)


def _peers():
    x, y, c = _my_place()
    out = []
    for k, (fx, fy, fc) in enumerate(_RELATIONS):
        px, py, pc = _flip(x, fx), _flip(y, fy), _flip(c, fc)
        out.append((k, (px, py, pc), 4 * px + 2 * py + pc))
    return out, 4 * x + 2 * y + c


def _grid_ends(*grid):
    def first():
        return functools.reduce(lambda a, b: a & b, [pl.program_id(d) == 0 for d in range(len(grid))])

    def last():
        return functools.reduce(lambda a, b: a & b, [pl.program_id(d) == n - 1 for d, n in enumerate(grid)])

    return {"first": first, "last": last}


def _call(body, operands, *, comm=None, first=None, last=None, **kw):
    if comm is None:
        return pl.pallas_call(body, **kw)(*operands)
    in_specs, out_specs, out_shape = list(kw.pop("in_specs")), list(kw.pop("out_specs")), list(kw.pop("out_shape"))
    scratch = list(kw.pop("scratch_shapes", ()))
    n_in, n_out, n_scr, n_src = len(in_specs), len(out_specs), len(scratch), len(comm)

    def wrapped(*refs):
        ins, src_refs = refs[:n_in], refs[n_in:n_in + n_src]
        outs = refs[n_in + n_src:n_in + n_src + n_out]
        land_refs = refs[n_in + n_src + n_out:n_in + 2 * n_src + n_out]
        scr = refs[n_in + 2 * n_src + n_out:n_in + 2 * n_src + n_out + n_scr]
        send_sems, recv_sems, local_sems = refs[n_in + 2 * n_src + n_out + n_scr:]
        peers, me = _peers()
        mine, going, coming = [], [], []
        for s, (_, per_peer) in enumerate(comm):
            src_ref, land_ref = src_refs[s], land_refs[s]
            mine.append(pltpu.make_async_copy(src_ref.at[me] if per_peer else src_ref, land_ref.at[me], local_sems.at[s]))
            for k, where, slab in peers:
                piece = src_ref.at[slab] if per_peer else src_ref
                going.append(pltpu.make_async_remote_copy(
                    src_ref=piece, dst_ref=land_ref.at[me], send_sem=send_sems.at[s, k], recv_sem=recv_sems.at[s, k],
                    device_id=where, device_id_type=pl.DeviceIdType.MESH))
                coming.append(pltpu.make_async_remote_copy(
                    src_ref=piece, dst_ref=land_ref.at[slab], send_sem=send_sems.at[s, k], recv_sem=recv_sems.at[s, k],
                    device_id=where, device_id_type=pl.DeviceIdType.MESH))

        @pl.when(first())
        def _():
            for cp in mine + going:
                cp.start()

        body(*ins, *outs, *scr)

        @pl.when(last())
        def _():
            for cp in coming:
                cp.wait_recv()
            for cp in going:
                cp.wait_send()
            for cp in mine:
                cp.wait()

    anywhere = pl.BlockSpec(memory_space=pl.ANY)
    lands = [jax.ShapeDtypeStruct(src.shape if per_peer else (N_DEV,) + src.shape, src.dtype) for src, per_peer in comm]
    return pl.pallas_call(
        wrapped, in_specs=in_specs + [anywhere] * n_src, out_specs=out_specs + [anywhere] * n_src,
        out_shape=out_shape + lands,
        scratch_shapes=scratch + [pltpu.SemaphoreType.DMA((n_src, N_DEV - 1)), pltpu.SemaphoreType.DMA((n_src, N_DEV - 1)),
                                  pltpu.SemaphoreType.DMA((n_src,))],
        **kw)(*operands, *[src for src, _ in comm])


def _adamw(w, g, m, v):
    m = ADAM_B1 * m + (1.0 - ADAM_B1) * g
    v = ADAM_B2 * v + (1.0 - ADAM_B2) * jnp.square(g)
    m_hat = m / (1.0 - ADAM_B1 ** ADAM_STEP)
    v_hat = v / (1.0 - ADAM_B2 ** ADAM_STEP)
    delta = -ADAM_LR * (m_hat / (jnp.sqrt(v_hat) + ADAM_EPS) + ADAM_WD * w)
    return delta, m, v


def _sum_and_adamw(parts, w, m, v, tr, tag):
    rows = w.shape[0]
    assert rows % tr == 0

    def body(p_ref, w_ref, m_ref, v_ref, g_out, d_out, m_out, v_out):
        g = p_ref[0].astype(F32)
        for d in range(1, N_DEV):
            g = g + p_ref[d].astype(F32)
        delta, mn, vn = _adamw(w_ref[...], g, m_ref[...], v_ref[...])
        g_out[...] = g
        d_out[...] = delta
        m_out[...] = mn
        v_out[...] = vn

    sp = pl.BlockSpec((tr, D_MODEL), lambda i: (i, 0))
    return pl.pallas_call(
        body, name=f"sum_and_adamw_{tag}",
        grid=(rows // tr,),
        in_specs=[pl.BlockSpec((N_DEV, tr, D_MODEL), lambda i: (0, i, 0)), sp, sp, sp],
        out_specs=[sp] * 4,
        out_shape=[jax.ShapeDtypeStruct(w.shape, F32)] * 4,
        compiler_params=_cparams("parallel"),
    )(parts, w, m, v)


def _small_allreduce_adamw(part, w, m, v):
    def body(p_ref, w_ref, m_ref, v_ref, g_out, d_out, m_out, v_out, buf, send_sems, recv_sems):
        x, y, c = _my_place()
        me = 4 * x + 2 * y + c
        buf[me] = p_ref[...]
        copies = []
        for k, (fx, fy, fc) in enumerate(_RELATIONS):
            px, py, pc = _flip(x, fx), _flip(y, fy), _flip(c, fc)
            peer = 4 * px + 2 * py + pc
            copies.append((
                pltpu.make_async_remote_copy(
                    src_ref=buf.at[me], dst_ref=buf.at[me], send_sem=send_sems.at[k], recv_sem=recv_sems.at[k],
                    device_id=(px, py, pc), device_id_type=pl.DeviceIdType.MESH),
                pltpu.make_async_remote_copy(
                    src_ref=buf.at[me], dst_ref=buf.at[peer], send_sem=send_sems.at[k], recv_sem=recv_sems.at[k],
                    device_id=(px, py, pc), device_id_type=pl.DeviceIdType.MESH)))
        for out_cp, _ in copies:
            out_cp.start()
        for _, in_cp in copies:
            in_cp.wait_recv()
        for out_cp, _ in copies:
            out_cp.wait_send()
        g = buf[0]
        for d in range(1, N_DEV):
            g = g + buf[d]
        delta, mn, vn = _adamw(w_ref[...], g, m_ref[...], v_ref[...])
        g_out[...] = g
        d_out[...] = delta
        m_out[...] = mn
        v_out[...] = vn

    vm = pl.BlockSpec(memory_space=pltpu.VMEM)
    return pl.pallas_call(
        body, name="small_allreduce_adamw",
        in_specs=[vm] * 4, out_specs=[vm] * 4,
        out_shape=[jax.ShapeDtypeStruct(w.shape, F32)] * 4,
        scratch_shapes=[pltpu.VMEM((N_DEV,) + part.shape, F32),
                        pltpu.SemaphoreType.DMA((7,)), pltpu.SemaphoreType.DMA((7,))],
    )(part, w, m, v)


_TRANSPOSED = ("ffn1_w1", "ffn1_w3", "w_in", "ffn2_w1", "ffn2_w3")
_BRANCH = ("w_branch_swa", "w_branch_sb")


def _pack_shards(t, names):
    parts = []
    for name in names:
        a = t[name][0]
        if name in _TRANSPOSED:
            a = a.T
        elif name in _BRANCH:
            a = a.reshape(64, D_MODEL)
        parts.append(a)
    return jnp.concatenate(parts, axis=0)


def _unpack_shards(p, names):
    out, lo = {}, 0
    for name in names:
        a = p[lo:lo + BIG_ROWS[BIG_NAMES.index(name)]]
        lo += a.shape[0]
        if name in _TRANSPOSED:
            a = a.T
        elif name in _BRANCH:
            a = a.reshape(512, 128)
        out[name] = a[None]
    return out


def _full_weights(zones, names):
    out = {}
    for name, a in zip(names, zones):
        if name in _BRANCH:
            a = a.reshape(N_DEV, 512, 128).transpose(1, 0, 2).reshape(512, D_MODEL)
        out[_GRAD_KEY[name]] = a.reshape(-1, D_MODEL)
    return out


_GRAD_KEY = {"ffn1_w1": "ffn1_w1t", "ffn1_w3": "ffn1_w3t", "ffn1_w2": "ffn1_w2", "w_in": "w_int",
             "w_branch_swa": "w_swa", "w_branch_sb": "w_sb", "w_out": "w_out",
             "ffn2_w1": "ffn2_w1t", "ffn2_w3": "ffn2_w3t", "ffn2_w2": "ffn2_w2"}


def _pack_full_grads(big, names):
    parts = []
    for name in names:
        a = big[_GRAD_KEY[name]]
        if name in _BRANCH:
            a = a.reshape(512, N_DEV, 128).transpose(1, 0, 2)
        parts.append(a.reshape(N_DEV, BIG_ROWS[BIG_NAMES.index(name)], D_MODEL).astype(BF16))
    return jnp.concatenate(parts, axis=1)


_SMALL_NAMES = ("norm_ffn1", "norm_mix", "norm_ffn2", "norm_final", "swa_sinks", "rel_bias")


def _pack_small(vals):
    rows = []
    for a in vals:
        a = a.reshape(-1)
        rows.append(jnp.pad(a, (0, D_MODEL - a.shape[0])))
    rows += [jnp.zeros((D_MODEL,), F32)] * (SMALL_ROWS - len(rows))
    return jnp.stack(rows)


def _unpack_small(p):
    return {"norm_ffn1": p[0:1], "norm_mix": p[1:2], "norm_ffn2": p[2:3], "norm_final": p[3],
            "swa_sinks": p[4:5, :N_HEADS], "rel_bias": p[5, :REL_BUCKETS * N_HEADS].reshape(REL_BUCKETS, N_HEADS)}


ALL_NAMES = ("norm_ffn1", "ffn1_w1", "ffn1_w3", "ffn1_w2", "norm_mix", "w_in", "swa_sinks", "rel_bias",
             "w_branch_swa", "w_branch_sb", "w_out", "norm_ffn2", "ffn2_w1", "ffn2_w3", "ffn2_w2", "norm_final")


def kernel(x, norm_ffn1, ffn1_w1, ffn1_w3, ffn1_w2, norm_mix, w_in, swa_sinks, rel_bias, w_branch_swa, w_branch_sb, w_out, norm_ffn2, ffn2_w1, ffn2_w3, ffn2_w2, norm_final, loss_target, m_norm_ffn1, m_ffn1_w1, m_ffn1_w3, m_ffn1_w2, m_norm_mix, m_w_in, m_swa_sinks, m_rel_bias, m_w_branch_swa, m_w_branch_sb, m_w_out, m_norm_ffn2, m_ffn2_w1, m_ffn2_w3, m_ffn2_w2, m_norm_final, v_norm_ffn1, v_ffn1_w1, v_ffn1_w3, v_ffn1_w2, v_norm_mix, v_w_in, v_swa_sinks, v_rel_bias, v_w_branch_swa, v_w_branch_sb, v_w_out, v_norm_ffn2, v_ffn2_w1, v_ffn2_w3, v_ffn2_w2, v_norm_final):
    w = dict(zip(ALL_NAMES, (norm_ffn1, ffn1_w1, ffn1_w3, ffn1_w2, norm_mix, w_in, swa_sinks, rel_bias,
                             w_branch_swa, w_branch_sb, w_out, norm_ffn2, ffn2_w1, ffn2_w3, ffn2_w2, norm_final)))
    m = dict(zip(ALL_NAMES, (m_norm_ffn1, m_ffn1_w1, m_ffn1_w3, m_ffn1_w2, m_norm_mix, m_w_in, m_swa_sinks, m_rel_bias,
                             m_w_branch_swa, m_w_branch_sb, m_w_out, m_norm_ffn2, m_ffn2_w1, m_ffn2_w3, m_ffn2_w2,
                             m_norm_final)))
    v = dict(zip(ALL_NAMES, (v_norm_ffn1, v_ffn1_w1, v_ffn1_w3, v_ffn1_w2, v_norm_mix, v_w_in, v_swa_sinks, v_rel_bias,
                             v_w_branch_swa, v_w_branch_sb, v_w_out, v_norm_ffn2, v_ffn2_w1, v_ffn2_w3, v_ffn2_w2,
                             v_norm_final)))

    def my_blocks(group):
        return [_pack_shards(w, (name,)).astype(BF16) for name in GROUPS[group]]

    gathered0 = _gather_weights(my_blocks(0), "group0")

    def weights_of(group, landed):
        return _full_weights(gathered0 if group == 0 else landed, GROUPS[group])

    def ship(kind, which, grads=None):
        if kind == "weights":
            return [(block, False) for block in my_blocks(which)]
        return [(_pack_full_grads(grads, which), True)]

    gains = (norm_ffn1, norm_mix, norm_ffn2, norm_final.reshape(1, D_MODEL))
    loss, dx, small, parts = _local_step(x[0], loss_target[0], gains, swa_sinks, rel_bias, weights_of, ship)

    big_outs = [{}, {}, {}, {}]
    for names, tile in zip(SUM_GROUPS, SUM_TILE):
        landed = parts[names]
        if isinstance(landed, dict):
            landed = _exchange_grads(_pack_full_grads(landed, names), names[0])
        res = _sum_and_adamw(landed, _pack_shards(w, names), _pack_shards(m, names), _pack_shards(v, names),
                             tile, names[0])
        for acc, packed in zip(big_outs, res):
            acc.update(_unpack_shards(packed, names))
    g_big, d_big, m_big, v_big = big_outs

    small_part = _pack_small(small["gains"] + (small["sinks"], small["rel_bias"], loss))
    zero = jnp.zeros((1,), F32)
    small_res = _small_allreduce_adamw(
        small_part, _pack_small([w[n] for n in _SMALL_NAMES] + [zero]), _pack_small([m[n] for n in _SMALL_NAMES] + [zero]),
        _pack_small([v[n] for n in _SMALL_NAMES] + [zero]))
    g_sm, d_sm, m_sm, v_sm = (_unpack_small(p) for p in small_res)

    outs = [small_res[0][len(_SMALL_NAMES), 0], dx[None]]
    for big_d, small_d in ((g_big, g_sm), (d_big, d_sm), (m_big, m_sm), (v_big, v_sm)):
        merged = {**big_d, **small_d}
        outs += [merged[n] for n in ALL_NAMES]
    return tuple(outs)
```

```python
import functools

import jax
import jax.numpy as jnp
import numpy as np
from jax import lax
from jax.experimental import pallas as pl
from jax.experimental.pallas import tpu as pltpu

F32 = jnp.float32
BF16 = jnp.bfloat16

D_MODEL = 1024
D_FF = 2816
HEAD_DIM = 64
N_HEADS = 8
SWA_KV_HEADS = 2
SWA_GROUP = 4
SWA_BLOCK = 128
REL_BUCKETS = 32
REL_MAX_DIST = 128
RMS_EPS = 1e-6
NEG_BIG = -1e30
Q_SCALE = HEAD_DIM ** -0.5
LANES = 128

N_DEV = 8

ADAM_LR = 0.001
ADAM_B1 = 0.9
ADAM_B2 = 0.999
ADAM_EPS = 1e-08
ADAM_WD = 0.01
ADAM_STEP = 10

IN_SIZES = (512, 128, 128, 512, 512, 512, 1024, 1024)
IN_OFFS = tuple(int(v) for v in np.cumsum((0,) + IN_SIZES))
IN_W = IN_OFFS[-1]

BIG_NAMES = ("ffn1_w1", "ffn1_w3", "ffn1_w2", "w_in", "w_branch_swa", "w_branch_sb", "w_out",
             "ffn2_w1", "ffn2_w3", "ffn2_w2")
BIG_ROWS = (352, 352, 352, 544, 64, 64, 128, 352, 352, 352)
SMALL_ROWS = 8
GROUPS = (BIG_NAMES[0:3], BIG_NAMES[3:7], BIG_NAMES[7:10])
SUM_GROUPS = tuple((n,) for n in GROUPS[0]) + (GROUPS[1],) + tuple((n,) for n in GROUPS[2])
SUM_TILE = (176, 176, 176, 160, 176, 176, 176)

VMEM_LIMIT = 56 * 1024 * 1024
FFN_PIECES = 2
SB_QUERIES = 512
SB_KEYS = 256
SB_ROWS = 256
SB_SUM_PARTS = 1
SB_LOGIT_CAP = 80.0
SB_DEAD_CARRY = -110.0


def _dot(a, b):
    return jnp.dot(a, b, preferred_element_type=F32)


def _dot_nt(a, b):
    return lax.dot_general(a, b, (((1,), (1,)), ((), ())), preferred_element_type=F32)


def _dot_tn(a, b):
    return lax.dot_general(a, b, (((0,), (0,)), ((), ())), preferred_element_type=F32)


def _cparams(*sem):
    return pltpu.CompilerParams(dimension_semantics=sem, vmem_limit_bytes=VMEM_LIMIT)


def _rms_rstd(xv):
    return lax.rsqrt(jnp.mean(xv * xv, axis=-1, keepdims=True) + RMS_EPS)


def _rms_bwd(dh, xv, r, g):
    xhat = xv * r
    dg = jnp.sum(dh * xhat, axis=0, keepdims=True)
    dxn = dh * g
    dx = r * (dxn - xhat * jnp.mean(dxn * xhat, axis=-1, keepdims=True))
    return dx, dg


def _ffn_fwd(x, g, w1t, w3t, w2, tag, comm=None):
    s_len = x.shape[0]
    tm, tf = min(1024, s_len), 256
    nf = D_FF // tf

    def body(x_ref, g_ref, w1_ref, w3_ref, w2_ref, xo_ref, h_ref, a_ref, b_ref, u_ref, acc_ref, hs_ref):
        j = pl.program_id(1)

        @pl.when(j == 0)
        def _():
            xv = x_ref[...]
            h = (xv * _rms_rstd(xv) * g_ref[...]).astype(BF16)
            hs_ref[...] = h
            h_ref[...] = h
            acc_ref[...] = jnp.zeros_like(acc_ref)

        st = {}

        def s_up(rs):
            h = hs_ref[rs, :]
            st[rs.start, "ab"] = (_dot_nt(h, w1_ref[...]), _dot_nt(h, w3_ref[...]))

        def s_act(rs):
            a, b = st.pop((rs.start, "ab"))
            a_ref[rs, :] = a.astype(BF16)
            b_ref[rs, :] = b.astype(BF16)
            uh = (0.5 * (a * jax.nn.sigmoid(a) * b)).astype(BF16)
            u_ref[rs, :] = uh
            st[rs.start, "u"] = uh

        def s_down(rs):
            acc_ref[rs, :] += _dot(st.pop((rs.start, "u")), w2_ref[...])

        _emit_skewed(([slice(r, r + tm // FFN_PIECES) for r in range(0, tm, tm // FFN_PIECES)], [s_up, s_act, s_down]))

        @pl.when(j == nf - 1)
        def _():
            xo_ref[...] = x_ref[...] + acc_ref[...]

    row = lambda i, j: (i, 0)
    return _call(
        body, (x, g, w1t, w3t, w2), comm=comm, **_grid_ends(s_len // tm, nf), name=f"ffn_fwd_{tag}",
        grid=(s_len // tm, nf),
        in_specs=[pl.BlockSpec((tm, D_MODEL), row), pl.BlockSpec((1, D_MODEL), lambda i, j: (0, 0)),
                  pl.BlockSpec((tf, D_MODEL), lambda i, j: (j, 0)), pl.BlockSpec((tf, D_MODEL), lambda i, j: (j, 0)),
                  pl.BlockSpec((tf, D_MODEL), lambda i, j: (j, 0))],
        out_specs=[pl.BlockSpec((tm, D_MODEL), row), pl.BlockSpec((tm, D_MODEL), row),
                   pl.BlockSpec((tm, tf), lambda i, j: (i, j)), pl.BlockSpec((tm, tf), lambda i, j: (i, j)),
                   pl.BlockSpec((tm, tf), lambda i, j: (i, j))],
        out_shape=[jax.ShapeDtypeStruct((s_len, D_MODEL), F32), jax.ShapeDtypeStruct((s_len, D_MODEL), BF16),
                   jax.ShapeDtypeStruct((s_len, D_FF), BF16), jax.ShapeDtypeStruct((s_len, D_FF), BF16),
                   jax.ShapeDtypeStruct((s_len, D_FF), BF16)],
        scratch_shapes=[pltpu.VMEM((tm, D_MODEL), F32), pltpu.VMEM((tm, D_MODEL), BF16)],
        compiler_params=_cparams("arbitrary", "arbitrary"),
    )


def _ffn_bwd(dy, x, g, a, b, w1t, w3t, w2, tag, comm=None):
    s_len = x.shape[0]
    tm, tf = min(1024, s_len), 256
    nf = D_FF // tf

    def body(dy_ref, x_ref, g_ref, a_ref, b_ref, w1_ref, w3_ref, w2_ref,
             dx_ref, dg_ref, da_ref, db_ref, dyb_ref, acc_ref, dys_ref):
        i, j = pl.program_id(0), pl.program_id(1)

        @pl.when(j == 0)
        def _():
            dyb = dy_ref[...].astype(BF16)
            dys_ref[...] = dyb
            dyb_ref[...] = dyb
            acc_ref[...] = jnp.zeros_like(acc_ref)

        @pl.when((i == 0) & (j == 0))
        def _():
            dg_ref[...] = jnp.zeros_like(dg_ref)

        st = {}

        def s_du(rs):
            st[rs.start, "du"] = 0.5 * _dot_nt(dys_ref[rs, :], w2_ref[...])

        def s_act(rs):
            du = st.pop((rs.start, "du"))
            av = a_ref[rs, :].astype(F32)
            bv = b_ref[rs, :].astype(F32)
            sg = jax.nn.sigmoid(av)
            sil = av * sg
            da = (du * bv * (sg + sil * (1.0 - sg))).astype(BF16)
            db = (du * sil).astype(BF16)
            da_ref[rs, :] = da
            db_ref[rs, :] = db
            st[rs.start, "dab"] = (da, db)

        def s_dh(rs):
            da, db = st.pop((rs.start, "dab"))
            acc_ref[rs, :] += _dot(da, w1_ref[...]) + _dot(db, w3_ref[...])

        _emit_skewed(([slice(r, r + tm // FFN_PIECES) for r in range(0, tm, tm // FFN_PIECES)], [s_du, s_act, s_dh]))

        @pl.when(j == nf - 1)
        def _():
            xv = x_ref[...]
            dx, dg = _rms_bwd(acc_ref[...], xv, _rms_rstd(xv), g_ref[...])
            dx_ref[...] = dy_ref[...] + dx
            dg_ref[...] += dg

    row = lambda i, j: (i, 0)
    blk = lambda i, j: (i, j)
    wsp = pl.BlockSpec((tf, D_MODEL), lambda i, j: (j, 0))
    return _call(
        body, (dy, x, g, a, b, w1t, w3t, w2), comm=comm, **_grid_ends(s_len // tm, nf), name=f"ffn_bwd_{tag}",
        grid=(s_len // tm, nf),
        in_specs=[pl.BlockSpec((tm, D_MODEL), row), pl.BlockSpec((tm, D_MODEL), row),
                  pl.BlockSpec((1, D_MODEL), lambda i, j: (0, 0)),
                  pl.BlockSpec((tm, tf), blk), pl.BlockSpec((tm, tf), blk), wsp, wsp, wsp],
        out_specs=[pl.BlockSpec((tm, D_MODEL), row), pl.BlockSpec((1, D_MODEL), lambda i, j: (0, 0)),
                   pl.BlockSpec((tm, tf), blk), pl.BlockSpec((tm, tf), blk), pl.BlockSpec((tm, D_MODEL), row)],
        out_shape=[jax.ShapeDtypeStruct((s_len, D_MODEL), F32), jax.ShapeDtypeStruct((1, D_MODEL), F32),
                   jax.ShapeDtypeStruct((s_len, D_FF), BF16), jax.ShapeDtypeStruct((s_len, D_FF), BF16),
                   jax.ShapeDtypeStruct((s_len, D_MODEL), BF16)],
        scratch_shapes=[pltpu.VMEM((tm, D_MODEL), F32), pltpu.VMEM((tm, D_MODEL), BF16)],
        compiler_params=_cparams("arbitrary", "arbitrary"),
    )


def _matmul_tn(lhs, rhs, tag, comm=None):
    s_len, m = lhs.shape
    n = rhs.shape[1]
    tm = min(512, s_len)
    tj = m if m <= 1024 else 1408
    assert m % tj == 0
    last_rows = s_len // tm - 1

    def body(l_ref, r_ref, o_ref, acc_ref):
        i = pl.program_id(1)

        @pl.when(i == 0)
        def _():
            acc_ref[...] = jnp.zeros_like(acc_ref)

        acc_ref[...] += _dot_tn(l_ref[...], r_ref[...])

        @pl.when(i == last_rows)
        def _():
            o_ref[...] = acc_ref[...].astype(BF16)

    res = _call(
        body, (lhs, rhs), comm=comm, **_grid_ends(m // tj, s_len // tm), name=f"matmul_tn_{tag}",
        grid=(m // tj, s_len // tm),
        in_specs=[pl.BlockSpec((tm, tj), lambda j, i: (i, j)), pl.BlockSpec((tm, n), lambda j, i: (i, 0))],
        out_specs=[pl.BlockSpec((tj, n), lambda j, i: (j, 0))],
        out_shape=[jax.ShapeDtypeStruct((m, n), BF16)],
        scratch_shapes=[pltpu.VMEM((tj, n), F32)],
        compiler_params=_cparams("arbitrary", "arbitrary"),
    )
    return res[0] if comm is None else tuple(res)


def _matmul_tn_stacked(pieces, rhs, tag):
    s_len, n = rhs.shape
    widths = [p.shape[1] for p in pieces]
    offs = [sum(widths[:k]) for k in range(len(widths) + 1)]
    tm = min(256, s_len)
    last_rows = s_len // tm - 1

    def body(*refs):
        l_refs, r_ref, o_ref, acc_ref = refs[:len(pieces)], refs[-3], refs[-2], refs[-1]
        i = pl.program_id(0)

        @pl.when(i == 0)
        def _():
            acc_ref[...] = jnp.zeros_like(acc_ref)

        rv = r_ref[...]
        for k, l_ref in enumerate(l_refs):
            acc_ref[offs[k]:offs[k + 1], :] += _dot_tn(l_ref[...], rv)

        @pl.when(i == last_rows)
        def _():
            o_ref[...] = acc_ref[...].astype(BF16)

    row = lambda i: (i, 0)
    return pl.pallas_call(
        body, name=f"matmul_tn_{tag}",
        grid=(s_len // tm,),
        in_specs=[pl.BlockSpec((tm, w), row) for w in widths] + [pl.BlockSpec((tm, n), row)],
        out_specs=pl.BlockSpec((offs[-1], n), lambda i: (0, 0)),
        out_shape=jax.ShapeDtypeStruct((offs[-1], n), BF16),
        scratch_shapes=[pltpu.VMEM((offs[-1], n), F32)],
        compiler_params=_cparams("arbitrary"),
    )(*pieces, rhs)


def _proj_fwd(x1, g, wint):
    s_len = x1.shape[0]
    tm = min(512, s_len)
    dts = (BF16, BF16, BF16, BF16, BF16, BF16, F32, F32)

    def body(x_ref, g_ref, w_ref, h_ref, *outs):
        xv = x_ref[...]
        h = (xv * _rms_rstd(xv) * g_ref[...]).astype(BF16)
        h_ref[...] = h
        for p, o_ref in enumerate(outs):
            val = _dot_nt(h, w_ref[IN_OFFS[p]:IN_OFFS[p + 1], :])
            if p == 3:
                val = val * Q_SCALE
            o_ref[...] = val.astype(dts[p])

    row = lambda i: (i, 0)
    return pl.pallas_call(
        body, name="proj_fwd",
        grid=(s_len // tm,),
        in_specs=[pl.BlockSpec((tm, D_MODEL), row), pl.BlockSpec((1, D_MODEL), lambda i: (0, 0)),
                  pl.BlockSpec((IN_W, D_MODEL), lambda i: (0, 0))],
        out_specs=[pl.BlockSpec((tm, D_MODEL), row)] + [pl.BlockSpec((tm, w), row) for w in IN_SIZES],
        out_shape=[jax.ShapeDtypeStruct((s_len, D_MODEL), BF16)]
        + [jax.ShapeDtypeStruct((s_len, w), dt) for w, dt in zip(IN_SIZES, dts)],
        compiler_params=_cparams("parallel"),
    )(x1, g, wint)


def _proj_bwd(dpieces, dx2, x1, g, wint):
    s_len = x1.shape[0]
    tm = min(512, s_len)

    def body(*refs):
        dps = refs[:8]
        dx2_ref, x_ref, g_ref, w_ref, dx_ref, dg_ref = refs[8:]

        @pl.when(pl.program_id(0) == 0)
        def _():
            dg_ref[...] = jnp.zeros_like(dg_ref)

        dh = _dot(dps[0][...], w_ref[IN_OFFS[0]:IN_OFFS[1], :])
        for p in range(1, 8):
            dh += _dot(dps[p][...], w_ref[IN_OFFS[p]:IN_OFFS[p + 1], :])
        xv = x_ref[...]
        dx, dg = _rms_bwd(dh, xv, _rms_rstd(xv), g_ref[...])
        dx_ref[...] = dx2_ref[...] + dx
        dg_ref[...] += dg

    row = lambda i: (i, 0)
    return pl.pallas_call(
        body, name="proj_bwd",
        grid=(s_len // tm,),
        in_specs=[pl.BlockSpec((tm, w), row) for w in IN_SIZES]
        + [pl.BlockSpec((tm, D_MODEL), row), pl.BlockSpec((tm, D_MODEL), row),
           pl.BlockSpec((1, D_MODEL), lambda i: (0, 0)), pl.BlockSpec((IN_W, D_MODEL), lambda i: (0, 0))],
        out_specs=[pl.BlockSpec((tm, D_MODEL), row), pl.BlockSpec((1, D_MODEL), lambda i: (0, 0))],
        out_shape=[jax.ShapeDtypeStruct((s_len, D_MODEL), F32), jax.ShapeDtypeStruct((1, D_MODEL), F32)],
        compiler_params=_cparams("arbitrary"),
    )(*dpieces, dx2, x1, g, wint)


def _merge_fwd(x1, oa, ob, ga, gb, wswa, wsb, wout):
    s_len = x1.shape[0]
    tm = min(512, s_len)

    def body(x_ref, oa_ref, ob_ref, ga_ref, gb_ref, wa_ref, wb_ref, wo_ref, xo_ref, mg_ref):
        pa = _dot(oa_ref[...], wa_ref[...])
        pb = _dot(ob_ref[...], wb_ref[...])
        mg = (jax.nn.sigmoid(ga_ref[...]) * pa + jax.nn.sigmoid(gb_ref[...]) * pb).astype(BF16)
        mg_ref[...] = mg
        xo_ref[...] = x_ref[...] + _dot(mg, wo_ref[...])

    row = lambda i: (i, 0)
    full = lambda i: (0, 0)
    return pl.pallas_call(
        body, name="merge_fwd",
        grid=(s_len // tm,),
        in_specs=[pl.BlockSpec((tm, D_MODEL), row), pl.BlockSpec((tm, 512), row), pl.BlockSpec((tm, 512), row),
                  pl.BlockSpec((tm, D_MODEL), row), pl.BlockSpec((tm, D_MODEL), row),
                  pl.BlockSpec((512, D_MODEL), full), pl.BlockSpec((512, D_MODEL), full),
                  pl.BlockSpec((D_MODEL, D_MODEL), full)],
        out_specs=[pl.BlockSpec((tm, D_MODEL), row), pl.BlockSpec((tm, D_MODEL), row)],
        out_shape=[jax.ShapeDtypeStruct((s_len, D_MODEL), F32), jax.ShapeDtypeStruct((s_len, D_MODEL), BF16)],
        compiler_params=_cparams("parallel"),
    )(x1, oa, ob, ga, gb, wswa, wsb, wout)


def _merge_bwd(dx2, oa, ob, ga, gb, wswa, wsb, wout, comm=None):
    s_len = dx2.shape[0]
    tm = min(512, s_len)

    def body(dx_ref, oa_ref, ob_ref, ga_ref, gb_ref, wa_ref, wb_ref, wo_ref,
             doa_ref, dob_ref, dga_ref, dgb_ref, dpa_ref, dpb_ref, dxb_ref):
        dxb = dx_ref[...].astype(BF16)
        dxb_ref[...] = dxb
        dmg = _dot_nt(dxb, wo_ref[...])
        for o_ref, g_ref, w_ref, do_ref, dg_ref, dp_ref in (
                (oa_ref, ga_ref, wa_ref, doa_ref, dga_ref, dpa_ref),
                (ob_ref, gb_ref, wb_ref, dob_ref, dgb_ref, dpb_ref)):
            pv = _dot(o_ref[...], w_ref[...])
            sg = jax.nn.sigmoid(g_ref[...])
            dp = (dmg * sg).astype(BF16)
            dp_ref[...] = dp
            dg_ref[...] = (dmg * pv * sg * (1.0 - sg)).astype(BF16)
            do_ref[...] = _dot_nt(dp, w_ref[...]).astype(BF16)

    row = lambda i: (i, 0)
    full = lambda i: (0, 0)
    wide = pl.BlockSpec((tm, D_MODEL), row)
    half = pl.BlockSpec((tm, 512), row)
    return _call(
        body, (dx2, oa, ob, ga, gb, wswa, wsb, wout), comm=comm, **_grid_ends(s_len // tm), name="merge_bwd",
        grid=(s_len // tm,),
        in_specs=[wide, half, half, wide, wide, pl.BlockSpec((512, D_MODEL), full),
                  pl.BlockSpec((512, D_MODEL), full), pl.BlockSpec((D_MODEL, D_MODEL), full)],
        out_specs=[half, half, wide, wide, wide, wide, wide],
        out_shape=[jax.ShapeDtypeStruct((s_len, 512), BF16)] * 2 + [jax.ShapeDtypeStruct((s_len, D_MODEL), BF16)] * 5,
        compiler_params=_cparams("arbitrary"),
    )


def _loss_fwd_bwd(x3, tgt, g):
    s_len = x3.shape[0]
    tm = min(1024, s_len)

    def body(x_ref, t_ref, g_ref, dx_ref, loss_ref, dg_ref):
        @pl.when(pl.program_id(0) == 0)
        def _():
            loss_ref[...] = jnp.zeros_like(loss_ref)
            dg_ref[...] = jnp.zeros_like(dg_ref)

        xv = x_ref[...]
        gv = g_ref[...]
        r = _rms_rstd(xv)
        err = xv * r * gv - t_ref[...]
        loss_ref[...] += 0.5 * jnp.sum(jnp.mean(err * err, axis=-1, keepdims=True), axis=0, keepdims=True)
        dx, dg = _rms_bwd(err * (1.0 / D_MODEL), xv, r, gv)
        dx_ref[...] = dx
        dg_ref[...] += dg

    row = lambda i: (i, 0)
    return pl.pallas_call(
        body, name="loss_fwd_bwd",
        grid=(s_len // tm,),
        in_specs=[pl.BlockSpec((tm, D_MODEL), row), pl.BlockSpec((tm, D_MODEL), row),
                  pl.BlockSpec((1, D_MODEL), lambda i: (0, 0))],
        out_specs=[pl.BlockSpec((tm, D_MODEL), row), pl.BlockSpec((1, 1), lambda i: (0, 0)),
                   pl.BlockSpec((1, D_MODEL), lambda i: (0, 0))],
        out_shape=[jax.ShapeDtypeStruct((s_len, D_MODEL), F32), jax.ShapeDtypeStruct((1, 1), F32),
                   jax.ShapeDtypeStruct((1, D_MODEL), F32)],
        compiler_params=_cparams("arbitrary"),
    )(x3, tgt, g)


def _rel_bucket_matrix():
    qi = jnp.arange(SWA_BLOCK)[:, None] + SWA_BLOCK
    kj = jnp.arange(2 * SWA_BLOCK)[None, :]
    dist = jnp.maximum(qi - kj, 0)
    max_exact = REL_BUCKETS // 2
    d = jnp.maximum(dist, 1).astype(F32)
    large = max_exact + (jnp.log(d / max_exact) / np.log(REL_MAX_DIST / max_exact)
                         * (REL_BUCKETS - max_exact)).astype(jnp.int32)
    large = jnp.minimum(large, REL_BUCKETS - 1)
    return jnp.where(dist < max_exact, dist, large).astype(jnp.int32)


def _swa_bias_into(bias_ref, bkt_ref, tab_ref):
    bk = bkt_ref[...]
    for h in range(N_HEADS):
        acc = jnp.zeros(bk.shape, F32)
        for bucket in range(REL_BUCKETS):
            acc = jnp.where(bk == bucket, tab_ref[bucket, h], acc)
        bias_ref[h] = acc


def _swa_valid(n):
    shape = (SWA_BLOCK, 2 * SWA_BLOCK)
    row = lax.broadcasted_iota(jnp.int32, shape, 0)
    col = lax.broadcasted_iota(jnp.int32, shape, 1)
    dist = row + SWA_BLOCK - col
    return (dist >= 0) & (dist < SWA_BLOCK) & ((col >= SWA_BLOCK) | (n > 0))


def _swa_windows(kp_ref, kc_ref, vp_ref, vc_ref):
    return (jnp.concatenate([kp_ref[...], kc_ref[...]], axis=0), jnp.concatenate([vp_ref[...], vc_ref[...]], axis=0))


def _swa_place(h):
    return slice(h // 2 * LANES, (h // 2 + 1) * LANES), h % 2, h // SWA_GROUP


def _move_half(x, src, dst):
    moved = x if src == dst else pltpu.roll(x, HEAD_DIM, 1)
    in_dst = (lax.broadcasted_iota(jnp.int32, x.shape, 1) >= HEAD_DIM) == bool(dst)
    return jnp.where(in_dst, moved, 0.0)


def _swa_probs(qk, bias, sink, valid):
    lg = jnp.where(valid, qk * Q_SCALE + bias, NEG_BIG)
    m = jnp.maximum(jnp.max(lg, axis=-1, keepdims=True), sink)
    e = jnp.exp(lg - m)
    es = jnp.exp(sink - m)
    inv = 1.0 / (jnp.sum(e, axis=-1, keepdims=True) + es)
    return e * inv, es * inv


def _swa_specs(s_len):
    blk = SWA_BLOCK
    cur = lambda n: (n, 0)
    prev = lambda n: (jnp.maximum(n - 1, 0), 0)
    kvw = SWA_KV_HEADS * HEAD_DIM
    return [pl.BlockSpec(memory_space=pltpu.SMEM), pl.BlockSpec(memory_space=pltpu.SMEM),
            pl.BlockSpec((blk, 2 * blk), lambda n: (0, 0)),
            pl.BlockSpec((blk, N_HEADS * HEAD_DIM), cur),
            pl.BlockSpec((blk, kvw), prev), pl.BlockSpec((blk, kvw), cur),
            pl.BlockSpec((blk, kvw), prev), pl.BlockSpec((blk, kvw), cur)]


def _swa_fwd(tab, sinks, bkt, q, k, v):
    s_len = q.shape[0]
    blk = SWA_BLOCK

    def body(tab_ref, sink_ref, bkt_ref, q_ref, kp_ref, kc_ref, vp_ref, vc_ref, o_ref, bias_ref):
        n = pl.program_id(0)

        @pl.when(n == 0)
        def _():
            _swa_bias_into(bias_ref, bkt_ref, tab_ref)

        valid = _swa_valid(n)
        kk, vv = _swa_windows(kp_ref, kc_ref, vp_ref, vc_ref)
        st = {}

        def s_logits(h):
            tile, mine, kv = _swa_place(h)
            st[h, "lg"] = _dot_nt(_move_half(q_ref[:, tile].astype(F32), mine, kv).astype(BF16), kk)

        def s_probs(h):
            st[h, "p"] = _swa_probs(st.pop((h, "lg")), bias_ref[h], sink_ref[0, h], valid)[0].astype(BF16)

        def s_values(h):
            tile, mine, kv = _swa_place(h)
            part = _move_half(_dot(st.pop((h, "p")), vv), kv, mine)
            if mine == 0:
                st[h + 1, "o"] = part
            else:
                o_ref[:, tile] = (st.pop((h, "o")) + part).astype(BF16)

        _emit_skewed((list(range(N_HEADS)), [s_logits, s_probs, s_values]))

    return pl.pallas_call(
        body, name="swa_fwd",
        grid=(s_len // blk,),
        in_specs=_swa_specs(s_len),
        out_specs=pl.BlockSpec((blk, N_HEADS * HEAD_DIM), lambda n: (n, 0)),
        out_shape=jax.ShapeDtypeStruct((s_len, N_HEADS * HEAD_DIM), BF16),
        scratch_shapes=[pltpu.VMEM((N_HEADS, blk, 2 * blk), F32)],
        compiler_params=_cparams("arbitrary"),
    )(tab, sinks, bkt, q, k, k, v, v)


def _swa_bwd(tab, sinks, bkt, q, k, v, do, comm=None):
    s_len = q.shape[0]
    blk = SWA_BLOCK
    nb = s_len // blk
    kvw = SWA_KV_HEADS * HEAD_DIM

    def body(tab_ref, sink_ref, bkt_ref, q_ref, kp_ref, kc_ref, vp_ref, vc_ref, do_ref,
             dq_ref, dk_ref, dv_ref, dtab_ref, dsink_ref, bias_ref, dbias_ref):
        n = pl.program_id(0)

        @pl.when(n == 0)
        def _():
            _swa_bias_into(bias_ref, bkt_ref, tab_ref)
            dbias_ref[...] = jnp.zeros_like(dbias_ref)
            dk_ref[...] = jnp.zeros_like(dk_ref)
            dv_ref[...] = jnp.zeros_like(dv_ref)
            dsink_ref[...] = jnp.zeros_like(dsink_ref)
            dtab_ref[...] = jnp.zeros_like(dtab_ref)

        valid = _swa_valid(n)
        cur_rows = pl.ds(pl.multiple_of(n * blk, blk), blk)
        prev_rows = pl.ds(pl.multiple_of(jnp.maximum(n - 1, 0) * blk, blk), blk)
        kk, vv = _swa_windows(kp_ref, kc_ref, vp_ref, vc_ref)
        st = {}

        def s_logits(h):
            tile, mine, kv = _swa_place(h)
            st[h, "q"] = _move_half(q_ref[:, tile].astype(F32), mine, kv).astype(BF16)
            st[h, "do"] = _move_half(do_ref[:, tile].astype(F32), mine, kv).astype(BF16)
            st[h, "lg"] = _dot_nt(st[h, "q"], kk)
            st[h, "dp"] = _dot_nt(st[h, "do"], vv)

        def s_probs(h):
            p, ps = _swa_probs(st.pop((h, "lg")), bias_ref[h], sink_ref[0, h], valid)
            dp = st.pop((h, "dp"))
            delta = jnp.sum(p * dp, axis=-1, keepdims=True)
            dl = p * (dp - delta)
            dsink_ref[h:h + 1, :] += jnp.broadcast_to(-jnp.sum(ps * delta, axis=0, keepdims=True), (1, LANES))
            dbias_ref[h] += dl
            st[h, "dl"], st[h, "p"] = dl.astype(BF16), p.astype(BF16)

        def s_products(h):
            tile, mine, kv = _swa_place(h)
            dlb = st.pop((h, "dl"))
            part = _move_half(Q_SCALE * _dot(dlb, kk), kv, mine)
            if mine == 0:
                st[h + 1, "dq"] = part
            else:
                dq_ref[:, tile] = (st.pop((h, "dq")) + part).astype(BF16)
            dk_win = Q_SCALE * _dot_tn(dlb, st.pop((h, "q")))
            dv_win = _dot_tn(st.pop((h, "p")), st.pop((h, "do")))
            dk_ref[prev_rows, :] += dk_win[:blk]
            dv_ref[prev_rows, :] += dv_win[:blk]
            dk_ref[cur_rows, :] += dk_win[blk:]
            dv_ref[cur_rows, :] += dv_win[blk:]

        _emit_skewed((list(range(N_HEADS)), [s_logits, s_probs, s_products]))

        @pl.when(n == nb - 1)
        def _():
            bk = bkt_ref[...]
            lane = lax.broadcasted_iota(jnp.int32, (1, LANES), 1)
            for bucket in range(REL_BUCKETS):
                rowv = jnp.zeros((1, LANES), F32)
                for h in range(N_HEADS):
                    val = jnp.sum(jnp.where(bk == bucket, dbias_ref[h], 0.0), axis=1, keepdims=True)
                    val = jnp.sum(val, axis=0, keepdims=True)
                    rowv = jnp.where(lane == h, val, rowv)
                dtab_ref[bucket:bucket + 1, :] = rowv

    return _call(
        body, (tab, sinks, bkt, q, k, k, v, v, do), comm=comm, **_grid_ends(nb), name="swa_bwd",
        grid=(nb,),
        in_specs=_swa_specs(s_len) + [pl.BlockSpec((blk, N_HEADS * HEAD_DIM), lambda n: (n, 0))],
        out_specs=[pl.BlockSpec((blk, N_HEADS * HEAD_DIM), lambda n: (n, 0)),
                   pl.BlockSpec((s_len, kvw), lambda n: (0, 0)), pl.BlockSpec((s_len, kvw), lambda n: (0, 0)),
                   pl.BlockSpec((REL_BUCKETS, LANES), lambda n: (0, 0)), pl.BlockSpec((N_HEADS, LANES), lambda n: (0, 0))],
        out_shape=[jax.ShapeDtypeStruct((s_len, N_HEADS * HEAD_DIM), BF16),
                   jax.ShapeDtypeStruct((s_len, kvw), F32), jax.ShapeDtypeStruct((s_len, kvw), F32),
                   jax.ShapeDtypeStruct((REL_BUCKETS, LANES), F32), jax.ShapeDtypeStruct((N_HEADS, LANES), F32)],
        scratch_shapes=[pltpu.VMEM((N_HEADS, blk, 2 * blk), F32), pltpu.VMEM((N_HEADS, blk, 2 * blk), F32)],
        compiler_params=_cparams("arbitrary"),
    )


def _sb_terms(z, valid):
    zc = jnp.minimum(z, SB_LOGIT_CAP)
    lk = -jnp.log(1.0 + jnp.exp(zc))
    lsz = zc + lk
    return lsz, (lk if valid is None else jnp.where(valid, lk, 0.0))


def _bf16_parts(vals):
    parts, rest = [], vals
    for n in range(SB_SUM_PARTS):
        parts.append(rest.astype(BF16))
        if n + 1 < SB_SUM_PARTS:
            rest = rest - parts[-1].astype(F32)
    return parts[0] if len(parts) == 1 else jnp.concatenate(parts, axis=1)


def _row_sum_lanes(vals):
    return jnp.broadcast_to(jnp.sum(vals, axis=-1, keepdims=True), (vals.shape[0], LANES))


def _emit_skewed(*groups):
    for step in range(max(len(items) + len(stages) - 1 for items, stages in groups)):
        for items, stages in groups:
            for s, stage in enumerate(stages):
                if 0 <= step - s < len(items):
                    stage(items[step - s])


def _sb_items(edge):
    items = []
    for h in range(2):
        for r0 in range(0, SB_QUERIES, SB_ROWS):
            if edge is None or r0 >= (edge + 1) * SB_KEYS:
                items.append((h, r0, False))
            elif r0 + SB_ROWS - 1 > edge * SB_KEYS:
                items.append((h, r0, True))
    return items


def _sb_valid(w, edge):
    row = lax.broadcasted_iota(jnp.int32, (SB_ROWS, SB_KEYS), 0) + w[1]
    col = lax.broadcasted_iota(jnp.int32, (SB_ROWS, SB_KEYS), 1) + edge * SB_KEYS
    return col < row


def _sb_consts(tq, tk):
    low = lax.broadcasted_iota(jnp.int32, (tq, LANES), 1) < HEAD_DIM
    row = lax.broadcasted_iota(jnp.int32, (tk, tk), 0)
    col = lax.broadcasted_iota(jnp.int32, (tk, tk), 1)
    right = (row > col).astype(BF16)
    left = (row < col).astype(BF16)
    return low, jnp.concatenate([right] * SB_SUM_PARTS, axis=0), jnp.concatenate([left] * SB_SUM_PARTS, axis=0)


def _sb_fwd(q, k, v, comm=None):
    s_len = q.shape[0]
    tq, tk, tr = SB_QUERIES, SB_KEYS, SB_ROWS
    nk, ratio = s_len // tk, tq // tk
    assert nk <= LANES

    def body(q_ref, k_ref, v_ref, o_ref, car_ref, c_ref, oacc_ref, logw_ref, lksum_ref):
        i = pl.program_id(1)
        qv = q_ref[...]
        low, tri2, _ = _sb_consts(tq, tk)
        lane = lax.broadcasted_iota(jnp.int32, (tr, LANES), 1)
        zero = jnp.zeros_like(qv)
        q_heads = (jnp.where(low, qv, zero), jnp.where(low, zero, qv))
        c_ref[...] = jnp.zeros_like(c_ref)
        oacc_ref[...] = jnp.zeros_like(oacc_ref)
        car_ref[...] = jnp.full_like(car_ref, NEG_BIG)

        def front(j, edge):
            keys = k_ref[pl.ds(pl.multiple_of(j * tk, tk), tk), :]
            slot = j % 2
            st = {}

            def s_logits(w):
                st[w, "z"] = _dot_nt(q_heads[w[0]][w[1]:w[1] + tr], keys)

            def s_terms(w):
                valid = _sb_valid(w, edge) if w[2] else None
                lsz, lk = _sb_terms(st.pop((w, "z")), valid)
                st[w, "parts"] = _bf16_parts(lk)
                st[w, "lsz"] = lsz if valid is None else jnp.where(valid, lsz, NEG_BIG)
                lksum_ref[slot, w[0], w[1]:w[1] + tr, :] = _row_sum_lanes(lk)

            def s_suffix(w):
                logw_ref[slot, w[0], w[1]:w[1] + tr, :] = st.pop((w, "lsz")) + _dot(st.pop((w, "parts")), tri2)

            return _sb_items(edge), [s_logits, s_terms, s_suffix]

        def back(j, edge):
            vv = v_ref[pl.ds(pl.multiple_of(j * tk, tk), tk), :]
            slot = j % 2
            st = {}

            def s_weights(w):
                h, rs = w[0], slice(w[1], w[1] + tr)
                c = c_ref[h, rs, :]
                st[w, "a"] = jnp.exp(logw_ref[slot, h, rs, :] + jnp.tile(c, (1, tk // LANES))).astype(BF16)
                car_ref[h, rs, :] = jnp.where(lane == j, c, car_ref[h, rs, :])
                c_ref[h, rs, :] = c + lksum_ref[slot, h, rs, :]

            def s_values(w):
                oacc_ref[w[0], w[1]:w[1] + tr, :] += _dot(st.pop((w, "a")), vv)

            return _sb_items(edge), [s_weights, s_values]

        first = i * ratio
        _emit_skewed(front(first + ratio - 1, ratio - 1))
        for m in reversed(range(ratio - 1)):
            _emit_skewed(front(first + m, m), back(first + m + 1, m + 1))

        @pl.when(i == 0)
        def _():
            _emit_skewed(back(0, 0))

        def alive():
            return (jnp.max(c_ref[...]) >= SB_DEAD_CARRY).astype(jnp.int32)

        @pl.when(i > 0)
        def _():
            _emit_skewed(front(first - 1, None), back(first, 0))

            def step(state):
                pending, _ = state
                _emit_skewed(front(pending - 1, None), back(pending, None))
                return pending - 1, alive()

            pending, live = lax.while_loop(lambda s: (s[0] > 0) & (s[1] > 0), step, (first - 1, alive()))

            @pl.when(live > 0)
            def _():
                _emit_skewed(back(pending, None))

        o_ref[...] = jnp.where(low, oacc_ref[0], oacc_ref[1]).astype(BF16)

    return _call(
        body, (q, k, v), comm=comm, **_grid_ends(N_HEADS // 2, s_len // tq), name="sb_fwd",
        grid=(N_HEADS // 2, s_len // tq),
        in_specs=[pl.BlockSpec((tq, LANES), lambda p, i: (i, p)),
                  pl.BlockSpec((s_len, LANES), lambda p, i: (0, p)),
                  pl.BlockSpec((s_len, LANES), lambda p, i: (0, p))],
        out_specs=[pl.BlockSpec((tq, LANES), lambda p, i: (i, p)), pl.BlockSpec((2, tq, LANES), lambda p, i: (p, i, 0))],
        out_shape=[jax.ShapeDtypeStruct((s_len, N_HEADS * HEAD_DIM), BF16),
                   jax.ShapeDtypeStruct((N_HEADS, s_len, LANES), F32)],
        scratch_shapes=[pltpu.VMEM((2, tq, LANES), F32), pltpu.VMEM((2, tq, LANES), F32),
                        pltpu.VMEM((2, 2, tq, tk), F32), pltpu.VMEM((2, 2, tq, LANES), F32)],
        compiler_params=_cparams("arbitrary", "arbitrary"),
    )


def _sb_bwd(q, k, v, do, cars):
    s_len = q.shape[0]
    tq, tk, tr = SB_QUERIES, SB_KEYS, SB_ROWS
    nk, ratio = s_len // tk, tq // tk

    def body(q_ref, k_ref, v_ref, do_ref, car_ref, dq_ref, dk_ref, dv_ref,
             gleft_ref, dqacc_ref, dkacc_ref, dvacc_ref, logw_ref, lsz_ref, da_ref, a_ref, dz_ref):
        i = pl.program_id(1)

        @pl.when(i == 0)
        def _():
            dkacc_ref[...] = jnp.zeros_like(dkacc_ref)
            dvacc_ref[...] = jnp.zeros_like(dvacc_ref)

        qv = q_ref[...]
        dov = do_ref[...]
        low, tri_right2, tri_left2 = _sb_consts(tq, tk)
        lane = lax.broadcasted_iota(jnp.int32, (tr, LANES), 1)
        zero = jnp.zeros_like(qv)
        q_heads = (jnp.where(low, qv, zero), jnp.where(low, zero, qv))
        do_heads = (jnp.where(low, dov, zero), jnp.where(low, zero, dov))
        q_t = qv.astype(F32).T.astype(BF16)
        do_t = dov.astype(F32).T.astype(BF16)
        gleft_ref[...] = jnp.zeros_like(gleft_ref)
        dqacc_ref[...] = jnp.zeros_like(dqacc_ref)

        def front(j, edge):
            key_rows = pl.ds(pl.multiple_of(j * tk, tk), tk)
            keys, values = k_ref[key_rows, :], v_ref[key_rows, :]
            slot = j % 2
            st = {}

            def s_logits(w):
                h, rs = w[0], slice(w[1], w[1] + tr)
                st[w, "z"] = _dot_nt(q_heads[h][rs], keys)
                da_ref[slot, h, rs, :] = _dot_nt(do_heads[h][rs], values)

            def s_terms(w):
                h, rs = w[0], slice(w[1], w[1] + tr)
                valid = _sb_valid(w, edge) if w[2] else None
                lsz, lk = _sb_terms(st.pop((w, "z")), valid)
                st[w, "parts"] = _bf16_parts(lk)
                lsz = lsz if valid is None else jnp.where(valid, lsz, NEG_BIG)
                lsz_ref[slot, h, rs, :] = lsz
                st[w, "lszc"] = lsz + jnp.sum(jnp.where(lane == j, car_ref[h, rs, :], 0.0), axis=-1, keepdims=True)

            def s_suffix(w):
                logw_ref[slot, w[0], w[1]:w[1] + tr, :] = st.pop((w, "lszc")) + _dot(st.pop((w, "parts")), tri_right2)

            return _sb_items(edge), [s_logits, s_terms, s_suffix]

        def back(j, edge):
            kv = k_ref[pl.ds(pl.multiple_of(j * tk, tk), tk), :]
            slot = j % 2
            st = {}

            items = _sb_items(edge)
            head_rows = [[r0 for hh, r0, _ in items if hh == h] for h in range(2)]

            def s_weights(w):
                h, rs = w[0], slice(w[1], w[1] + tr)
                a = jnp.exp(logw_ref[slot, h, rs, :])
                g = a * da_ref[slot, h, rs, :]
                a_ref[h, rs, :] = a.astype(BF16)
                st[w, "g"], st[w, "parts"] = g, _bf16_parts(g)

            def s_prefix(w):
                st[w, "gs"] = _dot(st.pop((w, "parts")), tri_left2)

            def s_dz(w):
                h, rs = w[0], slice(w[1], w[1] + tr)
                g = st.pop((w, "g"))
                gleft = gleft_ref[h, rs, :]
                gsum = st.pop((w, "gs")) + jnp.tile(gleft, (1, tk // LANES))
                dz = (g - jnp.exp(lsz_ref[slot, h, rs, :]) * (g + gsum)).astype(BF16)
                st[w, "dz"] = dz
                dz_ref[h, rs, :] = dz
                gleft_ref[h, rs, :] = gleft + _row_sum_lanes(g)

            def s_products(w):
                h, rs = w[0], slice(w[1], w[1] + tr)
                dqacc_ref[h, rs, :] += _dot(st.pop((w, "dz")), kv)
                if w[1] == head_rows[h][-1]:
                    feat = slice(h * HEAD_DIM, (h + 1) * HEAD_DIM)
                    hr = slice(head_rows[h][0], tq)
                    dkacc_ref[j, feat, :] += _dot(q_t[feat, hr], dz_ref[h, hr, :])
                    dvacc_ref[j, feat, :] += _dot(do_t[feat, hr], a_ref[h, hr, :])

            return items, [s_weights, s_prefix, s_dz, s_products]

        first = i * ratio
        tile_max = jnp.max(jnp.maximum(car_ref[0], car_ref[1]), axis=0, keepdims=True)
        start = jnp.clip(first + ratio - jnp.sum(jnp.where(tile_max >= SB_DEAD_CARRY, 1, 0)), 0, first)

        @pl.when(start == first)
        def _():
            _emit_skewed(front(first, 0))

        @pl.when(start < first)
        def _():
            _emit_skewed(front(start, None))

            def step(jj, carry):
                _emit_skewed(front(jj, None), back(jj - 1, None))
                return carry

            lax.fori_loop(start + 1, first, step, 0)
            _emit_skewed(front(first, 0), back(first - 1, None))

        for m in range(1, ratio):
            _emit_skewed(front(first + m, m), back(first + m - 1, m - 1))
        _emit_skewed(back(first + ratio - 1, ratio - 1))
        dq_ref[...] = (Q_SCALE * jnp.where(low, dqacc_ref[0], dqacc_ref[1])).astype(BF16)

        @pl.when(i == s_len // tq - 1)
        def _():
            for j in range(nk):
                dk_ref[j * tk:(j + 1) * tk, :] = dkacc_ref[j].T.astype(BF16)
                dv_ref[j * tk:(j + 1) * tk, :] = dvacc_ref[j].T.astype(BF16)

    qblk = pl.BlockSpec((tq, LANES), lambda p, i: (i, p))
    col_full = pl.BlockSpec((s_len, LANES), lambda p, i: (0, p))
    return pl.pallas_call(
        body, name="sb_bwd",
        grid=(N_HEADS // 2, s_len // tq),
        in_specs=[qblk, col_full, col_full, qblk, pl.BlockSpec((2, tq, LANES), lambda p, i: (p, i, 0))],
        out_specs=[qblk, col_full, col_full],
        out_shape=[jax.ShapeDtypeStruct((s_len, N_HEADS * HEAD_DIM), BF16)] * 3,
        scratch_shapes=[pltpu.VMEM((2, tq, LANES), F32), pltpu.VMEM((2, tq, LANES), F32),
                        pltpu.VMEM((nk, LANES, tk), F32), pltpu.VMEM((nk, LANES, tk), F32)]
        + [pltpu.VMEM((2, 2, tq, tk), F32)] * 3 + [pltpu.VMEM((2, tq, tk), BF16)] * 2,
        compiler_params=_cparams("parallel", "arbitrary"),
    )(q, k, v, do, cars)


def _local_step(xs, tgt, gains, sinks, rel_bias, weights_of, ship):
    g1, gmix, g2, gfin = gains
    bkt = _rel_bucket_matrix()
    grads = {}

    def carried(outs, comm, count):
        return outs[:count], (list(outs[count:]) if comm is not None else None)

    wts = dict(weights_of(0, None))
    comm = ship("weights", 1)
    (x1, h1, a1, b1, u1), landed = carried(
        _ffn_fwd(xs, g1, wts["ffn1_w1t"], wts["ffn1_w3t"], wts["ffn1_w2"], "1", comm), comm, 5)
    wts.update(weights_of(1, landed))
    hm, qa, ka, va, qb, kb, vb, ga, gb = _proj_fwd(x1, gmix, wts["w_int"])
    oa = _swa_fwd(rel_bias, sinks, bkt, qa, ka, va)
    comm = ship("weights", 2)
    (ob, cars), landed = carried(_sb_fwd(qb, kb, vb, comm), comm, 2)
    wts.update(weights_of(2, landed))
    x2, mg = _merge_fwd(x1, oa, ob, ga, gb, wts["w_swa"], wts["w_sb"], wts["w_out"])
    x3, h3, a3, b3, u3 = _ffn_fwd(x2, g2, wts["ffn2_w1t"], wts["ffn2_w3t"], wts["ffn2_w2"], "2")
    dx3, loss, dgfin = _loss_fwd_bwd(x3, tgt, gfin)

    def grad_chain(items):
        prev = None
        for name, lhs, rhs in items:
            comm = None if prev is None else ship("grads", (prev[0],), prev[1])
            res = _matmul_tn(lhs, rhs, name, comm)
            if prev is not None:
                grads[(prev[0],)] = prev[1] if comm is None else res[1]
            prev = (name, {_GRAD_KEY[name]: res if comm is None else res[0]})
        return prev

    dx2, dg2, da3, db3, dx3b = _ffn_bwd(dx3, x2, g2, a3, b3, wts["ffn2_w1t"], wts["ffn2_w3t"], wts["ffn2_w2"], "2")
    last = grad_chain((("ffn2_w1", da3, h3), ("ffn2_w3", db3, h3), ("ffn2_w2", u3, dx3b)))
    comm = ship("grads", (last[0],), last[1])
    (doa, dob, dga, dgb, dpa, dpb, dx2b), landed = carried(
        _merge_bwd(dx2, oa, ob, ga, gb, wts["w_swa"], wts["w_sb"], wts["w_out"], comm), comm, 7)
    grads[(last[0],)] = last[1] if comm is None else landed[0]
    dqa, dka, dva, dtab, dsink = _swa_bwd(rel_bias, sinks, bkt, qa, ka, va, doa)

    big = {"w_out": _matmul_tn(mg, dx2b, "w_out"), "w_swa": _matmul_tn(oa, dpa, "w_swa"),
           "w_sb": _matmul_tn(ob, dpb, "w_sb")}
    dqb, dkb, dvb = _sb_bwd(qb, kb, vb, dob, cars)
    dpieces = (dqa, dka.astype(BF16), dva.astype(BF16), dqb, dkb, dvb, dga, dgb)
    big["w_int"] = _matmul_tn_stacked(dpieces, hm, "w_in")
    dx1, dgmix = _proj_bwd(dpieces, dx2, x1, gmix, wts["w_int"])

    comm = ship("grads", GROUPS[1], big)
    (dx0, dg1, da1, db1, dx1b), landed = carried(
        _ffn_bwd(dx1, xs, g1, a1, b1, wts["ffn1_w1t"], wts["ffn1_w3t"], wts["ffn1_w2"], "1", comm), comm, 5)
    grads[GROUPS[1]] = big if comm is None else landed[0]

    last = grad_chain((("ffn1_w1", da1, h1), ("ffn1_w3", db1, h1), ("ffn1_w2", u1, dx1b)))
    grads[(last[0],)] = last[1]

    small = {"gains": (dg1, dgmix, dg2, dgfin), "sinks": dsink[:, 0], "rel_bias": dtab[:, :N_HEADS]}
    return loss, dx0, small, grads


def _my_place():
    return lax.axis_index("x"), lax.axis_index("y"), lax.axis_index("c")


def _flip(v, bit):
    return 1 - v if bit else v


_RELATIONS = tuple((k >> 2 & 1, k >> 1 & 1, k & 1) for k in range(1, N_DEV))


def _gather_weights(blocks, tag):
    count = len(blocks)

    def body(*refs):
        x_refs, out_refs = refs[:count], refs[count:2 * count]
        send_sems, recv_sems, local_sems = refs[2 * count:]
        x, y, c = _my_place()
        me, sibling = (x, y, c), (x, y, 1 - c)
        chips = [(1 - x, y), (x, 1 - y), (1 - x, 1 - y)]

        def rows(s, px, py, pc):
            return out_refs[s].at[4 * px + 2 * py + pc]

        def copy(s, k, block, to, src=None):
            return pltpu.make_async_remote_copy(
                src_ref=rows(s, *block) if src is None else src, dst_ref=rows(s, *block),
                send_sem=send_sems.at[s, k], recv_sem=recv_sems.at[s, k],
                device_id=to, device_id_type=pl.DeviceIdType.MESH)

        mine = [pltpu.make_async_copy(x_refs[s], rows(s, *me), local_sems.at[s]) for s in range(count)]
        first, passed = [], []
        for s in range(count):
            mine[s].start()
            first.append(copy(s, 0, me, sibling, src=x_refs[s]))
            first += [copy(s, 1 + j, me, (*chip, c), src=x_refs[s]) for j, chip in enumerate(chips)]
        for cp in first:
            cp.start()
        for s in range(count):
            for j, chip in enumerate(chips):
                copy(s, 1 + j, (*chip, c), me).wait_recv()
                passed.append(copy(s, 4 + j, (*chip, c), sibling))
                passed[-1].start()
        for s in range(count):
            copy(s, 0, sibling, me).wait_recv()
            for j, chip in enumerate(chips):
                copy(s, 4 + j, (*chip, 1 - c), me).wait_recv()
        for cp in first + passed:
            cp.wait_send()
        for cp in mine:
            cp.wait()

    anywhere = pl.BlockSpec(memory_space=pl.ANY)
    return pl.pallas_call(
        body, name=f"gather_weights_{tag}",
        out_shape=[jax.ShapeDtypeStruct((N_DEV,) + b.shape, b.dtype) for b in blocks],
        in_specs=[anywhere] * count, out_specs=[anywhere] * count,
        scratch_shapes=[pltpu.SemaphoreType.DMA((count, N_DEV - 1)), pltpu.SemaphoreType.DMA((count, N_DEV - 1)),
                        pltpu.SemaphoreType.DMA((count,))],
    )(*blocks)


def _exchange_grads(gp, tag):
    def body(g_ref, out_ref, send_sems, recv_sems, local_sem):
        x, y, c = _my_place()
        me = 4 * x + 2 * y + c
        mine = pltpu.make_async_copy(g_ref.at[me], out_ref.at[me], local_sem)
        mine.start()
        copies = []
        for k, (fx, fy, fc) in enumerate(_RELATIONS):
            px, py, pc = _flip(x, fx), _flip(y, fy), _flip(c, fc)
            peer = 4 * px + 2 * py + pc
            copies.append((
                pltpu.make_async_remote_copy(
                    src_ref=g_ref.at[peer], dst_ref=out_ref.at[me], send_sem=send_sems.at[k], recv_sem=recv_sems.at[k],
                    device_id=(px, py, pc), device_id_type=pl.DeviceIdType.MESH),
                pltpu.make_async_remote_copy(
                    src_ref=g_ref.at[peer], dst_ref=out_ref.at[peer], send_sem=send_sems.at[k], recv_sem=recv_sems.at[k],
                    device_id=(px, py, pc), device_id_type=pl.DeviceIdType.MESH)))
        for out_cp, _ in copies:
            out_cp.start()
        for _, in_cp in copies:
            in_cp.wait_recv()
        for out_cp, _ in copies:
            out_cp.wait_send()
        mine.wait()

    return pl.pallas_call(
        body, name=f"exchange_grads_{tag}",
        out_shape=jax.ShapeDtypeStruct(gp.shape, gp.dtype),
        in_specs=[pl.BlockSpec(memory_space=pl.ANY)],
        out_specs=pl.BlockSpec(memory_space=pl.ANY),
        scratch_shapes=[pltpu.SemaphoreType.DMA((7,)), pltpu.SemaphoreType.DMA((7,)), pltpu.SemaphoreType.DMA(())],
    )(gp)


def _peers():
    x, y, c = _my_place()
    out = []
    for k, (fx, fy, fc) in enumerate(_RELATIONS):
        px, py, pc = _flip(x, fx), _flip(y, fy), _flip(c, fc)
        out.append((k, (px, py, pc), 4 * px + 2 * py + pc))
    return out, 4 * x + 2 * y + c


def _grid_ends(*grid):
    def first():
        return functools.reduce(lambda a, b: a & b, [pl.program_id(d) == 0 for d in range(len(grid))])

    def last():
        return functools.reduce(lambda a, b: a & b, [pl.program_id(d) == n - 1 for d, n in enumerate(grid)])

    return {"first": first, "last": last}


def _call(body, operands, *, comm=None, first=None, last=None, **kw):
    if comm is None:
        return pl.pallas_call(body, **kw)(*operands)
    in_specs, out_specs, out_shape = list(kw.pop("in_specs")), list(kw.pop("out_specs")), list(kw.pop("out_shape"))
    scratch = list(kw.pop("scratch_shapes", ()))
    n_in, n_out, n_scr, n_src = len(in_specs), len(out_specs), len(scratch), len(comm)

    def wrapped(*refs):
        ins, src_refs = refs[:n_in], refs[n_in:n_in + n_src]
        outs = refs[n_in + n_src:n_in + n_src + n_out]
        land_refs = refs[n_in + n_src + n_out:n_in + 2 * n_src + n_out]
        scr = refs[n_in + 2 * n_src + n_out:n_in + 2 * n_src + n_out + n_scr]
        send_sems, recv_sems, local_sems = refs[n_in + 2 * n_src + n_out + n_scr:]
        peers, me = _peers()
        mine, going, coming = [], [], []
        for s, (_, per_peer) in enumerate(comm):
            src_ref, land_ref = src_refs[s], land_refs[s]
            mine.append(pltpu.make_async_copy(src_ref.at[me] if per_peer else src_ref, land_ref.at[me], local_sems.at[s]))
            for k, where, slab in peers:
                piece = src_ref.at[slab] if per_peer else src_ref
                going.append(pltpu.make_async_remote_copy(
                    src_ref=piece, dst_ref=land_ref.at[me], send_sem=send_sems.at[s, k], recv_sem=recv_sems.at[s, k],
                    device_id=where, device_id_type=pl.DeviceIdType.MESH))
                coming.append(pltpu.make_async_remote_copy(
                    src_ref=piece, dst_ref=land_ref.at[slab], send_sem=send_sems.at[s, k], recv_sem=recv_sems.at[s, k],
                    device_id=where, device_id_type=pl.DeviceIdType.MESH))

        @pl.when(first())
        def _():
            for cp in mine + going:
                cp.start()

        body(*ins, *outs, *scr)

        @pl.when(last())
        def _():
            for cp in coming:
                cp.wait_recv()
            for cp in going:
                cp.wait_send()
            for cp in mine:
                cp.wait()

    anywhere = pl.BlockSpec(memory_space=pl.ANY)
    lands = [jax.ShapeDtypeStruct(src.shape if per_peer else (N_DEV,) + src.shape, src.dtype) for src, per_peer in comm]
    return pl.pallas_call(
        wrapped, in_specs=in_specs + [anywhere] * n_src, out_specs=out_specs + [anywhere] * n_src,
        out_shape=out_shape + lands,
        scratch_shapes=scratch + [pltpu.SemaphoreType.DMA((n_src, N_DEV - 1)), pltpu.SemaphoreType.DMA((n_src, N_DEV - 1)),
                                  pltpu.SemaphoreType.DMA((n_src,))],
        **kw)(*operands, *[src for src, _ in comm])


def _adamw(w, g, m, v):
    m = ADAM_B1 * m + (1.0 - ADAM_B1) * g
    v = ADAM_B2 * v + (1.0 - ADAM_B2) * jnp.square(g)
    m_hat = m / (1.0 - ADAM_B1 ** ADAM_STEP)
    v_hat = v / (1.0 - ADAM_B2 ** ADAM_STEP)
    delta = -ADAM_LR * (m_hat / (jnp.sqrt(v_hat) + ADAM_EPS) + ADAM_WD * w)
    return delta, m, v


def _sum_and_adamw(parts, w, m, v, tr, tag):
    rows = w.shape[0]
    assert rows % tr == 0

    def body(p_ref, w_ref, m_ref, v_ref, g_out, d_out, m_out, v_out):
        g = p_ref[0].astype(F32)
        for d in range(1, N_DEV):
            g = g + p_ref[d].astype(F32)
        delta, mn, vn = _adamw(w_ref[...], g, m_ref[...], v_ref[...])
        g_out[...] = g
        d_out[...] = delta
        m_out[...] = mn
        v_out[...] = vn

    sp = pl.BlockSpec((tr, D_MODEL), lambda i: (i, 0))
    return pl.pallas_call(
        body, name=f"sum_and_adamw_{tag}",
        grid=(rows // tr,),
        in_specs=[pl.BlockSpec((N_DEV, tr, D_MODEL), lambda i: (0, i, 0)), sp, sp, sp],
        out_specs=[sp] * 4,
        out_shape=[jax.ShapeDtypeStruct(w.shape, F32)] * 4,
        compiler_params=_cparams("parallel"),
    )(parts, w, m, v)


def _small_allreduce_adamw(part, w, m, v):
    def body(p_ref, w_ref, m_ref, v_ref, g_out, d_out, m_out, v_out, buf, send_sems, recv_sems):
        x, y, c = _my_place()
        me = 4 * x + 2 * y + c
        buf[me] = p_ref[...]
        copies = []
        for k, (fx, fy, fc) in enumerate(_RELATIONS):
            px, py, pc = _flip(x, fx), _flip(y, fy), _flip(c, fc)
            peer = 4 * px + 2 * py + pc
            copies.append((
                pltpu.make_async_remote_copy(
                    src_ref=buf.at[me], dst_ref=buf.at[me], send_sem=send_sems.at[k], recv_sem=recv_sems.at[k],
                    device_id=(px, py, pc), device_id_type=pl.DeviceIdType.MESH),
                pltpu.make_async_remote_copy(
                    src_ref=buf.at[me], dst_ref=buf.at[peer], send_sem=send_sems.at[k], recv_sem=recv_sems.at[k],
                    device_id=(px, py, pc), device_id_type=pl.DeviceIdType.MESH)))
        for out_cp, _ in copies:
            out_cp.start()
        for _, in_cp in copies:
            in_cp.wait_recv()
        for out_cp, _ in copies:
            out_cp.wait_send()
        g = buf[0]
        for d in range(1, N_DEV):
            g = g + buf[d]
        delta, mn, vn = _adamw(w_ref[...], g, m_ref[...], v_ref[...])
        g_out[...] = g
        d_out[...] = delta
        m_out[...] = mn
        v_out[...] = vn

    vm = pl.BlockSpec(memory_space=pltpu.VMEM)
    return pl.pallas_call(
        body, name="small_allreduce_adamw",
        in_specs=[vm] * 4, out_specs=[vm] * 4,
        out_shape=[jax.ShapeDtypeStruct(w.shape, F32)] * 4,
        scratch_shapes=[pltpu.VMEM((N_DEV,) + part.shape, F32),
                        pltpu.SemaphoreType.DMA((7,)), pltpu.SemaphoreType.DMA((7,))],
    )(part, w, m, v)


_TRANSPOSED = ("ffn1_w1", "ffn1_w3", "w_in", "ffn2_w1", "ffn2_w3")
_BRANCH = ("w_branch_swa", "w_branch_sb")


def _pack_shards(t, names):
    parts = []
    for name in names:
        a = t[name][0]
        if name in _TRANSPOSED:
            a = a.T
        elif name in _BRANCH:
            a = a.reshape(64, D_MODEL)
        parts.append(a)
    return jnp.concatenate(parts, axis=0)


def _unpack_shards(p, names):
    out, lo = {}, 0
    for name in names:
        a = p[lo:lo + BIG_ROWS[BIG_NAMES.index(name)]]
        lo += a.shape[0]
        if name in _TRANSPOSED:
            a = a.T
        elif name in _BRANCH:
            a = a.reshape(512, 128)
        out[name] = a[None]
    return out


def _full_weights(zones, names):
    out = {}
    for name, a in zip(names, zones):
        if name in _BRANCH:
            a = a.reshape(N_DEV, 512, 128).transpose(1, 0, 2).reshape(512, D_MODEL)
        out[_GRAD_KEY[name]] = a.reshape(-1, D_MODEL)
    return out


_GRAD_KEY = {"ffn1_w1": "ffn1_w1t", "ffn1_w3": "ffn1_w3t", "ffn1_w2": "ffn1_w2", "w_in": "w_int",
             "w_branch_swa": "w_swa", "w_branch_sb": "w_sb", "w_out": "w_out",
             "ffn2_w1": "ffn2_w1t", "ffn2_w3": "ffn2_w3t", "ffn2_w2": "ffn2_w2"}


def _pack_full_grads(big, names):
    parts = []
    for name in names:
        a = big[_GRAD_KEY[name]]
        if name in _BRANCH:
            a = a.reshape(512, N_DEV, 128).transpose(1, 0, 2)
        parts.append(a.reshape(N_DEV, BIG_ROWS[BIG_NAMES.index(name)], D_MODEL).astype(BF16))
    return jnp.concatenate(parts, axis=1)


_SMALL_NAMES = ("norm_ffn1", "norm_mix", "norm_ffn2", "norm_final", "swa_sinks", "rel_bias")


def _pack_small(vals):
    rows = []
    for a in vals:
        a = a.reshape(-1)
        rows.append(jnp.pad(a, (0, D_MODEL - a.shape[0])))
    rows += [jnp.zeros((D_MODEL,), F32)] * (SMALL_ROWS - len(rows))
    return jnp.stack(rows)


def _unpack_small(p):
    return {"norm_ffn1": p[0:1], "norm_mix": p[1:2], "norm_ffn2": p[2:3], "norm_final": p[3],
            "swa_sinks": p[4:5, :N_HEADS], "rel_bias": p[5, :REL_BUCKETS * N_HEADS].reshape(REL_BUCKETS, N_HEADS)}


ALL_NAMES = ("norm_ffn1", "ffn1_w1", "ffn1_w3", "ffn1_w2", "norm_mix", "w_in", "swa_sinks", "rel_bias",
             "w_branch_swa", "w_branch_sb", "w_out", "norm_ffn2", "ffn2_w1", "ffn2_w3", "ffn2_w2", "norm_final")


def kernel(x, norm_ffn1, ffn1_w1, ffn1_w3, ffn1_w2, norm_mix, w_in, swa_sinks, rel_bias, w_branch_swa, w_branch_sb, w_out, norm_ffn2, ffn2_w1, ffn2_w3, ffn2_w2, norm_final, loss_target, m_norm_ffn1, m_ffn1_w1, m_ffn1_w3, m_ffn1_w2, m_norm_mix, m_w_in, m_swa_sinks, m_rel_bias, m_w_branch_swa, m_w_branch_sb, m_w_out, m_norm_ffn2, m_ffn2_w1, m_ffn2_w3, m_ffn2_w2, m_norm_final, v_norm_ffn1, v_ffn1_w1, v_ffn1_w3, v_ffn1_w2, v_norm_mix, v_w_in, v_swa_sinks, v_rel_bias, v_w_branch_swa, v_w_branch_sb, v_w_out, v_norm_ffn2, v_ffn2_w1, v_ffn2_w3, v_ffn2_w2, v_norm_final):
    w = dict(zip(ALL_NAMES, (norm_ffn1, ffn1_w1, ffn1_w3, ffn1_w2, norm_mix, w_in, swa_sinks, rel_bias,
                             w_branch_swa, w_branch_sb, w_out, norm_ffn2, ffn2_w1, ffn2_w3, ffn2_w2, norm_final)))
    m = dict(zip(ALL_NAMES, (m_norm_ffn1, m_ffn1_w1, m_ffn1_w3, m_ffn1_w2, m_norm_mix, m_w_in, m_swa_sinks, m_rel_bias,
                             m_w_branch_swa, m_w_branch_sb, m_w_out, m_norm_ffn2, m_ffn2_w1, m_ffn2_w3, m_ffn2_w2,
                             m_norm_final)))
    v = dict(zip(ALL_NAMES, (v_norm_ffn1, v_ffn1_w1, v_ffn1_w3, v_ffn1_w2, v_norm_mix, v_w_in, v_swa_sinks, v_rel_bias,
                             v_w_branch_swa, v_w_branch_sb, v_w_out, v_norm_ffn2, v_ffn2_w1, v_ffn2_w3, v_ffn2_w2,
                             v_norm_final)))

    def my_blocks(group):
        return [_pack_shards(w, (name,)).astype(BF16) for name in GROUPS[group]]

    gathered0 = _gather_weights(my_blocks(0), "group0")

    def weights_of(group, landed):
        return _full_weights(gathered0 if group == 0 else landed, GROUPS[group])

    def ship(kind, which, grads=None):
        if kind == "weights":
            return [(block, False) for block in my_blocks(which)]
        return [(_pack_full_grads(grads, which), True)]

    gains = (norm_ffn1, norm_mix, norm_ffn2, norm_final.reshape(1, D_MODEL))
    loss, dx, small, parts = _local_step(x[0], loss_target[0], gains, swa_sinks, rel_bias, weights_of, ship)

    big_outs = [{}, {}, {}, {}]
    for names, tile in zip(SUM_GROUPS, SUM_TILE):
        landed = parts[names]
        if isinstance(landed, dict):
            landed = _exchange_grads(_pack_full_grads(landed, names), names[0])
        res = _sum_and_adamw(landed, _pack_shards(w, names), _pack_shards(m, names), _pack_shards(v, names),
                             tile, names[0])
        for acc, packed in zip(big_outs, res):
            acc.update(_unpack_shards(packed, names))
    g_big, d_big, m_big, v_big = big_outs

    small_part = _pack_small(small["gains"] + (small["sinks"], small["rel_bias"], loss))
    zero = jnp.zeros((1,), F32)
    small_res = _small_allreduce_adamw(
        small_part, _pack_small([w[n] for n in _SMALL_NAMES] + [zero]), _pack_small([m[n] for n in _SMALL_NAMES] + [zero]),
        _pack_small([v[n] for n in _SMALL_NAMES] + [zero]))
    g_sm, d_sm, m_sm, v_sm = (_unpack_small(p) for p in small_res)

    outs = [small_res[0][len(_SMALL_NAMES), 0], dx[None]]
    for big_d, small_d in ((g_big, g_sm), (d_big, d_sm), (m_big, m_sm), (v_big, v_sm)):
        merged = {**big_d, **small_d}
        outs += [merged[n] for n in ALL_NAMES]
    return tuple(outs)
```

```python
import functools

import jax
import jax.numpy as jnp
import numpy as np
from jax import lax
from jax.experimental import pallas as pl
from jax.experimental.pallas import tpu as pltpu

F32 = jnp.float32
BF16 = jnp.bfloat16

D_MODEL = 1024
D_FF = 2816
HEAD_DIM = 64
N_HEADS = 8
SWA_KV_HEADS = 2
SWA_GROUP = 4
SWA_BLOCK = 128
REL_BUCKETS = 32
REL_MAX_DIST = 128
RMS_EPS = 1e-6
NEG_BIG = -1e30
Q_SCALE = HEAD_DIM ** -0.5
LANES = 128

N_DEV = 8

ADAM_LR = 0.001
ADAM_B1 = 0.9
ADAM_B2 = 0.999
ADAM_EPS = 1e-08
ADAM_WD = 0.01
ADAM_STEP = 10

IN_SIZES = (512, 128, 128, 512, 512, 512, 1024, 1024)
IN_OFFS = tuple(int(v) for v in np.cumsum((0,) + IN_SIZES))
IN_W = IN_OFFS[-1]

BIG_NAMES = ("ffn1_w1", "ffn1_w3", "ffn1_w2", "w_in", "w_branch_swa", "w_branch_sb", "w_out",
             "ffn2_w1", "ffn2_w3", "ffn2_w2")
BIG_ROWS = (352, 352, 352, 544, 64, 64, 128, 352, 352, 352)
SMALL_ROWS = 8
GROUPS = (BIG_NAMES[0:3], BIG_NAMES[3:7], BIG_NAMES[7:10])
SUM_GROUPS = tuple((n,) for n in GROUPS[0]) + (GROUPS[1],) + tuple((n,) for n in GROUPS[2])
SUM_TILE = (176, 176, 176, 160, 176, 176, 176)

VMEM_LIMIT = 56 * 1024 * 1024
FFN_PIECES = 2
SB_QUERIES = 512
SB_KEYS = 256
SB_ROWS = 128
SB_SUM_PARTS = 1
SB_LOGIT_CAP = 80.0
SB_DEAD_CARRY = -110.0


def _dot(a, b):
    return jnp.dot(a, b, preferred_element_type=F32)


def _dot_nt(a, b):
    return lax.dot_general(a, b, (((1,), (1,)), ((), ())), preferred_element_type=F32)


def _dot_tn(a, b):
    return lax.dot_general(a, b, (((0,), (0,)), ((), ())), preferred_element_type=F32)


def _cparams(*sem):
    return pltpu.CompilerParams(dimension_semantics=sem, vmem_limit_bytes=VMEM_LIMIT)


def _rms_rstd(xv):
    return lax.rsqrt(jnp.mean(xv * xv, axis=-1, keepdims=True) + RMS_EPS)


def _rms_bwd(dh, xv, r, g):
    xhat = xv * r
    dg = jnp.sum(dh * xhat, axis=0, keepdims=True)
    dxn = dh * g
    dx = r * (dxn - xhat * jnp.mean(dxn * xhat, axis=-1, keepdims=True))
    return dx, dg


def _ffn_fwd(x, g, w1t, w3t, w2, tag, comm=None):
    s_len = x.shape[0]
    tm, tf = min(1024, s_len), 256
    nf = D_FF // tf

    def body(x_ref, g_ref, w1_ref, w3_ref, w2_ref, xo_ref, h_ref, a_ref, b_ref, u_ref, acc_ref, hs_ref):
        j = pl.program_id(1)

        @pl.when(j == 0)
        def _():
            xv = x_ref[...]
            h = (xv * _rms_rstd(xv) * g_ref[...]).astype(BF16)
            hs_ref[...] = h
            h_ref[...] = h
            acc_ref[...] = jnp.zeros_like(acc_ref)

        st = {}

        def s_up(rs):
            h = hs_ref[rs, :]
            st[rs.start, "ab"] = (_dot_nt(h, w1_ref[...]), _dot_nt(h, w3_ref[...]))

        def s_act(rs):
            a, b = st.pop((rs.start, "ab"))
            a_ref[rs, :] = a.astype(BF16)
            b_ref[rs, :] = b.astype(BF16)
            uh = (0.5 * (a * jax.nn.sigmoid(a) * b)).astype(BF16)
            u_ref[rs, :] = uh
            st[rs.start, "u"] = uh

        def s_down(rs):
            acc_ref[rs, :] += _dot(st.pop((rs.start, "u")), w2_ref[...])

        _emit_skewed(([slice(r, r + tm // FFN_PIECES) for r in range(0, tm, tm // FFN_PIECES)], [s_up, s_act, s_down]))

        @pl.when(j == nf - 1)
        def _():
            xo_ref[...] = x_ref[...] + acc_ref[...]

    row = lambda i, j: (i, 0)
    return _call(
        body, (x, g, w1t, w3t, w2), comm=comm, **_grid_ends(s_len // tm, nf), name=f"ffn_fwd_{tag}",
        grid=(s_len // tm, nf),
        in_specs=[pl.BlockSpec((tm, D_MODEL), row), pl.BlockSpec((1, D_MODEL), lambda i, j: (0, 0)),
                  pl.BlockSpec((tf, D_MODEL), lambda i, j: (j, 0)), pl.BlockSpec((tf, D_MODEL), lambda i, j: (j, 0)),
                  pl.BlockSpec((tf, D_MODEL), lambda i, j: (j, 0))],
        out_specs=[pl.BlockSpec((tm, D_MODEL), row), pl.BlockSpec((tm, D_MODEL), row),
                   pl.BlockSpec((tm, tf), lambda i, j: (i, j)), pl.BlockSpec((tm, tf), lambda i, j: (i, j)),
                   pl.BlockSpec((tm, tf), lambda i, j: (i, j))],
        out_shape=[jax.ShapeDtypeStruct((s_len, D_MODEL), F32), jax.ShapeDtypeStruct((s_len, D_MODEL), BF16),
                   jax.ShapeDtypeStruct((s_len, D_FF), BF16), jax.ShapeDtypeStruct((s_len, D_FF), BF16),
                   jax.ShapeDtypeStruct((s_len, D_FF), BF16)],
        scratch_shapes=[pltpu.VMEM((tm, D_MODEL), F32), pltpu.VMEM((tm, D_MODEL), BF16)],
        compiler_params=_cparams("arbitrary", "arbitrary"),
    )


def _ffn_bwd(dy, x, g, a, b, w1t, w3t, w2, tag, comm=None):
    s_len = x.shape[0]
    tm, tf = min(1024, s_len), 256
    nf = D_FF // tf

    def body(dy_ref, x_ref, g_ref, a_ref, b_ref, w1_ref, w3_ref, w2_ref,
             dx_ref, dg_ref, da_ref, db_ref, dyb_ref, acc_ref, dys_ref):
        i, j = pl.program_id(0), pl.program_id(1)

        @pl.when(j == 0)
        def _():
            dyb = dy_ref[...].astype(BF16)
            dys_ref[...] = 0.5 * dyb
            dyb_ref[...] = dyb
            acc_ref[...] = jnp.zeros_like(acc_ref)

        @pl.when((i == 0) & (j == 0))
        def _():
            dg_ref[...] = jnp.zeros_like(dg_ref)

        st = {}

        def s_du(rs):
            st[rs.start, "du"] = _dot_nt(dys_ref[rs, :], w2_ref[...])

        def s_act(rs):
            du = st.pop((rs.start, "du"))
            av = a_ref[rs, :].astype(F32)
            bv = b_ref[rs, :].astype(F32)
            sg = jax.nn.sigmoid(av)
            sil = av * sg
            da = (du * bv * (sg + sil * (1.0 - sg))).astype(BF16)
            db = (du * sil).astype(BF16)
            da_ref[rs, :] = da
            db_ref[rs, :] = db
            st[rs.start, "dab"] = (da, db)

        def s_dh(rs):
            da, db = st.pop((rs.start, "dab"))
            acc_ref[rs, :] += _dot(da, w1_ref[...]) + _dot(db, w3_ref[...])

        _emit_skewed(([slice(r, r + tm // FFN_PIECES) for r in range(0, tm, tm // FFN_PIECES)], [s_du, s_act, s_dh]))

        @pl.when(j == nf - 1)
        def _():
            xv = x_ref[...]
            dx, dg = _rms_bwd(acc_ref[...], xv, _rms_rstd(xv), g_ref[...])
            dx_ref[...] = dy_ref[...] + dx
            dg_ref[...] += dg

    row = lambda i, j: (i, 0)
    blk = lambda i, j: (i, j)
    wsp = pl.BlockSpec((tf, D_MODEL), lambda i, j: (j, 0))
    return _call(
        body, (dy, x, g, a, b, w1t, w3t, w2), comm=comm, **_grid_ends(s_len // tm, nf), name=f"ffn_bwd_{tag}",
        grid=(s_len // tm, nf),
        in_specs=[pl.BlockSpec((tm, D_MODEL), row), pl.BlockSpec((tm, D_MODEL), row),
                  pl.BlockSpec((1, D_MODEL), lambda i, j: (0, 0)),
                  pl.BlockSpec((tm, tf), blk), pl.BlockSpec((tm, tf), blk), wsp, wsp, wsp],
        out_specs=[pl.BlockSpec((tm, D_MODEL), row), pl.BlockSpec((1, D_MODEL), lambda i, j: (0, 0)),
                   pl.BlockSpec((tm, tf), blk), pl.BlockSpec((tm, tf), blk), pl.BlockSpec((tm, D_MODEL), row)],
        out_shape=[jax.ShapeDtypeStruct((s_len, D_MODEL), F32), jax.ShapeDtypeStruct((1, D_MODEL), F32),
                   jax.ShapeDtypeStruct((s_len, D_FF), BF16), jax.ShapeDtypeStruct((s_len, D_FF), BF16),
                   jax.ShapeDtypeStruct((s_len, D_MODEL), BF16)],
        scratch_shapes=[pltpu.VMEM((tm, D_MODEL), F32), pltpu.VMEM((tm, D_MODEL), BF16)],
        compiler_params=_cparams("arbitrary", "arbitrary"),
    )


def _matmul_tn(lhs, rhs, tag, comm=None):
    s_len, m = lhs.shape
    n = rhs.shape[1]
    tm = min(512, s_len)
    tj = m if m <= 1024 else 1408
    assert m % tj == 0
    last_rows = s_len // tm - 1

    def body(l_ref, r_ref, o_ref, acc_ref):
        i = pl.program_id(1)

        @pl.when(i == 0)
        def _():
            acc_ref[...] = jnp.zeros_like(acc_ref)

        acc_ref[...] += _dot_tn(l_ref[...], r_ref[...])

        @pl.when(i == last_rows)
        def _():
            o_ref[...] = acc_ref[...].astype(BF16)

    res = _call(
        body, (lhs, rhs), comm=comm, **_grid_ends(m // tj, s_len // tm), name=f"matmul_tn_{tag}",
        grid=(m // tj, s_len // tm),
        in_specs=[pl.BlockSpec((tm, tj), lambda j, i: (i, j)), pl.BlockSpec((tm, n), lambda j, i: (i, 0))],
        out_specs=[pl.BlockSpec((tj, n), lambda j, i: (j, 0))],
        out_shape=[jax.ShapeDtypeStruct((m, n), BF16)],
        scratch_shapes=[pltpu.VMEM((tj, n), F32)],
        compiler_params=_cparams("arbitrary", "arbitrary"),
    )
    return res[0] if comm is None else tuple(res)


def _matmul_tn_stacked(pieces, rhs, tag):
    s_len, n = rhs.shape
    widths = [p.shape[1] for p in pieces]
    offs = [sum(widths[:k]) for k in range(len(widths) + 1)]
    tm = min(256, s_len)
    last_rows = s_len // tm - 1

    def body(*refs):
        l_refs, r_ref, o_ref, acc_ref = refs[:len(pieces)], refs[-3], refs[-2], refs[-1]
        i = pl.program_id(0)

        @pl.when(i == 0)
        def _():
            acc_ref[...] = jnp.zeros_like(acc_ref)

        rv = r_ref[...]
        for k, l_ref in enumerate(l_refs):
            acc_ref[offs[k]:offs[k + 1], :] += _dot_tn(l_ref[...], rv)

        @pl.when(i == last_rows)
        def _():
            o_ref[...] = acc_ref[...].astype(BF16)

    row = lambda i: (i, 0)
    return pl.pallas_call(
        body, name=f"matmul_tn_{tag}",
        grid=(s_len // tm,),
        in_specs=[pl.BlockSpec((tm, w), row) for w in widths] + [pl.BlockSpec((tm, n), row)],
        out_specs=pl.BlockSpec((offs[-1], n), lambda i: (0, 0)),
        out_shape=jax.ShapeDtypeStruct((offs[-1], n), BF16),
        scratch_shapes=[pltpu.VMEM((offs[-1], n), F32)],
        compiler_params=_cparams("arbitrary"),
    )(*pieces, rhs)


def _proj_fwd(x1, g, wint):
    s_len = x1.shape[0]
    tm = min(512, s_len)
    dts = (BF16, BF16, BF16, BF16, BF16, BF16, F32, F32)

    def body(x_ref, g_ref, w_ref, h_ref, *outs):
        xv = x_ref[...]
        h = (xv * _rms_rstd(xv) * g_ref[...]).astype(BF16)
        h_ref[...] = h
        for p, o_ref in enumerate(outs):
            val = _dot_nt(h, w_ref[IN_OFFS[p]:IN_OFFS[p + 1], :])
            if p == 3:
                val = val * Q_SCALE
            o_ref[...] = val.astype(dts[p])

    row = lambda i: (i, 0)
    return pl.pallas_call(
        body, name="proj_fwd",
        grid=(s_len // tm,),
        in_specs=[pl.BlockSpec((tm, D_MODEL), row), pl.BlockSpec((1, D_MODEL), lambda i: (0, 0)),
                  pl.BlockSpec((IN_W, D_MODEL), lambda i: (0, 0))],
        out_specs=[pl.BlockSpec((tm, D_MODEL), row)] + [pl.BlockSpec((tm, w), row) for w in IN_SIZES],
        out_shape=[jax.ShapeDtypeStruct((s_len, D_MODEL), BF16)]
        + [jax.ShapeDtypeStruct((s_len, w), dt) for w, dt in zip(IN_SIZES, dts)],
        compiler_params=_cparams("parallel"),
    )(x1, g, wint)


def _proj_bwd(dpieces, dx2, x1, g, wint):
    s_len = x1.shape[0]
    tm = min(512, s_len)

    def body(*refs):
        dps = refs[:8]
        dx2_ref, x_ref, g_ref, w_ref, dx_ref, dg_ref = refs[8:]

        @pl.when(pl.program_id(0) == 0)
        def _():
            dg_ref[...] = jnp.zeros_like(dg_ref)

        dh = _dot(dps[0][...], w_ref[IN_OFFS[0]:IN_OFFS[1], :])
        for p in range(1, 8):
            dh += _dot(dps[p][...], w_ref[IN_OFFS[p]:IN_OFFS[p + 1], :])
        xv = x_ref[...]
        dx, dg = _rms_bwd(dh, xv, _rms_rstd(xv), g_ref[...])
        dx_ref[...] = dx2_ref[...] + dx
        dg_ref[...] += dg

    row = lambda i: (i, 0)
    return pl.pallas_call(
        body, name="proj_bwd",
        grid=(s_len // tm,),
        in_specs=[pl.BlockSpec((tm, w), row) for w in IN_SIZES]
        + [pl.BlockSpec((tm, D_MODEL), row), pl.BlockSpec((tm, D_MODEL), row),
           pl.BlockSpec((1, D_MODEL), lambda i: (0, 0)), pl.BlockSpec((IN_W, D_MODEL), lambda i: (0, 0))],
        out_specs=[pl.BlockSpec((tm, D_MODEL), row), pl.BlockSpec((1, D_MODEL), lambda i: (0, 0))],
        out_shape=[jax.ShapeDtypeStruct((s_len, D_MODEL), F32), jax.ShapeDtypeStruct((1, D_MODEL), F32)],
        compiler_params=_cparams("arbitrary"),
    )(*dpieces, dx2, x1, g, wint)


def _merge_fwd(x1, oa, ob, ga, gb, wswa, wsb, wout):
    s_len = x1.shape[0]
    tm = min(512, s_len)

    def body(x_ref, oa_ref, ob_ref, ga_ref, gb_ref, wa_ref, wb_ref, wo_ref, xo_ref, mg_ref):
        pa = _dot(oa_ref[...], wa_ref[...])
        pb = _dot(ob_ref[...], wb_ref[...])
        mg = (jax.nn.sigmoid(ga_ref[...]) * pa + jax.nn.sigmoid(gb_ref[...]) * pb).astype(BF16)
        mg_ref[...] = mg
        xo_ref[...] = x_ref[...] + _dot(mg, wo_ref[...])

    row = lambda i: (i, 0)
    full = lambda i: (0, 0)
    return pl.pallas_call(
        body, name="merge_fwd",
        grid=(s_len // tm,),
        in_specs=[pl.BlockSpec((tm, D_MODEL), row), pl.BlockSpec((tm, 512), row), pl.BlockSpec((tm, 512), row),
                  pl.BlockSpec((tm, D_MODEL), row), pl.BlockSpec((tm, D_MODEL), row),
                  pl.BlockSpec((512, D_MODEL), full), pl.BlockSpec((512, D_MODEL), full),
                  pl.BlockSpec((D_MODEL, D_MODEL), full)],
        out_specs=[pl.BlockSpec((tm, D_MODEL), row), pl.BlockSpec((tm, D_MODEL), row)],
        out_shape=[jax.ShapeDtypeStruct((s_len, D_MODEL), F32), jax.ShapeDtypeStruct((s_len, D_MODEL), BF16)],
        compiler_params=_cparams("parallel"),
    )(x1, oa, ob, ga, gb, wswa, wsb, wout)


def _merge_bwd(dx2, oa, ob, ga, gb, wswa, wsb, wout, comm=None):
    s_len = dx2.shape[0]
    tm = min(512, s_len)

    def body(dx_ref, oa_ref, ob_ref, ga_ref, gb_ref, wa_ref, wb_ref, wo_ref,
             doa_ref, dob_ref, dga_ref, dgb_ref, dpa_ref, dpb_ref, dxb_ref):
        dxb = dx_ref[...].astype(BF16)
        dxb_ref[...] = dxb
        dmg = _dot_nt(dxb, wo_ref[...])
        for o_ref, g_ref, w_ref, do_ref, dg_ref, dp_ref in (
                (oa_ref, ga_ref, wa_ref, doa_ref, dga_ref, dpa_ref),
                (ob_ref, gb_ref, wb_ref, dob_ref, dgb_ref, dpb_ref)):
            pv = _dot(o_ref[...], w_ref[...])
            sg = jax.nn.sigmoid(g_ref[...])
            dp = (dmg * sg).astype(BF16)
            dp_ref[...] = dp
            dg_ref[...] = (dmg * pv * sg * (1.0 - sg)).astype(BF16)
            do_ref[...] = _dot_nt(dp, w_ref[...]).astype(BF16)

    row = lambda i: (i, 0)
    full = lambda i: (0, 0)
    wide = pl.BlockSpec((tm, D_MODEL), row)
    half = pl.BlockSpec((tm, 512), row)
    return _call(
        body, (dx2, oa, ob, ga, gb, wswa, wsb, wout), comm=comm, **_grid_ends(s_len // tm), name="merge_bwd",
        grid=(s_len // tm,),
        in_specs=[wide, half, half, wide, wide, pl.BlockSpec((512, D_MODEL), full),
                  pl.BlockSpec((512, D_MODEL), full), pl.BlockSpec((D_MODEL, D_MODEL), full)],
        out_specs=[half, half, wide, wide, wide, wide, wide],
        out_shape=[jax.ShapeDtypeStruct((s_len, 512), BF16)] * 2 + [jax.ShapeDtypeStruct((s_len, D_MODEL), BF16)] * 5,
        compiler_params=_cparams("arbitrary"),
    )


def _loss_fwd_bwd(x3, tgt, g):
    s_len = x3.shape[0]
    tm = min(1024, s_len)

    def body(x_ref, t_ref, g_ref, dx_ref, loss_ref, dg_ref):
        @pl.when(pl.program_id(0) == 0)
        def _():
            loss_ref[...] = jnp.zeros_like(loss_ref)
            dg_ref[...] = jnp.zeros_like(dg_ref)

        xv = x_ref[...]
        gv = g_ref[...]
        r = _rms_rstd(xv)
        err = xv * r * gv - t_ref[...]
        loss_ref[...] += 0.5 * jnp.sum(jnp.mean(err * err, axis=-1, keepdims=True), axis=0, keepdims=True)
        dx, dg = _rms_bwd(err * (1.0 / D_MODEL), xv, r, gv)
        dx_ref[...] = dx
        dg_ref[...] += dg

    row = lambda i: (i, 0)
    return pl.pallas_call(
        body, name="loss_fwd_bwd",
        grid=(s_len // tm,),
        in_specs=[pl.BlockSpec((tm, D_MODEL), row), pl.BlockSpec((tm, D_MODEL), row),
                  pl.BlockSpec((1, D_MODEL), lambda i: (0, 0))],
        out_specs=[pl.BlockSpec((tm, D_MODEL), row), pl.BlockSpec((1, 1), lambda i: (0, 0)),
                   pl.BlockSpec((1, D_MODEL), lambda i: (0, 0))],
        out_shape=[jax.ShapeDtypeStruct((s_len, D_MODEL), F32), jax.ShapeDtypeStruct((1, 1), F32),
                   jax.ShapeDtypeStruct((1, D_MODEL), F32)],
        compiler_params=_cparams("arbitrary"),
    )(x3, tgt, g)


def _rel_bucket_matrix():
    qi = jnp.arange(SWA_BLOCK)[:, None] + SWA_BLOCK
    kj = jnp.arange(2 * SWA_BLOCK)[None, :]
    dist = jnp.maximum(qi - kj, 0)
    max_exact = REL_BUCKETS // 2
    d = jnp.maximum(dist, 1).astype(F32)
    large = max_exact + (jnp.log(d / max_exact) / np.log(REL_MAX_DIST / max_exact)
                         * (REL_BUCKETS - max_exact)).astype(jnp.int32)
    large = jnp.minimum(large, REL_BUCKETS - 1)
    return jnp.where(dist < max_exact, dist, large).astype(jnp.int32)


def _swa_bias_into(bias_ref, bkt_ref, tab_ref):
    bk = bkt_ref[...]
    for h in range(N_HEADS):
        acc = jnp.zeros(bk.shape, F32)
        for bucket in range(REL_BUCKETS):
            acc = jnp.where(bk == bucket, tab_ref[bucket, h], acc)
        bias_ref[h] = acc


def _swa_valid(n):
    shape = (SWA_BLOCK, 2 * SWA_BLOCK)
    row = lax.broadcasted_iota(jnp.int32, shape, 0)
    col = lax.broadcasted_iota(jnp.int32, shape, 1)
    dist = row + SWA_BLOCK - col
    return (dist >= 0) & (dist < SWA_BLOCK) & ((col >= SWA_BLOCK) | (n > 0))


def _swa_windows(kp_ref, kc_ref, vp_ref, vc_ref):
    return (jnp.concatenate([kp_ref[...], kc_ref[...]], axis=0), jnp.concatenate([vp_ref[...], vc_ref[...]], axis=0))


def _swa_place(h):
    return slice(h // 2 * LANES, (h // 2 + 1) * LANES), h % 2, h // SWA_GROUP


def _move_half(x, src, dst):
    moved = x if src == dst else pltpu.roll(x, HEAD_DIM, 1)
    in_dst = (lax.broadcasted_iota(jnp.int32, x.shape, 1) >= HEAD_DIM) == bool(dst)
    return jnp.where(in_dst, moved, 0.0)


def _swa_probs(qk, bias, sink, valid):
    lg = jnp.where(valid, qk * Q_SCALE + bias, NEG_BIG)
    m = jnp.maximum(jnp.max(lg, axis=-1, keepdims=True), sink)
    e = jnp.exp(lg - m)
    es = jnp.exp(sink - m)
    inv = 1.0 / (jnp.sum(e, axis=-1, keepdims=True) + es)
    return e * inv, es * inv


def _swa_specs(s_len):
    blk = SWA_BLOCK
    cur = lambda n: (n, 0)
    prev = lambda n: (jnp.maximum(n - 1, 0), 0)
    kvw = SWA_KV_HEADS * HEAD_DIM
    return [pl.BlockSpec(memory_space=pltpu.SMEM), pl.BlockSpec(memory_space=pltpu.SMEM),
            pl.BlockSpec((blk, 2 * blk), lambda n: (0, 0)),
            pl.BlockSpec((blk, N_HEADS * HEAD_DIM), cur),
            pl.BlockSpec((blk, kvw), prev), pl.BlockSpec((blk, kvw), cur),
            pl.BlockSpec((blk, kvw), prev), pl.BlockSpec((blk, kvw), cur)]


def _swa_fwd(tab, sinks, bkt, q, k, v):
    s_len = q.shape[0]
    blk = SWA_BLOCK

    def body(tab_ref, sink_ref, bkt_ref, q_ref, kp_ref, kc_ref, vp_ref, vc_ref, o_ref, bias_ref):
        n = pl.program_id(0)

        @pl.when(n == 0)
        def _():
            _swa_bias_into(bias_ref, bkt_ref, tab_ref)

        valid = _swa_valid(n)
        kk, vv = _swa_windows(kp_ref, kc_ref, vp_ref, vc_ref)
        st = {}

        def s_logits(h):
            tile, mine, kv = _swa_place(h)
            st[h, "lg"] = _dot_nt(_move_half(q_ref[:, tile].astype(F32), mine, kv).astype(BF16), kk)

        def s_probs(h):
            st[h, "p"] = _swa_probs(st.pop((h, "lg")), bias_ref[h], sink_ref[0, h], valid)[0].astype(BF16)

        def s_values(h):
            tile, mine, kv = _swa_place(h)
            part = _move_half(_dot(st.pop((h, "p")), vv), kv, mine)
            if mine == 0:
                st[h + 1, "o"] = part
            else:
                o_ref[:, tile] = (st.pop((h, "o")) + part).astype(BF16)

        _emit_skewed((list(range(N_HEADS)), [s_logits, s_probs, s_values]))

    return pl.pallas_call(
        body, name="swa_fwd",
        grid=(s_len // blk,),
        in_specs=_swa_specs(s_len),
        out_specs=pl.BlockSpec((blk, N_HEADS * HEAD_DIM), lambda n: (n, 0)),
        out_shape=jax.ShapeDtypeStruct((s_len, N_HEADS * HEAD_DIM), BF16),
        scratch_shapes=[pltpu.VMEM((N_HEADS, blk, 2 * blk), F32)],
        compiler_params=_cparams("arbitrary"),
    )(tab, sinks, bkt, q, k, k, v, v)


def _swa_bwd(tab, sinks, bkt, q, k, v, do, comm=None):
    s_len = q.shape[0]
    blk = SWA_BLOCK
    nb = s_len // blk
    kvw = SWA_KV_HEADS * HEAD_DIM

    def body(tab_ref, sink_ref, bkt_ref, q_ref, kp_ref, kc_ref, vp_ref, vc_ref, do_ref,
             dq_ref, dk_ref, dv_ref, dtab_ref, dsink_ref, bias_ref, dbias_ref):
        n = pl.program_id(0)

        @pl.when(n == 0)
        def _():
            _swa_bias_into(bias_ref, bkt_ref, tab_ref)
            dbias_ref[...] = jnp.zeros_like(dbias_ref)
            dk_ref[...] = jnp.zeros_like(dk_ref)
            dv_ref[...] = jnp.zeros_like(dv_ref)
            dsink_ref[...] = jnp.zeros_like(dsink_ref)
            dtab_ref[...] = jnp.zeros_like(dtab_ref)

        valid = _swa_valid(n)
        cur_rows = pl.ds(pl.multiple_of(n * blk, blk), blk)
        prev_rows = pl.ds(pl.multiple_of(jnp.maximum(n - 1, 0) * blk, blk), blk)
        kk, vv = _swa_windows(kp_ref, kc_ref, vp_ref, vc_ref)
        st = {}

        def s_logits(h):
            tile, mine, kv = _swa_place(h)
            st[h, "q"] = _move_half(q_ref[:, tile].astype(F32), mine, kv).astype(BF16)
            st[h, "do"] = _move_half(do_ref[:, tile].astype(F32), mine, kv).astype(BF16)
            st[h, "lg"] = _dot_nt(st[h, "q"], kk)
            st[h, "dp"] = _dot_nt(st[h, "do"], vv)

        def s_probs(h):
            p, ps = _swa_probs(st.pop((h, "lg")), bias_ref[h], sink_ref[0, h], valid)
            dp = st.pop((h, "dp"))
            delta = jnp.sum(p * dp, axis=-1, keepdims=True)
            dl = p * (dp - delta)
            dsink_ref[h:h + 1, :] += jnp.broadcast_to(-jnp.sum(ps * delta, axis=0, keepdims=True), (1, LANES))
            dbias_ref[h] += dl
            st[h, "dl"], st[h, "p"] = dl.astype(BF16), p.astype(BF16)

        def s_products(h):
            tile, mine, kv = _swa_place(h)
            dlb = st.pop((h, "dl"))
            part = _move_half(Q_SCALE * _dot(dlb, kk), kv, mine)
            if mine == 0:
                st[h + 1, "dq"] = part
            else:
                dq_ref[:, tile] = (st.pop((h, "dq")) + part).astype(BF16)
            dk_win = Q_SCALE * _dot_tn(dlb, st.pop((h, "q")))
            dv_win = _dot_tn(st.pop((h, "p")), st.pop((h, "do")))
            dk_ref[prev_rows, :] += dk_win[:blk]
            dv_ref[prev_rows, :] += dv_win[:blk]
            dk_ref[cur_rows, :] += dk_win[blk:]
            dv_ref[cur_rows, :] += dv_win[blk:]

        _emit_skewed((list(range(N_HEADS)), [s_logits, s_probs, s_products]))

        @pl.when(n == nb - 1)
        def _():
            bk = bkt_ref[...]
            lane = lax.broadcasted_iota(jnp.int32, (1, LANES), 1)
            for bucket in range(REL_BUCKETS):
                rowv = jnp.zeros((1, LANES), F32)
                for h in range(N_HEADS):
                    val = jnp.sum(jnp.where(bk == bucket, dbias_ref[h], 0.0), axis=1, keepdims=True)
                    val = jnp.sum(val, axis=0, keepdims=True)
                    rowv = jnp.where(lane == h, val, rowv)
                dtab_ref[bucket:bucket + 1, :] = rowv

    return _call(
        body, (tab, sinks, bkt, q, k, k, v, v, do), comm=comm, **_grid_ends(nb), name="swa_bwd",
        grid=(nb,),
        in_specs=_swa_specs(s_len) + [pl.BlockSpec((blk, N_HEADS * HEAD_DIM), lambda n: (n, 0))],
        out_specs=[pl.BlockSpec((blk, N_HEADS * HEAD_DIM), lambda n: (n, 0)),
                   pl.BlockSpec((s_len, kvw), lambda n: (0, 0)), pl.BlockSpec((s_len, kvw), lambda n: (0, 0)),
                   pl.BlockSpec((REL_BUCKETS, LANES), lambda n: (0, 0)), pl.BlockSpec((N_HEADS, LANES), lambda n: (0, 0))],
        out_shape=[jax.ShapeDtypeStruct((s_len, N_HEADS * HEAD_DIM), BF16),
                   jax.ShapeDtypeStruct((s_len, kvw), F32), jax.ShapeDtypeStruct((s_len, kvw), F32),
                   jax.ShapeDtypeStruct((REL_BUCKETS, LANES), F32), jax.ShapeDtypeStruct((N_HEADS, LANES), F32)],
        scratch_shapes=[pltpu.VMEM((N_HEADS, blk, 2 * blk), F32), pltpu.VMEM((N_HEADS, blk, 2 * blk), F32)],
        compiler_params=_cparams("arbitrary"),
    )


def _sb_terms(z, valid):
    zc = jnp.minimum(z, SB_LOGIT_CAP)
    lk = -jnp.log(1.0 + jnp.exp(zc))
    lsz = zc + lk
    return lsz, (lk if valid is None else jnp.where(valid, lk, 0.0))


def _bf16_parts(vals):
    parts, rest = [], vals
    for n in range(SB_SUM_PARTS):
        parts.append(rest.astype(BF16))
        if n + 1 < SB_SUM_PARTS:
            rest = rest - parts[-1].astype(F32)
    return parts[0] if len(parts) == 1 else jnp.concatenate(parts, axis=1)


def _row_sum_lanes(vals):
    return jnp.broadcast_to(jnp.sum(vals, axis=-1, keepdims=True), (vals.shape[0], LANES))


def _emit_skewed(*groups):
    for step in range(max(len(items) + len(stages) - 1 for items, stages in groups)):
        for items, stages in groups:
            for s, stage in enumerate(stages):
                if 0 <= step - s < len(items):
                    stage(items[step - s])


def _sb_items(edge):
    items = []
    for h in range(2):
        for r0 in range(0, SB_QUERIES, SB_ROWS):
            if edge is None or r0 >= (edge + 1) * SB_KEYS:
                items.append((h, r0, False))
            elif r0 + SB_ROWS - 1 > edge * SB_KEYS:
                items.append((h, r0, True))
    return items


def _sb_valid(w, edge):
    row = lax.broadcasted_iota(jnp.int32, (SB_ROWS, SB_KEYS), 0) + w[1]
    col = lax.broadcasted_iota(jnp.int32, (SB_ROWS, SB_KEYS), 1) + edge * SB_KEYS
    return col < row


def _sb_consts(tq, tk):
    low = lax.broadcasted_iota(jnp.int32, (tq, LANES), 1) < HEAD_DIM
    row = lax.broadcasted_iota(jnp.int32, (tk, tk), 0)
    col = lax.broadcasted_iota(jnp.int32, (tk, tk), 1)
    right = (row > col).astype(BF16)
    left = (row < col).astype(BF16)
    return low, jnp.concatenate([right] * SB_SUM_PARTS, axis=0), jnp.concatenate([left] * SB_SUM_PARTS, axis=0)


def _sb_fwd(q, k, v, comm=None):
    s_len = q.shape[0]
    tq, tk, tr = SB_QUERIES, SB_KEYS, SB_ROWS
    nk, ratio = s_len // tk, tq // tk
    assert nk <= LANES

    def body(q_ref, k_ref, v_ref, o_ref, car_ref, c_ref, oacc_ref, logw_ref, lksum_ref):
        i = pl.program_id(1)
        qv = q_ref[...]
        low, tri2, _ = _sb_consts(tq, tk)
        lane = lax.broadcasted_iota(jnp.int32, (tr, LANES), 1)
        zero = jnp.zeros_like(qv)
        q_heads = (jnp.where(low, qv, zero), jnp.where(low, zero, qv))
        c_ref[...] = jnp.zeros_like(c_ref)
        oacc_ref[...] = jnp.zeros_like(oacc_ref)
        car_ref[...] = jnp.full_like(car_ref, NEG_BIG)

        def front(j, edge):
            keys = k_ref[pl.ds(pl.multiple_of(j * tk, tk), tk), :]
            slot = j % 2
            st = {}

            def s_logits(w):
                st[w, "z"] = _dot_nt(q_heads[w[0]][w[1]:w[1] + tr], keys)

            def s_terms(w):
                valid = _sb_valid(w, edge) if w[2] else None
                lsz, lk = _sb_terms(st.pop((w, "z")), valid)
                st[w, "parts"] = _bf16_parts(lk)
                st[w, "lsz"] = lsz if valid is None else jnp.where(valid, lsz, NEG_BIG)
                lksum_ref[slot, w[0], w[1]:w[1] + tr, :] = _row_sum_lanes(lk)

            def s_suffix(w):
                logw_ref[slot, w[0], w[1]:w[1] + tr, :] = st.pop((w, "lsz")) + _dot(st.pop((w, "parts")), tri2)

            return _sb_items(edge), [s_logits, s_terms, s_suffix]

        def back(j, edge):
            vv = v_ref[pl.ds(pl.multiple_of(j * tk, tk), tk), :]
            slot = j % 2
            st = {}

            def s_weights(w):
                h, rs = w[0], slice(w[1], w[1] + tr)
                c = c_ref[h, rs, :]
                st[w, "a"] = jnp.exp(logw_ref[slot, h, rs, :] + jnp.tile(c, (1, tk // LANES))).astype(BF16)
                car_ref[h, rs, :] = jnp.where(lane == j, c, car_ref[h, rs, :])
                c_ref[h, rs, :] = c + lksum_ref[slot, h, rs, :]

            def s_values(w):
                oacc_ref[w[0], w[1]:w[1] + tr, :] += _dot(st.pop((w, "a")), vv)

            return _sb_items(edge), [s_weights, s_values]

        first = i * ratio
        _emit_skewed(front(first + ratio - 1, ratio - 1))
        for m in reversed(range(ratio - 1)):
            _emit_skewed(front(first + m, m), back(first + m + 1, m + 1))

        @pl.when(i == 0)
        def _():
            _emit_skewed(back(0, 0))

        def alive():
            return (jnp.max(c_ref[...]) >= SB_DEAD_CARRY).astype(jnp.int32)

        @pl.when(i > 0)
        def _():
            _emit_skewed(front(first - 1, None), back(first, 0))

            def step(state):
                pending, _ = state
                _emit_skewed(front(pending - 1, None), back(pending, None))
                return pending - 1, alive()

            pending, live = lax.while_loop(lambda s: (s[0] > 0) & (s[1] > 0), step, (first - 1, alive()))

            @pl.when(live > 0)
            def _():
                _emit_skewed(back(pending, None))

        o_ref[...] = jnp.where(low, oacc_ref[0], oacc_ref[1]).astype(BF16)

    return _call(
        body, (q, k, v), comm=comm, **_grid_ends(N_HEADS // 2, s_len // tq), name="sb_fwd",
        grid=(N_HEADS // 2, s_len // tq),
        in_specs=[pl.BlockSpec((tq, LANES), lambda p, i: (i, p)),
                  pl.BlockSpec((s_len, LANES), lambda p, i: (0, p)),
                  pl.BlockSpec((s_len, LANES), lambda p, i: (0, p))],
        out_specs=[pl.BlockSpec((tq, LANES), lambda p, i: (i, p)), pl.BlockSpec((2, tq, LANES), lambda p, i: (p, i, 0))],
        out_shape=[jax.ShapeDtypeStruct((s_len, N_HEADS * HEAD_DIM), BF16),
                   jax.ShapeDtypeStruct((N_HEADS, s_len, LANES), F32)],
        scratch_shapes=[pltpu.VMEM((2, tq, LANES), F32), pltpu.VMEM((2, tq, LANES), F32),
                        pltpu.VMEM((2, 2, tq, tk), F32), pltpu.VMEM((2, 2, tq, LANES), F32)],
        compiler_params=_cparams("arbitrary", "arbitrary"),
    )


def _sb_bwd(q, k, v, do, cars):
    s_len = q.shape[0]
    tq, tk, tr = SB_QUERIES, SB_KEYS, SB_ROWS
    nk, ratio = s_len // tk, tq // tk

    def body(q_ref, k_ref, v_ref, do_ref, car_ref, dq_ref, dk_ref, dv_ref,
             gleft_ref, dqacc_ref, dkacc_ref, dvacc_ref, logw_ref, lsz_ref, da_ref, a_ref, dz_ref):
        i = pl.program_id(1)

        @pl.when(i == 0)
        def _():
            dkacc_ref[...] = jnp.zeros_like(dkacc_ref)
            dvacc_ref[...] = jnp.zeros_like(dvacc_ref)

        qv = q_ref[...]
        dov = do_ref[...]
        low, tri_right2, tri_left2 = _sb_consts(tq, tk)
        lane = lax.broadcasted_iota(jnp.int32, (tr, LANES), 1)
        zero = jnp.zeros_like(qv)
        q_heads = (jnp.where(low, qv, zero), jnp.where(low, zero, qv))
        do_heads = (jnp.where(low, dov, zero), jnp.where(low, zero, dov))
        q_t = qv.astype(F32).T.astype(BF16)
        do_t = dov.astype(F32).T.astype(BF16)
        gleft_ref[...] = jnp.zeros_like(gleft_ref)
        dqacc_ref[...] = jnp.zeros_like(dqacc_ref)

        def front(j, edge):
            key_rows = pl.ds(pl.multiple_of(j * tk, tk), tk)
            keys, values = k_ref[key_rows, :], v_ref[key_rows, :]
            slot = j % 2
            st = {}

            def s_logits(w):
                h, rs = w[0], slice(w[1], w[1] + tr)
                st[w, "z"] = _dot_nt(q_heads[h][rs], keys)
                da_ref[slot, h, rs, :] = _dot_nt(do_heads[h][rs], values)

            def s_terms(w):
                h, rs = w[0], slice(w[1], w[1] + tr)
                valid = _sb_valid(w, edge) if w[2] else None
                lsz, lk = _sb_terms(st.pop((w, "z")), valid)
                st[w, "parts"] = _bf16_parts(lk)
                lsz = lsz if valid is None else jnp.where(valid, lsz, NEG_BIG)
                lsz_ref[slot, h, rs, :] = lsz
                st[w, "lszc"] = lsz + jnp.sum(jnp.where(lane == j, car_ref[h, rs, :], 0.0), axis=-1, keepdims=True)

            def s_suffix(w):
                logw_ref[slot, w[0], w[1]:w[1] + tr, :] = st.pop((w, "lszc")) + _dot(st.pop((w, "parts")), tri_right2)

            return _sb_items(edge), [s_logits, s_terms, s_suffix]

        def back(j, edge):
            kv = k_ref[pl.ds(pl.multiple_of(j * tk, tk), tk), :]
            slot = j % 2
            st = {}

            items = _sb_items(edge)
            head_rows = [[r0 for hh, r0, _ in items if hh == h] for h in range(2)]

            def s_weights(w):
                h, rs = w[0], slice(w[1], w[1] + tr)
                a = jnp.exp(logw_ref[slot, h, rs, :])
                g = a * da_ref[slot, h, rs, :]
                a_ref[h, rs, :] = a.astype(BF16)
                st[w, "g"], st[w, "parts"] = g, _bf16_parts(g)

            def s_prefix(w):
                st[w, "gs"] = _dot(st.pop((w, "parts")), tri_left2)

            def s_dz(w):
                h, rs = w[0], slice(w[1], w[1] + tr)
                g = st.pop((w, "g"))
                gleft = gleft_ref[h, rs, :]
                gsum = st.pop((w, "gs")) + jnp.tile(gleft, (1, tk // LANES))
                dz = (g - jnp.exp(lsz_ref[slot, h, rs, :]) * (g + gsum)).astype(BF16)
                st[w, "dz"] = dz
                dz_ref[h, rs, :] = dz
                gleft_ref[h, rs, :] = gleft + _row_sum_lanes(g)

            def s_products(w):
                h, rs = w[0], slice(w[1], w[1] + tr)
                dqacc_ref[h, rs, :] += _dot(st.pop((w, "dz")), kv)
                if w[1] == head_rows[h][-1]:
                    feat = slice(h * HEAD_DIM, (h + 1) * HEAD_DIM)
                    hr = slice(head_rows[h][0], tq)
                    dkacc_ref[j, feat, :] += _dot(q_t[feat, hr], dz_ref[h, hr, :])
                    dvacc_ref[j, feat, :] += _dot(do_t[feat, hr], a_ref[h, hr, :])

            return items, [s_weights, s_prefix, s_dz, s_products]

        first = i * ratio
        tile_max = jnp.max(jnp.maximum(car_ref[0], car_ref[1]), axis=0, keepdims=True)
        start = jnp.clip(first + ratio - jnp.sum(jnp.where(tile_max >= SB_DEAD_CARRY, 1, 0)), 0, first)

        @pl.when(start == first)
        def _():
            _emit_skewed(front(first, 0))

        @pl.when(start < first)
        def _():
            _emit_skewed(front(start, None))

            def step(jj, carry):
                _emit_skewed(front(jj, None), back(jj - 1, None))
                return carry

            lax.fori_loop(start + 1, first, step, 0)
            _emit_skewed(front(first, 0), back(first - 1, None))

        for m in range(1, ratio):
            _emit_skewed(front(first + m, m), back(first + m - 1, m - 1))
        _emit_skewed(back(first + ratio - 1, ratio - 1))
        dq_ref[...] = (Q_SCALE * jnp.where(low, dqacc_ref[0], dqacc_ref[1])).astype(BF16)

        @pl.when(i == s_len // tq - 1)
        def _():
            for j in range(nk):
                dk_ref[j * tk:(j + 1) * tk, :] = dkacc_ref[j].T.astype(BF16)
                dv_ref[j * tk:(j + 1) * tk, :] = dvacc_ref[j].T.astype(BF16)

    qblk = pl.BlockSpec((tq, LANES), lambda p, i: (i, p))
    col_full = pl.BlockSpec((s_len, LANES), lambda p, i: (0, p))
    return pl.pallas_call(
        body, name="sb_bwd",
        grid=(N_HEADS // 2, s_len // tq),
        in_specs=[qblk, col_full, col_full, qblk, pl.BlockSpec((2, tq, LANES), lambda p, i: (p, i, 0))],
        out_specs=[qblk, col_full, col_full],
        out_shape=[jax.ShapeDtypeStruct((s_len, N_HEADS * HEAD_DIM), BF16)] * 3,
        scratch_shapes=[pltpu.VMEM((2, tq, LANES), F32), pltpu.VMEM((2, tq, LANES), F32),
                        pltpu.VMEM((nk, LANES, tk), F32), pltpu.VMEM((nk, LANES, tk), F32)]
        + [pltpu.VMEM((2, 2, tq, tk), F32)] * 3 + [pltpu.VMEM((2, tq, tk), BF16)] * 2,
        compiler_params=_cparams("parallel", "arbitrary"),
    )(q, k, v, do, cars)


def _local_step(xs, tgt, gains, sinks, rel_bias, weights_of, ship):
    g1, gmix, g2, gfin = gains
    bkt = _rel_bucket_matrix()
    grads = {}

    def carried(outs, comm, count):
        return outs[:count], (list(outs[count:]) if comm is not None else None)

    wts = dict(weights_of(0, None))
    comm = ship("weights", 1)
    (x1, h1, a1, b1, u1), landed = carried(
        _ffn_fwd(xs, g1, wts["ffn1_w1t"], wts["ffn1_w3t"], wts["ffn1_w2"], "1", comm), comm, 5)
    wts.update(weights_of(1, landed))
    hm, qa, ka, va, qb, kb, vb, ga, gb = _proj_fwd(x1, gmix, wts["w_int"])
    oa = _swa_fwd(rel_bias, sinks, bkt, qa, ka, va)
    comm = ship("weights", 2)
    (ob, cars), landed = carried(_sb_fwd(qb, kb, vb, comm), comm, 2)
    wts.update(weights_of(2, landed))
    x2, mg = _merge_fwd(x1, oa, ob, ga, gb, wts["w_swa"], wts["w_sb"], wts["w_out"])
    x3, h3, a3, b3, u3 = _ffn_fwd(x2, g2, wts["ffn2_w1t"], wts["ffn2_w3t"], wts["ffn2_w2"], "2")
    dx3, loss, dgfin = _loss_fwd_bwd(x3, tgt, gfin)

    def grad_chain(items):
        prev = None
        for name, lhs, rhs in items:
            comm = None if prev is None else ship("grads", (prev[0],), prev[1])
            res = _matmul_tn(lhs, rhs, name, comm)
            if prev is not None:
                grads[(prev[0],)] = prev[1] if comm is None else res[1]
            prev = (name, {_GRAD_KEY[name]: res if comm is None else res[0]})
        return prev

    dx2, dg2, da3, db3, dx3b = _ffn_bwd(dx3, x2, g2, a3, b3, wts["ffn2_w1t"], wts["ffn2_w3t"], wts["ffn2_w2"], "2")
    last = grad_chain((("ffn2_w1", da3, h3), ("ffn2_w3", db3, h3), ("ffn2_w2", u3, dx3b)))
    comm = ship("grads", (last[0],), last[1])
    (doa, dob, dga, dgb, dpa, dpb, dx2b), landed = carried(
        _merge_bwd(dx2, oa, ob, ga, gb, wts["w_swa"], wts["w_sb"], wts["w_out"], comm), comm, 7)
    grads[(last[0],)] = last[1] if comm is None else landed[0]
    dqa, dka, dva, dtab, dsink = _swa_bwd(rel_bias, sinks, bkt, qa, ka, va, doa)

    big = {"w_out": _matmul_tn(mg, dx2b, "w_out"), "w_swa": _matmul_tn(oa, dpa, "w_swa"),
           "w_sb": _matmul_tn(ob, dpb, "w_sb")}
    dqb, dkb, dvb = _sb_bwd(qb, kb, vb, dob, cars)
    dpieces = (dqa, dka.astype(BF16), dva.astype(BF16), dqb, dkb, dvb, dga, dgb)
    big["w_int"] = _matmul_tn_stacked(dpieces, hm, "w_in")
    dx1, dgmix = _proj_bwd(dpieces, dx2, x1, gmix, wts["w_int"])

    comm = ship("grads", GROUPS[1], big)
    (dx0, dg1, da1, db1, dx1b), landed = carried(
        _ffn_bwd(dx1, xs, g1, a1, b1, wts["ffn1_w1t"], wts["ffn1_w3t"], wts["ffn1_w2"], "1", comm), comm, 5)
    grads[GROUPS[1]] = big if comm is None else landed[0]

    last = grad_chain((("ffn1_w1", da1, h1), ("ffn1_w3", db1, h1), ("ffn1_w2", u1, dx1b)))
    grads[(last[0],)] = last[1]

    small = {"gains": (dg1, dgmix, dg2, dgfin), "sinks": dsink[:, 0], "rel_bias": dtab[:, :N_HEADS]}
    return loss, dx0, small, grads


def _my_place():
    return lax.axis_index("x"), lax.axis_index("y"), lax.axis_index("c")


def _flip(v, bit):
    return 1 - v if bit else v


_RELATIONS = tuple((k >> 2 & 1, k >> 1 & 1, k & 1) for k in range(1, N_DEV))


def _gather_weights(blocks, tag):
    count = len(blocks)

    def body(*refs):
        x_refs, out_refs = refs[:count], refs[count:2 * count]
        send_sems, recv_sems, local_sems = refs[2 * count:]
        x, y, c = _my_place()
        me, sibling = (x, y, c), (x, y, 1 - c)
        chips = [(1 - x, y), (x, 1 - y), (1 - x, 1 - y)]

        def rows(s, px, py, pc):
            return out_refs[s].at[4 * px + 2 * py + pc]

        def copy(s, k, block, to, src=None):
            return pltpu.make_async_remote_copy(
                src_ref=rows(s, *block) if src is None else src, dst_ref=rows(s, *block),
                send_sem=send_sems.at[s, k], recv_sem=recv_sems.at[s, k],
                device_id=to, device_id_type=pl.DeviceIdType.MESH)

        mine = [pltpu.make_async_copy(x_refs[s], rows(s, *me), local_sems.at[s]) for s in range(count)]
        first, passed = [], []
        for s in range(count):
            mine[s].start()
            first.append(copy(s, 0, me, sibling, src=x_refs[s]))
            first += [copy(s, 1 + j, me, (*chip, c), src=x_refs[s]) for j, chip in enumerate(chips)]
        for cp in first:
            cp.start()
        for s in range(count):
            for j, chip in enumerate(chips):
                copy(s, 1 + j, (*chip, c), me).wait_recv()
                passed.append(copy(s, 4 + j, (*chip, c), sibling))
                passed[-1].start()
        for s in range(count):
            copy(s, 0, sibling, me).wait_recv()
            for j, chip in enumerate(chips):
                copy(s, 4 + j, (*chip, 1 - c), me).wait_recv()
        for cp in first + passed:
            cp.wait_send()
        for cp in mine:
            cp.wait()

    anywhere = pl.BlockSpec(memory_space=pl.ANY)
    return pl.pallas_call(
        body, name=f"gather_weights_{tag}",
        out_shape=[jax.ShapeDtypeStruct((N_DEV,) + b.shape, b.dtype) for b in blocks],
        in_specs=[anywhere] * count, out_specs=[anywhere] * count,
        scratch_shapes=[pltpu.SemaphoreType.DMA((count, N_DEV - 1)), pltpu.SemaphoreType.DMA((count, N_DEV - 1)),
                        pltpu.SemaphoreType.DMA((count,))],
    )(*blocks)


def _exchange_grads(gp, tag):
    def body(g_ref, out_ref, send_sems, recv_sems, local_sem):
        x, y, c = _my_place()
        me = 4 * x + 2 * y + c
        mine = pltpu.make_async_copy(g_ref.at[me], out_ref.at[me], local_sem)
        mine.start()
        copies = []
        for k, (fx, fy, fc) in enumerate(_RELATIONS):
            px, py, pc = _flip(x, fx), _flip(y, fy), _flip(c, fc)
            peer = 4 * px + 2 * py + pc
            copies.append((
                pltpu.make_async_remote_copy(
                    src_ref=g_ref.at[peer], dst_ref=out_ref.at[me], send_sem=send_sems.at[k], recv_sem=recv_sems.at[k],
                    device_id=(px, py, pc), device_id_type=pl.DeviceIdType.MESH),
                pltpu.make_async_remote_copy(
                    src_ref=g_ref.at[peer], dst_ref=out_ref.at[peer], send_sem=send_sems.at[k], recv_sem=recv_sems.at[k],
                    device_id=(px, py, pc), device_id_type=pl.DeviceIdType.MESH)))
        for out_cp, _ in copies:
            out_cp.start()
        for _, in_cp in copies:
            in_cp.wait_recv()
        for out_cp, _ in copies:
            out_cp.wait_send()
        mine.wait()

    return pl.pallas_call(
        body, name=f"exchange_grads_{tag}",
        out_shape=jax.ShapeDtypeStruct(gp.shape, gp.dtype),
        in_specs=[pl.BlockSpec(memory_space=pl.ANY)],
        out_specs=pl.BlockSpec(memory_space=pl.ANY),
        scratch_shapes=[pltpu.SemaphoreType.DMA((7,)), pltpu.SemaphoreType.DMA((7,)), pltpu.SemaphoreType.DMA(())],
    )(gp)


def _peers():
    x, y, c = _my_place()
    out = []
    for k, (fx, fy, fc) in enumerate(_RELATIONS):
        px, py, pc = _flip(x, fx), _flip(y, fy), _flip(c, fc)
        out.append((k, (px, py, pc), 4 * px + 2 * py + pc))
    return out, 4 * x + 2 * y + c


def _grid_ends(*grid):
    def first():
        return functools.reduce(lambda a, b: a & b, [pl.program_id(d) == 0 for d in range(len(grid))])

    def last():
        return functools.reduce(lambda a, b: a & b, [pl.program_id(d) == n - 1 for d, n in enumerate(grid)])

    return {"first": first, "last": last}


def _call(body, operands, *, comm=None, first=None, last=None, **kw):
    if comm is None:
        return pl.pallas_call(body, **kw)(*operands)
    in_specs, out_specs, out_shape = list(kw.pop("in_specs")), list(kw.pop("out_specs")), list(kw.pop("out_shape"))
    scratch = list(kw.pop("scratch_shapes", ()))
    n_in, n_out, n_scr, n_src = len(in_specs), len(out_specs), len(scratch), len(comm)

    def wrapped(*refs):
        ins, src_refs = refs[:n_in], refs[n_in:n_in + n_src]
        outs = refs[n_in + n_src:n_in + n_src + n_out]
        land_refs = refs[n_in + n_src + n_out:n_in + 2 * n_src + n_out]
        scr = refs[n_in + 2 * n_src + n_out:n_in + 2 * n_src + n_out + n_scr]
        send_sems, recv_sems, local_sems = refs[n_in + 2 * n_src + n_out + n_scr:]
        peers, me = _peers()
        mine, going, coming = [], [], []
        for s, (_, per_peer) in enumerate(comm):
            src_ref, land_ref = src_refs[s], land_refs[s]
            mine.append(pltpu.make_async_copy(src_ref.at[me] if per_peer else src_ref, land_ref.at[me], local_sems.at[s]))
            for k, where, slab in peers:
                piece = src_ref.at[slab] if per_peer else src_ref
                going.append(pltpu.make_async_remote_copy(
                    src_ref=piece, dst_ref=land_ref.at[me], send_sem=send_sems.at[s, k], recv_sem=recv_sems.at[s, k],
                    device_id=where, device_id_type=pl.DeviceIdType.MESH))
                coming.append(pltpu.make_async_remote_copy(
                    src_ref=piece, dst_ref=land_ref.at[slab], send_sem=send_sems.at[s, k], recv_sem=recv_sems.at[s, k],
                    device_id=where, device_id_type=pl.DeviceIdType.MESH))

        @pl.when(first())
        def _():
            for cp in mine + going:
                cp.start()

        body(*ins, *outs, *scr)

        @pl.when(last())
        def _():
            for cp in coming:
                cp.wait_recv()
            for cp in going:
                cp.wait_send()
            for cp in mine:
                cp.wait()

    anywhere = pl.BlockSpec(memory_space=pl.ANY)
    lands = [jax.ShapeDtypeStruct(src.shape if per_peer else (N_DEV,) + src.shape, src.dtype) for src, per_peer in comm]
    return pl.pallas_call(
        wrapped, in_specs=in_specs + [anywhere] * n_src, out_specs=out_specs + [anywhere] * n_src,
        out_shape=out_shape + lands,
        scratch_shapes=scratch + [pltpu.SemaphoreType.DMA((n_src, N_DEV - 1)), pltpu.SemaphoreType.DMA((n_src, N_DEV - 1)),
                                  pltpu.SemaphoreType.DMA((n_src,))],
        **kw)(*operands, *[src for src, _ in comm])


def _adamw(w, g, m, v):
    m = ADAM_B1 * m + (1.0 - ADAM_B1) * g
    v = ADAM_B2 * v + (1.0 - ADAM_B2) * jnp.square(g)
    m_hat = m / (1.0 - ADAM_B1 ** ADAM_STEP)
    v_hat = v / (1.0 - ADAM_B2 ** ADAM_STEP)
    delta = -ADAM_LR * (m_hat / (jnp.sqrt(v_hat) + ADAM_EPS) + ADAM_WD * w)
    return delta, m, v


def _sum_and_adamw(parts, w, m, v, tr, tag):
    rows = w.shape[0]
    assert rows % tr == 0

    def body(p_ref, w_ref, m_ref, v_ref, g_out, d_out, m_out, v_out):
        g = p_ref[0].astype(F32)
        for d in range(1, N_DEV):
            g = g + p_ref[d].astype(F32)
        delta, mn, vn = _adamw(w_ref[...], g, m_ref[...], v_ref[...])
        g_out[...] = g
        d_out[...] = delta
        m_out[...] = mn
        v_out[...] = vn

    sp = pl.BlockSpec((tr, D_MODEL), lambda i: (i, 0))
    return pl.pallas_call(
        body, name=f"sum_and_adamw_{tag}",
        grid=(rows // tr,),
        in_specs=[pl.BlockSpec((N_DEV, tr, D_MODEL), lambda i: (0, i, 0)), sp, sp, sp],
        out_specs=[sp] * 4,
        out_shape=[jax.ShapeDtypeStruct(w.shape, F32)] * 4,
        compiler_params=_cparams("parallel"),
    )(parts, w, m, v)


def _small_allreduce_adamw(part, w, m, v):
    def body(p_ref, w_ref, m_ref, v_ref, g_out, d_out, m_out, v_out, buf, send_sems, recv_sems):
        x, y, c = _my_place()
        me = 4 * x + 2 * y + c
        buf[me] = p_ref[...]
        copies = []
        for k, (fx, fy, fc) in enumerate(_RELATIONS):
            px, py, pc = _flip(x, fx), _flip(y, fy), _flip(c, fc)
            peer = 4 * px + 2 * py + pc
            copies.append((
                pltpu.make_async_remote_copy(
                    src_ref=buf.at[me], dst_ref=buf.at[me], send_sem=send_sems.at[k], recv_sem=recv_sems.at[k],
                    device_id=(px, py, pc), device_id_type=pl.DeviceIdType.MESH),
                pltpu.make_async_remote_copy(
                    src_ref=buf.at[me], dst_ref=buf.at[peer], send_sem=send_sems.at[k], recv_sem=recv_sems.at[k],
                    device_id=(px, py, pc), device_id_type=pl.DeviceIdType.MESH)))
        for out_cp, _ in copies:
            out_cp.start()
        for _, in_cp in copies:
            in_cp.wait_recv()
        for out_cp, _ in copies:
            out_cp.wait_send()
        g = buf[0]
        for d in range(1, N_DEV):
            g = g + buf[d]
        delta, mn, vn = _adamw(w_ref[...], g, m_ref[...], v_ref[...])
        g_out[...] = g
        d_out[...] = delta
        m_out[...] = mn
        v_out[...] = vn

    vm = pl.BlockSpec(memory_space=pltpu.VMEM)
    return pl.pallas_call(
        body, name="small_allreduce_adamw",
        in_specs=[vm] * 4, out_specs=[vm] * 4,
        out_shape=[jax.ShapeDtypeStruct(w.shape, F32)] * 4,
        scratch_shapes=[pltpu.VMEM((N_DEV,) + part.shape, F32),
                        pltpu.SemaphoreType.DMA((7,)), pltpu.SemaphoreType.DMA((7,))],
    )(part, w, m, v)


_TRANSPOSED = ("ffn1_w1", "ffn1_w3", "w_in", "ffn2_w1", "ffn2_w3")
_BRANCH = ("w_branch_swa", "w_branch_sb")


def _pack_shards(t, names):
    parts = []
    for name in names:
        a = t[name][0]
        if name in _TRANSPOSED:
            a = a.T
        elif name in _BRANCH:
            a = a.reshape(64, D_MODEL)
        parts.append(a)
    return jnp.concatenate(parts, axis=0)


def _unpack_shards(p, names):
    out, lo = {}, 0
    for name in names:
        a = p[lo:lo + BIG_ROWS[BIG_NAMES.index(name)]]
        lo += a.shape[0]
        if name in _TRANSPOSED:
            a = a.T
        elif name in _BRANCH:
            a = a.reshape(512, 128)
        out[name] = a[None]
    return out


def _full_weights(zones, names):
    out = {}
    for name, a in zip(names, zones):
        if name in _BRANCH:
            a = a.reshape(N_DEV, 512, 128).transpose(1, 0, 2).reshape(512, D_MODEL)
        out[_GRAD_KEY[name]] = a.reshape(-1, D_MODEL)
    return out


_GRAD_KEY = {"ffn1_w1": "ffn1_w1t", "ffn1_w3": "ffn1_w3t", "ffn1_w2": "ffn1_w2", "w_in": "w_int",
             "w_branch_swa": "w_swa", "w_branch_sb": "w_sb", "w_out": "w_out",
             "ffn2_w1": "ffn2_w1t", "ffn2_w3": "ffn2_w3t", "ffn2_w2": "ffn2_w2"}


def _pack_full_grads(big, names):
    parts = []
    for name in names:
        a = big[_GRAD_KEY[name]]
        if name in _BRANCH:
            a = a.reshape(512, N_DEV, 128).transpose(1, 0, 2)
        parts.append(a.reshape(N_DEV, BIG_ROWS[BIG_NAMES.index(name)], D_MODEL).astype(BF16))
    return jnp.concatenate(parts, axis=1)


_SMALL_NAMES = ("norm_ffn1", "norm_mix", "norm_ffn2", "norm_final", "swa_sinks", "rel_bias")


def _pack_small(vals):
    rows = []
    for a in vals:
        a = a.reshape(-1)
        rows.append(jnp.pad(a, (0, D_MODEL - a.shape[0])))
    rows += [jnp.zeros((D_MODEL,), F32)] * (SMALL_ROWS - len(rows))
    return jnp.stack(rows)


def _unpack_small(p):
    return {"norm_ffn1": p[0:1], "norm_mix": p[1:2], "norm_ffn2": p[2:3], "norm_final": p[3],
            "swa_sinks": p[4:5, :N_HEADS], "rel_bias": p[5, :REL_BUCKETS * N_HEADS].reshape(REL_BUCKETS, N_HEADS)}


ALL_NAMES = ("norm_ffn1", "ffn1_w1", "ffn1_w3", "ffn1_w2", "norm_mix", "w_in", "swa_sinks", "rel_bias",
             "w_branch_swa", "w_branch_sb", "w_out", "norm_ffn2", "ffn2_w1", "ffn2_w3", "ffn2_w2", "norm_final")


def kernel(x, norm_ffn1, ffn1_w1, ffn1_w3, ffn1_w2, norm_mix, w_in, swa_sinks, rel_bias, w_branch_swa, w_branch_sb, w_out, norm_ffn2, ffn2_w1, ffn2_w3, ffn2_w2, norm_final, loss_target, m_norm_ffn1, m_ffn1_w1, m_ffn1_w3, m_ffn1_w2, m_norm_mix, m_w_in, m_swa_sinks, m_rel_bias, m_w_branch_swa, m_w_branch_sb, m_w_out, m_norm_ffn2, m_ffn2_w1, m_ffn2_w3, m_ffn2_w2, m_norm_final, v_norm_ffn1, v_ffn1_w1, v_ffn1_w3, v_ffn1_w2, v_norm_mix, v_w_in, v_swa_sinks, v_rel_bias, v_w_branch_swa, v_w_branch_sb, v_w_out, v_norm_ffn2, v_ffn2_w1, v_ffn2_w3, v_ffn2_w2, v_norm_final):
    w = dict(zip(ALL_NAMES, (norm_ffn1, ffn1_w1, ffn1_w3, ffn1_w2, norm_mix, w_in, swa_sinks, rel_bias,
                             w_branch_swa, w_branch_sb, w_out, norm_ffn2, ffn2_w1, ffn2_w3, ffn2_w2, norm_final)))
    m = dict(zip(ALL_NAMES, (m_norm_ffn1, m_ffn1_w1, m_ffn1_w3, m_ffn1_w2, m_norm_mix, m_w_in, m_swa_sinks, m_rel_bias,
                             m_w_branch_swa, m_w_branch_sb, m_w_out, m_norm_ffn2, m_ffn2_w1, m_ffn2_w3, m_ffn2_w2,
                             m_norm_final)))
    v = dict(zip(ALL_NAMES, (v_norm_ffn1, v_ffn1_w1, v_ffn1_w3, v_ffn1_w2, v_norm_mix, v_w_in, v_swa_sinks, v_rel_bias,
                             v_w_branch_swa, v_w_branch_sb, v_w_out, v_norm_ffn2, v_ffn2_w1, v_ffn2_w3, v_ffn2_w2,
                             v_norm_final)))

    def my_blocks(group):
        return [_pack_shards(w, (name,)).astype(BF16) for name in GROUPS[group]]

    gathered0 = _gather_weights(my_blocks(0), "group0")

    def weights_of(group, landed):
        return _full_weights(gathered0 if group == 0 else landed, GROUPS[group])

    def ship(kind, which, grads=None):
        if kind == "weights":
            return [(block, False) for block in my_blocks(which)]
        return [(_pack_full_grads(grads, which), True)]

    gains = (norm_ffn1, norm_mix, norm_ffn2, norm_final.reshape(1, D_MODEL))
    loss, dx, small, parts = _local_step(x[0], loss_target[0], gains, swa_sinks, rel_bias, weights_of, ship)

    big_outs = [{}, {}, {}, {}]
    for names, tile in zip(SUM_GROUPS, SUM_TILE):
        landed = parts[names]
        if isinstance(landed, dict):
            landed = _exchange_grads(_pack_full_grads(landed, names), names[0])
        res = _sum_and_adamw(landed, _pack_shards(w, names), _pack_shards(m, names), _pack_shards(v, names),
                             tile, names[0])
        for acc, packed in zip(big_outs, res):
            acc.update(_unpack_shards(packed, names))
    g_big, d_big, m_big, v_big = big_outs

    small_part = _pack_small(small["gains"] + (small["sinks"], small["rel_bias"], loss))
    zero = jnp.zeros((1,), F32)
    small_res = _small_allreduce_adamw(
        small_part, _pack_small([w[n] for n in _SMALL_NAMES] + [zero]), _pack_small([m[n] for n in _SMALL_NAMES] + [zero]),
        _pack_small([v[n] for n in _SMALL_NAMES] + [zero]))
    g_sm, d_sm, m_sm, v_sm = (_unpack_small(p) for p in small_res)

    outs = [small_res[0][len(_SMALL_NAMES), 0], dx[None]]
    for big_d, small_d in ((g_big, g_sm), (d_big, d_sm), (m_big, m_sm), (v_big, v_sm)):
        merged = {**big_d, **small_d}
        outs += [merged[n] for n in ALL_NAMES]
    return tuple(outs)
```

```python
import functools

import jax
import jax.numpy as jnp
import numpy as np
from jax import lax
from jax.experimental import pallas as pl
from jax.experimental.pallas import tpu as pltpu

F32 = jnp.float32
BF16 = jnp.bfloat16

D_MODEL = 1024
D_FF = 2816
HEAD_DIM = 64
N_HEADS = 8
SWA_KV_HEADS = 2
SWA_GROUP = 4
SWA_BLOCK = 128
REL_BUCKETS = 32
REL_MAX_DIST = 128
RMS_EPS = 1e-6
NEG_BIG = -1e30
Q_SCALE = HEAD_DIM ** -0.5
LANES = 128

N_DEV = 8

ADAM_LR = 0.001
ADAM_B1 = 0.9
ADAM_B2 = 0.999
ADAM_EPS = 1e-08
ADAM_WD = 0.01
ADAM_STEP = 10

IN_SIZES = (512, 128, 128, 512, 512, 512, 1024, 1024)
IN_OFFS = tuple(int(v) for v in np.cumsum((0,) + IN_SIZES))
IN_W = IN_OFFS[-1]

BIG_NAMES = ("ffn1_w1", "ffn1_w3", "ffn1_w2", "w_in", "w_branch_swa", "w_branch_sb", "w_out",
             "ffn2_w1", "ffn2_w3", "ffn2_w2")
BIG_ROWS = (352, 352, 352, 544, 64, 64, 128, 352, 352, 352)
SMALL_ROWS = 8
GROUPS = (BIG_NAMES[0:3], BIG_NAMES[3:7], BIG_NAMES[7:10])
SUM_GROUPS = tuple((n,) for n in GROUPS[0]) + (GROUPS[1],) + tuple((n,) for n in GROUPS[2])
SUM_TILE = (176, 176, 176, 160, 176, 176, 176)

VMEM_LIMIT = 56 * 1024 * 1024
FFN_PIECES = 2
SB_QUERIES = 512
SB_KEYS = 256
SB_ROWS = 256
SB_SLOTS = 3
SB_SUM_PARTS = 1
SB_LOGIT_CAP = 80.0
SB_DEAD_CARRY = -110.0


def _dot(a, b):
    return jnp.dot(a, b, preferred_element_type=F32)


def _dot_nt(a, b):
    return lax.dot_general(a, b, (((1,), (1,)), ((), ())), preferred_element_type=F32)


def _dot_tn(a, b):
    return lax.dot_general(a, b, (((0,), (0,)), ((), ())), preferred_element_type=F32)


def _cparams(*sem):
    return pltpu.CompilerParams(dimension_semantics=sem, vmem_limit_bytes=VMEM_LIMIT)


def _rms_rstd(xv):
    return lax.rsqrt(jnp.mean(xv * xv, axis=-1, keepdims=True) + RMS_EPS)


def _rms_bwd(dh, xv, r, g):
    xhat = xv * r
    dg = jnp.sum(dh * xhat, axis=0, keepdims=True)
    dxn = dh * g
    dx = r * (dxn - xhat * jnp.mean(dxn * xhat, axis=-1, keepdims=True))
    return dx, dg


def _ffn_fwd(x, g, w1t, w3t, w2, tag, comm=None):
    s_len = x.shape[0]
    tm, tf = min(1024, s_len), 256
    nf = D_FF // tf

    def body(x_ref, g_ref, w1_ref, w3_ref, w2_ref, xo_ref, h_ref, a_ref, b_ref, u_ref, acc_ref, hs_ref):
        j = pl.program_id(1)

        @pl.when(j == 0)
        def _():
            xv = x_ref[...]
            h = (xv * _rms_rstd(xv) * g_ref[...]).astype(BF16)
            hs_ref[...] = h
            h_ref[...] = h
            acc_ref[...] = jnp.zeros_like(acc_ref)

        st = {}

        def s_up(rs):
            h = hs_ref[rs, :]
            st[rs.start, "ab"] = (_dot_nt(h, w1_ref[...]), _dot_nt(h, w3_ref[...]))

        def s_act(rs):
            a, b = st.pop((rs.start, "ab"))
            a_ref[rs, :] = a.astype(BF16)
            b_ref[rs, :] = b.astype(BF16)
            uh = (0.5 * (a * jax.nn.sigmoid(a) * b)).astype(BF16)
            u_ref[rs, :] = uh
            st[rs.start, "u"] = uh

        def s_down(rs):
            acc_ref[rs, :] += _dot(st.pop((rs.start, "u")), w2_ref[...])

        _emit_skewed(([slice(r, r + tm // FFN_PIECES) for r in range(0, tm, tm // FFN_PIECES)], [s_up, s_act, s_down]))

        @pl.when(j == nf - 1)
        def _():
            xo_ref[...] = x_ref[...] + acc_ref[...]

    row = lambda i, j: (i, 0)
    return _call(
        body, (x, g, w1t, w3t, w2), comm=comm, **_grid_ends(s_len // tm, nf), name=f"ffn_fwd_{tag}",
        grid=(s_len // tm, nf),
        in_specs=[pl.BlockSpec((tm, D_MODEL), row), pl.BlockSpec((1, D_MODEL), lambda i, j: (0, 0)),
                  pl.BlockSpec((tf, D_MODEL), lambda i, j: (j, 0)), pl.BlockSpec((tf, D_MODEL), lambda i, j: (j, 0)),
                  pl.BlockSpec((tf, D_MODEL), lambda i, j: (j, 0))],
        out_specs=[pl.BlockSpec((tm, D_MODEL), row), pl.BlockSpec((tm, D_MODEL), row),
                   pl.BlockSpec((tm, tf), lambda i, j: (i, j)), pl.BlockSpec((tm, tf), lambda i, j: (i, j)),
                   pl.BlockSpec((tm, tf), lambda i, j: (i, j))],
        out_shape=[jax.ShapeDtypeStruct((s_len, D_MODEL), F32), jax.ShapeDtypeStruct((s_len, D_MODEL), BF16),
                   jax.ShapeDtypeStruct((s_len, D_FF), BF16), jax.ShapeDtypeStruct((s_len, D_FF), BF16),
                   jax.ShapeDtypeStruct((s_len, D_FF), BF16)],
        scratch_shapes=[pltpu.VMEM((tm, D_MODEL), F32), pltpu.VMEM((tm, D_MODEL), BF16)],
        compiler_params=_cparams("arbitrary", "arbitrary"),
    )


def _ffn_bwd(dy, x, g, a, b, w1t, w3t, w2, tag, comm=None):
    s_len = x.shape[0]
    tm, tf = min(1024, s_len), 256
    nf = D_FF // tf

    def body(dy_ref, x_ref, g_ref, a_ref, b_ref, w1_ref, w3_ref, w2_ref,
             dx_ref, dg_ref, da_ref, db_ref, dyb_ref, acc_ref, dys_ref):
        i, j = pl.program_id(0), pl.program_id(1)

        @pl.when(j == 0)
        def _():
            dyb = dy_ref[...].astype(BF16)
            dys_ref[...] = 0.5 * dyb
            dyb_ref[...] = dyb
            acc_ref[...] = jnp.zeros_like(acc_ref)

        @pl.when((i == 0) & (j == 0))
        def _():
            dg_ref[...] = jnp.zeros_like(dg_ref)

        st = {}

        def s_du(rs):
            st[rs.start, "du"] = _dot_nt(dys_ref[rs, :], w2_ref[...])

        def s_act(rs):
            du = st.pop((rs.start, "du"))
            av = a_ref[rs, :].astype(F32)
            bv = b_ref[rs, :].astype(F32)
            sg = jax.nn.sigmoid(av)
            sil = av * sg
            da = (du * bv * (sg + sil * (1.0 - sg))).astype(BF16)
            db = (du * sil).astype(BF16)
            da_ref[rs, :] = da
            db_ref[rs, :] = db
            st[rs.start, "dab"] = (da, db)

        def s_dh(rs):
            da, db = st.pop((rs.start, "dab"))
            acc_ref[rs, :] += _dot(da, w1_ref[...]) + _dot(db, w3_ref[...])

        _emit_skewed(([slice(r, r + tm // FFN_PIECES) for r in range(0, tm, tm // FFN_PIECES)], [s_du, s_act, s_dh]))

        @pl.when(j == nf - 1)
        def _():
            xv = x_ref[...]
            dx, dg = _rms_bwd(acc_ref[...], xv, _rms_rstd(xv), g_ref[...])
            dx_ref[...] = dy_ref[...] + dx
            dg_ref[...] += dg

    row = lambda i, j: (i, 0)
    blk = lambda i, j: (i, j)
    wsp = pl.BlockSpec((tf, D_MODEL), lambda i, j: (j, 0))
    return _call(
        body, (dy, x, g, a, b, w1t, w3t, w2), comm=comm, **_grid_ends(s_len // tm, nf), name=f"ffn_bwd_{tag}",
        grid=(s_len // tm, nf),
        in_specs=[pl.BlockSpec((tm, D_MODEL), row), pl.BlockSpec((tm, D_MODEL), row),
                  pl.BlockSpec((1, D_MODEL), lambda i, j: (0, 0)),
                  pl.BlockSpec((tm, tf), blk), pl.BlockSpec((tm, tf), blk), wsp, wsp, wsp],
        out_specs=[pl.BlockSpec((tm, D_MODEL), row), pl.BlockSpec((1, D_MODEL), lambda i, j: (0, 0)),
                   pl.BlockSpec((tm, tf), blk), pl.BlockSpec((tm, tf), blk), pl.BlockSpec((tm, D_MODEL), row)],
        out_shape=[jax.ShapeDtypeStruct((s_len, D_MODEL), F32), jax.ShapeDtypeStruct((1, D_MODEL), F32),
                   jax.ShapeDtypeStruct((s_len, D_FF), BF16), jax.ShapeDtypeStruct((s_len, D_FF), BF16),
                   jax.ShapeDtypeStruct((s_len, D_MODEL), BF16)],
        scratch_shapes=[pltpu.VMEM((tm, D_MODEL), F32), pltpu.VMEM((tm, D_MODEL), BF16)],
        compiler_params=_cparams("arbitrary", "arbitrary"),
    )


def _matmul_tn(lhs, rhs, tag, comm=None):
    s_len, m = lhs.shape
    n = rhs.shape[1]
    tm = min(512, s_len)
    tj = m if m <= 1024 else 1408
    assert m % tj == 0
    last_rows = s_len // tm - 1

    def body(l_ref, r_ref, o_ref, acc_ref):
        i = pl.program_id(1)

        @pl.when(i == 0)
        def _():
            acc_ref[...] = jnp.zeros_like(acc_ref)

        acc_ref[...] += _dot_tn(l_ref[...], r_ref[...])

        @pl.when(i == last_rows)
        def _():
            o_ref[...] = acc_ref[...].astype(BF16)

    res = _call(
        body, (lhs, rhs), comm=comm, **_grid_ends(m // tj, s_len // tm), name=f"matmul_tn_{tag}",
        grid=(m // tj, s_len // tm),
        in_specs=[pl.BlockSpec((tm, tj), lambda j, i: (i, j)), pl.BlockSpec((tm, n), lambda j, i: (i, 0))],
        out_specs=[pl.BlockSpec((tj, n), lambda j, i: (j, 0))],
        out_shape=[jax.ShapeDtypeStruct((m, n), BF16)],
        scratch_shapes=[pltpu.VMEM((tj, n), F32)],
        compiler_params=_cparams("arbitrary", "arbitrary"),
    )
    return res[0] if comm is None else tuple(res)


def _matmul_tn_stacked(pieces, rhs, tag):
    s_len, n = rhs.shape
    widths = [p.shape[1] for p in pieces]
    offs = [sum(widths[:k]) for k in range(len(widths) + 1)]
    tm = min(256, s_len)
    last_rows = s_len // tm - 1

    def body(*refs):
        l_refs, r_ref, o_ref, acc_ref = refs[:len(pieces)], refs[-3], refs[-2], refs[-1]
        i = pl.program_id(0)

        @pl.when(i == 0)
        def _():
            acc_ref[...] = jnp.zeros_like(acc_ref)

        rv = r_ref[...]
        for k, l_ref in enumerate(l_refs):
            acc_ref[offs[k]:offs[k + 1], :] += _dot_tn(l_ref[...], rv)

        @pl.when(i == last_rows)
        def _():
            o_ref[...] = acc_ref[...].astype(BF16)

    row = lambda i: (i, 0)
    return pl.pallas_call(
        body, name=f"matmul_tn_{tag}",
        grid=(s_len // tm,),
        in_specs=[pl.BlockSpec((tm, w), row) for w in widths] + [pl.BlockSpec((tm, n), row)],
        out_specs=pl.BlockSpec((offs[-1], n), lambda i: (0, 0)),
        out_shape=jax.ShapeDtypeStruct((offs[-1], n), BF16),
        scratch_shapes=[pltpu.VMEM((offs[-1], n), F32)],
        compiler_params=_cparams("arbitrary"),
    )(*pieces, rhs)


def _proj_fwd(x1, g, wint):
    s_len = x1.shape[0]
    tm = min(512, s_len)
    dts = (BF16, BF16, BF16, BF16, BF16, BF16, F32, F32)

    def body(x_ref, g_ref, w_ref, h_ref, *outs):
        xv = x_ref[...]
        h = (xv * _rms_rstd(xv) * g_ref[...]).astype(BF16)
        h_ref[...] = h
        for p, o_ref in enumerate(outs):
            val = _dot_nt(h, w_ref[IN_OFFS[p]:IN_OFFS[p + 1], :])
            if p == 3:
                val = val * Q_SCALE
            o_ref[...] = val.astype(dts[p])

    row = lambda i: (i, 0)
    return pl.pallas_call(
        body, name="proj_fwd",
        grid=(s_len // tm,),
        in_specs=[pl.BlockSpec((tm, D_MODEL), row), pl.BlockSpec((1, D_MODEL), lambda i: (0, 0)),
                  pl.BlockSpec((IN_W, D_MODEL), lambda i: (0, 0))],
        out_specs=[pl.BlockSpec((tm, D_MODEL), row)] + [pl.BlockSpec((tm, w), row) for w in IN_SIZES],
        out_shape=[jax.ShapeDtypeStruct((s_len, D_MODEL), BF16)]
        + [jax.ShapeDtypeStruct((s_len, w), dt) for w, dt in zip(IN_SIZES, dts)],
        compiler_params=_cparams("parallel"),
    )(x1, g, wint)


def _proj_bwd(dpieces, dx2, x1, g, wint):
    s_len = x1.shape[0]
    tm = min(512, s_len)

    def body(*refs):
        dps = refs[:8]
        dx2_ref, x_ref, g_ref, w_ref, dx_ref, dg_ref = refs[8:]

        @pl.when(pl.program_id(0) == 0)
        def _():
            dg_ref[...] = jnp.zeros_like(dg_ref)

        dh = _dot(dps[0][...], w_ref[IN_OFFS[0]:IN_OFFS[1], :])
        for p in range(1, 8):
            dh += _dot(dps[p][...], w_ref[IN_OFFS[p]:IN_OFFS[p + 1], :])
        xv = x_ref[...]
        dx, dg = _rms_bwd(dh, xv, _rms_rstd(xv), g_ref[...])
        dx_ref[...] = dx2_ref[...] + dx
        dg_ref[...] += dg

    row = lambda i: (i, 0)
    return pl.pallas_call(
        body, name="proj_bwd",
        grid=(s_len // tm,),
        in_specs=[pl.BlockSpec((tm, w), row) for w in IN_SIZES]
        + [pl.BlockSpec((tm, D_MODEL), row), pl.BlockSpec((tm, D_MODEL), row),
           pl.BlockSpec((1, D_MODEL), lambda i: (0, 0)), pl.BlockSpec((IN_W, D_MODEL), lambda i: (0, 0))],
        out_specs=[pl.BlockSpec((tm, D_MODEL), row), pl.BlockSpec((1, D_MODEL), lambda i: (0, 0))],
        out_shape=[jax.ShapeDtypeStruct((s_len, D_MODEL), F32), jax.ShapeDtypeStruct((1, D_MODEL), F32)],
        compiler_params=_cparams("arbitrary"),
    )(*dpieces, dx2, x1, g, wint)


def _merge_fwd(x1, oa, ob, ga, gb, wswa, wsb, wout):
    s_len = x1.shape[0]
    tm = min(512, s_len)

    def body(x_ref, oa_ref, ob_ref, ga_ref, gb_ref, wa_ref, wb_ref, wo_ref, xo_ref, mg_ref):
        pa = _dot(oa_ref[...], wa_ref[...])
        pb = _dot(ob_ref[...], wb_ref[...])
        mg = (jax.nn.sigmoid(ga_ref[...]) * pa + jax.nn.sigmoid(gb_ref[...]) * pb).astype(BF16)
        mg_ref[...] = mg
        xo_ref[...] = x_ref[...] + _dot(mg, wo_ref[...])

    row = lambda i: (i, 0)
    full = lambda i: (0, 0)
    return pl.pallas_call(
        body, name="merge_fwd",
        grid=(s_len // tm,),
        in_specs=[pl.BlockSpec((tm, D_MODEL), row), pl.BlockSpec((tm, 512), row), pl.BlockSpec((tm, 512), row),
                  pl.BlockSpec((tm, D_MODEL), row), pl.BlockSpec((tm, D_MODEL), row),
                  pl.BlockSpec((512, D_MODEL), full), pl.BlockSpec((512, D_MODEL), full),
                  pl.BlockSpec((D_MODEL, D_MODEL), full)],
        out_specs=[pl.BlockSpec((tm, D_MODEL), row), pl.BlockSpec((tm, D_MODEL), row)],
        out_shape=[jax.ShapeDtypeStruct((s_len, D_MODEL), F32), jax.ShapeDtypeStruct((s_len, D_MODEL), BF16)],
        compiler_params=_cparams("parallel"),
    )(x1, oa, ob, ga, gb, wswa, wsb, wout)


def _merge_bwd(dx2, oa, ob, ga, gb, wswa, wsb, wout, comm=None):
    s_len = dx2.shape[0]
    tm = min(512, s_len)

    def body(dx_ref, oa_ref, ob_ref, ga_ref, gb_ref, wa_ref, wb_ref, wo_ref,
             doa_ref, dob_ref, dga_ref, dgb_ref, dpa_ref, dpb_ref, dxb_ref):
        dxb = dx_ref[...].astype(BF16)
        dxb_ref[...] = dxb
        dmg = _dot_nt(dxb, wo_ref[...])
        for o_ref, g_ref, w_ref, do_ref, dg_ref, dp_ref in (
                (oa_ref, ga_ref, wa_ref, doa_ref, dga_ref, dpa_ref),
                (ob_ref, gb_ref, wb_ref, dob_ref, dgb_ref, dpb_ref)):
            pv = _dot(o_ref[...], w_ref[...])
            sg = jax.nn.sigmoid(g_ref[...])
            dp = (dmg * sg).astype(BF16)
            dp_ref[...] = dp
            dg_ref[...] = (dmg * pv * sg * (1.0 - sg)).astype(BF16)
            do_ref[...] = _dot_nt(dp, w_ref[...]).astype(BF16)

    row = lambda i: (i, 0)
    full = lambda i: (0, 0)
    wide = pl.BlockSpec((tm, D_MODEL), row)
    half = pl.BlockSpec((tm, 512), row)
    return _call(
        body, (dx2, oa, ob, ga, gb, wswa, wsb, wout), comm=comm, **_grid_ends(s_len // tm), name="merge_bwd",
        grid=(s_len // tm,),
        in_specs=[wide, half, half, wide, wide, pl.BlockSpec((512, D_MODEL), full),
                  pl.BlockSpec((512, D_MODEL), full), pl.BlockSpec((D_MODEL, D_MODEL), full)],
        out_specs=[half, half, wide, wide, wide, wide, wide],
        out_shape=[jax.ShapeDtypeStruct((s_len, 512), BF16)] * 2 + [jax.ShapeDtypeStruct((s_len, D_MODEL), BF16)] * 5,
        compiler_params=_cparams("arbitrary"),
    )


def _loss_fwd_bwd(x3, tgt, g):
    s_len = x3.shape[0]
    tm = min(1024, s_len)

    def body(x_ref, t_ref, g_ref, dx_ref, loss_ref, dg_ref):
        @pl.when(pl.program_id(0) == 0)
        def _():
            loss_ref[...] = jnp.zeros_like(loss_ref)
            dg_ref[...] = jnp.zeros_like(dg_ref)

        xv = x_ref[...]
        gv = g_ref[...]
        r = _rms_rstd(xv)
        err = xv * r * gv - t_ref[...]
        loss_ref[...] += 0.5 * jnp.sum(jnp.mean(err * err, axis=-1, keepdims=True), axis=0, keepdims=True)
        dx, dg = _rms_bwd(err * (1.0 / D_MODEL), xv, r, gv)
        dx_ref[...] = dx
        dg_ref[...] += dg

    row = lambda i: (i, 0)
    return pl.pallas_call(
        body, name="loss_fwd_bwd",
        grid=(s_len // tm,),
        in_specs=[pl.BlockSpec((tm, D_MODEL), row), pl.BlockSpec((tm, D_MODEL), row),
                  pl.BlockSpec((1, D_MODEL), lambda i: (0, 0))],
        out_specs=[pl.BlockSpec((tm, D_MODEL), row), pl.BlockSpec((1, 1), lambda i: (0, 0)),
                   pl.BlockSpec((1, D_MODEL), lambda i: (0, 0))],
        out_shape=[jax.ShapeDtypeStruct((s_len, D_MODEL), F32), jax.ShapeDtypeStruct((1, 1), F32),
                   jax.ShapeDtypeStruct((1, D_MODEL), F32)],
        compiler_params=_cparams("arbitrary"),
    )(x3, tgt, g)


def _rel_bucket_matrix():
    qi = jnp.arange(SWA_BLOCK)[:, None] + SWA_BLOCK
    kj = jnp.arange(2 * SWA_BLOCK)[None, :]
    dist = jnp.maximum(qi - kj, 0)
    max_exact = REL_BUCKETS // 2
    d = jnp.maximum(dist, 1).astype(F32)
    large = max_exact + (jnp.log(d / max_exact) / np.log(REL_MAX_DIST / max_exact)
                         * (REL_BUCKETS - max_exact)).astype(jnp.int32)
    large = jnp.minimum(large, REL_BUCKETS - 1)
    return jnp.where(dist < max_exact, dist, large).astype(jnp.int32)


def _swa_bias_into(bias_ref, bkt_ref, tab_ref):
    bk = bkt_ref[...]
    for h in range(N_HEADS):
        acc = jnp.zeros(bk.shape, F32)
        for bucket in range(REL_BUCKETS):
            acc = jnp.where(bk == bucket, tab_ref[bucket, h], acc)
        bias_ref[h] = acc


def _swa_valid(n):
    shape = (SWA_BLOCK, 2 * SWA_BLOCK)
    row = lax.broadcasted_iota(jnp.int32, shape, 0)
    col = lax.broadcasted_iota(jnp.int32, shape, 1)
    dist = row + SWA_BLOCK - col
    return (dist >= 0) & (dist < SWA_BLOCK) & ((col >= SWA_BLOCK) | (n > 0))


def _swa_windows(kp_ref, kc_ref, vp_ref, vc_ref):
    return (jnp.concatenate([kp_ref[...], kc_ref[...]], axis=0), jnp.concatenate([vp_ref[...], vc_ref[...]], axis=0))


def _swa_place(h):
    return slice(h // 2 * LANES, (h // 2 + 1) * LANES), h % 2, h // SWA_GROUP


def _move_half(x, src, dst):
    moved = x if src == dst else pltpu.roll(x, HEAD_DIM, 1)
    in_dst = (lax.broadcasted_iota(jnp.int32, x.shape, 1) >= HEAD_DIM) == bool(dst)
    return jnp.where(in_dst, moved, 0.0)


def _swa_probs(qk, bias, sink, valid):
    lg = jnp.where(valid, qk * Q_SCALE + bias, NEG_BIG)
    m = jnp.maximum(jnp.max(lg, axis=-1, keepdims=True), sink)
    e = jnp.exp(lg - m)
    es = jnp.exp(sink - m)
    inv = 1.0 / (jnp.sum(e, axis=-1, keepdims=True) + es)
    return e * inv, es * inv


def _swa_specs(s_len):
    blk = SWA_BLOCK
    cur = lambda n: (n, 0)
    prev = lambda n: (jnp.maximum(n - 1, 0), 0)
    kvw = SWA_KV_HEADS * HEAD_DIM
    return [pl.BlockSpec(memory_space=pltpu.SMEM), pl.BlockSpec(memory_space=pltpu.SMEM),
            pl.BlockSpec((blk, 2 * blk), lambda n: (0, 0)),
            pl.BlockSpec((blk, N_HEADS * HEAD_DIM), cur),
            pl.BlockSpec((blk, kvw), prev), pl.BlockSpec((blk, kvw), cur),
            pl.BlockSpec((blk, kvw), prev), pl.BlockSpec((blk, kvw), cur)]


def _swa_fwd(tab, sinks, bkt, q, k, v):
    s_len = q.shape[0]
    blk = SWA_BLOCK

    def body(tab_ref, sink_ref, bkt_ref, q_ref, kp_ref, kc_ref, vp_ref, vc_ref, o_ref, bias_ref):
        n = pl.program_id(0)

        @pl.when(n == 0)
        def _():
            _swa_bias_into(bias_ref, bkt_ref, tab_ref)

        valid = _swa_valid(n)
        kk, vv = _swa_windows(kp_ref, kc_ref, vp_ref, vc_ref)
        st = {}

        def s_logits(h):
            tile, mine, kv = _swa_place(h)
            st[h, "lg"] = _dot_nt(_move_half(q_ref[:, tile].astype(F32), mine, kv).astype(BF16), kk)

        def s_probs(h):
            st[h, "p"] = _swa_probs(st.pop((h, "lg")), bias_ref[h], sink_ref[0, h], valid)[0].astype(BF16)

        def s_values(h):
            tile, mine, kv = _swa_place(h)
            part = _move_half(_dot(st.pop((h, "p")), vv), kv, mine)
            if mine == 0:
                st[h + 1, "o"] = part
            else:
                o_ref[:, tile] = (st.pop((h, "o")) + part).astype(BF16)

        _emit_skewed((list(range(N_HEADS)), [s_logits, s_probs, s_values]))

    return pl.pallas_call(
        body, name="swa_fwd",
        grid=(s_len // blk,),
        in_specs=_swa_specs(s_len),
        out_specs=pl.BlockSpec((blk, N_HEADS * HEAD_DIM), lambda n: (n, 0)),
        out_shape=jax.ShapeDtypeStruct((s_len, N_HEADS * HEAD_DIM), BF16),
        scratch_shapes=[pltpu.VMEM((N_HEADS, blk, 2 * blk), F32)],
        compiler_params=_cparams("arbitrary"),
    )(tab, sinks, bkt, q, k, k, v, v)


def _swa_bwd(tab, sinks, bkt, q, k, v, do, comm=None):
    s_len = q.shape[0]
    blk = SWA_BLOCK
    nb = s_len // blk
    kvw = SWA_KV_HEADS * HEAD_DIM

    def body(tab_ref, sink_ref, bkt_ref, q_ref, kp_ref, kc_ref, vp_ref, vc_ref, do_ref,
             dq_ref, dk_ref, dv_ref, dtab_ref, dsink_ref, bias_ref, dbias_ref):
        n = pl.program_id(0)

        @pl.when(n == 0)
        def _():
            _swa_bias_into(bias_ref, bkt_ref, tab_ref)
            dbias_ref[...] = jnp.zeros_like(dbias_ref)
            dk_ref[...] = jnp.zeros_like(dk_ref)
            dv_ref[...] = jnp.zeros_like(dv_ref)
            dsink_ref[...] = jnp.zeros_like(dsink_ref)
            dtab_ref[...] = jnp.zeros_like(dtab_ref)

        valid = _swa_valid(n)
        cur_rows = pl.ds(pl.multiple_of(n * blk, blk), blk)
        prev_rows = pl.ds(pl.multiple_of(jnp.maximum(n - 1, 0) * blk, blk), blk)
        kk, vv = _swa_windows(kp_ref, kc_ref, vp_ref, vc_ref)
        st = {}

        def s_logits(h):
            tile, mine, kv = _swa_place(h)
            st[h, "q"] = _move_half(q_ref[:, tile].astype(F32), mine, kv).astype(BF16)
            st[h, "do"] = _move_half(do_ref[:, tile].astype(F32), mine, kv).astype(BF16)
            st[h, "lg"] = _dot_nt(st[h, "q"], kk)
            st[h, "dp"] = _dot_nt(st[h, "do"], vv)

        def s_probs(h):
            p, ps = _swa_probs(st.pop((h, "lg")), bias_ref[h], sink_ref[0, h], valid)
            dp = st.pop((h, "dp"))
            delta = jnp.sum(p * dp, axis=-1, keepdims=True)
            dl = p * (dp - delta)
            dsink_ref[h:h + 1, :] += jnp.broadcast_to(-jnp.sum(ps * delta, axis=0, keepdims=True), (1, LANES))
            dbias_ref[h] += dl
            st[h, "dl"], st[h, "p"] = dl.astype(BF16), p.astype(BF16)

        def s_products(h):
            tile, mine, kv = _swa_place(h)
            dlb = st.pop((h, "dl"))
            part = _move_half(Q_SCALE * _dot(dlb, kk), kv, mine)
            if mine == 0:
                st[h + 1, "dq"] = part
            else:
                dq_ref[:, tile] = (st.pop((h, "dq")) + part).astype(BF16)
            dk_win = Q_SCALE * _dot_tn(dlb, st.pop((h, "q")))
            dv_win = _dot_tn(st.pop((h, "p")), st.pop((h, "do")))
            dk_ref[prev_rows, :] += dk_win[:blk]
            dv_ref[prev_rows, :] += dv_win[:blk]
            dk_ref[cur_rows, :] += dk_win[blk:]
            dv_ref[cur_rows, :] += dv_win[blk:]

        _emit_skewed((list(range(N_HEADS)), [s_logits, s_probs, s_products]))

        @pl.when(n == nb - 1)
        def _():
            bk = bkt_ref[...]
            lane = lax.broadcasted_iota(jnp.int32, (1, LANES), 1)
            for bucket in range(REL_BUCKETS):
                rowv = jnp.zeros((1, LANES), F32)
                for h in range(N_HEADS):
                    val = jnp.sum(jnp.where(bk == bucket, dbias_ref[h], 0.0), axis=1, keepdims=True)
                    val = jnp.sum(val, axis=0, keepdims=True)
                    rowv = jnp.where(lane == h, val, rowv)
                dtab_ref[bucket:bucket + 1, :] = rowv

    return _call(
        body, (tab, sinks, bkt, q, k, k, v, v, do), comm=comm, **_grid_ends(nb), name="swa_bwd",
        grid=(nb,),
        in_specs=_swa_specs(s_len) + [pl.BlockSpec((blk, N_HEADS * HEAD_DIM), lambda n: (n, 0))],
        out_specs=[pl.BlockSpec((blk, N_HEADS * HEAD_DIM), lambda n: (n, 0)),
                   pl.BlockSpec((s_len, kvw), lambda n: (0, 0)), pl.BlockSpec((s_len, kvw), lambda n: (0, 0)),
                   pl.BlockSpec((REL_BUCKETS, LANES), lambda n: (0, 0)), pl.BlockSpec((N_HEADS, LANES), lambda n: (0, 0))],
        out_shape=[jax.ShapeDtypeStruct((s_len, N_HEADS * HEAD_DIM), BF16),
                   jax.ShapeDtypeStruct((s_len, kvw), F32), jax.ShapeDtypeStruct((s_len, kvw), F32),
                   jax.ShapeDtypeStruct((REL_BUCKETS, LANES), F32), jax.ShapeDtypeStruct((N_HEADS, LANES), F32)],
        scratch_shapes=[pltpu.VMEM((N_HEADS, blk, 2 * blk), F32), pltpu.VMEM((N_HEADS, blk, 2 * blk), F32)],
        compiler_params=_cparams("arbitrary"),
    )


def _sb_terms(z, valid):
    zc = jnp.minimum(z, SB_LOGIT_CAP)
    lk = -jnp.log(1.0 + jnp.exp(zc))
    lsz = zc + lk
    return lsz, (lk if valid is None else jnp.where(valid, lk, 0.0))


def _bf16_parts(vals):
    parts, rest = [], vals
    for n in range(SB_SUM_PARTS):
        parts.append(rest.astype(BF16))
        if n + 1 < SB_SUM_PARTS:
            rest = rest - parts[-1].astype(F32)
    return parts[0] if len(parts) == 1 else jnp.concatenate(parts, axis=1)


def _row_sum_lanes(vals):
    return jnp.broadcast_to(jnp.sum(vals, axis=-1, keepdims=True), (vals.shape[0], LANES))


def _emit_skewed(*groups):
    for step in range(max(len(items) + len(stages) - 1 for items, stages in groups)):
        for items, stages in groups:
            for s, stage in enumerate(stages):
                if 0 <= step - s < len(items) and items[step - s] is not None:
                    stage(items[step - s])


def _sb_items(edge):
    items = []
    for h in range(2):
        for r0 in range(0, SB_QUERIES, SB_ROWS):
            if edge is None or r0 >= (edge + 1) * SB_KEYS:
                items.append((h, r0, False))
            else:
                items.append((h, r0, True) if r0 + SB_ROWS - 1 > edge * SB_KEYS else None)
    return items


def _sb_valid(w, edge):
    row = lax.broadcasted_iota(jnp.int32, (SB_ROWS, SB_KEYS), 0) + w[1]
    col = lax.broadcasted_iota(jnp.int32, (SB_ROWS, SB_KEYS), 1) + edge * SB_KEYS
    return col < row


def _sb_consts(tq, tk):
    low = lax.broadcasted_iota(jnp.int32, (tq, LANES), 1) < HEAD_DIM
    row = lax.broadcasted_iota(jnp.int32, (tk, tk), 0)
    col = lax.broadcasted_iota(jnp.int32, (tk, tk), 1)
    right = (row > col).astype(BF16)
    left = (row < col).astype(BF16)
    return low, jnp.concatenate([right] * SB_SUM_PARTS, axis=0), jnp.concatenate([left] * SB_SUM_PARTS, axis=0)


def _sb_fwd(q, k, v, comm=None):
    s_len = q.shape[0]
    tq, tk, tr = SB_QUERIES, SB_KEYS, SB_ROWS
    nk, ratio = s_len // tk, tq // tk
    assert nk <= LANES

    def body(q_ref, k_ref, v_ref, o_ref, car_ref, c_ref, oacc_ref, logw_ref, lksum_ref):
        i = pl.program_id(1)
        qv = q_ref[...]
        low, tri2, _ = _sb_consts(tq, tk)
        lane = lax.broadcasted_iota(jnp.int32, (tr, LANES), 1)
        zero = jnp.zeros_like(qv)
        q_heads = (jnp.where(low, qv, zero), jnp.where(low, zero, qv))
        c_ref[...] = jnp.zeros_like(c_ref)
        oacc_ref[...] = jnp.zeros_like(oacc_ref)
        car_ref[...] = jnp.full_like(car_ref, NEG_BIG)

        def front(j, edge):
            keys = k_ref[pl.ds(pl.multiple_of(j * tk, tk), tk), :]
            slot = j % SB_SLOTS
            st = {}

            def s_logits(w):
                st[w, "z"] = _dot_nt(q_heads[w[0]][w[1]:w[1] + tr], keys)

            def s_terms(w):
                valid = _sb_valid(w, edge) if w[2] else None
                lsz, lk = _sb_terms(st.pop((w, "z")), valid)
                st[w, "parts"] = _bf16_parts(lk)
                st[w, "lsz"] = lsz if valid is None else jnp.where(valid, lsz, NEG_BIG)
                lksum_ref[slot, w[0], w[1]:w[1] + tr, :] = _row_sum_lanes(lk)

            def s_suffix(w):
                logw_ref[slot, w[0], w[1]:w[1] + tr, :] = st.pop((w, "lsz")) + _dot(st.pop((w, "parts")), tri2)

            return _sb_items(edge), [s_logits, s_terms, s_suffix]

        def back(j, edge):
            vv = v_ref[pl.ds(pl.multiple_of(j * tk, tk), tk), :]
            slot = j % SB_SLOTS
            st = {}

            def s_weights(w):
                h, rs = w[0], slice(w[1], w[1] + tr)
                c = c_ref[h, rs, :]
                st[w, "a"] = jnp.exp(logw_ref[slot, h, rs, :] + jnp.tile(c, (1, tk // LANES))).astype(BF16)
                car_ref[h, rs, :] = jnp.where(lane == j, c, car_ref[h, rs, :])
                c_ref[h, rs, :] = c + lksum_ref[slot, h, rs, :]

            def s_values(w):
                oacc_ref[w[0], w[1]:w[1] + tr, :] += _dot(st.pop((w, "a")), vv)

            return _sb_items(edge), [s_weights, s_values]

        first = i * ratio
        edge_tiles = [(first + m, m) for m in reversed(range(ratio))]

        def alive():
            return (jnp.max(c_ref[...]) >= SB_DEAD_CARRY).astype(jnp.int32)

        @pl.when(i == 0)
        def _():
            _emit_skewed(*[front(j, m) for j, m in edge_tiles])
            _emit_skewed(*[back(j, m) for j, m in edge_tiles])

        @pl.when(i > 0)
        def _():
            tiles = edge_tiles + [(first - 1, None)]
            _emit_skewed(*[front(j, m) for j, m in tiles])
            _emit_skewed(*[back(j, m) for j, m in tiles])

            @pl.when((alive() > 0) & (first >= 2))
            def _():
                _emit_skewed(front(first - 2, None))

                def step(state):
                    pending, _ = state
                    _emit_skewed(front(pending - 1, None), back(pending, None))
                    return pending - 1, alive()

                pending, live = lax.while_loop(lambda s: (s[0] > 0) & (s[1] > 0), step, (first - 2, jnp.int32(1)))

                @pl.when(live > 0)
                def _():
                    _emit_skewed(back(pending, None))

        o_ref[...] = jnp.where(low, oacc_ref[0], oacc_ref[1]).astype(BF16)

    return _call(
        body, (q, k, v), comm=comm, **_grid_ends(N_HEADS // 2, s_len // tq), name="sb_fwd",
        grid=(N_HEADS // 2, s_len // tq),
        in_specs=[pl.BlockSpec((tq, LANES), lambda p, i: (i, p)),
                  pl.BlockSpec((s_len, LANES), lambda p, i: (0, p)),
                  pl.BlockSpec((s_len, LANES), lambda p, i: (0, p))],
        out_specs=[pl.BlockSpec((tq, LANES), lambda p, i: (i, p)), pl.BlockSpec((2, tq, LANES), lambda p, i: (p, i, 0))],
        out_shape=[jax.ShapeDtypeStruct((s_len, N_HEADS * HEAD_DIM), BF16),
                   jax.ShapeDtypeStruct((N_HEADS, s_len, LANES), F32)],
        scratch_shapes=[pltpu.VMEM((2, tq, LANES), F32), pltpu.VMEM((2, tq, LANES), F32),
                        pltpu.VMEM((SB_SLOTS, 2, tq, tk), F32), pltpu.VMEM((SB_SLOTS, 2, tq, LANES), F32)],
        compiler_params=_cparams("arbitrary", "arbitrary"),
    )


def _sb_bwd(q, k, v, do, cars):
    s_len = q.shape[0]
    tq, tk, tr = SB_QUERIES, SB_KEYS, SB_ROWS
    nk, ratio = s_len // tk, tq // tk

    def body(q_ref, k_ref, v_ref, do_ref, car_ref, dq_ref, dk_ref, dv_ref,
             gleft_ref, dqacc_ref, dkacc_ref, dvacc_ref, logw_ref, lsz_ref, da_ref, a_ref, dz_ref):
        i = pl.program_id(1)

        @pl.when(i == 0)
        def _():
            dkacc_ref[...] = jnp.zeros_like(dkacc_ref)
            dvacc_ref[...] = jnp.zeros_like(dvacc_ref)

        qv = q_ref[...]
        dov = do_ref[...]
        low, tri_right2, tri_left2 = _sb_consts(tq, tk)
        lane = lax.broadcasted_iota(jnp.int32, (tr, LANES), 1)
        zero = jnp.zeros_like(qv)
        q_heads = (jnp.where(low, qv, zero), jnp.where(low, zero, qv))
        do_heads = (jnp.where(low, dov, zero), jnp.where(low, zero, dov))
        q_t = qv.astype(F32).T.astype(BF16)
        do_t = dov.astype(F32).T.astype(BF16)
        gleft_ref[...] = jnp.zeros_like(gleft_ref)
        dqacc_ref[...] = jnp.zeros_like(dqacc_ref)

        def front(j, edge):
            key_rows = pl.ds(pl.multiple_of(j * tk, tk), tk)
            keys, values = k_ref[key_rows, :], v_ref[key_rows, :]
            slot = j % SB_SLOTS
            st = {}

            def s_logits(w):
                h, rs = w[0], slice(w[1], w[1] + tr)
                st[w, "z"] = _dot_nt(q_heads[h][rs], keys)
                da_ref[slot, h, rs, :] = _dot_nt(do_heads[h][rs], values)

            def s_terms(w):
                h, rs = w[0], slice(w[1], w[1] + tr)
                valid = _sb_valid(w, edge) if w[2] else None
                lsz, lk = _sb_terms(st.pop((w, "z")), valid)
                st[w, "parts"] = _bf16_parts(lk)
                lsz = lsz if valid is None else jnp.where(valid, lsz, NEG_BIG)
                lsz_ref[slot, h, rs, :] = lsz
                st[w, "lszc"] = lsz + jnp.sum(jnp.where(lane == j, car_ref[h, rs, :], 0.0), axis=-1, keepdims=True)

            def s_suffix(w):
                logw_ref[slot, w[0], w[1]:w[1] + tr, :] = st.pop((w, "lszc")) + _dot(st.pop((w, "parts")), tri_right2)

            return _sb_items(edge), [s_logits, s_terms, s_suffix]

        def back(j, edge):
            kv = k_ref[pl.ds(pl.multiple_of(j * tk, tk), tk), :]
            slot = j % SB_SLOTS
            st = {}

            items = _sb_items(edge)
            head_rows = [[w[1] for w in items if w is not None and w[0] == h] for h in range(2)]

            def s_weights(w):
                h, rs = w[0], slice(w[1], w[1] + tr)
                a = jnp.exp(logw_ref[slot, h, rs, :])
                g = a * da_ref[slot, h, rs, :]
                a_ref[slot, h, rs, :] = a.astype(BF16)
                st[w, "g"], st[w, "parts"] = g, _bf16_parts(g)

            def s_prefix(w):
                st[w, "gs"] = _dot(st.pop((w, "parts")), tri_left2)

            def s_dz(w):
                h, rs = w[0], slice(w[1], w[1] + tr)
                g = st.pop((w, "g"))
                gleft = gleft_ref[h, rs, :]
                gsum = st.pop((w, "gs")) + jnp.tile(gleft, (1, tk // LANES))
                dz = (g - jnp.exp(lsz_ref[slot, h, rs, :]) * (g + gsum)).astype(BF16)
                st[w, "dz"] = dz
                dz_ref[slot, h, rs, :] = dz
                gleft_ref[h, rs, :] = gleft + _row_sum_lanes(g)

            def s_products(w):
                h, rs = w[0], slice(w[1], w[1] + tr)
                dqacc_ref[h, rs, :] += _dot(st.pop((w, "dz")), kv)
                if w[1] == head_rows[h][-1]:
                    feat = slice(h * HEAD_DIM, (h + 1) * HEAD_DIM)
                    hr = slice(head_rows[h][0], tq)
                    dkacc_ref[j, feat, :] += _dot(q_t[feat, hr], dz_ref[slot, h, hr, :])
                    dvacc_ref[j, feat, :] += _dot(do_t[feat, hr], a_ref[slot, h, hr, :])

            return items, [s_weights, s_prefix, s_dz, s_products]

        first = i * ratio
        tile_max = jnp.max(jnp.maximum(car_ref[0], car_ref[1]), axis=0, keepdims=True)
        start = jnp.clip(first + ratio - jnp.sum(jnp.where(tile_max >= SB_DEAD_CARRY, 1, 0)), 0, first)

        edge_tiles = [(first + m, m) for m in range(ratio)]

        @pl.when(start == first)
        def _():
            _emit_skewed(*[front(j, m) for j, m in edge_tiles])
            _emit_skewed(*[back(j, m) for j, m in edge_tiles])

        @pl.when(start == first - 1)
        def _():
            tiles = [(first - 1, None)] + edge_tiles
            _emit_skewed(*[front(j, m) for j, m in tiles])
            _emit_skewed(*[back(j, m) for j, m in tiles])

        @pl.when(start < first - 1)
        def _():
            _emit_skewed(front(start, None))

            def step(jj, carry):
                _emit_skewed(front(jj, None), back(jj - 1, None))
                return carry

            lax.fori_loop(start + 1, first, step, 0)
            _emit_skewed(front(first, 0), back(first - 1, None))
            for m in range(1, ratio):
                _emit_skewed(front(first + m, m), back(first + m - 1, m - 1))
            _emit_skewed(back(first + ratio - 1, ratio - 1))

        dq_ref[...] = (Q_SCALE * jnp.where(low, dqacc_ref[0], dqacc_ref[1])).astype(BF16)

        @pl.when(i == s_len // tq - 1)
        def _():
            for j in range(nk):
                dk_ref[j * tk:(j + 1) * tk, :] = dkacc_ref[j].T.astype(BF16)
                dv_ref[j * tk:(j + 1) * tk, :] = dvacc_ref[j].T.astype(BF16)

    qblk = pl.BlockSpec((tq, LANES), lambda p, i: (i, p))
    col_full = pl.BlockSpec((s_len, LANES), lambda p, i: (0, p))
    return pl.pallas_call(
        body, name="sb_bwd",
        grid=(N_HEADS // 2, s_len // tq),
        in_specs=[qblk, col_full, col_full, qblk, pl.BlockSpec((2, tq, LANES), lambda p, i: (p, i, 0))],
        out_specs=[qblk, col_full, col_full],
        out_shape=[jax.ShapeDtypeStruct((s_len, N_HEADS * HEAD_DIM), BF16)] * 3,
        scratch_shapes=[pltpu.VMEM((2, tq, LANES), F32), pltpu.VMEM((2, tq, LANES), F32),
                        pltpu.VMEM((nk, LANES, tk), F32), pltpu.VMEM((nk, LANES, tk), F32)]
        + [pltpu.VMEM((SB_SLOTS, 2, tq, tk), F32)] * 3 + [pltpu.VMEM((SB_SLOTS, 2, tq, tk), BF16)] * 2,
        compiler_params=_cparams("parallel", "arbitrary"),
    )(q, k, v, do, cars)


def _local_step(xs, tgt, gains, sinks, rel_bias, weights_of, ship):
    g1, gmix, g2, gfin = gains
    bkt = _rel_bucket_matrix()
    grads = {}

    def carried(outs, comm, count):
        return outs[:count], (list(outs[count:]) if comm is not None else None)

    wts = dict(weights_of(0, None))
    comm = ship("weights", 1)
    (x1, h1, a1, b1, u1), landed = carried(
        _ffn_fwd(xs, g1, wts["ffn1_w1t"], wts["ffn1_w3t"], wts["ffn1_w2"], "1", comm), comm, 5)
    wts.update(weights_of(1, landed))
    hm, qa, ka, va, qb, kb, vb, ga, gb = _proj_fwd(x1, gmix, wts["w_int"])
    oa = _swa_fwd(rel_bias, sinks, bkt, qa, ka, va)
    comm = ship("weights", 2)
    (ob, cars), landed = carried(_sb_fwd(qb, kb, vb, comm), comm, 2)
    wts.update(weights_of(2, landed))
    x2, mg = _merge_fwd(x1, oa, ob, ga, gb, wts["w_swa"], wts["w_sb"], wts["w_out"])
    x3, h3, a3, b3, u3 = _ffn_fwd(x2, g2, wts["ffn2_w1t"], wts["ffn2_w3t"], wts["ffn2_w2"], "2")
    dx3, loss, dgfin = _loss_fwd_bwd(x3, tgt, gfin)

    def grad_chain(items):
        prev = None
        for name, lhs, rhs in items:
            comm = None if prev is None else ship("grads", (prev[0],), prev[1])
            res = _matmul_tn(lhs, rhs, name, comm)
            if prev is not None:
                grads[(prev[0],)] = prev[1] if comm is None else res[1]
            prev = (name, {_GRAD_KEY[name]: res if comm is None else res[0]})
        return prev

    dx2, dg2, da3, db3, dx3b = _ffn_bwd(dx3, x2, g2, a3, b3, wts["ffn2_w1t"], wts["ffn2_w3t"], wts["ffn2_w2"], "2")
    last = grad_chain((("ffn2_w1", da3, h3), ("ffn2_w3", db3, h3), ("ffn2_w2", u3, dx3b)))
    comm = ship("grads", (last[0],), last[1])
    (doa, dob, dga, dgb, dpa, dpb, dx2b), landed = carried(
        _merge_bwd(dx2, oa, ob, ga, gb, wts["w_swa"], wts["w_sb"], wts["w_out"], comm), comm, 7)
    grads[(last[0],)] = last[1] if comm is None else landed[0]
    dqa, dka, dva, dtab, dsink = _swa_bwd(rel_bias, sinks, bkt, qa, ka, va, doa)

    big = {"w_out": _matmul_tn(mg, dx2b, "w_out"), "w_swa": _matmul_tn(oa, dpa, "w_swa"),
           "w_sb": _matmul_tn(ob, dpb, "w_sb")}
    dqb, dkb, dvb = _sb_bwd(qb, kb, vb, dob, cars)
    dpieces = (dqa, dka.astype(BF16), dva.astype(BF16), dqb, dkb, dvb, dga, dgb)
    big["w_int"] = _matmul_tn_stacked(dpieces, hm, "w_in")
    dx1, dgmix = _proj_bwd(dpieces, dx2, x1, gmix, wts["w_int"])

    comm = ship("grads", GROUPS[1], big)
    (dx0, dg1, da1, db1, dx1b), landed = carried(
        _ffn_bwd(dx1, xs, g1, a1, b1, wts["ffn1_w1t"], wts["ffn1_w3t"], wts["ffn1_w2"], "1", comm), comm, 5)
    grads[GROUPS[1]] = big if comm is None else landed[0]

    last = grad_chain((("ffn1_w1", da1, h1), ("ffn1_w3", db1, h1), ("ffn1_w2", u1, dx1b)))
    grads[(last[0],)] = last[1]

    small = {"gains": (dg1, dgmix, dg2, dgfin), "sinks": dsink[:, 0], "rel_bias": dtab[:, :N_HEADS]}
    return loss, dx0, small, grads


def _my_place():
    return lax.axis_index("x"), lax.axis_index("y"), lax.axis_index("c")


def _flip(v, bit):
    return 1 - v if bit else v


_RELATIONS = tuple((k >> 2 & 1, k >> 1 & 1, k & 1) for k in range(1, N_DEV))


def _gather_weights(blocks, tag):
    count = len(blocks)

    def body(*refs):
        x_refs, out_refs = refs[:count], refs[count:2 * count]
        send_sems, recv_sems, local_sems = refs[2 * count:]
        x, y, c = _my_place()
        me, sibling = (x, y, c), (x, y, 1 - c)
        chips = [(1 - x, y), (x, 1 - y), (1 - x, 1 - y)]

        def rows(s, px, py, pc):
            return out_refs[s].at[4 * px + 2 * py + pc]

        def copy(s, k, block, to, src=None):
            return pltpu.make_async_remote_copy(
                src_ref=rows(s, *block) if src is None else src, dst_ref=rows(s, *block),
                send_sem=send_sems.at[s, k], recv_sem=recv_sems.at[s, k],
                device_id=to, device_id_type=pl.DeviceIdType.MESH)

        mine = [pltpu.make_async_copy(x_refs[s], rows(s, *me), local_sems.at[s]) for s in range(count)]
        first, passed = [], []
        for s in range(count):
            mine[s].start()
            first.append(copy(s, 0, me, sibling, src=x_refs[s]))
            first += [copy(s, 1 + j, me, (*chip, c), src=x_refs[s]) for j, chip in enumerate(chips)]
        for cp in first:
            cp.start()
        for s in range(count):
            for j, chip in enumerate(chips):
                copy(s, 1 + j, (*chip, c), me).wait_recv()
                passed.append(copy(s, 4 + j, (*chip, c), sibling))
                passed[-1].start()
        for s in range(count):
            copy(s, 0, sibling, me).wait_recv()
            for j, chip in enumerate(chips):
                copy(s, 4 + j, (*chip, 1 - c), me).wait_recv()
        for cp in first + passed:
            cp.wait_send()
        for cp in mine:
            cp.wait()

    anywhere = pl.BlockSpec(memory_space=pl.ANY)
    return pl.pallas_call(
        body, name=f"gather_weights_{tag}",
        out_shape=[jax.ShapeDtypeStruct((N_DEV,) + b.shape, b.dtype) for b in blocks],
        in_specs=[anywhere] * count, out_specs=[anywhere] * count,
        scratch_shapes=[pltpu.SemaphoreType.DMA((count, N_DEV - 1)), pltpu.SemaphoreType.DMA((count, N_DEV - 1)),
                        pltpu.SemaphoreType.DMA((count,))],
    )(*blocks)


def _exchange_grads(gp, tag):
    def body(g_ref, out_ref, send_sems, recv_sems, local_sem):
        x, y, c = _my_place()
        me = 4 * x + 2 * y + c
        mine = pltpu.make_async_copy(g_ref.at[me], out_ref.at[me], local_sem)
        mine.start()
        copies = []
        for k, (fx, fy, fc) in enumerate(_RELATIONS):
            px, py, pc = _flip(x, fx), _flip(y, fy), _flip(c, fc)
            peer = 4 * px + 2 * py + pc
            copies.append((
                pltpu.make_async_remote_copy(
                    src_ref=g_ref.at[peer], dst_ref=out_ref.at[me], send_sem=send_sems.at[k], recv_sem=recv_sems.at[k],
                    device_id=(px, py, pc), device_id_type=pl.DeviceIdType.MESH),
                pltpu.make_async_remote_copy(
                    src_ref=g_ref.at[peer], dst_ref=out_ref.at[peer], send_sem=send_sems.at[k], recv_sem=recv_sems.at[k],
                    device_id=(px, py, pc), device_id_type=pl.DeviceIdType.MESH)))
        for out_cp, _ in copies:
            out_cp.start()
        for _, in_cp in copies:
            in_cp.wait_recv()
        for out_cp, _ in copies:
            out_cp.wait_send()
        mine.wait()

    return pl.pallas_call(
        body, name=f"exchange_grads_{tag}",
        out_shape=jax.ShapeDtypeStruct(gp.shape, gp.dtype),
        in_specs=[pl.BlockSpec(memory_space=pl.ANY)],
        out_specs=pl.BlockSpec(memory_space=pl.ANY),
        scratch_shapes=[pltpu.SemaphoreType.DMA((7,)), pltpu.SemaphoreType.DMA((7,)), pltpu.SemaphoreType.DMA(())],
    )(gp)


def _peers():
    x, y, c = _my_place()
    out = []
    for k, (fx, fy, fc) in enumerate(_RELATIONS):
        px, py, pc = _flip(x, fx), _flip(y, fy), _flip(c, fc)
        out.append((k, (px, py, pc), 4 * px + 2 * py + pc))
    return out, 4 * x + 2 * y + c


def _grid_ends(*grid):
    def first():
        return functools.reduce(lambda a, b: a & b, [pl.program_id(d) == 0 for d in range(len(grid))])

    def last():
        return functools.reduce(lambda a, b: a & b, [pl.program_id(d) == n - 1 for d, n in enumerate(grid)])

    return {"first": first, "last": last}


def _call(body, operands, *, comm=None, first=None, last=None, **kw):
    if comm is None:
        return pl.pallas_call(body, **kw)(*operands)
    in_specs, out_specs, out_shape = list(kw.pop("in_specs")), list(kw.pop("out_specs")), list(kw.pop("out_shape"))
    scratch = list(kw.pop("scratch_shapes", ()))
    n_in, n_out, n_scr, n_src = len(in_specs), len(out_specs), len(scratch), len(comm)

    def wrapped(*refs):
        ins, src_refs = refs[:n_in], refs[n_in:n_in + n_src]
        outs = refs[n_in + n_src:n_in + n_src + n_out]
        land_refs = refs[n_in + n_src + n_out:n_in + 2 * n_src + n_out]
        scr = refs[n_in + 2 * n_src + n_out:n_in + 2 * n_src + n_out + n_scr]
        send_sems, recv_sems, local_sems = refs[n_in + 2 * n_src + n_out + n_scr:]
        peers, me = _peers()
        mine, going, coming = [], [], []
        for s, (_, per_peer) in enumerate(comm):
            src_ref, land_ref = src_refs[s], land_refs[s]
            mine.append(pltpu.make_async_copy(src_ref.at[me] if per_peer else src_ref, land_ref.at[me], local_sems.at[s]))
            for k, where, slab in peers:
                piece = src_ref.at[slab] if per_peer else src_ref
                going.append(pltpu.make_async_remote_copy(
                    src_ref=piece, dst_ref=land_ref.at[me], send_sem=send_sems.at[s, k], recv_sem=recv_sems.at[s, k],
                    device_id=where, device_id_type=pl.DeviceIdType.MESH))
                coming.append(pltpu.make_async_remote_copy(
                    src_ref=piece, dst_ref=land_ref.at[slab], send_sem=send_sems.at[s, k], recv_sem=recv_sems.at[s, k],
                    device_id=where, device_id_type=pl.DeviceIdType.MESH))

        @pl.when(first())
        def _():
            for cp in mine + going:
                cp.start()

        body(*ins, *outs, *scr)

        @pl.when(last())
        def _():
            for cp in coming:
                cp.wait_recv()
            for cp in going:
                cp.wait_send()
            for cp in mine:
                cp.wait()

    anywhere = pl.BlockSpec(memory_space=pl.ANY)
    lands = [jax.ShapeDtypeStruct(src.shape if per_peer else (N_DEV,) + src.shape, src.dtype) for src, per_peer in comm]
    return pl.pallas_call(
        wrapped, in_specs=in_specs + [anywhere] * n_src, out_specs=out_specs + [anywhere] * n_src,
        out_shape=out_shape + lands,
        scratch_shapes=scratch + [pltpu.SemaphoreType.DMA((n_src, N_DEV - 1)), pltpu.SemaphoreType.DMA((n_src, N_DEV - 1)),
                                  pltpu.SemaphoreType.DMA((n_src,))],
        **kw)(*operands, *[src for src, _ in comm])


def _adamw(w, g, m, v):
    m = ADAM_B1 * m + (1.0 - ADAM_B1) * g
    v = ADAM_B2 * v + (1.0 - ADAM_B2) * jnp.square(g)
    m_hat = m / (1.0 - ADAM_B1 ** ADAM_STEP)
    v_hat = v / (1.0 - ADAM_B2 ** ADAM_STEP)
    delta = -ADAM_LR * (m_hat / (jnp.sqrt(v_hat) + ADAM_EPS) + ADAM_WD * w)
    return delta, m, v


def _sum_and_adamw(parts, w, m, v, tr, tag):
    rows = w.shape[0]
    assert rows % tr == 0

    def body(p_ref, w_ref, m_ref, v_ref, g_out, d_out, m_out, v_out):
        g = p_ref[0].astype(F32)
        for d in range(1, N_DEV):
            g = g + p_ref[d].astype(F32)
        delta, mn, vn = _adamw(w_ref[...], g, m_ref[...], v_ref[...])
        g_out[...] = g
        d_out[...] = delta
        m_out[...] = mn
        v_out[...] = vn

    sp = pl.BlockSpec((tr, D_MODEL), lambda i: (i, 0))
    return pl.pallas_call(
        body, name=f"sum_and_adamw_{tag}",
        grid=(rows // tr,),
        in_specs=[pl.BlockSpec((N_DEV, tr, D_MODEL), lambda i: (0, i, 0)), sp, sp, sp],
        out_specs=[sp] * 4,
        out_shape=[jax.ShapeDtypeStruct(w.shape, F32)] * 4,
        compiler_params=_cparams("parallel"),
    )(parts, w, m, v)


def _small_allreduce_adamw(part, w, m, v):
    def body(p_ref, w_ref, m_ref, v_ref, g_out, d_out, m_out, v_out, buf, send_sems, recv_sems):
        x, y, c = _my_place()
        me = 4 * x + 2 * y + c
        buf[me] = p_ref[...]
        copies = []
        for k, (fx, fy, fc) in enumerate(_RELATIONS):
            px, py, pc = _flip(x, fx), _flip(y, fy), _flip(c, fc)
            peer = 4 * px + 2 * py + pc
            copies.append((
                pltpu.make_async_remote_copy(
                    src_ref=buf.at[me], dst_ref=buf.at[me], send_sem=send_sems.at[k], recv_sem=recv_sems.at[k],
                    device_id=(px, py, pc), device_id_type=pl.DeviceIdType.MESH),
                pltpu.make_async_remote_copy(
                    src_ref=buf.at[me], dst_ref=buf.at[peer], send_sem=send_sems.at[k], recv_sem=recv_sems.at[k],
                    device_id=(px, py, pc), device_id_type=pl.DeviceIdType.MESH)))
        for out_cp, _ in copies:
            out_cp.start()
        for _, in_cp in copies:
            in_cp.wait_recv()
        for out_cp, _ in copies:
            out_cp.wait_send()
        g = buf[0]
        for d in range(1, N_DEV):
            g = g + buf[d]
        delta, mn, vn = _adamw(w_ref[...], g, m_ref[...], v_ref[...])
        g_out[...] = g
        d_out[...] = delta
        m_out[...] = mn
        v_out[...] = vn

    vm = pl.BlockSpec(memory_space=pltpu.VMEM)
    return pl.pallas_call(
        body, name="small_allreduce_adamw",
        in_specs=[vm] * 4, out_specs=[vm] * 4,
        out_shape=[jax.ShapeDtypeStruct(w.shape, F32)] * 4,
        scratch_shapes=[pltpu.VMEM((N_DEV,) + part.shape, F32),
                        pltpu.SemaphoreType.DMA((7,)), pltpu.SemaphoreType.DMA((7,))],
    )(part, w, m, v)


_TRANSPOSED = ("ffn1_w1", "ffn1_w3", "w_in", "ffn2_w1", "ffn2_w3")
_BRANCH = ("w_branch_swa", "w_branch_sb")


def _pack_shards(t, names):
    parts = []
    for name in names:
        a = t[name][0]
        if name in _TRANSPOSED:
            a = a.T
        elif name in _BRANCH:
            a = a.reshape(64, D_MODEL)
        parts.append(a)
    return jnp.concatenate(parts, axis=0)


def _unpack_shards(p, names):
    out, lo = {}, 0
    for name in names:
        a = p[lo:lo + BIG_ROWS[BIG_NAMES.index(name)]]
        lo += a.shape[0]
        if name in _TRANSPOSED:
            a = a.T
        elif name in _BRANCH:
            a = a.reshape(512, 128)
        out[name] = a[None]
    return out


def _full_weights(zones, names):
    out = {}
    for name, a in zip(names, zones):
        if name in _BRANCH:
            a = a.reshape(N_DEV, 512, 128).transpose(1, 0, 2).reshape(512, D_MODEL)
        out[_GRAD_KEY[name]] = a.reshape(-1, D_MODEL)
    return out


_GRAD_KEY = {"ffn1_w1": "ffn1_w1t", "ffn1_w3": "ffn1_w3t", "ffn1_w2": "ffn1_w2", "w_in": "w_int",
             "w_branch_swa": "w_swa", "w_branch_sb": "w_sb", "w_out": "w_out",
             "ffn2_w1": "ffn2_w1t", "ffn2_w3": "ffn2_w3t", "ffn2_w2": "ffn2_w2"}


def _pack_full_grads(big, names):
    parts = []
    for name in names:
        a = big[_GRAD_KEY[name]]
        if name in _BRANCH:
            a = a.reshape(512, N_DEV, 128).transpose(1, 0, 2)
        parts.append(a.reshape(N_DEV, BIG_ROWS[BIG_NAMES.index(name)], D_MODEL).astype(BF16))
    return jnp.concatenate(parts, axis=1)


_SMALL_NAMES = ("norm_ffn1", "norm_mix", "norm_ffn2", "norm_final", "swa_sinks", "rel_bias")


def _pack_small(vals):
    rows = []
    for a in vals:
        a = a.reshape(-1)
        rows.append(jnp.pad(a, (0, D_MODEL - a.shape[0])))
    rows += [jnp.zeros((D_MODEL,), F32)] * (SMALL_ROWS - len(rows))
    return jnp.stack(rows)


def _unpack_small(p):
    return {"norm_ffn1": p[0:1], "norm_mix": p[1:2], "norm_ffn2": p[2:3], "norm_final": p[3],
            "swa_sinks": p[4:5, :N_HEADS], "rel_bias": p[5, :REL_BUCKETS * N_HEADS].reshape(REL_BUCKETS, N_HEADS)}


ALL_NAMES = ("norm_ffn1", "ffn1_w1", "ffn1_w3", "ffn1_w2", "norm_mix", "w_in", "swa_sinks", "rel_bias",
             "w_branch_swa", "w_branch_sb", "w_out", "norm_ffn2", "ffn2_w1", "ffn2_w3", "ffn2_w2", "norm_final")


def kernel(x, norm_ffn1, ffn1_w1, ffn1_w3, ffn1_w2, norm_mix, w_in, swa_sinks, rel_bias, w_branch_swa, w_branch_sb, w_out, norm_ffn2, ffn2_w1, ffn2_w3, ffn2_w2, norm_final, loss_target, m_norm_ffn1, m_ffn1_w1, m_ffn1_w3, m_ffn1_w2, m_norm_mix, m_w_in, m_swa_sinks, m_rel_bias, m_w_branch_swa, m_w_branch_sb, m_w_out, m_norm_ffn2, m_ffn2_w1, m_ffn2_w3, m_ffn2_w2, m_norm_final, v_norm_ffn1, v_ffn1_w1, v_ffn1_w3, v_ffn1_w2, v_norm_mix, v_w_in, v_swa_sinks, v_rel_bias, v_w_branch_swa, v_w_branch_sb, v_w_out, v_norm_ffn2, v_ffn2_w1, v_ffn2_w3, v_ffn2_w2, v_norm_final):
    w = dict(zip(ALL_NAMES, (norm_ffn1, ffn1_w1, ffn1_w3, ffn1_w2, norm_mix, w_in, swa_sinks, rel_bias,
                             w_branch_swa, w_branch_sb, w_out, norm_ffn2, ffn2_w1, ffn2_w3, ffn2_w2, norm_final)))
    m = dict(zip(ALL_NAMES, (m_norm_ffn1, m_ffn1_w1, m_ffn1_w3, m_ffn1_w2, m_norm_mix, m_w_in, m_swa_sinks, m_rel_bias,
                             m_w_branch_swa, m_w_branch_sb, m_w_out, m_norm_ffn2, m_ffn2_w1, m_ffn2_w3, m_ffn2_w2,
                             m_norm_final)))
    v = dict(zip(ALL_NAMES, (v_norm_ffn1, v_ffn1_w1, v_ffn1_w3, v_ffn1_w2, v_norm_mix, v_w_in, v_swa_sinks, v_rel_bias,
                             v_w_branch_swa, v_w_branch_sb, v_w_out, v_norm_ffn2, v_ffn2_w1, v_ffn2_w3, v_ffn2_w2,
                             v_norm_final)))

    def my_blocks(group):
        return [_pack_shards(w, (name,)).astype(BF16) for name in GROUPS[group]]

    gathered0 = _gather_weights(my_blocks(0), "group0")

    def weights_of(group, landed):
        return _full_weights(gathered0 if group == 0 else landed, GROUPS[group])

    def ship(kind, which, grads=None):
        if kind == "weights":
            return [(block, False) for block in my_blocks(which)]
        return [(_pack_full_grads(grads, which), True)]

    gains = (norm_ffn1, norm_mix, norm_ffn2, norm_final.reshape(1, D_MODEL))
    loss, dx, small, parts = _local_step(x[0], loss_target[0], gains, swa_sinks, rel_bias, weights_of, ship)

    big_outs = [{}, {}, {}, {}]
    for names, tile in zip(SUM_GROUPS, SUM_TILE):
        landed = parts[names]
        if isinstance(landed, dict):
            landed = _exchange_grads(_pack_full_grads(landed, names), names[0])
        res = _sum_and_adamw(landed, _pack_shards(w, names), _pack_shards(m, names), _pack_shards(v, names),
                             tile, names[0])
        for acc, packed in zip(big_outs, res):
            acc.update(_unpack_shards(packed, names))
    g_big, d_big, m_big, v_big = big_outs

    small_part = _pack_small(small["gains"] + (small["sinks"], small["rel_bias"], loss))
    zero = jnp.zeros((1,), F32)
    small_res = _small_allreduce_adamw(
        small_part, _pack_small([w[n] for n in _SMALL_NAMES] + [zero]), _pack_small([m[n] for n in _SMALL_NAMES] + [zero]),
        _pack_small([v[n] for n in _SMALL_NAMES] + [zero]))
    g_sm, d_sm, m_sm, v_sm = (_unpack_small(p) for p in small_res)

    outs = [small_res[0][len(_SMALL_NAMES), 0], dx[None]]
    for big_d, small_d in ((g_big, g_sm), (d_big, d_sm), (m_big, m_sm), (v_big, v_sm)):
        merged = {**big_d, **small_d}
        outs += [merged[n] for n in ALL_NAMES]
    return tuple(outs)
```

```python
import functools

import jax
import jax.numpy as jnp
import numpy as np
from jax import lax
from jax.experimental import pallas as pl
from jax.experimental.pallas import tpu as pltpu

F32 = jnp.float32
BF16 = jnp.bfloat16

D_MODEL = 1024
D_FF = 2816
HEAD_DIM = 64
N_HEADS = 8
SWA_KV_HEADS = 2
SWA_GROUP = 4
SWA_BLOCK = 128
REL_BUCKETS = 32
REL_MAX_DIST = 128
RMS_EPS = 1e-6
NEG_BIG = -1e30
Q_SCALE = HEAD_DIM ** -0.5
LANES = 128

N_DEV = 8

ADAM_LR = 0.001
ADAM_B1 = 0.9
ADAM_B2 = 0.999
ADAM_EPS = 1e-08
ADAM_WD = 0.01
ADAM_STEP = 10

IN_SIZES = (512, 128, 128, 512, 512, 512, 1024, 1024)
IN_OFFS = tuple(int(v) for v in np.cumsum((0,) + IN_SIZES))
IN_W = IN_OFFS[-1]

BIG_NAMES = ("ffn1_w1", "ffn1_w3", "ffn1_w2", "w_in", "w_branch_swa", "w_branch_sb", "w_out",
             "ffn2_w1", "ffn2_w3", "ffn2_w2")
BIG_ROWS = (352, 352, 352, 544, 64, 64, 128, 352, 352, 352)
SMALL_ROWS = 8
GROUPS = (BIG_NAMES[0:3], BIG_NAMES[3:7], BIG_NAMES[7:10])
SUM_GROUPS = tuple((n,) for n in GROUPS[0]) + (GROUPS[1],) + tuple((n,) for n in GROUPS[2])
SUM_TILE = (176, 176, 176, 160, 176, 176, 176)

VMEM_LIMIT = 56 * 1024 * 1024
FFN_PIECES = 2
SB_QUERIES = 512
SB_KEYS = 256
SB_ROWS = 256
SB_SLOTS = 3
SB_SUM_PARTS = 1
SB_LOGIT_CAP = 80.0
SB_DEAD_CARRY = -110.0


def _dot(a, b):
    return jnp.dot(a, b, preferred_element_type=F32)


def _dot_nt(a, b):
    return lax.dot_general(a, b, (((1,), (1,)), ((), ())), preferred_element_type=F32)


def _dot_tn(a, b):
    return lax.dot_general(a, b, (((0,), (0,)), ((), ())), preferred_element_type=F32)


def _cparams(*sem):
    return pltpu.CompilerParams(dimension_semantics=sem, vmem_limit_bytes=VMEM_LIMIT)


def _rms_rstd(xv):
    return lax.rsqrt(jnp.mean(xv * xv, axis=-1, keepdims=True) + RMS_EPS)


def _rms_bwd(dh, xv, r, g):
    xhat = xv * r
    dg = jnp.sum(dh * xhat, axis=0, keepdims=True)
    dxn = dh * g
    dx = r * (dxn - xhat * jnp.mean(dxn * xhat, axis=-1, keepdims=True))
    return dx, dg


def _ffn_fwd(x, g, w1t, w3t, w2, tag, comm=None):
    s_len = x.shape[0]
    tm, tf = min(1024, s_len), 256
    nf = D_FF // tf

    def body(x_ref, g_ref, w1_ref, w3_ref, w2_ref, xo_ref, h_ref, a_ref, b_ref, u_ref, acc_ref, hs_ref):
        j = pl.program_id(1)

        @pl.when(j == 0)
        def _():
            xv = x_ref[...]
            h = (xv * _rms_rstd(xv) * g_ref[...]).astype(BF16)
            hs_ref[...] = h
            h_ref[...] = h
            acc_ref[...] = jnp.zeros_like(acc_ref)

        st = {}

        def s_up(rs):
            h = hs_ref[rs, :]
            st[rs.start, "ab"] = (_dot_nt(h, w1_ref[...]), _dot_nt(h, w3_ref[...]))

        def s_act(rs):
            a, b = st.pop((rs.start, "ab"))
            a_ref[rs, :] = a.astype(BF16)
            b_ref[rs, :] = b.astype(BF16)
            uh = (0.5 * (a * jax.nn.sigmoid(a) * b)).astype(BF16)
            u_ref[rs, :] = uh
            st[rs.start, "u"] = uh

        def s_down(rs):
            acc_ref[rs, :] += _dot(st.pop((rs.start, "u")), w2_ref[...])

        _emit_skewed(([slice(r, r + tm // FFN_PIECES) for r in range(0, tm, tm // FFN_PIECES)], [s_up, s_act, s_down]))

        @pl.when(j == nf - 1)
        def _():
            xo_ref[...] = x_ref[...] + acc_ref[...]

    row = lambda i, j: (i, 0)
    return _call(
        body, (x, g, w1t, w3t, w2), comm=comm, **_grid_ends(s_len // tm, nf), name=f"ffn_fwd_{tag}",
        grid=(s_len // tm, nf),
        in_specs=[pl.BlockSpec((tm, D_MODEL), row), pl.BlockSpec((1, D_MODEL), lambda i, j: (0, 0)),
                  pl.BlockSpec((tf, D_MODEL), lambda i, j: (j, 0)), pl.BlockSpec((tf, D_MODEL), lambda i, j: (j, 0)),
                  pl.BlockSpec((tf, D_MODEL), lambda i, j: (j, 0))],
        out_specs=[pl.BlockSpec((tm, D_MODEL), row), pl.BlockSpec((tm, D_MODEL), row),
                   pl.BlockSpec((tm, tf), lambda i, j: (i, j)), pl.BlockSpec((tm, tf), lambda i, j: (i, j)),
                   pl.BlockSpec((tm, tf), lambda i, j: (i, j))],
        out_shape=[jax.ShapeDtypeStruct((s_len, D_MODEL), F32), jax.ShapeDtypeStruct((s_len, D_MODEL), BF16),
                   jax.ShapeDtypeStruct((s_len, D_FF), BF16), jax.ShapeDtypeStruct((s_len, D_FF), BF16),
                   jax.ShapeDtypeStruct((s_len, D_FF), BF16)],
        scratch_shapes=[pltpu.VMEM((tm, D_MODEL), F32), pltpu.VMEM((tm, D_MODEL), BF16)],
        compiler_params=_cparams("arbitrary", "arbitrary"),
    )


def _ffn_up(x, g, w1t, w3t, tag, comm=None):
    s_len = x.shape[0]
    tm, tf = min(1024, s_len), 256
    nf = D_FF // tf

    def body(x_ref, g_ref, w1_ref, w3_ref, h_ref, a_ref, b_ref, u_ref, hs_ref):
        @pl.when(pl.program_id(1) == 0)
        def _():
            xv = x_ref[...]
            h = (xv * _rms_rstd(xv) * g_ref[...]).astype(BF16)
            hs_ref[...] = h
            h_ref[...] = h

        st = {}

        def s_up(rs):
            h = hs_ref[rs, :]
            st[rs.start, "ab"] = (_dot_nt(h, w1_ref[...]), _dot_nt(h, w3_ref[...]))

        def s_act(rs):
            a, b = st.pop((rs.start, "ab"))
            a_ref[rs, :] = a.astype(BF16)
            b_ref[rs, :] = b.astype(BF16)
            u_ref[rs, :] = (0.5 * (a * jax.nn.sigmoid(a) * b)).astype(BF16)

        _emit_skewed(([slice(r, r + tm // FFN_PIECES) for r in range(0, tm, tm // FFN_PIECES)], [s_up, s_act]))

    row = lambda i, j: (i, 0)
    blk = lambda i, j: (i, j)
    wsp = pl.BlockSpec((tf, D_MODEL), lambda i, j: (j, 0))
    return _call(
        body, (x, g, w1t, w3t), comm=comm, **_grid_ends(s_len // tm, nf), name=f"ffn_up_{tag}",
        grid=(s_len // tm, nf),
        in_specs=[pl.BlockSpec((tm, D_MODEL), row), pl.BlockSpec((1, D_MODEL), lambda i, j: (0, 0)), wsp, wsp],
        out_specs=[pl.BlockSpec((tm, D_MODEL), row), pl.BlockSpec((tm, tf), blk), pl.BlockSpec((tm, tf), blk),
                   pl.BlockSpec((tm, tf), blk)],
        out_shape=[jax.ShapeDtypeStruct((s_len, D_MODEL), BF16)] + [jax.ShapeDtypeStruct((s_len, D_FF), BF16)] * 3,
        scratch_shapes=[pltpu.VMEM((tm, D_MODEL), BF16)],
        compiler_params=_cparams("arbitrary", "arbitrary"),
    )


def _ffn_down(x, u, w2, tag, comm=None):
    s_len = x.shape[0]
    tm = min(512, s_len)

    def body(x_ref, u_ref, w2_ref, xo_ref):
        xo_ref[...] = x_ref[...] + _dot(u_ref[...], w2_ref[...])

    row = lambda i: (i, 0)
    return _call(
        body, (x, u, w2), comm=comm, **_grid_ends(s_len // tm), name=f"ffn_down_{tag}",
        grid=(s_len // tm,),
        in_specs=[pl.BlockSpec((tm, D_MODEL), row), pl.BlockSpec((tm, D_FF), row),
                  pl.BlockSpec((D_FF, D_MODEL), lambda i: (0, 0))],
        out_specs=[pl.BlockSpec((tm, D_MODEL), row)],
        out_shape=[jax.ShapeDtypeStruct((s_len, D_MODEL), F32)],
        compiler_params=_cparams("arbitrary"),
    )


def _ffn_bwd(dy, x, g, a, b, w1t, w3t, w2, tag, comm=None):
    s_len = x.shape[0]
    tm, tf = min(1024, s_len), 256
    nf = D_FF // tf

    def body(dy_ref, x_ref, g_ref, a_ref, b_ref, w1_ref, w3_ref, w2_ref,
             dx_ref, dg_ref, da_ref, db_ref, dyb_ref, acc_ref, dys_ref):
        i, j = pl.program_id(0), pl.program_id(1)

        @pl.when(j == 0)
        def _():
            dyb = dy_ref[...].astype(BF16)
            dys_ref[...] = 0.5 * dyb
            dyb_ref[...] = dyb
            acc_ref[...] = jnp.zeros_like(acc_ref)

        @pl.when((i == 0) & (j == 0))
        def _():
            dg_ref[...] = jnp.zeros_like(dg_ref)

        st = {}

        def s_du(rs):
            st[rs.start, "du"] = _dot_nt(dys_ref[rs, :], w2_ref[...])

        def s_act(rs):
            du = st.pop((rs.start, "du"))
            av = a_ref[rs, :].astype(F32)
            bv = b_ref[rs, :].astype(F32)
            sg = jax.nn.sigmoid(av)
            sil = av * sg
            da = (du * bv * (sg + sil * (1.0 - sg))).astype(BF16)
            db = (du * sil).astype(BF16)
            da_ref[rs, :] = da
            db_ref[rs, :] = db
            st[rs.start, "dab"] = (da, db)

        def s_dh(rs):
            da, db = st.pop((rs.start, "dab"))
            acc_ref[rs, :] += _dot(da, w1_ref[...]) + _dot(db, w3_ref[...])

        _emit_skewed(([slice(r, r + tm // FFN_PIECES) for r in range(0, tm, tm // FFN_PIECES)], [s_du, s_act, s_dh]))

        @pl.when(j == nf - 1)
        def _():
            xv = x_ref[...]
            dx, dg = _rms_bwd(acc_ref[...], xv, _rms_rstd(xv), g_ref[...])
            dx_ref[...] = dy_ref[...] + dx
            dg_ref[...] += dg

    row = lambda i, j: (i, 0)
    blk = lambda i, j: (i, j)
    wsp = pl.BlockSpec((tf, D_MODEL), lambda i, j: (j, 0))
    return _call(
        body, (dy, x, g, a, b, w1t, w3t, w2), comm=comm, **_grid_ends(s_len // tm, nf), name=f"ffn_bwd_{tag}",
        grid=(s_len // tm, nf),
        in_specs=[pl.BlockSpec((tm, D_MODEL), row), pl.BlockSpec((tm, D_MODEL), row),
                  pl.BlockSpec((1, D_MODEL), lambda i, j: (0, 0)),
                  pl.BlockSpec((tm, tf), blk), pl.BlockSpec((tm, tf), blk), wsp, wsp, wsp],
        out_specs=[pl.BlockSpec((tm, D_MODEL), row), pl.BlockSpec((1, D_MODEL), lambda i, j: (0, 0)),
                   pl.BlockSpec((tm, tf), blk), pl.BlockSpec((tm, tf), blk), pl.BlockSpec((tm, D_MODEL), row)],
        out_shape=[jax.ShapeDtypeStruct((s_len, D_MODEL), F32), jax.ShapeDtypeStruct((1, D_MODEL), F32),
                   jax.ShapeDtypeStruct((s_len, D_FF), BF16), jax.ShapeDtypeStruct((s_len, D_FF), BF16),
                   jax.ShapeDtypeStruct((s_len, D_MODEL), BF16)],
        scratch_shapes=[pltpu.VMEM((tm, D_MODEL), F32), pltpu.VMEM((tm, D_MODEL), BF16)],
        compiler_params=_cparams("arbitrary", "arbitrary"),
    )


def _matmul_tn(lhs, rhs, tag, comm=None):
    s_len, m = lhs.shape
    n = rhs.shape[1]
    tm = min(512, s_len)
    tj = m if m <= 1024 else 1408
    assert m % tj == 0
    last_rows = s_len // tm - 1

    def body(l_ref, r_ref, o_ref, acc_ref):
        i = pl.program_id(1)

        @pl.when(i == 0)
        def _():
            acc_ref[...] = jnp.zeros_like(acc_ref)

        acc_ref[...] += _dot_tn(l_ref[...], r_ref[...])

        @pl.when(i == last_rows)
        def _():
            o_ref[...] = acc_ref[...].astype(BF16)

    res = _call(
        body, (lhs, rhs), comm=comm, **_grid_ends(m // tj, s_len // tm), name=f"matmul_tn_{tag}",
        grid=(m // tj, s_len // tm),
        in_specs=[pl.BlockSpec((tm, tj), lambda j, i: (i, j)), pl.BlockSpec((tm, n), lambda j, i: (i, 0))],
        out_specs=[pl.BlockSpec((tj, n), lambda j, i: (j, 0))],
        out_shape=[jax.ShapeDtypeStruct((m, n), BF16)],
        scratch_shapes=[pltpu.VMEM((tj, n), F32)],
        compiler_params=_cparams("arbitrary", "arbitrary"),
    )
    return res[0] if comm is None else tuple(res)


def _matmul_tn_stacked(pieces, rhs, tag):
    s_len, n = rhs.shape
    widths = [p.shape[1] for p in pieces]
    offs = [sum(widths[:k]) for k in range(len(widths) + 1)]
    tm = min(256, s_len)
    last_rows = s_len // tm - 1

    def body(*refs):
        l_refs, r_ref, o_ref, acc_ref = refs[:len(pieces)], refs[-3], refs[-2], refs[-1]
        i = pl.program_id(0)

        @pl.when(i == 0)
        def _():
            acc_ref[...] = jnp.zeros_like(acc_ref)

        rv = r_ref[...]
        for k, l_ref in enumerate(l_refs):
            acc_ref[offs[k]:offs[k + 1], :] += _dot_tn(l_ref[...], rv)

        @pl.when(i == last_rows)
        def _():
            o_ref[...] = acc_ref[...].astype(BF16)

    row = lambda i: (i, 0)
    return pl.pallas_call(
        body, name=f"matmul_tn_{tag}",
        grid=(s_len // tm,),
        in_specs=[pl.BlockSpec((tm, w), row) for w in widths] + [pl.BlockSpec((tm, n), row)],
        out_specs=pl.BlockSpec((offs[-1], n), lambda i: (0, 0)),
        out_shape=jax.ShapeDtypeStruct((offs[-1], n), BF16),
        scratch_shapes=[pltpu.VMEM((offs[-1], n), F32)],
        compiler_params=_cparams("arbitrary"),
    )(*pieces, rhs)


def _proj_fwd(x1, g, wint):
    s_len = x1.shape[0]
    tm = min(512, s_len)
    dts = (BF16, BF16, BF16, BF16, BF16, BF16, F32, F32)

    def body(x_ref, g_ref, w_ref, h_ref, *outs):
        xv = x_ref[...]
        h = (xv * _rms_rstd(xv) * g_ref[...]).astype(BF16)
        h_ref[...] = h
        for p, o_ref in enumerate(outs):
            val = _dot_nt(h, w_ref[IN_OFFS[p]:IN_OFFS[p + 1], :])
            if p == 3:
                val = val * Q_SCALE
            o_ref[...] = val.astype(dts[p])

    row = lambda i: (i, 0)
    return pl.pallas_call(
        body, name="proj_fwd",
        grid=(s_len // tm,),
        in_specs=[pl.BlockSpec((tm, D_MODEL), row), pl.BlockSpec((1, D_MODEL), lambda i: (0, 0)),
                  pl.BlockSpec((IN_W, D_MODEL), lambda i: (0, 0))],
        out_specs=[pl.BlockSpec((tm, D_MODEL), row)] + [pl.BlockSpec((tm, w), row) for w in IN_SIZES],
        out_shape=[jax.ShapeDtypeStruct((s_len, D_MODEL), BF16)]
        + [jax.ShapeDtypeStruct((s_len, w), dt) for w, dt in zip(IN_SIZES, dts)],
        compiler_params=_cparams("parallel"),
    )(x1, g, wint)


def _proj_bwd(dpieces, dx2, x1, g, wint):
    s_len = x1.shape[0]
    tm = min(512, s_len)

    def body(*refs):
        dps = refs[:8]
        dx2_ref, x_ref, g_ref, w_ref, dx_ref, dg_ref = refs[8:]

        @pl.when(pl.program_id(0) == 0)
        def _():
            dg_ref[...] = jnp.zeros_like(dg_ref)

        dh = _dot(dps[0][...], w_ref[IN_OFFS[0]:IN_OFFS[1], :])
        for p in range(1, 8):
            dh += _dot(dps[p][...], w_ref[IN_OFFS[p]:IN_OFFS[p + 1], :])
        xv = x_ref[...]
        dx, dg = _rms_bwd(dh, xv, _rms_rstd(xv), g_ref[...])
        dx_ref[...] = dx2_ref[...] + dx
        dg_ref[...] += dg

    row = lambda i: (i, 0)
    return pl.pallas_call(
        body, name="proj_bwd",
        grid=(s_len // tm,),
        in_specs=[pl.BlockSpec((tm, w), row) for w in IN_SIZES]
        + [pl.BlockSpec((tm, D_MODEL), row), pl.BlockSpec((tm, D_MODEL), row),
           pl.BlockSpec((1, D_MODEL), lambda i: (0, 0)), pl.BlockSpec((IN_W, D_MODEL), lambda i: (0, 0))],
        out_specs=[pl.BlockSpec((tm, D_MODEL), row), pl.BlockSpec((1, D_MODEL), lambda i: (0, 0))],
        out_shape=[jax.ShapeDtypeStruct((s_len, D_MODEL), F32), jax.ShapeDtypeStruct((1, D_MODEL), F32)],
        compiler_params=_cparams("arbitrary"),
    )(*dpieces, dx2, x1, g, wint)


def _merge_fwd(x1, oa, ob, ga, gb, wswa, wsb, wout):
    s_len = x1.shape[0]
    tm = min(512, s_len)

    def body(x_ref, oa_ref, ob_ref, ga_ref, gb_ref, wa_ref, wb_ref, wo_ref, xo_ref, mg_ref):
        pa = _dot(oa_ref[...], wa_ref[...])
        pb = _dot(ob_ref[...], wb_ref[...])
        mg = (jax.nn.sigmoid(ga_ref[...]) * pa + jax.nn.sigmoid(gb_ref[...]) * pb).astype(BF16)
        mg_ref[...] = mg
        xo_ref[...] = x_ref[...] + _dot(mg, wo_ref[...])

    row = lambda i: (i, 0)
    full = lambda i: (0, 0)
    return pl.pallas_call(
        body, name="merge_fwd",
        grid=(s_len // tm,),
        in_specs=[pl.BlockSpec((tm, D_MODEL), row), pl.BlockSpec((tm, 512), row), pl.BlockSpec((tm, 512), row),
                  pl.BlockSpec((tm, D_MODEL), row), pl.BlockSpec((tm, D_MODEL), row),
                  pl.BlockSpec((512, D_MODEL), full), pl.BlockSpec((512, D_MODEL), full),
                  pl.BlockSpec((D_MODEL, D_MODEL), full)],
        out_specs=[pl.BlockSpec((tm, D_MODEL), row), pl.BlockSpec((tm, D_MODEL), row)],
        out_shape=[jax.ShapeDtypeStruct((s_len, D_MODEL), F32), jax.ShapeDtypeStruct((s_len, D_MODEL), BF16)],
        compiler_params=_cparams("parallel"),
    )(x1, oa, ob, ga, gb, wswa, wsb, wout)


def _merge_bwd(dx2, oa, ob, ga, gb, wswa, wsb, wout, comm=None):
    s_len = dx2.shape[0]
    tm = min(512, s_len)

    def body(dx_ref, oa_ref, ob_ref, ga_ref, gb_ref, wa_ref, wb_ref, wo_ref,
             doa_ref, dob_ref, dga_ref, dgb_ref, dpa_ref, dpb_ref, dxb_ref):
        dxb = dx_ref[...].astype(BF16)
        dxb_ref[...] = dxb
        dmg = _dot_nt(dxb, wo_ref[...])
        for o_ref, g_ref, w_ref, do_ref, dg_ref, dp_ref in (
                (oa_ref, ga_ref, wa_ref, doa_ref, dga_ref, dpa_ref),
                (ob_ref, gb_ref, wb_ref, dob_ref, dgb_ref, dpb_ref)):
            pv = _dot(o_ref[...], w_ref[...])
            sg = jax.nn.sigmoid(g_ref[...])
            dp = (dmg * sg).astype(BF16)
            dp_ref[...] = dp
            dg_ref[...] = (dmg * pv * sg * (1.0 - sg)).astype(BF16)
            do_ref[...] = _dot_nt(dp, w_ref[...]).astype(BF16)

    row = lambda i: (i, 0)
    full = lambda i: (0, 0)
    wide = pl.BlockSpec((tm, D_MODEL), row)
    half = pl.BlockSpec((tm, 512), row)
    return _call(
        body, (dx2, oa, ob, ga, gb, wswa, wsb, wout), comm=comm, **_grid_ends(s_len // tm), name="merge_bwd",
        grid=(s_len // tm,),
        in_specs=[wide, half, half, wide, wide, pl.BlockSpec((512, D_MODEL), full),
                  pl.BlockSpec((512, D_MODEL), full), pl.BlockSpec((D_MODEL, D_MODEL), full)],
        out_specs=[half, half, wide, wide, wide, wide, wide],
        out_shape=[jax.ShapeDtypeStruct((s_len, 512), BF16)] * 2 + [jax.ShapeDtypeStruct((s_len, D_MODEL), BF16)] * 5,
        compiler_params=_cparams("arbitrary"),
    )


def _loss_fwd_bwd(x3, tgt, g):
    s_len = x3.shape[0]
    tm = min(1024, s_len)

    def body(x_ref, t_ref, g_ref, dx_ref, loss_ref, dg_ref):
        @pl.when(pl.program_id(0) == 0)
        def _():
            loss_ref[...] = jnp.zeros_like(loss_ref)
            dg_ref[...] = jnp.zeros_like(dg_ref)

        xv = x_ref[...]
        gv = g_ref[...]
        r = _rms_rstd(xv)
        err = xv * r * gv - t_ref[...]
        loss_ref[...] += 0.5 * jnp.sum(jnp.mean(err * err, axis=-1, keepdims=True), axis=0, keepdims=True)
        dx, dg = _rms_bwd(err * (1.0 / D_MODEL), xv, r, gv)
        dx_ref[...] = dx
        dg_ref[...] += dg

    row = lambda i: (i, 0)
    return pl.pallas_call(
        body, name="loss_fwd_bwd",
        grid=(s_len // tm,),
        in_specs=[pl.BlockSpec((tm, D_MODEL), row), pl.BlockSpec((tm, D_MODEL), row),
                  pl.BlockSpec((1, D_MODEL), lambda i: (0, 0))],
        out_specs=[pl.BlockSpec((tm, D_MODEL), row), pl.BlockSpec((1, 1), lambda i: (0, 0)),
                   pl.BlockSpec((1, D_MODEL), lambda i: (0, 0))],
        out_shape=[jax.ShapeDtypeStruct((s_len, D_MODEL), F32), jax.ShapeDtypeStruct((1, 1), F32),
                   jax.ShapeDtypeStruct((1, D_MODEL), F32)],
        compiler_params=_cparams("arbitrary"),
    )(x3, tgt, g)


def _rel_bucket_matrix():
    qi = jnp.arange(SWA_BLOCK)[:, None] + SWA_BLOCK
    kj = jnp.arange(2 * SWA_BLOCK)[None, :]
    dist = jnp.maximum(qi - kj, 0)
    max_exact = REL_BUCKETS // 2
    d = jnp.maximum(dist, 1).astype(F32)
    large = max_exact + (jnp.log(d / max_exact) / np.log(REL_MAX_DIST / max_exact)
                         * (REL_BUCKETS - max_exact)).astype(jnp.int32)
    large = jnp.minimum(large, REL_BUCKETS - 1)
    return jnp.where(dist < max_exact, dist, large).astype(jnp.int32)


def _swa_bias_into(bias_ref, bkt_ref, tab_ref):
    bk = bkt_ref[...]
    for h in range(N_HEADS):
        acc = jnp.zeros(bk.shape, F32)
        for bucket in range(REL_BUCKETS):
            acc = jnp.where(bk == bucket, tab_ref[bucket, h], acc)
        bias_ref[h] = acc


def _swa_valid(n):
    shape = (SWA_BLOCK, 2 * SWA_BLOCK)
    row = lax.broadcasted_iota(jnp.int32, shape, 0)
    col = lax.broadcasted_iota(jnp.int32, shape, 1)
    dist = row + SWA_BLOCK - col
    return (dist >= 0) & (dist < SWA_BLOCK) & ((col >= SWA_BLOCK) | (n > 0))


def _swa_windows(kp_ref, kc_ref, vp_ref, vc_ref):
    return (jnp.concatenate([kp_ref[...], kc_ref[...]], axis=0), jnp.concatenate([vp_ref[...], vc_ref[...]], axis=0))


def _swa_place(h):
    return slice(h // 2 * LANES, (h // 2 + 1) * LANES), h % 2, h // SWA_GROUP


def _move_half(x, src, dst):
    moved = x if src == dst else pltpu.roll(x, HEAD_DIM, 1)
    in_dst = (lax.broadcasted_iota(jnp.int32, x.shape, 1) >= HEAD_DIM) == bool(dst)
    return jnp.where(in_dst, moved, 0.0)


def _swa_probs(qk, bias, sink, valid):
    lg = jnp.where(valid, qk * Q_SCALE + bias, NEG_BIG)
    m = jnp.maximum(jnp.max(lg, axis=-1, keepdims=True), sink)
    e = jnp.exp(lg - m)
    es = jnp.exp(sink - m)
    inv = 1.0 / (jnp.sum(e, axis=-1, keepdims=True) + es)
    return e * inv, es * inv


def _swa_specs(s_len):
    blk = SWA_BLOCK
    cur = lambda n: (n, 0)
    prev = lambda n: (jnp.maximum(n - 1, 0), 0)
    kvw = SWA_KV_HEADS * HEAD_DIM
    return [pl.BlockSpec(memory_space=pltpu.SMEM), pl.BlockSpec(memory_space=pltpu.SMEM),
            pl.BlockSpec((blk, 2 * blk), lambda n: (0, 0)),
            pl.BlockSpec((blk, N_HEADS * HEAD_DIM), cur),
            pl.BlockSpec((blk, kvw), prev), pl.BlockSpec((blk, kvw), cur),
            pl.BlockSpec((blk, kvw), prev), pl.BlockSpec((blk, kvw), cur)]


def _swa_fwd(tab, sinks, bkt, q, k, v):
    s_len = q.shape[0]
    blk = SWA_BLOCK

    def body(tab_ref, sink_ref, bkt_ref, q_ref, kp_ref, kc_ref, vp_ref, vc_ref, o_ref, bias_ref):
        n = pl.program_id(0)

        @pl.when(n == 0)
        def _():
            _swa_bias_into(bias_ref, bkt_ref, tab_ref)

        valid = _swa_valid(n)
        kk, vv = _swa_windows(kp_ref, kc_ref, vp_ref, vc_ref)
        st = {}

        def s_logits(h):
            tile, mine, kv = _swa_place(h)
            st[h, "lg"] = _dot_nt(_move_half(q_ref[:, tile].astype(F32), mine, kv).astype(BF16), kk)

        def s_probs(h):
            st[h, "p"] = _swa_probs(st.pop((h, "lg")), bias_ref[h], sink_ref[0, h], valid)[0].astype(BF16)

        def s_values(h):
            tile, mine, kv = _swa_place(h)
            part = _move_half(_dot(st.pop((h, "p")), vv), kv, mine)
            if mine == 0:
                st[h + 1, "o"] = part
            else:
                o_ref[:, tile] = (st.pop((h, "o")) + part).astype(BF16)

        _emit_skewed((list(range(N_HEADS)), [s_logits, s_probs, s_values]))

    return pl.pallas_call(
        body, name="swa_fwd",
        grid=(s_len // blk,),
        in_specs=_swa_specs(s_len),
        out_specs=pl.BlockSpec((blk, N_HEADS * HEAD_DIM), lambda n: (n, 0)),
        out_shape=jax.ShapeDtypeStruct((s_len, N_HEADS * HEAD_DIM), BF16),
        scratch_shapes=[pltpu.VMEM((N_HEADS, blk, 2 * blk), F32)],
        compiler_params=_cparams("arbitrary"),
    )(tab, sinks, bkt, q, k, k, v, v)


def _swa_bwd(tab, sinks, bkt, q, k, v, do, comm=None):
    s_len = q.shape[0]
    blk = SWA_BLOCK
    nb = s_len // blk
    kvw = SWA_KV_HEADS * HEAD_DIM

    def body(tab_ref, sink_ref, bkt_ref, q_ref, kp_ref, kc_ref, vp_ref, vc_ref, do_ref,
             dq_ref, dk_ref, dv_ref, dtab_ref, dsink_ref, bias_ref, dbias_ref):
        n = pl.program_id(0)

        @pl.when(n == 0)
        def _():
            _swa_bias_into(bias_ref, bkt_ref, tab_ref)
            dbias_ref[...] = jnp.zeros_like(dbias_ref)
            dk_ref[...] = jnp.zeros_like(dk_ref)
            dv_ref[...] = jnp.zeros_like(dv_ref)
            dsink_ref[...] = jnp.zeros_like(dsink_ref)
            dtab_ref[...] = jnp.zeros_like(dtab_ref)

        valid = _swa_valid(n)
        cur_rows = pl.ds(pl.multiple_of(n * blk, blk), blk)
        prev_rows = pl.ds(pl.multiple_of(jnp.maximum(n - 1, 0) * blk, blk), blk)
        kk, vv = _swa_windows(kp_ref, kc_ref, vp_ref, vc_ref)
        st = {}

        def s_logits(h):
            tile, mine, kv = _swa_place(h)
            st[h, "q"] = _move_half(q_ref[:, tile].astype(F32), mine, kv).astype(BF16)
            st[h, "do"] = _move_half(do_ref[:, tile].astype(F32), mine, kv).astype(BF16)
            st[h, "lg"] = _dot_nt(st[h, "q"], kk)
            st[h, "dp"] = _dot_nt(st[h, "do"], vv)

        def s_probs(h):
            p, ps = _swa_probs(st.pop((h, "lg")), bias_ref[h], sink_ref[0, h], valid)
            dp = st.pop((h, "dp"))
            delta = jnp.sum(p * dp, axis=-1, keepdims=True)
            dl = p * (dp - delta)
            dsink_ref[h:h + 1, :] += jnp.broadcast_to(-jnp.sum(ps * delta, axis=0, keepdims=True), (1, LANES))
            dbias_ref[h] += dl
            st[h, "dl"], st[h, "p"] = dl.astype(BF16), p.astype(BF16)

        def s_products(h):
            tile, mine, kv = _swa_place(h)
            dlb = st.pop((h, "dl"))
            part = _move_half(Q_SCALE * _dot(dlb, kk), kv, mine)
            if mine == 0:
                st[h + 1, "dq"] = part
            else:
                dq_ref[:, tile] = (st.pop((h, "dq")) + part).astype(BF16)
            dk_win = Q_SCALE * _dot_tn(dlb, st.pop((h, "q")))
            dv_win = _dot_tn(st.pop((h, "p")), st.pop((h, "do")))
            dk_ref[prev_rows, :] += dk_win[:blk]
            dv_ref[prev_rows, :] += dv_win[:blk]
            dk_ref[cur_rows, :] += dk_win[blk:]
            dv_ref[cur_rows, :] += dv_win[blk:]

        _emit_skewed((list(range(N_HEADS)), [s_logits, s_probs, s_products]))

        @pl.when(n == nb - 1)
        def _():
            bk = bkt_ref[...]
            lane = lax.broadcasted_iota(jnp.int32, (1, LANES), 1)
            for bucket in range(REL_BUCKETS):
                rowv = jnp.zeros((1, LANES), F32)
                for h in range(N_HEADS):
                    val = jnp.sum(jnp.where(bk == bucket, dbias_ref[h], 0.0), axis=1, keepdims=True)
                    val = jnp.sum(val, axis=0, keepdims=True)
                    rowv = jnp.where(lane == h, val, rowv)
                dtab_ref[bucket:bucket + 1, :] = rowv

    return _call(
        body, (tab, sinks, bkt, q, k, k, v, v, do), comm=comm, **_grid_ends(nb), name="swa_bwd",
        grid=(nb,),
        in_specs=_swa_specs(s_len) + [pl.BlockSpec((blk, N_HEADS * HEAD_DIM), lambda n: (n, 0))],
        out_specs=[pl.BlockSpec((blk, N_HEADS * HEAD_DIM), lambda n: (n, 0)),
                   pl.BlockSpec((s_len, kvw), lambda n: (0, 0)), pl.BlockSpec((s_len, kvw), lambda n: (0, 0)),
                   pl.BlockSpec((REL_BUCKETS, LANES), lambda n: (0, 0)), pl.BlockSpec((N_HEADS, LANES), lambda n: (0, 0))],
        out_shape=[jax.ShapeDtypeStruct((s_len, N_HEADS * HEAD_DIM), BF16),
                   jax.ShapeDtypeStruct((s_len, kvw), F32), jax.ShapeDtypeStruct((s_len, kvw), F32),
                   jax.ShapeDtypeStruct((REL_BUCKETS, LANES), F32), jax.ShapeDtypeStruct((N_HEADS, LANES), F32)],
        scratch_shapes=[pltpu.VMEM((N_HEADS, blk, 2 * blk), F32), pltpu.VMEM((N_HEADS, blk, 2 * blk), F32)],
        compiler_params=_cparams("arbitrary"),
    )


def _sb_terms(z, valid):
    zc = jnp.minimum(z, SB_LOGIT_CAP)
    lk = -jnp.log(1.0 + jnp.exp(zc))
    lsz = zc + lk
    return lsz, (lk if valid is None else jnp.where(valid, lk, 0.0))


def _bf16_parts(vals):
    parts, rest = [], vals
    for n in range(SB_SUM_PARTS):
        parts.append(rest.astype(BF16))
        if n + 1 < SB_SUM_PARTS:
            rest = rest - parts[-1].astype(F32)
    return parts[0] if len(parts) == 1 else jnp.concatenate(parts, axis=1)


def _row_sum_lanes(vals):
    return jnp.broadcast_to(jnp.sum(vals, axis=-1, keepdims=True), (vals.shape[0], LANES))


def _emit_skewed(*groups):
    for step in range(max(len(items) + len(stages) - 1 for items, stages in groups)):
        for items, stages in groups:
            for s, stage in enumerate(stages):
                if 0 <= step - s < len(items) and items[step - s] is not None:
                    stage(items[step - s])


def _sb_items(edge):
    items = []
    for h in range(2):
        for r0 in range(0, SB_QUERIES, SB_ROWS):
            if edge is None or r0 >= (edge + 1) * SB_KEYS:
                items.append((h, r0, False))
            else:
                items.append((h, r0, True) if r0 + SB_ROWS - 1 > edge * SB_KEYS else None)
    return items


def _sb_valid(w, edge):
    row = lax.broadcasted_iota(jnp.int32, (SB_ROWS, SB_KEYS), 0) + w[1]
    col = lax.broadcasted_iota(jnp.int32, (SB_ROWS, SB_KEYS), 1) + edge * SB_KEYS
    return col < row


def _sb_consts(tq, tk):
    low = lax.broadcasted_iota(jnp.int32, (tq, LANES), 1) < HEAD_DIM
    row = lax.broadcasted_iota(jnp.int32, (tk, tk), 0)
    col = lax.broadcasted_iota(jnp.int32, (tk, tk), 1)
    right = (row > col).astype(BF16)
    left = (row < col).astype(BF16)
    return low, jnp.concatenate([right] * SB_SUM_PARTS, axis=0), jnp.concatenate([left] * SB_SUM_PARTS, axis=0)


def _sb_fwd(q, k, v, comm=None):
    s_len = q.shape[0]
    tq, tk, tr = SB_QUERIES, SB_KEYS, SB_ROWS
    nk, ratio = s_len // tk, tq // tk
    assert nk <= LANES

    def body(q_ref, k_ref, v_ref, o_ref, car_ref, c_ref, oacc_ref, logw_ref, lksum_ref):
        i = pl.program_id(1)
        qv = q_ref[...]
        low, tri2, _ = _sb_consts(tq, tk)
        lane = lax.broadcasted_iota(jnp.int32, (tr, LANES), 1)
        zero = jnp.zeros_like(qv)
        q_heads = (jnp.where(low, qv, zero), jnp.where(low, zero, qv))
        c_ref[...] = jnp.zeros_like(c_ref)
        oacc_ref[...] = jnp.zeros_like(oacc_ref)
        car_ref[...] = jnp.full_like(car_ref, NEG_BIG)

        def front(j, edge):
            keys = k_ref[pl.ds(pl.multiple_of(j * tk, tk), tk), :]
            slot = j % SB_SLOTS
            st = {}

            def s_logits(w):
                st[w, "z"] = _dot_nt(q_heads[w[0]][w[1]:w[1] + tr], keys)

            def s_terms(w):
                valid = _sb_valid(w, edge) if w[2] else None
                lsz, lk = _sb_terms(st.pop((w, "z")), valid)
                st[w, "parts"] = _bf16_parts(lk)
                st[w, "lsz"] = lsz if valid is None else jnp.where(valid, lsz, NEG_BIG)
                lksum_ref[slot, w[0], w[1]:w[1] + tr, :] = _row_sum_lanes(lk)

            def s_suffix(w):
                logw_ref[slot, w[0], w[1]:w[1] + tr, :] = st.pop((w, "lsz")) + _dot(st.pop((w, "parts")), tri2)

            return _sb_items(edge), [s_logits, s_terms, s_suffix]

        def back(j, edge):
            vv = v_ref[pl.ds(pl.multiple_of(j * tk, tk), tk), :]
            slot = j % SB_SLOTS
            st = {}

            def s_weights(w):
                h, rs = w[0], slice(w[1], w[1] + tr)
                c = c_ref[h, rs, :]
                st[w, "a"] = jnp.exp(logw_ref[slot, h, rs, :] + jnp.tile(c, (1, tk // LANES))).astype(BF16)
                car_ref[h, rs, :] = jnp.where(lane == j, c, car_ref[h, rs, :])
                c_ref[h, rs, :] = c + lksum_ref[slot, h, rs, :]

            def s_values(w):
                oacc_ref[w[0], w[1]:w[1] + tr, :] += _dot(st.pop((w, "a")), vv)

            return _sb_items(edge), [s_weights, s_values]

        first = i * ratio
        edge_tiles = [(first + m, m) for m in reversed(range(ratio))]

        def alive():
            return (jnp.max(c_ref[...]) >= SB_DEAD_CARRY).astype(jnp.int32)

        @pl.when(i == 0)
        def _():
            _emit_skewed(*[front(j, m) for j, m in edge_tiles])
            _emit_skewed(*[back(j, m) for j, m in edge_tiles])

        @pl.when(i > 0)
        def _():
            tiles = edge_tiles + [(first - 1, None)]
            _emit_skewed(*[front(j, m) for j, m in tiles])
            _emit_skewed(*[back(j, m) for j, m in tiles])

            @pl.when((alive() > 0) & (first >= 2))
            def _():
                _emit_skewed(front(first - 2, None))

                def step(state):
                    pending, _ = state
                    _emit_skewed(front(pending - 1, None), back(pending, None))
                    return pending - 1, alive()

                pending, live = lax.while_loop(lambda s: (s[0] > 0) & (s[1] > 0), step, (first - 2, jnp.int32(1)))

                @pl.when(live > 0)
                def _():
                    _emit_skewed(back(pending, None))

        o_ref[...] = jnp.where(low, oacc_ref[0], oacc_ref[1]).astype(BF16)

    return _call(
        body, (q, k, v), comm=comm, **_grid_ends(N_HEADS // 2, s_len // tq), name="sb_fwd",
        grid=(N_HEADS // 2, s_len // tq),
        in_specs=[pl.BlockSpec((tq, LANES), lambda p, i: (i, p)),
                  pl.BlockSpec((s_len, LANES), lambda p, i: (0, p)),
                  pl.BlockSpec((s_len, LANES), lambda p, i: (0, p))],
        out_specs=[pl.BlockSpec((tq, LANES), lambda p, i: (i, p)), pl.BlockSpec((2, tq, LANES), lambda p, i: (p, i, 0))],
        out_shape=[jax.ShapeDtypeStruct((s_len, N_HEADS * HEAD_DIM), BF16),
                   jax.ShapeDtypeStruct((N_HEADS, s_len, LANES), F32)],
        scratch_shapes=[pltpu.VMEM((2, tq, LANES), F32), pltpu.VMEM((2, tq, LANES), F32),
                        pltpu.VMEM((SB_SLOTS, 2, tq, tk), F32), pltpu.VMEM((SB_SLOTS, 2, tq, LANES), F32)],
        compiler_params=_cparams("arbitrary", "arbitrary"),
    )


def _sb_bwd(q, k, v, do, cars):
    s_len = q.shape[0]
    tq, tk, tr = SB_QUERIES, SB_KEYS, SB_ROWS
    nk, ratio = s_len // tk, tq // tk

    def body(q_ref, k_ref, v_ref, do_ref, car_ref, dq_ref, dk_ref, dv_ref,
             gleft_ref, dqacc_ref, dkacc_ref, dvacc_ref, logw_ref, lsz_ref, da_ref, a_ref, dz_ref):
        i = pl.program_id(1)

        @pl.when(i == 0)
        def _():
            dkacc_ref[...] = jnp.zeros_like(dkacc_ref)
            dvacc_ref[...] = jnp.zeros_like(dvacc_ref)

        qv = q_ref[...]
        dov = do_ref[...]
        low, tri_right2, tri_left2 = _sb_consts(tq, tk)
        lane = lax.broadcasted_iota(jnp.int32, (tr, LANES), 1)
        zero = jnp.zeros_like(qv)
        q_heads = (jnp.where(low, qv, zero), jnp.where(low, zero, qv))
        do_heads = (jnp.where(low, dov, zero), jnp.where(low, zero, dov))
        q_t = qv.astype(F32).T.astype(BF16)
        do_t = dov.astype(F32).T.astype(BF16)
        gleft_ref[...] = jnp.zeros_like(gleft_ref)
        dqacc_ref[...] = jnp.zeros_like(dqacc_ref)

        def front(j, edge):
            key_rows = pl.ds(pl.multiple_of(j * tk, tk), tk)
            keys, values = k_ref[key_rows, :], v_ref[key_rows, :]
            slot = j % SB_SLOTS
            st = {}

            def s_logits(w):
                h, rs = w[0], slice(w[1], w[1] + tr)
                st[w, "z"] = _dot_nt(q_heads[h][rs], keys)
                da_ref[slot, h, rs, :] = _dot_nt(do_heads[h][rs], values)

            def s_terms(w):
                h, rs = w[0], slice(w[1], w[1] + tr)
                valid = _sb_valid(w, edge) if w[2] else None
                lsz, lk = _sb_terms(st.pop((w, "z")), valid)
                st[w, "parts"] = _bf16_parts(lk)
                lsz = lsz if valid is None else jnp.where(valid, lsz, NEG_BIG)
                lsz_ref[slot, h, rs, :] = lsz
                st[w, "lszc"] = lsz + jnp.sum(jnp.where(lane == j, car_ref[h, rs, :], 0.0), axis=-1, keepdims=True)

            def s_suffix(w):
                logw_ref[slot, w[0], w[1]:w[1] + tr, :] = st.pop((w, "lszc")) + _dot(st.pop((w, "parts")), tri_right2)

            return _sb_items(edge), [s_logits, s_terms, s_suffix]

        def back(j, edge):
            kv = k_ref[pl.ds(pl.multiple_of(j * tk, tk), tk), :]
            slot = j % SB_SLOTS
            st = {}

            items = _sb_items(edge)
            head_rows = [[w[1] for w in items if w is not None and w[0] == h] for h in range(2)]

            def s_weights(w):
                h, rs = w[0], slice(w[1], w[1] + tr)
                a = jnp.exp(logw_ref[slot, h, rs, :])
                g = a * da_ref[slot, h, rs, :]
                a_ref[slot, h, rs, :] = a.astype(BF16)
                st[w, "g"], st[w, "parts"] = g, _bf16_parts(g)

            def s_prefix(w):
                st[w, "gs"] = _dot(st.pop((w, "parts")), tri_left2)

            def s_dz(w):
                h, rs = w[0], slice(w[1], w[1] + tr)
                g = st.pop((w, "g"))
                gleft = gleft_ref[h, rs, :]
                gsum = st.pop((w, "gs")) + jnp.tile(gleft, (1, tk // LANES))
                dz = (g - jnp.exp(lsz_ref[slot, h, rs, :]) * (g + gsum)).astype(BF16)
                st[w, "dz"] = dz
                dz_ref[slot, h, rs, :] = dz
                gleft_ref[h, rs, :] = gleft + _row_sum_lanes(g)

            def s_products(w):
                h, rs = w[0], slice(w[1], w[1] + tr)
                dqacc_ref[h, rs, :] += _dot(st.pop((w, "dz")), kv)
                if w[1] == head_rows[h][-1]:
                    feat = slice(h * HEAD_DIM, (h + 1) * HEAD_DIM)
                    hr = slice(head_rows[h][0], tq)
                    dkacc_ref[j, feat, :] += _dot(q_t[feat, hr], dz_ref[slot, h, hr, :])
                    dvacc_ref[j, feat, :] += _dot(do_t[feat, hr], a_ref[slot, h, hr, :])

            return items, [s_weights, s_prefix, s_dz, s_products]

        first = i * ratio
        tile_max = jnp.max(jnp.maximum(car_ref[0], car_ref[1]), axis=0, keepdims=True)
        start = jnp.clip(first + ratio - jnp.sum(jnp.where(tile_max >= SB_DEAD_CARRY, 1, 0)), 0, first)

        edge_tiles = [(first + m, m) for m in range(ratio)]

        @pl.when(start == first)
        def _():
            _emit_skewed(*[front(j, m) for j, m in edge_tiles])
            _emit_skewed(*[back(j, m) for j, m in edge_tiles])

        @pl.when(start == first - 1)
        def _():
            tiles = [(first - 1, None)] + edge_tiles
            _emit_skewed(*[front(j, m) for j, m in tiles])
            _emit_skewed(*[back(j, m) for j, m in tiles])

        @pl.when(start < first - 1)
        def _():
            _emit_skewed(front(start, None))

            def step(jj, carry):
                _emit_skewed(front(jj, None), back(jj - 1, None))
                return carry

            lax.fori_loop(start + 1, first, step, 0)
            _emit_skewed(front(first, 0), back(first - 1, None))
            for m in range(1, ratio):
                _emit_skewed(front(first + m, m), back(first + m - 1, m - 1))
            _emit_skewed(back(first + ratio - 1, ratio - 1))

        dq_ref[...] = (Q_SCALE * jnp.where(low, dqacc_ref[0], dqacc_ref[1])).astype(BF16)

        @pl.when(i == s_len // tq - 1)
        def _():
            for j in range(nk):
                dk_ref[j * tk:(j + 1) * tk, :] = dkacc_ref[j].T.astype(BF16)
                dv_ref[j * tk:(j + 1) * tk, :] = dvacc_ref[j].T.astype(BF16)

    qblk = pl.BlockSpec((tq, LANES), lambda p, i: (i, p))
    col_full = pl.BlockSpec((s_len, LANES), lambda p, i: (0, p))
    return pl.pallas_call(
        body, name="sb_bwd",
        grid=(N_HEADS // 2, s_len // tq),
        in_specs=[qblk, col_full, col_full, qblk, pl.BlockSpec((2, tq, LANES), lambda p, i: (p, i, 0))],
        out_specs=[qblk, col_full, col_full],
        out_shape=[jax.ShapeDtypeStruct((s_len, N_HEADS * HEAD_DIM), BF16)] * 3,
        scratch_shapes=[pltpu.VMEM((2, tq, LANES), F32), pltpu.VMEM((2, tq, LANES), F32),
                        pltpu.VMEM((nk, LANES, tk), F32), pltpu.VMEM((nk, LANES, tk), F32)]
        + [pltpu.VMEM((SB_SLOTS, 2, tq, tk), F32)] * 3 + [pltpu.VMEM((SB_SLOTS, 2, tq, tk), BF16)] * 2,
        compiler_params=_cparams("parallel", "arbitrary"),
    )(q, k, v, do, cars)


def _local_step(xs, tgt, gains, sinks, rel_bias, weights_of, ship):
    g1, gmix, g2, gfin = gains
    bkt = _rel_bucket_matrix()
    grads = {}

    def carried(outs, comm, count):
        return outs[:count], (list(outs[count:]) if comm is not None else None)

    wts = dict(weights_of(0, None))
    comm = ship("weights", 1)
    (x1, h1, a1, b1, u1), landed = carried(
        _ffn_fwd(xs, g1, wts["ffn1_w1t"], wts["ffn1_w3t"], wts["ffn1_w2"], "1", comm), comm, 5)
    wts.update(weights_of(1, landed))
    hm, qa, ka, va, qb, kb, vb, ga, gb = _proj_fwd(x1, gmix, wts["w_int"])
    oa = _swa_fwd(rel_bias, sinks, bkt, qa, ka, va)
    comm = ship("weights", 2)
    (ob, cars), landed = carried(_sb_fwd(qb, kb, vb, comm), comm, 2)
    wts.update(weights_of(2, landed))
    x2, mg = _merge_fwd(x1, oa, ob, ga, gb, wts["w_swa"], wts["w_sb"], wts["w_out"])
    h3, a3, b3, u3 = _ffn_up(x2, g2, wts["ffn2_w1t"], wts["ffn2_w3t"], "2")
    x3, = _ffn_down(x2, u3, wts["ffn2_w2"], "2")
    dx3, loss, dgfin = _loss_fwd_bwd(x3, tgt, gfin)

    def grad_chain(items):
        prev = None
        for name, lhs, rhs in items:
            comm = None if prev is None else ship("grads", (prev[0],), prev[1])
            res = _matmul_tn(lhs, rhs, name, comm)
            if prev is not None:
                grads[(prev[0],)] = prev[1] if comm is None else res[1]
            prev = (name, {_GRAD_KEY[name]: res if comm is None else res[0]})
        return prev

    dx2, dg2, da3, db3, dx3b = _ffn_bwd(dx3, x2, g2, a3, b3, wts["ffn2_w1t"], wts["ffn2_w3t"], wts["ffn2_w2"], "2")
    last = grad_chain((("ffn2_w1", da3, h3), ("ffn2_w3", db3, h3), ("ffn2_w2", u3, dx3b)))
    comm = ship("grads", (last[0],), last[1])
    (doa, dob, dga, dgb, dpa, dpb, dx2b), landed = carried(
        _merge_bwd(dx2, oa, ob, ga, gb, wts["w_swa"], wts["w_sb"], wts["w_out"], comm), comm, 7)
    grads[(last[0],)] = last[1] if comm is None else landed[0]
    dqa, dka, dva, dtab, dsink = _swa_bwd(rel_bias, sinks, bkt, qa, ka, va, doa)

    big = {"w_out": _matmul_tn(mg, dx2b, "w_out"), "w_swa": _matmul_tn(oa, dpa, "w_swa"),
           "w_sb": _matmul_tn(ob, dpb, "w_sb")}
    dqb, dkb, dvb = _sb_bwd(qb, kb, vb, dob, cars)
    dpieces = (dqa, dka.astype(BF16), dva.astype(BF16), dqb, dkb, dvb, dga, dgb)
    big["w_int"] = _matmul_tn_stacked(dpieces, hm, "w_in")
    dx1, dgmix = _proj_bwd(dpieces, dx2, x1, gmix, wts["w_int"])

    comm = ship("grads", GROUPS[1], big)
    (dx0, dg1, da1, db1, dx1b), landed = carried(
        _ffn_bwd(dx1, xs, g1, a1, b1, wts["ffn1_w1t"], wts["ffn1_w3t"], wts["ffn1_w2"], "1", comm), comm, 5)
    grads[GROUPS[1]] = big if comm is None else landed[0]

    last = grad_chain((("ffn1_w1", da1, h1), ("ffn1_w3", db1, h1), ("ffn1_w2", u1, dx1b)))
    grads[(last[0],)] = last[1]

    small = {"gains": (dg1, dgmix, dg2, dgfin), "sinks": dsink[:, 0], "rel_bias": dtab[:, :N_HEADS]}
    return loss, dx0, small, grads


def _my_place():
    return lax.axis_index("x"), lax.axis_index("y"), lax.axis_index("c")


def _flip(v, bit):
    return 1 - v if bit else v


_RELATIONS = tuple((k >> 2 & 1, k >> 1 & 1, k & 1) for k in range(1, N_DEV))


def _gather_weights(blocks, tag):
    count = len(blocks)

    def body(*refs):
        x_refs, out_refs = refs[:count], refs[count:2 * count]
        send_sems, recv_sems, local_sems = refs[2 * count:]
        x, y, c = _my_place()
        me, sibling = (x, y, c), (x, y, 1 - c)
        chips = [(1 - x, y), (x, 1 - y), (1 - x, 1 - y)]

        def rows(s, px, py, pc):
            return out_refs[s].at[4 * px + 2 * py + pc]

        def copy(s, k, block, to, src=None):
            return pltpu.make_async_remote_copy(
                src_ref=rows(s, *block) if src is None else src, dst_ref=rows(s, *block),
                send_sem=send_sems.at[s, k], recv_sem=recv_sems.at[s, k],
                device_id=to, device_id_type=pl.DeviceIdType.MESH)

        mine = [pltpu.make_async_copy(x_refs[s], rows(s, *me), local_sems.at[s]) for s in range(count)]
        first, passed = [], []
        for s in range(count):
            mine[s].start()
            first.append(copy(s, 0, me, sibling, src=x_refs[s]))
            first += [copy(s, 1 + j, me, (*chip, c), src=x_refs[s]) for j, chip in enumerate(chips)]
        for cp in first:
            cp.start()
        for s in range(count):
            for j, chip in enumerate(chips):
                copy(s, 1 + j, (*chip, c), me).wait_recv()
                passed.append(copy(s, 4 + j, (*chip, c), sibling))
                passed[-1].start()
        for s in range(count):
            copy(s, 0, sibling, me).wait_recv()
            for j, chip in enumerate(chips):
                copy(s, 4 + j, (*chip, 1 - c), me).wait_recv()
        for cp in first + passed:
            cp.wait_send()
        for cp in mine:
            cp.wait()

    anywhere = pl.BlockSpec(memory_space=pl.ANY)
    return pl.pallas_call(
        body, name=f"gather_weights_{tag}",
        out_shape=[jax.ShapeDtypeStruct((N_DEV,) + b.shape, b.dtype) for b in blocks],
        in_specs=[anywhere] * count, out_specs=[anywhere] * count,
        scratch_shapes=[pltpu.SemaphoreType.DMA((count, N_DEV - 1)), pltpu.SemaphoreType.DMA((count, N_DEV - 1)),
                        pltpu.SemaphoreType.DMA((count,))],
    )(*blocks)


def _exchange_grads(gp, tag):
    def body(g_ref, out_ref, send_sems, recv_sems, local_sem):
        x, y, c = _my_place()
        me = 4 * x + 2 * y + c
        mine = pltpu.make_async_copy(g_ref.at[me], out_ref.at[me], local_sem)
        mine.start()
        copies = []
        for k, (fx, fy, fc) in enumerate(_RELATIONS):
            px, py, pc = _flip(x, fx), _flip(y, fy), _flip(c, fc)
            peer = 4 * px + 2 * py + pc
            copies.append((
                pltpu.make_async_remote_copy(
                    src_ref=g_ref.at[peer], dst_ref=out_ref.at[me], send_sem=send_sems.at[k], recv_sem=recv_sems.at[k],
                    device_id=(px, py, pc), device_id_type=pl.DeviceIdType.MESH),
                pltpu.make_async_remote_copy(
                    src_ref=g_ref.at[peer], dst_ref=out_ref.at[peer], send_sem=send_sems.at[k], recv_sem=recv_sems.at[k],
                    device_id=(px, py, pc), device_id_type=pl.DeviceIdType.MESH)))
        for out_cp, _ in copies:
            out_cp.start()
        for _, in_cp in copies:
            in_cp.wait_recv()
        for out_cp, _ in copies:
            out_cp.wait_send()
        mine.wait()

    return pl.pallas_call(
        body, name=f"exchange_grads_{tag}",
        out_shape=jax.ShapeDtypeStruct(gp.shape, gp.dtype),
        in_specs=[pl.BlockSpec(memory_space=pl.ANY)],
        out_specs=pl.BlockSpec(memory_space=pl.ANY),
        scratch_shapes=[pltpu.SemaphoreType.DMA((7,)), pltpu.SemaphoreType.DMA((7,)), pltpu.SemaphoreType.DMA(())],
    )(gp)


def _peers():
    x, y, c = _my_place()
    out = []
    for k, (fx, fy, fc) in enumerate(_RELATIONS):
        px, py, pc = _flip(x, fx), _flip(y, fy), _flip(c, fc)
        out.append((k, (px, py, pc), 4 * px + 2 * py + pc))
    return out, 4 * x + 2 * y + c


def _grid_ends(*grid):
    def first():
        return functools.reduce(lambda a, b: a & b, [pl.program_id(d) == 0 for d in range(len(grid))])

    def last():
        return functools.reduce(lambda a, b: a & b, [pl.program_id(d) == n - 1 for d, n in enumerate(grid)])

    return {"first": first, "last": last}


def _call(body, operands, *, comm=None, first=None, last=None, **kw):
    if comm is None:
        return pl.pallas_call(body, **kw)(*operands)
    in_specs, out_specs, out_shape = list(kw.pop("in_specs")), list(kw.pop("out_specs")), list(kw.pop("out_shape"))
    scratch = list(kw.pop("scratch_shapes", ()))
    n_in, n_out, n_scr, n_src = len(in_specs), len(out_specs), len(scratch), len(comm)

    def wrapped(*refs):
        ins, src_refs = refs[:n_in], refs[n_in:n_in + n_src]
        outs = refs[n_in + n_src:n_in + n_src + n_out]
        land_refs = refs[n_in + n_src + n_out:n_in + 2 * n_src + n_out]
        scr = refs[n_in + 2 * n_src + n_out:n_in + 2 * n_src + n_out + n_scr]
        send_sems, recv_sems, local_sems = refs[n_in + 2 * n_src + n_out + n_scr:]
        peers, me = _peers()
        mine, going, coming = [], [], []
        for s, (_, per_peer) in enumerate(comm):
            src_ref, land_ref = src_refs[s], land_refs[s]
            mine.append(pltpu.make_async_copy(src_ref.at[me] if per_peer else src_ref, land_ref.at[me], local_sems.at[s]))
            for k, where, slab in peers:
                piece = src_ref.at[slab] if per_peer else src_ref
                going.append(pltpu.make_async_remote_copy(
                    src_ref=piece, dst_ref=land_ref.at[me], send_sem=send_sems.at[s, k], recv_sem=recv_sems.at[s, k],
                    device_id=where, device_id_type=pl.DeviceIdType.MESH))
                coming.append(pltpu.make_async_remote_copy(
                    src_ref=piece, dst_ref=land_ref.at[slab], send_sem=send_sems.at[s, k], recv_sem=recv_sems.at[s, k],
                    device_id=where, device_id_type=pl.DeviceIdType.MESH))

        @pl.when(first())
        def _():
            for cp in mine + going:
                cp.start()

        body(*ins, *outs, *scr)

        @pl.when(last())
        def _():
            for cp in coming:
                cp.wait_recv()
            for cp in going:
                cp.wait_send()
            for cp in mine:
                cp.wait()

    anywhere = pl.BlockSpec(memory_space=pl.ANY)
    lands = [jax.ShapeDtypeStruct(src.shape if per_peer else (N_DEV,) + src.shape, src.dtype) for src, per_peer in comm]
    return pl.pallas_call(
        wrapped, in_specs=in_specs + [anywhere] * n_src, out_specs=out_specs + [anywhere] * n_src,
        out_shape=out_shape + lands,
        scratch_shapes=scratch + [pltpu.SemaphoreType.DMA((n_src, N_DEV - 1)), pltpu.SemaphoreType.DMA((n_src, N_DEV - 1)),
                                  pltpu.SemaphoreType.DMA((n_src,))],
        **kw)(*operands, *[src for src, _ in comm])


def _adamw(w, g, m, v):
    m = ADAM_B1 * m + (1.0 - ADAM_B1) * g
    v = ADAM_B2 * v + (1.0 - ADAM_B2) * jnp.square(g)
    m_hat = m / (1.0 - ADAM_B1 ** ADAM_STEP)
    v_hat = v / (1.0 - ADAM_B2 ** ADAM_STEP)
    delta = -ADAM_LR * (m_hat / (jnp.sqrt(v_hat) + ADAM_EPS) + ADAM_WD * w)
    return delta, m, v


def _sum_and_adamw(parts, w, m, v, tr, tag):
    rows = w.shape[0]
    assert rows % tr == 0

    def body(p_ref, w_ref, m_ref, v_ref, g_out, d_out, m_out, v_out):
        g = p_ref[0].astype(F32)
        for d in range(1, N_DEV):
            g = g + p_ref[d].astype(F32)
        delta, mn, vn = _adamw(w_ref[...], g, m_ref[...], v_ref[...])
        g_out[...] = g
        d_out[...] = delta
        m_out[...] = mn
        v_out[...] = vn

    sp = pl.BlockSpec((tr, D_MODEL), lambda i: (i, 0))
    return pl.pallas_call(
        body, name=f"sum_and_adamw_{tag}",
        grid=(rows // tr,),
        in_specs=[pl.BlockSpec((N_DEV, tr, D_MODEL), lambda i: (0, i, 0)), sp, sp, sp],
        out_specs=[sp] * 4,
        out_shape=[jax.ShapeDtypeStruct(w.shape, F32)] * 4,
        compiler_params=_cparams("parallel"),
    )(parts, w, m, v)


def _small_allreduce_adamw(part, w, m, v):
    def body(p_ref, w_ref, m_ref, v_ref, g_out, d_out, m_out, v_out, buf, send_sems, recv_sems):
        x, y, c = _my_place()
        me = 4 * x + 2 * y + c
        buf[me] = p_ref[...]
        copies = []
        for k, (fx, fy, fc) in enumerate(_RELATIONS):
            px, py, pc = _flip(x, fx), _flip(y, fy), _flip(c, fc)
            peer = 4 * px + 2 * py + pc
            copies.append((
                pltpu.make_async_remote_copy(
                    src_ref=buf.at[me], dst_ref=buf.at[me], send_sem=send_sems.at[k], recv_sem=recv_sems.at[k],
                    device_id=(px, py, pc), device_id_type=pl.DeviceIdType.MESH),
                pltpu.make_async_remote_copy(
                    src_ref=buf.at[me], dst_ref=buf.at[peer], send_sem=send_sems.at[k], recv_sem=recv_sems.at[k],
                    device_id=(px, py, pc), device_id_type=pl.DeviceIdType.MESH)))
        for out_cp, _ in copies:
            out_cp.start()
        for _, in_cp in copies:
            in_cp.wait_recv()
        for out_cp, _ in copies:
            out_cp.wait_send()
        g = buf[0]
        for d in range(1, N_DEV):
            g = g + buf[d]
        delta, mn, vn = _adamw(w_ref[...], g, m_ref[...], v_ref[...])
        g_out[...] = g
        d_out[...] = delta
        m_out[...] = mn
        v_out[...] = vn

    vm = pl.BlockSpec(memory_space=pltpu.VMEM)
    return pl.pallas_call(
        body, name="small_allreduce_adamw",
        in_specs=[vm] * 4, out_specs=[vm] * 4,
        out_shape=[jax.ShapeDtypeStruct(w.shape, F32)] * 4,
        scratch_shapes=[pltpu.VMEM((N_DEV,) + part.shape, F32),
                        pltpu.SemaphoreType.DMA((7,)), pltpu.SemaphoreType.DMA((7,))],
    )(part, w, m, v)


_TRANSPOSED = ("ffn1_w1", "ffn1_w3", "w_in", "ffn2_w1", "ffn2_w3")
_BRANCH = ("w_branch_swa", "w_branch_sb")


def _pack_shards(t, names):
    parts = []
    for name in names:
        a = t[name][0]
        if name in _TRANSPOSED:
            a = a.T
        elif name in _BRANCH:
            a = a.reshape(64, D_MODEL)
        parts.append(a)
    return jnp.concatenate(parts, axis=0)


def _unpack_shards(p, names):
    out, lo = {}, 0
    for name in names:
        a = p[lo:lo + BIG_ROWS[BIG_NAMES.index(name)]]
        lo += a.shape[0]
        if name in _TRANSPOSED:
            a = a.T
        elif name in _BRANCH:
            a = a.reshape(512, 128)
        out[name] = a[None]
    return out


def _full_weights(zones, names):
    out = {}
    for name, a in zip(names, zones):
        if name in _BRANCH:
            a = a.reshape(N_DEV, 512, 128).transpose(1, 0, 2).reshape(512, D_MODEL)
        out[_GRAD_KEY[name]] = a.reshape(-1, D_MODEL)
    return out


_GRAD_KEY = {"ffn1_w1": "ffn1_w1t", "ffn1_w3": "ffn1_w3t", "ffn1_w2": "ffn1_w2", "w_in": "w_int",
             "w_branch_swa": "w_swa", "w_branch_sb": "w_sb", "w_out": "w_out",
             "ffn2_w1": "ffn2_w1t", "ffn2_w3": "ffn2_w3t", "ffn2_w2": "ffn2_w2"}


def _pack_full_grads(big, names):
    parts = []
    for name in names:
        a = big[_GRAD_KEY[name]]
        if name in _BRANCH:
            a = a.reshape(512, N_DEV, 128).transpose(1, 0, 2)
        parts.append(a.reshape(N_DEV, BIG_ROWS[BIG_NAMES.index(name)], D_MODEL).astype(BF16))
    return jnp.concatenate(parts, axis=1)


_SMALL_NAMES = ("norm_ffn1", "norm_mix", "norm_ffn2", "norm_final", "swa_sinks", "rel_bias")


def _pack_small(vals):
    rows = []
    for a in vals:
        a = a.reshape(-1)
        rows.append(jnp.pad(a, (0, D_MODEL - a.shape[0])))
    rows += [jnp.zeros((D_MODEL,), F32)] * (SMALL_ROWS - len(rows))
    return jnp.stack(rows)


def _unpack_small(p):
    return {"norm_ffn1": p[0:1], "norm_mix": p[1:2], "norm_ffn2": p[2:3], "norm_final": p[3],
            "swa_sinks": p[4:5, :N_HEADS], "rel_bias": p[5, :REL_BUCKETS * N_HEADS].reshape(REL_BUCKETS, N_HEADS)}


ALL_NAMES = ("norm_ffn1", "ffn1_w1", "ffn1_w3", "ffn1_w2", "norm_mix", "w_in", "swa_sinks", "rel_bias",
             "w_branch_swa", "w_branch_sb", "w_out", "norm_ffn2", "ffn2_w1", "ffn2_w3", "ffn2_w2", "norm_final")


def kernel(x, norm_ffn1, ffn1_w1, ffn1_w3, ffn1_w2, norm_mix, w_in, swa_sinks, rel_bias, w_branch_swa, w_branch_sb, w_out, norm_ffn2, ffn2_w1, ffn2_w3, ffn2_w2, norm_final, loss_target, m_norm_ffn1, m_ffn1_w1, m_ffn1_w3, m_ffn1_w2, m_norm_mix, m_w_in, m_swa_sinks, m_rel_bias, m_w_branch_swa, m_w_branch_sb, m_w_out, m_norm_ffn2, m_ffn2_w1, m_ffn2_w3, m_ffn2_w2, m_norm_final, v_norm_ffn1, v_ffn1_w1, v_ffn1_w3, v_ffn1_w2, v_norm_mix, v_w_in, v_swa_sinks, v_rel_bias, v_w_branch_swa, v_w_branch_sb, v_w_out, v_norm_ffn2, v_ffn2_w1, v_ffn2_w3, v_ffn2_w2, v_norm_final):
    w = dict(zip(ALL_NAMES, (norm_ffn1, ffn1_w1, ffn1_w3, ffn1_w2, norm_mix, w_in, swa_sinks, rel_bias,
                             w_branch_swa, w_branch_sb, w_out, norm_ffn2, ffn2_w1, ffn2_w3, ffn2_w2, norm_final)))
    m = dict(zip(ALL_NAMES, (m_norm_ffn1, m_ffn1_w1, m_ffn1_w3, m_ffn1_w2, m_norm_mix, m_w_in, m_swa_sinks, m_rel_bias,
                             m_w_branch_swa, m_w_branch_sb, m_w_out, m_norm_ffn2, m_ffn2_w1, m_ffn2_w3, m_ffn2_w2,
                             m_norm_final)))
    v = dict(zip(ALL_NAMES, (v_norm_ffn1, v_ffn1_w1, v_ffn1_w3, v_ffn1_w2, v_norm_mix, v_w_in, v_swa_sinks, v_rel_bias,
                             v_w_branch_swa, v_w_branch_sb, v_w_out, v_norm_ffn2, v_ffn2_w1, v_ffn2_w3, v_ffn2_w2,
                             v_norm_final)))

    def my_blocks(group):
        return [_pack_shards(w, (name,)).astype(BF16) for name in GROUPS[group]]

    gathered0 = _gather_weights(my_blocks(0), "group0")

    def weights_of(group, landed):
        return _full_weights(gathered0 if group == 0 else landed, GROUPS[group])

    def ship(kind, which, grads=None):
        if kind == "weights":
            return [(block, False) for block in my_blocks(which)]
        return [(_pack_full_grads(grads, which), True)]

    gains = (norm_ffn1, norm_mix, norm_ffn2, norm_final.reshape(1, D_MODEL))
    loss, dx, small, parts = _local_step(x[0], loss_target[0], gains, swa_sinks, rel_bias, weights_of, ship)

    big_outs = [{}, {}, {}, {}]
    for names, tile in zip(SUM_GROUPS, SUM_TILE):
        landed = parts[names]
        if isinstance(landed, dict):
            landed = _exchange_grads(_pack_full_grads(landed, names), names[0])
        res = _sum_and_adamw(landed, _pack_shards(w, names), _pack_shards(m, names), _pack_shards(v, names),
                             tile, names[0])
        for acc, packed in zip(big_outs, res):
            acc.update(_unpack_shards(packed, names))
    g_big, d_big, m_big, v_big = big_outs

    small_part = _pack_small(small["gains"] + (small["sinks"], small["rel_bias"], loss))
    zero = jnp.zeros((1,), F32)
    small_res = _small_allreduce_adamw(
        small_part, _pack_small([w[n] for n in _SMALL_NAMES] + [zero]), _pack_small([m[n] for n in _SMALL_NAMES] + [zero]),
        _pack_small([v[n] for n in _SMALL_NAMES] + [zero]))
    g_sm, d_sm, m_sm, v_sm = (_unpack_small(p) for p in small_res)

    outs = [small_res[0][len(_SMALL_NAMES), 0], dx[None]]
    for big_d, small_d in ((g_big, g_sm), (d_big, d_sm), (m_big, m_sm), (v_big, v_sm)):
        merged = {**big_d, **small_d}
        outs += [merged[n] for n in ALL_NAMES]
    return tuple(outs)
```

```python
import functools

import jax
import jax.numpy as jnp
import numpy as np
from jax import lax
from jax.experimental import pallas as pl
from jax.experimental.pallas import tpu as pltpu

F32 = jnp.float32
BF16 = jnp.bfloat16

D_MODEL = 1024
D_FF = 2816
HEAD_DIM = 64
N_HEADS = 8
SWA_KV_HEADS = 2
SWA_GROUP = 4
SWA_BLOCK = 128
REL_BUCKETS = 32
REL_MAX_DIST = 128
RMS_EPS = 1e-6
NEG_BIG = -1e30
Q_SCALE = HEAD_DIM ** -0.5
LANES = 128

N_DEV = 8

ADAM_LR = 0.001
ADAM_B1 = 0.9
ADAM_B2 = 0.999
ADAM_EPS = 1e-08
ADAM_WD = 0.01
ADAM_STEP = 10

IN_SIZES = (512, 128, 128, 512, 512, 512, 1024, 1024)
IN_OFFS = tuple(int(v) for v in np.cumsum((0,) + IN_SIZES))
IN_W = IN_OFFS[-1]

BIG_NAMES = ("ffn1_w1", "ffn1_w3", "ffn1_w2", "w_in", "w_branch_swa", "w_branch_sb", "w_out",
             "ffn2_w1", "ffn2_w3", "ffn2_w2")
BIG_ROWS = (352, 352, 352, 544, 64, 64, 128, 352, 352, 352)
SMALL_ROWS = 8
GROUPS = (BIG_NAMES[0:3], BIG_NAMES[3:7], BIG_NAMES[7:10])
SUM_GROUPS = tuple((n,) for n in GROUPS[0]) + (GROUPS[1],) + tuple((n,) for n in GROUPS[2])
SUM_TILE = (176, 176, 176, 160, 176, 176, 176)

VMEM_LIMIT = 56 * 1024 * 1024
FFN_PIECES = 2
SB_QUERIES = 512
SB_KEYS = 256
SB_ROWS = 256
SB_SLOTS = 3
SB_SUM_PARTS = 1
SB_LOGIT_CAP = 80.0
SB_DEAD_CARRY = -110.0


def _dot(a, b):
    return jnp.dot(a, b, preferred_element_type=F32)


def _dot_nt(a, b):
    return lax.dot_general(a, b, (((1,), (1,)), ((), ())), preferred_element_type=F32)


def _dot_tn(a, b):
    return lax.dot_general(a, b, (((0,), (0,)), ((), ())), preferred_element_type=F32)


def _cparams(*sem):
    return pltpu.CompilerParams(dimension_semantics=sem, vmem_limit_bytes=VMEM_LIMIT)


def _rms_rstd(xv):
    return lax.rsqrt(jnp.mean(xv * xv, axis=-1, keepdims=True) + RMS_EPS)


def _rms_bwd(dh, xv, r, g):
    xhat = xv * r
    dg = jnp.sum(dh * xhat, axis=0, keepdims=True)
    dxn = dh * g
    dx = r * (dxn - xhat * jnp.mean(dxn * xhat, axis=-1, keepdims=True))
    return dx, dg


def _ff_tile_spec(tm, tf):
    return pl.BlockSpec((1, tm, tf), lambda i, j: (j, i, 0))


def _ffn_fwd(x, g, w1t, w3t, w2, tag, comm=None):
    s_len = x.shape[0]
    tm, tf = min(1024, s_len), 256
    nf = D_FF // tf

    def body(x_ref, g_ref, w1_ref, w3_ref, w2_ref, xo_ref, h_ref, a_ref, b_ref, u_ref, acc_ref, hs_ref):
        j = pl.program_id(1)

        @pl.when(j == 0)
        def _():
            xv = x_ref[...]
            h = (xv * _rms_rstd(xv) * g_ref[...]).astype(BF16)
            hs_ref[...] = h
            h_ref[...] = h
            acc_ref[...] = jnp.zeros_like(acc_ref)

        st = {}

        def s_up(rs):
            h = hs_ref[rs, :]
            st[rs.start, "ab"] = (_dot_nt(h, w1_ref[...]), _dot_nt(h, w3_ref[...]))

        def s_act(rs):
            a, b = st.pop((rs.start, "ab"))
            a_ref[0, rs, :] = a.astype(BF16)
            b_ref[0, rs, :] = b.astype(BF16)
            uh = (0.5 * (a * jax.nn.sigmoid(a) * b)).astype(BF16)
            u_ref[0, rs, :] = uh
            st[rs.start, "u"] = uh

        def s_down(rs):
            acc_ref[rs, :] += _dot(st.pop((rs.start, "u")), w2_ref[...])

        _emit_skewed(([slice(r, r + tm // FFN_PIECES) for r in range(0, tm, tm // FFN_PIECES)], [s_up, s_act, s_down]))

        @pl.when(j == nf - 1)
        def _():
            xo_ref[...] = x_ref[...] + acc_ref[...]

    row = lambda i, j: (i, 0)
    return _call(
        body, (x, g, w1t, w3t, w2), comm=comm, **_grid_ends(s_len // tm, nf), name=f"ffn_fwd_{tag}",
        grid=(s_len // tm, nf),
        in_specs=[pl.BlockSpec((tm, D_MODEL), row), pl.BlockSpec((1, D_MODEL), lambda i, j: (0, 0)),
                  pl.BlockSpec((tf, D_MODEL), lambda i, j: (j, 0)), pl.BlockSpec((tf, D_MODEL), lambda i, j: (j, 0)),
                  pl.BlockSpec((tf, D_MODEL), lambda i, j: (j, 0))],
        out_specs=[pl.BlockSpec((tm, D_MODEL), row), pl.BlockSpec((tm, D_MODEL), row)] + [_ff_tile_spec(tm, tf)] * 3,
        out_shape=[jax.ShapeDtypeStruct((s_len, D_MODEL), F32), jax.ShapeDtypeStruct((s_len, D_MODEL), BF16)]
        + [jax.ShapeDtypeStruct((nf, s_len, tf), BF16)] * 3,
        scratch_shapes=[pltpu.VMEM((tm, D_MODEL), F32), pltpu.VMEM((tm, D_MODEL), BF16)],
        compiler_params=_cparams("arbitrary", "arbitrary"),
    )


def _ffn_bwd(dy, x, g, a, b, w1t, w3t, w2, tag, comm=None):
    s_len = x.shape[0]
    tm, tf = min(1024, s_len), 256
    nf = D_FF // tf

    def body(dy_ref, x_ref, g_ref, a_ref, b_ref, w1_ref, w3_ref, w2_ref,
             dx_ref, dg_ref, da_ref, db_ref, dyb_ref, acc_ref, dys_ref):
        i, j = pl.program_id(0), pl.program_id(1)

        @pl.when(j == 0)
        def _():
            dyb = dy_ref[...].astype(BF16)
            dys_ref[...] = 0.5 * dyb
            dyb_ref[...] = dyb
            acc_ref[...] = jnp.zeros_like(acc_ref)

        @pl.when((i == 0) & (j == 0))
        def _():
            dg_ref[...] = jnp.zeros_like(dg_ref)

        st = {}

        def s_du(rs):
            st[rs.start, "du"] = _dot_nt(dys_ref[rs, :], w2_ref[...])

        def s_act(rs):
            du = st.pop((rs.start, "du"))
            av = a_ref[0, rs, :].astype(F32)
            bv = b_ref[0, rs, :].astype(F32)
            sg = jax.nn.sigmoid(av)
            sil = av * sg
            da = (du * bv * (sg + sil * (1.0 - sg))).astype(BF16)
            db = (du * sil).astype(BF16)
            da_ref[0, rs, :] = da
            db_ref[0, rs, :] = db
            st[rs.start, "dab"] = (da, db)

        def s_dh(rs):
            da, db = st.pop((rs.start, "dab"))
            acc_ref[rs, :] += _dot(da, w1_ref[...]) + _dot(db, w3_ref[...])

        _emit_skewed(([slice(r, r + tm // FFN_PIECES) for r in range(0, tm, tm // FFN_PIECES)], [s_du, s_act, s_dh]))

        @pl.when(j == nf - 1)
        def _():
            xv = x_ref[...]
            dx, dg = _rms_bwd(acc_ref[...], xv, _rms_rstd(xv), g_ref[...])
            dx_ref[...] = dy_ref[...] + dx
            dg_ref[...] += dg

    row = lambda i, j: (i, 0)
    wsp = pl.BlockSpec((tf, D_MODEL), lambda i, j: (j, 0))
    return _call(
        body, (dy, x, g, a, b, w1t, w3t, w2), comm=comm, **_grid_ends(s_len // tm, nf), name=f"ffn_bwd_{tag}",
        grid=(s_len // tm, nf),
        in_specs=[pl.BlockSpec((tm, D_MODEL), row), pl.BlockSpec((tm, D_MODEL), row),
                  pl.BlockSpec((1, D_MODEL), lambda i, j: (0, 0)),
                  _ff_tile_spec(tm, tf), _ff_tile_spec(tm, tf), wsp, wsp, wsp],
        out_specs=[pl.BlockSpec((tm, D_MODEL), row), pl.BlockSpec((1, D_MODEL), lambda i, j: (0, 0)),
                   _ff_tile_spec(tm, tf), _ff_tile_spec(tm, tf), pl.BlockSpec((tm, D_MODEL), row)],
        out_shape=[jax.ShapeDtypeStruct((s_len, D_MODEL), F32), jax.ShapeDtypeStruct((1, D_MODEL), F32),
                   jax.ShapeDtypeStruct((nf, s_len, tf), BF16), jax.ShapeDtypeStruct((nf, s_len, tf), BF16),
                   jax.ShapeDtypeStruct((s_len, D_MODEL), BF16)],
        scratch_shapes=[pltpu.VMEM((tm, D_MODEL), F32), pltpu.VMEM((tm, D_MODEL), BF16)],
        compiler_params=_cparams("arbitrary", "arbitrary"),
    )


def _matmul_tn(lhs, rhs, tag, comm=None):
    s_len, m = lhs.shape
    n = rhs.shape[1]
    tm = min(512, s_len)
    tj = m if m <= 1024 else 1408
    assert m % tj == 0
    last_rows = s_len // tm - 1

    def body(l_ref, r_ref, o_ref, acc_ref):
        i = pl.program_id(1)

        @pl.when(i == 0)
        def _():
            acc_ref[...] = jnp.zeros_like(acc_ref)

        acc_ref[...] += _dot_tn(l_ref[...], r_ref[...])

        @pl.when(i == last_rows)
        def _():
            o_ref[...] = acc_ref[...].astype(BF16)

    res = _call(
        body, (lhs, rhs), comm=comm, **_grid_ends(m // tj, s_len // tm), name=f"matmul_tn_{tag}",
        grid=(m // tj, s_len // tm),
        in_specs=[pl.BlockSpec((tm, tj), lambda j, i: (i, j)), pl.BlockSpec((tm, n), lambda j, i: (i, 0))],
        out_specs=[pl.BlockSpec((tj, n), lambda j, i: (j, 0))],
        out_shape=[jax.ShapeDtypeStruct((m, n), BF16)],
        scratch_shapes=[pltpu.VMEM((tj, n), F32)],
        compiler_params=_cparams("arbitrary", "arbitrary"),
    )
    return res[0] if comm is None else tuple(res)


def _matmul_tn_tiled(lhs, rhs, tag, comm=None):
    nf, s_len, tf = lhs.shape
    n = rhs.shape[1]
    tm = min(512, s_len)
    last_rows = s_len // tm - 1

    def body(l_ref, r_ref, o_ref, acc_ref):
        i = pl.program_id(0)

        @pl.when(i == 0)
        def _():
            acc_ref[...] = jnp.zeros_like(acc_ref)

        rv = r_ref[...]
        for t in range(nf):
            acc_ref[t * tf:(t + 1) * tf, :] += _dot_tn(l_ref[t], rv)

        @pl.when(i == last_rows)
        def _():
            o_ref[...] = acc_ref[...].astype(BF16)

    res = _call(
        body, (lhs, rhs), comm=comm, **_grid_ends(s_len // tm), name=f"matmul_tn_{tag}",
        grid=(s_len // tm,),
        in_specs=[pl.BlockSpec((nf, tm, tf), lambda i: (0, i, 0)), pl.BlockSpec((tm, n), lambda i: (i, 0))],
        out_specs=[pl.BlockSpec((nf * tf, n), lambda i: (0, 0))],
        out_shape=[jax.ShapeDtypeStruct((nf * tf, n), BF16)],
        scratch_shapes=[pltpu.VMEM((nf * tf, n), F32)],
        compiler_params=_cparams("arbitrary"),
    )
    return res[0] if comm is None else tuple(res)


def _matmul_tn_stacked(pieces, rhs, tag):
    s_len, n = rhs.shape
    widths = [p.shape[1] for p in pieces]
    offs = [sum(widths[:k]) for k in range(len(widths) + 1)]
    tm = min(256, s_len)
    last_rows = s_len // tm - 1

    def body(*refs):
        l_refs, r_ref, o_ref, acc_ref = refs[:len(pieces)], refs[-3], refs[-2], refs[-1]
        i = pl.program_id(0)

        @pl.when(i == 0)
        def _():
            acc_ref[...] = jnp.zeros_like(acc_ref)

        rv = r_ref[...]
        for k, l_ref in enumerate(l_refs):
            acc_ref[offs[k]:offs[k + 1], :] += _dot_tn(l_ref[...], rv)

        @pl.when(i == last_rows)
        def _():
            o_ref[...] = acc_ref[...].astype(BF16)

    row = lambda i: (i, 0)
    return pl.pallas_call(
        body, name=f"matmul_tn_{tag}",
        grid=(s_len // tm,),
        in_specs=[pl.BlockSpec((tm, w), row) for w in widths] + [pl.BlockSpec((tm, n), row)],
        out_specs=pl.BlockSpec((offs[-1], n), lambda i: (0, 0)),
        out_shape=jax.ShapeDtypeStruct((offs[-1], n), BF16),
        scratch_shapes=[pltpu.VMEM((offs[-1], n), F32)],
        compiler_params=_cparams("arbitrary"),
    )(*pieces, rhs)


def _proj_fwd(x1, g, wint):
    s_len = x1.shape[0]
    tm = min(512, s_len)
    dts = (BF16, BF16, BF16, BF16, BF16, BF16, F32, F32)

    def body(x_ref, g_ref, w_ref, h_ref, *outs):
        xv = x_ref[...]
        h = (xv * _rms_rstd(xv) * g_ref[...]).astype(BF16)
        h_ref[...] = h
        for p, o_ref in enumerate(outs):
            val = _dot_nt(h, w_ref[IN_OFFS[p]:IN_OFFS[p + 1], :])
            if p == 3:
                val = val * Q_SCALE
            o_ref[...] = val.astype(dts[p])

    row = lambda i: (i, 0)
    return pl.pallas_call(
        body, name="proj_fwd",
        grid=(s_len // tm,),
        in_specs=[pl.BlockSpec((tm, D_MODEL), row), pl.BlockSpec((1, D_MODEL), lambda i: (0, 0)),
                  pl.BlockSpec((IN_W, D_MODEL), lambda i: (0, 0))],
        out_specs=[pl.BlockSpec((tm, D_MODEL), row)] + [pl.BlockSpec((tm, w), row) for w in IN_SIZES],
        out_shape=[jax.ShapeDtypeStruct((s_len, D_MODEL), BF16)]
        + [jax.ShapeDtypeStruct((s_len, w), dt) for w, dt in zip(IN_SIZES, dts)],
        compiler_params=_cparams("parallel"),
    )(x1, g, wint)


def _proj_bwd(dpieces, dx2, x1, g, wint):
    s_len = x1.shape[0]
    tm = min(512, s_len)

    def body(*refs):
        dps = refs[:8]
        dx2_ref, x_ref, g_ref, w_ref, dx_ref, dg_ref = refs[8:]

        @pl.when(pl.program_id(0) == 0)
        def _():
            dg_ref[...] = jnp.zeros_like(dg_ref)

        dh = _dot(dps[0][...], w_ref[IN_OFFS[0]:IN_OFFS[1], :])
        for p in range(1, 8):
            dh += _dot(dps[p][...], w_ref[IN_OFFS[p]:IN_OFFS[p + 1], :])
        xv = x_ref[...]
        dx, dg = _rms_bwd(dh, xv, _rms_rstd(xv), g_ref[...])
        dx_ref[...] = dx2_ref[...] + dx
        dg_ref[...] += dg

    row = lambda i: (i, 0)
    return pl.pallas_call(
        body, name="proj_bwd",
        grid=(s_len // tm,),
        in_specs=[pl.BlockSpec((tm, w), row) for w in IN_SIZES]
        + [pl.BlockSpec((tm, D_MODEL), row), pl.BlockSpec((tm, D_MODEL), row),
           pl.BlockSpec((1, D_MODEL), lambda i: (0, 0)), pl.BlockSpec((IN_W, D_MODEL), lambda i: (0, 0))],
        out_specs=[pl.BlockSpec((tm, D_MODEL), row), pl.BlockSpec((1, D_MODEL), lambda i: (0, 0))],
        out_shape=[jax.ShapeDtypeStruct((s_len, D_MODEL), F32), jax.ShapeDtypeStruct((1, D_MODEL), F32)],
        compiler_params=_cparams("arbitrary"),
    )(*dpieces, dx2, x1, g, wint)


def _merge_fwd(x1, oa, ob, ga, gb, wswa, wsb, wout):
    s_len = x1.shape[0]
    tm = min(512, s_len)

    def body(x_ref, oa_ref, ob_ref, ga_ref, gb_ref, wa_ref, wb_ref, wo_ref, xo_ref, mg_ref):
        pa = _dot(oa_ref[...], wa_ref[...])
        pb = _dot(ob_ref[...], wb_ref[...])
        mg = (jax.nn.sigmoid(ga_ref[...]) * pa + jax.nn.sigmoid(gb_ref[...]) * pb).astype(BF16)
        mg_ref[...] = mg
        xo_ref[...] = x_ref[...] + _dot(mg, wo_ref[...])

    row = lambda i: (i, 0)
    full = lambda i: (0, 0)
    return pl.pallas_call(
        body, name="merge_fwd",
        grid=(s_len // tm,),
        in_specs=[pl.BlockSpec((tm, D_MODEL), row), pl.BlockSpec((tm, 512), row), pl.BlockSpec((tm, 512), row),
                  pl.BlockSpec((tm, D_MODEL), row), pl.BlockSpec((tm, D_MODEL), row),
                  pl.BlockSpec((512, D_MODEL), full), pl.BlockSpec((512, D_MODEL), full),
                  pl.BlockSpec((D_MODEL, D_MODEL), full)],
        out_specs=[pl.BlockSpec((tm, D_MODEL), row), pl.BlockSpec((tm, D_MODEL), row)],
        out_shape=[jax.ShapeDtypeStruct((s_len, D_MODEL), F32), jax.ShapeDtypeStruct((s_len, D_MODEL), BF16)],
        compiler_params=_cparams("parallel"),
    )(x1, oa, ob, ga, gb, wswa, wsb, wout)


def _merge_bwd(dx2, oa, ob, ga, gb, wswa, wsb, wout, comm=None):
    s_len = dx2.shape[0]
    tm = min(512, s_len)

    def body(dx_ref, oa_ref, ob_ref, ga_ref, gb_ref, wa_ref, wb_ref, wo_ref,
             doa_ref, dob_ref, dga_ref, dgb_ref, dpa_ref, dpb_ref, dxb_ref):
        dxb = dx_ref[...].astype(BF16)
        dxb_ref[...] = dxb
        dmg = _dot_nt(dxb, wo_ref[...])
        for o_ref, g_ref, w_ref, do_ref, dg_ref, dp_ref in (
                (oa_ref, ga_ref, wa_ref, doa_ref, dga_ref, dpa_ref),
                (ob_ref, gb_ref, wb_ref, dob_ref, dgb_ref, dpb_ref)):
            pv = _dot(o_ref[...], w_ref[...])
            sg = jax.nn.sigmoid(g_ref[...])
            dp = (dmg * sg).astype(BF16)
            dp_ref[...] = dp
            dg_ref[...] = (dmg * pv * sg * (1.0 - sg)).astype(BF16)
            do_ref[...] = _dot_nt(dp, w_ref[...]).astype(BF16)

    row = lambda i: (i, 0)
    full = lambda i: (0, 0)
    wide = pl.BlockSpec((tm, D_MODEL), row)
    half = pl.BlockSpec((tm, 512), row)
    return _call(
        body, (dx2, oa, ob, ga, gb, wswa, wsb, wout), comm=comm, **_grid_ends(s_len // tm), name="merge_bwd",
        grid=(s_len // tm,),
        in_specs=[wide, half, half, wide, wide, pl.BlockSpec((512, D_MODEL), full),
                  pl.BlockSpec((512, D_MODEL), full), pl.BlockSpec((D_MODEL, D_MODEL), full)],
        out_specs=[half, half, wide, wide, wide, wide, wide],
        out_shape=[jax.ShapeDtypeStruct((s_len, 512), BF16)] * 2 + [jax.ShapeDtypeStruct((s_len, D_MODEL), BF16)] * 5,
        compiler_params=_cparams("arbitrary"),
    )


def _loss_fwd_bwd(x3, tgt, g):
    s_len = x3.shape[0]
    tm = min(1024, s_len)

    def body(x_ref, t_ref, g_ref, dx_ref, loss_ref, dg_ref):
        @pl.when(pl.program_id(0) == 0)
        def _():
            loss_ref[...] = jnp.zeros_like(loss_ref)
            dg_ref[...] = jnp.zeros_like(dg_ref)

        xv = x_ref[...]
        gv = g_ref[...]
        r = _rms_rstd(xv)
        err = xv * r * gv - t_ref[...]
        loss_ref[...] += 0.5 * jnp.sum(jnp.mean(err * err, axis=-1, keepdims=True), axis=0, keepdims=True)
        dx, dg = _rms_bwd(err * (1.0 / D_MODEL), xv, r, gv)
        dx_ref[...] = dx
        dg_ref[...] += dg

    row = lambda i: (i, 0)
    return pl.pallas_call(
        body, name="loss_fwd_bwd",
        grid=(s_len // tm,),
        in_specs=[pl.BlockSpec((tm, D_MODEL), row), pl.BlockSpec((tm, D_MODEL), row),
                  pl.BlockSpec((1, D_MODEL), lambda i: (0, 0))],
        out_specs=[pl.BlockSpec((tm, D_MODEL), row), pl.BlockSpec((1, 1), lambda i: (0, 0)),
                   pl.BlockSpec((1, D_MODEL), lambda i: (0, 0))],
        out_shape=[jax.ShapeDtypeStruct((s_len, D_MODEL), F32), jax.ShapeDtypeStruct((1, 1), F32),
                   jax.ShapeDtypeStruct((1, D_MODEL), F32)],
        compiler_params=_cparams("arbitrary"),
    )(x3, tgt, g)


def _rel_bucket_matrix():
    qi = jnp.arange(SWA_BLOCK)[:, None] + SWA_BLOCK
    kj = jnp.arange(2 * SWA_BLOCK)[None, :]
    dist = jnp.maximum(qi - kj, 0)
    max_exact = REL_BUCKETS // 2
    d = jnp.maximum(dist, 1).astype(F32)
    large = max_exact + (jnp.log(d / max_exact) / np.log(REL_MAX_DIST / max_exact)
                         * (REL_BUCKETS - max_exact)).astype(jnp.int32)
    large = jnp.minimum(large, REL_BUCKETS - 1)
    return jnp.where(dist < max_exact, dist, large).astype(jnp.int32)


def _swa_bias_into(bias_ref, bkt_ref, tab_ref):
    bk = bkt_ref[...]
    for h in range(N_HEADS):
        acc = jnp.zeros(bk.shape, F32)
        for bucket in range(REL_BUCKETS):
            acc = jnp.where(bk == bucket, tab_ref[bucket, h], acc)
        bias_ref[h] = acc


def _swa_valid(n):
    shape = (SWA_BLOCK, 2 * SWA_BLOCK)
    row = lax.broadcasted_iota(jnp.int32, shape, 0)
    col = lax.broadcasted_iota(jnp.int32, shape, 1)
    dist = row + SWA_BLOCK - col
    return (dist >= 0) & (dist < SWA_BLOCK) & ((col >= SWA_BLOCK) | (n > 0))


def _swa_windows(kp_ref, kc_ref, vp_ref, vc_ref):
    return (jnp.concatenate([kp_ref[...], kc_ref[...]], axis=0), jnp.concatenate([vp_ref[...], vc_ref[...]], axis=0))


def _swa_place(h):
    return slice(h // 2 * LANES, (h // 2 + 1) * LANES), h % 2, h // SWA_GROUP


def _move_half(x, src, dst):
    moved = x if src == dst else pltpu.roll(x, HEAD_DIM, 1)
    in_dst = (lax.broadcasted_iota(jnp.int32, x.shape, 1) >= HEAD_DIM) == bool(dst)
    return jnp.where(in_dst, moved, 0.0)


def _swa_probs(qk, bias, sink, valid):
    lg = jnp.where(valid, qk * Q_SCALE + bias, NEG_BIG)
    m = jnp.maximum(jnp.max(lg, axis=-1, keepdims=True), sink)
    e = jnp.exp(lg - m)
    es = jnp.exp(sink - m)
    inv = 1.0 / (jnp.sum(e, axis=-1, keepdims=True) + es)
    return e * inv, es * inv


def _swa_specs(s_len):
    blk = SWA_BLOCK
    cur = lambda n: (n, 0)
    prev = lambda n: (jnp.maximum(n - 1, 0), 0)
    kvw = SWA_KV_HEADS * HEAD_DIM
    return [pl.BlockSpec(memory_space=pltpu.SMEM), pl.BlockSpec(memory_space=pltpu.SMEM),
            pl.BlockSpec((blk, 2 * blk), lambda n: (0, 0)),
            pl.BlockSpec((blk, N_HEADS * HEAD_DIM), cur),
            pl.BlockSpec((blk, kvw), prev), pl.BlockSpec((blk, kvw), cur),
            pl.BlockSpec((blk, kvw), prev), pl.BlockSpec((blk, kvw), cur)]


def _swa_fwd(tab, sinks, bkt, q, k, v):
    s_len = q.shape[0]
    blk = SWA_BLOCK

    def body(tab_ref, sink_ref, bkt_ref, q_ref, kp_ref, kc_ref, vp_ref, vc_ref, o_ref, bias_ref):
        n = pl.program_id(0)

        @pl.when(n == 0)
        def _():
            _swa_bias_into(bias_ref, bkt_ref, tab_ref)

        valid = _swa_valid(n)
        kk, vv = _swa_windows(kp_ref, kc_ref, vp_ref, vc_ref)
        st = {}

        def s_logits(h):
            tile, mine, kv = _swa_place(h)
            st[h, "lg"] = _dot_nt(_move_half(q_ref[:, tile].astype(F32), mine, kv).astype(BF16), kk)

        def s_probs(h):
            st[h, "p"] = _swa_probs(st.pop((h, "lg")), bias_ref[h], sink_ref[0, h], valid)[0].astype(BF16)

        def s_values(h):
            tile, mine, kv = _swa_place(h)
            part = _move_half(_dot(st.pop((h, "p")), vv), kv, mine)
            if mine == 0:
                st[h + 1, "o"] = part
            else:
                o_ref[:, tile] = (st.pop((h, "o")) + part).astype(BF16)

        _emit_skewed((list(range(N_HEADS)), [s_logits, s_probs, s_values]))

    return pl.pallas_call(
        body, name="swa_fwd",
        grid=(s_len // blk,),
        in_specs=_swa_specs(s_len),
        out_specs=pl.BlockSpec((blk, N_HEADS * HEAD_DIM), lambda n: (n, 0)),
        out_shape=jax.ShapeDtypeStruct((s_len, N_HEADS * HEAD_DIM), BF16),
        scratch_shapes=[pltpu.VMEM((N_HEADS, blk, 2 * blk), F32)],
        compiler_params=_cparams("arbitrary"),
    )(tab, sinks, bkt, q, k, k, v, v)


def _swa_bwd(tab, sinks, bkt, q, k, v, do, comm=None):
    s_len = q.shape[0]
    blk = SWA_BLOCK
    nb = s_len // blk
    kvw = SWA_KV_HEADS * HEAD_DIM

    def body(tab_ref, sink_ref, bkt_ref, q_ref, kp_ref, kc_ref, vp_ref, vc_ref, do_ref,
             dq_ref, dk_ref, dv_ref, dtab_ref, dsink_ref, bias_ref, dbias_ref):
        n = pl.program_id(0)

        @pl.when(n == 0)
        def _():
            _swa_bias_into(bias_ref, bkt_ref, tab_ref)
            dbias_ref[...] = jnp.zeros_like(dbias_ref)
            dk_ref[...] = jnp.zeros_like(dk_ref)
            dv_ref[...] = jnp.zeros_like(dv_ref)
            dsink_ref[...] = jnp.zeros_like(dsink_ref)
            dtab_ref[...] = jnp.zeros_like(dtab_ref)

        valid = _swa_valid(n)
        cur_rows = pl.ds(pl.multiple_of(n * blk, blk), blk)
        prev_rows = pl.ds(pl.multiple_of(jnp.maximum(n - 1, 0) * blk, blk), blk)
        kk, vv = _swa_windows(kp_ref, kc_ref, vp_ref, vc_ref)
        st = {}

        def s_logits(h):
            tile, mine, kv = _swa_place(h)
            st[h, "q"] = _move_half(q_ref[:, tile].astype(F32), mine, kv).astype(BF16)
            st[h, "do"] = _move_half(do_ref[:, tile].astype(F32), mine, kv).astype(BF16)
            st[h, "lg"] = _dot_nt(st[h, "q"], kk)
            st[h, "dp"] = _dot_nt(st[h, "do"], vv)

        def s_probs(h):
            p, ps = _swa_probs(st.pop((h, "lg")), bias_ref[h], sink_ref[0, h], valid)
            dp = st.pop((h, "dp"))
            delta = jnp.sum(p * dp, axis=-1, keepdims=True)
            dl = p * (dp - delta)
            dsink_ref[h:h + 1, :] += jnp.broadcast_to(-jnp.sum(ps * delta, axis=0, keepdims=True), (1, LANES))
            dbias_ref[h] += dl
            st[h, "dl"], st[h, "p"] = dl.astype(BF16), p.astype(BF16)

        def s_products(h):
            tile, mine, kv = _swa_place(h)
            dlb = st.pop((h, "dl"))
            part = _move_half(Q_SCALE * _dot(dlb, kk), kv, mine)
            if mine == 0:
                st[h + 1, "dq"] = part
            else:
                dq_ref[:, tile] = (st.pop((h, "dq")) + part).astype(BF16)
            dk_win = Q_SCALE * _dot_tn(dlb, st.pop((h, "q")))
            dv_win = _dot_tn(st.pop((h, "p")), st.pop((h, "do")))
            dk_ref[prev_rows, :] += dk_win[:blk]
            dv_ref[prev_rows, :] += dv_win[:blk]
            dk_ref[cur_rows, :] += dk_win[blk:]
            dv_ref[cur_rows, :] += dv_win[blk:]

        _emit_skewed((list(range(N_HEADS)), [s_logits, s_probs, s_products]))

        @pl.when(n == nb - 1)
        def _():
            bk = bkt_ref[...]
            lane = lax.broadcasted_iota(jnp.int32, (1, LANES), 1)
            for bucket in range(REL_BUCKETS):
                rowv = jnp.zeros((1, LANES), F32)
                for h in range(N_HEADS):
                    val = jnp.sum(jnp.where(bk == bucket, dbias_ref[h], 0.0), axis=1, keepdims=True)
                    val = jnp.sum(val, axis=0, keepdims=True)
                    rowv = jnp.where(lane == h, val, rowv)
                dtab_ref[bucket:bucket + 1, :] = rowv

    return _call(
        body, (tab, sinks, bkt, q, k, k, v, v, do), comm=comm, **_grid_ends(nb), name="swa_bwd",
        grid=(nb,),
        in_specs=_swa_specs(s_len) + [pl.BlockSpec((blk, N_HEADS * HEAD_DIM), lambda n: (n, 0))],
        out_specs=[pl.BlockSpec((blk, N_HEADS * HEAD_DIM), lambda n: (n, 0)),
                   pl.BlockSpec((s_len, kvw), lambda n: (0, 0)), pl.BlockSpec((s_len, kvw), lambda n: (0, 0)),
                   pl.BlockSpec((REL_BUCKETS, LANES), lambda n: (0, 0)), pl.BlockSpec((N_HEADS, LANES), lambda n: (0, 0))],
        out_shape=[jax.ShapeDtypeStruct((s_len, N_HEADS * HEAD_DIM), BF16),
                   jax.ShapeDtypeStruct((s_len, kvw), F32), jax.ShapeDtypeStruct((s_len, kvw), F32),
                   jax.ShapeDtypeStruct((REL_BUCKETS, LANES), F32), jax.ShapeDtypeStruct((N_HEADS, LANES), F32)],
        scratch_shapes=[pltpu.VMEM((N_HEADS, blk, 2 * blk), F32), pltpu.VMEM((N_HEADS, blk, 2 * blk), F32)],
        compiler_params=_cparams("arbitrary"),
    )


def _sb_terms(z, valid):
    zc = jnp.minimum(z, SB_LOGIT_CAP)
    lk = -jnp.log(1.0 + jnp.exp(zc))
    lsz = zc + lk
    return lsz, (lk if valid is None else jnp.where(valid, lk, 0.0))


def _bf16_parts(vals):
    parts, rest = [], vals
    for n in range(SB_SUM_PARTS):
        parts.append(rest.astype(BF16))
        if n + 1 < SB_SUM_PARTS:
            rest = rest - parts[-1].astype(F32)
    return parts[0] if len(parts) == 1 else jnp.concatenate(parts, axis=1)


def _row_sum_lanes(vals):
    return jnp.broadcast_to(jnp.sum(vals, axis=-1, keepdims=True), (vals.shape[0], LANES))


def _emit_skewed(*groups):
    for step in range(max(len(items) + len(stages) - 1 for items, stages in groups)):
        for items, stages in groups:
            for s, stage in enumerate(stages):
                if 0 <= step - s < len(items) and items[step - s] is not None:
                    stage(items[step - s])


def _sb_items(edge):
    items = []
    for h in range(2):
        for r0 in range(0, SB_QUERIES, SB_ROWS):
            if edge is None or r0 >= (edge + 1) * SB_KEYS:
                items.append((h, r0, False))
            else:
                items.append((h, r0, True) if r0 + SB_ROWS - 1 > edge * SB_KEYS else None)
    return items


def _sb_valid(w, edge):
    row = lax.broadcasted_iota(jnp.int32, (SB_ROWS, SB_KEYS), 0) + w[1]
    col = lax.broadcasted_iota(jnp.int32, (SB_ROWS, SB_KEYS), 1) + edge * SB_KEYS
    return col < row


def _sb_consts(tq, tk):
    low = lax.broadcasted_iota(jnp.int32, (tq, LANES), 1) < HEAD_DIM
    row = lax.broadcasted_iota(jnp.int32, (tk, tk), 0)
    col = lax.broadcasted_iota(jnp.int32, (tk, tk), 1)
    right = (row > col).astype(BF16)
    left = (row < col).astype(BF16)
    return low, jnp.concatenate([right] * SB_SUM_PARTS, axis=0), jnp.concatenate([left] * SB_SUM_PARTS, axis=0)


def _sb_fwd(q, k, v, comm=None):
    s_len = q.shape[0]
    tq, tk, tr = SB_QUERIES, SB_KEYS, SB_ROWS
    nk, ratio = s_len // tk, tq // tk
    assert nk <= LANES

    def body(q_ref, k_ref, v_ref, o_ref, car_ref, c_ref, oacc_ref, logw_ref, lksum_ref):
        i = pl.program_id(1)
        qv = q_ref[...]
        low, tri2, _ = _sb_consts(tq, tk)
        lane = lax.broadcasted_iota(jnp.int32, (tr, LANES), 1)
        zero = jnp.zeros_like(qv)
        q_heads = (jnp.where(low, qv, zero), jnp.where(low, zero, qv))
        c_ref[...] = jnp.zeros_like(c_ref)
        oacc_ref[...] = jnp.zeros_like(oacc_ref)
        car_ref[...] = jnp.full_like(car_ref, NEG_BIG)

        def front(j, edge):
            keys = k_ref[pl.ds(pl.multiple_of(j * tk, tk), tk), :]
            slot = j % SB_SLOTS
            st = {}

            def s_logits(w):
                st[w, "z"] = _dot_nt(q_heads[w[0]][w[1]:w[1] + tr], keys)

            def s_terms(w):
                valid = _sb_valid(w, edge) if w[2] else None
                lsz, lk = _sb_terms(st.pop((w, "z")), valid)
                st[w, "parts"] = _bf16_parts(lk)
                st[w, "lsz"] = lsz if valid is None else jnp.where(valid, lsz, NEG_BIG)
                lksum_ref[slot, w[0], w[1]:w[1] + tr, :] = _row_sum_lanes(lk)

            def s_suffix(w):
                logw_ref[slot, w[0], w[1]:w[1] + tr, :] = st.pop((w, "lsz")) + _dot(st.pop((w, "parts")), tri2)

            return _sb_items(edge), [s_logits, s_terms, s_suffix]

        def back(j, edge):
            vv = v_ref[pl.ds(pl.multiple_of(j * tk, tk), tk), :]
            slot = j % SB_SLOTS
            st = {}

            def s_weights(w):
                h, rs = w[0], slice(w[1], w[1] + tr)
                c = c_ref[h, rs, :]
                st[w, "a"] = jnp.exp(logw_ref[slot, h, rs, :] + jnp.tile(c, (1, tk // LANES))).astype(BF16)
                car_ref[h, rs, :] = jnp.where(lane == j, c, car_ref[h, rs, :])
                c_ref[h, rs, :] = c + lksum_ref[slot, h, rs, :]

            def s_values(w):
                oacc_ref[w[0], w[1]:w[1] + tr, :] += _dot(st.pop((w, "a")), vv)

            return _sb_items(edge), [s_weights, s_values]

        first = i * ratio
        edge_tiles = [(first + m, m) for m in reversed(range(ratio))]

        def alive():
            return (jnp.max(c_ref[...]) >= SB_DEAD_CARRY).astype(jnp.int32)

        @pl.when(i == 0)
        def _():
            _emit_skewed(*[front(j, m) for j, m in edge_tiles])
            _emit_skewed(*[back(j, m) for j, m in edge_tiles])

        @pl.when(i > 0)
        def _():
            tiles = edge_tiles + [(first - 1, None)]
            _emit_skewed(*[front(j, m) for j, m in tiles])
            _emit_skewed(*[back(j, m) for j, m in tiles])

            @pl.when((alive() > 0) & (first >= 2))
            def _():
                _emit_skewed(front(first - 2, None))

                def step(state):
                    pending, _ = state
                    _emit_skewed(front(pending - 1, None), back(pending, None))
                    return pending - 1, alive()

                pending, live = lax.while_loop(lambda s: (s[0] > 0) & (s[1] > 0), step, (first - 2, jnp.int32(1)))

                @pl.when(live > 0)
                def _():
                    _emit_skewed(back(pending, None))

        o_ref[...] = jnp.where(low, oacc_ref[0], oacc_ref[1]).astype(BF16)

    return _call(
        body, (q, k, v), comm=comm, **_grid_ends(N_HEADS // 2, s_len // tq), name="sb_fwd",
        grid=(N_HEADS // 2, s_len // tq),
        in_specs=[pl.BlockSpec((tq, LANES), lambda p, i: (i, p)),
                  pl.BlockSpec((s_len, LANES), lambda p, i: (0, p)),
                  pl.BlockSpec((s_len, LANES), lambda p, i: (0, p))],
        out_specs=[pl.BlockSpec((tq, LANES), lambda p, i: (i, p)), pl.BlockSpec((2, tq, LANES), lambda p, i: (p, i, 0))],
        out_shape=[jax.ShapeDtypeStruct((s_len, N_HEADS * HEAD_DIM), BF16),
                   jax.ShapeDtypeStruct((N_HEADS, s_len, LANES), F32)],
        scratch_shapes=[pltpu.VMEM((2, tq, LANES), F32), pltpu.VMEM((2, tq, LANES), F32),
                        pltpu.VMEM((SB_SLOTS, 2, tq, tk), F32), pltpu.VMEM((SB_SLOTS, 2, tq, LANES), F32)],
        compiler_params=_cparams("arbitrary", "arbitrary"),
    )


def _sb_bwd(q, k, v, do, cars):
    s_len = q.shape[0]
    tq, tk, tr = SB_QUERIES, SB_KEYS, SB_ROWS
    nk, ratio = s_len // tk, tq // tk

    def body(q_ref, k_ref, v_ref, do_ref, car_ref, dq_ref, dk_ref, dv_ref,
             gleft_ref, dqacc_ref, dkacc_ref, dvacc_ref, logw_ref, lsz_ref, da_ref, a_ref, dz_ref):
        i = pl.program_id(1)

        @pl.when(i == 0)
        def _():
            dkacc_ref[...] = jnp.zeros_like(dkacc_ref)
            dvacc_ref[...] = jnp.zeros_like(dvacc_ref)

        qv = q_ref[...]
        dov = do_ref[...]
        low, tri_right2, tri_left2 = _sb_consts(tq, tk)
        lane = lax.broadcasted_iota(jnp.int32, (tr, LANES), 1)
        zero = jnp.zeros_like(qv)
        q_heads = (jnp.where(low, qv, zero), jnp.where(low, zero, qv))
        do_heads = (jnp.where(low, dov, zero), jnp.where(low, zero, dov))
        q_t = qv.astype(F32).T.astype(BF16)
        do_t = dov.astype(F32).T.astype(BF16)
        gleft_ref[...] = jnp.zeros_like(gleft_ref)
        dqacc_ref[...] = jnp.zeros_like(dqacc_ref)

        def front(j, edge):
            key_rows = pl.ds(pl.multiple_of(j * tk, tk), tk)
            keys, values = k_ref[key_rows, :], v_ref[key_rows, :]
            slot = j % SB_SLOTS
            st = {}

            def s_logits(w):
                h, rs = w[0], slice(w[1], w[1] + tr)
                st[w, "z"] = _dot_nt(q_heads[h][rs], keys)
                da_ref[slot, h, rs, :] = _dot_nt(do_heads[h][rs], values)

            def s_terms(w):
                h, rs = w[0], slice(w[1], w[1] + tr)
                valid = _sb_valid(w, edge) if w[2] else None
                lsz, lk = _sb_terms(st.pop((w, "z")), valid)
                st[w, "parts"] = _bf16_parts(lk)
                lsz = lsz if valid is None else jnp.where(valid, lsz, NEG_BIG)
                lsz_ref[slot, h, rs, :] = lsz
                st[w, "lszc"] = lsz + jnp.sum(jnp.where(lane == j, car_ref[h, rs, :], 0.0), axis=-1, keepdims=True)

            def s_suffix(w):
                logw_ref[slot, w[0], w[1]:w[1] + tr, :] = st.pop((w, "lszc")) + _dot(st.pop((w, "parts")), tri_right2)

            return _sb_items(edge), [s_logits, s_terms, s_suffix]

        def back(j, edge):
            kv = k_ref[pl.ds(pl.multiple_of(j * tk, tk), tk), :]
            slot = j % SB_SLOTS
            st = {}

            items = _sb_items(edge)
            head_rows = [[w[1] for w in items if w is not None and w[0] == h] for h in range(2)]

            def s_weights(w):
                h, rs = w[0], slice(w[1], w[1] + tr)
                a = jnp.exp(logw_ref[slot, h, rs, :])
                g = a * da_ref[slot, h, rs, :]
                a_ref[slot, h, rs, :] = a.astype(BF16)
                st[w, "g"], st[w, "parts"] = g, _bf16_parts(g)

            def s_prefix(w):
                st[w, "gs"] = _dot(st.pop((w, "parts")), tri_left2)

            def s_dz(w):
                h, rs = w[0], slice(w[1], w[1] + tr)
                g = st.pop((w, "g"))
                gleft = gleft_ref[h, rs, :]
                gsum = st.pop((w, "gs")) + jnp.tile(gleft, (1, tk // LANES))
                dz = (g - jnp.exp(lsz_ref[slot, h, rs, :]) * (g + gsum)).astype(BF16)
                st[w, "dz"] = dz
                dz_ref[slot, h, rs, :] = dz
                gleft_ref[h, rs, :] = gleft + _row_sum_lanes(g)

            def s_products(w):
                h, rs = w[0], slice(w[1], w[1] + tr)
                dqacc_ref[h, rs, :] += _dot(st.pop((w, "dz")), kv)
                if w[1] == head_rows[h][-1]:
                    feat = slice(h * HEAD_DIM, (h + 1) * HEAD_DIM)
                    hr = slice(head_rows[h][0], tq)
                    dkacc_ref[j, feat, :] += _dot(q_t[feat, hr], dz_ref[slot, h, hr, :])
                    dvacc_ref[j, feat, :] += _dot(do_t[feat, hr], a_ref[slot, h, hr, :])

            return items, [s_weights, s_prefix, s_dz, s_products]

        first = i * ratio
        tile_max = jnp.max(jnp.maximum(car_ref[0], car_ref[1]), axis=0, keepdims=True)
        start = jnp.clip(first + ratio - jnp.sum(jnp.where(tile_max >= SB_DEAD_CARRY, 1, 0)), 0, first)

        edge_tiles = [(first + m, m) for m in range(ratio)]

        @pl.when(start == first)
        def _():
            _emit_skewed(*[front(j, m) for j, m in edge_tiles])
            _emit_skewed(*[back(j, m) for j, m in edge_tiles])

        @pl.when(start == first - 1)
        def _():
            tiles = [(first - 1, None)] + edge_tiles
            _emit_skewed(*[front(j, m) for j, m in tiles])
            _emit_skewed(*[back(j, m) for j, m in tiles])

        @pl.when(start < first - 1)
        def _():
            _emit_skewed(front(start, None))

            def step(jj, carry):
                _emit_skewed(front(jj, None), back(jj - 1, None))
                return carry

            lax.fori_loop(start + 1, first, step, 0)
            _emit_skewed(front(first, 0), back(first - 1, None))
            for m in range(1, ratio):
                _emit_skewed(front(first + m, m), back(first + m - 1, m - 1))
            _emit_skewed(back(first + ratio - 1, ratio - 1))

        dq_ref[...] = (Q_SCALE * jnp.where(low, dqacc_ref[0], dqacc_ref[1])).astype(BF16)

        @pl.when(i == s_len // tq - 1)
        def _():
            for j in range(nk):
                dk_ref[j * tk:(j + 1) * tk, :] = dkacc_ref[j].T.astype(BF16)
                dv_ref[j * tk:(j + 1) * tk, :] = dvacc_ref[j].T.astype(BF16)

    qblk = pl.BlockSpec((tq, LANES), lambda p, i: (i, p))
    col_full = pl.BlockSpec((s_len, LANES), lambda p, i: (0, p))
    return pl.pallas_call(
        body, name="sb_bwd",
        grid=(N_HEADS // 2, s_len // tq),
        in_specs=[qblk, col_full, col_full, qblk, pl.BlockSpec((2, tq, LANES), lambda p, i: (p, i, 0))],
        out_specs=[qblk, col_full, col_full],
        out_shape=[jax.ShapeDtypeStruct((s_len, N_HEADS * HEAD_DIM), BF16)] * 3,
        scratch_shapes=[pltpu.VMEM((2, tq, LANES), F32), pltpu.VMEM((2, tq, LANES), F32),
                        pltpu.VMEM((nk, LANES, tk), F32), pltpu.VMEM((nk, LANES, tk), F32)]
        + [pltpu.VMEM((SB_SLOTS, 2, tq, tk), F32)] * 3 + [pltpu.VMEM((SB_SLOTS, 2, tq, tk), BF16)] * 2,
        compiler_params=_cparams("parallel", "arbitrary"),
    )(q, k, v, do, cars)


def _local_step(xs, tgt, gains, sinks, rel_bias, weights_of, ship):
    g1, gmix, g2, gfin = gains
    bkt = _rel_bucket_matrix()
    grads = {}

    def carried(outs, comm, count):
        return outs[:count], (list(outs[count:]) if comm is not None else None)

    wts = dict(weights_of(0, None))
    comm = ship("weights", 1)
    (x1, h1, a1, b1, u1), landed = carried(
        _ffn_fwd(xs, g1, wts["ffn1_w1t"], wts["ffn1_w3t"], wts["ffn1_w2"], "1", comm), comm, 5)
    wts.update(weights_of(1, landed))
    hm, qa, ka, va, qb, kb, vb, ga, gb = _proj_fwd(x1, gmix, wts["w_int"])
    oa = _swa_fwd(rel_bias, sinks, bkt, qa, ka, va)
    comm = ship("weights", 2)
    (ob, cars), landed = carried(_sb_fwd(qb, kb, vb, comm), comm, 2)
    wts.update(weights_of(2, landed))
    x2, mg = _merge_fwd(x1, oa, ob, ga, gb, wts["w_swa"], wts["w_sb"], wts["w_out"])
    x3, h3, a3, b3, u3 = _ffn_fwd(x2, g2, wts["ffn2_w1t"], wts["ffn2_w3t"], wts["ffn2_w2"], "2")
    dx3, loss, dgfin = _loss_fwd_bwd(x3, tgt, gfin)

    def grad_chain(items):
        prev = None
        for name, lhs, rhs in items:
            comm = None if prev is None else ship("grads", (prev[0],), prev[1])
            res = _matmul_tn_tiled(lhs, rhs, name, comm)
            if prev is not None:
                grads[(prev[0],)] = prev[1] if comm is None else res[1]
            prev = (name, {_GRAD_KEY[name]: res if comm is None else res[0]})
        return prev

    dx2, dg2, da3, db3, dx3b = _ffn_bwd(dx3, x2, g2, a3, b3, wts["ffn2_w1t"], wts["ffn2_w3t"], wts["ffn2_w2"], "2")
    last = grad_chain((("ffn2_w1", da3, h3), ("ffn2_w3", db3, h3), ("ffn2_w2", u3, dx3b)))
    comm = ship("grads", (last[0],), last[1])
    (doa, dob, dga, dgb, dpa, dpb, dx2b), landed = carried(
        _merge_bwd(dx2, oa, ob, ga, gb, wts["w_swa"], wts["w_sb"], wts["w_out"], comm), comm, 7)
    grads[(last[0],)] = last[1] if comm is None else landed[0]
    dqa, dka, dva, dtab, dsink = _swa_bwd(rel_bias, sinks, bkt, qa, ka, va, doa)

    big = {"w_out": _matmul_tn(mg, dx2b, "w_out"), "w_swa": _matmul_tn(oa, dpa, "w_swa"),
           "w_sb": _matmul_tn(ob, dpb, "w_sb")}
    dqb, dkb, dvb = _sb_bwd(qb, kb, vb, dob, cars)
    dpieces = (dqa, dka.astype(BF16), dva.astype(BF16), dqb, dkb, dvb, dga, dgb)
    big["w_int"] = _matmul_tn_stacked(dpieces, hm, "w_in")
    dx1, dgmix = _proj_bwd(dpieces, dx2, x1, gmix, wts["w_int"])

    comm = ship("grads", GROUPS[1], big)
    (dx0, dg1, da1, db1, dx1b), landed = carried(
        _ffn_bwd(dx1, xs, g1, a1, b1, wts["ffn1_w1t"], wts["ffn1_w3t"], wts["ffn1_w2"], "1", comm), comm, 5)
    grads[GROUPS[1]] = big if comm is None else landed[0]

    last = grad_chain((("ffn1_w1", da1, h1), ("ffn1_w3", db1, h1), ("ffn1_w2", u1, dx1b)))
    grads[(last[0],)] = last[1]

    small = {"gains": (dg1, dgmix, dg2, dgfin), "sinks": dsink[:, 0], "rel_bias": dtab[:, :N_HEADS]}
    return loss, dx0, small, grads


def _my_place():
    return lax.axis_index("x"), lax.axis_index("y"), lax.axis_index("c")


def _flip(v, bit):
    return 1 - v if bit else v


_RELATIONS = tuple((k >> 2 & 1, k >> 1 & 1, k & 1) for k in range(1, N_DEV))


def _gather_weights(blocks, tag):
    count = len(blocks)

    def body(*refs):
        x_refs, out_refs = refs[:count], refs[count:2 * count]
        send_sems, recv_sems, local_sems = refs[2 * count:]
        x, y, c = _my_place()
        me, sibling = (x, y, c), (x, y, 1 - c)
        chips = [(1 - x, y), (x, 1 - y), (1 - x, 1 - y)]

        def rows(s, px, py, pc):
            return out_refs[s].at[4 * px + 2 * py + pc]

        def copy(s, k, block, to, src=None):
            return pltpu.make_async_remote_copy(
                src_ref=rows(s, *block) if src is None else src, dst_ref=rows(s, *block),
                send_sem=send_sems.at[s, k], recv_sem=recv_sems.at[s, k],
                device_id=to, device_id_type=pl.DeviceIdType.MESH)

        mine = [pltpu.make_async_copy(x_refs[s], rows(s, *me), local_sems.at[s]) for s in range(count)]
        first, passed = [], []
        for s in range(count):
            mine[s].start()
            first.append(copy(s, 0, me, sibling, src=x_refs[s]))
            first += [copy(s, 1 + j, me, (*chip, c), src=x_refs[s]) for j, chip in enumerate(chips)]
        for cp in first:
            cp.start()
        for s in range(count):
            for j, chip in enumerate(chips):
                copy(s, 1 + j, (*chip, c), me).wait_recv()
                passed.append(copy(s, 4 + j, (*chip, c), sibling))
                passed[-1].start()
        for s in range(count):
            copy(s, 0, sibling, me).wait_recv()
            for j, chip in enumerate(chips):
                copy(s, 4 + j, (*chip, 1 - c), me).wait_recv()
        for cp in first + passed:
            cp.wait_send()
        for cp in mine:
            cp.wait()

    anywhere = pl.BlockSpec(memory_space=pl.ANY)
    return pl.pallas_call(
        body, name=f"gather_weights_{tag}",
        out_shape=[jax.ShapeDtypeStruct((N_DEV,) + b.shape, b.dtype) for b in blocks],
        in_specs=[anywhere] * count, out_specs=[anywhere] * count,
        scratch_shapes=[pltpu.SemaphoreType.DMA((count, N_DEV - 1)), pltpu.SemaphoreType.DMA((count, N_DEV - 1)),
                        pltpu.SemaphoreType.DMA((count,))],
    )(*blocks)


def _exchange_grads(gp, tag):
    def body(g_ref, out_ref, send_sems, recv_sems, local_sem):
        x, y, c = _my_place()
        me = 4 * x + 2 * y + c
        mine = pltpu.make_async_copy(g_ref.at[me], out_ref.at[me], local_sem)
        mine.start()
        copies = []
        for k, (fx, fy, fc) in enumerate(_RELATIONS):
            px, py, pc = _flip(x, fx), _flip(y, fy), _flip(c, fc)
            peer = 4 * px + 2 * py + pc
            copies.append((
                pltpu.make_async_remote_copy(
                    src_ref=g_ref.at[peer], dst_ref=out_ref.at[me], send_sem=send_sems.at[k], recv_sem=recv_sems.at[k],
                    device_id=(px, py, pc), device_id_type=pl.DeviceIdType.MESH),
                pltpu.make_async_remote_copy(
                    src_ref=g_ref.at[peer], dst_ref=out_ref.at[peer], send_sem=send_sems.at[k], recv_sem=recv_sems.at[k],
                    device_id=(px, py, pc), device_id_type=pl.DeviceIdType.MESH)))
        for out_cp, _ in copies:
            out_cp.start()
        for _, in_cp in copies:
            in_cp.wait_recv()
        for out_cp, _ in copies:
            out_cp.wait_send()
        mine.wait()

    return pl.pallas_call(
        body, name=f"exchange_grads_{tag}",
        out_shape=jax.ShapeDtypeStruct(gp.shape, gp.dtype),
        in_specs=[pl.BlockSpec(memory_space=pl.ANY)],
        out_specs=pl.BlockSpec(memory_space=pl.ANY),
        scratch_shapes=[pltpu.SemaphoreType.DMA((7,)), pltpu.SemaphoreType.DMA((7,)), pltpu.SemaphoreType.DMA(())],
    )(gp)


def _peers():
    x, y, c = _my_place()
    out = []
    for k, (fx, fy, fc) in enumerate(_RELATIONS):
        px, py, pc = _flip(x, fx), _flip(y, fy), _flip(c, fc)
        out.append((k, (px, py, pc), 4 * px + 2 * py + pc))
    return out, 4 * x + 2 * y + c


def _grid_ends(*grid):
    def first():
        return functools.reduce(lambda a, b: a & b, [pl.program_id(d) == 0 for d in range(len(grid))])

    def last():
        return functools.reduce(lambda a, b: a & b, [pl.program_id(d) == n - 1 for d, n in enumerate(grid)])

    return {"first": first, "last": last}


def _call(body, operands, *, comm=None, first=None, last=None, **kw):
    if comm is None:
        return pl.pallas_call(body, **kw)(*operands)
    in_specs, out_specs, out_shape = list(kw.pop("in_specs")), list(kw.pop("out_specs")), list(kw.pop("out_shape"))
    scratch = list(kw.pop("scratch_shapes", ()))
    n_in, n_out, n_scr, n_src = len(in_specs), len(out_specs), len(scratch), len(comm)

    def wrapped(*refs):
        ins, src_refs = refs[:n_in], refs[n_in:n_in + n_src]
        outs = refs[n_in + n_src:n_in + n_src + n_out]
        land_refs = refs[n_in + n_src + n_out:n_in + 2 * n_src + n_out]
        scr = refs[n_in + 2 * n_src + n_out:n_in + 2 * n_src + n_out + n_scr]
        send_sems, recv_sems, local_sems = refs[n_in + 2 * n_src + n_out + n_scr:]
        peers, me = _peers()
        mine, going, coming = [], [], []
        for s, (_, per_peer) in enumerate(comm):
            src_ref, land_ref = src_refs[s], land_refs[s]
            mine.append(pltpu.make_async_copy(src_ref.at[me] if per_peer else src_ref, land_ref.at[me], local_sems.at[s]))
            for k, where, slab in peers:
                piece = src_ref.at[slab] if per_peer else src_ref
                going.append(pltpu.make_async_remote_copy(
                    src_ref=piece, dst_ref=land_ref.at[me], send_sem=send_sems.at[s, k], recv_sem=recv_sems.at[s, k],
                    device_id=where, device_id_type=pl.DeviceIdType.MESH))
                coming.append(pltpu.make_async_remote_copy(
                    src_ref=piece, dst_ref=land_ref.at[slab], send_sem=send_sems.at[s, k], recv_sem=recv_sems.at[s, k],
                    device_id=where, device_id_type=pl.DeviceIdType.MESH))

        @pl.when(first())
        def _():
            for cp in mine + going:
                cp.start()

        body(*ins, *outs, *scr)

        @pl.when(last())
        def _():
            for cp in coming:
                cp.wait_recv()
            for cp in going:
                cp.wait_send()
            for cp in mine:
                cp.wait()

    anywhere = pl.BlockSpec(memory_space=pl.ANY)
    lands = [jax.ShapeDtypeStruct(src.shape if per_peer else (N_DEV,) + src.shape, src.dtype) for src, per_peer in comm]
    return pl.pallas_call(
        wrapped, in_specs=in_specs + [anywhere] * n_src, out_specs=out_specs + [anywhere] * n_src,
        out_shape=out_shape + lands,
        scratch_shapes=scratch + [pltpu.SemaphoreType.DMA((n_src, N_DEV - 1)), pltpu.SemaphoreType.DMA((n_src, N_DEV - 1)),
                                  pltpu.SemaphoreType.DMA((n_src,))],
        **kw)(*operands, *[src for src, _ in comm])


def _adamw(w, g, m, v):
    m = ADAM_B1 * m + (1.0 - ADAM_B1) * g
    v = ADAM_B2 * v + (1.0 - ADAM_B2) * jnp.square(g)
    m_hat = m / (1.0 - ADAM_B1 ** ADAM_STEP)
    v_hat = v / (1.0 - ADAM_B2 ** ADAM_STEP)
    delta = -ADAM_LR * (m_hat / (jnp.sqrt(v_hat) + ADAM_EPS) + ADAM_WD * w)
    return delta, m, v


def _sum_and_adamw(parts, w, m, v, tr, tag):
    rows = w.shape[0]
    assert rows % tr == 0

    def body(p_ref, w_ref, m_ref, v_ref, g_out, d_out, m_out, v_out):
        g = p_ref[0].astype(F32)
        for d in range(1, N_DEV):
            g = g + p_ref[d].astype(F32)
        delta, mn, vn = _adamw(w_ref[...], g, m_ref[...], v_ref[...])
        g_out[...] = g
        d_out[...] = delta
        m_out[...] = mn
        v_out[...] = vn

    sp = pl.BlockSpec((tr, D_MODEL), lambda i: (i, 0))
    return pl.pallas_call(
        body, name=f"sum_and_adamw_{tag}",
        grid=(rows // tr,),
        in_specs=[pl.BlockSpec((N_DEV, tr, D_MODEL), lambda i: (0, i, 0)), sp, sp, sp],
        out_specs=[sp] * 4,
        out_shape=[jax.ShapeDtypeStruct(w.shape, F32)] * 4,
        compiler_params=_cparams("parallel"),
    )(parts, w, m, v)


def _small_allreduce_adamw(part, w, m, v):
    def body(p_ref, w_ref, m_ref, v_ref, g_out, d_out, m_out, v_out, buf, send_sems, recv_sems):
        x, y, c = _my_place()
        me = 4 * x + 2 * y + c
        buf[me] = p_ref[...]
        copies = []
        for k, (fx, fy, fc) in enumerate(_RELATIONS):
            px, py, pc = _flip(x, fx), _flip(y, fy), _flip(c, fc)
            peer = 4 * px + 2 * py + pc
            copies.append((
                pltpu.make_async_remote_copy(
                    src_ref=buf.at[me], dst_ref=buf.at[me], send_sem=send_sems.at[k], recv_sem=recv_sems.at[k],
                    device_id=(px, py, pc), device_id_type=pl.DeviceIdType.MESH),
                pltpu.make_async_remote_copy(
                    src_ref=buf.at[me], dst_ref=buf.at[peer], send_sem=send_sems.at[k], recv_sem=recv_sems.at[k],
                    device_id=(px, py, pc), device_id_type=pl.DeviceIdType.MESH)))
        for out_cp, _ in copies:
            out_cp.start()
        for _, in_cp in copies:
            in_cp.wait_recv()
        for out_cp, _ in copies:
            out_cp.wait_send()
        g = buf[0]
        for d in range(1, N_DEV):
            g = g + buf[d]
        delta, mn, vn = _adamw(w_ref[...], g, m_ref[...], v_ref[...])
        g_out[...] = g
        d_out[...] = delta
        m_out[...] = mn
        v_out[...] = vn

    vm = pl.BlockSpec(memory_space=pltpu.VMEM)
    return pl.pallas_call(
        body, name="small_allreduce_adamw",
        in_specs=[vm] * 4, out_specs=[vm] * 4,
        out_shape=[jax.ShapeDtypeStruct(w.shape, F32)] * 4,
        scratch_shapes=[pltpu.VMEM((N_DEV,) + part.shape, F32),
                        pltpu.SemaphoreType.DMA((7,)), pltpu.SemaphoreType.DMA((7,))],
    )(part, w, m, v)


_TRANSPOSED = ("ffn1_w1", "ffn1_w3", "w_in", "ffn2_w1", "ffn2_w3")
_BRANCH = ("w_branch_swa", "w_branch_sb")


def _pack_shards(t, names):
    parts = []
    for name in names:
        a = t[name][0]
        if name in _TRANSPOSED:
            a = a.T
        elif name in _BRANCH:
            a = a.reshape(64, D_MODEL)
        parts.append(a)
    return jnp.concatenate(parts, axis=0)


def _unpack_shards(p, names):
    out, lo = {}, 0
    for name in names:
        a = p[lo:lo + BIG_ROWS[BIG_NAMES.index(name)]]
        lo += a.shape[0]
        if name in _TRANSPOSED:
            a = a.T
        elif name in _BRANCH:
            a = a.reshape(512, 128)
        out[name] = a[None]
    return out


def _full_weights(zones, names):
    out = {}
    for name, a in zip(names, zones):
        if name in _BRANCH:
            a = a.reshape(N_DEV, 512, 128).transpose(1, 0, 2).reshape(512, D_MODEL)
        out[_GRAD_KEY[name]] = a.reshape(-1, D_MODEL)
    return out


_GRAD_KEY = {"ffn1_w1": "ffn1_w1t", "ffn1_w3": "ffn1_w3t", "ffn1_w2": "ffn1_w2", "w_in": "w_int",
             "w_branch_swa": "w_swa", "w_branch_sb": "w_sb", "w_out": "w_out",
             "ffn2_w1": "ffn2_w1t", "ffn2_w3": "ffn2_w3t", "ffn2_w2": "ffn2_w2"}


def _pack_full_grads(big, names):
    parts = []
    for name in names:
        a = big[_GRAD_KEY[name]]
        if name in _BRANCH:
            a = a.reshape(512, N_DEV, 128).transpose(1, 0, 2)
        parts.append(a.reshape(N_DEV, BIG_ROWS[BIG_NAMES.index(name)], D_MODEL).astype(BF16))
    return jnp.concatenate(parts, axis=1)


_SMALL_NAMES = ("norm_ffn1", "norm_mix", "norm_ffn2", "norm_final", "swa_sinks", "rel_bias")


def _pack_small(vals):
    rows = []
    for a in vals:
        a = a.reshape(-1)
        rows.append(jnp.pad(a, (0, D_MODEL - a.shape[0])))
    rows += [jnp.zeros((D_MODEL,), F32)] * (SMALL_ROWS - len(rows))
    return jnp.stack(rows)


def _unpack_small(p):
    return {"norm_ffn1": p[0:1], "norm_mix": p[1:2], "norm_ffn2": p[2:3], "norm_final": p[3],
            "swa_sinks": p[4:5, :N_HEADS], "rel_bias": p[5, :REL_BUCKETS * N_HEADS].reshape(REL_BUCKETS, N_HEADS)}


ALL_NAMES = ("norm_ffn1", "ffn1_w1", "ffn1_w3", "ffn1_w2", "norm_mix", "w_in", "swa_sinks", "rel_bias",
             "w_branch_swa", "w_branch_sb", "w_out", "norm_ffn2", "ffn2_w1", "ffn2_w3", "ffn2_w2", "norm_final")


def kernel(x, norm_ffn1, ffn1_w1, ffn1_w3, ffn1_w2, norm_mix, w_in, swa_sinks, rel_bias, w_branch_swa, w_branch_sb, w_out, norm_ffn2, ffn2_w1, ffn2_w3, ffn2_w2, norm_final, loss_target, m_norm_ffn1, m_ffn1_w1, m_ffn1_w3, m_ffn1_w2, m_norm_mix, m_w_in, m_swa_sinks, m_rel_bias, m_w_branch_swa, m_w_branch_sb, m_w_out, m_norm_ffn2, m_ffn2_w1, m_ffn2_w3, m_ffn2_w2, m_norm_final, v_norm_ffn1, v_ffn1_w1, v_ffn1_w3, v_ffn1_w2, v_norm_mix, v_w_in, v_swa_sinks, v_rel_bias, v_w_branch_swa, v_w_branch_sb, v_w_out, v_norm_ffn2, v_ffn2_w1, v_ffn2_w3, v_ffn2_w2, v_norm_final):
    w = dict(zip(ALL_NAMES, (norm_ffn1, ffn1_w1, ffn1_w3, ffn1_w2, norm_mix, w_in, swa_sinks, rel_bias,
                             w_branch_swa, w_branch_sb, w_out, norm_ffn2, ffn2_w1, ffn2_w3, ffn2_w2, norm_final)))
    m = dict(zip(ALL_NAMES, (m_norm_ffn1, m_ffn1_w1, m_ffn1_w3, m_ffn1_w2, m_norm_mix, m_w_in, m_swa_sinks, m_rel_bias,
                             m_w_branch_swa, m_w_branch_sb, m_w_out, m_norm_ffn2, m_ffn2_w1, m_ffn2_w3, m_ffn2_w2,
                             m_norm_final)))
    v = dict(zip(ALL_NAMES, (v_norm_ffn1, v_ffn1_w1, v_ffn1_w3, v_ffn1_w2, v_norm_mix, v_w_in, v_swa_sinks, v_rel_bias,
                             v_w_branch_swa, v_w_branch_sb, v_w_out, v_norm_ffn2, v_ffn2_w1, v_ffn2_w3, v_ffn2_w2,
                             v_norm_final)))

    def my_blocks(group):
        return [_pack_shards(w, (name,)).astype(BF16) for name in GROUPS[group]]

    gathered0 = _gather_weights(my_blocks(0), "group0")

    def weights_of(group, landed):
        return _full_weights(gathered0 if group == 0 else landed, GROUPS[group])

    def ship(kind, which, grads=None):
        if kind == "weights":
            return [(block, False) for block in my_blocks(which)]
        return [(_pack_full_grads(grads, which), True)]

    gains = (norm_ffn1, norm_mix, norm_ffn2, norm_final.reshape(1, D_MODEL))
    loss, dx, small, parts = _local_step(x[0], loss_target[0], gains, swa_sinks, rel_bias, weights_of, ship)

    big_outs = [{}, {}, {}, {}]
    for names, tile in zip(SUM_GROUPS, SUM_TILE):
        landed = parts[names]
        if isinstance(landed, dict):
            landed = _exchange_grads(_pack_full_grads(landed, names), names[0])
        res = _sum_and_adamw(landed, _pack_shards(w, names), _pack_shards(m, names), _pack_shards(v, names),
                             tile, names[0])
        for acc, packed in zip(big_outs, res):
            acc.update(_unpack_shards(packed, names))
    g_big, d_big, m_big, v_big = big_outs

    small_part = _pack_small(small["gains"] + (small["sinks"], small["rel_bias"], loss))
    zero = jnp.zeros((1,), F32)
    small_res = _small_allreduce_adamw(
        small_part, _pack_small([w[n] for n in _SMALL_NAMES] + [zero]), _pack_small([m[n] for n in _SMALL_NAMES] + [zero]),
        _pack_small([v[n] for n in _SMALL_NAMES] + [zero]))
    g_sm, d_sm, m_sm, v_sm = (_unpack_small(p) for p in small_res)

    outs = [small_res[0][len(_SMALL_NAMES), 0], dx[None]]
    for big_d, small_d in ((g_big, g_sm), (d_big, d_sm), (m_big, m_sm), (v_big, v_sm)):
        merged = {**big_d, **small_d}
        outs += [merged[n] for n in ALL_NAMES]
    return tuple(outs)
```

```python
import functools

import jax
import jax.numpy as jnp
import numpy as np
from jax import lax
from jax.experimental import pallas as pl
from jax.experimental.pallas import tpu as pltpu

F32 = jnp.float32
BF16 = jnp.bfloat16

D_MODEL = 1024
D_FF = 2816
HEAD_DIM = 64
N_HEADS = 8
SWA_KV_HEADS = 2
SWA_GROUP = 4
SWA_BLOCK = 128
REL_BUCKETS = 32
REL_MAX_DIST = 128
RMS_EPS = 1e-6
NEG_BIG = -1e30
Q_SCALE = HEAD_DIM ** -0.5
LANES = 128

N_DEV = 8

ADAM_LR = 0.001
ADAM_B1 = 0.9
ADAM_B2 = 0.999
ADAM_EPS = 1e-08
ADAM_WD = 0.01
ADAM_STEP = 10

IN_SIZES = (512, 128, 128, 512, 512, 512, 1024, 1024)
IN_OFFS = tuple(int(v) for v in np.cumsum((0,) + IN_SIZES))
IN_W = IN_OFFS[-1]

BIG_NAMES = ("ffn1_w1", "ffn1_w3", "ffn1_w2", "w_in", "w_branch_swa", "w_branch_sb", "w_out",
             "ffn2_w1", "ffn2_w3", "ffn2_w2")
BIG_ROWS = (352, 352, 352, 544, 64, 64, 128, 352, 352, 352)
SMALL_ROWS = 8
GROUPS = (BIG_NAMES[0:3], BIG_NAMES[3:7], BIG_NAMES[7:10])
SUM_GROUPS = tuple((n,) for n in BIG_NAMES)
SUM_TILE = (176, 176, 176, 272, 64, 64, 128, 176, 176, 176)

VMEM_LIMIT = 56 * 1024 * 1024
FFN_PIECES = 2
SB_QUERIES = 512
SB_KEYS = 256
SB_ROWS = 256
SB_SLOTS = 3
SB_SUM_PARTS = 1
SB_LOGIT_CAP = 80.0
SB_DEAD_CARRY = -110.0


def _dot(a, b):
    return jnp.dot(a, b, preferred_element_type=F32)


def _dot_nt(a, b):
    return lax.dot_general(a, b, (((1,), (1,)), ((), ())), preferred_element_type=F32)


def _dot_tn(a, b):
    return lax.dot_general(a, b, (((0,), (0,)), ((), ())), preferred_element_type=F32)


def _cparams(*sem):
    return pltpu.CompilerParams(dimension_semantics=sem, vmem_limit_bytes=VMEM_LIMIT)


def _rms_rstd(xv):
    return lax.rsqrt(jnp.mean(xv * xv, axis=-1, keepdims=True) + RMS_EPS)


def _rms_bwd(dh, xv, r, g):
    xhat = xv * r
    dg = jnp.sum(dh * xhat, axis=0, keepdims=True)
    dxn = dh * g
    dx = r * (dxn - xhat * jnp.mean(dxn * xhat, axis=-1, keepdims=True))
    return dx, dg


def _ff_tile_spec(tm, tf):
    return pl.BlockSpec((1, tm, tf), lambda i, j: (j, i, 0))


def _ffn_fwd(x, g, w1t, w3t, w2, tag, comm=None):
    s_len = x.shape[0]
    tm, tf = min(1024, s_len), 256
    nf = D_FF // tf

    def body(x_ref, g_ref, w1_ref, w3_ref, w2_ref, xo_ref, h_ref, a_ref, b_ref, u_ref, acc_ref, hs_ref):
        j = pl.program_id(1)

        @pl.when(j == 0)
        def _():
            xv = x_ref[...]
            h = (xv * _rms_rstd(xv) * g_ref[...]).astype(BF16)
            hs_ref[...] = h
            h_ref[...] = h
            acc_ref[...] = jnp.zeros_like(acc_ref)

        st = {}

        def s_up(rs):
            h = hs_ref[rs, :]
            st[rs.start, "ab"] = (_dot_nt(h, w1_ref[...]), _dot_nt(h, w3_ref[...]))

        def s_act(rs):
            a, b = st.pop((rs.start, "ab"))
            a_ref[0, rs, :] = a.astype(BF16)
            b_ref[0, rs, :] = b.astype(BF16)
            uh = (0.5 * (a * jax.nn.sigmoid(a) * b)).astype(BF16)
            u_ref[0, rs, :] = uh
            st[rs.start, "u"] = uh

        def s_down(rs):
            acc_ref[rs, :] += _dot(st.pop((rs.start, "u")), w2_ref[...])

        _emit_skewed(([slice(r, r + tm // FFN_PIECES) for r in range(0, tm, tm // FFN_PIECES)], [s_up, s_act, s_down]))

        @pl.when(j == nf - 1)
        def _():
            xo_ref[...] = x_ref[...] + acc_ref[...]

    row = lambda i, j: (i, 0)
    return _call(
        body, (x, g, w1t, w3t, w2), comm=comm, **_grid_ends(s_len // tm, nf), name=f"ffn_fwd_{tag}",
        grid=(s_len // tm, nf),
        in_specs=[pl.BlockSpec((tm, D_MODEL), row), pl.BlockSpec((1, D_MODEL), lambda i, j: (0, 0)),
                  pl.BlockSpec((tf, D_MODEL), lambda i, j: (j, 0)), pl.BlockSpec((tf, D_MODEL), lambda i, j: (j, 0)),
                  pl.BlockSpec((tf, D_MODEL), lambda i, j: (j, 0))],
        out_specs=[pl.BlockSpec((tm, D_MODEL), row), pl.BlockSpec((tm, D_MODEL), row)] + [_ff_tile_spec(tm, tf)] * 3,
        out_shape=[jax.ShapeDtypeStruct((s_len, D_MODEL), F32), jax.ShapeDtypeStruct((s_len, D_MODEL), BF16)]
        + [jax.ShapeDtypeStruct((nf, s_len, tf), BF16)] * 3,
        scratch_shapes=[pltpu.VMEM((tm, D_MODEL), F32), pltpu.VMEM((tm, D_MODEL), BF16)],
        compiler_params=_cparams("arbitrary", "arbitrary"),
    )


def _ffn_bwd(dy, x, g, a, b, w1t, w3t, w2, tag, comm=None):
    s_len = x.shape[0]
    tm, tf = min(1024, s_len), 256
    nf = D_FF // tf

    def body(dy_ref, x_ref, g_ref, a_ref, b_ref, w1_ref, w3_ref, w2_ref,
             dx_ref, dg_ref, da_ref, db_ref, dyb_ref, acc_ref, dys_ref):
        i, j = pl.program_id(0), pl.program_id(1)

        @pl.when(j == 0)
        def _():
            dyb = dy_ref[...].astype(BF16)
            dys_ref[...] = 0.5 * dyb
            dyb_ref[...] = dyb
            acc_ref[...] = jnp.zeros_like(acc_ref)

        @pl.when((i == 0) & (j == 0))
        def _():
            dg_ref[...] = jnp.zeros_like(dg_ref)

        st = {}

        def s_du(rs):
            st[rs.start, "du"] = _dot_nt(dys_ref[rs, :], w2_ref[...])

        def s_act(rs):
            du = st.pop((rs.start, "du"))
            av = a_ref[0, rs, :].astype(F32)
            bv = b_ref[0, rs, :].astype(F32)
            sg = jax.nn.sigmoid(av)
            sil = av * sg
            da = (du * bv * (sg + sil * (1.0 - sg))).astype(BF16)
            db = (du * sil).astype(BF16)
            da_ref[0, rs, :] = da
            db_ref[0, rs, :] = db
            st[rs.start, "dab"] = (da, db)

        def s_dh(rs):
            da, db = st.pop((rs.start, "dab"))
            acc_ref[rs, :] += _dot(da, w1_ref[...]) + _dot(db, w3_ref[...])

        _emit_skewed(([slice(r, r + tm // FFN_PIECES) for r in range(0, tm, tm // FFN_PIECES)], [s_du, s_act, s_dh]))

        @pl.when(j == nf - 1)
        def _():
            xv = x_ref[...]
            dx, dg = _rms_bwd(acc_ref[...], xv, _rms_rstd(xv), g_ref[...])
            dx_ref[...] = dy_ref[...] + dx
            dg_ref[...] += dg

    row = lambda i, j: (i, 0)
    wsp = pl.BlockSpec((tf, D_MODEL), lambda i, j: (j, 0))
    return _call(
        body, (dy, x, g, a, b, w1t, w3t, w2), comm=comm, **_grid_ends(s_len // tm, nf), name=f"ffn_bwd_{tag}",
        grid=(s_len // tm, nf),
        in_specs=[pl.BlockSpec((tm, D_MODEL), row), pl.BlockSpec((tm, D_MODEL), row),
                  pl.BlockSpec((1, D_MODEL), lambda i, j: (0, 0)),
                  _ff_tile_spec(tm, tf), _ff_tile_spec(tm, tf), wsp, wsp, wsp],
        out_specs=[pl.BlockSpec((tm, D_MODEL), row), pl.BlockSpec((1, D_MODEL), lambda i, j: (0, 0)),
                   _ff_tile_spec(tm, tf), _ff_tile_spec(tm, tf), pl.BlockSpec((tm, D_MODEL), row)],
        out_shape=[jax.ShapeDtypeStruct((s_len, D_MODEL), F32), jax.ShapeDtypeStruct((1, D_MODEL), F32),
                   jax.ShapeDtypeStruct((nf, s_len, tf), BF16), jax.ShapeDtypeStruct((nf, s_len, tf), BF16),
                   jax.ShapeDtypeStruct((s_len, D_MODEL), BF16)],
        scratch_shapes=[pltpu.VMEM((tm, D_MODEL), F32), pltpu.VMEM((tm, D_MODEL), BF16)],
        compiler_params=_cparams("arbitrary", "arbitrary"),
    )


def _matmul_tn(lhs, rhs, tag, comm=None):
    s_len, m = lhs.shape
    n = rhs.shape[1]
    tm = min(512, s_len)
    tj = m if m <= 1024 else 1408
    assert m % tj == 0
    last_rows = s_len // tm - 1

    def body(l_ref, r_ref, o_ref, acc_ref):
        i = pl.program_id(1)

        @pl.when(i == 0)
        def _():
            acc_ref[...] = jnp.zeros_like(acc_ref)

        acc_ref[...] += _dot_tn(l_ref[...], r_ref[...])

        @pl.when(i == last_rows)
        def _():
            o_ref[...] = acc_ref[...].astype(BF16)

    res = _call(
        body, (lhs, rhs), comm=comm, **_grid_ends(m // tj, s_len // tm), name=f"matmul_tn_{tag}",
        grid=(m // tj, s_len // tm),
        in_specs=[pl.BlockSpec((tm, tj), lambda j, i: (i, j)), pl.BlockSpec((tm, n), lambda j, i: (i, 0))],
        out_specs=[pl.BlockSpec((tj, n), lambda j, i: (j, 0))],
        out_shape=[jax.ShapeDtypeStruct((m, n), BF16)],
        scratch_shapes=[pltpu.VMEM((tj, n), F32)],
        compiler_params=_cparams("arbitrary", "arbitrary"),
    )
    return res[0] if comm is None else tuple(res)


def _matmul_tn_tiled(lhs, rhs, tag, comm=None):
    nf, s_len, tf = lhs.shape
    n = rhs.shape[1]
    tm = min(512, s_len)
    last_rows = s_len // tm - 1

    def body(l_ref, r_ref, o_ref, acc_ref):
        i = pl.program_id(0)

        @pl.when(i == 0)
        def _():
            acc_ref[...] = jnp.zeros_like(acc_ref)

        rv = r_ref[...]
        for t in range(nf):
            acc_ref[t * tf:(t + 1) * tf, :] += _dot_tn(l_ref[t], rv)

        @pl.when(i == last_rows)
        def _():
            o_ref[...] = acc_ref[...].astype(BF16)

    res = _call(
        body, (lhs, rhs), comm=comm, **_grid_ends(s_len // tm), name=f"matmul_tn_{tag}",
        grid=(s_len // tm,),
        in_specs=[pl.BlockSpec((nf, tm, tf), lambda i: (0, i, 0)), pl.BlockSpec((tm, n), lambda i: (i, 0))],
        out_specs=[pl.BlockSpec((nf * tf, n), lambda i: (0, 0))],
        out_shape=[jax.ShapeDtypeStruct((nf * tf, n), BF16)],
        scratch_shapes=[pltpu.VMEM((nf * tf, n), F32)],
        compiler_params=_cparams("arbitrary"),
    )
    return res[0] if comm is None else tuple(res)


def _matmul_tn_stacked(pieces, rhs, tag):
    s_len, n = rhs.shape
    widths = [p.shape[1] for p in pieces]
    offs = [sum(widths[:k]) for k in range(len(widths) + 1)]
    tm = min(256, s_len)
    last_rows = s_len // tm - 1

    def body(*refs):
        l_refs, r_ref, o_ref, acc_ref = refs[:len(pieces)], refs[-3], refs[-2], refs[-1]
        i = pl.program_id(0)

        @pl.when(i == 0)
        def _():
            acc_ref[...] = jnp.zeros_like(acc_ref)

        rv = r_ref[...]
        for k, l_ref in enumerate(l_refs):
            acc_ref[offs[k]:offs[k + 1], :] += _dot_tn(l_ref[...], rv)

        @pl.when(i == last_rows)
        def _():
            o_ref[...] = acc_ref[...].astype(BF16)

    row = lambda i: (i, 0)
    return pl.pallas_call(
        body, name=f"matmul_tn_{tag}",
        grid=(s_len // tm,),
        in_specs=[pl.BlockSpec((tm, w), row) for w in widths] + [pl.BlockSpec((tm, n), row)],
        out_specs=pl.BlockSpec((offs[-1], n), lambda i: (0, 0)),
        out_shape=jax.ShapeDtypeStruct((offs[-1], n), BF16),
        scratch_shapes=[pltpu.VMEM((offs[-1], n), F32)],
        compiler_params=_cparams("arbitrary"),
    )(*pieces, rhs)


def _proj_fwd(x1, g, wint):
    s_len = x1.shape[0]
    tm = min(512, s_len)
    dts = (BF16, BF16, BF16, BF16, BF16, BF16, F32, F32)

    def body(x_ref, g_ref, w_ref, h_ref, *outs):
        xv = x_ref[...]
        h = (xv * _rms_rstd(xv) * g_ref[...]).astype(BF16)
        h_ref[...] = h
        for p, o_ref in enumerate(outs):
            val = _dot_nt(h, w_ref[IN_OFFS[p]:IN_OFFS[p + 1], :])
            if p == 3:
                val = val * Q_SCALE
            o_ref[...] = val.astype(dts[p])

    row = lambda i: (i, 0)
    return pl.pallas_call(
        body, name="proj_fwd",
        grid=(s_len // tm,),
        in_specs=[pl.BlockSpec((tm, D_MODEL), row), pl.BlockSpec((1, D_MODEL), lambda i: (0, 0)),
                  pl.BlockSpec((IN_W, D_MODEL), lambda i: (0, 0))],
        out_specs=[pl.BlockSpec((tm, D_MODEL), row)] + [pl.BlockSpec((tm, w), row) for w in IN_SIZES],
        out_shape=[jax.ShapeDtypeStruct((s_len, D_MODEL), BF16)]
        + [jax.ShapeDtypeStruct((s_len, w), dt) for w, dt in zip(IN_SIZES, dts)],
        compiler_params=_cparams("parallel"),
    )(x1, g, wint)


def _proj_bwd(dpieces, dx2, x1, g, wint):
    s_len = x1.shape[0]
    tm = min(512, s_len)

    def body(*refs):
        dps = refs[:8]
        dx2_ref, x_ref, g_ref, w_ref, dx_ref, dg_ref = refs[8:]

        @pl.when(pl.program_id(0) == 0)
        def _():
            dg_ref[...] = jnp.zeros_like(dg_ref)

        dh = _dot(dps[0][...], w_ref[IN_OFFS[0]:IN_OFFS[1], :])
        for p in range(1, 8):
            dh += _dot(dps[p][...], w_ref[IN_OFFS[p]:IN_OFFS[p + 1], :])
        xv = x_ref[...]
        dx, dg = _rms_bwd(dh, xv, _rms_rstd(xv), g_ref[...])
        dx_ref[...] = dx2_ref[...] + dx
        dg_ref[...] += dg

    row = lambda i: (i, 0)
    return pl.pallas_call(
        body, name="proj_bwd",
        grid=(s_len // tm,),
        in_specs=[pl.BlockSpec((tm, w), row) for w in IN_SIZES]
        + [pl.BlockSpec((tm, D_MODEL), row), pl.BlockSpec((tm, D_MODEL), row),
           pl.BlockSpec((1, D_MODEL), lambda i: (0, 0)), pl.BlockSpec((IN_W, D_MODEL), lambda i: (0, 0))],
        out_specs=[pl.BlockSpec((tm, D_MODEL), row), pl.BlockSpec((1, D_MODEL), lambda i: (0, 0))],
        out_shape=[jax.ShapeDtypeStruct((s_len, D_MODEL), F32), jax.ShapeDtypeStruct((1, D_MODEL), F32)],
        compiler_params=_cparams("arbitrary"),
    )(*dpieces, dx2, x1, g, wint)


def _merge_fwd(x1, oa, ob, ga, gb, wswa, wsb, wout):
    s_len = x1.shape[0]
    tm = min(512, s_len)

    def body(x_ref, oa_ref, ob_ref, ga_ref, gb_ref, wa_ref, wb_ref, wo_ref, xo_ref, mg_ref):
        pa = _dot(oa_ref[...], wa_ref[...])
        pb = _dot(ob_ref[...], wb_ref[...])
        mg = (jax.nn.sigmoid(ga_ref[...]) * pa + jax.nn.sigmoid(gb_ref[...]) * pb).astype(BF16)
        mg_ref[...] = mg
        xo_ref[...] = x_ref[...] + _dot(mg, wo_ref[...])

    row = lambda i: (i, 0)
    full = lambda i: (0, 0)
    return pl.pallas_call(
        body, name="merge_fwd",
        grid=(s_len // tm,),
        in_specs=[pl.BlockSpec((tm, D_MODEL), row), pl.BlockSpec((tm, 512), row), pl.BlockSpec((tm, 512), row),
                  pl.BlockSpec((tm, D_MODEL), row), pl.BlockSpec((tm, D_MODEL), row),
                  pl.BlockSpec((512, D_MODEL), full), pl.BlockSpec((512, D_MODEL), full),
                  pl.BlockSpec((D_MODEL, D_MODEL), full)],
        out_specs=[pl.BlockSpec((tm, D_MODEL), row), pl.BlockSpec((tm, D_MODEL), row)],
        out_shape=[jax.ShapeDtypeStruct((s_len, D_MODEL), F32), jax.ShapeDtypeStruct((s_len, D_MODEL), BF16)],
        compiler_params=_cparams("parallel"),
    )(x1, oa, ob, ga, gb, wswa, wsb, wout)


def _merge_bwd(dx2, oa, ob, ga, gb, wswa, wsb, wout, comm=None):
    s_len = dx2.shape[0]
    tm = min(512, s_len)

    def body(dx_ref, oa_ref, ob_ref, ga_ref, gb_ref, wa_ref, wb_ref, wo_ref,
             doa_ref, dob_ref, dga_ref, dgb_ref, dpa_ref, dpb_ref, dxb_ref):
        dxb = dx_ref[...].astype(BF16)
        dxb_ref[...] = dxb
        dmg = _dot_nt(dxb, wo_ref[...])
        for o_ref, g_ref, w_ref, do_ref, dg_ref, dp_ref in (
                (oa_ref, ga_ref, wa_ref, doa_ref, dga_ref, dpa_ref),
                (ob_ref, gb_ref, wb_ref, dob_ref, dgb_ref, dpb_ref)):
            pv = _dot(o_ref[...], w_ref[...])
            sg = jax.nn.sigmoid(g_ref[...])
            dp = (dmg * sg).astype(BF16)
            dp_ref[...] = dp
            dg_ref[...] = (dmg * pv * sg * (1.0 - sg)).astype(BF16)
            do_ref[...] = _dot_nt(dp, w_ref[...]).astype(BF16)

    row = lambda i: (i, 0)
    full = lambda i: (0, 0)
    wide = pl.BlockSpec((tm, D_MODEL), row)
    half = pl.BlockSpec((tm, 512), row)
    return _call(
        body, (dx2, oa, ob, ga, gb, wswa, wsb, wout), comm=comm, **_grid_ends(s_len // tm), name="merge_bwd",
        grid=(s_len // tm,),
        in_specs=[wide, half, half, wide, wide, pl.BlockSpec((512, D_MODEL), full),
                  pl.BlockSpec((512, D_MODEL), full), pl.BlockSpec((D_MODEL, D_MODEL), full)],
        out_specs=[half, half, wide, wide, wide, wide, wide],
        out_shape=[jax.ShapeDtypeStruct((s_len, 512), BF16)] * 2 + [jax.ShapeDtypeStruct((s_len, D_MODEL), BF16)] * 5,
        compiler_params=_cparams("arbitrary"),
    )


def _loss_fwd_bwd(x3, tgt, g):
    s_len = x3.shape[0]
    tm = min(1024, s_len)

    def body(x_ref, t_ref, g_ref, dx_ref, loss_ref, dg_ref):
        @pl.when(pl.program_id(0) == 0)
        def _():
            loss_ref[...] = jnp.zeros_like(loss_ref)
            dg_ref[...] = jnp.zeros_like(dg_ref)

        xv = x_ref[...]
        gv = g_ref[...]
        r = _rms_rstd(xv)
        err = xv * r * gv - t_ref[...]
        loss_ref[...] += 0.5 * jnp.sum(jnp.mean(err * err, axis=-1, keepdims=True), axis=0, keepdims=True)
        dx, dg = _rms_bwd(err * (1.0 / D_MODEL), xv, r, gv)
        dx_ref[...] = dx
        dg_ref[...] += dg

    row = lambda i: (i, 0)
    return pl.pallas_call(
        body, name="loss_fwd_bwd",
        grid=(s_len // tm,),
        in_specs=[pl.BlockSpec((tm, D_MODEL), row), pl.BlockSpec((tm, D_MODEL), row),
                  pl.BlockSpec((1, D_MODEL), lambda i: (0, 0))],
        out_specs=[pl.BlockSpec((tm, D_MODEL), row), pl.BlockSpec((1, 1), lambda i: (0, 0)),
                   pl.BlockSpec((1, D_MODEL), lambda i: (0, 0))],
        out_shape=[jax.ShapeDtypeStruct((s_len, D_MODEL), F32), jax.ShapeDtypeStruct((1, 1), F32),
                   jax.ShapeDtypeStruct((1, D_MODEL), F32)],
        compiler_params=_cparams("arbitrary"),
    )(x3, tgt, g)


def _rel_bucket_matrix():
    qi = jnp.arange(SWA_BLOCK)[:, None] + SWA_BLOCK
    kj = jnp.arange(2 * SWA_BLOCK)[None, :]
    dist = jnp.maximum(qi - kj, 0)
    max_exact = REL_BUCKETS // 2
    d = jnp.maximum(dist, 1).astype(F32)
    large = max_exact + (jnp.log(d / max_exact) / np.log(REL_MAX_DIST / max_exact)
                         * (REL_BUCKETS - max_exact)).astype(jnp.int32)
    large = jnp.minimum(large, REL_BUCKETS - 1)
    return jnp.where(dist < max_exact, dist, large).astype(jnp.int32)


def _swa_bias_into(bias_ref, bkt_ref, tab_ref):
    bk = bkt_ref[...]
    for h in range(N_HEADS):
        acc = jnp.zeros(bk.shape, F32)
        for bucket in range(REL_BUCKETS):
            acc = jnp.where(bk == bucket, tab_ref[bucket, h], acc)
        bias_ref[h] = acc


def _swa_valid(n):
    shape = (SWA_BLOCK, 2 * SWA_BLOCK)
    row = lax.broadcasted_iota(jnp.int32, shape, 0)
    col = lax.broadcasted_iota(jnp.int32, shape, 1)
    dist = row + SWA_BLOCK - col
    return (dist >= 0) & (dist < SWA_BLOCK) & ((col >= SWA_BLOCK) | (n > 0))


def _swa_windows(kp_ref, kc_ref, vp_ref, vc_ref):
    return (jnp.concatenate([kp_ref[...], kc_ref[...]], axis=0), jnp.concatenate([vp_ref[...], vc_ref[...]], axis=0))


def _swa_place(h):
    return slice(h // 2 * LANES, (h // 2 + 1) * LANES), h % 2, h // SWA_GROUP


def _move_half(x, src, dst):
    moved = x if src == dst else pltpu.roll(x, HEAD_DIM, 1)
    in_dst = (lax.broadcasted_iota(jnp.int32, x.shape, 1) >= HEAD_DIM) == bool(dst)
    return jnp.where(in_dst, moved, 0.0)


def _swa_probs(qk, bias, sink, valid):
    lg = jnp.where(valid, qk * Q_SCALE + bias, NEG_BIG)
    m = jnp.maximum(jnp.max(lg, axis=-1, keepdims=True), sink)
    e = jnp.exp(lg - m)
    es = jnp.exp(sink - m)
    inv = 1.0 / (jnp.sum(e, axis=-1, keepdims=True) + es)
    return e * inv, es * inv


def _swa_specs(s_len):
    blk = SWA_BLOCK
    cur = lambda n: (n, 0)
    prev = lambda n: (jnp.maximum(n - 1, 0), 0)
    kvw = SWA_KV_HEADS * HEAD_DIM
    return [pl.BlockSpec(memory_space=pltpu.SMEM), pl.BlockSpec(memory_space=pltpu.SMEM),
            pl.BlockSpec((blk, 2 * blk), lambda n: (0, 0)),
            pl.BlockSpec((blk, N_HEADS * HEAD_DIM), cur),
            pl.BlockSpec((blk, kvw), prev), pl.BlockSpec((blk, kvw), cur),
            pl.BlockSpec((blk, kvw), prev), pl.BlockSpec((blk, kvw), cur)]


def _swa_fwd(tab, sinks, bkt, q, k, v):
    s_len = q.shape[0]
    blk = SWA_BLOCK

    def body(tab_ref, sink_ref, bkt_ref, q_ref, kp_ref, kc_ref, vp_ref, vc_ref, o_ref, bias_ref):
        n = pl.program_id(0)

        @pl.when(n == 0)
        def _():
            _swa_bias_into(bias_ref, bkt_ref, tab_ref)

        valid = _swa_valid(n)
        kk, vv = _swa_windows(kp_ref, kc_ref, vp_ref, vc_ref)
        st = {}

        def s_logits(h):
            tile, mine, kv = _swa_place(h)
            st[h, "lg"] = _dot_nt(_move_half(q_ref[:, tile].astype(F32), mine, kv).astype(BF16), kk)

        def s_probs(h):
            st[h, "p"] = _swa_probs(st.pop((h, "lg")), bias_ref[h], sink_ref[0, h], valid)[0].astype(BF16)

        def s_values(h):
            tile, mine, kv = _swa_place(h)
            part = _move_half(_dot(st.pop((h, "p")), vv), kv, mine)
            if mine == 0:
                st[h + 1, "o"] = part
            else:
                o_ref[:, tile] = (st.pop((h, "o")) + part).astype(BF16)

        _emit_skewed((list(range(N_HEADS)), [s_logits, s_probs, s_values]))

    return pl.pallas_call(
        body, name="swa_fwd",
        grid=(s_len // blk,),
        in_specs=_swa_specs(s_len),
        out_specs=pl.BlockSpec((blk, N_HEADS * HEAD_DIM), lambda n: (n, 0)),
        out_shape=jax.ShapeDtypeStruct((s_len, N_HEADS * HEAD_DIM), BF16),
        scratch_shapes=[pltpu.VMEM((N_HEADS, blk, 2 * blk), F32)],
        compiler_params=_cparams("arbitrary"),
    )(tab, sinks, bkt, q, k, k, v, v)


def _swa_bwd(tab, sinks, bkt, q, k, v, do, comm=None):
    s_len = q.shape[0]
    blk = SWA_BLOCK
    nb = s_len // blk
    kvw = SWA_KV_HEADS * HEAD_DIM

    def body(tab_ref, sink_ref, bkt_ref, q_ref, kp_ref, kc_ref, vp_ref, vc_ref, do_ref,
             dq_ref, dk_ref, dv_ref, dtab_ref, dsink_ref, bias_ref, dbias_ref):
        n = pl.program_id(0)

        @pl.when(n == 0)
        def _():
            _swa_bias_into(bias_ref, bkt_ref, tab_ref)
            dbias_ref[...] = jnp.zeros_like(dbias_ref)
            dk_ref[...] = jnp.zeros_like(dk_ref)
            dv_ref[...] = jnp.zeros_like(dv_ref)
            dsink_ref[...] = jnp.zeros_like(dsink_ref)
            dtab_ref[...] = jnp.zeros_like(dtab_ref)

        valid = _swa_valid(n)
        cur_rows = pl.ds(pl.multiple_of(n * blk, blk), blk)
        prev_rows = pl.ds(pl.multiple_of(jnp.maximum(n - 1, 0) * blk, blk), blk)
        kk, vv = _swa_windows(kp_ref, kc_ref, vp_ref, vc_ref)
        st = {}

        def s_logits(h):
            tile, mine, kv = _swa_place(h)
            st[h, "q"] = _move_half(q_ref[:, tile].astype(F32), mine, kv).astype(BF16)
            st[h, "do"] = _move_half(do_ref[:, tile].astype(F32), mine, kv).astype(BF16)
            st[h, "lg"] = _dot_nt(st[h, "q"], kk)
            st[h, "dp"] = _dot_nt(st[h, "do"], vv)

        def s_probs(h):
            p, ps = _swa_probs(st.pop((h, "lg")), bias_ref[h], sink_ref[0, h], valid)
            dp = st.pop((h, "dp"))
            delta = jnp.sum(p * dp, axis=-1, keepdims=True)
            dl = p * (dp - delta)
            dsink_ref[h:h + 1, :] += jnp.broadcast_to(-jnp.sum(ps * delta, axis=0, keepdims=True), (1, LANES))
            dbias_ref[h] += dl
            st[h, "dl"], st[h, "p"] = dl.astype(BF16), p.astype(BF16)

        def s_products(h):
            tile, mine, kv = _swa_place(h)
            dlb = st.pop((h, "dl"))
            part = _move_half(Q_SCALE * _dot(dlb, kk), kv, mine)
            if mine == 0:
                st[h + 1, "dq"] = part
            else:
                dq_ref[:, tile] = (st.pop((h, "dq")) + part).astype(BF16)
            dk_win = Q_SCALE * _dot_tn(dlb, st.pop((h, "q")))
            dv_win = _dot_tn(st.pop((h, "p")), st.pop((h, "do")))
            dk_ref[prev_rows, :] += dk_win[:blk]
            dv_ref[prev_rows, :] += dv_win[:blk]
            dk_ref[cur_rows, :] += dk_win[blk:]
            dv_ref[cur_rows, :] += dv_win[blk:]

        _emit_skewed((list(range(N_HEADS)), [s_logits, s_probs, s_products]))

        @pl.when(n == nb - 1)
        def _():
            bk = bkt_ref[...]
            lane = lax.broadcasted_iota(jnp.int32, (1, LANES), 1)
            for bucket in range(REL_BUCKETS):
                rowv = jnp.zeros((1, LANES), F32)
                for h in range(N_HEADS):
                    val = jnp.sum(jnp.where(bk == bucket, dbias_ref[h], 0.0), axis=1, keepdims=True)
                    val = jnp.sum(val, axis=0, keepdims=True)
                    rowv = jnp.where(lane == h, val, rowv)
                dtab_ref[bucket:bucket + 1, :] = rowv

    return _call(
        body, (tab, sinks, bkt, q, k, k, v, v, do), comm=comm, **_grid_ends(nb), name="swa_bwd",
        grid=(nb,),
        in_specs=_swa_specs(s_len) + [pl.BlockSpec((blk, N_HEADS * HEAD_DIM), lambda n: (n, 0))],
        out_specs=[pl.BlockSpec((blk, N_HEADS * HEAD_DIM), lambda n: (n, 0)),
                   pl.BlockSpec((s_len, kvw), lambda n: (0, 0)), pl.BlockSpec((s_len, kvw), lambda n: (0, 0)),
                   pl.BlockSpec((REL_BUCKETS, LANES), lambda n: (0, 0)), pl.BlockSpec((N_HEADS, LANES), lambda n: (0, 0))],
        out_shape=[jax.ShapeDtypeStruct((s_len, N_HEADS * HEAD_DIM), BF16),
                   jax.ShapeDtypeStruct((s_len, kvw), F32), jax.ShapeDtypeStruct((s_len, kvw), F32),
                   jax.ShapeDtypeStruct((REL_BUCKETS, LANES), F32), jax.ShapeDtypeStruct((N_HEADS, LANES), F32)],
        scratch_shapes=[pltpu.VMEM((N_HEADS, blk, 2 * blk), F32), pltpu.VMEM((N_HEADS, blk, 2 * blk), F32)],
        compiler_params=_cparams("arbitrary"),
    )


def _sb_terms(z, valid):
    zc = jnp.minimum(z, SB_LOGIT_CAP)
    lk = -jnp.log(1.0 + jnp.exp(zc))
    lsz = zc + lk
    return lsz, (lk if valid is None else jnp.where(valid, lk, 0.0))


def _bf16_parts(vals):
    parts, rest = [], vals
    for n in range(SB_SUM_PARTS):
        parts.append(rest.astype(BF16))
        if n + 1 < SB_SUM_PARTS:
            rest = rest - parts[-1].astype(F32)
    return parts[0] if len(parts) == 1 else jnp.concatenate(parts, axis=1)


def _row_sum_lanes(vals):
    return jnp.broadcast_to(jnp.sum(vals, axis=-1, keepdims=True), (vals.shape[0], LANES))


def _emit_skewed(*groups):
    for step in range(max(len(items) + len(stages) - 1 for items, stages in groups)):
        for items, stages in groups:
            for s, stage in enumerate(stages):
                if 0 <= step - s < len(items) and items[step - s] is not None:
                    stage(items[step - s])


def _sb_items(edge):
    items = []
    for h in range(2):
        for r0 in range(0, SB_QUERIES, SB_ROWS):
            if edge is None or r0 >= (edge + 1) * SB_KEYS:
                items.append((h, r0, False))
            else:
                items.append((h, r0, True) if r0 + SB_ROWS - 1 > edge * SB_KEYS else None)
    return items


def _sb_valid(w, edge):
    row = lax.broadcasted_iota(jnp.int32, (SB_ROWS, SB_KEYS), 0) + w[1]
    col = lax.broadcasted_iota(jnp.int32, (SB_ROWS, SB_KEYS), 1) + edge * SB_KEYS
    return col < row


def _sb_consts(tq, tk):
    low = lax.broadcasted_iota(jnp.int32, (tq, LANES), 1) < HEAD_DIM
    row = lax.broadcasted_iota(jnp.int32, (tk, tk), 0)
    col = lax.broadcasted_iota(jnp.int32, (tk, tk), 1)
    right = (row > col).astype(BF16)
    left = (row < col).astype(BF16)
    return low, jnp.concatenate([right] * SB_SUM_PARTS, axis=0), jnp.concatenate([left] * SB_SUM_PARTS, axis=0)


def _sb_fwd(q, k, v, comm=None):
    s_len = q.shape[0]
    tq, tk, tr = SB_QUERIES, SB_KEYS, SB_ROWS
    nk, ratio = s_len // tk, tq // tk
    assert nk <= LANES

    def body(q_ref, k_ref, v_ref, o_ref, car_ref, c_ref, oacc_ref, logw_ref, lksum_ref):
        i = pl.program_id(1)
        qv = q_ref[...]
        low, tri2, _ = _sb_consts(tq, tk)
        lane = lax.broadcasted_iota(jnp.int32, (tr, LANES), 1)
        zero = jnp.zeros_like(qv)
        q_heads = (jnp.where(low, qv, zero), jnp.where(low, zero, qv))
        c_ref[...] = jnp.zeros_like(c_ref)
        oacc_ref[...] = jnp.zeros_like(oacc_ref)
        car_ref[...] = jnp.full_like(car_ref, NEG_BIG)

        def front(j, edge):
            keys = k_ref[pl.ds(pl.multiple_of(j * tk, tk), tk), :]
            slot = j % SB_SLOTS
            st = {}

            def s_logits(w):
                st[w, "z"] = _dot_nt(q_heads[w[0]][w[1]:w[1] + tr], keys)

            def s_terms(w):
                valid = _sb_valid(w, edge) if w[2] else None
                lsz, lk = _sb_terms(st.pop((w, "z")), valid)
                st[w, "parts"] = _bf16_parts(lk)
                st[w, "lsz"] = lsz if valid is None else jnp.where(valid, lsz, NEG_BIG)
                lksum_ref[slot, w[0], w[1]:w[1] + tr, :] = _row_sum_lanes(lk)

            def s_suffix(w):
                logw_ref[slot, w[0], w[1]:w[1] + tr, :] = st.pop((w, "lsz")) + _dot(st.pop((w, "parts")), tri2)

            return _sb_items(edge), [s_logits, s_terms, s_suffix]

        def back(j, edge):
            vv = v_ref[pl.ds(pl.multiple_of(j * tk, tk), tk), :]
            slot = j % SB_SLOTS
            st = {}

            def s_weights(w):
                h, rs = w[0], slice(w[1], w[1] + tr)
                c = c_ref[h, rs, :]
                st[w, "a"] = jnp.exp(logw_ref[slot, h, rs, :] + jnp.tile(c, (1, tk // LANES))).astype(BF16)
                car_ref[h, rs, :] = jnp.where(lane == j, c, car_ref[h, rs, :])
                c_ref[h, rs, :] = c + lksum_ref[slot, h, rs, :]

            def s_values(w):
                oacc_ref[w[0], w[1]:w[1] + tr, :] += _dot(st.pop((w, "a")), vv)

            return _sb_items(edge), [s_weights, s_values]

        first = i * ratio
        edge_tiles = [(first + m, m) for m in reversed(range(ratio))]

        def alive():
            return (jnp.max(c_ref[...]) >= SB_DEAD_CARRY).astype(jnp.int32)

        @pl.when(i == 0)
        def _():
            _emit_skewed(*[front(j, m) for j, m in edge_tiles])
            _emit_skewed(*[back(j, m) for j, m in edge_tiles])

        @pl.when(i > 0)
        def _():
            tiles = edge_tiles + [(first - 1, None)]
            _emit_skewed(*[front(j, m) for j, m in tiles])
            _emit_skewed(*[back(j, m) for j, m in tiles])

            @pl.when((alive() > 0) & (first >= 2))
            def _():
                _emit_skewed(front(first - 2, None))

                def step(state):
                    pending, _ = state
                    _emit_skewed(front(pending - 1, None), back(pending, None))
                    return pending - 1, alive()

                pending, live = lax.while_loop(lambda s: (s[0] > 0) & (s[1] > 0), step, (first - 2, jnp.int32(1)))

                @pl.when(live > 0)
                def _():
                    _emit_skewed(back(pending, None))

        o_ref[...] = jnp.where(low, oacc_ref[0], oacc_ref[1]).astype(BF16)

    return _call(
        body, (q, k, v), comm=comm, **_grid_ends(N_HEADS // 2, s_len // tq), name="sb_fwd",
        grid=(N_HEADS // 2, s_len // tq),
        in_specs=[pl.BlockSpec((tq, LANES), lambda p, i: (i, p)),
                  pl.BlockSpec((s_len, LANES), lambda p, i: (0, p)),
                  pl.BlockSpec((s_len, LANES), lambda p, i: (0, p))],
        out_specs=[pl.BlockSpec((tq, LANES), lambda p, i: (i, p)), pl.BlockSpec((2, tq, LANES), lambda p, i: (p, i, 0))],
        out_shape=[jax.ShapeDtypeStruct((s_len, N_HEADS * HEAD_DIM), BF16),
                   jax.ShapeDtypeStruct((N_HEADS, s_len, LANES), F32)],
        scratch_shapes=[pltpu.VMEM((2, tq, LANES), F32), pltpu.VMEM((2, tq, LANES), F32),
                        pltpu.VMEM((SB_SLOTS, 2, tq, tk), F32), pltpu.VMEM((SB_SLOTS, 2, tq, LANES), F32)],
        compiler_params=_cparams("arbitrary", "arbitrary"),
    )


def _sb_bwd(q, k, v, do, cars):
    s_len = q.shape[0]
    tq, tk, tr = SB_QUERIES, SB_KEYS, SB_ROWS
    nk, ratio = s_len // tk, tq // tk

    def body(q_ref, k_ref, v_ref, do_ref, car_ref, dq_ref, dk_ref, dv_ref,
             gleft_ref, dqacc_ref, dkacc_ref, dvacc_ref, logw_ref, lsz_ref, da_ref, a_ref, dz_ref):
        i = pl.program_id(1)

        @pl.when(i == 0)
        def _():
            dkacc_ref[...] = jnp.zeros_like(dkacc_ref)
            dvacc_ref[...] = jnp.zeros_like(dvacc_ref)

        qv = q_ref[...]
        dov = do_ref[...]
        low, tri_right2, tri_left2 = _sb_consts(tq, tk)
        lane = lax.broadcasted_iota(jnp.int32, (tr, LANES), 1)
        zero = jnp.zeros_like(qv)
        q_heads = (jnp.where(low, qv, zero), jnp.where(low, zero, qv))
        do_heads = (jnp.where(low, dov, zero), jnp.where(low, zero, dov))
        q_t = qv.astype(F32).T.astype(BF16)
        do_t = dov.astype(F32).T.astype(BF16)
        gleft_ref[...] = jnp.zeros_like(gleft_ref)
        dqacc_ref[...] = jnp.zeros_like(dqacc_ref)

        def front(j, edge):
            key_rows = pl.ds(pl.multiple_of(j * tk, tk), tk)
            keys, values = k_ref[key_rows, :], v_ref[key_rows, :]
            slot = j % SB_SLOTS
            st = {}

            def s_logits(w):
                h, rs = w[0], slice(w[1], w[1] + tr)
                st[w, "z"] = _dot_nt(q_heads[h][rs], keys)
                da_ref[slot, h, rs, :] = _dot_nt(do_heads[h][rs], values)

            def s_terms(w):
                h, rs = w[0], slice(w[1], w[1] + tr)
                valid = _sb_valid(w, edge) if w[2] else None
                lsz, lk = _sb_terms(st.pop((w, "z")), valid)
                st[w, "parts"] = _bf16_parts(lk)
                lsz = lsz if valid is None else jnp.where(valid, lsz, NEG_BIG)
                lsz_ref[slot, h, rs, :] = lsz
                st[w, "lszc"] = lsz + jnp.sum(jnp.where(lane == j, car_ref[h, rs, :], 0.0), axis=-1, keepdims=True)

            def s_suffix(w):
                logw_ref[slot, w[0], w[1]:w[1] + tr, :] = st.pop((w, "lszc")) + _dot(st.pop((w, "parts")), tri_right2)

            return _sb_items(edge), [s_logits, s_terms, s_suffix]

        def back(j, edge):
            kv = k_ref[pl.ds(pl.multiple_of(j * tk, tk), tk), :]
            slot = j % SB_SLOTS
            st = {}

            items = _sb_items(edge)
            head_rows = [[w[1] for w in items if w is not None and w[0] == h] for h in range(2)]

            def s_weights(w):
                h, rs = w[0], slice(w[1], w[1] + tr)
                a = jnp.exp(logw_ref[slot, h, rs, :])
                g = a * da_ref[slot, h, rs, :]
                a_ref[slot, h, rs, :] = a.astype(BF16)
                st[w, "g"], st[w, "parts"] = g, _bf16_parts(g)

            def s_prefix(w):
                st[w, "gs"] = _dot(st.pop((w, "parts")), tri_left2)

            def s_dz(w):
                h, rs = w[0], slice(w[1], w[1] + tr)
                g = st.pop((w, "g"))
                gleft = gleft_ref[h, rs, :]
                gsum = st.pop((w, "gs")) + jnp.tile(gleft, (1, tk // LANES))
                dz = (g - jnp.exp(lsz_ref[slot, h, rs, :]) * (g + gsum)).astype(BF16)
                st[w, "dz"] = dz
                dz_ref[slot, h, rs, :] = dz
                gleft_ref[h, rs, :] = gleft + _row_sum_lanes(g)

            def s_products(w):
                h, rs = w[0], slice(w[1], w[1] + tr)
                dqacc_ref[h, rs, :] += _dot(st.pop((w, "dz")), kv)
                if w[1] == head_rows[h][-1]:
                    feat = slice(h * HEAD_DIM, (h + 1) * HEAD_DIM)
                    hr = slice(head_rows[h][0], tq)
                    dkacc_ref[j, feat, :] += _dot(q_t[feat, hr], dz_ref[slot, h, hr, :])
                    dvacc_ref[j, feat, :] += _dot(do_t[feat, hr], a_ref[slot, h, hr, :])

            return items, [s_weights, s_prefix, s_dz, s_products]

        first = i * ratio
        tile_max = jnp.max(jnp.maximum(car_ref[0], car_ref[1]), axis=0, keepdims=True)
        start = jnp.clip(first + ratio - jnp.sum(jnp.where(tile_max >= SB_DEAD_CARRY, 1, 0)), 0, first)

        edge_tiles = [(first + m, m) for m in range(ratio)]

        @pl.when(start == first)
        def _():
            _emit_skewed(*[front(j, m) for j, m in edge_tiles])
            _emit_skewed(*[back(j, m) for j, m in edge_tiles])

        @pl.when(start == first - 1)
        def _():
            tiles = [(first - 1, None)] + edge_tiles
            _emit_skewed(*[front(j, m) for j, m in tiles])
            _emit_skewed(*[back(j, m) for j, m in tiles])

        @pl.when(start < first - 1)
        def _():
            _emit_skewed(front(start, None))

            def step(jj, carry):
                _emit_skewed(front(jj, None), back(jj - 1, None))
                return carry

            lax.fori_loop(start + 1, first, step, 0)
            _emit_skewed(front(first, 0), back(first - 1, None))
            for m in range(1, ratio):
                _emit_skewed(front(first + m, m), back(first + m - 1, m - 1))
            _emit_skewed(back(first + ratio - 1, ratio - 1))

        dq_ref[...] = (Q_SCALE * jnp.where(low, dqacc_ref[0], dqacc_ref[1])).astype(BF16)

        @pl.when(i == s_len // tq - 1)
        def _():
            for j in range(nk):
                dk_ref[j * tk:(j + 1) * tk, :] = dkacc_ref[j].T.astype(BF16)
                dv_ref[j * tk:(j + 1) * tk, :] = dvacc_ref[j].T.astype(BF16)

    qblk = pl.BlockSpec((tq, LANES), lambda p, i: (i, p))
    col_full = pl.BlockSpec((s_len, LANES), lambda p, i: (0, p))
    return pl.pallas_call(
        body, name="sb_bwd",
        grid=(N_HEADS // 2, s_len // tq),
        in_specs=[qblk, col_full, col_full, qblk, pl.BlockSpec((2, tq, LANES), lambda p, i: (p, i, 0))],
        out_specs=[qblk, col_full, col_full],
        out_shape=[jax.ShapeDtypeStruct((s_len, N_HEADS * HEAD_DIM), BF16)] * 3,
        scratch_shapes=[pltpu.VMEM((2, tq, LANES), F32), pltpu.VMEM((2, tq, LANES), F32),
                        pltpu.VMEM((nk, LANES, tk), F32), pltpu.VMEM((nk, LANES, tk), F32)]
        + [pltpu.VMEM((SB_SLOTS, 2, tq, tk), F32)] * 3 + [pltpu.VMEM((SB_SLOTS, 2, tq, tk), BF16)] * 2,
        compiler_params=_cparams("parallel", "arbitrary"),
    )(q, k, v, do, cars)


def _local_step(xs, tgt, gains, sinks, rel_bias, weights_of, ship):
    g1, gmix, g2, gfin = gains
    bkt = _rel_bucket_matrix()
    grads = {}

    def carried(outs, comm, count):
        return outs[:count], (list(outs[count:]) if comm is not None else None)

    wts = dict(weights_of(0, None))
    comm = ship("weights", 1)
    (x1, h1, a1, b1, u1), landed = carried(
        _ffn_fwd(xs, g1, wts["ffn1_w1t"], wts["ffn1_w3t"], wts["ffn1_w2"], "1", comm), comm, 5)
    wts.update(weights_of(1, landed))
    hm, qa, ka, va, qb, kb, vb, ga, gb = _proj_fwd(x1, gmix, wts["w_int"])
    oa = _swa_fwd(rel_bias, sinks, bkt, qa, ka, va)
    comm = ship("weights", 2)
    (ob, cars), landed = carried(_sb_fwd(qb, kb, vb, comm), comm, 2)
    wts.update(weights_of(2, landed))
    x2, mg = _merge_fwd(x1, oa, ob, ga, gb, wts["w_swa"], wts["w_sb"], wts["w_out"])
    x3, h3, a3, b3, u3 = _ffn_fwd(x2, g2, wts["ffn2_w1t"], wts["ffn2_w3t"], wts["ffn2_w2"], "2")
    dx3, loss, dgfin = _loss_fwd_bwd(x3, tgt, gfin)

    def grad_chain(items):
        prev = None
        for name, lhs, rhs in items:
            comm = None if prev is None else ship("grads", (prev[0],), prev[1])
            res = _matmul_tn_tiled(lhs, rhs, name, comm)
            if prev is not None:
                grads[(prev[0],)] = prev[1] if comm is None else res[1]
            prev = (name, {_GRAD_KEY[name]: res if comm is None else res[0]})
        return prev

    dx2, dg2, da3, db3, dx3b = _ffn_bwd(dx3, x2, g2, a3, b3, wts["ffn2_w1t"], wts["ffn2_w3t"], wts["ffn2_w2"], "2")
    last = grad_chain((("ffn2_w1", da3, h3), ("ffn2_w3", db3, h3), ("ffn2_w2", u3, dx3b)))
    comm = ship("grads", (last[0],), last[1])
    (doa, dob, dga, dgb, dpa, dpb, dx2b), landed = carried(
        _merge_bwd(dx2, oa, ob, ga, gb, wts["w_swa"], wts["w_sb"], wts["w_out"], comm), comm, 7)
    grads[(last[0],)] = last[1] if comm is None else landed[0]
    dqa, dka, dva, dtab, dsink = _swa_bwd(rel_bias, sinks, bkt, qa, ka, va, doa)

    big = {"w_out": _matmul_tn(mg, dx2b, "w_out"), "w_swa": _matmul_tn(oa, dpa, "w_swa"),
           "w_sb": _matmul_tn(ob, dpb, "w_sb")}
    dqb, dkb, dvb = _sb_bwd(qb, kb, vb, dob, cars)
    dpieces = (dqa, dka.astype(BF16), dva.astype(BF16), dqb, dkb, dvb, dga, dgb)
    big["w_int"] = _matmul_tn_stacked(dpieces, hm, "w_in")
    dx1, dgmix = _proj_bwd(dpieces, dx2, x1, gmix, wts["w_int"])

    comm = ship("grads", GROUPS[1], big)
    (dx0, dg1, da1, db1, dx1b), landed = carried(
        _ffn_bwd(dx1, xs, g1, a1, b1, wts["ffn1_w1t"], wts["ffn1_w3t"], wts["ffn1_w2"], "1", comm), comm, 5)
    for i, name in enumerate(GROUPS[1]):
        grads[(name,)] = {_GRAD_KEY[name]: big[_GRAD_KEY[name]]} if comm is None else landed[i]

    last = grad_chain((("ffn1_w1", da1, h1), ("ffn1_w3", db1, h1), ("ffn1_w2", u1, dx1b)))
    grads[(last[0],)] = last[1]

    small = {"gains": (dg1, dgmix, dg2, dgfin), "sinks": dsink[:, 0], "rel_bias": dtab[:, :N_HEADS]}
    return loss, dx0, small, grads


def _my_place():
    return lax.axis_index("x"), lax.axis_index("y"), lax.axis_index("c")


def _flip(v, bit):
    return 1 - v if bit else v


_RELATIONS = tuple((k >> 2 & 1, k >> 1 & 1, k & 1) for k in range(1, N_DEV))


def _gather_weights(blocks, tag):
    count = len(blocks)

    def body(*refs):
        x_refs, out_refs = refs[:count], refs[count:2 * count]
        send_sems, recv_sems, local_sems = refs[2 * count:]
        x, y, c = _my_place()
        me, sibling = (x, y, c), (x, y, 1 - c)
        chips = [(1 - x, y), (x, 1 - y), (1 - x, 1 - y)]

        def rows(s, px, py, pc):
            return out_refs[s].at[4 * px + 2 * py + pc]

        def copy(s, k, block, to, src=None):
            return pltpu.make_async_remote_copy(
                src_ref=rows(s, *block) if src is None else src, dst_ref=rows(s, *block),
                send_sem=send_sems.at[s, k], recv_sem=recv_sems.at[s, k],
                device_id=to, device_id_type=pl.DeviceIdType.MESH)

        mine = [pltpu.make_async_copy(x_refs[s], rows(s, *me), local_sems.at[s]) for s in range(count)]
        first, passed = [], []
        for s in range(count):
            mine[s].start()
            first.append(copy(s, 0, me, sibling, src=x_refs[s]))
            first += [copy(s, 1 + j, me, (*chip, c), src=x_refs[s]) for j, chip in enumerate(chips)]
        for cp in first:
            cp.start()
        for s in range(count):
            for j, chip in enumerate(chips):
                copy(s, 1 + j, (*chip, c), me).wait_recv()
                passed.append(copy(s, 4 + j, (*chip, c), sibling))
                passed[-1].start()
        for s in range(count):
            copy(s, 0, sibling, me).wait_recv()
            for j, chip in enumerate(chips):
                copy(s, 4 + j, (*chip, 1 - c), me).wait_recv()
        for cp in first + passed:
            cp.wait_send()
        for cp in mine:
            cp.wait()

    anywhere = pl.BlockSpec(memory_space=pl.ANY)
    return pl.pallas_call(
        body, name=f"gather_weights_{tag}",
        out_shape=[jax.ShapeDtypeStruct((N_DEV,) + b.shape, b.dtype) for b in blocks],
        in_specs=[anywhere] * count, out_specs=[anywhere] * count,
        scratch_shapes=[pltpu.SemaphoreType.DMA((count, N_DEV - 1)), pltpu.SemaphoreType.DMA((count, N_DEV - 1)),
                        pltpu.SemaphoreType.DMA((count,))],
    )(*blocks)


def _exchange_grads(gp, tag):
    def body(g_ref, out_ref, send_sems, recv_sems, local_sem):
        x, y, c = _my_place()
        me = 4 * x + 2 * y + c
        mine = pltpu.make_async_copy(g_ref.at[me], out_ref.at[me], local_sem)
        mine.start()
        copies = []
        for k, (fx, fy, fc) in enumerate(_RELATIONS):
            px, py, pc = _flip(x, fx), _flip(y, fy), _flip(c, fc)
            peer = 4 * px + 2 * py + pc
            copies.append((
                pltpu.make_async_remote_copy(
                    src_ref=g_ref.at[peer], dst_ref=out_ref.at[me], send_sem=send_sems.at[k], recv_sem=recv_sems.at[k],
                    device_id=(px, py, pc), device_id_type=pl.DeviceIdType.MESH),
                pltpu.make_async_remote_copy(
                    src_ref=g_ref.at[peer], dst_ref=out_ref.at[peer], send_sem=send_sems.at[k], recv_sem=recv_sems.at[k],
                    device_id=(px, py, pc), device_id_type=pl.DeviceIdType.MESH)))
        for out_cp, _ in copies:
            out_cp.start()
        for _, in_cp in copies:
            in_cp.wait_recv()
        for out_cp, _ in copies:
            out_cp.wait_send()
        mine.wait()

    return pl.pallas_call(
        body, name=f"exchange_grads_{tag}",
        out_shape=jax.ShapeDtypeStruct(gp.shape, gp.dtype),
        in_specs=[pl.BlockSpec(memory_space=pl.ANY)],
        out_specs=pl.BlockSpec(memory_space=pl.ANY),
        scratch_shapes=[pltpu.SemaphoreType.DMA((7,)), pltpu.SemaphoreType.DMA((7,)), pltpu.SemaphoreType.DMA(())],
    )(gp)


def _peers():
    x, y, c = _my_place()
    out = []
    for k, (fx, fy, fc) in enumerate(_RELATIONS):
        px, py, pc = _flip(x, fx), _flip(y, fy), _flip(c, fc)
        out.append((k, (px, py, pc), 4 * px + 2 * py + pc))
    return out, 4 * x + 2 * y + c


def _grid_ends(*grid):
    def first():
        return functools.reduce(lambda a, b: a & b, [pl.program_id(d) == 0 for d in range(len(grid))])

    def last():
        return functools.reduce(lambda a, b: a & b, [pl.program_id(d) == n - 1 for d, n in enumerate(grid)])

    return {"first": first, "last": last}


def _call(body, operands, *, comm=None, first=None, last=None, **kw):
    if comm is None:
        return pl.pallas_call(body, **kw)(*operands)
    in_specs, out_specs, out_shape = list(kw.pop("in_specs")), list(kw.pop("out_specs")), list(kw.pop("out_shape"))
    scratch = list(kw.pop("scratch_shapes", ()))
    n_in, n_out, n_scr, n_src = len(in_specs), len(out_specs), len(scratch), len(comm)

    def wrapped(*refs):
        ins, src_refs = refs[:n_in], refs[n_in:n_in + n_src]
        outs = refs[n_in + n_src:n_in + n_src + n_out]
        land_refs = refs[n_in + n_src + n_out:n_in + 2 * n_src + n_out]
        scr = refs[n_in + 2 * n_src + n_out:n_in + 2 * n_src + n_out + n_scr]
        send_sems, recv_sems, local_sems = refs[n_in + 2 * n_src + n_out + n_scr:]
        peers, me = _peers()
        mine, going, coming = [], [], []
        for s, (_, per_peer) in enumerate(comm):
            src_ref, land_ref = src_refs[s], land_refs[s]
            mine.append(pltpu.make_async_copy(src_ref.at[me] if per_peer else src_ref, land_ref.at[me], local_sems.at[s]))
            for k, where, slab in peers:
                piece = src_ref.at[slab] if per_peer else src_ref
                going.append(pltpu.make_async_remote_copy(
                    src_ref=piece, dst_ref=land_ref.at[me], send_sem=send_sems.at[s, k], recv_sem=recv_sems.at[s, k],
                    device_id=where, device_id_type=pl.DeviceIdType.MESH))
                coming.append(pltpu.make_async_remote_copy(
                    src_ref=piece, dst_ref=land_ref.at[slab], send_sem=send_sems.at[s, k], recv_sem=recv_sems.at[s, k],
                    device_id=where, device_id_type=pl.DeviceIdType.MESH))

        @pl.when(first())
        def _():
            for cp in mine + going:
                cp.start()

        body(*ins, *outs, *scr)

        @pl.when(last())
        def _():
            for cp in coming:
                cp.wait_recv()
            for cp in going:
                cp.wait_send()
            for cp in mine:
                cp.wait()

    anywhere = pl.BlockSpec(memory_space=pl.ANY)
    lands = [jax.ShapeDtypeStruct(src.shape if per_peer else (N_DEV,) + src.shape, src.dtype) for src, per_peer in comm]
    return pl.pallas_call(
        wrapped, in_specs=in_specs + [anywhere] * n_src, out_specs=out_specs + [anywhere] * n_src,
        out_shape=out_shape + lands,
        scratch_shapes=scratch + [pltpu.SemaphoreType.DMA((n_src, N_DEV - 1)), pltpu.SemaphoreType.DMA((n_src, N_DEV - 1)),
                                  pltpu.SemaphoreType.DMA((n_src,))],
        **kw)(*operands, *[src for src, _ in comm])


def _adamw(w, g, m, v):
    m = ADAM_B1 * m + (1.0 - ADAM_B1) * g
    v = ADAM_B2 * v + (1.0 - ADAM_B2) * jnp.square(g)
    m_hat = m / (1.0 - ADAM_B1 ** ADAM_STEP)
    v_hat = v / (1.0 - ADAM_B2 ** ADAM_STEP)
    delta = -ADAM_LR * (m_hat / (jnp.sqrt(v_hat) + ADAM_EPS) + ADAM_WD * w)
    return delta, m, v


def _sum_and_adamw(parts, w, m, v, tr, tag):
    rows = w.shape[0]
    assert rows % tr == 0

    def body(p_ref, w_ref, m_ref, v_ref, g_out, d_out, m_out, v_out):
        g = p_ref[0].astype(F32)
        for d in range(1, N_DEV):
            g = g + p_ref[d].astype(F32)
        delta, mn, vn = _adamw(w_ref[...], g, m_ref[...], v_ref[...])
        g_out[...] = g
        d_out[...] = delta
        m_out[...] = mn
        v_out[...] = vn

    sp = pl.BlockSpec((tr, D_MODEL), lambda i: (i, 0))
    return pl.pallas_call(
        body, name=f"sum_and_adamw_{tag}",
        grid=(rows // tr,),
        in_specs=[pl.BlockSpec((N_DEV, tr, D_MODEL), lambda i: (0, i, 0)), sp, sp, sp],
        out_specs=[sp] * 4,
        out_shape=[jax.ShapeDtypeStruct(w.shape, F32)] * 4,
        compiler_params=_cparams("parallel"),
    )(parts, w, m, v)


def _small_allreduce_adamw(part, w, m, v):
    def body(p_ref, w_ref, m_ref, v_ref, g_out, d_out, m_out, v_out, buf, send_sems, recv_sems):
        x, y, c = _my_place()
        me = 4 * x + 2 * y + c
        buf[me] = p_ref[...]
        copies = []
        for k, (fx, fy, fc) in enumerate(_RELATIONS):
            px, py, pc = _flip(x, fx), _flip(y, fy), _flip(c, fc)
            peer = 4 * px + 2 * py + pc
            copies.append((
                pltpu.make_async_remote_copy(
                    src_ref=buf.at[me], dst_ref=buf.at[me], send_sem=send_sems.at[k], recv_sem=recv_sems.at[k],
                    device_id=(px, py, pc), device_id_type=pl.DeviceIdType.MESH),
                pltpu.make_async_remote_copy(
                    src_ref=buf.at[me], dst_ref=buf.at[peer], send_sem=send_sems.at[k], recv_sem=recv_sems.at[k],
                    device_id=(px, py, pc), device_id_type=pl.DeviceIdType.MESH)))
        for out_cp, _ in copies:
            out_cp.start()
        for _, in_cp in copies:
            in_cp.wait_recv()
        for out_cp, _ in copies:
            out_cp.wait_send()
        g = buf[0]
        for d in range(1, N_DEV):
            g = g + buf[d]
        delta, mn, vn = _adamw(w_ref[...], g, m_ref[...], v_ref[...])
        g_out[...] = g
        d_out[...] = delta
        m_out[...] = mn
        v_out[...] = vn

    vm = pl.BlockSpec(memory_space=pltpu.VMEM)
    return pl.pallas_call(
        body, name="small_allreduce_adamw",
        in_specs=[vm] * 4, out_specs=[vm] * 4,
        out_shape=[jax.ShapeDtypeStruct(w.shape, F32)] * 4,
        scratch_shapes=[pltpu.VMEM((N_DEV,) + part.shape, F32),
                        pltpu.SemaphoreType.DMA((7,)), pltpu.SemaphoreType.DMA((7,))],
    )(part, w, m, v)


_TRANSPOSED = ("ffn1_w1", "ffn1_w3", "w_in", "ffn2_w1", "ffn2_w3")
_BRANCH = ("w_branch_swa", "w_branch_sb")


def _pack_shards(t, names):
    parts = []
    for name in names:
        a = t[name][0]
        if name in _TRANSPOSED:
            a = a.T
        elif name in _BRANCH:
            a = a.reshape(64, D_MODEL)
        parts.append(a)
    return jnp.concatenate(parts, axis=0)


def _unpack_shards(p, names):
    out, lo = {}, 0
    for name in names:
        a = p[lo:lo + BIG_ROWS[BIG_NAMES.index(name)]]
        lo += a.shape[0]
        if name in _TRANSPOSED:
            a = a.T
        elif name in _BRANCH:
            a = a.reshape(512, 128)
        out[name] = a[None]
    return out


def _full_weights(zones, names):
    out = {}
    for name, a in zip(names, zones):
        if name in _BRANCH:
            a = a.reshape(N_DEV, 512, 128).transpose(1, 0, 2).reshape(512, D_MODEL)
        out[_GRAD_KEY[name]] = a.reshape(-1, D_MODEL)
    return out


_GRAD_KEY = {"ffn1_w1": "ffn1_w1t", "ffn1_w3": "ffn1_w3t", "ffn1_w2": "ffn1_w2", "w_in": "w_int",
             "w_branch_swa": "w_swa", "w_branch_sb": "w_sb", "w_out": "w_out",
             "ffn2_w1": "ffn2_w1t", "ffn2_w3": "ffn2_w3t", "ffn2_w2": "ffn2_w2"}


def _pack_full_grads(big, names):
    parts = []
    for name in names:
        a = big[_GRAD_KEY[name]]
        if name in _BRANCH:
            a = a.reshape(512, N_DEV, 128).transpose(1, 0, 2)
        parts.append(a.reshape(N_DEV, BIG_ROWS[BIG_NAMES.index(name)], D_MODEL).astype(BF16))
    return jnp.concatenate(parts, axis=1)


_SMALL_NAMES = ("norm_ffn1", "norm_mix", "norm_ffn2", "norm_final", "swa_sinks", "rel_bias")


def _pack_small(vals):
    rows = []
    for a in vals:
        a = a.reshape(-1)
        rows.append(jnp.pad(a, (0, D_MODEL - a.shape[0])))
    rows += [jnp.zeros((D_MODEL,), F32)] * (SMALL_ROWS - len(rows))
    return jnp.stack(rows)


def _unpack_small(p):
    return {"norm_ffn1": p[0:1], "norm_mix": p[1:2], "norm_ffn2": p[2:3], "norm_final": p[3],
            "swa_sinks": p[4:5, :N_HEADS], "rel_bias": p[5, :REL_BUCKETS * N_HEADS].reshape(REL_BUCKETS, N_HEADS)}


ALL_NAMES = ("norm_ffn1", "ffn1_w1", "ffn1_w3", "ffn1_w2", "norm_mix", "w_in", "swa_sinks", "rel_bias",
             "w_branch_swa", "w_branch_sb", "w_out", "norm_ffn2", "ffn2_w1", "ffn2_w3", "ffn2_w2", "norm_final")


def kernel(x, norm_ffn1, ffn1_w1, ffn1_w3, ffn1_w2, norm_mix, w_in, swa_sinks, rel_bias, w_branch_swa, w_branch_sb, w_out, norm_ffn2, ffn2_w1, ffn2_w3, ffn2_w2, norm_final, loss_target, m_norm_ffn1, m_ffn1_w1, m_ffn1_w3, m_ffn1_w2, m_norm_mix, m_w_in, m_swa_sinks, m_rel_bias, m_w_branch_swa, m_w_branch_sb, m_w_out, m_norm_ffn2, m_ffn2_w1, m_ffn2_w3, m_ffn2_w2, m_norm_final, v_norm_ffn1, v_ffn1_w1, v_ffn1_w3, v_ffn1_w2, v_norm_mix, v_w_in, v_swa_sinks, v_rel_bias, v_w_branch_swa, v_w_branch_sb, v_w_out, v_norm_ffn2, v_ffn2_w1, v_ffn2_w3, v_ffn2_w2, v_norm_final):
    w = dict(zip(ALL_NAMES, (norm_ffn1, ffn1_w1, ffn1_w3, ffn1_w2, norm_mix, w_in, swa_sinks, rel_bias,
                             w_branch_swa, w_branch_sb, w_out, norm_ffn2, ffn2_w1, ffn2_w3, ffn2_w2, norm_final)))
    m = dict(zip(ALL_NAMES, (m_norm_ffn1, m_ffn1_w1, m_ffn1_w3, m_ffn1_w2, m_norm_mix, m_w_in, m_swa_sinks, m_rel_bias,
                             m_w_branch_swa, m_w_branch_sb, m_w_out, m_norm_ffn2, m_ffn2_w1, m_ffn2_w3, m_ffn2_w2,
                             m_norm_final)))
    v = dict(zip(ALL_NAMES, (v_norm_ffn1, v_ffn1_w1, v_ffn1_w3, v_ffn1_w2, v_norm_mix, v_w_in, v_swa_sinks, v_rel_bias,
                             v_w_branch_swa, v_w_branch_sb, v_w_out, v_norm_ffn2, v_ffn2_w1, v_ffn2_w3, v_ffn2_w2,
                             v_norm_final)))

    def my_blocks(group):
        return [_pack_shards(w, (name,)).astype(BF16) for name in GROUPS[group]]

    gathered0 = _gather_weights(my_blocks(0), "group0")

    def weights_of(group, landed):
        return _full_weights(gathered0 if group == 0 else landed, GROUPS[group])

    def ship(kind, which, grads=None):
        if kind == "weights":
            return [(block, False) for block in my_blocks(which)]
        return [(_pack_full_grads(grads, (name,)), True) for name in which]

    gains = (norm_ffn1, norm_mix, norm_ffn2, norm_final.reshape(1, D_MODEL))
    loss, dx, small, parts = _local_step(x[0], loss_target[0], gains, swa_sinks, rel_bias, weights_of, ship)

    big_outs = [{}, {}, {}, {}]
    for names, tile in zip(SUM_GROUPS, SUM_TILE):
        landed = parts[names]
        if isinstance(landed, dict):
            landed = _exchange_grads(_pack_full_grads(landed, names), names[0])
        res = _sum_and_adamw(landed, _pack_shards(w, names), _pack_shards(m, names), _pack_shards(v, names),
                             tile, names[0])
        for acc, packed in zip(big_outs, res):
            acc.update(_unpack_shards(packed, names))
    g_big, d_big, m_big, v_big = big_outs

    small_part = _pack_small(small["gains"] + (small["sinks"], small["rel_bias"], loss))
    zero = jnp.zeros((1,), F32)
    small_res = _small_allreduce_adamw(
        small_part, _pack_small([w[n] for n in _SMALL_NAMES] + [zero]), _pack_small([m[n] for n in _SMALL_NAMES] + [zero]),
        _pack_small([v[n] for n in _SMALL_NAMES] + [zero]))
    g_sm, d_sm, m_sm, v_sm = (_unpack_small(p) for p in small_res)

    outs = [small_res[0][len(_SMALL_NAMES), 0], dx[None]]
    for big_d, small_d in ((g_big, g_sm), (d_big, d_sm), (m_big, m_sm), (v_big, v_sm)):
        merged = {**big_d, **small_d}
        outs += [merged[n] for n in ALL_NAMES]
    return tuple(outs)
```

```python
import functools

import jax
import jax.numpy as jnp
import numpy as np
from jax import lax
from jax.experimental import pallas as pl
from jax.experimental.pallas import tpu as pltpu

F32 = jnp.float32
BF16 = jnp.bfloat16

D_MODEL = 1024
D_FF = 2816
HEAD_DIM = 64
N_HEADS = 8
SWA_KV_HEADS = 2
SWA_GROUP = 4
SWA_BLOCK = 128
REL_BUCKETS = 32
REL_MAX_DIST = 128
RMS_EPS = 1e-6
NEG_BIG = -1e30
Q_SCALE = HEAD_DIM ** -0.5
LANES = 128

N_DEV = 8

ADAM_LR = 0.001
ADAM_B1 = 0.9
ADAM_B2 = 0.999
ADAM_EPS = 1e-08
ADAM_WD = 0.01
ADAM_STEP = 10

IN_SIZES = (512, 128, 128, 512, 512, 512, 1024, 1024)
IN_OFFS = tuple(int(v) for v in np.cumsum((0,) + IN_SIZES))
IN_W = IN_OFFS[-1]

BIG_NAMES = ("ffn1_w1", "ffn1_w3", "ffn1_w2", "w_in", "w_branch_swa", "w_branch_sb", "w_out",
             "ffn2_w1", "ffn2_w3", "ffn2_w2")
BIG_ROWS = (352, 352, 352, 544, 64, 64, 128, 352, 352, 352)
SMALL_ROWS = 8
GROUPS = (BIG_NAMES[0:3], BIG_NAMES[3:7], BIG_NAMES[7:10])
SUM_GROUPS = tuple((n,) for n in BIG_NAMES)
SUM_TILE = (176, 176, 176, 272, 64, 64, 128, 176, 176, 176)

VMEM_LIMIT = 56 * 1024 * 1024
FFN_PIECES = 2
SB_QUERIES = 512
SB_KEYS = 256
SB_ROWS = 256
SB_SLOTS = 3
SB_SUM_PARTS = 1
SB_LOGIT_CAP = 80.0
SB_DEAD_CARRY = -110.0


def _dot(a, b):
    return jnp.dot(a, b, preferred_element_type=F32)


def _dot_nt(a, b):
    return lax.dot_general(a, b, (((1,), (1,)), ((), ())), preferred_element_type=F32)


def _dot_tn(a, b):
    return lax.dot_general(a, b, (((0,), (0,)), ((), ())), preferred_element_type=F32)


def _cparams(*sem):
    return pltpu.CompilerParams(dimension_semantics=sem, vmem_limit_bytes=VMEM_LIMIT)


def _rms_rstd(xv):
    return lax.rsqrt(jnp.mean(xv * xv, axis=-1, keepdims=True) + RMS_EPS)


def _rms_bwd(dh, xv, r, g):
    xhat = xv * r
    dg = jnp.sum(dh * xhat, axis=0, keepdims=True)
    dxn = dh * g
    dx = r * (dxn - xhat * jnp.mean(dxn * xhat, axis=-1, keepdims=True))
    return dx, dg


def _ff_tile_spec(tm, tf):
    return pl.BlockSpec((1, tm, tf), lambda i, j: (j, i, 0))


def _ffn_fwd(x, g, w1t, w3t, w2, tag, comm=None):
    s_len = x.shape[0]
    tm, tf = min(1024, s_len), 256
    nf = D_FF // tf

    def body(x_ref, g_ref, w1_ref, w3_ref, w2_ref, xo_ref, h_ref, a_ref, b_ref, u_ref, acc_ref, hs_ref):
        j = pl.program_id(1)

        @pl.when(j == 0)
        def _():
            xv = x_ref[...]
            h = (xv * _rms_rstd(xv) * g_ref[...]).astype(BF16)
            hs_ref[...] = h
            h_ref[...] = h
            acc_ref[...] = jnp.zeros_like(acc_ref)

        st = {}

        def s_up(rs):
            h = hs_ref[rs, :]
            st[rs.start, "ab"] = (_dot_nt(h, w1_ref[...]), _dot_nt(h, w3_ref[...]))

        def s_act(rs):
            a, b = st.pop((rs.start, "ab"))
            a_ref[0, rs, :] = a.astype(BF16)
            b_ref[0, rs, :] = b.astype(BF16)
            uh = (0.5 * (a * jax.nn.sigmoid(a) * b)).astype(BF16)
            u_ref[0, rs, :] = uh
            st[rs.start, "u"] = uh

        def s_down(rs):
            acc_ref[rs, :] += _dot(st.pop((rs.start, "u")), w2_ref[...])

        _emit_skewed(([slice(r, r + tm // FFN_PIECES) for r in range(0, tm, tm // FFN_PIECES)], [s_up, s_act, s_down]))

        @pl.when(j == nf - 1)
        def _():
            xo_ref[...] = x_ref[...] + acc_ref[...]

    row = lambda i, j: (i, 0)
    return _call(
        body, (x, g, w1t, w3t, w2), comm=comm, **_grid_ends(s_len // tm, nf), name=f"ffn_fwd_{tag}",
        grid=(s_len // tm, nf),
        in_specs=[pl.BlockSpec((tm, D_MODEL), row), pl.BlockSpec((1, D_MODEL), lambda i, j: (0, 0)),
                  pl.BlockSpec((tf, D_MODEL), lambda i, j: (j, 0)), pl.BlockSpec((tf, D_MODEL), lambda i, j: (j, 0)),
                  pl.BlockSpec((tf, D_MODEL), lambda i, j: (j, 0))],
        out_specs=[pl.BlockSpec((tm, D_MODEL), row), pl.BlockSpec((tm, D_MODEL), row)] + [_ff_tile_spec(tm, tf)] * 3,
        out_shape=[jax.ShapeDtypeStruct((s_len, D_MODEL), F32), jax.ShapeDtypeStruct((s_len, D_MODEL), BF16)]
        + [jax.ShapeDtypeStruct((nf, s_len, tf), BF16)] * 3,
        scratch_shapes=[pltpu.VMEM((tm, D_MODEL), F32), pltpu.VMEM((tm, D_MODEL), BF16)],
        compiler_params=_cparams("arbitrary", "arbitrary"),
    )


def _ffn_bwd(dy, x, g, a, b, w1t, w3t, w2, tag, comm=None):
    s_len = x.shape[0]
    tm, tf = min(1024, s_len), 256
    nf = D_FF // tf

    def body(dy_ref, x_ref, g_ref, a_ref, b_ref, w1_ref, w3_ref, w2_ref,
             dx_ref, dg_ref, da_ref, db_ref, dyb_ref, acc_ref, dys_ref):
        i, j = pl.program_id(0), pl.program_id(1)

        @pl.when(j == 0)
        def _():
            dyb = dy_ref[...].astype(BF16)
            dys_ref[...] = 0.5 * dyb
            dyb_ref[...] = dyb
            acc_ref[...] = jnp.zeros_like(acc_ref)

        @pl.when((i == 0) & (j == 0))
        def _():
            dg_ref[...] = jnp.zeros_like(dg_ref)

        st = {}

        def s_du(rs):
            st[rs.start, "du"] = _dot_nt(dys_ref[rs, :], w2_ref[...])

        def s_act(rs):
            du = st.pop((rs.start, "du"))
            av = a_ref[0, rs, :].astype(F32)
            bv = b_ref[0, rs, :].astype(F32)
            sg = jax.nn.sigmoid(av)
            sil = av * sg
            da = (du * bv * (sg + sil * (1.0 - sg))).astype(BF16)
            db = (du * sil).astype(BF16)
            da_ref[0, rs, :] = da
            db_ref[0, rs, :] = db
            st[rs.start, "dab"] = (da, db)

        def s_dh(rs):
            da, db = st.pop((rs.start, "dab"))
            acc_ref[rs, :] += _dot(da, w1_ref[...]) + _dot(db, w3_ref[...])

        _emit_skewed(([slice(r, r + tm // FFN_PIECES) for r in range(0, tm, tm // FFN_PIECES)], [s_du, s_act, s_dh]))

        @pl.when(j == nf - 1)
        def _():
            xv = x_ref[...]
            dx, dg = _rms_bwd(acc_ref[...], xv, _rms_rstd(xv), g_ref[...])
            dx_ref[...] = dy_ref[...] + dx
            dg_ref[...] += dg

    row = lambda i, j: (i, 0)
    wsp = pl.BlockSpec((tf, D_MODEL), lambda i, j: (j, 0))
    return _call(
        body, (dy, x, g, a, b, w1t, w3t, w2), comm=comm, **_grid_ends(s_len // tm, nf), name=f"ffn_bwd_{tag}",
        grid=(s_len // tm, nf),
        in_specs=[pl.BlockSpec((tm, D_MODEL), row), pl.BlockSpec((tm, D_MODEL), row),
                  pl.BlockSpec((1, D_MODEL), lambda i, j: (0, 0)),
                  _ff_tile_spec(tm, tf), _ff_tile_spec(tm, tf), wsp, wsp, wsp],
        out_specs=[pl.BlockSpec((tm, D_MODEL), row), pl.BlockSpec((1, D_MODEL), lambda i, j: (0, 0)),
                   _ff_tile_spec(tm, tf), _ff_tile_spec(tm, tf), pl.BlockSpec((tm, D_MODEL), row)],
        out_shape=[jax.ShapeDtypeStruct((s_len, D_MODEL), F32), jax.ShapeDtypeStruct((1, D_MODEL), F32),
                   jax.ShapeDtypeStruct((nf, s_len, tf), BF16), jax.ShapeDtypeStruct((nf, s_len, tf), BF16),
                   jax.ShapeDtypeStruct((s_len, D_MODEL), BF16)],
        scratch_shapes=[pltpu.VMEM((tm, D_MODEL), F32), pltpu.VMEM((tm, D_MODEL), BF16)],
        compiler_params=_cparams("arbitrary", "arbitrary"),
    )


def _matmul_tn(lhs, rhs, tag, comm=None):
    s_len, m = lhs.shape
    n = rhs.shape[1]
    tm = min(512, s_len)
    tj = m if m <= 1024 else 1408
    assert m % tj == 0
    last_rows = s_len // tm - 1

    def body(l_ref, r_ref, o_ref, acc_ref):
        i = pl.program_id(1)

        @pl.when(i == 0)
        def _():
            acc_ref[...] = jnp.zeros_like(acc_ref)

        acc_ref[...] += _dot_tn(l_ref[...], r_ref[...])

        @pl.when(i == last_rows)
        def _():
            o_ref[...] = acc_ref[...].astype(BF16)

    res = _call(
        body, (lhs, rhs), comm=comm, **_grid_ends(m // tj, s_len // tm), name=f"matmul_tn_{tag}",
        grid=(m // tj, s_len // tm),
        in_specs=[pl.BlockSpec((tm, tj), lambda j, i: (i, j)), pl.BlockSpec((tm, n), lambda j, i: (i, 0))],
        out_specs=[pl.BlockSpec((tj, n), lambda j, i: (j, 0))],
        out_shape=[jax.ShapeDtypeStruct((m, n), BF16)],
        scratch_shapes=[pltpu.VMEM((tj, n), F32)],
        compiler_params=_cparams("arbitrary", "arbitrary"),
    )
    return res[0] if comm is None else tuple(res)


def _matmul_tn_tiled(lhs, rhs, tag, comm=None):
    nf, s_len, tf = lhs.shape
    n = rhs.shape[1]
    tm = min(512, s_len)
    last_rows = s_len // tm - 1

    def body(l_ref, r_ref, o_ref, acc_ref):
        i = pl.program_id(0)

        @pl.when(i == 0)
        def _():
            acc_ref[...] = jnp.zeros_like(acc_ref)

        rv = r_ref[...]
        for t in range(nf):
            acc_ref[t * tf:(t + 1) * tf, :] += _dot_tn(l_ref[t], rv)

        @pl.when(i == last_rows)
        def _():
            o_ref[...] = acc_ref[...].astype(BF16)

    res = _call(
        body, (lhs, rhs), comm=comm, **_grid_ends(s_len // tm), name=f"matmul_tn_{tag}",
        grid=(s_len // tm,),
        in_specs=[pl.BlockSpec((nf, tm, tf), lambda i: (0, i, 0)), pl.BlockSpec((tm, n), lambda i: (i, 0))],
        out_specs=[pl.BlockSpec((nf * tf, n), lambda i: (0, 0))],
        out_shape=[jax.ShapeDtypeStruct((nf * tf, n), BF16)],
        scratch_shapes=[pltpu.VMEM((nf * tf, n), F32)],
        compiler_params=_cparams("arbitrary"),
    )
    return res[0] if comm is None else tuple(res)


def _matmul_tn_stacked(pieces, rhs, tag):
    s_len, n = rhs.shape
    widths = [p.shape[1] for p in pieces]
    offs = [sum(widths[:k]) for k in range(len(widths) + 1)]
    tm = min(256, s_len)
    last_rows = s_len // tm - 1

    def body(*refs):
        l_refs, r_ref, o_ref, acc_ref = refs[:len(pieces)], refs[-3], refs[-2], refs[-1]
        i = pl.program_id(0)

        @pl.when(i == 0)
        def _():
            acc_ref[...] = jnp.zeros_like(acc_ref)

        rv = r_ref[...]
        for k, l_ref in enumerate(l_refs):
            acc_ref[offs[k]:offs[k + 1], :] += _dot_tn(l_ref[...], rv)

        @pl.when(i == last_rows)
        def _():
            o_ref[...] = acc_ref[...].astype(BF16)

    row = lambda i: (i, 0)
    return pl.pallas_call(
        body, name=f"matmul_tn_{tag}",
        grid=(s_len // tm,),
        in_specs=[pl.BlockSpec((tm, w), row) for w in widths] + [pl.BlockSpec((tm, n), row)],
        out_specs=pl.BlockSpec((offs[-1], n), lambda i: (0, 0)),
        out_shape=jax.ShapeDtypeStruct((offs[-1], n), BF16),
        scratch_shapes=[pltpu.VMEM((offs[-1], n), F32)],
        compiler_params=_cparams("arbitrary"),
    )(*pieces, rhs)


def _proj_fwd(x1, g, wint):
    s_len = x1.shape[0]
    tm = min(512, s_len)
    dts = (BF16, BF16, BF16, BF16, BF16, BF16, F32, F32)

    def body(x_ref, g_ref, w_ref, h_ref, *outs):
        xv = x_ref[...]
        h = (xv * _rms_rstd(xv) * g_ref[...]).astype(BF16)
        h_ref[...] = h
        for p, o_ref in enumerate(outs):
            val = _dot_nt(h, w_ref[IN_OFFS[p]:IN_OFFS[p + 1], :])
            if p == 3:
                val = val * Q_SCALE
            o_ref[...] = val.astype(dts[p])

    row = lambda i: (i, 0)
    return pl.pallas_call(
        body, name="proj_fwd",
        grid=(s_len // tm,),
        in_specs=[pl.BlockSpec((tm, D_MODEL), row), pl.BlockSpec((1, D_MODEL), lambda i: (0, 0)),
                  pl.BlockSpec((IN_W, D_MODEL), lambda i: (0, 0))],
        out_specs=[pl.BlockSpec((tm, D_MODEL), row)] + [pl.BlockSpec((tm, w), row) for w in IN_SIZES],
        out_shape=[jax.ShapeDtypeStruct((s_len, D_MODEL), BF16)]
        + [jax.ShapeDtypeStruct((s_len, w), dt) for w, dt in zip(IN_SIZES, dts)],
        compiler_params=_cparams("parallel"),
    )(x1, g, wint)


def _proj_bwd(dpieces, dx2, x1, g, wint, comm=None):
    s_len = x1.shape[0]
    tm = min(512, s_len)

    def body(*refs):
        dps = refs[:8]
        dx2_ref, x_ref, g_ref, w_ref, dx_ref, dg_ref = refs[8:]

        @pl.when(pl.program_id(0) == 0)
        def _():
            dg_ref[...] = jnp.zeros_like(dg_ref)

        dh = _dot(dps[0][...], w_ref[IN_OFFS[0]:IN_OFFS[1], :])
        for p in range(1, 8):
            dh += _dot(dps[p][...], w_ref[IN_OFFS[p]:IN_OFFS[p + 1], :])
        xv = x_ref[...]
        dx, dg = _rms_bwd(dh, xv, _rms_rstd(xv), g_ref[...])
        dx_ref[...] = dx2_ref[...] + dx
        dg_ref[...] += dg

    row = lambda i: (i, 0)
    return _call(
        body, (*dpieces, dx2, x1, g, wint), comm=comm, **_grid_ends(s_len // tm), name="proj_bwd",
        grid=(s_len // tm,),
        in_specs=[pl.BlockSpec((tm, w), row) for w in IN_SIZES]
        + [pl.BlockSpec((tm, D_MODEL), row), pl.BlockSpec((tm, D_MODEL), row),
           pl.BlockSpec((1, D_MODEL), lambda i: (0, 0)), pl.BlockSpec((IN_W, D_MODEL), lambda i: (0, 0))],
        out_specs=[pl.BlockSpec((tm, D_MODEL), row), pl.BlockSpec((1, D_MODEL), lambda i: (0, 0))],
        out_shape=[jax.ShapeDtypeStruct((s_len, D_MODEL), F32), jax.ShapeDtypeStruct((1, D_MODEL), F32)],
        compiler_params=_cparams("arbitrary"),
    )


def _merge_fwd(x1, oa, ob, ga, gb, wswa, wsb, wout):
    s_len = x1.shape[0]
    tm = min(512, s_len)

    def body(x_ref, oa_ref, ob_ref, ga_ref, gb_ref, wa_ref, wb_ref, wo_ref, xo_ref, mg_ref):
        pa = _dot(oa_ref[...], wa_ref[...])
        pb = _dot(ob_ref[...], wb_ref[...])
        mg = (jax.nn.sigmoid(ga_ref[...]) * pa + jax.nn.sigmoid(gb_ref[...]) * pb).astype(BF16)
        mg_ref[...] = mg
        xo_ref[...] = x_ref[...] + _dot(mg, wo_ref[...])

    row = lambda i: (i, 0)
    full = lambda i: (0, 0)
    return pl.pallas_call(
        body, name="merge_fwd",
        grid=(s_len // tm,),
        in_specs=[pl.BlockSpec((tm, D_MODEL), row), pl.BlockSpec((tm, 512), row), pl.BlockSpec((tm, 512), row),
                  pl.BlockSpec((tm, D_MODEL), row), pl.BlockSpec((tm, D_MODEL), row),
                  pl.BlockSpec((512, D_MODEL), full), pl.BlockSpec((512, D_MODEL), full),
                  pl.BlockSpec((D_MODEL, D_MODEL), full)],
        out_specs=[pl.BlockSpec((tm, D_MODEL), row), pl.BlockSpec((tm, D_MODEL), row)],
        out_shape=[jax.ShapeDtypeStruct((s_len, D_MODEL), F32), jax.ShapeDtypeStruct((s_len, D_MODEL), BF16)],
        compiler_params=_cparams("parallel"),
    )(x1, oa, ob, ga, gb, wswa, wsb, wout)


def _merge_bwd(dx2, oa, ob, ga, gb, wswa, wsb, wout, comm=None):
    s_len = dx2.shape[0]
    tm = min(512, s_len)

    def body(dx_ref, oa_ref, ob_ref, ga_ref, gb_ref, wa_ref, wb_ref, wo_ref,
             doa_ref, dob_ref, dga_ref, dgb_ref, dpa_ref, dpb_ref, dxb_ref):
        dxb = dx_ref[...].astype(BF16)
        dxb_ref[...] = dxb
        dmg = _dot_nt(dxb, wo_ref[...])
        for o_ref, g_ref, w_ref, do_ref, dg_ref, dp_ref in (
                (oa_ref, ga_ref, wa_ref, doa_ref, dga_ref, dpa_ref),
                (ob_ref, gb_ref, wb_ref, dob_ref, dgb_ref, dpb_ref)):
            pv = _dot(o_ref[...], w_ref[...])
            sg = jax.nn.sigmoid(g_ref[...])
            dp = (dmg * sg).astype(BF16)
            dp_ref[...] = dp
            dg_ref[...] = (dmg * pv * sg * (1.0 - sg)).astype(BF16)
            do_ref[...] = _dot_nt(dp, w_ref[...]).astype(BF16)

    row = lambda i: (i, 0)
    full = lambda i: (0, 0)
    wide = pl.BlockSpec((tm, D_MODEL), row)
    half = pl.BlockSpec((tm, 512), row)
    return _call(
        body, (dx2, oa, ob, ga, gb, wswa, wsb, wout), comm=comm, **_grid_ends(s_len // tm), name="merge_bwd",
        grid=(s_len // tm,),
        in_specs=[wide, half, half, wide, wide, pl.BlockSpec((512, D_MODEL), full),
                  pl.BlockSpec((512, D_MODEL), full), pl.BlockSpec((D_MODEL, D_MODEL), full)],
        out_specs=[half, half, wide, wide, wide, wide, wide],
        out_shape=[jax.ShapeDtypeStruct((s_len, 512), BF16)] * 2 + [jax.ShapeDtypeStruct((s_len, D_MODEL), BF16)] * 5,
        compiler_params=_cparams("arbitrary"),
    )


def _loss_fwd_bwd(x3, tgt, g):
    s_len = x3.shape[0]
    tm = min(1024, s_len)

    def body(x_ref, t_ref, g_ref, dx_ref, loss_ref, dg_ref):
        @pl.when(pl.program_id(0) == 0)
        def _():
            loss_ref[...] = jnp.zeros_like(loss_ref)
            dg_ref[...] = jnp.zeros_like(dg_ref)

        xv = x_ref[...]
        gv = g_ref[...]
        r = _rms_rstd(xv)
        err = xv * r * gv - t_ref[...]
        loss_ref[...] += 0.5 * jnp.sum(jnp.mean(err * err, axis=-1, keepdims=True), axis=0, keepdims=True)
        dx, dg = _rms_bwd(err * (1.0 / D_MODEL), xv, r, gv)
        dx_ref[...] = dx
        dg_ref[...] += dg

    row = lambda i: (i, 0)
    return pl.pallas_call(
        body, name="loss_fwd_bwd",
        grid=(s_len // tm,),
        in_specs=[pl.BlockSpec((tm, D_MODEL), row), pl.BlockSpec((tm, D_MODEL), row),
                  pl.BlockSpec((1, D_MODEL), lambda i: (0, 0))],
        out_specs=[pl.BlockSpec((tm, D_MODEL), row), pl.BlockSpec((1, 1), lambda i: (0, 0)),
                   pl.BlockSpec((1, D_MODEL), lambda i: (0, 0))],
        out_shape=[jax.ShapeDtypeStruct((s_len, D_MODEL), F32), jax.ShapeDtypeStruct((1, 1), F32),
                   jax.ShapeDtypeStruct((1, D_MODEL), F32)],
        compiler_params=_cparams("arbitrary"),
    )(x3, tgt, g)


def _rel_bucket_matrix():
    qi = jnp.arange(SWA_BLOCK)[:, None] + SWA_BLOCK
    kj = jnp.arange(2 * SWA_BLOCK)[None, :]
    dist = jnp.maximum(qi - kj, 0)
    max_exact = REL_BUCKETS // 2
    d = jnp.maximum(dist, 1).astype(F32)
    large = max_exact + (jnp.log(d / max_exact) / np.log(REL_MAX_DIST / max_exact)
                         * (REL_BUCKETS - max_exact)).astype(jnp.int32)
    large = jnp.minimum(large, REL_BUCKETS - 1)
    return jnp.where(dist < max_exact, dist, large).astype(jnp.int32)


def _swa_bias_into(bias_ref, bkt_ref, tab_ref):
    bk = bkt_ref[...]
    for h in range(N_HEADS):
        acc = jnp.zeros(bk.shape, F32)
        for bucket in range(REL_BUCKETS):
            acc = jnp.where(bk == bucket, tab_ref[bucket, h], acc)
        bias_ref[h] = acc


def _swa_valid(n):
    shape = (SWA_BLOCK, 2 * SWA_BLOCK)
    row = lax.broadcasted_iota(jnp.int32, shape, 0)
    col = lax.broadcasted_iota(jnp.int32, shape, 1)
    dist = row + SWA_BLOCK - col
    return (dist >= 0) & (dist < SWA_BLOCK) & ((col >= SWA_BLOCK) | (n > 0))


def _swa_windows(kp_ref, kc_ref, vp_ref, vc_ref):
    return (jnp.concatenate([kp_ref[...], kc_ref[...]], axis=0), jnp.concatenate([vp_ref[...], vc_ref[...]], axis=0))


def _swa_place(h):
    return slice(h // 2 * LANES, (h // 2 + 1) * LANES), h % 2, h // SWA_GROUP


def _move_half(x, src, dst):
    moved = x if src == dst else pltpu.roll(x, HEAD_DIM, 1)
    in_dst = (lax.broadcasted_iota(jnp.int32, x.shape, 1) >= HEAD_DIM) == bool(dst)
    return jnp.where(in_dst, moved, 0.0)


def _swa_probs(qk, bias, sink, valid):
    lg = jnp.where(valid, qk * Q_SCALE + bias, NEG_BIG)
    m = jnp.maximum(jnp.max(lg, axis=-1, keepdims=True), sink)
    e = jnp.exp(lg - m)
    es = jnp.exp(sink - m)
    inv = 1.0 / (jnp.sum(e, axis=-1, keepdims=True) + es)
    return e * inv, es * inv


def _swa_specs(s_len):
    blk = SWA_BLOCK
    cur = lambda n: (n, 0)
    prev = lambda n: (jnp.maximum(n - 1, 0), 0)
    kvw = SWA_KV_HEADS * HEAD_DIM
    return [pl.BlockSpec(memory_space=pltpu.SMEM), pl.BlockSpec(memory_space=pltpu.SMEM),
            pl.BlockSpec((blk, 2 * blk), lambda n: (0, 0)),
            pl.BlockSpec((blk, N_HEADS * HEAD_DIM), cur),
            pl.BlockSpec((blk, kvw), prev), pl.BlockSpec((blk, kvw), cur),
            pl.BlockSpec((blk, kvw), prev), pl.BlockSpec((blk, kvw), cur)]


def _swa_fwd(tab, sinks, bkt, q, k, v):
    s_len = q.shape[0]
    blk = SWA_BLOCK

    def body(tab_ref, sink_ref, bkt_ref, q_ref, kp_ref, kc_ref, vp_ref, vc_ref, o_ref, bias_ref):
        n = pl.program_id(0)

        @pl.when(n == 0)
        def _():
            _swa_bias_into(bias_ref, bkt_ref, tab_ref)

        valid = _swa_valid(n)
        kk, vv = _swa_windows(kp_ref, kc_ref, vp_ref, vc_ref)
        st = {}

        def s_logits(h):
            tile, mine, kv = _swa_place(h)
            st[h, "lg"] = _dot_nt(_move_half(q_ref[:, tile].astype(F32), mine, kv).astype(BF16), kk)

        def s_probs(h):
            st[h, "p"] = _swa_probs(st.pop((h, "lg")), bias_ref[h], sink_ref[0, h], valid)[0].astype(BF16)

        def s_values(h):
            tile, mine, kv = _swa_place(h)
            part = _move_half(_dot(st.pop((h, "p")), vv), kv, mine)
            if mine == 0:
                st[h + 1, "o"] = part
            else:
                o_ref[:, tile] = (st.pop((h, "o")) + part).astype(BF16)

        _emit_skewed((list(range(N_HEADS)), [s_logits, s_probs, s_values]))

    return pl.pallas_call(
        body, name="swa_fwd",
        grid=(s_len // blk,),
        in_specs=_swa_specs(s_len),
        out_specs=pl.BlockSpec((blk, N_HEADS * HEAD_DIM), lambda n: (n, 0)),
        out_shape=jax.ShapeDtypeStruct((s_len, N_HEADS * HEAD_DIM), BF16),
        scratch_shapes=[pltpu.VMEM((N_HEADS, blk, 2 * blk), F32)],
        compiler_params=_cparams("arbitrary"),
    )(tab, sinks, bkt, q, k, k, v, v)


def _swa_bwd(tab, sinks, bkt, q, k, v, do, comm=None):
    s_len = q.shape[0]
    blk = SWA_BLOCK
    nb = s_len // blk
    kvw = SWA_KV_HEADS * HEAD_DIM

    def body(tab_ref, sink_ref, bkt_ref, q_ref, kp_ref, kc_ref, vp_ref, vc_ref, do_ref,
             dq_ref, dk_ref, dv_ref, dtab_ref, dsink_ref, bias_ref, dbias_ref):
        n = pl.program_id(0)

        @pl.when(n == 0)
        def _():
            _swa_bias_into(bias_ref, bkt_ref, tab_ref)
            dbias_ref[...] = jnp.zeros_like(dbias_ref)
            dk_ref[...] = jnp.zeros_like(dk_ref)
            dv_ref[...] = jnp.zeros_like(dv_ref)
            dsink_ref[...] = jnp.zeros_like(dsink_ref)
            dtab_ref[...] = jnp.zeros_like(dtab_ref)

        valid = _swa_valid(n)
        cur_rows = pl.ds(pl.multiple_of(n * blk, blk), blk)
        prev_rows = pl.ds(pl.multiple_of(jnp.maximum(n - 1, 0) * blk, blk), blk)
        kk, vv = _swa_windows(kp_ref, kc_ref, vp_ref, vc_ref)
        st = {}

        def s_logits(h):
            tile, mine, kv = _swa_place(h)
            st[h, "q"] = _move_half(q_ref[:, tile].astype(F32), mine, kv).astype(BF16)
            st[h, "do"] = _move_half(do_ref[:, tile].astype(F32), mine, kv).astype(BF16)
            st[h, "lg"] = _dot_nt(st[h, "q"], kk)
            st[h, "dp"] = _dot_nt(st[h, "do"], vv)

        def s_probs(h):
            p, ps = _swa_probs(st.pop((h, "lg")), bias_ref[h], sink_ref[0, h], valid)
            dp = st.pop((h, "dp"))
            delta = jnp.sum(p * dp, axis=-1, keepdims=True)
            dl = p * (dp - delta)
            dsink_ref[h:h + 1, :] += jnp.broadcast_to(-jnp.sum(ps * delta, axis=0, keepdims=True), (1, LANES))
            dbias_ref[h] += dl
            st[h, "dl"], st[h, "p"] = dl.astype(BF16), p.astype(BF16)

        def s_products(h):
            tile, mine, kv = _swa_place(h)
            dlb = st.pop((h, "dl"))
            part = _move_half(Q_SCALE * _dot(dlb, kk), kv, mine)
            if mine == 0:
                st[h + 1, "dq"] = part
            else:
                dq_ref[:, tile] = (st.pop((h, "dq")) + part).astype(BF16)
            dk_win = Q_SCALE * _dot_tn(dlb, st.pop((h, "q")))
            dv_win = _dot_tn(st.pop((h, "p")), st.pop((h, "do")))
            dk_ref[prev_rows, :] += dk_win[:blk]
            dv_ref[prev_rows, :] += dv_win[:blk]
            dk_ref[cur_rows, :] += dk_win[blk:]
            dv_ref[cur_rows, :] += dv_win[blk:]

        _emit_skewed((list(range(N_HEADS)), [s_logits, s_probs, s_products]))

        @pl.when(n == nb - 1)
        def _():
            bk = bkt_ref[...]
            lane = lax.broadcasted_iota(jnp.int32, (1, LANES), 1)
            for bucket in range(REL_BUCKETS):
                rowv = jnp.zeros((1, LANES), F32)
                for h in range(N_HEADS):
                    val = jnp.sum(jnp.where(bk == bucket, dbias_ref[h], 0.0), axis=1, keepdims=True)
                    val = jnp.sum(val, axis=0, keepdims=True)
                    rowv = jnp.where(lane == h, val, rowv)
                dtab_ref[bucket:bucket + 1, :] = rowv

    return _call(
        body, (tab, sinks, bkt, q, k, k, v, v, do), comm=comm, **_grid_ends(nb), name="swa_bwd",
        grid=(nb,),
        in_specs=_swa_specs(s_len) + [pl.BlockSpec((blk, N_HEADS * HEAD_DIM), lambda n: (n, 0))],
        out_specs=[pl.BlockSpec((blk, N_HEADS * HEAD_DIM), lambda n: (n, 0)),
                   pl.BlockSpec((s_len, kvw), lambda n: (0, 0)), pl.BlockSpec((s_len, kvw), lambda n: (0, 0)),
                   pl.BlockSpec((REL_BUCKETS, LANES), lambda n: (0, 0)), pl.BlockSpec((N_HEADS, LANES), lambda n: (0, 0))],
        out_shape=[jax.ShapeDtypeStruct((s_len, N_HEADS * HEAD_DIM), BF16),
                   jax.ShapeDtypeStruct((s_len, kvw), F32), jax.ShapeDtypeStruct((s_len, kvw), F32),
                   jax.ShapeDtypeStruct((REL_BUCKETS, LANES), F32), jax.ShapeDtypeStruct((N_HEADS, LANES), F32)],
        scratch_shapes=[pltpu.VMEM((N_HEADS, blk, 2 * blk), F32), pltpu.VMEM((N_HEADS, blk, 2 * blk), F32)],
        compiler_params=_cparams("arbitrary"),
    )


def _sb_terms(z, valid):
    zc = jnp.minimum(z, SB_LOGIT_CAP)
    lk = -jnp.log(1.0 + jnp.exp(zc))
    lsz = zc + lk
    return lsz, (lk if valid is None else jnp.where(valid, lk, 0.0))


def _bf16_parts(vals):
    parts, rest = [], vals
    for n in range(SB_SUM_PARTS):
        parts.append(rest.astype(BF16))
        if n + 1 < SB_SUM_PARTS:
            rest = rest - parts[-1].astype(F32)
    return parts[0] if len(parts) == 1 else jnp.concatenate(parts, axis=1)


def _row_sum_lanes(vals):
    return jnp.broadcast_to(jnp.sum(vals, axis=-1, keepdims=True), (vals.shape[0], LANES))


def _emit_skewed(*groups):
    for step in range(max(len(items) + len(stages) - 1 for items, stages in groups)):
        for items, stages in groups:
            for s, stage in enumerate(stages):
                if 0 <= step - s < len(items) and items[step - s] is not None:
                    stage(items[step - s])


def _sb_items(edge):
    items = []
    for h in range(2):
        for r0 in range(0, SB_QUERIES, SB_ROWS):
            if edge is None or r0 >= (edge + 1) * SB_KEYS:
                items.append((h, r0, False))
            else:
                items.append((h, r0, True) if r0 + SB_ROWS - 1 > edge * SB_KEYS else None)
    return items


def _sb_valid(w, edge):
    row = lax.broadcasted_iota(jnp.int32, (SB_ROWS, SB_KEYS), 0) + w[1]
    col = lax.broadcasted_iota(jnp.int32, (SB_ROWS, SB_KEYS), 1) + edge * SB_KEYS
    return col < row


def _sb_consts(tq, tk):
    low = lax.broadcasted_iota(jnp.int32, (tq, LANES), 1) < HEAD_DIM
    row = lax.broadcasted_iota(jnp.int32, (tk, tk), 0)
    col = lax.broadcasted_iota(jnp.int32, (tk, tk), 1)
    right = (row > col).astype(BF16)
    left = (row < col).astype(BF16)
    return low, jnp.concatenate([right] * SB_SUM_PARTS, axis=0), jnp.concatenate([left] * SB_SUM_PARTS, axis=0)


def _sb_fwd(q, k, v, comm=None):
    s_len = q.shape[0]
    tq, tk, tr = SB_QUERIES, SB_KEYS, SB_ROWS
    nk, ratio = s_len // tk, tq // tk
    assert nk <= LANES

    def body(q_ref, k_ref, v_ref, o_ref, car_ref, c_ref, oacc_ref, logw_ref, lksum_ref):
        i = pl.program_id(1)
        qv = q_ref[...]
        low, tri2, _ = _sb_consts(tq, tk)
        lane = lax.broadcasted_iota(jnp.int32, (tr, LANES), 1)
        zero = jnp.zeros_like(qv)
        q_heads = (jnp.where(low, qv, zero), jnp.where(low, zero, qv))
        c_ref[...] = jnp.zeros_like(c_ref)
        oacc_ref[...] = jnp.zeros_like(oacc_ref)
        car_ref[...] = jnp.full_like(car_ref, NEG_BIG)

        def front(j, edge):
            keys = k_ref[pl.ds(pl.multiple_of(j * tk, tk), tk), :]
            slot = j % SB_SLOTS
            st = {}

            def s_logits(w):
                st[w, "z"] = _dot_nt(q_heads[w[0]][w[1]:w[1] + tr], keys)

            def s_terms(w):
                valid = _sb_valid(w, edge) if w[2] else None
                lsz, lk = _sb_terms(st.pop((w, "z")), valid)
                st[w, "parts"] = _bf16_parts(lk)
                st[w, "lsz"] = lsz if valid is None else jnp.where(valid, lsz, NEG_BIG)
                lksum_ref[slot, w[0], w[1]:w[1] + tr, :] = _row_sum_lanes(lk)

            def s_suffix(w):
                logw_ref[slot, w[0], w[1]:w[1] + tr, :] = st.pop((w, "lsz")) + _dot(st.pop((w, "parts")), tri2)

            return _sb_items(edge), [s_logits, s_terms, s_suffix]

        def back(j, edge):
            vv = v_ref[pl.ds(pl.multiple_of(j * tk, tk), tk), :]
            slot = j % SB_SLOTS
            st = {}

            def s_weights(w):
                h, rs = w[0], slice(w[1], w[1] + tr)
                c = c_ref[h, rs, :]
                st[w, "a"] = jnp.exp(logw_ref[slot, h, rs, :] + jnp.tile(c, (1, tk // LANES))).astype(BF16)
                car_ref[h, rs, :] = jnp.where(lane == j, c, car_ref[h, rs, :])
                c_ref[h, rs, :] = c + lksum_ref[slot, h, rs, :]

            def s_values(w):
                oacc_ref[w[0], w[1]:w[1] + tr, :] += _dot(st.pop((w, "a")), vv)

            return _sb_items(edge), [s_weights, s_values]

        first = i * ratio
        edge_tiles = [(first + m, m) for m in reversed(range(ratio))]

        def alive():
            return (jnp.max(c_ref[...]) >= SB_DEAD_CARRY).astype(jnp.int32)

        @pl.when(i == 0)
        def _():
            _emit_skewed(*[front(j, m) for j, m in edge_tiles])
            _emit_skewed(*[back(j, m) for j, m in edge_tiles])

        @pl.when(i > 0)
        def _():
            tiles = edge_tiles + [(first - 1, None)]
            _emit_skewed(*[front(j, m) for j, m in tiles])
            _emit_skewed(*[back(j, m) for j, m in tiles])

            @pl.when((alive() > 0) & (first >= 2))
            def _():
                _emit_skewed(front(first - 2, None))

                def step(state):
                    pending, _ = state
                    _emit_skewed(front(pending - 1, None), back(pending, None))
                    return pending - 1, alive()

                pending, live = lax.while_loop(lambda s: (s[0] > 0) & (s[1] > 0), step, (first - 2, jnp.int32(1)))

                @pl.when(live > 0)
                def _():
                    _emit_skewed(back(pending, None))

        o_ref[...] = jnp.where(low, oacc_ref[0], oacc_ref[1]).astype(BF16)

    return _call(
        body, (q, k, v), comm=comm, **_grid_ends(N_HEADS // 2, s_len // tq), name="sb_fwd",
        grid=(N_HEADS // 2, s_len // tq),
        in_specs=[pl.BlockSpec((tq, LANES), lambda p, i: (i, p)),
                  pl.BlockSpec((s_len, LANES), lambda p, i: (0, p)),
                  pl.BlockSpec((s_len, LANES), lambda p, i: (0, p))],
        out_specs=[pl.BlockSpec((tq, LANES), lambda p, i: (i, p)), pl.BlockSpec((2, tq, LANES), lambda p, i: (p, i, 0))],
        out_shape=[jax.ShapeDtypeStruct((s_len, N_HEADS * HEAD_DIM), BF16),
                   jax.ShapeDtypeStruct((N_HEADS, s_len, LANES), F32)],
        scratch_shapes=[pltpu.VMEM((2, tq, LANES), F32), pltpu.VMEM((2, tq, LANES), F32),
                        pltpu.VMEM((SB_SLOTS, 2, tq, tk), F32), pltpu.VMEM((SB_SLOTS, 2, tq, LANES), F32)],
        compiler_params=_cparams("arbitrary", "arbitrary"),
    )


def _sb_bwd(q, k, v, do, cars):
    s_len = q.shape[0]
    tq, tk, tr = SB_QUERIES, SB_KEYS, SB_ROWS
    nk, ratio = s_len // tk, tq // tk

    def body(q_ref, k_ref, v_ref, do_ref, car_ref, dq_ref, dk_ref, dv_ref,
             gleft_ref, dqacc_ref, dkacc_ref, dvacc_ref, logw_ref, lsz_ref, da_ref, a_ref, dz_ref):
        i = pl.program_id(1)

        @pl.when(i == 0)
        def _():
            dkacc_ref[...] = jnp.zeros_like(dkacc_ref)
            dvacc_ref[...] = jnp.zeros_like(dvacc_ref)

        qv = q_ref[...]
        dov = do_ref[...]
        low, tri_right2, tri_left2 = _sb_consts(tq, tk)
        lane = lax.broadcasted_iota(jnp.int32, (tr, LANES), 1)
        zero = jnp.zeros_like(qv)
        q_heads = (jnp.where(low, qv, zero), jnp.where(low, zero, qv))
        do_heads = (jnp.where(low, dov, zero), jnp.where(low, zero, dov))
        q_t = qv.astype(F32).T.astype(BF16)
        do_t = dov.astype(F32).T.astype(BF16)
        gleft_ref[...] = jnp.zeros_like(gleft_ref)
        dqacc_ref[...] = jnp.zeros_like(dqacc_ref)

        def front(j, edge):
            key_rows = pl.ds(pl.multiple_of(j * tk, tk), tk)
            keys, values = k_ref[key_rows, :], v_ref[key_rows, :]
            slot = j % SB_SLOTS
            st = {}

            def s_logits(w):
                h, rs = w[0], slice(w[1], w[1] + tr)
                st[w, "z"] = _dot_nt(q_heads[h][rs], keys)
                da_ref[slot, h, rs, :] = _dot_nt(do_heads[h][rs], values)

            def s_terms(w):
                h, rs = w[0], slice(w[1], w[1] + tr)
                valid = _sb_valid(w, edge) if w[2] else None
                lsz, lk = _sb_terms(st.pop((w, "z")), valid)
                st[w, "parts"] = _bf16_parts(lk)
                lsz = lsz if valid is None else jnp.where(valid, lsz, NEG_BIG)
                lsz_ref[slot, h, rs, :] = lsz
                st[w, "lszc"] = lsz + jnp.sum(jnp.where(lane == j, car_ref[h, rs, :], 0.0), axis=-1, keepdims=True)

            def s_suffix(w):
                logw_ref[slot, w[0], w[1]:w[1] + tr, :] = st.pop((w, "lszc")) + _dot(st.pop((w, "parts")), tri_right2)

            return _sb_items(edge), [s_logits, s_terms, s_suffix]

        def back(j, edge):
            kv = k_ref[pl.ds(pl.multiple_of(j * tk, tk), tk), :]
            slot = j % SB_SLOTS
            st = {}

            items = _sb_items(edge)
            head_rows = [[w[1] for w in items if w is not None and w[0] == h] for h in range(2)]

            def s_weights(w):
                h, rs = w[0], slice(w[1], w[1] + tr)
                a = jnp.exp(logw_ref[slot, h, rs, :])
                g = a * da_ref[slot, h, rs, :]
                a_ref[slot, h, rs, :] = a.astype(BF16)
                st[w, "g"], st[w, "parts"] = g, _bf16_parts(g)

            def s_prefix(w):
                st[w, "gs"] = _dot(st.pop((w, "parts")), tri_left2)

            def s_dz(w):
                h, rs = w[0], slice(w[1], w[1] + tr)
                g = st.pop((w, "g"))
                gleft = gleft_ref[h, rs, :]
                gsum = st.pop((w, "gs")) + jnp.tile(gleft, (1, tk // LANES))
                dz = (g - jnp.exp(lsz_ref[slot, h, rs, :]) * (g + gsum)).astype(BF16)
                st[w, "dz"] = dz
                dz_ref[slot, h, rs, :] = dz
                gleft_ref[h, rs, :] = gleft + _row_sum_lanes(g)

            def s_products(w):
                h, rs = w[0], slice(w[1], w[1] + tr)
                dqacc_ref[h, rs, :] += _dot(st.pop((w, "dz")), kv)
                if w[1] == head_rows[h][-1]:
                    feat = slice(h * HEAD_DIM, (h + 1) * HEAD_DIM)
                    hr = slice(head_rows[h][0], tq)
                    dkacc_ref[j, feat, :] += _dot(q_t[feat, hr], dz_ref[slot, h, hr, :])
                    dvacc_ref[j, feat, :] += _dot(do_t[feat, hr], a_ref[slot, h, hr, :])

            return items, [s_weights, s_prefix, s_dz, s_products]

        first = i * ratio
        tile_max = jnp.max(jnp.maximum(car_ref[0], car_ref[1]), axis=0, keepdims=True)
        start = jnp.clip(first + ratio - jnp.sum(jnp.where(tile_max >= SB_DEAD_CARRY, 1, 0)), 0, first)

        edge_tiles = [(first + m, m) for m in range(ratio)]

        @pl.when(start == first)
        def _():
            _emit_skewed(*[front(j, m) for j, m in edge_tiles])
            _emit_skewed(*[back(j, m) for j, m in edge_tiles])

        @pl.when(start == first - 1)
        def _():
            tiles = [(first - 1, None)] + edge_tiles
            _emit_skewed(*[front(j, m) for j, m in tiles])
            _emit_skewed(*[back(j, m) for j, m in tiles])

        @pl.when(start < first - 1)
        def _():
            _emit_skewed(front(start, None))

            def step(jj, carry):
                _emit_skewed(front(jj, None), back(jj - 1, None))
                return carry

            lax.fori_loop(start + 1, first, step, 0)
            _emit_skewed(front(first, 0), back(first - 1, None))
            for m in range(1, ratio):
                _emit_skewed(front(first + m, m), back(first + m - 1, m - 1))
            _emit_skewed(back(first + ratio - 1, ratio - 1))

        dq_ref[...] = (Q_SCALE * jnp.where(low, dqacc_ref[0], dqacc_ref[1])).astype(BF16)

        @pl.when(i == s_len // tq - 1)
        def _():
            for j in range(nk):
                dk_ref[j * tk:(j + 1) * tk, :] = dkacc_ref[j].T.astype(BF16)
                dv_ref[j * tk:(j + 1) * tk, :] = dvacc_ref[j].T.astype(BF16)

    qblk = pl.BlockSpec((tq, LANES), lambda p, i: (i, p))
    col_full = pl.BlockSpec((s_len, LANES), lambda p, i: (0, p))
    return pl.pallas_call(
        body, name="sb_bwd",
        grid=(N_HEADS // 2, s_len // tq),
        in_specs=[qblk, col_full, col_full, qblk, pl.BlockSpec((2, tq, LANES), lambda p, i: (p, i, 0))],
        out_specs=[qblk, col_full, col_full],
        out_shape=[jax.ShapeDtypeStruct((s_len, N_HEADS * HEAD_DIM), BF16)] * 3,
        scratch_shapes=[pltpu.VMEM((2, tq, LANES), F32), pltpu.VMEM((2, tq, LANES), F32),
                        pltpu.VMEM((nk, LANES, tk), F32), pltpu.VMEM((nk, LANES, tk), F32)]
        + [pltpu.VMEM((SB_SLOTS, 2, tq, tk), F32)] * 3 + [pltpu.VMEM((SB_SLOTS, 2, tq, tk), BF16)] * 2,
        compiler_params=_cparams("parallel", "arbitrary"),
    )(q, k, v, do, cars)


def _local_step(xs, tgt, gains, sinks, rel_bias, weights_of, ship):
    g1, gmix, g2, gfin = gains
    bkt = _rel_bucket_matrix()
    grads = {}

    def carried(outs, comm, count):
        return outs[:count], (list(outs[count:]) if comm is not None else None)

    wts = dict(weights_of(0, None))
    comm = ship("weights", 1)
    (x1, h1, a1, b1, u1), landed = carried(
        _ffn_fwd(xs, g1, wts["ffn1_w1t"], wts["ffn1_w3t"], wts["ffn1_w2"], "1", comm), comm, 5)
    wts.update(weights_of(1, landed))
    hm, qa, ka, va, qb, kb, vb, ga, gb = _proj_fwd(x1, gmix, wts["w_int"])
    oa = _swa_fwd(rel_bias, sinks, bkt, qa, ka, va)
    comm = ship("weights", 2)
    (ob, cars), landed = carried(_sb_fwd(qb, kb, vb, comm), comm, 2)
    wts.update(weights_of(2, landed))
    x2, mg = _merge_fwd(x1, oa, ob, ga, gb, wts["w_swa"], wts["w_sb"], wts["w_out"])
    x3, h3, a3, b3, u3 = _ffn_fwd(x2, g2, wts["ffn2_w1t"], wts["ffn2_w3t"], wts["ffn2_w2"], "2")
    dx3, loss, dgfin = _loss_fwd_bwd(x3, tgt, gfin)

    def grad_chain(items):
        prev = None
        for name, lhs, rhs in items:
            comm = None if prev is None else ship("grads", (prev[0],), prev[1])
            res = _matmul_tn_tiled(lhs, rhs, name, comm)
            if prev is not None:
                grads[(prev[0],)] = prev[1] if comm is None else res[1]
            prev = (name, {_GRAD_KEY[name]: res if comm is None else res[0]})
        return prev

    dx2, dg2, da3, db3, dx3b = _ffn_bwd(dx3, x2, g2, a3, b3, wts["ffn2_w1t"], wts["ffn2_w3t"], wts["ffn2_w2"], "2")
    last = grad_chain((("ffn2_w1", da3, h3), ("ffn2_w3", db3, h3), ("ffn2_w2", u3, dx3b)))
    comm = ship("grads", (last[0],), last[1])
    (doa, dob, dga, dgb, dpa, dpb, dx2b), landed = carried(
        _merge_bwd(dx2, oa, ob, ga, gb, wts["w_swa"], wts["w_sb"], wts["w_out"], comm), comm, 7)
    grads[(last[0],)] = last[1] if comm is None else landed[0]

    def keep(names, big, comm, landed):
        for i, name in enumerate(names):
            grads[(name,)] = {_GRAD_KEY[name]: big[_GRAD_KEY[name]]} if comm is None else landed[i]

    big = {"w_out": _matmul_tn(mg, dx2b, "w_out"), "w_swa": _matmul_tn(oa, dpa, "w_swa"),
           "w_sb": _matmul_tn(ob, dpb, "w_sb")}
    comm = ship("grads", GROUPS[1][1:], big)
    (dqa, dka, dva, dtab, dsink), landed = carried(_swa_bwd(rel_bias, sinks, bkt, qa, ka, va, doa, comm), comm, 5)
    keep(GROUPS[1][1:], big, comm, landed)
    dqb, dkb, dvb = _sb_bwd(qb, kb, vb, dob, cars)
    dpieces = (dqa, dka.astype(BF16), dva.astype(BF16), dqb, dkb, dvb, dga, dgb)
    big = {"w_int": _matmul_tn_stacked(dpieces, hm, "w_in")}
    comm = ship("grads", GROUPS[1][:1], big)
    (dx1, dgmix), landed = carried(_proj_bwd(dpieces, dx2, x1, gmix, wts["w_int"], comm), comm, 2)
    keep(GROUPS[1][:1], big, comm, landed)

    dx0, dg1, da1, db1, dx1b = _ffn_bwd(dx1, xs, g1, a1, b1, wts["ffn1_w1t"], wts["ffn1_w3t"], wts["ffn1_w2"], "1")

    last = grad_chain((("ffn1_w1", da1, h1), ("ffn1_w3", db1, h1), ("ffn1_w2", u1, dx1b)))
    grads[(last[0],)] = last[1]

    small = {"gains": (dg1, dgmix, dg2, dgfin), "sinks": dsink[:, 0], "rel_bias": dtab[:, :N_HEADS]}
    return loss, dx0, small, grads


def _my_place():
    return lax.axis_index("x"), lax.axis_index("y"), lax.axis_index("c")


def _flip(v, bit):
    return 1 - v if bit else v


_RELATIONS = tuple((k >> 2 & 1, k >> 1 & 1, k & 1) for k in range(1, N_DEV))


def _gather_weights(blocks, tag):
    count = len(blocks)

    def body(*refs):
        x_refs, out_refs = refs[:count], refs[count:2 * count]
        send_sems, recv_sems, local_sems = refs[2 * count:]
        x, y, c = _my_place()
        me, sibling = (x, y, c), (x, y, 1 - c)
        chips = [(1 - x, y), (x, 1 - y), (1 - x, 1 - y)]

        def rows(s, px, py, pc):
            return out_refs[s].at[4 * px + 2 * py + pc]

        def copy(s, k, block, to, src=None):
            return pltpu.make_async_remote_copy(
                src_ref=rows(s, *block) if src is None else src, dst_ref=rows(s, *block),
                send_sem=send_sems.at[s, k], recv_sem=recv_sems.at[s, k],
                device_id=to, device_id_type=pl.DeviceIdType.MESH)

        mine = [pltpu.make_async_copy(x_refs[s], rows(s, *me), local_sems.at[s]) for s in range(count)]
        first, passed = [], []
        for s in range(count):
            mine[s].start()
            first.append(copy(s, 0, me, sibling, src=x_refs[s]))
            first += [copy(s, 1 + j, me, (*chip, c), src=x_refs[s]) for j, chip in enumerate(chips)]
        for cp in first:
            cp.start()
        for s in range(count):
            for j, chip in enumerate(chips):
                copy(s, 1 + j, (*chip, c), me).wait_recv()
                passed.append(copy(s, 4 + j, (*chip, c), sibling))
                passed[-1].start()
        for s in range(count):
            copy(s, 0, sibling, me).wait_recv()
            for j, chip in enumerate(chips):
                copy(s, 4 + j, (*chip, 1 - c), me).wait_recv()
        for cp in first + passed:
            cp.wait_send()
        for cp in mine:
            cp.wait()

    anywhere = pl.BlockSpec(memory_space=pl.ANY)
    return pl.pallas_call(
        body, name=f"gather_weights_{tag}",
        out_shape=[jax.ShapeDtypeStruct((N_DEV,) + b.shape, b.dtype) for b in blocks],
        in_specs=[anywhere] * count, out_specs=[anywhere] * count,
        scratch_shapes=[pltpu.SemaphoreType.DMA((count, N_DEV - 1)), pltpu.SemaphoreType.DMA((count, N_DEV - 1)),
                        pltpu.SemaphoreType.DMA((count,))],
    )(*blocks)


def _exchange_grads(gp, tag):
    def body(g_ref, out_ref, send_sems, recv_sems, local_sem):
        x, y, c = _my_place()
        me = 4 * x + 2 * y + c
        mine = pltpu.make_async_copy(g_ref.at[me], out_ref.at[me], local_sem)
        mine.start()
        copies = []
        for k, (fx, fy, fc) in enumerate(_RELATIONS):
            px, py, pc = _flip(x, fx), _flip(y, fy), _flip(c, fc)
            peer = 4 * px + 2 * py + pc
            copies.append((
                pltpu.make_async_remote_copy(
                    src_ref=g_ref.at[peer], dst_ref=out_ref.at[me], send_sem=send_sems.at[k], recv_sem=recv_sems.at[k],
                    device_id=(px, py, pc), device_id_type=pl.DeviceIdType.MESH),
                pltpu.make_async_remote_copy(
                    src_ref=g_ref.at[peer], dst_ref=out_ref.at[peer], send_sem=send_sems.at[k], recv_sem=recv_sems.at[k],
                    device_id=(px, py, pc), device_id_type=pl.DeviceIdType.MESH)))
        for out_cp, _ in copies:
            out_cp.start()
        for _, in_cp in copies:
            in_cp.wait_recv()
        for out_cp, _ in copies:
            out_cp.wait_send()
        mine.wait()

    return pl.pallas_call(
        body, name=f"exchange_grads_{tag}",
        out_shape=jax.ShapeDtypeStruct(gp.shape, gp.dtype),
        in_specs=[pl.BlockSpec(memory_space=pl.ANY)],
        out_specs=pl.BlockSpec(memory_space=pl.ANY),
        scratch_shapes=[pltpu.SemaphoreType.DMA((7,)), pltpu.SemaphoreType.DMA((7,)), pltpu.SemaphoreType.DMA(())],
    )(gp)


def _peers():
    x, y, c = _my_place()
    out = []
    for k, (fx, fy, fc) in enumerate(_RELATIONS):
        px, py, pc = _flip(x, fx), _flip(y, fy), _flip(c, fc)
        out.append((k, (px, py, pc), 4 * px + 2 * py + pc))
    return out, 4 * x + 2 * y + c


def _grid_ends(*grid):
    def first():
        return functools.reduce(lambda a, b: a & b, [pl.program_id(d) == 0 for d in range(len(grid))])

    def last():
        return functools.reduce(lambda a, b: a & b, [pl.program_id(d) == n - 1 for d, n in enumerate(grid)])

    return {"first": first, "last": last}


def _call(body, operands, *, comm=None, first=None, last=None, **kw):
    if comm is None:
        return pl.pallas_call(body, **kw)(*operands)
    in_specs, out_specs, out_shape = list(kw.pop("in_specs")), list(kw.pop("out_specs")), list(kw.pop("out_shape"))
    scratch = list(kw.pop("scratch_shapes", ()))
    n_in, n_out, n_scr, n_src = len(in_specs), len(out_specs), len(scratch), len(comm)

    def wrapped(*refs):
        ins, src_refs = refs[:n_in], refs[n_in:n_in + n_src]
        outs = refs[n_in + n_src:n_in + n_src + n_out]
        land_refs = refs[n_in + n_src + n_out:n_in + 2 * n_src + n_out]
        scr = refs[n_in + 2 * n_src + n_out:n_in + 2 * n_src + n_out + n_scr]
        send_sems, recv_sems, local_sems = refs[n_in + 2 * n_src + n_out + n_scr:]
        peers, me = _peers()
        mine, going, coming = [], [], []
        for s, (_, per_peer) in enumerate(comm):
            src_ref, land_ref = src_refs[s], land_refs[s]
            mine.append(pltpu.make_async_copy(src_ref.at[me] if per_peer else src_ref, land_ref.at[me], local_sems.at[s]))
            for k, where, slab in peers:
                piece = src_ref.at[slab] if per_peer else src_ref
                going.append(pltpu.make_async_remote_copy(
                    src_ref=piece, dst_ref=land_ref.at[me], send_sem=send_sems.at[s, k], recv_sem=recv_sems.at[s, k],
                    device_id=where, device_id_type=pl.DeviceIdType.MESH))
                coming.append(pltpu.make_async_remote_copy(
                    src_ref=piece, dst_ref=land_ref.at[slab], send_sem=send_sems.at[s, k], recv_sem=recv_sems.at[s, k],
                    device_id=where, device_id_type=pl.DeviceIdType.MESH))

        @pl.when(first())
        def _():
            for cp in mine + going:
                cp.start()

        body(*ins, *outs, *scr)

        @pl.when(last())
        def _():
            for cp in coming:
                cp.wait_recv()
            for cp in going:
                cp.wait_send()
            for cp in mine:
                cp.wait()

    anywhere = pl.BlockSpec(memory_space=pl.ANY)
    lands = [jax.ShapeDtypeStruct(src.shape if per_peer else (N_DEV,) + src.shape, src.dtype) for src, per_peer in comm]
    return pl.pallas_call(
        wrapped, in_specs=in_specs + [anywhere] * n_src, out_specs=out_specs + [anywhere] * n_src,
        out_shape=out_shape + lands,
        scratch_shapes=scratch + [pltpu.SemaphoreType.DMA((n_src, N_DEV - 1)), pltpu.SemaphoreType.DMA((n_src, N_DEV - 1)),
                                  pltpu.SemaphoreType.DMA((n_src,))],
        **kw)(*operands, *[src for src, _ in comm])


def _adamw(w, g, m, v):
    m = ADAM_B1 * m + (1.0 - ADAM_B1) * g
    v = ADAM_B2 * v + (1.0 - ADAM_B2) * jnp.square(g)
    m_hat = m / (1.0 - ADAM_B1 ** ADAM_STEP)
    v_hat = v / (1.0 - ADAM_B2 ** ADAM_STEP)
    delta = -ADAM_LR * (m_hat / (jnp.sqrt(v_hat) + ADAM_EPS) + ADAM_WD * w)
    return delta, m, v


def _sum_and_adamw(parts, w, m, v, tr, tag):
    rows = w.shape[0]
    assert rows % tr == 0

    def body(p_ref, w_ref, m_ref, v_ref, g_out, d_out, m_out, v_out):
        g = p_ref[0].astype(F32)
        for d in range(1, N_DEV):
            g = g + p_ref[d].astype(F32)
        delta, mn, vn = _adamw(w_ref[...], g, m_ref[...], v_ref[...])
        g_out[...] = g
        d_out[...] = delta
        m_out[...] = mn
        v_out[...] = vn

    sp = pl.BlockSpec((tr, D_MODEL), lambda i: (i, 0))
    return pl.pallas_call(
        body, name=f"sum_and_adamw_{tag}",
        grid=(rows // tr,),
        in_specs=[pl.BlockSpec((N_DEV, tr, D_MODEL), lambda i: (0, i, 0)), sp, sp, sp],
        out_specs=[sp] * 4,
        out_shape=[jax.ShapeDtypeStruct(w.shape, F32)] * 4,
        compiler_params=_cparams("parallel"),
    )(parts, w, m, v)


def _small_allreduce_adamw(part, w, m, v):
    def body(p_ref, w_ref, m_ref, v_ref, g_out, d_out, m_out, v_out, buf, send_sems, recv_sems):
        x, y, c = _my_place()
        me = 4 * x + 2 * y + c
        buf[me] = p_ref[...]
        copies = []
        for k, (fx, fy, fc) in enumerate(_RELATIONS):
            px, py, pc = _flip(x, fx), _flip(y, fy), _flip(c, fc)
            peer = 4 * px + 2 * py + pc
            copies.append((
                pltpu.make_async_remote_copy(
                    src_ref=buf.at[me], dst_ref=buf.at[me], send_sem=send_sems.at[k], recv_sem=recv_sems.at[k],
                    device_id=(px, py, pc), device_id_type=pl.DeviceIdType.MESH),
                pltpu.make_async_remote_copy(
                    src_ref=buf.at[me], dst_ref=buf.at[peer], send_sem=send_sems.at[k], recv_sem=recv_sems.at[k],
                    device_id=(px, py, pc), device_id_type=pl.DeviceIdType.MESH)))
        for out_cp, _ in copies:
            out_cp.start()
        for _, in_cp in copies:
            in_cp.wait_recv()
        for out_cp, _ in copies:
            out_cp.wait_send()
        g = buf[0]
        for d in range(1, N_DEV):
            g = g + buf[d]
        delta, mn, vn = _adamw(w_ref[...], g, m_ref[...], v_ref[...])
        g_out[...] = g
        d_out[...] = delta
        m_out[...] = mn
        v_out[...] = vn

    vm = pl.BlockSpec(memory_space=pltpu.VMEM)
    return pl.pallas_call(
        body, name="small_allreduce_adamw",
        in_specs=[vm] * 4, out_specs=[vm] * 4,
        out_shape=[jax.ShapeDtypeStruct(w.shape, F32)] * 4,
        scratch_shapes=[pltpu.VMEM((N_DEV,) + part.shape, F32),
                        pltpu.SemaphoreType.DMA((7,)), pltpu.SemaphoreType.DMA((7,))],
    )(part, w, m, v)


_TRANSPOSED = ("ffn1_w1", "ffn1_w3", "w_in", "ffn2_w1", "ffn2_w3")
_BRANCH = ("w_branch_swa", "w_branch_sb")


def _pack_shards(t, names):
    parts = []
    for name in names:
        a = t[name][0]
        if name in _TRANSPOSED:
            a = a.T
        elif name in _BRANCH:
            a = a.reshape(64, D_MODEL)
        parts.append(a)
    return jnp.concatenate(parts, axis=0)


def _unpack_shards(p, names):
    out, lo = {}, 0
    for name in names:
        a = p[lo:lo + BIG_ROWS[BIG_NAMES.index(name)]]
        lo += a.shape[0]
        if name in _TRANSPOSED:
            a = a.T
        elif name in _BRANCH:
            a = a.reshape(512, 128)
        out[name] = a[None]
    return out


def _full_weights(zones, names):
    out = {}
    for name, a in zip(names, zones):
        if name in _BRANCH:
            a = a.reshape(N_DEV, 512, 128).transpose(1, 0, 2).reshape(512, D_MODEL)
        out[_GRAD_KEY[name]] = a.reshape(-1, D_MODEL)
    return out


_GRAD_KEY = {"ffn1_w1": "ffn1_w1t", "ffn1_w3": "ffn1_w3t", "ffn1_w2": "ffn1_w2", "w_in": "w_int",
             "w_branch_swa": "w_swa", "w_branch_sb": "w_sb", "w_out": "w_out",
             "ffn2_w1": "ffn2_w1t", "ffn2_w3": "ffn2_w3t", "ffn2_w2": "ffn2_w2"}


def _pack_full_grads(big, names):
    parts = []
    for name in names:
        a = big[_GRAD_KEY[name]]
        if name in _BRANCH:
            a = a.reshape(512, N_DEV, 128).transpose(1, 0, 2)
        parts.append(a.reshape(N_DEV, BIG_ROWS[BIG_NAMES.index(name)], D_MODEL).astype(BF16))
    return jnp.concatenate(parts, axis=1)


_SMALL_NAMES = ("norm_ffn1", "norm_mix", "norm_ffn2", "norm_final", "swa_sinks", "rel_bias")


def _pack_small(vals):
    rows = []
    for a in vals:
        a = a.reshape(-1)
        rows.append(jnp.pad(a, (0, D_MODEL - a.shape[0])))
    rows += [jnp.zeros((D_MODEL,), F32)] * (SMALL_ROWS - len(rows))
    return jnp.stack(rows)


def _unpack_small(p):
    return {"norm_ffn1": p[0:1], "norm_mix": p[1:2], "norm_ffn2": p[2:3], "norm_final": p[3],
            "swa_sinks": p[4:5, :N_HEADS], "rel_bias": p[5, :REL_BUCKETS * N_HEADS].reshape(REL_BUCKETS, N_HEADS)}


ALL_NAMES = ("norm_ffn1", "ffn1_w1", "ffn1_w3", "ffn1_w2", "norm_mix", "w_in", "swa_sinks", "rel_bias",
             "w_branch_swa", "w_branch_sb", "w_out", "norm_ffn2", "ffn2_w1", "ffn2_w3", "ffn2_w2", "norm_final")


def kernel(x, norm_ffn1, ffn1_w1, ffn1_w3, ffn1_w2, norm_mix, w_in, swa_sinks, rel_bias, w_branch_swa, w_branch_sb, w_out, norm_ffn2, ffn2_w1, ffn2_w3, ffn2_w2, norm_final, loss_target, m_norm_ffn1, m_ffn1_w1, m_ffn1_w3, m_ffn1_w2, m_norm_mix, m_w_in, m_swa_sinks, m_rel_bias, m_w_branch_swa, m_w_branch_sb, m_w_out, m_norm_ffn2, m_ffn2_w1, m_ffn2_w3, m_ffn2_w2, m_norm_final, v_norm_ffn1, v_ffn1_w1, v_ffn1_w3, v_ffn1_w2, v_norm_mix, v_w_in, v_swa_sinks, v_rel_bias, v_w_branch_swa, v_w_branch_sb, v_w_out, v_norm_ffn2, v_ffn2_w1, v_ffn2_w3, v_ffn2_w2, v_norm_final):
    w = dict(zip(ALL_NAMES, (norm_ffn1, ffn1_w1, ffn1_w3, ffn1_w2, norm_mix, w_in, swa_sinks, rel_bias,
                             w_branch_swa, w_branch_sb, w_out, norm_ffn2, ffn2_w1, ffn2_w3, ffn2_w2, norm_final)))
    m = dict(zip(ALL_NAMES, (m_norm_ffn1, m_ffn1_w1, m_ffn1_w3, m_ffn1_w2, m_norm_mix, m_w_in, m_swa_sinks, m_rel_bias,
                             m_w_branch_swa, m_w_branch_sb, m_w_out, m_norm_ffn2, m_ffn2_w1, m_ffn2_w3, m_ffn2_w2,
                             m_norm_final)))
    v = dict(zip(ALL_NAMES, (v_norm_ffn1, v_ffn1_w1, v_ffn1_w3, v_ffn1_w2, v_norm_mix, v_w_in, v_swa_sinks, v_rel_bias,
                             v_w_branch_swa, v_w_branch_sb, v_w_out, v_norm_ffn2, v_ffn2_w1, v_ffn2_w3, v_ffn2_w2,
                             v_norm_final)))

    def my_blocks(group):
        return [_pack_shards(w, (name,)).astype(BF16) for name in GROUPS[group]]

    gathered0 = _gather_weights(my_blocks(0), "group0")

    def weights_of(group, landed):
        return _full_weights(gathered0 if group == 0 else landed, GROUPS[group])

    def ship(kind, which, grads=None):
        if kind == "weights":
            return [(block, False) for block in my_blocks(which)]
        return [(_pack_full_grads(grads, (name,)), True) for name in which]

    gains = (norm_ffn1, norm_mix, norm_ffn2, norm_final.reshape(1, D_MODEL))
    loss, dx, small, parts = _local_step(x[0], loss_target[0], gains, swa_sinks, rel_bias, weights_of, ship)

    big_outs = [{}, {}, {}, {}]
    for names, tile in zip(SUM_GROUPS, SUM_TILE):
        landed = parts[names]
        if isinstance(landed, dict):
            landed = _exchange_grads(_pack_full_grads(landed, names), names[0])
        res = _sum_and_adamw(landed, _pack_shards(w, names), _pack_shards(m, names), _pack_shards(v, names),
                             tile, names[0])
        for acc, packed in zip(big_outs, res):
            acc.update(_unpack_shards(packed, names))
    g_big, d_big, m_big, v_big = big_outs

    small_part = _pack_small(small["gains"] + (small["sinks"], small["rel_bias"], loss))
    zero = jnp.zeros((1,), F32)
    small_res = _small_allreduce_adamw(
        small_part, _pack_small([w[n] for n in _SMALL_NAMES] + [zero]), _pack_small([m[n] for n in _SMALL_NAMES] + [zero]),
        _pack_small([v[n] for n in _SMALL_NAMES] + [zero]))
    g_sm, d_sm, m_sm, v_sm = (_unpack_small(p) for p in small_res)

    outs = [small_res[0][len(_SMALL_NAMES), 0], dx[None]]
    for big_d, small_d in ((g_big, g_sm), (d_big, d_sm), (m_big, m_sm), (v_big, v_sm)):
        merged = {**big_d, **small_d}
        outs += [merged[n] for n in ALL_NAMES]
    return tuple(outs)
```

```python
import functools

import jax
import jax.numpy as jnp
import numpy as np
from jax import lax
from jax.experimental import pallas as pl
from jax.experimental.pallas import tpu as pltpu

F32 = jnp.float32
BF16 = jnp.bfloat16

D_MODEL = 1024
D_FF = 2816
HEAD_DIM = 64
N_HEADS = 8
SWA_KV_HEADS = 2
SWA_GROUP = 4
SWA_BLOCK = 128
REL_BUCKETS = 32
REL_MAX_DIST = 128
RMS_EPS = 1e-6
NEG_BIG = -1e30
Q_SCALE = HEAD_DIM ** -0.5
LANES = 128

N_DEV = 8

ADAM_LR = 0.001
ADAM_B1 = 0.9
ADAM_B2 = 0.999
ADAM_EPS = 1e-08
ADAM_WD = 0.01
ADAM_STEP = 10

IN_SIZES = (512, 128, 128, 512, 512, 512, 1024, 1024)
IN_OFFS = tuple(int(v) for v in np.cumsum((0,) + IN_SIZES))
IN_W = IN_OFFS[-1]

BIG_NAMES = ("ffn1_w1", "ffn1_w3", "ffn1_w2", "w_in", "w_branch_swa", "w_branch_sb", "w_out",
             "ffn2_w1", "ffn2_w3", "ffn2_w2")
BIG_ROWS = (352, 352, 352, 544, 64, 64, 128, 352, 352, 352)
SMALL_ROWS = 8
GROUPS = (BIG_NAMES[0:3], BIG_NAMES[3:7], BIG_NAMES[7:10])
SUM_GROUPS = tuple((n,) for n in BIG_NAMES)
SUM_TILE = (176, 176, 176, 272, 64, 64, 128, 176, 176, 176)

VMEM_LIMIT = 56 * 1024 * 1024
FFN_PIECES = 2
SB_QUERIES = 512
SB_KEYS = 256
SB_ROWS = 256
SB_SLOTS = 3
SB_SUM_PARTS = 1
SB_LOGIT_CAP = 80.0
SB_DEAD_CARRY = -110.0


def _dot(a, b):
    return jnp.dot(a, b, preferred_element_type=F32)


def _dot_nt(a, b):
    return lax.dot_general(a, b, (((1,), (1,)), ((), ())), preferred_element_type=F32)


def _dot_tn(a, b):
    return lax.dot_general(a, b, (((0,), (0,)), ((), ())), preferred_element_type=F32)


def _cparams(*sem):
    return pltpu.CompilerParams(dimension_semantics=sem, vmem_limit_bytes=VMEM_LIMIT)


def _rms_rstd(xv):
    return lax.rsqrt(jnp.mean(xv * xv, axis=-1, keepdims=True) + RMS_EPS)


def _rms_bwd(dh, xv, r, g):
    xhat = xv * r
    dg = jnp.sum(dh * xhat, axis=0, keepdims=True)
    dxn = dh * g
    dx = r * (dxn - xhat * jnp.mean(dxn * xhat, axis=-1, keepdims=True))
    return dx, dg


def _ff_tile_spec(tm, tf):
    return pl.BlockSpec((1, tm, tf), lambda i, j: (j, i, 0))


def _ffn_fwd(x, g, w1t, w3t, w2, tag, comm=None, loss=None):
    s_len = x.shape[0]
    tm, tf = min(1024, s_len), 256
    nf = D_FF // tf

    def body(*refs):
        if loss is None:
            x_ref, g_ref, w1_ref, w3_ref, w2_ref, xo_ref, h_ref, a_ref, b_ref, u_ref, acc_ref, hs_ref = refs
        else:
            (x_ref, g_ref, w1_ref, w3_ref, w2_ref, t_ref, gf_ref,
             xo_ref, h_ref, a_ref, b_ref, u_ref, loss_ref, dgf_ref, acc_ref, hs_ref) = refs
        j = pl.program_id(1)

        @pl.when(j == 0)
        def _():
            xv = x_ref[...]
            h = (xv * _rms_rstd(xv) * g_ref[...]).astype(BF16)
            hs_ref[...] = h
            h_ref[...] = h
            acc_ref[...] = jnp.zeros_like(acc_ref)

        st = {}

        def s_up(rs):
            h = hs_ref[rs, :]
            st[rs.start, "ab"] = (_dot_nt(h, w1_ref[...]), _dot_nt(h, w3_ref[...]))

        def s_act(rs):
            a, b = st.pop((rs.start, "ab"))
            a_ref[0, rs, :] = a.astype(BF16)
            b_ref[0, rs, :] = b.astype(BF16)
            uh = (0.5 * (a * jax.nn.sigmoid(a) * b)).astype(BF16)
            u_ref[0, rs, :] = uh
            st[rs.start, "u"] = uh

        def s_down(rs):
            acc_ref[rs, :] += _dot(st.pop((rs.start, "u")), w2_ref[...])

        _emit_skewed(([slice(r, r + tm // FFN_PIECES) for r in range(0, tm, tm // FFN_PIECES)], [s_up, s_act, s_down]))

        if loss is not None:
            @pl.when((pl.program_id(0) == 0) & (j == 0))
            def _():
                loss_ref[...] = jnp.zeros_like(loss_ref)
                dgf_ref[...] = jnp.zeros_like(dgf_ref)

        @pl.when(j == nf - 1)
        def _():
            xo = x_ref[...] + acc_ref[...]
            if loss is None:
                xo_ref[...] = xo
            else:
                gv = gf_ref[...]
                r = _rms_rstd(xo)
                err = xo * r * gv - t_ref[...]
                loss_ref[...] += 0.5 * jnp.sum(jnp.mean(err * err, axis=-1, keepdims=True), axis=0, keepdims=True)
                dx, dg = _rms_bwd(err * (1.0 / D_MODEL), xo, r, gv)
                xo_ref[...] = dx
                dgf_ref[...] += dg

    row = lambda i, j: (i, 0)
    fixed = lambda i, j: (0, 0)
    with_loss = loss is not None
    return _call(
        body, (x, g, w1t, w3t, w2) + (tuple(loss) if with_loss else ()), comm=comm, **_grid_ends(s_len // tm, nf),
        name=f"ffn_fwd_{tag}",
        grid=(s_len // tm, nf),
        in_specs=[pl.BlockSpec((tm, D_MODEL), row), pl.BlockSpec((1, D_MODEL), fixed),
                  pl.BlockSpec((tf, D_MODEL), lambda i, j: (j, 0)), pl.BlockSpec((tf, D_MODEL), lambda i, j: (j, 0)),
                  pl.BlockSpec((tf, D_MODEL), lambda i, j: (j, 0))]
        + ([pl.BlockSpec((tm, D_MODEL), row), pl.BlockSpec((1, D_MODEL), fixed)] if with_loss else []),
        out_specs=[pl.BlockSpec((tm, D_MODEL), row), pl.BlockSpec((tm, D_MODEL), row)] + [_ff_tile_spec(tm, tf)] * 3
        + ([pl.BlockSpec((1, 1), fixed), pl.BlockSpec((1, D_MODEL), fixed)] if with_loss else []),
        out_shape=[jax.ShapeDtypeStruct((s_len, D_MODEL), F32), jax.ShapeDtypeStruct((s_len, D_MODEL), BF16)]
        + [jax.ShapeDtypeStruct((nf, s_len, tf), BF16)] * 3
        + ([jax.ShapeDtypeStruct((1, 1), F32), jax.ShapeDtypeStruct((1, D_MODEL), F32)] if with_loss else []),
        scratch_shapes=[pltpu.VMEM((tm, D_MODEL), F32), pltpu.VMEM((tm, D_MODEL), BF16)],
        compiler_params=_cparams("arbitrary", "arbitrary"),
    )


def _ffn_bwd(dy, x, g, a, b, w1t, w3t, w2, tag, comm=None):
    s_len = x.shape[0]
    tm, tf = min(1024, s_len), 256
    nf = D_FF // tf

    def body(dy_ref, x_ref, g_ref, a_ref, b_ref, w1_ref, w3_ref, w2_ref,
             dx_ref, dg_ref, da_ref, db_ref, dyb_ref, acc_ref, dys_ref):
        i, j = pl.program_id(0), pl.program_id(1)

        @pl.when(j == 0)
        def _():
            dyb = dy_ref[...].astype(BF16)
            dys_ref[...] = 0.5 * dyb
            dyb_ref[...] = dyb
            acc_ref[...] = jnp.zeros_like(acc_ref)

        @pl.when((i == 0) & (j == 0))
        def _():
            dg_ref[...] = jnp.zeros_like(dg_ref)

        st = {}

        def s_du(rs):
            st[rs.start, "du"] = _dot_nt(dys_ref[rs, :], w2_ref[...])

        def s_act(rs):
            du = st.pop((rs.start, "du"))
            av = a_ref[0, rs, :].astype(F32)
            bv = b_ref[0, rs, :].astype(F32)
            sg = jax.nn.sigmoid(av)
            sil = av * sg
            da = (du * bv * (sg + sil * (1.0 - sg))).astype(BF16)
            db = (du * sil).astype(BF16)
            da_ref[0, rs, :] = da
            db_ref[0, rs, :] = db
            st[rs.start, "dab"] = (da, db)

        def s_dh(rs):
            da, db = st.pop((rs.start, "dab"))
            acc_ref[rs, :] += _dot(da, w1_ref[...]) + _dot(db, w3_ref[...])

        _emit_skewed(([slice(r, r + tm // FFN_PIECES) for r in range(0, tm, tm // FFN_PIECES)], [s_du, s_act, s_dh]))

        @pl.when(j == nf - 1)
        def _():
            xv = x_ref[...]
            dx, dg = _rms_bwd(acc_ref[...], xv, _rms_rstd(xv), g_ref[...])
            dx_ref[...] = dy_ref[...] + dx
            dg_ref[...] += dg

    row = lambda i, j: (i, 0)
    wsp = pl.BlockSpec((tf, D_MODEL), lambda i, j: (j, 0))
    return _call(
        body, (dy, x, g, a, b, w1t, w3t, w2), comm=comm, **_grid_ends(s_len // tm, nf), name=f"ffn_bwd_{tag}",
        grid=(s_len // tm, nf),
        in_specs=[pl.BlockSpec((tm, D_MODEL), row), pl.BlockSpec((tm, D_MODEL), row),
                  pl.BlockSpec((1, D_MODEL), lambda i, j: (0, 0)),
                  _ff_tile_spec(tm, tf), _ff_tile_spec(tm, tf), wsp, wsp, wsp],
        out_specs=[pl.BlockSpec((tm, D_MODEL), row), pl.BlockSpec((1, D_MODEL), lambda i, j: (0, 0)),
                   _ff_tile_spec(tm, tf), _ff_tile_spec(tm, tf), pl.BlockSpec((tm, D_MODEL), row)],
        out_shape=[jax.ShapeDtypeStruct((s_len, D_MODEL), F32), jax.ShapeDtypeStruct((1, D_MODEL), F32),
                   jax.ShapeDtypeStruct((nf, s_len, tf), BF16), jax.ShapeDtypeStruct((nf, s_len, tf), BF16),
                   jax.ShapeDtypeStruct((s_len, D_MODEL), BF16)],
        scratch_shapes=[pltpu.VMEM((tm, D_MODEL), F32), pltpu.VMEM((tm, D_MODEL), BF16)],
        compiler_params=_cparams("arbitrary", "arbitrary"),
    )


def _matmul_tn(lhs, rhs, tag, comm=None):
    s_len, m = lhs.shape
    n = rhs.shape[1]
    tm = min(512, s_len)
    tj = m if m <= 1024 else 1408
    assert m % tj == 0
    last_rows = s_len // tm - 1

    def body(l_ref, r_ref, o_ref, acc_ref):
        i = pl.program_id(1)

        @pl.when(i == 0)
        def _():
            acc_ref[...] = jnp.zeros_like(acc_ref)

        acc_ref[...] += _dot_tn(l_ref[...], r_ref[...])

        @pl.when(i == last_rows)
        def _():
            o_ref[...] = acc_ref[...].astype(BF16)

    res = _call(
        body, (lhs, rhs), comm=comm, **_grid_ends(m // tj, s_len // tm), name=f"matmul_tn_{tag}",
        grid=(m // tj, s_len // tm),
        in_specs=[pl.BlockSpec((tm, tj), lambda j, i: (i, j)), pl.BlockSpec((tm, n), lambda j, i: (i, 0))],
        out_specs=[pl.BlockSpec((tj, n), lambda j, i: (j, 0))],
        out_shape=[jax.ShapeDtypeStruct((m, n), BF16)],
        scratch_shapes=[pltpu.VMEM((tj, n), F32)],
        compiler_params=_cparams("arbitrary", "arbitrary"),
    )
    return res[0] if comm is None else tuple(res)


def _matmul_tn_tiled(lhs, rhs, tag, comm=None):
    nf, s_len, tf = lhs.shape
    n = rhs.shape[1]
    tm = min(512, s_len)
    last_rows = s_len // tm - 1

    def body(l_ref, r_ref, o_ref, acc_ref):
        i = pl.program_id(0)

        @pl.when(i == 0)
        def _():
            acc_ref[...] = jnp.zeros_like(acc_ref)

        rv = r_ref[...]
        for t in range(nf):
            acc_ref[t * tf:(t + 1) * tf, :] += _dot_tn(l_ref[t], rv)

        @pl.when(i == last_rows)
        def _():
            o_ref[...] = acc_ref[...].astype(BF16)

    res = _call(
        body, (lhs, rhs), comm=comm, **_grid_ends(s_len // tm), name=f"matmul_tn_{tag}",
        grid=(s_len // tm,),
        in_specs=[pl.BlockSpec((nf, tm, tf), lambda i: (0, i, 0)), pl.BlockSpec((tm, n), lambda i: (i, 0))],
        out_specs=[pl.BlockSpec((nf * tf, n), lambda i: (0, 0))],
        out_shape=[jax.ShapeDtypeStruct((nf * tf, n), BF16)],
        scratch_shapes=[pltpu.VMEM((nf * tf, n), F32)],
        compiler_params=_cparams("arbitrary"),
    )
    return res[0] if comm is None else tuple(res)


def _matmul_tn_stacked(pieces, rhs, tag):
    s_len, n = rhs.shape
    widths = [p.shape[1] for p in pieces]
    offs = [sum(widths[:k]) for k in range(len(widths) + 1)]
    tm = min(256, s_len)
    last_rows = s_len // tm - 1

    def body(*refs):
        l_refs, r_ref, o_ref, acc_ref = refs[:len(pieces)], refs[-3], refs[-2], refs[-1]
        i = pl.program_id(0)

        @pl.when(i == 0)
        def _():
            acc_ref[...] = jnp.zeros_like(acc_ref)

        rv = r_ref[...]
        for k, l_ref in enumerate(l_refs):
            acc_ref[offs[k]:offs[k + 1], :] += _dot_tn(l_ref[...], rv)

        @pl.when(i == last_rows)
        def _():
            o_ref[...] = acc_ref[...].astype(BF16)

    row = lambda i: (i, 0)
    return pl.pallas_call(
        body, name=f"matmul_tn_{tag}",
        grid=(s_len // tm,),
        in_specs=[pl.BlockSpec((tm, w), row) for w in widths] + [pl.BlockSpec((tm, n), row)],
        out_specs=pl.BlockSpec((offs[-1], n), lambda i: (0, 0)),
        out_shape=jax.ShapeDtypeStruct((offs[-1], n), BF16),
        scratch_shapes=[pltpu.VMEM((offs[-1], n), F32)],
        compiler_params=_cparams("arbitrary"),
    )(*pieces, rhs)


def _proj_fwd(x1, g, wint):
    s_len = x1.shape[0]
    tm = min(512, s_len)
    dts = (BF16, BF16, BF16, BF16, BF16, BF16, F32, F32)

    def body(x_ref, g_ref, w_ref, h_ref, *outs):
        xv = x_ref[...]
        h = (xv * _rms_rstd(xv) * g_ref[...]).astype(BF16)
        h_ref[...] = h
        for p, o_ref in enumerate(outs):
            val = _dot_nt(h, w_ref[IN_OFFS[p]:IN_OFFS[p + 1], :])
            if p == 3:
                val = val * Q_SCALE
            o_ref[...] = val.astype(dts[p])

    row = lambda i: (i, 0)
    return pl.pallas_call(
        body, name="proj_fwd",
        grid=(s_len // tm,),
        in_specs=[pl.BlockSpec((tm, D_MODEL), row), pl.BlockSpec((1, D_MODEL), lambda i: (0, 0)),
                  pl.BlockSpec((IN_W, D_MODEL), lambda i: (0, 0))],
        out_specs=[pl.BlockSpec((tm, D_MODEL), row)] + [pl.BlockSpec((tm, w), row) for w in IN_SIZES],
        out_shape=[jax.ShapeDtypeStruct((s_len, D_MODEL), BF16)]
        + [jax.ShapeDtypeStruct((s_len, w), dt) for w, dt in zip(IN_SIZES, dts)],
        compiler_params=_cparams("parallel"),
    )(x1, g, wint)


def _proj_bwd(dpieces, dx2, x1, g, wint, comm=None):
    s_len = x1.shape[0]
    tm = min(512, s_len)

    def body(*refs):
        dps = refs[:8]
        dx2_ref, x_ref, g_ref, w_ref, dx_ref, dg_ref = refs[8:]

        @pl.when(pl.program_id(0) == 0)
        def _():
            dg_ref[...] = jnp.zeros_like(dg_ref)

        dh = _dot(dps[0][...], w_ref[IN_OFFS[0]:IN_OFFS[1], :])
        for p in range(1, 8):
            dh += _dot(dps[p][...], w_ref[IN_OFFS[p]:IN_OFFS[p + 1], :])
        xv = x_ref[...]
        dx, dg = _rms_bwd(dh, xv, _rms_rstd(xv), g_ref[...])
        dx_ref[...] = dx2_ref[...] + dx
        dg_ref[...] += dg

    row = lambda i: (i, 0)
    return _call(
        body, (*dpieces, dx2, x1, g, wint), comm=comm, **_grid_ends(s_len // tm), name="proj_bwd",
        grid=(s_len // tm,),
        in_specs=[pl.BlockSpec((tm, w), row) for w in IN_SIZES]
        + [pl.BlockSpec((tm, D_MODEL), row), pl.BlockSpec((tm, D_MODEL), row),
           pl.BlockSpec((1, D_MODEL), lambda i: (0, 0)), pl.BlockSpec((IN_W, D_MODEL), lambda i: (0, 0))],
        out_specs=[pl.BlockSpec((tm, D_MODEL), row), pl.BlockSpec((1, D_MODEL), lambda i: (0, 0))],
        out_shape=[jax.ShapeDtypeStruct((s_len, D_MODEL), F32), jax.ShapeDtypeStruct((1, D_MODEL), F32)],
        compiler_params=_cparams("arbitrary"),
    )


def _merge_fwd(x1, oa, ob, ga, gb, wswa, wsb, wout):
    s_len = x1.shape[0]
    tm = min(512, s_len)

    def body(x_ref, oa_ref, ob_ref, ga_ref, gb_ref, wa_ref, wb_ref, wo_ref, xo_ref, mg_ref):
        pa = _dot(oa_ref[...], wa_ref[...])
        pb = _dot(ob_ref[...], wb_ref[...])
        mg = (jax.nn.sigmoid(ga_ref[...]) * pa + jax.nn.sigmoid(gb_ref[...]) * pb).astype(BF16)
        mg_ref[...] = mg
        xo_ref[...] = x_ref[...] + _dot(mg, wo_ref[...])

    row = lambda i: (i, 0)
    full = lambda i: (0, 0)
    return pl.pallas_call(
        body, name="merge_fwd",
        grid=(s_len // tm,),
        in_specs=[pl.BlockSpec((tm, D_MODEL), row), pl.BlockSpec((tm, 512), row), pl.BlockSpec((tm, 512), row),
                  pl.BlockSpec((tm, D_MODEL), row), pl.BlockSpec((tm, D_MODEL), row),
                  pl.BlockSpec((512, D_MODEL), full), pl.BlockSpec((512, D_MODEL), full),
                  pl.BlockSpec((D_MODEL, D_MODEL), full)],
        out_specs=[pl.BlockSpec((tm, D_MODEL), row), pl.BlockSpec((tm, D_MODEL), row)],
        out_shape=[jax.ShapeDtypeStruct((s_len, D_MODEL), F32), jax.ShapeDtypeStruct((s_len, D_MODEL), BF16)],
        compiler_params=_cparams("parallel"),
    )(x1, oa, ob, ga, gb, wswa, wsb, wout)


def _merge_bwd(dx2, oa, ob, ga, gb, wswa, wsb, wout, comm=None):
    s_len = dx2.shape[0]
    tm = min(512, s_len)

    def body(dx_ref, oa_ref, ob_ref, ga_ref, gb_ref, wa_ref, wb_ref, wo_ref,
             doa_ref, dob_ref, dga_ref, dgb_ref, dpa_ref, dpb_ref, dxb_ref):
        dxb = dx_ref[...].astype(BF16)
        dxb_ref[...] = dxb
        dmg = _dot_nt(dxb, wo_ref[...])
        for o_ref, g_ref, w_ref, do_ref, dg_ref, dp_ref in (
                (oa_ref, ga_ref, wa_ref, doa_ref, dga_ref, dpa_ref),
                (ob_ref, gb_ref, wb_ref, dob_ref, dgb_ref, dpb_ref)):
            pv = _dot(o_ref[...], w_ref[...])
            sg = jax.nn.sigmoid(g_ref[...])
            dp = (dmg * sg).astype(BF16)
            dp_ref[...] = dp
            dg_ref[...] = (dmg * pv * sg * (1.0 - sg)).astype(BF16)
            do_ref[...] = _dot_nt(dp, w_ref[...]).astype(BF16)

    row = lambda i: (i, 0)
    full = lambda i: (0, 0)
    wide = pl.BlockSpec((tm, D_MODEL), row)
    half = pl.BlockSpec((tm, 512), row)
    return _call(
        body, (dx2, oa, ob, ga, gb, wswa, wsb, wout), comm=comm, **_grid_ends(s_len // tm), name="merge_bwd",
        grid=(s_len // tm,),
        in_specs=[wide, half, half, wide, wide, pl.BlockSpec((512, D_MODEL), full),
                  pl.BlockSpec((512, D_MODEL), full), pl.BlockSpec((D_MODEL, D_MODEL), full)],
        out_specs=[half, half, wide, wide, wide, wide, wide],
        out_shape=[jax.ShapeDtypeStruct((s_len, 512), BF16)] * 2 + [jax.ShapeDtypeStruct((s_len, D_MODEL), BF16)] * 5,
        compiler_params=_cparams("arbitrary"),
    )


def _rel_bucket_matrix():
    qi = jnp.arange(SWA_BLOCK)[:, None] + SWA_BLOCK
    kj = jnp.arange(2 * SWA_BLOCK)[None, :]
    dist = jnp.maximum(qi - kj, 0)
    max_exact = REL_BUCKETS // 2
    d = jnp.maximum(dist, 1).astype(F32)
    large = max_exact + (jnp.log(d / max_exact) / np.log(REL_MAX_DIST / max_exact)
                         * (REL_BUCKETS - max_exact)).astype(jnp.int32)
    large = jnp.minimum(large, REL_BUCKETS - 1)
    return jnp.where(dist < max_exact, dist, large).astype(jnp.int32)


def _swa_bias_into(bias_ref, bkt_ref, tab_ref):
    bk = bkt_ref[...]
    for h in range(N_HEADS):
        acc = jnp.zeros(bk.shape, F32)
        for bucket in range(REL_BUCKETS):
            acc = jnp.where(bk == bucket, tab_ref[bucket, h], acc)
        bias_ref[h] = acc


def _swa_valid(n):
    shape = (SWA_BLOCK, 2 * SWA_BLOCK)
    row = lax.broadcasted_iota(jnp.int32, shape, 0)
    col = lax.broadcasted_iota(jnp.int32, shape, 1)
    dist = row + SWA_BLOCK - col
    return (dist >= 0) & (dist < SWA_BLOCK) & ((col >= SWA_BLOCK) | (n > 0))


def _swa_windows(kp_ref, kc_ref, vp_ref, vc_ref):
    return (jnp.concatenate([kp_ref[...], kc_ref[...]], axis=0), jnp.concatenate([vp_ref[...], vc_ref[...]], axis=0))


def _swa_place(h):
    return slice(h // 2 * LANES, (h // 2 + 1) * LANES), h % 2, h // SWA_GROUP


def _move_half(x, src, dst):
    moved = x if src == dst else pltpu.roll(x, HEAD_DIM, 1)
    in_dst = (lax.broadcasted_iota(jnp.int32, x.shape, 1) >= HEAD_DIM) == bool(dst)
    return jnp.where(in_dst, moved, 0.0)


def _swa_probs(qk, bias, sink, valid):
    lg = jnp.where(valid, qk * Q_SCALE + bias, NEG_BIG)
    m = jnp.maximum(jnp.max(lg, axis=-1, keepdims=True), sink)
    e = jnp.exp(lg - m)
    es = jnp.exp(sink - m)
    inv = 1.0 / (jnp.sum(e, axis=-1, keepdims=True) + es)
    return e * inv, es * inv


def _swa_specs(s_len):
    blk = SWA_BLOCK
    cur = lambda n: (n, 0)
    prev = lambda n: (jnp.maximum(n - 1, 0), 0)
    kvw = SWA_KV_HEADS * HEAD_DIM
    return [pl.BlockSpec(memory_space=pltpu.SMEM), pl.BlockSpec(memory_space=pltpu.SMEM),
            pl.BlockSpec((blk, 2 * blk), lambda n: (0, 0)),
            pl.BlockSpec((blk, N_HEADS * HEAD_DIM), cur),
            pl.BlockSpec((blk, kvw), prev), pl.BlockSpec((blk, kvw), cur),
            pl.BlockSpec((blk, kvw), prev), pl.BlockSpec((blk, kvw), cur)]


def _swa_fwd(tab, sinks, bkt, q, k, v):
    s_len = q.shape[0]
    blk = SWA_BLOCK

    def body(tab_ref, sink_ref, bkt_ref, q_ref, kp_ref, kc_ref, vp_ref, vc_ref, o_ref, bias_ref):
        n = pl.program_id(0)

        @pl.when(n == 0)
        def _():
            _swa_bias_into(bias_ref, bkt_ref, tab_ref)

        valid = _swa_valid(n)
        kk, vv = _swa_windows(kp_ref, kc_ref, vp_ref, vc_ref)
        st = {}

        def s_logits(h):
            tile, mine, kv = _swa_place(h)
            st[h, "lg"] = _dot_nt(_move_half(q_ref[:, tile].astype(F32), mine, kv).astype(BF16), kk)

        def s_probs(h):
            st[h, "p"] = _swa_probs(st.pop((h, "lg")), bias_ref[h], sink_ref[0, h], valid)[0].astype(BF16)

        def s_values(h):
            tile, mine, kv = _swa_place(h)
            part = _move_half(_dot(st.pop((h, "p")), vv), kv, mine)
            if mine == 0:
                st[h + 1, "o"] = part
            else:
                o_ref[:, tile] = (st.pop((h, "o")) + part).astype(BF16)

        _emit_skewed((list(range(N_HEADS)), [s_logits, s_probs, s_values]))

    return pl.pallas_call(
        body, name="swa_fwd",
        grid=(s_len // blk,),
        in_specs=_swa_specs(s_len),
        out_specs=pl.BlockSpec((blk, N_HEADS * HEAD_DIM), lambda n: (n, 0)),
        out_shape=jax.ShapeDtypeStruct((s_len, N_HEADS * HEAD_DIM), BF16),
        scratch_shapes=[pltpu.VMEM((N_HEADS, blk, 2 * blk), F32)],
        compiler_params=_cparams("arbitrary"),
    )(tab, sinks, bkt, q, k, k, v, v)


def _swa_bwd(tab, sinks, bkt, q, k, v, do, comm=None):
    s_len = q.shape[0]
    blk = SWA_BLOCK
    nb = s_len // blk
    kvw = SWA_KV_HEADS * HEAD_DIM

    def body(tab_ref, sink_ref, bkt_ref, q_ref, kp_ref, kc_ref, vp_ref, vc_ref, do_ref,
             dq_ref, dk_ref, dv_ref, dtab_ref, dsink_ref, bias_ref, dbias_ref):
        n = pl.program_id(0)

        @pl.when(n == 0)
        def _():
            _swa_bias_into(bias_ref, bkt_ref, tab_ref)
            dbias_ref[...] = jnp.zeros_like(dbias_ref)
            dk_ref[...] = jnp.zeros_like(dk_ref)
            dv_ref[...] = jnp.zeros_like(dv_ref)
            dsink_ref[...] = jnp.zeros_like(dsink_ref)
            dtab_ref[...] = jnp.zeros_like(dtab_ref)

        valid = _swa_valid(n)
        cur_rows = pl.ds(pl.multiple_of(n * blk, blk), blk)
        prev_rows = pl.ds(pl.multiple_of(jnp.maximum(n - 1, 0) * blk, blk), blk)
        kk, vv = _swa_windows(kp_ref, kc_ref, vp_ref, vc_ref)
        st = {}

        def s_logits(h):
            tile, mine, kv = _swa_place(h)
            st[h, "q"] = _move_half(q_ref[:, tile].astype(F32), mine, kv).astype(BF16)
            st[h, "do"] = _move_half(do_ref[:, tile].astype(F32), mine, kv).astype(BF16)
            st[h, "lg"] = _dot_nt(st[h, "q"], kk)
            st[h, "dp"] = _dot_nt(st[h, "do"], vv)

        def s_probs(h):
            p, ps = _swa_probs(st.pop((h, "lg")), bias_ref[h], sink_ref[0, h], valid)
            dp = st.pop((h, "dp"))
            delta = jnp.sum(p * dp, axis=-1, keepdims=True)
            dl = p * (dp - delta)
            dsink_ref[h:h + 1, :] += jnp.broadcast_to(-jnp.sum(ps * delta, axis=0, keepdims=True), (1, LANES))
            dbias_ref[h] += dl
            st[h, "dl"], st[h, "p"] = dl.astype(BF16), p.astype(BF16)

        def s_products(h):
            tile, mine, kv = _swa_place(h)
            dlb = st.pop((h, "dl"))
            part = _move_half(Q_SCALE * _dot(dlb, kk), kv, mine)
            if mine == 0:
                st[h + 1, "dq"] = part
            else:
                dq_ref[:, tile] = (st.pop((h, "dq")) + part).astype(BF16)
            dk_win = Q_SCALE * _dot_tn(dlb, st.pop((h, "q")))
            dv_win = _dot_tn(st.pop((h, "p")), st.pop((h, "do")))
            dk_ref[prev_rows, :] += dk_win[:blk]
            dv_ref[prev_rows, :] += dv_win[:blk]
            dk_ref[cur_rows, :] += dk_win[blk:]
            dv_ref[cur_rows, :] += dv_win[blk:]

        _emit_skewed((list(range(N_HEADS)), [s_logits, s_probs, s_products]))

        @pl.when(n == nb - 1)
        def _():
            bk = bkt_ref[...]
            lane = lax.broadcasted_iota(jnp.int32, (1, LANES), 1)
            for bucket in range(REL_BUCKETS):
                rowv = jnp.zeros((1, LANES), F32)
                for h in range(N_HEADS):
                    val = jnp.sum(jnp.where(bk == bucket, dbias_ref[h], 0.0), axis=1, keepdims=True)
                    val = jnp.sum(val, axis=0, keepdims=True)
                    rowv = jnp.where(lane == h, val, rowv)
                dtab_ref[bucket:bucket + 1, :] = rowv

    return _call(
        body, (tab, sinks, bkt, q, k, k, v, v, do), comm=comm, **_grid_ends(nb), name="swa_bwd",
        grid=(nb,),
        in_specs=_swa_specs(s_len) + [pl.BlockSpec((blk, N_HEADS * HEAD_DIM), lambda n: (n, 0))],
        out_specs=[pl.BlockSpec((blk, N_HEADS * HEAD_DIM), lambda n: (n, 0)),
                   pl.BlockSpec((s_len, kvw), lambda n: (0, 0)), pl.BlockSpec((s_len, kvw), lambda n: (0, 0)),
                   pl.BlockSpec((REL_BUCKETS, LANES), lambda n: (0, 0)), pl.BlockSpec((N_HEADS, LANES), lambda n: (0, 0))],
        out_shape=[jax.ShapeDtypeStruct((s_len, N_HEADS * HEAD_DIM), BF16),
                   jax.ShapeDtypeStruct((s_len, kvw), F32), jax.ShapeDtypeStruct((s_len, kvw), F32),
                   jax.ShapeDtypeStruct((REL_BUCKETS, LANES), F32), jax.ShapeDtypeStruct((N_HEADS, LANES), F32)],
        scratch_shapes=[pltpu.VMEM((N_HEADS, blk, 2 * blk), F32), pltpu.VMEM((N_HEADS, blk, 2 * blk), F32)],
        compiler_params=_cparams("arbitrary"),
    )


def _sb_terms(z, valid):
    zc = jnp.minimum(z, SB_LOGIT_CAP)
    lk = -jnp.log(1.0 + jnp.exp(zc))
    lsz = zc + lk
    return lsz, (lk if valid is None else jnp.where(valid, lk, 0.0))


def _bf16_parts(vals):
    parts, rest = [], vals
    for n in range(SB_SUM_PARTS):
        parts.append(rest.astype(BF16))
        if n + 1 < SB_SUM_PARTS:
            rest = rest - parts[-1].astype(F32)
    return parts[0] if len(parts) == 1 else jnp.concatenate(parts, axis=1)


def _row_sum_lanes(vals):
    return jnp.broadcast_to(jnp.sum(vals, axis=-1, keepdims=True), (vals.shape[0], LANES))


def _emit_skewed(*groups):
    for step in range(max(len(items) + len(stages) - 1 for items, stages in groups)):
        for items, stages in groups:
            for s, stage in enumerate(stages):
                if 0 <= step - s < len(items) and items[step - s] is not None:
                    stage(items[step - s])


def _sb_items(edge):
    items = []
    for h in range(2):
        for r0 in range(0, SB_QUERIES, SB_ROWS):
            if edge is None or r0 >= (edge + 1) * SB_KEYS:
                items.append((h, r0, False))
            else:
                items.append((h, r0, True) if r0 + SB_ROWS - 1 > edge * SB_KEYS else None)
    return items


def _sb_valid(w, edge):
    row = lax.broadcasted_iota(jnp.int32, (SB_ROWS, SB_KEYS), 0) + w[1]
    col = lax.broadcasted_iota(jnp.int32, (SB_ROWS, SB_KEYS), 1) + edge * SB_KEYS
    return col < row


def _sb_consts(tq, tk):
    low = lax.broadcasted_iota(jnp.int32, (tq, LANES), 1) < HEAD_DIM
    row = lax.broadcasted_iota(jnp.int32, (tk, tk), 0)
    col = lax.broadcasted_iota(jnp.int32, (tk, tk), 1)
    right = (row > col).astype(BF16)
    left = (row < col).astype(BF16)
    return low, jnp.concatenate([right] * SB_SUM_PARTS, axis=0), jnp.concatenate([left] * SB_SUM_PARTS, axis=0)


def _sb_fwd(q, k, v, comm=None):
    s_len = q.shape[0]
    tq, tk, tr = SB_QUERIES, SB_KEYS, SB_ROWS
    nk, ratio = s_len // tk, tq // tk
    assert nk <= LANES

    def body(q_ref, k_ref, v_ref, o_ref, car_ref, c_ref, oacc_ref, logw_ref, lksum_ref):
        i = pl.program_id(1)
        qv = q_ref[...]
        low, tri2, _ = _sb_consts(tq, tk)
        lane = lax.broadcasted_iota(jnp.int32, (tr, LANES), 1)
        zero = jnp.zeros_like(qv)
        q_heads = (jnp.where(low, qv, zero), jnp.where(low, zero, qv))
        c_ref[...] = jnp.zeros_like(c_ref)
        oacc_ref[...] = jnp.zeros_like(oacc_ref)
        car_ref[...] = jnp.full_like(car_ref, NEG_BIG)

        def front(j, edge):
            keys = k_ref[pl.ds(pl.multiple_of(j * tk, tk), tk), :]
            slot = j % SB_SLOTS
            st = {}

            def s_logits(w):
                st[w, "z"] = _dot_nt(q_heads[w[0]][w[1]:w[1] + tr], keys)

            def s_terms(w):
                valid = _sb_valid(w, edge) if w[2] else None
                lsz, lk = _sb_terms(st.pop((w, "z")), valid)
                st[w, "parts"] = _bf16_parts(lk)
                st[w, "lsz"] = lsz if valid is None else jnp.where(valid, lsz, NEG_BIG)
                lksum_ref[slot, w[0], w[1]:w[1] + tr, :] = _row_sum_lanes(lk)

            def s_suffix(w):
                logw_ref[slot, w[0], w[1]:w[1] + tr, :] = st.pop((w, "lsz")) + _dot(st.pop((w, "parts")), tri2)

            return _sb_items(edge), [s_logits, s_terms, s_suffix]

        def back(j, edge):
            vv = v_ref[pl.ds(pl.multiple_of(j * tk, tk), tk), :]
            slot = j % SB_SLOTS
            st = {}

            def s_weights(w):
                h, rs = w[0], slice(w[1], w[1] + tr)
                c = c_ref[h, rs, :]
                st[w, "a"] = jnp.exp(logw_ref[slot, h, rs, :] + jnp.tile(c, (1, tk // LANES))).astype(BF16)
                car_ref[h, rs, :] = jnp.where(lane == j, c, car_ref[h, rs, :])
                c_ref[h, rs, :] = c + lksum_ref[slot, h, rs, :]

            def s_values(w):
                oacc_ref[w[0], w[1]:w[1] + tr, :] += _dot(st.pop((w, "a")), vv)

            return _sb_items(edge), [s_weights, s_values]

        first = i * ratio
        edge_tiles = [(first + m, m) for m in reversed(range(ratio))]

        def alive():
            return (jnp.max(c_ref[...]) >= SB_DEAD_CARRY).astype(jnp.int32)

        @pl.when(i == 0)
        def _():
            _emit_skewed(*[front(j, m) for j, m in edge_tiles])
            _emit_skewed(*[back(j, m) for j, m in edge_tiles])

        @pl.when(i > 0)
        def _():
            tiles = edge_tiles + [(first - 1, None)]
            _emit_skewed(*[front(j, m) for j, m in tiles])
            _emit_skewed(*[back(j, m) for j, m in tiles])

            @pl.when((alive() > 0) & (first >= 2))
            def _():
                _emit_skewed(front(first - 2, None))

                def step(state):
                    pending, _ = state
                    _emit_skewed(front(pending - 1, None), back(pending, None))
                    return pending - 1, alive()

                pending, live = lax.while_loop(lambda s: (s[0] > 0) & (s[1] > 0), step, (first - 2, jnp.int32(1)))

                @pl.when(live > 0)
                def _():
                    _emit_skewed(back(pending, None))

        o_ref[...] = jnp.where(low, oacc_ref[0], oacc_ref[1]).astype(BF16)

    return _call(
        body, (q, k, v), comm=comm, **_grid_ends(N_HEADS // 2, s_len // tq), name="sb_fwd",
        grid=(N_HEADS // 2, s_len // tq),
        in_specs=[pl.BlockSpec((tq, LANES), lambda p, i: (i, p)),
                  pl.BlockSpec((s_len, LANES), lambda p, i: (0, p)),
                  pl.BlockSpec((s_len, LANES), lambda p, i: (0, p))],
        out_specs=[pl.BlockSpec((tq, LANES), lambda p, i: (i, p)), pl.BlockSpec((2, tq, LANES), lambda p, i: (p, i, 0))],
        out_shape=[jax.ShapeDtypeStruct((s_len, N_HEADS * HEAD_DIM), BF16),
                   jax.ShapeDtypeStruct((N_HEADS, s_len, LANES), F32)],
        scratch_shapes=[pltpu.VMEM((2, tq, LANES), F32), pltpu.VMEM((2, tq, LANES), F32),
                        pltpu.VMEM((SB_SLOTS, 2, tq, tk), F32), pltpu.VMEM((SB_SLOTS, 2, tq, LANES), F32)],
        compiler_params=_cparams("arbitrary", "arbitrary"),
    )


def _sb_bwd(q, k, v, do, cars):
    s_len = q.shape[0]
    tq, tk, tr = SB_QUERIES, SB_KEYS, SB_ROWS
    nk, ratio = s_len // tk, tq // tk

    def body(q_ref, k_ref, v_ref, do_ref, car_ref, dq_ref, dk_ref, dv_ref,
             gleft_ref, dqacc_ref, dkacc_ref, dvacc_ref, logw_ref, lsz_ref, da_ref, a_ref, dz_ref):
        i = pl.program_id(1)

        @pl.when(i == 0)
        def _():
            dkacc_ref[...] = jnp.zeros_like(dkacc_ref)
            dvacc_ref[...] = jnp.zeros_like(dvacc_ref)

        qv = q_ref[...]
        dov = do_ref[...]
        low, tri_right2, tri_left2 = _sb_consts(tq, tk)
        lane = lax.broadcasted_iota(jnp.int32, (tr, LANES), 1)
        zero = jnp.zeros_like(qv)
        q_heads = (jnp.where(low, qv, zero), jnp.where(low, zero, qv))
        do_heads = (jnp.where(low, dov, zero), jnp.where(low, zero, dov))
        q_t = qv.astype(F32).T.astype(BF16)
        do_t = dov.astype(F32).T.astype(BF16)
        gleft_ref[...] = jnp.zeros_like(gleft_ref)
        dqacc_ref[...] = jnp.zeros_like(dqacc_ref)

        def front(j, edge):
            key_rows = pl.ds(pl.multiple_of(j * tk, tk), tk)
            keys, values = k_ref[key_rows, :], v_ref[key_rows, :]
            slot = j % SB_SLOTS
            st = {}

            def s_logits(w):
                h, rs = w[0], slice(w[1], w[1] + tr)
                st[w, "z"] = _dot_nt(q_heads[h][rs], keys)
                da_ref[slot, h, rs, :] = _dot_nt(do_heads[h][rs], values)

            def s_terms(w):
                h, rs = w[0], slice(w[1], w[1] + tr)
                valid = _sb_valid(w, edge) if w[2] else None
                lsz, lk = _sb_terms(st.pop((w, "z")), valid)
                st[w, "parts"] = _bf16_parts(lk)
                lsz = lsz if valid is None else jnp.where(valid, lsz, NEG_BIG)
                lsz_ref[slot, h, rs, :] = lsz
                st[w, "lszc"] = lsz + jnp.sum(jnp.where(lane == j, car_ref[h, rs, :], 0.0), axis=-1, keepdims=True)

            def s_suffix(w):
                logw_ref[slot, w[0], w[1]:w[1] + tr, :] = st.pop((w, "lszc")) + _dot(st.pop((w, "parts")), tri_right2)

            return _sb_items(edge), [s_logits, s_terms, s_suffix]

        def back(j, edge):
            kv = k_ref[pl.ds(pl.multiple_of(j * tk, tk), tk), :]
            slot = j % SB_SLOTS
            st = {}

            items = _sb_items(edge)
            head_rows = [[w[1] for w in items if w is not None and w[0] == h] for h in range(2)]

            def s_weights(w):
                h, rs = w[0], slice(w[1], w[1] + tr)
                a = jnp.exp(logw_ref[slot, h, rs, :])
                g = a * da_ref[slot, h, rs, :]
                a_ref[slot, h, rs, :] = a.astype(BF16)
                st[w, "g"], st[w, "parts"] = g, _bf16_parts(g)

            def s_prefix(w):
                st[w, "gs"] = _dot(st.pop((w, "parts")), tri_left2)

            def s_dz(w):
                h, rs = w[0], slice(w[1], w[1] + tr)
                g = st.pop((w, "g"))
                gleft = gleft_ref[h, rs, :]
                gsum = st.pop((w, "gs")) + jnp.tile(gleft, (1, tk // LANES))
                dz = (g - jnp.exp(lsz_ref[slot, h, rs, :]) * (g + gsum)).astype(BF16)
                st[w, "dz"] = dz
                dz_ref[slot, h, rs, :] = dz
                gleft_ref[h, rs, :] = gleft + _row_sum_lanes(g)

            def s_products(w):
                h, rs = w[0], slice(w[1], w[1] + tr)
                dqacc_ref[h, rs, :] += _dot(st.pop((w, "dz")), kv)
                if w[1] == head_rows[h][-1]:
                    feat = slice(h * HEAD_DIM, (h + 1) * HEAD_DIM)
                    hr = slice(head_rows[h][0], tq)
                    dkacc_ref[j, feat, :] += _dot(q_t[feat, hr], dz_ref[slot, h, hr, :])
                    dvacc_ref[j, feat, :] += _dot(do_t[feat, hr], a_ref[slot, h, hr, :])

            return items, [s_weights, s_prefix, s_dz, s_products]

        first = i * ratio
        tile_max = jnp.max(jnp.maximum(car_ref[0], car_ref[1]), axis=0, keepdims=True)
        start = jnp.clip(first + ratio - jnp.sum(jnp.where(tile_max >= SB_DEAD_CARRY, 1, 0)), 0, first)

        edge_tiles = [(first + m, m) for m in range(ratio)]

        @pl.when(start == first)
        def _():
            _emit_skewed(*[front(j, m) for j, m in edge_tiles])
            _emit_skewed(*[back(j, m) for j, m in edge_tiles])

        @pl.when(start == first - 1)
        def _():
            tiles = [(first - 1, None)] + edge_tiles
            _emit_skewed(*[front(j, m) for j, m in tiles])
            _emit_skewed(*[back(j, m) for j, m in tiles])

        @pl.when(start < first - 1)
        def _():
            _emit_skewed(front(start, None))

            def step(jj, carry):
                _emit_skewed(front(jj, None), back(jj - 1, None))
                return carry

            lax.fori_loop(start + 1, first, step, 0)
            _emit_skewed(front(first, 0), back(first - 1, None))
            for m in range(1, ratio):
                _emit_skewed(front(first + m, m), back(first + m - 1, m - 1))
            _emit_skewed(back(first + ratio - 1, ratio - 1))

        dq_ref[...] = (Q_SCALE * jnp.where(low, dqacc_ref[0], dqacc_ref[1])).astype(BF16)

        @pl.when(i == s_len // tq - 1)
        def _():
            for j in range(nk):
                dk_ref[j * tk:(j + 1) * tk, :] = dkacc_ref[j].T.astype(BF16)
                dv_ref[j * tk:(j + 1) * tk, :] = dvacc_ref[j].T.astype(BF16)

    qblk = pl.BlockSpec((tq, LANES), lambda p, i: (i, p))
    col_full = pl.BlockSpec((s_len, LANES), lambda p, i: (0, p))
    return pl.pallas_call(
        body, name="sb_bwd",
        grid=(N_HEADS // 2, s_len // tq),
        in_specs=[qblk, col_full, col_full, qblk, pl.BlockSpec((2, tq, LANES), lambda p, i: (p, i, 0))],
        out_specs=[qblk, col_full, col_full],
        out_shape=[jax.ShapeDtypeStruct((s_len, N_HEADS * HEAD_DIM), BF16)] * 3,
        scratch_shapes=[pltpu.VMEM((2, tq, LANES), F32), pltpu.VMEM((2, tq, LANES), F32),
                        pltpu.VMEM((nk, LANES, tk), F32), pltpu.VMEM((nk, LANES, tk), F32)]
        + [pltpu.VMEM((SB_SLOTS, 2, tq, tk), F32)] * 3 + [pltpu.VMEM((SB_SLOTS, 2, tq, tk), BF16)] * 2,
        compiler_params=_cparams("parallel", "arbitrary"),
    )(q, k, v, do, cars)


def _local_step(xs, tgt, gains, sinks, rel_bias, weights_of, ship):
    g1, gmix, g2, gfin = gains
    bkt = _rel_bucket_matrix()
    grads = {}

    def carried(outs, comm, count):
        return outs[:count], (list(outs[count:]) if comm is not None else None)

    wts = dict(weights_of(0, None))
    comm = ship("weights", 1)
    (x1, h1, a1, b1, u1), landed = carried(
        _ffn_fwd(xs, g1, wts["ffn1_w1t"], wts["ffn1_w3t"], wts["ffn1_w2"], "1", comm), comm, 5)
    wts.update(weights_of(1, landed))
    hm, qa, ka, va, qb, kb, vb, ga, gb = _proj_fwd(x1, gmix, wts["w_int"])
    oa = _swa_fwd(rel_bias, sinks, bkt, qa, ka, va)
    comm = ship("weights", 2)
    (ob, cars), landed = carried(_sb_fwd(qb, kb, vb, comm), comm, 2)
    wts.update(weights_of(2, landed))
    x2, mg = _merge_fwd(x1, oa, ob, ga, gb, wts["w_swa"], wts["w_sb"], wts["w_out"])
    dx3, h3, a3, b3, u3, loss, dgfin = _ffn_fwd(x2, g2, wts["ffn2_w1t"], wts["ffn2_w3t"], wts["ffn2_w2"], "2",
                                                loss=(tgt, gfin))

    def grad_chain(items):
        prev = None
        for name, lhs, rhs in items:
            comm = None if prev is None else ship("grads", (prev[0],), prev[1])
            res = _matmul_tn_tiled(lhs, rhs, name, comm)
            if prev is not None:
                grads[(prev[0],)] = prev[1] if comm is None else res[1]
            prev = (name, {_GRAD_KEY[name]: res if comm is None else res[0]})
        return prev

    dx2, dg2, da3, db3, dx3b = _ffn_bwd(dx3, x2, g2, a3, b3, wts["ffn2_w1t"], wts["ffn2_w3t"], wts["ffn2_w2"], "2")
    last = grad_chain((("ffn2_w1", da3, h3), ("ffn2_w3", db3, h3), ("ffn2_w2", u3, dx3b)))
    comm = ship("grads", (last[0],), last[1])
    (doa, dob, dga, dgb, dpa, dpb, dx2b), landed = carried(
        _merge_bwd(dx2, oa, ob, ga, gb, wts["w_swa"], wts["w_sb"], wts["w_out"], comm), comm, 7)
    grads[(last[0],)] = last[1] if comm is None else landed[0]

    def keep(names, big, comm, landed):
        for i, name in enumerate(names):
            grads[(name,)] = {_GRAD_KEY[name]: big[_GRAD_KEY[name]]} if comm is None else landed[i]

    big = {"w_out": _matmul_tn(mg, dx2b, "w_out"), "w_swa": _matmul_tn(oa, dpa, "w_swa"),
           "w_sb": _matmul_tn(ob, dpb, "w_sb")}
    comm = ship("grads", GROUPS[1][1:], big)
    (dqa, dka, dva, dtab, dsink), landed = carried(_swa_bwd(rel_bias, sinks, bkt, qa, ka, va, doa, comm), comm, 5)
    keep(GROUPS[1][1:], big, comm, landed)
    dqb, dkb, dvb = _sb_bwd(qb, kb, vb, dob, cars)
    dpieces = (dqa, dka.astype(BF16), dva.astype(BF16), dqb, dkb, dvb, dga, dgb)
    big = {"w_int": _matmul_tn_stacked(dpieces, hm, "w_in")}
    comm = ship("grads", GROUPS[1][:1], big)
    (dx1, dgmix), landed = carried(_proj_bwd(dpieces, dx2, x1, gmix, wts["w_int"], comm), comm, 2)
    keep(GROUPS[1][:1], big, comm, landed)

    dx0, dg1, da1, db1, dx1b = _ffn_bwd(dx1, xs, g1, a1, b1, wts["ffn1_w1t"], wts["ffn1_w3t"], wts["ffn1_w2"], "1")

    last = grad_chain((("ffn1_w1", da1, h1), ("ffn1_w3", db1, h1), ("ffn1_w2", u1, dx1b)))
    grads[(last[0],)] = last[1]

    small = {"gains": (dg1, dgmix, dg2, dgfin), "sinks": dsink[:, 0], "rel_bias": dtab[:, :N_HEADS]}
    return loss, dx0, small, grads


def _my_place():
    return lax.axis_index("x"), lax.axis_index("y"), lax.axis_index("c")


def _flip(v, bit):
    return 1 - v if bit else v


_RELATIONS = tuple((k >> 2 & 1, k >> 1 & 1, k & 1) for k in range(1, N_DEV))


def _gather_weights(blocks, tag):
    count = len(blocks)

    def body(*refs):
        x_refs, out_refs = refs[:count], refs[count:2 * count]
        send_sems, recv_sems, local_sems = refs[2 * count:]
        x, y, c = _my_place()
        me, sibling = (x, y, c), (x, y, 1 - c)
        chips = [(1 - x, y), (x, 1 - y), (1 - x, 1 - y)]

        def rows(s, px, py, pc):
            return out_refs[s].at[4 * px + 2 * py + pc]

        def copy(s, k, block, to, src=None):
            return pltpu.make_async_remote_copy(
                src_ref=rows(s, *block) if src is None else src, dst_ref=rows(s, *block),
                send_sem=send_sems.at[s, k], recv_sem=recv_sems.at[s, k],
                device_id=to, device_id_type=pl.DeviceIdType.MESH)

        mine = [pltpu.make_async_copy(x_refs[s], rows(s, *me), local_sems.at[s]) for s in range(count)]
        first, passed = [], []
        for s in range(count):
            mine[s].start()
            first.append(copy(s, 0, me, sibling, src=x_refs[s]))
            first += [copy(s, 1 + j, me, (*chip, c), src=x_refs[s]) for j, chip in enumerate(chips)]
        for cp in first:
            cp.start()
        for s in range(count):
            for j, chip in enumerate(chips):
                copy(s, 1 + j, (*chip, c), me).wait_recv()
                passed.append(copy(s, 4 + j, (*chip, c), sibling))
                passed[-1].start()
        for s in range(count):
            copy(s, 0, sibling, me).wait_recv()
            for j, chip in enumerate(chips):
                copy(s, 4 + j, (*chip, 1 - c), me).wait_recv()
        for cp in first + passed:
            cp.wait_send()
        for cp in mine:
            cp.wait()

    anywhere = pl.BlockSpec(memory_space=pl.ANY)
    return pl.pallas_call(
        body, name=f"gather_weights_{tag}",
        out_shape=[jax.ShapeDtypeStruct((N_DEV,) + b.shape, b.dtype) for b in blocks],
        in_specs=[anywhere] * count, out_specs=[anywhere] * count,
        scratch_shapes=[pltpu.SemaphoreType.DMA((count, N_DEV - 1)), pltpu.SemaphoreType.DMA((count, N_DEV - 1)),
                        pltpu.SemaphoreType.DMA((count,))],
    )(*blocks)


def _exchange_grads(gp, tag):
    def body(g_ref, out_ref, send_sems, recv_sems, local_sem):
        x, y, c = _my_place()
        me = 4 * x + 2 * y + c
        mine = pltpu.make_async_copy(g_ref.at[me], out_ref.at[me], local_sem)
        mine.start()
        copies = []
        for k, (fx, fy, fc) in enumerate(_RELATIONS):
            px, py, pc = _flip(x, fx), _flip(y, fy), _flip(c, fc)
            peer = 4 * px + 2 * py + pc
            copies.append((
                pltpu.make_async_remote_copy(
                    src_ref=g_ref.at[peer], dst_ref=out_ref.at[me], send_sem=send_sems.at[k], recv_sem=recv_sems.at[k],
                    device_id=(px, py, pc), device_id_type=pl.DeviceIdType.MESH),
                pltpu.make_async_remote_copy(
                    src_ref=g_ref.at[peer], dst_ref=out_ref.at[peer], send_sem=send_sems.at[k], recv_sem=recv_sems.at[k],
                    device_id=(px, py, pc), device_id_type=pl.DeviceIdType.MESH)))
        for out_cp, _ in copies:
            out_cp.start()
        for _, in_cp in copies:
            in_cp.wait_recv()
        for out_cp, _ in copies:
            out_cp.wait_send()
        mine.wait()

    return pl.pallas_call(
        body, name=f"exchange_grads_{tag}",
        out_shape=jax.ShapeDtypeStruct(gp.shape, gp.dtype),
        in_specs=[pl.BlockSpec(memory_space=pl.ANY)],
        out_specs=pl.BlockSpec(memory_space=pl.ANY),
        scratch_shapes=[pltpu.SemaphoreType.DMA((7,)), pltpu.SemaphoreType.DMA((7,)), pltpu.SemaphoreType.DMA(())],
    )(gp)


def _peers():
    x, y, c = _my_place()
    out = []
    for k, (fx, fy, fc) in enumerate(_RELATIONS):
        px, py, pc = _flip(x, fx), _flip(y, fy), _flip(c, fc)
        out.append((k, (px, py, pc), 4 * px + 2 * py + pc))
    return out, 4 * x + 2 * y + c


def _grid_ends(*grid):
    def first():
        return functools.reduce(lambda a, b: a & b, [pl.program_id(d) == 0 for d in range(len(grid))])

    def last():
        return functools.reduce(lambda a, b: a & b, [pl.program_id(d) == n - 1 for d, n in enumerate(grid)])

    return {"first": first, "last": last}


def _call(body, operands, *, comm=None, first=None, last=None, **kw):
    if comm is None:
        return pl.pallas_call(body, **kw)(*operands)
    in_specs, out_specs, out_shape = list(kw.pop("in_specs")), list(kw.pop("out_specs")), list(kw.pop("out_shape"))
    scratch = list(kw.pop("scratch_shapes", ()))
    n_in, n_out, n_scr, n_src = len(in_specs), len(out_specs), len(scratch), len(comm)

    def wrapped(*refs):
        ins, src_refs = refs[:n_in], refs[n_in:n_in + n_src]
        outs = refs[n_in + n_src:n_in + n_src + n_out]
        land_refs = refs[n_in + n_src + n_out:n_in + 2 * n_src + n_out]
        scr = refs[n_in + 2 * n_src + n_out:n_in + 2 * n_src + n_out + n_scr]
        send_sems, recv_sems, local_sems = refs[n_in + 2 * n_src + n_out + n_scr:]
        peers, me = _peers()
        mine, going, coming = [], [], []
        for s, (_, per_peer) in enumerate(comm):
            src_ref, land_ref = src_refs[s], land_refs[s]
            mine.append(pltpu.make_async_copy(src_ref.at[me] if per_peer else src_ref, land_ref.at[me], local_sems.at[s]))
            for k, where, slab in peers:
                piece = src_ref.at[slab] if per_peer else src_ref
                going.append(pltpu.make_async_remote_copy(
                    src_ref=piece, dst_ref=land_ref.at[me], send_sem=send_sems.at[s, k], recv_sem=recv_sems.at[s, k],
                    device_id=where, device_id_type=pl.DeviceIdType.MESH))
                coming.append(pltpu.make_async_remote_copy(
                    src_ref=piece, dst_ref=land_ref.at[slab], send_sem=send_sems.at[s, k], recv_sem=recv_sems.at[s, k],
                    device_id=where, device_id_type=pl.DeviceIdType.MESH))

        @pl.when(first())
        def _():
            for cp in mine + going:
                cp.start()

        body(*ins, *outs, *scr)

        @pl.when(last())
        def _():
            for cp in coming:
                cp.wait_recv()
            for cp in going:
                cp.wait_send()
            for cp in mine:
                cp.wait()

    anywhere = pl.BlockSpec(memory_space=pl.ANY)
    lands = [jax.ShapeDtypeStruct(src.shape if per_peer else (N_DEV,) + src.shape, src.dtype) for src, per_peer in comm]
    return pl.pallas_call(
        wrapped, in_specs=in_specs + [anywhere] * n_src, out_specs=out_specs + [anywhere] * n_src,
        out_shape=out_shape + lands,
        scratch_shapes=scratch + [pltpu.SemaphoreType.DMA((n_src, N_DEV - 1)), pltpu.SemaphoreType.DMA((n_src, N_DEV - 1)),
                                  pltpu.SemaphoreType.DMA((n_src,))],
        **kw)(*operands, *[src for src, _ in comm])


def _adamw(w, g, m, v):
    m = ADAM_B1 * m + (1.0 - ADAM_B1) * g
    v = ADAM_B2 * v + (1.0 - ADAM_B2) * jnp.square(g)
    m_hat = m / (1.0 - ADAM_B1 ** ADAM_STEP)
    v_hat = v / (1.0 - ADAM_B2 ** ADAM_STEP)
    delta = -ADAM_LR * (m_hat / (jnp.sqrt(v_hat) + ADAM_EPS) + ADAM_WD * w)
    return delta, m, v


def _sum_and_adamw(parts, w, m, v, tr, tag):
    rows = w.shape[0]
    assert rows % tr == 0

    def body(p_ref, w_ref, m_ref, v_ref, g_out, d_out, m_out, v_out):
        g = p_ref[0].astype(F32)
        for d in range(1, N_DEV):
            g = g + p_ref[d].astype(F32)
        delta, mn, vn = _adamw(w_ref[...], g, m_ref[...], v_ref[...])
        g_out[...] = g
        d_out[...] = delta
        m_out[...] = mn
        v_out[...] = vn

    sp = pl.BlockSpec((tr, D_MODEL), lambda i: (i, 0))
    return pl.pallas_call(
        body, name=f"sum_and_adamw_{tag}",
        grid=(rows // tr,),
        in_specs=[pl.BlockSpec((N_DEV, tr, D_MODEL), lambda i: (0, i, 0)), sp, sp, sp],
        out_specs=[sp] * 4,
        out_shape=[jax.ShapeDtypeStruct(w.shape, F32)] * 4,
        compiler_params=_cparams("parallel"),
    )(parts, w, m, v)


def _small_allreduce_adamw(part, w, m, v):
    def body(p_ref, w_ref, m_ref, v_ref, g_out, d_out, m_out, v_out, buf, send_sems, recv_sems):
        x, y, c = _my_place()
        me = 4 * x + 2 * y + c
        buf[me] = p_ref[...]
        copies = []
        for k, (fx, fy, fc) in enumerate(_RELATIONS):
            px, py, pc = _flip(x, fx), _flip(y, fy), _flip(c, fc)
            peer = 4 * px + 2 * py + pc
            copies.append((
                pltpu.make_async_remote_copy(
                    src_ref=buf.at[me], dst_ref=buf.at[me], send_sem=send_sems.at[k], recv_sem=recv_sems.at[k],
                    device_id=(px, py, pc), device_id_type=pl.DeviceIdType.MESH),
                pltpu.make_async_remote_copy(
                    src_ref=buf.at[me], dst_ref=buf.at[peer], send_sem=send_sems.at[k], recv_sem=recv_sems.at[k],
                    device_id=(px, py, pc), device_id_type=pl.DeviceIdType.MESH)))
        for out_cp, _ in copies:
            out_cp.start()
        for _, in_cp in copies:
            in_cp.wait_recv()
        for out_cp, _ in copies:
            out_cp.wait_send()
        g = buf[0]
        for d in range(1, N_DEV):
            g = g + buf[d]
        delta, mn, vn = _adamw(w_ref[...], g, m_ref[...], v_ref[...])
        g_out[...] = g
        d_out[...] = delta
        m_out[...] = mn
        v_out[...] = vn

    vm = pl.BlockSpec(memory_space=pltpu.VMEM)
    return pl.pallas_call(
        body, name="small_allreduce_adamw",
        in_specs=[vm] * 4, out_specs=[vm] * 4,
        out_shape=[jax.ShapeDtypeStruct(w.shape, F32)] * 4,
        scratch_shapes=[pltpu.VMEM((N_DEV,) + part.shape, F32),
                        pltpu.SemaphoreType.DMA((7,)), pltpu.SemaphoreType.DMA((7,))],
    )(part, w, m, v)


_TRANSPOSED = ("ffn1_w1", "ffn1_w3", "w_in", "ffn2_w1", "ffn2_w3")
_BRANCH = ("w_branch_swa", "w_branch_sb")


def _pack_shards(t, names):
    parts = []
    for name in names:
        a = t[name][0]
        if name in _TRANSPOSED:
            a = a.T
        elif name in _BRANCH:
            a = a.reshape(64, D_MODEL)
        parts.append(a)
    return jnp.concatenate(parts, axis=0)


def _unpack_shards(p, names):
    out, lo = {}, 0
    for name in names:
        a = p[lo:lo + BIG_ROWS[BIG_NAMES.index(name)]]
        lo += a.shape[0]
        if name in _TRANSPOSED:
            a = a.T
        elif name in _BRANCH:
            a = a.reshape(512, 128)
        out[name] = a[None]
    return out


def _full_weights(zones, names):
    out = {}
    for name, a in zip(names, zones):
        if name in _BRANCH:
            a = a.reshape(N_DEV, 512, 128).transpose(1, 0, 2).reshape(512, D_MODEL)
        out[_GRAD_KEY[name]] = a.reshape(-1, D_MODEL)
    return out


_GRAD_KEY = {"ffn1_w1": "ffn1_w1t", "ffn1_w3": "ffn1_w3t", "ffn1_w2": "ffn1_w2", "w_in": "w_int",
             "w_branch_swa": "w_swa", "w_branch_sb": "w_sb", "w_out": "w_out",
             "ffn2_w1": "ffn2_w1t", "ffn2_w3": "ffn2_w3t", "ffn2_w2": "ffn2_w2"}


def _pack_full_grads(big, names):
    parts = []
    for name in names:
        a = big[_GRAD_KEY[name]]
        if name in _BRANCH:
            a = a.reshape(512, N_DEV, 128).transpose(1, 0, 2)
        parts.append(a.reshape(N_DEV, BIG_ROWS[BIG_NAMES.index(name)], D_MODEL).astype(BF16))
    return jnp.concatenate(parts, axis=1)


_SMALL_NAMES = ("norm_ffn1", "norm_mix", "norm_ffn2", "norm_final", "swa_sinks", "rel_bias")


def _pack_small(vals):
    rows = []
    for a in vals:
        a = a.reshape(-1)
        rows.append(jnp.pad(a, (0, D_MODEL - a.shape[0])))
    rows += [jnp.zeros((D_MODEL,), F32)] * (SMALL_ROWS - len(rows))
    return jnp.stack(rows)


def _unpack_small(p):
    return {"norm_ffn1": p[0:1], "norm_mix": p[1:2], "norm_ffn2": p[2:3], "norm_final": p[3],
            "swa_sinks": p[4:5, :N_HEADS], "rel_bias": p[5, :REL_BUCKETS * N_HEADS].reshape(REL_BUCKETS, N_HEADS)}


ALL_NAMES = ("norm_ffn1", "ffn1_w1", "ffn1_w3", "ffn1_w2", "norm_mix", "w_in", "swa_sinks", "rel_bias",
             "w_branch_swa", "w_branch_sb", "w_out", "norm_ffn2", "ffn2_w1", "ffn2_w3", "ffn2_w2", "norm_final")


def kernel(x, norm_ffn1, ffn1_w1, ffn1_w3, ffn1_w2, norm_mix, w_in, swa_sinks, rel_bias, w_branch_swa, w_branch_sb, w_out, norm_ffn2, ffn2_w1, ffn2_w3, ffn2_w2, norm_final, loss_target, m_norm_ffn1, m_ffn1_w1, m_ffn1_w3, m_ffn1_w2, m_norm_mix, m_w_in, m_swa_sinks, m_rel_bias, m_w_branch_swa, m_w_branch_sb, m_w_out, m_norm_ffn2, m_ffn2_w1, m_ffn2_w3, m_ffn2_w2, m_norm_final, v_norm_ffn1, v_ffn1_w1, v_ffn1_w3, v_ffn1_w2, v_norm_mix, v_w_in, v_swa_sinks, v_rel_bias, v_w_branch_swa, v_w_branch_sb, v_w_out, v_norm_ffn2, v_ffn2_w1, v_ffn2_w3, v_ffn2_w2, v_norm_final):
    w = dict(zip(ALL_NAMES, (norm_ffn1, ffn1_w1, ffn1_w3, ffn1_w2, norm_mix, w_in, swa_sinks, rel_bias,
                             w_branch_swa, w_branch_sb, w_out, norm_ffn2, ffn2_w1, ffn2_w3, ffn2_w2, norm_final)))
    m = dict(zip(ALL_NAMES, (m_norm_ffn1, m_ffn1_w1, m_ffn1_w3, m_ffn1_w2, m_norm_mix, m_w_in, m_swa_sinks, m_rel_bias,
                             m_w_branch_swa, m_w_branch_sb, m_w_out, m_norm_ffn2, m_ffn2_w1, m_ffn2_w3, m_ffn2_w2,
                             m_norm_final)))
    v = dict(zip(ALL_NAMES, (v_norm_ffn1, v_ffn1_w1, v_ffn1_w3, v_ffn1_w2, v_norm_mix, v_w_in, v_swa_sinks, v_rel_bias,
                             v_w_branch_swa, v_w_branch_sb, v_w_out, v_norm_ffn2, v_ffn2_w1, v_ffn2_w3, v_ffn2_w2,
                             v_norm_final)))

    def my_blocks(group):
        return [_pack_shards(w, (name,)).astype(BF16) for name in GROUPS[group]]

    gathered0 = _gather_weights(my_blocks(0), "group0")

    def weights_of(group, landed):
        return _full_weights(gathered0 if group == 0 else landed, GROUPS[group])

    def ship(kind, which, grads=None):
        if kind == "weights":
            return [(block, False) for block in my_blocks(which)]
        return [(_pack_full_grads(grads, (name,)), True) for name in which]

    gains = (norm_ffn1, norm_mix, norm_ffn2, norm_final.reshape(1, D_MODEL))
    loss, dx, small, parts = _local_step(x[0], loss_target[0], gains, swa_sinks, rel_bias, weights_of, ship)

    big_outs = [{}, {}, {}, {}]
    for names, tile in zip(SUM_GROUPS, SUM_TILE):
        landed = parts[names]
        if isinstance(landed, dict):
            landed = _exchange_grads(_pack_full_grads(landed, names), names[0])
        res = _sum_and_adamw(landed, _pack_shards(w, names), _pack_shards(m, names), _pack_shards(v, names),
                             tile, names[0])
        for acc, packed in zip(big_outs, res):
            acc.update(_unpack_shards(packed, names))
    g_big, d_big, m_big, v_big = big_outs

    small_part = _pack_small(small["gains"] + (small["sinks"], small["rel_bias"], loss))
    zero = jnp.zeros((1,), F32)
    small_res = _small_allreduce_adamw(
        small_part, _pack_small([w[n] for n in _SMALL_NAMES] + [zero]), _pack_small([m[n] for n in _SMALL_NAMES] + [zero]),
        _pack_small([v[n] for n in _SMALL_NAMES] + [zero]))
    g_sm, d_sm, m_sm, v_sm = (_unpack_small(p) for p in small_res)

    outs = [small_res[0][len(_SMALL_NAMES), 0], dx[None]]
    for big_d, small_d in ((g_big, g_sm), (d_big, d_sm), (m_big, m_sm), (v_big, v_sm)):
        merged = {**big_d, **small_d}
        outs += [merged[n] for n in ALL_NAMES]
    return tuple(outs)
```

```python
import functools

import jax
import jax.numpy as jnp
import numpy as np
from jax import lax
from jax.experimental import pallas as pl
from jax.experimental.pallas import tpu as pltpu

F32 = jnp.float32
BF16 = jnp.bfloat16

D_MODEL = 1024
D_FF = 2816
HEAD_DIM = 64
N_HEADS = 8
SWA_KV_HEADS = 2
SWA_GROUP = 4
SWA_BLOCK = 128
REL_BUCKETS = 32
REL_MAX_DIST = 128
RMS_EPS = 1e-6
NEG_BIG = -1e30
Q_SCALE = HEAD_DIM ** -0.5
LANES = 128

N_DEV = 8

ADAM_LR = 0.001
ADAM_B1 = 0.9
ADAM_B2 = 0.999
ADAM_EPS = 1e-08
ADAM_WD = 0.01
ADAM_STEP = 10

IN_SIZES = (512, 128, 128, 512, 512, 512, 1024, 1024)
IN_OFFS = tuple(int(v) for v in np.cumsum((0,) + IN_SIZES))
IN_W = IN_OFFS[-1]

BIG_NAMES = ("ffn1_w1", "ffn1_w3", "ffn1_w2", "w_in", "w_branch_swa", "w_branch_sb", "w_out",
             "ffn2_w1", "ffn2_w3", "ffn2_w2")
BIG_ROWS = (352, 352, 352, 544, 64, 64, 128, 352, 352, 352)
SMALL_ROWS = 8
GROUPS = (BIG_NAMES[0:3], BIG_NAMES[3:7], BIG_NAMES[7:10])
SUM_GROUPS = tuple((n,) for n in BIG_NAMES)
SUM_TILE = (176, 176, 176, 272, 64, 64, 128, 176, 176, 176)

VMEM_LIMIT = 56 * 1024 * 1024
FFN_PIECES = 2
SB_QUERIES = 512
SB_KEYS = 256
SB_ROWS = 256
SB_SLOTS = 3
SB_SUM_PARTS = 1
SB_LOGIT_CAP = 80.0
SB_DEAD_CARRY = -110.0


def _dot(a, b):
    return jnp.dot(a, b, preferred_element_type=F32)


def _dot_nt(a, b):
    return lax.dot_general(a, b, (((1,), (1,)), ((), ())), preferred_element_type=F32)


def _dot_tn(a, b):
    return lax.dot_general(a, b, (((0,), (0,)), ((), ())), preferred_element_type=F32)


def _cparams(*sem):
    return pltpu.CompilerParams(dimension_semantics=sem, vmem_limit_bytes=VMEM_LIMIT)


def _rms_rstd(xv):
    return lax.rsqrt(jnp.mean(xv * xv, axis=-1, keepdims=True) + RMS_EPS)


def _rms_bwd(dh, xv, r, g):
    xhat = xv * r
    dg = jnp.sum(dh * xhat, axis=0, keepdims=True)
    dxn = dh * g
    dx = r * (dxn - xhat * jnp.mean(dxn * xhat, axis=-1, keepdims=True))
    return dx, dg


def _ff_tile_spec(tm, tf):
    return pl.BlockSpec((1, tm, tf), lambda i, j: (j, i, 0))


def _ffn_fwd(x, g, w1t, w3t, w2, tag, comm=None, loss=None):
    s_len = x.shape[0]
    tm, tf = min(1024, s_len), 256
    nf = D_FF // tf

    def body(*refs):
        if loss is None:
            x_ref, g_ref, w1_ref, w3_ref, w2_ref, xo_ref, h_ref, a_ref, b_ref, u_ref, acc_ref, hs_ref = refs
        else:
            (x_ref, g_ref, w1_ref, w3_ref, w2_ref, t_ref, gf_ref,
             xo_ref, h_ref, a_ref, b_ref, u_ref, loss_ref, dgf_ref, acc_ref, hs_ref) = refs
        j = pl.program_id(1)

        @pl.when(j == 0)
        def _():
            xv = x_ref[...]
            h = (xv * _rms_rstd(xv) * g_ref[...]).astype(BF16)
            hs_ref[...] = h
            h_ref[...] = h
            acc_ref[...] = jnp.zeros_like(acc_ref)

        st = {}

        def s_up(rs):
            h = hs_ref[rs, :]
            st[rs.start, "ab"] = (_dot_nt(h, w1_ref[...]), _dot_nt(h, w3_ref[...]))

        def s_act(rs):
            a, b = st.pop((rs.start, "ab"))
            a_ref[0, rs, :] = a.astype(BF16)
            b_ref[0, rs, :] = b.astype(BF16)
            uh = (0.5 * (a * jax.nn.sigmoid(a) * b)).astype(BF16)
            u_ref[0, rs, :] = uh
            st[rs.start, "u"] = uh

        def s_down(rs):
            acc_ref[rs, :] += _dot(st.pop((rs.start, "u")), w2_ref[...])

        _emit_skewed(([slice(r, r + tm // FFN_PIECES) for r in range(0, tm, tm // FFN_PIECES)], [s_up, s_act, s_down]))

        if loss is not None:
            @pl.when((pl.program_id(0) == 0) & (j == 0))
            def _():
                loss_ref[...] = jnp.zeros_like(loss_ref)
                dgf_ref[...] = jnp.zeros_like(dgf_ref)

        @pl.when(j == nf - 1)
        def _():
            xo = x_ref[...] + acc_ref[...]
            if loss is None:
                xo_ref[...] = xo
            else:
                gv = gf_ref[...]
                r = _rms_rstd(xo)
                err = xo * r * gv - t_ref[...]
                loss_ref[...] += 0.5 * jnp.sum(jnp.mean(err * err, axis=-1, keepdims=True), axis=0, keepdims=True)
                dx, dg = _rms_bwd(err * (1.0 / D_MODEL), xo, r, gv)
                xo_ref[...] = dx
                dgf_ref[...] += dg

    row = lambda i, j: (i, 0)
    fixed = lambda i, j: (0, 0)
    with_loss = loss is not None
    return _call(
        body, (x, g, w1t, w3t, w2) + (tuple(loss) if with_loss else ()), comm=comm, **_grid_ends(s_len // tm, nf),
        name=f"ffn_fwd_{tag}",
        grid=(s_len // tm, nf),
        in_specs=[pl.BlockSpec((tm, D_MODEL), row), pl.BlockSpec((1, D_MODEL), fixed),
                  pl.BlockSpec((tf, D_MODEL), lambda i, j: (j, 0)), pl.BlockSpec((tf, D_MODEL), lambda i, j: (j, 0)),
                  pl.BlockSpec((tf, D_MODEL), lambda i, j: (j, 0))]
        + ([pl.BlockSpec((tm, D_MODEL), row), pl.BlockSpec((1, D_MODEL), fixed)] if with_loss else []),
        out_specs=[pl.BlockSpec((tm, D_MODEL), row), pl.BlockSpec((tm, D_MODEL), row)] + [_ff_tile_spec(tm, tf)] * 3
        + ([pl.BlockSpec((1, 1), fixed), pl.BlockSpec((1, D_MODEL), fixed)] if with_loss else []),
        out_shape=[jax.ShapeDtypeStruct((s_len, D_MODEL), F32), jax.ShapeDtypeStruct((s_len, D_MODEL), BF16)]
        + [jax.ShapeDtypeStruct((nf, s_len, tf), BF16)] * 3
        + ([jax.ShapeDtypeStruct((1, 1), F32), jax.ShapeDtypeStruct((1, D_MODEL), F32)] if with_loss else []),
        scratch_shapes=[pltpu.VMEM((tm, D_MODEL), F32), pltpu.VMEM((tm, D_MODEL), BF16)],
        compiler_params=_cparams("arbitrary", "arbitrary"),
    )


def _ffn_bwd(dy, x, g, a, b, w1t, w3t, w2, tag, comm=None):
    s_len = x.shape[0]
    tm, tf = min(1024, s_len), 256
    nf = D_FF // tf

    def body(dy_ref, x_ref, g_ref, a_ref, b_ref, w1_ref, w3_ref, w2_ref,
             dx_ref, dg_ref, da_ref, db_ref, dyb_ref, acc_ref, dys_ref):
        i, j = pl.program_id(0), pl.program_id(1)

        @pl.when(j == 0)
        def _():
            dyb = dy_ref[...].astype(BF16)
            dys_ref[...] = 0.5 * dyb
            dyb_ref[...] = dyb
            acc_ref[...] = jnp.zeros_like(acc_ref)

        @pl.when((i == 0) & (j == 0))
        def _():
            dg_ref[...] = jnp.zeros_like(dg_ref)

        st = {}

        def s_du(rs):
            st[rs.start, "du"] = _dot_nt(dys_ref[rs, :], w2_ref[...])

        def s_act(rs):
            du = st.pop((rs.start, "du"))
            av = a_ref[0, rs, :].astype(F32)
            bv = b_ref[0, rs, :].astype(F32)
            sg = jax.nn.sigmoid(av)
            sil = av * sg
            da = (du * bv * (sg + sil * (1.0 - sg))).astype(BF16)
            db = (du * sil).astype(BF16)
            da_ref[0, rs, :] = da
            db_ref[0, rs, :] = db
            st[rs.start, "dab"] = (da, db)

        def s_dh(rs):
            da, db = st.pop((rs.start, "dab"))
            acc_ref[rs, :] += _dot(da, w1_ref[...]) + _dot(db, w3_ref[...])

        _emit_skewed(([slice(r, r + tm // FFN_PIECES) for r in range(0, tm, tm // FFN_PIECES)], [s_du, s_act, s_dh]))

        @pl.when(j == nf - 1)
        def _():
            xv = x_ref[...]
            dx, dg = _rms_bwd(acc_ref[...], xv, _rms_rstd(xv), g_ref[...])
            dx_ref[...] = dy_ref[...] + dx
            dg_ref[...] += dg

    row = lambda i, j: (i, 0)
    wsp = pl.BlockSpec((tf, D_MODEL), lambda i, j: (j, 0))
    return _call(
        body, (dy, x, g, a, b, w1t, w3t, w2), comm=comm, **_grid_ends(s_len // tm, nf), name=f"ffn_bwd_{tag}",
        grid=(s_len // tm, nf),
        in_specs=[pl.BlockSpec((tm, D_MODEL), row), pl.BlockSpec((tm, D_MODEL), row),
                  pl.BlockSpec((1, D_MODEL), lambda i, j: (0, 0)),
                  _ff_tile_spec(tm, tf), _ff_tile_spec(tm, tf), wsp, wsp, wsp],
        out_specs=[pl.BlockSpec((tm, D_MODEL), row), pl.BlockSpec((1, D_MODEL), lambda i, j: (0, 0)),
                   _ff_tile_spec(tm, tf), _ff_tile_spec(tm, tf), pl.BlockSpec((tm, D_MODEL), row)],
        out_shape=[jax.ShapeDtypeStruct((s_len, D_MODEL), F32), jax.ShapeDtypeStruct((1, D_MODEL), F32),
                   jax.ShapeDtypeStruct((nf, s_len, tf), BF16), jax.ShapeDtypeStruct((nf, s_len, tf), BF16),
                   jax.ShapeDtypeStruct((s_len, D_MODEL), BF16)],
        scratch_shapes=[pltpu.VMEM((tm, D_MODEL), F32), pltpu.VMEM((tm, D_MODEL), BF16)],
        compiler_params=_cparams("arbitrary", "arbitrary"),
    )


def _matmul_tn(lhs, rhs, tag, comm=None):
    s_len, m = lhs.shape
    n = rhs.shape[1]
    tm = min(512, s_len)
    tj = m if m <= 1024 else 1408
    assert m % tj == 0
    last_rows = s_len // tm - 1

    def body(l_ref, r_ref, o_ref, acc_ref):
        i = pl.program_id(1)

        @pl.when(i == 0)
        def _():
            acc_ref[...] = jnp.zeros_like(acc_ref)

        acc_ref[...] += _dot_tn(l_ref[...], r_ref[...])

        @pl.when(i == last_rows)
        def _():
            o_ref[...] = acc_ref[...].astype(BF16)

    res = _call(
        body, (lhs, rhs), comm=comm, **_grid_ends(m // tj, s_len // tm), name=f"matmul_tn_{tag}",
        grid=(m // tj, s_len // tm),
        in_specs=[pl.BlockSpec((tm, tj), lambda j, i: (i, j)), pl.BlockSpec((tm, n), lambda j, i: (i, 0))],
        out_specs=[pl.BlockSpec((tj, n), lambda j, i: (j, 0))],
        out_shape=[jax.ShapeDtypeStruct((m, n), BF16)],
        scratch_shapes=[pltpu.VMEM((tj, n), F32)],
        compiler_params=_cparams("arbitrary", "arbitrary"),
    )
    return res[0] if comm is None else tuple(res)


def _matmul_tn_tiled(lhs, rhs, tag, comm=None, part=(0, 1)):
    nf, s_len, tf = lhs.shape
    n = rhs.shape[1] // part[1]
    tm = min(512, s_len)
    last_rows = s_len // tm - 1

    def body(l_ref, r_ref, o_ref, acc_ref):
        i = pl.program_id(0)

        @pl.when(i == 0)
        def _():
            acc_ref[...] = jnp.zeros_like(acc_ref)

        rv = r_ref[...]
        for t in range(nf):
            acc_ref[t * tf:(t + 1) * tf, :] += _dot_tn(l_ref[t], rv)

        @pl.when(i == last_rows)
        def _():
            o_ref[...] = acc_ref[...].astype(BF16)

    res = _call(
        body, (lhs, rhs), comm=comm, **_grid_ends(s_len // tm), name=f"matmul_tn_{tag}",
        grid=(s_len // tm,),
        in_specs=[pl.BlockSpec((nf, tm, tf), lambda i: (0, i, 0)), pl.BlockSpec((tm, n), lambda i: (i, part[0]))],
        out_specs=[pl.BlockSpec((nf * tf, n), lambda i: (0, 0))],
        out_shape=[jax.ShapeDtypeStruct((nf * tf, n), BF16)],
        scratch_shapes=[pltpu.VMEM((nf * tf, n), F32)],
        compiler_params=_cparams("arbitrary"),
    )
    return res[0] if comm is None else tuple(res)


def _matmul_tn_stacked(pieces, rhs, tag):
    s_len, n = rhs.shape
    widths = [p.shape[1] for p in pieces]
    offs = [sum(widths[:k]) for k in range(len(widths) + 1)]
    tm = min(256, s_len)
    last_rows = s_len // tm - 1

    def body(*refs):
        l_refs, r_ref, o_ref, acc_ref = refs[:len(pieces)], refs[-3], refs[-2], refs[-1]
        i = pl.program_id(0)

        @pl.when(i == 0)
        def _():
            acc_ref[...] = jnp.zeros_like(acc_ref)

        rv = r_ref[...]
        for k, l_ref in enumerate(l_refs):
            acc_ref[offs[k]:offs[k + 1], :] += _dot_tn(l_ref[...], rv)

        @pl.when(i == last_rows)
        def _():
            o_ref[...] = acc_ref[...].astype(BF16)

    row = lambda i: (i, 0)
    return pl.pallas_call(
        body, name=f"matmul_tn_{tag}",
        grid=(s_len // tm,),
        in_specs=[pl.BlockSpec((tm, w), row) for w in widths] + [pl.BlockSpec((tm, n), row)],
        out_specs=pl.BlockSpec((offs[-1], n), lambda i: (0, 0)),
        out_shape=jax.ShapeDtypeStruct((offs[-1], n), BF16),
        scratch_shapes=[pltpu.VMEM((offs[-1], n), F32)],
        compiler_params=_cparams("arbitrary"),
    )(*pieces, rhs)


def _proj_fwd(x1, g, wint):
    s_len = x1.shape[0]
    tm = min(512, s_len)
    dts = (BF16, BF16, BF16, BF16, BF16, BF16, F32, F32)

    def body(x_ref, g_ref, w_ref, h_ref, *outs):
        xv = x_ref[...]
        h = (xv * _rms_rstd(xv) * g_ref[...]).astype(BF16)
        h_ref[...] = h
        for p, o_ref in enumerate(outs):
            val = _dot_nt(h, w_ref[IN_OFFS[p]:IN_OFFS[p + 1], :])
            if p == 3:
                val = val * Q_SCALE
            o_ref[...] = val.astype(dts[p])

    row = lambda i: (i, 0)
    return pl.pallas_call(
        body, name="proj_fwd",
        grid=(s_len // tm,),
        in_specs=[pl.BlockSpec((tm, D_MODEL), row), pl.BlockSpec((1, D_MODEL), lambda i: (0, 0)),
                  pl.BlockSpec((IN_W, D_MODEL), lambda i: (0, 0))],
        out_specs=[pl.BlockSpec((tm, D_MODEL), row)] + [pl.BlockSpec((tm, w), row) for w in IN_SIZES],
        out_shape=[jax.ShapeDtypeStruct((s_len, D_MODEL), BF16)]
        + [jax.ShapeDtypeStruct((s_len, w), dt) for w, dt in zip(IN_SIZES, dts)],
        compiler_params=_cparams("parallel"),
    )(x1, g, wint)


def _proj_bwd(dpieces, dx2, x1, g, wint, comm=None):
    s_len = x1.shape[0]
    tm = min(512, s_len)

    def body(*refs):
        dps = refs[:8]
        dx2_ref, x_ref, g_ref, w_ref, dx_ref, dg_ref = refs[8:]

        @pl.when(pl.program_id(0) == 0)
        def _():
            dg_ref[...] = jnp.zeros_like(dg_ref)

        dh = _dot(dps[0][...], w_ref[IN_OFFS[0]:IN_OFFS[1], :])
        for p in range(1, 8):
            dh += _dot(dps[p][...], w_ref[IN_OFFS[p]:IN_OFFS[p + 1], :])
        xv = x_ref[...]
        dx, dg = _rms_bwd(dh, xv, _rms_rstd(xv), g_ref[...])
        dx_ref[...] = dx2_ref[...] + dx
        dg_ref[...] += dg

    row = lambda i: (i, 0)
    return _call(
        body, (*dpieces, dx2, x1, g, wint), comm=comm, **_grid_ends(s_len // tm), name="proj_bwd",
        grid=(s_len // tm,),
        in_specs=[pl.BlockSpec((tm, w), row) for w in IN_SIZES]
        + [pl.BlockSpec((tm, D_MODEL), row), pl.BlockSpec((tm, D_MODEL), row),
           pl.BlockSpec((1, D_MODEL), lambda i: (0, 0)), pl.BlockSpec((IN_W, D_MODEL), lambda i: (0, 0))],
        out_specs=[pl.BlockSpec((tm, D_MODEL), row), pl.BlockSpec((1, D_MODEL), lambda i: (0, 0))],
        out_shape=[jax.ShapeDtypeStruct((s_len, D_MODEL), F32), jax.ShapeDtypeStruct((1, D_MODEL), F32)],
        compiler_params=_cparams("arbitrary"),
    )


def _merge_fwd(x1, oa, ob, ga, gb, wswa, wsb, wout):
    s_len = x1.shape[0]
    tm = min(512, s_len)

    def body(x_ref, oa_ref, ob_ref, ga_ref, gb_ref, wa_ref, wb_ref, wo_ref, xo_ref, mg_ref):
        pa = _dot(oa_ref[...], wa_ref[...])
        pb = _dot(ob_ref[...], wb_ref[...])
        mg = (jax.nn.sigmoid(ga_ref[...]) * pa + jax.nn.sigmoid(gb_ref[...]) * pb).astype(BF16)
        mg_ref[...] = mg
        xo_ref[...] = x_ref[...] + _dot(mg, wo_ref[...])

    row = lambda i: (i, 0)
    full = lambda i: (0, 0)
    return pl.pallas_call(
        body, name="merge_fwd",
        grid=(s_len // tm,),
        in_specs=[pl.BlockSpec((tm, D_MODEL), row), pl.BlockSpec((tm, 512), row), pl.BlockSpec((tm, 512), row),
                  pl.BlockSpec((tm, D_MODEL), row), pl.BlockSpec((tm, D_MODEL), row),
                  pl.BlockSpec((512, D_MODEL), full), pl.BlockSpec((512, D_MODEL), full),
                  pl.BlockSpec((D_MODEL, D_MODEL), full)],
        out_specs=[pl.BlockSpec((tm, D_MODEL), row), pl.BlockSpec((tm, D_MODEL), row)],
        out_shape=[jax.ShapeDtypeStruct((s_len, D_MODEL), F32), jax.ShapeDtypeStruct((s_len, D_MODEL), BF16)],
        compiler_params=_cparams("parallel"),
    )(x1, oa, ob, ga, gb, wswa, wsb, wout)


def _merge_bwd(dx2, oa, ob, ga, gb, wswa, wsb, wout, comm=None):
    s_len = dx2.shape[0]
    tm = min(512, s_len)

    def body(dx_ref, oa_ref, ob_ref, ga_ref, gb_ref, wa_ref, wb_ref, wo_ref,
             doa_ref, dob_ref, dga_ref, dgb_ref, dpa_ref, dpb_ref, dxb_ref):
        dxb = dx_ref[...].astype(BF16)
        dxb_ref[...] = dxb
        dmg = _dot_nt(dxb, wo_ref[...])
        for o_ref, g_ref, w_ref, do_ref, dg_ref, dp_ref in (
                (oa_ref, ga_ref, wa_ref, doa_ref, dga_ref, dpa_ref),
                (ob_ref, gb_ref, wb_ref, dob_ref, dgb_ref, dpb_ref)):
            pv = _dot(o_ref[...], w_ref[...])
            sg = jax.nn.sigmoid(g_ref[...])
            dp = (dmg * sg).astype(BF16)
            dp_ref[...] = dp
            dg_ref[...] = (dmg * pv * sg * (1.0 - sg)).astype(BF16)
            do_ref[...] = _dot_nt(dp, w_ref[...]).astype(BF16)

    row = lambda i: (i, 0)
    full = lambda i: (0, 0)
    wide = pl.BlockSpec((tm, D_MODEL), row)
    half = pl.BlockSpec((tm, 512), row)
    return _call(
        body, (dx2, oa, ob, ga, gb, wswa, wsb, wout), comm=comm, **_grid_ends(s_len // tm), name="merge_bwd",
        grid=(s_len // tm,),
        in_specs=[wide, half, half, wide, wide, pl.BlockSpec((512, D_MODEL), full),
                  pl.BlockSpec((512, D_MODEL), full), pl.BlockSpec((D_MODEL, D_MODEL), full)],
        out_specs=[half, half, wide, wide, wide, wide, wide],
        out_shape=[jax.ShapeDtypeStruct((s_len, 512), BF16)] * 2 + [jax.ShapeDtypeStruct((s_len, D_MODEL), BF16)] * 5,
        compiler_params=_cparams("arbitrary"),
    )


def _rel_bucket_matrix():
    qi = jnp.arange(SWA_BLOCK)[:, None] + SWA_BLOCK
    kj = jnp.arange(2 * SWA_BLOCK)[None, :]
    dist = jnp.maximum(qi - kj, 0)
    max_exact = REL_BUCKETS // 2
    d = jnp.maximum(dist, 1).astype(F32)
    large = max_exact + (jnp.log(d / max_exact) / np.log(REL_MAX_DIST / max_exact)
                         * (REL_BUCKETS - max_exact)).astype(jnp.int32)
    large = jnp.minimum(large, REL_BUCKETS - 1)
    return jnp.where(dist < max_exact, dist, large).astype(jnp.int32)


def _swa_bias_into(bias_ref, bkt_ref, tab_ref):
    bk = bkt_ref[...]
    for h in range(N_HEADS):
        acc = jnp.zeros(bk.shape, F32)
        for bucket in range(REL_BUCKETS):
            acc = jnp.where(bk == bucket, tab_ref[bucket, h], acc)
        bias_ref[h] = acc


def _swa_valid(n):
    shape = (SWA_BLOCK, 2 * SWA_BLOCK)
    row = lax.broadcasted_iota(jnp.int32, shape, 0)
    col = lax.broadcasted_iota(jnp.int32, shape, 1)
    dist = row + SWA_BLOCK - col
    return (dist >= 0) & (dist < SWA_BLOCK) & ((col >= SWA_BLOCK) | (n > 0))


def _swa_windows(kp_ref, kc_ref, vp_ref, vc_ref):
    return (jnp.concatenate([kp_ref[...], kc_ref[...]], axis=0), jnp.concatenate([vp_ref[...], vc_ref[...]], axis=0))


def _swa_place(h):
    return slice(h // 2 * LANES, (h // 2 + 1) * LANES), h % 2, h // SWA_GROUP


def _move_half(x, src, dst):
    moved = x if src == dst else pltpu.roll(x, HEAD_DIM, 1)
    in_dst = (lax.broadcasted_iota(jnp.int32, x.shape, 1) >= HEAD_DIM) == bool(dst)
    return jnp.where(in_dst, moved, 0.0)


def _swa_probs(qk, bias, sink, valid):
    lg = jnp.where(valid, qk * Q_SCALE + bias, NEG_BIG)
    m = jnp.maximum(jnp.max(lg, axis=-1, keepdims=True), sink)
    e = jnp.exp(lg - m)
    es = jnp.exp(sink - m)
    inv = 1.0 / (jnp.sum(e, axis=-1, keepdims=True) + es)
    return e * inv, es * inv


def _swa_specs(s_len):
    blk = SWA_BLOCK
    cur = lambda n: (n, 0)
    prev = lambda n: (jnp.maximum(n - 1, 0), 0)
    kvw = SWA_KV_HEADS * HEAD_DIM
    return [pl.BlockSpec(memory_space=pltpu.SMEM), pl.BlockSpec(memory_space=pltpu.SMEM),
            pl.BlockSpec((blk, 2 * blk), lambda n: (0, 0)),
            pl.BlockSpec((blk, N_HEADS * HEAD_DIM), cur),
            pl.BlockSpec((blk, kvw), prev), pl.BlockSpec((blk, kvw), cur),
            pl.BlockSpec((blk, kvw), prev), pl.BlockSpec((blk, kvw), cur)]


def _swa_fwd(tab, sinks, bkt, q, k, v):
    s_len = q.shape[0]
    blk = SWA_BLOCK

    def body(tab_ref, sink_ref, bkt_ref, q_ref, kp_ref, kc_ref, vp_ref, vc_ref, o_ref, bias_ref):
        n = pl.program_id(0)

        @pl.when(n == 0)
        def _():
            _swa_bias_into(bias_ref, bkt_ref, tab_ref)

        valid = _swa_valid(n)
        kk, vv = _swa_windows(kp_ref, kc_ref, vp_ref, vc_ref)
        st = {}

        def s_logits(h):
            tile, mine, kv = _swa_place(h)
            st[h, "lg"] = _dot_nt(_move_half(q_ref[:, tile].astype(F32), mine, kv).astype(BF16), kk)

        def s_probs(h):
            st[h, "p"] = _swa_probs(st.pop((h, "lg")), bias_ref[h], sink_ref[0, h], valid)[0].astype(BF16)

        def s_values(h):
            tile, mine, kv = _swa_place(h)
            part = _move_half(_dot(st.pop((h, "p")), vv), kv, mine)
            if mine == 0:
                st[h + 1, "o"] = part
            else:
                o_ref[:, tile] = (st.pop((h, "o")) + part).astype(BF16)

        _emit_skewed((list(range(N_HEADS)), [s_logits, s_probs, s_values]))

    return pl.pallas_call(
        body, name="swa_fwd",
        grid=(s_len // blk,),
        in_specs=_swa_specs(s_len),
        out_specs=pl.BlockSpec((blk, N_HEADS * HEAD_DIM), lambda n: (n, 0)),
        out_shape=jax.ShapeDtypeStruct((s_len, N_HEADS * HEAD_DIM), BF16),
        scratch_shapes=[pltpu.VMEM((N_HEADS, blk, 2 * blk), F32)],
        compiler_params=_cparams("arbitrary"),
    )(tab, sinks, bkt, q, k, k, v, v)


def _swa_bwd(tab, sinks, bkt, q, k, v, do, comm=None):
    s_len = q.shape[0]
    blk = SWA_BLOCK
    nb = s_len // blk
    kvw = SWA_KV_HEADS * HEAD_DIM

    def body(tab_ref, sink_ref, bkt_ref, q_ref, kp_ref, kc_ref, vp_ref, vc_ref, do_ref,
             dq_ref, dk_ref, dv_ref, dtab_ref, dsink_ref, bias_ref, dbias_ref):
        n = pl.program_id(0)

        @pl.when(n == 0)
        def _():
            _swa_bias_into(bias_ref, bkt_ref, tab_ref)
            dbias_ref[...] = jnp.zeros_like(dbias_ref)
            dk_ref[...] = jnp.zeros_like(dk_ref)
            dv_ref[...] = jnp.zeros_like(dv_ref)
            dsink_ref[...] = jnp.zeros_like(dsink_ref)
            dtab_ref[...] = jnp.zeros_like(dtab_ref)

        valid = _swa_valid(n)
        cur_rows = pl.ds(pl.multiple_of(n * blk, blk), blk)
        prev_rows = pl.ds(pl.multiple_of(jnp.maximum(n - 1, 0) * blk, blk), blk)
        kk, vv = _swa_windows(kp_ref, kc_ref, vp_ref, vc_ref)
        st = {}

        def s_logits(h):
            tile, mine, kv = _swa_place(h)
            st[h, "q"] = _move_half(q_ref[:, tile].astype(F32), mine, kv).astype(BF16)
            st[h, "do"] = _move_half(do_ref[:, tile].astype(F32), mine, kv).astype(BF16)
            st[h, "lg"] = _dot_nt(st[h, "q"], kk)
            st[h, "dp"] = _dot_nt(st[h, "do"], vv)

        def s_probs(h):
            p, ps = _swa_probs(st.pop((h, "lg")), bias_ref[h], sink_ref[0, h], valid)
            dp = st.pop((h, "dp"))
            delta = jnp.sum(p * dp, axis=-1, keepdims=True)
            dl = p * (dp - delta)
            dsink_ref[h:h + 1, :] += jnp.broadcast_to(-jnp.sum(ps * delta, axis=0, keepdims=True), (1, LANES))
            dbias_ref[h] += dl
            st[h, "dl"], st[h, "p"] = dl.astype(BF16), p.astype(BF16)

        def s_products(h):
            tile, mine, kv = _swa_place(h)
            dlb = st.pop((h, "dl"))
            part = _move_half(Q_SCALE * _dot(dlb, kk), kv, mine)
            if mine == 0:
                st[h + 1, "dq"] = part
            else:
                dq_ref[:, tile] = (st.pop((h, "dq")) + part).astype(BF16)
            dk_win = Q_SCALE * _dot_tn(dlb, st.pop((h, "q")))
            dv_win = _dot_tn(st.pop((h, "p")), st.pop((h, "do")))
            dk_ref[prev_rows, :] += dk_win[:blk]
            dv_ref[prev_rows, :] += dv_win[:blk]
            dk_ref[cur_rows, :] += dk_win[blk:]
            dv_ref[cur_rows, :] += dv_win[blk:]

        _emit_skewed((list(range(N_HEADS)), [s_logits, s_probs, s_products]))

        @pl.when(n == nb - 1)
        def _():
            bk = bkt_ref[...]
            lane = lax.broadcasted_iota(jnp.int32, (1, LANES), 1)
            for bucket in range(REL_BUCKETS):
                rowv = jnp.zeros((1, LANES), F32)
                for h in range(N_HEADS):
                    val = jnp.sum(jnp.where(bk == bucket, dbias_ref[h], 0.0), axis=1, keepdims=True)
                    val = jnp.sum(val, axis=0, keepdims=True)
                    rowv = jnp.where(lane == h, val, rowv)
                dtab_ref[bucket:bucket + 1, :] = rowv

    return _call(
        body, (tab, sinks, bkt, q, k, k, v, v, do), comm=comm, **_grid_ends(nb), name="swa_bwd",
        grid=(nb,),
        in_specs=_swa_specs(s_len) + [pl.BlockSpec((blk, N_HEADS * HEAD_DIM), lambda n: (n, 0))],
        out_specs=[pl.BlockSpec((blk, N_HEADS * HEAD_DIM), lambda n: (n, 0)),
                   pl.BlockSpec((s_len, kvw), lambda n: (0, 0)), pl.BlockSpec((s_len, kvw), lambda n: (0, 0)),
                   pl.BlockSpec((REL_BUCKETS, LANES), lambda n: (0, 0)), pl.BlockSpec((N_HEADS, LANES), lambda n: (0, 0))],
        out_shape=[jax.ShapeDtypeStruct((s_len, N_HEADS * HEAD_DIM), BF16),
                   jax.ShapeDtypeStruct((s_len, kvw), F32), jax.ShapeDtypeStruct((s_len, kvw), F32),
                   jax.ShapeDtypeStruct((REL_BUCKETS, LANES), F32), jax.ShapeDtypeStruct((N_HEADS, LANES), F32)],
        scratch_shapes=[pltpu.VMEM((N_HEADS, blk, 2 * blk), F32), pltpu.VMEM((N_HEADS, blk, 2 * blk), F32)],
        compiler_params=_cparams("arbitrary"),
    )


def _sb_terms(z, valid):
    zc = jnp.minimum(z, SB_LOGIT_CAP)
    lk = -jnp.log(1.0 + jnp.exp(zc))
    lsz = zc + lk
    return lsz, (lk if valid is None else jnp.where(valid, lk, 0.0))


def _bf16_parts(vals):
    parts, rest = [], vals
    for n in range(SB_SUM_PARTS):
        parts.append(rest.astype(BF16))
        if n + 1 < SB_SUM_PARTS:
            rest = rest - parts[-1].astype(F32)
    return parts[0] if len(parts) == 1 else jnp.concatenate(parts, axis=1)


def _row_sum_lanes(vals):
    return jnp.broadcast_to(jnp.sum(vals, axis=-1, keepdims=True), (vals.shape[0], LANES))


def _emit_skewed(*groups):
    for step in range(max(len(items) + len(stages) - 1 for items, stages in groups)):
        for items, stages in groups:
            for s, stage in enumerate(stages):
                if 0 <= step - s < len(items) and items[step - s] is not None:
                    stage(items[step - s])


def _sb_items(edge):
    items = []
    for h in range(2):
        for r0 in range(0, SB_QUERIES, SB_ROWS):
            if edge is None or r0 >= (edge + 1) * SB_KEYS:
                items.append((h, r0, False))
            else:
                items.append((h, r0, True) if r0 + SB_ROWS - 1 > edge * SB_KEYS else None)
    return items


def _sb_valid(w, edge):
    row = lax.broadcasted_iota(jnp.int32, (SB_ROWS, SB_KEYS), 0) + w[1]
    col = lax.broadcasted_iota(jnp.int32, (SB_ROWS, SB_KEYS), 1) + edge * SB_KEYS
    return col < row


def _sb_consts(tq, tk):
    low = lax.broadcasted_iota(jnp.int32, (tq, LANES), 1) < HEAD_DIM
    row = lax.broadcasted_iota(jnp.int32, (tk, tk), 0)
    col = lax.broadcasted_iota(jnp.int32, (tk, tk), 1)
    right = (row > col).astype(BF16)
    left = (row < col).astype(BF16)
    return low, jnp.concatenate([right] * SB_SUM_PARTS, axis=0), jnp.concatenate([left] * SB_SUM_PARTS, axis=0)


def _sb_fwd(q, k, v, comm=None):
    s_len = q.shape[0]
    tq, tk, tr = SB_QUERIES, SB_KEYS, SB_ROWS
    nk, ratio = s_len // tk, tq // tk
    assert nk <= LANES

    def body(q_ref, k_ref, v_ref, o_ref, car_ref, c_ref, oacc_ref, logw_ref, lksum_ref):
        i = pl.program_id(1)
        qv = q_ref[...]
        low, tri2, _ = _sb_consts(tq, tk)
        lane = lax.broadcasted_iota(jnp.int32, (tr, LANES), 1)
        zero = jnp.zeros_like(qv)
        q_heads = (jnp.where(low, qv, zero), jnp.where(low, zero, qv))
        c_ref[...] = jnp.zeros_like(c_ref)
        oacc_ref[...] = jnp.zeros_like(oacc_ref)
        car_ref[...] = jnp.full_like(car_ref, NEG_BIG)

        def front(j, edge):
            keys = k_ref[pl.ds(pl.multiple_of(j * tk, tk), tk), :]
            slot = j % SB_SLOTS
            st = {}

            def s_logits(w):
                st[w, "z"] = _dot_nt(q_heads[w[0]][w[1]:w[1] + tr], keys)

            def s_terms(w):
                valid = _sb_valid(w, edge) if w[2] else None
                lsz, lk = _sb_terms(st.pop((w, "z")), valid)
                st[w, "parts"] = _bf16_parts(lk)
                st[w, "lsz"] = lsz if valid is None else jnp.where(valid, lsz, NEG_BIG)
                lksum_ref[slot, w[0], w[1]:w[1] + tr, :] = _row_sum_lanes(lk)

            def s_suffix(w):
                logw_ref[slot, w[0], w[1]:w[1] + tr, :] = st.pop((w, "lsz")) + _dot(st.pop((w, "parts")), tri2)

            return _sb_items(edge), [s_logits, s_terms, s_suffix]

        def back(j, edge):
            vv = v_ref[pl.ds(pl.multiple_of(j * tk, tk), tk), :]
            slot = j % SB_SLOTS
            st = {}

            def s_weights(w):
                h, rs = w[0], slice(w[1], w[1] + tr)
                c = c_ref[h, rs, :]
                st[w, "a"] = jnp.exp(logw_ref[slot, h, rs, :] + jnp.tile(c, (1, tk // LANES))).astype(BF16)
                car_ref[h, rs, :] = jnp.where(lane == j, c, car_ref[h, rs, :])
                c_ref[h, rs, :] = c + lksum_ref[slot, h, rs, :]

            def s_values(w):
                oacc_ref[w[0], w[1]:w[1] + tr, :] += _dot(st.pop((w, "a")), vv)

            return _sb_items(edge), [s_weights, s_values]

        first = i * ratio
        edge_tiles = [(first + m, m) for m in reversed(range(ratio))]

        def alive():
            return (jnp.max(c_ref[...]) >= SB_DEAD_CARRY).astype(jnp.int32)

        @pl.when(i == 0)
        def _():
            _emit_skewed(*[front(j, m) for j, m in edge_tiles])
            _emit_skewed(*[back(j, m) for j, m in edge_tiles])

        @pl.when(i > 0)
        def _():
            tiles = edge_tiles + [(first - 1, None)]
            _emit_skewed(*[front(j, m) for j, m in tiles])
            _emit_skewed(*[back(j, m) for j, m in tiles])

            @pl.when((alive() > 0) & (first >= 2))
            def _():
                _emit_skewed(front(first - 2, None))

                def step(state):
                    pending, _ = state
                    _emit_skewed(front(pending - 1, None), back(pending, None))
                    return pending - 1, alive()

                pending, live = lax.while_loop(lambda s: (s[0] > 0) & (s[1] > 0), step, (first - 2, jnp.int32(1)))

                @pl.when(live > 0)
                def _():
                    _emit_skewed(back(pending, None))

        o_ref[...] = jnp.where(low, oacc_ref[0], oacc_ref[1]).astype(BF16)

    return _call(
        body, (q, k, v), comm=comm, **_grid_ends(N_HEADS // 2, s_len // tq), name="sb_fwd",
        grid=(N_HEADS // 2, s_len // tq),
        in_specs=[pl.BlockSpec((tq, LANES), lambda p, i: (i, p)),
                  pl.BlockSpec((s_len, LANES), lambda p, i: (0, p)),
                  pl.BlockSpec((s_len, LANES), lambda p, i: (0, p))],
        out_specs=[pl.BlockSpec((tq, LANES), lambda p, i: (i, p)), pl.BlockSpec((2, tq, LANES), lambda p, i: (p, i, 0))],
        out_shape=[jax.ShapeDtypeStruct((s_len, N_HEADS * HEAD_DIM), BF16),
                   jax.ShapeDtypeStruct((N_HEADS, s_len, LANES), F32)],
        scratch_shapes=[pltpu.VMEM((2, tq, LANES), F32), pltpu.VMEM((2, tq, LANES), F32),
                        pltpu.VMEM((SB_SLOTS, 2, tq, tk), F32), pltpu.VMEM((SB_SLOTS, 2, tq, LANES), F32)],
        compiler_params=_cparams("arbitrary", "arbitrary"),
    )


def _sb_bwd(q, k, v, do, cars):
    s_len = q.shape[0]
    tq, tk, tr = SB_QUERIES, SB_KEYS, SB_ROWS
    nk, ratio = s_len // tk, tq // tk

    def body(q_ref, k_ref, v_ref, do_ref, car_ref, dq_ref, dk_ref, dv_ref,
             gleft_ref, dqacc_ref, dkacc_ref, dvacc_ref, logw_ref, lsz_ref, da_ref, a_ref, dz_ref):
        i = pl.program_id(1)

        @pl.when(i == 0)
        def _():
            dkacc_ref[...] = jnp.zeros_like(dkacc_ref)
            dvacc_ref[...] = jnp.zeros_like(dvacc_ref)

        qv = q_ref[...]
        dov = do_ref[...]
        low, tri_right2, tri_left2 = _sb_consts(tq, tk)
        lane = lax.broadcasted_iota(jnp.int32, (tr, LANES), 1)
        zero = jnp.zeros_like(qv)
        q_heads = (jnp.where(low, qv, zero), jnp.where(low, zero, qv))
        do_heads = (jnp.where(low, dov, zero), jnp.where(low, zero, dov))
        q_t = qv.astype(F32).T.astype(BF16)
        do_t = dov.astype(F32).T.astype(BF16)
        gleft_ref[...] = jnp.zeros_like(gleft_ref)
        dqacc_ref[...] = jnp.zeros_like(dqacc_ref)

        def front(j, edge):
            key_rows = pl.ds(pl.multiple_of(j * tk, tk), tk)
            keys, values = k_ref[key_rows, :], v_ref[key_rows, :]
            slot = j % SB_SLOTS
            st = {}

            def s_logits(w):
                h, rs = w[0], slice(w[1], w[1] + tr)
                st[w, "z"] = _dot_nt(q_heads[h][rs], keys)
                da_ref[slot, h, rs, :] = _dot_nt(do_heads[h][rs], values)

            def s_terms(w):
                h, rs = w[0], slice(w[1], w[1] + tr)
                valid = _sb_valid(w, edge) if w[2] else None
                lsz, lk = _sb_terms(st.pop((w, "z")), valid)
                st[w, "parts"] = _bf16_parts(lk)
                lsz = lsz if valid is None else jnp.where(valid, lsz, NEG_BIG)
                lsz_ref[slot, h, rs, :] = lsz
                st[w, "lszc"] = lsz + jnp.sum(jnp.where(lane == j, car_ref[h, rs, :], 0.0), axis=-1, keepdims=True)

            def s_suffix(w):
                logw_ref[slot, w[0], w[1]:w[1] + tr, :] = st.pop((w, "lszc")) + _dot(st.pop((w, "parts")), tri_right2)

            return _sb_items(edge), [s_logits, s_terms, s_suffix]

        def back(j, edge):
            kv = k_ref[pl.ds(pl.multiple_of(j * tk, tk), tk), :]
            slot = j % SB_SLOTS
            st = {}

            items = _sb_items(edge)
            head_rows = [[w[1] for w in items if w is not None and w[0] == h] for h in range(2)]

            def s_weights(w):
                h, rs = w[0], slice(w[1], w[1] + tr)
                a = jnp.exp(logw_ref[slot, h, rs, :])
                g = a * da_ref[slot, h, rs, :]
                a_ref[slot, h, rs, :] = a.astype(BF16)
                st[w, "g"], st[w, "parts"] = g, _bf16_parts(g)

            def s_prefix(w):
                st[w, "gs"] = _dot(st.pop((w, "parts")), tri_left2)

            def s_dz(w):
                h, rs = w[0], slice(w[1], w[1] + tr)
                g = st.pop((w, "g"))
                gleft = gleft_ref[h, rs, :]
                gsum = st.pop((w, "gs")) + jnp.tile(gleft, (1, tk // LANES))
                dz = (g - jnp.exp(lsz_ref[slot, h, rs, :]) * (g + gsum)).astype(BF16)
                st[w, "dz"] = dz
                dz_ref[slot, h, rs, :] = dz
                gleft_ref[h, rs, :] = gleft + _row_sum_lanes(g)

            def s_products(w):
                h, rs = w[0], slice(w[1], w[1] + tr)
                dqacc_ref[h, rs, :] += _dot(st.pop((w, "dz")), kv)
                if w[1] == head_rows[h][-1]:
                    feat = slice(h * HEAD_DIM, (h + 1) * HEAD_DIM)
                    hr = slice(head_rows[h][0], tq)
                    dkacc_ref[j, feat, :] += _dot(q_t[feat, hr], dz_ref[slot, h, hr, :])
                    dvacc_ref[j, feat, :] += _dot(do_t[feat, hr], a_ref[slot, h, hr, :])

            return items, [s_weights, s_prefix, s_dz, s_products]

        first = i * ratio
        tile_max = jnp.max(jnp.maximum(car_ref[0], car_ref[1]), axis=0, keepdims=True)
        start = jnp.clip(first + ratio - jnp.sum(jnp.where(tile_max >= SB_DEAD_CARRY, 1, 0)), 0, first)

        edge_tiles = [(first + m, m) for m in range(ratio)]

        @pl.when(start == first)
        def _():
            _emit_skewed(*[front(j, m) for j, m in edge_tiles])
            _emit_skewed(*[back(j, m) for j, m in edge_tiles])

        @pl.when(start == first - 1)
        def _():
            tiles = [(first - 1, None)] + edge_tiles
            _emit_skewed(*[front(j, m) for j, m in tiles])
            _emit_skewed(*[back(j, m) for j, m in tiles])

        @pl.when(start < first - 1)
        def _():
            _emit_skewed(front(start, None))

            def step(jj, carry):
                _emit_skewed(front(jj, None), back(jj - 1, None))
                return carry

            lax.fori_loop(start + 1, first, step, 0)
            _emit_skewed(front(first, 0), back(first - 1, None))
            for m in range(1, ratio):
                _emit_skewed(front(first + m, m), back(first + m - 1, m - 1))
            _emit_skewed(back(first + ratio - 1, ratio - 1))

        dq_ref[...] = (Q_SCALE * jnp.where(low, dqacc_ref[0], dqacc_ref[1])).astype(BF16)

        @pl.when(i == s_len // tq - 1)
        def _():
            for j in range(nk):
                dk_ref[j * tk:(j + 1) * tk, :] = dkacc_ref[j].T.astype(BF16)
                dv_ref[j * tk:(j + 1) * tk, :] = dvacc_ref[j].T.astype(BF16)

    qblk = pl.BlockSpec((tq, LANES), lambda p, i: (i, p))
    col_full = pl.BlockSpec((s_len, LANES), lambda p, i: (0, p))
    return pl.pallas_call(
        body, name="sb_bwd",
        grid=(N_HEADS // 2, s_len // tq),
        in_specs=[qblk, col_full, col_full, qblk, pl.BlockSpec((2, tq, LANES), lambda p, i: (p, i, 0))],
        out_specs=[qblk, col_full, col_full],
        out_shape=[jax.ShapeDtypeStruct((s_len, N_HEADS * HEAD_DIM), BF16)] * 3,
        scratch_shapes=[pltpu.VMEM((2, tq, LANES), F32), pltpu.VMEM((2, tq, LANES), F32),
                        pltpu.VMEM((nk, LANES, tk), F32), pltpu.VMEM((nk, LANES, tk), F32)]
        + [pltpu.VMEM((SB_SLOTS, 2, tq, tk), F32)] * 3 + [pltpu.VMEM((SB_SLOTS, 2, tq, tk), BF16)] * 2,
        compiler_params=_cparams("parallel", "arbitrary"),
    )(q, k, v, do, cars)


def _local_step(xs, tgt, gains, sinks, rel_bias, weights_of, ship):
    g1, gmix, g2, gfin = gains
    bkt = _rel_bucket_matrix()
    grads = {}

    def carried(outs, comm, count):
        return outs[:count], (list(outs[count:]) if comm is not None else None)

    wts = dict(weights_of(0, None))
    comm = ship("weights", 1)
    (x1, h1, a1, b1, u1), landed = carried(
        _ffn_fwd(xs, g1, wts["ffn1_w1t"], wts["ffn1_w3t"], wts["ffn1_w2"], "1", comm), comm, 5)
    wts.update(weights_of(1, landed))
    hm, qa, ka, va, qb, kb, vb, ga, gb = _proj_fwd(x1, gmix, wts["w_int"])
    oa = _swa_fwd(rel_bias, sinks, bkt, qa, ka, va)
    comm = ship("weights", 2)
    (ob, cars), landed = carried(_sb_fwd(qb, kb, vb, comm), comm, 2)
    wts.update(weights_of(2, landed))
    x2, mg = _merge_fwd(x1, oa, ob, ga, gb, wts["w_swa"], wts["w_sb"], wts["w_out"])
    dx3, h3, a3, b3, u3, loss, dgfin = _ffn_fwd(x2, g2, wts["ffn2_w1t"], wts["ffn2_w3t"], wts["ffn2_w2"], "2",
                                                loss=(tgt, gfin))

    def grad_chain(items, halve_last=False):
        prev = None
        for name, lhs, rhs in items:
            comm = None if prev is None else ship("grads", (prev[0],), prev[1])
            halves = halve_last and name == items[-1][0] and comm is not None
            res = _matmul_tn_tiled(lhs, rhs, name + "_left" if halves else name, comm, (0, 2) if halves else (0, 1))
            if prev is not None:
                grads[(prev[0],)] = prev[1] if comm is None else res[1]
            prev = (name, {_GRAD_KEY[name]: res if comm is None else res[0]})
        if halves:
            right, landed = _matmul_tn_tiled(lhs, rhs, name + "_right", ship("grads", (name,), prev[1]), (1, 2))
            prev = (name, [landed, {_GRAD_KEY[name]: right}])
        return prev

    dx2, dg2, da3, db3, dx3b = _ffn_bwd(dx3, x2, g2, a3, b3, wts["ffn2_w1t"], wts["ffn2_w3t"], wts["ffn2_w2"], "2")
    last = grad_chain((("ffn2_w1", da3, h3), ("ffn2_w3", db3, h3), ("ffn2_w2", u3, dx3b)))
    comm = ship("grads", (last[0],), last[1])
    (doa, dob, dga, dgb, dpa, dpb, dx2b), landed = carried(
        _merge_bwd(dx2, oa, ob, ga, gb, wts["w_swa"], wts["w_sb"], wts["w_out"], comm), comm, 7)
    grads[(last[0],)] = last[1] if comm is None else landed[0]

    def keep(names, big, comm, landed):
        for i, name in enumerate(names):
            grads[(name,)] = {_GRAD_KEY[name]: big[_GRAD_KEY[name]]} if comm is None else landed[i]

    big = {"w_out": _matmul_tn(mg, dx2b, "w_out"), "w_swa": _matmul_tn(oa, dpa, "w_swa"),
           "w_sb": _matmul_tn(ob, dpb, "w_sb")}
    comm = ship("grads", GROUPS[1][1:], big)
    (dqa, dka, dva, dtab, dsink), landed = carried(_swa_bwd(rel_bias, sinks, bkt, qa, ka, va, doa, comm), comm, 5)
    keep(GROUPS[1][1:], big, comm, landed)
    dqb, dkb, dvb = _sb_bwd(qb, kb, vb, dob, cars)
    dpieces = (dqa, dka.astype(BF16), dva.astype(BF16), dqb, dkb, dvb, dga, dgb)
    big = {"w_int": _matmul_tn_stacked(dpieces, hm, "w_in")}
    comm = ship("grads", GROUPS[1][:1], big)
    (dx1, dgmix), landed = carried(_proj_bwd(dpieces, dx2, x1, gmix, wts["w_int"], comm), comm, 2)
    keep(GROUPS[1][:1], big, comm, landed)

    dx0, dg1, da1, db1, dx1b = _ffn_bwd(dx1, xs, g1, a1, b1, wts["ffn1_w1t"], wts["ffn1_w3t"], wts["ffn1_w2"], "1")

    last = grad_chain((("ffn1_w1", da1, h1), ("ffn1_w3", db1, h1), ("ffn1_w2", u1, dx1b)), halve_last=True)
    grads[(last[0],)] = last[1]

    small = {"gains": (dg1, dgmix, dg2, dgfin), "sinks": dsink[:, 0], "rel_bias": dtab[:, :N_HEADS]}
    return loss, dx0, small, grads


def _my_place():
    return lax.axis_index("x"), lax.axis_index("y"), lax.axis_index("c")


def _flip(v, bit):
    return 1 - v if bit else v


_RELATIONS = tuple((k >> 2 & 1, k >> 1 & 1, k & 1) for k in range(1, N_DEV))


def _gather_weights(blocks, tag):
    count = len(blocks)

    def body(*refs):
        x_refs, out_refs = refs[:count], refs[count:2 * count]
        send_sems, recv_sems, local_sems = refs[2 * count:]
        x, y, c = _my_place()
        me, sibling = (x, y, c), (x, y, 1 - c)
        chips = [(1 - x, y), (x, 1 - y), (1 - x, 1 - y)]

        def rows(s, px, py, pc):
            return out_refs[s].at[4 * px + 2 * py + pc]

        def copy(s, k, block, to, src=None):
            return pltpu.make_async_remote_copy(
                src_ref=rows(s, *block) if src is None else src, dst_ref=rows(s, *block),
                send_sem=send_sems.at[s, k], recv_sem=recv_sems.at[s, k],
                device_id=to, device_id_type=pl.DeviceIdType.MESH)

        mine = [pltpu.make_async_copy(x_refs[s], rows(s, *me), local_sems.at[s]) for s in range(count)]
        first, passed = [], []
        for s in range(count):
            mine[s].start()
            first.append(copy(s, 0, me, sibling, src=x_refs[s]))
            first += [copy(s, 1 + j, me, (*chip, c), src=x_refs[s]) for j, chip in enumerate(chips)]
        for cp in first:
            cp.start()
        for s in range(count):
            for j, chip in enumerate(chips):
                copy(s, 1 + j, (*chip, c), me).wait_recv()
                passed.append(copy(s, 4 + j, (*chip, c), sibling))
                passed[-1].start()
        for s in range(count):
            copy(s, 0, sibling, me).wait_recv()
            for j, chip in enumerate(chips):
                copy(s, 4 + j, (*chip, 1 - c), me).wait_recv()
        for cp in first + passed:
            cp.wait_send()
        for cp in mine:
            cp.wait()

    anywhere = pl.BlockSpec(memory_space=pl.ANY)
    return pl.pallas_call(
        body, name=f"gather_weights_{tag}",
        out_shape=[jax.ShapeDtypeStruct((N_DEV,) + b.shape, b.dtype) for b in blocks],
        in_specs=[anywhere] * count, out_specs=[anywhere] * count,
        scratch_shapes=[pltpu.SemaphoreType.DMA((count, N_DEV - 1)), pltpu.SemaphoreType.DMA((count, N_DEV - 1)),
                        pltpu.SemaphoreType.DMA((count,))],
    )(*blocks)


def _exchange_grads(gp, tag):
    def body(g_ref, out_ref, send_sems, recv_sems, local_sem):
        x, y, c = _my_place()
        me = 4 * x + 2 * y + c
        mine = pltpu.make_async_copy(g_ref.at[me], out_ref.at[me], local_sem)
        mine.start()
        copies = []
        for k, (fx, fy, fc) in enumerate(_RELATIONS):
            px, py, pc = _flip(x, fx), _flip(y, fy), _flip(c, fc)
            peer = 4 * px + 2 * py + pc
            copies.append((
                pltpu.make_async_remote_copy(
                    src_ref=g_ref.at[peer], dst_ref=out_ref.at[me], send_sem=send_sems.at[k], recv_sem=recv_sems.at[k],
                    device_id=(px, py, pc), device_id_type=pl.DeviceIdType.MESH),
                pltpu.make_async_remote_copy(
                    src_ref=g_ref.at[peer], dst_ref=out_ref.at[peer], send_sem=send_sems.at[k], recv_sem=recv_sems.at[k],
                    device_id=(px, py, pc), device_id_type=pl.DeviceIdType.MESH)))
        for out_cp, _ in copies:
            out_cp.start()
        for _, in_cp in copies:
            in_cp.wait_recv()
        for out_cp, _ in copies:
            out_cp.wait_send()
        mine.wait()

    return pl.pallas_call(
        body, name=f"exchange_grads_{tag}",
        out_shape=jax.ShapeDtypeStruct(gp.shape, gp.dtype),
        in_specs=[pl.BlockSpec(memory_space=pl.ANY)],
        out_specs=pl.BlockSpec(memory_space=pl.ANY),
        scratch_shapes=[pltpu.SemaphoreType.DMA((7,)), pltpu.SemaphoreType.DMA((7,)), pltpu.SemaphoreType.DMA(())],
    )(gp)


def _peers():
    x, y, c = _my_place()
    out = []
    for k, (fx, fy, fc) in enumerate(_RELATIONS):
        px, py, pc = _flip(x, fx), _flip(y, fy), _flip(c, fc)
        out.append((k, (px, py, pc), 4 * px + 2 * py + pc))
    return out, 4 * x + 2 * y + c


def _grid_ends(*grid):
    def first():
        return functools.reduce(lambda a, b: a & b, [pl.program_id(d) == 0 for d in range(len(grid))])

    def last():
        return functools.reduce(lambda a, b: a & b, [pl.program_id(d) == n - 1 for d, n in enumerate(grid)])

    return {"first": first, "last": last}


def _call(body, operands, *, comm=None, first=None, last=None, **kw):
    if comm is None:
        return pl.pallas_call(body, **kw)(*operands)
    in_specs, out_specs, out_shape = list(kw.pop("in_specs")), list(kw.pop("out_specs")), list(kw.pop("out_shape"))
    scratch = list(kw.pop("scratch_shapes", ()))
    n_in, n_out, n_scr, n_src = len(in_specs), len(out_specs), len(scratch), len(comm)

    def wrapped(*refs):
        ins, src_refs = refs[:n_in], refs[n_in:n_in + n_src]
        outs = refs[n_in + n_src:n_in + n_src + n_out]
        land_refs = refs[n_in + n_src + n_out:n_in + 2 * n_src + n_out]
        scr = refs[n_in + 2 * n_src + n_out:n_in + 2 * n_src + n_out + n_scr]
        send_sems, recv_sems, local_sems = refs[n_in + 2 * n_src + n_out + n_scr:]
        peers, me = _peers()
        mine, going, coming = [], [], []
        for s, (_, per_peer) in enumerate(comm):
            src_ref, land_ref = src_refs[s], land_refs[s]
            mine.append(pltpu.make_async_copy(src_ref.at[me] if per_peer else src_ref, land_ref.at[me], local_sems.at[s]))
            for k, where, slab in peers:
                piece = src_ref.at[slab] if per_peer else src_ref
                going.append(pltpu.make_async_remote_copy(
                    src_ref=piece, dst_ref=land_ref.at[me], send_sem=send_sems.at[s, k], recv_sem=recv_sems.at[s, k],
                    device_id=where, device_id_type=pl.DeviceIdType.MESH))
                coming.append(pltpu.make_async_remote_copy(
                    src_ref=piece, dst_ref=land_ref.at[slab], send_sem=send_sems.at[s, k], recv_sem=recv_sems.at[s, k],
                    device_id=where, device_id_type=pl.DeviceIdType.MESH))

        @pl.when(first())
        def _():
            for cp in mine + going:
                cp.start()

        body(*ins, *outs, *scr)

        @pl.when(last())
        def _():
            for cp in coming:
                cp.wait_recv()
            for cp in going:
                cp.wait_send()
            for cp in mine:
                cp.wait()

    anywhere = pl.BlockSpec(memory_space=pl.ANY)
    lands = [jax.ShapeDtypeStruct(src.shape if per_peer else (N_DEV,) + src.shape, src.dtype) for src, per_peer in comm]
    return pl.pallas_call(
        wrapped, in_specs=in_specs + [anywhere] * n_src, out_specs=out_specs + [anywhere] * n_src,
        out_shape=out_shape + lands,
        scratch_shapes=scratch + [pltpu.SemaphoreType.DMA((n_src, N_DEV - 1)), pltpu.SemaphoreType.DMA((n_src, N_DEV - 1)),
                                  pltpu.SemaphoreType.DMA((n_src,))],
        **kw)(*operands, *[src for src, _ in comm])


def _adamw(w, g, m, v):
    m = ADAM_B1 * m + (1.0 - ADAM_B1) * g
    v = ADAM_B2 * v + (1.0 - ADAM_B2) * jnp.square(g)
    m_hat = m / (1.0 - ADAM_B1 ** ADAM_STEP)
    v_hat = v / (1.0 - ADAM_B2 ** ADAM_STEP)
    delta = -ADAM_LR * (m_hat / (jnp.sqrt(v_hat) + ADAM_EPS) + ADAM_WD * w)
    return delta, m, v


def _sum_and_adamw(parts, w, m, v, tr, tag):
    rows = w.shape[0]
    assert rows % tr == 0 and sum(p.shape[2] for p in parts) == D_MODEL

    def body(*refs):
        p_refs = refs[:len(parts)]
        w_ref, m_ref, v_ref, g_out, d_out, m_out, v_out = refs[len(parts):]
        lo = 0
        for p_ref in p_refs:
            cols = slice(lo, lo + p_ref.shape[2])
            lo = cols.stop
            g = p_ref[0].astype(F32)
            for d in range(1, N_DEV):
                g = g + p_ref[d].astype(F32)
            delta, mn, vn = _adamw(w_ref[:, cols], g, m_ref[:, cols], v_ref[:, cols])
            g_out[:, cols] = g
            d_out[:, cols] = delta
            m_out[:, cols] = mn
            v_out[:, cols] = vn

    sp = pl.BlockSpec((tr, D_MODEL), lambda i: (i, 0))
    return pl.pallas_call(
        body, name=f"sum_and_adamw_{tag}",
        grid=(rows // tr,),
        in_specs=[pl.BlockSpec((N_DEV, tr, p.shape[2]), lambda i: (0, i, 0)) for p in parts] + [sp, sp, sp],
        out_specs=[sp] * 4,
        out_shape=[jax.ShapeDtypeStruct(w.shape, F32)] * 4,
        compiler_params=_cparams("parallel"),
    )(*parts, w, m, v)


def _small_allreduce_adamw(part, w, m, v):
    def body(p_ref, w_ref, m_ref, v_ref, g_out, d_out, m_out, v_out, buf, send_sems, recv_sems):
        x, y, c = _my_place()
        me = 4 * x + 2 * y + c
        buf[me] = p_ref[...]
        copies = []
        for k, (fx, fy, fc) in enumerate(_RELATIONS):
            px, py, pc = _flip(x, fx), _flip(y, fy), _flip(c, fc)
            peer = 4 * px + 2 * py + pc
            copies.append((
                pltpu.make_async_remote_copy(
                    src_ref=buf.at[me], dst_ref=buf.at[me], send_sem=send_sems.at[k], recv_sem=recv_sems.at[k],
                    device_id=(px, py, pc), device_id_type=pl.DeviceIdType.MESH),
                pltpu.make_async_remote_copy(
                    src_ref=buf.at[me], dst_ref=buf.at[peer], send_sem=send_sems.at[k], recv_sem=recv_sems.at[k],
                    device_id=(px, py, pc), device_id_type=pl.DeviceIdType.MESH)))
        for out_cp, _ in copies:
            out_cp.start()
        for _, in_cp in copies:
            in_cp.wait_recv()
        for out_cp, _ in copies:
            out_cp.wait_send()
        g = buf[0]
        for d in range(1, N_DEV):
            g = g + buf[d]
        delta, mn, vn = _adamw(w_ref[...], g, m_ref[...], v_ref[...])
        g_out[...] = g
        d_out[...] = delta
        m_out[...] = mn
        v_out[...] = vn

    vm = pl.BlockSpec(memory_space=pltpu.VMEM)
    return pl.pallas_call(
        body, name="small_allreduce_adamw",
        in_specs=[vm] * 4, out_specs=[vm] * 4,
        out_shape=[jax.ShapeDtypeStruct(w.shape, F32)] * 4,
        scratch_shapes=[pltpu.VMEM((N_DEV,) + part.shape, F32),
                        pltpu.SemaphoreType.DMA((7,)), pltpu.SemaphoreType.DMA((7,))],
    )(part, w, m, v)


_TRANSPOSED = ("ffn1_w1", "ffn1_w3", "w_in", "ffn2_w1", "ffn2_w3")
_BRANCH = ("w_branch_swa", "w_branch_sb")


def _pack_shards(t, names):
    parts = []
    for name in names:
        a = t[name][0]
        if name in _TRANSPOSED:
            a = a.T
        elif name in _BRANCH:
            a = a.reshape(64, D_MODEL)
        parts.append(a)
    return jnp.concatenate(parts, axis=0)


def _unpack_shards(p, names):
    out, lo = {}, 0
    for name in names:
        a = p[lo:lo + BIG_ROWS[BIG_NAMES.index(name)]]
        lo += a.shape[0]
        if name in _TRANSPOSED:
            a = a.T
        elif name in _BRANCH:
            a = a.reshape(512, 128)
        out[name] = a[None]
    return out


def _full_weights(zones, names):
    out = {}
    for name, a in zip(names, zones):
        if name in _BRANCH:
            a = a.reshape(N_DEV, 512, 128).transpose(1, 0, 2).reshape(512, D_MODEL)
        out[_GRAD_KEY[name]] = a.reshape(-1, D_MODEL)
    return out


_GRAD_KEY = {"ffn1_w1": "ffn1_w1t", "ffn1_w3": "ffn1_w3t", "ffn1_w2": "ffn1_w2", "w_in": "w_int",
             "w_branch_swa": "w_swa", "w_branch_sb": "w_sb", "w_out": "w_out",
             "ffn2_w1": "ffn2_w1t", "ffn2_w3": "ffn2_w3t", "ffn2_w2": "ffn2_w2"}


def _pack_full_grads(big, names):
    parts = []
    for name in names:
        a = big[_GRAD_KEY[name]]
        if name in _BRANCH:
            a = a.reshape(512, N_DEV, 128).transpose(1, 0, 2)
        parts.append(a.reshape(N_DEV, BIG_ROWS[BIG_NAMES.index(name)], -1).astype(BF16))
    return jnp.concatenate(parts, axis=1)


_SMALL_NAMES = ("norm_ffn1", "norm_mix", "norm_ffn2", "norm_final", "swa_sinks", "rel_bias")


def _pack_small(vals):
    rows = []
    for a in vals:
        a = a.reshape(-1)
        rows.append(jnp.pad(a, (0, D_MODEL - a.shape[0])))
    rows += [jnp.zeros((D_MODEL,), F32)] * (SMALL_ROWS - len(rows))
    return jnp.stack(rows)


def _unpack_small(p):
    return {"norm_ffn1": p[0:1], "norm_mix": p[1:2], "norm_ffn2": p[2:3], "norm_final": p[3],
            "swa_sinks": p[4:5, :N_HEADS], "rel_bias": p[5, :REL_BUCKETS * N_HEADS].reshape(REL_BUCKETS, N_HEADS)}


ALL_NAMES = ("norm_ffn1", "ffn1_w1", "ffn1_w3", "ffn1_w2", "norm_mix", "w_in", "swa_sinks", "rel_bias",
             "w_branch_swa", "w_branch_sb", "w_out", "norm_ffn2", "ffn2_w1", "ffn2_w3", "ffn2_w2", "norm_final")


def kernel(x, norm_ffn1, ffn1_w1, ffn1_w3, ffn1_w2, norm_mix, w_in, swa_sinks, rel_bias, w_branch_swa, w_branch_sb, w_out, norm_ffn2, ffn2_w1, ffn2_w3, ffn2_w2, norm_final, loss_target, m_norm_ffn1, m_ffn1_w1, m_ffn1_w3, m_ffn1_w2, m_norm_mix, m_w_in, m_swa_sinks, m_rel_bias, m_w_branch_swa, m_w_branch_sb, m_w_out, m_norm_ffn2, m_ffn2_w1, m_ffn2_w3, m_ffn2_w2, m_norm_final, v_norm_ffn1, v_ffn1_w1, v_ffn1_w3, v_ffn1_w2, v_norm_mix, v_w_in, v_swa_sinks, v_rel_bias, v_w_branch_swa, v_w_branch_sb, v_w_out, v_norm_ffn2, v_ffn2_w1, v_ffn2_w3, v_ffn2_w2, v_norm_final):
    w = dict(zip(ALL_NAMES, (norm_ffn1, ffn1_w1, ffn1_w3, ffn1_w2, norm_mix, w_in, swa_sinks, rel_bias,
                             w_branch_swa, w_branch_sb, w_out, norm_ffn2, ffn2_w1, ffn2_w3, ffn2_w2, norm_final)))
    m = dict(zip(ALL_NAMES, (m_norm_ffn1, m_ffn1_w1, m_ffn1_w3, m_ffn1_w2, m_norm_mix, m_w_in, m_swa_sinks, m_rel_bias,
                             m_w_branch_swa, m_w_branch_sb, m_w_out, m_norm_ffn2, m_ffn2_w1, m_ffn2_w3, m_ffn2_w2,
                             m_norm_final)))
    v = dict(zip(ALL_NAMES, (v_norm_ffn1, v_ffn1_w1, v_ffn1_w3, v_ffn1_w2, v_norm_mix, v_w_in, v_swa_sinks, v_rel_bias,
                             v_w_branch_swa, v_w_branch_sb, v_w_out, v_norm_ffn2, v_ffn2_w1, v_ffn2_w3, v_ffn2_w2,
                             v_norm_final)))

    def my_blocks(group):
        return [_pack_shards(w, (name,)).astype(BF16) for name in GROUPS[group]]

    gathered0 = _gather_weights(my_blocks(0), "group0")

    def weights_of(group, landed):
        return _full_weights(gathered0 if group == 0 else landed, GROUPS[group])

    def ship(kind, which, grads=None):
        if kind == "weights":
            return [(block, False) for block in my_blocks(which)]
        return [(_pack_full_grads(grads, (name,)), True) for name in which]

    gains = (norm_ffn1, norm_mix, norm_ffn2, norm_final.reshape(1, D_MODEL))
    loss, dx, small, parts = _local_step(x[0], loss_target[0], gains, swa_sinks, rel_bias, weights_of, ship)

    big_outs = [{}, {}, {}, {}]
    for names, tile in zip(SUM_GROUPS, SUM_TILE):
        landed = parts[names] if isinstance(parts[names], list) else [parts[names]]
        landed = [_exchange_grads(_pack_full_grads(z, names), names[0]) if isinstance(z, dict) else z for z in landed]
        res = _sum_and_adamw(landed, _pack_shards(w, names), _pack_shards(m, names), _pack_shards(v, names),
                             tile, names[0])
        for acc, packed in zip(big_outs, res):
            acc.update(_unpack_shards(packed, names))
    g_big, d_big, m_big, v_big = big_outs

    small_part = _pack_small(small["gains"] + (small["sinks"], small["rel_bias"], loss))
    zero = jnp.zeros((1,), F32)
    small_res = _small_allreduce_adamw(
        small_part, _pack_small([w[n] for n in _SMALL_NAMES] + [zero]), _pack_small([m[n] for n in _SMALL_NAMES] + [zero]),
        _pack_small([v[n] for n in _SMALL_NAMES] + [zero]))
    g_sm, d_sm, m_sm, v_sm = (_unpack_small(p) for p in small_res)

    outs = [small_res[0][len(_SMALL_NAMES), 0], dx[None]]
    for big_d, small_d in ((g_big, g_sm), (d_big, d_sm), (m_big, m_sm), (v_big, v_sm)):
        merged = {**big_d, **small_d}
        outs += [merged[n] for n in ALL_NAMES]
    return tuple(outs)
```

```python
import functools

import jax
import jax.numpy as jnp
import numpy as np
from jax import lax
from jax.experimental import pallas as pl
from jax.experimental.pallas import tpu as pltpu

F32 = jnp.float32
BF16 = jnp.bfloat16

D_MODEL = 1024
D_FF = 2816
HEAD_DIM = 64
N_HEADS = 8
SWA_KV_HEADS = 2
SWA_GROUP = 4
SWA_BLOCK = 128
REL_BUCKETS = 32
REL_MAX_DIST = 128
RMS_EPS = 1e-6
NEG_BIG = -1e30
Q_SCALE = HEAD_DIM ** -0.5
LANES = 128

N_DEV = 8

ADAM_LR = 0.001
ADAM_B1 = 0.9
ADAM_B2 = 0.999
ADAM_EPS = 1e-08
ADAM_WD = 0.01
ADAM_STEP = 10

IN_SIZES = (512, 128, 128, 512, 512, 512, 1024, 1024)
IN_OFFS = tuple(int(v) for v in np.cumsum((0,) + IN_SIZES))
IN_W = IN_OFFS[-1]

BIG_NAMES = ("ffn1_w1", "ffn1_w3", "ffn1_w2", "w_in", "w_branch_swa", "w_branch_sb", "w_out",
             "ffn2_w1", "ffn2_w3", "ffn2_w2")
BIG_ROWS = (352, 352, 352, 544, 64, 64, 128, 352, 352, 352)
SMALL_ROWS = 8
GROUPS = (BIG_NAMES[0:3], BIG_NAMES[3:7], BIG_NAMES[7:10])
SUM_GROUPS = tuple((n,) for n in BIG_NAMES)
SUM_TILE = (176, 176, 176, 272, 64, 64, 128, 176, 176, 176)

VMEM_LIMIT = 56 * 1024 * 1024
FFN_PIECES = 2
SB_QUERIES = 512
SB_KEYS = 256
SB_ROWS = 256
SB_SLOTS = 3
SB_SUM_PARTS = 1
SB_LOGIT_CAP = 80.0
SB_DEAD_CARRY = -110.0


def _dot(a, b):
    return jnp.dot(a, b, preferred_element_type=F32)


def _dot_nt(a, b):
    return lax.dot_general(a, b, (((1,), (1,)), ((), ())), preferred_element_type=F32)


def _dot_tn(a, b):
    return lax.dot_general(a, b, (((0,), (0,)), ((), ())), preferred_element_type=F32)


def _cparams(*sem):
    return pltpu.CompilerParams(dimension_semantics=sem, vmem_limit_bytes=VMEM_LIMIT)


def _rms_rstd(xv):
    return lax.rsqrt(jnp.mean(xv * xv, axis=-1, keepdims=True) + RMS_EPS)


def _rms_bwd(dh, xv, r, g):
    xhat = xv * r
    dg = jnp.sum(dh * xhat, axis=0, keepdims=True)
    dxn = dh * g
    dx = r * (dxn - xhat * jnp.mean(dxn * xhat, axis=-1, keepdims=True))
    return dx, dg


def _ff_tile_spec(tm, tf):
    return pl.BlockSpec((1, tm, tf), lambda i, j: (j, i, 0))


def _ffn_fwd(x, g, w1t, w3t, w2, tag, comm=None, loss=None):
    s_len = x.shape[0]
    tm, tf = min(1024, s_len), 256
    nf = D_FF // tf

    n_rows = s_len // tm
    with_loss = loss is not None

    def body(*refs):
        if loss is None:
            x_hbm, g_ref, w1_ref, w3_ref, w2_ref, xo_ref, h_hbm, a_ref, b_ref, u_ref, acc_ref, hs_ref, *ring = refs
            x_buf, in_sems, h_sem = ring
            early = [(x_hbm, x_buf)]
        else:
            (x_hbm, g_ref, w1_ref, w3_ref, w2_ref, t_hbm, gf_ref,
             xo_ref, h_hbm, a_ref, b_ref, u_ref, loss_ref, dgf_ref, acc_ref, hs_ref, *ring) = refs
            x_buf, t_buf, in_sems, h_sem = ring
            early = [(x_hbm, x_buf), (t_hbm, t_buf)]
        i, j = pl.program_id(0), pl.program_id(1)
        slot = i % 2

        def fetch(tile, sl):
            rows = pl.ds(pl.multiple_of(tile * tm, tm), tm)
            return [pltpu.make_async_copy(src.at[rows], buf.at[sl], in_sems.at[k, sl]) for k, (src, buf) in enumerate(early)]

        h_out = pltpu.make_async_copy(hs_ref, h_hbm.at[pl.ds(pl.multiple_of(i * tm, tm), tm)], h_sem)

        @pl.when((j == 0) & (i == 0))
        def _():
            for cp in fetch(0, 0):
                cp.start()

        @pl.when((j == 0) & (i + 1 < n_rows))
        def _():
            for cp in fetch(i + 1, 1 - slot):
                cp.start()

        @pl.when(j == 0)
        def _():
            for cp in fetch(i, slot):
                cp.wait()
            xv = x_buf[slot]
            hs_ref[...] = (xv * _rms_rstd(xv) * g_ref[...]).astype(BF16)
            h_out.start()
            acc_ref[...] = jnp.zeros_like(acc_ref)

        st = {}

        def s_up(rs):
            h = hs_ref[rs, :]
            st[rs.start, "ab"] = (_dot_nt(h, w1_ref[...]), _dot_nt(h, w3_ref[...]))

        def s_act(rs):
            a, b = st.pop((rs.start, "ab"))
            a_ref[0, rs, :] = a.astype(BF16)
            b_ref[0, rs, :] = b.astype(BF16)
            uh = (0.5 * (a * jax.nn.sigmoid(a) * b)).astype(BF16)
            u_ref[0, rs, :] = uh
            st[rs.start, "u"] = uh

        def s_down(rs):
            acc_ref[rs, :] += _dot(st.pop((rs.start, "u")), w2_ref[...])

        _emit_skewed(([slice(r, r + tm // FFN_PIECES) for r in range(0, tm, tm // FFN_PIECES)], [s_up, s_act, s_down]))

        if with_loss:
            @pl.when((i == 0) & (j == 0))
            def _():
                loss_ref[...] = jnp.zeros_like(loss_ref)
                dgf_ref[...] = jnp.zeros_like(dgf_ref)

        @pl.when(j == nf - 1)
        def _():
            h_out.wait()
            xo = x_buf[slot] + acc_ref[...]
            if not with_loss:
                xo_ref[...] = xo
            else:
                gv = gf_ref[...]
                r = _rms_rstd(xo)
                err = xo * r * gv - t_buf[slot]
                loss_ref[...] += 0.5 * jnp.sum(jnp.mean(err * err, axis=-1, keepdims=True), axis=0, keepdims=True)
                dx, dg = _rms_bwd(err * (1.0 / D_MODEL), xo, r, gv)
                xo_ref[...] = dx
                dgf_ref[...] += dg

    row = lambda i, j: (i, 0)
    fixed = lambda i, j: (0, 0)
    anywhere = pl.BlockSpec(memory_space=pl.ANY)
    n_early = 2 if with_loss else 1
    return _call(
        body, (x, g, w1t, w3t, w2) + (tuple(loss) if with_loss else ()), comm=comm, **_grid_ends(n_rows, nf),
        name=f"ffn_fwd_{tag}",
        grid=(n_rows, nf),
        in_specs=[anywhere, pl.BlockSpec((1, D_MODEL), fixed),
                  pl.BlockSpec((tf, D_MODEL), lambda i, j: (j, 0)), pl.BlockSpec((tf, D_MODEL), lambda i, j: (j, 0)),
                  pl.BlockSpec((tf, D_MODEL), lambda i, j: (j, 0))]
        + ([anywhere, pl.BlockSpec((1, D_MODEL), fixed)] if with_loss else []),
        out_specs=[pl.BlockSpec((tm, D_MODEL), row), anywhere] + [_ff_tile_spec(tm, tf)] * 3
        + ([pl.BlockSpec((1, 1), fixed), pl.BlockSpec((1, D_MODEL), fixed)] if with_loss else []),
        out_shape=[jax.ShapeDtypeStruct((s_len, D_MODEL), F32), jax.ShapeDtypeStruct((s_len, D_MODEL), BF16)]
        + [jax.ShapeDtypeStruct((nf, s_len, tf), BF16)] * 3
        + ([jax.ShapeDtypeStruct((1, 1), F32), jax.ShapeDtypeStruct((1, D_MODEL), F32)] if with_loss else []),
        scratch_shapes=[pltpu.VMEM((tm, D_MODEL), F32), pltpu.VMEM((tm, D_MODEL), BF16)]
        + [pltpu.VMEM((2, tm, D_MODEL), F32)] * n_early
        + [pltpu.SemaphoreType.DMA((n_early, 2)), pltpu.SemaphoreType.DMA(())],
        compiler_params=_cparams("arbitrary", "arbitrary"),
    )


def _ffn_bwd(dy, x, g, a, b, w1t, w3t, w2, tag, comm=None):
    s_len = x.shape[0]
    tm, tf = min(1024, s_len), 256
    nf = D_FF // tf

    def body(dy_ref, x_ref, g_ref, a_ref, b_ref, w1_ref, w3_ref, w2_ref,
             dx_ref, dg_ref, da_ref, db_ref, dyb_ref, acc_ref, dys_ref):
        i, j = pl.program_id(0), pl.program_id(1)

        @pl.when(j == 0)
        def _():
            dyb = dy_ref[...].astype(BF16)
            dys_ref[...] = 0.5 * dyb
            dyb_ref[...] = dyb
            acc_ref[...] = jnp.zeros_like(acc_ref)

        @pl.when((i == 0) & (j == 0))
        def _():
            dg_ref[...] = jnp.zeros_like(dg_ref)

        st = {}

        def s_du(rs):
            st[rs.start, "du"] = _dot_nt(dys_ref[rs, :], w2_ref[...])

        def s_act(rs):
            du = st.pop((rs.start, "du"))
            av = a_ref[0, rs, :].astype(F32)
            bv = b_ref[0, rs, :].astype(F32)
            sg = jax.nn.sigmoid(av)
            sil = av * sg
            da = (du * bv * (sg + sil * (1.0 - sg))).astype(BF16)
            db = (du * sil).astype(BF16)
            da_ref[0, rs, :] = da
            db_ref[0, rs, :] = db
            st[rs.start, "dab"] = (da, db)

        def s_dh(rs):
            da, db = st.pop((rs.start, "dab"))
            acc_ref[rs, :] += _dot(da, w1_ref[...]) + _dot(db, w3_ref[...])

        _emit_skewed(([slice(r, r + tm // FFN_PIECES) for r in range(0, tm, tm // FFN_PIECES)], [s_du, s_act, s_dh]))

        @pl.when(j == nf - 1)
        def _():
            xv = x_ref[...]
            dx, dg = _rms_bwd(acc_ref[...], xv, _rms_rstd(xv), g_ref[...])
            dx_ref[...] = dy_ref[...] + dx
            dg_ref[...] += dg

    row = lambda i, j: (i, 0)
    wsp = pl.BlockSpec((tf, D_MODEL), lambda i, j: (j, 0))
    return _call(
        body, (dy, x, g, a, b, w1t, w3t, w2), comm=comm, **_grid_ends(s_len // tm, nf), name=f"ffn_bwd_{tag}",
        grid=(s_len // tm, nf),
        in_specs=[pl.BlockSpec((tm, D_MODEL), row), pl.BlockSpec((tm, D_MODEL), row),
                  pl.BlockSpec((1, D_MODEL), lambda i, j: (0, 0)),
                  _ff_tile_spec(tm, tf), _ff_tile_spec(tm, tf), wsp, wsp, wsp],
        out_specs=[pl.BlockSpec((tm, D_MODEL), row), pl.BlockSpec((1, D_MODEL), lambda i, j: (0, 0)),
                   _ff_tile_spec(tm, tf), _ff_tile_spec(tm, tf), pl.BlockSpec((tm, D_MODEL), row)],
        out_shape=[jax.ShapeDtypeStruct((s_len, D_MODEL), F32), jax.ShapeDtypeStruct((1, D_MODEL), F32),
                   jax.ShapeDtypeStruct((nf, s_len, tf), BF16), jax.ShapeDtypeStruct((nf, s_len, tf), BF16),
                   jax.ShapeDtypeStruct((s_len, D_MODEL), BF16)],
        scratch_shapes=[pltpu.VMEM((tm, D_MODEL), F32), pltpu.VMEM((tm, D_MODEL), BF16)],
        compiler_params=_cparams("arbitrary", "arbitrary"),
    )


def _matmul_tn(lhs, rhs, tag, comm=None):
    s_len, m = lhs.shape
    n = rhs.shape[1]
    tm = min(512, s_len)
    tj = m if m <= 1024 else 1408
    assert m % tj == 0
    last_rows = s_len // tm - 1

    def body(l_ref, r_ref, o_ref, acc_ref):
        i = pl.program_id(1)

        @pl.when(i == 0)
        def _():
            acc_ref[...] = jnp.zeros_like(acc_ref)

        acc_ref[...] += _dot_tn(l_ref[...], r_ref[...])

        @pl.when(i == last_rows)
        def _():
            o_ref[...] = acc_ref[...].astype(BF16)

    res = _call(
        body, (lhs, rhs), comm=comm, **_grid_ends(m // tj, s_len // tm), name=f"matmul_tn_{tag}",
        grid=(m // tj, s_len // tm),
        in_specs=[pl.BlockSpec((tm, tj), lambda j, i: (i, j)), pl.BlockSpec((tm, n), lambda j, i: (i, 0))],
        out_specs=[pl.BlockSpec((tj, n), lambda j, i: (j, 0))],
        out_shape=[jax.ShapeDtypeStruct((m, n), BF16)],
        scratch_shapes=[pltpu.VMEM((tj, n), F32)],
        compiler_params=_cparams("arbitrary", "arbitrary"),
    )
    return res[0] if comm is None else tuple(res)


def _matmul_tn_tiled(lhs, rhs, tag, comm=None):
    nf, s_len, tf = lhs.shape
    n = rhs.shape[1]
    tm = min(512, s_len)
    last_rows = s_len // tm - 1

    def body(l_ref, r_ref, o_ref, acc_ref):
        i = pl.program_id(0)

        @pl.when(i == 0)
        def _():
            acc_ref[...] = jnp.zeros_like(acc_ref)

        rv = r_ref[...]
        for t in range(nf):
            acc_ref[t * tf:(t + 1) * tf, :] += _dot_tn(l_ref[t], rv)

        @pl.when(i == last_rows)
        def _():
            o_ref[...] = acc_ref[...].astype(BF16)

    res = _call(
        body, (lhs, rhs), comm=comm, **_grid_ends(s_len // tm), name=f"matmul_tn_{tag}",
        grid=(s_len // tm,),
        in_specs=[pl.BlockSpec((nf, tm, tf), lambda i: (0, i, 0)), pl.BlockSpec((tm, n), lambda i: (i, 0))],
        out_specs=[pl.BlockSpec((nf * tf, n), lambda i: (0, 0))],
        out_shape=[jax.ShapeDtypeStruct((nf * tf, n), BF16)],
        scratch_shapes=[pltpu.VMEM((nf * tf, n), F32)],
        compiler_params=_cparams("arbitrary"),
    )
    return res[0] if comm is None else tuple(res)


def _matmul_tn_stacked(pieces, rhs, tag):
    s_len, n = rhs.shape
    widths = [p.shape[1] for p in pieces]
    offs = [sum(widths[:k]) for k in range(len(widths) + 1)]
    tm = min(256, s_len)
    last_rows = s_len // tm - 1

    def body(*refs):
        l_refs, r_ref, o_ref, acc_ref = refs[:len(pieces)], refs[-3], refs[-2], refs[-1]
        i = pl.program_id(0)

        @pl.when(i == 0)
        def _():
            acc_ref[...] = jnp.zeros_like(acc_ref)

        rv = r_ref[...]
        for k, l_ref in enumerate(l_refs):
            acc_ref[offs[k]:offs[k + 1], :] += _dot_tn(l_ref[...], rv)

        @pl.when(i == last_rows)
        def _():
            o_ref[...] = acc_ref[...].astype(BF16)

    row = lambda i: (i, 0)
    return pl.pallas_call(
        body, name=f"matmul_tn_{tag}",
        grid=(s_len // tm,),
        in_specs=[pl.BlockSpec((tm, w), row) for w in widths] + [pl.BlockSpec((tm, n), row)],
        out_specs=pl.BlockSpec((offs[-1], n), lambda i: (0, 0)),
        out_shape=jax.ShapeDtypeStruct((offs[-1], n), BF16),
        scratch_shapes=[pltpu.VMEM((offs[-1], n), F32)],
        compiler_params=_cparams("arbitrary"),
    )(*pieces, rhs)


def _proj_fwd(x1, g, wint):
    s_len = x1.shape[0]
    tm = min(512, s_len)
    dts = (BF16, BF16, BF16, BF16, BF16, BF16, F32, F32)

    def body(x_ref, g_ref, w_ref, h_ref, *outs):
        xv = x_ref[...]
        h = (xv * _rms_rstd(xv) * g_ref[...]).astype(BF16)
        h_ref[...] = h
        for p, o_ref in enumerate(outs):
            val = _dot_nt(h, w_ref[IN_OFFS[p]:IN_OFFS[p + 1], :])
            if p == 3:
                val = val * Q_SCALE
            o_ref[...] = val.astype(dts[p])

    row = lambda i: (i, 0)
    return pl.pallas_call(
        body, name="proj_fwd",
        grid=(s_len // tm,),
        in_specs=[pl.BlockSpec((tm, D_MODEL), row), pl.BlockSpec((1, D_MODEL), lambda i: (0, 0)),
                  pl.BlockSpec((IN_W, D_MODEL), lambda i: (0, 0))],
        out_specs=[pl.BlockSpec((tm, D_MODEL), row)] + [pl.BlockSpec((tm, w), row) for w in IN_SIZES],
        out_shape=[jax.ShapeDtypeStruct((s_len, D_MODEL), BF16)]
        + [jax.ShapeDtypeStruct((s_len, w), dt) for w, dt in zip(IN_SIZES, dts)],
        compiler_params=_cparams("parallel"),
    )(x1, g, wint)


def _proj_bwd(dpieces, dx2, x1, g, wint, comm=None):
    s_len = x1.shape[0]
    tm = min(512, s_len)

    def body(*refs):
        dps = refs[:8]
        dx2_ref, x_ref, g_ref, w_ref, dx_ref, dg_ref = refs[8:]

        @pl.when(pl.program_id(0) == 0)
        def _():
            dg_ref[...] = jnp.zeros_like(dg_ref)

        dh = _dot(dps[0][...], w_ref[IN_OFFS[0]:IN_OFFS[1], :])
        for p in range(1, 8):
            dh += _dot(dps[p][...], w_ref[IN_OFFS[p]:IN_OFFS[p + 1], :])
        xv = x_ref[...]
        dx, dg = _rms_bwd(dh, xv, _rms_rstd(xv), g_ref[...])
        dx_ref[...] = dx2_ref[...] + dx
        dg_ref[...] += dg

    row = lambda i: (i, 0)
    return _call(
        body, (*dpieces, dx2, x1, g, wint), comm=comm, **_grid_ends(s_len // tm), name="proj_bwd",
        grid=(s_len // tm,),
        in_specs=[pl.BlockSpec((tm, w), row) for w in IN_SIZES]
        + [pl.BlockSpec((tm, D_MODEL), row), pl.BlockSpec((tm, D_MODEL), row),
           pl.BlockSpec((1, D_MODEL), lambda i: (0, 0)), pl.BlockSpec((IN_W, D_MODEL), lambda i: (0, 0))],
        out_specs=[pl.BlockSpec((tm, D_MODEL), row), pl.BlockSpec((1, D_MODEL), lambda i: (0, 0))],
        out_shape=[jax.ShapeDtypeStruct((s_len, D_MODEL), F32), jax.ShapeDtypeStruct((1, D_MODEL), F32)],
        compiler_params=_cparams("arbitrary"),
    )


def _merge_fwd(x1, oa, ob, ga, gb, wswa, wsb, wout):
    s_len = x1.shape[0]
    tm = min(512, s_len)

    def body(x_ref, oa_ref, ob_ref, ga_ref, gb_ref, wa_ref, wb_ref, wo_ref, xo_ref, mg_ref):
        pa = _dot(oa_ref[...], wa_ref[...])
        pb = _dot(ob_ref[...], wb_ref[...])
        mg = (jax.nn.sigmoid(ga_ref[...]) * pa + jax.nn.sigmoid(gb_ref[...]) * pb).astype(BF16)
        mg_ref[...] = mg
        xo_ref[...] = x_ref[...] + _dot(mg, wo_ref[...])

    row = lambda i: (i, 0)
    full = lambda i: (0, 0)
    return pl.pallas_call(
        body, name="merge_fwd",
        grid=(s_len // tm,),
        in_specs=[pl.BlockSpec((tm, D_MODEL), row), pl.BlockSpec((tm, 512), row), pl.BlockSpec((tm, 512), row),
                  pl.BlockSpec((tm, D_MODEL), row), pl.BlockSpec((tm, D_MODEL), row),
                  pl.BlockSpec((512, D_MODEL), full), pl.BlockSpec((512, D_MODEL), full),
                  pl.BlockSpec((D_MODEL, D_MODEL), full)],
        out_specs=[pl.BlockSpec((tm, D_MODEL), row), pl.BlockSpec((tm, D_MODEL), row)],
        out_shape=[jax.ShapeDtypeStruct((s_len, D_MODEL), F32), jax.ShapeDtypeStruct((s_len, D_MODEL), BF16)],
        compiler_params=_cparams("parallel"),
    )(x1, oa, ob, ga, gb, wswa, wsb, wout)


def _merge_bwd(dx2, oa, ob, ga, gb, wswa, wsb, wout, comm=None):
    s_len = dx2.shape[0]
    tm = min(512, s_len)

    def body(dx_ref, oa_ref, ob_ref, ga_ref, gb_ref, wa_ref, wb_ref, wo_ref,
             doa_ref, dob_ref, dga_ref, dgb_ref, dpa_ref, dpb_ref, dxb_ref):
        dxb = dx_ref[...].astype(BF16)
        dxb_ref[...] = dxb
        dmg = _dot_nt(dxb, wo_ref[...])
        for o_ref, g_ref, w_ref, do_ref, dg_ref, dp_ref in (
                (oa_ref, ga_ref, wa_ref, doa_ref, dga_ref, dpa_ref),
                (ob_ref, gb_ref, wb_ref, dob_ref, dgb_ref, dpb_ref)):
            pv = _dot(o_ref[...], w_ref[...])
            sg = jax.nn.sigmoid(g_ref[...])
            dp = (dmg * sg).astype(BF16)
            dp_ref[...] = dp
            dg_ref[...] = (dmg * pv * sg * (1.0 - sg)).astype(BF16)
            do_ref[...] = _dot_nt(dp, w_ref[...]).astype(BF16)

    row = lambda i: (i, 0)
    full = lambda i: (0, 0)
    wide = pl.BlockSpec((tm, D_MODEL), row)
    half = pl.BlockSpec((tm, 512), row)
    return _call(
        body, (dx2, oa, ob, ga, gb, wswa, wsb, wout), comm=comm, **_grid_ends(s_len // tm), name="merge_bwd",
        grid=(s_len // tm,),
        in_specs=[wide, half, half, wide, wide, pl.BlockSpec((512, D_MODEL), full),
                  pl.BlockSpec((512, D_MODEL), full), pl.BlockSpec((D_MODEL, D_MODEL), full)],
        out_specs=[half, half, wide, wide, wide, wide, wide],
        out_shape=[jax.ShapeDtypeStruct((s_len, 512), BF16)] * 2 + [jax.ShapeDtypeStruct((s_len, D_MODEL), BF16)] * 5,
        compiler_params=_cparams("arbitrary"),
    )


def _rel_bucket_matrix():
    qi = jnp.arange(SWA_BLOCK)[:, None] + SWA_BLOCK
    kj = jnp.arange(2 * SWA_BLOCK)[None, :]
    dist = jnp.maximum(qi - kj, 0)
    max_exact = REL_BUCKETS // 2
    d = jnp.maximum(dist, 1).astype(F32)
    large = max_exact + (jnp.log(d / max_exact) / np.log(REL_MAX_DIST / max_exact)
                         * (REL_BUCKETS - max_exact)).astype(jnp.int32)
    large = jnp.minimum(large, REL_BUCKETS - 1)
    return jnp.where(dist < max_exact, dist, large).astype(jnp.int32)


def _swa_bias_into(bias_ref, bkt_ref, tab_ref):
    bk = bkt_ref[...]
    for h in range(N_HEADS):
        acc = jnp.zeros(bk.shape, F32)
        for bucket in range(REL_BUCKETS):
            acc = jnp.where(bk == bucket, tab_ref[bucket, h], acc)
        bias_ref[h] = acc


def _swa_valid(n):
    shape = (SWA_BLOCK, 2 * SWA_BLOCK)
    row = lax.broadcasted_iota(jnp.int32, shape, 0)
    col = lax.broadcasted_iota(jnp.int32, shape, 1)
    dist = row + SWA_BLOCK - col
    return (dist >= 0) & (dist < SWA_BLOCK) & ((col >= SWA_BLOCK) | (n > 0))


def _swa_windows(kp_ref, kc_ref, vp_ref, vc_ref):
    return (jnp.concatenate([kp_ref[...], kc_ref[...]], axis=0), jnp.concatenate([vp_ref[...], vc_ref[...]], axis=0))


def _swa_place(h):
    return slice(h // 2 * LANES, (h // 2 + 1) * LANES), h % 2, h // SWA_GROUP


def _move_half(x, src, dst):
    moved = x if src == dst else pltpu.roll(x, HEAD_DIM, 1)
    in_dst = (lax.broadcasted_iota(jnp.int32, x.shape, 1) >= HEAD_DIM) == bool(dst)
    return jnp.where(in_dst, moved, 0.0)


def _swa_probs(qk, bias, sink, valid):
    lg = jnp.where(valid, qk * Q_SCALE + bias, NEG_BIG)
    m = jnp.maximum(jnp.max(lg, axis=-1, keepdims=True), sink)
    e = jnp.exp(lg - m)
    es = jnp.exp(sink - m)
    inv = 1.0 / (jnp.sum(e, axis=-1, keepdims=True) + es)
    return e * inv, es * inv


def _swa_specs(s_len):
    blk = SWA_BLOCK
    cur = lambda n: (n, 0)
    prev = lambda n: (jnp.maximum(n - 1, 0), 0)
    kvw = SWA_KV_HEADS * HEAD_DIM
    return [pl.BlockSpec(memory_space=pltpu.SMEM), pl.BlockSpec(memory_space=pltpu.SMEM),
            pl.BlockSpec((blk, 2 * blk), lambda n: (0, 0)),
            pl.BlockSpec((blk, N_HEADS * HEAD_DIM), cur),
            pl.BlockSpec((blk, kvw), prev), pl.BlockSpec((blk, kvw), cur),
            pl.BlockSpec((blk, kvw), prev), pl.BlockSpec((blk, kvw), cur)]


def _swa_fwd(tab, sinks, bkt, q, k, v):
    s_len = q.shape[0]
    blk = SWA_BLOCK

    def body(tab_ref, sink_ref, bkt_ref, q_ref, kp_ref, kc_ref, vp_ref, vc_ref, o_ref, bias_ref):
        n = pl.program_id(0)

        @pl.when(n == 0)
        def _():
            _swa_bias_into(bias_ref, bkt_ref, tab_ref)

        valid = _swa_valid(n)
        kk, vv = _swa_windows(kp_ref, kc_ref, vp_ref, vc_ref)
        st = {}

        def s_logits(h):
            tile, mine, kv = _swa_place(h)
            st[h, "lg"] = _dot_nt(_move_half(q_ref[:, tile].astype(F32), mine, kv).astype(BF16), kk)

        def s_probs(h):
            st[h, "p"] = _swa_probs(st.pop((h, "lg")), bias_ref[h], sink_ref[0, h], valid)[0].astype(BF16)

        def s_values(h):
            tile, mine, kv = _swa_place(h)
            part = _move_half(_dot(st.pop((h, "p")), vv), kv, mine)
            if mine == 0:
                st[h + 1, "o"] = part
            else:
                o_ref[:, tile] = (st.pop((h, "o")) + part).astype(BF16)

        _emit_skewed((list(range(N_HEADS)), [s_logits, s_probs, s_values]))

    return pl.pallas_call(
        body, name="swa_fwd",
        grid=(s_len // blk,),
        in_specs=_swa_specs(s_len),
        out_specs=pl.BlockSpec((blk, N_HEADS * HEAD_DIM), lambda n: (n, 0)),
        out_shape=jax.ShapeDtypeStruct((s_len, N_HEADS * HEAD_DIM), BF16),
        scratch_shapes=[pltpu.VMEM((N_HEADS, blk, 2 * blk), F32)],
        compiler_params=_cparams("arbitrary"),
    )(tab, sinks, bkt, q, k, k, v, v)


def _swa_bwd(tab, sinks, bkt, q, k, v, do, comm=None):
    s_len = q.shape[0]
    blk = SWA_BLOCK
    nb = s_len // blk
    kvw = SWA_KV_HEADS * HEAD_DIM

    def body(tab_ref, sink_ref, bkt_ref, q_ref, kp_ref, kc_ref, vp_ref, vc_ref, do_ref,
             dq_ref, dk_ref, dv_ref, dtab_ref, dsink_ref, bias_ref, dbias_ref):
        n = pl.program_id(0)

        @pl.when(n == 0)
        def _():
            _swa_bias_into(bias_ref, bkt_ref, tab_ref)
            dbias_ref[...] = jnp.zeros_like(dbias_ref)
            dk_ref[...] = jnp.zeros_like(dk_ref)
            dv_ref[...] = jnp.zeros_like(dv_ref)
            dsink_ref[...] = jnp.zeros_like(dsink_ref)
            dtab_ref[...] = jnp.zeros_like(dtab_ref)

        valid = _swa_valid(n)
        cur_rows = pl.ds(pl.multiple_of(n * blk, blk), blk)
        prev_rows = pl.ds(pl.multiple_of(jnp.maximum(n - 1, 0) * blk, blk), blk)
        kk, vv = _swa_windows(kp_ref, kc_ref, vp_ref, vc_ref)
        st = {}

        def s_logits(h):
            tile, mine, kv = _swa_place(h)
            st[h, "q"] = _move_half(q_ref[:, tile].astype(F32), mine, kv).astype(BF16)
            st[h, "do"] = _move_half(do_ref[:, tile].astype(F32), mine, kv).astype(BF16)
            st[h, "lg"] = _dot_nt(st[h, "q"], kk)
            st[h, "dp"] = _dot_nt(st[h, "do"], vv)

        def s_probs(h):
            p, ps = _swa_probs(st.pop((h, "lg")), bias_ref[h], sink_ref[0, h], valid)
            dp = st.pop((h, "dp"))
            delta = jnp.sum(p * dp, axis=-1, keepdims=True)
            dl = p * (dp - delta)
            dsink_ref[h:h + 1, :] += jnp.broadcast_to(-jnp.sum(ps * delta, axis=0, keepdims=True), (1, LANES))
            dbias_ref[h] += dl
            st[h, "dl"], st[h, "p"] = dl.astype(BF16), p.astype(BF16)

        def s_products(h):
            tile, mine, kv = _swa_place(h)
            dlb = st.pop((h, "dl"))
            part = _move_half(Q_SCALE * _dot(dlb, kk), kv, mine)
            if mine == 0:
                st[h + 1, "dq"] = part
            else:
                dq_ref[:, tile] = (st.pop((h, "dq")) + part).astype(BF16)
            dk_win = Q_SCALE * _dot_tn(dlb, st.pop((h, "q")))
            dv_win = _dot_tn(st.pop((h, "p")), st.pop((h, "do")))
            dk_ref[prev_rows, :] += dk_win[:blk]
            dv_ref[prev_rows, :] += dv_win[:blk]
            dk_ref[cur_rows, :] += dk_win[blk:]
            dv_ref[cur_rows, :] += dv_win[blk:]

        _emit_skewed((list(range(N_HEADS)), [s_logits, s_probs, s_products]))

        @pl.when(n == nb - 1)
        def _():
            bk = bkt_ref[...]
            lane = lax.broadcasted_iota(jnp.int32, (1, LANES), 1)
            for bucket in range(REL_BUCKETS):
                rowv = jnp.zeros((1, LANES), F32)
                for h in range(N_HEADS):
                    val = jnp.sum(jnp.where(bk == bucket, dbias_ref[h], 0.0), axis=1, keepdims=True)
                    val = jnp.sum(val, axis=0, keepdims=True)
                    rowv = jnp.where(lane == h, val, rowv)
                dtab_ref[bucket:bucket + 1, :] = rowv

    return _call(
        body, (tab, sinks, bkt, q, k, k, v, v, do), comm=comm, **_grid_ends(nb), name="swa_bwd",
        grid=(nb,),
        in_specs=_swa_specs(s_len) + [pl.BlockSpec((blk, N_HEADS * HEAD_DIM), lambda n: (n, 0))],
        out_specs=[pl.BlockSpec((blk, N_HEADS * HEAD_DIM), lambda n: (n, 0)),
                   pl.BlockSpec((s_len, kvw), lambda n: (0, 0)), pl.BlockSpec((s_len, kvw), lambda n: (0, 0)),
                   pl.BlockSpec((REL_BUCKETS, LANES), lambda n: (0, 0)), pl.BlockSpec((N_HEADS, LANES), lambda n: (0, 0))],
        out_shape=[jax.ShapeDtypeStruct((s_len, N_HEADS * HEAD_DIM), BF16),
                   jax.ShapeDtypeStruct((s_len, kvw), F32), jax.ShapeDtypeStruct((s_len, kvw), F32),
                   jax.ShapeDtypeStruct((REL_BUCKETS, LANES), F32), jax.ShapeDtypeStruct((N_HEADS, LANES), F32)],
        scratch_shapes=[pltpu.VMEM((N_HEADS, blk, 2 * blk), F32), pltpu.VMEM((N_HEADS, blk, 2 * blk), F32)],
        compiler_params=_cparams("arbitrary"),
    )


def _sb_terms(z, valid):
    zc = jnp.minimum(z, SB_LOGIT_CAP)
    lk = -jnp.log(1.0 + jnp.exp(zc))
    lsz = zc + lk
    return lsz, (lk if valid is None else jnp.where(valid, lk, 0.0))


def _bf16_parts(vals):
    parts, rest = [], vals
    for n in range(SB_SUM_PARTS):
        parts.append(rest.astype(BF16))
        if n + 1 < SB_SUM_PARTS:
            rest = rest - parts[-1].astype(F32)
    return parts[0] if len(parts) == 1 else jnp.concatenate(parts, axis=1)


def _row_sum_lanes(vals):
    return jnp.broadcast_to(jnp.sum(vals, axis=-1, keepdims=True), (vals.shape[0], LANES))


def _emit_skewed(*groups):
    for step in range(max(len(items) + len(stages) - 1 for items, stages in groups)):
        for items, stages in groups:
            for s, stage in enumerate(stages):
                if 0 <= step - s < len(items) and items[step - s] is not None:
                    stage(items[step - s])


def _sb_items(edge):
    items = []
    for h in range(2):
        for r0 in range(0, SB_QUERIES, SB_ROWS):
            if edge is None or r0 >= (edge + 1) * SB_KEYS:
                items.append((h, r0, False))
            else:
                items.append((h, r0, True) if r0 + SB_ROWS - 1 > edge * SB_KEYS else None)
    return items


def _sb_valid(w, edge):
    row = lax.broadcasted_iota(jnp.int32, (SB_ROWS, SB_KEYS), 0) + w[1]
    col = lax.broadcasted_iota(jnp.int32, (SB_ROWS, SB_KEYS), 1) + edge * SB_KEYS
    return col < row


def _sb_consts(tq, tk):
    low = lax.broadcasted_iota(jnp.int32, (tq, LANES), 1) < HEAD_DIM
    row = lax.broadcasted_iota(jnp.int32, (tk, tk), 0)
    col = lax.broadcasted_iota(jnp.int32, (tk, tk), 1)
    right = (row > col).astype(BF16)
    left = (row < col).astype(BF16)
    return low, jnp.concatenate([right] * SB_SUM_PARTS, axis=0), jnp.concatenate([left] * SB_SUM_PARTS, axis=0)


def _sb_fwd(q, k, v, comm=None):
    s_len = q.shape[0]
    tq, tk, tr = SB_QUERIES, SB_KEYS, SB_ROWS
    nk, ratio = s_len // tk, tq // tk
    assert nk <= LANES

    def body(q_ref, k_ref, v_ref, o_ref, car_ref, c_ref, oacc_ref, logw_ref, lksum_ref):
        i = pl.program_id(1)
        qv = q_ref[...]
        low, tri2, _ = _sb_consts(tq, tk)
        lane = lax.broadcasted_iota(jnp.int32, (tr, LANES), 1)
        zero = jnp.zeros_like(qv)
        q_heads = (jnp.where(low, qv, zero), jnp.where(low, zero, qv))
        c_ref[...] = jnp.zeros_like(c_ref)
        oacc_ref[...] = jnp.zeros_like(oacc_ref)
        car_ref[...] = jnp.full_like(car_ref, NEG_BIG)

        def front(j, edge):
            keys = k_ref[pl.ds(pl.multiple_of(j * tk, tk), tk), :]
            slot = j % SB_SLOTS
            st = {}

            def s_logits(w):
                st[w, "z"] = _dot_nt(q_heads[w[0]][w[1]:w[1] + tr], keys)

            def s_terms(w):
                valid = _sb_valid(w, edge) if w[2] else None
                lsz, lk = _sb_terms(st.pop((w, "z")), valid)
                st[w, "parts"] = _bf16_parts(lk)
                st[w, "lsz"] = lsz if valid is None else jnp.where(valid, lsz, NEG_BIG)
                lksum_ref[slot, w[0], w[1]:w[1] + tr, :] = _row_sum_lanes(lk)

            def s_suffix(w):
                logw_ref[slot, w[0], w[1]:w[1] + tr, :] = st.pop((w, "lsz")) + _dot(st.pop((w, "parts")), tri2)

            return _sb_items(edge), [s_logits, s_terms, s_suffix]

        def back(j, edge):
            vv = v_ref[pl.ds(pl.multiple_of(j * tk, tk), tk), :]
            slot = j % SB_SLOTS
            st = {}

            def s_weights(w):
                h, rs = w[0], slice(w[1], w[1] + tr)
                c = c_ref[h, rs, :]
                st[w, "a"] = jnp.exp(logw_ref[slot, h, rs, :] + jnp.tile(c, (1, tk // LANES))).astype(BF16)
                car_ref[h, rs, :] = jnp.where(lane == j, c, car_ref[h, rs, :])
                c_ref[h, rs, :] = c + lksum_ref[slot, h, rs, :]

            def s_values(w):
                oacc_ref[w[0], w[1]:w[1] + tr, :] += _dot(st.pop((w, "a")), vv)

            return _sb_items(edge), [s_weights, s_values]

        first = i * ratio
        edge_tiles = [(first + m, m) for m in reversed(range(ratio))]

        def alive():
            return (jnp.max(c_ref[...]) >= SB_DEAD_CARRY).astype(jnp.int32)

        @pl.when(i == 0)
        def _():
            _emit_skewed(*[front(j, m) for j, m in edge_tiles])
            _emit_skewed(*[back(j, m) for j, m in edge_tiles])

        @pl.when(i > 0)
        def _():
            tiles = edge_tiles + [(first - 1, None)]
            _emit_skewed(*[front(j, m) for j, m in tiles])
            _emit_skewed(*[back(j, m) for j, m in tiles])

            @pl.when((alive() > 0) & (first >= 2))
            def _():
                _emit_skewed(front(first - 2, None))

                def step(state):
                    pending, _ = state
                    _emit_skewed(front(pending - 1, None), back(pending, None))
                    return pending - 1, alive()

                pending, live = lax.while_loop(lambda s: (s[0] > 0) & (s[1] > 0), step, (first - 2, jnp.int32(1)))

                @pl.when(live > 0)
                def _():
                    _emit_skewed(back(pending, None))

        o_ref[...] = jnp.where(low, oacc_ref[0], oacc_ref[1]).astype(BF16)

    return _call(
        body, (q, k, v), comm=comm, **_grid_ends(N_HEADS // 2, s_len // tq), name="sb_fwd",
        grid=(N_HEADS // 2, s_len // tq),
        in_specs=[pl.BlockSpec((tq, LANES), lambda p, i: (i, p)),
                  pl.BlockSpec((s_len, LANES), lambda p, i: (0, p)),
                  pl.BlockSpec((s_len, LANES), lambda p, i: (0, p))],
        out_specs=[pl.BlockSpec((tq, LANES), lambda p, i: (i, p)), pl.BlockSpec((2, tq, LANES), lambda p, i: (p, i, 0))],
        out_shape=[jax.ShapeDtypeStruct((s_len, N_HEADS * HEAD_DIM), BF16),
                   jax.ShapeDtypeStruct((N_HEADS, s_len, LANES), F32)],
        scratch_shapes=[pltpu.VMEM((2, tq, LANES), F32), pltpu.VMEM((2, tq, LANES), F32),
                        pltpu.VMEM((SB_SLOTS, 2, tq, tk), F32), pltpu.VMEM((SB_SLOTS, 2, tq, LANES), F32)],
        compiler_params=_cparams("arbitrary", "arbitrary"),
    )


def _sb_bwd(q, k, v, do, cars):
    s_len = q.shape[0]
    tq, tk, tr = SB_QUERIES, SB_KEYS, SB_ROWS
    nk, ratio = s_len // tk, tq // tk

    def body(q_ref, k_ref, v_ref, do_ref, car_ref, dq_ref, dk_ref, dv_ref,
             gleft_ref, dqacc_ref, dkacc_ref, dvacc_ref, logw_ref, lsz_ref, da_ref, a_ref, dz_ref):
        i = pl.program_id(1)

        @pl.when(i == 0)
        def _():
            dkacc_ref[...] = jnp.zeros_like(dkacc_ref)
            dvacc_ref[...] = jnp.zeros_like(dvacc_ref)

        qv = q_ref[...]
        dov = do_ref[...]
        low, tri_right2, tri_left2 = _sb_consts(tq, tk)
        lane = lax.broadcasted_iota(jnp.int32, (tr, LANES), 1)
        zero = jnp.zeros_like(qv)
        q_heads = (jnp.where(low, qv, zero), jnp.where(low, zero, qv))
        do_heads = (jnp.where(low, dov, zero), jnp.where(low, zero, dov))
        q_t = qv.astype(F32).T.astype(BF16)
        do_t = dov.astype(F32).T.astype(BF16)
        gleft_ref[...] = jnp.zeros_like(gleft_ref)
        dqacc_ref[...] = jnp.zeros_like(dqacc_ref)

        def front(j, edge):
            key_rows = pl.ds(pl.multiple_of(j * tk, tk), tk)
            keys, values = k_ref[key_rows, :], v_ref[key_rows, :]
            slot = j % SB_SLOTS
            st = {}

            def s_logits(w):
                h, rs = w[0], slice(w[1], w[1] + tr)
                st[w, "z"] = _dot_nt(q_heads[h][rs], keys)
                da_ref[slot, h, rs, :] = _dot_nt(do_heads[h][rs], values)

            def s_terms(w):
                h, rs = w[0], slice(w[1], w[1] + tr)
                valid = _sb_valid(w, edge) if w[2] else None
                lsz, lk = _sb_terms(st.pop((w, "z")), valid)
                st[w, "parts"] = _bf16_parts(lk)
                lsz = lsz if valid is None else jnp.where(valid, lsz, NEG_BIG)
                lsz_ref[slot, h, rs, :] = lsz
                st[w, "lszc"] = lsz + jnp.sum(jnp.where(lane == j, car_ref[h, rs, :], 0.0), axis=-1, keepdims=True)

            def s_suffix(w):
                logw_ref[slot, w[0], w[1]:w[1] + tr, :] = st.pop((w, "lszc")) + _dot(st.pop((w, "parts")), tri_right2)

            return _sb_items(edge), [s_logits, s_terms, s_suffix]

        def back(j, edge):
            kv = k_ref[pl.ds(pl.multiple_of(j * tk, tk), tk), :]
            slot = j % SB_SLOTS
            st = {}

            items = _sb_items(edge)
            head_rows = [[w[1] for w in items if w is not None and w[0] == h] for h in range(2)]

            def s_weights(w):
                h, rs = w[0], slice(w[1], w[1] + tr)
                a = jnp.exp(logw_ref[slot, h, rs, :])
                g = a * da_ref[slot, h, rs, :]
                a_ref[slot, h, rs, :] = a.astype(BF16)
                st[w, "g"], st[w, "parts"] = g, _bf16_parts(g)

            def s_prefix(w):
                st[w, "gs"] = _dot(st.pop((w, "parts")), tri_left2)

            def s_dz(w):
                h, rs = w[0], slice(w[1], w[1] + tr)
                g = st.pop((w, "g"))
                gleft = gleft_ref[h, rs, :]
                gsum = st.pop((w, "gs")) + jnp.tile(gleft, (1, tk // LANES))
                dz = (g - jnp.exp(lsz_ref[slot, h, rs, :]) * (g + gsum)).astype(BF16)
                st[w, "dz"] = dz
                dz_ref[slot, h, rs, :] = dz
                gleft_ref[h, rs, :] = gleft + _row_sum_lanes(g)

            def s_products(w):
                h, rs = w[0], slice(w[1], w[1] + tr)
                dqacc_ref[h, rs, :] += _dot(st.pop((w, "dz")), kv)
                if w[1] == head_rows[h][-1]:
                    feat = slice(h * HEAD_DIM, (h + 1) * HEAD_DIM)
                    hr = slice(head_rows[h][0], tq)
                    dkacc_ref[j, feat, :] += _dot(q_t[feat, hr], dz_ref[slot, h, hr, :])
                    dvacc_ref[j, feat, :] += _dot(do_t[feat, hr], a_ref[slot, h, hr, :])

            return items, [s_weights, s_prefix, s_dz, s_products]

        first = i * ratio
        tile_max = jnp.max(jnp.maximum(car_ref[0], car_ref[1]), axis=0, keepdims=True)
        start = jnp.clip(first + ratio - jnp.sum(jnp.where(tile_max >= SB_DEAD_CARRY, 1, 0)), 0, first)

        edge_tiles = [(first + m, m) for m in range(ratio)]

        @pl.when(start == first)
        def _():
            _emit_skewed(*[front(j, m) for j, m in edge_tiles])
            _emit_skewed(*[back(j, m) for j, m in edge_tiles])

        @pl.when(start == first - 1)
        def _():
            tiles = [(first - 1, None)] + edge_tiles
            _emit_skewed(*[front(j, m) for j, m in tiles])
            _emit_skewed(*[back(j, m) for j, m in tiles])

        @pl.when(start < first - 1)
        def _():
            _emit_skewed(front(start, None))

            def step(jj, carry):
                _emit_skewed(front(jj, None), back(jj - 1, None))
                return carry

            lax.fori_loop(start + 1, first, step, 0)
            _emit_skewed(front(first, 0), back(first - 1, None))
            for m in range(1, ratio):
                _emit_skewed(front(first + m, m), back(first + m - 1, m - 1))
            _emit_skewed(back(first + ratio - 1, ratio - 1))

        dq_ref[...] = (Q_SCALE * jnp.where(low, dqacc_ref[0], dqacc_ref[1])).astype(BF16)

        @pl.when(i == s_len // tq - 1)
        def _():
            for j in range(nk):
                dk_ref[j * tk:(j + 1) * tk, :] = dkacc_ref[j].T.astype(BF16)
                dv_ref[j * tk:(j + 1) * tk, :] = dvacc_ref[j].T.astype(BF16)

    qblk = pl.BlockSpec((tq, LANES), lambda p, i: (i, p))
    col_full = pl.BlockSpec((s_len, LANES), lambda p, i: (0, p))
    return pl.pallas_call(
        body, name="sb_bwd",
        grid=(N_HEADS // 2, s_len // tq),
        in_specs=[qblk, col_full, col_full, qblk, pl.BlockSpec((2, tq, LANES), lambda p, i: (p, i, 0))],
        out_specs=[qblk, col_full, col_full],
        out_shape=[jax.ShapeDtypeStruct((s_len, N_HEADS * HEAD_DIM), BF16)] * 3,
        scratch_shapes=[pltpu.VMEM((2, tq, LANES), F32), pltpu.VMEM((2, tq, LANES), F32),
                        pltpu.VMEM((nk, LANES, tk), F32), pltpu.VMEM((nk, LANES, tk), F32)]
        + [pltpu.VMEM((SB_SLOTS, 2, tq, tk), F32)] * 3 + [pltpu.VMEM((SB_SLOTS, 2, tq, tk), BF16)] * 2,
        compiler_params=_cparams("parallel", "arbitrary"),
    )(q, k, v, do, cars)


def _local_step(xs, tgt, gains, sinks, rel_bias, weights_of, ship):
    g1, gmix, g2, gfin = gains
    bkt = _rel_bucket_matrix()
    grads = {}

    def carried(outs, comm, count):
        return outs[:count], (list(outs[count:]) if comm is not None else None)

    wts = dict(weights_of(0, None))
    comm = ship("weights", 1)
    (x1, h1, a1, b1, u1), landed = carried(
        _ffn_fwd(xs, g1, wts["ffn1_w1t"], wts["ffn1_w3t"], wts["ffn1_w2"], "1", comm), comm, 5)
    wts.update(weights_of(1, landed))
    hm, qa, ka, va, qb, kb, vb, ga, gb = _proj_fwd(x1, gmix, wts["w_int"])
    oa = _swa_fwd(rel_bias, sinks, bkt, qa, ka, va)
    comm = ship("weights", 2)
    (ob, cars), landed = carried(_sb_fwd(qb, kb, vb, comm), comm, 2)
    wts.update(weights_of(2, landed))
    x2, mg = _merge_fwd(x1, oa, ob, ga, gb, wts["w_swa"], wts["w_sb"], wts["w_out"])
    dx3, h3, a3, b3, u3, loss, dgfin = _ffn_fwd(x2, g2, wts["ffn2_w1t"], wts["ffn2_w3t"], wts["ffn2_w2"], "2",
                                                loss=(tgt, gfin))

    def grad_chain(items):
        prev = None
        for name, lhs, rhs in items:
            comm = None if prev is None else ship("grads", (prev[0],), prev[1])
            res = _matmul_tn_tiled(lhs, rhs, name, comm)
            if prev is not None:
                grads[(prev[0],)] = prev[1] if comm is None else res[1]
            prev = (name, {_GRAD_KEY[name]: res if comm is None else res[0]})
        return prev

    dx2, dg2, da3, db3, dx3b = _ffn_bwd(dx3, x2, g2, a3, b3, wts["ffn2_w1t"], wts["ffn2_w3t"], wts["ffn2_w2"], "2")
    last = grad_chain((("ffn2_w1", da3, h3), ("ffn2_w3", db3, h3), ("ffn2_w2", u3, dx3b)))
    comm = ship("grads", (last[0],), last[1])
    (doa, dob, dga, dgb, dpa, dpb, dx2b), landed = carried(
        _merge_bwd(dx2, oa, ob, ga, gb, wts["w_swa"], wts["w_sb"], wts["w_out"], comm), comm, 7)
    grads[(last[0],)] = last[1] if comm is None else landed[0]

    def keep(names, big, comm, landed):
        for i, name in enumerate(names):
            grads[(name,)] = {_GRAD_KEY[name]: big[_GRAD_KEY[name]]} if comm is None else landed[i]

    big = {"w_out": _matmul_tn(mg, dx2b, "w_out"), "w_swa": _matmul_tn(oa, dpa, "w_swa"),
           "w_sb": _matmul_tn(ob, dpb, "w_sb")}
    comm = ship("grads", GROUPS[1][1:], big)
    (dqa, dka, dva, dtab, dsink), landed = carried(_swa_bwd(rel_bias, sinks, bkt, qa, ka, va, doa, comm), comm, 5)
    keep(GROUPS[1][1:], big, comm, landed)
    dqb, dkb, dvb = _sb_bwd(qb, kb, vb, dob, cars)
    dpieces = (dqa, dka.astype(BF16), dva.astype(BF16), dqb, dkb, dvb, dga, dgb)
    big = {"w_int": _matmul_tn_stacked(dpieces, hm, "w_in")}
    comm = ship("grads", GROUPS[1][:1], big)
    (dx1, dgmix), landed = carried(_proj_bwd(dpieces, dx2, x1, gmix, wts["w_int"], comm), comm, 2)
    keep(GROUPS[1][:1], big, comm, landed)

    dx0, dg1, da1, db1, dx1b = _ffn_bwd(dx1, xs, g1, a1, b1, wts["ffn1_w1t"], wts["ffn1_w3t"], wts["ffn1_w2"], "1")

    last = grad_chain((("ffn1_w1", da1, h1), ("ffn1_w3", db1, h1), ("ffn1_w2", u1, dx1b)))
    grads[(last[0],)] = last[1]

    small = {"gains": (dg1, dgmix, dg2, dgfin), "sinks": dsink[:, 0], "rel_bias": dtab[:, :N_HEADS]}
    return loss, dx0, small, grads


def _my_place():
    return lax.axis_index("x"), lax.axis_index("y"), lax.axis_index("c")


def _flip(v, bit):
    return 1 - v if bit else v


_RELATIONS = tuple((k >> 2 & 1, k >> 1 & 1, k & 1) for k in range(1, N_DEV))


def _gather_weights(blocks, tag):
    count = len(blocks)

    def body(*refs):
        x_refs, out_refs = refs[:count], refs[count:2 * count]
        send_sems, recv_sems, local_sems = refs[2 * count:]
        x, y, c = _my_place()
        me, sibling = (x, y, c), (x, y, 1 - c)
        chips = [(1 - x, y), (x, 1 - y), (1 - x, 1 - y)]

        def rows(s, px, py, pc):
            return out_refs[s].at[4 * px + 2 * py + pc]

        def copy(s, k, block, to, src=None):
            return pltpu.make_async_remote_copy(
                src_ref=rows(s, *block) if src is None else src, dst_ref=rows(s, *block),
                send_sem=send_sems.at[s, k], recv_sem=recv_sems.at[s, k],
                device_id=to, device_id_type=pl.DeviceIdType.MESH)

        mine = [pltpu.make_async_copy(x_refs[s], rows(s, *me), local_sems.at[s]) for s in range(count)]
        first, passed = [], []
        for s in range(count):
            mine[s].start()
            first.append(copy(s, 0, me, sibling, src=x_refs[s]))
            first += [copy(s, 1 + j, me, (*chip, c), src=x_refs[s]) for j, chip in enumerate(chips)]
        for cp in first:
            cp.start()
        for s in range(count):
            for j, chip in enumerate(chips):
                copy(s, 1 + j, (*chip, c), me).wait_recv()
                passed.append(copy(s, 4 + j, (*chip, c), sibling))
                passed[-1].start()
        for s in range(count):
            copy(s, 0, sibling, me).wait_recv()
            for j, chip in enumerate(chips):
                copy(s, 4 + j, (*chip, 1 - c), me).wait_recv()
        for cp in first + passed:
            cp.wait_send()
        for cp in mine:
            cp.wait()

    anywhere = pl.BlockSpec(memory_space=pl.ANY)
    return pl.pallas_call(
        body, name=f"gather_weights_{tag}",
        out_shape=[jax.ShapeDtypeStruct((N_DEV,) + b.shape, b.dtype) for b in blocks],
        in_specs=[anywhere] * count, out_specs=[anywhere] * count,
        scratch_shapes=[pltpu.SemaphoreType.DMA((count, N_DEV - 1)), pltpu.SemaphoreType.DMA((count, N_DEV - 1)),
                        pltpu.SemaphoreType.DMA((count,))],
    )(*blocks)


def _exchange_grads(gp, tag):
    def body(g_ref, out_ref, send_sems, recv_sems, local_sem):
        x, y, c = _my_place()
        me = 4 * x + 2 * y + c
        mine = pltpu.make_async_copy(g_ref.at[me], out_ref.at[me], local_sem)
        mine.start()
        copies = []
        for k, (fx, fy, fc) in enumerate(_RELATIONS):
            px, py, pc = _flip(x, fx), _flip(y, fy), _flip(c, fc)
            peer = 4 * px + 2 * py + pc
            copies.append((
                pltpu.make_async_remote_copy(
                    src_ref=g_ref.at[peer], dst_ref=out_ref.at[me], send_sem=send_sems.at[k], recv_sem=recv_sems.at[k],
                    device_id=(px, py, pc), device_id_type=pl.DeviceIdType.MESH),
                pltpu.make_async_remote_copy(
                    src_ref=g_ref.at[peer], dst_ref=out_ref.at[peer], send_sem=send_sems.at[k], recv_sem=recv_sems.at[k],
                    device_id=(px, py, pc), device_id_type=pl.DeviceIdType.MESH)))
        for out_cp, _ in copies:
            out_cp.start()
        for _, in_cp in copies:
            in_cp.wait_recv()
        for out_cp, _ in copies:
            out_cp.wait_send()
        mine.wait()

    return pl.pallas_call(
        body, name=f"exchange_grads_{tag}",
        out_shape=jax.ShapeDtypeStruct(gp.shape, gp.dtype),
        in_specs=[pl.BlockSpec(memory_space=pl.ANY)],
        out_specs=pl.BlockSpec(memory_space=pl.ANY),
        scratch_shapes=[pltpu.SemaphoreType.DMA((7,)), pltpu.SemaphoreType.DMA((7,)), pltpu.SemaphoreType.DMA(())],
    )(gp)


def _peers():
    x, y, c = _my_place()
    out = []
    for k, (fx, fy, fc) in enumerate(_RELATIONS):
        px, py, pc = _flip(x, fx), _flip(y, fy), _flip(c, fc)
        out.append((k, (px, py, pc), 4 * px + 2 * py + pc))
    return out, 4 * x + 2 * y + c


def _grid_ends(*grid):
    def first():
        return functools.reduce(lambda a, b: a & b, [pl.program_id(d) == 0 for d in range(len(grid))])

    def last():
        return functools.reduce(lambda a, b: a & b, [pl.program_id(d) == n - 1 for d, n in enumerate(grid)])

    return {"first": first, "last": last}


def _call(body, operands, *, comm=None, first=None, last=None, **kw):
    if comm is None:
        return pl.pallas_call(body, **kw)(*operands)
    in_specs, out_specs, out_shape = list(kw.pop("in_specs")), list(kw.pop("out_specs")), list(kw.pop("out_shape"))
    scratch = list(kw.pop("scratch_shapes", ()))
    n_in, n_out, n_scr, n_src = len(in_specs), len(out_specs), len(scratch), len(comm)

    def wrapped(*refs):
        ins, src_refs = refs[:n_in], refs[n_in:n_in + n_src]
        outs = refs[n_in + n_src:n_in + n_src + n_out]
        land_refs = refs[n_in + n_src + n_out:n_in + 2 * n_src + n_out]
        scr = refs[n_in + 2 * n_src + n_out:n_in + 2 * n_src + n_out + n_scr]
        send_sems, recv_sems, local_sems = refs[n_in + 2 * n_src + n_out + n_scr:]
        peers, me = _peers()
        mine, going, coming = [], [], []
        for s, (_, per_peer) in enumerate(comm):
            src_ref, land_ref = src_refs[s], land_refs[s]
            mine.append(pltpu.make_async_copy(src_ref.at[me] if per_peer else src_ref, land_ref.at[me], local_sems.at[s]))
            for k, where, slab in peers:
                piece = src_ref.at[slab] if per_peer else src_ref
                going.append(pltpu.make_async_remote_copy(
                    src_ref=piece, dst_ref=land_ref.at[me], send_sem=send_sems.at[s, k], recv_sem=recv_sems.at[s, k],
                    device_id=where, device_id_type=pl.DeviceIdType.MESH))
                coming.append(pltpu.make_async_remote_copy(
                    src_ref=piece, dst_ref=land_ref.at[slab], send_sem=send_sems.at[s, k], recv_sem=recv_sems.at[s, k],
                    device_id=where, device_id_type=pl.DeviceIdType.MESH))

        @pl.when(first())
        def _():
            for cp in mine + going:
                cp.start()

        body(*ins, *outs, *scr)

        @pl.when(last())
        def _():
            for cp in coming:
                cp.wait_recv()
            for cp in going:
                cp.wait_send()
            for cp in mine:
                cp.wait()

    anywhere = pl.BlockSpec(memory_space=pl.ANY)
    lands = [jax.ShapeDtypeStruct(src.shape if per_peer else (N_DEV,) + src.shape, src.dtype) for src, per_peer in comm]
    return pl.pallas_call(
        wrapped, in_specs=in_specs + [anywhere] * n_src, out_specs=out_specs + [anywhere] * n_src,
        out_shape=out_shape + lands,
        scratch_shapes=scratch + [pltpu.SemaphoreType.DMA((n_src, N_DEV - 1)), pltpu.SemaphoreType.DMA((n_src, N_DEV - 1)),
                                  pltpu.SemaphoreType.DMA((n_src,))],
        **kw)(*operands, *[src for src, _ in comm])


def _adamw(w, g, m, v):
    m = ADAM_B1 * m + (1.0 - ADAM_B1) * g
    v = ADAM_B2 * v + (1.0 - ADAM_B2) * jnp.square(g)
    m_hat = m / (1.0 - ADAM_B1 ** ADAM_STEP)
    v_hat = v / (1.0 - ADAM_B2 ** ADAM_STEP)
    delta = -ADAM_LR * (m_hat / (jnp.sqrt(v_hat) + ADAM_EPS) + ADAM_WD * w)
    return delta, m, v


def _sum_and_adamw(parts, w, m, v, tr, tag):
    rows = w.shape[0]
    assert rows % tr == 0

    def body(p_ref, w_ref, m_ref, v_ref, g_out, d_out, m_out, v_out):
        g = p_ref[0].astype(F32)
        for d in range(1, N_DEV):
            g = g + p_ref[d].astype(F32)
        delta, mn, vn = _adamw(w_ref[...], g, m_ref[...], v_ref[...])
        g_out[...] = g
        d_out[...] = delta
        m_out[...] = mn
        v_out[...] = vn

    sp = pl.BlockSpec((tr, D_MODEL), lambda i: (i, 0))
    return pl.pallas_call(
        body, name=f"sum_and_adamw_{tag}",
        grid=(rows // tr,),
        in_specs=[pl.BlockSpec((N_DEV, tr, D_MODEL), lambda i: (0, i, 0)), sp, sp, sp],
        out_specs=[sp] * 4,
        out_shape=[jax.ShapeDtypeStruct(w.shape, F32)] * 4,
        compiler_params=_cparams("parallel"),
    )(parts, w, m, v)


def _small_allreduce_adamw(part, w, m, v):
    def body(p_ref, w_ref, m_ref, v_ref, g_out, d_out, m_out, v_out, buf, send_sems, recv_sems):
        x, y, c = _my_place()
        me = 4 * x + 2 * y + c
        buf[me] = p_ref[...]
        copies = []
        for k, (fx, fy, fc) in enumerate(_RELATIONS):
            px, py, pc = _flip(x, fx), _flip(y, fy), _flip(c, fc)
            peer = 4 * px + 2 * py + pc
            copies.append((
                pltpu.make_async_remote_copy(
                    src_ref=buf.at[me], dst_ref=buf.at[me], send_sem=send_sems.at[k], recv_sem=recv_sems.at[k],
                    device_id=(px, py, pc), device_id_type=pl.DeviceIdType.MESH),
                pltpu.make_async_remote_copy(
                    src_ref=buf.at[me], dst_ref=buf.at[peer], send_sem=send_sems.at[k], recv_sem=recv_sems.at[k],
                    device_id=(px, py, pc), device_id_type=pl.DeviceIdType.MESH)))
        for out_cp, _ in copies:
            out_cp.start()
        for _, in_cp in copies:
            in_cp.wait_recv()
        for out_cp, _ in copies:
            out_cp.wait_send()
        g = buf[0]
        for d in range(1, N_DEV):
            g = g + buf[d]
        delta, mn, vn = _adamw(w_ref[...], g, m_ref[...], v_ref[...])
        g_out[...] = g
        d_out[...] = delta
        m_out[...] = mn
        v_out[...] = vn

    vm = pl.BlockSpec(memory_space=pltpu.VMEM)
    return pl.pallas_call(
        body, name="small_allreduce_adamw",
        in_specs=[vm] * 4, out_specs=[vm] * 4,
        out_shape=[jax.ShapeDtypeStruct(w.shape, F32)] * 4,
        scratch_shapes=[pltpu.VMEM((N_DEV,) + part.shape, F32),
                        pltpu.SemaphoreType.DMA((7,)), pltpu.SemaphoreType.DMA((7,))],
    )(part, w, m, v)


_TRANSPOSED = ("ffn1_w1", "ffn1_w3", "w_in", "ffn2_w1", "ffn2_w3")
_BRANCH = ("w_branch_swa", "w_branch_sb")


def _pack_shards(t, names):
    parts = []
    for name in names:
        a = t[name][0]
        if name in _TRANSPOSED:
            a = a.T
        elif name in _BRANCH:
            a = a.reshape(64, D_MODEL)
        parts.append(a)
    return jnp.concatenate(parts, axis=0)


def _unpack_shards(p, names):
    out, lo = {}, 0
    for name in names:
        a = p[lo:lo + BIG_ROWS[BIG_NAMES.index(name)]]
        lo += a.shape[0]
        if name in _TRANSPOSED:
            a = a.T
        elif name in _BRANCH:
            a = a.reshape(512, 128)
        out[name] = a[None]
    return out


def _full_weights(zones, names):
    out = {}
    for name, a in zip(names, zones):
        if name in _BRANCH:
            a = a.reshape(N_DEV, 512, 128).transpose(1, 0, 2).reshape(512, D_MODEL)
        out[_GRAD_KEY[name]] = a.reshape(-1, D_MODEL)
    return out


_GRAD_KEY = {"ffn1_w1": "ffn1_w1t", "ffn1_w3": "ffn1_w3t", "ffn1_w2": "ffn1_w2", "w_in": "w_int",
             "w_branch_swa": "w_swa", "w_branch_sb": "w_sb", "w_out": "w_out",
             "ffn2_w1": "ffn2_w1t", "ffn2_w3": "ffn2_w3t", "ffn2_w2": "ffn2_w2"}


def _pack_full_grads(big, names):
    parts = []
    for name in names:
        a = big[_GRAD_KEY[name]]
        if name in _BRANCH:
            a = a.reshape(512, N_DEV, 128).transpose(1, 0, 2)
        parts.append(a.reshape(N_DEV, BIG_ROWS[BIG_NAMES.index(name)], D_MODEL).astype(BF16))
    return jnp.concatenate(parts, axis=1)


_SMALL_NAMES = ("norm_ffn1", "norm_mix", "norm_ffn2", "norm_final", "swa_sinks", "rel_bias")


def _pack_small(vals):
    rows = []
    for a in vals:
        a = a.reshape(-1)
        rows.append(jnp.pad(a, (0, D_MODEL - a.shape[0])))
    rows += [jnp.zeros((D_MODEL,), F32)] * (SMALL_ROWS - len(rows))
    return jnp.stack(rows)


def _unpack_small(p):
    return {"norm_ffn1": p[0:1], "norm_mix": p[1:2], "norm_ffn2": p[2:3], "norm_final": p[3],
            "swa_sinks": p[4:5, :N_HEADS], "rel_bias": p[5, :REL_BUCKETS * N_HEADS].reshape(REL_BUCKETS, N_HEADS)}


ALL_NAMES = ("norm_ffn1", "ffn1_w1", "ffn1_w3", "ffn1_w2", "norm_mix", "w_in", "swa_sinks", "rel_bias",
             "w_branch_swa", "w_branch_sb", "w_out", "norm_ffn2", "ffn2_w1", "ffn2_w3", "ffn2_w2", "norm_final")


def kernel(x, norm_ffn1, ffn1_w1, ffn1_w3, ffn1_w2, norm_mix, w_in, swa_sinks, rel_bias, w_branch_swa, w_branch_sb, w_out, norm_ffn2, ffn2_w1, ffn2_w3, ffn2_w2, norm_final, loss_target, m_norm_ffn1, m_ffn1_w1, m_ffn1_w3, m_ffn1_w2, m_norm_mix, m_w_in, m_swa_sinks, m_rel_bias, m_w_branch_swa, m_w_branch_sb, m_w_out, m_norm_ffn2, m_ffn2_w1, m_ffn2_w3, m_ffn2_w2, m_norm_final, v_norm_ffn1, v_ffn1_w1, v_ffn1_w3, v_ffn1_w2, v_norm_mix, v_w_in, v_swa_sinks, v_rel_bias, v_w_branch_swa, v_w_branch_sb, v_w_out, v_norm_ffn2, v_ffn2_w1, v_ffn2_w3, v_ffn2_w2, v_norm_final):
    w = dict(zip(ALL_NAMES, (norm_ffn1, ffn1_w1, ffn1_w3, ffn1_w2, norm_mix, w_in, swa_sinks, rel_bias,
                             w_branch_swa, w_branch_sb, w_out, norm_ffn2, ffn2_w1, ffn2_w3, ffn2_w2, norm_final)))
    m = dict(zip(ALL_NAMES, (m_norm_ffn1, m_ffn1_w1, m_ffn1_w3, m_ffn1_w2, m_norm_mix, m_w_in, m_swa_sinks, m_rel_bias,
                             m_w_branch_swa, m_w_branch_sb, m_w_out, m_norm_ffn2, m_ffn2_w1, m_ffn2_w3, m_ffn2_w2,
                             m_norm_final)))
    v = dict(zip(ALL_NAMES, (v_norm_ffn1, v_ffn1_w1, v_ffn1_w3, v_ffn1_w2, v_norm_mix, v_w_in, v_swa_sinks, v_rel_bias,
                             v_w_branch_swa, v_w_branch_sb, v_w_out, v_norm_ffn2, v_ffn2_w1, v_ffn2_w3, v_ffn2_w2,
                             v_norm_final)))

    def my_blocks(group):
        return [_pack_shards(w, (name,)).astype(BF16) for name in GROUPS[group]]

    gathered0 = _gather_weights(my_blocks(0), "group0")

    def weights_of(group, landed):
        return _full_weights(gathered0 if group == 0 else landed, GROUPS[group])

    def ship(kind, which, grads=None):
        if kind == "weights":
            return [(block, False) for block in my_blocks(which)]
        return [(_pack_full_grads(grads, (name,)), True) for name in which]

    gains = (norm_ffn1, norm_mix, norm_ffn2, norm_final.reshape(1, D_MODEL))
    loss, dx, small, parts = _local_step(x[0], loss_target[0], gains, swa_sinks, rel_bias, weights_of, ship)

    big_outs = [{}, {}, {}, {}]
    for names, tile in zip(SUM_GROUPS, SUM_TILE):
        landed = parts[names]
        if isinstance(landed, dict):
            landed = _exchange_grads(_pack_full_grads(landed, names), names[0])
        res = _sum_and_adamw(landed, _pack_shards(w, names), _pack_shards(m, names), _pack_shards(v, names),
                             tile, names[0])
        for acc, packed in zip(big_outs, res):
            acc.update(_unpack_shards(packed, names))
    g_big, d_big, m_big, v_big = big_outs

    small_part = _pack_small(small["gains"] + (small["sinks"], small["rel_bias"], loss))
    zero = jnp.zeros((1,), F32)
    small_res = _small_allreduce_adamw(
        small_part, _pack_small([w[n] for n in _SMALL_NAMES] + [zero]), _pack_small([m[n] for n in _SMALL_NAMES] + [zero]),
        _pack_small([v[n] for n in _SMALL_NAMES] + [zero]))
    g_sm, d_sm, m_sm, v_sm = (_unpack_small(p) for p in small_res)

    outs = [small_res[0][len(_SMALL_NAMES), 0], dx[None]]
    for big_d, small_d in ((g_big, g_sm), (d_big, d_sm), (m_big, m_sm), (v_big, v_sm)):
        merged = {**big_d, **small_d}
        outs += [merged[n] for n in ALL_NAMES]
    return tuple(outs)
```

```python
import functools

import jax
import jax.numpy as jnp
import numpy as np
from jax import lax
from jax.experimental import pallas as pl
from jax.experimental.pallas import tpu as pltpu

F32 = jnp.float32
BF16 = jnp.bfloat16

D_MODEL = 1024
D_FF = 2816
HEAD_DIM = 64
N_HEADS = 8
SWA_KV_HEADS = 2
SWA_GROUP = 4
SWA_BLOCK = 128
REL_BUCKETS = 32
REL_MAX_DIST = 128
RMS_EPS = 1e-6
NEG_BIG = -1e30
Q_SCALE = HEAD_DIM ** -0.5
LANES = 128

N_DEV = 8

ADAM_LR = 0.001
ADAM_B1 = 0.9
ADAM_B2 = 0.999
ADAM_EPS = 1e-08
ADAM_WD = 0.01
ADAM_STEP = 10

IN_SIZES = (512, 128, 128, 512, 512, 512, 1024, 1024)
IN_OFFS = tuple(int(v) for v in np.cumsum((0,) + IN_SIZES))
IN_W = IN_OFFS[-1]

BIG_NAMES = ("ffn1_w1", "ffn1_w3", "ffn1_w2", "w_in", "w_branch_swa", "w_branch_sb", "w_out",
             "ffn2_w1", "ffn2_w3", "ffn2_w2")
BIG_ROWS = (352, 352, 352, 544, 64, 64, 128, 352, 352, 352)
SMALL_ROWS = 8
GROUPS = (BIG_NAMES[0:3], BIG_NAMES[3:7], BIG_NAMES[7:10])
SUM_GROUPS = tuple((n,) for n in BIG_NAMES)
SUM_TILE = (176, 176, 176, 272, 64, 64, 128, 176, 176, 176)

VMEM_LIMIT = 56 * 1024 * 1024
FFN_PIECES = 2
SB_QUERIES = 512
SB_KEYS = 256
SB_ROWS = 256
SB_SLOTS = 3
SB_SUM_PARTS = 1
SB_LOGIT_CAP = 80.0
SB_DEAD_CARRY = -110.0


def _dot(a, b):
    return jnp.dot(a, b, preferred_element_type=F32)


def _dot_nt(a, b):
    return lax.dot_general(a, b, (((1,), (1,)), ((), ())), preferred_element_type=F32)


def _dot_tn(a, b):
    return lax.dot_general(a, b, (((0,), (0,)), ((), ())), preferred_element_type=F32)


def _cparams(*sem):
    return pltpu.CompilerParams(dimension_semantics=sem, vmem_limit_bytes=VMEM_LIMIT)


def _rms_rstd(xv):
    return lax.rsqrt(jnp.mean(xv * xv, axis=-1, keepdims=True) + RMS_EPS)


def _rms_bwd(dh, xv, r, g):
    xhat = xv * r
    dg = jnp.sum(dh * xhat, axis=0, keepdims=True)
    dxn = dh * g
    dx = r * (dxn - xhat * jnp.mean(dxn * xhat, axis=-1, keepdims=True))
    return dx, dg


def _ff_tile_spec(tm, tf):
    return pl.BlockSpec((1, tm, tf), lambda i, j: (j, i, 0))


def _ffn_fwd(x, g, w1t, w3t, w2, tag, comm=None, loss=None):
    s_len = x.shape[0]
    tm, tf = min(1024, s_len), 256
    nf = D_FF // tf

    def body(*refs):
        if loss is None:
            x_ref, g_ref, w1_ref, w3_ref, w2_ref, xo_ref, h_ref, a_ref, b_ref, u_ref, acc_ref, hs_ref = refs
        else:
            (x_ref, g_ref, w1_ref, w3_ref, w2_ref, t_ref, gf_ref,
             xo_ref, h_ref, a_ref, b_ref, u_ref, loss_ref, dgf_ref, acc_ref, hs_ref) = refs
        j = pl.program_id(1)

        @pl.when(j == 0)
        def _():
            xv = x_ref[...]
            h = (xv * _rms_rstd(xv) * g_ref[...]).astype(BF16)
            hs_ref[...] = h
            h_ref[...] = h
            acc_ref[...] = jnp.zeros_like(acc_ref)

        st = {}

        def s_up(rs):
            h = hs_ref[rs, :]
            st[rs.start, "ab"] = (_dot_nt(h, w1_ref[...]), _dot_nt(h, w3_ref[...]))

        def s_act(rs):
            a, b = st.pop((rs.start, "ab"))
            a_ref[0, rs, :] = a.astype(BF16)
            b_ref[0, rs, :] = b.astype(BF16)
            uh = (0.5 * (a * jax.nn.sigmoid(a) * b)).astype(BF16)
            u_ref[0, rs, :] = uh
            st[rs.start, "u"] = uh

        def s_down(rs):
            acc_ref[rs, :] += _dot(st.pop((rs.start, "u")), w2_ref[...])

        _emit_skewed(([slice(r, r + tm // FFN_PIECES) for r in range(0, tm, tm // FFN_PIECES)], [s_up, s_act, s_down]))

        if loss is not None:
            @pl.when((pl.program_id(0) == 0) & (j == 0))
            def _():
                loss_ref[...] = jnp.zeros_like(loss_ref)
                dgf_ref[...] = jnp.zeros_like(dgf_ref)

        @pl.when(j == nf - 1)
        def _():
            xo = x_ref[...] + acc_ref[...]
            if loss is None:
                xo_ref[...] = xo
            else:
                gv = gf_ref[...]
                r = _rms_rstd(xo)
                err = xo * r * gv - t_ref[...]
                loss_ref[...] += 0.5 * jnp.sum(jnp.mean(err * err, axis=-1, keepdims=True), axis=0, keepdims=True)
                dx, dg = _rms_bwd(err * (1.0 / D_MODEL), xo, r, gv)
                xo_ref[...] = dx
                dgf_ref[...] += dg

    row = lambda i, j: (i, 0)
    fixed = lambda i, j: (0, 0)
    with_loss = loss is not None
    return _call(
        body, (x, g, w1t, w3t, w2) + (tuple(loss) if with_loss else ()), comm=comm, **_grid_ends(s_len // tm, nf),
        name=f"ffn_fwd_{tag}",
        grid=(s_len // tm, nf),
        in_specs=[pl.BlockSpec((tm, D_MODEL), row), pl.BlockSpec((1, D_MODEL), fixed),
                  pl.BlockSpec((tf, D_MODEL), lambda i, j: (j, 0)), pl.BlockSpec((tf, D_MODEL), lambda i, j: (j, 0)),
                  pl.BlockSpec((tf, D_MODEL), lambda i, j: (j, 0))]
        + ([pl.BlockSpec((tm, D_MODEL), row), pl.BlockSpec((1, D_MODEL), fixed)] if with_loss else []),
        out_specs=[pl.BlockSpec((tm, D_MODEL), row), pl.BlockSpec((tm, D_MODEL), row)] + [_ff_tile_spec(tm, tf)] * 3
        + ([pl.BlockSpec((1, 1), fixed), pl.BlockSpec((1, D_MODEL), fixed)] if with_loss else []),
        out_shape=[jax.ShapeDtypeStruct((s_len, D_MODEL), F32), jax.ShapeDtypeStruct((s_len, D_MODEL), BF16)]
        + [jax.ShapeDtypeStruct((nf, s_len, tf), BF16)] * 3
        + ([jax.ShapeDtypeStruct((1, 1), F32), jax.ShapeDtypeStruct((1, D_MODEL), F32)] if with_loss else []),
        scratch_shapes=[pltpu.VMEM((tm, D_MODEL), F32), pltpu.VMEM((tm, D_MODEL), BF16)],
        compiler_params=_cparams("arbitrary", "arbitrary"),
    )


def _ffn_bwd(dy, x, g, a, b, w1t, w3t, w2, tag, comm=None):
    s_len = x.shape[0]
    tm, tf = min(1024, s_len), 256
    nf = D_FF // tf

    def body(dy_ref, x_ref, g_ref, a_ref, b_ref, w1_ref, w3_ref, w2_ref,
             dx_ref, dg_ref, da_ref, db_ref, dyb_ref, acc_ref, dys_ref):
        i, j = pl.program_id(0), pl.program_id(1)

        @pl.when(j == 0)
        def _():
            dyb = dy_ref[...].astype(BF16)
            dys_ref[...] = 0.5 * dyb
            dyb_ref[...] = dyb
            acc_ref[...] = jnp.zeros_like(acc_ref)

        @pl.when((i == 0) & (j == 0))
        def _():
            dg_ref[...] = jnp.zeros_like(dg_ref)

        st = {}

        def s_du(rs):
            st[rs.start, "du"] = _dot_nt(dys_ref[rs, :], w2_ref[...])

        def s_act(rs):
            du = st.pop((rs.start, "du"))
            av = a_ref[0, rs, :].astype(F32)
            bv = b_ref[0, rs, :].astype(F32)
            sg = jax.nn.sigmoid(av)
            sil = av * sg
            da = (du * bv * (sg + sil * (1.0 - sg))).astype(BF16)
            db = (du * sil).astype(BF16)
            da_ref[0, rs, :] = da
            db_ref[0, rs, :] = db
            st[rs.start, "dab"] = (da, db)

        def s_dh(rs):
            da, db = st.pop((rs.start, "dab"))
            acc_ref[rs, :] += _dot(da, w1_ref[...]) + _dot(db, w3_ref[...])

        _emit_skewed(([slice(r, r + tm // FFN_PIECES) for r in range(0, tm, tm // FFN_PIECES)], [s_du, s_act, s_dh]))

        @pl.when(j == nf - 1)
        def _():
            xv = x_ref[...]
            dx, dg = _rms_bwd(acc_ref[...], xv, _rms_rstd(xv), g_ref[...])
            dx_ref[...] = dy_ref[...] + dx
            dg_ref[...] += dg

    row = lambda i, j: (i, 0)
    wsp = pl.BlockSpec((tf, D_MODEL), lambda i, j: (j, 0))
    return _call(
        body, (dy, x, g, a, b, w1t, w3t, w2), comm=comm, **_grid_ends(s_len // tm, nf), name=f"ffn_bwd_{tag}",
        grid=(s_len // tm, nf),
        in_specs=[pl.BlockSpec((tm, D_MODEL), row), pl.BlockSpec((tm, D_MODEL), row),
                  pl.BlockSpec((1, D_MODEL), lambda i, j: (0, 0)),
                  _ff_tile_spec(tm, tf), _ff_tile_spec(tm, tf), wsp, wsp, wsp],
        out_specs=[pl.BlockSpec((tm, D_MODEL), row), pl.BlockSpec((1, D_MODEL), lambda i, j: (0, 0)),
                   _ff_tile_spec(tm, tf), _ff_tile_spec(tm, tf), pl.BlockSpec((tm, D_MODEL), row)],
        out_shape=[jax.ShapeDtypeStruct((s_len, D_MODEL), F32), jax.ShapeDtypeStruct((1, D_MODEL), F32),
                   jax.ShapeDtypeStruct((nf, s_len, tf), BF16), jax.ShapeDtypeStruct((nf, s_len, tf), BF16),
                   jax.ShapeDtypeStruct((s_len, D_MODEL), BF16)],
        scratch_shapes=[pltpu.VMEM((tm, D_MODEL), F32), pltpu.VMEM((tm, D_MODEL), BF16)],
        compiler_params=_cparams("arbitrary", "arbitrary"),
    )


def _matmul_tn_tiled(lhs, rhs, tag, comm=None):
    nf, s_len, tf = lhs.shape
    n = rhs.shape[1]
    tm = min(512, s_len)
    last_rows = s_len // tm - 1

    def body(l_ref, r_ref, o_ref, acc_ref):
        i = pl.program_id(0)

        @pl.when(i == 0)
        def _():
            acc_ref[...] = jnp.zeros_like(acc_ref)

        rv = r_ref[...]
        for t in range(nf):
            acc_ref[t * tf:(t + 1) * tf, :] += _dot_tn(l_ref[t], rv)

        @pl.when(i == last_rows)
        def _():
            o_ref[...] = acc_ref[...].astype(BF16)

    res = _call(
        body, (lhs, rhs), comm=comm, **_grid_ends(s_len // tm), name=f"matmul_tn_{tag}",
        grid=(s_len // tm,),
        in_specs=[pl.BlockSpec((nf, tm, tf), lambda i: (0, i, 0)), pl.BlockSpec((tm, n), lambda i: (i, 0))],
        out_specs=[pl.BlockSpec((nf * tf, n), lambda i: (0, 0))],
        out_shape=[jax.ShapeDtypeStruct((nf * tf, n), BF16)],
        scratch_shapes=[pltpu.VMEM((nf * tf, n), F32)],
        compiler_params=_cparams("arbitrary"),
    )
    return res[0] if comm is None else tuple(res)


def _matmul_tn_stacked(pieces, rhs, tag):
    s_len, n = rhs.shape
    widths = [p.shape[1] for p in pieces]
    offs = [sum(widths[:k]) for k in range(len(widths) + 1)]
    tm = min(256, s_len)
    last_rows = s_len // tm - 1

    def body(*refs):
        l_refs, r_ref, o_ref, acc_ref = refs[:len(pieces)], refs[-3], refs[-2], refs[-1]
        i = pl.program_id(0)

        @pl.when(i == 0)
        def _():
            acc_ref[...] = jnp.zeros_like(acc_ref)

        rv = r_ref[...]
        for k, l_ref in enumerate(l_refs):
            acc_ref[offs[k]:offs[k + 1], :] += _dot_tn(l_ref[...], rv)

        @pl.when(i == last_rows)
        def _():
            o_ref[...] = acc_ref[...].astype(BF16)

    row = lambda i: (i, 0)
    return pl.pallas_call(
        body, name=f"matmul_tn_{tag}",
        grid=(s_len // tm,),
        in_specs=[pl.BlockSpec((tm, w), row) for w in widths] + [pl.BlockSpec((tm, n), row)],
        out_specs=pl.BlockSpec((offs[-1], n), lambda i: (0, 0)),
        out_shape=jax.ShapeDtypeStruct((offs[-1], n), BF16),
        scratch_shapes=[pltpu.VMEM((offs[-1], n), F32)],
        compiler_params=_cparams("arbitrary"),
    )(*pieces, rhs)


def _proj_fwd(x1, g, wint):
    s_len = x1.shape[0]
    tm = min(512, s_len)
    dts = (BF16, BF16, BF16, BF16, BF16, BF16, F32, F32)

    def body(x_ref, g_ref, w_ref, h_ref, *outs):
        xv = x_ref[...]
        h = (xv * _rms_rstd(xv) * g_ref[...]).astype(BF16)
        h_ref[...] = h
        for p, o_ref in enumerate(outs):
            val = _dot_nt(h, w_ref[IN_OFFS[p]:IN_OFFS[p + 1], :])
            if p == 3:
                val = val * Q_SCALE
            o_ref[...] = val.astype(dts[p])

    row = lambda i: (i, 0)
    return pl.pallas_call(
        body, name="proj_fwd",
        grid=(s_len // tm,),
        in_specs=[pl.BlockSpec((tm, D_MODEL), row), pl.BlockSpec((1, D_MODEL), lambda i: (0, 0)),
                  pl.BlockSpec((IN_W, D_MODEL), lambda i: (0, 0))],
        out_specs=[pl.BlockSpec((tm, D_MODEL), row)] + [pl.BlockSpec((tm, w), row) for w in IN_SIZES],
        out_shape=[jax.ShapeDtypeStruct((s_len, D_MODEL), BF16)]
        + [jax.ShapeDtypeStruct((s_len, w), dt) for w, dt in zip(IN_SIZES, dts)],
        compiler_params=_cparams("parallel"),
    )(x1, g, wint)


def _proj_bwd(dpieces, dx2, x1, g, wint, comm=None):
    s_len = x1.shape[0]
    tm = min(512, s_len)

    def body(*refs):
        dps = refs[:8]
        dx2_ref, x_ref, g_ref, w_ref, dx_ref, dg_ref = refs[8:]

        @pl.when(pl.program_id(0) == 0)
        def _():
            dg_ref[...] = jnp.zeros_like(dg_ref)

        dh = _dot(dps[0][...], w_ref[IN_OFFS[0]:IN_OFFS[1], :])
        for p in range(1, 8):
            dh += _dot(dps[p][...], w_ref[IN_OFFS[p]:IN_OFFS[p + 1], :])
        xv = x_ref[...]
        dx, dg = _rms_bwd(dh, xv, _rms_rstd(xv), g_ref[...])
        dx_ref[...] = dx2_ref[...] + dx
        dg_ref[...] += dg

    row = lambda i: (i, 0)
    return _call(
        body, (*dpieces, dx2, x1, g, wint), comm=comm, **_grid_ends(s_len // tm), name="proj_bwd",
        grid=(s_len // tm,),
        in_specs=[pl.BlockSpec((tm, w), row) for w in IN_SIZES]
        + [pl.BlockSpec((tm, D_MODEL), row), pl.BlockSpec((tm, D_MODEL), row),
           pl.BlockSpec((1, D_MODEL), lambda i: (0, 0)), pl.BlockSpec((IN_W, D_MODEL), lambda i: (0, 0))],
        out_specs=[pl.BlockSpec((tm, D_MODEL), row), pl.BlockSpec((1, D_MODEL), lambda i: (0, 0))],
        out_shape=[jax.ShapeDtypeStruct((s_len, D_MODEL), F32), jax.ShapeDtypeStruct((1, D_MODEL), F32)],
        compiler_params=_cparams("arbitrary"),
    )


def _merge_fwd(x1, oa, ob, ga, gb, wswa, wsb, wout):
    s_len = x1.shape[0]
    tm = min(512, s_len)

    def body(x_ref, oa_ref, ob_ref, ga_ref, gb_ref, wa_ref, wb_ref, wo_ref, xo_ref, mg_ref):
        pa = _dot(oa_ref[...], wa_ref[...])
        pb = _dot(ob_ref[...], wb_ref[...])
        mg = (jax.nn.sigmoid(ga_ref[...]) * pa + jax.nn.sigmoid(gb_ref[...]) * pb).astype(BF16)
        mg_ref[...] = mg
        xo_ref[...] = x_ref[...] + _dot(mg, wo_ref[...])

    row = lambda i: (i, 0)
    full = lambda i: (0, 0)
    return pl.pallas_call(
        body, name="merge_fwd",
        grid=(s_len // tm,),
        in_specs=[pl.BlockSpec((tm, D_MODEL), row), pl.BlockSpec((tm, 512), row), pl.BlockSpec((tm, 512), row),
                  pl.BlockSpec((tm, D_MODEL), row), pl.BlockSpec((tm, D_MODEL), row),
                  pl.BlockSpec((512, D_MODEL), full), pl.BlockSpec((512, D_MODEL), full),
                  pl.BlockSpec((D_MODEL, D_MODEL), full)],
        out_specs=[pl.BlockSpec((tm, D_MODEL), row), pl.BlockSpec((tm, D_MODEL), row)],
        out_shape=[jax.ShapeDtypeStruct((s_len, D_MODEL), F32), jax.ShapeDtypeStruct((s_len, D_MODEL), BF16)],
        compiler_params=_cparams("parallel"),
    )(x1, oa, ob, ga, gb, wswa, wsb, wout)


def _merge_bwd(dx2, oa, ob, ga, gb, mg, wswa, wsb, wout, comm=None):
    s_len = dx2.shape[0]
    tm = min(512, s_len)
    last_rows = s_len // tm - 1

    def body(dx_ref, oa_ref, ob_ref, ga_ref, gb_ref, mg_ref, wa_ref, wb_ref, wo_ref,
             doa_ref, dob_ref, dga_ref, dgb_ref, gwo_ref, gwa_ref, gwb_ref, acco_ref, acca_ref, accb_ref):
        i = pl.program_id(0)

        @pl.when(i == 0)
        def _():
            acco_ref[...] = jnp.zeros_like(acco_ref)
            acca_ref[...] = jnp.zeros_like(acca_ref)
            accb_ref[...] = jnp.zeros_like(accb_ref)

        dxb = dx_ref[...].astype(BF16)
        acco_ref[...] += _dot_tn(mg_ref[...], dxb)
        dmg = _dot_nt(dxb, wo_ref[...])
        for o_ref, g_ref, w_ref, do_ref, dg_ref, acc_ref in (
                (oa_ref, ga_ref, wa_ref, doa_ref, dga_ref, acca_ref),
                (ob_ref, gb_ref, wb_ref, dob_ref, dgb_ref, accb_ref)):
            pv = _dot(o_ref[...], w_ref[...])
            sg = jax.nn.sigmoid(g_ref[...])
            dp = (dmg * sg).astype(BF16)
            acc_ref[...] += _dot_tn(o_ref[...], dp)
            dg_ref[...] = (dmg * pv * sg * (1.0 - sg)).astype(BF16)
            do_ref[...] = _dot_nt(dp, w_ref[...]).astype(BF16)

        @pl.when(i == last_rows)
        def _():
            gwo_ref[...] = acco_ref[...].astype(BF16)
            gwa_ref[...] = acca_ref[...].astype(BF16)
            gwb_ref[...] = accb_ref[...].astype(BF16)

    row = lambda i: (i, 0)
    full = lambda i: (0, 0)
    wide = pl.BlockSpec((tm, D_MODEL), row)
    half = pl.BlockSpec((tm, 512), row)
    w_branch, w_out = pl.BlockSpec((512, D_MODEL), full), pl.BlockSpec((D_MODEL, D_MODEL), full)
    return _call(
        body, (dx2, oa, ob, ga, gb, mg, wswa, wsb, wout), comm=comm, **_grid_ends(s_len // tm), name="merge_bwd",
        grid=(s_len // tm,),
        in_specs=[wide, half, half, wide, wide, wide, w_branch, w_branch, w_out],
        out_specs=[half, half, wide, wide, w_out, w_branch, w_branch],
        out_shape=[jax.ShapeDtypeStruct((s_len, 512), BF16)] * 2 + [jax.ShapeDtypeStruct((s_len, D_MODEL), BF16)] * 2
        + [jax.ShapeDtypeStruct((D_MODEL, D_MODEL), BF16)] + [jax.ShapeDtypeStruct((512, D_MODEL), BF16)] * 2,
        scratch_shapes=[pltpu.VMEM((D_MODEL, D_MODEL), F32), pltpu.VMEM((512, D_MODEL), F32),
                        pltpu.VMEM((512, D_MODEL), F32)],
        compiler_params=_cparams("arbitrary"),
    )


def _rel_bucket_matrix():
    qi = jnp.arange(SWA_BLOCK)[:, None] + SWA_BLOCK
    kj = jnp.arange(2 * SWA_BLOCK)[None, :]
    dist = jnp.maximum(qi - kj, 0)
    max_exact = REL_BUCKETS // 2
    d = jnp.maximum(dist, 1).astype(F32)
    large = max_exact + (jnp.log(d / max_exact) / np.log(REL_MAX_DIST / max_exact)
                         * (REL_BUCKETS - max_exact)).astype(jnp.int32)
    large = jnp.minimum(large, REL_BUCKETS - 1)
    return jnp.where(dist < max_exact, dist, large).astype(jnp.int32)


def _swa_bias_into(bias_ref, bkt_ref, tab_ref):
    bk = bkt_ref[...]
    for h in range(N_HEADS):
        acc = jnp.zeros(bk.shape, F32)
        for bucket in range(REL_BUCKETS):
            acc = jnp.where(bk == bucket, tab_ref[bucket, h], acc)
        bias_ref[h] = acc


def _swa_valid(n):
    shape = (SWA_BLOCK, 2 * SWA_BLOCK)
    row = lax.broadcasted_iota(jnp.int32, shape, 0)
    col = lax.broadcasted_iota(jnp.int32, shape, 1)
    dist = row + SWA_BLOCK - col
    return (dist >= 0) & (dist < SWA_BLOCK) & ((col >= SWA_BLOCK) | (n > 0))


def _swa_windows(kp_ref, kc_ref, vp_ref, vc_ref):
    return (jnp.concatenate([kp_ref[...], kc_ref[...]], axis=0), jnp.concatenate([vp_ref[...], vc_ref[...]], axis=0))


def _swa_place(h):
    return slice(h // 2 * LANES, (h // 2 + 1) * LANES), h % 2, h // SWA_GROUP


def _move_half(x, src, dst):
    moved = x if src == dst else pltpu.roll(x, HEAD_DIM, 1)
    in_dst = (lax.broadcasted_iota(jnp.int32, x.shape, 1) >= HEAD_DIM) == bool(dst)
    return jnp.where(in_dst, moved, 0.0)


def _swa_probs(qk, bias, sink, valid):
    lg = jnp.where(valid, qk * Q_SCALE + bias, NEG_BIG)
    m = jnp.maximum(jnp.max(lg, axis=-1, keepdims=True), sink)
    e = jnp.exp(lg - m)
    es = jnp.exp(sink - m)
    inv = 1.0 / (jnp.sum(e, axis=-1, keepdims=True) + es)
    return e * inv, es * inv


def _swa_specs(s_len):
    blk = SWA_BLOCK
    cur = lambda n: (n, 0)
    prev = lambda n: (jnp.maximum(n - 1, 0), 0)
    kvw = SWA_KV_HEADS * HEAD_DIM
    return [pl.BlockSpec(memory_space=pltpu.SMEM), pl.BlockSpec(memory_space=pltpu.SMEM),
            pl.BlockSpec((blk, 2 * blk), lambda n: (0, 0)),
            pl.BlockSpec((blk, N_HEADS * HEAD_DIM), cur),
            pl.BlockSpec((blk, kvw), prev), pl.BlockSpec((blk, kvw), cur),
            pl.BlockSpec((blk, kvw), prev), pl.BlockSpec((blk, kvw), cur)]


def _swa_fwd(tab, sinks, bkt, q, k, v):
    s_len = q.shape[0]
    blk = SWA_BLOCK

    def body(tab_ref, sink_ref, bkt_ref, q_ref, kp_ref, kc_ref, vp_ref, vc_ref, o_ref, bias_ref):
        n = pl.program_id(0)

        @pl.when(n == 0)
        def _():
            _swa_bias_into(bias_ref, bkt_ref, tab_ref)

        valid = _swa_valid(n)
        kk, vv = _swa_windows(kp_ref, kc_ref, vp_ref, vc_ref)
        st = {}

        def s_logits(h):
            tile, mine, kv = _swa_place(h)
            st[h, "lg"] = _dot_nt(_move_half(q_ref[:, tile].astype(F32), mine, kv).astype(BF16), kk)

        def s_probs(h):
            st[h, "p"] = _swa_probs(st.pop((h, "lg")), bias_ref[h], sink_ref[0, h], valid)[0].astype(BF16)

        def s_values(h):
            tile, mine, kv = _swa_place(h)
            part = _move_half(_dot(st.pop((h, "p")), vv), kv, mine)
            if mine == 0:
                st[h + 1, "o"] = part
            else:
                o_ref[:, tile] = (st.pop((h, "o")) + part).astype(BF16)

        _emit_skewed((list(range(N_HEADS)), [s_logits, s_probs, s_values]))

    return pl.pallas_call(
        body, name="swa_fwd",
        grid=(s_len // blk,),
        in_specs=_swa_specs(s_len),
        out_specs=pl.BlockSpec((blk, N_HEADS * HEAD_DIM), lambda n: (n, 0)),
        out_shape=jax.ShapeDtypeStruct((s_len, N_HEADS * HEAD_DIM), BF16),
        scratch_shapes=[pltpu.VMEM((N_HEADS, blk, 2 * blk), F32)],
        compiler_params=_cparams("arbitrary"),
    )(tab, sinks, bkt, q, k, k, v, v)


def _swa_bwd(tab, sinks, bkt, q, k, v, do, comm=None):
    s_len = q.shape[0]
    blk = SWA_BLOCK
    nb = s_len // blk
    kvw = SWA_KV_HEADS * HEAD_DIM

    def body(tab_ref, sink_ref, bkt_ref, q_ref, kp_ref, kc_ref, vp_ref, vc_ref, do_ref,
             dq_ref, dk_ref, dv_ref, dtab_ref, dsink_ref, bias_ref, dbias_ref):
        n = pl.program_id(0)

        @pl.when(n == 0)
        def _():
            _swa_bias_into(bias_ref, bkt_ref, tab_ref)
            dbias_ref[...] = jnp.zeros_like(dbias_ref)
            dk_ref[...] = jnp.zeros_like(dk_ref)
            dv_ref[...] = jnp.zeros_like(dv_ref)
            dsink_ref[...] = jnp.zeros_like(dsink_ref)
            dtab_ref[...] = jnp.zeros_like(dtab_ref)

        valid = _swa_valid(n)
        cur_rows = pl.ds(pl.multiple_of(n * blk, blk), blk)
        prev_rows = pl.ds(pl.multiple_of(jnp.maximum(n - 1, 0) * blk, blk), blk)
        kk, vv = _swa_windows(kp_ref, kc_ref, vp_ref, vc_ref)
        st = {}

        def s_logits(h):
            tile, mine, kv = _swa_place(h)
            st[h, "q"] = _move_half(q_ref[:, tile].astype(F32), mine, kv).astype(BF16)
            st[h, "do"] = _move_half(do_ref[:, tile].astype(F32), mine, kv).astype(BF16)
            st[h, "lg"] = _dot_nt(st[h, "q"], kk)
            st[h, "dp"] = _dot_nt(st[h, "do"], vv)

        def s_probs(h):
            p, ps = _swa_probs(st.pop((h, "lg")), bias_ref[h], sink_ref[0, h], valid)
            dp = st.pop((h, "dp"))
            delta = jnp.sum(p * dp, axis=-1, keepdims=True)
            dl = p * (dp - delta)
            dsink_ref[h:h + 1, :] += jnp.broadcast_to(-jnp.sum(ps * delta, axis=0, keepdims=True), (1, LANES))
            dbias_ref[h] += dl
            st[h, "dl"], st[h, "p"] = dl.astype(BF16), p.astype(BF16)

        def s_products(h):
            tile, mine, kv = _swa_place(h)
            dlb = st.pop((h, "dl"))
            part = _move_half(Q_SCALE * _dot(dlb, kk), kv, mine)
            if mine == 0:
                st[h + 1, "dq"] = part
            else:
                dq_ref[:, tile] = (st.pop((h, "dq")) + part).astype(BF16)
            dk_win = Q_SCALE * _dot_tn(dlb, st.pop((h, "q")))
            dv_win = _dot_tn(st.pop((h, "p")), st.pop((h, "do")))
            dk_ref[prev_rows, :] += dk_win[:blk]
            dv_ref[prev_rows, :] += dv_win[:blk]
            dk_ref[cur_rows, :] += dk_win[blk:]
            dv_ref[cur_rows, :] += dv_win[blk:]

        _emit_skewed((list(range(N_HEADS)), [s_logits, s_probs, s_products]))

        @pl.when(n == nb - 1)
        def _():
            bk = bkt_ref[...]
            lane = lax.broadcasted_iota(jnp.int32, (1, LANES), 1)
            for bucket in range(REL_BUCKETS):
                rowv = jnp.zeros((1, LANES), F32)
                for h in range(N_HEADS):
                    val = jnp.sum(jnp.where(bk == bucket, dbias_ref[h], 0.0), axis=1, keepdims=True)
                    val = jnp.sum(val, axis=0, keepdims=True)
                    rowv = jnp.where(lane == h, val, rowv)
                dtab_ref[bucket:bucket + 1, :] = rowv

    return _call(
        body, (tab, sinks, bkt, q, k, k, v, v, do), comm=comm, **_grid_ends(nb), name="swa_bwd",
        grid=(nb,),
        in_specs=_swa_specs(s_len) + [pl.BlockSpec((blk, N_HEADS * HEAD_DIM), lambda n: (n, 0))],
        out_specs=[pl.BlockSpec((blk, N_HEADS * HEAD_DIM), lambda n: (n, 0)),
                   pl.BlockSpec((s_len, kvw), lambda n: (0, 0)), pl.BlockSpec((s_len, kvw), lambda n: (0, 0)),
                   pl.BlockSpec((REL_BUCKETS, LANES), lambda n: (0, 0)), pl.BlockSpec((N_HEADS, LANES), lambda n: (0, 0))],
        out_shape=[jax.ShapeDtypeStruct((s_len, N_HEADS * HEAD_DIM), BF16),
                   jax.ShapeDtypeStruct((s_len, kvw), F32), jax.ShapeDtypeStruct((s_len, kvw), F32),
                   jax.ShapeDtypeStruct((REL_BUCKETS, LANES), F32), jax.ShapeDtypeStruct((N_HEADS, LANES), F32)],
        scratch_shapes=[pltpu.VMEM((N_HEADS, blk, 2 * blk), F32), pltpu.VMEM((N_HEADS, blk, 2 * blk), F32)],
        compiler_params=_cparams("arbitrary"),
    )


def _sb_terms(z, valid):
    zc = jnp.minimum(z, SB_LOGIT_CAP)
    lk = -jnp.log(1.0 + jnp.exp(zc))
    lsz = zc + lk
    return lsz, (lk if valid is None else jnp.where(valid, lk, 0.0))


def _bf16_parts(vals):
    parts, rest = [], vals
    for n in range(SB_SUM_PARTS):
        parts.append(rest.astype(BF16))
        if n + 1 < SB_SUM_PARTS:
            rest = rest - parts[-1].astype(F32)
    return parts[0] if len(parts) == 1 else jnp.concatenate(parts, axis=1)


def _row_sum_lanes(vals):
    return jnp.broadcast_to(jnp.sum(vals, axis=-1, keepdims=True), (vals.shape[0], LANES))


def _emit_skewed(*groups):
    for step in range(max(len(items) + len(stages) - 1 for items, stages in groups)):
        for items, stages in groups:
            for s, stage in enumerate(stages):
                if 0 <= step - s < len(items) and items[step - s] is not None:
                    stage(items[step - s])


def _sb_items(edge):
    items = []
    for h in range(2):
        for r0 in range(0, SB_QUERIES, SB_ROWS):
            if edge is None or r0 >= (edge + 1) * SB_KEYS:
                items.append((h, r0, False))
            else:
                items.append((h, r0, True) if r0 + SB_ROWS - 1 > edge * SB_KEYS else None)
    return items


def _sb_valid(w, edge):
    row = lax.broadcasted_iota(jnp.int32, (SB_ROWS, SB_KEYS), 0) + w[1]
    col = lax.broadcasted_iota(jnp.int32, (SB_ROWS, SB_KEYS), 1) + edge * SB_KEYS
    return col < row


def _sb_consts(tq, tk):
    low = lax.broadcasted_iota(jnp.int32, (tq, LANES), 1) < HEAD_DIM
    row = lax.broadcasted_iota(jnp.int32, (tk, tk), 0)
    col = lax.broadcasted_iota(jnp.int32, (tk, tk), 1)
    right = (row > col).astype(BF16)
    left = (row < col).astype(BF16)
    return low, jnp.concatenate([right] * SB_SUM_PARTS, axis=0), jnp.concatenate([left] * SB_SUM_PARTS, axis=0)


def _sb_fwd(q, k, v, comm=None):
    s_len = q.shape[0]
    tq, tk, tr = SB_QUERIES, SB_KEYS, SB_ROWS
    nk, ratio = s_len // tk, tq // tk
    assert nk <= LANES

    def body(q_ref, k_ref, v_ref, o_ref, car_ref, c_ref, oacc_ref, logw_ref, lksum_ref):
        i = pl.program_id(1)
        qv = q_ref[...]
        low, tri2, _ = _sb_consts(tq, tk)
        lane = lax.broadcasted_iota(jnp.int32, (tr, LANES), 1)
        zero = jnp.zeros_like(qv)
        q_heads = (jnp.where(low, qv, zero), jnp.where(low, zero, qv))
        c_ref[...] = jnp.zeros_like(c_ref)
        oacc_ref[...] = jnp.zeros_like(oacc_ref)
        car_ref[...] = jnp.full_like(car_ref, NEG_BIG)

        def front(j, edge):
            keys = k_ref[pl.ds(pl.multiple_of(j * tk, tk), tk), :]
            slot = j % SB_SLOTS
            st = {}

            def s_logits(w):
                st[w, "z"] = _dot_nt(q_heads[w[0]][w[1]:w[1] + tr], keys)

            def s_terms(w):
                valid = _sb_valid(w, edge) if w[2] else None
                lsz, lk = _sb_terms(st.pop((w, "z")), valid)
                st[w, "parts"] = _bf16_parts(lk)
                st[w, "lsz"] = lsz if valid is None else jnp.where(valid, lsz, NEG_BIG)
                lksum_ref[slot, w[0], w[1]:w[1] + tr, :] = _row_sum_lanes(lk)

            def s_suffix(w):
                logw_ref[slot, w[0], w[1]:w[1] + tr, :] = st.pop((w, "lsz")) + _dot(st.pop((w, "parts")), tri2)

            return _sb_items(edge), [s_logits, s_terms, s_suffix]

        def back(j, edge):
            vv = v_ref[pl.ds(pl.multiple_of(j * tk, tk), tk), :]
            slot = j % SB_SLOTS
            st = {}

            def s_weights(w):
                h, rs = w[0], slice(w[1], w[1] + tr)
                c = c_ref[h, rs, :]
                st[w, "a"] = jnp.exp(logw_ref[slot, h, rs, :] + jnp.tile(c, (1, tk // LANES))).astype(BF16)
                car_ref[h, rs, :] = jnp.where(lane == j, c, car_ref[h, rs, :])
                c_ref[h, rs, :] = c + lksum_ref[slot, h, rs, :]

            def s_values(w):
                oacc_ref[w[0], w[1]:w[1] + tr, :] += _dot(st.pop((w, "a")), vv)

            return _sb_items(edge), [s_weights, s_values]

        first = i * ratio
        edge_tiles = [(first + m, m) for m in reversed(range(ratio))]

        def alive():
            return (jnp.max(c_ref[...]) >= SB_DEAD_CARRY).astype(jnp.int32)

        @pl.when(i == 0)
        def _():
            _emit_skewed(*[front(j, m) for j, m in edge_tiles])
            _emit_skewed(*[back(j, m) for j, m in edge_tiles])

        @pl.when(i > 0)
        def _():
            tiles = edge_tiles + [(first - 1, None)]
            _emit_skewed(*[front(j, m) for j, m in tiles])
            _emit_skewed(*[back(j, m) for j, m in tiles])

            @pl.when((alive() > 0) & (first >= 2))
            def _():
                _emit_skewed(front(first - 2, None))

                def step(state):
                    pending, _ = state
                    _emit_skewed(front(pending - 1, None), back(pending, None))
                    return pending - 1, alive()

                pending, live = lax.while_loop(lambda s: (s[0] > 0) & (s[1] > 0), step, (first - 2, jnp.int32(1)))

                @pl.when(live > 0)
                def _():
                    _emit_skewed(back(pending, None))

        o_ref[...] = jnp.where(low, oacc_ref[0], oacc_ref[1]).astype(BF16)

    return _call(
        body, (q, k, v), comm=comm, **_grid_ends(N_HEADS // 2, s_len // tq), name="sb_fwd",
        grid=(N_HEADS // 2, s_len // tq),
        in_specs=[pl.BlockSpec((tq, LANES), lambda p, i: (i, p)),
                  pl.BlockSpec((s_len, LANES), lambda p, i: (0, p)),
                  pl.BlockSpec((s_len, LANES), lambda p, i: (0, p))],
        out_specs=[pl.BlockSpec((tq, LANES), lambda p, i: (i, p)), pl.BlockSpec((2, tq, LANES), lambda p, i: (p, i, 0))],
        out_shape=[jax.ShapeDtypeStruct((s_len, N_HEADS * HEAD_DIM), BF16),
                   jax.ShapeDtypeStruct((N_HEADS, s_len, LANES), F32)],
        scratch_shapes=[pltpu.VMEM((2, tq, LANES), F32), pltpu.VMEM((2, tq, LANES), F32),
                        pltpu.VMEM((SB_SLOTS, 2, tq, tk), F32), pltpu.VMEM((SB_SLOTS, 2, tq, LANES), F32)],
        compiler_params=_cparams("arbitrary", "arbitrary"),
    )


def _sb_bwd(q, k, v, do, cars):
    s_len = q.shape[0]
    tq, tk, tr = SB_QUERIES, SB_KEYS, SB_ROWS
    nk, ratio = s_len // tk, tq // tk

    def body(q_ref, k_ref, v_ref, do_ref, car_ref, dq_ref, dk_ref, dv_ref,
             gleft_ref, dqacc_ref, dkacc_ref, dvacc_ref, logw_ref, lsz_ref, da_ref, a_ref, dz_ref):
        i = pl.program_id(1)

        @pl.when(i == 0)
        def _():
            dkacc_ref[...] = jnp.zeros_like(dkacc_ref)
            dvacc_ref[...] = jnp.zeros_like(dvacc_ref)

        qv = q_ref[...]
        dov = do_ref[...]
        low, tri_right2, tri_left2 = _sb_consts(tq, tk)
        lane = lax.broadcasted_iota(jnp.int32, (tr, LANES), 1)
        zero = jnp.zeros_like(qv)
        q_heads = (jnp.where(low, qv, zero), jnp.where(low, zero, qv))
        do_heads = (jnp.where(low, dov, zero), jnp.where(low, zero, dov))
        q_t = qv.astype(F32).T.astype(BF16)
        do_t = dov.astype(F32).T.astype(BF16)
        gleft_ref[...] = jnp.zeros_like(gleft_ref)
        dqacc_ref[...] = jnp.zeros_like(dqacc_ref)

        def front(j, edge):
            key_rows = pl.ds(pl.multiple_of(j * tk, tk), tk)
            keys, values = k_ref[key_rows, :], v_ref[key_rows, :]
            slot = j % SB_SLOTS
            st = {}

            def s_logits(w):
                h, rs = w[0], slice(w[1], w[1] + tr)
                st[w, "z"] = _dot_nt(q_heads[h][rs], keys)
                da_ref[slot, h, rs, :] = _dot_nt(do_heads[h][rs], values)

            def s_terms(w):
                h, rs = w[0], slice(w[1], w[1] + tr)
                valid = _sb_valid(w, edge) if w[2] else None
                lsz, lk = _sb_terms(st.pop((w, "z")), valid)
                st[w, "parts"] = _bf16_parts(lk)
                lsz = lsz if valid is None else jnp.where(valid, lsz, NEG_BIG)
                lsz_ref[slot, h, rs, :] = lsz
                st[w, "lszc"] = lsz + jnp.sum(jnp.where(lane == j, car_ref[h, rs, :], 0.0), axis=-1, keepdims=True)

            def s_suffix(w):
                logw_ref[slot, w[0], w[1]:w[1] + tr, :] = st.pop((w, "lszc")) + _dot(st.pop((w, "parts")), tri_right2)

            return _sb_items(edge), [s_logits, s_terms, s_suffix]

        def back(j, edge):
            kv = k_ref[pl.ds(pl.multiple_of(j * tk, tk), tk), :]
            slot = j % SB_SLOTS
            st = {}

            items = _sb_items(edge)
            head_rows = [[w[1] for w in items if w is not None and w[0] == h] for h in range(2)]

            def s_weights(w):
                h, rs = w[0], slice(w[1], w[1] + tr)
                a = jnp.exp(logw_ref[slot, h, rs, :])
                g = a * da_ref[slot, h, rs, :]
                a_ref[slot, h, rs, :] = a.astype(BF16)
                st[w, "g"], st[w, "parts"] = g, _bf16_parts(g)

            def s_prefix(w):
                st[w, "gs"] = _dot(st.pop((w, "parts")), tri_left2)

            def s_dz(w):
                h, rs = w[0], slice(w[1], w[1] + tr)
                g = st.pop((w, "g"))
                gleft = gleft_ref[h, rs, :]
                gsum = st.pop((w, "gs")) + jnp.tile(gleft, (1, tk // LANES))
                dz = (g - jnp.exp(lsz_ref[slot, h, rs, :]) * (g + gsum)).astype(BF16)
                st[w, "dz"] = dz
                dz_ref[slot, h, rs, :] = dz
                gleft_ref[h, rs, :] = gleft + _row_sum_lanes(g)

            def s_products(w):
                h, rs = w[0], slice(w[1], w[1] + tr)
                dqacc_ref[h, rs, :] += _dot(st.pop((w, "dz")), kv)
                if w[1] == head_rows[h][-1]:
                    feat = slice(h * HEAD_DIM, (h + 1) * HEAD_DIM)
                    hr = slice(head_rows[h][0], tq)
                    dkacc_ref[j, feat, :] += _dot(q_t[feat, hr], dz_ref[slot, h, hr, :])
                    dvacc_ref[j, feat, :] += _dot(do_t[feat, hr], a_ref[slot, h, hr, :])

            return items, [s_weights, s_prefix, s_dz, s_products]

        first = i * ratio
        tile_max = jnp.max(jnp.maximum(car_ref[0], car_ref[1]), axis=0, keepdims=True)
        start = jnp.clip(first + ratio - jnp.sum(jnp.where(tile_max >= SB_DEAD_CARRY, 1, 0)), 0, first)

        edge_tiles = [(first + m, m) for m in range(ratio)]

        @pl.when(start == first)
        def _():
            _emit_skewed(*[front(j, m) for j, m in edge_tiles])
            _emit_skewed(*[back(j, m) for j, m in edge_tiles])

        @pl.when(start == first - 1)
        def _():
            tiles = [(first - 1, None)] + edge_tiles
            _emit_skewed(*[front(j, m) for j, m in tiles])
            _emit_skewed(*[back(j, m) for j, m in tiles])

        @pl.when(start < first - 1)
        def _():
            _emit_skewed(front(start, None))

            def step(jj, carry):
                _emit_skewed(front(jj, None), back(jj - 1, None))
                return carry

            lax.fori_loop(start + 1, first, step, 0)
            _emit_skewed(front(first, 0), back(first - 1, None))
            for m in range(1, ratio):
                _emit_skewed(front(first + m, m), back(first + m - 1, m - 1))
            _emit_skewed(back(first + ratio - 1, ratio - 1))

        dq_ref[...] = (Q_SCALE * jnp.where(low, dqacc_ref[0], dqacc_ref[1])).astype(BF16)

        @pl.when(i == s_len // tq - 1)
        def _():
            for j in range(nk):
                dk_ref[j * tk:(j + 1) * tk, :] = dkacc_ref[j].T.astype(BF16)
                dv_ref[j * tk:(j + 1) * tk, :] = dvacc_ref[j].T.astype(BF16)

    qblk = pl.BlockSpec((tq, LANES), lambda p, i: (i, p))
    col_full = pl.BlockSpec((s_len, LANES), lambda p, i: (0, p))
    return pl.pallas_call(
        body, name="sb_bwd",
        grid=(N_HEADS // 2, s_len // tq),
        in_specs=[qblk, col_full, col_full, qblk, pl.BlockSpec((2, tq, LANES), lambda p, i: (p, i, 0))],
        out_specs=[qblk, col_full, col_full],
        out_shape=[jax.ShapeDtypeStruct((s_len, N_HEADS * HEAD_DIM), BF16)] * 3,
        scratch_shapes=[pltpu.VMEM((2, tq, LANES), F32), pltpu.VMEM((2, tq, LANES), F32),
                        pltpu.VMEM((nk, LANES, tk), F32), pltpu.VMEM((nk, LANES, tk), F32)]
        + [pltpu.VMEM((SB_SLOTS, 2, tq, tk), F32)] * 3 + [pltpu.VMEM((SB_SLOTS, 2, tq, tk), BF16)] * 2,
        compiler_params=_cparams("parallel", "arbitrary"),
    )(q, k, v, do, cars)


def _local_step(xs, tgt, gains, sinks, rel_bias, weights_of, ship):
    g1, gmix, g2, gfin = gains
    bkt = _rel_bucket_matrix()
    grads = {}

    def carried(outs, comm, count):
        return outs[:count], (list(outs[count:]) if comm is not None else None)

    wts = dict(weights_of(0, None))
    comm = ship("weights", 1)
    (x1, h1, a1, b1, u1), landed = carried(
        _ffn_fwd(xs, g1, wts["ffn1_w1t"], wts["ffn1_w3t"], wts["ffn1_w2"], "1", comm), comm, 5)
    wts.update(weights_of(1, landed))
    hm, qa, ka, va, qb, kb, vb, ga, gb = _proj_fwd(x1, gmix, wts["w_int"])
    oa = _swa_fwd(rel_bias, sinks, bkt, qa, ka, va)
    comm = ship("weights", 2)
    (ob, cars), landed = carried(_sb_fwd(qb, kb, vb, comm), comm, 2)
    wts.update(weights_of(2, landed))
    x2, mg = _merge_fwd(x1, oa, ob, ga, gb, wts["w_swa"], wts["w_sb"], wts["w_out"])
    dx3, h3, a3, b3, u3, loss, dgfin = _ffn_fwd(x2, g2, wts["ffn2_w1t"], wts["ffn2_w3t"], wts["ffn2_w2"], "2",
                                                loss=(tgt, gfin))

    def grad_chain(items):
        prev = None
        for name, lhs, rhs in items:
            comm = None if prev is None else ship("grads", (prev[0],), prev[1])
            res = _matmul_tn_tiled(lhs, rhs, name, comm)
            if prev is not None:
                grads[(prev[0],)] = prev[1] if comm is None else res[1]
            prev = (name, {_GRAD_KEY[name]: res if comm is None else res[0]})
        return prev

    dx2, dg2, da3, db3, dx3b = _ffn_bwd(dx3, x2, g2, a3, b3, wts["ffn2_w1t"], wts["ffn2_w3t"], wts["ffn2_w2"], "2")
    last = grad_chain((("ffn2_w1", da3, h3), ("ffn2_w3", db3, h3), ("ffn2_w2", u3, dx3b)))
    comm = ship("grads", (last[0],), last[1])
    (doa, dob, dga, dgb, gw_out, gw_swa, gw_sb), landed = carried(
        _merge_bwd(dx2, oa, ob, ga, gb, mg, wts["w_swa"], wts["w_sb"], wts["w_out"], comm), comm, 7)
    grads[(last[0],)] = last[1] if comm is None else landed[0]

    def keep(names, big, comm, landed):
        for i, name in enumerate(names):
            grads[(name,)] = {_GRAD_KEY[name]: big[_GRAD_KEY[name]]} if comm is None else landed[i]

    big = {"w_out": gw_out, "w_swa": gw_swa, "w_sb": gw_sb}
    comm = ship("grads", GROUPS[1][1:], big)
    (dqa, dka, dva, dtab, dsink), landed = carried(_swa_bwd(rel_bias, sinks, bkt, qa, ka, va, doa, comm), comm, 5)
    keep(GROUPS[1][1:], big, comm, landed)
    dqb, dkb, dvb = _sb_bwd(qb, kb, vb, dob, cars)
    dpieces = (dqa, dka.astype(BF16), dva.astype(BF16), dqb, dkb, dvb, dga, dgb)
    big = {"w_int": _matmul_tn_stacked(dpieces, hm, "w_in")}
    comm = ship("grads", GROUPS[1][:1], big)
    (dx1, dgmix), landed = carried(_proj_bwd(dpieces, dx2, x1, gmix, wts["w_int"], comm), comm, 2)
    keep(GROUPS[1][:1], big, comm, landed)

    dx0, dg1, da1, db1, dx1b = _ffn_bwd(dx1, xs, g1, a1, b1, wts["ffn1_w1t"], wts["ffn1_w3t"], wts["ffn1_w2"], "1")

    last = grad_chain((("ffn1_w1", da1, h1), ("ffn1_w3", db1, h1), ("ffn1_w2", u1, dx1b)))
    grads[(last[0],)] = last[1]

    small = {"gains": (dg1, dgmix, dg2, dgfin), "sinks": dsink[:, 0], "rel_bias": dtab[:, :N_HEADS]}
    return loss, dx0, small, grads


def _my_place():
    return lax.axis_index("x"), lax.axis_index("y"), lax.axis_index("c")


def _flip(v, bit):
    return 1 - v if bit else v


_RELATIONS = tuple((k >> 2 & 1, k >> 1 & 1, k & 1) for k in range(1, N_DEV))


def _gather_weights(blocks, tag):
    count = len(blocks)

    def body(*refs):
        x_refs, out_refs = refs[:count], refs[count:2 * count]
        send_sems, recv_sems, local_sems = refs[2 * count:]
        x, y, c = _my_place()
        me, sibling = (x, y, c), (x, y, 1 - c)
        chips = [(1 - x, y), (x, 1 - y), (1 - x, 1 - y)]

        def rows(s, px, py, pc):
            return out_refs[s].at[4 * px + 2 * py + pc]

        def copy(s, k, block, to, src=None):
            return pltpu.make_async_remote_copy(
                src_ref=rows(s, *block) if src is None else src, dst_ref=rows(s, *block),
                send_sem=send_sems.at[s, k], recv_sem=recv_sems.at[s, k],
                device_id=to, device_id_type=pl.DeviceIdType.MESH)

        mine = [pltpu.make_async_copy(x_refs[s], rows(s, *me), local_sems.at[s]) for s in range(count)]
        first, passed = [], []
        for s in range(count):
            mine[s].start()
            first.append(copy(s, 0, me, sibling, src=x_refs[s]))
            first += [copy(s, 1 + j, me, (*chip, c), src=x_refs[s]) for j, chip in enumerate(chips)]
        for cp in first:
            cp.start()
        for s in range(count):
            for j, chip in enumerate(chips):
                copy(s, 1 + j, (*chip, c), me).wait_recv()
                passed.append(copy(s, 4 + j, (*chip, c), sibling))
                passed[-1].start()
        for s in range(count):
            copy(s, 0, sibling, me).wait_recv()
            for j, chip in enumerate(chips):
                copy(s, 4 + j, (*chip, 1 - c), me).wait_recv()
        for cp in first + passed:
            cp.wait_send()
        for cp in mine:
            cp.wait()

    anywhere = pl.BlockSpec(memory_space=pl.ANY)
    return pl.pallas_call(
        body, name=f"gather_weights_{tag}",
        out_shape=[jax.ShapeDtypeStruct((N_DEV,) + b.shape, b.dtype) for b in blocks],
        in_specs=[anywhere] * count, out_specs=[anywhere] * count,
        scratch_shapes=[pltpu.SemaphoreType.DMA((count, N_DEV - 1)), pltpu.SemaphoreType.DMA((count, N_DEV - 1)),
                        pltpu.SemaphoreType.DMA((count,))],
    )(*blocks)


def _exchange_grads(gp, tag):
    def body(g_ref, out_ref, send_sems, recv_sems, local_sem):
        x, y, c = _my_place()
        me = 4 * x + 2 * y + c
        mine = pltpu.make_async_copy(g_ref.at[me], out_ref.at[me], local_sem)
        mine.start()
        copies = []
        for k, (fx, fy, fc) in enumerate(_RELATIONS):
            px, py, pc = _flip(x, fx), _flip(y, fy), _flip(c, fc)
            peer = 4 * px + 2 * py + pc
            copies.append((
                pltpu.make_async_remote_copy(
                    src_ref=g_ref.at[peer], dst_ref=out_ref.at[me], send_sem=send_sems.at[k], recv_sem=recv_sems.at[k],
                    device_id=(px, py, pc), device_id_type=pl.DeviceIdType.MESH),
                pltpu.make_async_remote_copy(
                    src_ref=g_ref.at[peer], dst_ref=out_ref.at[peer], send_sem=send_sems.at[k], recv_sem=recv_sems.at[k],
                    device_id=(px, py, pc), device_id_type=pl.DeviceIdType.MESH)))
        for out_cp, _ in copies:
            out_cp.start()
        for _, in_cp in copies:
            in_cp.wait_recv()
        for out_cp, _ in copies:
            out_cp.wait_send()
        mine.wait()

    return pl.pallas_call(
        body, name=f"exchange_grads_{tag}",
        out_shape=jax.ShapeDtypeStruct(gp.shape, gp.dtype),
        in_specs=[pl.BlockSpec(memory_space=pl.ANY)],
        out_specs=pl.BlockSpec(memory_space=pl.ANY),
        scratch_shapes=[pltpu.SemaphoreType.DMA((7,)), pltpu.SemaphoreType.DMA((7,)), pltpu.SemaphoreType.DMA(())],
    )(gp)


def _peers():
    x, y, c = _my_place()
    out = []
    for k, (fx, fy, fc) in enumerate(_RELATIONS):
        px, py, pc = _flip(x, fx), _flip(y, fy), _flip(c, fc)
        out.append((k, (px, py, pc), 4 * px + 2 * py + pc))
    return out, 4 * x + 2 * y + c


def _grid_ends(*grid):
    def first():
        return functools.reduce(lambda a, b: a & b, [pl.program_id(d) == 0 for d in range(len(grid))])

    def last():
        return functools.reduce(lambda a, b: a & b, [pl.program_id(d) == n - 1 for d, n in enumerate(grid)])

    return {"first": first, "last": last}


def _call(body, operands, *, comm=None, first=None, last=None, **kw):
    if comm is None:
        return pl.pallas_call(body, **kw)(*operands)
    in_specs, out_specs, out_shape = list(kw.pop("in_specs")), list(kw.pop("out_specs")), list(kw.pop("out_shape"))
    scratch = list(kw.pop("scratch_shapes", ()))
    n_in, n_out, n_scr, n_src = len(in_specs), len(out_specs), len(scratch), len(comm)

    def wrapped(*refs):
        ins, src_refs = refs[:n_in], refs[n_in:n_in + n_src]
        outs = refs[n_in + n_src:n_in + n_src + n_out]
        land_refs = refs[n_in + n_src + n_out:n_in + 2 * n_src + n_out]
        scr = refs[n_in + 2 * n_src + n_out:n_in + 2 * n_src + n_out + n_scr]
        send_sems, recv_sems, local_sems = refs[n_in + 2 * n_src + n_out + n_scr:]
        peers, me = _peers()
        mine, going, coming = [], [], []
        for s, (_, per_peer) in enumerate(comm):
            src_ref, land_ref = src_refs[s], land_refs[s]
            mine.append(pltpu.make_async_copy(src_ref.at[me] if per_peer else src_ref, land_ref.at[me], local_sems.at[s]))
            for k, where, slab in peers:
                piece = src_ref.at[slab] if per_peer else src_ref
                going.append(pltpu.make_async_remote_copy(
                    src_ref=piece, dst_ref=land_ref.at[me], send_sem=send_sems.at[s, k], recv_sem=recv_sems.at[s, k],
                    device_id=where, device_id_type=pl.DeviceIdType.MESH))
                coming.append(pltpu.make_async_remote_copy(
                    src_ref=piece, dst_ref=land_ref.at[slab], send_sem=send_sems.at[s, k], recv_sem=recv_sems.at[s, k],
                    device_id=where, device_id_type=pl.DeviceIdType.MESH))

        @pl.when(first())
        def _():
            for cp in mine + going:
                cp.start()

        body(*ins, *outs, *scr)

        @pl.when(last())
        def _():
            for cp in coming:
                cp.wait_recv()
            for cp in going:
                cp.wait_send()
            for cp in mine:
                cp.wait()

    anywhere = pl.BlockSpec(memory_space=pl.ANY)
    lands = [jax.ShapeDtypeStruct(src.shape if per_peer else (N_DEV,) + src.shape, src.dtype) for src, per_peer in comm]
    return pl.pallas_call(
        wrapped, in_specs=in_specs + [anywhere] * n_src, out_specs=out_specs + [anywhere] * n_src,
        out_shape=out_shape + lands,
        scratch_shapes=scratch + [pltpu.SemaphoreType.DMA((n_src, N_DEV - 1)), pltpu.SemaphoreType.DMA((n_src, N_DEV - 1)),
                                  pltpu.SemaphoreType.DMA((n_src,))],
        **kw)(*operands, *[src for src, _ in comm])


def _adamw(w, g, m, v):
    m = ADAM_B1 * m + (1.0 - ADAM_B1) * g
    v = ADAM_B2 * v + (1.0 - ADAM_B2) * jnp.square(g)
    m_hat = m / (1.0 - ADAM_B1 ** ADAM_STEP)
    v_hat = v / (1.0 - ADAM_B2 ** ADAM_STEP)
    delta = -ADAM_LR * (m_hat / (jnp.sqrt(v_hat) + ADAM_EPS) + ADAM_WD * w)
    return delta, m, v


def _sum_and_adamw(parts, w, m, v, tr, tag):
    rows = w.shape[0]
    assert rows % tr == 0

    def body(p_ref, w_ref, m_ref, v_ref, g_out, d_out, m_out, v_out):
        g = p_ref[0].astype(F32)
        for d in range(1, N_DEV):
            g = g + p_ref[d].astype(F32)
        delta, mn, vn = _adamw(w_ref[...], g, m_ref[...], v_ref[...])
        g_out[...] = g
        d_out[...] = delta
        m_out[...] = mn
        v_out[...] = vn

    sp = pl.BlockSpec((tr, D_MODEL), lambda i: (i, 0))
    return pl.pallas_call(
        body, name=f"sum_and_adamw_{tag}",
        grid=(rows // tr,),
        in_specs=[pl.BlockSpec((N_DEV, tr, D_MODEL), lambda i: (0, i, 0)), sp, sp, sp],
        out_specs=[sp] * 4,
        out_shape=[jax.ShapeDtypeStruct(w.shape, F32)] * 4,
        compiler_params=_cparams("parallel"),
    )(parts, w, m, v)


def _small_allreduce_adamw(part, w, m, v):
    def body(p_ref, w_ref, m_ref, v_ref, g_out, d_out, m_out, v_out, buf, send_sems, recv_sems):
        x, y, c = _my_place()
        me = 4 * x + 2 * y + c
        buf[me] = p_ref[...]
        copies = []
        for k, (fx, fy, fc) in enumerate(_RELATIONS):
            px, py, pc = _flip(x, fx), _flip(y, fy), _flip(c, fc)
            peer = 4 * px + 2 * py + pc
            copies.append((
                pltpu.make_async_remote_copy(
                    src_ref=buf.at[me], dst_ref=buf.at[me], send_sem=send_sems.at[k], recv_sem=recv_sems.at[k],
                    device_id=(px, py, pc), device_id_type=pl.DeviceIdType.MESH),
                pltpu.make_async_remote_copy(
                    src_ref=buf.at[me], dst_ref=buf.at[peer], send_sem=send_sems.at[k], recv_sem=recv_sems.at[k],
                    device_id=(px, py, pc), device_id_type=pl.DeviceIdType.MESH)))
        for out_cp, _ in copies:
            out_cp.start()
        for _, in_cp in copies:
            in_cp.wait_recv()
        for out_cp, _ in copies:
            out_cp.wait_send()
        g = buf[0]
        for d in range(1, N_DEV):
            g = g + buf[d]
        delta, mn, vn = _adamw(w_ref[...], g, m_ref[...], v_ref[...])
        g_out[...] = g
        d_out[...] = delta
        m_out[...] = mn
        v_out[...] = vn

    vm = pl.BlockSpec(memory_space=pltpu.VMEM)
    return pl.pallas_call(
        body, name="small_allreduce_adamw",
        in_specs=[vm] * 4, out_specs=[vm] * 4,
        out_shape=[jax.ShapeDtypeStruct(w.shape, F32)] * 4,
        scratch_shapes=[pltpu.VMEM((N_DEV,) + part.shape, F32),
                        pltpu.SemaphoreType.DMA((7,)), pltpu.SemaphoreType.DMA((7,))],
    )(part, w, m, v)


_TRANSPOSED = ("ffn1_w1", "ffn1_w3", "w_in", "ffn2_w1", "ffn2_w3")
_BRANCH = ("w_branch_swa", "w_branch_sb")


def _pack_shards(t, names):
    parts = []
    for name in names:
        a = t[name][0]
        if name in _TRANSPOSED:
            a = a.T
        elif name in _BRANCH:
            a = a.reshape(64, D_MODEL)
        parts.append(a)
    return jnp.concatenate(parts, axis=0)


def _unpack_shards(p, names):
    out, lo = {}, 0
    for name in names:
        a = p[lo:lo + BIG_ROWS[BIG_NAMES.index(name)]]
        lo += a.shape[0]
        if name in _TRANSPOSED:
            a = a.T
        elif name in _BRANCH:
            a = a.reshape(512, 128)
        out[name] = a[None]
    return out


def _full_weights(zones, names):
    out = {}
    for name, a in zip(names, zones):
        if name in _BRANCH:
            a = a.reshape(N_DEV, 512, 128).transpose(1, 0, 2).reshape(512, D_MODEL)
        out[_GRAD_KEY[name]] = a.reshape(-1, D_MODEL)
    return out


_GRAD_KEY = {"ffn1_w1": "ffn1_w1t", "ffn1_w3": "ffn1_w3t", "ffn1_w2": "ffn1_w2", "w_in": "w_int",
             "w_branch_swa": "w_swa", "w_branch_sb": "w_sb", "w_out": "w_out",
             "ffn2_w1": "ffn2_w1t", "ffn2_w3": "ffn2_w3t", "ffn2_w2": "ffn2_w2"}


def _pack_full_grads(big, names):
    parts = []
    for name in names:
        a = big[_GRAD_KEY[name]]
        if name in _BRANCH:
            a = a.reshape(512, N_DEV, 128).transpose(1, 0, 2)
        parts.append(a.reshape(N_DEV, BIG_ROWS[BIG_NAMES.index(name)], D_MODEL).astype(BF16))
    return jnp.concatenate(parts, axis=1)


_SMALL_NAMES = ("norm_ffn1", "norm_mix", "norm_ffn2", "norm_final", "swa_sinks", "rel_bias")


def _pack_small(vals):
    rows = []
    for a in vals:
        a = a.reshape(-1)
        rows.append(jnp.pad(a, (0, D_MODEL - a.shape[0])))
    rows += [jnp.zeros((D_MODEL,), F32)] * (SMALL_ROWS - len(rows))
    return jnp.stack(rows)


def _unpack_small(p):
    return {"norm_ffn1": p[0:1], "norm_mix": p[1:2], "norm_ffn2": p[2:3], "norm_final": p[3],
            "swa_sinks": p[4:5, :N_HEADS], "rel_bias": p[5, :REL_BUCKETS * N_HEADS].reshape(REL_BUCKETS, N_HEADS)}


ALL_NAMES = ("norm_ffn1", "ffn1_w1", "ffn1_w3", "ffn1_w2", "norm_mix", "w_in", "swa_sinks", "rel_bias",
             "w_branch_swa", "w_branch_sb", "w_out", "norm_ffn2", "ffn2_w1", "ffn2_w3", "ffn2_w2", "norm_final")


def kernel(x, norm_ffn1, ffn1_w1, ffn1_w3, ffn1_w2, norm_mix, w_in, swa_sinks, rel_bias, w_branch_swa, w_branch_sb, w_out, norm_ffn2, ffn2_w1, ffn2_w3, ffn2_w2, norm_final, loss_target, m_norm_ffn1, m_ffn1_w1, m_ffn1_w3, m_ffn1_w2, m_norm_mix, m_w_in, m_swa_sinks, m_rel_bias, m_w_branch_swa, m_w_branch_sb, m_w_out, m_norm_ffn2, m_ffn2_w1, m_ffn2_w3, m_ffn2_w2, m_norm_final, v_norm_ffn1, v_ffn1_w1, v_ffn1_w3, v_ffn1_w2, v_norm_mix, v_w_in, v_swa_sinks, v_rel_bias, v_w_branch_swa, v_w_branch_sb, v_w_out, v_norm_ffn2, v_ffn2_w1, v_ffn2_w3, v_ffn2_w2, v_norm_final):
    w = dict(zip(ALL_NAMES, (norm_ffn1, ffn1_w1, ffn1_w3, ffn1_w2, norm_mix, w_in, swa_sinks, rel_bias,
                             w_branch_swa, w_branch_sb, w_out, norm_ffn2, ffn2_w1, ffn2_w3, ffn2_w2, norm_final)))
    m = dict(zip(ALL_NAMES, (m_norm_ffn1, m_ffn1_w1, m_ffn1_w3, m_ffn1_w2, m_norm_mix, m_w_in, m_swa_sinks, m_rel_bias,
                             m_w_branch_swa, m_w_branch_sb, m_w_out, m_norm_ffn2, m_ffn2_w1, m_ffn2_w3, m_ffn2_w2,
                             m_norm_final)))
    v = dict(zip(ALL_NAMES, (v_norm_ffn1, v_ffn1_w1, v_ffn1_w3, v_ffn1_w2, v_norm_mix, v_w_in, v_swa_sinks, v_rel_bias,
                             v_w_branch_swa, v_w_branch_sb, v_w_out, v_norm_ffn2, v_ffn2_w1, v_ffn2_w3, v_ffn2_w2,
                             v_norm_final)))

    def my_blocks(group):
        return [_pack_shards(w, (name,)).astype(BF16) for name in GROUPS[group]]

    gathered0 = _gather_weights(my_blocks(0), "group0")

    def weights_of(group, landed):
        return _full_weights(gathered0 if group == 0 else landed, GROUPS[group])

    def ship(kind, which, grads=None):
        if kind == "weights":
            return [(block, False) for block in my_blocks(which)]
        return [(_pack_full_grads(grads, (name,)), True) for name in which]

    gains = (norm_ffn1, norm_mix, norm_ffn2, norm_final.reshape(1, D_MODEL))
    loss, dx, small, parts = _local_step(x[0], loss_target[0], gains, swa_sinks, rel_bias, weights_of, ship)

    big_outs = [{}, {}, {}, {}]
    for names, tile in zip(SUM_GROUPS, SUM_TILE):
        landed = parts[names]
        if isinstance(landed, dict):
            landed = _exchange_grads(_pack_full_grads(landed, names), names[0])
        res = _sum_and_adamw(landed, _pack_shards(w, names), _pack_shards(m, names), _pack_shards(v, names),
                             tile, names[0])
        for acc, packed in zip(big_outs, res):
            acc.update(_unpack_shards(packed, names))
    g_big, d_big, m_big, v_big = big_outs

    small_part = _pack_small(small["gains"] + (small["sinks"], small["rel_bias"], loss))
    zero = jnp.zeros((1,), F32)
    small_res = _small_allreduce_adamw(
        small_part, _pack_small([w[n] for n in _SMALL_NAMES] + [zero]), _pack_small([m[n] for n in _SMALL_NAMES] + [zero]),
        _pack_small([v[n] for n in _SMALL_NAMES] + [zero]))
    g_sm, d_sm, m_sm, v_sm = (_unpack_small(p) for p in small_res)

    outs = [small_res[0][len(_SMALL_NAMES), 0], dx[None]]
    for big_d, small_d in ((g_big, g_sm), (d_big, d_sm), (m_big, m_sm), (v_big, v_sm)):
        merged = {**big_d, **small_d}
        outs += [merged[n] for n in ALL_NAMES]
    return tuple(outs)
```

```python
import functools

import jax
import jax.numpy as jnp
import numpy as np
from jax import lax
from jax.experimental import pallas as pl
from jax.experimental.pallas import tpu as pltpu

F32 = jnp.float32
BF16 = jnp.bfloat16

D_MODEL = 1024
D_FF = 2816
HEAD_DIM = 64
N_HEADS = 8
SWA_KV_HEADS = 2
SWA_GROUP = 4
SWA_BLOCK = 128
REL_BUCKETS = 32
REL_MAX_DIST = 128
RMS_EPS = 1e-6
NEG_BIG = -1e30
Q_SCALE = HEAD_DIM ** -0.5
LANES = 128

N_DEV = 8

ADAM_LR = 0.001
ADAM_B1 = 0.9
ADAM_B2 = 0.999
ADAM_EPS = 1e-08
ADAM_WD = 0.01
ADAM_STEP = 10

IN_SIZES = (512, 128, 128, 512, 512, 512, 1024, 1024)
IN_OFFS = tuple(int(v) for v in np.cumsum((0,) + IN_SIZES))
IN_W = IN_OFFS[-1]

BIG_NAMES = ("ffn1_w1", "ffn1_w3", "ffn1_w2", "w_in", "w_branch_swa", "w_branch_sb", "w_out",
             "ffn2_w1", "ffn2_w3", "ffn2_w2")
BIG_ROWS = (352, 352, 352, 544, 64, 64, 128, 352, 352, 352)
SMALL_ROWS = 8
GROUPS = (BIG_NAMES[0:3], BIG_NAMES[3:7], BIG_NAMES[7:10])
SUM_GROUPS = tuple((n,) for n in BIG_NAMES)
SUM_TILE = (176, 176, 176, 272, 64, 64, 128, 176, 176, 176)

VMEM_LIMIT = 56 * 1024 * 1024
FFN_PIECES = 2
SB_QUERIES = 512
SB_KEYS = 256
SB_ROWS = 256
SB_SLOTS = 3
SB_SUM_PARTS = 1
SB_LOGIT_CAP = 80.0
SB_DEAD_CARRY = -110.0


def _dot(a, b):
    return jnp.dot(a, b, preferred_element_type=F32)


def _dot_nt(a, b):
    return lax.dot_general(a, b, (((1,), (1,)), ((), ())), preferred_element_type=F32)


def _dot_tn(a, b):
    return lax.dot_general(a, b, (((0,), (0,)), ((), ())), preferred_element_type=F32)


def _cparams(*sem):
    return pltpu.CompilerParams(dimension_semantics=sem, vmem_limit_bytes=VMEM_LIMIT)


def _rms_rstd(xv):
    return lax.rsqrt(jnp.mean(xv * xv, axis=-1, keepdims=True) + RMS_EPS)


def _rms_bwd(dh, xv, r, g):
    xhat = xv * r
    dg = jnp.sum(dh * xhat, axis=0, keepdims=True)
    dxn = dh * g
    dx = r * (dxn - xhat * jnp.mean(dxn * xhat, axis=-1, keepdims=True))
    return dx, dg


def _ff_tile_spec(tm, tf):
    return pl.BlockSpec((1, tm, tf), lambda i, j: (j, i, 0))


def _ffn_fwd(x, g, w1t, w3t, w2, tag, comm=None, loss=None):
    s_len = x.shape[0]
    tm, tf = min(1024, s_len), 256
    nf = D_FF // tf

    def body(*refs):
        if loss is None:
            x_ref, g_ref, w1_ref, w3_ref, w2_ref, xo_ref, h_ref, a_ref, b_ref, u_ref, acc_ref, hs_ref = refs
        else:
            (x_ref, g_ref, w1_ref, w3_ref, w2_ref, t_ref, gf_ref,
             xo_ref, h_ref, a_ref, b_ref, u_ref, loss_ref, dgf_ref, acc_ref, hs_ref) = refs
        j = pl.program_id(1)

        @pl.when(j == 0)
        def _():
            xv = x_ref[...]
            h = (xv * _rms_rstd(xv) * g_ref[...]).astype(BF16)
            hs_ref[...] = h
            h_ref[...] = h
            acc_ref[...] = jnp.zeros_like(acc_ref)

        st = {}

        def s_up(rs):
            h = hs_ref[rs, :]
            st[rs.start, "ab"] = (_dot_nt(h, w1_ref[...]), _dot_nt(h, w3_ref[...]))

        def s_act(rs):
            a, b = st.pop((rs.start, "ab"))
            a_ref[0, rs, :] = a.astype(BF16)
            b_ref[0, rs, :] = b.astype(BF16)
            uh = (0.5 * (a * jax.nn.sigmoid(a) * b)).astype(BF16)
            u_ref[0, rs, :] = uh
            st[rs.start, "u"] = uh

        def s_down(rs):
            acc_ref[rs, :] += _dot(st.pop((rs.start, "u")), w2_ref[...])

        _emit_skewed(([slice(r, r + tm // FFN_PIECES) for r in range(0, tm, tm // FFN_PIECES)], [s_up, s_act, s_down]))

        if loss is not None:
            @pl.when((pl.program_id(0) == 0) & (j == 0))
            def _():
                loss_ref[...] = jnp.zeros_like(loss_ref)
                dgf_ref[...] = jnp.zeros_like(dgf_ref)

        @pl.when(j == nf - 1)
        def _():
            xo = x_ref[...] + acc_ref[...]
            if loss is None:
                xo_ref[...] = xo
            else:
                gv = gf_ref[...]
                r = _rms_rstd(xo)
                err = xo * r * gv - t_ref[...]
                loss_ref[...] += 0.5 * jnp.sum(jnp.mean(err * err, axis=-1, keepdims=True), axis=0, keepdims=True)
                dx, dg = _rms_bwd(err * (1.0 / D_MODEL), xo, r, gv)
                xo_ref[...] = dx
                dgf_ref[...] += dg

    row = lambda i, j: (i, 0)
    fixed = lambda i, j: (0, 0)
    with_loss = loss is not None
    return _call(
        body, (x, g, w1t, w3t, w2) + (tuple(loss) if with_loss else ()), comm=comm, **_grid_ends(s_len // tm, nf),
        name=f"ffn_fwd_{tag}",
        grid=(s_len // tm, nf),
        in_specs=[pl.BlockSpec((tm, D_MODEL), row), pl.BlockSpec((1, D_MODEL), fixed),
                  pl.BlockSpec((tf, D_MODEL), lambda i, j: (j, 0)), pl.BlockSpec((tf, D_MODEL), lambda i, j: (j, 0)),
                  pl.BlockSpec((tf, D_MODEL), lambda i, j: (j, 0))]
        + ([pl.BlockSpec((tm, D_MODEL), row), pl.BlockSpec((1, D_MODEL), fixed)] if with_loss else []),
        out_specs=[pl.BlockSpec((tm, D_MODEL), row), pl.BlockSpec((tm, D_MODEL), row)] + [_ff_tile_spec(tm, tf)] * 3
        + ([pl.BlockSpec((1, 1), fixed), pl.BlockSpec((1, D_MODEL), fixed)] if with_loss else []),
        out_shape=[jax.ShapeDtypeStruct((s_len, D_MODEL), F32), jax.ShapeDtypeStruct((s_len, D_MODEL), BF16)]
        + [jax.ShapeDtypeStruct((nf, s_len, tf), BF16)] * 3
        + ([jax.ShapeDtypeStruct((1, 1), F32), jax.ShapeDtypeStruct((1, D_MODEL), F32)] if with_loss else []),
        scratch_shapes=[pltpu.VMEM((tm, D_MODEL), F32), pltpu.VMEM((tm, D_MODEL), BF16)],
        compiler_params=_cparams("arbitrary", "arbitrary"),
    )


def _ffn_bwd(dy, x, g, a, b, w1t, w3t, w2, tag, comm=None):
    s_len = x.shape[0]
    tm, tf = min(1024, s_len), 256
    nf = D_FF // tf

    def body(dy_ref, x_ref, g_ref, a_ref, b_ref, w1_ref, w3_ref, w2_ref,
             dx_ref, dg_ref, da_ref, db_ref, dyb_ref, acc_ref, dys_ref):
        i, j = pl.program_id(0), pl.program_id(1)

        @pl.when(j == 0)
        def _():
            dyb = dy_ref[...].astype(BF16)
            dys_ref[...] = 0.5 * dyb
            dyb_ref[...] = dyb
            acc_ref[...] = jnp.zeros_like(acc_ref)

        @pl.when((i == 0) & (j == 0))
        def _():
            dg_ref[...] = jnp.zeros_like(dg_ref)

        st = {}

        def s_du(rs):
            st[rs.start, "du"] = _dot_nt(dys_ref[rs, :], w2_ref[...])

        def s_act(rs):
            du = st.pop((rs.start, "du"))
            av = a_ref[0, rs, :].astype(F32)
            bv = b_ref[0, rs, :].astype(F32)
            sg = jax.nn.sigmoid(av)
            sil = av * sg
            da = (du * bv * (sg + sil * (1.0 - sg))).astype(BF16)
            db = (du * sil).astype(BF16)
            da_ref[0, rs, :] = da
            db_ref[0, rs, :] = db
            st[rs.start, "dab"] = (da, db)

        def s_dh(rs):
            da, db = st.pop((rs.start, "dab"))
            acc_ref[rs, :] += _dot(da, w1_ref[...]) + _dot(db, w3_ref[...])

        _emit_skewed(([slice(r, r + tm // FFN_PIECES) for r in range(0, tm, tm // FFN_PIECES)], [s_du, s_act, s_dh]))

        @pl.when(j == nf - 1)
        def _():
            xv = x_ref[...]
            dx, dg = _rms_bwd(acc_ref[...], xv, _rms_rstd(xv), g_ref[...])
            dx_ref[...] = dy_ref[...] + dx
            dg_ref[...] += dg

    row = lambda i, j: (i, 0)
    wsp = pl.BlockSpec((tf, D_MODEL), lambda i, j: (j, 0))
    return _call(
        body, (dy, x, g, a, b, w1t, w3t, w2), comm=comm, **_grid_ends(s_len // tm, nf), name=f"ffn_bwd_{tag}",
        grid=(s_len // tm, nf),
        in_specs=[pl.BlockSpec((tm, D_MODEL), row), pl.BlockSpec((tm, D_MODEL), row),
                  pl.BlockSpec((1, D_MODEL), lambda i, j: (0, 0)),
                  _ff_tile_spec(tm, tf), _ff_tile_spec(tm, tf), wsp, wsp, wsp],
        out_specs=[pl.BlockSpec((tm, D_MODEL), row), pl.BlockSpec((1, D_MODEL), lambda i, j: (0, 0)),
                   _ff_tile_spec(tm, tf), _ff_tile_spec(tm, tf), pl.BlockSpec((tm, D_MODEL), row)],
        out_shape=[jax.ShapeDtypeStruct((s_len, D_MODEL), F32), jax.ShapeDtypeStruct((1, D_MODEL), F32),
                   jax.ShapeDtypeStruct((nf, s_len, tf), BF16), jax.ShapeDtypeStruct((nf, s_len, tf), BF16),
                   jax.ShapeDtypeStruct((s_len, D_MODEL), BF16)],
        scratch_shapes=[pltpu.VMEM((tm, D_MODEL), F32), pltpu.VMEM((tm, D_MODEL), BF16)],
        compiler_params=_cparams("arbitrary", "arbitrary"),
    )


def _matmul_tn_tiled(lhs, rhs, tag, comm=None):
    nf, s_len, tf = lhs.shape
    n = rhs.shape[1]
    tm = min(512, s_len)
    last_rows = s_len // tm - 1

    def body(l_ref, r_ref, o_ref, acc_ref):
        i = pl.program_id(0)

        @pl.when(i == 0)
        def _():
            acc_ref[...] = jnp.zeros_like(acc_ref)

        rv = r_ref[...]
        for t in range(nf):
            acc_ref[t * tf:(t + 1) * tf, :] += _dot_tn(l_ref[t], rv)

        @pl.when(i == last_rows)
        def _():
            o_ref[...] = acc_ref[...].astype(BF16)

    res = _call(
        body, (lhs, rhs), comm=comm, **_grid_ends(s_len // tm), name=f"matmul_tn_{tag}",
        grid=(s_len // tm,),
        in_specs=[pl.BlockSpec((nf, tm, tf), lambda i: (0, i, 0)), pl.BlockSpec((tm, n), lambda i: (i, 0))],
        out_specs=[pl.BlockSpec((nf * tf, n), lambda i: (0, 0))],
        out_shape=[jax.ShapeDtypeStruct((nf * tf, n), BF16)],
        scratch_shapes=[pltpu.VMEM((nf * tf, n), F32)],
        compiler_params=_cparams("arbitrary"),
    )
    return res[0] if comm is None else tuple(res)


def _matmul_tn_stacked(pieces, rhs, tag):
    s_len, n = rhs.shape
    widths = [p.shape[1] for p in pieces]
    offs = [sum(widths[:k]) for k in range(len(widths) + 1)]
    tm = min(256, s_len)
    last_rows = s_len // tm - 1

    def body(*refs):
        l_refs, r_ref, o_ref, acc_ref = refs[:len(pieces)], refs[-3], refs[-2], refs[-1]
        i = pl.program_id(0)

        @pl.when(i == 0)
        def _():
            acc_ref[...] = jnp.zeros_like(acc_ref)

        rv = r_ref[...]
        for k, l_ref in enumerate(l_refs):
            acc_ref[offs[k]:offs[k + 1], :] += _dot_tn(l_ref[...], rv)

        @pl.when(i == last_rows)
        def _():
            o_ref[...] = acc_ref[...].astype(BF16)

    row = lambda i: (i, 0)
    return pl.pallas_call(
        body, name=f"matmul_tn_{tag}",
        grid=(s_len // tm,),
        in_specs=[pl.BlockSpec((tm, w), row) for w in widths] + [pl.BlockSpec((tm, n), row)],
        out_specs=pl.BlockSpec((offs[-1], n), lambda i: (0, 0)),
        out_shape=jax.ShapeDtypeStruct((offs[-1], n), BF16),
        scratch_shapes=[pltpu.VMEM((offs[-1], n), F32)],
        compiler_params=_cparams("arbitrary"),
    )(*pieces, rhs)


def _proj_fwd(x1, g, wint):
    s_len = x1.shape[0]
    tm = min(512, s_len)
    dts = (BF16, BF16, BF16, BF16, BF16, BF16, F32, F32)

    def body(x_ref, g_ref, w_ref, h_ref, *outs):
        xv = x_ref[...]
        h = (xv * _rms_rstd(xv) * g_ref[...]).astype(BF16)
        h_ref[...] = h
        for p, o_ref in enumerate(outs):
            val = _dot_nt(h, w_ref[IN_OFFS[p]:IN_OFFS[p + 1], :])
            if p == 3:
                val = val * Q_SCALE
            o_ref[...] = val.astype(dts[p])

    row = lambda i: (i, 0)
    return pl.pallas_call(
        body, name="proj_fwd",
        grid=(s_len // tm,),
        in_specs=[pl.BlockSpec((tm, D_MODEL), row), pl.BlockSpec((1, D_MODEL), lambda i: (0, 0)),
                  pl.BlockSpec((IN_W, D_MODEL), lambda i: (0, 0))],
        out_specs=[pl.BlockSpec((tm, D_MODEL), row)] + [pl.BlockSpec((tm, w), row) for w in IN_SIZES],
        out_shape=[jax.ShapeDtypeStruct((s_len, D_MODEL), BF16)]
        + [jax.ShapeDtypeStruct((s_len, w), dt) for w, dt in zip(IN_SIZES, dts)],
        compiler_params=_cparams("parallel"),
    )(x1, g, wint)


def _proj_bwd(dpieces, dx2, x1, g, wint, comm=None):
    s_len = x1.shape[0]
    tm = min(512, s_len)

    def body(*refs):
        dps = refs[:8]
        dx2_ref, x_ref, g_ref, w_ref, dx_ref, dg_ref = refs[8:]

        @pl.when(pl.program_id(0) == 0)
        def _():
            dg_ref[...] = jnp.zeros_like(dg_ref)

        dh = _dot(dps[0][...], w_ref[IN_OFFS[0]:IN_OFFS[1], :])
        for p in range(1, 8):
            dh += _dot(dps[p][...], w_ref[IN_OFFS[p]:IN_OFFS[p + 1], :])
        xv = x_ref[...]
        dx, dg = _rms_bwd(dh, xv, _rms_rstd(xv), g_ref[...])
        dx_ref[...] = dx2_ref[...] + dx
        dg_ref[...] += dg

    row = lambda i: (i, 0)
    return _call(
        body, (*dpieces, dx2, x1, g, wint), comm=comm, **_grid_ends(s_len // tm), name="proj_bwd",
        grid=(s_len // tm,),
        in_specs=[pl.BlockSpec((tm, w), row) for w in IN_SIZES]
        + [pl.BlockSpec((tm, D_MODEL), row), pl.BlockSpec((tm, D_MODEL), row),
           pl.BlockSpec((1, D_MODEL), lambda i: (0, 0)), pl.BlockSpec((IN_W, D_MODEL), lambda i: (0, 0))],
        out_specs=[pl.BlockSpec((tm, D_MODEL), row), pl.BlockSpec((1, D_MODEL), lambda i: (0, 0))],
        out_shape=[jax.ShapeDtypeStruct((s_len, D_MODEL), F32), jax.ShapeDtypeStruct((1, D_MODEL), F32)],
        compiler_params=_cparams("arbitrary"),
    )


def _merge_fwd(x1, oa, ob, ga, gb, wswa, wsb, wout):
    s_len = x1.shape[0]
    tm = min(512, s_len)

    def body(x_ref, oa_ref, ob_ref, ga_ref, gb_ref, wa_ref, wb_ref, wo_ref, xo_ref, mg_ref):
        pa = _dot(oa_ref[...], wa_ref[...])
        pb = _dot(ob_ref[...], wb_ref[...])
        mg = (jax.nn.sigmoid(ga_ref[...]) * pa + jax.nn.sigmoid(gb_ref[...]) * pb).astype(BF16)
        mg_ref[...] = mg
        xo_ref[...] = x_ref[...] + _dot(mg, wo_ref[...])

    row = lambda i: (i, 0)
    full = lambda i: (0, 0)
    return pl.pallas_call(
        body, name="merge_fwd",
        grid=(s_len // tm,),
        in_specs=[pl.BlockSpec((tm, D_MODEL), row), pl.BlockSpec((tm, 512), row), pl.BlockSpec((tm, 512), row),
                  pl.BlockSpec((tm, D_MODEL), row), pl.BlockSpec((tm, D_MODEL), row),
                  pl.BlockSpec((512, D_MODEL), full), pl.BlockSpec((512, D_MODEL), full),
                  pl.BlockSpec((D_MODEL, D_MODEL), full)],
        out_specs=[pl.BlockSpec((tm, D_MODEL), row), pl.BlockSpec((tm, D_MODEL), row)],
        out_shape=[jax.ShapeDtypeStruct((s_len, D_MODEL), F32), jax.ShapeDtypeStruct((s_len, D_MODEL), BF16)],
        compiler_params=_cparams("parallel"),
    )(x1, oa, ob, ga, gb, wswa, wsb, wout)


def _merge_bwd(dx2, oa, ob, ga, gb, mg, wswa, wsb, wout, comm=None):
    s_len = dx2.shape[0]
    tm = min(512, s_len)
    last_rows = s_len // tm - 1

    def body(dx_ref, oa_ref, ob_ref, ga_ref, gb_ref, mg_ref, wa_ref, wb_ref, wo_ref,
             doa_ref, dob_ref, dga_ref, dgb_ref, gwo_ref, gwa_ref, gwb_ref, acco_ref, acca_ref, accb_ref):
        i = pl.program_id(0)

        @pl.when(i == 0)
        def _():
            acco_ref[...] = jnp.zeros_like(acco_ref)
            acca_ref[...] = jnp.zeros_like(acca_ref)
            accb_ref[...] = jnp.zeros_like(accb_ref)

        dxb = dx_ref[...].astype(BF16)
        acco_ref[...] += _dot_tn(mg_ref[...], dxb)
        dmg = _dot_nt(dxb, wo_ref[...])
        for o_ref, g_ref, w_ref, do_ref, dg_ref, acc_ref in (
                (oa_ref, ga_ref, wa_ref, doa_ref, dga_ref, acca_ref),
                (ob_ref, gb_ref, wb_ref, dob_ref, dgb_ref, accb_ref)):
            pv = _dot(o_ref[...], w_ref[...])
            sg = jax.nn.sigmoid(g_ref[...])
            dp = (dmg * sg).astype(BF16)
            acc_ref[...] += _dot_tn(o_ref[...], dp)
            dg_ref[...] = (dmg * pv * sg * (1.0 - sg)).astype(BF16)
            do_ref[...] = _dot_nt(dp, w_ref[...]).astype(BF16)

        @pl.when(i == last_rows)
        def _():
            gwo_ref[...] = acco_ref[...].astype(BF16)
            gwa_ref[...] = acca_ref[...].astype(BF16)
            gwb_ref[...] = accb_ref[...].astype(BF16)

    row = lambda i: (i, 0)
    full = lambda i: (0, 0)
    wide = pl.BlockSpec((tm, D_MODEL), row)
    half = pl.BlockSpec((tm, 512), row)
    w_branch, w_out = pl.BlockSpec((512, D_MODEL), full), pl.BlockSpec((D_MODEL, D_MODEL), full)
    return _call(
        body, (dx2, oa, ob, ga, gb, mg, wswa, wsb, wout), comm=comm, **_grid_ends(s_len // tm), name="merge_bwd",
        grid=(s_len // tm,),
        in_specs=[wide, half, half, wide, wide, wide, w_branch, w_branch, w_out],
        out_specs=[half, half, wide, wide, w_out, w_branch, w_branch],
        out_shape=[jax.ShapeDtypeStruct((s_len, 512), BF16)] * 2 + [jax.ShapeDtypeStruct((s_len, D_MODEL), BF16)] * 2
        + [jax.ShapeDtypeStruct((D_MODEL, D_MODEL), BF16)] + [jax.ShapeDtypeStruct((512, D_MODEL), BF16)] * 2,
        scratch_shapes=[pltpu.VMEM((D_MODEL, D_MODEL), F32), pltpu.VMEM((512, D_MODEL), F32),
                        pltpu.VMEM((512, D_MODEL), F32)],
        compiler_params=_cparams("arbitrary"),
    )


def _rel_bucket_matrix():
    qi = jnp.arange(SWA_BLOCK)[:, None] + SWA_BLOCK
    kj = jnp.arange(2 * SWA_BLOCK)[None, :]
    dist = jnp.maximum(qi - kj, 0)
    max_exact = REL_BUCKETS // 2
    d = jnp.maximum(dist, 1).astype(F32)
    large = max_exact + (jnp.log(d / max_exact) / np.log(REL_MAX_DIST / max_exact)
                         * (REL_BUCKETS - max_exact)).astype(jnp.int32)
    large = jnp.minimum(large, REL_BUCKETS - 1)
    return jnp.where(dist < max_exact, dist, large).astype(jnp.int32)


def _swa_bias_into(bias_ref, bkt_ref, tab_ref):
    bk = bkt_ref[...]
    for h in range(N_HEADS):
        acc = jnp.zeros(bk.shape, F32)
        for bucket in range(REL_BUCKETS):
            acc = jnp.where(bk == bucket, tab_ref[bucket, h], acc)
        bias_ref[h] = acc


def _swa_valid(n):
    shape = (SWA_BLOCK, 2 * SWA_BLOCK)
    row = lax.broadcasted_iota(jnp.int32, shape, 0)
    col = lax.broadcasted_iota(jnp.int32, shape, 1)
    dist = row + SWA_BLOCK - col
    return (dist >= 0) & (dist < SWA_BLOCK) & ((col >= SWA_BLOCK) | (n > 0))


def _swa_windows(kp_ref, kc_ref, vp_ref, vc_ref):
    return (jnp.concatenate([kp_ref[...], kc_ref[...]], axis=0), jnp.concatenate([vp_ref[...], vc_ref[...]], axis=0))


def _swa_place(h):
    return slice(h // 2 * LANES, (h // 2 + 1) * LANES), h % 2, h // SWA_GROUP


def _move_half(x, src, dst):
    moved = x if src == dst else pltpu.roll(x, HEAD_DIM, 1)
    in_dst = (lax.broadcasted_iota(jnp.int32, x.shape, 1) >= HEAD_DIM) == bool(dst)
    return jnp.where(in_dst, moved, 0.0)


def _swa_probs(qk, bias, sink, valid):
    lg = jnp.where(valid, qk * Q_SCALE + bias, NEG_BIG)
    m = jnp.maximum(jnp.max(lg, axis=-1, keepdims=True), sink)
    e = jnp.exp(lg - m)
    es = jnp.exp(sink - m)
    inv = 1.0 / (jnp.sum(e, axis=-1, keepdims=True) + es)
    return e * inv, es * inv


def _swa_specs(s_len):
    blk = SWA_BLOCK
    cur = lambda n: (n, 0)
    prev = lambda n: (jnp.maximum(n - 1, 0), 0)
    kvw = SWA_KV_HEADS * HEAD_DIM
    return [pl.BlockSpec(memory_space=pltpu.SMEM), pl.BlockSpec(memory_space=pltpu.SMEM),
            pl.BlockSpec((blk, 2 * blk), lambda n: (0, 0)),
            pl.BlockSpec((blk, N_HEADS * HEAD_DIM), cur),
            pl.BlockSpec((blk, kvw), prev), pl.BlockSpec((blk, kvw), cur),
            pl.BlockSpec((blk, kvw), prev), pl.BlockSpec((blk, kvw), cur)]


def _swa_fwd(tab, sinks, bkt, q, k, v):
    s_len = q.shape[0]
    blk = SWA_BLOCK

    def body(tab_ref, sink_ref, bkt_ref, q_ref, kp_ref, kc_ref, vp_ref, vc_ref, o_ref, bias_ref):
        n = pl.program_id(0)

        @pl.when(n == 0)
        def _():
            _swa_bias_into(bias_ref, bkt_ref, tab_ref)

        valid = _swa_valid(n)
        kk, vv = _swa_windows(kp_ref, kc_ref, vp_ref, vc_ref)
        st = {}

        def s_logits(h):
            tile, mine, kv = _swa_place(h)
            st[h, "lg"] = _dot_nt(_move_half(q_ref[:, tile].astype(F32), mine, kv).astype(BF16), kk)

        def s_probs(h):
            st[h, "p"] = _swa_probs(st.pop((h, "lg")), bias_ref[h], sink_ref[0, h], valid)[0].astype(BF16)

        def s_values(h):
            tile, mine, kv = _swa_place(h)
            part = _move_half(_dot(st.pop((h, "p")), vv), kv, mine)
            if mine == 0:
                st[h + 1, "o"] = part
            else:
                o_ref[:, tile] = (st.pop((h, "o")) + part).astype(BF16)

        _emit_skewed((list(range(N_HEADS)), [s_logits, s_probs, s_values]))

    return pl.pallas_call(
        body, name="swa_fwd",
        grid=(s_len // blk,),
        in_specs=_swa_specs(s_len),
        out_specs=pl.BlockSpec((blk, N_HEADS * HEAD_DIM), lambda n: (n, 0)),
        out_shape=jax.ShapeDtypeStruct((s_len, N_HEADS * HEAD_DIM), BF16),
        scratch_shapes=[pltpu.VMEM((N_HEADS, blk, 2 * blk), F32)],
        compiler_params=_cparams("arbitrary"),
    )(tab, sinks, bkt, q, k, k, v, v)


def _swa_bwd(tab, sinks, bkt, q, k, v, do, comm=None):
    s_len = q.shape[0]
    blk = SWA_BLOCK
    nb = s_len // blk
    kvw = SWA_KV_HEADS * HEAD_DIM

    def body(tab_ref, sink_ref, bkt_ref, q_ref, kp_ref, kc_ref, vp_ref, vc_ref, do_ref,
             dq_ref, dk_ref, dv_ref, dtab_ref, dsink_ref, bias_ref, dbias_ref, dkacc_ref, dvacc_ref):
        n = pl.program_id(0)

        @pl.when(n == 0)
        def _():
            _swa_bias_into(bias_ref, bkt_ref, tab_ref)
            dbias_ref[...] = jnp.zeros_like(dbias_ref)
            dkacc_ref[...] = jnp.zeros_like(dkacc_ref)
            dvacc_ref[...] = jnp.zeros_like(dvacc_ref)
            dsink_ref[...] = jnp.zeros_like(dsink_ref)
            dtab_ref[...] = jnp.zeros_like(dtab_ref)

        valid = _swa_valid(n)
        cur_rows = pl.ds(pl.multiple_of(n * blk, blk), blk)
        prev_rows = pl.ds(pl.multiple_of(jnp.maximum(n - 1, 0) * blk, blk), blk)
        kk, vv = _swa_windows(kp_ref, kc_ref, vp_ref, vc_ref)
        st = {}

        def s_logits(h):
            tile, mine, kv = _swa_place(h)
            st[h, "q"] = _move_half(q_ref[:, tile].astype(F32), mine, kv).astype(BF16)
            st[h, "do"] = _move_half(do_ref[:, tile].astype(F32), mine, kv).astype(BF16)
            st[h, "lg"] = _dot_nt(st[h, "q"], kk)
            st[h, "dp"] = _dot_nt(st[h, "do"], vv)

        def s_probs(h):
            p, ps = _swa_probs(st.pop((h, "lg")), bias_ref[h], sink_ref[0, h], valid)
            dp = st.pop((h, "dp"))
            delta = jnp.sum(p * dp, axis=-1, keepdims=True)
            dl = p * (dp - delta)
            dsink_ref[h:h + 1, :] += jnp.broadcast_to(-jnp.sum(ps * delta, axis=0, keepdims=True), (1, LANES))
            dbias_ref[h] += dl
            st[h, "dl"], st[h, "p"] = dl.astype(BF16), p.astype(BF16)

        def s_products(h):
            tile, mine, kv = _swa_place(h)
            dlb = st.pop((h, "dl"))
            part = _move_half(Q_SCALE * _dot(dlb, kk), kv, mine)
            if mine == 0:
                st[h + 1, "dq"] = part
            else:
                dq_ref[:, tile] = (st.pop((h, "dq")) + part).astype(BF16)
            dk_win = Q_SCALE * _dot_tn(dlb, st.pop((h, "q")))
            dv_win = _dot_tn(st.pop((h, "p")), st.pop((h, "do")))
            dkacc_ref[prev_rows, :] += dk_win[:blk]
            dvacc_ref[prev_rows, :] += dv_win[:blk]
            dkacc_ref[cur_rows, :] += dk_win[blk:]
            dvacc_ref[cur_rows, :] += dv_win[blk:]

        _emit_skewed((list(range(N_HEADS)), [s_logits, s_probs, s_products]))

        @pl.when(n == nb - 1)
        def _():
            dk_ref[...] = dkacc_ref[...].astype(BF16)
            dv_ref[...] = dvacc_ref[...].astype(BF16)
            bk = bkt_ref[...]
            lane = lax.broadcasted_iota(jnp.int32, (1, LANES), 1)
            for bucket in range(REL_BUCKETS):
                rowv = jnp.zeros((1, LANES), F32)
                for h in range(N_HEADS):
                    val = jnp.sum(jnp.where(bk == bucket, dbias_ref[h], 0.0), axis=1, keepdims=True)
                    val = jnp.sum(val, axis=0, keepdims=True)
                    rowv = jnp.where(lane == h, val, rowv)
                dtab_ref[bucket:bucket + 1, :] = rowv

    return _call(
        body, (tab, sinks, bkt, q, k, k, v, v, do), comm=comm, **_grid_ends(nb), name="swa_bwd",
        grid=(nb,),
        in_specs=_swa_specs(s_len) + [pl.BlockSpec((blk, N_HEADS * HEAD_DIM), lambda n: (n, 0))],
        out_specs=[pl.BlockSpec((blk, N_HEADS * HEAD_DIM), lambda n: (n, 0)),
                   pl.BlockSpec((s_len, kvw), lambda n: (0, 0)), pl.BlockSpec((s_len, kvw), lambda n: (0, 0)),
                   pl.BlockSpec((REL_BUCKETS, LANES), lambda n: (0, 0)), pl.BlockSpec((N_HEADS, LANES), lambda n: (0, 0))],
        out_shape=[jax.ShapeDtypeStruct((s_len, N_HEADS * HEAD_DIM), BF16),
                   jax.ShapeDtypeStruct((s_len, kvw), BF16), jax.ShapeDtypeStruct((s_len, kvw), BF16),
                   jax.ShapeDtypeStruct((REL_BUCKETS, LANES), F32), jax.ShapeDtypeStruct((N_HEADS, LANES), F32)],
        scratch_shapes=[pltpu.VMEM((N_HEADS, blk, 2 * blk), F32), pltpu.VMEM((N_HEADS, blk, 2 * blk), F32),
                        pltpu.VMEM((s_len, kvw), F32), pltpu.VMEM((s_len, kvw), F32)],
        compiler_params=_cparams("arbitrary"),
    )


def _sb_terms(z, valid):
    zc = jnp.minimum(z, SB_LOGIT_CAP)
    lk = -jnp.log(1.0 + jnp.exp(zc))
    lsz = zc + lk
    return lsz, (lk if valid is None else jnp.where(valid, lk, 0.0))


def _bf16_parts(vals):
    parts, rest = [], vals
    for n in range(SB_SUM_PARTS):
        parts.append(rest.astype(BF16))
        if n + 1 < SB_SUM_PARTS:
            rest = rest - parts[-1].astype(F32)
    return parts[0] if len(parts) == 1 else jnp.concatenate(parts, axis=1)


def _row_sum_lanes(vals):
    return jnp.broadcast_to(jnp.sum(vals, axis=-1, keepdims=True), (vals.shape[0], LANES))


def _emit_skewed(*groups):
    for step in range(max(len(items) + len(stages) - 1 for items, stages in groups)):
        for items, stages in groups:
            for s, stage in enumerate(stages):
                if 0 <= step - s < len(items) and items[step - s] is not None:
                    stage(items[step - s])


def _sb_items(edge):
    items = []
    for h in range(2):
        for r0 in range(0, SB_QUERIES, SB_ROWS):
            if edge is None or r0 >= (edge + 1) * SB_KEYS:
                items.append((h, r0, False))
            else:
                items.append((h, r0, True) if r0 + SB_ROWS - 1 > edge * SB_KEYS else None)
    return items


def _sb_valid(w, edge):
    row = lax.broadcasted_iota(jnp.int32, (SB_ROWS, SB_KEYS), 0) + w[1]
    col = lax.broadcasted_iota(jnp.int32, (SB_ROWS, SB_KEYS), 1) + edge * SB_KEYS
    return col < row


def _sb_consts(tq, tk):
    low = lax.broadcasted_iota(jnp.int32, (tq, LANES), 1) < HEAD_DIM
    row = lax.broadcasted_iota(jnp.int32, (tk, tk), 0)
    col = lax.broadcasted_iota(jnp.int32, (tk, tk), 1)
    right = (row > col).astype(BF16)
    left = (row < col).astype(BF16)
    return low, jnp.concatenate([right] * SB_SUM_PARTS, axis=0), jnp.concatenate([left] * SB_SUM_PARTS, axis=0)


def _sb_fwd(q, k, v, comm=None):
    s_len = q.shape[0]
    tq, tk, tr = SB_QUERIES, SB_KEYS, SB_ROWS
    nk, ratio = s_len // tk, tq // tk
    assert nk <= LANES

    def body(q_ref, k_ref, v_ref, o_ref, car_ref, c_ref, oacc_ref, logw_ref, lksum_ref):
        i = pl.program_id(1)
        qv = q_ref[...]
        low, tri2, _ = _sb_consts(tq, tk)
        lane = lax.broadcasted_iota(jnp.int32, (tr, LANES), 1)
        zero = jnp.zeros_like(qv)
        q_heads = (jnp.where(low, qv, zero), jnp.where(low, zero, qv))
        c_ref[...] = jnp.zeros_like(c_ref)
        oacc_ref[...] = jnp.zeros_like(oacc_ref)
        car_ref[...] = jnp.full_like(car_ref, NEG_BIG)

        def front(j, edge):
            keys = k_ref[pl.ds(pl.multiple_of(j * tk, tk), tk), :]
            slot = j % SB_SLOTS
            st = {}

            def s_logits(w):
                st[w, "z"] = _dot_nt(q_heads[w[0]][w[1]:w[1] + tr], keys)

            def s_terms(w):
                valid = _sb_valid(w, edge) if w[2] else None
                lsz, lk = _sb_terms(st.pop((w, "z")), valid)
                st[w, "parts"] = _bf16_parts(lk)
                st[w, "lsz"] = lsz if valid is None else jnp.where(valid, lsz, NEG_BIG)
                lksum_ref[slot, w[0], w[1]:w[1] + tr, :] = _row_sum_lanes(lk)

            def s_suffix(w):
                logw_ref[slot, w[0], w[1]:w[1] + tr, :] = st.pop((w, "lsz")) + _dot(st.pop((w, "parts")), tri2)

            return _sb_items(edge), [s_logits, s_terms, s_suffix]

        def back(j, edge):
            vv = v_ref[pl.ds(pl.multiple_of(j * tk, tk), tk), :]
            slot = j % SB_SLOTS
            st = {}

            def s_weights(w):
                h, rs = w[0], slice(w[1], w[1] + tr)
                c = c_ref[h, rs, :]
                st[w, "a"] = jnp.exp(logw_ref[slot, h, rs, :] + jnp.tile(c, (1, tk // LANES))).astype(BF16)
                car_ref[h, rs, :] = jnp.where(lane == j, c, car_ref[h, rs, :])
                c_ref[h, rs, :] = c + lksum_ref[slot, h, rs, :]

            def s_values(w):
                oacc_ref[w[0], w[1]:w[1] + tr, :] += _dot(st.pop((w, "a")), vv)

            return _sb_items(edge), [s_weights, s_values]

        first = i * ratio
        edge_tiles = [(first + m, m) for m in reversed(range(ratio))]

        def alive():
            return (jnp.max(c_ref[...]) >= SB_DEAD_CARRY).astype(jnp.int32)

        @pl.when(i == 0)
        def _():
            _emit_skewed(*[front(j, m) for j, m in edge_tiles])
            _emit_skewed(*[back(j, m) for j, m in edge_tiles])

        @pl.when(i > 0)
        def _():
            tiles = edge_tiles + [(first - 1, None)]
            _emit_skewed(*[front(j, m) for j, m in tiles])
            _emit_skewed(*[back(j, m) for j, m in tiles])

            @pl.when((alive() > 0) & (first >= 2))
            def _():
                _emit_skewed(front(first - 2, None))

                def step(state):
                    pending, _ = state
                    _emit_skewed(front(pending - 1, None), back(pending, None))
                    return pending - 1, alive()

                pending, live = lax.while_loop(lambda s: (s[0] > 0) & (s[1] > 0), step, (first - 2, jnp.int32(1)))

                @pl.when(live > 0)
                def _():
                    _emit_skewed(back(pending, None))

        o_ref[...] = jnp.where(low, oacc_ref[0], oacc_ref[1]).astype(BF16)

    return _call(
        body, (q, k, v), comm=comm, **_grid_ends(N_HEADS // 2, s_len // tq), name="sb_fwd",
        grid=(N_HEADS // 2, s_len // tq),
        in_specs=[pl.BlockSpec((tq, LANES), lambda p, i: (i, p)),
                  pl.BlockSpec((s_len, LANES), lambda p, i: (0, p)),
                  pl.BlockSpec((s_len, LANES), lambda p, i: (0, p))],
        out_specs=[pl.BlockSpec((tq, LANES), lambda p, i: (i, p)), pl.BlockSpec((2, tq, LANES), lambda p, i: (p, i, 0))],
        out_shape=[jax.ShapeDtypeStruct((s_len, N_HEADS * HEAD_DIM), BF16),
                   jax.ShapeDtypeStruct((N_HEADS, s_len, LANES), F32)],
        scratch_shapes=[pltpu.VMEM((2, tq, LANES), F32), pltpu.VMEM((2, tq, LANES), F32),
                        pltpu.VMEM((SB_SLOTS, 2, tq, tk), F32), pltpu.VMEM((SB_SLOTS, 2, tq, LANES), F32)],
        compiler_params=_cparams("arbitrary", "arbitrary"),
    )


def _sb_bwd(q, k, v, do, cars):
    s_len = q.shape[0]
    tq, tk, tr = SB_QUERIES, SB_KEYS, SB_ROWS
    nk, ratio = s_len // tk, tq // tk

    def body(q_ref, k_ref, v_ref, do_ref, car_ref, dq_ref, dk_ref, dv_ref,
             gleft_ref, dqacc_ref, dkacc_ref, dvacc_ref, logw_ref, lsz_ref, da_ref, a_ref, dz_ref):
        i = pl.program_id(1)

        @pl.when(i == 0)
        def _():
            dkacc_ref[...] = jnp.zeros_like(dkacc_ref)
            dvacc_ref[...] = jnp.zeros_like(dvacc_ref)

        qv = q_ref[...]
        dov = do_ref[...]
        low, tri_right2, tri_left2 = _sb_consts(tq, tk)
        lane = lax.broadcasted_iota(jnp.int32, (tr, LANES), 1)
        zero = jnp.zeros_like(qv)
        q_heads = (jnp.where(low, qv, zero), jnp.where(low, zero, qv))
        do_heads = (jnp.where(low, dov, zero), jnp.where(low, zero, dov))
        q_t = qv.astype(F32).T.astype(BF16)
        do_t = dov.astype(F32).T.astype(BF16)
        gleft_ref[...] = jnp.zeros_like(gleft_ref)
        dqacc_ref[...] = jnp.zeros_like(dqacc_ref)

        def front(j, edge):
            key_rows = pl.ds(pl.multiple_of(j * tk, tk), tk)
            keys, values = k_ref[key_rows, :], v_ref[key_rows, :]
            slot = j % SB_SLOTS
            st = {}

            def s_logits(w):
                h, rs = w[0], slice(w[1], w[1] + tr)
                st[w, "z"] = _dot_nt(q_heads[h][rs], keys)
                da_ref[slot, h, rs, :] = _dot_nt(do_heads[h][rs], values)

            def s_terms(w):
                h, rs = w[0], slice(w[1], w[1] + tr)
                valid = _sb_valid(w, edge) if w[2] else None
                lsz, lk = _sb_terms(st.pop((w, "z")), valid)
                st[w, "parts"] = _bf16_parts(lk)
                lsz = lsz if valid is None else jnp.where(valid, lsz, NEG_BIG)
                lsz_ref[slot, h, rs, :] = lsz
                st[w, "lszc"] = lsz + jnp.sum(jnp.where(lane == j, car_ref[h, rs, :], 0.0), axis=-1, keepdims=True)

            def s_suffix(w):
                logw_ref[slot, w[0], w[1]:w[1] + tr, :] = st.pop((w, "lszc")) + _dot(st.pop((w, "parts")), tri_right2)

            return _sb_items(edge), [s_logits, s_terms, s_suffix]

        def back(j, edge):
            kv = k_ref[pl.ds(pl.multiple_of(j * tk, tk), tk), :]
            slot = j % SB_SLOTS
            st = {}

            items = _sb_items(edge)
            head_rows = [[w[1] for w in items if w is not None and w[0] == h] for h in range(2)]

            def s_weights(w):
                h, rs = w[0], slice(w[1], w[1] + tr)
                a = jnp.exp(logw_ref[slot, h, rs, :])
                g = a * da_ref[slot, h, rs, :]
                a_ref[slot, h, rs, :] = a.astype(BF16)
                st[w, "g"], st[w, "parts"] = g, _bf16_parts(g)

            def s_prefix(w):
                st[w, "gs"] = _dot(st.pop((w, "parts")), tri_left2)

            def s_dz(w):
                h, rs = w[0], slice(w[1], w[1] + tr)
                g = st.pop((w, "g"))
                gleft = gleft_ref[h, rs, :]
                gsum = st.pop((w, "gs")) + jnp.tile(gleft, (1, tk // LANES))
                dz = (g - jnp.exp(lsz_ref[slot, h, rs, :]) * (g + gsum)).astype(BF16)
                st[w, "dz"] = dz
                dz_ref[slot, h, rs, :] = dz
                gleft_ref[h, rs, :] = gleft + _row_sum_lanes(g)

            def s_products(w):
                h, rs = w[0], slice(w[1], w[1] + tr)
                dqacc_ref[h, rs, :] += _dot(st.pop((w, "dz")), kv)
                if w[1] == head_rows[h][-1]:
                    feat = slice(h * HEAD_DIM, (h + 1) * HEAD_DIM)
                    hr = slice(head_rows[h][0], tq)
                    dkacc_ref[j, feat, :] += _dot(q_t[feat, hr], dz_ref[slot, h, hr, :])
                    dvacc_ref[j, feat, :] += _dot(do_t[feat, hr], a_ref[slot, h, hr, :])

            return items, [s_weights, s_prefix, s_dz, s_products]

        first = i * ratio
        tile_max = jnp.max(jnp.maximum(car_ref[0], car_ref[1]), axis=0, keepdims=True)
        start = jnp.clip(first + ratio - jnp.sum(jnp.where(tile_max >= SB_DEAD_CARRY, 1, 0)), 0, first)

        edge_tiles = [(first + m, m) for m in range(ratio)]

        @pl.when(start == first)
        def _():
            _emit_skewed(*[front(j, m) for j, m in edge_tiles])
            _emit_skewed(*[back(j, m) for j, m in edge_tiles])

        @pl.when(start == first - 1)
        def _():
            tiles = [(first - 1, None)] + edge_tiles
            _emit_skewed(*[front(j, m) for j, m in tiles])
            _emit_skewed(*[back(j, m) for j, m in tiles])

        @pl.when(start < first - 1)
        def _():
            _emit_skewed(front(start, None))

            def step(jj, carry):
                _emit_skewed(front(jj, None), back(jj - 1, None))
                return carry

            lax.fori_loop(start + 1, first, step, 0)
            _emit_skewed(front(first, 0), back(first - 1, None))
            for m in range(1, ratio):
                _emit_skewed(front(first + m, m), back(first + m - 1, m - 1))
            _emit_skewed(back(first + ratio - 1, ratio - 1))

        dq_ref[...] = (Q_SCALE * jnp.where(low, dqacc_ref[0], dqacc_ref[1])).astype(BF16)

        @pl.when(i == s_len // tq - 1)
        def _():
            for j in range(nk):
                dk_ref[j * tk:(j + 1) * tk, :] = dkacc_ref[j].T.astype(BF16)
                dv_ref[j * tk:(j + 1) * tk, :] = dvacc_ref[j].T.astype(BF16)

    qblk = pl.BlockSpec((tq, LANES), lambda p, i: (i, p))
    col_full = pl.BlockSpec((s_len, LANES), lambda p, i: (0, p))
    return pl.pallas_call(
        body, name="sb_bwd",
        grid=(N_HEADS // 2, s_len // tq),
        in_specs=[qblk, col_full, col_full, qblk, pl.BlockSpec((2, tq, LANES), lambda p, i: (p, i, 0))],
        out_specs=[qblk, col_full, col_full],
        out_shape=[jax.ShapeDtypeStruct((s_len, N_HEADS * HEAD_DIM), BF16)] * 3,
        scratch_shapes=[pltpu.VMEM((2, tq, LANES), F32), pltpu.VMEM((2, tq, LANES), F32),
                        pltpu.VMEM((nk, LANES, tk), F32), pltpu.VMEM((nk, LANES, tk), F32)]
        + [pltpu.VMEM((SB_SLOTS, 2, tq, tk), F32)] * 3 + [pltpu.VMEM((SB_SLOTS, 2, tq, tk), BF16)] * 2,
        compiler_params=_cparams("parallel", "arbitrary"),
    )(q, k, v, do, cars)


def _local_step(xs, tgt, gains, sinks, rel_bias, weights_of, ship):
    g1, gmix, g2, gfin = gains
    bkt = _rel_bucket_matrix()
    grads = {}

    def carried(outs, comm, count):
        return outs[:count], (list(outs[count:]) if comm is not None else None)

    wts = dict(weights_of(0, None))
    comm = ship("weights", 1)
    (x1, h1, a1, b1, u1), landed = carried(
        _ffn_fwd(xs, g1, wts["ffn1_w1t"], wts["ffn1_w3t"], wts["ffn1_w2"], "1", comm), comm, 5)
    wts.update(weights_of(1, landed))
    hm, qa, ka, va, qb, kb, vb, ga, gb = _proj_fwd(x1, gmix, wts["w_int"])
    oa = _swa_fwd(rel_bias, sinks, bkt, qa, ka, va)
    comm = ship("weights", 2)
    (ob, cars), landed = carried(_sb_fwd(qb, kb, vb, comm), comm, 2)
    wts.update(weights_of(2, landed))
    x2, mg = _merge_fwd(x1, oa, ob, ga, gb, wts["w_swa"], wts["w_sb"], wts["w_out"])
    dx3, h3, a3, b3, u3, loss, dgfin = _ffn_fwd(x2, g2, wts["ffn2_w1t"], wts["ffn2_w3t"], wts["ffn2_w2"], "2",
                                                loss=(tgt, gfin))

    def grad_chain(items):
        prev = None
        for name, lhs, rhs in items:
            comm = None if prev is None else ship("grads", (prev[0],), prev[1])
            res = _matmul_tn_tiled(lhs, rhs, name, comm)
            if prev is not None:
                grads[(prev[0],)] = prev[1] if comm is None else res[1]
            prev = (name, {_GRAD_KEY[name]: res if comm is None else res[0]})
        return prev

    dx2, dg2, da3, db3, dx3b = _ffn_bwd(dx3, x2, g2, a3, b3, wts["ffn2_w1t"], wts["ffn2_w3t"], wts["ffn2_w2"], "2")
    last = grad_chain((("ffn2_w1", da3, h3), ("ffn2_w3", db3, h3), ("ffn2_w2", u3, dx3b)))
    comm = ship("grads", (last[0],), last[1])
    (doa, dob, dga, dgb, gw_out, gw_swa, gw_sb), landed = carried(
        _merge_bwd(dx2, oa, ob, ga, gb, mg, wts["w_swa"], wts["w_sb"], wts["w_out"], comm), comm, 7)
    grads[(last[0],)] = last[1] if comm is None else landed[0]

    def keep(names, big, comm, landed):
        for i, name in enumerate(names):
            grads[(name,)] = {_GRAD_KEY[name]: big[_GRAD_KEY[name]]} if comm is None else landed[i]

    big = {"w_out": gw_out, "w_swa": gw_swa, "w_sb": gw_sb}
    comm = ship("grads", GROUPS[1][1:], big)
    (dqa, dka, dva, dtab, dsink), landed = carried(_swa_bwd(rel_bias, sinks, bkt, qa, ka, va, doa, comm), comm, 5)
    keep(GROUPS[1][1:], big, comm, landed)
    dqb, dkb, dvb = _sb_bwd(qb, kb, vb, dob, cars)
    dpieces = (dqa, dka, dva, dqb, dkb, dvb, dga, dgb)
    big = {"w_int": _matmul_tn_stacked(dpieces, hm, "w_in")}
    comm = ship("grads", GROUPS[1][:1], big)
    (dx1, dgmix), landed = carried(_proj_bwd(dpieces, dx2, x1, gmix, wts["w_int"], comm), comm, 2)
    keep(GROUPS[1][:1], big, comm, landed)

    dx0, dg1, da1, db1, dx1b = _ffn_bwd(dx1, xs, g1, a1, b1, wts["ffn1_w1t"], wts["ffn1_w3t"], wts["ffn1_w2"], "1")

    last = grad_chain((("ffn1_w1", da1, h1), ("ffn1_w3", db1, h1), ("ffn1_w2", u1, dx1b)))
    grads[(last[0],)] = last[1]

    small = {"gains": (dg1, dgmix, dg2, dgfin), "sinks": dsink[:, 0], "rel_bias": dtab[:, :N_HEADS]}
    return loss, dx0, small, grads


def _my_place():
    return lax.axis_index("x"), lax.axis_index("y"), lax.axis_index("c")


def _flip(v, bit):
    return 1 - v if bit else v


_RELATIONS = tuple((k >> 2 & 1, k >> 1 & 1, k & 1) for k in range(1, N_DEV))


def _gather_weights(blocks, tag):
    count = len(blocks)

    def body(*refs):
        x_refs, out_refs = refs[:count], refs[count:2 * count]
        send_sems, recv_sems, local_sems = refs[2 * count:]
        x, y, c = _my_place()
        me, sibling = (x, y, c), (x, y, 1 - c)
        chips = [(1 - x, y), (x, 1 - y), (1 - x, 1 - y)]

        def rows(s, px, py, pc):
            return out_refs[s].at[4 * px + 2 * py + pc]

        def copy(s, k, block, to, src=None):
            return pltpu.make_async_remote_copy(
                src_ref=rows(s, *block) if src is None else src, dst_ref=rows(s, *block),
                send_sem=send_sems.at[s, k], recv_sem=recv_sems.at[s, k],
                device_id=to, device_id_type=pl.DeviceIdType.MESH)

        mine = [pltpu.make_async_copy(x_refs[s], rows(s, *me), local_sems.at[s]) for s in range(count)]
        first, passed = [], []
        for s in range(count):
            mine[s].start()
            first.append(copy(s, 0, me, sibling, src=x_refs[s]))
            first += [copy(s, 1 + j, me, (*chip, c), src=x_refs[s]) for j, chip in enumerate(chips)]
        for cp in first:
            cp.start()
        for s in range(count):
            for j, chip in enumerate(chips):
                copy(s, 1 + j, (*chip, c), me).wait_recv()
                passed.append(copy(s, 4 + j, (*chip, c), sibling))
                passed[-1].start()
        for s in range(count):
            copy(s, 0, sibling, me).wait_recv()
            for j, chip in enumerate(chips):
                copy(s, 4 + j, (*chip, 1 - c), me).wait_recv()
        for cp in first + passed:
            cp.wait_send()
        for cp in mine:
            cp.wait()

    anywhere = pl.BlockSpec(memory_space=pl.ANY)
    return pl.pallas_call(
        body, name=f"gather_weights_{tag}",
        out_shape=[jax.ShapeDtypeStruct((N_DEV,) + b.shape, b.dtype) for b in blocks],
        in_specs=[anywhere] * count, out_specs=[anywhere] * count,
        scratch_shapes=[pltpu.SemaphoreType.DMA((count, N_DEV - 1)), pltpu.SemaphoreType.DMA((count, N_DEV - 1)),
                        pltpu.SemaphoreType.DMA((count,))],
    )(*blocks)


def _exchange_grads(gp, tag):
    def body(g_ref, out_ref, send_sems, recv_sems, local_sem):
        x, y, c = _my_place()
        me = 4 * x + 2 * y + c
        mine = pltpu.make_async_copy(g_ref.at[me], out_ref.at[me], local_sem)
        mine.start()
        copies = []
        for k, (fx, fy, fc) in enumerate(_RELATIONS):
            px, py, pc = _flip(x, fx), _flip(y, fy), _flip(c, fc)
            peer = 4 * px + 2 * py + pc
            copies.append((
                pltpu.make_async_remote_copy(
                    src_ref=g_ref.at[peer], dst_ref=out_ref.at[me], send_sem=send_sems.at[k], recv_sem=recv_sems.at[k],
                    device_id=(px, py, pc), device_id_type=pl.DeviceIdType.MESH),
                pltpu.make_async_remote_copy(
                    src_ref=g_ref.at[peer], dst_ref=out_ref.at[peer], send_sem=send_sems.at[k], recv_sem=recv_sems.at[k],
                    device_id=(px, py, pc), device_id_type=pl.DeviceIdType.MESH)))
        for out_cp, _ in copies:
            out_cp.start()
        for _, in_cp in copies:
            in_cp.wait_recv()
        for out_cp, _ in copies:
            out_cp.wait_send()
        mine.wait()

    return pl.pallas_call(
        body, name=f"exchange_grads_{tag}",
        out_shape=jax.ShapeDtypeStruct(gp.shape, gp.dtype),
        in_specs=[pl.BlockSpec(memory_space=pl.ANY)],
        out_specs=pl.BlockSpec(memory_space=pl.ANY),
        scratch_shapes=[pltpu.SemaphoreType.DMA((7,)), pltpu.SemaphoreType.DMA((7,)), pltpu.SemaphoreType.DMA(())],
    )(gp)


def _peers():
    x, y, c = _my_place()
    out = []
    for k, (fx, fy, fc) in enumerate(_RELATIONS):
        px, py, pc = _flip(x, fx), _flip(y, fy), _flip(c, fc)
        out.append((k, (px, py, pc), 4 * px + 2 * py + pc))
    return out, 4 * x + 2 * y + c


def _grid_ends(*grid):
    def first():
        return functools.reduce(lambda a, b: a & b, [pl.program_id(d) == 0 for d in range(len(grid))])

    def last():
        return functools.reduce(lambda a, b: a & b, [pl.program_id(d) == n - 1 for d, n in enumerate(grid)])

    return {"first": first, "last": last}


def _call(body, operands, *, comm=None, first=None, last=None, **kw):
    if comm is None:
        return pl.pallas_call(body, **kw)(*operands)
    in_specs, out_specs, out_shape = list(kw.pop("in_specs")), list(kw.pop("out_specs")), list(kw.pop("out_shape"))
    scratch = list(kw.pop("scratch_shapes", ()))
    n_in, n_out, n_scr, n_src = len(in_specs), len(out_specs), len(scratch), len(comm)

    def wrapped(*refs):
        ins, src_refs = refs[:n_in], refs[n_in:n_in + n_src]
        outs = refs[n_in + n_src:n_in + n_src + n_out]
        land_refs = refs[n_in + n_src + n_out:n_in + 2 * n_src + n_out]
        scr = refs[n_in + 2 * n_src + n_out:n_in + 2 * n_src + n_out + n_scr]
        send_sems, recv_sems, local_sems = refs[n_in + 2 * n_src + n_out + n_scr:]
        peers, me = _peers()
        mine, going, coming = [], [], []
        for s, (_, per_peer) in enumerate(comm):
            src_ref, land_ref = src_refs[s], land_refs[s]
            mine.append(pltpu.make_async_copy(src_ref.at[me] if per_peer else src_ref, land_ref.at[me], local_sems.at[s]))
            for k, where, slab in peers:
                piece = src_ref.at[slab] if per_peer else src_ref
                going.append(pltpu.make_async_remote_copy(
                    src_ref=piece, dst_ref=land_ref.at[me], send_sem=send_sems.at[s, k], recv_sem=recv_sems.at[s, k],
                    device_id=where, device_id_type=pl.DeviceIdType.MESH))
                coming.append(pltpu.make_async_remote_copy(
                    src_ref=piece, dst_ref=land_ref.at[slab], send_sem=send_sems.at[s, k], recv_sem=recv_sems.at[s, k],
                    device_id=where, device_id_type=pl.DeviceIdType.MESH))

        @pl.when(first())
        def _():
            for cp in mine + going:
                cp.start()

        body(*ins, *outs, *scr)

        @pl.when(last())
        def _():
            for cp in coming:
                cp.wait_recv()
            for cp in going:
                cp.wait_send()
            for cp in mine:
                cp.wait()

    anywhere = pl.BlockSpec(memory_space=pl.ANY)
    lands = [jax.ShapeDtypeStruct(src.shape if per_peer else (N_DEV,) + src.shape, src.dtype) for src, per_peer in comm]
    return pl.pallas_call(
        wrapped, in_specs=in_specs + [anywhere] * n_src, out_specs=out_specs + [anywhere] * n_src,
        out_shape=out_shape + lands,
        scratch_shapes=scratch + [pltpu.SemaphoreType.DMA((n_src, N_DEV - 1)), pltpu.SemaphoreType.DMA((n_src, N_DEV - 1)),
                                  pltpu.SemaphoreType.DMA((n_src,))],
        **kw)(*operands, *[src for src, _ in comm])


def _adamw(w, g, m, v):
    m = ADAM_B1 * m + (1.0 - ADAM_B1) * g
    v = ADAM_B2 * v + (1.0 - ADAM_B2) * jnp.square(g)
    m_hat = m / (1.0 - ADAM_B1 ** ADAM_STEP)
    v_hat = v / (1.0 - ADAM_B2 ** ADAM_STEP)
    delta = -ADAM_LR * (m_hat / (jnp.sqrt(v_hat) + ADAM_EPS) + ADAM_WD * w)
    return delta, m, v


def _sum_and_adamw(parts, w, m, v, tr, tag):
    rows = w.shape[0]
    assert rows % tr == 0

    def body(p_ref, w_ref, m_ref, v_ref, g_out, d_out, m_out, v_out):
        g = p_ref[0].astype(F32)
        for d in range(1, N_DEV):
            g = g + p_ref[d].astype(F32)
        delta, mn, vn = _adamw(w_ref[...], g, m_ref[...], v_ref[...])
        g_out[...] = g
        d_out[...] = delta
        m_out[...] = mn
        v_out[...] = vn

    sp = pl.BlockSpec((tr, D_MODEL), lambda i: (i, 0))
    return pl.pallas_call(
        body, name=f"sum_and_adamw_{tag}",
        grid=(rows // tr,),
        in_specs=[pl.BlockSpec((N_DEV, tr, D_MODEL), lambda i: (0, i, 0)), sp, sp, sp],
        out_specs=[sp] * 4,
        out_shape=[jax.ShapeDtypeStruct(w.shape, F32)] * 4,
        compiler_params=_cparams("parallel"),
    )(parts, w, m, v)


def _small_allreduce_adamw(part, w, m, v):
    def body(p_ref, w_ref, m_ref, v_ref, g_out, d_out, m_out, v_out, buf, send_sems, recv_sems):
        x, y, c = _my_place()
        me = 4 * x + 2 * y + c
        buf[me] = p_ref[...]
        copies = []
        for k, (fx, fy, fc) in enumerate(_RELATIONS):
            px, py, pc = _flip(x, fx), _flip(y, fy), _flip(c, fc)
            peer = 4 * px + 2 * py + pc
            copies.append((
                pltpu.make_async_remote_copy(
                    src_ref=buf.at[me], dst_ref=buf.at[me], send_sem=send_sems.at[k], recv_sem=recv_sems.at[k],
                    device_id=(px, py, pc), device_id_type=pl.DeviceIdType.MESH),
                pltpu.make_async_remote_copy(
                    src_ref=buf.at[me], dst_ref=buf.at[peer], send_sem=send_sems.at[k], recv_sem=recv_sems.at[k],
                    device_id=(px, py, pc), device_id_type=pl.DeviceIdType.MESH)))
        for out_cp, _ in copies:
            out_cp.start()
        for _, in_cp in copies:
            in_cp.wait_recv()
        for out_cp, _ in copies:
            out_cp.wait_send()
        g = buf[0]
        for d in range(1, N_DEV):
            g = g + buf[d]
        delta, mn, vn = _adamw(w_ref[...], g, m_ref[...], v_ref[...])
        g_out[...] = g
        d_out[...] = delta
        m_out[...] = mn
        v_out[...] = vn

    vm = pl.BlockSpec(memory_space=pltpu.VMEM)
    return pl.pallas_call(
        body, name="small_allreduce_adamw",
        in_specs=[vm] * 4, out_specs=[vm] * 4,
        out_shape=[jax.ShapeDtypeStruct(w.shape, F32)] * 4,
        scratch_shapes=[pltpu.VMEM((N_DEV,) + part.shape, F32),
                        pltpu.SemaphoreType.DMA((7,)), pltpu.SemaphoreType.DMA((7,))],
    )(part, w, m, v)


_TRANSPOSED = ("ffn1_w1", "ffn1_w3", "w_in", "ffn2_w1", "ffn2_w3")
_BRANCH = ("w_branch_swa", "w_branch_sb")


def _pack_shards(t, names):
    parts = []
    for name in names:
        a = t[name][0]
        if name in _TRANSPOSED:
            a = a.T
        elif name in _BRANCH:
            a = a.reshape(64, D_MODEL)
        parts.append(a)
    return jnp.concatenate(parts, axis=0)


def _unpack_shards(p, names):
    out, lo = {}, 0
    for name in names:
        a = p[lo:lo + BIG_ROWS[BIG_NAMES.index(name)]]
        lo += a.shape[0]
        if name in _TRANSPOSED:
            a = a.T
        elif name in _BRANCH:
            a = a.reshape(512, 128)
        out[name] = a[None]
    return out


def _full_weights(zones, names):
    out = {}
    for name, a in zip(names, zones):
        if name in _BRANCH:
            a = a.reshape(N_DEV, 512, 128).transpose(1, 0, 2).reshape(512, D_MODEL)
        out[_GRAD_KEY[name]] = a.reshape(-1, D_MODEL)
    return out


_GRAD_KEY = {"ffn1_w1": "ffn1_w1t", "ffn1_w3": "ffn1_w3t", "ffn1_w2": "ffn1_w2", "w_in": "w_int",
             "w_branch_swa": "w_swa", "w_branch_sb": "w_sb", "w_out": "w_out",
             "ffn2_w1": "ffn2_w1t", "ffn2_w3": "ffn2_w3t", "ffn2_w2": "ffn2_w2"}


def _pack_full_grads(big, names):
    parts = []
    for name in names:
        a = big[_GRAD_KEY[name]]
        if name in _BRANCH:
            a = a.reshape(512, N_DEV, 128).transpose(1, 0, 2)
        parts.append(a.reshape(N_DEV, BIG_ROWS[BIG_NAMES.index(name)], D_MODEL).astype(BF16))
    return jnp.concatenate(parts, axis=1)


_SMALL_NAMES = ("norm_ffn1", "norm_mix", "norm_ffn2", "norm_final", "swa_sinks", "rel_bias")


def _pack_small(vals):
    rows = []
    for a in vals:
        a = a.reshape(-1)
        rows.append(jnp.pad(a, (0, D_MODEL - a.shape[0])))
    rows += [jnp.zeros((D_MODEL,), F32)] * (SMALL_ROWS - len(rows))
    return jnp.stack(rows)


def _unpack_small(p):
    return {"norm_ffn1": p[0:1], "norm_mix": p[1:2], "norm_ffn2": p[2:3], "norm_final": p[3],
            "swa_sinks": p[4:5, :N_HEADS], "rel_bias": p[5, :REL_BUCKETS * N_HEADS].reshape(REL_BUCKETS, N_HEADS)}


ALL_NAMES = ("norm_ffn1", "ffn1_w1", "ffn1_w3", "ffn1_w2", "norm_mix", "w_in", "swa_sinks", "rel_bias",
             "w_branch_swa", "w_branch_sb", "w_out", "norm_ffn2", "ffn2_w1", "ffn2_w3", "ffn2_w2", "norm_final")


def kernel(x, norm_ffn1, ffn1_w1, ffn1_w3, ffn1_w2, norm_mix, w_in, swa_sinks, rel_bias, w_branch_swa, w_branch_sb, w_out, norm_ffn2, ffn2_w1, ffn2_w3, ffn2_w2, norm_final, loss_target, m_norm_ffn1, m_ffn1_w1, m_ffn1_w3, m_ffn1_w2, m_norm_mix, m_w_in, m_swa_sinks, m_rel_bias, m_w_branch_swa, m_w_branch_sb, m_w_out, m_norm_ffn2, m_ffn2_w1, m_ffn2_w3, m_ffn2_w2, m_norm_final, v_norm_ffn1, v_ffn1_w1, v_ffn1_w3, v_ffn1_w2, v_norm_mix, v_w_in, v_swa_sinks, v_rel_bias, v_w_branch_swa, v_w_branch_sb, v_w_out, v_norm_ffn2, v_ffn2_w1, v_ffn2_w3, v_ffn2_w2, v_norm_final):
    w = dict(zip(ALL_NAMES, (norm_ffn1, ffn1_w1, ffn1_w3, ffn1_w2, norm_mix, w_in, swa_sinks, rel_bias,
                             w_branch_swa, w_branch_sb, w_out, norm_ffn2, ffn2_w1, ffn2_w3, ffn2_w2, norm_final)))
    m = dict(zip(ALL_NAMES, (m_norm_ffn1, m_ffn1_w1, m_ffn1_w3, m_ffn1_w2, m_norm_mix, m_w_in, m_swa_sinks, m_rel_bias,
                             m_w_branch_swa, m_w_branch_sb, m_w_out, m_norm_ffn2, m_ffn2_w1, m_ffn2_w3, m_ffn2_w2,
                             m_norm_final)))
    v = dict(zip(ALL_NAMES, (v_norm_ffn1, v_ffn1_w1, v_ffn1_w3, v_ffn1_w2, v_norm_mix, v_w_in, v_swa_sinks, v_rel_bias,
                             v_w_branch_swa, v_w_branch_sb, v_w_out, v_norm_ffn2, v_ffn2_w1, v_ffn2_w3, v_ffn2_w2,
                             v_norm_final)))

    def my_blocks(group):
        return [_pack_shards(w, (name,)).astype(BF16) for name in GROUPS[group]]

    gathered0 = _gather_weights(my_blocks(0), "group0")

    def weights_of(group, landed):
        return _full_weights(gathered0 if group == 0 else landed, GROUPS[group])

    def ship(kind, which, grads=None):
        if kind == "weights":
            return [(block, False) for block in my_blocks(which)]
        return [(_pack_full_grads(grads, (name,)), True) for name in which]

    gains = (norm_ffn1, norm_mix, norm_ffn2, norm_final.reshape(1, D_MODEL))
    loss, dx, small, parts = _local_step(x[0], loss_target[0], gains, swa_sinks, rel_bias, weights_of, ship)

    big_outs = [{}, {}, {}, {}]
    for names, tile in zip(SUM_GROUPS, SUM_TILE):
        landed = parts[names]
        if isinstance(landed, dict):
            landed = _exchange_grads(_pack_full_grads(landed, names), names[0])
        res = _sum_and_adamw(landed, _pack_shards(w, names), _pack_shards(m, names), _pack_shards(v, names),
                             tile, names[0])
        for acc, packed in zip(big_outs, res):
            acc.update(_unpack_shards(packed, names))
    g_big, d_big, m_big, v_big = big_outs

    small_part = _pack_small(small["gains"] + (small["sinks"], small["rel_bias"], loss))
    zero = jnp.zeros((1,), F32)
    small_res = _small_allreduce_adamw(
        small_part, _pack_small([w[n] for n in _SMALL_NAMES] + [zero]), _pack_small([m[n] for n in _SMALL_NAMES] + [zero]),
        _pack_small([v[n] for n in _SMALL_NAMES] + [zero]))
    g_sm, d_sm, m_sm, v_sm = (_unpack_small(p) for p in small_res)

    outs = [small_res[0][len(_SMALL_NAMES), 0], dx[None]]
    for big_d, small_d in ((g_big, g_sm), (d_big, d_sm), (m_big, m_sm), (v_big, v_sm)):
        merged = {**big_d, **small_d}
        outs += [merged[n] for n in ALL_NAMES]
    return tuple(outs)
```
